```python
import math
import jax
import jax.numpy as jnp
from jax import lax
import numpy as np

D_MODEL = 2048
BATCH = 8
SEQ = 2048
DEPTH = 1

MIX_WIDTH = D_MODEL
ATTN_WIDTH = MIX_WIDTH // 2
ATTN_HEAD_DIM = 128
N_ATTN_HEADS = ATTN_WIDTH // ATTN_HEAD_DIM
RET_WIDTH = MIX_WIDTH - ATTN_WIDTH
RET_HEAD_DIM = 256
N_RET_HEADS = RET_WIDTH // RET_HEAD_DIM
RET_CHUNK = 128
DILATED_PATTERNS = ((128, 1), (512, 4), (2048, 16))
FFN_HIDDEN = ((8 * D_MODEL // 3 + 255) // 256) * 256
IN_PROJ_WIDTH = 3 * ATTN_WIDTH + 4 * RET_WIDTH
NORM_EPS = 1e-6

kernel_name = "hybrid_dilated_attn_retention_block"


def _rmsnorm(x, w):
    x32 = x.astype(jnp.float32)
    y = x32 * lax.rsqrt(jnp.mean(x32 * x32, axis=-1, keepdims=True) + NORM_EPS)
    return (y * w.astype(jnp.float32)).astype(x.dtype)


def _alibi_slopes(n_heads):
    return jnp.exp2(-8.0 * jnp.arange(1, n_heads + 1, dtype=jnp.float32) / n_heads)


def _dilated_branch(q, k, v, slopes, window, dilation):
    B, H, S, hd = q.shape
    blk = window // dilation
    span = dilation * blk
    sp = -(-S // span) * span
    L = sp // dilation
    nb = L // blk

    def to_sub(t):
        t = jnp.pad(t, ((0, 0), (0, 0), (0, sp - S), (0, 0)))
        t = t.reshape(B, H, L, dilation, hd).transpose(0, 1, 3, 2, 4)
        return t.reshape(B, H, dilation, nb, blk, hd)

    def two_block(t):
        prev = jnp.concatenate([jnp.zeros_like(t[:, :, :, :1]), t[:, :, :, :-1]], axis=3)
        return jnp.concatenate([prev, t], axis=4)

    qb = to_sub(q)
    kk = two_block(to_sub(k))
    vv = two_block(to_sub(v))
    s = jnp.einsum('bhrnqd,bhrnkd->bhrnqk', qb, kk).astype(jnp.float32) * (1.0 / math.sqrt(hd))
    qi = jnp.arange(blk)[:, None]
    kj = jnp.arange(2 * blk)[None, :]
    diff = qi - kj + blk
    key_idx = jnp.arange(nb)[:, None, None] * blk + kj[None] - blk
    valid = (diff >= 0) & (diff <= blk) & (key_idx >= 0)
    bias = -slopes[:, None, None] * (diff * dilation).astype(jnp.float32)
    s = s + bias[None, :, None, None]
    s = jnp.where(valid, s, -jnp.inf)
    lse = jax.nn.logsumexp(s, axis=-1)
    p = jnp.exp(s - lse[..., None])
    o = jnp.einsum('bhrnqk,bhrnkd->bhrnqd', p.astype(v.dtype), vv)

    def from_sub(t):
        tail = t.shape[5:]
        t = t.reshape((B, H, dilation, L) + tail)
        t = jnp.moveaxis(t, 2, 3)
        return t.reshape((B, H, sp) + tail)[:, :, :S]

    return from_sub(o), from_sub(lse)


def _dilated_attention(q, k, v):
    slopes = _alibi_slopes(q.shape[1])
    outs, lses = [], []
    for window, dilation in DILATED_PATTERNS:
        o, l = _dilated_branch(q, k, v, slopes, window, dilation)
        outs.append(o.astype(jnp.float32))
        lses.append(l)
    wts = jax.nn.softmax(jnp.stack(lses), axis=0)
    return jnp.sum(wts[..., None] * jnp.stack(outs), axis=0)


def _retention_chunkwise(q, k, v):
    B, H, S, dh = q.shape
    C = RET_CHUNK
    nc = S // C
    log_gamma = jnp.log(1.0 - jnp.exp2(-5.0 - jnp.arange(H, dtype=jnp.float32)))
    k = k * (1.0 / math.sqrt(dh))
    qc = q.reshape(B, H, nc, C, dh)
    kc = k.reshape(B, H, nc, C, dh)
    vc = v.reshape(B, H, nc, C, dh)
    idx = jnp.arange(C, dtype=jnp.float32)
    dif = idx[:, None] - idx[None, :]
    decay = jnp.where(dif >= 0, jnp.exp(log_gamma[:, None, None] * jnp.maximum(dif, 0.0)), 0.0)
    scores = jnp.einsum('bhnid,bhnjd->bhnij', qc, kc) * decay[None, :, None]
    inner = jnp.einsum('bhnij,bhnjd->bhnid', scores, vc)
    zeta = jnp.exp(log_gamma[:, None] * (C - 1.0 - idx))
    kv = jnp.einsum('bhnjd,bhnje->bhnde', kc * zeta[None, :, None, :, None], vc)
    gamma_chunk = jnp.exp(log_gamma * C)[None, :, None, None]

    def step(state, kv_n):
        return state * gamma_chunk + kv_n, state

    _, r_prev = lax.scan(step, jnp.zeros((B, H, dh, dh), jnp.float32), jnp.moveaxis(kv, 2, 0))
    r_prev = jnp.moveaxis(r_prev, 0, 2)
    xi = jnp.exp(log_gamma[:, None] * (idx + 1.0))
    cross = jnp.einsum('bhnid,bhnde->bhnie', qc, r_prev) * xi[None, :, None, :, None]
    return (inner + cross).reshape(B, H, S, dh)


def _heads(t, n_heads, head_dim):
    B, S, _ = t.shape
    return t.reshape(B, S, n_heads, head_dim).transpose(0, 2, 1, 3)


def _merge(t):
    B, H, S, hd = t.shape
    return t.transpose(0, 2, 1, 3).reshape(B, S, H * hd)


def _hybrid_mixer(h, w_in, w_out):
    proj = jnp.einsum('bsd,de->bse', h, w_in)
    cuts = np.cumsum([ATTN_WIDTH] * 3 + [RET_WIDTH] * 3)
    qa, ka, va, qr, kr, vr, gr = jnp.split(proj, cuts, axis=-1)
    attn = _dilated_attention(_heads(qa, N_ATTN_HEADS, ATTN_HEAD_DIM),
                              _heads(ka, N_ATTN_HEADS, ATTN_HEAD_DIM),
                              _heads(va, N_ATTN_HEADS, ATTN_HEAD_DIM))
    f32 = jnp.float32
    ret = _retention_chunkwise(_heads(qr, N_RET_HEADS, RET_HEAD_DIM).astype(f32),
                               _heads(kr, N_RET_HEADS, RET_HEAD_DIM).astype(f32),
                               _heads(vr, N_RET_HEADS, RET_HEAD_DIM).astype(f32))
    ret = ret * lax.rsqrt(jnp.mean(ret * ret, axis=-1, keepdims=True) + NORM_EPS)
    ret = jax.nn.silu(gr.astype(f32)) * _merge(ret)
    mixed = jnp.concatenate([_merge(attn), ret], axis=-1).astype(h.dtype)
    return jnp.einsum('bse,ed->bsd', mixed, w_out)


def _swiglu(h, w_gate, w_up, w_down):
    g = jnp.einsum('bsd,df->bsf', h, w_gate)
    u = jnp.einsum('bsd,df->bsf', h, w_up)
    return jnp.einsum('bsf,fd->bsd', jax.nn.silu(g) * u, w_down)


def _fwd_setup_inputs(seed: int = 0) -> dict:
    key = jax.random.key(seed)
    ks = jax.random.split(key, 10)
    f32 = jnp.float32

    def normal(k, shape, fan_in):
        return jax.random.normal(k, shape, f32) * (fan_in ** -0.5)

    return {
        "x": jax.random.normal(ks[0], (BATCH, SEQ, D_MODEL), f32),
        "norm_mix_w": 1.0 + 0.02 * jax.random.normal(ks[1], (DEPTH, D_MODEL), f32),
        "w_in": normal(ks[2], (DEPTH, D_MODEL, IN_PROJ_WIDTH), D_MODEL),
        "w_out": normal(ks[3], (DEPTH, MIX_WIDTH, D_MODEL), MIX_WIDTH),
        "norm_ffn_w": 1.0 + 0.02 * jax.random.normal(ks[4], (DEPTH, D_MODEL), f32),
        "w_gate": normal(ks[5], (DEPTH, D_MODEL, FFN_HIDDEN), D_MODEL),
        "w_up": normal(ks[6], (DEPTH, D_MODEL, FFN_HIDDEN), D_MODEL),
        "w_down": normal(ks[7], (DEPTH, FFN_HIDDEN, D_MODEL), FFN_HIDDEN),
        "norm_final_w": 1.0 + 0.02 * jax.random.normal(ks[8], (D_MODEL,), f32),
    }


def _fwd_reference(x, norm_mix_w, w_in, w_out, norm_ffn_w, w_gate, w_up, w_down, norm_final_w):
    for layer in range(DEPTH):
        h = _rmsnorm(x, norm_mix_w[layer])
        x = x + _hybrid_mixer(h, w_in[layer], w_out[layer])
        h = _rmsnorm(x, norm_ffn_w[layer])
        x = x + _swiglu(h, w_gate[layer], w_up[layer], w_down[layer])
    return _rmsnorm(x, norm_final_w)


import jax as _jax
import jax.numpy as _jnp

TWIN_FORMAT = 'train_step'
FWD_PARAMS = ['x', 'norm_mix_w', 'w_in', 'w_out', 'norm_ffn_w', 'w_gate', 'w_up', 'w_down', 'norm_final_w']
TWIN_WEIGHTS = ['norm_mix_w', 'w_in', 'w_out', 'norm_ffn_w', 'w_gate', 'w_up', 'w_down', 'norm_final_w']
TWIN_DIFF_INPUT = 'x'
TWIN_INPUTS = ['x', 'norm_mix_w', 'w_in', 'w_out', 'norm_ffn_w', 'w_gate', 'w_up', 'w_down', 'norm_final_w', 'loss_target', 'm_norm_mix_w', 'm_w_in', 'm_w_out', 'm_norm_ffn_w', 'm_w_gate', 'm_w_up', 'm_w_down', 'm_norm_final_w', 'v_norm_mix_w', 'v_w_in', 'v_w_out', 'v_norm_ffn_w', 'v_w_gate', 'v_w_up', 'v_w_down', 'v_norm_final_w']
TWIN_OUTPUTS = ['loss', 'grad_x', 'grad_norm_mix_w', 'grad_w_in', 'grad_w_out', 'grad_norm_ffn_w', 'grad_w_gate', 'grad_w_up', 'grad_w_down', 'grad_norm_final_w', 'delta_norm_mix_w', 'delta_w_in', 'delta_w_out', 'delta_norm_ffn_w', 'delta_w_gate', 'delta_w_up', 'delta_w_down', 'delta_norm_final_w', 'new_m_norm_mix_w', 'new_m_w_in', 'new_m_w_out', 'new_m_norm_ffn_w', 'new_m_w_gate', 'new_m_w_up', 'new_m_w_down', 'new_m_norm_final_w', 'new_v_norm_mix_w', 'new_v_w_in', 'new_v_w_out', 'new_v_norm_ffn_w', 'new_v_w_gate', 'new_v_w_up', 'new_v_w_down', 'new_v_norm_final_w']
TWIN_LEAF_KINDS = {'loss': 'loss', 'grad_x': 'grad_x', 'grad_norm_mix_w': 'grad_w', 'grad_w_in': 'grad_w', 'grad_w_out': 'grad_w', 'grad_norm_ffn_w': 'grad_w', 'grad_w_gate': 'grad_w', 'grad_w_up': 'grad_w', 'grad_w_down': 'grad_w', 'grad_norm_final_w': 'grad_w', 'delta_norm_mix_w': 'delta_w', 'delta_w_in': 'delta_w', 'delta_w_out': 'delta_w', 'delta_norm_ffn_w': 'delta_w', 'delta_w_gate': 'delta_w', 'delta_w_up': 'delta_w', 'delta_w_down': 'delta_w', 'delta_norm_final_w': 'delta_w', 'new_m_norm_mix_w': 'new_m', 'new_m_w_in': 'new_m', 'new_m_w_out': 'new_m', 'new_m_norm_ffn_w': 'new_m', 'new_m_w_gate': 'new_m', 'new_m_w_up': 'new_m', 'new_m_w_down': 'new_m', 'new_m_norm_final_w': 'new_m', 'new_v_norm_mix_w': 'new_v', 'new_v_w_in': 'new_v', 'new_v_w_out': 'new_v', 'new_v_norm_ffn_w': 'new_v', 'new_v_w_gate': 'new_v', 'new_v_w_up': 'new_v', 'new_v_w_down': 'new_v', 'new_v_norm_final_w': 'new_v'}


def _forward(args):
    return _fwd_reference(*[args[k] for k in FWD_PARAMS])


def _output_shape():
    out = _jax.eval_shape(lambda: _forward(_fwd_setup_inputs(0)))
    return out.shape, out.dtype

N_MICROBATCH = 1
ADAM_LR = 0.001
ADAM_B1 = 0.9
ADAM_B2 = 0.999
ADAM_EPS = 1e-08
ADAM_WD = 0.01
ADAM_STEP = 10
PER_EXAMPLE_BATCH_AXIS = {'x': 0, 'loss_target': 0}
SHARED_INPUTS = []
_WEIGHT_DTYPES = {'norm_mix_w': _jnp.float32, 'w_in': _jnp.float32, 'w_out': _jnp.float32, 'norm_ffn_w': _jnp.float32, 'w_gate': _jnp.float32, 'w_up': _jnp.float32, 'w_down': _jnp.float32, 'norm_final_w': _jnp.float32}
MOMENT_SCALE = {'norm_mix_w': 5.923557e-02, 'w_in': 3.115505e-02, 'w_out': 3.111981e-02, 'norm_ffn_w': 4.365106e-02, 'w_gate': 1.839305e-02, 'w_up': 1.785073e-02, 'w_down': 2.955715e-02, 'norm_final_w': 8.002085e+00}


def _to_microbatches(a, axis):
    t = _jnp.moveaxis(a, axis, 0)
    t = t.reshape((N_MICROBATCH, t.shape[0] // N_MICROBATCH) + t.shape[1:])
    return _jnp.moveaxis(t, 1, axis + 1)


def setup_inputs(seed: int = 0) -> dict:
    inp = _fwd_setup_inputs(seed)
    key = _jax.random.fold_in(_jax.random.key(seed), 7919)
    shape, _ = _output_shape()
    out = dict(inp)
    out["loss_target"] = _jax.random.normal(_jax.random.fold_in(key, 0), shape, _jnp.float32)
    for i, name in enumerate(TWIN_WEIGHTS):
        w = inp[name].astype(_jnp.float32)
        if MOMENT_SCALE is None:
            s = _jnp.sqrt(_jnp.mean(_jnp.square(w)) + 1e-30)
        else:
            s = MOMENT_SCALE[name]
        km, kv = _jax.random.split(_jax.random.fold_in(key, i + 1))
        out[name] = w
        out["m_" + name] = s * _jax.random.normal(km, w.shape, _jnp.float32)
        out["v_" + name] = (s * s) * _jax.random.uniform(kv, w.shape, _jnp.float32, 0.5, 1.5)
    if N_MICROBATCH > 1:
        for name, axis in PER_EXAMPLE_BATCH_AXIS.items():
            out[name] = _to_microbatches(out[name], axis)
    return {'x': out['x'], 'norm_mix_w': out['norm_mix_w'], 'w_in': out['w_in'], 'w_out': out['w_out'], 'norm_ffn_w': out['norm_ffn_w'], 'w_gate': out['w_gate'], 'w_up': out['w_up'], 'w_down': out['w_down'], 'norm_final_w': out['norm_final_w'], 'loss_target': out['loss_target'], 'm_norm_mix_w': out['m_norm_mix_w'], 'm_w_in': out['m_w_in'], 'm_w_out': out['m_w_out'], 'm_norm_ffn_w': out['m_norm_ffn_w'], 'm_w_gate': out['m_w_gate'], 'm_w_up': out['m_w_up'], 'm_w_down': out['m_w_down'], 'm_norm_final_w': out['m_norm_final_w'], 'v_norm_mix_w': out['v_norm_mix_w'], 'v_w_in': out['v_w_in'], 'v_w_out': out['v_w_out'], 'v_norm_ffn_w': out['v_norm_ffn_w'], 'v_w_gate': out['v_w_gate'], 'v_w_up': out['v_w_up'], 'v_w_down': out['v_w_down'], 'v_norm_final_w': out['v_norm_final_w']}


def _loss(weights, diff, rest, loss_target):
    with _jax.named_scope("forward"):
        args = {**rest, TWIN_DIFF_INPUT: diff, **{k: w.astype(_WEIGHT_DTYPES[k]) for k, w in weights.items()}}
        y = _forward(args)
    with _jax.named_scope("loss_head"):
        err = _jnp.square(y.astype(_jnp.float32) - loss_target)
        return 0.5 * _jnp.sum(_jnp.mean(err, axis=-1)) if err.ndim else 0.5 * err


def _adamw(w, g, m, v):
    m = ADAM_B1 * m + (1.0 - ADAM_B1) * g
    v = ADAM_B2 * v + (1.0 - ADAM_B2) * _jnp.square(g)
    m_hat = m / (1.0 - ADAM_B1 ** ADAM_STEP)
    v_hat = v / (1.0 - ADAM_B2 ** ADAM_STEP)
    delta = -ADAM_LR * (m_hat / (_jnp.sqrt(v_hat) + ADAM_EPS) + ADAM_WD * w)
    return delta, m, v


def reference(x, norm_mix_w, w_in, w_out, norm_ffn_w, w_gate, w_up, w_down, norm_final_w, loss_target, m_norm_mix_w, m_w_in, m_w_out, m_norm_ffn_w, m_w_gate, m_w_up, m_w_down, m_norm_final_w, v_norm_mix_w, v_w_in, v_w_out, v_norm_ffn_w, v_w_gate, v_w_up, v_w_down, v_norm_final_w):
    given = dict(x=x, norm_mix_w=norm_mix_w, w_in=w_in, w_out=w_out, norm_ffn_w=norm_ffn_w, w_gate=w_gate, w_up=w_up, w_down=w_down, norm_final_w=norm_final_w, loss_target=loss_target, m_norm_mix_w=m_norm_mix_w, m_w_in=m_w_in, m_w_out=m_w_out, m_norm_ffn_w=m_norm_ffn_w, m_w_gate=m_w_gate, m_w_up=m_w_up, m_w_down=m_w_down, m_norm_final_w=m_norm_final_w, v_norm_mix_w=v_norm_mix_w, v_w_in=v_w_in, v_w_out=v_w_out, v_norm_ffn_w=v_norm_ffn_w, v_w_gate=v_w_gate, v_w_up=v_w_up, v_w_down=v_w_down, v_norm_final_w=v_norm_final_w)
    weights = {n: given[n] for n in TWIN_WEIGHTS}
    shared = {n: given[n] for n in SHARED_INPUTS}
    per_example = {n: given[n] for n in ['x']}
    grad_fn = _jax.value_and_grad(_loss, argnums=(0, 1))

    def one_microbatch(ex, loss_target):
        ex = dict(ex)
        diff = ex.pop(TWIN_DIFF_INPUT)
        return grad_fn(weights, diff, {**shared, **ex}, loss_target)

    if N_MICROBATCH == 1:
        loss, (grad_w, grad_x) = one_microbatch(per_example, given["loss_target"])
    else:
        def body(carry, xs):
            loss_sum, grad_sum = carry
            l_k, (gw_k, gx_k) = one_microbatch(xs[0], xs[1])
            with _jax.named_scope("update"):
                return (loss_sum + l_k, _jax.tree.map(_jnp.add, grad_sum, gw_k)), gx_k

        init = (_jnp.zeros((), _jnp.float32), _jax.tree.map(_jnp.zeros_like, weights))
        (loss, grad_w), grad_x = _jax.lax.scan(body, init, (per_example, given["loss_target"]))
    with _jax.named_scope("update"):
        delta_w, new_m, new_v = {}, {}, {}
        for n in TWIN_WEIGHTS:
            delta_w[n], new_m[n], new_v[n] = _adamw(weights[n], grad_w[n], given["m_" + n], given["v_" + n])
    return (loss, grad_x, *[grad_w[n] for n in TWIN_WEIGHTS], *[delta_w[n] for n in TWIN_WEIGHTS],
            *[new_m[n] for n in TWIN_WEIGHTS], *[new_v[n] for n in TWIN_WEIGHTS])
```

```python
import functools
import math

import jax
import jax.numpy as jnp
from jax import lax
from jax.experimental import pallas as pl
from jax.experimental.pallas import tpu as pltpu

F32 = jnp.float32
BF16 = jnp.bfloat16
I32 = jnp.int32
MESH = pl.DeviceIdType.MESH
ANY = pl.BlockSpec(memory_space=pl.ANY)

ATTN_HEADS = 8
ATTN_HEAD_DIM = 128
RET_HEADS = 4
RET_HEAD_DIM = 256
ATTN_WIDTH = ATTN_HEADS * ATTN_HEAD_DIM
RET_WIDTH = RET_HEADS * RET_HEAD_DIM
DILATED_PATTERNS = ((128, 1), (512, 4), (2048, 16))
NORM_EPS = 1e-6
ADAM_LR = 0.001
ADAM_B1 = 0.9
ADAM_B2 = 0.999
ADAM_EPS = 1e-08
ADAM_WD = 0.01
ADAM_STEP = 10

N_CHIPS = 4
N_DEV = 8
NEG_BIG = -1e30
SEQ_TILE = 256
VMEM_LIMIT_BYTES = 56 * 1024 * 1024


def _params(semantics=None, vmem=VMEM_LIMIT_BYTES):
    return pltpu.CompilerParams(dimension_semantics=semantics, vmem_limit_bytes=vmem)


def _row_tile(rows, row_bytes, limit=2 * 1024 * 1024, mult=16):
    best = None
    for t in range(mult, rows + 1, mult):
        if rows % t == 0 and t * row_bytes <= limit:
            best = t
    assert best is not None, (rows, row_bytes)
    return best


def _sigmoid(x):
    return 1.0 / (1.0 + jnp.exp(-x))


def _select_by_index(idx, values):
    out = jnp.float32(values[-1])
    for i in range(len(values) - 2, -1, -1):
        out = jnp.where(idx == i, jnp.float32(values[i]), out)
    return out


def _place():
    x, y, c = lax.axis_index("x"), lax.axis_index("y"), lax.axis_index("c")
    return x, y, c


def _cast_bf16(w, name):
    rows, cols = w.shape
    tr = _row_tile(rows, cols * 4)

    def body(w_ref, o_ref):
        o_ref[...] = w_ref[...].astype(BF16)

    return pl.pallas_call(
        body, name=name, grid=(rows // tr,),
        in_specs=[pl.BlockSpec((tr, cols), lambda i: (i, 0))],
        out_specs=pl.BlockSpec((tr, cols), lambda i: (i, 0)),
        out_shape=jax.ShapeDtypeStruct((rows, cols), BF16),
        compiler_params=_params(("parallel",)),
    )(w)


def _rms_fwd(x, w, name):
    rows, d = x.shape
    tr = 256

    def body(x_ref, w_ref, h_ref):
        xv = x_ref[...]
        r = lax.rsqrt(jnp.mean(xv * xv, axis=-1, keepdims=True) + NORM_EPS)
        h_ref[...] = (xv * r * w_ref[...]).astype(BF16)

    return pl.pallas_call(
        body, name=name, grid=(rows // tr,),
        in_specs=[pl.BlockSpec((tr, d), lambda i: (i, 0)), pl.BlockSpec((1, d), lambda i: (0, 0))],
        out_specs=pl.BlockSpec((tr, d), lambda i: (i, 0)),
        out_shape=jax.ShapeDtypeStruct((rows, d), BF16),
        compiler_params=_params(("parallel",)),
    )(x, w)


def _rms_bwd(x, w, dh, dres, name):
    rows, d = x.shape
    tr = 256

    def body(x_ref, w_ref, dh_ref, dres_ref, dx_ref, dxb_ref, dw_ref):
        xv = x_ref[...]
        r = lax.rsqrt(jnp.mean(xv * xv, axis=-1, keepdims=True) + NORM_EPS)
        xhat = xv * r
        dy = dh_ref[...]
        dxhat = dy * w_ref[...]
        dx = dres_ref[...] + r * (dxhat - xhat * jnp.mean(dxhat * xhat, axis=-1, keepdims=True))
        dx_ref[...] = dx
        dxb_ref[...] = dx.astype(BF16)
        part = jnp.sum(dy * xhat, axis=0, keepdims=True)

        @pl.when(pl.program_id(0) == 0)
        def _():
            dw_ref[...] = part

        @pl.when(pl.program_id(0) != 0)
        def _():
            dw_ref[...] += part

    row = pl.BlockSpec((tr, d), lambda i: (i, 0))
    vec = pl.BlockSpec((1, d), lambda i: (0, 0))
    return pl.pallas_call(
        body, name=name, grid=(rows // tr,),
        in_specs=[row, vec, row, row],
        out_specs=[row, row, vec],
        out_shape=[jax.ShapeDtypeStruct((rows, d), F32), jax.ShapeDtypeStruct((rows, d), BF16),
                   jax.ShapeDtypeStruct((1, d), F32)],
        compiler_params=_params(("arbitrary",)),
    )(x, w, dh, dres)


def _final_norm_loss(x2, w, target, name):
    rows, d = x2.shape
    tr = 256

    def body(x_ref, w_ref, t_ref, loss_ref, dx_ref, dxb_ref, dw_ref):
        xv = x_ref[...]
        wv = w_ref[...]
        r = lax.rsqrt(jnp.mean(xv * xv, axis=-1, keepdims=True) + NORM_EPS)
        xhat = xv * r
        err = xhat * wv - t_ref[...]
        part_loss = 0.5 * jnp.sum(jnp.mean(err * err, axis=-1, keepdims=True), axis=0, keepdims=True)
        dy = err * (1.0 / d)
        dxhat = dy * wv
        dx = r * (dxhat - xhat * jnp.mean(dxhat * xhat, axis=-1, keepdims=True))
        dx_ref[...] = dx
        dxb_ref[...] = dx.astype(BF16)
        part_dw = jnp.sum(dy * xhat, axis=0, keepdims=True)
        part_loss = jnp.broadcast_to(part_loss, (1, 128))

        @pl.when(pl.program_id(0) == 0)
        def _():
            dw_ref[...] = part_dw
            loss_ref[...] = part_loss

        @pl.when(pl.program_id(0) != 0)
        def _():
            dw_ref[...] += part_dw
            loss_ref[...] += part_loss

    row = pl.BlockSpec((tr, d), lambda i: (i, 0))
    vec = pl.BlockSpec((1, d), lambda i: (0, 0))
    return pl.pallas_call(
        body, name=name, grid=(rows // tr,),
        in_specs=[row, vec, row],
        out_specs=[pl.BlockSpec((1, 128), lambda i: (0, 0)), row, row, vec],
        out_shape=[jax.ShapeDtypeStruct((1, 128), F32), jax.ShapeDtypeStruct((rows, d), F32),
                   jax.ShapeDtypeStruct((rows, d), BF16), jax.ShapeDtypeStruct((1, d), F32)],
        compiler_params=_params(("arbitrary",)),
    )(x2, w, target)


def _adamw_math(w, g, m, v):
    m = ADAM_B1 * m + (1.0 - ADAM_B1) * g
    v = ADAM_B2 * v + (1.0 - ADAM_B2) * (g * g)
    m_hat = m / (1.0 - ADAM_B1 ** ADAM_STEP)
    v_hat = v / (1.0 - ADAM_B2 ** ADAM_STEP)
    delta = -ADAM_LR * (m_hat / (jnp.sqrt(v_hat) + ADAM_EPS) + ADAM_WD * w)
    return delta, m, v


def _adamw(w, g, m, v, name):
    rows, cols = w.shape
    tr = _row_tile(rows, cols * 4, limit=1024 * 1024)

    def body(w_ref, g_ref, m_ref, v_ref, d_ref, mo_ref, vo_ref):
        delta, m_new, v_new = _adamw_math(w_ref[...], g_ref[...], m_ref[...], v_ref[...])
        d_ref[...] = delta
        mo_ref[...] = m_new
        vo_ref[...] = v_new

    blk = pl.BlockSpec((tr, cols), lambda i: (i, 0))
    shp = jax.ShapeDtypeStruct((rows, cols), F32)
    return pl.pallas_call(
        body, name=name, grid=(rows // tr,),
        in_specs=[blk] * 4, out_specs=[blk] * 3, out_shape=[shp] * 3,
        compiler_params=_params(("parallel",)),
    )(w, g, m, v)


_DOT_DIMS = {"nn": ((1,), (0,)), "nt": ((1,), (1,)), "tn": ((0,), (0,))}


def _matmul(name, mode, a_list, b_list, acc_of, m, n, k, tm, tn, tk, extras, out_dtypes, epilogue,
            a_koff=None, b_koff=None):
    assert m % tm == 0 and n % tn == 0 and k % tk == 0, (name, m, n, k, tm, tn, tk)
    nk = k // tk
    n_acc = max(acc_of) + 1
    n_pairs = len(a_list)
    a_koff = a_koff or [0] * n_pairs
    b_koff = b_koff or [0] * n_pairs
    dims = (_DOT_DIMS[mode], ((), ()))
    n_ext, n_out = len(extras), len(out_dtypes)

    def body(*refs):
        a_refs = refs[:n_pairs]
        b_refs = refs[n_pairs:2 * n_pairs]
        e_refs = refs[2 * n_pairs:2 * n_pairs + n_ext]
        o_refs = refs[2 * n_pairs + n_ext:2 * n_pairs + n_ext + n_out]
        acc_refs = refs[2 * n_pairs + n_ext + n_out:]
        parts = [None] * n_acc
        for p in range(n_pairs):
            d = lax.dot_general(a_refs[p][...], b_refs[p][...], dims, preferred_element_type=F32)
            parts[acc_of[p]] = d if parts[acc_of[p]] is None else parts[acc_of[p]] + d

        def finish(accs):
            outs = epilogue(accs, [e[...] for e in e_refs])
            for o_ref, o in zip(o_refs, outs):
                o_ref[...] = o.astype(o_ref.dtype)

        if nk == 1:
            finish(parts)
        else:
            kk = pl.program_id(2)

            @pl.when(kk == 0)
            def _():
                for acc_ref, part in zip(acc_refs, parts):
                    acc_ref[...] = part

            @pl.when(kk != 0)
            def _():
                for acc_ref, part in zip(acc_refs, parts):
                    acc_ref[...] += part

            @pl.when(kk == nk - 1)
            def _():
                finish([acc_ref[...] for acc_ref in acc_refs])

    def a_spec(off):
        if mode == "tn":
            return pl.BlockSpec((tk, tm), lambda i, j, kk: (kk + off, i))
        return pl.BlockSpec((tm, tk), lambda i, j, kk: (i, kk + off))

    def b_spec(off):
        if mode == "nt":
            return pl.BlockSpec((tn, tk), lambda i, j, kk: (j, kk + off))
        return pl.BlockSpec((tk, tn), lambda i, j, kk: (kk + off, j))

    tile = pl.BlockSpec((tm, tn), lambda i, j, kk: (i, j))
    scratch = [pltpu.VMEM((tm, tn), F32) for _ in range(n_acc)] if nk > 1 else []
    return pl.pallas_call(
        body, name=name, grid=(m // tm, n // tn, nk),
        in_specs=[a_spec(o) for o in a_koff] + [b_spec(o) for o in b_koff] + [tile] * n_ext,
        out_specs=[tile] * n_out,
        out_shape=[jax.ShapeDtypeStruct((m, n), dt) for dt in out_dtypes],
        scratch_shapes=scratch,
        compiler_params=_params(("parallel", "parallel", "arbitrary")),
    )(*a_list, *b_list, *extras)


def _epi_plain(accs, extras):
    return (accs[0],)


def _epi_residual(accs, extras):
    return (accs[0] + extras[0],)


def _epi_swiglu(accs, extras):
    g, u = accs
    return g, u, g * _sigmoid(g) * u


def _epi_swiglu_bwd(accs, extras):
    da = accs[0]
    g, u = extras
    sg = _sigmoid(g)
    dg = da * u * sg * (1.0 + g * (1.0 - sg))
    du = da * g * sg
    return dg, du


_NT_DIMS = (((1,), (1,)), ((), ()))
_TN_DIMS = (((0,), (0,)), ((), ()))


def _tile_delta(tq, tk):
    return lax.broadcasted_iota(I32, (tq, tk), 0) - lax.broadcasted_iota(I32, (tq, tk), 1)


def _attn_mask_bias(delta, slope):
    count = jnp.zeros(delta.shape, I32)
    for window, dilation in DILATED_PATTERNS:
        hit = ((delta & (dilation - 1)) == 0) & (delta <= window)
        count = count + jnp.where(hit, 1, 0)
    valid = (delta >= 0) & (count > 0)
    logm = jnp.where(count == 3, math.log(3.0), jnp.where(count == 2, math.log(2.0), 0.0))
    return valid, logm - slope * delta.astype(F32)


def _alibi_slopes():
    return [2.0 ** (-8.0 * (h + 1) / ATTN_HEADS) for h in range(ATTN_HEADS)]


def _attn_fwd(proj):
    s = proj.shape[0]
    t = SEQ_TILE
    hd = ATTN_HEAD_DIM
    nh = ATTN_HEADS
    scale = 1.0 / math.sqrt(hd)
    slopes = _alibi_slopes()

    def body(q_ref, k_ref, v_ref, mix_ref, o_ref, lse_ref):
        h = pl.program_id(0)
        i = pl.program_id(1)
        slope = _select_by_index(h, slopes)
        q = q_ref[...].astype(BF16)
        base = _tile_delta(t, t)

        def step(j, carry):
            m_i, l_i, acc = carry
            rows = pl.ds(pl.multiple_of(j * t, t), t)
            kj = k_ref[rows, :].astype(BF16)
            vj = v_ref[rows, :].astype(BF16)
            sc = lax.dot_general(q, kj, _NT_DIMS, preferred_element_type=F32) * scale
            valid, bias = _attn_mask_bias(base + (i - j) * t, slope)
            sc = jnp.where(valid, sc + bias, NEG_BIG)
            m_new = jnp.maximum(m_i, jnp.max(sc, axis=-1, keepdims=True))
            p = jnp.exp(sc - m_new)
            alpha = jnp.exp(m_i - m_new)
            l_new = alpha * l_i + jnp.sum(p, axis=-1, keepdims=True)
            acc = alpha * acc + jnp.dot(p.astype(BF16), vj, preferred_element_type=F32)
            return m_new, l_new, acc

        init = (jnp.full((t, 1), NEG_BIG, F32), jnp.zeros((t, 1), F32), jnp.zeros((t, hd), F32))
        m_i, l_i, acc = lax.fori_loop(0, i + 1, step, init)
        out = acc / l_i
        o_ref[...] = out
        mix_ref[...] = out.astype(BF16)
        lse_ref[...] = jnp.broadcast_to(m_i + jnp.log(l_i), (t, hd))

    return pl.pallas_call(
        body, name="attn_fwd", grid=(nh, s // t),
        in_specs=[pl.BlockSpec((t, hd), lambda h, i: (i, h)),
                  pl.BlockSpec((s, hd), lambda h, i: (0, nh + h)),
                  pl.BlockSpec((s, hd), lambda h, i: (0, 2 * nh + h))],
        out_specs=[pl.BlockSpec((t, hd), lambda h, i: (i, h))] * 3,
        out_shape=[jax.ShapeDtypeStruct((s, ATTN_WIDTH), BF16),
                   jax.ShapeDtypeStruct((s, ATTN_WIDTH), F32),
                   jax.ShapeDtypeStruct((s, ATTN_WIDTH), F32)],
        compiler_params=_params(("parallel", "arbitrary")),
    )(proj, proj, proj)


def _attn_bwd(proj, attn_out, lse, dmixed):
    s = proj.shape[0]
    t = SEQ_TILE
    nt = s // t
    hd = ATTN_HEAD_DIM
    nh = ATTN_HEADS
    scale = 1.0 / math.sqrt(hd)
    slopes = _alibi_slopes()

    def body(q_ref, k_ref, v_ref, o_ref, lse_ref, do_ref, dq_ref, dk_ref, dv_ref,
             qb, kb, vb, dob, dsum, dq_acc):
        h = pl.program_id(0)
        slope = _select_by_index(h, slopes)
        qb[...] = q_ref[...].astype(BF16)
        kb[...] = k_ref[...].astype(BF16)
        vb[...] = v_ref[...].astype(BF16)
        do = do_ref[...]
        dob[...] = do.astype(BF16)
        dsum[...] = jnp.broadcast_to(jnp.sum(do * o_ref[...], axis=-1, keepdims=True), (s, hd))
        dq_acc[...] = jnp.zeros((s, hd), F32)
        base = _tile_delta(t, t)

        def over_keys(j, _):
            krows = pl.ds(pl.multiple_of(j * t, t), t)
            kj = kb[krows, :]
            vj = vb[krows, :]

            def over_queries(i, carry):
                dk, dv = carry
                qrows = pl.ds(pl.multiple_of(i * t, t), t)
                qi = qb[qrows, :]
                doi = dob[qrows, :]
                lse_i = lse_ref[qrows, :][:, :1]
                dsum_i = dsum[qrows, :][:, :1]
                sc = lax.dot_general(qi, kj, _NT_DIMS, preferred_element_type=F32) * scale
                valid, bias = _attn_mask_bias(base + (i - j) * t, slope)
                p = jnp.exp(jnp.where(valid, sc + bias, NEG_BIG) - lse_i)
                dp = lax.dot_general(doi, vj, _NT_DIMS, preferred_element_type=F32)
                ds = (p * (dp - dsum_i) * scale).astype(BF16)
                dv = dv + lax.dot_general(p.astype(BF16), doi, _TN_DIMS, preferred_element_type=F32)
                dk = dk + lax.dot_general(ds, qi, _TN_DIMS, preferred_element_type=F32)
                dq_acc[qrows, :] += jnp.dot(ds, kj, preferred_element_type=F32)
                return dk, dv

            zero = jnp.zeros((t, hd), F32)
            dk, dv = lax.fori_loop(j, nt, over_queries, (zero, zero))
            dk_ref[krows, :] = dk.astype(BF16)
            dv_ref[krows, :] = dv.astype(BF16)
            return 0

        lax.fori_loop(0, nt, over_keys, 0)
        dq_ref[...] = dq_acc[...].astype(BF16)

    def col(off):
        return pl.BlockSpec((s, hd), lambda h: (0, off + h))

    return pl.pallas_call(
        body, name="attn_bwd", grid=(nh,),
        in_specs=[col(0), col(nh), col(2 * nh), col(0), col(0), col(0)],
        out_specs=[col(0)] * 3,
        out_shape=[jax.ShapeDtypeStruct((s, ATTN_WIDTH), BF16)] * 3,
        scratch_shapes=[pltpu.VMEM((s, hd), BF16)] * 4 + [pltpu.VMEM((s, hd), F32)] * 2,
        compiler_params=_params(("arbitrary",)),
    )(proj, proj, proj, attn_out, lse, dmixed)


def _ret_log_gammas():
    return [math.log(1.0 - 2.0 ** (-5.0 - h)) for h in range(RET_HEADS)]


def _ret_decay(delta, log_gamma):
    dec = jnp.exp(delta.astype(F32) * log_gamma) * (1.0 / math.sqrt(RET_HEAD_DIM))
    return jnp.where(delta >= 0, dec, 0.0)


def _ret_fwd(proj):
    s = proj.shape[0]
    t = SEQ_TILE
    hd = RET_HEAD_DIM
    nh = RET_HEADS
    log_gammas = _ret_log_gammas()
    c0 = 3 * ATTN_WIDTH // hd

    def body(q_ref, k_ref, v_ref, g_ref, mix_ref, raw_ref):
        h = pl.program_id(0)
        i = pl.program_id(1)
        log_gamma = _select_by_index(h, log_gammas)
        q = q_ref[...].astype(BF16)
        base = _tile_delta(t, t)

        def step(j, acc):
            rows = pl.ds(pl.multiple_of(j * t, t), t)
            kj = k_ref[rows, :].astype(BF16)
            vj = v_ref[rows, :].astype(BF16)
            sc = lax.dot_general(q, kj, _NT_DIMS, preferred_element_type=F32)
            sc = sc * _ret_decay(base + (i - j) * t, log_gamma)
            return acc + jnp.dot(sc.astype(BF16), vj, preferred_element_type=F32)

        ret = lax.fori_loop(0, i + 1, step, jnp.zeros((t, hd), F32))
        raw_ref[...] = ret
        r = lax.rsqrt(jnp.mean(ret * ret, axis=-1, keepdims=True) + NORM_EPS)
        g = g_ref[...]
        mix_ref[...] = (g * _sigmoid(g) * (ret * r)).astype(BF16)

    return pl.pallas_call(
        body, name="ret_fwd", grid=(nh, s // t),
        in_specs=[pl.BlockSpec((t, hd), lambda h, i: (i, c0 + h)),
                  pl.BlockSpec((s, hd), lambda h, i: (0, c0 + nh + h)),
                  pl.BlockSpec((s, hd), lambda h, i: (0, c0 + 2 * nh + h)),
                  pl.BlockSpec((t, hd), lambda h, i: (i, c0 + 3 * nh + h))],
        out_specs=[pl.BlockSpec((t, hd), lambda h, i: (i, h))] * 2,
        out_shape=[jax.ShapeDtypeStruct((s, RET_WIDTH), BF16), jax.ShapeDtypeStruct((s, RET_WIDTH), F32)],
        compiler_params=_params(("parallel", "arbitrary")),
    )(proj, proj, proj, proj)


def _ret_bwd(proj, ret_raw, dmixed):
    s = proj.shape[0]
    t = SEQ_TILE
    nt = s // t
    hd = RET_HEAD_DIM
    nh = RET_HEADS
    log_gammas = _ret_log_gammas()
    c0 = 3 * ATTN_WIDTH // hd
    mixed_blocks = ATTN_WIDTH // hd

    def body(q_ref, k_ref, v_ref, g_ref, raw_ref, dmix_ref, dq_ref, dk_ref, dv_ref, dg_ref,
             qb, kb, vb, dretb, dq_acc):
        h = pl.program_id(0)
        log_gamma = _select_by_index(h, log_gammas)
        qb[...] = q_ref[...].astype(BF16)
        kb[...] = k_ref[...].astype(BF16)
        vb[...] = v_ref[...].astype(BF16)
        ret = raw_ref[...]
        r = lax.rsqrt(jnp.mean(ret * ret, axis=-1, keepdims=True) + NORM_EPS)
        normed = ret * r
        g = g_ref[...]
        sg = _sigmoid(g)
        dout = dmix_ref[...]
        dg_ref[...] = (dout * normed * sg * (1.0 + g * (1.0 - sg))).astype(BF16)
        dn = dout * g * sg
        dret = r * (dn - normed * jnp.mean(dn * normed, axis=-1, keepdims=True))
        dretb[...] = dret.astype(BF16)
        dq_acc[...] = jnp.zeros((s, hd), F32)
        base = _tile_delta(t, t)

        def over_keys(j, _):
            krows = pl.ds(pl.multiple_of(j * t, t), t)
            kj = kb[krows, :]
            vj = vb[krows, :]

            def over_queries(i, carry):
                dk, dv = carry
                qrows = pl.ds(pl.multiple_of(i * t, t), t)
                qi = qb[qrows, :]
                doi = dretb[qrows, :]
                dec = _ret_decay(base + (i - j) * t, log_gamma)
                a = (lax.dot_general(qi, kj, _NT_DIMS, preferred_element_type=F32) * dec).astype(BF16)
                da = (lax.dot_general(doi, vj, _NT_DIMS, preferred_element_type=F32) * dec).astype(BF16)
                dv = dv + lax.dot_general(a, doi, _TN_DIMS, preferred_element_type=F32)
                dk = dk + lax.dot_general(da, qi, _TN_DIMS, preferred_element_type=F32)
                dq_acc[qrows, :] += jnp.dot(da, kj, preferred_element_type=F32)
                return dk, dv

            zero = jnp.zeros((t, hd), F32)
            dk, dv = lax.fori_loop(j, nt, over_queries, (zero, zero))
            dk_ref[krows, :] = dk.astype(BF16)
            dv_ref[krows, :] = dv.astype(BF16)
            return 0

        lax.fori_loop(0, nt, over_keys, 0)
        dq_ref[...] = dq_acc[...].astype(BF16)

    def col(off):
        return pl.BlockSpec((s, hd), lambda h: (0, off + h))

    return pl.pallas_call(
        body, name="ret_bwd", grid=(nh,),
        in_specs=[col(c0), col(c0 + nh), col(c0 + 2 * nh), col(c0 + 3 * nh), col(0), col(mixed_blocks)],
        out_specs=[col(0)] * 4,
        out_shape=[jax.ShapeDtypeStruct((s, RET_WIDTH), BF16)] * 4,
        scratch_shapes=[pltpu.VMEM((s, hd), BF16)] * 4 + [pltpu.VMEM((s, hd), F32)],
        compiler_params=_params(("arbitrary",)),
    )(proj, proj, proj, proj, ret_raw, dmixed)


_FLIPS = (2, 1, 3)


def _other_chips(x, y):
    return [(1 - x, y), (x, 1 - y), (1 - x, 1 - y)]


def _all_gather_weights(col_shards, row_shards):
    n_col, n_row = len(col_shards), len(row_shards)
    n_w = n_col + n_row
    col_in = [w.reshape(2, w.shape[0] // 2, w.shape[1]) for w in col_shards]
    row_in = [w.reshape(2, w.shape[0] // 2, w.shape[1]) for w in row_shards]
    col_out = [jax.ShapeDtypeStruct((2, w.shape[1], N_CHIPS * w.shape[2]), BF16) for w in col_in]
    row_out = [jax.ShapeDtypeStruct((N_CHIPS, 2, w.shape[1], w.shape[2]), BF16) for w in row_in]

    def body(*refs):
        ins = refs[:n_w]
        outs = refs[n_w:2 * n_w]
        send_sems, recv_sems, local_sems = refs[2 * n_w:]
        x, y, c = _place()
        my_shard = 2 * x + y
        chips = _other_chips(x, y)
        sibling = (x, y, 1 - c)

        def region(w, shard, half):
            if w < n_col:
                cols = ins[w].shape[2]
                return outs[w].at[half, :, pl.ds(shard * cols, cols)]
            return outs[w].at[shard, half]

        def remote(w, k, src, dst, to):
            return pltpu.make_async_remote_copy(src_ref=src, dst_ref=dst, send_sem=send_sems.at[w, k],
                                                recv_sem=recv_sems.at[w, k], device_id=to, device_id_type=MESH)

        def start_all(ss):
            for w in range(n_w):
                for half in range(2):
                    pltpu.make_async_copy(ins[w].at[half], region(w, ss, half), local_sems.at[w, half]).start()
            for w in range(n_w):
                for j, chip in enumerate(chips):
                    remote(w, j, ins[w].at[c], region(w, ss, c), (*chip, c)).start()
            for w in range(n_w):
                for j in range(3):
                    landed = region(w, ss ^ _FLIPS[j], c)
                    remote(w, j, ins[w].at[c], landed, sibling).wait_recv()
                    remote(w, 3 + j, landed, landed, sibling).start()

        for ss in range(N_CHIPS):
            pl.when(my_shard == ss)(functools.partial(start_all, ss))

        for w in range(n_w):
            for j in range(3):
                anywhere = region(w, 0, 0)
                remote(w, 3 + j, anywhere, anywhere, sibling).wait_recv()
        for w in range(n_w):
            for k in range(6):
                anywhere = region(w, 0, 0)
                remote(w, k, anywhere, anywhere, sibling).wait_send()
            for half in range(2):
                pltpu.make_async_copy(ins[w].at[half], region(w, 0, half), local_sems.at[w, half]).wait()

    outs = pl.pallas_call(
        body, name="all_gather_weights",
        in_specs=[ANY] * n_w, out_specs=[ANY] * n_w, out_shape=col_out + row_out,
        scratch_shapes=[pltpu.SemaphoreType.DMA((n_w, 6)), pltpu.SemaphoreType.DMA((n_w, 6)),
                        pltpu.SemaphoreType.DMA((n_w, 2))],
        compiler_params=pltpu.CompilerParams(has_side_effects=True),
    )(*col_in, *row_in)
    full_col = [o.reshape(2 * o.shape[1], o.shape[2]) for o in outs[:n_col]]
    full_row = [o.reshape(N_CHIPS * 2 * o.shape[2], o.shape[3]) for o in outs[n_col:]]
    return full_col, full_row


def _exchange_halves(grads_col, grads_row):
    n_col, n_row = len(grads_col), len(grads_row)
    n_w = n_col + n_row
    col_in = [g.reshape(2, g.shape[0] // 2, g.shape[1]) for g in grads_col]
    row_in = [g.reshape(N_CHIPS, 2, g.shape[0] // (2 * N_CHIPS), g.shape[1]) for g in grads_row]
    out_shape = ([jax.ShapeDtypeStruct(g.shape[1:], BF16) for g in col_in]
                 + [jax.ShapeDtypeStruct((N_CHIPS,) + g.shape[2:], BF16) for g in row_in])

    def body(*refs):
        ins = refs[:n_w]
        outs = refs[n_w:2 * n_w]
        send_sems, recv_sems = refs[2 * n_w:]
        x, y, c = _place()
        copies = []
        for w in range(n_w):
            src = ins[w].at[1 - c] if w < n_col else ins[w].at[:, 1 - c]
            copies.append(pltpu.make_async_remote_copy(
                src_ref=src, dst_ref=outs[w], send_sem=send_sems.at[w], recv_sem=recv_sems.at[w],
                device_id=(x, y, 1 - c), device_id_type=MESH))
        for cp in copies:
            cp.start()
        for cp in copies:
            cp.wait()

    return pl.pallas_call(
        body, name="grad_exchange_halves",
        in_specs=[ANY] * n_w, out_specs=[ANY] * n_w, out_shape=out_shape,
        scratch_shapes=[pltpu.SemaphoreType.DMA((n_w,)), pltpu.SemaphoreType.DMA((n_w,))],
        compiler_params=pltpu.CompilerParams(has_side_effects=True),
    )(*col_in, *row_in), col_in, row_in


def _chip_sum_col(g3, sib, c_arr, name):
    _, hk, n = g3.shape
    cols = n // N_CHIPS
    tr = _row_tile(hk, cols * 2, limit=1024 * 1024)

    def body(c_ref, g_ref, s_ref, o_ref):
        del c_ref
        o_ref[...] = (g_ref[...].astype(F32) + s_ref[...].astype(F32)).astype(BF16)

    grid_spec = pltpu.PrefetchScalarGridSpec(
        num_scalar_prefetch=1, grid=(N_CHIPS, hk // tr),
        in_specs=[pl.BlockSpec((None, tr, cols), lambda p, r, c_ref: (c_ref[0], r, p)),
                  pl.BlockSpec((tr, cols), lambda p, r, c_ref: (r, p))],
        out_specs=pl.BlockSpec((None, tr, cols), lambda p, r, c_ref: (p, r, 0)))
    return pl.pallas_call(
        body, name=name, grid_spec=grid_spec,
        out_shape=jax.ShapeDtypeStruct((N_CHIPS, hk, cols), BF16),
        compiler_params=_params(("parallel", "parallel")),
    )(c_arr, g3, sib)


def _chip_sum_row(g4, sib, c_arr, name):
    _, _, hr, n = g4.shape
    tr = _row_tile(hr, n * 2, limit=1024 * 1024)

    def body(c_ref, g_ref, s_ref, o_ref):
        del c_ref
        o_ref[...] = (g_ref[...].astype(F32) + s_ref[...].astype(F32)).astype(BF16)

    grid_spec = pltpu.PrefetchScalarGridSpec(
        num_scalar_prefetch=1, grid=(N_CHIPS, hr // tr),
        in_specs=[pl.BlockSpec((None, None, tr, n), lambda p, r, c_ref: (p, c_ref[0], r, 0)),
                  pl.BlockSpec((None, tr, n), lambda p, r, c_ref: (p, r, 0))],
        out_specs=pl.BlockSpec((None, tr, n), lambda p, r, c_ref: (p, r, 0)))
    return pl.pallas_call(
        body, name=name, grid_spec=grid_spec,
        out_shape=jax.ShapeDtypeStruct((N_CHIPS, hr, n), BF16),
        compiler_params=_params(("parallel", "parallel")),
    )(c_arr, g4, sib)


def _scatter_pieces(pieces):
    n_w = len(pieces)
    out_shape = [jax.ShapeDtypeStruct((3,) + p.shape[1:], BF16) for p in pieces]

    def body(*refs):
        ins = refs[:n_w]
        outs = refs[n_w:2 * n_w]
        send_sems, recv_sems = refs[2 * n_w:]
        x, y, c = _place()
        copies = []
        for w in range(n_w):
            for j, (cx, cy) in enumerate(_other_chips(x, y)):
                copies.append(pltpu.make_async_remote_copy(
                    src_ref=ins[w].at[2 * cx + cy], dst_ref=outs[w].at[j],
                    send_sem=send_sems.at[w, j], recv_sem=recv_sems.at[w, j],
                    device_id=(cx, cy, c), device_id_type=MESH))
        for cp in copies:
            cp.start()
        for cp in copies:
            cp.wait()

    return pl.pallas_call(
        body, name="grad_scatter_pieces",
        in_specs=[ANY] * n_w, out_specs=[ANY] * n_w, out_shape=out_shape,
        scratch_shapes=[pltpu.SemaphoreType.DMA((n_w, 3)), pltpu.SemaphoreType.DMA((n_w, 3))],
        compiler_params=pltpu.CompilerParams(has_side_effects=True),
    )(*pieces)


def _sum_pieces(pieces, received, shard_arr, name):
    _, r, n = pieces.shape
    tr = _row_tile(r, n * 4, limit=1024 * 1024)

    def body(s_ref, own_ref, r0_ref, r1_ref, r2_ref, o_ref):
        del s_ref
        acc = own_ref[...].astype(F32) + r0_ref[...].astype(F32)
        acc = acc + r1_ref[...].astype(F32)
        o_ref[...] = acc + r2_ref[...].astype(F32)

    def recv_spec(j):
        return pl.BlockSpec((None, tr, n), lambda i, s_ref: (j, i, 0))

    grid_spec = pltpu.PrefetchScalarGridSpec(
        num_scalar_prefetch=1, grid=(r // tr,),
        in_specs=[pl.BlockSpec((None, tr, n), lambda i, s_ref: (s_ref[0], i, 0)),
                  recv_spec(0), recv_spec(1), recv_spec(2)],
        out_specs=pl.BlockSpec((tr, n), lambda i, s_ref: (i, 0)))
    return pl.pallas_call(
        body, name=name, grid_spec=grid_spec,
        out_shape=jax.ShapeDtypeStruct((r, n), F32),
        compiler_params=_params(("parallel",)),
    )(shard_arr, pieces, received, received, received)


def _join_halves(halves):
    n_w = len(halves)
    out_shape = [jax.ShapeDtypeStruct((2,) + h.shape, F32) for h in halves]

    def body(*refs):
        ins = refs[:n_w]
        outs = refs[n_w:2 * n_w]
        send_sems, recv_sems, local_sems = refs[2 * n_w:]
        x, y, c = _place()
        remote = [pltpu.make_async_remote_copy(
            src_ref=ins[w], dst_ref=outs[w].at[c], send_sem=send_sems.at[w], recv_sem=recv_sems.at[w],
            device_id=(x, y, 1 - c), device_id_type=MESH) for w in range(n_w)]
        local = [pltpu.make_async_copy(ins[w], outs[w].at[c], local_sems.at[w]) for w in range(n_w)]
        for cp in remote + local:
            cp.start()
        for cp in remote + local:
            cp.wait()

    outs = pl.pallas_call(
        body, name="grad_join_halves",
        in_specs=[ANY] * n_w, out_specs=[ANY] * n_w, out_shape=out_shape,
        scratch_shapes=[pltpu.SemaphoreType.DMA((n_w,))] * 3,
        compiler_params=pltpu.CompilerParams(has_side_effects=True),
    )(*halves)
    return [o.reshape(2 * o.shape[1], o.shape[2]) for o in outs]


def _norm_weights_step(parts, w, m, v):
    rows, d = parts.shape

    def body(p_ref, w_ref, m_ref, v_ref, g_ref, d_ref, mo_ref, vo_ref, gathered, send_sems, recv_sems):
        x, y, c = _place()
        me = 4 * x + 2 * y + c
        gathered[me] = p_ref[...]
        copies = []
        for k in range(1, N_DEV):
            peer = (x ^ ((k >> 2) & 1), y ^ ((k >> 1) & 1), c ^ (k & 1))
            copies.append(pltpu.make_async_remote_copy(
                src_ref=p_ref, dst_ref=gathered.at[me], send_sem=send_sems.at[k - 1],
                recv_sem=recv_sems.at[k - 1], device_id=peer, device_id_type=MESH))
        for cp in copies:
            cp.start()
        for cp in copies:
            cp.wait()
        g = gathered[0]
        for k in range(1, N_DEV):
            g = g + gathered[k]
        delta, m_new, v_new = _adamw_math(w_ref[...], g, m_ref[...], v_ref[...])
        g_ref[...] = g
        d_ref[...] = delta
        mo_ref[...] = m_new
        vo_ref[...] = v_new

    vmem = pl.BlockSpec(memory_space=pltpu.VMEM)
    shp = jax.ShapeDtypeStruct((rows, d), F32)
    return pl.pallas_call(
        body, name="norm_weights_step",
        in_specs=[vmem] * 4, out_specs=[vmem] * 4, out_shape=[shp] * 4,
        scratch_shapes=[pltpu.VMEM((N_DEV, rows, d), F32), pltpu.SemaphoreType.DMA((N_DEV - 1,)),
                        pltpu.SemaphoreType.DMA((N_DEV - 1,))],
        compiler_params=pltpu.CompilerParams(has_side_effects=True),
    )(parts, w, m, v)


def kernel(x, norm_mix_w, w_in, w_out, norm_ffn_w, w_gate, w_up, w_down, norm_final_w, loss_target, m_norm_mix_w, m_w_in, m_w_out, m_norm_ffn_w, m_w_gate, m_w_up, m_w_down, m_norm_final_w, v_norm_mix_w, v_w_in, v_w_out, v_norm_ffn_w, v_w_gate, v_w_up, v_w_down, v_norm_final_w):
    s, d = x.shape[1], x.shape[2]
    xs = x.reshape(s, d)
    target = loss_target.reshape(s, d)
    big = {"w_in": (w_in, m_w_in, v_w_in), "w_out": (w_out, m_w_out, v_w_out),
           "w_gate": (w_gate, m_w_gate, v_w_gate), "w_up": (w_up, m_w_up, v_w_up),
           "w_down": (w_down, m_w_down, v_w_down)}
    big = {k: tuple(a.reshape(a.shape[1:]) for a in t) for k, t in big.items()}
    col_names, row_names = ("w_in", "w_gate", "w_up"), ("w_out", "w_down")
    n_in = N_CHIPS * big["w_in"][0].shape[1]
    ffn = N_CHIPS * big["w_gate"][0].shape[1]
    mix = ATTN_WIDTH + RET_WIDTH
    c_arr = lax.axis_index("c").astype(I32).reshape(1)
    shard_arr = (2 * lax.axis_index("x") + lax.axis_index("y")).astype(I32).reshape(1)

    shards = {k: _cast_bf16(big[k][0], "cast_" + k) for k in big}
    full_col, full_row = _all_gather_weights([shards[k] for k in col_names], [shards[k] for k in row_names])
    wi, wg, wu = full_col
    wo, wd = full_row

    h1 = _rms_fwd(xs, norm_mix_w, "rms_mix_fwd")
    proj, = _matmul("in_proj", "nn", [h1], [wi], [0], s, n_in, d, 512, 1024, d, [], [F32], _epi_plain)
    attn_b, attn_o, lse = _attn_fwd(proj)
    ret_b, ret_raw = _ret_fwd(proj)
    mixed = jnp.concatenate([attn_b, ret_b], axis=1)
    x1, = _matmul("out_proj", "nn", [mixed], [wo], [0], s, d, mix, 512, 1024, mix, [xs], [F32], _epi_residual)
    h2 = _rms_fwd(x1, norm_ffn_w, "rms_ffn_fwd")
    gate, up, act = _matmul("gate_up", "nn", [h2, h2], [wg, wu], [0, 1], s, ffn, d, 512, 512, d, [],
                            [F32, F32, BF16], _epi_swiglu)
    x2, = _matmul("down_proj", "nn", [act], [wd], [0], s, d, ffn, 512, 1024, ffn // 4, [x1], [F32],
                  _epi_residual)
    loss_row, dx2, dx2b, dwf = _final_norm_loss(x2, norm_final_w.reshape(1, d), target, "final_norm_loss")

    dgate, dup = _matmul("d_act", "nt", [dx2b], [wd], [0], s, ffn, d, 512, 512, d, [gate, up],
                         [BF16, BF16], _epi_swiglu_bwd)
    g_wd, = _matmul("g_w_down", "tn", [act], [dx2b], [0], ffn, d, s, 512, 1024, s, [], [BF16], _epi_plain)
    dh2, = _matmul("d_h2", "nt", [dgate, dup], [wg, wu], [0, 0], s, d, ffn, 512, 512, ffn // 4, [], [F32],
                   _epi_plain)
    g_wg, = _matmul("g_w_gate", "tn", [h2], [dgate], [0], d, ffn, s, 512, 512, s, [], [BF16], _epi_plain)
    g_wu, = _matmul("g_w_up", "tn", [h2], [dup], [0], d, ffn, s, 512, 512, s, [], [BF16], _epi_plain)
    dx1, dx1b, dw_ffn = _rms_bwd(x1, norm_ffn_w, dh2, dx2, "rms_ffn_bwd")

    dmixed, = _matmul("d_mixed", "nt", [dx1b], [wo], [0], s, mix, d, 512, 1024, d, [], [F32], _epi_plain)
    g_wo, = _matmul("g_w_out", "tn", [mixed], [dx1b], [0], mix, d, s, 512, 1024, s, [], [BF16], _epi_plain)
    dqa, dka, dva = _attn_bwd(proj, attn_o, lse, dmixed)
    dqr, dkr, dvr, dgr = _ret_bwd(proj, ret_raw, dmixed)
    dproj = jnp.concatenate([dqa, dka, dva, dqr, dkr, dvr, dgr], axis=1)
    dh1, = _matmul("d_h1", "nt", [dproj], [wi], [0], s, d, n_in, 512, 512, n_in // 7, [], [F32], _epi_plain)
    g_wi, = _matmul("g_w_in", "tn", [h1], [dproj], [0], d, n_in, s, 512, 1024, s, [], [BF16], _epi_plain)
    grad_x, _, dw_mix = _rms_bwd(xs, norm_mix_w, dh1, dx1, "rms_mix_bwd")

    g_col = {"w_in": g_wi, "w_gate": g_wg, "w_up": g_wu}
    g_row = {"w_out": g_wo, "w_down": g_wd}
    sib, col_views, row_views = _exchange_halves([g_col[k] for k in col_names], [g_row[k] for k in row_names])
    pieces = [_chip_sum_col(col_views[i], sib[i], c_arr, "chip_sum_" + k) for i, k in enumerate(col_names)]
    pieces += [_chip_sum_row(row_views[i], sib[len(col_names) + i], c_arr, "chip_sum_" + k)
               for i, k in enumerate(row_names)]
    received = _scatter_pieces(pieces)
    names = col_names + row_names
    halves = [_sum_pieces(pieces[i], received[i], shard_arr, "sum_pieces_" + k) for i, k in enumerate(names)]
    grads = dict(zip(names, _join_halves(halves)))

    new = {k: _adamw(big[k][0], grads[k], big[k][1], big[k][2], "adamw_" + k) for k in names}

    def rows8(*vs):
        return jnp.concatenate([v.reshape(1, d) for v in vs] + [jnp.zeros((8 - len(vs), d), F32)], axis=0)

    ng, nd, nm, nv = _norm_weights_step(
        rows8(dw_mix, dw_ffn, dwf), rows8(norm_mix_w, norm_ffn_w, norm_final_w),
        rows8(m_norm_mix_w, m_norm_ffn_w, m_norm_final_w), rows8(v_norm_mix_w, v_norm_ffn_w, v_norm_final_w))

    loss = lax.psum(loss_row[0, 0], ("x", "y", "c"))

    def pack(small, per_weight):
        lead = lambda a: a.reshape((1,) + a.shape)
        return (small[0:1], lead(per_weight["w_in"]), lead(per_weight["w_out"]), small[1:2],
                lead(per_weight["w_gate"]), lead(per_weight["w_up"]), lead(per_weight["w_down"]), small[2])

    return (loss, grad_x.reshape(1, s, d),
            *pack(ng, grads),
            *pack(nd, {k: new[k][0] for k in names}),
            *pack(nm, {k: new[k][1] for k in names}),
            *pack(nv, {k: new[k][2] for k in names}))
```

```python
import functools
import math

import jax
import jax.numpy as jnp
from jax import lax
from jax.experimental import pallas as pl
from jax.experimental.pallas import tpu as pltpu

F32 = jnp.float32
BF16 = jnp.bfloat16
I32 = jnp.int32
MESH = pl.DeviceIdType.MESH
ANY = pl.BlockSpec(memory_space=pl.ANY)

ATTN_HEADS = 8
ATTN_HEAD_DIM = 128
RET_HEADS = 4
RET_HEAD_DIM = 256
ATTN_WIDTH = ATTN_HEADS * ATTN_HEAD_DIM
RET_WIDTH = RET_HEADS * RET_HEAD_DIM
DILATED_PATTERNS = ((128, 1), (512, 4), (2048, 16))
NORM_EPS = 1e-6
ADAM_LR = 0.001
ADAM_B1 = 0.9
ADAM_B2 = 0.999
ADAM_EPS = 1e-08
ADAM_WD = 0.01
ADAM_STEP = 10

N_CHIPS = 4
N_DEV = 8
NEG_BIG = -1e30
SEQ_TILE = 256
VMEM_LIMIT_BYTES = 56 * 1024 * 1024


def _params(semantics=None, vmem=VMEM_LIMIT_BYTES):
    return pltpu.CompilerParams(dimension_semantics=semantics, vmem_limit_bytes=vmem)


def _row_tile(rows, row_bytes, limit=2 * 1024 * 1024, mult=16):
    best = None
    for t in range(mult, rows + 1, mult):
        if rows % t == 0 and t * row_bytes <= limit:
            best = t
    assert best is not None, (rows, row_bytes)
    return best


def _sigmoid(x):
    return 1.0 / (1.0 + jnp.exp(-x))


def _select_by_index(idx, values):
    out = jnp.float32(values[-1])
    for i in range(len(values) - 2, -1, -1):
        out = jnp.where(idx == i, jnp.float32(values[i]), out)
    return out


def _place():
    x, y, c = lax.axis_index("x"), lax.axis_index("y"), lax.axis_index("c")
    return x, y, c


def _cast_into_full(w, shard_arr, column_sharded, name):
    rows, cols = w.shape
    tr = _row_tile(rows, cols * 4)
    steps = rows // tr
    if column_sharded:
        out_shape, out_map = (rows, N_CHIPS * cols), (lambda i, s_ref: (i, s_ref[0]))
    else:
        out_shape, out_map = (N_CHIPS * rows, cols), (lambda i, s_ref: (s_ref[0] * steps + i, 0))

    def body(s_ref, w_ref, o_ref):
        del s_ref
        o_ref[...] = w_ref[...].astype(BF16)

    grid_spec = pltpu.PrefetchScalarGridSpec(
        num_scalar_prefetch=1, grid=(steps,),
        in_specs=[pl.BlockSpec((tr, cols), lambda i, s_ref: (i, 0))],
        out_specs=pl.BlockSpec((tr, cols), out_map))
    return pl.pallas_call(
        body, name=name, grid_spec=grid_spec,
        out_shape=jax.ShapeDtypeStruct(out_shape, BF16),
        compiler_params=_params(("parallel",)),
    )(shard_arr, w)


def _rms_fwd(x, w, name):
    rows, d = x.shape
    tr = 256

    def body(x_ref, w_ref, h_ref):
        xv = x_ref[...]
        r = lax.rsqrt(jnp.mean(xv * xv, axis=-1, keepdims=True) + NORM_EPS)
        h_ref[...] = (xv * r * w_ref[...]).astype(BF16)

    return pl.pallas_call(
        body, name=name, grid=(rows // tr,),
        in_specs=[pl.BlockSpec((tr, d), lambda i: (i, 0)), pl.BlockSpec((1, d), lambda i: (0, 0))],
        out_specs=pl.BlockSpec((tr, d), lambda i: (i, 0)),
        out_shape=jax.ShapeDtypeStruct((rows, d), BF16),
        compiler_params=_params(("parallel",)),
    )(x, w)


def _rms_bwd(x, w, dh, dres, name):
    rows, d = x.shape
    tr = 256

    def body(x_ref, w_ref, dh_ref, dres_ref, dx_ref, dxb_ref, dw_ref):
        xv = x_ref[...]
        r = lax.rsqrt(jnp.mean(xv * xv, axis=-1, keepdims=True) + NORM_EPS)
        xhat = xv * r
        dy = dh_ref[...]
        dxhat = dy * w_ref[...]
        dx = dres_ref[...] + r * (dxhat - xhat * jnp.mean(dxhat * xhat, axis=-1, keepdims=True))
        dx_ref[...] = dx
        dxb_ref[...] = dx.astype(BF16)
        part = jnp.sum(dy * xhat, axis=0, keepdims=True)

        @pl.when(pl.program_id(0) == 0)
        def _():
            dw_ref[...] = part

        @pl.when(pl.program_id(0) != 0)
        def _():
            dw_ref[...] += part

    row = pl.BlockSpec((tr, d), lambda i: (i, 0))
    vec = pl.BlockSpec((1, d), lambda i: (0, 0))
    return pl.pallas_call(
        body, name=name, grid=(rows // tr,),
        in_specs=[row, vec, row, row],
        out_specs=[row, row, vec],
        out_shape=[jax.ShapeDtypeStruct((rows, d), F32), jax.ShapeDtypeStruct((rows, d), BF16),
                   jax.ShapeDtypeStruct((1, d), F32)],
        compiler_params=_params(("arbitrary",)),
    )(x, w, dh, dres)


def _final_norm_loss(x2, w, target, name):
    rows, d = x2.shape
    tr = 256

    def body(x_ref, w_ref, t_ref, loss_ref, dx_ref, dxb_ref, dw_ref):
        xv = x_ref[...]
        wv = w_ref[...]
        r = lax.rsqrt(jnp.mean(xv * xv, axis=-1, keepdims=True) + NORM_EPS)
        xhat = xv * r
        err = xhat * wv - t_ref[...]
        part_loss = 0.5 * jnp.sum(jnp.mean(err * err, axis=-1, keepdims=True), axis=0, keepdims=True)
        dy = err * (1.0 / d)
        dxhat = dy * wv
        dx = r * (dxhat - xhat * jnp.mean(dxhat * xhat, axis=-1, keepdims=True))
        dx_ref[...] = dx
        dxb_ref[...] = dx.astype(BF16)
        part_dw = jnp.sum(dy * xhat, axis=0, keepdims=True)
        part_loss = jnp.broadcast_to(part_loss, (1, 128))

        @pl.when(pl.program_id(0) == 0)
        def _():
            dw_ref[...] = part_dw
            loss_ref[...] = part_loss

        @pl.when(pl.program_id(0) != 0)
        def _():
            dw_ref[...] += part_dw
            loss_ref[...] += part_loss

    row = pl.BlockSpec((tr, d), lambda i: (i, 0))
    vec = pl.BlockSpec((1, d), lambda i: (0, 0))
    return pl.pallas_call(
        body, name=name, grid=(rows // tr,),
        in_specs=[row, vec, row],
        out_specs=[pl.BlockSpec((1, 128), lambda i: (0, 0)), row, row, vec],
        out_shape=[jax.ShapeDtypeStruct((1, 128), F32), jax.ShapeDtypeStruct((rows, d), F32),
                   jax.ShapeDtypeStruct((rows, d), BF16), jax.ShapeDtypeStruct((1, d), F32)],
        compiler_params=_params(("arbitrary",)),
    )(x2, w, target)


def _adamw_math(w, g, m, v):
    m = ADAM_B1 * m + (1.0 - ADAM_B1) * g
    v = ADAM_B2 * v + (1.0 - ADAM_B2) * (g * g)
    m_hat = m / (1.0 - ADAM_B1 ** ADAM_STEP)
    v_hat = v / (1.0 - ADAM_B2 ** ADAM_STEP)
    delta = -ADAM_LR * (m_hat / (jnp.sqrt(v_hat) + ADAM_EPS) + ADAM_WD * w)
    return delta, m, v


def _adamw(w, g, m, v, name):
    rows, cols = w.shape
    tr = _row_tile(rows, cols * 4, limit=1024 * 1024)

    def body(w_ref, g_ref, m_ref, v_ref, d_ref, mo_ref, vo_ref):
        delta, m_new, v_new = _adamw_math(w_ref[...], g_ref[...], m_ref[...], v_ref[...])
        d_ref[...] = delta
        mo_ref[...] = m_new
        vo_ref[...] = v_new

    blk = pl.BlockSpec((tr, cols), lambda i: (i, 0))
    shp = jax.ShapeDtypeStruct((rows, cols), F32)
    return pl.pallas_call(
        body, name=name, grid=(rows // tr,),
        in_specs=[blk] * 4, out_specs=[blk] * 3, out_shape=[shp] * 3,
        compiler_params=_params(("parallel",)),
    )(w, g, m, v)


_DOT_DIMS = {"nn": ((1,), (0,)), "nt": ((1,), (1,)), "tn": ((0,), (0,))}


def _matmul(name, mode, a_list, b_list, acc_of, m, n, k, tm, tn, tk, extras, out_dtypes, epilogue,
            a_koff=None, b_koff=None):
    assert m % tm == 0 and n % tn == 0 and k % tk == 0, (name, m, n, k, tm, tn, tk)
    nk = k // tk
    n_acc = max(acc_of) + 1
    n_pairs = len(a_list)
    a_koff = a_koff or [0] * n_pairs
    b_koff = b_koff or [0] * n_pairs
    dims = (_DOT_DIMS[mode], ((), ()))
    n_ext, n_out = len(extras), len(out_dtypes)

    def body(*refs):
        a_refs = refs[:n_pairs]
        b_refs = refs[n_pairs:2 * n_pairs]
        e_refs = refs[2 * n_pairs:2 * n_pairs + n_ext]
        o_refs = refs[2 * n_pairs + n_ext:2 * n_pairs + n_ext + n_out]
        acc_refs = refs[2 * n_pairs + n_ext + n_out:]
        parts = [None] * n_acc
        for p in range(n_pairs):
            d = lax.dot_general(a_refs[p][...], b_refs[p][...], dims, preferred_element_type=F32)
            parts[acc_of[p]] = d if parts[acc_of[p]] is None else parts[acc_of[p]] + d

        def finish(accs):
            outs = epilogue(accs, [e[...] for e in e_refs])
            for o_ref, o in zip(o_refs, outs):
                o_ref[...] = o.astype(o_ref.dtype)

        if nk == 1:
            finish(parts)
        else:
            kk = pl.program_id(2)

            @pl.when(kk == 0)
            def _():
                for acc_ref, part in zip(acc_refs, parts):
                    acc_ref[...] = part

            @pl.when(kk != 0)
            def _():
                for acc_ref, part in zip(acc_refs, parts):
                    acc_ref[...] += part

            @pl.when(kk == nk - 1)
            def _():
                finish([acc_ref[...] for acc_ref in acc_refs])

    def a_spec(off):
        if mode == "tn":
            return pl.BlockSpec((tk, tm), lambda i, j, kk: (kk + off, i))
        return pl.BlockSpec((tm, tk), lambda i, j, kk: (i, kk + off))

    def b_spec(off):
        if mode == "nt":
            return pl.BlockSpec((tn, tk), lambda i, j, kk: (j, kk + off))
        return pl.BlockSpec((tk, tn), lambda i, j, kk: (kk + off, j))

    tile = pl.BlockSpec((tm, tn), lambda i, j, kk: (i, j))
    scratch = [pltpu.VMEM((tm, tn), F32) for _ in range(n_acc)] if nk > 1 else []
    return pl.pallas_call(
        body, name=name, grid=(m // tm, n // tn, nk),
        in_specs=[a_spec(o) for o in a_koff] + [b_spec(o) for o in b_koff] + [tile] * n_ext,
        out_specs=[tile] * n_out,
        out_shape=[jax.ShapeDtypeStruct((m, n), dt) for dt in out_dtypes],
        scratch_shapes=scratch,
        compiler_params=_params(("parallel", "parallel", "arbitrary")),
    )(*a_list, *b_list, *extras)


def _epi_plain(accs, extras):
    return (accs[0],)


def _epi_residual(accs, extras):
    return (accs[0] + extras[0],)


def _epi_two(accs, extras):
    return accs[0], accs[1]


def _epi_swiglu(accs, extras):
    g, u = accs
    return g, u, g * _sigmoid(g) * u


def _epi_swiglu_bwd(accs, extras):
    da = accs[0]
    g, u = extras
    sg = _sigmoid(g)
    dg = da * u * sg * (1.0 + g * (1.0 - sg))
    du = da * g * sg
    return dg, du


_NT_DIMS = (((1,), (1,)), ((), ()))
_TN_DIMS = (((0,), (0,)), ((), ()))


def _tile_delta(tq, tk):
    return lax.broadcasted_iota(I32, (tq, tk), 0) - lax.broadcasted_iota(I32, (tq, tk), 1)


def _attn_mask_bias(delta, slope):
    count = jnp.zeros(delta.shape, I32)
    for window, dilation in DILATED_PATTERNS:
        hit = ((delta & (dilation - 1)) == 0) & (delta <= window)
        count = count + jnp.where(hit, 1, 0)
    valid = (delta >= 0) & (count > 0)
    logm = jnp.where(count == 3, math.log(3.0), jnp.where(count == 2, math.log(2.0), 0.0))
    return valid, logm - slope * delta.astype(F32)


def _alibi_slopes():
    return [2.0 ** (-8.0 * (h + 1) / ATTN_HEADS) for h in range(ATTN_HEADS)]


def _attn_fwd(proj):
    s = proj.shape[0]
    t = SEQ_TILE
    hd = ATTN_HEAD_DIM
    nh = ATTN_HEADS
    scale = 1.0 / math.sqrt(hd)
    slopes = _alibi_slopes()

    def body(q_ref, k_ref, v_ref, mix_ref, o_ref, lse_ref):
        h = pl.program_id(0)
        i = pl.program_id(1)
        slope = _select_by_index(h, slopes)
        q = q_ref[...].astype(BF16)
        base = _tile_delta(t, t)

        def step(j, carry):
            m_i, l_i, acc = carry
            rows = pl.ds(pl.multiple_of(j * t, t), t)
            kj = k_ref[rows, :].astype(BF16)
            vj = v_ref[rows, :].astype(BF16)
            sc = lax.dot_general(q, kj, _NT_DIMS, preferred_element_type=F32) * scale
            valid, bias = _attn_mask_bias(base + (i - j) * t, slope)
            sc = jnp.where(valid, sc + bias, NEG_BIG)
            m_new = jnp.maximum(m_i, jnp.max(sc, axis=-1, keepdims=True))
            p = jnp.exp(sc - m_new)
            alpha = jnp.exp(m_i - m_new)
            l_new = alpha * l_i + jnp.sum(p, axis=-1, keepdims=True)
            acc = alpha * acc + jnp.dot(p.astype(BF16), vj, preferred_element_type=F32)
            return m_new, l_new, acc

        init = (jnp.full((t, 1), NEG_BIG, F32), jnp.zeros((t, 1), F32), jnp.zeros((t, hd), F32))
        m_i, l_i, acc = lax.fori_loop(0, i + 1, step, init)
        out = acc / l_i
        o_ref[...] = out
        mix_ref[...] = out.astype(BF16)
        lse_ref[...] = jnp.broadcast_to(m_i + jnp.log(l_i), (t, hd))

    return pl.pallas_call(
        body, name="attn_fwd", grid=(nh, s // t),
        in_specs=[pl.BlockSpec((t, hd), lambda h, i: (i, h)),
                  pl.BlockSpec((s, hd), lambda h, i: (0, nh + h)),
                  pl.BlockSpec((s, hd), lambda h, i: (0, 2 * nh + h))],
        out_specs=[pl.BlockSpec((t, hd), lambda h, i: (i, h))] * 3,
        out_shape=[jax.ShapeDtypeStruct((s, ATTN_WIDTH), BF16),
                   jax.ShapeDtypeStruct((s, ATTN_WIDTH), F32),
                   jax.ShapeDtypeStruct((s, ATTN_WIDTH), F32)],
        compiler_params=_params(("parallel", "arbitrary")),
    )(proj, proj, proj)


def _attn_bwd(proj, attn_out, lse, dmixed):
    s = proj.shape[0]
    t = SEQ_TILE
    nt = s // t
    hd = ATTN_HEAD_DIM
    nh = ATTN_HEADS
    scale = 1.0 / math.sqrt(hd)
    slopes = _alibi_slopes()

    def body(q_ref, k_ref, v_ref, o_ref, lse_ref, do_ref, dq_ref, dk_ref, dv_ref,
             qb, kb, vb, dob, dsum, dq_acc):
        h = pl.program_id(0)
        slope = _select_by_index(h, slopes)
        qb[...] = q_ref[...].astype(BF16)
        kb[...] = k_ref[...].astype(BF16)
        vb[...] = v_ref[...].astype(BF16)
        do = do_ref[...]
        dob[...] = do.astype(BF16)
        dsum[...] = jnp.broadcast_to(jnp.sum(do * o_ref[...], axis=-1, keepdims=True), (s, hd))
        dq_acc[...] = jnp.zeros((s, hd), F32)
        base = _tile_delta(t, t)

        def over_keys(j, _):
            krows = pl.ds(pl.multiple_of(j * t, t), t)
            kj = kb[krows, :]
            vj = vb[krows, :]

            def over_queries(i, carry):
                dk, dv = carry
                qrows = pl.ds(pl.multiple_of(i * t, t), t)
                qi = qb[qrows, :]
                doi = dob[qrows, :]
                lse_i = lse_ref[qrows, :][:, :1]
                dsum_i = dsum[qrows, :][:, :1]
                sc = lax.dot_general(qi, kj, _NT_DIMS, preferred_element_type=F32) * scale
                valid, bias = _attn_mask_bias(base + (i - j) * t, slope)
                p = jnp.exp(jnp.where(valid, sc + bias, NEG_BIG) - lse_i)
                dp = lax.dot_general(doi, vj, _NT_DIMS, preferred_element_type=F32)
                ds = (p * (dp - dsum_i) * scale).astype(BF16)
                dv = dv + lax.dot_general(p.astype(BF16), doi, _TN_DIMS, preferred_element_type=F32)
                dk = dk + lax.dot_general(ds, qi, _TN_DIMS, preferred_element_type=F32)
                dq_acc[qrows, :] += jnp.dot(ds, kj, preferred_element_type=F32)
                return dk, dv

            zero = jnp.zeros((t, hd), F32)
            dk, dv = lax.fori_loop(j, nt, over_queries, (zero, zero))
            dk_ref[krows, :] = dk.astype(BF16)
            dv_ref[krows, :] = dv.astype(BF16)
            return 0

        lax.fori_loop(0, nt, over_keys, 0)
        dq_ref[...] = dq_acc[...].astype(BF16)

    def col(off):
        return pl.BlockSpec((s, hd), lambda h: (0, off + h))

    return pl.pallas_call(
        body, name="attn_bwd", grid=(nh,),
        in_specs=[col(0), col(nh), col(2 * nh), col(0), col(0), col(0)],
        out_specs=[col(0)] * 3,
        out_shape=[jax.ShapeDtypeStruct((s, ATTN_WIDTH), BF16)] * 3,
        scratch_shapes=[pltpu.VMEM((s, hd), BF16)] * 4 + [pltpu.VMEM((s, hd), F32)] * 2,
        compiler_params=_params(("arbitrary",)),
    )(proj, proj, proj, attn_out, lse, dmixed)


def _ret_log_gammas():
    return [math.log(1.0 - 2.0 ** (-5.0 - h)) for h in range(RET_HEADS)]


def _ret_decay(delta, log_gamma):
    dec = jnp.exp(delta.astype(F32) * log_gamma) * (1.0 / math.sqrt(RET_HEAD_DIM))
    return jnp.where(delta >= 0, dec, 0.0)


def _ret_fwd(proj):
    s = proj.shape[0]
    t = SEQ_TILE
    hd = RET_HEAD_DIM
    nh = RET_HEADS
    log_gammas = _ret_log_gammas()
    c0 = 3 * ATTN_WIDTH // hd

    def body(q_ref, k_ref, v_ref, g_ref, mix_ref, raw_ref):
        h = pl.program_id(0)
        i = pl.program_id(1)
        log_gamma = _select_by_index(h, log_gammas)
        q = q_ref[...].astype(BF16)
        base = _tile_delta(t, t)

        def step(j, acc):
            rows = pl.ds(pl.multiple_of(j * t, t), t)
            kj = k_ref[rows, :].astype(BF16)
            vj = v_ref[rows, :].astype(BF16)
            sc = lax.dot_general(q, kj, _NT_DIMS, preferred_element_type=F32)
            sc = sc * _ret_decay(base + (i - j) * t, log_gamma)
            return acc + jnp.dot(sc.astype(BF16), vj, preferred_element_type=F32)

        ret = lax.fori_loop(0, i + 1, step, jnp.zeros((t, hd), F32))
        raw_ref[...] = ret
        r = lax.rsqrt(jnp.mean(ret * ret, axis=-1, keepdims=True) + NORM_EPS)
        g = g_ref[...]
        mix_ref[...] = (g * _sigmoid(g) * (ret * r)).astype(BF16)

    return pl.pallas_call(
        body, name="ret_fwd", grid=(nh, s // t),
        in_specs=[pl.BlockSpec((t, hd), lambda h, i: (i, c0 + h)),
                  pl.BlockSpec((s, hd), lambda h, i: (0, c0 + nh + h)),
                  pl.BlockSpec((s, hd), lambda h, i: (0, c0 + 2 * nh + h)),
                  pl.BlockSpec((t, hd), lambda h, i: (i, c0 + 3 * nh + h))],
        out_specs=[pl.BlockSpec((t, hd), lambda h, i: (i, h))] * 2,
        out_shape=[jax.ShapeDtypeStruct((s, RET_WIDTH), BF16), jax.ShapeDtypeStruct((s, RET_WIDTH), F32)],
        compiler_params=_params(("parallel", "arbitrary")),
    )(proj, proj, proj, proj)


def _ret_bwd(proj, ret_raw, dmixed):
    s = proj.shape[0]
    t = SEQ_TILE
    nt = s // t
    hd = RET_HEAD_DIM
    nh = RET_HEADS
    log_gammas = _ret_log_gammas()
    c0 = 3 * ATTN_WIDTH // hd
    mixed_blocks = ATTN_WIDTH // hd

    def body(q_ref, k_ref, v_ref, g_ref, raw_ref, dmix_ref, dq_ref, dk_ref, dv_ref, dg_ref,
             qb, kb, vb, dretb, dq_acc):
        h = pl.program_id(0)
        log_gamma = _select_by_index(h, log_gammas)
        qb[...] = q_ref[...].astype(BF16)
        kb[...] = k_ref[...].astype(BF16)
        vb[...] = v_ref[...].astype(BF16)
        ret = raw_ref[...]
        r = lax.rsqrt(jnp.mean(ret * ret, axis=-1, keepdims=True) + NORM_EPS)
        normed = ret * r
        g = g_ref[...]
        sg = _sigmoid(g)
        dout = dmix_ref[...]
        dg_ref[...] = (dout * normed * sg * (1.0 + g * (1.0 - sg))).astype(BF16)
        dn = dout * g * sg
        dret = r * (dn - normed * jnp.mean(dn * normed, axis=-1, keepdims=True))
        dretb[...] = dret.astype(BF16)
        dq_acc[...] = jnp.zeros((s, hd), F32)
        base = _tile_delta(t, t)

        def over_keys(j, _):
            krows = pl.ds(pl.multiple_of(j * t, t), t)
            kj = kb[krows, :]
            vj = vb[krows, :]

            def over_queries(i, carry):
                dk, dv = carry
                qrows = pl.ds(pl.multiple_of(i * t, t), t)
                qi = qb[qrows, :]
                doi = dretb[qrows, :]
                dec = _ret_decay(base + (i - j) * t, log_gamma)
                a = (lax.dot_general(qi, kj, _NT_DIMS, preferred_element_type=F32) * dec).astype(BF16)
                da = (lax.dot_general(doi, vj, _NT_DIMS, preferred_element_type=F32) * dec).astype(BF16)
                dv = dv + lax.dot_general(a, doi, _TN_DIMS, preferred_element_type=F32)
                dk = dk + lax.dot_general(da, qi, _TN_DIMS, preferred_element_type=F32)
                dq_acc[qrows, :] += jnp.dot(da, kj, preferred_element_type=F32)
                return dk, dv

            zero = jnp.zeros((t, hd), F32)
            dk, dv = lax.fori_loop(j, nt, over_queries, (zero, zero))
            dk_ref[krows, :] = dk.astype(BF16)
            dv_ref[krows, :] = dv.astype(BF16)
            return 0

        lax.fori_loop(0, nt, over_keys, 0)
        dq_ref[...] = dq_acc[...].astype(BF16)

    def col(off):
        return pl.BlockSpec((s, hd), lambda h: (0, off + h))

    return pl.pallas_call(
        body, name="ret_bwd", grid=(nh,),
        in_specs=[col(c0), col(c0 + nh), col(c0 + 2 * nh), col(c0 + 3 * nh), col(0), col(mixed_blocks)],
        out_specs=[col(0)] * 4,
        out_shape=[jax.ShapeDtypeStruct((s, RET_WIDTH), BF16)] * 4,
        scratch_shapes=[pltpu.VMEM((s, hd), BF16)] * 4 + [pltpu.VMEM((s, hd), F32)],
        compiler_params=_params(("arbitrary",)),
    )(proj, proj, proj, proj, ret_raw, dmixed)


_FLIPS = (2, 1, 3)


def _other_chips(x, y):
    return [(1 - x, y), (x, 1 - y), (1 - x, 1 - y)]


def _all_gather_weights(col_full, row_full):
    n_col, n_row = len(col_full), len(row_full)
    n_w = n_col + n_row
    views = [w.reshape(2, w.shape[0] // 2, w.shape[1]) for w in col_full]
    views += [w.reshape(N_CHIPS, 2, w.shape[0] // (2 * N_CHIPS), w.shape[1]) for w in row_full]

    def body(*refs):
        bufs = refs[n_w:2 * n_w]
        send_sems, recv_sems = refs[2 * n_w:]
        x, y, c = _place()
        my_shard = 2 * x + y
        chips = _other_chips(x, y)
        sibling = (x, y, 1 - c)

        def region(w, shard, half):
            if w < n_col:
                cols = bufs[w].shape[2] // N_CHIPS
                return bufs[w].at[half, :, pl.ds(shard * cols, cols)]
            return bufs[w].at[shard, half]

        def remote(w, k, where, to):
            return pltpu.make_async_remote_copy(src_ref=where, dst_ref=where, send_sem=send_sems.at[w, k],
                                                recv_sem=recv_sems.at[w, k], device_id=to, device_id_type=MESH)

        def start_all(ss):
            for w in range(n_w):
                for j, chip in enumerate(chips):
                    remote(w, j, region(w, ss, c), (*chip, c)).start()
            for w in range(n_w):
                for j in range(3):
                    landed = region(w, ss ^ _FLIPS[j], c)
                    remote(w, j, landed, sibling).wait_recv()
                    remote(w, 3 + j, landed, sibling).start()

        for ss in range(N_CHIPS):
            pl.when(my_shard == ss)(functools.partial(start_all, ss))

        for w in range(n_w):
            for j in range(3):
                remote(w, 3 + j, region(w, 0, 0), sibling).wait_recv()
        for w in range(n_w):
            for k in range(6):
                remote(w, k, region(w, 0, 0), sibling).wait_send()

    outs = pl.pallas_call(
        body, name="all_gather_weights",
        in_specs=[ANY] * n_w, out_specs=[ANY] * n_w,
        out_shape=[jax.ShapeDtypeStruct(v.shape, BF16) for v in views],
        input_output_aliases={w: w for w in range(n_w)},
        scratch_shapes=[pltpu.SemaphoreType.DMA((n_w, 6)), pltpu.SemaphoreType.DMA((n_w, 6))],
        compiler_params=pltpu.CompilerParams(has_side_effects=True),
    )(*views)
    full_col = [o.reshape(2 * o.shape[1], o.shape[2]) for o in outs[:n_col]]
    full_row = [o.reshape(N_CHIPS * 2 * o.shape[2], o.shape[3]) for o in outs[n_col:]]
    return full_col, full_row


def _exchange_halves(grads_col, grads_row):
    n_col, n_row = len(grads_col), len(grads_row)
    n_w = n_col + n_row
    col_in = [g.reshape(2, g.shape[0] // 2, g.shape[1]) for g in grads_col]
    row_in = [g.reshape(N_CHIPS, 2, g.shape[0] // (2 * N_CHIPS), g.shape[1]) for g in grads_row]
    out_shape = ([jax.ShapeDtypeStruct(g.shape[1:], BF16) for g in col_in]
                 + [jax.ShapeDtypeStruct((N_CHIPS,) + g.shape[2:], BF16) for g in row_in])

    def body(*refs):
        ins = refs[:n_w]
        outs = refs[n_w:2 * n_w]
        send_sems, recv_sems = refs[2 * n_w:]
        x, y, c = _place()
        copies = []
        for w in range(n_w):
            src = ins[w].at[1 - c] if w < n_col else ins[w].at[:, 1 - c]
            copies.append(pltpu.make_async_remote_copy(
                src_ref=src, dst_ref=outs[w], send_sem=send_sems.at[w], recv_sem=recv_sems.at[w],
                device_id=(x, y, 1 - c), device_id_type=MESH))
        for cp in copies:
            cp.start()
        for cp in copies:
            cp.wait()

    return pl.pallas_call(
        body, name="grad_exchange_halves",
        in_specs=[ANY] * n_w, out_specs=[ANY] * n_w, out_shape=out_shape,
        scratch_shapes=[pltpu.SemaphoreType.DMA((n_w,)), pltpu.SemaphoreType.DMA((n_w,))],
        compiler_params=pltpu.CompilerParams(has_side_effects=True),
    )(*col_in, *row_in), col_in, row_in


def _chip_sum_col(g3, sib, c_arr, name):
    _, hk, n = g3.shape
    cols = n // N_CHIPS
    tr = _row_tile(hk, cols * 2, limit=1024 * 1024)

    def body(c_ref, g_ref, s_ref, o_ref):
        del c_ref
        o_ref[...] = (g_ref[...].astype(F32) + s_ref[...].astype(F32)).astype(BF16)

    grid_spec = pltpu.PrefetchScalarGridSpec(
        num_scalar_prefetch=1, grid=(N_CHIPS, hk // tr),
        in_specs=[pl.BlockSpec((None, tr, cols), lambda p, r, c_ref: (c_ref[0], r, p)),
                  pl.BlockSpec((tr, cols), lambda p, r, c_ref: (r, p))],
        out_specs=pl.BlockSpec((None, tr, cols), lambda p, r, c_ref: (p, r, 0)))
    return pl.pallas_call(
        body, name=name, grid_spec=grid_spec,
        out_shape=jax.ShapeDtypeStruct((N_CHIPS, hk, cols), BF16),
        compiler_params=_params(("parallel", "parallel")),
    )(c_arr, g3, sib)


def _chip_sum_row(g4, sib, c_arr, name):
    _, _, hr, n = g4.shape
    tr = _row_tile(hr, n * 2, limit=1024 * 1024)

    def body(c_ref, g_ref, s_ref, o_ref):
        del c_ref
        o_ref[...] = (g_ref[...].astype(F32) + s_ref[...].astype(F32)).astype(BF16)

    grid_spec = pltpu.PrefetchScalarGridSpec(
        num_scalar_prefetch=1, grid=(N_CHIPS, hr // tr),
        in_specs=[pl.BlockSpec((None, None, tr, n), lambda p, r, c_ref: (p, c_ref[0], r, 0)),
                  pl.BlockSpec((None, tr, n), lambda p, r, c_ref: (p, r, 0))],
        out_specs=pl.BlockSpec((None, tr, n), lambda p, r, c_ref: (p, r, 0)))
    return pl.pallas_call(
        body, name=name, grid_spec=grid_spec,
        out_shape=jax.ShapeDtypeStruct((N_CHIPS, hr, n), BF16),
        compiler_params=_params(("parallel", "parallel")),
    )(c_arr, g4, sib)


def _scatter_pieces(pieces):
    n_w = len(pieces)
    out_shape = [jax.ShapeDtypeStruct((3,) + p.shape[1:], BF16) for p in pieces]

    def body(*refs):
        ins = refs[:n_w]
        outs = refs[n_w:2 * n_w]
        send_sems, recv_sems = refs[2 * n_w:]
        x, y, c = _place()
        copies = []
        for w in range(n_w):
            for j, (cx, cy) in enumerate(_other_chips(x, y)):
                copies.append(pltpu.make_async_remote_copy(
                    src_ref=ins[w].at[2 * cx + cy], dst_ref=outs[w].at[j],
                    send_sem=send_sems.at[w, j], recv_sem=recv_sems.at[w, j],
                    device_id=(cx, cy, c), device_id_type=MESH))
        for cp in copies:
            cp.start()
        for cp in copies:
            cp.wait()

    return pl.pallas_call(
        body, name="grad_scatter_pieces",
        in_specs=[ANY] * n_w, out_specs=[ANY] * n_w, out_shape=out_shape,
        scratch_shapes=[pltpu.SemaphoreType.DMA((n_w, 3)), pltpu.SemaphoreType.DMA((n_w, 3))],
        compiler_params=pltpu.CompilerParams(has_side_effects=True),
    )(*pieces)


def _sum_pieces(pieces, received, place_arr, name):
    _, r, n = pieces.shape
    tr = _row_tile(r, n * 4, limit=1024 * 1024)

    def body(p_ref, own_ref, r0_ref, r1_ref, r2_ref, o_ref):
        del p_ref
        acc = own_ref[...].astype(F32) + r0_ref[...].astype(F32)
        acc = acc + r1_ref[...].astype(F32)
        o_ref[...] = acc + r2_ref[...].astype(F32)

    def recv_spec(j):
        return pl.BlockSpec((None, tr, n), lambda i, p_ref: (j, i, 0))

    grid_spec = pltpu.PrefetchScalarGridSpec(
        num_scalar_prefetch=1, grid=(r // tr,),
        in_specs=[pl.BlockSpec((None, tr, n), lambda i, p_ref: (p_ref[0], i, 0)),
                  recv_spec(0), recv_spec(1), recv_spec(2)],
        out_specs=pl.BlockSpec((None, tr, n), lambda i, p_ref: (p_ref[1], i, 0)))
    return pl.pallas_call(
        body, name=name, grid_spec=grid_spec,
        out_shape=jax.ShapeDtypeStruct((2, r, n), F32),
        compiler_params=_params(("parallel",)),
    )(place_arr, pieces, received, received, received)


def _join_halves(shards):
    n_w = len(shards)

    def body(*refs):
        bufs = refs[n_w:2 * n_w]
        send_sems, recv_sems = refs[2 * n_w:]
        x, y, c = _place()

        def half(w, which):
            return pltpu.make_async_remote_copy(
                src_ref=bufs[w].at[which], dst_ref=bufs[w].at[which], send_sem=send_sems.at[w],
                recv_sem=recv_sems.at[w], device_id=(x, y, 1 - c), device_id_type=MESH)

        for w in range(n_w):
            half(w, c).start()
        for w in range(n_w):
            half(w, 1 - c).wait_recv()
        for w in range(n_w):
            half(w, c).wait_send()

    outs = pl.pallas_call(
        body, name="grad_join_halves",
        in_specs=[ANY] * n_w, out_specs=[ANY] * n_w,
        out_shape=[jax.ShapeDtypeStruct(h.shape, F32) for h in shards],
        input_output_aliases={w: w for w in range(n_w)},
        scratch_shapes=[pltpu.SemaphoreType.DMA((n_w,))] * 2,
        compiler_params=pltpu.CompilerParams(has_side_effects=True),
    )(*shards)
    return [o.reshape(2 * o.shape[1], o.shape[2]) for o in outs]


def _norm_weights_step(parts, w, m, v):
    rows, d = parts.shape

    def body(p_ref, w_ref, m_ref, v_ref, g_ref, d_ref, mo_ref, vo_ref, gathered, send_sems, recv_sems):
        x, y, c = _place()
        me = 4 * x + 2 * y + c
        gathered[me] = p_ref[...]
        copies = []
        for k in range(1, N_DEV):
            peer = (x ^ ((k >> 2) & 1), y ^ ((k >> 1) & 1), c ^ (k & 1))
            copies.append(pltpu.make_async_remote_copy(
                src_ref=p_ref, dst_ref=gathered.at[me], send_sem=send_sems.at[k - 1],
                recv_sem=recv_sems.at[k - 1], device_id=peer, device_id_type=MESH))
        for cp in copies:
            cp.start()
        for cp in copies:
            cp.wait()
        g = gathered[0]
        for k in range(1, N_DEV):
            g = g + gathered[k]
        delta, m_new, v_new = _adamw_math(w_ref[...], g, m_ref[...], v_ref[...])
        g_ref[...] = g
        d_ref[...] = delta
        mo_ref[...] = m_new
        vo_ref[...] = v_new

    vmem = pl.BlockSpec(memory_space=pltpu.VMEM)
    shp = jax.ShapeDtypeStruct((rows, d), F32)
    return pl.pallas_call(
        body, name="norm_weights_step",
        in_specs=[vmem] * 4, out_specs=[vmem] * 4, out_shape=[shp] * 4,
        scratch_shapes=[pltpu.VMEM((N_DEV, rows, d), F32), pltpu.SemaphoreType.DMA((N_DEV - 1,)),
                        pltpu.SemaphoreType.DMA((N_DEV - 1,))],
        compiler_params=pltpu.CompilerParams(has_side_effects=True),
    )(parts, w, m, v)


def kernel(x, norm_mix_w, w_in, w_out, norm_ffn_w, w_gate, w_up, w_down, norm_final_w, loss_target, m_norm_mix_w, m_w_in, m_w_out, m_norm_ffn_w, m_w_gate, m_w_up, m_w_down, m_norm_final_w, v_norm_mix_w, v_w_in, v_w_out, v_norm_ffn_w, v_w_gate, v_w_up, v_w_down, v_norm_final_w):
    s, d = x.shape[1], x.shape[2]
    xs = x.reshape(s, d)
    target = loss_target.reshape(s, d)
    big = {"w_in": (w_in, m_w_in, v_w_in), "w_out": (w_out, m_w_out, v_w_out),
           "w_gate": (w_gate, m_w_gate, v_w_gate), "w_up": (w_up, m_w_up, v_w_up),
           "w_down": (w_down, m_w_down, v_w_down)}
    big = {k: tuple(a.reshape(a.shape[1:]) for a in t) for k, t in big.items()}
    col_names, row_names = ("w_in", "w_gate", "w_up"), ("w_out", "w_down")
    n_in = N_CHIPS * big["w_in"][0].shape[1]
    ffn = N_CHIPS * big["w_gate"][0].shape[1]
    mix = ATTN_WIDTH + RET_WIDTH
    c_arr = lax.axis_index("c").astype(I32).reshape(1)
    shard_arr = (2 * lax.axis_index("x") + lax.axis_index("y")).astype(I32).reshape(1)
    place_arr = jnp.concatenate([shard_arr, c_arr])

    full_col, full_row = _all_gather_weights(
        [_cast_into_full(big[k][0], shard_arr, True, "cast_" + k) for k in col_names],
        [_cast_into_full(big[k][0], shard_arr, False, "cast_" + k) for k in row_names])
    wi, wg, wu = full_col
    wo, wd = full_row

    h1 = _rms_fwd(xs, norm_mix_w, "rms_mix_fwd")
    proj, = _matmul("in_proj", "nn", [h1], [wi], [0], s, n_in, d, s, 512, d, [], [F32], _epi_plain)
    attn_b, attn_o, lse = _attn_fwd(proj)
    ret_b, ret_raw = _ret_fwd(proj)
    mixed = jnp.concatenate([attn_b, ret_b], axis=1)
    x1, = _matmul("out_proj", "nn", [mixed], [wo], [0], s, d, mix, s, 512, mix, [xs], [F32], _epi_residual)
    h2 = _rms_fwd(x1, norm_ffn_w, "rms_ffn_fwd")
    gate, up, act = _matmul("gate_up", "nn", [h2, h2], [wg, wu], [0, 1], s, ffn, d, s, 256, d, [],
                            [F32, F32, BF16], _epi_swiglu)
    x2, = _matmul("down_proj", "nn", [act], [wd], [0], s, d, ffn, s, 512, ffn // 4, [x1], [F32],
                  _epi_residual)
    loss_row, dx2, dx2b, dwf = _final_norm_loss(x2, norm_final_w.reshape(1, d), target, "final_norm_loss")

    dgate, dup = _matmul("d_act", "nt", [dx2b], [wd], [0], s, ffn, d, s, 256, d, [gate, up],
                         [BF16, BF16], _epi_swiglu_bwd)
    g_wd, = _matmul("g_w_down", "tn", [act], [dx2b], [0], ffn, d, s, 512, d, s, [], [BF16], _epi_plain)
    dh2, = _matmul("d_h2", "nt", [dgate, dup], [wg, wu], [0, 0], s, d, ffn, s, 1024, 512, [], [F32],
                   _epi_plain)
    g_wg, g_wu = _matmul("g_w_gate_up", "tn", [h2, h2], [dgate, dup], [0, 1], d, ffn, s, 1024, 512, s, [],
                         [BF16, BF16], _epi_two)
    dx1, dx1b, dw_ffn = _rms_bwd(x1, norm_ffn_w, dh2, dx2, "rms_ffn_bwd")

    dmixed, = _matmul("d_mixed", "nt", [dx1b], [wo], [0], s, mix, d, s, 512, d, [], [F32], _epi_plain)
    g_wo, = _matmul("g_w_out", "tn", [mixed], [dx1b], [0], mix, d, s, 512, d, s, [], [BF16], _epi_plain)
    dqa, dka, dva = _attn_bwd(proj, attn_o, lse, dmixed)
    dqr, dkr, dvr, dgr = _ret_bwd(proj, ret_raw, dmixed)
    dproj = jnp.concatenate([dqa, dka, dva, dqr, dkr, dvr, dgr], axis=1)
    dh1, = _matmul("d_h1", "nt", [dproj], [wi], [0], s, d, n_in, s, 1024, n_in // 7, [], [F32], _epi_plain)
    g_wi, = _matmul("g_w_in", "tn", [h1], [dproj], [0], d, n_in, s, 1024, 1024, s, [], [BF16], _epi_plain)
    grad_x, _, dw_mix = _rms_bwd(xs, norm_mix_w, dh1, dx1, "rms_mix_bwd")

    g_col = {"w_in": g_wi, "w_gate": g_wg, "w_up": g_wu}
    g_row = {"w_out": g_wo, "w_down": g_wd}
    sib, col_views, row_views = _exchange_halves([g_col[k] for k in col_names], [g_row[k] for k in row_names])
    pieces = [_chip_sum_col(col_views[i], sib[i], c_arr, "chip_sum_" + k) for i, k in enumerate(col_names)]
    pieces += [_chip_sum_row(row_views[i], sib[len(col_names) + i], c_arr, "chip_sum_" + k)
               for i, k in enumerate(row_names)]
    received = _scatter_pieces(pieces)
    names = col_names + row_names
    halves = [_sum_pieces(pieces[i], received[i], place_arr, "sum_pieces_" + k) for i, k in enumerate(names)]
    grads = dict(zip(names, _join_halves(halves)))

    new = {k: _adamw(big[k][0], grads[k], big[k][1], big[k][2], "adamw_" + k) for k in names}

    def rows8(*vs):
        return jnp.concatenate([v.reshape(1, d) for v in vs] + [jnp.zeros((8 - len(vs), d), F32)], axis=0)

    ng, nd, nm, nv = _norm_weights_step(
        rows8(dw_mix, dw_ffn, dwf), rows8(norm_mix_w, norm_ffn_w, norm_final_w),
        rows8(m_norm_mix_w, m_norm_ffn_w, m_norm_final_w), rows8(v_norm_mix_w, v_norm_ffn_w, v_norm_final_w))

    loss = lax.psum(loss_row[0, 0], ("x", "y", "c"))

    def pack(small, per_weight):
        lead = lambda a: a.reshape((1,) + a.shape)
        return (small[0:1], lead(per_weight["w_in"]), lead(per_weight["w_out"]), small[1:2],
                lead(per_weight["w_gate"]), lead(per_weight["w_up"]), lead(per_weight["w_down"]), small[2])

    return (loss, grad_x.reshape(1, s, d),
            *pack(ng, grads),
            *pack(nd, {k: new[k][0] for k in names}),
            *pack(nm, {k: new[k][1] for k in names}),
            *pack(nv, {k: new[k][2] for k in names}))
```

```python
import functools
import math

import jax
import jax.numpy as jnp
from jax import lax
from jax.experimental import pallas as pl
from jax.experimental.pallas import tpu as pltpu

F32 = jnp.float32
BF16 = jnp.bfloat16
I32 = jnp.int32
MESH = pl.DeviceIdType.MESH
ANY = pl.BlockSpec(memory_space=pl.ANY)

ATTN_HEADS = 8
ATTN_HEAD_DIM = 128
RET_HEADS = 4
RET_HEAD_DIM = 256
ATTN_WIDTH = ATTN_HEADS * ATTN_HEAD_DIM
RET_WIDTH = RET_HEADS * RET_HEAD_DIM
DILATED_PATTERNS = ((128, 1), (512, 4), (2048, 16))
NORM_EPS = 1e-6
ADAM_LR = 0.001
ADAM_B1 = 0.9
ADAM_B2 = 0.999
ADAM_EPS = 1e-08
ADAM_WD = 0.01
ADAM_STEP = 10

N_CHIPS = 4
N_DEV = 8
NEG_BIG = -1e30
SEQ_TILE = 256
VMEM_LIMIT_BYTES = 56 * 1024 * 1024


def _params(semantics=None, vmem=VMEM_LIMIT_BYTES):
    return pltpu.CompilerParams(dimension_semantics=semantics, vmem_limit_bytes=vmem)


def _row_tile(rows, row_bytes, limit=2 * 1024 * 1024, mult=16):
    best = None
    for t in range(mult, rows + 1, mult):
        if rows % t == 0 and t * row_bytes <= limit:
            best = t
    assert best is not None, (rows, row_bytes)
    return best


def _sigmoid(x):
    return 1.0 / (1.0 + jnp.exp(-x))


def _select_by_index(idx, values):
    out = jnp.float32(values[-1])
    for i in range(len(values) - 2, -1, -1):
        out = jnp.where(idx == i, jnp.float32(values[i]), out)
    return out


def _place():
    x, y, c = lax.axis_index("x"), lax.axis_index("y"), lax.axis_index("c")
    return x, y, c


def _cast_into_full(w, shard_arr, column_sharded, name):
    rows, cols = w.shape
    tr = _row_tile(rows, cols * 4)
    steps = rows // tr
    if column_sharded:
        out_shape, out_map = (rows, N_CHIPS * cols), (lambda i, s_ref: (i, s_ref[0]))
    else:
        out_shape, out_map = (N_CHIPS * rows, cols), (lambda i, s_ref: (s_ref[0] * steps + i, 0))

    def body(s_ref, w_ref, o_ref):
        del s_ref
        o_ref[...] = w_ref[...].astype(BF16)

    grid_spec = pltpu.PrefetchScalarGridSpec(
        num_scalar_prefetch=1, grid=(steps,),
        in_specs=[pl.BlockSpec((tr, cols), lambda i, s_ref: (i, 0))],
        out_specs=pl.BlockSpec((tr, cols), out_map))
    return pl.pallas_call(
        body, name=name, grid_spec=grid_spec,
        out_shape=jax.ShapeDtypeStruct(out_shape, BF16),
        compiler_params=_params(("parallel",)),
    )(shard_arr, w)


def _rms_fwd(x, w, name):
    rows, d = x.shape
    tr = 256

    def body(x_ref, w_ref, h_ref):
        xv = x_ref[...]
        r = lax.rsqrt(jnp.mean(xv * xv, axis=-1, keepdims=True) + NORM_EPS)
        h_ref[...] = (xv * r * w_ref[...]).astype(BF16)

    return pl.pallas_call(
        body, name=name, grid=(rows // tr,),
        in_specs=[pl.BlockSpec((tr, d), lambda i: (i, 0)), pl.BlockSpec((1, d), lambda i: (0, 0))],
        out_specs=pl.BlockSpec((tr, d), lambda i: (i, 0)),
        out_shape=jax.ShapeDtypeStruct((rows, d), BF16),
        compiler_params=_params(("parallel",)),
    )(x, w)


def _rms_bwd(x, w, dh, dres, name):
    rows, d = x.shape
    tr = 256

    def body(x_ref, w_ref, dh_ref, dres_ref, dx_ref, dxb_ref, dw_ref):
        xv = x_ref[...]
        r = lax.rsqrt(jnp.mean(xv * xv, axis=-1, keepdims=True) + NORM_EPS)
        xhat = xv * r
        dy = dh_ref[...]
        dxhat = dy * w_ref[...]
        dx = dres_ref[...] + r * (dxhat - xhat * jnp.mean(dxhat * xhat, axis=-1, keepdims=True))
        dx_ref[...] = dx
        dxb_ref[...] = dx.astype(BF16)
        part = jnp.sum(dy * xhat, axis=0, keepdims=True)

        @pl.when(pl.program_id(0) == 0)
        def _():
            dw_ref[...] = part

        @pl.when(pl.program_id(0) != 0)
        def _():
            dw_ref[...] += part

    row = pl.BlockSpec((tr, d), lambda i: (i, 0))
    vec = pl.BlockSpec((1, d), lambda i: (0, 0))
    return pl.pallas_call(
        body, name=name, grid=(rows // tr,),
        in_specs=[row, vec, row, row],
        out_specs=[row, row, vec],
        out_shape=[jax.ShapeDtypeStruct((rows, d), F32), jax.ShapeDtypeStruct((rows, d), BF16),
                   jax.ShapeDtypeStruct((1, d), F32)],
        compiler_params=_params(("arbitrary",)),
    )(x, w, dh, dres)


def _final_norm_loss(x2, w, target, name):
    rows, d = x2.shape
    tr = 256

    def body(x_ref, w_ref, t_ref, loss_ref, dx_ref, dxb_ref, dw_ref):
        xv = x_ref[...]
        wv = w_ref[...]
        r = lax.rsqrt(jnp.mean(xv * xv, axis=-1, keepdims=True) + NORM_EPS)
        xhat = xv * r
        err = xhat * wv - t_ref[...]
        part_loss = 0.5 * jnp.sum(jnp.mean(err * err, axis=-1, keepdims=True), axis=0, keepdims=True)
        dy = err * (1.0 / d)
        dxhat = dy * wv
        dx = r * (dxhat - xhat * jnp.mean(dxhat * xhat, axis=-1, keepdims=True))
        dx_ref[...] = dx
        dxb_ref[...] = dx.astype(BF16)
        part_dw = jnp.sum(dy * xhat, axis=0, keepdims=True)
        part_loss = jnp.broadcast_to(part_loss, (1, 128))

        @pl.when(pl.program_id(0) == 0)
        def _():
            dw_ref[...] = part_dw
            loss_ref[...] = part_loss

        @pl.when(pl.program_id(0) != 0)
        def _():
            dw_ref[...] += part_dw
            loss_ref[...] += part_loss

    row = pl.BlockSpec((tr, d), lambda i: (i, 0))
    vec = pl.BlockSpec((1, d), lambda i: (0, 0))
    return pl.pallas_call(
        body, name=name, grid=(rows // tr,),
        in_specs=[row, vec, row],
        out_specs=[pl.BlockSpec((1, 128), lambda i: (0, 0)), row, row, vec],
        out_shape=[jax.ShapeDtypeStruct((1, 128), F32), jax.ShapeDtypeStruct((rows, d), F32),
                   jax.ShapeDtypeStruct((rows, d), BF16), jax.ShapeDtypeStruct((1, d), F32)],
        compiler_params=_params(("arbitrary",)),
    )(x2, w, target)


def _adamw_math(w, g, m, v):
    m = ADAM_B1 * m + (1.0 - ADAM_B1) * g
    v = ADAM_B2 * v + (1.0 - ADAM_B2) * (g * g)
    m_hat = m / (1.0 - ADAM_B1 ** ADAM_STEP)
    v_hat = v / (1.0 - ADAM_B2 ** ADAM_STEP)
    delta = -ADAM_LR * (m_hat / (jnp.sqrt(v_hat) + ADAM_EPS) + ADAM_WD * w)
    return delta, m, v


def _adamw(w, g, m, v, name):
    rows, cols = w.shape
    tr = _row_tile(rows, cols * 4, limit=1024 * 1024)

    def body(w_ref, g_ref, m_ref, v_ref, d_ref, mo_ref, vo_ref):
        delta, m_new, v_new = _adamw_math(w_ref[...], g_ref[...], m_ref[...], v_ref[...])
        d_ref[...] = delta
        mo_ref[...] = m_new
        vo_ref[...] = v_new

    blk = pl.BlockSpec((tr, cols), lambda i: (i, 0))
    shp = jax.ShapeDtypeStruct((rows, cols), F32)
    return pl.pallas_call(
        body, name=name, grid=(rows // tr,),
        in_specs=[blk] * 4, out_specs=[blk] * 3, out_shape=[shp] * 3,
        compiler_params=_params(("parallel",)),
    )(w, g, m, v)


_DOT_DIMS = {"nn": ((1,), (0,)), "nt": ((1,), (1,)), "tn": ((0,), (0,))}


def _matmul(name, mode, a_list, b_list, acc_of, m, n, k, tm, tn, tk, extras, out_dtypes, epilogue,
            a_koff=None, b_koff=None):
    assert m % tm == 0 and n % tn == 0 and k % tk == 0, (name, m, n, k, tm, tn, tk)
    nk = k // tk
    n_acc = max(acc_of) + 1
    n_pairs = len(a_list)
    a_koff = a_koff or [0] * n_pairs
    b_koff = b_koff or [0] * n_pairs
    dims = (_DOT_DIMS[mode], ((), ()))
    n_ext, n_out = len(extras), len(out_dtypes)

    def body(*refs):
        a_refs = refs[:n_pairs]
        b_refs = refs[n_pairs:2 * n_pairs]
        e_refs = refs[2 * n_pairs:2 * n_pairs + n_ext]
        o_refs = refs[2 * n_pairs + n_ext:2 * n_pairs + n_ext + n_out]
        acc_refs = refs[2 * n_pairs + n_ext + n_out:]
        parts = [None] * n_acc
        for p in range(n_pairs):
            d = lax.dot_general(a_refs[p][...], b_refs[p][...], dims, preferred_element_type=F32)
            parts[acc_of[p]] = d if parts[acc_of[p]] is None else parts[acc_of[p]] + d

        def finish(accs):
            outs = epilogue(accs, [e[...] for e in e_refs])
            for o_ref, o in zip(o_refs, outs):
                o_ref[...] = o.astype(o_ref.dtype)

        if nk == 1:
            finish(parts)
        else:
            kk = pl.program_id(2)

            @pl.when(kk == 0)
            def _():
                for acc_ref, part in zip(acc_refs, parts):
                    acc_ref[...] = part

            @pl.when(kk != 0)
            def _():
                for acc_ref, part in zip(acc_refs, parts):
                    acc_ref[...] += part

            @pl.when(kk == nk - 1)
            def _():
                finish([acc_ref[...] for acc_ref in acc_refs])

    def a_spec(off):
        if mode == "tn":
            return pl.BlockSpec((tk, tm), lambda i, j, kk: (kk + off, i))
        return pl.BlockSpec((tm, tk), lambda i, j, kk: (i, kk + off))

    def b_spec(off):
        if mode == "nt":
            return pl.BlockSpec((tn, tk), lambda i, j, kk: (j, kk + off))
        return pl.BlockSpec((tk, tn), lambda i, j, kk: (kk + off, j))

    tile = pl.BlockSpec((tm, tn), lambda i, j, kk: (i, j))
    scratch = [pltpu.VMEM((tm, tn), F32) for _ in range(n_acc)] if nk > 1 else []
    return pl.pallas_call(
        body, name=name, grid=(m // tm, n // tn, nk),
        in_specs=[a_spec(o) for o in a_koff] + [b_spec(o) for o in b_koff] + [tile] * n_ext,
        out_specs=[tile] * n_out,
        out_shape=[jax.ShapeDtypeStruct((m, n), dt) for dt in out_dtypes],
        scratch_shapes=scratch,
        compiler_params=_params(("parallel", "parallel", "arbitrary")),
    )(*a_list, *b_list, *extras)


def _epi_plain(accs, extras):
    return (accs[0],)


def _epi_residual(accs, extras):
    return (accs[0] + extras[0],)


def _epi_two(accs, extras):
    return accs[0], accs[1]


def _epi_swiglu(accs, extras):
    g, u = accs
    return g, u, g * _sigmoid(g) * u


def _epi_swiglu_bwd(accs, extras):
    da = accs[0]
    g, u = extras
    sg = _sigmoid(g)
    dg = da * u * sg * (1.0 + g * (1.0 - sg))
    du = da * g * sg
    return dg, du


_NT_DIMS = (((1,), (1,)), ((), ()))
_TN_DIMS = (((0,), (0,)), ((), ()))


def _tile_delta(tq, tk):
    return lax.broadcasted_iota(I32, (tq, tk), 0) - lax.broadcasted_iota(I32, (tq, tk), 1)


def _attn_mask_bias(delta, slope):
    count = jnp.zeros(delta.shape, I32)
    for window, dilation in DILATED_PATTERNS:
        hit = ((delta & (dilation - 1)) == 0) & (delta <= window)
        count = count + jnp.where(hit, 1, 0)
    valid = (delta >= 0) & (count > 0)
    logm = jnp.where(count == 3, math.log(3.0), jnp.where(count == 2, math.log(2.0), 0.0))
    return valid, logm - slope * delta.astype(F32)


def _alibi_slopes():
    return [2.0 ** (-8.0 * (h + 1) / ATTN_HEADS) for h in range(ATTN_HEADS)]


def _attn_fwd(proj):
    s = proj.shape[0]
    t = SEQ_TILE
    hd = ATTN_HEAD_DIM
    nh = ATTN_HEADS
    scale = 1.0 / math.sqrt(hd)
    slopes = _alibi_slopes()

    def body(q_ref, k_ref, v_ref, mix_ref, o_ref, lse_ref):
        h = pl.program_id(0)
        i = pl.program_id(1)
        slope = _select_by_index(h, slopes)
        q = q_ref[...].astype(BF16)
        base = _tile_delta(t, t)

        def step(j, carry):
            m_i, l_i, acc = carry
            rows = pl.ds(pl.multiple_of(j * t, t), t)
            kj = k_ref[rows, :].astype(BF16)
            vj = v_ref[rows, :].astype(BF16)
            sc = lax.dot_general(q, kj, _NT_DIMS, preferred_element_type=F32) * scale
            valid, bias = _attn_mask_bias(base + (i - j) * t, slope)
            sc = jnp.where(valid, sc + bias, NEG_BIG)
            m_new = jnp.maximum(m_i, jnp.max(sc, axis=-1, keepdims=True))
            p = jnp.exp(sc - m_new)
            alpha = jnp.exp(m_i - m_new)
            l_new = alpha * l_i + jnp.sum(p, axis=-1, keepdims=True)
            acc = alpha * acc + jnp.dot(p.astype(BF16), vj, preferred_element_type=F32)
            return m_new, l_new, acc

        init = (jnp.full((t, 1), NEG_BIG, F32), jnp.zeros((t, 1), F32), jnp.zeros((t, hd), F32))
        m_i, l_i, acc = lax.fori_loop(0, i + 1, step, init)
        out = acc / l_i
        o_ref[...] = out
        mix_ref[...] = out.astype(BF16)
        lse_ref[...] = jnp.broadcast_to(m_i + jnp.log(l_i), (t, hd))

    return pl.pallas_call(
        body, name="attn_fwd", grid=(nh, s // t),
        in_specs=[pl.BlockSpec((t, hd), lambda h, i: (i, h)),
                  pl.BlockSpec((s, hd), lambda h, i: (0, nh + h)),
                  pl.BlockSpec((s, hd), lambda h, i: (0, 2 * nh + h))],
        out_specs=[pl.BlockSpec((t, hd), lambda h, i: (i, h))] * 3,
        out_shape=[jax.ShapeDtypeStruct((s, ATTN_WIDTH), BF16),
                   jax.ShapeDtypeStruct((s, ATTN_WIDTH), F32),
                   jax.ShapeDtypeStruct((s, ATTN_WIDTH), F32)],
        compiler_params=_params(("parallel", "arbitrary")),
    )(proj, proj, proj)


def _attn_bwd(proj, attn_out, lse, dmixed):
    s = proj.shape[0]
    t = SEQ_TILE
    nt = s // t
    hd = ATTN_HEAD_DIM
    nh = ATTN_HEADS
    scale = 1.0 / math.sqrt(hd)
    slopes = _alibi_slopes()

    def body(q_ref, k_ref, v_ref, o_ref, lse_ref, do_ref, dq_ref, dk_ref, dv_ref,
             qb, kb, vb, dob, dsum, dq_acc):
        h = pl.program_id(0)
        slope = _select_by_index(h, slopes)
        qb[...] = q_ref[...].astype(BF16)
        kb[...] = k_ref[...].astype(BF16)
        vb[...] = v_ref[...].astype(BF16)
        do = do_ref[...]
        dob[...] = do.astype(BF16)
        dsum[...] = jnp.broadcast_to(jnp.sum(do * o_ref[...], axis=-1, keepdims=True), (s, hd))
        dq_acc[...] = jnp.zeros((s, hd), F32)
        base = _tile_delta(t, t)

        def over_keys(j, _):
            krows = pl.ds(pl.multiple_of(j * t, t), t)
            kj = kb[krows, :]
            vj = vb[krows, :]

            def over_queries(i, carry):
                dk, dv = carry
                qrows = pl.ds(pl.multiple_of(i * t, t), t)
                qi = qb[qrows, :]
                doi = dob[qrows, :]
                lse_i = lse_ref[qrows, :][:, :1]
                dsum_i = dsum[qrows, :][:, :1]
                sc = lax.dot_general(qi, kj, _NT_DIMS, preferred_element_type=F32) * scale
                valid, bias = _attn_mask_bias(base + (i - j) * t, slope)
                p = jnp.exp(jnp.where(valid, sc + bias, NEG_BIG) - lse_i)
                dp = lax.dot_general(doi, vj, _NT_DIMS, preferred_element_type=F32)
                ds = (p * (dp - dsum_i) * scale).astype(BF16)
                dv = dv + lax.dot_general(p.astype(BF16), doi, _TN_DIMS, preferred_element_type=F32)
                dk = dk + lax.dot_general(ds, qi, _TN_DIMS, preferred_element_type=F32)
                dq_acc[qrows, :] += jnp.dot(ds, kj, preferred_element_type=F32)
                return dk, dv

            zero = jnp.zeros((t, hd), F32)
            dk, dv = lax.fori_loop(j, nt, over_queries, (zero, zero))
            dk_ref[krows, :] = dk.astype(BF16)
            dv_ref[krows, :] = dv.astype(BF16)
            return 0

        lax.fori_loop(0, nt, over_keys, 0)
        dq_ref[...] = dq_acc[...].astype(BF16)

    def col(off):
        return pl.BlockSpec((s, hd), lambda h: (0, off + h))

    return pl.pallas_call(
        body, name="attn_bwd", grid=(nh,),
        in_specs=[col(0), col(nh), col(2 * nh), col(0), col(0), col(0)],
        out_specs=[col(0)] * 3,
        out_shape=[jax.ShapeDtypeStruct((s, ATTN_WIDTH), BF16)] * 3,
        scratch_shapes=[pltpu.VMEM((s, hd), BF16)] * 4 + [pltpu.VMEM((s, hd), F32)] * 2,
        compiler_params=_params(("arbitrary",)),
    )(proj, proj, proj, attn_out, lse, dmixed)


def _ret_log_gammas():
    return [math.log(1.0 - 2.0 ** (-5.0 - h)) for h in range(RET_HEADS)]


def _ret_decay(delta, log_gamma):
    dec = jnp.exp(delta.astype(F32) * log_gamma) * (1.0 / math.sqrt(RET_HEAD_DIM))
    return jnp.where(delta >= 0, dec, 0.0)


def _ret_fwd(proj):
    s = proj.shape[0]
    t = SEQ_TILE
    hd = RET_HEAD_DIM
    nh = RET_HEADS
    log_gammas = _ret_log_gammas()
    c0 = 3 * ATTN_WIDTH // hd

    def body(q_ref, k_ref, v_ref, g_ref, mix_ref, raw_ref):
        h = pl.program_id(0)
        i = pl.program_id(1)
        log_gamma = _select_by_index(h, log_gammas)
        q = q_ref[...].astype(BF16)
        base = _tile_delta(t, t)

        def step(j, acc):
            rows = pl.ds(pl.multiple_of(j * t, t), t)
            kj = k_ref[rows, :].astype(BF16)
            vj = v_ref[rows, :].astype(BF16)
            sc = lax.dot_general(q, kj, _NT_DIMS, preferred_element_type=F32)
            sc = sc * _ret_decay(base + (i - j) * t, log_gamma)
            return acc + jnp.dot(sc.astype(BF16), vj, preferred_element_type=F32)

        ret = lax.fori_loop(0, i + 1, step, jnp.zeros((t, hd), F32))
        raw_ref[...] = ret
        r = lax.rsqrt(jnp.mean(ret * ret, axis=-1, keepdims=True) + NORM_EPS)
        g = g_ref[...]
        mix_ref[...] = (g * _sigmoid(g) * (ret * r)).astype(BF16)

    return pl.pallas_call(
        body, name="ret_fwd", grid=(nh, s // t),
        in_specs=[pl.BlockSpec((t, hd), lambda h, i: (i, c0 + h)),
                  pl.BlockSpec((s, hd), lambda h, i: (0, c0 + nh + h)),
                  pl.BlockSpec((s, hd), lambda h, i: (0, c0 + 2 * nh + h)),
                  pl.BlockSpec((t, hd), lambda h, i: (i, c0 + 3 * nh + h))],
        out_specs=[pl.BlockSpec((t, hd), lambda h, i: (i, h))] * 2,
        out_shape=[jax.ShapeDtypeStruct((s, RET_WIDTH), BF16), jax.ShapeDtypeStruct((s, RET_WIDTH), F32)],
        compiler_params=_params(("parallel", "arbitrary")),
    )(proj, proj, proj, proj)


def _ret_bwd(proj, ret_raw, dmixed):
    s = proj.shape[0]
    t = SEQ_TILE
    nt = s // t
    hd = RET_HEAD_DIM
    nh = RET_HEADS
    log_gammas = _ret_log_gammas()
    c0 = 3 * ATTN_WIDTH // hd
    mixed_blocks = ATTN_WIDTH // hd

    def body(q_ref, k_ref, v_ref, g_ref, raw_ref, dmix_ref, dq_ref, dk_ref, dv_ref, dg_ref,
             qb, kb, vb, dretb, dq_acc):
        h = pl.program_id(0)
        log_gamma = _select_by_index(h, log_gammas)
        qb[...] = q_ref[...].astype(BF16)
        kb[...] = k_ref[...].astype(BF16)
        vb[...] = v_ref[...].astype(BF16)
        ret = raw_ref[...]
        r = lax.rsqrt(jnp.mean(ret * ret, axis=-1, keepdims=True) + NORM_EPS)
        normed = ret * r
        g = g_ref[...]
        sg = _sigmoid(g)
        dout = dmix_ref[...]
        dg_ref[...] = (dout * normed * sg * (1.0 + g * (1.0 - sg))).astype(BF16)
        dn = dout * g * sg
        dret = r * (dn - normed * jnp.mean(dn * normed, axis=-1, keepdims=True))
        dretb[...] = dret.astype(BF16)
        dq_acc[...] = jnp.zeros((s, hd), F32)
        base = _tile_delta(t, t)

        def over_keys(j, _):
            krows = pl.ds(pl.multiple_of(j * t, t), t)
            kj = kb[krows, :]
            vj = vb[krows, :]

            def over_queries(i, carry):
                dk, dv = carry
                qrows = pl.ds(pl.multiple_of(i * t, t), t)
                qi = qb[qrows, :]
                doi = dretb[qrows, :]
                dec = _ret_decay(base + (i - j) * t, log_gamma)
                a = (lax.dot_general(qi, kj, _NT_DIMS, preferred_element_type=F32) * dec).astype(BF16)
                da = (lax.dot_general(doi, vj, _NT_DIMS, preferred_element_type=F32) * dec).astype(BF16)
                dv = dv + lax.dot_general(a, doi, _TN_DIMS, preferred_element_type=F32)
                dk = dk + lax.dot_general(da, qi, _TN_DIMS, preferred_element_type=F32)
                dq_acc[qrows, :] += jnp.dot(da, kj, preferred_element_type=F32)
                return dk, dv

            zero = jnp.zeros((t, hd), F32)
            dk, dv = lax.fori_loop(j, nt, over_queries, (zero, zero))
            dk_ref[krows, :] = dk.astype(BF16)
            dv_ref[krows, :] = dv.astype(BF16)
            return 0

        lax.fori_loop(0, nt, over_keys, 0)
        dq_ref[...] = dq_acc[...].astype(BF16)

    def col(off):
        return pl.BlockSpec((s, hd), lambda h: (0, off + h))

    return pl.pallas_call(
        body, name="ret_bwd", grid=(nh,),
        in_specs=[col(c0), col(c0 + nh), col(c0 + 2 * nh), col(c0 + 3 * nh), col(0), col(mixed_blocks)],
        out_specs=[col(0)] * 4,
        out_shape=[jax.ShapeDtypeStruct((s, RET_WIDTH), BF16)] * 4,
        scratch_shapes=[pltpu.VMEM((s, hd), BF16)] * 4 + [pltpu.VMEM((s, hd), F32)],
        compiler_params=_params(("arbitrary",)),
    )(proj, proj, proj, proj, ret_raw, dmixed)


_FLIPS = (2, 1, 3)


def _other_chips(x, y):
    return [(1 - x, y), (x, 1 - y), (1 - x, 1 - y)]


_HBM = pl.BlockSpec(memory_space=pltpu.HBM)
_SEM = pl.BlockSpec(memory_space=pltpu.SEMAPHORE)
_EFFECT = pltpu.SideEffectType.DATAFLOW_SIDE_EFFECTING


def _in_hbm(a):
    return pltpu.with_memory_space_constraint(a, pltpu.HBM)


def _weight_view(w, column_sharded):
    if column_sharded:
        return w.reshape(2, w.shape[0] // 2, w.shape[1])
    return w.reshape(N_CHIPS, 2, w.shape[0] // (2 * N_CHIPS), w.shape[1])


def _weight_unview(v):
    if v.ndim == 3:
        return v.reshape(2 * v.shape[1], v.shape[2])
    return v.reshape(N_CHIPS * 2 * v.shape[2], v.shape[3])


def _weight_region(buf, shard, half):
    if len(buf.shape) == 3:
        cols = buf.shape[2] // N_CHIPS
        return buf.at[half, :, pl.ds(shard * cols, cols)]
    return buf.at[shard, half]


def _remote(where, send_sem, recv_sem, to):
    return pltpu.make_async_remote_copy(src_ref=where, dst_ref=where, send_sem=send_sem, recv_sem=recv_sem,
                                        device_id=to, device_id_type=MESH)


def _for_my_shard(fn):
    x, y, _ = _place()
    for ss in range(N_CHIPS):
        pl.when(2 * x + y == ss)(functools.partial(fn, ss))


def _gather_start(views):
    n_w = len(views)

    def body(*refs):
        send_sems, recv_sems = refs[n_w:n_w + 2]
        bufs = refs[n_w + 2:]
        x, y, c = _place()

        def start(ss):
            for w in range(n_w):
                for j, chip in enumerate(_other_chips(x, y)):
                    _remote(_weight_region(bufs[w], ss, c), send_sems.at[3 * w + j], recv_sems.at[3 * w + j],
                            (*chip, c)).start()

        _for_my_shard(start)

    return pl.pallas_call(
        body, name="gather_start",
        in_specs=[_HBM] * n_w, out_specs=[_SEM, _SEM] + [_HBM] * n_w,
        out_shape=[pltpu.SemaphoreType.DMA((3 * n_w,)), pltpu.SemaphoreType.DMA((3 * n_w,))]
        + [pltpu.HBM(v.shape, BF16) for v in views],
        input_output_aliases={w: 2 + w for w in range(n_w)},
        compiler_params=pltpu.CompilerParams(has_side_effects=_EFFECT),
    )(*[_in_hbm(v) for v in views])


def _gather_forward(views, which, send_sems, recv_sems, after, name):
    n_w = len(views)

    def body(*refs):
        send_in, recv_in = refs[n_w:n_w + 2]
        fwd_send, fwd_recv = refs[n_w + 3:n_w + 5]
        bufs = refs[n_w + 5:]
        x, y, c = _place()
        sibling = (x, y, 1 - c)

        def forward(ss):
            for i, w in enumerate(which):
                for j in range(3):
                    landed = _weight_region(bufs[i], ss ^ _FLIPS[j], c)
                    _remote(landed, send_in.at[3 * w + j], recv_in.at[3 * w + j], sibling).wait_recv()
                    _remote(landed, fwd_send.at[3 * i + j], fwd_recv.at[3 * i + j], sibling).start()

        _for_my_shard(forward)
        for i, w in enumerate(which):
            for j in range(3):
                _remote(_weight_region(bufs[i], 0, 0), send_in.at[3 * w + j], recv_in.at[3 * w + j],
                        sibling).wait_send()

    return pl.pallas_call(
        body, name=name,
        in_specs=[_HBM] * n_w + [_SEM, _SEM, ANY], out_specs=[_SEM, _SEM] + [_HBM] * n_w,
        out_shape=[pltpu.SemaphoreType.DMA((3 * n_w,)), pltpu.SemaphoreType.DMA((3 * n_w,))]
        + [pltpu.HBM(v.shape, BF16) for v in views],
        input_output_aliases={w: 2 + w for w in range(n_w)},
        compiler_params=pltpu.CompilerParams(has_side_effects=_EFFECT),
    )(*views, send_sems, recv_sems, after)


def _gather_end(views, fwd_send, fwd_recv, after, name):
    n_w = len(views)

    def body(*refs):
        fwd_send_ref, fwd_recv_ref = refs[n_w:n_w + 2]
        bufs = refs[n_w + 3:]
        x, y, c = _place()
        for i in range(n_w):
            for j in range(3):
                cp = _remote(_weight_region(bufs[i], 0, 0), fwd_send_ref.at[3 * i + j], fwd_recv_ref.at[3 * i + j],
                             (x, y, 1 - c))
                cp.wait_recv()
                cp.wait_send()

    outs = pl.pallas_call(
        body, name=name,
        in_specs=[_HBM] * n_w + [_SEM, _SEM, ANY], out_specs=[_HBM] * n_w,
        out_shape=[pltpu.HBM(v.shape, BF16) for v in views],
        input_output_aliases={w: w for w in range(n_w)},
        compiler_params=pltpu.CompilerParams(has_side_effects=_EFFECT),
    )(*views, fwd_send, fwd_recv, after)
    return [_weight_unview(o) for o in outs]


def _exchange_halves(grads_col, grads_row):
    n_col, n_row = len(grads_col), len(grads_row)
    n_w = n_col + n_row
    col_in = [g.reshape(2, g.shape[0] // 2, g.shape[1]) for g in grads_col]
    row_in = [g.reshape(N_CHIPS, 2, g.shape[0] // (2 * N_CHIPS), g.shape[1]) for g in grads_row]
    out_shape = ([jax.ShapeDtypeStruct(g.shape[1:], BF16) for g in col_in]
                 + [jax.ShapeDtypeStruct((N_CHIPS,) + g.shape[2:], BF16) for g in row_in])

    def body(*refs):
        ins = refs[:n_w]
        outs = refs[n_w:2 * n_w]
        send_sems, recv_sems = refs[2 * n_w:]
        x, y, c = _place()
        copies = []
        for w in range(n_w):
            src = ins[w].at[1 - c] if w < n_col else ins[w].at[:, 1 - c]
            copies.append(pltpu.make_async_remote_copy(
                src_ref=src, dst_ref=outs[w], send_sem=send_sems.at[w], recv_sem=recv_sems.at[w],
                device_id=(x, y, 1 - c), device_id_type=MESH))
        for cp in copies:
            cp.start()
        for cp in copies:
            cp.wait()

    return pl.pallas_call(
        body, name="grad_exchange_halves",
        in_specs=[ANY] * n_w, out_specs=[ANY] * n_w, out_shape=out_shape,
        scratch_shapes=[pltpu.SemaphoreType.DMA((n_w,)), pltpu.SemaphoreType.DMA((n_w,))],
        compiler_params=pltpu.CompilerParams(has_side_effects=True),
    )(*col_in, *row_in), col_in, row_in


def _chip_sum_col(g3, sib, c_arr, name):
    _, hk, n = g3.shape
    cols = n // N_CHIPS
    tr = _row_tile(hk, cols * 2, limit=1024 * 1024)

    def body(c_ref, g_ref, s_ref, o_ref):
        del c_ref
        o_ref[...] = (g_ref[...].astype(F32) + s_ref[...].astype(F32)).astype(BF16)

    grid_spec = pltpu.PrefetchScalarGridSpec(
        num_scalar_prefetch=1, grid=(N_CHIPS, hk // tr),
        in_specs=[pl.BlockSpec((None, tr, cols), lambda p, r, c_ref: (c_ref[0], r, p)),
                  pl.BlockSpec((tr, cols), lambda p, r, c_ref: (r, p))],
        out_specs=pl.BlockSpec((None, tr, cols), lambda p, r, c_ref: (p, r, 0)))
    return pl.pallas_call(
        body, name=name, grid_spec=grid_spec,
        out_shape=jax.ShapeDtypeStruct((N_CHIPS, hk, cols), BF16),
        compiler_params=_params(("parallel", "parallel")),
    )(c_arr, g3, sib)


def _chip_sum_row(g4, sib, c_arr, name):
    _, _, hr, n = g4.shape
    tr = _row_tile(hr, n * 2, limit=1024 * 1024)

    def body(c_ref, g_ref, s_ref, o_ref):
        del c_ref
        o_ref[...] = (g_ref[...].astype(F32) + s_ref[...].astype(F32)).astype(BF16)

    grid_spec = pltpu.PrefetchScalarGridSpec(
        num_scalar_prefetch=1, grid=(N_CHIPS, hr // tr),
        in_specs=[pl.BlockSpec((None, None, tr, n), lambda p, r, c_ref: (p, c_ref[0], r, 0)),
                  pl.BlockSpec((None, tr, n), lambda p, r, c_ref: (p, r, 0))],
        out_specs=pl.BlockSpec((None, tr, n), lambda p, r, c_ref: (p, r, 0)))
    return pl.pallas_call(
        body, name=name, grid_spec=grid_spec,
        out_shape=jax.ShapeDtypeStruct((N_CHIPS, hr, n), BF16),
        compiler_params=_params(("parallel", "parallel")),
    )(c_arr, g4, sib)


def _scatter_pieces(pieces):
    n_w = len(pieces)
    out_shape = [jax.ShapeDtypeStruct((3,) + p.shape[1:], BF16) for p in pieces]

    def body(*refs):
        ins = refs[:n_w]
        outs = refs[n_w:2 * n_w]
        send_sems, recv_sems = refs[2 * n_w:]
        x, y, c = _place()
        copies = []
        for w in range(n_w):
            for j, (cx, cy) in enumerate(_other_chips(x, y)):
                copies.append(pltpu.make_async_remote_copy(
                    src_ref=ins[w].at[2 * cx + cy], dst_ref=outs[w].at[j],
                    send_sem=send_sems.at[w, j], recv_sem=recv_sems.at[w, j],
                    device_id=(cx, cy, c), device_id_type=MESH))
        for cp in copies:
            cp.start()
        for cp in copies:
            cp.wait()

    return pl.pallas_call(
        body, name="grad_scatter_pieces",
        in_specs=[ANY] * n_w, out_specs=[ANY] * n_w, out_shape=out_shape,
        scratch_shapes=[pltpu.SemaphoreType.DMA((n_w, 3)), pltpu.SemaphoreType.DMA((n_w, 3))],
        compiler_params=pltpu.CompilerParams(has_side_effects=True),
    )(*pieces)


def _sum_pieces(pieces, received, place_arr, name):
    _, r, n = pieces.shape
    tr = _row_tile(r, n * 4, limit=1024 * 1024)

    def body(p_ref, own_ref, r0_ref, r1_ref, r2_ref, o_ref):
        del p_ref
        acc = own_ref[...].astype(F32) + r0_ref[...].astype(F32)
        acc = acc + r1_ref[...].astype(F32)
        o_ref[...] = acc + r2_ref[...].astype(F32)

    def recv_spec(j):
        return pl.BlockSpec((None, tr, n), lambda i, p_ref: (j, i, 0))

    grid_spec = pltpu.PrefetchScalarGridSpec(
        num_scalar_prefetch=1, grid=(r // tr,),
        in_specs=[pl.BlockSpec((None, tr, n), lambda i, p_ref: (p_ref[0], i, 0)),
                  recv_spec(0), recv_spec(1), recv_spec(2)],
        out_specs=pl.BlockSpec((None, tr, n), lambda i, p_ref: (p_ref[1], i, 0)))
    return pl.pallas_call(
        body, name=name, grid_spec=grid_spec,
        out_shape=jax.ShapeDtypeStruct((2, r, n), F32),
        compiler_params=_params(("parallel",)),
    )(place_arr, pieces, received, received, received)


def _join_halves(shards):
    n_w = len(shards)

    def body(*refs):
        bufs = refs[n_w:2 * n_w]
        send_sems, recv_sems = refs[2 * n_w:]
        x, y, c = _place()

        def half(w, which):
            return pltpu.make_async_remote_copy(
                src_ref=bufs[w].at[which], dst_ref=bufs[w].at[which], send_sem=send_sems.at[w],
                recv_sem=recv_sems.at[w], device_id=(x, y, 1 - c), device_id_type=MESH)

        for w in range(n_w):
            half(w, c).start()
        for w in range(n_w):
            half(w, 1 - c).wait_recv()
        for w in range(n_w):
            half(w, c).wait_send()

    outs = pl.pallas_call(
        body, name="grad_join_halves",
        in_specs=[ANY] * n_w, out_specs=[ANY] * n_w,
        out_shape=[jax.ShapeDtypeStruct(h.shape, F32) for h in shards],
        input_output_aliases={w: w for w in range(n_w)},
        scratch_shapes=[pltpu.SemaphoreType.DMA((n_w,))] * 2,
        compiler_params=pltpu.CompilerParams(has_side_effects=True),
    )(*shards)
    return [o.reshape(2 * o.shape[1], o.shape[2]) for o in outs]


def _norm_weights_step(parts, w, m, v):
    rows, d = parts.shape

    def body(p_ref, w_ref, m_ref, v_ref, g_ref, d_ref, mo_ref, vo_ref, gathered, send_sems, recv_sems):
        x, y, c = _place()
        me = 4 * x + 2 * y + c
        gathered[me] = p_ref[...]
        copies = []
        for k in range(1, N_DEV):
            peer = (x ^ ((k >> 2) & 1), y ^ ((k >> 1) & 1), c ^ (k & 1))
            copies.append(pltpu.make_async_remote_copy(
                src_ref=p_ref, dst_ref=gathered.at[me], send_sem=send_sems.at[k - 1],
                recv_sem=recv_sems.at[k - 1], device_id=peer, device_id_type=MESH))
        for cp in copies:
            cp.start()
        for cp in copies:
            cp.wait()
        g = gathered[0]
        for k in range(1, N_DEV):
            g = g + gathered[k]
        delta, m_new, v_new = _adamw_math(w_ref[...], g, m_ref[...], v_ref[...])
        g_ref[...] = g
        d_ref[...] = delta
        mo_ref[...] = m_new
        vo_ref[...] = v_new

    vmem = pl.BlockSpec(memory_space=pltpu.VMEM)
    shp = jax.ShapeDtypeStruct((rows, d), F32)
    return pl.pallas_call(
        body, name="norm_weights_step",
        in_specs=[vmem] * 4, out_specs=[vmem] * 4, out_shape=[shp] * 4,
        scratch_shapes=[pltpu.VMEM((N_DEV, rows, d), F32), pltpu.SemaphoreType.DMA((N_DEV - 1,)),
                        pltpu.SemaphoreType.DMA((N_DEV - 1,))],
        compiler_params=pltpu.CompilerParams(has_side_effects=True),
    )(parts, w, m, v)


def kernel(x, norm_mix_w, w_in, w_out, norm_ffn_w, w_gate, w_up, w_down, norm_final_w, loss_target, m_norm_mix_w, m_w_in, m_w_out, m_norm_ffn_w, m_w_gate, m_w_up, m_w_down, m_norm_final_w, v_norm_mix_w, v_w_in, v_w_out, v_norm_ffn_w, v_w_gate, v_w_up, v_w_down, v_norm_final_w):
    s, d = x.shape[1], x.shape[2]
    xs = x.reshape(s, d)
    target = loss_target.reshape(s, d)
    big = {"w_in": (w_in, m_w_in, v_w_in), "w_out": (w_out, m_w_out, v_w_out),
           "w_gate": (w_gate, m_w_gate, v_w_gate), "w_up": (w_up, m_w_up, v_w_up),
           "w_down": (w_down, m_w_down, v_w_down)}
    big = {k: tuple(a.reshape(a.shape[1:]) for a in t) for k, t in big.items()}
    col_names, row_names = ("w_in", "w_gate", "w_up"), ("w_out", "w_down")
    n_in = N_CHIPS * big["w_in"][0].shape[1]
    ffn = N_CHIPS * big["w_gate"][0].shape[1]
    mix = ATTN_WIDTH + RET_WIDTH
    c_arr = lax.axis_index("c").astype(I32).reshape(1)
    shard_arr = (2 * lax.axis_index("x") + lax.axis_index("y")).astype(I32).reshape(1)
    place_arr = jnp.concatenate([shard_arr, c_arr])

    order = ("w_in", "w_out", "w_gate", "w_up", "w_down")
    views = [_weight_view(_cast_into_full(big[k][0], shard_arr, k in col_names, "cast_" + k), k in col_names)
             for k in order]
    send_sems, recv_sems, v_in, v_out, v_gate, v_up, v_down = _gather_start(views)

    h1 = _rms_fwd(xs, norm_mix_w, "rms_mix_fwd")
    fs, fr, v_in = _gather_forward([v_in], [0], send_sems, recv_sems, h1, "gather_forward_in")
    wi, = _gather_end([v_in], fs, fr, h1, "gather_end_in")
    proj, = _matmul("in_proj", "nn", [h1], [wi], [0], s, n_in, d, s, 512, d, [], [F32], _epi_plain)
    fs, fr, v_out = _gather_forward([v_out], [1], send_sems, recv_sems, proj, "gather_forward_out")
    attn_b, attn_o, lse = _attn_fwd(proj)
    wo, = _gather_end([v_out], fs, fr, attn_o, "gather_end_out")
    fs, fr, v_gate, v_up = _gather_forward([v_gate, v_up], [2, 3], send_sems, recv_sems, attn_o,
                                           "gather_forward_gate_up")
    ret_b, ret_raw = _ret_fwd(proj)
    mixed = jnp.concatenate([attn_b, ret_b], axis=1)
    x1, = _matmul("out_proj", "nn", [mixed], [wo], [0], s, d, mix, s, 512, mix, [xs], [F32], _epi_residual)
    h2 = _rms_fwd(x1, norm_ffn_w, "rms_ffn_fwd")
    wg, wu = _gather_end([v_gate, v_up], fs, fr, h2, "gather_end_gate_up")
    gate, up, act = _matmul("gate_up", "nn", [h2, h2], [wg, wu], [0, 1], s, ffn, d, s, 256, d, [],
                            [F32, F32, BF16], _epi_swiglu)
    fs, fr, v_down = _gather_forward([v_down], [4], send_sems, recv_sems, act, "gather_forward_down")
    wd, = _gather_end([v_down], fs, fr, act, "gather_end_down")
    x2, = _matmul("down_proj", "nn", [act], [wd], [0], s, d, ffn, s, 512, ffn // 4, [x1], [F32],
                  _epi_residual)
    loss_row, dx2, dx2b, dwf = _final_norm_loss(x2, norm_final_w.reshape(1, d), target, "final_norm_loss")

    dgate, dup = _matmul("d_act", "nt", [dx2b], [wd], [0], s, ffn, d, s, 256, d, [gate, up],
                         [BF16, BF16], _epi_swiglu_bwd)
    g_wd, = _matmul("g_w_down", "tn", [act], [dx2b], [0], ffn, d, s, 512, d, s, [], [BF16], _epi_plain)
    dh2, = _matmul("d_h2", "nt", [dgate, dup], [wg, wu], [0, 0], s, d, ffn, s, 1024, 512, [], [F32],
                   _epi_plain)
    g_wg, g_wu = _matmul("g_w_gate_up", "tn", [h2, h2], [dgate, dup], [0, 1], d, ffn, s, 1024, 512, s, [],
                         [BF16, BF16], _epi_two)
    dx1, dx1b, dw_ffn = _rms_bwd(x1, norm_ffn_w, dh2, dx2, "rms_ffn_bwd")

    dmixed, = _matmul("d_mixed", "nt", [dx1b], [wo], [0], s, mix, d, s, 512, d, [], [F32], _epi_plain)
    g_wo, = _matmul("g_w_out", "tn", [mixed], [dx1b], [0], mix, d, s, 512, d, s, [], [BF16], _epi_plain)
    dqa, dka, dva = _attn_bwd(proj, attn_o, lse, dmixed)
    dqr, dkr, dvr, dgr = _ret_bwd(proj, ret_raw, dmixed)
    dproj = jnp.concatenate([dqa, dka, dva, dqr, dkr, dvr, dgr], axis=1)
    dh1, = _matmul("d_h1", "nt", [dproj], [wi], [0], s, d, n_in, s, 1024, n_in // 7, [], [F32], _epi_plain)
    g_wi, = _matmul("g_w_in", "tn", [h1], [dproj], [0], d, n_in, s, 1024, 1024, s, [], [BF16], _epi_plain)
    grad_x, _, dw_mix = _rms_bwd(xs, norm_mix_w, dh1, dx1, "rms_mix_bwd")

    g_col = {"w_in": g_wi, "w_gate": g_wg, "w_up": g_wu}
    g_row = {"w_out": g_wo, "w_down": g_wd}
    sib, col_views, row_views = _exchange_halves([g_col[k] for k in col_names], [g_row[k] for k in row_names])
    pieces = [_chip_sum_col(col_views[i], sib[i], c_arr, "chip_sum_" + k) for i, k in enumerate(col_names)]
    pieces += [_chip_sum_row(row_views[i], sib[len(col_names) + i], c_arr, "chip_sum_" + k)
               for i, k in enumerate(row_names)]
    received = _scatter_pieces(pieces)
    names = col_names + row_names
    halves = [_sum_pieces(pieces[i], received[i], place_arr, "sum_pieces_" + k) for i, k in enumerate(names)]
    grads = dict(zip(names, _join_halves(halves)))

    new = {k: _adamw(big[k][0], grads[k], big[k][1], big[k][2], "adamw_" + k) for k in names}

    def rows8(*vs):
        return jnp.concatenate([v.reshape(1, d) for v in vs] + [jnp.zeros((8 - len(vs), d), F32)], axis=0)

    ng, nd, nm, nv = _norm_weights_step(
        rows8(dw_mix, dw_ffn, dwf), rows8(norm_mix_w, norm_ffn_w, norm_final_w),
        rows8(m_norm_mix_w, m_norm_ffn_w, m_norm_final_w), rows8(v_norm_mix_w, v_norm_ffn_w, v_norm_final_w))

    loss = lax.psum(loss_row[0, 0], ("x", "y", "c"))

    def pack(small, per_weight):
        lead = lambda a: a.reshape((1,) + a.shape)
        return (small[0:1], lead(per_weight["w_in"]), lead(per_weight["w_out"]), small[1:2],
                lead(per_weight["w_gate"]), lead(per_weight["w_up"]), lead(per_weight["w_down"]), small[2])

    return (loss, grad_x.reshape(1, s, d),
            *pack(ng, grads),
            *pack(nd, {k: new[k][0] for k in names}),
            *pack(nm, {k: new[k][1] for k in names}),
            *pack(nv, {k: new[k][2] for k in names}))
```

```python
import functools
import math

import jax
import jax.numpy as jnp
from jax import lax
from jax.experimental import pallas as pl
from jax.experimental.pallas import tpu as pltpu

F32 = jnp.float32
BF16 = jnp.bfloat16
I32 = jnp.int32
MESH = pl.DeviceIdType.MESH
ANY = pl.BlockSpec(memory_space=pl.ANY)

ATTN_HEADS = 8
ATTN_HEAD_DIM = 128
RET_HEADS = 4
RET_HEAD_DIM = 256
ATTN_WIDTH = ATTN_HEADS * ATTN_HEAD_DIM
RET_WIDTH = RET_HEADS * RET_HEAD_DIM
DILATED_PATTERNS = ((128, 1), (512, 4), (2048, 16))
NORM_EPS = 1e-6
ADAM_LR = 0.001
ADAM_B1 = 0.9
ADAM_B2 = 0.999
ADAM_EPS = 1e-08
ADAM_WD = 0.01
ADAM_STEP = 10

N_CHIPS = 4
N_DEV = 8
NEG_BIG = -1e30
SEQ_TILE = 256
VMEM_LIMIT_BYTES = 56 * 1024 * 1024


def _params(semantics=None, vmem=VMEM_LIMIT_BYTES):
    return pltpu.CompilerParams(dimension_semantics=semantics, vmem_limit_bytes=vmem)


def _row_tile(rows, row_bytes, limit=2 * 1024 * 1024, mult=16):
    best = None
    for t in range(mult, rows + 1, mult):
        if rows % t == 0 and t * row_bytes <= limit:
            best = t
    assert best is not None, (rows, row_bytes)
    return best


def _sigmoid(x):
    return 1.0 / (1.0 + jnp.exp(-x))


def _select_by_index(idx, values):
    out = jnp.float32(values[-1])
    for i in range(len(values) - 2, -1, -1):
        out = jnp.where(idx == i, jnp.float32(values[i]), out)
    return out


def _place():
    x, y, c = lax.axis_index("x"), lax.axis_index("y"), lax.axis_index("c")
    return x, y, c


def _cast_into_full(w, shard_arr, column_sharded, name):
    rows, cols = w.shape
    tr = _row_tile(rows, cols * 4)
    steps = rows // tr
    if column_sharded:
        out_shape, out_map = (rows, N_CHIPS * cols), (lambda i, s_ref: (i, s_ref[0]))
    else:
        out_shape, out_map = (N_CHIPS * rows, cols), (lambda i, s_ref: (s_ref[0] * steps + i, 0))

    def body(s_ref, w_ref, o_ref):
        del s_ref
        o_ref[...] = w_ref[...].astype(BF16)

    grid_spec = pltpu.PrefetchScalarGridSpec(
        num_scalar_prefetch=1, grid=(steps,),
        in_specs=[pl.BlockSpec((tr, cols), lambda i, s_ref: (i, 0))],
        out_specs=pl.BlockSpec((tr, cols), out_map))
    return pl.pallas_call(
        body, name=name, grid_spec=grid_spec,
        out_shape=jax.ShapeDtypeStruct(out_shape, BF16),
        compiler_params=_params(("parallel",)),
    )(shard_arr, w)


def _rms_fwd(x, w, name):
    rows, d = x.shape
    tr = 256

    def body(x_ref, w_ref, h_ref):
        xv = x_ref[...]
        r = lax.rsqrt(jnp.mean(xv * xv, axis=-1, keepdims=True) + NORM_EPS)
        h_ref[...] = (xv * r * w_ref[...]).astype(BF16)

    return pl.pallas_call(
        body, name=name, grid=(rows // tr,),
        in_specs=[pl.BlockSpec((tr, d), lambda i: (i, 0)), pl.BlockSpec((1, d), lambda i: (0, 0))],
        out_specs=pl.BlockSpec((tr, d), lambda i: (i, 0)),
        out_shape=jax.ShapeDtypeStruct((rows, d), BF16),
        compiler_params=_params(("parallel",)),
    )(x, w)


def _rms_bwd(x, w, dh, dres, name, after=()):
    rows, d = x.shape
    tr = 256
    after = tuple(after)

    def body(x_ref, w_ref, dh_ref, dres_ref, *rest):
        dx_ref, dxb_ref, dw_ref = rest[len(after):]
        xv = x_ref[...]
        r = lax.rsqrt(jnp.mean(xv * xv, axis=-1, keepdims=True) + NORM_EPS)
        xhat = xv * r
        dy = dh_ref[...]
        dxhat = dy * w_ref[...]
        dx = dres_ref[...] + r * (dxhat - xhat * jnp.mean(dxhat * xhat, axis=-1, keepdims=True))
        dx_ref[...] = dx
        dxb_ref[...] = dx.astype(BF16)
        part = jnp.sum(dy * xhat, axis=0, keepdims=True)

        @pl.when(pl.program_id(0) == 0)
        def _():
            dw_ref[...] = part

        @pl.when(pl.program_id(0) != 0)
        def _():
            dw_ref[...] += part

    row = pl.BlockSpec((tr, d), lambda i: (i, 0))
    vec = pl.BlockSpec((1, d), lambda i: (0, 0))
    return pl.pallas_call(
        body, name=name, grid=(rows // tr,),
        in_specs=[row, vec, row, row] + [ANY] * len(after),
        out_specs=[row, row, vec],
        out_shape=[jax.ShapeDtypeStruct((rows, d), F32), jax.ShapeDtypeStruct((rows, d), BF16),
                   jax.ShapeDtypeStruct((1, d), F32)],
        compiler_params=_params(("arbitrary",)),
    )(x, w, dh, dres, *after)


def _final_norm_loss(x2, w, target, name):
    rows, d = x2.shape
    tr = 256

    def body(x_ref, w_ref, t_ref, loss_ref, dx_ref, dxb_ref, dw_ref):
        xv = x_ref[...]
        wv = w_ref[...]
        r = lax.rsqrt(jnp.mean(xv * xv, axis=-1, keepdims=True) + NORM_EPS)
        xhat = xv * r
        err = xhat * wv - t_ref[...]
        part_loss = 0.5 * jnp.sum(jnp.mean(err * err, axis=-1, keepdims=True), axis=0, keepdims=True)
        dy = err * (1.0 / d)
        dxhat = dy * wv
        dx = r * (dxhat - xhat * jnp.mean(dxhat * xhat, axis=-1, keepdims=True))
        dx_ref[...] = dx
        dxb_ref[...] = dx.astype(BF16)
        part_dw = jnp.sum(dy * xhat, axis=0, keepdims=True)
        part_loss = jnp.broadcast_to(part_loss, (1, 128))

        @pl.when(pl.program_id(0) == 0)
        def _():
            dw_ref[...] = part_dw
            loss_ref[...] = part_loss

        @pl.when(pl.program_id(0) != 0)
        def _():
            dw_ref[...] += part_dw
            loss_ref[...] += part_loss

    row = pl.BlockSpec((tr, d), lambda i: (i, 0))
    vec = pl.BlockSpec((1, d), lambda i: (0, 0))
    return pl.pallas_call(
        body, name=name, grid=(rows // tr,),
        in_specs=[row, vec, row],
        out_specs=[pl.BlockSpec((1, 128), lambda i: (0, 0)), row, row, vec],
        out_shape=[jax.ShapeDtypeStruct((1, 128), F32), jax.ShapeDtypeStruct((rows, d), F32),
                   jax.ShapeDtypeStruct((rows, d), BF16), jax.ShapeDtypeStruct((1, d), F32)],
        compiler_params=_params(("arbitrary",)),
    )(x2, w, target)


def _adamw_math(w, g, m, v):
    m = ADAM_B1 * m + (1.0 - ADAM_B1) * g
    v = ADAM_B2 * v + (1.0 - ADAM_B2) * (g * g)
    m_hat = m / (1.0 - ADAM_B1 ** ADAM_STEP)
    v_hat = v / (1.0 - ADAM_B2 ** ADAM_STEP)
    delta = -ADAM_LR * (m_hat / (jnp.sqrt(v_hat) + ADAM_EPS) + ADAM_WD * w)
    return delta, m, v


def _adamw(w, g, m, v, name):
    rows, cols = w.shape
    tr = _row_tile(rows, cols * 4, limit=1024 * 1024)

    def body(w_ref, g_ref, m_ref, v_ref, d_ref, mo_ref, vo_ref):
        delta, m_new, v_new = _adamw_math(w_ref[...], g_ref[...], m_ref[...], v_ref[...])
        d_ref[...] = delta
        mo_ref[...] = m_new
        vo_ref[...] = v_new

    blk = pl.BlockSpec((tr, cols), lambda i: (i, 0))
    shp = jax.ShapeDtypeStruct((rows, cols), F32)
    return pl.pallas_call(
        body, name=name, grid=(rows // tr,),
        in_specs=[blk] * 4, out_specs=[blk] * 3, out_shape=[shp] * 3,
        compiler_params=_params(("parallel",)),
    )(w, g, m, v)


_DOT_DIMS = {"nn": ((1,), (0,)), "nt": ((1,), (1,)), "tn": ((0,), (0,))}


def _matmul(name, mode, a_list, b_list, acc_of, m, n, k, tm, tn, tk, extras, out_dtypes, epilogue,
            a_koff=None, b_koff=None, after=()):
    after = tuple(after)
    assert m % tm == 0 and n % tn == 0 and k % tk == 0, (name, m, n, k, tm, tn, tk)
    nk = k // tk
    n_acc = max(acc_of) + 1
    n_pairs = len(a_list)
    a_koff = a_koff or [0] * n_pairs
    b_koff = b_koff or [0] * n_pairs
    dims = (_DOT_DIMS[mode], ((), ()))
    n_ext, n_out = len(extras), len(out_dtypes)

    def body(*refs):
        a_refs = refs[:n_pairs]
        b_refs = refs[n_pairs:2 * n_pairs]
        e_refs = refs[2 * n_pairs:2 * n_pairs + n_ext]
        first_out = 2 * n_pairs + n_ext + len(after)
        o_refs = refs[first_out:first_out + n_out]
        acc_refs = refs[first_out + n_out:]
        parts = [None] * n_acc
        for p in range(n_pairs):
            d = lax.dot_general(a_refs[p][...], b_refs[p][...], dims, preferred_element_type=F32)
            parts[acc_of[p]] = d if parts[acc_of[p]] is None else parts[acc_of[p]] + d

        def finish(accs):
            outs = epilogue(accs, [e[...] for e in e_refs])
            for o_ref, o in zip(o_refs, outs):
                o_ref[...] = o.astype(o_ref.dtype)

        if nk == 1:
            finish(parts)
        else:
            kk = pl.program_id(2)

            @pl.when(kk == 0)
            def _():
                for acc_ref, part in zip(acc_refs, parts):
                    acc_ref[...] = part

            @pl.when(kk != 0)
            def _():
                for acc_ref, part in zip(acc_refs, parts):
                    acc_ref[...] += part

            @pl.when(kk == nk - 1)
            def _():
                finish([acc_ref[...] for acc_ref in acc_refs])

    def a_spec(off):
        if mode == "tn":
            return pl.BlockSpec((tk, tm), lambda i, j, kk: (kk + off, i))
        return pl.BlockSpec((tm, tk), lambda i, j, kk: (i, kk + off))

    def b_spec(off):
        if mode == "nt":
            return pl.BlockSpec((tn, tk), lambda i, j, kk: (j, kk + off))
        return pl.BlockSpec((tk, tn), lambda i, j, kk: (kk + off, j))

    tile = pl.BlockSpec((tm, tn), lambda i, j, kk: (i, j))
    scratch = [pltpu.VMEM((tm, tn), F32) for _ in range(n_acc)] if nk > 1 else []
    return pl.pallas_call(
        body, name=name, grid=(m // tm, n // tn, nk),
        in_specs=[a_spec(o) for o in a_koff] + [b_spec(o) for o in b_koff] + [tile] * n_ext + [ANY] * len(after),
        out_specs=[tile] * n_out,
        out_shape=[jax.ShapeDtypeStruct((m, n), dt) for dt in out_dtypes],
        scratch_shapes=scratch,
        compiler_params=_params(("parallel", "parallel", "arbitrary")),
    )(*a_list, *b_list, *extras, *after)


def _epi_plain(accs, extras):
    return (accs[0],)


def _epi_residual(accs, extras):
    return (accs[0] + extras[0],)


def _epi_two(accs, extras):
    return accs[0], accs[1]


def _epi_swiglu(accs, extras):
    g, u = accs
    return g, u, g * _sigmoid(g) * u


def _epi_swiglu_bwd(accs, extras):
    da = accs[0]
    g, u = extras
    sg = _sigmoid(g)
    dg = da * u * sg * (1.0 + g * (1.0 - sg))
    du = da * g * sg
    return dg, du


_NT_DIMS = (((1,), (1,)), ((), ()))
_TN_DIMS = (((0,), (0,)), ((), ()))


def _tile_delta(tq, tk):
    return lax.broadcasted_iota(I32, (tq, tk), 0) - lax.broadcasted_iota(I32, (tq, tk), 1)


def _attn_mask_bias(delta, slope):
    count = jnp.zeros(delta.shape, I32)
    for window, dilation in DILATED_PATTERNS:
        hit = ((delta & (dilation - 1)) == 0) & (delta <= window)
        count = count + jnp.where(hit, 1, 0)
    valid = (delta >= 0) & (count > 0)
    logm = jnp.where(count == 3, math.log(3.0), jnp.where(count == 2, math.log(2.0), 0.0))
    return valid, logm - slope * delta.astype(F32)


def _alibi_slopes():
    return [2.0 ** (-8.0 * (h + 1) / ATTN_HEADS) for h in range(ATTN_HEADS)]


def _attn_fwd(proj):
    s = proj.shape[0]
    t = SEQ_TILE
    hd = ATTN_HEAD_DIM
    nh = ATTN_HEADS
    scale = 1.0 / math.sqrt(hd)
    slopes = _alibi_slopes()

    def body(q_ref, k_ref, v_ref, mix_ref, o_ref, lse_ref):
        h = pl.program_id(0)
        i = pl.program_id(1)
        slope = _select_by_index(h, slopes)
        q = q_ref[...].astype(BF16)
        base = _tile_delta(t, t)

        def step(j, carry):
            m_i, l_i, acc = carry
            rows = pl.ds(pl.multiple_of(j * t, t), t)
            kj = k_ref[rows, :].astype(BF16)
            vj = v_ref[rows, :].astype(BF16)
            sc = lax.dot_general(q, kj, _NT_DIMS, preferred_element_type=F32) * scale
            valid, bias = _attn_mask_bias(base + (i - j) * t, slope)
            sc = jnp.where(valid, sc + bias, NEG_BIG)
            m_new = jnp.maximum(m_i, jnp.max(sc, axis=-1, keepdims=True))
            p = jnp.exp(sc - m_new)
            alpha = jnp.exp(m_i - m_new)
            l_new = alpha * l_i + jnp.sum(p, axis=-1, keepdims=True)
            acc = alpha * acc + jnp.dot(p.astype(BF16), vj, preferred_element_type=F32)
            return m_new, l_new, acc

        init = (jnp.full((t, 1), NEG_BIG, F32), jnp.zeros((t, 1), F32), jnp.zeros((t, hd), F32))
        m_i, l_i, acc = lax.fori_loop(0, i + 1, step, init)
        out = acc / l_i
        o_ref[...] = out
        mix_ref[...] = out.astype(BF16)
        lse_ref[...] = jnp.broadcast_to(m_i + jnp.log(l_i), (t, hd))

    return pl.pallas_call(
        body, name="attn_fwd", grid=(nh, s // t),
        in_specs=[pl.BlockSpec((t, hd), lambda h, i: (i, h)),
                  pl.BlockSpec((s, hd), lambda h, i: (0, nh + h)),
                  pl.BlockSpec((s, hd), lambda h, i: (0, 2 * nh + h))],
        out_specs=[pl.BlockSpec((t, hd), lambda h, i: (i, h))] * 3,
        out_shape=[jax.ShapeDtypeStruct((s, ATTN_WIDTH), BF16),
                   jax.ShapeDtypeStruct((s, ATTN_WIDTH), F32),
                   jax.ShapeDtypeStruct((s, ATTN_WIDTH), F32)],
        compiler_params=_params(("parallel", "arbitrary")),
    )(proj, proj, proj)


def _attn_bwd(proj, attn_out, lse, dmixed, after=()):
    after = tuple(after)
    s = proj.shape[0]
    t = SEQ_TILE
    nt = s // t
    hd = ATTN_HEAD_DIM
    nh = ATTN_HEADS
    scale = 1.0 / math.sqrt(hd)
    slopes = _alibi_slopes()

    def body(q_ref, k_ref, v_ref, o_ref, lse_ref, do_ref, *rest):
        dq_ref, dk_ref, dv_ref, qb, kb, vb, dob, dsum, dq_acc = rest[len(after):]
        h = pl.program_id(0)
        slope = _select_by_index(h, slopes)
        qb[...] = q_ref[...].astype(BF16)
        kb[...] = k_ref[...].astype(BF16)
        vb[...] = v_ref[...].astype(BF16)
        do = do_ref[...]
        dob[...] = do.astype(BF16)
        dsum[...] = jnp.broadcast_to(jnp.sum(do * o_ref[...], axis=-1, keepdims=True), (s, hd))
        dq_acc[...] = jnp.zeros((s, hd), F32)
        base = _tile_delta(t, t)

        def over_keys(j, _):
            krows = pl.ds(pl.multiple_of(j * t, t), t)
            kj = kb[krows, :]
            vj = vb[krows, :]

            def over_queries(i, carry):
                dk, dv = carry
                qrows = pl.ds(pl.multiple_of(i * t, t), t)
                qi = qb[qrows, :]
                doi = dob[qrows, :]
                lse_i = lse_ref[qrows, :][:, :1]
                dsum_i = dsum[qrows, :][:, :1]
                sc = lax.dot_general(qi, kj, _NT_DIMS, preferred_element_type=F32) * scale
                valid, bias = _attn_mask_bias(base + (i - j) * t, slope)
                p = jnp.exp(jnp.where(valid, sc + bias, NEG_BIG) - lse_i)
                dp = lax.dot_general(doi, vj, _NT_DIMS, preferred_element_type=F32)
                ds = (p * (dp - dsum_i) * scale).astype(BF16)
                dv = dv + lax.dot_general(p.astype(BF16), doi, _TN_DIMS, preferred_element_type=F32)
                dk = dk + lax.dot_general(ds, qi, _TN_DIMS, preferred_element_type=F32)
                dq_acc[qrows, :] += jnp.dot(ds, kj, preferred_element_type=F32)
                return dk, dv

            zero = jnp.zeros((t, hd), F32)
            dk, dv = lax.fori_loop(j, nt, over_queries, (zero, zero))
            dk_ref[krows, :] = dk.astype(BF16)
            dv_ref[krows, :] = dv.astype(BF16)
            return 0

        lax.fori_loop(0, nt, over_keys, 0)
        dq_ref[...] = dq_acc[...].astype(BF16)

    def col(off):
        return pl.BlockSpec((s, hd), lambda h: (0, off + h))

    return pl.pallas_call(
        body, name="attn_bwd", grid=(nh,),
        in_specs=[col(0), col(nh), col(2 * nh), col(0), col(0), col(0)] + [ANY] * len(after),
        out_specs=[col(0)] * 3,
        out_shape=[jax.ShapeDtypeStruct((s, ATTN_WIDTH), BF16)] * 3,
        scratch_shapes=[pltpu.VMEM((s, hd), BF16)] * 4 + [pltpu.VMEM((s, hd), F32)] * 2,
        compiler_params=_params(("arbitrary",)),
    )(proj, proj, proj, attn_out, lse, dmixed, *after)


def _ret_log_gammas():
    return [math.log(1.0 - 2.0 ** (-5.0 - h)) for h in range(RET_HEADS)]


def _ret_decay(delta, log_gamma):
    dec = jnp.exp(delta.astype(F32) * log_gamma) * (1.0 / math.sqrt(RET_HEAD_DIM))
    return jnp.where(delta >= 0, dec, 0.0)


def _ret_fwd(proj):
    s = proj.shape[0]
    t = SEQ_TILE
    hd = RET_HEAD_DIM
    nh = RET_HEADS
    log_gammas = _ret_log_gammas()
    c0 = 3 * ATTN_WIDTH // hd

    def body(q_ref, k_ref, v_ref, g_ref, mix_ref, raw_ref):
        h = pl.program_id(0)
        i = pl.program_id(1)
        log_gamma = _select_by_index(h, log_gammas)
        q = q_ref[...].astype(BF16)
        base = _tile_delta(t, t)

        def step(j, acc):
            rows = pl.ds(pl.multiple_of(j * t, t), t)
            kj = k_ref[rows, :].astype(BF16)
            vj = v_ref[rows, :].astype(BF16)
            sc = lax.dot_general(q, kj, _NT_DIMS, preferred_element_type=F32)
            sc = sc * _ret_decay(base + (i - j) * t, log_gamma)
            return acc + jnp.dot(sc.astype(BF16), vj, preferred_element_type=F32)

        ret = lax.fori_loop(0, i + 1, step, jnp.zeros((t, hd), F32))
        raw_ref[...] = ret
        r = lax.rsqrt(jnp.mean(ret * ret, axis=-1, keepdims=True) + NORM_EPS)
        g = g_ref[...]
        mix_ref[...] = (g * _sigmoid(g) * (ret * r)).astype(BF16)

    return pl.pallas_call(
        body, name="ret_fwd", grid=(nh, s // t),
        in_specs=[pl.BlockSpec((t, hd), lambda h, i: (i, c0 + h)),
                  pl.BlockSpec((s, hd), lambda h, i: (0, c0 + nh + h)),
                  pl.BlockSpec((s, hd), lambda h, i: (0, c0 + 2 * nh + h)),
                  pl.BlockSpec((t, hd), lambda h, i: (i, c0 + 3 * nh + h))],
        out_specs=[pl.BlockSpec((t, hd), lambda h, i: (i, h))] * 2,
        out_shape=[jax.ShapeDtypeStruct((s, RET_WIDTH), BF16), jax.ShapeDtypeStruct((s, RET_WIDTH), F32)],
        compiler_params=_params(("parallel", "arbitrary")),
    )(proj, proj, proj, proj)


def _ret_bwd(proj, ret_raw, dmixed, after=()):
    after = tuple(after)
    s = proj.shape[0]
    t = SEQ_TILE
    nt = s // t
    hd = RET_HEAD_DIM
    nh = RET_HEADS
    log_gammas = _ret_log_gammas()
    c0 = 3 * ATTN_WIDTH // hd
    mixed_blocks = ATTN_WIDTH // hd

    def body(q_ref, k_ref, v_ref, g_ref, raw_ref, dmix_ref, *rest):
        dq_ref, dk_ref, dv_ref, dg_ref, qb, kb, vb, dretb, dq_acc = rest[len(after):]
        h = pl.program_id(0)
        log_gamma = _select_by_index(h, log_gammas)
        qb[...] = q_ref[...].astype(BF16)
        kb[...] = k_ref[...].astype(BF16)
        vb[...] = v_ref[...].astype(BF16)
        ret = raw_ref[...]
        r = lax.rsqrt(jnp.mean(ret * ret, axis=-1, keepdims=True) + NORM_EPS)
        normed = ret * r
        g = g_ref[...]
        sg = _sigmoid(g)
        dout = dmix_ref[...]
        dg_ref[...] = (dout * normed * sg * (1.0 + g * (1.0 - sg))).astype(BF16)
        dn = dout * g * sg
        dret = r * (dn - normed * jnp.mean(dn * normed, axis=-1, keepdims=True))
        dretb[...] = dret.astype(BF16)
        dq_acc[...] = jnp.zeros((s, hd), F32)
        base = _tile_delta(t, t)

        def over_keys(j, _):
            krows = pl.ds(pl.multiple_of(j * t, t), t)
            kj = kb[krows, :]
            vj = vb[krows, :]

            def over_queries(i, carry):
                dk, dv = carry
                qrows = pl.ds(pl.multiple_of(i * t, t), t)
                qi = qb[qrows, :]
                doi = dretb[qrows, :]
                dec = _ret_decay(base + (i - j) * t, log_gamma)
                a = (lax.dot_general(qi, kj, _NT_DIMS, preferred_element_type=F32) * dec).astype(BF16)
                da = (lax.dot_general(doi, vj, _NT_DIMS, preferred_element_type=F32) * dec).astype(BF16)
                dv = dv + lax.dot_general(a, doi, _TN_DIMS, preferred_element_type=F32)
                dk = dk + lax.dot_general(da, qi, _TN_DIMS, preferred_element_type=F32)
                dq_acc[qrows, :] += jnp.dot(da, kj, preferred_element_type=F32)
                return dk, dv

            zero = jnp.zeros((t, hd), F32)
            dk, dv = lax.fori_loop(j, nt, over_queries, (zero, zero))
            dk_ref[krows, :] = dk.astype(BF16)
            dv_ref[krows, :] = dv.astype(BF16)
            return 0

        lax.fori_loop(0, nt, over_keys, 0)
        dq_ref[...] = dq_acc[...].astype(BF16)

    def col(off):
        return pl.BlockSpec((s, hd), lambda h: (0, off + h))

    return pl.pallas_call(
        body, name="ret_bwd", grid=(nh,),
        in_specs=[col(c0), col(c0 + nh), col(c0 + 2 * nh), col(c0 + 3 * nh), col(0), col(mixed_blocks)]
        + [ANY] * len(after),
        out_specs=[col(0)] * 4,
        out_shape=[jax.ShapeDtypeStruct((s, RET_WIDTH), BF16)] * 4,
        scratch_shapes=[pltpu.VMEM((s, hd), BF16)] * 4 + [pltpu.VMEM((s, hd), F32)],
        compiler_params=_params(("arbitrary",)),
    )(proj, proj, proj, proj, ret_raw, dmixed, *after)


_FLIPS = (2, 1, 3)


def _other_chips(x, y):
    return [(1 - x, y), (x, 1 - y), (1 - x, 1 - y)]


_HBM = pl.BlockSpec(memory_space=pltpu.HBM)
_SEM = pl.BlockSpec(memory_space=pltpu.SEMAPHORE)
_EFFECT = pltpu.SideEffectType.DATAFLOW_SIDE_EFFECTING


def _in_hbm(a):
    return pltpu.with_memory_space_constraint(a, pltpu.HBM)


def _weight_view(w, column_sharded):
    if column_sharded:
        return w.reshape(2, w.shape[0] // 2, w.shape[1])
    return w.reshape(N_CHIPS, 2, w.shape[0] // (2 * N_CHIPS), w.shape[1])


def _weight_unview(v):
    if v.ndim == 3:
        return v.reshape(2 * v.shape[1], v.shape[2])
    return v.reshape(N_CHIPS * 2 * v.shape[2], v.shape[3])


def _weight_region(buf, shard, half):
    if len(buf.shape) == 3:
        cols = buf.shape[2] // N_CHIPS
        return buf.at[half, :, pl.ds(shard * cols, cols)]
    return buf.at[shard, half]


def _remote(where, send_sem, recv_sem, to):
    return pltpu.make_async_remote_copy(src_ref=where, dst_ref=where, send_sem=send_sem, recv_sem=recv_sem,
                                        device_id=to, device_id_type=MESH)


def _for_my_shard(fn):
    x, y, _ = _place()
    for ss in range(N_CHIPS):
        pl.when(2 * x + y == ss)(functools.partial(fn, ss))


def _gather_start(views):
    n_w = len(views)

    def body(*refs):
        send_sems, recv_sems = refs[n_w:n_w + 2]
        bufs = refs[n_w + 2:]
        x, y, c = _place()

        def start(ss):
            for w in range(n_w):
                for j, chip in enumerate(_other_chips(x, y)):
                    _remote(_weight_region(bufs[w], ss, c), send_sems.at[3 * w + j], recv_sems.at[3 * w + j],
                            (*chip, c)).start()

        _for_my_shard(start)

    return pl.pallas_call(
        body, name="gather_start",
        in_specs=[_HBM] * n_w, out_specs=[_SEM, _SEM] + [_HBM] * n_w,
        out_shape=[pltpu.SemaphoreType.DMA((3 * n_w,)), pltpu.SemaphoreType.DMA((3 * n_w,))]
        + [pltpu.HBM(v.shape, BF16) for v in views],
        input_output_aliases={w: 2 + w for w in range(n_w)},
        compiler_params=pltpu.CompilerParams(has_side_effects=_EFFECT),
    )(*[_in_hbm(v) for v in views])


def _gather_forward(views, which, send_sems, recv_sems, after, name):
    n_w = len(views)

    def body(*refs):
        send_in, recv_in = refs[n_w:n_w + 2]
        fwd_send, fwd_recv = refs[n_w + 3:n_w + 5]
        bufs = refs[n_w + 5:]
        x, y, c = _place()
        sibling = (x, y, 1 - c)

        def forward(ss):
            for i, w in enumerate(which):
                for j in range(3):
                    landed = _weight_region(bufs[i], ss ^ _FLIPS[j], c)
                    _remote(landed, send_in.at[3 * w + j], recv_in.at[3 * w + j], sibling).wait_recv()
                    _remote(landed, fwd_send.at[3 * i + j], fwd_recv.at[3 * i + j], sibling).start()

        _for_my_shard(forward)
        for i, w in enumerate(which):
            for j in range(3):
                _remote(_weight_region(bufs[i], 0, 0), send_in.at[3 * w + j], recv_in.at[3 * w + j],
                        sibling).wait_send()

    return pl.pallas_call(
        body, name=name,
        in_specs=[_HBM] * n_w + [_SEM, _SEM, ANY], out_specs=[_SEM, _SEM] + [_HBM] * n_w,
        out_shape=[pltpu.SemaphoreType.DMA((3 * n_w,)), pltpu.SemaphoreType.DMA((3 * n_w,))]
        + [pltpu.HBM(v.shape, BF16) for v in views],
        input_output_aliases={w: 2 + w for w in range(n_w)},
        compiler_params=pltpu.CompilerParams(has_side_effects=_EFFECT),
    )(*views, send_sems, recv_sems, after)


def _gather_end(views, fwd_send, fwd_recv, after, name):
    n_w = len(views)

    def body(*refs):
        fwd_send_ref, fwd_recv_ref = refs[n_w:n_w + 2]
        bufs = refs[n_w + 3:]
        x, y, c = _place()
        for i in range(n_w):
            for j in range(3):
                cp = _remote(_weight_region(bufs[i], 0, 0), fwd_send_ref.at[3 * i + j], fwd_recv_ref.at[3 * i + j],
                             (x, y, 1 - c))
                cp.wait_recv()
                cp.wait_send()

    outs = pl.pallas_call(
        body, name=name,
        in_specs=[_HBM] * n_w + [_SEM, _SEM, ANY], out_specs=[_HBM] * n_w,
        out_shape=[pltpu.HBM(v.shape, BF16) for v in views],
        input_output_aliases={w: w for w in range(n_w)},
        compiler_params=pltpu.CompilerParams(has_side_effects=_EFFECT),
    )(*views, fwd_send, fwd_recv, after)
    return [_weight_unview(o) for o in outs]


def _split_start(name, bufs, n_sems, copies):
    n = len(bufs)

    def body(*refs):
        send_sems, recv_sems = refs[n:n + 2]
        for cp in copies(refs[n + 2:], send_sems, recv_sems):
            cp.start()

    outs = pl.pallas_call(
        body, name=name,
        in_specs=[_HBM] * n, out_specs=[_SEM, _SEM] + [_HBM] * n,
        out_shape=[pltpu.SemaphoreType.DMA((n_sems,)), pltpu.SemaphoreType.DMA((n_sems,))]
        + [pltpu.HBM(b.shape, b.dtype) for b in bufs],
        input_output_aliases={i: 2 + i for i in range(n)},
        compiler_params=pltpu.CompilerParams(has_side_effects=_EFFECT),
    )(*[_in_hbm(b) for b in bufs])
    return outs[0], outs[1], list(outs[2:])


def _split_wait(name, bufs, send_sems, recv_sems, copies, after):
    n = len(bufs)

    def body(*refs):
        send_ref, recv_ref = refs[n:n + 2]
        for cp in copies(refs[n + 3:], send_ref, recv_ref):
            cp.wait()

    return list(pl.pallas_call(
        body, name=name,
        in_specs=[_HBM] * n + [_SEM, _SEM, ANY], out_specs=[_HBM] * n,
        out_shape=[pltpu.HBM(b.shape, b.dtype) for b in bufs],
        input_output_aliases={i: i for i in range(n)},
        compiler_params=pltpu.CompilerParams(has_side_effects=_EFFECT),
    )(*bufs, send_sems, recv_sems, after))


def _halves_copies(n_w):
    def copies(bufs, send_sems, recv_sems):
        x, y, c = _place()
        out = []
        for w in range(n_w):
            view, land = bufs[w], bufs[n_w + w]
            src = view.at[1 - c] if len(view.shape) == 3 else view.at[:, 1 - c]
            out.append(pltpu.make_async_remote_copy(
                src_ref=src, dst_ref=land, send_sem=send_sems.at[w], recv_sem=recv_sems.at[w],
                device_id=(x, y, 1 - c), device_id_type=MESH))
        return out
    return copies


def _pieces_copies(n_w):
    def copies(bufs, send_sems, recv_sems):
        x, y, c = _place()
        out = []
        for w in range(n_w):
            for j, (cx, cy) in enumerate(_other_chips(x, y)):
                out.append(pltpu.make_async_remote_copy(
                    src_ref=bufs[w].at[2 * cx + cy], dst_ref=bufs[n_w + w].at[j],
                    send_sem=send_sems.at[3 * w + j], recv_sem=recv_sems.at[3 * w + j],
                    device_id=(cx, cy, c), device_id_type=MESH))
        return out
    return copies


def _join_copies(n_w):
    def copies(bufs, send_sems, recv_sems):
        x, y, c = _place()
        return [pltpu.make_async_remote_copy(
            src_ref=bufs[w].at[c], dst_ref=bufs[w].at[c], send_sem=send_sems.at[w], recv_sem=recv_sems.at[w],
            device_id=(x, y, 1 - c), device_id_type=MESH) for w in range(n_w)]
    return copies


def _grad_view(g, column_sharded):
    return _weight_view(g, column_sharded)


def _halves_landing(view):
    shape = view.shape[1:] if view.ndim == 3 else (N_CHIPS,) + view.shape[2:]
    return lax.empty(shape, BF16)


def _halves_start(tag, grads, column_sharded):
    views = [_weight_view(g, cs) for g, cs in zip(grads, column_sharded)]
    n = len(views)
    return _split_start("halves_start_" + tag, views + [_halves_landing(v) for v in views], n, _halves_copies(n))


def _halves_wait(tag, state, after):
    send_sems, recv_sems, bufs = state
    n = len(bufs) // 2
    bufs = _split_wait("halves_wait_" + tag, bufs, send_sems, recv_sems, _halves_copies(n), after)
    return bufs[:n], bufs[n:]


def _pieces_start(tag, pieces):
    n = len(pieces)
    landing = [lax.empty((3,) + p.shape[1:], BF16) for p in pieces]
    return _split_start("pieces_start_" + tag, list(pieces) + landing, 3 * n, _pieces_copies(n))


def _pieces_wait(tag, state, after):
    send_sems, recv_sems, bufs = state
    n = len(bufs) // 2
    bufs = _split_wait("pieces_wait_" + tag, bufs, send_sems, recv_sems, _pieces_copies(n), after)
    return bufs[:n], bufs[n:]


def _join_start(tag, shards):
    n = len(shards)
    return _split_start("join_start_" + tag, list(shards), n, _join_copies(n))


def _join_wait(tag, state, after):
    send_sems, recv_sems, bufs = state
    bufs = _split_wait("join_wait_" + tag, bufs, send_sems, recv_sems, _join_copies(len(bufs)), after)
    return [b.reshape(2 * b.shape[1], b.shape[2]) for b in bufs]


def _chip_sum_col(g3, sib, c_arr, name):
    _, hk, n = g3.shape
    cols = n // N_CHIPS
    tr = _row_tile(hk, cols * 2, limit=1024 * 1024)

    def body(c_ref, g_ref, s_ref, o_ref):
        del c_ref
        o_ref[...] = (g_ref[...].astype(F32) + s_ref[...].astype(F32)).astype(BF16)

    grid_spec = pltpu.PrefetchScalarGridSpec(
        num_scalar_prefetch=1, grid=(N_CHIPS, hk // tr),
        in_specs=[pl.BlockSpec((None, tr, cols), lambda p, r, c_ref: (c_ref[0], r, p)),
                  pl.BlockSpec((tr, cols), lambda p, r, c_ref: (r, p))],
        out_specs=pl.BlockSpec((None, tr, cols), lambda p, r, c_ref: (p, r, 0)))
    return pl.pallas_call(
        body, name=name, grid_spec=grid_spec,
        out_shape=jax.ShapeDtypeStruct((N_CHIPS, hk, cols), BF16),
        compiler_params=_params(("parallel", "parallel")),
    )(c_arr, g3, sib)


def _chip_sum_row(g4, sib, c_arr, name):
    _, _, hr, n = g4.shape
    tr = _row_tile(hr, n * 2, limit=1024 * 1024)

    def body(c_ref, g_ref, s_ref, o_ref):
        del c_ref
        o_ref[...] = (g_ref[...].astype(F32) + s_ref[...].astype(F32)).astype(BF16)

    grid_spec = pltpu.PrefetchScalarGridSpec(
        num_scalar_prefetch=1, grid=(N_CHIPS, hr // tr),
        in_specs=[pl.BlockSpec((None, None, tr, n), lambda p, r, c_ref: (p, c_ref[0], r, 0)),
                  pl.BlockSpec((None, tr, n), lambda p, r, c_ref: (p, r, 0))],
        out_specs=pl.BlockSpec((None, tr, n), lambda p, r, c_ref: (p, r, 0)))
    return pl.pallas_call(
        body, name=name, grid_spec=grid_spec,
        out_shape=jax.ShapeDtypeStruct((N_CHIPS, hr, n), BF16),
        compiler_params=_params(("parallel", "parallel")),
    )(c_arr, g4, sib)


def _sum_pieces(pieces, received, place_arr, name):
    _, r, n = pieces.shape
    tr = _row_tile(r, n * 4, limit=1024 * 1024)

    def body(p_ref, own_ref, r0_ref, r1_ref, r2_ref, o_ref):
        del p_ref
        acc = own_ref[...].astype(F32) + r0_ref[...].astype(F32)
        acc = acc + r1_ref[...].astype(F32)
        o_ref[...] = acc + r2_ref[...].astype(F32)

    def recv_spec(j):
        return pl.BlockSpec((None, tr, n), lambda i, p_ref: (j, i, 0))

    grid_spec = pltpu.PrefetchScalarGridSpec(
        num_scalar_prefetch=1, grid=(r // tr,),
        in_specs=[pl.BlockSpec((None, tr, n), lambda i, p_ref: (p_ref[0], i, 0)),
                  recv_spec(0), recv_spec(1), recv_spec(2)],
        out_specs=pl.BlockSpec((None, tr, n), lambda i, p_ref: (p_ref[1], i, 0)))
    return pl.pallas_call(
        body, name=name, grid_spec=grid_spec,
        out_shape=jax.ShapeDtypeStruct((2, r, n), F32),
        compiler_params=_params(("parallel",)),
    )(place_arr, pieces, received, received, received)


def _norm_weights_step(parts, w, m, v):
    rows, d = parts.shape

    def body(p_ref, w_ref, m_ref, v_ref, g_ref, d_ref, mo_ref, vo_ref, gathered, send_sems, recv_sems):
        x, y, c = _place()
        me = 4 * x + 2 * y + c
        gathered[me] = p_ref[...]
        copies = []
        for k in range(1, N_DEV):
            peer = (x ^ ((k >> 2) & 1), y ^ ((k >> 1) & 1), c ^ (k & 1))
            copies.append(pltpu.make_async_remote_copy(
                src_ref=p_ref, dst_ref=gathered.at[me], send_sem=send_sems.at[k - 1],
                recv_sem=recv_sems.at[k - 1], device_id=peer, device_id_type=MESH))
        for cp in copies:
            cp.start()
        for cp in copies:
            cp.wait()
        g = gathered[0]
        for k in range(1, N_DEV):
            g = g + gathered[k]
        delta, m_new, v_new = _adamw_math(w_ref[...], g, m_ref[...], v_ref[...])
        g_ref[...] = g
        d_ref[...] = delta
        mo_ref[...] = m_new
        vo_ref[...] = v_new

    vmem = pl.BlockSpec(memory_space=pltpu.VMEM)
    shp = jax.ShapeDtypeStruct((rows, d), F32)
    return pl.pallas_call(
        body, name="norm_weights_step",
        in_specs=[vmem] * 4, out_specs=[vmem] * 4, out_shape=[shp] * 4,
        scratch_shapes=[pltpu.VMEM((N_DEV, rows, d), F32), pltpu.SemaphoreType.DMA((N_DEV - 1,)),
                        pltpu.SemaphoreType.DMA((N_DEV - 1,))],
        compiler_params=pltpu.CompilerParams(has_side_effects=True),
    )(parts, w, m, v)


def kernel(x, norm_mix_w, w_in, w_out, norm_ffn_w, w_gate, w_up, w_down, norm_final_w, loss_target, m_norm_mix_w, m_w_in, m_w_out, m_norm_ffn_w, m_w_gate, m_w_up, m_w_down, m_norm_final_w, v_norm_mix_w, v_w_in, v_w_out, v_norm_ffn_w, v_w_gate, v_w_up, v_w_down, v_norm_final_w):
    s, d = x.shape[1], x.shape[2]
    xs = x.reshape(s, d)
    target = loss_target.reshape(s, d)
    big = {"w_in": (w_in, m_w_in, v_w_in), "w_out": (w_out, m_w_out, v_w_out),
           "w_gate": (w_gate, m_w_gate, v_w_gate), "w_up": (w_up, m_w_up, v_w_up),
           "w_down": (w_down, m_w_down, v_w_down)}
    big = {k: tuple(a.reshape(a.shape[1:]) for a in t) for k, t in big.items()}
    col_names, row_names = ("w_in", "w_gate", "w_up"), ("w_out", "w_down")
    n_in = N_CHIPS * big["w_in"][0].shape[1]
    ffn = N_CHIPS * big["w_gate"][0].shape[1]
    mix = ATTN_WIDTH + RET_WIDTH
    c_arr = lax.axis_index("c").astype(I32).reshape(1)
    shard_arr = (2 * lax.axis_index("x") + lax.axis_index("y")).astype(I32).reshape(1)
    place_arr = jnp.concatenate([shard_arr, c_arr])

    order = ("w_in", "w_out", "w_gate", "w_up", "w_down")
    views = [_weight_view(_cast_into_full(big[k][0], shard_arr, k in col_names, "cast_" + k), k in col_names)
             for k in order]
    send_sems, recv_sems, v_in, v_out, v_gate, v_up, v_down = _gather_start(views)

    h1 = _rms_fwd(xs, norm_mix_w, "rms_mix_fwd")
    fs, fr, v_in = _gather_forward([v_in], [0], send_sems, recv_sems, h1, "gather_forward_in")
    wi, = _gather_end([v_in], fs, fr, h1, "gather_end_in")
    proj, = _matmul("in_proj", "nn", [h1], [wi], [0], s, n_in, d, s, 512, d, [], [F32], _epi_plain)
    fs, fr, v_out = _gather_forward([v_out], [1], send_sems, recv_sems, proj, "gather_forward_out")
    attn_b, attn_o, lse = _attn_fwd(proj)
    wo, = _gather_end([v_out], fs, fr, attn_o, "gather_end_out")
    fs, fr, v_gate, v_up = _gather_forward([v_gate, v_up], [2, 3], send_sems, recv_sems, attn_o,
                                           "gather_forward_gate_up")
    ret_b, ret_raw = _ret_fwd(proj)
    mixed = jnp.concatenate([attn_b, ret_b], axis=1)
    x1, = _matmul("out_proj", "nn", [mixed], [wo], [0], s, d, mix, s, 512, mix, [xs], [F32], _epi_residual)
    h2 = _rms_fwd(x1, norm_ffn_w, "rms_ffn_fwd")
    wg, wu = _gather_end([v_gate, v_up], fs, fr, h2, "gather_end_gate_up")
    gate, up, act = _matmul("gate_up", "nn", [h2, h2], [wg, wu], [0, 1], s, ffn, d, s, 256, d, [],
                            [F32, F32, BF16], _epi_swiglu)
    fs, fr, v_down = _gather_forward([v_down], [4], send_sems, recv_sems, act, "gather_forward_down")
    wd, = _gather_end([v_down], fs, fr, act, "gather_end_down")
    x2, = _matmul("down_proj", "nn", [act], [wd], [0], s, d, ffn, s, 512, ffn // 4, [x1], [F32],
                  _epi_residual)
    loss_row, dx2, dx2b, dwf = _final_norm_loss(x2, norm_final_w.reshape(1, d), target, "final_norm_loss")

    names = col_names + row_names
    grads, new = {}, {}

    def chip_sums(tag_names, views, sibs):
        return [(_chip_sum_col if k in col_names else _chip_sum_row)(v, sb, c_arr, "chip_sum_" + k)
                for k, v, sb in zip(tag_names, views, sibs)]

    def piece_sums(tag_names, pieces, received):
        return [_sum_pieces(p, r, place_arr, "sum_pieces_" + k) for k, p, r in zip(tag_names, pieces, received)]

    def update(k):
        new[k] = _adamw(big[k][0], grads[k], big[k][1], big[k][2], "adamw_" + k)

    dgate, dup = _matmul("d_act", "nt", [dx2b], [wd], [0], s, ffn, d, s, 256, d, [gate, up],
                         [BF16, BF16], _epi_swiglu_bwd)
    g_wd, = _matmul("g_w_down", "tn", [act], [dx2b], [0], ffn, d, s, 512, d, s, [], [BF16], _epi_plain)
    halves_d = _halves_start("down", [g_wd], [False])
    dh2, = _matmul("d_h2", "nt", [dgate, dup], [wg, wu], [0, 0], s, d, ffn, s, 1024, 512, [], [F32],
                   _epi_plain, after=halves_d[2][-1:])
    pieces_d = _pieces_start("down", chip_sums(["w_down"], *_halves_wait("down", halves_d, dh2)))
    g_wg, g_wu = _matmul("g_w_gate_up", "tn", [h2, h2], [dgate, dup], [0, 1], d, ffn, s, 1024, 512, s, [],
                         [BF16, BF16], _epi_two, after=pieces_d[2][-1:])
    halves_gu = _halves_start("gate_up", [g_wg, g_wu], [True, True])
    dx1, dx1b, dw_ffn = _rms_bwd(x1, norm_ffn_w, dh2, dx2, "rms_ffn_bwd", after=halves_gu[2][-1:])

    dmixed, = _matmul("d_mixed", "nt", [dx1b], [wo], [0], s, mix, d, s, 512, d, [], [F32], _epi_plain)
    pieces_gu = _pieces_start("gate_up", chip_sums(["w_gate", "w_up"], *_halves_wait("gate_up", halves_gu, dmixed)))
    g_wo, = _matmul("g_w_out", "tn", [mixed], [dx1b], [0], mix, d, s, 512, d, s, [], [BF16], _epi_plain,
                    after=pieces_gu[2][-1:])
    halves_o = _halves_start("out", [g_wo], [False])
    dqa, dka, dva = _attn_bwd(proj, attn_o, lse, dmixed, after=halves_o[2][-1:])
    pieces_o = _pieces_start("out", chip_sums(["w_out"], *_halves_wait("out", halves_o, dqa)))
    dqr, dkr, dvr, dgr = _ret_bwd(proj, ret_raw, dmixed, after=pieces_o[2][-1:])
    dproj = jnp.concatenate([dqa, dka, dva, dqr, dkr, dvr, dgr], axis=1)
    g_wi, = _matmul("g_w_in", "tn", [h1], [dproj], [0], d, n_in, s, 1024, 1024, s, [], [BF16], _epi_plain)
    halves_i = _halves_start("in", [g_wi], [True])
    dh1, = _matmul("d_h1", "nt", [dproj], [wi], [0], s, d, n_in, s, 1024, n_in // 7, [], [F32], _epi_plain,
                   after=halves_i[2][-1:])
    pieces_i = _pieces_start("in", chip_sums(["w_in"], *_halves_wait("in", halves_i, dh1)))
    grad_x, _, dw_mix = _rms_bwd(xs, norm_mix_w, dh1, dx1, "rms_mix_bwd", after=pieces_i[2][-1:])

    def rows8(*vs):
        return jnp.concatenate([v.reshape(1, d) for v in vs] + [jnp.zeros((8 - len(vs), d), F32)], axis=0)

    ng, nd, nm, nv = _norm_weights_step(
        rows8(dw_mix, dw_ffn, dwf), rows8(norm_mix_w, norm_ffn_w, norm_final_w),
        rows8(m_norm_mix_w, m_norm_ffn_w, m_norm_final_w), rows8(v_norm_mix_w, v_norm_ffn_w, v_norm_final_w))

    join_d = _join_start("down", piece_sums(["w_down"], *_pieces_wait("down", pieces_d, ng)))
    join_gu = _join_start("gate_up", piece_sums(["w_gate", "w_up"], *_pieces_wait("gate_up", pieces_gu, join_d[2][0])))
    join_o = _join_start("out", piece_sums(["w_out"], *_pieces_wait("out", pieces_o, join_gu[2][0])))
    grads["w_down"], = _join_wait("down", join_d, join_o[2][0])
    update("w_down")
    grads["w_gate"], grads["w_up"] = _join_wait("gate_up", join_gu, new["w_down"][0])
    update("w_gate")
    update("w_up")
    grads["w_out"], = _join_wait("out", join_o, new["w_up"][0])
    update("w_out")
    join_i = _join_start("in", piece_sums(["w_in"], *_pieces_wait("in", pieces_i, new["w_out"][0])))
    grads["w_in"], = _join_wait("in", join_i, new["w_out"][1])
    update("w_in")

    loss = lax.psum(loss_row[0, 0], ("x", "y", "c"))

    def pack(small, per_weight):
        lead = lambda a: a.reshape((1,) + a.shape)
        return (small[0:1], lead(per_weight["w_in"]), lead(per_weight["w_out"]), small[1:2],
                lead(per_weight["w_gate"]), lead(per_weight["w_up"]), lead(per_weight["w_down"]), small[2])

    return (loss, grad_x.reshape(1, s, d),
            *pack(ng, grads),
            *pack(nd, {k: new[k][0] for k in names}),
            *pack(nm, {k: new[k][1] for k in names}),
            *pack(nv, {k: new[k][2] for k in names}))
```

```python
import functools
import math

import jax
import jax.numpy as jnp
from jax import lax
from jax.experimental import pallas as pl
from jax.experimental.pallas import tpu as pltpu

F32 = jnp.float32
BF16 = jnp.bfloat16
I32 = jnp.int32
MESH = pl.DeviceIdType.MESH
ANY = pl.BlockSpec(memory_space=pl.ANY)

ATTN_HEADS = 8
ATTN_HEAD_DIM = 128
RET_HEADS = 4
RET_HEAD_DIM = 256
ATTN_WIDTH = ATTN_HEADS * ATTN_HEAD_DIM
RET_WIDTH = RET_HEADS * RET_HEAD_DIM
DILATED_PATTERNS = ((128, 1), (512, 4), (2048, 16))
NORM_EPS = 1e-6
ADAM_LR = 0.001
ADAM_B1 = 0.9
ADAM_B2 = 0.999
ADAM_EPS = 1e-08
ADAM_WD = 0.01
ADAM_STEP = 10

N_CHIPS = 4
N_DEV = 8
NEG_BIG = -1e30
SEQ_TILE = 256
VMEM_LIMIT_BYTES = 56 * 1024 * 1024


def _params(semantics=None, vmem=VMEM_LIMIT_BYTES):
    return pltpu.CompilerParams(dimension_semantics=semantics, vmem_limit_bytes=vmem)


def _row_tile(rows, row_bytes, limit=2 * 1024 * 1024, mult=16):
    best = None
    for t in range(mult, rows + 1, mult):
        if rows % t == 0 and t * row_bytes <= limit:
            best = t
    assert best is not None, (rows, row_bytes)
    return best


def _sigmoid(x):
    return 1.0 / (1.0 + jnp.exp(-x))


def _select_by_index(idx, values):
    out = jnp.float32(values[-1])
    for i in range(len(values) - 2, -1, -1):
        out = jnp.where(idx == i, jnp.float32(values[i]), out)
    return out


def _place():
    x, y, c = lax.axis_index("x"), lax.axis_index("y"), lax.axis_index("c")
    return x, y, c


def _cast_into_full(w, shard_arr, column_sharded, name):
    rows, cols = w.shape
    tr = _row_tile(rows, cols * 4)
    steps = rows // tr
    if column_sharded:
        out_shape, out_map = (rows, N_CHIPS * cols), (lambda i, s_ref: (i, s_ref[0]))
    else:
        out_shape, out_map = (N_CHIPS * rows, cols), (lambda i, s_ref: (s_ref[0] * steps + i, 0))

    def body(s_ref, w_ref, o_ref):
        del s_ref
        o_ref[...] = w_ref[...].astype(BF16)

    grid_spec = pltpu.PrefetchScalarGridSpec(
        num_scalar_prefetch=1, grid=(steps,),
        in_specs=[pl.BlockSpec((tr, cols), lambda i, s_ref: (i, 0))],
        out_specs=pl.BlockSpec((tr, cols), out_map))
    return pl.pallas_call(
        body, name=name, grid_spec=grid_spec,
        out_shape=jax.ShapeDtypeStruct(out_shape, BF16),
        compiler_params=_params(("parallel",)),
    )(shard_arr, w)


def _rms_fwd(x, w, name):
    rows, d = x.shape
    tr = 256

    def body(x_ref, w_ref, h_ref):
        xv = x_ref[...]
        r = lax.rsqrt(jnp.mean(xv * xv, axis=-1, keepdims=True) + NORM_EPS)
        h_ref[...] = (xv * r * w_ref[...]).astype(BF16)

    return pl.pallas_call(
        body, name=name, grid=(rows // tr,),
        in_specs=[pl.BlockSpec((tr, d), lambda i: (i, 0)), pl.BlockSpec((1, d), lambda i: (0, 0))],
        out_specs=pl.BlockSpec((tr, d), lambda i: (i, 0)),
        out_shape=jax.ShapeDtypeStruct((rows, d), BF16),
        compiler_params=_params(("parallel",)),
    )(x, w)


def _rms_bwd(x, w, dh, dres, name, after=()):
    rows, d = x.shape
    tr = 256
    after = tuple(after)

    def body(x_ref, w_ref, dh_ref, dres_ref, *rest):
        dx_ref, dxb_ref, dw_ref = rest[len(after):]
        xv = x_ref[...]
        r = lax.rsqrt(jnp.mean(xv * xv, axis=-1, keepdims=True) + NORM_EPS)
        xhat = xv * r
        dy = dh_ref[...]
        dxhat = dy * w_ref[...]
        dx = dres_ref[...] + r * (dxhat - xhat * jnp.mean(dxhat * xhat, axis=-1, keepdims=True))
        dx_ref[...] = dx
        dxb_ref[...] = dx.astype(BF16)
        part = jnp.sum(dy * xhat, axis=0, keepdims=True)

        @pl.when(pl.program_id(0) == 0)
        def _():
            dw_ref[...] = part

        @pl.when(pl.program_id(0) != 0)
        def _():
            dw_ref[...] += part

    row = pl.BlockSpec((tr, d), lambda i: (i, 0))
    vec = pl.BlockSpec((1, d), lambda i: (0, 0))
    return pl.pallas_call(
        body, name=name, grid=(rows // tr,),
        in_specs=[row, vec, row, row] + [ANY] * len(after),
        out_specs=[row, row, vec],
        out_shape=[jax.ShapeDtypeStruct((rows, d), F32), jax.ShapeDtypeStruct((rows, d), BF16),
                   jax.ShapeDtypeStruct((1, d), F32)],
        compiler_params=_params(("arbitrary",)),
    )(x, w, dh, dres, *after)


def _final_norm_loss(x2, w, target, name):
    rows, d = x2.shape
    tr = 256

    def body(x_ref, w_ref, t_ref, loss_ref, dx_ref, dxb_ref, dw_ref):
        xv = x_ref[...]
        wv = w_ref[...]
        r = lax.rsqrt(jnp.mean(xv * xv, axis=-1, keepdims=True) + NORM_EPS)
        xhat = xv * r
        err = xhat * wv - t_ref[...]
        part_loss = 0.5 * jnp.sum(jnp.mean(err * err, axis=-1, keepdims=True), axis=0, keepdims=True)
        dy = err * (1.0 / d)
        dxhat = dy * wv
        dx = r * (dxhat - xhat * jnp.mean(dxhat * xhat, axis=-1, keepdims=True))
        dx_ref[...] = dx
        dxb_ref[...] = dx.astype(BF16)
        part_dw = jnp.sum(dy * xhat, axis=0, keepdims=True)
        part_loss = jnp.broadcast_to(part_loss, (1, 128))

        @pl.when(pl.program_id(0) == 0)
        def _():
            dw_ref[...] = part_dw
            loss_ref[...] = part_loss

        @pl.when(pl.program_id(0) != 0)
        def _():
            dw_ref[...] += part_dw
            loss_ref[...] += part_loss

    row = pl.BlockSpec((tr, d), lambda i: (i, 0))
    vec = pl.BlockSpec((1, d), lambda i: (0, 0))
    return pl.pallas_call(
        body, name=name, grid=(rows // tr,),
        in_specs=[row, vec, row],
        out_specs=[pl.BlockSpec((1, 128), lambda i: (0, 0)), row, row, vec],
        out_shape=[jax.ShapeDtypeStruct((1, 128), F32), jax.ShapeDtypeStruct((rows, d), F32),
                   jax.ShapeDtypeStruct((rows, d), BF16), jax.ShapeDtypeStruct((1, d), F32)],
        compiler_params=_params(("arbitrary",)),
    )(x2, w, target)


def _adamw_math(w, g, m, v):
    m = ADAM_B1 * m + (1.0 - ADAM_B1) * g
    v = ADAM_B2 * v + (1.0 - ADAM_B2) * (g * g)
    m_hat = m / (1.0 - ADAM_B1 ** ADAM_STEP)
    v_hat = v / (1.0 - ADAM_B2 ** ADAM_STEP)
    delta = -ADAM_LR * (m_hat / (jnp.sqrt(v_hat) + ADAM_EPS) + ADAM_WD * w)
    return delta, m, v


def _adamw(w, g, m, v, name):
    rows, cols = w.shape
    tr = _row_tile(rows, cols * 4, limit=1024 * 1024)

    def body(w_ref, g_ref, m_ref, v_ref, d_ref, mo_ref, vo_ref):
        delta, m_new, v_new = _adamw_math(w_ref[...], g_ref[...], m_ref[...], v_ref[...])
        d_ref[...] = delta
        mo_ref[...] = m_new
        vo_ref[...] = v_new

    blk = pl.BlockSpec((tr, cols), lambda i: (i, 0))
    shp = jax.ShapeDtypeStruct((rows, cols), F32)
    return pl.pallas_call(
        body, name=name, grid=(rows // tr,),
        in_specs=[blk] * 4, out_specs=[blk] * 3, out_shape=[shp] * 3,
        compiler_params=_params(("parallel",)),
    )(w, g, m, v)


_DOT_DIMS = {"nn": ((1,), (0,)), "nt": ((1,), (1,)), "tn": ((0,), (0,))}


def _matmul(name, mode, a_list, b_list, acc_of, m, n, k, tm, tn, tk, extras, out_dtypes, epilogue,
            a_koff=None, b_koff=None, after=()):
    after = tuple(after)
    assert m % tm == 0 and n % tn == 0 and k % tk == 0, (name, m, n, k, tm, tn, tk)
    nk = k // tk
    n_acc = max(acc_of) + 1
    n_pairs = len(a_list)
    a_koff = a_koff or [0] * n_pairs
    b_koff = b_koff or [0] * n_pairs
    dims = (_DOT_DIMS[mode], ((), ()))
    n_ext, n_out = len(extras), len(out_dtypes)

    def body(*refs):
        a_refs = refs[:n_pairs]
        b_refs = refs[n_pairs:2 * n_pairs]
        e_refs = refs[2 * n_pairs:2 * n_pairs + n_ext]
        first_out = 2 * n_pairs + n_ext + len(after)
        o_refs = refs[first_out:first_out + n_out]
        acc_refs = refs[first_out + n_out:]
        parts = [None] * n_acc
        for p in range(n_pairs):
            d = lax.dot_general(a_refs[p][...], b_refs[p][...], dims, preferred_element_type=F32)
            parts[acc_of[p]] = d if parts[acc_of[p]] is None else parts[acc_of[p]] + d

        def finish(accs):
            outs = epilogue(accs, [e[...] for e in e_refs])
            for o_ref, o in zip(o_refs, outs):
                o_ref[...] = o.astype(o_ref.dtype)

        if nk == 1:
            finish(parts)
        else:
            kk = pl.program_id(2)

            @pl.when(kk == 0)
            def _():
                for acc_ref, part in zip(acc_refs, parts):
                    acc_ref[...] = part

            @pl.when(kk != 0)
            def _():
                for acc_ref, part in zip(acc_refs, parts):
                    acc_ref[...] += part

            @pl.when(kk == nk - 1)
            def _():
                finish([acc_ref[...] for acc_ref in acc_refs])

    def a_spec(off):
        if mode == "tn":
            return pl.BlockSpec((tk, tm), lambda i, j, kk: (kk + off, i))
        return pl.BlockSpec((tm, tk), lambda i, j, kk: (i, kk + off))

    def b_spec(off):
        if mode == "nt":
            return pl.BlockSpec((tn, tk), lambda i, j, kk: (j, kk + off))
        return pl.BlockSpec((tk, tn), lambda i, j, kk: (kk + off, j))

    tile = pl.BlockSpec((tm, tn), lambda i, j, kk: (i, j))
    scratch = [pltpu.VMEM((tm, tn), F32) for _ in range(n_acc)] if nk > 1 else []
    return pl.pallas_call(
        body, name=name, grid=(m // tm, n // tn, nk),
        in_specs=[a_spec(o) for o in a_koff] + [b_spec(o) for o in b_koff] + [tile] * n_ext + [ANY] * len(after),
        out_specs=[tile] * n_out,
        out_shape=[jax.ShapeDtypeStruct((m, n), dt) for dt in out_dtypes],
        scratch_shapes=scratch,
        compiler_params=_params(("parallel", "parallel", "arbitrary")),
    )(*a_list, *b_list, *extras, *after)


def _epi_plain(accs, extras):
    return (accs[0],)


def _epi_residual(accs, extras):
    return (accs[0] + extras[0],)


def _epi_two(accs, extras):
    return accs[0], accs[1]


def _epi_swiglu(accs, extras):
    g, u = accs
    return g, u, g * _sigmoid(g) * u


def _epi_swiglu_bwd(accs, extras):
    da = accs[0]
    g, u = extras
    sg = _sigmoid(g)
    dg = da * u * sg * (1.0 + g * (1.0 - sg))
    du = da * g * sg
    return dg, du


_NT_DIMS = (((1,), (1,)), ((), ()))
_TN_DIMS = (((0,), (0,)), ((), ()))


def _tile_delta(tq, tk):
    return lax.broadcasted_iota(I32, (tq, tk), 0) - lax.broadcasted_iota(I32, (tq, tk), 1)


def _attn_mask_bias(delta, slope):
    count = jnp.zeros(delta.shape, I32)
    for window, dilation in DILATED_PATTERNS:
        hit = ((delta & (dilation - 1)) == 0) & (delta <= window)
        count = count + jnp.where(hit, 1, 0)
    valid = (delta >= 0) & (count > 0)
    logm = jnp.where(count == 3, math.log(3.0), jnp.where(count == 2, math.log(2.0), 0.0))
    return valid, logm - slope * delta.astype(F32)


def _fill_attn_bias(tab_ref, slope):
    nb, t, _ = tab_ref.shape
    base = _tile_delta(t, t)
    for b in range(nb):
        valid, bias = _attn_mask_bias(base + b * t, slope)
        tab_ref[b] = jnp.where(valid, bias, NEG_BIG)


def _fill_ret_decay(tab_ref, log_gamma):
    nb, t, _ = tab_ref.shape
    base = _tile_delta(t, t)
    for b in range(nb):
        tab_ref[b] = _ret_decay(base + b * t, log_gamma)


def _alibi_slopes():
    return [2.0 ** (-8.0 * (h + 1) / ATTN_HEADS) for h in range(ATTN_HEADS)]


def _attn_fwd(proj):
    s = proj.shape[0]
    t = SEQ_TILE
    hd = ATTN_HEAD_DIM
    nh = ATTN_HEADS
    scale = 1.0 / math.sqrt(hd)
    slopes = _alibi_slopes()

    def body(q_ref, k_ref, v_ref, mix_ref, o_ref, lse_ref, kb, vb, bias_tab):
        h = pl.program_id(0)
        i = pl.program_id(1)

        @pl.when(i == 0)
        def _():
            kb[...] = k_ref[...].astype(BF16)
            vb[...] = v_ref[...].astype(BF16)
            _fill_attn_bias(bias_tab, _select_by_index(h, slopes))

        q = q_ref[...].astype(BF16)

        def step(j, carry):
            m_i, l_i, acc = carry
            rows = pl.ds(pl.multiple_of(j * t, t), t)
            kj = kb[rows, :]
            vj = vb[rows, :]
            sc = lax.dot_general(q, kj, _NT_DIMS, preferred_element_type=F32) * scale + bias_tab[i - j]
            m_new = jnp.maximum(m_i, jnp.max(sc, axis=-1, keepdims=True))
            p = jnp.exp(sc - m_new)
            alpha = jnp.exp(m_i - m_new)
            l_new = alpha * l_i + jnp.sum(p, axis=-1, keepdims=True)
            acc = alpha * acc + jnp.dot(p.astype(BF16), vj, preferred_element_type=F32)
            return m_new, l_new, acc

        init = (jnp.full((t, 1), NEG_BIG, F32), jnp.zeros((t, 1), F32), jnp.zeros((t, hd), F32))
        m_i, l_i, acc = lax.fori_loop(0, i + 1, step, init)
        out = acc / l_i
        o_ref[...] = out
        mix_ref[...] = out.astype(BF16)
        lse_ref[...] = jnp.broadcast_to(m_i + jnp.log(l_i), (t, hd))

    return pl.pallas_call(
        body, name="attn_fwd", grid=(nh, s // t),
        in_specs=[pl.BlockSpec((t, hd), lambda h, i: (i, h)),
                  pl.BlockSpec((s, hd), lambda h, i: (0, nh + h)),
                  pl.BlockSpec((s, hd), lambda h, i: (0, 2 * nh + h))],
        out_specs=[pl.BlockSpec((t, hd), lambda h, i: (i, h))] * 3,
        out_shape=[jax.ShapeDtypeStruct((s, ATTN_WIDTH), BF16),
                   jax.ShapeDtypeStruct((s, ATTN_WIDTH), F32),
                   jax.ShapeDtypeStruct((s, ATTN_WIDTH), F32)],
        scratch_shapes=[pltpu.VMEM((s, hd), BF16), pltpu.VMEM((s, hd), BF16), pltpu.VMEM((s // t, t, t), F32)],
        compiler_params=_params(("arbitrary", "arbitrary")),
    )(proj, proj, proj)


def _attn_bwd(proj, attn_out, lse, dmixed, after=()):
    after = tuple(after)
    s = proj.shape[0]
    t = SEQ_TILE
    nt = s // t
    hd = ATTN_HEAD_DIM
    nh = ATTN_HEADS
    scale = 1.0 / math.sqrt(hd)
    slopes = _alibi_slopes()

    def body(q_ref, k_ref, v_ref, o_ref, lse_ref, do_ref, *rest):
        dq_ref, dk_ref, dv_ref, qb, kb, vb, dob, dsum, dq_acc, bias_tab = rest[len(after):]
        h = pl.program_id(0)
        _fill_attn_bias(bias_tab, _select_by_index(h, slopes))
        qb[...] = q_ref[...].astype(BF16)
        kb[...] = k_ref[...].astype(BF16)
        vb[...] = v_ref[...].astype(BF16)
        do = do_ref[...]
        dob[...] = do.astype(BF16)
        dsum[...] = jnp.broadcast_to(jnp.sum(do * o_ref[...], axis=-1, keepdims=True), (s, hd))
        dq_acc[...] = jnp.zeros((s, hd), F32)

        def over_keys(j, _):
            krows = pl.ds(pl.multiple_of(j * t, t), t)
            kj = kb[krows, :]
            vj = vb[krows, :]

            def over_queries(i, carry):
                dk, dv = carry
                qrows = pl.ds(pl.multiple_of(i * t, t), t)
                qi = qb[qrows, :]
                doi = dob[qrows, :]
                lse_i = lse_ref[qrows, :][:, :1]
                dsum_i = dsum[qrows, :][:, :1]
                sc = lax.dot_general(qi, kj, _NT_DIMS, preferred_element_type=F32) * scale + bias_tab[i - j]
                p = jnp.exp(sc - lse_i)
                dp = lax.dot_general(doi, vj, _NT_DIMS, preferred_element_type=F32)
                ds = (p * (dp - dsum_i)).astype(BF16)
                dv = dv + lax.dot_general(p.astype(BF16), doi, _TN_DIMS, preferred_element_type=F32)
                dk = dk + lax.dot_general(ds, qi, _TN_DIMS, preferred_element_type=F32)
                dq_acc[qrows, :] += jnp.dot(ds, kj, preferred_element_type=F32)
                return dk, dv

            zero = jnp.zeros((t, hd), F32)
            dk, dv = lax.fori_loop(j, nt, over_queries, (zero, zero))
            dk_ref[krows, :] = (dk * scale).astype(BF16)
            dv_ref[krows, :] = dv.astype(BF16)
            return 0

        lax.fori_loop(0, nt, over_keys, 0)
        dq_ref[...] = (dq_acc[...] * scale).astype(BF16)

    def col(off):
        return pl.BlockSpec((s, hd), lambda h: (0, off + h))

    return pl.pallas_call(
        body, name="attn_bwd", grid=(nh,),
        in_specs=[col(0), col(nh), col(2 * nh), col(0), col(0), col(0)] + [ANY] * len(after),
        out_specs=[col(0)] * 3,
        out_shape=[jax.ShapeDtypeStruct((s, ATTN_WIDTH), BF16)] * 3,
        scratch_shapes=[pltpu.VMEM((s, hd), BF16)] * 4 + [pltpu.VMEM((s, hd), F32)] * 2
        + [pltpu.VMEM((nt, t, t), F32)],
        compiler_params=_params(("arbitrary",)),
    )(proj, proj, proj, attn_out, lse, dmixed, *after)


def _ret_log_gammas():
    return [math.log(1.0 - 2.0 ** (-5.0 - h)) for h in range(RET_HEADS)]


def _ret_decay(delta, log_gamma):
    dec = jnp.exp(delta.astype(F32) * log_gamma) * (1.0 / math.sqrt(RET_HEAD_DIM))
    return jnp.where(delta >= 0, dec, 0.0)


def _ret_fwd(proj):
    s = proj.shape[0]
    t = SEQ_TILE
    hd = RET_HEAD_DIM
    nh = RET_HEADS
    log_gammas = _ret_log_gammas()
    c0 = 3 * ATTN_WIDTH // hd

    def body(q_ref, k_ref, v_ref, g_ref, mix_ref, raw_ref, kb, vb, decay_tab):
        h = pl.program_id(0)
        i = pl.program_id(1)

        @pl.when(i == 0)
        def _():
            kb[...] = k_ref[...].astype(BF16)
            vb[...] = v_ref[...].astype(BF16)
            _fill_ret_decay(decay_tab, _select_by_index(h, log_gammas))

        q = q_ref[...].astype(BF16)

        def step(j, acc):
            rows = pl.ds(pl.multiple_of(j * t, t), t)
            sc = lax.dot_general(q, kb[rows, :], _NT_DIMS, preferred_element_type=F32) * decay_tab[i - j]
            return acc + jnp.dot(sc.astype(BF16), vb[rows, :], preferred_element_type=F32)

        ret = lax.fori_loop(0, i + 1, step, jnp.zeros((t, hd), F32))
        raw_ref[...] = ret
        r = lax.rsqrt(jnp.mean(ret * ret, axis=-1, keepdims=True) + NORM_EPS)
        g = g_ref[...]
        mix_ref[...] = (g * _sigmoid(g) * (ret * r)).astype(BF16)

    return pl.pallas_call(
        body, name="ret_fwd", grid=(nh, s // t),
        in_specs=[pl.BlockSpec((t, hd), lambda h, i: (i, c0 + h)),
                  pl.BlockSpec((s, hd), lambda h, i: (0, c0 + nh + h)),
                  pl.BlockSpec((s, hd), lambda h, i: (0, c0 + 2 * nh + h)),
                  pl.BlockSpec((t, hd), lambda h, i: (i, c0 + 3 * nh + h))],
        out_specs=[pl.BlockSpec((t, hd), lambda h, i: (i, h))] * 2,
        out_shape=[jax.ShapeDtypeStruct((s, RET_WIDTH), BF16), jax.ShapeDtypeStruct((s, RET_WIDTH), F32)],
        scratch_shapes=[pltpu.VMEM((s, hd), BF16), pltpu.VMEM((s, hd), BF16), pltpu.VMEM((s // t, t, t), F32)],
        compiler_params=_params(("arbitrary", "arbitrary")),
    )(proj, proj, proj, proj)


def _ret_bwd(proj, ret_raw, dmixed, after=()):
    after = tuple(after)
    s = proj.shape[0]
    t = SEQ_TILE
    nt = s // t
    hd = RET_HEAD_DIM
    nh = RET_HEADS
    log_gammas = _ret_log_gammas()
    c0 = 3 * ATTN_WIDTH // hd
    mixed_blocks = ATTN_WIDTH // hd

    def body(q_ref, k_ref, v_ref, g_ref, raw_ref, dmix_ref, *rest):
        dq_ref, dk_ref, dv_ref, dg_ref, qb, kb, vb, dretb, dq_acc, decay_tab = rest[len(after):]
        h = pl.program_id(0)
        _fill_ret_decay(decay_tab, _select_by_index(h, log_gammas))
        qb[...] = q_ref[...].astype(BF16)
        kb[...] = k_ref[...].astype(BF16)
        vb[...] = v_ref[...].astype(BF16)
        ret = raw_ref[...]
        r = lax.rsqrt(jnp.mean(ret * ret, axis=-1, keepdims=True) + NORM_EPS)
        normed = ret * r
        g = g_ref[...]
        sg = _sigmoid(g)
        dout = dmix_ref[...]
        dg_ref[...] = (dout * normed * sg * (1.0 + g * (1.0 - sg))).astype(BF16)
        dn = dout * g * sg
        dret = r * (dn - normed * jnp.mean(dn * normed, axis=-1, keepdims=True))
        dretb[...] = dret.astype(BF16)
        dq_acc[...] = jnp.zeros((s, hd), F32)

        def over_keys(j, _):
            krows = pl.ds(pl.multiple_of(j * t, t), t)
            kj = kb[krows, :]
            vj = vb[krows, :]

            def over_queries(i, carry):
                dk, dv = carry
                qrows = pl.ds(pl.multiple_of(i * t, t), t)
                qi = qb[qrows, :]
                doi = dretb[qrows, :]
                dec = decay_tab[i - j]
                a = (lax.dot_general(qi, kj, _NT_DIMS, preferred_element_type=F32) * dec).astype(BF16)
                da = (lax.dot_general(doi, vj, _NT_DIMS, preferred_element_type=F32) * dec).astype(BF16)
                dv = dv + lax.dot_general(a, doi, _TN_DIMS, preferred_element_type=F32)
                dk = dk + lax.dot_general(da, qi, _TN_DIMS, preferred_element_type=F32)
                dq_acc[qrows, :] += jnp.dot(da, kj, preferred_element_type=F32)
                return dk, dv

            zero = jnp.zeros((t, hd), F32)
            dk, dv = lax.fori_loop(j, nt, over_queries, (zero, zero))
            dk_ref[krows, :] = dk.astype(BF16)
            dv_ref[krows, :] = dv.astype(BF16)
            return 0

        lax.fori_loop(0, nt, over_keys, 0)
        dq_ref[...] = dq_acc[...].astype(BF16)

    def col(off):
        return pl.BlockSpec((s, hd), lambda h: (0, off + h))

    return pl.pallas_call(
        body, name="ret_bwd", grid=(nh,),
        in_specs=[col(c0), col(c0 + nh), col(c0 + 2 * nh), col(c0 + 3 * nh), col(0), col(mixed_blocks)]
        + [ANY] * len(after),
        out_specs=[col(0)] * 4,
        out_shape=[jax.ShapeDtypeStruct((s, RET_WIDTH), BF16)] * 4,
        scratch_shapes=[pltpu.VMEM((s, hd), BF16)] * 4 + [pltpu.VMEM((s, hd), F32)]
        + [pltpu.VMEM((nt, t, t), F32)],
        compiler_params=_params(("arbitrary",)),
    )(proj, proj, proj, proj, ret_raw, dmixed, *after)


_FLIPS = (2, 1, 3)


def _other_chips(x, y):
    return [(1 - x, y), (x, 1 - y), (1 - x, 1 - y)]


_HBM = pl.BlockSpec(memory_space=pltpu.HBM)
_SEM = pl.BlockSpec(memory_space=pltpu.SEMAPHORE)
_EFFECT = pltpu.SideEffectType.DATAFLOW_SIDE_EFFECTING


def _in_hbm(a):
    return pltpu.with_memory_space_constraint(a, pltpu.HBM)


def _weight_view(w, column_sharded):
    if column_sharded:
        return w.reshape(2, w.shape[0] // 2, w.shape[1])
    return w.reshape(N_CHIPS, 2, w.shape[0] // (2 * N_CHIPS), w.shape[1])


def _weight_unview(v):
    if v.ndim == 3:
        return v.reshape(2 * v.shape[1], v.shape[2])
    return v.reshape(N_CHIPS * 2 * v.shape[2], v.shape[3])


def _weight_region(buf, shard, half):
    if len(buf.shape) == 3:
        cols = buf.shape[2] // N_CHIPS
        return buf.at[half, :, pl.ds(shard * cols, cols)]
    return buf.at[shard, half]


def _remote(where, send_sem, recv_sem, to):
    return pltpu.make_async_remote_copy(src_ref=where, dst_ref=where, send_sem=send_sem, recv_sem=recv_sem,
                                        device_id=to, device_id_type=MESH)


def _for_my_shard(fn):
    x, y, _ = _place()
    for ss in range(N_CHIPS):
        pl.when(2 * x + y == ss)(functools.partial(fn, ss))


def _gather_start(views):
    n_w = len(views)

    def body(*refs):
        send_sems, recv_sems = refs[n_w:n_w + 2]
        bufs = refs[n_w + 2:]
        x, y, c = _place()

        def start(ss):
            for w in range(n_w):
                for j, chip in enumerate(_other_chips(x, y)):
                    _remote(_weight_region(bufs[w], ss, c), send_sems.at[3 * w + j], recv_sems.at[3 * w + j],
                            (*chip, c)).start()

        _for_my_shard(start)

    return pl.pallas_call(
        body, name="gather_start",
        in_specs=[_HBM] * n_w, out_specs=[_SEM, _SEM] + [_HBM] * n_w,
        out_shape=[pltpu.SemaphoreType.DMA((3 * n_w,)), pltpu.SemaphoreType.DMA((3 * n_w,))]
        + [pltpu.HBM(v.shape, BF16) for v in views],
        input_output_aliases={w: 2 + w for w in range(n_w)},
        compiler_params=pltpu.CompilerParams(has_side_effects=_EFFECT),
    )(*[_in_hbm(v) for v in views])


def _gather_forward(views, which, send_sems, recv_sems, after, name):
    n_w = len(views)

    def body(*refs):
        send_in, recv_in = refs[n_w:n_w + 2]
        fwd_send, fwd_recv = refs[n_w + 3:n_w + 5]
        bufs = refs[n_w + 5:]
        x, y, c = _place()
        sibling = (x, y, 1 - c)

        def forward(ss):
            for i, w in enumerate(which):
                for j in range(3):
                    landed = _weight_region(bufs[i], ss ^ _FLIPS[j], c)
                    _remote(landed, send_in.at[3 * w + j], recv_in.at[3 * w + j], sibling).wait_recv()
                    _remote(landed, fwd_send.at[3 * i + j], fwd_recv.at[3 * i + j], sibling).start()

        _for_my_shard(forward)
        for i, w in enumerate(which):
            for j in range(3):
                _remote(_weight_region(bufs[i], 0, 0), send_in.at[3 * w + j], recv_in.at[3 * w + j],
                        sibling).wait_send()

    return pl.pallas_call(
        body, name=name,
        in_specs=[_HBM] * n_w + [_SEM, _SEM, ANY], out_specs=[_SEM, _SEM] + [_HBM] * n_w,
        out_shape=[pltpu.SemaphoreType.DMA((3 * n_w,)), pltpu.SemaphoreType.DMA((3 * n_w,))]
        + [pltpu.HBM(v.shape, BF16) for v in views],
        input_output_aliases={w: 2 + w for w in range(n_w)},
        compiler_params=pltpu.CompilerParams(has_side_effects=_EFFECT),
    )(*views, send_sems, recv_sems, after)


def _gather_end(views, fwd_send, fwd_recv, after, name):
    n_w = len(views)

    def body(*refs):
        fwd_send_ref, fwd_recv_ref = refs[n_w:n_w + 2]
        bufs = refs[n_w + 3:]
        x, y, c = _place()
        for i in range(n_w):
            for j in range(3):
                cp = _remote(_weight_region(bufs[i], 0, 0), fwd_send_ref.at[3 * i + j], fwd_recv_ref.at[3 * i + j],
                             (x, y, 1 - c))
                cp.wait_recv()
                cp.wait_send()

    outs = pl.pallas_call(
        body, name=name,
        in_specs=[_HBM] * n_w + [_SEM, _SEM, ANY], out_specs=[_HBM] * n_w,
        out_shape=[pltpu.HBM(v.shape, BF16) for v in views],
        input_output_aliases={w: w for w in range(n_w)},
        compiler_params=pltpu.CompilerParams(has_side_effects=_EFFECT),
    )(*views, fwd_send, fwd_recv, after)
    return [_weight_unview(o) for o in outs]


def _split_start(name, bufs, n_sems, copies):
    n = len(bufs)

    def body(*refs):
        send_sems, recv_sems = refs[n:n + 2]
        for cp in copies(refs[n + 2:], send_sems, recv_sems):
            cp.start()

    outs = pl.pallas_call(
        body, name=name,
        in_specs=[_HBM] * n, out_specs=[_SEM, _SEM] + [_HBM] * n,
        out_shape=[pltpu.SemaphoreType.DMA((n_sems,)), pltpu.SemaphoreType.DMA((n_sems,))]
        + [pltpu.HBM(b.shape, b.dtype) for b in bufs],
        input_output_aliases={i: 2 + i for i in range(n)},
        compiler_params=pltpu.CompilerParams(has_side_effects=_EFFECT),
    )(*[_in_hbm(b) for b in bufs])
    return outs[0], outs[1], list(outs[2:])


def _split_wait(name, bufs, send_sems, recv_sems, copies, after):
    n = len(bufs)

    def body(*refs):
        send_ref, recv_ref = refs[n:n + 2]
        for cp in copies(refs[n + 3:], send_ref, recv_ref):
            cp.wait()

    return list(pl.pallas_call(
        body, name=name,
        in_specs=[_HBM] * n + [_SEM, _SEM, ANY], out_specs=[_HBM] * n,
        out_shape=[pltpu.HBM(b.shape, b.dtype) for b in bufs],
        input_output_aliases={i: i for i in range(n)},
        compiler_params=pltpu.CompilerParams(has_side_effects=_EFFECT),
    )(*bufs, send_sems, recv_sems, after))


def _halves_copies(n_w):
    def copies(bufs, send_sems, recv_sems):
        x, y, c = _place()
        out = []
        for w in range(n_w):
            view, land = bufs[w], bufs[n_w + w]
            src = view.at[1 - c] if len(view.shape) == 3 else view.at[:, 1 - c]
            out.append(pltpu.make_async_remote_copy(
                src_ref=src, dst_ref=land, send_sem=send_sems.at[w], recv_sem=recv_sems.at[w],
                device_id=(x, y, 1 - c), device_id_type=MESH))
        return out
    return copies


def _pieces_copies(n_w):
    def copies(bufs, send_sems, recv_sems):
        x, y, c = _place()
        out = []
        for w in range(n_w):
            for j, (cx, cy) in enumerate(_other_chips(x, y)):
                out.append(pltpu.make_async_remote_copy(
                    src_ref=bufs[w].at[2 * cx + cy], dst_ref=bufs[n_w + w].at[j],
                    send_sem=send_sems.at[3 * w + j], recv_sem=recv_sems.at[3 * w + j],
                    device_id=(cx, cy, c), device_id_type=MESH))
        return out
    return copies


def _join_copies(n_w):
    def copies(bufs, send_sems, recv_sems):
        x, y, c = _place()
        return [pltpu.make_async_remote_copy(
            src_ref=bufs[w].at[c], dst_ref=bufs[w].at[c], send_sem=send_sems.at[w], recv_sem=recv_sems.at[w],
            device_id=(x, y, 1 - c), device_id_type=MESH) for w in range(n_w)]
    return copies


def _grad_view(g, column_sharded):
    return _weight_view(g, column_sharded)


def _halves_landing(view):
    shape = view.shape[1:] if view.ndim == 3 else (N_CHIPS,) + view.shape[2:]
    return lax.empty(shape, BF16)


def _halves_start(tag, grads, column_sharded):
    views = [_weight_view(g, cs) for g, cs in zip(grads, column_sharded)]
    n = len(views)
    return _split_start("halves_start_" + tag, views + [_halves_landing(v) for v in views], n, _halves_copies(n))


def _halves_wait(tag, state, after):
    send_sems, recv_sems, bufs = state
    n = len(bufs) // 2
    bufs = _split_wait("halves_wait_" + tag, bufs, send_sems, recv_sems, _halves_copies(n), after)
    return bufs[:n], bufs[n:]


def _pieces_start(tag, pieces):
    n = len(pieces)
    landing = [lax.empty((3,) + p.shape[1:], BF16) for p in pieces]
    return _split_start("pieces_start_" + tag, list(pieces) + landing, 3 * n, _pieces_copies(n))


def _pieces_wait(tag, state, after):
    send_sems, recv_sems, bufs = state
    n = len(bufs) // 2
    bufs = _split_wait("pieces_wait_" + tag, bufs, send_sems, recv_sems, _pieces_copies(n), after)
    return bufs[:n], bufs[n:]


def _join_start(tag, shards):
    n = len(shards)
    return _split_start("join_start_" + tag, list(shards), n, _join_copies(n))


def _join_wait(tag, state, after):
    send_sems, recv_sems, bufs = state
    bufs = _split_wait("join_wait_" + tag, bufs, send_sems, recv_sems, _join_copies(len(bufs)), after)
    return [b.reshape(2 * b.shape[1], b.shape[2]) for b in bufs]


def _chip_sum_col(g3, sib, c_arr, name):
    _, hk, n = g3.shape
    cols = n // N_CHIPS
    tr = _row_tile(hk, cols * 2, limit=1024 * 1024)

    def body(c_ref, g_ref, s_ref, o_ref):
        del c_ref
        o_ref[...] = (g_ref[...].astype(F32) + s_ref[...].astype(F32)).astype(BF16)

    grid_spec = pltpu.PrefetchScalarGridSpec(
        num_scalar_prefetch=1, grid=(N_CHIPS, hk // tr),
        in_specs=[pl.BlockSpec((None, tr, cols), lambda p, r, c_ref: (c_ref[0], r, p)),
                  pl.BlockSpec((tr, cols), lambda p, r, c_ref: (r, p))],
        out_specs=pl.BlockSpec((None, tr, cols), lambda p, r, c_ref: (p, r, 0)))
    return pl.pallas_call(
        body, name=name, grid_spec=grid_spec,
        out_shape=jax.ShapeDtypeStruct((N_CHIPS, hk, cols), BF16),
        compiler_params=_params(("parallel", "parallel")),
    )(c_arr, g3, sib)


def _chip_sum_row(g4, sib, c_arr, name):
    _, _, hr, n = g4.shape
    tr = _row_tile(hr, n * 2, limit=1024 * 1024)

    def body(c_ref, g_ref, s_ref, o_ref):
        del c_ref
        o_ref[...] = (g_ref[...].astype(F32) + s_ref[...].astype(F32)).astype(BF16)

    grid_spec = pltpu.PrefetchScalarGridSpec(
        num_scalar_prefetch=1, grid=(N_CHIPS, hr // tr),
        in_specs=[pl.BlockSpec((None, None, tr, n), lambda p, r, c_ref: (p, c_ref[0], r, 0)),
                  pl.BlockSpec((None, tr, n), lambda p, r, c_ref: (p, r, 0))],
        out_specs=pl.BlockSpec((None, tr, n), lambda p, r, c_ref: (p, r, 0)))
    return pl.pallas_call(
        body, name=name, grid_spec=grid_spec,
        out_shape=jax.ShapeDtypeStruct((N_CHIPS, hr, n), BF16),
        compiler_params=_params(("parallel", "parallel")),
    )(c_arr, g4, sib)


def _sum_pieces(pieces, received, place_arr, name):
    _, r, n = pieces.shape
    tr = _row_tile(r, n * 4, limit=1024 * 1024)

    def body(p_ref, own_ref, r0_ref, r1_ref, r2_ref, o_ref):
        del p_ref
        acc = own_ref[...].astype(F32) + r0_ref[...].astype(F32)
        acc = acc + r1_ref[...].astype(F32)
        o_ref[...] = acc + r2_ref[...].astype(F32)

    def recv_spec(j):
        return pl.BlockSpec((None, tr, n), lambda i, p_ref: (j, i, 0))

    grid_spec = pltpu.PrefetchScalarGridSpec(
        num_scalar_prefetch=1, grid=(r // tr,),
        in_specs=[pl.BlockSpec((None, tr, n), lambda i, p_ref: (p_ref[0], i, 0)),
                  recv_spec(0), recv_spec(1), recv_spec(2)],
        out_specs=pl.BlockSpec((None, tr, n), lambda i, p_ref: (p_ref[1], i, 0)))
    return pl.pallas_call(
        body, name=name, grid_spec=grid_spec,
        out_shape=jax.ShapeDtypeStruct((2, r, n), F32),
        compiler_params=_params(("parallel",)),
    )(place_arr, pieces, received, received, received)


def _norm_weights_step(parts, w, m, v, after=()):
    rows, d = parts.shape
    after = tuple(after)

    def body(p_ref, w_ref, m_ref, v_ref, *rest):
        g_ref, d_ref, mo_ref, vo_ref, gathered, send_sems, recv_sems = rest[len(after):]
        x, y, c = _place()
        me = 4 * x + 2 * y + c
        gathered[me] = p_ref[...]
        copies = []
        for k in range(1, N_DEV):
            peer = (x ^ ((k >> 2) & 1), y ^ ((k >> 1) & 1), c ^ (k & 1))
            copies.append(pltpu.make_async_remote_copy(
                src_ref=p_ref, dst_ref=gathered.at[me], send_sem=send_sems.at[k - 1],
                recv_sem=recv_sems.at[k - 1], device_id=peer, device_id_type=MESH))
        for cp in copies:
            cp.start()
        for cp in copies:
            cp.wait()
        g = gathered[0]
        for k in range(1, N_DEV):
            g = g + gathered[k]
        delta, m_new, v_new = _adamw_math(w_ref[...], g, m_ref[...], v_ref[...])
        g_ref[...] = g
        d_ref[...] = delta
        mo_ref[...] = m_new
        vo_ref[...] = v_new

    vmem = pl.BlockSpec(memory_space=pltpu.VMEM)
    shp = jax.ShapeDtypeStruct((rows, d), F32)
    return pl.pallas_call(
        body, name="norm_weights_step",
        in_specs=[vmem] * 4 + [ANY] * len(after), out_specs=[vmem] * 4, out_shape=[shp] * 4,
        scratch_shapes=[pltpu.VMEM((N_DEV, rows, d), F32), pltpu.SemaphoreType.DMA((N_DEV - 1,)),
                        pltpu.SemaphoreType.DMA((N_DEV - 1,))],
        compiler_params=pltpu.CompilerParams(has_side_effects=True),
    )(parts, w, m, v, *after)


def kernel(x, norm_mix_w, w_in, w_out, norm_ffn_w, w_gate, w_up, w_down, norm_final_w, loss_target, m_norm_mix_w, m_w_in, m_w_out, m_norm_ffn_w, m_w_gate, m_w_up, m_w_down, m_norm_final_w, v_norm_mix_w, v_w_in, v_w_out, v_norm_ffn_w, v_w_gate, v_w_up, v_w_down, v_norm_final_w):
    s, d = x.shape[1], x.shape[2]
    xs = x.reshape(s, d)
    target = loss_target.reshape(s, d)
    big = {"w_in": (w_in, m_w_in, v_w_in), "w_out": (w_out, m_w_out, v_w_out),
           "w_gate": (w_gate, m_w_gate, v_w_gate), "w_up": (w_up, m_w_up, v_w_up),
           "w_down": (w_down, m_w_down, v_w_down)}
    big = {k: tuple(a.reshape(a.shape[1:]) for a in t) for k, t in big.items()}
    col_names, row_names = ("w_in", "w_gate", "w_up"), ("w_out", "w_down")
    n_in = N_CHIPS * big["w_in"][0].shape[1]
    ffn = N_CHIPS * big["w_gate"][0].shape[1]
    mix = ATTN_WIDTH + RET_WIDTH
    c_arr = lax.axis_index("c").astype(I32).reshape(1)
    shard_arr = (2 * lax.axis_index("x") + lax.axis_index("y")).astype(I32).reshape(1)
    place_arr = jnp.concatenate([shard_arr, c_arr])

    order = ("w_in", "w_out", "w_gate", "w_up", "w_down")
    views = [_weight_view(_cast_into_full(big[k][0], shard_arr, k in col_names, "cast_" + k), k in col_names)
             for k in order]
    send_sems, recv_sems, v_in, v_out, v_gate, v_up, v_down = _gather_start(views)

    h1 = _rms_fwd(xs, norm_mix_w, "rms_mix_fwd")
    fs, fr, v_in = _gather_forward([v_in], [0], send_sems, recv_sems, h1, "gather_forward_in")
    wi, = _gather_end([v_in], fs, fr, h1, "gather_end_in")
    proj, = _matmul("in_proj", "nn", [h1], [wi], [0], s, n_in, d, s, 512, d, [], [F32], _epi_plain)
    fs, fr, v_out = _gather_forward([v_out], [1], send_sems, recv_sems, proj, "gather_forward_out")
    attn_b, attn_o, lse = _attn_fwd(proj)
    wo, = _gather_end([v_out], fs, fr, attn_o, "gather_end_out")
    fs, fr, v_gate, v_up = _gather_forward([v_gate, v_up], [2, 3], send_sems, recv_sems, attn_o,
                                           "gather_forward_gate_up")
    ret_b, ret_raw = _ret_fwd(proj)
    mixed = jnp.concatenate([attn_b, ret_b], axis=1)
    x1, = _matmul("out_proj", "nn", [mixed], [wo], [0], s, d, mix, s, 512, mix, [xs], [F32], _epi_residual)
    h2 = _rms_fwd(x1, norm_ffn_w, "rms_ffn_fwd")
    wg, wu = _gather_end([v_gate, v_up], fs, fr, h2, "gather_end_gate_up")
    gate, up, act = _matmul("gate_up", "nn", [h2, h2], [wg, wu], [0, 1], s, ffn, d, s, 256, d, [],
                            [F32, F32, BF16], _epi_swiglu)
    fs, fr, v_down = _gather_forward([v_down], [4], send_sems, recv_sems, act, "gather_forward_down")
    wd, = _gather_end([v_down], fs, fr, act, "gather_end_down")
    x2, = _matmul("down_proj", "nn", [act], [wd], [0], s, d, ffn, s, 512, ffn // 4, [x1], [F32],
                  _epi_residual)
    loss_row, dx2, dx2b, dwf = _final_norm_loss(x2, norm_final_w.reshape(1, d), target, "final_norm_loss")

    names = col_names + row_names
    grads, new = {}, {}

    def chip_sums(tag_names, views, sibs):
        return [(_chip_sum_col if k in col_names else _chip_sum_row)(v, sb, c_arr, "chip_sum_" + k)
                for k, v, sb in zip(tag_names, views, sibs)]

    def piece_sums(tag_names, pieces, received):
        return [_sum_pieces(p, r, place_arr, "sum_pieces_" + k) for k, p, r in zip(tag_names, pieces, received)]

    def update(k):
        new[k] = _adamw(big[k][0], grads[k], big[k][1], big[k][2], "adamw_" + k)

    dgate, dup = _matmul("d_act", "nt", [dx2b], [wd], [0], s, ffn, d, s, 256, d, [gate, up],
                         [BF16, BF16], _epi_swiglu_bwd)
    g_wd, = _matmul("g_w_down", "tn", [act], [dx2b], [0], ffn, d, s, 512, d, s, [], [BF16], _epi_plain)
    halves_d = _halves_start("down", [g_wd], [False])
    dh2, = _matmul("d_h2", "nt", [dgate, dup], [wg, wu], [0, 0], s, d, ffn, s, 1024, 512, [], [F32],
                   _epi_plain, after=halves_d[2][-1:])
    pieces_d = _pieces_start("down", chip_sums(["w_down"], *_halves_wait("down", halves_d, dh2)))
    g_wg, g_wu = _matmul("g_w_gate_up", "tn", [h2, h2], [dgate, dup], [0, 1], d, ffn, s, 1024, 512, s, [],
                         [BF16, BF16], _epi_two, after=pieces_d[2][-1:])
    halves_gu = _halves_start("gate_up", [g_wg, g_wu], [True, True])
    dx1, dx1b, dw_ffn = _rms_bwd(x1, norm_ffn_w, dh2, dx2, "rms_ffn_bwd", after=halves_gu[2][-1:])

    dmixed, = _matmul("d_mixed", "nt", [dx1b], [wo], [0], s, mix, d, s, 512, d, [], [F32], _epi_plain)
    pieces_gu = _pieces_start("gate_up", chip_sums(["w_gate", "w_up"], *_halves_wait("gate_up", halves_gu, dmixed)))
    g_wo, = _matmul("g_w_out", "tn", [mixed], [dx1b], [0], mix, d, s, 512, d, s, [], [BF16], _epi_plain,
                    after=pieces_gu[2][-1:])
    halves_o = _halves_start("out", [g_wo], [False])
    dqa, dka, dva = _attn_bwd(proj, attn_o, lse, dmixed, after=halves_o[2][-1:])
    pieces_o = _pieces_start("out", chip_sums(["w_out"], *_halves_wait("out", halves_o, dqa)))
    dqr, dkr, dvr, dgr = _ret_bwd(proj, ret_raw, dmixed, after=pieces_o[2][-1:])
    dproj = jnp.concatenate([dqa, dka, dva, dqr, dkr, dvr, dgr], axis=1)
    g_wi, = _matmul("g_w_in", "tn", [h1], [dproj], [0], d, n_in, s, 1024, 1024, s, [], [BF16], _epi_plain)
    halves_i = _halves_start("in", [g_wi], [True])
    dh1, = _matmul("d_h1", "nt", [dproj], [wi], [0], s, d, n_in, s, 1024, n_in // 7, [], [F32], _epi_plain,
                   after=halves_i[2][-1:])
    pieces_i = _pieces_start("in", chip_sums(["w_in"], *_halves_wait("in", halves_i, dh1)))
    grad_x, _, dw_mix = _rms_bwd(xs, norm_mix_w, dh1, dx1, "rms_mix_bwd", after=pieces_i[2][-1:])

    def rows8(*vs):
        return jnp.concatenate([v.reshape(1, d) for v in vs] + [jnp.zeros((8 - len(vs), d), F32)], axis=0)

    join_d = _join_start("down", piece_sums(["w_down"], *_pieces_wait("down", pieces_d, grad_x)))
    join_gu = _join_start("gate_up", piece_sums(["w_gate", "w_up"], *_pieces_wait("gate_up", pieces_gu, join_d[2][0])))
    join_o = _join_start("out", piece_sums(["w_out"], *_pieces_wait("out", pieces_o, join_gu[2][0])))
    grads["w_down"], = _join_wait("down", join_d, join_o[2][0])
    update("w_down")
    grads["w_gate"], grads["w_up"] = _join_wait("gate_up", join_gu, new["w_down"][0])
    update("w_gate")
    update("w_up")
    grads["w_out"], = _join_wait("out", join_o, new["w_up"][0])
    update("w_out")
    join_i = _join_start("in", piece_sums(["w_in"], *_pieces_wait("in", pieces_i, new["w_out"][0])))
    ng, nd, nm, nv = _norm_weights_step(
        rows8(dw_mix, dw_ffn, dwf), rows8(norm_mix_w, norm_ffn_w, norm_final_w),
        rows8(m_norm_mix_w, m_norm_ffn_w, m_norm_final_w), rows8(v_norm_mix_w, v_norm_ffn_w, v_norm_final_w),
        after=join_i[2][:1])
    grads["w_in"], = _join_wait("in", join_i, ng)
    update("w_in")

    loss = lax.psum(loss_row[0, 0], ("x", "y", "c"))

    def pack(small, per_weight):
        lead = lambda a: a.reshape((1,) + a.shape)
        return (small[0:1], lead(per_weight["w_in"]), lead(per_weight["w_out"]), small[1:2],
                lead(per_weight["w_gate"]), lead(per_weight["w_up"]), lead(per_weight["w_down"]), small[2])

    return (loss, grad_x.reshape(1, s, d),
            *pack(ng, grads),
            *pack(nd, {k: new[k][0] for k in names}),
            *pack(nm, {k: new[k][1] for k in names}),
            *pack(nv, {k: new[k][2] for k in names}))
```

```python
import functools
import math

import jax
import jax.numpy as jnp
from jax import lax
from jax.experimental import pallas as pl
from jax.experimental.pallas import tpu as pltpu

F32 = jnp.float32
BF16 = jnp.bfloat16
I32 = jnp.int32
MESH = pl.DeviceIdType.MESH
ANY = pl.BlockSpec(memory_space=pl.ANY)

ATTN_HEADS = 8
ATTN_HEAD_DIM = 128
RET_HEADS = 4
RET_HEAD_DIM = 256
ATTN_WIDTH = ATTN_HEADS * ATTN_HEAD_DIM
RET_WIDTH = RET_HEADS * RET_HEAD_DIM
DILATED_PATTERNS = ((128, 1), (512, 4), (2048, 16))
NORM_EPS = 1e-6
ADAM_LR = 0.001
ADAM_B1 = 0.9
ADAM_B2 = 0.999
ADAM_EPS = 1e-08
ADAM_WD = 0.01
ADAM_STEP = 10

N_CHIPS = 4
N_DEV = 8
NEG_BIG = -1e30
SEQ_TILE = 512
ATTN_HEADS_PER_STEP = 1
VMEM_LIMIT_BYTES = 56 * 1024 * 1024


def _params(semantics=None, vmem=VMEM_LIMIT_BYTES):
    return pltpu.CompilerParams(dimension_semantics=semantics, vmem_limit_bytes=vmem)


def _row_tile(rows, row_bytes, limit=2 * 1024 * 1024, mult=16):
    best = None
    for t in range(mult, rows + 1, mult):
        if rows % t == 0 and t * row_bytes <= limit:
            best = t
    assert best is not None, (rows, row_bytes)
    return best


def _sigmoid(x):
    return 1.0 / (1.0 + jnp.exp(-x))


def _select_by_index(idx, values):
    out = jnp.float32(values[-1])
    for i in range(len(values) - 2, -1, -1):
        out = jnp.where(idx == i, jnp.float32(values[i]), out)
    return out


def _place():
    x, y, c = lax.axis_index("x"), lax.axis_index("y"), lax.axis_index("c")
    return x, y, c


def _cast_into_full(w, shard_arr, column_sharded, name):
    rows, cols = w.shape
    tr = _row_tile(rows, cols * 4)
    steps = rows // tr
    if column_sharded:
        out_shape, out_map = (rows, N_CHIPS * cols), (lambda i, s_ref: (i, s_ref[0]))
    else:
        out_shape, out_map = (N_CHIPS * rows, cols), (lambda i, s_ref: (s_ref[0] * steps + i, 0))

    def body(s_ref, w_ref, o_ref):
        del s_ref
        o_ref[...] = w_ref[...].astype(BF16)

    grid_spec = pltpu.PrefetchScalarGridSpec(
        num_scalar_prefetch=1, grid=(steps,),
        in_specs=[pl.BlockSpec((tr, cols), lambda i, s_ref: (i, 0))],
        out_specs=pl.BlockSpec((tr, cols), out_map))
    return pl.pallas_call(
        body, name=name, grid_spec=grid_spec,
        out_shape=jax.ShapeDtypeStruct(out_shape, BF16),
        compiler_params=_params(("parallel",)),
    )(shard_arr, w)


def _rms_fwd(x, w, name):
    rows, d = x.shape
    tr = 256

    def body(x_ref, w_ref, h_ref):
        xv = x_ref[...]
        r = lax.rsqrt(jnp.mean(xv * xv, axis=-1, keepdims=True) + NORM_EPS)
        h_ref[...] = (xv * r * w_ref[...]).astype(BF16)

    return pl.pallas_call(
        body, name=name, grid=(rows // tr,),
        in_specs=[pl.BlockSpec((tr, d), lambda i: (i, 0)), pl.BlockSpec((1, d), lambda i: (0, 0))],
        out_specs=pl.BlockSpec((tr, d), lambda i: (i, 0)),
        out_shape=jax.ShapeDtypeStruct((rows, d), BF16),
        compiler_params=_params(("parallel",)),
    )(x, w)


def _rms_bwd(x, w, dh, dres, name, after=()):
    rows, d = x.shape
    tr = 256
    after = tuple(after)

    def body(x_ref, w_ref, dh_ref, dres_ref, *rest):
        dx_ref, dxb_ref, dw_ref = rest[len(after):]
        xv = x_ref[...]
        r = lax.rsqrt(jnp.mean(xv * xv, axis=-1, keepdims=True) + NORM_EPS)
        xhat = xv * r
        dy = dh_ref[...]
        dxhat = dy * w_ref[...]
        dx = dres_ref[...] + r * (dxhat - xhat * jnp.mean(dxhat * xhat, axis=-1, keepdims=True))
        dx_ref[...] = dx
        dxb_ref[...] = dx.astype(BF16)
        part = jnp.sum(dy * xhat, axis=0, keepdims=True)

        @pl.when(pl.program_id(0) == 0)
        def _():
            dw_ref[...] = part

        @pl.when(pl.program_id(0) != 0)
        def _():
            dw_ref[...] += part

    row = pl.BlockSpec((tr, d), lambda i: (i, 0))
    vec = pl.BlockSpec((1, d), lambda i: (0, 0))
    return pl.pallas_call(
        body, name=name, grid=(rows // tr,),
        in_specs=[row, vec, row, row] + [ANY] * len(after),
        out_specs=[row, row, vec],
        out_shape=[jax.ShapeDtypeStruct((rows, d), F32), jax.ShapeDtypeStruct((rows, d), BF16),
                   jax.ShapeDtypeStruct((1, d), F32)],
        compiler_params=_params(("arbitrary",)),
    )(x, w, dh, dres, *after)


def _final_norm_loss(x2, w, target, name):
    rows, d = x2.shape
    tr = 256

    def body(x_ref, w_ref, t_ref, loss_ref, dx_ref, dxb_ref, dw_ref):
        xv = x_ref[...]
        wv = w_ref[...]
        r = lax.rsqrt(jnp.mean(xv * xv, axis=-1, keepdims=True) + NORM_EPS)
        xhat = xv * r
        err = xhat * wv - t_ref[...]
        part_loss = 0.5 * jnp.sum(jnp.mean(err * err, axis=-1, keepdims=True), axis=0, keepdims=True)
        dy = err * (1.0 / d)
        dxhat = dy * wv
        dx = r * (dxhat - xhat * jnp.mean(dxhat * xhat, axis=-1, keepdims=True))
        dx_ref[...] = dx
        dxb_ref[...] = dx.astype(BF16)
        part_dw = jnp.sum(dy * xhat, axis=0, keepdims=True)
        part_loss = jnp.broadcast_to(part_loss, (1, 128))

        @pl.when(pl.program_id(0) == 0)
        def _():
            dw_ref[...] = part_dw
            loss_ref[...] = part_loss

        @pl.when(pl.program_id(0) != 0)
        def _():
            dw_ref[...] += part_dw
            loss_ref[...] += part_loss

    row = pl.BlockSpec((tr, d), lambda i: (i, 0))
    vec = pl.BlockSpec((1, d), lambda i: (0, 0))
    return pl.pallas_call(
        body, name=name, grid=(rows // tr,),
        in_specs=[row, vec, row],
        out_specs=[pl.BlockSpec((1, 128), lambda i: (0, 0)), row, row, vec],
        out_shape=[jax.ShapeDtypeStruct((1, 128), F32), jax.ShapeDtypeStruct((rows, d), F32),
                   jax.ShapeDtypeStruct((rows, d), BF16), jax.ShapeDtypeStruct((1, d), F32)],
        compiler_params=_params(("arbitrary",)),
    )(x2, w, target)


def _adamw_math(w, g, m, v):
    m = ADAM_B1 * m + (1.0 - ADAM_B1) * g
    v = ADAM_B2 * v + (1.0 - ADAM_B2) * (g * g)
    m_hat = m / (1.0 - ADAM_B1 ** ADAM_STEP)
    v_hat = v / (1.0 - ADAM_B2 ** ADAM_STEP)
    delta = -ADAM_LR * (m_hat / (jnp.sqrt(v_hat) + ADAM_EPS) + ADAM_WD * w)
    return delta, m, v


def _adamw(w, g, m, v, name):
    rows, cols = w.shape
    tr = _row_tile(rows, cols * 4, limit=1024 * 1024)

    def body(w_ref, g_ref, m_ref, v_ref, d_ref, mo_ref, vo_ref):
        delta, m_new, v_new = _adamw_math(w_ref[...], g_ref[...], m_ref[...], v_ref[...])
        d_ref[...] = delta
        mo_ref[...] = m_new
        vo_ref[...] = v_new

    blk = pl.BlockSpec((tr, cols), lambda i: (i, 0))
    shp = jax.ShapeDtypeStruct((rows, cols), F32)
    return pl.pallas_call(
        body, name=name, grid=(rows // tr,),
        in_specs=[blk] * 4, out_specs=[blk] * 3, out_shape=[shp] * 3,
        compiler_params=_params(("parallel",)),
    )(w, g, m, v)


_DOT_DIMS = {"nn": ((1,), (0,)), "nt": ((1,), (1,)), "tn": ((0,), (0,))}


def _matmul(name, mode, a_list, b_list, acc_of, m, n, k, tm, tn, tk, extras, out_dtypes, epilogue,
            a_koff=None, b_koff=None, after=()):
    after = tuple(after)
    assert m % tm == 0 and n % tn == 0 and k % tk == 0, (name, m, n, k, tm, tn, tk)
    nk = k // tk
    n_acc = max(acc_of) + 1
    n_pairs = len(a_list)
    a_koff = a_koff or [0] * n_pairs
    b_koff = b_koff or [0] * n_pairs
    dims = (_DOT_DIMS[mode], ((), ()))
    n_ext, n_out = len(extras), len(out_dtypes)

    def body(*refs):
        a_refs = refs[:n_pairs]
        b_refs = refs[n_pairs:2 * n_pairs]
        e_refs = refs[2 * n_pairs:2 * n_pairs + n_ext]
        first_out = 2 * n_pairs + n_ext + len(after)
        o_refs = refs[first_out:first_out + n_out]
        acc_refs = refs[first_out + n_out:]
        parts = [None] * n_acc
        for p in range(n_pairs):
            d = lax.dot_general(a_refs[p][...], b_refs[p][...], dims, preferred_element_type=F32)
            parts[acc_of[p]] = d if parts[acc_of[p]] is None else parts[acc_of[p]] + d

        def finish(accs):
            outs = epilogue(accs, [e[...] for e in e_refs])
            for o_ref, o in zip(o_refs, outs):
                o_ref[...] = o.astype(o_ref.dtype)

        if nk == 1:
            finish(parts)
        else:
            kk = pl.program_id(2)

            @pl.when(kk == 0)
            def _():
                for acc_ref, part in zip(acc_refs, parts):
                    acc_ref[...] = part

            @pl.when(kk != 0)
            def _():
                for acc_ref, part in zip(acc_refs, parts):
                    acc_ref[...] += part

            @pl.when(kk == nk - 1)
            def _():
                finish([acc_ref[...] for acc_ref in acc_refs])

    def a_spec(off):
        if mode == "tn":
            return pl.BlockSpec((tk, tm), lambda i, j, kk: (kk + off, i))
        return pl.BlockSpec((tm, tk), lambda i, j, kk: (i, kk + off))

    def b_spec(off):
        if mode == "nt":
            return pl.BlockSpec((tn, tk), lambda i, j, kk: (j, kk + off))
        return pl.BlockSpec((tk, tn), lambda i, j, kk: (kk + off, j))

    tile = pl.BlockSpec((tm, tn), lambda i, j, kk: (i, j))
    scratch = [pltpu.VMEM((tm, tn), F32) for _ in range(n_acc)] if nk > 1 else []
    return pl.pallas_call(
        body, name=name, grid=(m // tm, n // tn, nk),
        in_specs=[a_spec(o) for o in a_koff] + [b_spec(o) for o in b_koff] + [tile] * n_ext + [ANY] * len(after),
        out_specs=[tile] * n_out,
        out_shape=[jax.ShapeDtypeStruct((m, n), dt) for dt in out_dtypes],
        scratch_shapes=scratch,
        compiler_params=_params(("parallel", "parallel", "arbitrary")),
    )(*a_list, *b_list, *extras, *after)


def _epi_plain(accs, extras):
    return (accs[0],)


def _epi_residual(accs, extras):
    return (accs[0] + extras[0],)


def _epi_two(accs, extras):
    return accs[0], accs[1]


def _epi_swiglu(accs, extras):
    g, u = accs
    return g, u, g * _sigmoid(g) * u


def _epi_swiglu_bwd(accs, extras):
    da = accs[0]
    g, u = extras
    sg = _sigmoid(g)
    dg = da * u * sg * (1.0 + g * (1.0 - sg))
    du = da * g * sg
    return dg, du


_NT_DIMS = (((1,), (1,)), ((), ()))
_TN_DIMS = (((0,), (0,)), ((), ()))


def _tile_delta(tq, tk):
    return lax.broadcasted_iota(I32, (tq, tk), 0) - lax.broadcasted_iota(I32, (tq, tk), 1)


def _attn_mask_bias(delta, slope):
    count = jnp.zeros(delta.shape, I32)
    for window, dilation in DILATED_PATTERNS:
        hit = ((delta & (dilation - 1)) == 0) & (delta <= window)
        count = count + jnp.where(hit, 1, 0)
    valid = (delta >= 0) & (count > 0)
    logm = jnp.where(count == 3, math.log(3.0), jnp.where(count == 2, math.log(2.0), 0.0))
    return valid, logm - slope * delta.astype(F32)


def _fill_attn_bias(tab_ref, slope):
    nb, t, _ = tab_ref.shape
    base = _tile_delta(t, t)
    for b in range(nb):
        valid, bias = _attn_mask_bias(base + b * t, slope)
        tab_ref[b] = jnp.where(valid, bias, NEG_BIG)


def _fill_ret_decay(tab_ref, log_gamma):
    nb, t, _ = tab_ref.shape
    base = _tile_delta(t, t)
    for b in range(nb):
        tab_ref[b] = _ret_decay(base + b * t, log_gamma)


def _alibi_slopes():
    return [2.0 ** (-8.0 * (h + 1) / ATTN_HEADS) for h in range(ATTN_HEADS)]


def _attn_fwd(proj):
    s = proj.shape[0]
    t = SEQ_TILE
    hd = ATTN_HEAD_DIM
    hp = ATTN_HEADS_PER_STEP
    ng = ATTN_HEADS // hp
    w = hp * hd
    scale = 1.0 / math.sqrt(hd)
    slopes = _alibi_slopes()

    def body(q_ref, k_ref, v_ref, mix_ref, o_ref, lse_ref, kb, vb, bias_tab):
        g = pl.program_id(0)
        i = pl.program_id(1)

        @pl.when(i == 0)
        def _():
            kb[...] = k_ref[...].astype(BF16)
            vb[...] = v_ref[...].astype(BF16)
            for u in range(hp):
                _fill_attn_bias(bias_tab.at[u], _select_by_index(g * hp + u, slopes))

        qs = [q_ref[:, u * hd:(u + 1) * hd].astype(BF16) for u in range(hp)]

        def step(j, carry):
            rows = pl.ds(pl.multiple_of(j * t, t), t)
            out = []
            for u in range(hp):
                m_i, l_i, acc = carry[u]
                lanes = slice(u * hd, (u + 1) * hd)
                sc = lax.dot_general(qs[u], kb[rows, lanes], _NT_DIMS, preferred_element_type=F32) * scale
                sc = sc + bias_tab[u, i - j]
                m_new = jnp.maximum(m_i, jnp.max(sc, axis=-1, keepdims=True))
                p = jnp.exp(sc - m_new)
                alpha = jnp.exp(m_i - m_new)
                l_new = alpha * l_i + jnp.sum(p, axis=-1, keepdims=True)
                acc = alpha * acc + jnp.dot(p.astype(BF16), vb[rows, lanes], preferred_element_type=F32)
                out.append((m_new, l_new, acc))
            return tuple(out)

        init = (jnp.full((t, 1), NEG_BIG, F32), jnp.zeros((t, 1), F32), jnp.zeros((t, hd), F32))
        final = lax.fori_loop(0, i + 1, step, (init,) * hp)
        for u in range(hp):
            m_i, l_i, acc = final[u]
            lanes = slice(u * hd, (u + 1) * hd)
            out = acc / l_i
            o_ref[:, lanes] = out
            mix_ref[:, lanes] = out.astype(BF16)
            lse_ref[:, lanes] = jnp.broadcast_to(m_i + jnp.log(l_i), (t, hd))

    return pl.pallas_call(
        body, name="attn_fwd", grid=(ng, s // t),
        in_specs=[pl.BlockSpec((t, w), lambda g, i: (i, g)),
                  pl.BlockSpec((s, w), lambda g, i: (0, ng + g)),
                  pl.BlockSpec((s, w), lambda g, i: (0, 2 * ng + g))],
        out_specs=[pl.BlockSpec((t, w), lambda g, i: (i, g))] * 3,
        out_shape=[jax.ShapeDtypeStruct((s, ATTN_WIDTH), BF16),
                   jax.ShapeDtypeStruct((s, ATTN_WIDTH), F32),
                   jax.ShapeDtypeStruct((s, ATTN_WIDTH), F32)],
        scratch_shapes=[pltpu.VMEM((s, w), BF16), pltpu.VMEM((s, w), BF16), pltpu.VMEM((hp, s // t, t, t), F32)],
        compiler_params=_params(("arbitrary", "arbitrary")),
    )(proj, proj, proj)


def _attn_bwd(proj, attn_out, lse, dmixed, after=()):
    after = tuple(after)
    s = proj.shape[0]
    t = SEQ_TILE
    nt = s // t
    hd = ATTN_HEAD_DIM
    hp = ATTN_HEADS_PER_STEP
    ng = ATTN_HEADS // hp
    w = hp * hd
    scale = 1.0 / math.sqrt(hd)
    slopes = _alibi_slopes()

    def body(q_ref, k_ref, v_ref, o_ref, lse_ref, do_ref, *rest):
        dq_ref, dk_ref, dv_ref, qb, kb, vb, dob, dsum, dq_acc, bias_tab = rest[len(after):]
        g = pl.program_id(0)
        qb[...] = q_ref[...].astype(BF16)
        kb[...] = k_ref[...].astype(BF16)
        vb[...] = v_ref[...].astype(BF16)
        dob[...] = do_ref[...].astype(BF16)
        for u in range(hp):
            lanes = slice(u * hd, (u + 1) * hd)
            _fill_attn_bias(bias_tab.at[u], _select_by_index(g * hp + u, slopes))
            rowsum = jnp.sum(do_ref[:, lanes] * o_ref[:, lanes], axis=-1, keepdims=True)
            dsum[:, lanes] = jnp.broadcast_to(rowsum, (s, hd))
        dq_acc[...] = jnp.zeros((s, w), F32)

        def over_keys(j, _):
            krows = pl.ds(pl.multiple_of(j * t, t), t)

            def over_queries(i, carry):
                qrows = pl.ds(pl.multiple_of(i * t, t), t)
                out = []
                for u in range(hp):
                    dk, dv = carry[u]
                    lanes = slice(u * hd, (u + 1) * hd)
                    qi, doi = qb[qrows, lanes], dob[qrows, lanes]
                    kj, vj = kb[krows, lanes], vb[krows, lanes]
                    lse_i = lse_ref[qrows, lanes][:, :1]
                    dsum_i = dsum[qrows, lanes][:, :1]
                    sc = lax.dot_general(qi, kj, _NT_DIMS, preferred_element_type=F32) * scale
                    p = jnp.exp(sc + bias_tab[u, i - j] - lse_i)
                    dp = lax.dot_general(doi, vj, _NT_DIMS, preferred_element_type=F32)
                    ds = (p * (dp - dsum_i)).astype(BF16)
                    dv = dv + lax.dot_general(p.astype(BF16), doi, _TN_DIMS, preferred_element_type=F32)
                    dk = dk + lax.dot_general(ds, qi, _TN_DIMS, preferred_element_type=F32)
                    dq_acc[qrows, lanes] += jnp.dot(ds, kj, preferred_element_type=F32)
                    out.append((dk, dv))
                return tuple(out)

            zero = jnp.zeros((t, hd), F32)
            final = lax.fori_loop(j, nt, over_queries, ((zero, zero),) * hp)
            for u in range(hp):
                lanes = slice(u * hd, (u + 1) * hd)
                dk_ref[krows, lanes] = (final[u][0] * scale).astype(BF16)
                dv_ref[krows, lanes] = final[u][1].astype(BF16)
            return 0

        lax.fori_loop(0, nt, over_keys, 0)
        dq_ref[...] = (dq_acc[...] * scale).astype(BF16)

    def col(off):
        return pl.BlockSpec((s, w), lambda g: (0, off + g))

    return pl.pallas_call(
        body, name="attn_bwd", grid=(ng,),
        in_specs=[col(0), col(ng), col(2 * ng), col(0), col(0), col(0)] + [ANY] * len(after),
        out_specs=[col(0)] * 3,
        out_shape=[jax.ShapeDtypeStruct((s, ATTN_WIDTH), BF16)] * 3,
        scratch_shapes=[pltpu.VMEM((s, w), BF16)] * 4 + [pltpu.VMEM((s, w), F32)] * 2
        + [pltpu.VMEM((hp, nt, t, t), F32)],
        compiler_params=_params(("arbitrary",)),
    )(proj, proj, proj, attn_out, lse, dmixed, *after)


def _ret_log_gammas():
    return [math.log(1.0 - 2.0 ** (-5.0 - h)) for h in range(RET_HEADS)]


def _ret_decay(delta, log_gamma):
    dec = jnp.exp(delta.astype(F32) * log_gamma) * (1.0 / math.sqrt(RET_HEAD_DIM))
    return jnp.where(delta >= 0, dec, 0.0)


def _ret_fwd(proj):
    s = proj.shape[0]
    t = SEQ_TILE
    hd = RET_HEAD_DIM
    nh = RET_HEADS
    log_gammas = _ret_log_gammas()
    c0 = 3 * ATTN_WIDTH // hd

    def body(q_ref, k_ref, v_ref, g_ref, mix_ref, raw_ref, kb, vb, decay_tab):
        h = pl.program_id(0)
        i = pl.program_id(1)

        @pl.when(i == 0)
        def _():
            kb[...] = k_ref[...].astype(BF16)
            vb[...] = v_ref[...].astype(BF16)
            _fill_ret_decay(decay_tab, _select_by_index(h, log_gammas))

        q = q_ref[...].astype(BF16)

        def step(j, acc):
            rows = pl.ds(pl.multiple_of(j * t, t), t)
            sc = lax.dot_general(q, kb[rows, :], _NT_DIMS, preferred_element_type=F32) * decay_tab[i - j]
            return acc + jnp.dot(sc.astype(BF16), vb[rows, :], preferred_element_type=F32)

        ret = lax.fori_loop(0, i + 1, step, jnp.zeros((t, hd), F32))
        raw_ref[...] = ret
        r = lax.rsqrt(jnp.mean(ret * ret, axis=-1, keepdims=True) + NORM_EPS)
        g = g_ref[...]
        mix_ref[...] = (g * _sigmoid(g) * (ret * r)).astype(BF16)

    return pl.pallas_call(
        body, name="ret_fwd", grid=(nh, s // t),
        in_specs=[pl.BlockSpec((t, hd), lambda h, i: (i, c0 + h)),
                  pl.BlockSpec((s, hd), lambda h, i: (0, c0 + nh + h)),
                  pl.BlockSpec((s, hd), lambda h, i: (0, c0 + 2 * nh + h)),
                  pl.BlockSpec((t, hd), lambda h, i: (i, c0 + 3 * nh + h))],
        out_specs=[pl.BlockSpec((t, hd), lambda h, i: (i, h))] * 2,
        out_shape=[jax.ShapeDtypeStruct((s, RET_WIDTH), BF16), jax.ShapeDtypeStruct((s, RET_WIDTH), F32)],
        scratch_shapes=[pltpu.VMEM((s, hd), BF16), pltpu.VMEM((s, hd), BF16), pltpu.VMEM((s // t, t, t), F32)],
        compiler_params=_params(("arbitrary", "arbitrary")),
    )(proj, proj, proj, proj)


def _ret_bwd(proj, ret_raw, dmixed, after=()):
    after = tuple(after)
    s = proj.shape[0]
    t = SEQ_TILE
    nt = s // t
    hd = RET_HEAD_DIM
    nh = RET_HEADS
    log_gammas = _ret_log_gammas()
    c0 = 3 * ATTN_WIDTH // hd
    mixed_blocks = ATTN_WIDTH // hd

    def body(q_ref, k_ref, v_ref, g_ref, raw_ref, dmix_ref, *rest):
        dq_ref, dk_ref, dv_ref, dg_ref, qb, kb, vb, dretb, dq_acc, decay_tab = rest[len(after):]
        h = pl.program_id(0)
        _fill_ret_decay(decay_tab, _select_by_index(h, log_gammas))
        qb[...] = q_ref[...].astype(BF16)
        kb[...] = k_ref[...].astype(BF16)
        vb[...] = v_ref[...].astype(BF16)
        ret = raw_ref[...]
        r = lax.rsqrt(jnp.mean(ret * ret, axis=-1, keepdims=True) + NORM_EPS)
        normed = ret * r
        g = g_ref[...]
        sg = _sigmoid(g)
        dout = dmix_ref[...]
        dg_ref[...] = (dout * normed * sg * (1.0 + g * (1.0 - sg))).astype(BF16)
        dn = dout * g * sg
        dret = r * (dn - normed * jnp.mean(dn * normed, axis=-1, keepdims=True))
        dretb[...] = dret.astype(BF16)
        dq_acc[...] = jnp.zeros((s, hd), F32)

        def over_keys(j, _):
            krows = pl.ds(pl.multiple_of(j * t, t), t)
            kj = kb[krows, :]
            vj = vb[krows, :]

            def over_queries(i, carry):
                dk, dv = carry
                qrows = pl.ds(pl.multiple_of(i * t, t), t)
                qi = qb[qrows, :]
                doi = dretb[qrows, :]
                dec = decay_tab[i - j]
                a = (lax.dot_general(qi, kj, _NT_DIMS, preferred_element_type=F32) * dec).astype(BF16)
                da = (lax.dot_general(doi, vj, _NT_DIMS, preferred_element_type=F32) * dec).astype(BF16)
                dv = dv + lax.dot_general(a, doi, _TN_DIMS, preferred_element_type=F32)
                dk = dk + lax.dot_general(da, qi, _TN_DIMS, preferred_element_type=F32)
                dq_acc[qrows, :] += jnp.dot(da, kj, preferred_element_type=F32)
                return dk, dv

            zero = jnp.zeros((t, hd), F32)
            dk, dv = lax.fori_loop(j, nt, over_queries, (zero, zero))
            dk_ref[krows, :] = dk.astype(BF16)
            dv_ref[krows, :] = dv.astype(BF16)
            return 0

        lax.fori_loop(0, nt, over_keys, 0)
        dq_ref[...] = dq_acc[...].astype(BF16)

    def col(off):
        return pl.BlockSpec((s, hd), lambda h: (0, off + h))

    return pl.pallas_call(
        body, name="ret_bwd", grid=(nh,),
        in_specs=[col(c0), col(c0 + nh), col(c0 + 2 * nh), col(c0 + 3 * nh), col(0), col(mixed_blocks)]
        + [ANY] * len(after),
        out_specs=[col(0)] * 4,
        out_shape=[jax.ShapeDtypeStruct((s, RET_WIDTH), BF16)] * 4,
        scratch_shapes=[pltpu.VMEM((s, hd), BF16)] * 4 + [pltpu.VMEM((s, hd), F32)]
        + [pltpu.VMEM((nt, t, t), F32)],
        compiler_params=_params(("arbitrary",)),
    )(proj, proj, proj, proj, ret_raw, dmixed, *after)


_FLIPS = (2, 1, 3)


def _other_chips(x, y):
    return [(1 - x, y), (x, 1 - y), (1 - x, 1 - y)]


_HBM = pl.BlockSpec(memory_space=pltpu.HBM)
_SEM = pl.BlockSpec(memory_space=pltpu.SEMAPHORE)
_EFFECT = pltpu.SideEffectType.DATAFLOW_SIDE_EFFECTING


def _in_hbm(a):
    return pltpu.with_memory_space_constraint(a, pltpu.HBM)


def _weight_view(w, column_sharded):
    if column_sharded:
        return w.reshape(2, w.shape[0] // 2, w.shape[1])
    return w.reshape(N_CHIPS, 2, w.shape[0] // (2 * N_CHIPS), w.shape[1])


def _weight_unview(v):
    if v.ndim == 3:
        return v.reshape(2 * v.shape[1], v.shape[2])
    return v.reshape(N_CHIPS * 2 * v.shape[2], v.shape[3])


def _weight_region(buf, shard, half):
    if len(buf.shape) == 3:
        cols = buf.shape[2] // N_CHIPS
        return buf.at[half, :, pl.ds(shard * cols, cols)]
    return buf.at[shard, half]


def _remote(where, send_sem, recv_sem, to):
    return pltpu.make_async_remote_copy(src_ref=where, dst_ref=where, send_sem=send_sem, recv_sem=recv_sem,
                                        device_id=to, device_id_type=MESH)


def _for_my_shard(fn):
    x, y, _ = _place()
    for ss in range(N_CHIPS):
        pl.when(2 * x + y == ss)(functools.partial(fn, ss))


def _gather_start(views):
    n_w = len(views)

    def body(*refs):
        send_sems, recv_sems = refs[n_w:n_w + 2]
        bufs = refs[n_w + 2:]
        x, y, c = _place()

        def start(ss):
            for w in range(n_w):
                for j, chip in enumerate(_other_chips(x, y)):
                    _remote(_weight_region(bufs[w], ss, c), send_sems.at[3 * w + j], recv_sems.at[3 * w + j],
                            (*chip, c)).start()

        _for_my_shard(start)

    return pl.pallas_call(
        body, name="gather_start",
        in_specs=[_HBM] * n_w, out_specs=[_SEM, _SEM] + [_HBM] * n_w,
        out_shape=[pltpu.SemaphoreType.DMA((3 * n_w,)), pltpu.SemaphoreType.DMA((3 * n_w,))]
        + [pltpu.HBM(v.shape, BF16) for v in views],
        input_output_aliases={w: 2 + w for w in range(n_w)},
        compiler_params=pltpu.CompilerParams(has_side_effects=_EFFECT),
    )(*[_in_hbm(v) for v in views])


def _gather_forward(views, which, send_sems, recv_sems, after, name):
    n_w = len(views)

    def body(*refs):
        send_in, recv_in = refs[n_w:n_w + 2]
        fwd_send, fwd_recv = refs[n_w + 3:n_w + 5]
        bufs = refs[n_w + 5:]
        x, y, c = _place()
        sibling = (x, y, 1 - c)

        def forward(ss):
            for i, w in enumerate(which):
                for j in range(3):
                    landed = _weight_region(bufs[i], ss ^ _FLIPS[j], c)
                    _remote(landed, send_in.at[3 * w + j], recv_in.at[3 * w + j], sibling).wait_recv()
                    _remote(landed, fwd_send.at[3 * i + j], fwd_recv.at[3 * i + j], sibling).start()

        _for_my_shard(forward)
        for i, w in enumerate(which):
            for j in range(3):
                _remote(_weight_region(bufs[i], 0, 0), send_in.at[3 * w + j], recv_in.at[3 * w + j],
                        sibling).wait_send()

    return pl.pallas_call(
        body, name=name,
        in_specs=[_HBM] * n_w + [_SEM, _SEM, ANY], out_specs=[_SEM, _SEM] + [_HBM] * n_w,
        out_shape=[pltpu.SemaphoreType.DMA((3 * n_w,)), pltpu.SemaphoreType.DMA((3 * n_w,))]
        + [pltpu.HBM(v.shape, BF16) for v in views],
        input_output_aliases={w: 2 + w for w in range(n_w)},
        compiler_params=pltpu.CompilerParams(has_side_effects=_EFFECT),
    )(*views, send_sems, recv_sems, after)


def _gather_end(views, fwd_send, fwd_recv, after, name):
    n_w = len(views)

    def body(*refs):
        fwd_send_ref, fwd_recv_ref = refs[n_w:n_w + 2]
        bufs = refs[n_w + 3:]
        x, y, c = _place()
        for i in range(n_w):
            for j in range(3):
                cp = _remote(_weight_region(bufs[i], 0, 0), fwd_send_ref.at[3 * i + j], fwd_recv_ref.at[3 * i + j],
                             (x, y, 1 - c))
                cp.wait_recv()
                cp.wait_send()

    outs = pl.pallas_call(
        body, name=name,
        in_specs=[_HBM] * n_w + [_SEM, _SEM, ANY], out_specs=[_HBM] * n_w,
        out_shape=[pltpu.HBM(v.shape, BF16) for v in views],
        input_output_aliases={w: w for w in range(n_w)},
        compiler_params=pltpu.CompilerParams(has_side_effects=_EFFECT),
    )(*views, fwd_send, fwd_recv, after)
    return [_weight_unview(o) for o in outs]


def _split_start(name, bufs, n_sems, copies):
    n = len(bufs)

    def body(*refs):
        send_sems, recv_sems = refs[n:n + 2]
        for cp in copies(refs[n + 2:], send_sems, recv_sems):
            cp.start()

    outs = pl.pallas_call(
        body, name=name,
        in_specs=[_HBM] * n, out_specs=[_SEM, _SEM] + [_HBM] * n,
        out_shape=[pltpu.SemaphoreType.DMA((n_sems,)), pltpu.SemaphoreType.DMA((n_sems,))]
        + [pltpu.HBM(b.shape, b.dtype) for b in bufs],
        input_output_aliases={i: 2 + i for i in range(n)},
        compiler_params=pltpu.CompilerParams(has_side_effects=_EFFECT),
    )(*[_in_hbm(b) for b in bufs])
    return outs[0], outs[1], list(outs[2:])


def _split_wait(name, bufs, send_sems, recv_sems, copies, after):
    n = len(bufs)

    def body(*refs):
        send_ref, recv_ref = refs[n:n + 2]
        for cp in copies(refs[n + 3:], send_ref, recv_ref):
            cp.wait()

    return list(pl.pallas_call(
        body, name=name,
        in_specs=[_HBM] * n + [_SEM, _SEM, ANY], out_specs=[_HBM] * n,
        out_shape=[pltpu.HBM(b.shape, b.dtype) for b in bufs],
        input_output_aliases={i: i for i in range(n)},
        compiler_params=pltpu.CompilerParams(has_side_effects=_EFFECT),
    )(*bufs, send_sems, recv_sems, after))


def _halves_copies(n_w):
    def copies(bufs, send_sems, recv_sems):
        x, y, c = _place()
        out = []
        for w in range(n_w):
            view, land = bufs[w], bufs[n_w + w]
            src = view.at[1 - c] if len(view.shape) == 3 else view.at[:, 1 - c]
            out.append(pltpu.make_async_remote_copy(
                src_ref=src, dst_ref=land, send_sem=send_sems.at[w], recv_sem=recv_sems.at[w],
                device_id=(x, y, 1 - c), device_id_type=MESH))
        return out
    return copies


def _pieces_copies(n_w):
    def copies(bufs, send_sems, recv_sems):
        x, y, c = _place()
        out = []
        for w in range(n_w):
            for j, (cx, cy) in enumerate(_other_chips(x, y)):
                out.append(pltpu.make_async_remote_copy(
                    src_ref=bufs[w].at[2 * cx + cy], dst_ref=bufs[n_w + w].at[j],
                    send_sem=send_sems.at[3 * w + j], recv_sem=recv_sems.at[3 * w + j],
                    device_id=(cx, cy, c), device_id_type=MESH))
        return out
    return copies


def _join_copies(n_w):
    def copies(bufs, send_sems, recv_sems):
        x, y, c = _place()
        return [pltpu.make_async_remote_copy(
            src_ref=bufs[w].at[c], dst_ref=bufs[w].at[c], send_sem=send_sems.at[w], recv_sem=recv_sems.at[w],
            device_id=(x, y, 1 - c), device_id_type=MESH) for w in range(n_w)]
    return copies


def _grad_view(g, column_sharded):
    return _weight_view(g, column_sharded)


def _halves_landing(view):
    shape = view.shape[1:] if view.ndim == 3 else (N_CHIPS,) + view.shape[2:]
    return lax.empty(shape, BF16)


def _halves_start(tag, grads, column_sharded):
    views = [_weight_view(g, cs) for g, cs in zip(grads, column_sharded)]
    n = len(views)
    return _split_start("halves_start_" + tag, views + [_halves_landing(v) for v in views], n, _halves_copies(n))


def _halves_wait(tag, state, after):
    send_sems, recv_sems, bufs = state
    n = len(bufs) // 2
    bufs = _split_wait("halves_wait_" + tag, bufs, send_sems, recv_sems, _halves_copies(n), after)
    return bufs[:n], bufs[n:]


def _pieces_start(tag, pieces):
    n = len(pieces)
    landing = [lax.empty((3,) + p.shape[1:], BF16) for p in pieces]
    return _split_start("pieces_start_" + tag, list(pieces) + landing, 3 * n, _pieces_copies(n))


def _pieces_wait(tag, state, after):
    send_sems, recv_sems, bufs = state
    n = len(bufs) // 2
    bufs = _split_wait("pieces_wait_" + tag, bufs, send_sems, recv_sems, _pieces_copies(n), after)
    return bufs[:n], bufs[n:]


def _join_start(tag, shards):
    n = len(shards)
    return _split_start("join_start_" + tag, list(shards), n, _join_copies(n))


def _join_wait(tag, state, after):
    send_sems, recv_sems, bufs = state
    bufs = _split_wait("join_wait_" + tag, bufs, send_sems, recv_sems, _join_copies(len(bufs)), after)
    return [b.reshape(2 * b.shape[1], b.shape[2]) for b in bufs]


def _chip_sum_col(g3, sib, c_arr, name):
    _, hk, n = g3.shape
    cols = n // N_CHIPS
    tr = _row_tile(hk, cols * 2, limit=1024 * 1024)

    def body(c_ref, g_ref, s_ref, o_ref):
        del c_ref
        o_ref[...] = (g_ref[...].astype(F32) + s_ref[...].astype(F32)).astype(BF16)

    grid_spec = pltpu.PrefetchScalarGridSpec(
        num_scalar_prefetch=1, grid=(N_CHIPS, hk // tr),
        in_specs=[pl.BlockSpec((None, tr, cols), lambda p, r, c_ref: (c_ref[0], r, p)),
                  pl.BlockSpec((tr, cols), lambda p, r, c_ref: (r, p))],
        out_specs=pl.BlockSpec((None, tr, cols), lambda p, r, c_ref: (p, r, 0)))
    return pl.pallas_call(
        body, name=name, grid_spec=grid_spec,
        out_shape=jax.ShapeDtypeStruct((N_CHIPS, hk, cols), BF16),
        compiler_params=_params(("parallel", "parallel")),
    )(c_arr, g3, sib)


def _chip_sum_row(g4, sib, c_arr, name):
    _, _, hr, n = g4.shape
    tr = _row_tile(hr, n * 2, limit=1024 * 1024)

    def body(c_ref, g_ref, s_ref, o_ref):
        del c_ref
        o_ref[...] = (g_ref[...].astype(F32) + s_ref[...].astype(F32)).astype(BF16)

    grid_spec = pltpu.PrefetchScalarGridSpec(
        num_scalar_prefetch=1, grid=(N_CHIPS, hr // tr),
        in_specs=[pl.BlockSpec((None, None, tr, n), lambda p, r, c_ref: (p, c_ref[0], r, 0)),
                  pl.BlockSpec((None, tr, n), lambda p, r, c_ref: (p, r, 0))],
        out_specs=pl.BlockSpec((None, tr, n), lambda p, r, c_ref: (p, r, 0)))
    return pl.pallas_call(
        body, name=name, grid_spec=grid_spec,
        out_shape=jax.ShapeDtypeStruct((N_CHIPS, hr, n), BF16),
        compiler_params=_params(("parallel", "parallel")),
    )(c_arr, g4, sib)


def _sum_pieces(pieces, received, place_arr, name):
    _, r, n = pieces.shape
    tr = _row_tile(r, n * 4, limit=1024 * 1024)

    def body(p_ref, own_ref, r0_ref, r1_ref, r2_ref, o_ref):
        del p_ref
        acc = own_ref[...].astype(F32) + r0_ref[...].astype(F32)
        acc = acc + r1_ref[...].astype(F32)
        o_ref[...] = acc + r2_ref[...].astype(F32)

    def recv_spec(j):
        return pl.BlockSpec((None, tr, n), lambda i, p_ref: (j, i, 0))

    grid_spec = pltpu.PrefetchScalarGridSpec(
        num_scalar_prefetch=1, grid=(r // tr,),
        in_specs=[pl.BlockSpec((None, tr, n), lambda i, p_ref: (p_ref[0], i, 0)),
                  recv_spec(0), recv_spec(1), recv_spec(2)],
        out_specs=pl.BlockSpec((None, tr, n), lambda i, p_ref: (p_ref[1], i, 0)))
    return pl.pallas_call(
        body, name=name, grid_spec=grid_spec,
        out_shape=jax.ShapeDtypeStruct((2, r, n), F32),
        compiler_params=_params(("parallel",)),
    )(place_arr, pieces, received, received, received)


def _norm_weights_step(parts, w, m, v, after=()):
    rows, d = parts.shape
    after = tuple(after)

    def body(p_ref, w_ref, m_ref, v_ref, *rest):
        g_ref, d_ref, mo_ref, vo_ref, gathered, send_sems, recv_sems = rest[len(after):]
        x, y, c = _place()
        me = 4 * x + 2 * y + c
        gathered[me] = p_ref[...]
        copies = []
        for k in range(1, N_DEV):
            peer = (x ^ ((k >> 2) & 1), y ^ ((k >> 1) & 1), c ^ (k & 1))
            copies.append(pltpu.make_async_remote_copy(
                src_ref=p_ref, dst_ref=gathered.at[me], send_sem=send_sems.at[k - 1],
                recv_sem=recv_sems.at[k - 1], device_id=peer, device_id_type=MESH))
        for cp in copies:
            cp.start()
        for cp in copies:
            cp.wait()
        g = gathered[0]
        for k in range(1, N_DEV):
            g = g + gathered[k]
        delta, m_new, v_new = _adamw_math(w_ref[...], g, m_ref[...], v_ref[...])
        g_ref[...] = g
        d_ref[...] = delta
        mo_ref[...] = m_new
        vo_ref[...] = v_new

    vmem = pl.BlockSpec(memory_space=pltpu.VMEM)
    shp = jax.ShapeDtypeStruct((rows, d), F32)
    return pl.pallas_call(
        body, name="norm_weights_step",
        in_specs=[vmem] * 4 + [ANY] * len(after), out_specs=[vmem] * 4, out_shape=[shp] * 4,
        scratch_shapes=[pltpu.VMEM((N_DEV, rows, d), F32), pltpu.SemaphoreType.DMA((N_DEV - 1,)),
                        pltpu.SemaphoreType.DMA((N_DEV - 1,))],
        compiler_params=pltpu.CompilerParams(has_side_effects=True),
    )(parts, w, m, v, *after)


def kernel(x, norm_mix_w, w_in, w_out, norm_ffn_w, w_gate, w_up, w_down, norm_final_w, loss_target, m_norm_mix_w, m_w_in, m_w_out, m_norm_ffn_w, m_w_gate, m_w_up, m_w_down, m_norm_final_w, v_norm_mix_w, v_w_in, v_w_out, v_norm_ffn_w, v_w_gate, v_w_up, v_w_down, v_norm_final_w):
    s, d = x.shape[1], x.shape[2]
    xs = x.reshape(s, d)
    target = loss_target.reshape(s, d)
    big = {"w_in": (w_in, m_w_in, v_w_in), "w_out": (w_out, m_w_out, v_w_out),
           "w_gate": (w_gate, m_w_gate, v_w_gate), "w_up": (w_up, m_w_up, v_w_up),
           "w_down": (w_down, m_w_down, v_w_down)}
    big = {k: tuple(a.reshape(a.shape[1:]) for a in t) for k, t in big.items()}
    col_names, row_names = ("w_in", "w_gate", "w_up"), ("w_out", "w_down")
    n_in = N_CHIPS * big["w_in"][0].shape[1]
    ffn = N_CHIPS * big["w_gate"][0].shape[1]
    mix = ATTN_WIDTH + RET_WIDTH
    c_arr = lax.axis_index("c").astype(I32).reshape(1)
    shard_arr = (2 * lax.axis_index("x") + lax.axis_index("y")).astype(I32).reshape(1)
    place_arr = jnp.concatenate([shard_arr, c_arr])

    order = ("w_in", "w_out", "w_gate", "w_up", "w_down")
    views = [_weight_view(_cast_into_full(big[k][0], shard_arr, k in col_names, "cast_" + k), k in col_names)
             for k in order]
    send_sems, recv_sems, v_in, v_out, v_gate, v_up, v_down = _gather_start(views)

    h1 = _rms_fwd(xs, norm_mix_w, "rms_mix_fwd")
    fs, fr, v_in = _gather_forward([v_in], [0], send_sems, recv_sems, h1, "gather_forward_in")
    wi, = _gather_end([v_in], fs, fr, h1, "gather_end_in")
    proj, = _matmul("in_proj", "nn", [h1], [wi], [0], s, n_in, d, s, 512, d, [], [F32], _epi_plain)
    fs, fr, v_out = _gather_forward([v_out], [1], send_sems, recv_sems, proj, "gather_forward_out")
    attn_b, attn_o, lse = _attn_fwd(proj)
    wo, = _gather_end([v_out], fs, fr, attn_o, "gather_end_out")
    fs, fr, v_gate, v_up = _gather_forward([v_gate, v_up], [2, 3], send_sems, recv_sems, attn_o,
                                           "gather_forward_gate_up")
    ret_b, ret_raw = _ret_fwd(proj)
    mixed = jnp.concatenate([attn_b, ret_b], axis=1)
    x1, = _matmul("out_proj", "nn", [mixed], [wo], [0], s, d, mix, s, 512, mix, [xs], [F32], _epi_residual)
    h2 = _rms_fwd(x1, norm_ffn_w, "rms_ffn_fwd")
    wg, wu = _gather_end([v_gate, v_up], fs, fr, h2, "gather_end_gate_up")
    gate, up, act = _matmul("gate_up", "nn", [h2, h2], [wg, wu], [0, 1], s, ffn, d, s, 256, d, [],
                            [F32, F32, BF16], _epi_swiglu)
    fs, fr, v_down = _gather_forward([v_down], [4], send_sems, recv_sems, act, "gather_forward_down")
    wd, = _gather_end([v_down], fs, fr, act, "gather_end_down")
    x2, = _matmul("down_proj", "nn", [act], [wd], [0], s, d, ffn, s, 512, ffn // 4, [x1], [F32],
                  _epi_residual)
    loss_row, dx2, dx2b, dwf = _final_norm_loss(x2, norm_final_w.reshape(1, d), target, "final_norm_loss")

    names = col_names + row_names
    grads, new = {}, {}

    def chip_sums(tag_names, views, sibs):
        return [(_chip_sum_col if k in col_names else _chip_sum_row)(v, sb, c_arr, "chip_sum_" + k)
                for k, v, sb in zip(tag_names, views, sibs)]

    def piece_sums(tag_names, pieces, received):
        return [_sum_pieces(p, r, place_arr, "sum_pieces_" + k) for k, p, r in zip(tag_names, pieces, received)]

    def update(k):
        new[k] = _adamw(big[k][0], grads[k], big[k][1], big[k][2], "adamw_" + k)

    dgate, dup = _matmul("d_act", "nt", [dx2b], [wd], [0], s, ffn, d, s, 256, d, [gate, up],
                         [BF16, BF16], _epi_swiglu_bwd)
    g_wd, = _matmul("g_w_down", "tn", [act], [dx2b], [0], ffn, d, s, 512, d, s, [], [BF16], _epi_plain)
    halves_d = _halves_start("down", [g_wd], [False])
    dh2, = _matmul("d_h2", "nt", [dgate, dup], [wg, wu], [0, 0], s, d, ffn, s, 1024, 512, [], [F32],
                   _epi_plain, after=halves_d[2][-1:])
    pieces_d = _pieces_start("down", chip_sums(["w_down"], *_halves_wait("down", halves_d, dh2)))
    g_wg, g_wu = _matmul("g_w_gate_up", "tn", [h2, h2], [dgate, dup], [0, 1], d, ffn, s, 1024, 512, s, [],
                         [BF16, BF16], _epi_two, after=pieces_d[2][-1:])
    halves_gu = _halves_start("gate_up", [g_wg, g_wu], [True, True])
    dx1, dx1b, dw_ffn = _rms_bwd(x1, norm_ffn_w, dh2, dx2, "rms_ffn_bwd", after=halves_gu[2][-1:])

    dmixed, = _matmul("d_mixed", "nt", [dx1b], [wo], [0], s, mix, d, s, 512, d, [], [F32], _epi_plain)
    pieces_gu = _pieces_start("gate_up", chip_sums(["w_gate", "w_up"], *_halves_wait("gate_up", halves_gu, dmixed)))
    g_wo, = _matmul("g_w_out", "tn", [mixed], [dx1b], [0], mix, d, s, 512, d, s, [], [BF16], _epi_plain,
                    after=pieces_gu[2][-1:])
    halves_o = _halves_start("out", [g_wo], [False])
    dqa, dka, dva = _attn_bwd(proj, attn_o, lse, dmixed, after=halves_o[2][-1:])
    pieces_o = _pieces_start("out", chip_sums(["w_out"], *_halves_wait("out", halves_o, dqa)))
    dqr, dkr, dvr, dgr = _ret_bwd(proj, ret_raw, dmixed, after=pieces_o[2][-1:])
    dproj = jnp.concatenate([dqa, dka, dva, dqr, dkr, dvr, dgr], axis=1)
    g_wi, = _matmul("g_w_in", "tn", [h1], [dproj], [0], d, n_in, s, 1024, 1024, s, [], [BF16], _epi_plain)
    halves_i = _halves_start("in", [g_wi], [True])
    dh1, = _matmul("d_h1", "nt", [dproj], [wi], [0], s, d, n_in, s, 1024, n_in // 7, [], [F32], _epi_plain,
                   after=halves_i[2][-1:])
    pieces_i = _pieces_start("in", chip_sums(["w_in"], *_halves_wait("in", halves_i, dh1)))
    grad_x, _, dw_mix = _rms_bwd(xs, norm_mix_w, dh1, dx1, "rms_mix_bwd", after=pieces_i[2][-1:])

    def rows8(*vs):
        return jnp.concatenate([v.reshape(1, d) for v in vs] + [jnp.zeros((8 - len(vs), d), F32)], axis=0)

    join_d = _join_start("down", piece_sums(["w_down"], *_pieces_wait("down", pieces_d, grad_x)))
    join_gu = _join_start("gate_up", piece_sums(["w_gate", "w_up"], *_pieces_wait("gate_up", pieces_gu, join_d[2][0])))
    join_o = _join_start("out", piece_sums(["w_out"], *_pieces_wait("out", pieces_o, join_gu[2][0])))
    grads["w_down"], = _join_wait("down", join_d, join_o[2][0])
    update("w_down")
    grads["w_gate"], grads["w_up"] = _join_wait("gate_up", join_gu, new["w_down"][0])
    update("w_gate")
    update("w_up")
    grads["w_out"], = _join_wait("out", join_o, new["w_up"][0])
    update("w_out")
    join_i = _join_start("in", piece_sums(["w_in"], *_pieces_wait("in", pieces_i, new["w_out"][0])))
    ng, nd, nm, nv = _norm_weights_step(
        rows8(dw_mix, dw_ffn, dwf), rows8(norm_mix_w, norm_ffn_w, norm_final_w),
        rows8(m_norm_mix_w, m_norm_ffn_w, m_norm_final_w), rows8(v_norm_mix_w, v_norm_ffn_w, v_norm_final_w),
        after=join_i[2][:1])
    grads["w_in"], = _join_wait("in", join_i, ng)
    update("w_in")

    loss = lax.psum(loss_row[0, 0], ("x", "y", "c"))

    def pack(small, per_weight):
        lead = lambda a: a.reshape((1,) + a.shape)
        return (small[0:1], lead(per_weight["w_in"]), lead(per_weight["w_out"]), small[1:2],
                lead(per_weight["w_gate"]), lead(per_weight["w_up"]), lead(per_weight["w_down"]), small[2])

    return (loss, grad_x.reshape(1, s, d),
            *pack(ng, grads),
            *pack(nd, {k: new[k][0] for k in names}),
            *pack(nm, {k: new[k][1] for k in names}),
            *pack(nv, {k: new[k][2] for k in names}))
```

```python
import functools
import math

import jax
import jax.numpy as jnp
from jax import lax
from jax.experimental import pallas as pl
from jax.experimental.pallas import tpu as pltpu

F32 = jnp.float32
BF16 = jnp.bfloat16
I32 = jnp.int32
MESH = pl.DeviceIdType.MESH
ANY = pl.BlockSpec(memory_space=pl.ANY)

ATTN_HEADS = 8
ATTN_HEAD_DIM = 128
RET_HEADS = 4
RET_HEAD_DIM = 256
ATTN_WIDTH = ATTN_HEADS * ATTN_HEAD_DIM
RET_WIDTH = RET_HEADS * RET_HEAD_DIM
DILATED_PATTERNS = ((128, 1), (512, 4), (2048, 16))
NORM_EPS = 1e-6
ADAM_LR = 0.001
ADAM_B1 = 0.9
ADAM_B2 = 0.999
ADAM_EPS = 1e-08
ADAM_WD = 0.01
ADAM_STEP = 10

N_CHIPS = 4
N_DEV = 8
NEG_BIG = -1e30
SEQ_TILE = 512
ATTN_HEADS_PER_STEP = 1
VMEM_LIMIT_BYTES = 56 * 1024 * 1024


def _params(semantics=None, vmem=VMEM_LIMIT_BYTES):
    return pltpu.CompilerParams(dimension_semantics=semantics, vmem_limit_bytes=vmem)


def _row_tile(rows, row_bytes, limit=2 * 1024 * 1024, mult=16):
    best = None
    for t in range(mult, rows + 1, mult):
        if rows % t == 0 and t * row_bytes <= limit:
            best = t
    assert best is not None, (rows, row_bytes)
    return best


def _sigmoid(x):
    return 1.0 / (1.0 + jnp.exp(-x))


def _select_by_index(idx, values):
    out = jnp.float32(values[-1])
    for i in range(len(values) - 2, -1, -1):
        out = jnp.where(idx == i, jnp.float32(values[i]), out)
    return out


def _place():
    x, y, c = lax.axis_index("x"), lax.axis_index("y"), lax.axis_index("c")
    return x, y, c


def _cast_into_full(w, shard_arr, column_sharded, name, after=()):
    after = tuple(after)
    rows, cols = w.shape
    tr = _row_tile(rows, cols * 4)
    steps = rows // tr
    if column_sharded:
        out_shape, out_map = (rows, N_CHIPS * cols), (lambda i, s_ref: (i, s_ref[0]))
    else:
        out_shape, out_map = (N_CHIPS * rows, cols), (lambda i, s_ref: (s_ref[0] * steps + i, 0))

    def body(s_ref, w_ref, *rest):
        del s_ref
        rest[-1][...] = w_ref[...].astype(BF16)

    grid_spec = pltpu.PrefetchScalarGridSpec(
        num_scalar_prefetch=1, grid=(steps,),
        in_specs=[pl.BlockSpec((tr, cols), lambda i, s_ref: (i, 0))] + [ANY] * len(after),
        out_specs=pl.BlockSpec((tr, cols), out_map))
    return pl.pallas_call(
        body, name=name, grid_spec=grid_spec,
        out_shape=jax.ShapeDtypeStruct(out_shape, BF16),
        compiler_params=_params(("parallel",)),
    )(shard_arr, w, *after)


def _rms_fwd(x, w, name):
    rows, d = x.shape
    tr = 256

    def body(x_ref, w_ref, h_ref):
        xv = x_ref[...]
        r = lax.rsqrt(jnp.mean(xv * xv, axis=-1, keepdims=True) + NORM_EPS)
        h_ref[...] = (xv * r * w_ref[...]).astype(BF16)

    return pl.pallas_call(
        body, name=name, grid=(rows // tr,),
        in_specs=[pl.BlockSpec((tr, d), lambda i: (i, 0)), pl.BlockSpec((1, d), lambda i: (0, 0))],
        out_specs=pl.BlockSpec((tr, d), lambda i: (i, 0)),
        out_shape=jax.ShapeDtypeStruct((rows, d), BF16),
        compiler_params=_params(("parallel",)),
    )(x, w)


def _rms_bwd(x, w, dh, dres, name, after=()):
    rows, d = x.shape
    tr = 256
    after = tuple(after)

    def body(x_ref, w_ref, dh_ref, dres_ref, *rest):
        dx_ref, dxb_ref, dw_ref = rest[len(after):]
        xv = x_ref[...]
        r = lax.rsqrt(jnp.mean(xv * xv, axis=-1, keepdims=True) + NORM_EPS)
        xhat = xv * r
        dy = dh_ref[...]
        dxhat = dy * w_ref[...]
        dx = dres_ref[...] + r * (dxhat - xhat * jnp.mean(dxhat * xhat, axis=-1, keepdims=True))
        dx_ref[...] = dx
        dxb_ref[...] = dx.astype(BF16)
        part = jnp.sum(dy * xhat, axis=0, keepdims=True)

        @pl.when(pl.program_id(0) == 0)
        def _():
            dw_ref[...] = part

        @pl.when(pl.program_id(0) != 0)
        def _():
            dw_ref[...] += part

    row = pl.BlockSpec((tr, d), lambda i: (i, 0))
    vec = pl.BlockSpec((1, d), lambda i: (0, 0))
    return pl.pallas_call(
        body, name=name, grid=(rows // tr,),
        in_specs=[row, vec, row, row] + [ANY] * len(after),
        out_specs=[row, row, vec],
        out_shape=[jax.ShapeDtypeStruct((rows, d), F32), jax.ShapeDtypeStruct((rows, d), BF16),
                   jax.ShapeDtypeStruct((1, d), F32)],
        compiler_params=_params(("arbitrary",)),
    )(x, w, dh, dres, *after)


def _final_norm_loss(x2, w, target, name):
    rows, d = x2.shape
    tr = 256

    def body(x_ref, w_ref, t_ref, loss_ref, dx_ref, dxb_ref, dw_ref):
        xv = x_ref[...]
        wv = w_ref[...]
        r = lax.rsqrt(jnp.mean(xv * xv, axis=-1, keepdims=True) + NORM_EPS)
        xhat = xv * r
        err = xhat * wv - t_ref[...]
        part_loss = 0.5 * jnp.sum(jnp.mean(err * err, axis=-1, keepdims=True), axis=0, keepdims=True)
        dy = err * (1.0 / d)
        dxhat = dy * wv
        dx = r * (dxhat - xhat * jnp.mean(dxhat * xhat, axis=-1, keepdims=True))
        dx_ref[...] = dx
        dxb_ref[...] = dx.astype(BF16)
        part_dw = jnp.sum(dy * xhat, axis=0, keepdims=True)
        part_loss = jnp.broadcast_to(part_loss, (1, 128))

        @pl.when(pl.program_id(0) == 0)
        def _():
            dw_ref[...] = part_dw
            loss_ref[...] = part_loss

        @pl.when(pl.program_id(0) != 0)
        def _():
            dw_ref[...] += part_dw
            loss_ref[...] += part_loss

    row = pl.BlockSpec((tr, d), lambda i: (i, 0))
    vec = pl.BlockSpec((1, d), lambda i: (0, 0))
    return pl.pallas_call(
        body, name=name, grid=(rows // tr,),
        in_specs=[row, vec, row],
        out_specs=[pl.BlockSpec((1, 128), lambda i: (0, 0)), row, row, vec],
        out_shape=[jax.ShapeDtypeStruct((1, 128), F32), jax.ShapeDtypeStruct((rows, d), F32),
                   jax.ShapeDtypeStruct((rows, d), BF16), jax.ShapeDtypeStruct((1, d), F32)],
        compiler_params=_params(("arbitrary",)),
    )(x2, w, target)


def _adamw_math(w, g, m, v):
    m = ADAM_B1 * m + (1.0 - ADAM_B1) * g
    v = ADAM_B2 * v + (1.0 - ADAM_B2) * (g * g)
    m_hat = m / (1.0 - ADAM_B1 ** ADAM_STEP)
    v_hat = v / (1.0 - ADAM_B2 ** ADAM_STEP)
    delta = -ADAM_LR * (m_hat / (jnp.sqrt(v_hat) + ADAM_EPS) + ADAM_WD * w)
    return delta, m, v


def _adamw(w, g, m, v, name):
    rows, cols = w.shape
    tr = _row_tile(rows, cols * 4, limit=1024 * 1024)

    def body(w_ref, g_ref, m_ref, v_ref, d_ref, mo_ref, vo_ref, go_ref):
        g = g_ref[...]
        delta, m_new, v_new = _adamw_math(w_ref[...], g, m_ref[...], v_ref[...])
        d_ref[...] = delta
        mo_ref[...] = m_new
        vo_ref[...] = v_new
        go_ref[...] = g

    blk = pl.BlockSpec((tr, cols), lambda i: (i, 0))
    shp = jax.ShapeDtypeStruct((rows, cols), F32)
    return pl.pallas_call(
        body, name=name, grid=(rows // tr,),
        in_specs=[blk] * 4, out_specs=[blk] * 4, out_shape=[shp] * 4,
        compiler_params=_params(("parallel",)),
    )(w, g, m, v)


_DOT_DIMS = {"nn": ((1,), (0,)), "nt": ((1,), (1,)), "tn": ((0,), (0,))}


def _matmul(name, mode, a_list, b_list, acc_of, m, n, k, tm, tn, tk, extras, out_dtypes, epilogue,
            a_koff=None, b_koff=None, after=(), a_specs=None, b_specs=None):
    after = tuple(after)
    assert m % tm == 0 and n % tn == 0 and k % tk == 0, (name, m, n, k, tm, tn, tk)
    nk = k // tk
    n_acc = max(acc_of) + 1
    n_pairs = len(a_list)
    a_koff = a_koff or [0] * n_pairs
    b_koff = b_koff or [0] * n_pairs
    dims = (_DOT_DIMS[mode], ((), ()))
    n_ext, n_out = len(extras), len(out_dtypes)

    def body(*refs):
        a_refs = refs[:n_pairs]
        b_refs = refs[n_pairs:2 * n_pairs]
        e_refs = refs[2 * n_pairs:2 * n_pairs + n_ext]
        first_out = 2 * n_pairs + n_ext + len(after)
        o_refs = refs[first_out:first_out + n_out]
        acc_refs = refs[first_out + n_out:]
        parts = [None] * n_acc
        for p in range(n_pairs):
            d = lax.dot_general(a_refs[p][...], b_refs[p][...], dims, preferred_element_type=F32)
            parts[acc_of[p]] = d if parts[acc_of[p]] is None else parts[acc_of[p]] + d

        def finish(accs):
            outs = epilogue(accs, [e[...] for e in e_refs])
            for o_ref, o in zip(o_refs, outs):
                o_ref[...] = o.astype(o_ref.dtype)

        if nk == 1:
            finish(parts)
        else:
            kk = pl.program_id(2)

            @pl.when(kk == 0)
            def _():
                for acc_ref, part in zip(acc_refs, parts):
                    acc_ref[...] = part

            @pl.when(kk != 0)
            def _():
                for acc_ref, part in zip(acc_refs, parts):
                    acc_ref[...] += part

            @pl.when(kk == nk - 1)
            def _():
                finish([acc_ref[...] for acc_ref in acc_refs])

    def a_spec(off):
        if mode == "tn":
            return pl.BlockSpec((tk, tm), lambda i, j, kk: (kk + off, i))
        return pl.BlockSpec((tm, tk), lambda i, j, kk: (i, kk + off))

    def b_spec(off):
        if mode == "nt":
            return pl.BlockSpec((tn, tk), lambda i, j, kk: (j, kk + off))
        return pl.BlockSpec((tk, tn), lambda i, j, kk: (kk + off, j))

    tile = pl.BlockSpec((tm, tn), lambda i, j, kk: (i, j))
    scratch = [pltpu.VMEM((tm, tn), F32) for _ in range(n_acc)] if nk > 1 else []
    return pl.pallas_call(
        body, name=name, grid=(m // tm, n // tn, nk),
        in_specs=(a_specs or [a_spec(o) for o in a_koff]) + (b_specs or [b_spec(o) for o in b_koff])
        + [tile] * n_ext + [ANY] * len(after),
        out_specs=[tile] * n_out,
        out_shape=[jax.ShapeDtypeStruct((m, n), dt) for dt in out_dtypes],
        scratch_shapes=scratch,
        compiler_params=_params(("parallel", "parallel", "arbitrary")),
    )(*a_list, *b_list, *extras, *after)


def _epi_plain(accs, extras):
    return (accs[0],)


def _epi_residual(accs, extras):
    return (accs[0] + extras[0],)


def _epi_two(accs, extras):
    return accs[0], accs[1]


def _epi_swiglu(accs, extras):
    g, u = accs
    return g, u, g * _sigmoid(g) * u


def _epi_swiglu_bwd(accs, extras):
    da = accs[0]
    g, u = extras
    sg = _sigmoid(g)
    dg = da * u * sg * (1.0 + g * (1.0 - sg))
    du = da * g * sg
    return dg, du


_NT_DIMS = (((1,), (1,)), ((), ()))
_TN_DIMS = (((0,), (0,)), ((), ()))


def _tile_delta(tq, tk):
    return lax.broadcasted_iota(I32, (tq, tk), 0) - lax.broadcasted_iota(I32, (tq, tk), 1)


def _attn_mask_bias(delta, slope):
    count = jnp.zeros(delta.shape, I32)
    for window, dilation in DILATED_PATTERNS:
        hit = ((delta & (dilation - 1)) == 0) & (delta <= window)
        count = count + jnp.where(hit, 1, 0)
    valid = (delta >= 0) & (count > 0)
    logm = jnp.where(count == 3, math.log(3.0), jnp.where(count == 2, math.log(2.0), 0.0))
    return valid, logm - slope * delta.astype(F32)


def _fill_attn_bias(tab_ref, slope):
    nb, t, _ = tab_ref.shape
    base = _tile_delta(t, t)
    for b in range(nb):
        valid, bias = _attn_mask_bias(base + b * t, slope)
        tab_ref[b] = jnp.where(valid, bias, NEG_BIG)


def _fill_ret_decay(tab_ref, log_gamma):
    nb, t, _ = tab_ref.shape
    base = _tile_delta(t, t)
    for b in range(nb):
        tab_ref[b] = _ret_decay(base + b * t, log_gamma)


def _alibi_slopes():
    return [2.0 ** (-8.0 * (h + 1) / ATTN_HEADS) for h in range(ATTN_HEADS)]


def _attn_fwd(proj):
    s = proj.shape[0]
    t = SEQ_TILE
    hd = ATTN_HEAD_DIM
    hp = ATTN_HEADS_PER_STEP
    ng = ATTN_HEADS // hp
    w = hp * hd
    scale = 1.0 / math.sqrt(hd)
    slopes = _alibi_slopes()

    def body(q_ref, k_ref, v_ref, mix_ref, o_ref, lse_ref, kb, vb, bias_tab):
        g = pl.program_id(0)
        i = pl.program_id(1)

        @pl.when(i == 0)
        def _():
            kb[...] = k_ref[...].astype(BF16)
            vb[...] = v_ref[...].astype(BF16)
            for u in range(hp):
                _fill_attn_bias(bias_tab.at[u], _select_by_index(g * hp + u, slopes))

        qs = [q_ref[:, u * hd:(u + 1) * hd].astype(BF16) for u in range(hp)]

        def step(j, carry):
            rows = pl.ds(pl.multiple_of(j * t, t), t)
            out = []
            for u in range(hp):
                m_i, l_i, acc = carry[u]
                lanes = slice(u * hd, (u + 1) * hd)
                sc = lax.dot_general(qs[u], kb[rows, lanes], _NT_DIMS, preferred_element_type=F32) * scale
                sc = sc + bias_tab[u, i - j]
                m_new = jnp.maximum(m_i, jnp.max(sc, axis=-1, keepdims=True))
                p = jnp.exp(sc - m_new)
                alpha = jnp.exp(m_i - m_new)
                l_new = alpha * l_i + jnp.sum(p, axis=-1, keepdims=True)
                acc = alpha * acc + jnp.dot(p.astype(BF16), vb[rows, lanes], preferred_element_type=F32)
                out.append((m_new, l_new, acc))
            return tuple(out)

        init = (jnp.full((t, 1), NEG_BIG, F32), jnp.zeros((t, 1), F32), jnp.zeros((t, hd), F32))
        final = lax.fori_loop(0, i + 1, step, (init,) * hp)
        for u in range(hp):
            m_i, l_i, acc = final[u]
            lanes = slice(u * hd, (u + 1) * hd)
            out = acc / l_i
            o_ref[:, lanes] = out
            mix_ref[:, lanes] = out.astype(BF16)
            lse_ref[:, lanes] = jnp.broadcast_to(m_i + jnp.log(l_i), (t, hd))

    return pl.pallas_call(
        body, name="attn_fwd", grid=(ng, s // t),
        in_specs=[pl.BlockSpec((t, w), lambda g, i: (i, g)),
                  pl.BlockSpec((s, w), lambda g, i: (0, ng + g)),
                  pl.BlockSpec((s, w), lambda g, i: (0, 2 * ng + g))],
        out_specs=[pl.BlockSpec((None, t, w), lambda g, i: (0, i, g))] + [pl.BlockSpec((t, w), lambda g, i: (i, g))] * 2,
        out_shape=[jax.ShapeDtypeStruct((2, s, ATTN_WIDTH), BF16),
                   jax.ShapeDtypeStruct((s, ATTN_WIDTH), F32),
                   jax.ShapeDtypeStruct((s, ATTN_WIDTH), F32)],
        scratch_shapes=[pltpu.VMEM((s, w), BF16), pltpu.VMEM((s, w), BF16), pltpu.VMEM((hp, s // t, t, t), F32)],
        compiler_params=_params(("arbitrary", "arbitrary")),
    )(proj, proj, proj)


def _attn_bwd(proj, attn_out, lse, dmixed, after=()):
    after = tuple(after)
    s = proj.shape[0]
    t = SEQ_TILE
    nt = s // t
    hd = ATTN_HEAD_DIM
    hp = ATTN_HEADS_PER_STEP
    ng = ATTN_HEADS // hp
    w = hp * hd
    scale = 1.0 / math.sqrt(hd)
    slopes = _alibi_slopes()

    def body(q_ref, k_ref, v_ref, o_ref, lse_ref, do_ref, *rest):
        dsec_ref, qb, kb, vb, dob, dsum, dq_acc, bias_tab = rest[len(after):]
        g = pl.program_id(0)
        qb[...] = q_ref[...].astype(BF16)
        kb[...] = k_ref[...].astype(BF16)
        vb[...] = v_ref[...].astype(BF16)
        dob[...] = do_ref[...].astype(BF16)
        for u in range(hp):
            lanes = slice(u * hd, (u + 1) * hd)
            _fill_attn_bias(bias_tab.at[u], _select_by_index(g * hp + u, slopes))
            rowsum = jnp.sum(do_ref[:, lanes] * o_ref[:, lanes], axis=-1, keepdims=True)
            dsum[:, lanes] = jnp.broadcast_to(rowsum, (s, hd))
        dq_acc[...] = jnp.zeros((s, w), F32)

        def over_keys(j, _):
            krows = pl.ds(pl.multiple_of(j * t, t), t)

            def over_queries(i, carry):
                qrows = pl.ds(pl.multiple_of(i * t, t), t)
                out = []
                for u in range(hp):
                    dk, dv = carry[u]
                    lanes = slice(u * hd, (u + 1) * hd)
                    qi, doi = qb[qrows, lanes], dob[qrows, lanes]
                    kj, vj = kb[krows, lanes], vb[krows, lanes]
                    lse_i = lse_ref[qrows, lanes][:, :1]
                    dsum_i = dsum[qrows, lanes][:, :1]
                    sc = lax.dot_general(qi, kj, _NT_DIMS, preferred_element_type=F32) * scale
                    p = jnp.exp(sc + bias_tab[u, i - j] - lse_i)
                    dp = lax.dot_general(doi, vj, _NT_DIMS, preferred_element_type=F32)
                    ds = (p * (dp - dsum_i)).astype(BF16)
                    dv = dv + lax.dot_general(p.astype(BF16), doi, _TN_DIMS, preferred_element_type=F32)
                    dk = dk + lax.dot_general(ds, qi, _TN_DIMS, preferred_element_type=F32)
                    dq_acc[qrows, lanes] += jnp.dot(ds, kj, preferred_element_type=F32)
                    out.append((dk, dv))
                return tuple(out)

            zero = jnp.zeros((t, hd), F32)
            final = lax.fori_loop(j, nt, over_queries, ((zero, zero),) * hp)
            for u in range(hp):
                lanes = slice(u * hd, (u + 1) * hd)
                dsec_ref[1, krows, lanes] = (final[u][0] * scale).astype(BF16)
                dsec_ref[2, krows, lanes] = final[u][1].astype(BF16)
            return 0

        lax.fori_loop(0, nt, over_keys, 0)
        dsec_ref[0] = (dq_acc[...] * scale).astype(BF16)

    def col(off):
        return pl.BlockSpec((s, w), lambda g: (0, off + g))

    return pl.pallas_call(
        body, name="attn_bwd", grid=(ng,),
        in_specs=[col(0), col(ng), col(2 * ng), col(0), col(0), col(0)] + [ANY] * len(after),
        out_specs=pl.BlockSpec((4, s, w), lambda g: (0, 0, g)),
        out_shape=jax.ShapeDtypeStruct((8, s, ATTN_WIDTH), BF16),
        scratch_shapes=[pltpu.VMEM((s, w), BF16)] * 4 + [pltpu.VMEM((s, w), F32)] * 2
        + [pltpu.VMEM((hp, nt, t, t), F32)],
        compiler_params=_params(("arbitrary",)),
    )(proj, proj, proj, attn_out, lse, dmixed, *after)


def _ret_log_gammas():
    return [math.log(1.0 - 2.0 ** (-5.0 - h)) for h in range(RET_HEADS)]


def _ret_decay(delta, log_gamma):
    dec = jnp.exp(delta.astype(F32) * log_gamma) * (1.0 / math.sqrt(RET_HEAD_DIM))
    return jnp.where(delta >= 0, dec, 0.0)


def _ret_fwd(proj, mixed, after=()):
    after = tuple(after)
    s = proj.shape[0]
    t = SEQ_TILE
    hd = RET_HEAD_DIM
    nh = RET_HEADS
    log_gammas = _ret_log_gammas()
    c0 = 3 * ATTN_WIDTH // hd

    def body(q_ref, k_ref, v_ref, g_ref, *rest):
        mix_ref, raw_ref, kb, vb, decay_tab = rest[1 + len(after):]
        h = pl.program_id(0)
        i = pl.program_id(1)

        @pl.when(i == 0)
        def _():
            kb[...] = k_ref[...].astype(BF16)
            vb[...] = v_ref[...].astype(BF16)
            _fill_ret_decay(decay_tab, _select_by_index(h, log_gammas))

        q = q_ref[...].astype(BF16)

        def step(j, acc):
            rows = pl.ds(pl.multiple_of(j * t, t), t)
            sc = lax.dot_general(q, kb[rows, :], _NT_DIMS, preferred_element_type=F32) * decay_tab[i - j]
            return acc + jnp.dot(sc.astype(BF16), vb[rows, :], preferred_element_type=F32)

        ret = lax.fori_loop(0, i + 1, step, jnp.zeros((t, hd), F32))
        raw_ref[...] = ret
        r = lax.rsqrt(jnp.mean(ret * ret, axis=-1, keepdims=True) + NORM_EPS)
        g = g_ref[...]
        mix_ref[...] = (g * _sigmoid(g) * (ret * r)).astype(BF16)

    return pl.pallas_call(
        body, name="ret_fwd", grid=(nh, s // t),
        in_specs=[pl.BlockSpec((t, hd), lambda h, i: (i, c0 + h)),
                  pl.BlockSpec((s, hd), lambda h, i: (0, c0 + nh + h)),
                  pl.BlockSpec((s, hd), lambda h, i: (0, c0 + 2 * nh + h)),
                  pl.BlockSpec((t, hd), lambda h, i: (i, c0 + 3 * nh + h))] + [ANY] * (1 + len(after)),
        out_specs=[pl.BlockSpec((None, t, hd), lambda h, i: (1, i, h)), pl.BlockSpec((t, hd), lambda h, i: (i, h))],
        out_shape=[jax.ShapeDtypeStruct(mixed.shape, BF16), jax.ShapeDtypeStruct((s, RET_WIDTH), F32)],
        input_output_aliases={4: 0},
        scratch_shapes=[pltpu.VMEM((s, hd), BF16), pltpu.VMEM((s, hd), BF16), pltpu.VMEM((s // t, t, t), F32)],
        compiler_params=_params(("arbitrary", "arbitrary")),
    )(proj, proj, proj, proj, mixed, *after)


def _ret_bwd(proj, ret_raw, dmixed, dsec, after=()):
    after = tuple(after)
    s = proj.shape[0]
    t = SEQ_TILE
    nt = s // t
    hd = RET_HEAD_DIM
    nh = RET_HEADS
    log_gammas = _ret_log_gammas()
    c0 = 3 * ATTN_WIDTH // hd
    mixed_blocks = ATTN_WIDTH // hd

    def body(q_ref, k_ref, v_ref, g_ref, raw_ref, dmix_ref, *rest):
        dsec_ref, qb, kb, vb, dretb, dq_acc, decay_tab = rest[1 + len(after):]
        h = pl.program_id(0)
        _fill_ret_decay(decay_tab, _select_by_index(h, log_gammas))
        qb[...] = q_ref[...].astype(BF16)
        kb[...] = k_ref[...].astype(BF16)
        vb[...] = v_ref[...].astype(BF16)
        ret = raw_ref[...]
        r = lax.rsqrt(jnp.mean(ret * ret, axis=-1, keepdims=True) + NORM_EPS)
        normed = ret * r
        g = g_ref[...]
        sg = _sigmoid(g)
        dout = dmix_ref[...]
        dsec_ref[3] = (dout * normed * sg * (1.0 + g * (1.0 - sg))).astype(BF16)
        dn = dout * g * sg
        dret = r * (dn - normed * jnp.mean(dn * normed, axis=-1, keepdims=True))
        dretb[...] = dret.astype(BF16)
        dq_acc[...] = jnp.zeros((s, hd), F32)

        def over_keys(j, _):
            krows = pl.ds(pl.multiple_of(j * t, t), t)
            kj = kb[krows, :]
            vj = vb[krows, :]

            def over_queries(i, carry):
                dk, dv = carry
                qrows = pl.ds(pl.multiple_of(i * t, t), t)
                qi = qb[qrows, :]
                doi = dretb[qrows, :]
                dec = decay_tab[i - j]
                a = (lax.dot_general(qi, kj, _NT_DIMS, preferred_element_type=F32) * dec).astype(BF16)
                da = (lax.dot_general(doi, vj, _NT_DIMS, preferred_element_type=F32) * dec).astype(BF16)
                dv = dv + lax.dot_general(a, doi, _TN_DIMS, preferred_element_type=F32)
                dk = dk + lax.dot_general(da, qi, _TN_DIMS, preferred_element_type=F32)
                dq_acc[qrows, :] += jnp.dot(da, kj, preferred_element_type=F32)
                return dk, dv

            zero = jnp.zeros((t, hd), F32)
            dk, dv = lax.fori_loop(j, nt, over_queries, (zero, zero))
            dsec_ref[1, krows, :] = dk.astype(BF16)
            dsec_ref[2, krows, :] = dv.astype(BF16)
            return 0

        lax.fori_loop(0, nt, over_keys, 0)
        dsec_ref[0] = dq_acc[...].astype(BF16)

    def col(off):
        return pl.BlockSpec((s, hd), lambda h: (0, off + h))

    return pl.pallas_call(
        body, name="ret_bwd", grid=(nh,),
        in_specs=[col(c0), col(c0 + nh), col(c0 + 2 * nh), col(c0 + 3 * nh), col(0), col(mixed_blocks)]
        + [ANY] * (1 + len(after)),
        out_specs=pl.BlockSpec((4, s, hd), lambda h: (1, 0, h)),
        out_shape=jax.ShapeDtypeStruct(dsec.shape, BF16),
        input_output_aliases={6: 0},
        scratch_shapes=[pltpu.VMEM((s, hd), BF16)] * 4 + [pltpu.VMEM((s, hd), F32)]
        + [pltpu.VMEM((nt, t, t), F32)],
        compiler_params=_params(("arbitrary",)),
    )(proj, proj, proj, proj, ret_raw, dmixed, dsec, *after)


_FLIPS = (2, 1, 3)


def _other_chips(x, y):
    return [(1 - x, y), (x, 1 - y), (1 - x, 1 - y)]


_HBM = pl.BlockSpec(memory_space=pltpu.HBM)
_SEM = pl.BlockSpec(memory_space=pltpu.SEMAPHORE)
_EFFECT = pltpu.SideEffectType.DATAFLOW_SIDE_EFFECTING


def _in_hbm(a):
    return pltpu.with_memory_space_constraint(a, pltpu.HBM)


def _weight_view(w, column_sharded):
    if column_sharded:
        return w.reshape(2, w.shape[0] // 2, w.shape[1])
    return w.reshape(N_CHIPS, 2, w.shape[0] // (2 * N_CHIPS), w.shape[1])


def _weight_unview(v):
    if v.ndim == 3:
        return v.reshape(2 * v.shape[1], v.shape[2])
    return v.reshape(N_CHIPS * 2 * v.shape[2], v.shape[3])


def _weight_region(buf, shard, half):
    if len(buf.shape) == 3:
        cols = buf.shape[2] // N_CHIPS
        return buf.at[half, :, pl.ds(shard * cols, cols)]
    return buf.at[shard, half]


def _remote(where, send_sem, recv_sem, to):
    return pltpu.make_async_remote_copy(src_ref=where, dst_ref=where, send_sem=send_sem, recv_sem=recv_sem,
                                        device_id=to, device_id_type=MESH)


def _for_my_shard(fn):
    x, y, _ = _place()
    for ss in range(N_CHIPS):
        pl.when(2 * x + y == ss)(functools.partial(fn, ss))


def _gather_start(views, name):
    n_w = len(views)

    def body(*refs):
        send_sems, recv_sems = refs[n_w:n_w + 2]
        bufs = refs[n_w + 2:]
        x, y, c = _place()

        def start(ss):
            for w in range(n_w):
                for j, chip in enumerate(_other_chips(x, y)):
                    _remote(_weight_region(bufs[w], ss, c), send_sems.at[3 * w + j], recv_sems.at[3 * w + j],
                            (*chip, c)).start()

        _for_my_shard(start)

    return pl.pallas_call(
        body, name=name,
        in_specs=[_HBM] * n_w, out_specs=[_SEM, _SEM] + [_HBM] * n_w,
        out_shape=[pltpu.SemaphoreType.DMA((3 * n_w,)), pltpu.SemaphoreType.DMA((3 * n_w,))]
        + [pltpu.HBM(v.shape, BF16) for v in views],
        input_output_aliases={w: 2 + w for w in range(n_w)},
        compiler_params=pltpu.CompilerParams(has_side_effects=_EFFECT),
    )(*[_in_hbm(v) for v in views])


def _gather_forward(views, which, send_sems, recv_sems, after, name):
    n_w = len(views)

    def body(*refs):
        send_in, recv_in = refs[n_w:n_w + 2]
        fwd_send, fwd_recv = refs[n_w + 3:n_w + 5]
        bufs = refs[n_w + 5:]
        x, y, c = _place()
        sibling = (x, y, 1 - c)

        def forward(ss):
            for i, w in enumerate(which):
                for j in range(3):
                    landed = _weight_region(bufs[i], ss ^ _FLIPS[j], c)
                    _remote(landed, send_in.at[3 * w + j], recv_in.at[3 * w + j], sibling).wait_recv()
                    _remote(landed, fwd_send.at[3 * i + j], fwd_recv.at[3 * i + j], sibling).start()

        _for_my_shard(forward)
        for i, w in enumerate(which):
            for j in range(3):
                _remote(_weight_region(bufs[i], 0, 0), send_in.at[3 * w + j], recv_in.at[3 * w + j],
                        sibling).wait_send()

    return pl.pallas_call(
        body, name=name,
        in_specs=[_HBM] * n_w + [_SEM, _SEM, ANY], out_specs=[_SEM, _SEM] + [_HBM] * n_w,
        out_shape=[pltpu.SemaphoreType.DMA((3 * n_w,)), pltpu.SemaphoreType.DMA((3 * n_w,))]
        + [pltpu.HBM(v.shape, BF16) for v in views],
        input_output_aliases={w: 2 + w for w in range(n_w)},
        compiler_params=pltpu.CompilerParams(has_side_effects=_EFFECT),
    )(*views, send_sems, recv_sems, after)


def _gather_end(views, fwd_send, fwd_recv, after, name):
    n_w = len(views)

    def body(*refs):
        fwd_send_ref, fwd_recv_ref = refs[n_w:n_w + 2]
        bufs = refs[n_w + 3:]
        x, y, c = _place()
        for i in range(n_w):
            for j in range(3):
                cp = _remote(_weight_region(bufs[i], 0, 0), fwd_send_ref.at[3 * i + j], fwd_recv_ref.at[3 * i + j],
                             (x, y, 1 - c))
                cp.wait_recv()
                cp.wait_send()

    outs = pl.pallas_call(
        body, name=name,
        in_specs=[_HBM] * n_w + [_SEM, _SEM, ANY], out_specs=[_HBM] * n_w,
        out_shape=[pltpu.HBM(v.shape, BF16) for v in views],
        input_output_aliases={w: w for w in range(n_w)},
        compiler_params=pltpu.CompilerParams(has_side_effects=_EFFECT),
    )(*views, fwd_send, fwd_recv, after)
    return [_weight_unview(o) for o in outs]


def _split_start(name, bufs, n_sems, copies):
    n = len(bufs)

    def body(*refs):
        send_sems, recv_sems = refs[n:n + 2]
        for cp in copies(refs[n + 2:], send_sems, recv_sems):
            cp.start()

    outs = pl.pallas_call(
        body, name=name,
        in_specs=[_HBM] * n, out_specs=[_SEM, _SEM] + [_HBM] * n,
        out_shape=[pltpu.SemaphoreType.DMA((n_sems,)), pltpu.SemaphoreType.DMA((n_sems,))]
        + [pltpu.HBM(b.shape, b.dtype) for b in bufs],
        input_output_aliases={i: 2 + i for i in range(n)},
        compiler_params=pltpu.CompilerParams(has_side_effects=_EFFECT),
    )(*[_in_hbm(b) for b in bufs])
    return outs[0], outs[1], list(outs[2:])


def _split_wait(name, bufs, send_sems, recv_sems, copies, after):
    n = len(bufs)

    def body(*refs):
        send_ref, recv_ref = refs[n:n + 2]
        for cp in copies(refs[n + 3:], send_ref, recv_ref):
            cp.wait()

    return list(pl.pallas_call(
        body, name=name,
        in_specs=[_HBM] * n + [_SEM, _SEM, ANY], out_specs=[_HBM] * n,
        out_shape=[pltpu.HBM(b.shape, b.dtype) for b in bufs],
        input_output_aliases={i: i for i in range(n)},
        compiler_params=pltpu.CompilerParams(has_side_effects=_EFFECT),
    )(*bufs, send_sems, recv_sems, after))


def _halves_copies(n_w):
    def copies(bufs, send_sems, recv_sems):
        x, y, c = _place()
        out = []
        for w in range(n_w):
            view, land = bufs[w], bufs[n_w + w]
            src = view.at[1 - c] if len(view.shape) == 3 else view.at[:, 1 - c]
            out.append(pltpu.make_async_remote_copy(
                src_ref=src, dst_ref=land, send_sem=send_sems.at[w], recv_sem=recv_sems.at[w],
                device_id=(x, y, 1 - c), device_id_type=MESH))
        return out
    return copies


def _pieces_copies(n_w):
    def copies(bufs, send_sems, recv_sems):
        x, y, c = _place()
        out = []
        for w in range(n_w):
            for j, (cx, cy) in enumerate(_other_chips(x, y)):
                out.append(pltpu.make_async_remote_copy(
                    src_ref=bufs[w].at[2 * cx + cy], dst_ref=bufs[n_w + w].at[j],
                    send_sem=send_sems.at[3 * w + j], recv_sem=recv_sems.at[3 * w + j],
                    device_id=(cx, cy, c), device_id_type=MESH))
        return out
    return copies


def _join_copies(n_w):
    def copies(bufs, send_sems, recv_sems):
        x, y, c = _place()
        return [pltpu.make_async_remote_copy(
            src_ref=bufs[w].at[c], dst_ref=bufs[w].at[c], send_sem=send_sems.at[w], recv_sem=recv_sems.at[w],
            device_id=(x, y, 1 - c), device_id_type=MESH) for w in range(n_w)]
    return copies


def _grad_view(g, column_sharded):
    return _weight_view(g, column_sharded)


def _halves_landing(view):
    shape = view.shape[1:] if view.ndim == 3 else (N_CHIPS,) + view.shape[2:]
    return lax.empty(shape, BF16)


def _halves_start(tag, grads, column_sharded):
    views = [_weight_view(g, cs) for g, cs in zip(grads, column_sharded)]
    n = len(views)
    return _split_start("halves_start_" + tag, views + [_halves_landing(v) for v in views], n, _halves_copies(n))


def _halves_wait(tag, state, after):
    send_sems, recv_sems, bufs = state
    n = len(bufs) // 2
    bufs = _split_wait("halves_wait_" + tag, bufs, send_sems, recv_sems, _halves_copies(n), after)
    return bufs[:n], bufs[n:]


def _pieces_start(tag, pieces):
    n = len(pieces)
    landing = [lax.empty((3,) + p.shape[1:], BF16) for p in pieces]
    return _split_start("pieces_start_" + tag, list(pieces) + landing, 3 * n, _pieces_copies(n))


def _pieces_wait(tag, state, after):
    send_sems, recv_sems, bufs = state
    n = len(bufs) // 2
    bufs = _split_wait("pieces_wait_" + tag, bufs, send_sems, recv_sems, _pieces_copies(n), after)
    return bufs[:n], bufs[n:]


def _join_start(tag, shards):
    n = len(shards)
    return _split_start("join_start_" + tag, list(shards), n, _join_copies(n))


def _join_wait(tag, state, after):
    send_sems, recv_sems, bufs = state
    bufs = _split_wait("join_wait_" + tag, bufs, send_sems, recv_sems, _join_copies(len(bufs)), after)
    return [b.reshape(2 * b.shape[1], b.shape[2]) for b in bufs]


def _chip_sum_col(g3, sib, c_arr, name):
    _, hk, n = g3.shape
    cols = n // N_CHIPS
    tr = _row_tile(hk, cols * 2, limit=1024 * 1024)

    def body(c_ref, g_ref, s_ref, o_ref):
        del c_ref
        o_ref[...] = (g_ref[...].astype(F32) + s_ref[...].astype(F32)).astype(BF16)

    grid_spec = pltpu.PrefetchScalarGridSpec(
        num_scalar_prefetch=1, grid=(N_CHIPS, hk // tr),
        in_specs=[pl.BlockSpec((None, tr, cols), lambda p, r, c_ref: (c_ref[0], r, p)),
                  pl.BlockSpec((tr, cols), lambda p, r, c_ref: (r, p))],
        out_specs=pl.BlockSpec((None, tr, cols), lambda p, r, c_ref: (p, r, 0)))
    return pl.pallas_call(
        body, name=name, grid_spec=grid_spec,
        out_shape=jax.ShapeDtypeStruct((N_CHIPS, hk, cols), BF16),
        compiler_params=_params(("parallel", "parallel")),
    )(c_arr, g3, sib)


def _chip_sum_row(g4, sib, c_arr, name):
    _, _, hr, n = g4.shape
    tr = _row_tile(hr, n * 2, limit=1024 * 1024)

    def body(c_ref, g_ref, s_ref, o_ref):
        del c_ref
        o_ref[...] = (g_ref[...].astype(F32) + s_ref[...].astype(F32)).astype(BF16)

    grid_spec = pltpu.PrefetchScalarGridSpec(
        num_scalar_prefetch=1, grid=(N_CHIPS, hr // tr),
        in_specs=[pl.BlockSpec((None, None, tr, n), lambda p, r, c_ref: (p, c_ref[0], r, 0)),
                  pl.BlockSpec((None, tr, n), lambda p, r, c_ref: (p, r, 0))],
        out_specs=pl.BlockSpec((None, tr, n), lambda p, r, c_ref: (p, r, 0)))
    return pl.pallas_call(
        body, name=name, grid_spec=grid_spec,
        out_shape=jax.ShapeDtypeStruct((N_CHIPS, hr, n), BF16),
        compiler_params=_params(("parallel", "parallel")),
    )(c_arr, g4, sib)


def _sum_pieces(pieces, received, place_arr, name):
    _, r, n = pieces.shape
    tr = _row_tile(r, n * 4, limit=1024 * 1024)

    def body(p_ref, own_ref, r0_ref, r1_ref, r2_ref, o_ref):
        del p_ref
        acc = own_ref[...].astype(F32) + r0_ref[...].astype(F32)
        acc = acc + r1_ref[...].astype(F32)
        o_ref[...] = acc + r2_ref[...].astype(F32)

    def recv_spec(j):
        return pl.BlockSpec((None, tr, n), lambda i, p_ref: (j, i, 0))

    grid_spec = pltpu.PrefetchScalarGridSpec(
        num_scalar_prefetch=1, grid=(r // tr,),
        in_specs=[pl.BlockSpec((None, tr, n), lambda i, p_ref: (p_ref[0], i, 0)),
                  recv_spec(0), recv_spec(1), recv_spec(2)],
        out_specs=pl.BlockSpec((None, tr, n), lambda i, p_ref: (p_ref[1], i, 0)))
    return pl.pallas_call(
        body, name=name, grid_spec=grid_spec,
        out_shape=jax.ShapeDtypeStruct((2, r, n), F32),
        compiler_params=_params(("parallel",)),
    )(place_arr, pieces, received, received, received)


def _norm_weights_step(parts, w, m, v, after=()):
    rows, d = parts.shape
    after = tuple(after)

    def body(p_ref, w_ref, m_ref, v_ref, *rest):
        g_ref, d_ref, mo_ref, vo_ref, gathered, send_sems, recv_sems = rest[len(after):]
        x, y, c = _place()
        me = 4 * x + 2 * y + c
        gathered[me] = p_ref[...]
        copies = []
        for k in range(1, N_DEV):
            peer = (x ^ ((k >> 2) & 1), y ^ ((k >> 1) & 1), c ^ (k & 1))
            copies.append(pltpu.make_async_remote_copy(
                src_ref=p_ref, dst_ref=gathered.at[me], send_sem=send_sems.at[k - 1],
                recv_sem=recv_sems.at[k - 1], device_id=peer, device_id_type=MESH))
        for cp in copies:
            cp.start()
        for cp in copies:
            cp.wait()
        g = gathered[0]
        for k in range(1, N_DEV):
            g = g + gathered[k]
        delta, m_new, v_new = _adamw_math(w_ref[...], g, m_ref[...], v_ref[...])
        g_ref[...] = g
        d_ref[...] = delta
        mo_ref[...] = m_new
        vo_ref[...] = v_new

    vmem = pl.BlockSpec(memory_space=pltpu.VMEM)
    shp = jax.ShapeDtypeStruct((rows, d), F32)
    return pl.pallas_call(
        body, name="norm_weights_step",
        in_specs=[vmem] * 4 + [ANY] * len(after), out_specs=[vmem] * 4, out_shape=[shp] * 4,
        scratch_shapes=[pltpu.VMEM((N_DEV, rows, d), F32), pltpu.SemaphoreType.DMA((N_DEV - 1,)),
                        pltpu.SemaphoreType.DMA((N_DEV - 1,))],
        compiler_params=pltpu.CompilerParams(has_side_effects=True),
    )(parts, w, m, v, *after)


def kernel(x, norm_mix_w, w_in, w_out, norm_ffn_w, w_gate, w_up, w_down, norm_final_w, loss_target, m_norm_mix_w, m_w_in, m_w_out, m_norm_ffn_w, m_w_gate, m_w_up, m_w_down, m_norm_final_w, v_norm_mix_w, v_w_in, v_w_out, v_norm_ffn_w, v_w_gate, v_w_up, v_w_down, v_norm_final_w):
    s, d = x.shape[1], x.shape[2]
    xs = x.reshape(s, d)
    target = loss_target.reshape(s, d)
    big = {"w_in": (w_in, m_w_in, v_w_in), "w_out": (w_out, m_w_out, v_w_out),
           "w_gate": (w_gate, m_w_gate, v_w_gate), "w_up": (w_up, m_w_up, v_w_up),
           "w_down": (w_down, m_w_down, v_w_down)}
    big = {k: tuple(a.reshape(a.shape[1:]) for a in t) for k, t in big.items()}
    col_names, row_names = ("w_in", "w_gate", "w_up"), ("w_out", "w_down")
    n_in = N_CHIPS * big["w_in"][0].shape[1]
    ffn = N_CHIPS * big["w_gate"][0].shape[1]
    mix = ATTN_WIDTH + RET_WIDTH
    c_arr = lax.axis_index("c").astype(I32).reshape(1)
    shard_arr = (2 * lax.axis_index("x") + lax.axis_index("y")).astype(I32).reshape(1)
    place_arr = jnp.concatenate([shard_arr, c_arr])

    def cast(k, after=()):
        return _weight_view(_cast_into_full(big[k][0], shard_arr, k in col_names, "cast_" + k, after), k in col_names)

    send_in, recv_in, v_in = _gather_start([cast("w_in")], "gather_start_in")
    rest = ("w_out", "w_gate", "w_up", "w_down")
    send_sems, recv_sems, v_out, v_gate, v_up, v_down = _gather_start(
        [cast(k, after=[v_in]) for k in rest], "gather_start_rest")

    sec = ATTN_WIDTH

    def section(p, rows):
        return pl.BlockSpec((None, rows, sec), lambda i, j, kk: (p, i, 0))

    h1 = _rms_fwd(xs, norm_mix_w, "rms_mix_fwd")
    fs, fr, v_in = _gather_forward([v_in], [0], send_in, recv_in, h1, "gather_forward_in")
    wi, = _gather_end([v_in], fs, fr, h1, "gather_end_in")
    proj, = _matmul("in_proj", "nn", [h1], [wi], [0], s, n_in, d, s, 512, d, [], [F32], _epi_plain)
    fs, fr, v_out = _gather_forward([v_out], [0], send_sems, recv_sems, proj, "gather_forward_out")
    mixed, attn_o, lse = _attn_fwd(proj)
    wo, = _gather_end([v_out], fs, fr, attn_o, "gather_end_out")
    fs, fr, v_gate, v_up = _gather_forward([v_gate, v_up], [1, 2], send_sems, recv_sems, attn_o,
                                           "gather_forward_gate_up")
    mixed, ret_raw = _ret_fwd(proj, mixed, after=[v_gate])
    x1, = _matmul("out_proj", "nn", [mixed, mixed], [wo, wo], [0, 0], s, d, sec, s, 512, sec, [xs], [F32],
                  _epi_residual, b_koff=[0, 1], a_specs=[section(0, s), section(1, s)])
    h2 = _rms_fwd(x1, norm_ffn_w, "rms_ffn_fwd")
    wg, wu = _gather_end([v_gate, v_up], fs, fr, h2, "gather_end_gate_up")
    fs, fr, v_down = _gather_forward([v_down], [3], send_sems, recv_sems, h2, "gather_forward_down")
    gate, up, act = _matmul("gate_up", "nn", [h2, h2], [wg, wu], [0, 1], s, ffn, d, s, 256, d, [],
                            [F32, F32, BF16], _epi_swiglu, after=[v_down])
    wd, = _gather_end([v_down], fs, fr, act, "gather_end_down")
    x2, = _matmul("down_proj", "nn", [act], [wd], [0], s, d, ffn, s // 2, 256, ffn, [x1], [F32],
                  _epi_residual)
    loss_row, dx2, dx2b, dwf = _final_norm_loss(x2, norm_final_w.reshape(1, d), target, "final_norm_loss")

    names = col_names + row_names
    grads, new = {}, {}

    def chip_sums(tag_names, views, sibs):
        return [(_chip_sum_col if k in col_names else _chip_sum_row)(v, sb, c_arr, "chip_sum_" + k)
                for k, v, sb in zip(tag_names, views, sibs)]

    def piece_sums(tag_names, pieces, received):
        return [_sum_pieces(p, r, place_arr, "sum_pieces_" + k) for k, p, r in zip(tag_names, pieces, received)]

    def update(k):
        new[k] = _adamw(big[k][0], grads[k], big[k][1], big[k][2], "adamw_" + k)

    dgate, dup = _matmul("d_act", "nt", [dx2b], [wd], [0], s, ffn, d, s, 256, d, [gate, up],
                         [BF16, BF16], _epi_swiglu_bwd)
    g_wd, = _matmul("g_w_down", "tn", [act], [dx2b], [0], ffn, d, s, 512, d, s, [], [BF16], _epi_plain)
    halves_d = _halves_start("down", [g_wd], [False])
    dh2, = _matmul("d_h2", "nt", [dgate, dup], [wg, wu], [0, 0], s, d, ffn, s // 4, 256, ffn, [], [F32],
                   _epi_plain, after=halves_d[2][-1:])
    pieces_d = _pieces_start("down", chip_sums(["w_down"], *_halves_wait("down", halves_d, dh2)))
    g_wg, g_wu = _matmul("g_w_gate_up", "tn", [h2, h2], [dgate, dup], [0, 1], d, ffn, s, 1024, 512, s, [],
                         [BF16, BF16], _epi_two, after=pieces_d[2][-1:])
    halves_gu = _halves_start("gate_up", [g_wg, g_wu], [True, True])
    dx1, dx1b, dw_ffn = _rms_bwd(x1, norm_ffn_w, dh2, dx2, "rms_ffn_bwd", after=halves_gu[2][-1:])

    dmixed, = _matmul("d_mixed", "nt", [dx1b], [wo], [0], s, mix, d, s, 512, d, [], [F32], _epi_plain)
    pieces_gu = _pieces_start("gate_up", chip_sums(["w_gate", "w_up"], *_halves_wait("gate_up", halves_gu, dmixed)))
    per = sec // 512
    g_wo, = _matmul("g_w_out", "tn", [mixed], [dx1b], [0], mix, d, s, 512, d, s, [], [BF16], _epi_plain,
                    after=pieces_gu[2][-1:],
                    a_specs=[pl.BlockSpec((None, s, 512), lambda i, j, kk: (i // per, 0, i % per))])
    halves_o = _halves_start("out", [g_wo], [False])
    dsec = _attn_bwd(proj, attn_o, lse, dmixed, after=halves_o[2][-1:])
    pieces_o = _pieces_start("out", chip_sums(["w_out"], *_halves_wait("out", halves_o, dsec)))
    dsec = _ret_bwd(proj, ret_raw, dmixed, dsec, after=pieces_o[2][-1:])
    where = [0, 1, 2, 4, 5, 6, 7]
    n_sec = len(where)
    g_wi, = _matmul("g_w_in", "tn", [h1], [dsec], [0], d, n_in, s, 1024, sec, s, [], [BF16], _epi_plain,
                    b_specs=[pl.BlockSpec((None, s, sec), lambda i, j, kk: (j + (j >= 3).astype(I32), 0, 0))])
    halves_i = _halves_start("in", [g_wi], [True])
    dh1, = _matmul("d_h1", "nt", [dsec] * n_sec, [wi] * n_sec, [0] * n_sec, s, d, sec, s // 2, 256, sec, [], [F32],
                   _epi_plain, b_koff=list(range(n_sec)), after=halves_i[2][-1:],
                   a_specs=[section(p, s // 2) for p in where])
    pieces_i = _pieces_start("in", chip_sums(["w_in"], *_halves_wait("in", halves_i, dh1)))
    grad_x, _, dw_mix = _rms_bwd(xs, norm_mix_w, dh1, dx1, "rms_mix_bwd", after=pieces_i[2][-1:])

    def rows8(*vs):
        return jnp.concatenate([v.reshape(1, d) for v in vs] + [jnp.zeros((8 - len(vs), d), F32)], axis=0)

    join_d = _join_start("down", piece_sums(["w_down"], *_pieces_wait("down", pieces_d, grad_x)))
    join_gu = _join_start("gate_up", piece_sums(["w_gate", "w_up"], *_pieces_wait("gate_up", pieces_gu, join_d[2][0])))
    join_o = _join_start("out", piece_sums(["w_out"], *_pieces_wait("out", pieces_o, join_gu[2][0])))
    grads["w_down"], = _join_wait("down", join_d, join_o[2][0])
    update("w_down")
    grads["w_gate"], grads["w_up"] = _join_wait("gate_up", join_gu, new["w_down"][0])
    update("w_gate")
    update("w_up")
    grads["w_out"], = _join_wait("out", join_o, new["w_up"][0])
    update("w_out")
    join_i = _join_start("in", piece_sums(["w_in"], *_pieces_wait("in", pieces_i, new["w_out"][0])))
    ng, nd, nm, nv = _norm_weights_step(
        rows8(dw_mix, dw_ffn, dwf), rows8(norm_mix_w, norm_ffn_w, norm_final_w),
        rows8(m_norm_mix_w, m_norm_ffn_w, m_norm_final_w), rows8(v_norm_mix_w, v_norm_ffn_w, v_norm_final_w),
        after=join_i[2][:1])
    grads["w_in"], = _join_wait("in", join_i, ng)
    update("w_in")

    loss = lax.psum(loss_row[0, 0], ("x", "y", "c"))

    def pack(small, per_weight):
        lead = lambda a: a.reshape((1,) + a.shape)
        return (small[0:1], lead(per_weight["w_in"]), lead(per_weight["w_out"]), small[1:2],
                lead(per_weight["w_gate"]), lead(per_weight["w_up"]), lead(per_weight["w_down"]), small[2])

    return (loss, grad_x.reshape(1, s, d),
            *pack(ng, {k: new[k][3] for k in names}),
            *pack(nd, {k: new[k][0] for k in names}),
            *pack(nm, {k: new[k][1] for k in names}),
            *pack(nv, {k: new[k][2] for k in names}))
```

```python
import functools
import math

import jax
import jax.numpy as jnp
from jax import lax
from jax.experimental import pallas as pl
from jax.experimental.pallas import tpu as pltpu

F32 = jnp.float32
BF16 = jnp.bfloat16
I32 = jnp.int32
MESH = pl.DeviceIdType.MESH
ANY = pl.BlockSpec(memory_space=pl.ANY)

ATTN_HEADS = 8
ATTN_HEAD_DIM = 128
RET_HEADS = 4
RET_HEAD_DIM = 256
ATTN_WIDTH = ATTN_HEADS * ATTN_HEAD_DIM
RET_WIDTH = RET_HEADS * RET_HEAD_DIM
DILATED_PATTERNS = ((128, 1), (512, 4), (2048, 16))
NORM_EPS = 1e-6
ADAM_LR = 0.001
ADAM_B1 = 0.9
ADAM_B2 = 0.999
ADAM_EPS = 1e-08
ADAM_WD = 0.01
ADAM_STEP = 10

N_CHIPS = 4
N_DEV = 8
NEG_BIG = -1e30
SEQ_TILE = 512
ATTN_HEADS_PER_STEP = 1
VMEM_LIMIT_BYTES = 56 * 1024 * 1024


def _params(semantics=None, vmem=VMEM_LIMIT_BYTES):
    return pltpu.CompilerParams(dimension_semantics=semantics, vmem_limit_bytes=vmem)


def _row_tile(rows, row_bytes, limit=2 * 1024 * 1024, mult=16):
    best = None
    for t in range(mult, rows + 1, mult):
        if rows % t == 0 and t * row_bytes <= limit:
            best = t
    assert best is not None, (rows, row_bytes)
    return best


def _sigmoid(x):
    return 1.0 / (1.0 + jnp.exp(-x))


def _select_by_index(idx, values):
    out = jnp.float32(values[-1])
    for i in range(len(values) - 2, -1, -1):
        out = jnp.where(idx == i, jnp.float32(values[i]), out)
    return out


def _place():
    x, y, c = lax.axis_index("x"), lax.axis_index("y"), lax.axis_index("c")
    return x, y, c


def _cast_into_full(w, shard_arr, column_sharded, name, after=()):
    after = tuple(after)
    rows, cols = w.shape
    tr = _row_tile(rows, cols * 4)
    steps = rows // tr
    if column_sharded:
        out_shape, out_map = (rows, N_CHIPS * cols), (lambda i, s_ref: (i, s_ref[0]))
    else:
        out_shape, out_map = (N_CHIPS * rows, cols), (lambda i, s_ref: (s_ref[0] * steps + i, 0))

    def body(s_ref, w_ref, *rest):
        del s_ref
        rest[-1][...] = w_ref[...].astype(BF16)

    grid_spec = pltpu.PrefetchScalarGridSpec(
        num_scalar_prefetch=1, grid=(steps,),
        in_specs=[pl.BlockSpec((tr, cols), lambda i, s_ref: (i, 0))] + [ANY] * len(after),
        out_specs=pl.BlockSpec((tr, cols), out_map))
    return pl.pallas_call(
        body, name=name, grid_spec=grid_spec,
        out_shape=jax.ShapeDtypeStruct(out_shape, BF16),
        compiler_params=_params(("parallel",)),
    )(shard_arr, w, *after)


def _rms_fwd(x, w, name):
    rows, d = x.shape
    tr = 256

    def body(x_ref, w_ref, h_ref):
        xv = x_ref[...]
        r = lax.rsqrt(jnp.mean(xv * xv, axis=-1, keepdims=True) + NORM_EPS)
        h_ref[...] = (xv * r * w_ref[...]).astype(BF16)

    return pl.pallas_call(
        body, name=name, grid=(rows // tr,),
        in_specs=[pl.BlockSpec((tr, d), lambda i: (i, 0)), pl.BlockSpec((1, d), lambda i: (0, 0))],
        out_specs=pl.BlockSpec((tr, d), lambda i: (i, 0)),
        out_shape=jax.ShapeDtypeStruct((rows, d), BF16),
        compiler_params=_params(("parallel",)),
    )(x, w)


def _rms_bwd(x, w, dh, dres, name, after=()):
    rows, d = x.shape
    tr = 256
    after = tuple(after)

    def body(x_ref, w_ref, dh_ref, dres_ref, *rest):
        dx_ref, dxb_ref, dw_ref = rest[len(after):]
        xv = x_ref[...]
        r = lax.rsqrt(jnp.mean(xv * xv, axis=-1, keepdims=True) + NORM_EPS)
        xhat = xv * r
        dy = dh_ref[...]
        dxhat = dy * w_ref[...]
        dx = dres_ref[...] + r * (dxhat - xhat * jnp.mean(dxhat * xhat, axis=-1, keepdims=True))
        dx_ref[...] = dx
        dxb_ref[...] = dx.astype(BF16)
        part = jnp.sum(dy * xhat, axis=0, keepdims=True)

        @pl.when(pl.program_id(0) == 0)
        def _():
            dw_ref[...] = part

        @pl.when(pl.program_id(0) != 0)
        def _():
            dw_ref[...] += part

    row = pl.BlockSpec((tr, d), lambda i: (i, 0))
    vec = pl.BlockSpec((1, d), lambda i: (0, 0))
    return pl.pallas_call(
        body, name=name, grid=(rows // tr,),
        in_specs=[row, vec, row, row] + [ANY] * len(after),
        out_specs=[row, row, vec],
        out_shape=[jax.ShapeDtypeStruct((rows, d), F32), jax.ShapeDtypeStruct((rows, d), BF16),
                   jax.ShapeDtypeStruct((1, d), F32)],
        compiler_params=_params(("arbitrary",)),
    )(x, w, dh, dres, *after)


def _final_norm_loss(x2, w, target, name):
    rows, d = x2.shape
    tr = 256

    def body(x_ref, w_ref, t_ref, loss_ref, dx_ref, dxb_ref, dw_ref):
        xv = x_ref[...]
        wv = w_ref[...]
        r = lax.rsqrt(jnp.mean(xv * xv, axis=-1, keepdims=True) + NORM_EPS)
        xhat = xv * r
        err = xhat * wv - t_ref[...]
        part_loss = 0.5 * jnp.sum(jnp.mean(err * err, axis=-1, keepdims=True), axis=0, keepdims=True)
        dy = err * (1.0 / d)
        dxhat = dy * wv
        dx = r * (dxhat - xhat * jnp.mean(dxhat * xhat, axis=-1, keepdims=True))
        dx_ref[...] = dx
        dxb_ref[...] = dx.astype(BF16)
        part_dw = jnp.sum(dy * xhat, axis=0, keepdims=True)
        part_loss = jnp.broadcast_to(part_loss, (1, 128))

        @pl.when(pl.program_id(0) == 0)
        def _():
            dw_ref[...] = part_dw
            loss_ref[...] = part_loss

        @pl.when(pl.program_id(0) != 0)
        def _():
            dw_ref[...] += part_dw
            loss_ref[...] += part_loss

    row = pl.BlockSpec((tr, d), lambda i: (i, 0))
    vec = pl.BlockSpec((1, d), lambda i: (0, 0))
    return pl.pallas_call(
        body, name=name, grid=(rows // tr,),
        in_specs=[row, vec, row],
        out_specs=[pl.BlockSpec((1, 128), lambda i: (0, 0)), row, row, vec],
        out_shape=[jax.ShapeDtypeStruct((1, 128), F32), jax.ShapeDtypeStruct((rows, d), F32),
                   jax.ShapeDtypeStruct((rows, d), BF16), jax.ShapeDtypeStruct((1, d), F32)],
        compiler_params=_params(("arbitrary",)),
    )(x2, w, target)


def _adamw_math(w, g, m, v):
    m = ADAM_B1 * m + (1.0 - ADAM_B1) * g
    v = ADAM_B2 * v + (1.0 - ADAM_B2) * (g * g)
    m_hat = m / (1.0 - ADAM_B1 ** ADAM_STEP)
    v_hat = v / (1.0 - ADAM_B2 ** ADAM_STEP)
    delta = -ADAM_LR * (m_hat / (jnp.sqrt(v_hat) + ADAM_EPS) + ADAM_WD * w)
    return delta, m, v


def _adamw(w, g, m, v, name):
    rows, cols = w.shape
    tr = _row_tile(rows, cols * 4, limit=1024 * 1024)

    def body(w_ref, g_ref, m_ref, v_ref, d_ref, mo_ref, vo_ref, go_ref):
        g = g_ref[...]
        delta, m_new, v_new = _adamw_math(w_ref[...], g, m_ref[...], v_ref[...])
        d_ref[...] = delta
        mo_ref[...] = m_new
        vo_ref[...] = v_new
        go_ref[...] = g

    blk = pl.BlockSpec((tr, cols), lambda i: (i, 0))
    shp = jax.ShapeDtypeStruct((rows, cols), F32)
    return pl.pallas_call(
        body, name=name, grid=(rows // tr,),
        in_specs=[blk] * 4, out_specs=[blk] * 4, out_shape=[shp] * 4,
        compiler_params=_params(("parallel",)),
    )(w, g, m, v)


_DOT_DIMS = {"nn": ((1,), (0,)), "nt": ((1,), (1,)), "tn": ((0,), (0,))}


def _matmul(name, mode, a_list, b_list, acc_of, m, n, k, tm, tn, tk, extras, out_dtypes, epilogue,
            a_koff=None, b_koff=None, after=(), a_specs=None, b_specs=None):
    after = tuple(after)
    assert m % tm == 0 and n % tn == 0 and k % tk == 0, (name, m, n, k, tm, tn, tk)
    nk = k // tk
    n_acc = max(acc_of) + 1
    n_pairs = len(a_list)
    a_koff = a_koff or [0] * n_pairs
    b_koff = b_koff or [0] * n_pairs
    dims = (_DOT_DIMS[mode], ((), ()))
    n_ext, n_out = len(extras), len(out_dtypes)

    def body(*refs):
        a_refs = refs[:n_pairs]
        b_refs = refs[n_pairs:2 * n_pairs]
        e_refs = refs[2 * n_pairs:2 * n_pairs + n_ext]
        first_out = 2 * n_pairs + n_ext + len(after)
        o_refs = refs[first_out:first_out + n_out]
        acc_refs = refs[first_out + n_out:]
        parts = [None] * n_acc
        for p in range(n_pairs):
            d = lax.dot_general(a_refs[p][...], b_refs[p][...], dims, preferred_element_type=F32)
            parts[acc_of[p]] = d if parts[acc_of[p]] is None else parts[acc_of[p]] + d

        def finish(accs):
            outs = epilogue(accs, [e[...] for e in e_refs])
            for o_ref, o in zip(o_refs, outs):
                o_ref[...] = o.astype(o_ref.dtype)

        if nk == 1:
            finish(parts)
        else:
            kk = pl.program_id(2)

            @pl.when(kk == 0)
            def _():
                for acc_ref, part in zip(acc_refs, parts):
                    acc_ref[...] = part

            @pl.when(kk != 0)
            def _():
                for acc_ref, part in zip(acc_refs, parts):
                    acc_ref[...] += part

            @pl.when(kk == nk - 1)
            def _():
                finish([acc_ref[...] for acc_ref in acc_refs])

    def a_spec(off):
        if mode == "tn":
            return pl.BlockSpec((tk, tm), lambda i, j, kk: (kk + off, i))
        return pl.BlockSpec((tm, tk), lambda i, j, kk: (i, kk + off))

    def b_spec(off):
        if mode == "nt":
            return pl.BlockSpec((tn, tk), lambda i, j, kk: (j, kk + off))
        return pl.BlockSpec((tk, tn), lambda i, j, kk: (kk + off, j))

    tile = pl.BlockSpec((tm, tn), lambda i, j, kk: (i, j))
    scratch = [pltpu.VMEM((tm, tn), F32) for _ in range(n_acc)] if nk > 1 else []
    return pl.pallas_call(
        body, name=name, grid=(m // tm, n // tn, nk),
        in_specs=(a_specs or [a_spec(o) for o in a_koff]) + (b_specs or [b_spec(o) for o in b_koff])
        + [tile] * n_ext + [ANY] * len(after),
        out_specs=[tile] * n_out,
        out_shape=[jax.ShapeDtypeStruct((m, n), dt) for dt in out_dtypes],
        scratch_shapes=scratch,
        compiler_params=_params(("parallel", "parallel", "arbitrary")),
    )(*a_list, *b_list, *extras, *after)


def _epi_plain(accs, extras):
    return (accs[0],)


def _epi_residual(accs, extras):
    return (accs[0] + extras[0],)


def _epi_two(accs, extras):
    return accs[0], accs[1]


def _epi_swiglu(accs, extras):
    g, u = accs
    return g, u, g * _sigmoid(g) * u


def _epi_swiglu_bwd(accs, extras):
    da = accs[0]
    g, u = extras
    sg = _sigmoid(g)
    dg = da * u * sg * (1.0 + g * (1.0 - sg))
    du = da * g * sg
    return dg, du


_NT_DIMS = (((1,), (1,)), ((), ()))
_TN_DIMS = (((0,), (0,)), ((), ()))


def _tile_delta(tq, tk):
    return lax.broadcasted_iota(I32, (tq, tk), 0) - lax.broadcasted_iota(I32, (tq, tk), 1)


def _attn_mask_bias(delta, slope):
    count = jnp.zeros(delta.shape, I32)
    for window, dilation in DILATED_PATTERNS:
        hit = ((delta & (dilation - 1)) == 0) & (delta <= window)
        count = count + jnp.where(hit, 1, 0)
    valid = (delta >= 0) & (count > 0)
    logm = jnp.where(count == 3, math.log(3.0), jnp.where(count == 2, math.log(2.0), 0.0))
    return valid, logm - slope * delta.astype(F32)


def _fill_attn_bias(tab_ref, slope):
    nb, t, _ = tab_ref.shape
    base = _tile_delta(t, t)
    for b in range(nb):
        valid, bias = _attn_mask_bias(base + b * t, slope)
        tab_ref[b] = jnp.where(valid, bias, NEG_BIG)


def _fill_ret_decay(tab_ref, log_gamma):
    nb, t, _ = tab_ref.shape
    base = _tile_delta(t, t)
    for b in range(nb):
        tab_ref[b] = _ret_decay(base + b * t, log_gamma)


def _alibi_slopes():
    return [2.0 ** (-8.0 * (h + 1) / ATTN_HEADS) for h in range(ATTN_HEADS)]


def _attn_fwd(proj, after=()):
    s = proj.shape[0]
    t = SEQ_TILE
    hd = ATTN_HEAD_DIM
    hp = ATTN_HEADS_PER_STEP
    ng = ATTN_HEADS // hp
    w = hp * hd
    scale = 1.0 / math.sqrt(hd)
    slopes = _alibi_slopes()

    def body(q_ref, k_ref, v_ref, *rest):
        mix_ref, o_ref, lse_ref, kb, vb, bias_tab = rest[len(after):]
        g = pl.program_id(0)
        i = pl.program_id(1)

        @pl.when(i == 0)
        def _():
            kb[...] = k_ref[...].astype(BF16)
            vb[...] = v_ref[...].astype(BF16)
            for u in range(hp):
                _fill_attn_bias(bias_tab.at[u], _select_by_index(g * hp + u, slopes))

        qs = [q_ref[:, u * hd:(u + 1) * hd].astype(BF16) for u in range(hp)]

        def step(j, carry):
            rows = pl.ds(pl.multiple_of(j * t, t), t)
            out = []
            for u in range(hp):
                m_i, l_i, acc = carry[u]
                lanes = slice(u * hd, (u + 1) * hd)
                sc = lax.dot_general(qs[u], kb[rows, lanes], _NT_DIMS, preferred_element_type=F32) * scale
                sc = sc + bias_tab[u, i - j]
                m_new = jnp.maximum(m_i, jnp.max(sc, axis=-1, keepdims=True))
                p = jnp.exp(sc - m_new)
                alpha = jnp.exp(m_i - m_new)
                l_new = alpha * l_i + jnp.sum(p, axis=-1, keepdims=True)
                acc = alpha * acc + jnp.dot(p.astype(BF16), vb[rows, lanes], preferred_element_type=F32)
                out.append((m_new, l_new, acc))
            return tuple(out)

        init = (jnp.full((t, 1), NEG_BIG, F32), jnp.zeros((t, 1), F32), jnp.zeros((t, hd), F32))
        final = lax.fori_loop(0, i + 1, step, (init,) * hp)
        for u in range(hp):
            m_i, l_i, acc = final[u]
            lanes = slice(u * hd, (u + 1) * hd)
            out = acc / l_i
            o_ref[:, lanes] = out
            mix_ref[:, lanes] = out.astype(BF16)
            lse_ref[:, lanes] = jnp.broadcast_to(m_i + jnp.log(l_i), (t, hd))

    return pl.pallas_call(
        body, name="attn_fwd", grid=(ng, s // t),
        in_specs=[pl.BlockSpec((t, w), lambda g, i: (i, g)),
                  pl.BlockSpec((s, w), lambda g, i: (0, ng + g)),
                  pl.BlockSpec((s, w), lambda g, i: (0, 2 * ng + g))] + [ANY] * len(after),
        out_specs=[pl.BlockSpec((None, t, w), lambda g, i: (0, i, g))] + [pl.BlockSpec((t, w), lambda g, i: (i, g))] * 2,
        out_shape=[jax.ShapeDtypeStruct((2, s, ATTN_WIDTH), BF16),
                   jax.ShapeDtypeStruct((s, ATTN_WIDTH), F32),
                   jax.ShapeDtypeStruct((s, ATTN_WIDTH), F32)],
        scratch_shapes=[pltpu.VMEM((s, w), BF16), pltpu.VMEM((s, w), BF16), pltpu.VMEM((hp, s // t, t, t), F32)],
        compiler_params=_params(("arbitrary", "arbitrary")),
    )(proj, proj, proj, *after)


def _attn_bwd(proj, attn_out, lse, dmixed, after=()):
    after = tuple(after)
    s = proj.shape[0]
    t = SEQ_TILE
    nt = s // t
    hd = ATTN_HEAD_DIM
    hp = ATTN_HEADS_PER_STEP
    ng = ATTN_HEADS // hp
    w = hp * hd
    scale = 1.0 / math.sqrt(hd)
    slopes = _alibi_slopes()

    def body(q_ref, k_ref, v_ref, o_ref, lse_ref, do_ref, *rest):
        dsec_ref, qb, kb, vb, dob, dsum, dq_acc, bias_tab = rest[len(after):]
        g = pl.program_id(0)
        qb[...] = q_ref[...].astype(BF16)
        kb[...] = k_ref[...].astype(BF16)
        vb[...] = v_ref[...].astype(BF16)
        dob[...] = do_ref[...].astype(BF16)
        for u in range(hp):
            lanes = slice(u * hd, (u + 1) * hd)
            _fill_attn_bias(bias_tab.at[u], _select_by_index(g * hp + u, slopes))
            rowsum = jnp.sum(do_ref[:, lanes] * o_ref[:, lanes], axis=-1, keepdims=True)
            dsum[:, lanes] = jnp.broadcast_to(rowsum, (s, hd))
        dq_acc[...] = jnp.zeros((s, w), F32)

        def over_keys(j, _):
            krows = pl.ds(pl.multiple_of(j * t, t), t)

            def over_queries(i, carry):
                qrows = pl.ds(pl.multiple_of(i * t, t), t)
                out = []
                for u in range(hp):
                    dk, dv = carry[u]
                    lanes = slice(u * hd, (u + 1) * hd)
                    qi, doi = qb[qrows, lanes], dob[qrows, lanes]
                    kj, vj = kb[krows, lanes], vb[krows, lanes]
                    lse_i = lse_ref[qrows, lanes][:, :1]
                    dsum_i = dsum[qrows, lanes][:, :1]
                    sc = lax.dot_general(qi, kj, _NT_DIMS, preferred_element_type=F32) * scale
                    p = jnp.exp(sc + bias_tab[u, i - j] - lse_i)
                    dp = lax.dot_general(doi, vj, _NT_DIMS, preferred_element_type=F32)
                    ds = (p * (dp - dsum_i)).astype(BF16)
                    dv = dv + lax.dot_general(p.astype(BF16), doi, _TN_DIMS, preferred_element_type=F32)
                    dk = dk + lax.dot_general(ds, qi, _TN_DIMS, preferred_element_type=F32)
                    dq_acc[qrows, lanes] += jnp.dot(ds, kj, preferred_element_type=F32)
                    out.append((dk, dv))
                return tuple(out)

            zero = jnp.zeros((t, hd), F32)
            final = lax.fori_loop(j, nt, over_queries, ((zero, zero),) * hp)
            for u in range(hp):
                lanes = slice(u * hd, (u + 1) * hd)
                dsec_ref[1, krows, lanes] = (final[u][0] * scale).astype(BF16)
                dsec_ref[2, krows, lanes] = final[u][1].astype(BF16)
            return 0

        lax.fori_loop(0, nt, over_keys, 0)
        dsec_ref[0] = (dq_acc[...] * scale).astype(BF16)

    def col(off):
        return pl.BlockSpec((s, w), lambda g: (0, off + g))

    return pl.pallas_call(
        body, name="attn_bwd", grid=(ng,),
        in_specs=[col(0), col(ng), col(2 * ng), col(0), col(0), col(0)] + [ANY] * len(after),
        out_specs=pl.BlockSpec((4, s, w), lambda g: (0, 0, g)),
        out_shape=jax.ShapeDtypeStruct((8, s, ATTN_WIDTH), BF16),
        scratch_shapes=[pltpu.VMEM((s, w), BF16)] * 4 + [pltpu.VMEM((s, w), F32)] * 2
        + [pltpu.VMEM((hp, nt, t, t), F32)],
        compiler_params=_params(("arbitrary",)),
    )(proj, proj, proj, attn_out, lse, dmixed, *after)


def _ret_log_gammas():
    return [math.log(1.0 - 2.0 ** (-5.0 - h)) for h in range(RET_HEADS)]


def _ret_decay(delta, log_gamma):
    dec = jnp.exp(delta.astype(F32) * log_gamma) * (1.0 / math.sqrt(RET_HEAD_DIM))
    return jnp.where(delta >= 0, dec, 0.0)


def _ret_fwd(proj, mixed, after=()):
    after = tuple(after)
    s = proj.shape[0]
    t = SEQ_TILE
    hd = RET_HEAD_DIM
    nh = RET_HEADS
    log_gammas = _ret_log_gammas()
    c0 = 3 * ATTN_WIDTH // hd

    def body(q_ref, k_ref, v_ref, g_ref, *rest):
        mix_ref, raw_ref, kb, vb, decay_tab = rest[1 + len(after):]
        h = pl.program_id(0)
        i = pl.program_id(1)

        @pl.when(i == 0)
        def _():
            kb[...] = k_ref[...].astype(BF16)
            vb[...] = v_ref[...].astype(BF16)
            _fill_ret_decay(decay_tab, _select_by_index(h, log_gammas))

        q = q_ref[...].astype(BF16)

        def step(j, acc):
            rows = pl.ds(pl.multiple_of(j * t, t), t)
            sc = lax.dot_general(q, kb[rows, :], _NT_DIMS, preferred_element_type=F32) * decay_tab[i - j]
            return acc + jnp.dot(sc.astype(BF16), vb[rows, :], preferred_element_type=F32)

        ret = lax.fori_loop(0, i + 1, step, jnp.zeros((t, hd), F32))
        raw_ref[...] = ret
        r = lax.rsqrt(jnp.mean(ret * ret, axis=-1, keepdims=True) + NORM_EPS)
        g = g_ref[...]
        mix_ref[...] = (g * _sigmoid(g) * (ret * r)).astype(BF16)

    return pl.pallas_call(
        body, name="ret_fwd", grid=(nh, s // t),
        in_specs=[pl.BlockSpec((t, hd), lambda h, i: (i, c0 + h)),
                  pl.BlockSpec((s, hd), lambda h, i: (0, c0 + nh + h)),
                  pl.BlockSpec((s, hd), lambda h, i: (0, c0 + 2 * nh + h)),
                  pl.BlockSpec((t, hd), lambda h, i: (i, c0 + 3 * nh + h))] + [ANY] * (1 + len(after)),
        out_specs=[pl.BlockSpec((None, t, hd), lambda h, i: (1, i, h)), pl.BlockSpec((t, hd), lambda h, i: (i, h))],
        out_shape=[jax.ShapeDtypeStruct(mixed.shape, BF16), jax.ShapeDtypeStruct((s, RET_WIDTH), F32)],
        input_output_aliases={4: 0},
        scratch_shapes=[pltpu.VMEM((s, hd), BF16), pltpu.VMEM((s, hd), BF16), pltpu.VMEM((s // t, t, t), F32)],
        compiler_params=_params(("arbitrary", "arbitrary")),
    )(proj, proj, proj, proj, mixed, *after)


def _ret_bwd(proj, ret_raw, dmixed, dsec, after=()):
    after = tuple(after)
    s = proj.shape[0]
    t = SEQ_TILE
    nt = s // t
    hd = RET_HEAD_DIM
    nh = RET_HEADS
    log_gammas = _ret_log_gammas()
    c0 = 3 * ATTN_WIDTH // hd
    mixed_blocks = ATTN_WIDTH // hd

    def body(q_ref, k_ref, v_ref, g_ref, raw_ref, dmix_ref, *rest):
        dsec_ref, qb, kb, vb, dretb, dq_acc, decay_tab = rest[1 + len(after):]
        h = pl.program_id(0)
        _fill_ret_decay(decay_tab, _select_by_index(h, log_gammas))
        qb[...] = q_ref[...].astype(BF16)
        kb[...] = k_ref[...].astype(BF16)
        vb[...] = v_ref[...].astype(BF16)
        ret = raw_ref[...]
        r = lax.rsqrt(jnp.mean(ret * ret, axis=-1, keepdims=True) + NORM_EPS)
        normed = ret * r
        g = g_ref[...]
        sg = _sigmoid(g)
        dout = dmix_ref[...]
        dsec_ref[3] = (dout * normed * sg * (1.0 + g * (1.0 - sg))).astype(BF16)
        dn = dout * g * sg
        dret = r * (dn - normed * jnp.mean(dn * normed, axis=-1, keepdims=True))
        dretb[...] = dret.astype(BF16)
        dq_acc[...] = jnp.zeros((s, hd), F32)

        def over_keys(j, _):
            krows = pl.ds(pl.multiple_of(j * t, t), t)
            kj = kb[krows, :]
            vj = vb[krows, :]

            def over_queries(i, carry):
                dk, dv = carry
                qrows = pl.ds(pl.multiple_of(i * t, t), t)
                qi = qb[qrows, :]
                doi = dretb[qrows, :]
                dec = decay_tab[i - j]
                a = (lax.dot_general(qi, kj, _NT_DIMS, preferred_element_type=F32) * dec).astype(BF16)
                da = (lax.dot_general(doi, vj, _NT_DIMS, preferred_element_type=F32) * dec).astype(BF16)
                dv = dv + lax.dot_general(a, doi, _TN_DIMS, preferred_element_type=F32)
                dk = dk + lax.dot_general(da, qi, _TN_DIMS, preferred_element_type=F32)
                dq_acc[qrows, :] += jnp.dot(da, kj, preferred_element_type=F32)
                return dk, dv

            zero = jnp.zeros((t, hd), F32)
            dk, dv = lax.fori_loop(j, nt, over_queries, (zero, zero))
            dsec_ref[1, krows, :] = dk.astype(BF16)
            dsec_ref[2, krows, :] = dv.astype(BF16)
            return 0

        lax.fori_loop(0, nt, over_keys, 0)
        dsec_ref[0] = dq_acc[...].astype(BF16)

    def col(off):
        return pl.BlockSpec((s, hd), lambda h: (0, off + h))

    return pl.pallas_call(
        body, name="ret_bwd", grid=(nh,),
        in_specs=[col(c0), col(c0 + nh), col(c0 + 2 * nh), col(c0 + 3 * nh), col(0), col(mixed_blocks)]
        + [ANY] * (1 + len(after)),
        out_specs=pl.BlockSpec((4, s, hd), lambda h: (1, 0, h)),
        out_shape=jax.ShapeDtypeStruct(dsec.shape, BF16),
        input_output_aliases={6: 0},
        scratch_shapes=[pltpu.VMEM((s, hd), BF16)] * 4 + [pltpu.VMEM((s, hd), F32)]
        + [pltpu.VMEM((nt, t, t), F32)],
        compiler_params=_params(("arbitrary",)),
    )(proj, proj, proj, proj, ret_raw, dmixed, dsec, *after)


_FLIPS = (2, 1, 3)


def _other_chips(x, y):
    return [(1 - x, y), (x, 1 - y), (1 - x, 1 - y)]


_HBM = pl.BlockSpec(memory_space=pltpu.HBM)
_SEM = pl.BlockSpec(memory_space=pltpu.SEMAPHORE)
_EFFECT = pltpu.SideEffectType.DATAFLOW_SIDE_EFFECTING


def _in_hbm(a):
    return pltpu.with_memory_space_constraint(a, pltpu.HBM)


def _weight_view(w, column_sharded):
    if column_sharded:
        return w.reshape(2, w.shape[0] // 2, w.shape[1])
    return w.reshape(N_CHIPS, 2, w.shape[0] // (2 * N_CHIPS), w.shape[1])


def _weight_unview(v):
    if v.ndim == 3:
        return v.reshape(2 * v.shape[1], v.shape[2])
    return v.reshape(N_CHIPS * 2 * v.shape[2], v.shape[3])


def _weight_region(buf, shard, half):
    if len(buf.shape) == 3:
        cols = buf.shape[2] // N_CHIPS
        return buf.at[half, :, pl.ds(shard * cols, cols)]
    return buf.at[shard, half]


def _remote(where, send_sem, recv_sem, to):
    return pltpu.make_async_remote_copy(src_ref=where, dst_ref=where, send_sem=send_sem, recv_sem=recv_sem,
                                        device_id=to, device_id_type=MESH)


def _for_my_shard(fn):
    x, y, _ = _place()
    for ss in range(N_CHIPS):
        pl.when(2 * x + y == ss)(functools.partial(fn, ss))


def _gather_start(views, name):
    n_w = len(views)

    def body(*refs):
        send_sems, recv_sems = refs[n_w:n_w + 2]
        bufs = refs[n_w + 2:]
        x, y, c = _place()

        def start(ss):
            for w in range(n_w):
                for j, chip in enumerate(_other_chips(x, y)):
                    _remote(_weight_region(bufs[w], ss, c), send_sems.at[3 * w + j], recv_sems.at[3 * w + j],
                            (*chip, c)).start()

        _for_my_shard(start)

    return pl.pallas_call(
        body, name=name,
        in_specs=[_HBM] * n_w, out_specs=[_SEM, _SEM] + [_HBM] * n_w,
        out_shape=[pltpu.SemaphoreType.DMA((3 * n_w,)), pltpu.SemaphoreType.DMA((3 * n_w,))]
        + [pltpu.HBM(v.shape, BF16) for v in views],
        input_output_aliases={w: 2 + w for w in range(n_w)},
        compiler_params=pltpu.CompilerParams(has_side_effects=_EFFECT),
    )(*[_in_hbm(v) for v in views])


def _gather_forward(views, which, send_sems, recv_sems, after, name):
    n_w = len(views)

    def body(*refs):
        send_in, recv_in = refs[n_w:n_w + 2]
        fwd_send, fwd_recv = refs[n_w + 3:n_w + 5]
        bufs = refs[n_w + 5:]
        x, y, c = _place()
        sibling = (x, y, 1 - c)

        def forward(ss):
            for i, w in enumerate(which):
                for j in range(3):
                    landed = _weight_region(bufs[i], ss ^ _FLIPS[j], c)
                    _remote(landed, send_in.at[3 * w + j], recv_in.at[3 * w + j], sibling).wait_recv()
                    _remote(landed, fwd_send.at[3 * i + j], fwd_recv.at[3 * i + j], sibling).start()

        _for_my_shard(forward)
        for i, w in enumerate(which):
            for j in range(3):
                _remote(_weight_region(bufs[i], 0, 0), send_in.at[3 * w + j], recv_in.at[3 * w + j],
                        sibling).wait_send()

    return pl.pallas_call(
        body, name=name,
        in_specs=[_HBM] * n_w + [_SEM, _SEM, ANY], out_specs=[_SEM, _SEM] + [_HBM] * n_w,
        out_shape=[pltpu.SemaphoreType.DMA((3 * n_w,)), pltpu.SemaphoreType.DMA((3 * n_w,))]
        + [pltpu.HBM(v.shape, BF16) for v in views],
        input_output_aliases={w: 2 + w for w in range(n_w)},
        compiler_params=pltpu.CompilerParams(has_side_effects=_EFFECT),
    )(*views, send_sems, recv_sems, after)


def _gather_end(views, fwd_send, fwd_recv, after, name):
    n_w = len(views)

    def body(*refs):
        fwd_send_ref, fwd_recv_ref = refs[n_w:n_w + 2]
        bufs = refs[n_w + 3:]
        x, y, c = _place()
        for i in range(n_w):
            for j in range(3):
                cp = _remote(_weight_region(bufs[i], 0, 0), fwd_send_ref.at[3 * i + j], fwd_recv_ref.at[3 * i + j],
                             (x, y, 1 - c))
                cp.wait_recv()
                cp.wait_send()

    outs = pl.pallas_call(
        body, name=name,
        in_specs=[_HBM] * n_w + [_SEM, _SEM, ANY], out_specs=[_HBM] * n_w,
        out_shape=[pltpu.HBM(v.shape, BF16) for v in views],
        input_output_aliases={w: w for w in range(n_w)},
        compiler_params=pltpu.CompilerParams(has_side_effects=_EFFECT),
    )(*views, fwd_send, fwd_recv, after)
    return [_weight_unview(o) for o in outs]


def _split_start(name, bufs, n_sems, copies):
    n = len(bufs)

    def body(*refs):
        send_sems, recv_sems = refs[n:n + 2]
        for cp in copies(refs[n + 2:], send_sems, recv_sems):
            cp.start()

    outs = pl.pallas_call(
        body, name=name,
        in_specs=[_HBM] * n, out_specs=[_SEM, _SEM] + [_HBM] * n,
        out_shape=[pltpu.SemaphoreType.DMA((n_sems,)), pltpu.SemaphoreType.DMA((n_sems,))]
        + [pltpu.HBM(b.shape, b.dtype) for b in bufs],
        input_output_aliases={i: 2 + i for i in range(n)},
        compiler_params=pltpu.CompilerParams(has_side_effects=_EFFECT),
    )(*[_in_hbm(b) for b in bufs])
    return outs[0], outs[1], list(outs[2:])


def _split_wait(name, bufs, send_sems, recv_sems, copies, after):
    n = len(bufs)

    def body(*refs):
        send_ref, recv_ref = refs[n:n + 2]
        for cp in copies(refs[n + 3:], send_ref, recv_ref):
            cp.wait()

    return list(pl.pallas_call(
        body, name=name,
        in_specs=[_HBM] * n + [_SEM, _SEM, ANY], out_specs=[_HBM] * n,
        out_shape=[pltpu.HBM(b.shape, b.dtype) for b in bufs],
        input_output_aliases={i: i for i in range(n)},
        compiler_params=pltpu.CompilerParams(has_side_effects=_EFFECT),
    )(*bufs, send_sems, recv_sems, after))


def _halves_copies(n_w):
    def copies(bufs, send_sems, recv_sems):
        x, y, c = _place()
        out = []
        for w in range(n_w):
            view, land = bufs[w], bufs[n_w + w]
            src = view.at[1 - c] if len(view.shape) == 3 else view.at[:, 1 - c]
            out.append(pltpu.make_async_remote_copy(
                src_ref=src, dst_ref=land, send_sem=send_sems.at[w], recv_sem=recv_sems.at[w],
                device_id=(x, y, 1 - c), device_id_type=MESH))
        return out
    return copies


def _pieces_copies(n_w):
    def copies(bufs, send_sems, recv_sems):
        x, y, c = _place()
        out = []
        for w in range(n_w):
            for j, (cx, cy) in enumerate(_other_chips(x, y)):
                out.append(pltpu.make_async_remote_copy(
                    src_ref=bufs[w].at[2 * cx + cy], dst_ref=bufs[n_w + w].at[j],
                    send_sem=send_sems.at[3 * w + j], recv_sem=recv_sems.at[3 * w + j],
                    device_id=(cx, cy, c), device_id_type=MESH))
        return out
    return copies


def _join_copies(n_w):
    def copies(bufs, send_sems, recv_sems):
        x, y, c = _place()
        return [pltpu.make_async_remote_copy(
            src_ref=bufs[w].at[c], dst_ref=bufs[w].at[c], send_sem=send_sems.at[w], recv_sem=recv_sems.at[w],
            device_id=(x, y, 1 - c), device_id_type=MESH) for w in range(n_w)]
    return copies


def _grad_view(g, column_sharded):
    return _weight_view(g, column_sharded)


def _halves_landing(view):
    shape = view.shape[1:] if view.ndim == 3 else (N_CHIPS,) + view.shape[2:]
    return lax.empty(shape, BF16)


def _halves_start(tag, grads, column_sharded):
    views = [_weight_view(g, cs) for g, cs in zip(grads, column_sharded)]
    n = len(views)
    return _split_start("halves_start_" + tag, views + [_halves_landing(v) for v in views], n, _halves_copies(n))


def _halves_wait(tag, state, after):
    send_sems, recv_sems, bufs = state
    n = len(bufs) // 2
    bufs = _split_wait("halves_wait_" + tag, bufs, send_sems, recv_sems, _halves_copies(n), after)
    return bufs[:n], bufs[n:]


def _pieces_start(tag, pieces):
    n = len(pieces)
    landing = [lax.empty((3,) + p.shape[1:], BF16) for p in pieces]
    return _split_start("pieces_start_" + tag, list(pieces) + landing, 3 * n, _pieces_copies(n))


def _pieces_wait(tag, state, after):
    send_sems, recv_sems, bufs = state
    n = len(bufs) // 2
    bufs = _split_wait("pieces_wait_" + tag, bufs, send_sems, recv_sems, _pieces_copies(n), after)
    return bufs[:n], bufs[n:]


def _join_start(tag, shards):
    n = len(shards)
    return _split_start("join_start_" + tag, list(shards), n, _join_copies(n))


def _join_wait(tag, state, after):
    send_sems, recv_sems, bufs = state
    bufs = _split_wait("join_wait_" + tag, bufs, send_sems, recv_sems, _join_copies(len(bufs)), after)
    return [b.reshape(2 * b.shape[1], b.shape[2]) for b in bufs]


def _chip_sum_col(g3, sib, c_arr, name):
    _, hk, n = g3.shape
    cols = n // N_CHIPS
    tr = _row_tile(hk, cols * 2, limit=1024 * 1024)

    def body(c_ref, g_ref, s_ref, o_ref):
        del c_ref
        o_ref[...] = (g_ref[...].astype(F32) + s_ref[...].astype(F32)).astype(BF16)

    grid_spec = pltpu.PrefetchScalarGridSpec(
        num_scalar_prefetch=1, grid=(N_CHIPS, hk // tr),
        in_specs=[pl.BlockSpec((None, tr, cols), lambda p, r, c_ref: (c_ref[0], r, p)),
                  pl.BlockSpec((tr, cols), lambda p, r, c_ref: (r, p))],
        out_specs=pl.BlockSpec((None, tr, cols), lambda p, r, c_ref: (p, r, 0)))
    return pl.pallas_call(
        body, name=name, grid_spec=grid_spec,
        out_shape=jax.ShapeDtypeStruct((N_CHIPS, hk, cols), BF16),
        compiler_params=_params(("parallel", "parallel")),
    )(c_arr, g3, sib)


def _chip_sum_row(g4, sib, c_arr, name):
    _, _, hr, n = g4.shape
    tr = _row_tile(hr, n * 2, limit=1024 * 1024)

    def body(c_ref, g_ref, s_ref, o_ref):
        del c_ref
        o_ref[...] = (g_ref[...].astype(F32) + s_ref[...].astype(F32)).astype(BF16)

    grid_spec = pltpu.PrefetchScalarGridSpec(
        num_scalar_prefetch=1, grid=(N_CHIPS, hr // tr),
        in_specs=[pl.BlockSpec((None, None, tr, n), lambda p, r, c_ref: (p, c_ref[0], r, 0)),
                  pl.BlockSpec((None, tr, n), lambda p, r, c_ref: (p, r, 0))],
        out_specs=pl.BlockSpec((None, tr, n), lambda p, r, c_ref: (p, r, 0)))
    return pl.pallas_call(
        body, name=name, grid_spec=grid_spec,
        out_shape=jax.ShapeDtypeStruct((N_CHIPS, hr, n), BF16),
        compiler_params=_params(("parallel", "parallel")),
    )(c_arr, g4, sib)


def _sum_pieces(pieces, received, place_arr, name):
    _, r, n = pieces.shape
    tr = _row_tile(r, n * 4, limit=1024 * 1024)

    def body(p_ref, own_ref, r0_ref, r1_ref, r2_ref, o_ref):
        del p_ref
        acc = own_ref[...].astype(F32) + r0_ref[...].astype(F32)
        acc = acc + r1_ref[...].astype(F32)
        o_ref[...] = acc + r2_ref[...].astype(F32)

    def recv_spec(j):
        return pl.BlockSpec((None, tr, n), lambda i, p_ref: (j, i, 0))

    grid_spec = pltpu.PrefetchScalarGridSpec(
        num_scalar_prefetch=1, grid=(r // tr,),
        in_specs=[pl.BlockSpec((None, tr, n), lambda i, p_ref: (p_ref[0], i, 0)),
                  recv_spec(0), recv_spec(1), recv_spec(2)],
        out_specs=pl.BlockSpec((None, tr, n), lambda i, p_ref: (p_ref[1], i, 0)))
    return pl.pallas_call(
        body, name=name, grid_spec=grid_spec,
        out_shape=jax.ShapeDtypeStruct((2, r, n), F32),
        compiler_params=_params(("parallel",)),
    )(place_arr, pieces, received, received, received)


def _norm_weights_step(parts, w, m, v, after=()):
    rows, d = parts.shape
    after = tuple(after)

    def body(p_ref, w_ref, m_ref, v_ref, *rest):
        g_ref, d_ref, mo_ref, vo_ref, gathered, send_sems, recv_sems = rest[len(after):]
        x, y, c = _place()
        me = 4 * x + 2 * y + c
        gathered[me] = p_ref[...]
        copies = []
        for k in range(1, N_DEV):
            peer = (x ^ ((k >> 2) & 1), y ^ ((k >> 1) & 1), c ^ (k & 1))
            copies.append(pltpu.make_async_remote_copy(
                src_ref=p_ref, dst_ref=gathered.at[me], send_sem=send_sems.at[k - 1],
                recv_sem=recv_sems.at[k - 1], device_id=peer, device_id_type=MESH))
        for cp in copies:
            cp.start()
        for cp in copies:
            cp.wait()
        g = gathered[0]
        for k in range(1, N_DEV):
            g = g + gathered[k]
        delta, m_new, v_new = _adamw_math(w_ref[...], g, m_ref[...], v_ref[...])
        g_ref[...] = g
        d_ref[...] = delta
        mo_ref[...] = m_new
        vo_ref[...] = v_new

    vmem = pl.BlockSpec(memory_space=pltpu.VMEM)
    shp = jax.ShapeDtypeStruct((rows, d), F32)
    return pl.pallas_call(
        body, name="norm_weights_step",
        in_specs=[vmem] * 4 + [ANY] * len(after), out_specs=[vmem] * 4, out_shape=[shp] * 4,
        scratch_shapes=[pltpu.VMEM((N_DEV, rows, d), F32), pltpu.SemaphoreType.DMA((N_DEV - 1,)),
                        pltpu.SemaphoreType.DMA((N_DEV - 1,))],
        compiler_params=pltpu.CompilerParams(has_side_effects=True),
    )(parts, w, m, v, *after)


def kernel(x, norm_mix_w, w_in, w_out, norm_ffn_w, w_gate, w_up, w_down, norm_final_w, loss_target, m_norm_mix_w, m_w_in, m_w_out, m_norm_ffn_w, m_w_gate, m_w_up, m_w_down, m_norm_final_w, v_norm_mix_w, v_w_in, v_w_out, v_norm_ffn_w, v_w_gate, v_w_up, v_w_down, v_norm_final_w):
    s, d = x.shape[1], x.shape[2]
    xs = x.reshape(s, d)
    target = loss_target.reshape(s, d)
    big = {"w_in": (w_in, m_w_in, v_w_in), "w_out": (w_out, m_w_out, v_w_out),
           "w_gate": (w_gate, m_w_gate, v_w_gate), "w_up": (w_up, m_w_up, v_w_up),
           "w_down": (w_down, m_w_down, v_w_down)}
    big = {k: tuple(a.reshape(a.shape[1:]) for a in t) for k, t in big.items()}
    col_names, row_names = ("w_in", "w_gate", "w_up"), ("w_out", "w_down")
    n_in = N_CHIPS * big["w_in"][0].shape[1]
    ffn = N_CHIPS * big["w_gate"][0].shape[1]
    mix = ATTN_WIDTH + RET_WIDTH
    c_arr = lax.axis_index("c").astype(I32).reshape(1)
    shard_arr = (2 * lax.axis_index("x") + lax.axis_index("y")).astype(I32).reshape(1)
    place_arr = jnp.concatenate([shard_arr, c_arr])

    def cast(k, after=()):
        return _weight_view(_cast_into_full(big[k][0], shard_arr, k in col_names, "cast_" + k, after), k in col_names)

    send_in, recv_in, v_in = _gather_start([cast("w_in")], "gather_start_in")
    rest = ("w_out", "w_gate", "w_up", "w_down")
    send_sems, recv_sems, v_out, v_gate, v_up, v_down = _gather_start(
        [cast(k, after=[v_in]) for k in rest], "gather_start_rest")

    sec = ATTN_WIDTH

    def section(p, rows):
        return pl.BlockSpec((None, rows, sec), lambda i, j, kk: (p, i, 0))

    h1 = _rms_fwd(xs, norm_mix_w, "rms_mix_fwd")
    fs, fr, v_in = _gather_forward([v_in], [0], send_in, recv_in, v_out, "gather_forward_in")
    wi, = _gather_end([v_in], fs, fr, h1, "gather_end_in")
    proj, = _matmul("in_proj", "nn", [h1], [wi], [0], s, n_in, d, s, 512, d, [], [F32], _epi_plain)
    fs_o, fr_o, v_out = _gather_forward([v_out], [0], send_sems, recv_sems, proj, "gather_forward_out")
    mixed, attn_o, lse = _attn_fwd(proj, after=[v_out])
    fs_g, fr_g, v_gate = _gather_forward([v_gate], [1], send_sems, recv_sems, attn_o, "gather_forward_gate")
    mixed, ret_raw = _ret_fwd(proj, mixed, after=[v_gate])
    wo, = _gather_end([v_out], fs_o, fr_o, ret_raw, "gather_end_out")
    x1, = _matmul("out_proj", "nn", [mixed, mixed], [wo, wo], [0, 0], s, d, sec, s, 512, sec, [xs], [F32],
                  _epi_residual, b_koff=[0, 1], a_specs=[section(0, s), section(1, s)])
    h2 = _rms_fwd(x1, norm_ffn_w, "rms_ffn_fwd")
    fs_u, fr_u, v_up = _gather_forward([v_up], [2], send_sems, recv_sems, h2, "gather_forward_up")
    wg, = _gather_end([v_gate], fs_g, fr_g, v_up, "gather_end_gate")
    wu, = _gather_end([v_up], fs_u, fr_u, wg, "gather_end_up")
    gate, up, act = _matmul("gate_up", "nn", [h2, h2], [wg, wu], [0, 1], s, ffn, d, s, 256, d, [],
                            [F32, F32, BF16], _epi_swiglu)
    fs, fr, v_down = _gather_forward([v_down], [3], send_sems, recv_sems, act, "gather_forward_down")
    wd, = _gather_end([v_down], fs, fr, act, "gather_end_down")
    x2, = _matmul("down_proj", "nn", [act], [wd], [0], s, d, ffn, s // 2, 256, ffn, [x1], [F32],
                  _epi_residual)
    loss_row, dx2, dx2b, dwf = _final_norm_loss(x2, norm_final_w.reshape(1, d), target, "final_norm_loss")

    names = col_names + row_names
    grads, new = {}, {}

    def chip_sums(tag_names, views, sibs):
        return [(_chip_sum_col if k in col_names else _chip_sum_row)(v, sb, c_arr, "chip_sum_" + k)
                for k, v, sb in zip(tag_names, views, sibs)]

    def piece_sums(tag_names, pieces, received):
        return [_sum_pieces(p, r, place_arr, "sum_pieces_" + k) for k, p, r in zip(tag_names, pieces, received)]

    def update(k):
        new[k] = _adamw(big[k][0], grads[k], big[k][1], big[k][2], "adamw_" + k)

    dgate, dup = _matmul("d_act", "nt", [dx2b], [wd], [0], s, ffn, d, s, 256, d, [gate, up],
                         [BF16, BF16], _epi_swiglu_bwd)
    g_wd, = _matmul("g_w_down", "tn", [act], [dx2b], [0], ffn, d, s, 512, d, s, [], [BF16], _epi_plain)
    halves_d = _halves_start("down", [g_wd], [False])
    dh2, = _matmul("d_h2", "nt", [dgate, dup], [wg, wu], [0, 0], s, d, ffn, s // 4, 256, ffn, [], [F32],
                   _epi_plain, after=halves_d[2][-1:])
    pieces_d = _pieces_start("down", chip_sums(["w_down"], *_halves_wait("down", halves_d, dh2)))
    g_wg, g_wu = _matmul("g_w_gate_up", "tn", [h2, h2], [dgate, dup], [0, 1], d, ffn, s, 1024, 512, s, [],
                         [BF16, BF16], _epi_two, after=pieces_d[2][-1:])
    halves_gu = _halves_start("gate_up", [g_wg, g_wu], [True, True])
    dx1, dx1b, dw_ffn = _rms_bwd(x1, norm_ffn_w, dh2, dx2, "rms_ffn_bwd", after=halves_gu[2][-1:])

    dmixed, = _matmul("d_mixed", "nt", [dx1b], [wo], [0], s, mix, d, s, 512, d, [], [F32], _epi_plain)
    pieces_gu = _pieces_start("gate_up", chip_sums(["w_gate", "w_up"], *_halves_wait("gate_up", halves_gu, dmixed)))
    per = sec // 512
    g_wo, = _matmul("g_w_out", "tn", [mixed], [dx1b], [0], mix, d, s, 512, d, s, [], [BF16], _epi_plain,
                    after=pieces_gu[2][-1:],
                    a_specs=[pl.BlockSpec((None, s, 512), lambda i, j, kk: (i // per, 0, i % per))])
    halves_o = _halves_start("out", [g_wo], [False])
    dsec = _attn_bwd(proj, attn_o, lse, dmixed, after=halves_o[2][-1:])
    pieces_o = _pieces_start("out", chip_sums(["w_out"], *_halves_wait("out", halves_o, dsec)))
    dsec = _ret_bwd(proj, ret_raw, dmixed, dsec, after=pieces_o[2][-1:])
    where = [0, 1, 2, 4, 5, 6, 7]
    n_sec = len(where)
    g_wi, = _matmul("g_w_in", "tn", [h1], [dsec], [0], d, n_in, s, 1024, sec, s, [], [BF16], _epi_plain,
                    b_specs=[pl.BlockSpec((None, s, sec), lambda i, j, kk: (j + (j >= 3).astype(I32), 0, 0))])
    halves_i = _halves_start("in", [g_wi], [True])
    dh1, = _matmul("d_h1", "nt", [dsec] * n_sec, [wi] * n_sec, [0] * n_sec, s, d, sec, s // 2, 256, sec, [], [F32],
                   _epi_plain, b_koff=list(range(n_sec)), after=halves_i[2][-1:],
                   a_specs=[section(p, s // 2) for p in where])
    pieces_i = _pieces_start("in", chip_sums(["w_in"], *_halves_wait("in", halves_i, dh1)))
    grad_x, _, dw_mix = _rms_bwd(xs, norm_mix_w, dh1, dx1, "rms_mix_bwd", after=pieces_i[2][-1:])

    def rows8(*vs):
        return jnp.concatenate([v.reshape(1, d) for v in vs] + [jnp.zeros((8 - len(vs), d), F32)], axis=0)

    join_d = _join_start("down", piece_sums(["w_down"], *_pieces_wait("down", pieces_d, grad_x)))
    join_gu = _join_start("gate_up", piece_sums(["w_gate", "w_up"], *_pieces_wait("gate_up", pieces_gu, join_d[2][0])))
    join_o = _join_start("out", piece_sums(["w_out"], *_pieces_wait("out", pieces_o, join_gu[2][0])))
    grads["w_down"], = _join_wait("down", join_d, join_o[2][0])
    update("w_down")
    grads["w_gate"], grads["w_up"] = _join_wait("gate_up", join_gu, new["w_down"][0])
    update("w_gate")
    update("w_up")
    grads["w_out"], = _join_wait("out", join_o, new["w_up"][0])
    update("w_out")
    join_i = _join_start("in", piece_sums(["w_in"], *_pieces_wait("in", pieces_i, new["w_out"][0])))
    ng, nd, nm, nv = _norm_weights_step(
        rows8(dw_mix, dw_ffn, dwf), rows8(norm_mix_w, norm_ffn_w, norm_final_w),
        rows8(m_norm_mix_w, m_norm_ffn_w, m_norm_final_w), rows8(v_norm_mix_w, v_norm_ffn_w, v_norm_final_w),
        after=join_i[2][:1])
    grads["w_in"], = _join_wait("in", join_i, ng)
    update("w_in")

    loss = lax.psum(loss_row[0, 0], ("x", "y", "c"))

    def pack(small, per_weight):
        lead = lambda a: a.reshape((1,) + a.shape)
        return (small[0:1], lead(per_weight["w_in"]), lead(per_weight["w_out"]), small[1:2],
                lead(per_weight["w_gate"]), lead(per_weight["w_up"]), lead(per_weight["w_down"]), small[2])

    return (loss, grad_x.reshape(1, s, d),
            *pack(ng, {k: new[k][3] for k in names}),
            *pack(nd, {k: new[k][0] for k in names}),
            *pack(nm, {k: new[k][1] for k in names}),
            *pack(nv, {k: new[k][2] for k in names}))
```

```python
import functools
import math

import jax
import jax.numpy as jnp
from jax import lax
from jax.experimental import pallas as pl
from jax.experimental.pallas import tpu as pltpu

F32 = jnp.float32
BF16 = jnp.bfloat16
I32 = jnp.int32
MESH = pl.DeviceIdType.MESH
ANY = pl.BlockSpec(memory_space=pl.ANY)

ATTN_HEADS = 8
ATTN_HEAD_DIM = 128
RET_HEADS = 4
RET_HEAD_DIM = 256
ATTN_WIDTH = ATTN_HEADS * ATTN_HEAD_DIM
RET_WIDTH = RET_HEADS * RET_HEAD_DIM
DILATED_PATTERNS = ((128, 1), (512, 4), (2048, 16))
NORM_EPS = 1e-6
ADAM_LR = 0.001
ADAM_B1 = 0.9
ADAM_B2 = 0.999
ADAM_EPS = 1e-08
ADAM_WD = 0.01
ADAM_STEP = 10

N_CHIPS = 4
N_DEV = 8
NEG_BIG = -1e30
SEQ_TILE = 512
ATTN_FWD_HEADS_PER_STEP = 2
ATTN_HEADS_PER_STEP = 1
VMEM_LIMIT_BYTES = 56 * 1024 * 1024


def _params(semantics=None, vmem=VMEM_LIMIT_BYTES):
    return pltpu.CompilerParams(dimension_semantics=semantics, vmem_limit_bytes=vmem)


def _row_tile(rows, row_bytes, limit=2 * 1024 * 1024, mult=16):
    best = None
    for t in range(mult, rows + 1, mult):
        if rows % t == 0 and t * row_bytes <= limit:
            best = t
    assert best is not None, (rows, row_bytes)
    return best


def _sigmoid(x):
    return 1.0 / (1.0 + jnp.exp(-x))


def _select_by_index(idx, values):
    out = jnp.float32(values[-1])
    for i in range(len(values) - 2, -1, -1):
        out = jnp.where(idx == i, jnp.float32(values[i]), out)
    return out


def _place():
    x, y, c = lax.axis_index("x"), lax.axis_index("y"), lax.axis_index("c")
    return x, y, c


def _cast_into_full(w, shard_arr, column_sharded, name, after=()):
    after = tuple(after)
    rows, cols = w.shape
    tr = _row_tile(rows, cols * 4)
    steps = rows // tr
    if column_sharded:
        out_shape, out_map = (rows, N_CHIPS * cols), (lambda i, s_ref: (i, s_ref[0]))
    else:
        out_shape, out_map = (N_CHIPS * rows, cols), (lambda i, s_ref: (s_ref[0] * steps + i, 0))

    def body(s_ref, w_ref, *rest):
        del s_ref
        rest[-1][...] = w_ref[...].astype(BF16)

    grid_spec = pltpu.PrefetchScalarGridSpec(
        num_scalar_prefetch=1, grid=(steps,),
        in_specs=[pl.BlockSpec((tr, cols), lambda i, s_ref: (i, 0))] + [ANY] * len(after),
        out_specs=pl.BlockSpec((tr, cols), out_map))
    return pl.pallas_call(
        body, name=name, grid_spec=grid_spec,
        out_shape=jax.ShapeDtypeStruct(out_shape, BF16),
        compiler_params=_params(("parallel",)),
    )(shard_arr, w, *after)


def _rms_fwd(x, w, name):
    rows, d = x.shape
    tr = 256

    def body(x_ref, w_ref, h_ref):
        xv = x_ref[...]
        r = lax.rsqrt(jnp.mean(xv * xv, axis=-1, keepdims=True) + NORM_EPS)
        h_ref[...] = (xv * r * w_ref[...]).astype(BF16)

    return pl.pallas_call(
        body, name=name, grid=(rows // tr,),
        in_specs=[pl.BlockSpec((tr, d), lambda i: (i, 0)), pl.BlockSpec((1, d), lambda i: (0, 0))],
        out_specs=pl.BlockSpec((tr, d), lambda i: (i, 0)),
        out_shape=jax.ShapeDtypeStruct((rows, d), BF16),
        compiler_params=_params(("parallel",)),
    )(x, w)


def _rms_bwd(x, w, dh, dres, name, after=()):
    rows, d = x.shape
    tr = 256
    after = tuple(after)

    def body(x_ref, w_ref, dh_ref, dres_ref, *rest):
        dx_ref, dxb_ref, dw_ref = rest[len(after):]
        xv = x_ref[...]
        r = lax.rsqrt(jnp.mean(xv * xv, axis=-1, keepdims=True) + NORM_EPS)
        xhat = xv * r
        dy = dh_ref[...]
        dxhat = dy * w_ref[...]
        dx = dres_ref[...] + r * (dxhat - xhat * jnp.mean(dxhat * xhat, axis=-1, keepdims=True))
        dx_ref[...] = dx
        dxb_ref[...] = dx.astype(BF16)
        part = jnp.sum(dy * xhat, axis=0, keepdims=True)

        @pl.when(pl.program_id(0) == 0)
        def _():
            dw_ref[...] = part

        @pl.when(pl.program_id(0) != 0)
        def _():
            dw_ref[...] += part

    row = pl.BlockSpec((tr, d), lambda i: (i, 0))
    vec = pl.BlockSpec((1, d), lambda i: (0, 0))
    return pl.pallas_call(
        body, name=name, grid=(rows // tr,),
        in_specs=[row, vec, row, row] + [ANY] * len(after),
        out_specs=[row, row, vec],
        out_shape=[jax.ShapeDtypeStruct((rows, d), F32), jax.ShapeDtypeStruct((rows, d), BF16),
                   jax.ShapeDtypeStruct((1, d), F32)],
        compiler_params=_params(("arbitrary",)),
    )(x, w, dh, dres, *after)


def _final_norm_loss(x2, w, target, name):
    rows, d = x2.shape
    tr = 256

    def body(x_ref, w_ref, t_ref, loss_ref, dx_ref, dxb_ref, dw_ref):
        xv = x_ref[...]
        wv = w_ref[...]
        r = lax.rsqrt(jnp.mean(xv * xv, axis=-1, keepdims=True) + NORM_EPS)
        xhat = xv * r
        err = xhat * wv - t_ref[...]
        part_loss = 0.5 * jnp.sum(jnp.mean(err * err, axis=-1, keepdims=True), axis=0, keepdims=True)
        dy = err * (1.0 / d)
        dxhat = dy * wv
        dx = r * (dxhat - xhat * jnp.mean(dxhat * xhat, axis=-1, keepdims=True))
        dx_ref[...] = dx
        dxb_ref[...] = dx.astype(BF16)
        part_dw = jnp.sum(dy * xhat, axis=0, keepdims=True)
        part_loss = jnp.broadcast_to(part_loss, (1, 128))

        @pl.when(pl.program_id(0) == 0)
        def _():
            dw_ref[...] = part_dw
            loss_ref[...] = part_loss

        @pl.when(pl.program_id(0) != 0)
        def _():
            dw_ref[...] += part_dw
            loss_ref[...] += part_loss

    row = pl.BlockSpec((tr, d), lambda i: (i, 0))
    vec = pl.BlockSpec((1, d), lambda i: (0, 0))
    return pl.pallas_call(
        body, name=name, grid=(rows // tr,),
        in_specs=[row, vec, row],
        out_specs=[pl.BlockSpec((1, 128), lambda i: (0, 0)), row, row, vec],
        out_shape=[jax.ShapeDtypeStruct((1, 128), F32), jax.ShapeDtypeStruct((rows, d), F32),
                   jax.ShapeDtypeStruct((rows, d), BF16), jax.ShapeDtypeStruct((1, d), F32)],
        compiler_params=_params(("arbitrary",)),
    )(x2, w, target)


def _adamw_math(w, g, m, v):
    m = ADAM_B1 * m + (1.0 - ADAM_B1) * g
    v = ADAM_B2 * v + (1.0 - ADAM_B2) * (g * g)
    m_hat = m / (1.0 - ADAM_B1 ** ADAM_STEP)
    v_hat = v / (1.0 - ADAM_B2 ** ADAM_STEP)
    delta = -ADAM_LR * (m_hat / (jnp.sqrt(v_hat) + ADAM_EPS) + ADAM_WD * w)
    return delta, m, v


def _adamw(w, g, m, v, name):
    rows, cols = w.shape
    tr = _row_tile(rows, cols * 4)

    def body(w_ref, g_ref, m_ref, v_ref, d_ref, mo_ref, vo_ref, go_ref):
        g = g_ref[...]
        delta, m_new, v_new = _adamw_math(w_ref[...], g, m_ref[...], v_ref[...])
        d_ref[...] = delta
        mo_ref[...] = m_new
        vo_ref[...] = v_new
        go_ref[...] = g

    blk = pl.BlockSpec((tr, cols), lambda i: (i, 0))
    shp = jax.ShapeDtypeStruct((rows, cols), F32)
    return pl.pallas_call(
        body, name=name, grid=(rows // tr,),
        in_specs=[blk] * 4, out_specs=[blk] * 4, out_shape=[shp] * 4,
        compiler_params=_params(("parallel",)),
    )(w, g, m, v)


_DOT_DIMS = {"nn": ((1,), (0,)), "nt": ((1,), (1,)), "tn": ((0,), (0,))}


def _matmul(name, mode, a_list, b_list, acc_of, m, n, k, tm, tn, tk, extras, out_dtypes, epilogue,
            a_koff=None, b_koff=None, after=(), a_specs=None, b_specs=None):
    after = tuple(after)
    assert m % tm == 0 and n % tn == 0 and k % tk == 0, (name, m, n, k, tm, tn, tk)
    nk = k // tk
    n_acc = max(acc_of) + 1
    n_pairs = len(a_list)
    a_koff = a_koff or [0] * n_pairs
    b_koff = b_koff or [0] * n_pairs
    dims = (_DOT_DIMS[mode], ((), ()))
    n_ext, n_out = len(extras), len(out_dtypes)

    def body(*refs):
        a_refs = refs[:n_pairs]
        b_refs = refs[n_pairs:2 * n_pairs]
        e_refs = refs[2 * n_pairs:2 * n_pairs + n_ext]
        first_out = 2 * n_pairs + n_ext + len(after)
        o_refs = refs[first_out:first_out + n_out]
        acc_refs = refs[first_out + n_out:]
        parts = [None] * n_acc
        for p in range(n_pairs):
            d = lax.dot_general(a_refs[p][...], b_refs[p][...], dims, preferred_element_type=F32)
            parts[acc_of[p]] = d if parts[acc_of[p]] is None else parts[acc_of[p]] + d

        def finish(accs):
            outs = epilogue(accs, [e[...] for e in e_refs])
            for o_ref, o in zip(o_refs, outs):
                o_ref[...] = o.astype(o_ref.dtype)

        if nk == 1:
            finish(parts)
        else:
            kk = pl.program_id(2)

            @pl.when(kk == 0)
            def _():
                for acc_ref, part in zip(acc_refs, parts):
                    acc_ref[...] = part

            @pl.when(kk != 0)
            def _():
                for acc_ref, part in zip(acc_refs, parts):
                    acc_ref[...] += part

            @pl.when(kk == nk - 1)
            def _():
                finish([acc_ref[...] for acc_ref in acc_refs])

    def a_spec(off):
        if mode == "tn":
            return pl.BlockSpec((tk, tm), lambda i, j, kk: (kk + off, i))
        return pl.BlockSpec((tm, tk), lambda i, j, kk: (i, kk + off))

    def b_spec(off):
        if mode == "nt":
            return pl.BlockSpec((tn, tk), lambda i, j, kk: (j, kk + off))
        return pl.BlockSpec((tk, tn), lambda i, j, kk: (kk + off, j))

    tile = pl.BlockSpec((tm, tn), lambda i, j, kk: (i, j))
    scratch = [pltpu.VMEM((tm, tn), F32) for _ in range(n_acc)] if nk > 1 else []
    return pl.pallas_call(
        body, name=name, grid=(m // tm, n // tn, nk),
        in_specs=(a_specs or [a_spec(o) for o in a_koff]) + (b_specs or [b_spec(o) for o in b_koff])
        + [tile] * n_ext + [ANY] * len(after),
        out_specs=[tile] * n_out,
        out_shape=[jax.ShapeDtypeStruct((m, n), dt) for dt in out_dtypes],
        scratch_shapes=scratch,
        compiler_params=_params(("parallel", "parallel", "arbitrary")),
    )(*a_list, *b_list, *extras, *after)


def _epi_plain(accs, extras):
    return (accs[0],)


def _epi_residual(accs, extras):
    return (accs[0] + extras[0],)


def _epi_two(accs, extras):
    return accs[0], accs[1]


def _epi_swiglu(accs, extras):
    g, u = accs
    return g, u, g * _sigmoid(g) * u


def _epi_swiglu_bwd(accs, extras):
    da = accs[0]
    g, u = (e.astype(F32) for e in extras)
    sg = _sigmoid(g)
    dg = da * u * sg * (1.0 + g * (1.0 - sg))
    du = da * g * sg
    return dg, du


_NT_DIMS = (((1,), (1,)), ((), ()))
_TN_DIMS = (((0,), (0,)), ((), ()))


def _tile_delta(tq, tk):
    return lax.broadcasted_iota(I32, (tq, tk), 0) - lax.broadcasted_iota(I32, (tq, tk), 1)


def _attn_mask_bias(delta, slope):
    count = jnp.zeros(delta.shape, I32)
    for window, dilation in DILATED_PATTERNS:
        hit = ((delta & (dilation - 1)) == 0) & (delta <= window)
        count = count + jnp.where(hit, 1, 0)
    valid = (delta >= 0) & (count > 0)
    logm = jnp.where(count == 3, math.log(3.0), jnp.where(count == 2, math.log(2.0), 0.0))
    return valid, logm - slope * delta.astype(F32)


def _fill_attn_bias(tab_ref, slope):
    nb, t, _ = tab_ref.shape
    base = _tile_delta(t, t)
    for b in range(nb):
        valid, bias = _attn_mask_bias(base + b * t, slope)
        tab_ref[b] = jnp.where(valid, bias, NEG_BIG)


def _fill_ret_decay(tab_ref, log_gamma):
    nb, t, _ = tab_ref.shape
    base = _tile_delta(t, t)
    for b in range(nb):
        tab_ref[b] = _ret_decay(base + b * t, log_gamma)


def _alibi_slopes():
    return [2.0 ** (-8.0 * (h + 1) / ATTN_HEADS) for h in range(ATTN_HEADS)]


def _attn_fwd(proj, after=()):
    s = proj.shape[0]
    t = SEQ_TILE
    hd = ATTN_HEAD_DIM
    hp = ATTN_FWD_HEADS_PER_STEP
    ng = ATTN_HEADS // hp
    w = hp * hd
    scale = 1.0 / math.sqrt(hd)
    slopes = _alibi_slopes()

    def body(q_ref, k_ref, v_ref, *rest):
        mix_ref, o_ref, lse_ref, kb, vb, bias_tab = rest[len(after):]
        g = pl.program_id(0)
        i = pl.program_id(1)

        @pl.when(i == 0)
        def _():
            kb[...] = k_ref[...].astype(BF16)
            vb[...] = v_ref[...].astype(BF16)
            for u in range(hp):
                _fill_attn_bias(bias_tab.at[u], _select_by_index(g * hp + u, slopes))

        qs = [q_ref[:, u * hd:(u + 1) * hd].astype(BF16) for u in range(hp)]

        def step(j, carry):
            rows = pl.ds(pl.multiple_of(j * t, t), t)
            out = []
            for u in range(hp):
                m_i, l_i, acc = carry[u]
                lanes = slice(u * hd, (u + 1) * hd)
                sc = lax.dot_general(qs[u], kb[rows, lanes], _NT_DIMS, preferred_element_type=F32) * scale
                sc = sc + bias_tab[u, i - j]
                m_new = jnp.maximum(m_i, jnp.max(sc, axis=-1, keepdims=True))
                p = jnp.exp(sc - m_new)
                alpha = jnp.exp(m_i - m_new)
                l_new = alpha * l_i + jnp.sum(p, axis=-1, keepdims=True)
                acc = alpha * acc + jnp.dot(p.astype(BF16), vb[rows, lanes], preferred_element_type=F32)
                out.append((m_new, l_new, acc))
            return tuple(out)

        init = (jnp.full((t, 1), NEG_BIG, F32), jnp.zeros((t, 1), F32), jnp.zeros((t, hd), F32))
        final = lax.fori_loop(0, i + 1, step, (init,) * hp)
        for u in range(hp):
            m_i, l_i, acc = final[u]
            lanes = slice(u * hd, (u + 1) * hd)
            out = acc / l_i
            o_ref[:, lanes] = out
            mix_ref[:, lanes] = out.astype(BF16)
            lse_ref[:, lanes] = jnp.broadcast_to(m_i + jnp.log(l_i), (t, hd))

    return pl.pallas_call(
        body, name="attn_fwd", grid=(ng, s // t),
        in_specs=[pl.BlockSpec((t, w), lambda g, i: (i, g)),
                  pl.BlockSpec((s, w), lambda g, i: (0, ng + g)),
                  pl.BlockSpec((s, w), lambda g, i: (0, 2 * ng + g))] + [ANY] * len(after),
        out_specs=[pl.BlockSpec((None, t, w), lambda g, i: (0, i, g))] + [pl.BlockSpec((t, w), lambda g, i: (i, g))] * 2,
        out_shape=[jax.ShapeDtypeStruct((2, s, ATTN_WIDTH), BF16),
                   jax.ShapeDtypeStruct((s, ATTN_WIDTH), F32),
                   jax.ShapeDtypeStruct((s, ATTN_WIDTH), F32)],
        scratch_shapes=[pltpu.VMEM((s, w), BF16), pltpu.VMEM((s, w), BF16), pltpu.VMEM((hp, s // t, t, t), F32)],
        compiler_params=_params(("arbitrary", "arbitrary")),
    )(proj, proj, proj, *after)


def _attn_bwd(proj, attn_out, lse, dmixed, after=()):
    after = tuple(after)
    s = proj.shape[0]
    t = SEQ_TILE
    nt = s // t
    hd = ATTN_HEAD_DIM
    hp = ATTN_HEADS_PER_STEP
    ng = ATTN_HEADS // hp
    w = hp * hd
    scale = 1.0 / math.sqrt(hd)
    slopes = _alibi_slopes()

    def body(q_ref, k_ref, v_ref, o_ref, lse_ref, do_ref, *rest):
        dsec_ref, qb, kb, vb, dob, dsum, dq_acc, bias_tab = rest[len(after):]
        g = pl.program_id(0)
        qb[...] = q_ref[...].astype(BF16)
        kb[...] = k_ref[...].astype(BF16)
        vb[...] = v_ref[...].astype(BF16)
        dob[...] = do_ref[...].astype(BF16)
        for u in range(hp):
            lanes = slice(u * hd, (u + 1) * hd)
            _fill_attn_bias(bias_tab.at[u], _select_by_index(g * hp + u, slopes))
            rowsum = jnp.sum(do_ref[:, lanes] * o_ref[:, lanes], axis=-1, keepdims=True)
            dsum[:, lanes] = jnp.broadcast_to(rowsum, (s, hd))
        dq_acc[...] = jnp.zeros((s, w), F32)

        def over_keys(j, _):
            krows = pl.ds(pl.multiple_of(j * t, t), t)

            def over_queries(i, carry):
                qrows = pl.ds(pl.multiple_of(i * t, t), t)
                out = []
                for u in range(hp):
                    dk, dv = carry[u]
                    lanes = slice(u * hd, (u + 1) * hd)
                    qi, doi = qb[qrows, lanes], dob[qrows, lanes]
                    kj, vj = kb[krows, lanes], vb[krows, lanes]
                    lse_i = lse_ref[qrows, lanes][:, :1]
                    dsum_i = dsum[qrows, lanes][:, :1]
                    sc = lax.dot_general(qi, kj, _NT_DIMS, preferred_element_type=F32) * scale
                    p = jnp.exp(sc + bias_tab[u, i - j] - lse_i)
                    dp = lax.dot_general(doi, vj, _NT_DIMS, preferred_element_type=F32)
                    ds = (p * (dp - dsum_i)).astype(BF16)
                    dv = dv + lax.dot_general(p.astype(BF16), doi, _TN_DIMS, preferred_element_type=F32)
                    dk = dk + lax.dot_general(ds, qi, _TN_DIMS, preferred_element_type=F32)
                    dq_acc[qrows, lanes] += jnp.dot(ds, kj, preferred_element_type=F32)
                    out.append((dk, dv))
                return tuple(out)

            zero = jnp.zeros((t, hd), F32)
            final = lax.fori_loop(j, nt, over_queries, ((zero, zero),) * hp)
            for u in range(hp):
                lanes = slice(u * hd, (u + 1) * hd)
                dsec_ref[1, krows, lanes] = (final[u][0] * scale).astype(BF16)
                dsec_ref[2, krows, lanes] = final[u][1].astype(BF16)
            return 0

        lax.fori_loop(0, nt, over_keys, 0)
        dsec_ref[0] = (dq_acc[...] * scale).astype(BF16)

    def col(off):
        return pl.BlockSpec((s, w), lambda g: (0, off + g))

    return pl.pallas_call(
        body, name="attn_bwd", grid=(ng,),
        in_specs=[col(0), col(ng), col(2 * ng), col(0), col(0), col(0)] + [ANY] * len(after),
        out_specs=pl.BlockSpec((4, s, w), lambda g: (0, 0, g)),
        out_shape=jax.ShapeDtypeStruct((8, s, ATTN_WIDTH), BF16),
        scratch_shapes=[pltpu.VMEM((s, w), BF16)] * 4 + [pltpu.VMEM((s, w), F32)] * 2
        + [pltpu.VMEM((hp, nt, t, t), F32)],
        compiler_params=_params(("arbitrary",)),
    )(proj, proj, proj, attn_out, lse, dmixed, *after)


def _ret_log_gammas():
    return [math.log(1.0 - 2.0 ** (-5.0 - h)) for h in range(RET_HEADS)]


def _ret_decay(delta, log_gamma):
    dec = jnp.exp(delta.astype(F32) * log_gamma) * (1.0 / math.sqrt(RET_HEAD_DIM))
    return jnp.where(delta >= 0, dec, 0.0)


def _ret_fwd(proj, mixed, after=()):
    after = tuple(after)
    s = proj.shape[0]
    t = SEQ_TILE
    hd = RET_HEAD_DIM
    nh = RET_HEADS
    log_gammas = _ret_log_gammas()
    c0 = 3 * ATTN_WIDTH // hd

    def body(q_ref, k_ref, v_ref, g_ref, *rest):
        mix_ref, raw_ref, kb, vb, decay_tab = rest[1 + len(after):]
        h = pl.program_id(0)
        i = pl.program_id(1)

        @pl.when(i == 0)
        def _():
            kb[...] = k_ref[...].astype(BF16)
            vb[...] = v_ref[...].astype(BF16)
            _fill_ret_decay(decay_tab, _select_by_index(h, log_gammas))

        q = q_ref[...].astype(BF16)

        def step(j, acc):
            rows = pl.ds(pl.multiple_of(j * t, t), t)
            sc = lax.dot_general(q, kb[rows, :], _NT_DIMS, preferred_element_type=F32) * decay_tab[i - j]
            return acc + jnp.dot(sc.astype(BF16), vb[rows, :], preferred_element_type=F32)

        ret = lax.fori_loop(0, i + 1, step, jnp.zeros((t, hd), F32))
        raw_ref[...] = ret
        r = lax.rsqrt(jnp.mean(ret * ret, axis=-1, keepdims=True) + NORM_EPS)
        g = g_ref[...]
        mix_ref[...] = (g * _sigmoid(g) * (ret * r)).astype(BF16)

    return pl.pallas_call(
        body, name="ret_fwd", grid=(nh, s // t),
        in_specs=[pl.BlockSpec((t, hd), lambda h, i: (i, c0 + h)),
                  pl.BlockSpec((s, hd), lambda h, i: (0, c0 + nh + h)),
                  pl.BlockSpec((s, hd), lambda h, i: (0, c0 + 2 * nh + h)),
                  pl.BlockSpec((t, hd), lambda h, i: (i, c0 + 3 * nh + h))] + [ANY] * (1 + len(after)),
        out_specs=[pl.BlockSpec((None, t, hd), lambda h, i: (1, i, h)), pl.BlockSpec((t, hd), lambda h, i: (i, h))],
        out_shape=[jax.ShapeDtypeStruct(mixed.shape, BF16), jax.ShapeDtypeStruct((s, RET_WIDTH), F32)],
        input_output_aliases={4: 0},
        scratch_shapes=[pltpu.VMEM((s, hd), BF16), pltpu.VMEM((s, hd), BF16), pltpu.VMEM((s // t, t, t), F32)],
        compiler_params=_params(("arbitrary", "arbitrary")),
    )(proj, proj, proj, proj, mixed, *after)


def _ret_bwd(proj, ret_raw, dmixed, dsec, after=()):
    after = tuple(after)
    s = proj.shape[0]
    t = SEQ_TILE
    nt = s // t
    hd = RET_HEAD_DIM
    nh = RET_HEADS
    log_gammas = _ret_log_gammas()
    c0 = 3 * ATTN_WIDTH // hd
    mixed_blocks = ATTN_WIDTH // hd

    def body(q_ref, k_ref, v_ref, g_ref, raw_ref, dmix_ref, *rest):
        dsec_ref, qb, kb, vb, dretb, dq_acc, decay_tab = rest[1 + len(after):]
        h = pl.program_id(0)
        _fill_ret_decay(decay_tab, _select_by_index(h, log_gammas))
        qb[...] = q_ref[...].astype(BF16)
        kb[...] = k_ref[...].astype(BF16)
        vb[...] = v_ref[...].astype(BF16)
        ret = raw_ref[...]
        r = lax.rsqrt(jnp.mean(ret * ret, axis=-1, keepdims=True) + NORM_EPS)
        normed = ret * r
        g = g_ref[...]
        sg = _sigmoid(g)
        dout = dmix_ref[...]
        dsec_ref[3] = (dout * normed * sg * (1.0 + g * (1.0 - sg))).astype(BF16)
        dn = dout * g * sg
        dret = r * (dn - normed * jnp.mean(dn * normed, axis=-1, keepdims=True))
        dretb[...] = dret.astype(BF16)
        dq_acc[...] = jnp.zeros((s, hd), F32)

        def over_keys(j, _):
            krows = pl.ds(pl.multiple_of(j * t, t), t)
            kj = kb[krows, :]
            vj = vb[krows, :]

            def over_queries(i, carry):
                dk, dv = carry
                qrows = pl.ds(pl.multiple_of(i * t, t), t)
                qi = qb[qrows, :]
                doi = dretb[qrows, :]
                dec = decay_tab[i - j]
                a = (lax.dot_general(qi, kj, _NT_DIMS, preferred_element_type=F32) * dec).astype(BF16)
                da = (lax.dot_general(doi, vj, _NT_DIMS, preferred_element_type=F32) * dec).astype(BF16)
                dv = dv + lax.dot_general(a, doi, _TN_DIMS, preferred_element_type=F32)
                dk = dk + lax.dot_general(da, qi, _TN_DIMS, preferred_element_type=F32)
                dq_acc[qrows, :] += jnp.dot(da, kj, preferred_element_type=F32)
                return dk, dv

            zero = jnp.zeros((t, hd), F32)
            dk, dv = lax.fori_loop(j, nt, over_queries, (zero, zero))
            dsec_ref[1, krows, :] = dk.astype(BF16)
            dsec_ref[2, krows, :] = dv.astype(BF16)
            return 0

        lax.fori_loop(0, nt, over_keys, 0)
        dsec_ref[0] = dq_acc[...].astype(BF16)

    def col(off):
        return pl.BlockSpec((s, hd), lambda h: (0, off + h))

    return pl.pallas_call(
        body, name="ret_bwd", grid=(nh,),
        in_specs=[col(c0), col(c0 + nh), col(c0 + 2 * nh), col(c0 + 3 * nh), col(0), col(mixed_blocks)]
        + [ANY] * (1 + len(after)),
        out_specs=pl.BlockSpec((4, s, hd), lambda h: (1, 0, h)),
        out_shape=jax.ShapeDtypeStruct(dsec.shape, BF16),
        input_output_aliases={6: 0},
        scratch_shapes=[pltpu.VMEM((s, hd), BF16)] * 4 + [pltpu.VMEM((s, hd), F32)]
        + [pltpu.VMEM((nt, t, t), F32)],
        compiler_params=_params(("arbitrary",)),
    )(proj, proj, proj, proj, ret_raw, dmixed, dsec, *after)


_FLIPS = (2, 1, 3)


def _other_chips(x, y):
    return [(1 - x, y), (x, 1 - y), (1 - x, 1 - y)]


_HBM = pl.BlockSpec(memory_space=pltpu.HBM)
_SEM = pl.BlockSpec(memory_space=pltpu.SEMAPHORE)
_EFFECT = pltpu.SideEffectType.DATAFLOW_SIDE_EFFECTING


def _in_hbm(a):
    return pltpu.with_memory_space_constraint(a, pltpu.HBM)


def _weight_view(w, column_sharded):
    if column_sharded:
        return w.reshape(2, w.shape[0] // 2, w.shape[1])
    return w.reshape(N_CHIPS, 2, w.shape[0] // (2 * N_CHIPS), w.shape[1])


def _weight_unview(v):
    if v.ndim == 3:
        return v.reshape(2 * v.shape[1], v.shape[2])
    return v.reshape(N_CHIPS * 2 * v.shape[2], v.shape[3])


def _weight_region(buf, shard, half):
    if len(buf.shape) == 3:
        cols = buf.shape[2] // N_CHIPS
        return buf.at[half, :, pl.ds(shard * cols, cols)]
    return buf.at[shard, half]


def _remote(where, send_sem, recv_sem, to):
    return pltpu.make_async_remote_copy(src_ref=where, dst_ref=where, send_sem=send_sem, recv_sem=recv_sem,
                                        device_id=to, device_id_type=MESH)


def _for_my_shard(fn):
    x, y, _ = _place()
    for ss in range(N_CHIPS):
        pl.when(2 * x + y == ss)(functools.partial(fn, ss))


def _gather_start(views, name):
    n_w = len(views)

    def body(*refs):
        send_sems, recv_sems = refs[n_w:n_w + 2]
        bufs = refs[n_w + 2:]
        x, y, c = _place()

        def start(ss):
            for w in range(n_w):
                for j, chip in enumerate(_other_chips(x, y)):
                    _remote(_weight_region(bufs[w], ss, c), send_sems.at[3 * w + j], recv_sems.at[3 * w + j],
                            (*chip, c)).start()

        _for_my_shard(start)

    return pl.pallas_call(
        body, name=name,
        in_specs=[_HBM] * n_w, out_specs=[_SEM, _SEM] + [_HBM] * n_w,
        out_shape=[pltpu.SemaphoreType.DMA((3 * n_w,)), pltpu.SemaphoreType.DMA((3 * n_w,))]
        + [pltpu.HBM(v.shape, BF16) for v in views],
        input_output_aliases={w: 2 + w for w in range(n_w)},
        compiler_params=pltpu.CompilerParams(has_side_effects=_EFFECT),
    )(*[_in_hbm(v) for v in views])


def _gather_forward(views, which, send_sems, recv_sems, after, name):
    n_w = len(views)

    def body(*refs):
        send_in, recv_in = refs[n_w:n_w + 2]
        fwd_send, fwd_recv = refs[n_w + 3:n_w + 5]
        bufs = refs[n_w + 5:]
        x, y, c = _place()
        sibling = (x, y, 1 - c)

        def forward(ss):
            for i, w in enumerate(which):
                for j in range(3):
                    landed = _weight_region(bufs[i], ss ^ _FLIPS[j], c)
                    _remote(landed, send_in.at[3 * w + j], recv_in.at[3 * w + j], sibling).wait_recv()
                    _remote(landed, fwd_send.at[3 * i + j], fwd_recv.at[3 * i + j], sibling).start()

        _for_my_shard(forward)
        for i, w in enumerate(which):
            for j in range(3):
                _remote(_weight_region(bufs[i], 0, 0), send_in.at[3 * w + j], recv_in.at[3 * w + j],
                        sibling).wait_send()

    return pl.pallas_call(
        body, name=name,
        in_specs=[_HBM] * n_w + [_SEM, _SEM, ANY], out_specs=[_SEM, _SEM] + [_HBM] * n_w,
        out_shape=[pltpu.SemaphoreType.DMA((3 * n_w,)), pltpu.SemaphoreType.DMA((3 * n_w,))]
        + [pltpu.HBM(v.shape, BF16) for v in views],
        input_output_aliases={w: 2 + w for w in range(n_w)},
        compiler_params=pltpu.CompilerParams(has_side_effects=_EFFECT),
    )(*views, send_sems, recv_sems, after)


def _gather_end(views, fwd_send, fwd_recv, after, name):
    n_w = len(views)

    def body(*refs):
        fwd_send_ref, fwd_recv_ref = refs[n_w:n_w + 2]
        bufs = refs[n_w + 3:]
        x, y, c = _place()
        for i in range(n_w):
            for j in range(3):
                cp = _remote(_weight_region(bufs[i], 0, 0), fwd_send_ref.at[3 * i + j], fwd_recv_ref.at[3 * i + j],
                             (x, y, 1 - c))
                cp.wait_recv()
                cp.wait_send()

    outs = pl.pallas_call(
        body, name=name,
        in_specs=[_HBM] * n_w + [_SEM, _SEM, ANY], out_specs=[_HBM] * n_w,
        out_shape=[pltpu.HBM(v.shape, BF16) for v in views],
        input_output_aliases={w: w for w in range(n_w)},
        compiler_params=pltpu.CompilerParams(has_side_effects=_EFFECT),
    )(*views, fwd_send, fwd_recv, after)
    return [_weight_unview(o) for o in outs]


def _split_start(name, bufs, n_sems, copies):
    n = len(bufs)

    def body(*refs):
        send_sems, recv_sems = refs[n:n + 2]
        for cp in copies(refs[n + 2:], send_sems, recv_sems):
            cp.start()

    outs = pl.pallas_call(
        body, name=name,
        in_specs=[_HBM] * n, out_specs=[_SEM, _SEM] + [_HBM] * n,
        out_shape=[pltpu.SemaphoreType.DMA((n_sems,)), pltpu.SemaphoreType.DMA((n_sems,))]
        + [pltpu.HBM(b.shape, b.dtype) for b in bufs],
        input_output_aliases={i: 2 + i for i in range(n)},
        compiler_params=pltpu.CompilerParams(has_side_effects=_EFFECT),
    )(*[_in_hbm(b) for b in bufs])
    return outs[0], outs[1], list(outs[2:])


def _split_wait(name, bufs, send_sems, recv_sems, copies, after):
    n = len(bufs)

    def body(*refs):
        send_ref, recv_ref = refs[n:n + 2]
        for cp in copies(refs[n + 3:], send_ref, recv_ref):
            cp.wait()

    return list(pl.pallas_call(
        body, name=name,
        in_specs=[_HBM] * n + [_SEM, _SEM, ANY], out_specs=[_HBM] * n,
        out_shape=[pltpu.HBM(b.shape, b.dtype) for b in bufs],
        input_output_aliases={i: i for i in range(n)},
        compiler_params=pltpu.CompilerParams(has_side_effects=_EFFECT),
    )(*bufs, send_sems, recv_sems, after))


def _halves_copies(n_w):
    def copies(bufs, send_sems, recv_sems):
        x, y, c = _place()
        out = []
        for w in range(n_w):
            view, land = bufs[w], bufs[n_w + w]
            src = view.at[1 - c] if len(view.shape) == 3 else view.at[:, 1 - c]
            out.append(pltpu.make_async_remote_copy(
                src_ref=src, dst_ref=land, send_sem=send_sems.at[w], recv_sem=recv_sems.at[w],
                device_id=(x, y, 1 - c), device_id_type=MESH))
        return out
    return copies


def _pieces_copies(n_w):
    def copies(bufs, send_sems, recv_sems):
        x, y, c = _place()
        out = []
        for w in range(n_w):
            for j, (cx, cy) in enumerate(_other_chips(x, y)):
                out.append(pltpu.make_async_remote_copy(
                    src_ref=bufs[w].at[2 * cx + cy], dst_ref=bufs[n_w + w].at[j],
                    send_sem=send_sems.at[3 * w + j], recv_sem=recv_sems.at[3 * w + j],
                    device_id=(cx, cy, c), device_id_type=MESH))
        return out
    return copies


def _join_copies(n_w):
    def copies(bufs, send_sems, recv_sems):
        x, y, c = _place()
        return [pltpu.make_async_remote_copy(
            src_ref=bufs[w].at[c], dst_ref=bufs[w].at[c], send_sem=send_sems.at[w], recv_sem=recv_sems.at[w],
            device_id=(x, y, 1 - c), device_id_type=MESH) for w in range(n_w)]
    return copies


def _grad_view(g, column_sharded):
    return _weight_view(g, column_sharded)


def _halves_landing(view):
    shape = view.shape[1:] if view.ndim == 3 else (N_CHIPS,) + view.shape[2:]
    return lax.empty(shape, BF16)


def _halves_start(tag, grads, column_sharded):
    views = [_weight_view(g, cs) for g, cs in zip(grads, column_sharded)]
    n = len(views)
    return _split_start("halves_start_" + tag, views + [_halves_landing(v) for v in views], n, _halves_copies(n))


def _halves_wait(tag, state, after):
    send_sems, recv_sems, bufs = state
    n = len(bufs) // 2
    bufs = _split_wait("halves_wait_" + tag, bufs, send_sems, recv_sems, _halves_copies(n), after)
    return bufs[:n], bufs[n:]


def _pieces_start(tag, pieces):
    n = len(pieces)
    landing = [lax.empty((3,) + p.shape[1:], BF16) for p in pieces]
    return _split_start("pieces_start_" + tag, list(pieces) + landing, 3 * n, _pieces_copies(n))


def _pieces_wait(tag, state, after):
    send_sems, recv_sems, bufs = state
    n = len(bufs) // 2
    bufs = _split_wait("pieces_wait_" + tag, bufs, send_sems, recv_sems, _pieces_copies(n), after)
    return bufs[:n], bufs[n:]


def _join_start(tag, shards):
    n = len(shards)
    return _split_start("join_start_" + tag, list(shards), n, _join_copies(n))


def _join_wait(tag, state, after):
    send_sems, recv_sems, bufs = state
    bufs = _split_wait("join_wait_" + tag, bufs, send_sems, recv_sems, _join_copies(len(bufs)), after)
    return [b.reshape(2 * b.shape[1], b.shape[2]) for b in bufs]


def _chip_sum_col(g3, sib, c_arr, name):
    _, hk, n = g3.shape
    cols = n // N_CHIPS
    tr = _row_tile(hk, cols * 2, limit=4 * 1024 * 1024)

    def body(c_ref, g_ref, s_ref, o_ref):
        del c_ref
        o_ref[...] = (g_ref[...].astype(F32) + s_ref[...].astype(F32)).astype(BF16)

    grid_spec = pltpu.PrefetchScalarGridSpec(
        num_scalar_prefetch=1, grid=(N_CHIPS, hk // tr),
        in_specs=[pl.BlockSpec((None, tr, cols), lambda p, r, c_ref: (c_ref[0], r, p)),
                  pl.BlockSpec((tr, cols), lambda p, r, c_ref: (r, p))],
        out_specs=pl.BlockSpec((None, tr, cols), lambda p, r, c_ref: (p, r, 0)))
    return pl.pallas_call(
        body, name=name, grid_spec=grid_spec,
        out_shape=jax.ShapeDtypeStruct((N_CHIPS, hk, cols), BF16),
        compiler_params=_params(("parallel", "parallel")),
    )(c_arr, g3, sib)


def _chip_sum_row(g4, sib, c_arr, name):
    _, _, hr, n = g4.shape
    tr = _row_tile(hr, n * 2, limit=4 * 1024 * 1024)

    def body(c_ref, g_ref, s_ref, o_ref):
        del c_ref
        o_ref[...] = (g_ref[...].astype(F32) + s_ref[...].astype(F32)).astype(BF16)

    grid_spec = pltpu.PrefetchScalarGridSpec(
        num_scalar_prefetch=1, grid=(N_CHIPS, hr // tr),
        in_specs=[pl.BlockSpec((None, None, tr, n), lambda p, r, c_ref: (p, c_ref[0], r, 0)),
                  pl.BlockSpec((None, tr, n), lambda p, r, c_ref: (p, r, 0))],
        out_specs=pl.BlockSpec((None, tr, n), lambda p, r, c_ref: (p, r, 0)))
    return pl.pallas_call(
        body, name=name, grid_spec=grid_spec,
        out_shape=jax.ShapeDtypeStruct((N_CHIPS, hr, n), BF16),
        compiler_params=_params(("parallel", "parallel")),
    )(c_arr, g4, sib)


def _sum_pieces(pieces, received, place_arr, name):
    _, r, n = pieces.shape
    tr = _row_tile(r, n * 4, limit=4 * 1024 * 1024)

    def body(p_ref, own_ref, r0_ref, r1_ref, r2_ref, o_ref):
        del p_ref
        acc = own_ref[...].astype(F32) + r0_ref[...].astype(F32)
        acc = acc + r1_ref[...].astype(F32)
        o_ref[...] = acc + r2_ref[...].astype(F32)

    def recv_spec(j):
        return pl.BlockSpec((None, tr, n), lambda i, p_ref: (j, i, 0))

    grid_spec = pltpu.PrefetchScalarGridSpec(
        num_scalar_prefetch=1, grid=(r // tr,),
        in_specs=[pl.BlockSpec((None, tr, n), lambda i, p_ref: (p_ref[0], i, 0)),
                  recv_spec(0), recv_spec(1), recv_spec(2)],
        out_specs=pl.BlockSpec((None, tr, n), lambda i, p_ref: (p_ref[1], i, 0)))
    return pl.pallas_call(
        body, name=name, grid_spec=grid_spec,
        out_shape=jax.ShapeDtypeStruct((2, r, n), F32),
        compiler_params=_params(("parallel",)),
    )(place_arr, pieces, received, received, received)


def _norm_weights_step(parts, w, m, v, after=()):
    rows, d = parts.shape
    after = tuple(after)

    def body(p_ref, w_ref, m_ref, v_ref, *rest):
        g_ref, d_ref, mo_ref, vo_ref, gathered, send_sems, recv_sems = rest[len(after):]
        x, y, c = _place()
        me = 4 * x + 2 * y + c
        gathered[me] = p_ref[...]
        copies = []
        for k in range(1, N_DEV):
            peer = (x ^ ((k >> 2) & 1), y ^ ((k >> 1) & 1), c ^ (k & 1))
            copies.append(pltpu.make_async_remote_copy(
                src_ref=p_ref, dst_ref=gathered.at[me], send_sem=send_sems.at[k - 1],
                recv_sem=recv_sems.at[k - 1], device_id=peer, device_id_type=MESH))
        for cp in copies:
            cp.start()
        for cp in copies:
            cp.wait()
        g = gathered[0]
        for k in range(1, N_DEV):
            g = g + gathered[k]
        delta, m_new, v_new = _adamw_math(w_ref[...], g, m_ref[...], v_ref[...])
        g_ref[...] = g
        d_ref[...] = delta
        mo_ref[...] = m_new
        vo_ref[...] = v_new

    vmem = pl.BlockSpec(memory_space=pltpu.VMEM)
    shp = jax.ShapeDtypeStruct((rows, d), F32)
    return pl.pallas_call(
        body, name="norm_weights_step",
        in_specs=[vmem] * 4 + [ANY] * len(after), out_specs=[vmem] * 4, out_shape=[shp] * 4,
        scratch_shapes=[pltpu.VMEM((N_DEV, rows, d), F32), pltpu.SemaphoreType.DMA((N_DEV - 1,)),
                        pltpu.SemaphoreType.DMA((N_DEV - 1,))],
        compiler_params=pltpu.CompilerParams(has_side_effects=True),
    )(parts, w, m, v, *after)


def kernel(x, norm_mix_w, w_in, w_out, norm_ffn_w, w_gate, w_up, w_down, norm_final_w, loss_target, m_norm_mix_w, m_w_in, m_w_out, m_norm_ffn_w, m_w_gate, m_w_up, m_w_down, m_norm_final_w, v_norm_mix_w, v_w_in, v_w_out, v_norm_ffn_w, v_w_gate, v_w_up, v_w_down, v_norm_final_w):
    s, d = x.shape[1], x.shape[2]
    xs = x.reshape(s, d)
    target = loss_target.reshape(s, d)
    big = {"w_in": (w_in, m_w_in, v_w_in), "w_out": (w_out, m_w_out, v_w_out),
           "w_gate": (w_gate, m_w_gate, v_w_gate), "w_up": (w_up, m_w_up, v_w_up),
           "w_down": (w_down, m_w_down, v_w_down)}
    big = {k: tuple(a.reshape(a.shape[1:]) for a in t) for k, t in big.items()}
    col_names, row_names = ("w_in", "w_gate", "w_up"), ("w_out", "w_down")
    n_in = N_CHIPS * big["w_in"][0].shape[1]
    ffn = N_CHIPS * big["w_gate"][0].shape[1]
    mix = ATTN_WIDTH + RET_WIDTH
    c_arr = lax.axis_index("c").astype(I32).reshape(1)
    shard_arr = (2 * lax.axis_index("x") + lax.axis_index("y")).astype(I32).reshape(1)
    place_arr = jnp.concatenate([shard_arr, c_arr])

    def cast(k, after=()):
        return _weight_view(_cast_into_full(big[k][0], shard_arr, k in col_names, "cast_" + k, after), k in col_names)

    send_in, recv_in, v_in = _gather_start([cast("w_in")], "gather_start_in")
    rest = ("w_out", "w_gate", "w_up", "w_down")
    send_sems, recv_sems, v_out, v_gate, v_up, v_down = _gather_start(
        [cast(k, after=[v_in]) for k in rest], "gather_start_rest")

    sec = ATTN_WIDTH

    def section(p, rows):
        return pl.BlockSpec((None, rows, sec), lambda i, j, kk: (p, i, 0))

    h1 = _rms_fwd(xs, norm_mix_w, "rms_mix_fwd")
    fs, fr, v_in = _gather_forward([v_in], [0], send_in, recv_in, v_out, "gather_forward_in")
    wi, = _gather_end([v_in], fs, fr, h1, "gather_end_in")
    proj, = _matmul("in_proj", "nn", [h1], [wi], [0], s, n_in, d, s, 512, d, [], [F32], _epi_plain)
    fs_o, fr_o, v_out = _gather_forward([v_out], [0], send_sems, recv_sems, proj, "gather_forward_out")
    mixed, attn_o, lse = _attn_fwd(proj, after=[v_out])
    fs_g, fr_g, v_gate = _gather_forward([v_gate], [1], send_sems, recv_sems, attn_o, "gather_forward_gate")
    mixed, ret_raw = _ret_fwd(proj, mixed, after=[v_gate])
    wo, = _gather_end([v_out], fs_o, fr_o, ret_raw, "gather_end_out")
    x1, = _matmul("out_proj", "nn", [mixed, mixed], [wo, wo], [0, 0], s, d, sec, s, 512, sec, [xs], [F32],
                  _epi_residual, b_koff=[0, 1], a_specs=[section(0, s), section(1, s)])
    h2 = _rms_fwd(x1, norm_ffn_w, "rms_ffn_fwd")
    fs_u, fr_u, v_up = _gather_forward([v_up], [2], send_sems, recv_sems, h2, "gather_forward_up")
    wg, = _gather_end([v_gate], fs_g, fr_g, v_up, "gather_end_gate")
    wu, = _gather_end([v_up], fs_u, fr_u, wg, "gather_end_up")
    gate, up, act = _matmul("gate_up", "nn", [h2, h2], [wg, wu], [0, 1], s, ffn, d, s, 256, d, [],
                            [BF16, BF16, BF16], _epi_swiglu)
    fs, fr, v_down = _gather_forward([v_down], [3], send_sems, recv_sems, act, "gather_forward_down")
    wd, = _gather_end([v_down], fs, fr, act, "gather_end_down")
    x2, = _matmul("down_proj", "nn", [act], [wd], [0], s, d, ffn, s // 2, 256, ffn, [x1], [F32],
                  _epi_residual)
    loss_row, dx2, dx2b, dwf = _final_norm_loss(x2, norm_final_w.reshape(1, d), target, "final_norm_loss")

    names = col_names + row_names
    grads, new = {}, {}

    def chip_sums(tag_names, views, sibs):
        return [(_chip_sum_col if k in col_names else _chip_sum_row)(v, sb, c_arr, "chip_sum_" + k)
                for k, v, sb in zip(tag_names, views, sibs)]

    def piece_sums(tag_names, pieces, received):
        return [_sum_pieces(p, r, place_arr, "sum_pieces_" + k) for k, p, r in zip(tag_names, pieces, received)]

    def update(k):
        new[k] = _adamw(big[k][0], grads[k], big[k][1], big[k][2], "adamw_" + k)

    dgate, dup = _matmul("d_act", "nt", [dx2b], [wd], [0], s, ffn, d, s, 256, d, [gate, up],
                         [BF16, BF16], _epi_swiglu_bwd)
    g_wd, = _matmul("g_w_down", "tn", [act], [dx2b], [0], ffn, d, s, 512, d, s, [], [BF16], _epi_plain)
    halves_d = _halves_start("down", [g_wd], [False])
    dh2, = _matmul("d_h2", "nt", [dgate, dup], [wg, wu], [0, 0], s, d, ffn, s // 4, 256, ffn, [], [F32],
                   _epi_plain, after=halves_d[2][-1:])
    pieces_d = _pieces_start("down", chip_sums(["w_down"], *_halves_wait("down", halves_d, dh2)))
    g_wg, g_wu = _matmul("g_w_gate_up", "tn", [h2, h2], [dgate, dup], [0, 1], d, ffn, s, 1024, 512, s, [],
                         [BF16, BF16], _epi_two, after=pieces_d[2][-1:])
    halves_gu = _halves_start("gate_up", [g_wg, g_wu], [True, True])
    dx1, dx1b, dw_ffn = _rms_bwd(x1, norm_ffn_w, dh2, dx2, "rms_ffn_bwd", after=halves_gu[2][-1:])

    dmixed, = _matmul("d_mixed", "nt", [dx1b], [wo], [0], s, mix, d, s, 512, d, [], [F32], _epi_plain)
    pieces_gu = _pieces_start("gate_up", chip_sums(["w_gate", "w_up"], *_halves_wait("gate_up", halves_gu, dmixed)))
    per = sec // 512
    g_wo, = _matmul("g_w_out", "tn", [mixed], [dx1b], [0], mix, d, s, 512, d, s, [], [BF16], _epi_plain,
                    after=pieces_gu[2][-1:],
                    a_specs=[pl.BlockSpec((None, s, 512), lambda i, j, kk: (i // per, 0, i % per))])
    halves_o = _halves_start("out", [g_wo], [False])
    dsec = _attn_bwd(proj, attn_o, lse, dmixed, after=halves_o[2][-1:])
    pieces_o = _pieces_start("out", chip_sums(["w_out"], *_halves_wait("out", halves_o, dsec)))
    dsec = _ret_bwd(proj, ret_raw, dmixed, dsec, after=pieces_o[2][-1:])
    where = [0, 1, 2, 4, 5, 6, 7]
    n_sec = len(where)
    g_wi, = _matmul("g_w_in", "tn", [h1], [dsec], [0], d, n_in, s, 1024, sec, s, [], [BF16], _epi_plain,
                    b_specs=[pl.BlockSpec((None, s, sec), lambda i, j, kk: (j + (j >= 3).astype(I32), 0, 0))])
    halves_i = _halves_start("in", [g_wi], [True])
    dh1, = _matmul("d_h1", "nt", [dsec] * n_sec, [wi] * n_sec, [0] * n_sec, s, d, sec, s // 2, 256, sec, [], [F32],
                   _epi_plain, b_koff=list(range(n_sec)), after=halves_i[2][-1:],
                   a_specs=[section(p, s // 2) for p in where])
    pieces_i = _pieces_start("in", chip_sums(["w_in"], *_halves_wait("in", halves_i, dh1)))
    grad_x, _, dw_mix = _rms_bwd(xs, norm_mix_w, dh1, dx1, "rms_mix_bwd", after=pieces_i[2][-1:])

    def rows8(*vs):
        return jnp.concatenate([v.reshape(1, d) for v in vs] + [jnp.zeros((8 - len(vs), d), F32)], axis=0)

    join_d = _join_start("down", piece_sums(["w_down"], *_pieces_wait("down", pieces_d, grad_x)))
    join_gu = _join_start("gate_up", piece_sums(["w_gate", "w_up"], *_pieces_wait("gate_up", pieces_gu, join_d[2][0])))
    join_o = _join_start("out", piece_sums(["w_out"], *_pieces_wait("out", pieces_o, join_gu[2][0])))
    grads["w_down"], = _join_wait("down", join_d, join_o[2][0])
    update("w_down")
    grads["w_gate"], grads["w_up"] = _join_wait("gate_up", join_gu, new["w_down"][0])
    update("w_gate")
    update("w_up")
    grads["w_out"], = _join_wait("out", join_o, new["w_up"][0])
    update("w_out")
    join_i = _join_start("in", piece_sums(["w_in"], *_pieces_wait("in", pieces_i, new["w_out"][0])))
    ng, nd, nm, nv = _norm_weights_step(
        rows8(dw_mix, dw_ffn, dwf), rows8(norm_mix_w, norm_ffn_w, norm_final_w),
        rows8(m_norm_mix_w, m_norm_ffn_w, m_norm_final_w), rows8(v_norm_mix_w, v_norm_ffn_w, v_norm_final_w),
        after=join_i[2][:1])
    grads["w_in"], = _join_wait("in", join_i, ng)
    update("w_in")

    loss = lax.psum(loss_row[0, 0], ("x", "y", "c"))

    def pack(small, per_weight):
        lead = lambda a: a.reshape((1,) + a.shape)
        return (small[0:1], lead(per_weight["w_in"]), lead(per_weight["w_out"]), small[1:2],
                lead(per_weight["w_gate"]), lead(per_weight["w_up"]), lead(per_weight["w_down"]), small[2])

    return (loss, grad_x.reshape(1, s, d),
            *pack(ng, {k: new[k][3] for k in names}),
            *pack(nd, {k: new[k][0] for k in names}),
            *pack(nm, {k: new[k][1] for k in names}),
            *pack(nv, {k: new[k][2] for k in names}))
```

```python
import functools
import math

import jax
import jax.numpy as jnp
from jax import lax
from jax.experimental import pallas as pl
from jax.experimental.pallas import tpu as pltpu

F32 = jnp.float32
BF16 = jnp.bfloat16
I32 = jnp.int32
MESH = pl.DeviceIdType.MESH
ANY = pl.BlockSpec(memory_space=pl.ANY)

ATTN_HEADS = 8
ATTN_HEAD_DIM = 128
RET_HEADS = 4
RET_HEAD_DIM = 256
ATTN_WIDTH = ATTN_HEADS * ATTN_HEAD_DIM
RET_WIDTH = RET_HEADS * RET_HEAD_DIM
DILATED_PATTERNS = ((128, 1), (512, 4), (2048, 16))
NORM_EPS = 1e-6
ADAM_LR = 0.001
ADAM_B1 = 0.9
ADAM_B2 = 0.999
ADAM_EPS = 1e-08
ADAM_WD = 0.01
ADAM_STEP = 10

N_CHIPS = 4
N_DEV = 8
NEG_BIG = -1e30
SEQ_TILE = 512
ATTN_FWD_HEADS_PER_STEP = 2
ATTN_HEADS_PER_STEP = 1
VMEM_LIMIT_BYTES = 56 * 1024 * 1024


def _params(semantics=None, vmem=VMEM_LIMIT_BYTES):
    return pltpu.CompilerParams(dimension_semantics=semantics, vmem_limit_bytes=vmem)


def _row_tile(rows, row_bytes, limit=2 * 1024 * 1024, mult=16):
    best = None
    for t in range(mult, rows + 1, mult):
        if rows % t == 0 and t * row_bytes <= limit:
            best = t
    assert best is not None, (rows, row_bytes)
    return best


def _sigmoid(x):
    return 1.0 / (1.0 + jnp.exp(-x))


def _select_by_index(idx, values):
    out = jnp.float32(values[-1])
    for i in range(len(values) - 2, -1, -1):
        out = jnp.where(idx == i, jnp.float32(values[i]), out)
    return out


def _place():
    x, y, c = lax.axis_index("x"), lax.axis_index("y"), lax.axis_index("c")
    return x, y, c


def _cast_into_full(w, shard_arr, column_sharded, name, after=()):
    after = tuple(after)
    rows, cols = w.shape
    tr = _row_tile(rows, cols * 4)
    steps = rows // tr
    if column_sharded:
        out_shape, out_map = (rows, N_CHIPS * cols), (lambda i, s_ref: (i, s_ref[0]))
    else:
        out_shape, out_map = (N_CHIPS * rows, cols), (lambda i, s_ref: (s_ref[0] * steps + i, 0))

    def body(s_ref, w_ref, *rest):
        del s_ref
        rest[-1][...] = w_ref[...].astype(BF16)

    grid_spec = pltpu.PrefetchScalarGridSpec(
        num_scalar_prefetch=1, grid=(steps,),
        in_specs=[pl.BlockSpec((tr, cols), lambda i, s_ref: (i, 0))] + [ANY] * len(after),
        out_specs=pl.BlockSpec((tr, cols), out_map))
    return pl.pallas_call(
        body, name=name, grid_spec=grid_spec,
        out_shape=jax.ShapeDtypeStruct(out_shape, BF16),
        compiler_params=_params(("parallel",)),
    )(shard_arr, w, *after)


def _rms_fwd(x, w, name):
    rows, d = x.shape
    tr = 256

    def body(x_ref, w_ref, h_ref):
        xv = x_ref[...]
        r = lax.rsqrt(jnp.mean(xv * xv, axis=-1, keepdims=True) + NORM_EPS)
        h_ref[...] = (xv * r * w_ref[...]).astype(BF16)

    return pl.pallas_call(
        body, name=name, grid=(rows // tr,),
        in_specs=[pl.BlockSpec((tr, d), lambda i: (i, 0)), pl.BlockSpec((1, d), lambda i: (0, 0))],
        out_specs=pl.BlockSpec((tr, d), lambda i: (i, 0)),
        out_shape=jax.ShapeDtypeStruct((rows, d), BF16),
        compiler_params=_params(("parallel",)),
    )(x, w)


def _rms_bwd(x, w, dh, dres, name, after=()):
    rows, d = x.shape
    tr = 256
    after = tuple(after)

    def body(x_ref, w_ref, dh_ref, dres_ref, *rest):
        dx_ref, dxb_ref, dw_ref = rest[len(after):]
        xv = x_ref[...]
        r = lax.rsqrt(jnp.mean(xv * xv, axis=-1, keepdims=True) + NORM_EPS)
        xhat = xv * r
        dy = dh_ref[...]
        dxhat = dy * w_ref[...]
        dx = dres_ref[...] + r * (dxhat - xhat * jnp.mean(dxhat * xhat, axis=-1, keepdims=True))
        dx_ref[...] = dx
        dxb_ref[...] = dx.astype(BF16)
        part = jnp.sum(dy * xhat, axis=0, keepdims=True)

        @pl.when(pl.program_id(0) == 0)
        def _():
            dw_ref[...] = part

        @pl.when(pl.program_id(0) != 0)
        def _():
            dw_ref[...] += part

    row = pl.BlockSpec((tr, d), lambda i: (i, 0))
    vec = pl.BlockSpec((1, d), lambda i: (0, 0))
    return pl.pallas_call(
        body, name=name, grid=(rows // tr,),
        in_specs=[row, vec, row, row] + [ANY] * len(after),
        out_specs=[row, row, vec],
        out_shape=[jax.ShapeDtypeStruct((rows, d), F32), jax.ShapeDtypeStruct((rows, d), BF16),
                   jax.ShapeDtypeStruct((1, d), F32)],
        compiler_params=_params(("arbitrary",)),
    )(x, w, dh, dres, *after)


def _final_norm_loss(x2, w, target, name):
    rows, d = x2.shape
    tr = 256

    def body(x_ref, w_ref, t_ref, loss_ref, dx_ref, dxb_ref, dw_ref):
        xv = x_ref[...]
        wv = w_ref[...]
        r = lax.rsqrt(jnp.mean(xv * xv, axis=-1, keepdims=True) + NORM_EPS)
        xhat = xv * r
        err = xhat * wv - t_ref[...]
        part_loss = 0.5 * jnp.sum(jnp.mean(err * err, axis=-1, keepdims=True), axis=0, keepdims=True)
        dy = err * (1.0 / d)
        dxhat = dy * wv
        dx = r * (dxhat - xhat * jnp.mean(dxhat * xhat, axis=-1, keepdims=True))
        dx_ref[...] = dx
        dxb_ref[...] = dx.astype(BF16)
        part_dw = jnp.sum(dy * xhat, axis=0, keepdims=True)
        part_loss = jnp.broadcast_to(part_loss, (1, 128))

        @pl.when(pl.program_id(0) == 0)
        def _():
            dw_ref[...] = part_dw
            loss_ref[...] = part_loss

        @pl.when(pl.program_id(0) != 0)
        def _():
            dw_ref[...] += part_dw
            loss_ref[...] += part_loss

    row = pl.BlockSpec((tr, d), lambda i: (i, 0))
    vec = pl.BlockSpec((1, d), lambda i: (0, 0))
    return pl.pallas_call(
        body, name=name, grid=(rows // tr,),
        in_specs=[row, vec, row],
        out_specs=[pl.BlockSpec((1, 128), lambda i: (0, 0)), row, row, vec],
        out_shape=[jax.ShapeDtypeStruct((1, 128), F32), jax.ShapeDtypeStruct((rows, d), F32),
                   jax.ShapeDtypeStruct((rows, d), BF16), jax.ShapeDtypeStruct((1, d), F32)],
        compiler_params=_params(("arbitrary",)),
    )(x2, w, target)


def _adamw_math(w, g, m, v):
    m = ADAM_B1 * m + (1.0 - ADAM_B1) * g
    v = ADAM_B2 * v + (1.0 - ADAM_B2) * (g * g)
    m_hat = m / (1.0 - ADAM_B1 ** ADAM_STEP)
    v_hat = v / (1.0 - ADAM_B2 ** ADAM_STEP)
    delta = -ADAM_LR * (m_hat / (jnp.sqrt(v_hat) + ADAM_EPS) + ADAM_WD * w)
    return delta, m, v


def _adamw(w, g, m, v, name):
    rows, cols = w.shape
    tr = _row_tile(rows, cols * 4)

    def body(w_ref, g_ref, m_ref, v_ref, d_ref, mo_ref, vo_ref, go_ref):
        g = g_ref[...]
        delta, m_new, v_new = _adamw_math(w_ref[...], g, m_ref[...], v_ref[...])
        d_ref[...] = delta
        mo_ref[...] = m_new
        vo_ref[...] = v_new
        go_ref[...] = g

    blk = pl.BlockSpec((tr, cols), lambda i: (i, 0))
    shp = jax.ShapeDtypeStruct((rows, cols), F32)
    return pl.pallas_call(
        body, name=name, grid=(rows // tr,),
        in_specs=[blk] * 4, out_specs=[blk] * 4, out_shape=[shp] * 4,
        compiler_params=_params(("parallel",)),
    )(w, g, m, v)


_DOT_DIMS = {"nn": ((1,), (0,)), "nt": ((1,), (1,)), "tn": ((0,), (0,))}


def _matmul(name, mode, a_list, b_list, acc_of, m, n, k, tm, tn, tk, extras, out_dtypes, epilogue,
            a_koff=None, b_koff=None, after=(), a_specs=None, b_specs=None):
    after = tuple(after)
    assert m % tm == 0 and n % tn == 0 and k % tk == 0, (name, m, n, k, tm, tn, tk)
    nk = k // tk
    n_acc = max(acc_of) + 1
    n_pairs = len(a_list)
    a_koff = a_koff or [0] * n_pairs
    b_koff = b_koff or [0] * n_pairs
    dims = (_DOT_DIMS[mode], ((), ()))
    n_ext, n_out = len(extras), len(out_dtypes)

    def body(*refs):
        a_refs = refs[:n_pairs]
        b_refs = refs[n_pairs:2 * n_pairs]
        e_refs = refs[2 * n_pairs:2 * n_pairs + n_ext]
        first_out = 2 * n_pairs + n_ext + len(after)
        o_refs = refs[first_out:first_out + n_out]
        acc_refs = refs[first_out + n_out:]
        parts = [None] * n_acc
        for p in range(n_pairs):
            d = lax.dot_general(a_refs[p][...], b_refs[p][...], dims, preferred_element_type=F32)
            parts[acc_of[p]] = d if parts[acc_of[p]] is None else parts[acc_of[p]] + d

        def finish(accs):
            outs = epilogue(accs, [e[...] for e in e_refs])
            for o_ref, o in zip(o_refs, outs):
                o_ref[...] = o.astype(o_ref.dtype)

        if nk == 1:
            finish(parts)
        else:
            kk = pl.program_id(2)

            @pl.when(kk == 0)
            def _():
                for acc_ref, part in zip(acc_refs, parts):
                    acc_ref[...] = part

            @pl.when(kk != 0)
            def _():
                for acc_ref, part in zip(acc_refs, parts):
                    acc_ref[...] += part

            @pl.when(kk == nk - 1)
            def _():
                finish([acc_ref[...] for acc_ref in acc_refs])

    def a_spec(off):
        if mode == "tn":
            return pl.BlockSpec((tk, tm), lambda i, j, kk: (kk + off, i))
        return pl.BlockSpec((tm, tk), lambda i, j, kk: (i, kk + off))

    def b_spec(off):
        if mode == "nt":
            return pl.BlockSpec((tn, tk), lambda i, j, kk: (j, kk + off))
        return pl.BlockSpec((tk, tn), lambda i, j, kk: (kk + off, j))

    tile = pl.BlockSpec((tm, tn), lambda i, j, kk: (i, j))
    scratch = [pltpu.VMEM((tm, tn), F32) for _ in range(n_acc)] if nk > 1 else []
    return pl.pallas_call(
        body, name=name, grid=(m // tm, n // tn, nk),
        in_specs=(a_specs or [a_spec(o) for o in a_koff]) + (b_specs or [b_spec(o) for o in b_koff])
        + [tile] * n_ext + [ANY] * len(after),
        out_specs=[tile] * n_out,
        out_shape=[jax.ShapeDtypeStruct((m, n), dt) for dt in out_dtypes],
        scratch_shapes=scratch,
        compiler_params=_params(("parallel", "parallel", "arbitrary")),
    )(*a_list, *b_list, *extras, *after)


def _epi_plain(accs, extras):
    return (accs[0],)


def _epi_residual(accs, extras):
    return (accs[0] + extras[0],)


def _epi_two(accs, extras):
    return accs[0], accs[1]


def _epi_swiglu(accs, extras):
    g, u = accs
    return g, u, g * _sigmoid(g) * u


def _epi_swiglu_bwd(accs, extras):
    da = accs[0]
    g, u = (e.astype(F32) for e in extras)
    sg = _sigmoid(g)
    dg = da * u * sg * (1.0 + g * (1.0 - sg))
    du = da * g * sg
    return dg, du


_NT_DIMS = (((1,), (1,)), ((), ()))
_TN_DIMS = (((0,), (0,)), ((), ()))


def _tile_delta(tq, tk):
    return lax.broadcasted_iota(I32, (tq, tk), 0) - lax.broadcasted_iota(I32, (tq, tk), 1)


def _attn_log_count(delta):
    count = jnp.zeros(delta.shape, I32)
    for window, dilation in DILATED_PATTERNS:
        hit = ((delta & (dilation - 1)) == 0) & (delta <= window)
        count = count + jnp.where(hit, 1, 0)
    valid = (delta >= 0) & (count > 0)
    logm = jnp.where(count == 3, math.log(3.0), jnp.where(count == 2, math.log(2.0), 0.0))
    return jnp.where(valid, logm, NEG_BIG)


def _fill_attn_log_count(tab_ref):
    nb, t, _ = tab_ref.shape
    base = _tile_delta(t, t)
    for b in range(nb):
        tab_ref[b] = _attn_log_count(base + b * t)


def _fill_attn_bias(tab_ref, log_count_ref, slope):
    nb, t, _ = tab_ref.shape
    dist = _tile_delta(t, t).astype(F32)
    for b in range(nb):
        tab_ref[b] = log_count_ref[b] - slope * (dist + float(b * t))


def _fill_ret_decay(tab_ref, log_gamma):
    nb, t, _ = tab_ref.shape
    base = _tile_delta(t, t)
    for b in range(nb):
        tab_ref[b] = _ret_decay(base + b * t, log_gamma)


def _alibi_slopes():
    return [2.0 ** (-8.0 * (h + 1) / ATTN_HEADS) for h in range(ATTN_HEADS)]


def _attn_fwd(proj, after=()):
    s = proj.shape[0]
    t = SEQ_TILE
    hd = ATTN_HEAD_DIM
    hp = ATTN_FWD_HEADS_PER_STEP
    ng = ATTN_HEADS // hp
    w = hp * hd
    scale = 1.0 / math.sqrt(hd)
    slopes = _alibi_slopes()

    def body(q_ref, k_ref, v_ref, *rest):
        mix_ref, o_ref, lse_ref, kb, vb, bias_tab, log_count_tab = rest[len(after):]
        g = pl.program_id(0)
        i = pl.program_id(1)

        @pl.when((g == 0) & (i == 0))
        def _():
            _fill_attn_log_count(log_count_tab)

        @pl.when(i == 0)
        def _():
            kb[...] = k_ref[...].astype(BF16)
            vb[...] = v_ref[...].astype(BF16)
            for u in range(hp):
                _fill_attn_bias(bias_tab.at[u], log_count_tab, _select_by_index(g * hp + u, slopes))

        qs = [q_ref[:, u * hd:(u + 1) * hd].astype(BF16) for u in range(hp)]

        def step(j, carry):
            rows = pl.ds(pl.multiple_of(j * t, t), t)
            out = []
            for u in range(hp):
                m_i, l_i, acc = carry[u]
                lanes = slice(u * hd, (u + 1) * hd)
                sc = lax.dot_general(qs[u], kb[rows, lanes], _NT_DIMS, preferred_element_type=F32) * scale
                sc = sc + bias_tab[u, i - j]
                m_new = jnp.maximum(m_i, jnp.max(sc, axis=-1, keepdims=True))
                p = jnp.exp(sc - m_new)
                alpha = jnp.exp(m_i - m_new)
                l_new = alpha * l_i + jnp.sum(p, axis=-1, keepdims=True)
                acc = alpha * acc + jnp.dot(p.astype(BF16), vb[rows, lanes], preferred_element_type=F32)
                out.append((m_new, l_new, acc))
            return tuple(out)

        init = (jnp.full((t, 1), NEG_BIG, F32), jnp.zeros((t, 1), F32), jnp.zeros((t, hd), F32))
        final = lax.fori_loop(0, i + 1, step, (init,) * hp)
        for u in range(hp):
            m_i, l_i, acc = final[u]
            lanes = slice(u * hd, (u + 1) * hd)
            out = acc / l_i
            o_ref[:, lanes] = out
            mix_ref[:, lanes] = out.astype(BF16)
            lse_ref[:, lanes] = jnp.broadcast_to(m_i + jnp.log(l_i), (t, hd))

    return pl.pallas_call(
        body, name="attn_fwd", grid=(ng, s // t),
        in_specs=[pl.BlockSpec((t, w), lambda g, i: (i, g)),
                  pl.BlockSpec((s, w), lambda g, i: (0, ng + g)),
                  pl.BlockSpec((s, w), lambda g, i: (0, 2 * ng + g))] + [ANY] * len(after),
        out_specs=[pl.BlockSpec((None, t, w), lambda g, i: (0, i, g))] + [pl.BlockSpec((t, w), lambda g, i: (i, g))] * 2,
        out_shape=[jax.ShapeDtypeStruct((2, s, ATTN_WIDTH), BF16),
                   jax.ShapeDtypeStruct((s, ATTN_WIDTH), F32),
                   jax.ShapeDtypeStruct((s, ATTN_WIDTH), F32)],
        scratch_shapes=[pltpu.VMEM((s, w), BF16), pltpu.VMEM((s, w), BF16), pltpu.VMEM((hp, s // t, t, t), F32),
                        pltpu.VMEM((s // t, t, t), F32)],
        compiler_params=_params(("arbitrary", "arbitrary")),
    )(proj, proj, proj, *after)


def _attn_bwd(proj, attn_out, lse, dmixed, after=()):
    after = tuple(after)
    s = proj.shape[0]
    t = SEQ_TILE
    nt = s // t
    hd = ATTN_HEAD_DIM
    hp = ATTN_HEADS_PER_STEP
    ng = ATTN_HEADS // hp
    w = hp * hd
    scale = 1.0 / math.sqrt(hd)
    slopes = _alibi_slopes()

    def body(q_ref, k_ref, v_ref, o_ref, lse_ref, do_ref, *rest):
        dsec_ref, qb, kb, vb, dob, dsum, dq_acc, bias_tab, log_count_tab = rest[len(after):]
        g = pl.program_id(0)

        @pl.when(g == 0)
        def _():
            _fill_attn_log_count(log_count_tab)

        qb[...] = q_ref[...].astype(BF16)
        kb[...] = k_ref[...].astype(BF16)
        vb[...] = v_ref[...].astype(BF16)
        dob[...] = do_ref[...].astype(BF16)
        for u in range(hp):
            lanes = slice(u * hd, (u + 1) * hd)
            _fill_attn_bias(bias_tab.at[u], log_count_tab, _select_by_index(g * hp + u, slopes))
            rowsum = jnp.sum(do_ref[:, lanes] * o_ref[:, lanes], axis=-1, keepdims=True)
            dsum[:, lanes] = jnp.broadcast_to(rowsum, (s, hd))
        dq_acc[...] = jnp.zeros((s, w), F32)

        def over_keys(j, _):
            krows = pl.ds(pl.multiple_of(j * t, t), t)

            def over_queries(i, carry):
                qrows = pl.ds(pl.multiple_of(i * t, t), t)
                out = []
                for u in range(hp):
                    dk, dv = carry[u]
                    lanes = slice(u * hd, (u + 1) * hd)
                    qi, doi = qb[qrows, lanes], dob[qrows, lanes]
                    kj, vj = kb[krows, lanes], vb[krows, lanes]
                    lse_i = lse_ref[qrows, lanes][:, :1]
                    dsum_i = dsum[qrows, lanes][:, :1]
                    sc = lax.dot_general(qi, kj, _NT_DIMS, preferred_element_type=F32) * scale
                    p = jnp.exp(sc + bias_tab[u, i - j] - lse_i)
                    dp = lax.dot_general(doi, vj, _NT_DIMS, preferred_element_type=F32)
                    ds = (p * (dp - dsum_i)).astype(BF16)
                    dv = dv + lax.dot_general(p.astype(BF16), doi, _TN_DIMS, preferred_element_type=F32)
                    dk = dk + lax.dot_general(ds, qi, _TN_DIMS, preferred_element_type=F32)
                    dq_acc[qrows, lanes] += jnp.dot(ds, kj, preferred_element_type=F32)
                    out.append((dk, dv))
                return tuple(out)

            zero = jnp.zeros((t, hd), F32)
            final = lax.fori_loop(j, nt, over_queries, ((zero, zero),) * hp)
            for u in range(hp):
                lanes = slice(u * hd, (u + 1) * hd)
                dsec_ref[1, krows, lanes] = (final[u][0] * scale).astype(BF16)
                dsec_ref[2, krows, lanes] = final[u][1].astype(BF16)
            return 0

        lax.fori_loop(0, nt, over_keys, 0)
        dsec_ref[0] = (dq_acc[...] * scale).astype(BF16)

    def col(off):
        return pl.BlockSpec((s, w), lambda g: (0, off + g))

    return pl.pallas_call(
        body, name="attn_bwd", grid=(ng,),
        in_specs=[col(0), col(ng), col(2 * ng), col(0), col(0), col(0)] + [ANY] * len(after),
        out_specs=pl.BlockSpec((4, s, w), lambda g: (0, 0, g)),
        out_shape=jax.ShapeDtypeStruct((8, s, ATTN_WIDTH), BF16),
        scratch_shapes=[pltpu.VMEM((s, w), BF16)] * 4 + [pltpu.VMEM((s, w), F32)] * 2
        + [pltpu.VMEM((hp, nt, t, t), F32), pltpu.VMEM((nt, t, t), F32)],
        compiler_params=_params(("arbitrary",)),
    )(proj, proj, proj, attn_out, lse, dmixed, *after)


def _ret_log_gammas():
    return [math.log(1.0 - 2.0 ** (-5.0 - h)) for h in range(RET_HEADS)]


def _ret_decay(delta, log_gamma):
    dec = jnp.exp(delta.astype(F32) * log_gamma) * (1.0 / math.sqrt(RET_HEAD_DIM))
    return jnp.where(delta >= 0, dec, 0.0)


def _ret_fwd(proj, mixed, after=()):
    after = tuple(after)
    s = proj.shape[0]
    t = SEQ_TILE
    hd = RET_HEAD_DIM
    nh = RET_HEADS
    log_gammas = _ret_log_gammas()
    c0 = 3 * ATTN_WIDTH // hd

    def body(q_ref, k_ref, v_ref, g_ref, *rest):
        mix_ref, raw_ref, kb, vb, decay_tab = rest[1 + len(after):]
        h = pl.program_id(0)
        i = pl.program_id(1)

        @pl.when(i == 0)
        def _():
            kb[...] = k_ref[...].astype(BF16)
            vb[...] = v_ref[...].astype(BF16)
            _fill_ret_decay(decay_tab, _select_by_index(h, log_gammas))

        q = q_ref[...].astype(BF16)

        def step(j, acc):
            rows = pl.ds(pl.multiple_of(j * t, t), t)
            sc = lax.dot_general(q, kb[rows, :], _NT_DIMS, preferred_element_type=F32) * decay_tab[i - j]
            return acc + jnp.dot(sc.astype(BF16), vb[rows, :], preferred_element_type=F32)

        ret = lax.fori_loop(0, i + 1, step, jnp.zeros((t, hd), F32))
        raw_ref[...] = ret
        r = lax.rsqrt(jnp.mean(ret * ret, axis=-1, keepdims=True) + NORM_EPS)
        g = g_ref[...]
        mix_ref[...] = (g * _sigmoid(g) * (ret * r)).astype(BF16)

    return pl.pallas_call(
        body, name="ret_fwd", grid=(nh, s // t),
        in_specs=[pl.BlockSpec((t, hd), lambda h, i: (i, c0 + h)),
                  pl.BlockSpec((s, hd), lambda h, i: (0, c0 + nh + h)),
                  pl.BlockSpec((s, hd), lambda h, i: (0, c0 + 2 * nh + h)),
                  pl.BlockSpec((t, hd), lambda h, i: (i, c0 + 3 * nh + h))] + [ANY] * (1 + len(after)),
        out_specs=[pl.BlockSpec((None, t, hd), lambda h, i: (1, i, h)), pl.BlockSpec((t, hd), lambda h, i: (i, h))],
        out_shape=[jax.ShapeDtypeStruct(mixed.shape, BF16), jax.ShapeDtypeStruct((s, RET_WIDTH), F32)],
        input_output_aliases={4: 0},
        scratch_shapes=[pltpu.VMEM((s, hd), BF16), pltpu.VMEM((s, hd), BF16), pltpu.VMEM((s // t, t, t), F32)],
        compiler_params=_params(("arbitrary", "arbitrary")),
    )(proj, proj, proj, proj, mixed, *after)


def _ret_bwd(proj, ret_raw, dmixed, dsec, after=()):
    after = tuple(after)
    s = proj.shape[0]
    t = SEQ_TILE
    nt = s // t
    hd = RET_HEAD_DIM
    nh = RET_HEADS
    log_gammas = _ret_log_gammas()
    c0 = 3 * ATTN_WIDTH // hd
    mixed_blocks = ATTN_WIDTH // hd

    def body(q_ref, k_ref, v_ref, g_ref, raw_ref, dmix_ref, *rest):
        dsec_ref, qb, kb, vb, dretb, dq_acc, decay_tab = rest[1 + len(after):]
        h = pl.program_id(0)
        _fill_ret_decay(decay_tab, _select_by_index(h, log_gammas))
        qb[...] = q_ref[...].astype(BF16)
        kb[...] = k_ref[...].astype(BF16)
        vb[...] = v_ref[...].astype(BF16)
        ret = raw_ref[...]
        r = lax.rsqrt(jnp.mean(ret * ret, axis=-1, keepdims=True) + NORM_EPS)
        normed = ret * r
        g = g_ref[...]
        sg = _sigmoid(g)
        dout = dmix_ref[...]
        dsec_ref[3] = (dout * normed * sg * (1.0 + g * (1.0 - sg))).astype(BF16)
        dn = dout * g * sg
        dret = r * (dn - normed * jnp.mean(dn * normed, axis=-1, keepdims=True))
        dretb[...] = dret.astype(BF16)
        dq_acc[...] = jnp.zeros((s, hd), F32)

        def over_keys(j, _):
            krows = pl.ds(pl.multiple_of(j * t, t), t)
            kj = kb[krows, :]
            vj = vb[krows, :]

            def over_queries(i, carry):
                dk, dv = carry
                qrows = pl.ds(pl.multiple_of(i * t, t), t)
                qi = qb[qrows, :]
                doi = dretb[qrows, :]
                dec = decay_tab[i - j]
                a = (lax.dot_general(qi, kj, _NT_DIMS, preferred_element_type=F32) * dec).astype(BF16)
                da = (lax.dot_general(doi, vj, _NT_DIMS, preferred_element_type=F32) * dec).astype(BF16)
                dv = dv + lax.dot_general(a, doi, _TN_DIMS, preferred_element_type=F32)
                dk = dk + lax.dot_general(da, qi, _TN_DIMS, preferred_element_type=F32)
                dq_acc[qrows, :] += jnp.dot(da, kj, preferred_element_type=F32)
                return dk, dv

            zero = jnp.zeros((t, hd), F32)
            dk, dv = lax.fori_loop(j, nt, over_queries, (zero, zero))
            dsec_ref[1, krows, :] = dk.astype(BF16)
            dsec_ref[2, krows, :] = dv.astype(BF16)
            return 0

        lax.fori_loop(0, nt, over_keys, 0)
        dsec_ref[0] = dq_acc[...].astype(BF16)

    def col(off):
        return pl.BlockSpec((s, hd), lambda h: (0, off + h))

    return pl.pallas_call(
        body, name="ret_bwd", grid=(nh,),
        in_specs=[col(c0), col(c0 + nh), col(c0 + 2 * nh), col(c0 + 3 * nh), col(0), col(mixed_blocks)]
        + [ANY] * (1 + len(after)),
        out_specs=pl.BlockSpec((4, s, hd), lambda h: (1, 0, h)),
        out_shape=jax.ShapeDtypeStruct(dsec.shape, BF16),
        input_output_aliases={6: 0},
        scratch_shapes=[pltpu.VMEM((s, hd), BF16)] * 4 + [pltpu.VMEM((s, hd), F32)]
        + [pltpu.VMEM((nt, t, t), F32)],
        compiler_params=_params(("arbitrary",)),
    )(proj, proj, proj, proj, ret_raw, dmixed, dsec, *after)


_FLIPS = (2, 1, 3)


def _other_chips(x, y):
    return [(1 - x, y), (x, 1 - y), (1 - x, 1 - y)]


_HBM = pl.BlockSpec(memory_space=pltpu.HBM)
_SEM = pl.BlockSpec(memory_space=pltpu.SEMAPHORE)
_EFFECT = pltpu.SideEffectType.DATAFLOW_SIDE_EFFECTING


def _in_hbm(a):
    return pltpu.with_memory_space_constraint(a, pltpu.HBM)


def _weight_view(w, column_sharded):
    if column_sharded:
        return w.reshape(2, w.shape[0] // 2, w.shape[1])
    return w.reshape(N_CHIPS, 2, w.shape[0] // (2 * N_CHIPS), w.shape[1])


def _weight_unview(v):
    if v.ndim == 3:
        return v.reshape(2 * v.shape[1], v.shape[2])
    return v.reshape(N_CHIPS * 2 * v.shape[2], v.shape[3])


def _weight_region(buf, shard, half):
    if len(buf.shape) == 3:
        cols = buf.shape[2] // N_CHIPS
        return buf.at[half, :, pl.ds(shard * cols, cols)]
    return buf.at[shard, half]


def _remote(where, send_sem, recv_sem, to):
    return pltpu.make_async_remote_copy(src_ref=where, dst_ref=where, send_sem=send_sem, recv_sem=recv_sem,
                                        device_id=to, device_id_type=MESH)


def _for_my_shard(fn):
    x, y, _ = _place()
    for ss in range(N_CHIPS):
        pl.when(2 * x + y == ss)(functools.partial(fn, ss))


def _gather_start(views, name):
    n_w = len(views)

    def body(*refs):
        send_sems, recv_sems = refs[n_w:n_w + 2]
        bufs = refs[n_w + 2:]
        x, y, c = _place()

        def start(ss):
            for w in range(n_w):
                for j, chip in enumerate(_other_chips(x, y)):
                    _remote(_weight_region(bufs[w], ss, c), send_sems.at[3 * w + j], recv_sems.at[3 * w + j],
                            (*chip, c)).start()

        _for_my_shard(start)

    return pl.pallas_call(
        body, name=name,
        in_specs=[_HBM] * n_w, out_specs=[_SEM, _SEM] + [_HBM] * n_w,
        out_shape=[pltpu.SemaphoreType.DMA((3 * n_w,)), pltpu.SemaphoreType.DMA((3 * n_w,))]
        + [pltpu.HBM(v.shape, BF16) for v in views],
        input_output_aliases={w: 2 + w for w in range(n_w)},
        compiler_params=pltpu.CompilerParams(has_side_effects=_EFFECT),
    )(*[_in_hbm(v) for v in views])


def _gather_forward(views, which, send_sems, recv_sems, after, name):
    n_w = len(views)

    def body(*refs):
        send_in, recv_in = refs[n_w:n_w + 2]
        fwd_send, fwd_recv = refs[n_w + 3:n_w + 5]
        bufs = refs[n_w + 5:]
        x, y, c = _place()
        sibling = (x, y, 1 - c)

        def forward(ss):
            for i, w in enumerate(which):
                for j in range(3):
                    landed = _weight_region(bufs[i], ss ^ _FLIPS[j], c)
                    _remote(landed, send_in.at[3 * w + j], recv_in.at[3 * w + j], sibling).wait_recv()
                    _remote(landed, fwd_send.at[3 * i + j], fwd_recv.at[3 * i + j], sibling).start()

        _for_my_shard(forward)
        for i, w in enumerate(which):
            for j in range(3):
                _remote(_weight_region(bufs[i], 0, 0), send_in.at[3 * w + j], recv_in.at[3 * w + j],
                        sibling).wait_send()

    return pl.pallas_call(
        body, name=name,
        in_specs=[_HBM] * n_w + [_SEM, _SEM, ANY], out_specs=[_SEM, _SEM] + [_HBM] * n_w,
        out_shape=[pltpu.SemaphoreType.DMA((3 * n_w,)), pltpu.SemaphoreType.DMA((3 * n_w,))]
        + [pltpu.HBM(v.shape, BF16) for v in views],
        input_output_aliases={w: 2 + w for w in range(n_w)},
        compiler_params=pltpu.CompilerParams(has_side_effects=_EFFECT),
    )(*views, send_sems, recv_sems, after)


def _gather_end(views, fwd_send, fwd_recv, after, name):
    n_w = len(views)

    def body(*refs):
        fwd_send_ref, fwd_recv_ref = refs[n_w:n_w + 2]
        bufs = refs[n_w + 3:]
        x, y, c = _place()
        for i in range(n_w):
            for j in range(3):
                cp = _remote(_weight_region(bufs[i], 0, 0), fwd_send_ref.at[3 * i + j], fwd_recv_ref.at[3 * i + j],
                             (x, y, 1 - c))
                cp.wait_recv()
                cp.wait_send()

    outs = pl.pallas_call(
        body, name=name,
        in_specs=[_HBM] * n_w + [_SEM, _SEM, ANY], out_specs=[_HBM] * n_w,
        out_shape=[pltpu.HBM(v.shape, BF16) for v in views],
        input_output_aliases={w: w for w in range(n_w)},
        compiler_params=pltpu.CompilerParams(has_side_effects=_EFFECT),
    )(*views, fwd_send, fwd_recv, after)
    return [_weight_unview(o) for o in outs]


def _split_start(name, bufs, n_sems, copies):
    n = len(bufs)

    def body(*refs):
        send_sems, recv_sems = refs[n:n + 2]
        for cp in copies(refs[n + 2:], send_sems, recv_sems):
            cp.start()

    outs = pl.pallas_call(
        body, name=name,
        in_specs=[_HBM] * n, out_specs=[_SEM, _SEM] + [_HBM] * n,
        out_shape=[pltpu.SemaphoreType.DMA((n_sems,)), pltpu.SemaphoreType.DMA((n_sems,))]
        + [pltpu.HBM(b.shape, b.dtype) for b in bufs],
        input_output_aliases={i: 2 + i for i in range(n)},
        compiler_params=pltpu.CompilerParams(has_side_effects=_EFFECT),
    )(*[_in_hbm(b) for b in bufs])
    return outs[0], outs[1], list(outs[2:])


def _split_wait(name, bufs, send_sems, recv_sems, copies, after):
    n = len(bufs)

    def body(*refs):
        send_ref, recv_ref = refs[n:n + 2]
        for cp in copies(refs[n + 3:], send_ref, recv_ref):
            cp.wait()

    return list(pl.pallas_call(
        body, name=name,
        in_specs=[_HBM] * n + [_SEM, _SEM, ANY], out_specs=[_HBM] * n,
        out_shape=[pltpu.HBM(b.shape, b.dtype) for b in bufs],
        input_output_aliases={i: i for i in range(n)},
        compiler_params=pltpu.CompilerParams(has_side_effects=_EFFECT),
    )(*bufs, send_sems, recv_sems, after))


def _halves_copies(n_w):
    def copies(bufs, send_sems, recv_sems):
        x, y, c = _place()
        out = []
        for w in range(n_w):
            view, land = bufs[w], bufs[n_w + w]
            src = view.at[1 - c] if len(view.shape) == 3 else view.at[:, 1 - c]
            out.append(pltpu.make_async_remote_copy(
                src_ref=src, dst_ref=land, send_sem=send_sems.at[w], recv_sem=recv_sems.at[w],
                device_id=(x, y, 1 - c), device_id_type=MESH))
        return out
    return copies


def _pieces_copies(n_w):
    def copies(bufs, send_sems, recv_sems):
        x, y, c = _place()
        out = []
        for w in range(n_w):
            for j, (cx, cy) in enumerate(_other_chips(x, y)):
                out.append(pltpu.make_async_remote_copy(
                    src_ref=bufs[w].at[2 * cx + cy], dst_ref=bufs[n_w + w].at[j],
                    send_sem=send_sems.at[3 * w + j], recv_sem=recv_sems.at[3 * w + j],
                    device_id=(cx, cy, c), device_id_type=MESH))
        return out
    return copies


def _join_copies(n_w):
    def copies(bufs, send_sems, recv_sems):
        x, y, c = _place()
        return [pltpu.make_async_remote_copy(
            src_ref=bufs[w].at[c], dst_ref=bufs[w].at[c], send_sem=send_sems.at[w], recv_sem=recv_sems.at[w],
            device_id=(x, y, 1 - c), device_id_type=MESH) for w in range(n_w)]
    return copies


def _grad_view(g, column_sharded):
    return _weight_view(g, column_sharded)


def _halves_landing(view):
    shape = view.shape[1:] if view.ndim == 3 else (N_CHIPS,) + view.shape[2:]
    return lax.empty(shape, BF16)


def _halves_start(tag, grads, column_sharded):
    views = [_weight_view(g, cs) for g, cs in zip(grads, column_sharded)]
    n = len(views)
    return _split_start("halves_start_" + tag, views + [_halves_landing(v) for v in views], n, _halves_copies(n))


def _halves_wait(tag, state, after):
    send_sems, recv_sems, bufs = state
    n = len(bufs) // 2
    bufs = _split_wait("halves_wait_" + tag, bufs, send_sems, recv_sems, _halves_copies(n), after)
    return bufs[:n], bufs[n:]


def _pieces_start(tag, pieces):
    n = len(pieces)
    landing = [lax.empty((3,) + p.shape[1:], BF16) for p in pieces]
    return _split_start("pieces_start_" + tag, list(pieces) + landing, 3 * n, _pieces_copies(n))


def _pieces_wait(tag, state, after):
    send_sems, recv_sems, bufs = state
    n = len(bufs) // 2
    bufs = _split_wait("pieces_wait_" + tag, bufs, send_sems, recv_sems, _pieces_copies(n), after)
    return bufs[:n], bufs[n:]


def _join_start(tag, shards):
    n = len(shards)
    return _split_start("join_start_" + tag, list(shards), n, _join_copies(n))


def _join_wait(tag, state, after):
    send_sems, recv_sems, bufs = state
    bufs = _split_wait("join_wait_" + tag, bufs, send_sems, recv_sems, _join_copies(len(bufs)), after)
    return [b.reshape(2 * b.shape[1], b.shape[2]) for b in bufs]


def _chip_sum_col(g3, sib, c_arr, name):
    _, hk, n = g3.shape
    cols = n // N_CHIPS
    tr = _row_tile(hk, cols * 2, limit=4 * 1024 * 1024)

    def body(c_ref, g_ref, s_ref, o_ref):
        del c_ref
        o_ref[...] = (g_ref[...].astype(F32) + s_ref[...].astype(F32)).astype(BF16)

    grid_spec = pltpu.PrefetchScalarGridSpec(
        num_scalar_prefetch=1, grid=(N_CHIPS, hk // tr),
        in_specs=[pl.BlockSpec((None, tr, cols), lambda p, r, c_ref: (c_ref[0], r, p)),
                  pl.BlockSpec((tr, cols), lambda p, r, c_ref: (r, p))],
        out_specs=pl.BlockSpec((None, tr, cols), lambda p, r, c_ref: (p, r, 0)))
    return pl.pallas_call(
        body, name=name, grid_spec=grid_spec,
        out_shape=jax.ShapeDtypeStruct((N_CHIPS, hk, cols), BF16),
        compiler_params=_params(("parallel", "parallel")),
    )(c_arr, g3, sib)


def _chip_sum_row(g4, sib, c_arr, name):
    _, _, hr, n = g4.shape
    tr = _row_tile(hr, n * 2, limit=4 * 1024 * 1024)

    def body(c_ref, g_ref, s_ref, o_ref):
        del c_ref
        o_ref[...] = (g_ref[...].astype(F32) + s_ref[...].astype(F32)).astype(BF16)

    grid_spec = pltpu.PrefetchScalarGridSpec(
        num_scalar_prefetch=1, grid=(N_CHIPS, hr // tr),
        in_specs=[pl.BlockSpec((None, None, tr, n), lambda p, r, c_ref: (p, c_ref[0], r, 0)),
                  pl.BlockSpec((None, tr, n), lambda p, r, c_ref: (p, r, 0))],
        out_specs=pl.BlockSpec((None, tr, n), lambda p, r, c_ref: (p, r, 0)))
    return pl.pallas_call(
        body, name=name, grid_spec=grid_spec,
        out_shape=jax.ShapeDtypeStruct((N_CHIPS, hr, n), BF16),
        compiler_params=_params(("parallel", "parallel")),
    )(c_arr, g4, sib)


def _sum_pieces(pieces, received, place_arr, name):
    _, r, n = pieces.shape
    tr = _row_tile(r, n * 4, limit=4 * 1024 * 1024)

    def body(p_ref, own_ref, r0_ref, r1_ref, r2_ref, o_ref):
        del p_ref
        acc = own_ref[...].astype(F32) + r0_ref[...].astype(F32)
        acc = acc + r1_ref[...].astype(F32)
        o_ref[...] = acc + r2_ref[...].astype(F32)

    def recv_spec(j):
        return pl.BlockSpec((None, tr, n), lambda i, p_ref: (j, i, 0))

    grid_spec = pltpu.PrefetchScalarGridSpec(
        num_scalar_prefetch=1, grid=(r // tr,),
        in_specs=[pl.BlockSpec((None, tr, n), lambda i, p_ref: (p_ref[0], i, 0)),
                  recv_spec(0), recv_spec(1), recv_spec(2)],
        out_specs=pl.BlockSpec((None, tr, n), lambda i, p_ref: (p_ref[1], i, 0)))
    return pl.pallas_call(
        body, name=name, grid_spec=grid_spec,
        out_shape=jax.ShapeDtypeStruct((2, r, n), F32),
        compiler_params=_params(("parallel",)),
    )(place_arr, pieces, received, received, received)


def _norm_weights_step(parts, w, m, v, after=()):
    rows, d = parts.shape
    after = tuple(after)

    def body(p_ref, w_ref, m_ref, v_ref, *rest):
        g_ref, d_ref, mo_ref, vo_ref, gathered, send_sems, recv_sems = rest[len(after):]
        x, y, c = _place()
        me = 4 * x + 2 * y + c
        gathered[me] = p_ref[...]
        copies = []
        for k in range(1, N_DEV):
            peer = (x ^ ((k >> 2) & 1), y ^ ((k >> 1) & 1), c ^ (k & 1))
            copies.append(pltpu.make_async_remote_copy(
                src_ref=p_ref, dst_ref=gathered.at[me], send_sem=send_sems.at[k - 1],
                recv_sem=recv_sems.at[k - 1], device_id=peer, device_id_type=MESH))
        for cp in copies:
            cp.start()
        for cp in copies:
            cp.wait()
        g = gathered[0]
        for k in range(1, N_DEV):
            g = g + gathered[k]
        delta, m_new, v_new = _adamw_math(w_ref[...], g, m_ref[...], v_ref[...])
        g_ref[...] = g
        d_ref[...] = delta
        mo_ref[...] = m_new
        vo_ref[...] = v_new

    vmem = pl.BlockSpec(memory_space=pltpu.VMEM)
    shp = jax.ShapeDtypeStruct((rows, d), F32)
    return pl.pallas_call(
        body, name="norm_weights_step",
        in_specs=[vmem] * 4 + [ANY] * len(after), out_specs=[vmem] * 4, out_shape=[shp] * 4,
        scratch_shapes=[pltpu.VMEM((N_DEV, rows, d), F32), pltpu.SemaphoreType.DMA((N_DEV - 1,)),
                        pltpu.SemaphoreType.DMA((N_DEV - 1,))],
        compiler_params=pltpu.CompilerParams(has_side_effects=True),
    )(parts, w, m, v, *after)


def kernel(x, norm_mix_w, w_in, w_out, norm_ffn_w, w_gate, w_up, w_down, norm_final_w, loss_target, m_norm_mix_w, m_w_in, m_w_out, m_norm_ffn_w, m_w_gate, m_w_up, m_w_down, m_norm_final_w, v_norm_mix_w, v_w_in, v_w_out, v_norm_ffn_w, v_w_gate, v_w_up, v_w_down, v_norm_final_w):
    s, d = x.shape[1], x.shape[2]
    xs = x.reshape(s, d)
    target = loss_target.reshape(s, d)
    big = {"w_in": (w_in, m_w_in, v_w_in), "w_out": (w_out, m_w_out, v_w_out),
           "w_gate": (w_gate, m_w_gate, v_w_gate), "w_up": (w_up, m_w_up, v_w_up),
           "w_down": (w_down, m_w_down, v_w_down)}
    big = {k: tuple(a.reshape(a.shape[1:]) for a in t) for k, t in big.items()}
    col_names, row_names = ("w_in", "w_gate", "w_up"), ("w_out", "w_down")
    n_in = N_CHIPS * big["w_in"][0].shape[1]
    ffn = N_CHIPS * big["w_gate"][0].shape[1]
    mix = ATTN_WIDTH + RET_WIDTH
    c_arr = lax.axis_index("c").astype(I32).reshape(1)
    shard_arr = (2 * lax.axis_index("x") + lax.axis_index("y")).astype(I32).reshape(1)
    place_arr = jnp.concatenate([shard_arr, c_arr])

    def cast(k, after=()):
        return _weight_view(_cast_into_full(big[k][0], shard_arr, k in col_names, "cast_" + k, after), k in col_names)

    send_in, recv_in, v_in = _gather_start([cast("w_in")], "gather_start_in")
    rest = ("w_out", "w_gate", "w_up", "w_down")
    send_sems, recv_sems, v_out, v_gate, v_up, v_down = _gather_start(
        [cast(k, after=[v_in]) for k in rest], "gather_start_rest")

    sec = ATTN_WIDTH

    def section(p, rows):
        return pl.BlockSpec((None, rows, sec), lambda i, j, kk: (p, i, 0))

    h1 = _rms_fwd(xs, norm_mix_w, "rms_mix_fwd")
    fs, fr, v_in = _gather_forward([v_in], [0], send_in, recv_in, v_out, "gather_forward_in")
    wi, = _gather_end([v_in], fs, fr, h1, "gather_end_in")
    proj, = _matmul("in_proj", "nn", [h1], [wi], [0], s, n_in, d, s, 512, d, [], [F32], _epi_plain)
    fs_o, fr_o, v_out = _gather_forward([v_out], [0], send_sems, recv_sems, proj, "gather_forward_out")
    mixed, attn_o, lse = _attn_fwd(proj, after=[v_out])
    fs_g, fr_g, v_gate = _gather_forward([v_gate], [1], send_sems, recv_sems, attn_o, "gather_forward_gate")
    mixed, ret_raw = _ret_fwd(proj, mixed, after=[v_gate])
    wo, = _gather_end([v_out], fs_o, fr_o, ret_raw, "gather_end_out")
    x1, = _matmul("out_proj", "nn", [mixed, mixed], [wo, wo], [0, 0], s, d, sec, s, 512, sec, [xs], [F32],
                  _epi_residual, b_koff=[0, 1], a_specs=[section(0, s), section(1, s)])
    h2 = _rms_fwd(x1, norm_ffn_w, "rms_ffn_fwd")
    fs_u, fr_u, v_up = _gather_forward([v_up], [2], send_sems, recv_sems, h2, "gather_forward_up")
    wg, = _gather_end([v_gate], fs_g, fr_g, v_up, "gather_end_gate")
    wu, = _gather_end([v_up], fs_u, fr_u, wg, "gather_end_up")
    gate, up, act = _matmul("gate_up", "nn", [h2, h2], [wg, wu], [0, 1], s, ffn, d, s, 256, d, [],
                            [BF16, BF16, BF16], _epi_swiglu)
    fs, fr, v_down = _gather_forward([v_down], [3], send_sems, recv_sems, act, "gather_forward_down")
    wd, = _gather_end([v_down], fs, fr, act, "gather_end_down")
    x2, = _matmul("down_proj", "nn", [act], [wd], [0], s, d, ffn, s // 2, 256, ffn, [x1], [F32],
                  _epi_residual)
    loss_row, dx2, dx2b, dwf = _final_norm_loss(x2, norm_final_w.reshape(1, d), target, "final_norm_loss")

    names = col_names + row_names
    grads, new = {}, {}

    def chip_sums(tag_names, views, sibs):
        return [(_chip_sum_col if k in col_names else _chip_sum_row)(v, sb, c_arr, "chip_sum_" + k)
                for k, v, sb in zip(tag_names, views, sibs)]

    def piece_sums(tag_names, pieces, received):
        return [_sum_pieces(p, r, place_arr, "sum_pieces_" + k) for k, p, r in zip(tag_names, pieces, received)]

    def update(k):
        new[k] = _adamw(big[k][0], grads[k], big[k][1], big[k][2], "adamw_" + k)

    dgate, dup = _matmul("d_act", "nt", [dx2b], [wd], [0], s, ffn, d, s, 256, d, [gate, up],
                         [BF16, BF16], _epi_swiglu_bwd)
    g_wd, = _matmul("g_w_down", "tn", [act], [dx2b], [0], ffn, d, s, 512, d, s, [], [BF16], _epi_plain)
    halves_d = _halves_start("down", [g_wd], [False])
    dh2, = _matmul("d_h2", "nt", [dgate, dup], [wg, wu], [0, 0], s, d, ffn, s // 4, 256, ffn, [], [F32],
                   _epi_plain, after=halves_d[2][-1:])
    pieces_d = _pieces_start("down", chip_sums(["w_down"], *_halves_wait("down", halves_d, dh2)))
    g_wg, g_wu = _matmul("g_w_gate_up", "tn", [h2, h2], [dgate, dup], [0, 1], d, ffn, s, 1024, 512, s, [],
                         [BF16, BF16], _epi_two, after=pieces_d[2][-1:])
    halves_gu = _halves_start("gate_up", [g_wg, g_wu], [True, True])
    dx1, dx1b, dw_ffn = _rms_bwd(x1, norm_ffn_w, dh2, dx2, "rms_ffn_bwd", after=halves_gu[2][-1:])

    dmixed, = _matmul("d_mixed", "nt", [dx1b], [wo], [0], s, mix, d, s, 512, d, [], [F32], _epi_plain)
    pieces_gu = _pieces_start("gate_up", chip_sums(["w_gate", "w_up"], *_halves_wait("gate_up", halves_gu, dmixed)))
    per = sec // 512
    g_wo, = _matmul("g_w_out", "tn", [mixed], [dx1b], [0], mix, d, s, 512, d, s, [], [BF16], _epi_plain,
                    after=pieces_gu[2][-1:],
                    a_specs=[pl.BlockSpec((None, s, 512), lambda i, j, kk: (i // per, 0, i % per))])
    halves_o = _halves_start("out", [g_wo], [False])
    dsec = _attn_bwd(proj, attn_o, lse, dmixed, after=halves_o[2][-1:])
    pieces_o = _pieces_start("out", chip_sums(["w_out"], *_halves_wait("out", halves_o, dsec)))
    dsec = _ret_bwd(proj, ret_raw, dmixed, dsec, after=pieces_o[2][-1:])
    where = [0, 1, 2, 4, 5, 6, 7]
    n_sec = len(where)
    g_wi, = _matmul("g_w_in", "tn", [h1], [dsec], [0], d, n_in, s, 1024, sec, s, [], [BF16], _epi_plain,
                    b_specs=[pl.BlockSpec((None, s, sec), lambda i, j, kk: (j + (j >= 3).astype(I32), 0, 0))])
    halves_i = _halves_start("in", [g_wi], [True])
    dh1, = _matmul("d_h1", "nt", [dsec] * n_sec, [wi] * n_sec, [0] * n_sec, s, d, sec, s // 2, 256, sec, [], [F32],
                   _epi_plain, b_koff=list(range(n_sec)), after=halves_i[2][-1:],
                   a_specs=[section(p, s // 2) for p in where])
    pieces_i = _pieces_start("in", chip_sums(["w_in"], *_halves_wait("in", halves_i, dh1)))
    grad_x, _, dw_mix = _rms_bwd(xs, norm_mix_w, dh1, dx1, "rms_mix_bwd", after=pieces_i[2][-1:])

    def rows8(*vs):
        return jnp.concatenate([v.reshape(1, d) for v in vs] + [jnp.zeros((8 - len(vs), d), F32)], axis=0)

    join_d = _join_start("down", piece_sums(["w_down"], *_pieces_wait("down", pieces_d, grad_x)))
    join_gu = _join_start("gate_up", piece_sums(["w_gate", "w_up"], *_pieces_wait("gate_up", pieces_gu, join_d[2][0])))
    join_o = _join_start("out", piece_sums(["w_out"], *_pieces_wait("out", pieces_o, join_gu[2][0])))
    grads["w_down"], = _join_wait("down", join_d, join_o[2][0])
    update("w_down")
    grads["w_gate"], grads["w_up"] = _join_wait("gate_up", join_gu, new["w_down"][0])
    update("w_gate")
    update("w_up")
    grads["w_out"], = _join_wait("out", join_o, new["w_up"][0])
    update("w_out")
    join_i = _join_start("in", piece_sums(["w_in"], *_pieces_wait("in", pieces_i, new["w_out"][0])))
    ng, nd, nm, nv = _norm_weights_step(
        rows8(dw_mix, dw_ffn, dwf), rows8(norm_mix_w, norm_ffn_w, norm_final_w),
        rows8(m_norm_mix_w, m_norm_ffn_w, m_norm_final_w), rows8(v_norm_mix_w, v_norm_ffn_w, v_norm_final_w),
        after=join_i[2][:1])
    grads["w_in"], = _join_wait("in", join_i, ng)
    update("w_in")

    loss = lax.psum(loss_row[0, 0], ("x", "y", "c"))

    def pack(small, per_weight):
        lead = lambda a: a.reshape((1,) + a.shape)
        return (small[0:1], lead(per_weight["w_in"]), lead(per_weight["w_out"]), small[1:2],
                lead(per_weight["w_gate"]), lead(per_weight["w_up"]), lead(per_weight["w_down"]), small[2])

    return (loss, grad_x.reshape(1, s, d),
            *pack(ng, {k: new[k][3] for k in names}),
            *pack(nd, {k: new[k][0] for k in names}),
            *pack(nm, {k: new[k][1] for k in names}),
            *pack(nv, {k: new[k][2] for k in names}))
```

```python
import functools
import math

import jax
import jax.numpy as jnp
from jax import lax
from jax.experimental import pallas as pl
from jax.experimental.pallas import tpu as pltpu

F32 = jnp.float32
BF16 = jnp.bfloat16
I32 = jnp.int32
MESH = pl.DeviceIdType.MESH
ANY = pl.BlockSpec(memory_space=pl.ANY)

ATTN_HEADS = 8
ATTN_HEAD_DIM = 128
RET_HEADS = 4
RET_HEAD_DIM = 256
ATTN_WIDTH = ATTN_HEADS * ATTN_HEAD_DIM
RET_WIDTH = RET_HEADS * RET_HEAD_DIM
DILATED_PATTERNS = ((128, 1), (512, 4), (2048, 16))
NORM_EPS = 1e-6
ADAM_LR = 0.001
ADAM_B1 = 0.9
ADAM_B2 = 0.999
ADAM_EPS = 1e-08
ADAM_WD = 0.01
ADAM_STEP = 10

N_CHIPS = 4
N_DEV = 8
NEG_BIG = -1e30
SEQ_TILE = 512
ATTN_FWD_HEADS_PER_STEP = 2
ATTN_HEADS_PER_STEP = 1
VMEM_LIMIT_BYTES = 56 * 1024 * 1024


def _params(semantics=None, vmem=VMEM_LIMIT_BYTES):
    return pltpu.CompilerParams(dimension_semantics=semantics, vmem_limit_bytes=vmem)


def _row_tile(rows, row_bytes, limit=2 * 1024 * 1024, mult=16):
    best = None
    for t in range(mult, rows + 1, mult):
        if rows % t == 0 and t * row_bytes <= limit:
            best = t
    assert best is not None, (rows, row_bytes)
    return best


def _sigmoid(x):
    return 1.0 / (1.0 + jnp.exp(-x))


def _select_by_index(idx, values):
    out = jnp.float32(values[-1])
    for i in range(len(values) - 2, -1, -1):
        out = jnp.where(idx == i, jnp.float32(values[i]), out)
    return out


def _place():
    x, y, c = lax.axis_index("x"), lax.axis_index("y"), lax.axis_index("c")
    return x, y, c


def _cast_into_full(w, shard_arr, column_sharded, name, after=()):
    after = tuple(after)
    rows, cols = w.shape
    tr = _row_tile(rows, cols * 4)
    steps = rows // tr
    if column_sharded:
        out_shape, out_map = (rows, N_CHIPS * cols), (lambda i, s_ref: (i, s_ref[0]))
    else:
        out_shape, out_map = (N_CHIPS * rows, cols), (lambda i, s_ref: (s_ref[0] * steps + i, 0))

    def body(s_ref, w_ref, *rest):
        del s_ref
        rest[-1][...] = w_ref[...].astype(BF16)

    grid_spec = pltpu.PrefetchScalarGridSpec(
        num_scalar_prefetch=1, grid=(steps,),
        in_specs=[pl.BlockSpec((tr, cols), lambda i, s_ref: (i, 0))] + [ANY] * len(after),
        out_specs=pl.BlockSpec((tr, cols), out_map))
    return pl.pallas_call(
        body, name=name, grid_spec=grid_spec,
        out_shape=jax.ShapeDtypeStruct(out_shape, BF16),
        compiler_params=_params(("parallel",)),
    )(shard_arr, w, *after)


def _rms_fwd(x, w, name):
    rows, d = x.shape
    tr = 256

    def body(x_ref, w_ref, h_ref):
        xv = x_ref[...]
        r = lax.rsqrt(jnp.mean(xv * xv, axis=-1, keepdims=True) + NORM_EPS)
        h_ref[...] = (xv * r * w_ref[...]).astype(BF16)

    return pl.pallas_call(
        body, name=name, grid=(rows // tr,),
        in_specs=[pl.BlockSpec((tr, d), lambda i: (i, 0)), pl.BlockSpec((1, d), lambda i: (0, 0))],
        out_specs=pl.BlockSpec((tr, d), lambda i: (i, 0)),
        out_shape=jax.ShapeDtypeStruct((rows, d), BF16),
        compiler_params=_params(("parallel",)),
    )(x, w)


def _rms_bwd(x, w, dh, dres, name, after=()):
    rows, d = x.shape
    tr = 256
    after = tuple(after)

    def body(x_ref, w_ref, dh_ref, dres_ref, *rest):
        dx_ref, dxb_ref, dw_ref = rest[len(after):]
        xv = x_ref[...]
        r = lax.rsqrt(jnp.mean(xv * xv, axis=-1, keepdims=True) + NORM_EPS)
        xhat = xv * r
        dy = dh_ref[...]
        dxhat = dy * w_ref[...]
        dx = dres_ref[...] + r * (dxhat - xhat * jnp.mean(dxhat * xhat, axis=-1, keepdims=True))
        dx_ref[...] = dx
        dxb_ref[...] = dx.astype(BF16)
        part = jnp.sum(dy * xhat, axis=0, keepdims=True)

        @pl.when(pl.program_id(0) == 0)
        def _():
            dw_ref[...] = part

        @pl.when(pl.program_id(0) != 0)
        def _():
            dw_ref[...] += part

    row = pl.BlockSpec((tr, d), lambda i: (i, 0))
    vec = pl.BlockSpec((1, d), lambda i: (0, 0))
    return pl.pallas_call(
        body, name=name, grid=(rows // tr,),
        in_specs=[row, vec, row, row] + [ANY] * len(after),
        out_specs=[row, row, vec],
        out_shape=[jax.ShapeDtypeStruct((rows, d), F32), jax.ShapeDtypeStruct((rows, d), BF16),
                   jax.ShapeDtypeStruct((1, d), F32)],
        compiler_params=_params(("arbitrary",)),
    )(x, w, dh, dres, *after)


def _final_norm_loss(x2, w, target, name):
    rows, d = x2.shape
    tr = 256

    def body(x_ref, w_ref, t_ref, loss_ref, dx_ref, dxb_ref, dw_ref):
        xv = x_ref[...]
        wv = w_ref[...]
        r = lax.rsqrt(jnp.mean(xv * xv, axis=-1, keepdims=True) + NORM_EPS)
        xhat = xv * r
        err = xhat * wv - t_ref[...]
        part_loss = 0.5 * jnp.sum(jnp.mean(err * err, axis=-1, keepdims=True), axis=0, keepdims=True)
        dy = err * (1.0 / d)
        dxhat = dy * wv
        dx = r * (dxhat - xhat * jnp.mean(dxhat * xhat, axis=-1, keepdims=True))
        dx_ref[...] = dx
        dxb_ref[...] = dx.astype(BF16)
        part_dw = jnp.sum(dy * xhat, axis=0, keepdims=True)
        part_loss = jnp.broadcast_to(part_loss, (1, 128))

        @pl.when(pl.program_id(0) == 0)
        def _():
            dw_ref[...] = part_dw
            loss_ref[...] = part_loss

        @pl.when(pl.program_id(0) != 0)
        def _():
            dw_ref[...] += part_dw
            loss_ref[...] += part_loss

    row = pl.BlockSpec((tr, d), lambda i: (i, 0))
    vec = pl.BlockSpec((1, d), lambda i: (0, 0))
    return pl.pallas_call(
        body, name=name, grid=(rows // tr,),
        in_specs=[row, vec, row],
        out_specs=[pl.BlockSpec((1, 128), lambda i: (0, 0)), row, row, vec],
        out_shape=[jax.ShapeDtypeStruct((1, 128), F32), jax.ShapeDtypeStruct((rows, d), F32),
                   jax.ShapeDtypeStruct((rows, d), BF16), jax.ShapeDtypeStruct((1, d), F32)],
        compiler_params=_params(("arbitrary",)),
    )(x2, w, target)


def _adamw_math(w, g, m, v):
    m = ADAM_B1 * m + (1.0 - ADAM_B1) * g
    v = ADAM_B2 * v + (1.0 - ADAM_B2) * (g * g)
    m_hat = m / (1.0 - ADAM_B1 ** ADAM_STEP)
    v_hat = v / (1.0 - ADAM_B2 ** ADAM_STEP)
    delta = -ADAM_LR * (m_hat / (jnp.sqrt(v_hat) + ADAM_EPS) + ADAM_WD * w)
    return delta, m, v


def _adamw(w, g, m, v, name):
    rows, cols = w.shape
    tr = _row_tile(rows, cols * 4)

    def body(w_ref, g_ref, m_ref, v_ref, d_ref, mo_ref, vo_ref, go_ref):
        g = g_ref[...]
        delta, m_new, v_new = _adamw_math(w_ref[...], g, m_ref[...], v_ref[...])
        d_ref[...] = delta
        mo_ref[...] = m_new
        vo_ref[...] = v_new
        go_ref[...] = g

    blk = pl.BlockSpec((tr, cols), lambda i: (i, 0))
    shp = jax.ShapeDtypeStruct((rows, cols), F32)
    return pl.pallas_call(
        body, name=name, grid=(rows // tr,),
        in_specs=[blk] * 4, out_specs=[blk] * 4, out_shape=[shp] * 4,
        compiler_params=_params(("parallel",)),
    )(w, g, m, v)


_DOT_DIMS = {"nn": ((1,), (0,)), "nt": ((1,), (1,)), "tn": ((0,), (0,))}


def _matmul(name, mode, a_list, b_list, acc_of, m, n, k, tm, tn, tk, extras, out_dtypes, epilogue,
            a_koff=None, b_koff=None, after=(), a_specs=None, b_specs=None, a_single_buffer=False):
    after = tuple(after)
    assert m % tm == 0 and n % tn == 0 and k % tk == 0, (name, m, n, k, tm, tn, tk)
    nk = k // tk
    n_acc = max(acc_of) + 1
    n_pairs = len(a_list)
    a_koff = a_koff or [0] * n_pairs
    b_koff = b_koff or [0] * n_pairs
    dims = (_DOT_DIMS[mode], ((), ()))
    n_ext, n_out = len(extras), len(out_dtypes)

    def body(*refs):
        a_refs = refs[:n_pairs]
        b_refs = refs[n_pairs:2 * n_pairs]
        e_refs = refs[2 * n_pairs:2 * n_pairs + n_ext]
        first_out = 2 * n_pairs + n_ext + len(after)
        o_refs = refs[first_out:first_out + n_out]
        acc_refs = refs[first_out + n_out:]

        parts = [None] * n_acc
        for p in range(n_pairs):
            d = lax.dot_general(a_refs[p][...], b_refs[p][...], dims, preferred_element_type=F32)
            parts[acc_of[p]] = d if parts[acc_of[p]] is None else parts[acc_of[p]] + d

        def finish(accs):
            outs = epilogue(accs, [e[...] for e in e_refs])
            for o_ref, o in zip(o_refs, outs):
                o_ref[...] = o.astype(o_ref.dtype)

        if nk == 1:
            finish(parts)
        else:
            kk = pl.program_id(2)

            @pl.when(kk == 0)
            def _():
                for acc_ref, part in zip(acc_refs, parts):
                    acc_ref[...] = part

            @pl.when(kk != 0)
            def _():
                for acc_ref, part in zip(acc_refs, parts):
                    acc_ref[...] += part

            @pl.when(kk == nk - 1)
            def _():
                finish([acc_ref[...] for acc_ref in acc_refs])

    def a_spec(off):
        mode_a = pl.Buffered(1) if a_single_buffer else None
        if mode == "tn":
            return pl.BlockSpec((tk, tm), lambda i, j, kk: (kk + off, i), pipeline_mode=mode_a)
        return pl.BlockSpec((tm, tk), lambda i, j, kk: (i, kk + off), pipeline_mode=mode_a)

    def b_spec(off):
        if mode == "nt":
            return pl.BlockSpec((tn, tk), lambda i, j, kk: (j, kk + off))
        return pl.BlockSpec((tk, tn), lambda i, j, kk: (kk + off, j))

    tile = pl.BlockSpec((tm, tn), lambda i, j, kk: (i, j))
    scratch = [pltpu.VMEM((tm, tn), F32) for _ in range(n_acc)] if nk > 1 else []
    return pl.pallas_call(
        body, name=name, grid=(m // tm, n // tn, nk),
        in_specs=(a_specs or [a_spec(o) for o in a_koff]) + (b_specs or [b_spec(o) for o in b_koff])
        + [tile] * n_ext + [ANY] * len(after),
        out_specs=[tile] * n_out,
        out_shape=[jax.ShapeDtypeStruct((m, n), dt) for dt in out_dtypes],
        scratch_shapes=scratch,
        compiler_params=_params(("parallel", "parallel", "arbitrary")),
    )(*a_list, *b_list, *extras, *after)


def _epi_plain(accs, extras):
    return (accs[0],)


def _epi_residual(accs, extras):
    return (accs[0] + extras[0],)


def _epi_two(accs, extras):
    return accs[0], accs[1]


def _epi_swiglu(accs, extras):
    g, u = accs
    return g, u, g * _sigmoid(g) * u


def _epi_swiglu_bwd(accs, extras):
    da = accs[0]
    g, u = (e.astype(F32) for e in extras)
    sg = _sigmoid(g)
    dg = da * u * sg * (1.0 + g * (1.0 - sg))
    du = da * g * sg
    return dg, du


_NT_DIMS = (((1,), (1,)), ((), ()))
_TN_DIMS = (((0,), (0,)), ((), ()))


def _tile_delta(tq, tk):
    return lax.broadcasted_iota(I32, (tq, tk), 0) - lax.broadcasted_iota(I32, (tq, tk), 1)


def _attn_log_count(delta):
    count = jnp.zeros(delta.shape, I32)
    for window, dilation in DILATED_PATTERNS:
        hit = ((delta & (dilation - 1)) == 0) & (delta <= window)
        count = count + jnp.where(hit, 1, 0)
    valid = (delta >= 0) & (count > 0)
    logm = jnp.where(count == 3, math.log(3.0), jnp.where(count == 2, math.log(2.0), 0.0))
    return jnp.where(valid, logm, NEG_BIG)


def _fill_attn_log_count(tab_ref):
    nb, t, _ = tab_ref.shape
    base = _tile_delta(t, t)
    for b in range(nb):
        tab_ref[b] = _attn_log_count(base + b * t)


def _fill_attn_bias(tab_ref, log_count_ref, slope):
    nb, t, _ = tab_ref.shape
    dist = _tile_delta(t, t).astype(F32)
    for b in range(nb):
        tab_ref[b] = log_count_ref[b] - slope * (dist + float(b * t))


def _fill_ret_decay(tab_ref, log_gamma):
    nb, t, _ = tab_ref.shape
    base = _tile_delta(t, t)
    for b in range(nb):
        tab_ref[b] = _ret_decay(base + b * t, log_gamma)


def _alibi_slopes():
    return [2.0 ** (-8.0 * (h + 1) / ATTN_HEADS) for h in range(ATTN_HEADS)]


def _attn_fwd(proj, after=()):
    s = proj.shape[0]
    t = SEQ_TILE
    hd = ATTN_HEAD_DIM
    hp = ATTN_FWD_HEADS_PER_STEP
    ng = ATTN_HEADS // hp
    w = hp * hd
    scale = 1.0 / math.sqrt(hd)
    slopes = _alibi_slopes()

    def body(q_ref, k_ref, v_ref, *rest):
        mix_ref, o_ref, lse_ref, kb, vb, bias_tab, log_count_tab = rest[len(after):]
        g = pl.program_id(0)
        i = pl.program_id(1)

        @pl.when((g == 0) & (i == 0))
        def _():
            _fill_attn_log_count(log_count_tab)

        @pl.when(i == 0)
        def _():
            kb[...] = k_ref[...].astype(BF16)
            vb[...] = v_ref[...].astype(BF16)
            for u in range(hp):
                _fill_attn_bias(bias_tab.at[u], log_count_tab, _select_by_index(g * hp + u, slopes))

        qs = [q_ref[:, u * hd:(u + 1) * hd].astype(BF16) for u in range(hp)]

        def step(j, carry):
            rows = pl.ds(pl.multiple_of(j * t, t), t)
            out = []
            for u in range(hp):
                m_i, l_i, acc = carry[u]
                lanes = slice(u * hd, (u + 1) * hd)
                sc = lax.dot_general(qs[u], kb[rows, lanes], _NT_DIMS, preferred_element_type=F32) * scale
                sc = sc + bias_tab[u, i - j]
                m_new = jnp.maximum(m_i, jnp.max(sc, axis=-1, keepdims=True))
                p = jnp.exp(sc - m_new)
                alpha = jnp.exp(m_i - m_new)
                l_new = alpha * l_i + jnp.sum(p, axis=-1, keepdims=True)
                acc = alpha * acc + jnp.dot(p.astype(BF16), vb[rows, lanes], preferred_element_type=F32)
                out.append((m_new, l_new, acc))
            return tuple(out)

        init = (jnp.full((t, 1), NEG_BIG, F32), jnp.zeros((t, 1), F32), jnp.zeros((t, hd), F32))
        final = lax.fori_loop(0, i + 1, step, (init,) * hp)
        for u in range(hp):
            m_i, l_i, acc = final[u]
            lanes = slice(u * hd, (u + 1) * hd)
            out = acc / l_i
            o_ref[:, lanes] = out
            mix_ref[:, lanes] = out.astype(BF16)
            lse_ref[:, lanes] = jnp.broadcast_to(m_i + jnp.log(l_i), (t, hd))

    return pl.pallas_call(
        body, name="attn_fwd", grid=(ng, s // t),
        in_specs=[pl.BlockSpec((t, w), lambda g, i: (i, g)),
                  pl.BlockSpec((s, w), lambda g, i: (0, ng + g)),
                  pl.BlockSpec((s, w), lambda g, i: (0, 2 * ng + g))] + [ANY] * len(after),
        out_specs=[pl.BlockSpec((None, t, w), lambda g, i: (0, i, g))] + [pl.BlockSpec((t, w), lambda g, i: (i, g))] * 2,
        out_shape=[jax.ShapeDtypeStruct((2, s, ATTN_WIDTH), BF16),
                   jax.ShapeDtypeStruct((s, ATTN_WIDTH), F32),
                   jax.ShapeDtypeStruct((s, ATTN_WIDTH), F32)],
        scratch_shapes=[pltpu.VMEM((s, w), BF16), pltpu.VMEM((s, w), BF16), pltpu.VMEM((hp, s // t, t, t), F32),
                        pltpu.VMEM((s // t, t, t), F32)],
        compiler_params=_params(("arbitrary", "arbitrary")),
    )(proj, proj, proj, *after)


def _attn_bwd(proj, attn_out, lse, dmixed, after=()):
    after = tuple(after)
    s = proj.shape[0]
    t = SEQ_TILE
    nt = s // t
    hd = ATTN_HEAD_DIM
    hp = ATTN_HEADS_PER_STEP
    ng = ATTN_HEADS // hp
    w = hp * hd
    scale = 1.0 / math.sqrt(hd)
    slopes = _alibi_slopes()

    def body(q_ref, k_ref, v_ref, o_ref, lse_ref, do_ref, *rest):
        dsec_ref, qb, kb, vb, dob, dsum, dq_acc, bias_tab, log_count_tab = rest[len(after):]
        g = pl.program_id(0)

        @pl.when(g == 0)
        def _():
            _fill_attn_log_count(log_count_tab)

        qb[...] = q_ref[...].astype(BF16)
        kb[...] = k_ref[...].astype(BF16)
        vb[...] = v_ref[...].astype(BF16)
        dob[...] = do_ref[...].astype(BF16)
        for u in range(hp):
            lanes = slice(u * hd, (u + 1) * hd)
            _fill_attn_bias(bias_tab.at[u], log_count_tab, _select_by_index(g * hp + u, slopes))
            rowsum = jnp.sum(do_ref[:, lanes] * o_ref[:, lanes], axis=-1, keepdims=True)
            dsum[:, lanes] = jnp.broadcast_to(rowsum, (s, hd))
        dq_acc[...] = jnp.zeros((s, w), F32)

        def over_keys(j, _):
            krows = pl.ds(pl.multiple_of(j * t, t), t)

            def over_queries(i, carry):
                qrows = pl.ds(pl.multiple_of(i * t, t), t)
                out = []
                for u in range(hp):
                    dk, dv = carry[u]
                    lanes = slice(u * hd, (u + 1) * hd)
                    qi, doi = qb[qrows, lanes], dob[qrows, lanes]
                    kj, vj = kb[krows, lanes], vb[krows, lanes]
                    lse_i = lse_ref[qrows, lanes][:, :1]
                    dsum_i = dsum[qrows, lanes][:, :1]
                    sc = lax.dot_general(qi, kj, _NT_DIMS, preferred_element_type=F32) * scale
                    p = jnp.exp(sc + bias_tab[u, i - j] - lse_i)
                    dp = lax.dot_general(doi, vj, _NT_DIMS, preferred_element_type=F32)
                    ds = (p * (dp - dsum_i)).astype(BF16)
                    dv = dv + lax.dot_general(p.astype(BF16), doi, _TN_DIMS, preferred_element_type=F32)
                    dk = dk + lax.dot_general(ds, qi, _TN_DIMS, preferred_element_type=F32)
                    dq_acc[qrows, lanes] += jnp.dot(ds, kj, preferred_element_type=F32)
                    out.append((dk, dv))
                return tuple(out)

            zero = jnp.zeros((t, hd), F32)
            final = lax.fori_loop(j, nt, over_queries, ((zero, zero),) * hp)
            for u in range(hp):
                lanes = slice(u * hd, (u + 1) * hd)
                dsec_ref[1, krows, lanes] = (final[u][0] * scale).astype(BF16)
                dsec_ref[2, krows, lanes] = final[u][1].astype(BF16)
            return 0

        lax.fori_loop(0, nt, over_keys, 0)
        dsec_ref[0] = (dq_acc[...] * scale).astype(BF16)

    def col(off):
        return pl.BlockSpec((s, w), lambda g: (0, off + g))

    return pl.pallas_call(
        body, name="attn_bwd", grid=(ng,),
        in_specs=[col(0), col(ng), col(2 * ng), col(0), col(0), col(0)] + [ANY] * len(after),
        out_specs=pl.BlockSpec((4, s, w), lambda g: (0, 0, g)),
        out_shape=jax.ShapeDtypeStruct((8, s, ATTN_WIDTH), BF16),
        scratch_shapes=[pltpu.VMEM((s, w), BF16)] * 4 + [pltpu.VMEM((s, w), F32)] * 2
        + [pltpu.VMEM((hp, nt, t, t), F32), pltpu.VMEM((nt, t, t), F32)],
        compiler_params=_params(("arbitrary",)),
    )(proj, proj, proj, attn_out, lse, dmixed, *after)


def _ret_log_gammas():
    return [math.log(1.0 - 2.0 ** (-5.0 - h)) for h in range(RET_HEADS)]


def _ret_decay(delta, log_gamma):
    dec = jnp.exp(delta.astype(F32) * log_gamma) * (1.0 / math.sqrt(RET_HEAD_DIM))
    return jnp.where(delta >= 0, dec, 0.0)


def _ret_fwd(proj, mixed, after=()):
    after = tuple(after)
    s = proj.shape[0]
    t = SEQ_TILE
    hd = RET_HEAD_DIM
    nh = RET_HEADS
    log_gammas = _ret_log_gammas()
    c0 = 3 * ATTN_WIDTH // hd

    def body(q_ref, k_ref, v_ref, g_ref, *rest):
        mix_ref, raw_ref, kb, vb, decay_tab = rest[1 + len(after):]
        h = pl.program_id(0)
        i = pl.program_id(1)

        @pl.when(i == 0)
        def _():
            kb[...] = k_ref[...].astype(BF16)
            vb[...] = v_ref[...].astype(BF16)
            _fill_ret_decay(decay_tab, _select_by_index(h, log_gammas))

        q = q_ref[...].astype(BF16)

        def step(j, acc):
            rows = pl.ds(pl.multiple_of(j * t, t), t)
            sc = lax.dot_general(q, kb[rows, :], _NT_DIMS, preferred_element_type=F32) * decay_tab[i - j]
            return acc + jnp.dot(sc.astype(BF16), vb[rows, :], preferred_element_type=F32)

        ret = lax.fori_loop(0, i + 1, step, jnp.zeros((t, hd), F32))
        raw_ref[...] = ret
        r = lax.rsqrt(jnp.mean(ret * ret, axis=-1, keepdims=True) + NORM_EPS)
        g = g_ref[...]
        mix_ref[...] = (g * _sigmoid(g) * (ret * r)).astype(BF16)

    return pl.pallas_call(
        body, name="ret_fwd", grid=(nh, s // t),
        in_specs=[pl.BlockSpec((t, hd), lambda h, i: (i, c0 + h)),
                  pl.BlockSpec((s, hd), lambda h, i: (0, c0 + nh + h)),
                  pl.BlockSpec((s, hd), lambda h, i: (0, c0 + 2 * nh + h)),
                  pl.BlockSpec((t, hd), lambda h, i: (i, c0 + 3 * nh + h))] + [ANY] * (1 + len(after)),
        out_specs=[pl.BlockSpec((None, t, hd), lambda h, i: (1, i, h)), pl.BlockSpec((t, hd), lambda h, i: (i, h))],
        out_shape=[jax.ShapeDtypeStruct(mixed.shape, BF16), jax.ShapeDtypeStruct((s, RET_WIDTH), F32)],
        input_output_aliases={4: 0},
        scratch_shapes=[pltpu.VMEM((s, hd), BF16), pltpu.VMEM((s, hd), BF16), pltpu.VMEM((s // t, t, t), F32)],
        compiler_params=_params(("arbitrary", "arbitrary")),
    )(proj, proj, proj, proj, mixed, *after)


def _ret_bwd(proj, ret_raw, dmixed, dsec, after=()):
    after = tuple(after)
    s = proj.shape[0]
    t = SEQ_TILE
    nt = s // t
    hd = RET_HEAD_DIM
    nh = RET_HEADS
    log_gammas = _ret_log_gammas()
    c0 = 3 * ATTN_WIDTH // hd
    mixed_blocks = ATTN_WIDTH // hd

    def body(q_ref, k_ref, v_ref, g_ref, raw_ref, dmix_ref, *rest):
        dsec_ref, qb, kb, vb, dretb, dq_acc, decay_tab = rest[1 + len(after):]
        h = pl.program_id(0)
        _fill_ret_decay(decay_tab, _select_by_index(h, log_gammas))
        qb[...] = q_ref[...].astype(BF16)
        kb[...] = k_ref[...].astype(BF16)
        vb[...] = v_ref[...].astype(BF16)
        ret = raw_ref[...]
        r = lax.rsqrt(jnp.mean(ret * ret, axis=-1, keepdims=True) + NORM_EPS)
        normed = ret * r
        g = g_ref[...]
        sg = _sigmoid(g)
        dout = dmix_ref[...]
        dsec_ref[3] = (dout * normed * sg * (1.0 + g * (1.0 - sg))).astype(BF16)
        dn = dout * g * sg
        dret = r * (dn - normed * jnp.mean(dn * normed, axis=-1, keepdims=True))
        dretb[...] = dret.astype(BF16)
        dq_acc[...] = jnp.zeros((s, hd), F32)

        def over_keys(j, _):
            krows = pl.ds(pl.multiple_of(j * t, t), t)
            kj = kb[krows, :]
            vj = vb[krows, :]

            def over_queries(i, carry):
                dk, dv = carry
                qrows = pl.ds(pl.multiple_of(i * t, t), t)
                qi = qb[qrows, :]
                doi = dretb[qrows, :]
                dec = decay_tab[i - j]
                a = (lax.dot_general(qi, kj, _NT_DIMS, preferred_element_type=F32) * dec).astype(BF16)
                da = (lax.dot_general(doi, vj, _NT_DIMS, preferred_element_type=F32) * dec).astype(BF16)
                dv = dv + lax.dot_general(a, doi, _TN_DIMS, preferred_element_type=F32)
                dk = dk + lax.dot_general(da, qi, _TN_DIMS, preferred_element_type=F32)
                dq_acc[qrows, :] += jnp.dot(da, kj, preferred_element_type=F32)
                return dk, dv

            zero = jnp.zeros((t, hd), F32)
            dk, dv = lax.fori_loop(j, nt, over_queries, (zero, zero))
            dsec_ref[1, krows, :] = dk.astype(BF16)
            dsec_ref[2, krows, :] = dv.astype(BF16)
            return 0

        lax.fori_loop(0, nt, over_keys, 0)
        dsec_ref[0] = dq_acc[...].astype(BF16)

    def col(off):
        return pl.BlockSpec((s, hd), lambda h: (0, off + h))

    return pl.pallas_call(
        body, name="ret_bwd", grid=(nh,),
        in_specs=[col(c0), col(c0 + nh), col(c0 + 2 * nh), col(c0 + 3 * nh), col(0), col(mixed_blocks)]
        + [ANY] * (1 + len(after)),
        out_specs=pl.BlockSpec((4, s, hd), lambda h: (1, 0, h)),
        out_shape=jax.ShapeDtypeStruct(dsec.shape, BF16),
        input_output_aliases={6: 0},
        scratch_shapes=[pltpu.VMEM((s, hd), BF16)] * 4 + [pltpu.VMEM((s, hd), F32)]
        + [pltpu.VMEM((nt, t, t), F32)],
        compiler_params=_params(("arbitrary",)),
    )(proj, proj, proj, proj, ret_raw, dmixed, dsec, *after)


_FLIPS = (2, 1, 3)


def _other_chips(x, y):
    return [(1 - x, y), (x, 1 - y), (1 - x, 1 - y)]


_HBM = pl.BlockSpec(memory_space=pltpu.HBM)
_SEM = pl.BlockSpec(memory_space=pltpu.SEMAPHORE)
_EFFECT = pltpu.SideEffectType.DATAFLOW_SIDE_EFFECTING


def _in_hbm(a):
    return pltpu.with_memory_space_constraint(a, pltpu.HBM)


def _weight_view(w, column_sharded):
    if column_sharded:
        return w.reshape(2, w.shape[0] // 2, w.shape[1])
    return w.reshape(N_CHIPS, 2, w.shape[0] // (2 * N_CHIPS), w.shape[1])


def _weight_unview(v):
    if v.ndim == 3:
        return v.reshape(2 * v.shape[1], v.shape[2])
    return v.reshape(N_CHIPS * 2 * v.shape[2], v.shape[3])


def _weight_region(buf, shard, half):
    if len(buf.shape) == 3:
        cols = buf.shape[2] // N_CHIPS
        return buf.at[half, :, pl.ds(shard * cols, cols)]
    return buf.at[shard, half]


def _remote(where, send_sem, recv_sem, to):
    return pltpu.make_async_remote_copy(src_ref=where, dst_ref=where, send_sem=send_sem, recv_sem=recv_sem,
                                        device_id=to, device_id_type=MESH)


def _for_my_shard(fn):
    x, y, _ = _place()
    for ss in range(N_CHIPS):
        pl.when(2 * x + y == ss)(functools.partial(fn, ss))


def _gather_start(views, name, after=()):
    n_w = len(views)
    after = tuple(after)

    def body(*refs):
        send_sems, recv_sems = refs[n_w + len(after):n_w + len(after) + 2]
        bufs = refs[n_w + len(after) + 2:]
        x, y, c = _place()

        def start(ss):
            for w in range(n_w):
                for j, chip in enumerate(_other_chips(x, y)):
                    _remote(_weight_region(bufs[w], ss, c), send_sems.at[3 * w + j], recv_sems.at[3 * w + j],
                            (*chip, c)).start()

        _for_my_shard(start)

    return pl.pallas_call(
        body, name=name,
        in_specs=[_HBM] * n_w + [ANY] * len(after), out_specs=[_SEM, _SEM] + [_HBM] * n_w,
        out_shape=[pltpu.SemaphoreType.DMA((3 * n_w,)), pltpu.SemaphoreType.DMA((3 * n_w,))]
        + [pltpu.HBM(v.shape, BF16) for v in views],
        input_output_aliases={w: 2 + w for w in range(n_w)},
        compiler_params=pltpu.CompilerParams(has_side_effects=_EFFECT),
    )(*[_in_hbm(v) for v in views], *after)


def _gather_forward(views, which, send_sems, recv_sems, after, name):
    n_w = len(views)

    def body(*refs):
        send_in, recv_in = refs[n_w:n_w + 2]
        fwd_send, fwd_recv = refs[n_w + 3:n_w + 5]
        bufs = refs[n_w + 5:]
        x, y, c = _place()
        sibling = (x, y, 1 - c)

        def forward(ss):
            for i, w in enumerate(which):
                for j in range(3):
                    landed = _weight_region(bufs[i], ss ^ _FLIPS[j], c)
                    _remote(landed, send_in.at[3 * w + j], recv_in.at[3 * w + j], sibling).wait_recv()
                    _remote(landed, fwd_send.at[3 * i + j], fwd_recv.at[3 * i + j], sibling).start()

        _for_my_shard(forward)
        for i, w in enumerate(which):
            for j in range(3):
                _remote(_weight_region(bufs[i], 0, 0), send_in.at[3 * w + j], recv_in.at[3 * w + j],
                        sibling).wait_send()

    return pl.pallas_call(
        body, name=name,
        in_specs=[_HBM] * n_w + [_SEM, _SEM, ANY], out_specs=[_SEM, _SEM] + [_HBM] * n_w,
        out_shape=[pltpu.SemaphoreType.DMA((3 * n_w,)), pltpu.SemaphoreType.DMA((3 * n_w,))]
        + [pltpu.HBM(v.shape, BF16) for v in views],
        input_output_aliases={w: 2 + w for w in range(n_w)},
        compiler_params=pltpu.CompilerParams(has_side_effects=_EFFECT),
    )(*views, send_sems, recv_sems, after)


def _gather_end(views, fwd_send, fwd_recv, after, name):
    n_w = len(views)

    def body(*refs):
        fwd_send_ref, fwd_recv_ref = refs[n_w:n_w + 2]
        bufs = refs[n_w + 3:]
        x, y, c = _place()
        for i in range(n_w):
            for j in range(3):
                cp = _remote(_weight_region(bufs[i], 0, 0), fwd_send_ref.at[3 * i + j], fwd_recv_ref.at[3 * i + j],
                             (x, y, 1 - c))
                cp.wait_recv()
                cp.wait_send()

    outs = pl.pallas_call(
        body, name=name,
        in_specs=[_HBM] * n_w + [_SEM, _SEM, ANY], out_specs=[_HBM] * n_w,
        out_shape=[pltpu.HBM(v.shape, BF16) for v in views],
        input_output_aliases={w: w for w in range(n_w)},
        compiler_params=pltpu.CompilerParams(has_side_effects=_EFFECT),
    )(*views, fwd_send, fwd_recv, after)
    return [_weight_unview(o) for o in outs]


def _comm_call(name, bufs, sem_pairs, after, n_new, fn):
    n, n_sem, after = len(bufs), 2 * len(sem_pairs), tuple(after)
    n_out_sem = 2 if n_new else 0

    def body(*refs):
        sems = refs[n:n + n_sem]
        outs = refs[n + n_sem + len(after):]
        new = outs[:n_out_sem] if n_new else (None, None)
        fn(outs[n_out_sem:], [(sems[2 * i], sems[2 * i + 1]) for i in range(len(sem_pairs))], *new)

    res = pl.pallas_call(
        body, name=name,
        in_specs=[_HBM] * n + [_SEM] * n_sem + [ANY] * len(after),
        out_specs=[_SEM] * n_out_sem + [_HBM] * n,
        out_shape=[pltpu.SemaphoreType.DMA((n_new,))] * n_out_sem + [pltpu.HBM(b.shape, b.dtype) for b in bufs],
        input_output_aliases={i: n_out_sem + i for i in range(n)},
        compiler_params=pltpu.CompilerParams(has_side_effects=_EFFECT),
    )(*bufs, *[s for pair in sem_pairs for s in pair], *after)
    return list(res[:n_out_sem]), list(res[n_out_sem:])


def _quarter(piece, q):
    rows = piece.shape[0] // 2
    return piece.at[pl.ds(q * rows, rows)]


def _gather_in_start(view, name):
    def fn(bufs, _, send, recv):
        x, y, c = _place()

        def go(ss):
            for j, chip in enumerate(_other_chips(x, y)[:2]):
                _remote(_weight_region(bufs[0], ss, c), send.at[j], recv.at[j], (*chip, c)).start()

        _for_my_shard(go)

    sems, (view,) = _comm_call(name, [_in_hbm(view)], [], (), 2, fn)
    return sems, view


def _gather_in_relay(view, started, after, name):
    def fn(bufs, pairs, send, recv):
        (send_in, recv_in), = pairs
        x, y, c = _place()
        chips = _other_chips(x, y)
        sibling = (x, y, 1 - c)

        def go(ss):
            landed = [_weight_region(bufs[0], ss ^ _FLIPS[j], c) for j in range(2)]
            for j in range(2):
                _remote(landed[j], send_in.at[j], recv_in.at[j], sibling).wait_recv()
            for j in range(2):
                _remote(_quarter(landed[j], j), send.at[j], recv.at[j], (*chips[1 - j], c)).start()
            for j in range(2):
                _remote(landed[j], send.at[2 + j], recv.at[2 + j], sibling).start()

        _for_my_shard(go)
        for j in range(2):
            _remote(_weight_region(bufs[0], 0, 0), send_in.at[j], recv_in.at[j], sibling).wait_send()

    sems, (view,) = _comm_call(name, [view], [started], after, 4, fn)
    return sems, view


def _gather_in_neighbours_end(view, relayed, after, name):
    def fn(bufs, pairs, *_):
        (send, recv), = pairs
        x, y, c = _place()
        for j in range(2):
            cp = _remote(_weight_region(bufs[0], 0, 0), send.at[2 + j], recv.at[2 + j], (x, y, 1 - c))
            cp.wait_recv()
            cp.wait_send()

    _, (view,) = _comm_call(name, [view], [relayed], after, 0, fn)
    return view


def _gather_in_diagonal(view, relayed, after, name):
    def fn(bufs, pairs, send, recv):
        (send_in, recv_in), = pairs
        x, y, c = _place()
        sibling = (x, y, 1 - c)
        any_quarter = _quarter(_weight_region(bufs[0], 0, 0), 0)
        for j in range(2):
            cp = _remote(any_quarter, send_in.at[j], recv_in.at[j], sibling)
            cp.wait_recv()
            cp.wait_send()

        def go(ss):
            _remote(_weight_region(bufs[0], ss ^ _FLIPS[2], c), send.at[0], recv.at[0], sibling).start()

        _for_my_shard(go)

    sems, (view,) = _comm_call(name, [view], [relayed], after, 1, fn)
    return sems, view


def _gather_in_diagonal_end(view, forwarded, after, name):
    def fn(bufs, pairs, *_):
        (send, recv), = pairs
        x, y, c = _place()
        cp = _remote(_weight_region(bufs[0], 0, 0), send.at[0], recv.at[0], (x, y, 1 - c))
        cp.wait_recv()
        cp.wait_send()

    _, (view,) = _comm_call(name, [view], [forwarded], after, 0, fn)
    return view


def _in_proj_shard(h1, wi, proj, shard_arr, name):
    s, d = h1.shape
    n = wi.shape[1]
    tn = 256
    blocks = n // (N_CHIPS * tn)
    given = [] if proj is None else [proj]

    def body(shard_ref, h_ref, w_ref, *rest):
        del shard_ref
        rest[-1][...] = jnp.dot(h_ref[...], w_ref[...], preferred_element_type=F32)

    grid_spec = pltpu.PrefetchScalarGridSpec(
        num_scalar_prefetch=1, grid=(blocks,),
        in_specs=[pl.BlockSpec((s, d), lambda j, shard_ref: (0, 0)),
                  pl.BlockSpec((d, tn), lambda j, shard_ref: (0, shard_ref[0] * blocks + j))] + [ANY] * len(given),
        out_specs=pl.BlockSpec((s, tn), lambda j, shard_ref: (0, shard_ref[0] * blocks + j)))
    return pl.pallas_call(
        body, name=name, grid_spec=grid_spec,
        out_shape=jax.ShapeDtypeStruct((s, n), F32),
        input_output_aliases={3: 0} if given else {},
        compiler_params=_params(("arbitrary",)),
    )(shard_arr, h1, wi, *given)


def _split_start(name, bufs, n_sems, copies):
    n = len(bufs)

    def body(*refs):
        send_sems, recv_sems = refs[n:n + 2]
        for cp in copies(refs[n + 2:], send_sems, recv_sems):
            cp.start()

    outs = pl.pallas_call(
        body, name=name,
        in_specs=[_HBM] * n, out_specs=[_SEM, _SEM] + [_HBM] * n,
        out_shape=[pltpu.SemaphoreType.DMA((n_sems,)), pltpu.SemaphoreType.DMA((n_sems,))]
        + [pltpu.HBM(b.shape, b.dtype) for b in bufs],
        input_output_aliases={i: 2 + i for i in range(n)},
        compiler_params=pltpu.CompilerParams(has_side_effects=_EFFECT),
    )(*[_in_hbm(b) for b in bufs])
    return outs[0], outs[1], list(outs[2:])


def _split_wait(name, bufs, send_sems, recv_sems, copies, after):
    n = len(bufs)

    def body(*refs):
        send_ref, recv_ref = refs[n:n + 2]
        for cp in copies(refs[n + 3:], send_ref, recv_ref):
            cp.wait()

    return list(pl.pallas_call(
        body, name=name,
        in_specs=[_HBM] * n + [_SEM, _SEM, ANY], out_specs=[_HBM] * n,
        out_shape=[pltpu.HBM(b.shape, b.dtype) for b in bufs],
        input_output_aliases={i: i for i in range(n)},
        compiler_params=pltpu.CompilerParams(has_side_effects=_EFFECT),
    )(*bufs, send_sems, recv_sems, after))


def _halves_copies(n_w):
    def copies(bufs, send_sems, recv_sems):
        x, y, c = _place()
        out = []
        for w in range(n_w):
            view, land = bufs[w], bufs[n_w + w]
            src = view.at[1 - c] if len(view.shape) == 3 else view.at[:, 1 - c]
            out.append(pltpu.make_async_remote_copy(
                src_ref=src, dst_ref=land, send_sem=send_sems.at[w], recv_sem=recv_sems.at[w],
                device_id=(x, y, 1 - c), device_id_type=MESH))
        return out
    return copies


def _pieces_copies(n_w):
    def copies(bufs, send_sems, recv_sems):
        x, y, c = _place()
        out = []
        for w in range(n_w):
            for j, (cx, cy) in enumerate(_other_chips(x, y)):
                out.append(pltpu.make_async_remote_copy(
                    src_ref=bufs[w].at[2 * cx + cy], dst_ref=bufs[n_w + w].at[j],
                    send_sem=send_sems.at[3 * w + j], recv_sem=recv_sems.at[3 * w + j],
                    device_id=(cx, cy, c), device_id_type=MESH))
        return out
    return copies


def _join_copies(n_w):
    def copies(bufs, send_sems, recv_sems):
        x, y, c = _place()
        return [pltpu.make_async_remote_copy(
            src_ref=bufs[w].at[c], dst_ref=bufs[w].at[c], send_sem=send_sems.at[w], recv_sem=recv_sems.at[w],
            device_id=(x, y, 1 - c), device_id_type=MESH) for w in range(n_w)]
    return copies


def _grad_view(g, column_sharded):
    return _weight_view(g, column_sharded)


def _halves_landing(view):
    shape = view.shape[1:] if view.ndim == 3 else (N_CHIPS,) + view.shape[2:]
    return lax.empty(shape, BF16)


def _halves_start(tag, grads, column_sharded):
    views = [_weight_view(g, cs) for g, cs in zip(grads, column_sharded)]
    n = len(views)
    return _split_start("halves_start_" + tag, views + [_halves_landing(v) for v in views], n, _halves_copies(n))


def _halves_wait(tag, state, after):
    send_sems, recv_sems, bufs = state
    n = len(bufs) // 2
    bufs = _split_wait("halves_wait_" + tag, bufs, send_sems, recv_sems, _halves_copies(n), after)
    return bufs[:n], bufs[n:]


def _pieces_start(tag, pieces):
    n = len(pieces)
    landing = [lax.empty((3,) + p.shape[1:], BF16) for p in pieces]
    return _split_start("pieces_start_" + tag, list(pieces) + landing, 3 * n, _pieces_copies(n))


def _pieces_wait(tag, state, after):
    send_sems, recv_sems, bufs = state
    n = len(bufs) // 2
    bufs = _split_wait("pieces_wait_" + tag, bufs, send_sems, recv_sems, _pieces_copies(n), after)
    return bufs[:n], bufs[n:]


def _join_start(tag, shards):
    n = len(shards)
    return _split_start("join_start_" + tag, list(shards), n, _join_copies(n))


def _join_wait(tag, state, after):
    send_sems, recv_sems, bufs = state
    bufs = _split_wait("join_wait_" + tag, bufs, send_sems, recv_sems, _join_copies(len(bufs)), after)
    return [b.reshape(2 * b.shape[1], b.shape[2]) for b in bufs]


def _chip_sum_col(g3, sib, c_arr, name):
    _, hk, n = g3.shape
    cols = n // N_CHIPS
    tr = _row_tile(hk, cols * 2, limit=4 * 1024 * 1024)

    def body(c_ref, g_ref, s_ref, o_ref):
        del c_ref
        o_ref[...] = (g_ref[...].astype(F32) + s_ref[...].astype(F32)).astype(BF16)

    grid_spec = pltpu.PrefetchScalarGridSpec(
        num_scalar_prefetch=1, grid=(N_CHIPS, hk // tr),
        in_specs=[pl.BlockSpec((None, tr, cols), lambda p, r, c_ref: (c_ref[0], r, p)),
                  pl.BlockSpec((tr, cols), lambda p, r, c_ref: (r, p))],
        out_specs=pl.BlockSpec((None, tr, cols), lambda p, r, c_ref: (p, r, 0)))
    return pl.pallas_call(
        body, name=name, grid_spec=grid_spec,
        out_shape=jax.ShapeDtypeStruct((N_CHIPS, hk, cols), BF16),
        compiler_params=_params(("parallel", "parallel")),
    )(c_arr, g3, sib)


def _chip_sum_row(g4, sib, c_arr, name):
    _, _, hr, n = g4.shape
    tr = _row_tile(hr, n * 2, limit=4 * 1024 * 1024)

    def body(c_ref, g_ref, s_ref, o_ref):
        del c_ref
        o_ref[...] = (g_ref[...].astype(F32) + s_ref[...].astype(F32)).astype(BF16)

    grid_spec = pltpu.PrefetchScalarGridSpec(
        num_scalar_prefetch=1, grid=(N_CHIPS, hr // tr),
        in_specs=[pl.BlockSpec((None, None, tr, n), lambda p, r, c_ref: (p, c_ref[0], r, 0)),
                  pl.BlockSpec((None, tr, n), lambda p, r, c_ref: (p, r, 0))],
        out_specs=pl.BlockSpec((None, tr, n), lambda p, r, c_ref: (p, r, 0)))
    return pl.pallas_call(
        body, name=name, grid_spec=grid_spec,
        out_shape=jax.ShapeDtypeStruct((N_CHIPS, hr, n), BF16),
        compiler_params=_params(("parallel", "parallel")),
    )(c_arr, g4, sib)


def _sum_pieces(pieces, received, place_arr, name):
    _, r, n = pieces.shape
    tr = _row_tile(r, n * 4, limit=4 * 1024 * 1024)

    def body(p_ref, own_ref, r0_ref, r1_ref, r2_ref, o_ref):
        del p_ref
        acc = own_ref[...].astype(F32) + r0_ref[...].astype(F32)
        acc = acc + r1_ref[...].astype(F32)
        o_ref[...] = acc + r2_ref[...].astype(F32)

    def recv_spec(j):
        return pl.BlockSpec((None, tr, n), lambda i, p_ref: (j, i, 0))

    grid_spec = pltpu.PrefetchScalarGridSpec(
        num_scalar_prefetch=1, grid=(r // tr,),
        in_specs=[pl.BlockSpec((None, tr, n), lambda i, p_ref: (p_ref[0], i, 0)),
                  recv_spec(0), recv_spec(1), recv_spec(2)],
        out_specs=pl.BlockSpec((None, tr, n), lambda i, p_ref: (p_ref[1], i, 0)))
    return pl.pallas_call(
        body, name=name, grid_spec=grid_spec,
        out_shape=jax.ShapeDtypeStruct((2, r, n), F32),
        compiler_params=_params(("parallel",)),
    )(place_arr, pieces, received, received, received)


def _norm_weights_step(parts, w, m, v, after=()):
    rows, d = parts.shape
    after = tuple(after)

    def body(p_ref, w_ref, m_ref, v_ref, *rest):
        g_ref, d_ref, mo_ref, vo_ref, gathered, send_sems, recv_sems = rest[len(after):]
        x, y, c = _place()
        me = 4 * x + 2 * y + c
        gathered[me] = p_ref[...]
        copies = []
        for k in range(1, N_DEV):
            peer = (x ^ ((k >> 2) & 1), y ^ ((k >> 1) & 1), c ^ (k & 1))
            copies.append(pltpu.make_async_remote_copy(
                src_ref=p_ref, dst_ref=gathered.at[me], send_sem=send_sems.at[k - 1],
                recv_sem=recv_sems.at[k - 1], device_id=peer, device_id_type=MESH))
        for cp in copies:
            cp.start()
        for cp in copies:
            cp.wait()
        g = gathered[0]
        for k in range(1, N_DEV):
            g = g + gathered[k]
        delta, m_new, v_new = _adamw_math(w_ref[...], g, m_ref[...], v_ref[...])
        g_ref[...] = g
        d_ref[...] = delta
        mo_ref[...] = m_new
        vo_ref[...] = v_new

    vmem = pl.BlockSpec(memory_space=pltpu.VMEM)
    shp = jax.ShapeDtypeStruct((rows, d), F32)
    return pl.pallas_call(
        body, name="norm_weights_step",
        in_specs=[vmem] * 4 + [ANY] * len(after), out_specs=[vmem] * 4, out_shape=[shp] * 4,
        scratch_shapes=[pltpu.VMEM((N_DEV, rows, d), F32), pltpu.SemaphoreType.DMA((N_DEV - 1,)),
                        pltpu.SemaphoreType.DMA((N_DEV - 1,))],
        compiler_params=pltpu.CompilerParams(has_side_effects=True),
    )(parts, w, m, v, *after)


def kernel(x, norm_mix_w, w_in, w_out, norm_ffn_w, w_gate, w_up, w_down, norm_final_w, loss_target, m_norm_mix_w, m_w_in, m_w_out, m_norm_ffn_w, m_w_gate, m_w_up, m_w_down, m_norm_final_w, v_norm_mix_w, v_w_in, v_w_out, v_norm_ffn_w, v_w_gate, v_w_up, v_w_down, v_norm_final_w):
    s, d = x.shape[1], x.shape[2]
    xs = x.reshape(s, d)
    target = loss_target.reshape(s, d)
    big = {"w_in": (w_in, m_w_in, v_w_in), "w_out": (w_out, m_w_out, v_w_out),
           "w_gate": (w_gate, m_w_gate, v_w_gate), "w_up": (w_up, m_w_up, v_w_up),
           "w_down": (w_down, m_w_down, v_w_down)}
    big = {k: tuple(a.reshape(a.shape[1:]) for a in t) for k, t in big.items()}
    col_names, row_names = ("w_in", "w_gate", "w_up"), ("w_out", "w_down")
    n_in = N_CHIPS * big["w_in"][0].shape[1]
    ffn = N_CHIPS * big["w_gate"][0].shape[1]
    mix = ATTN_WIDTH + RET_WIDTH
    c_arr = lax.axis_index("c").astype(I32).reshape(1)
    shard_arr = (2 * lax.axis_index("x") + lax.axis_index("y")).astype(I32).reshape(1)
    place_arr = jnp.concatenate([shard_arr, c_arr])

    def cast(k, after=()):
        return _weight_view(_cast_into_full(big[k][0], shard_arr, k in col_names, "cast_" + k, after), k in col_names)

    started_in, v_in = _gather_in_start(cast("w_in"), "gather_in_start")

    sec = ATTN_WIDTH

    def section(p, rows):
        return pl.BlockSpec((None, rows, sec), lambda i, j, kk: (p, i, 0))

    h1 = _rms_fwd(xs, norm_mix_w, "rms_mix_fwd")
    my_shard = shard_arr[0]
    shard_of = [jnp.bitwise_xor(my_shard, f).astype(I32).reshape(1) for f in (0,) + _FLIPS]
    proj = _in_proj_shard(h1, _weight_unview(v_in), None, shard_of[0], "in_proj_own")
    rest = ("w_out", "w_gate", "w_up", "w_down")
    rest_views = [cast(k, after=[proj]) for k in rest]
    relayed_in, v_in = _gather_in_relay(v_in, started_in, [rest_views[-1]], "gather_in_relay")
    send_sems, recv_sems, v_out, v_gate, v_up, v_down = _gather_start(rest_views, "gather_start_rest", after=[v_in])
    v_in = _gather_in_neighbours_end(v_in, relayed_in, [v_out], "gather_in_neighbours_end")
    proj = _in_proj_shard(h1, _weight_unview(v_in), proj, shard_of[1], "in_proj_x")
    proj = _in_proj_shard(h1, _weight_unview(v_in), proj, shard_of[2], "in_proj_y")
    forwarded_in, v_in = _gather_in_diagonal(v_in, relayed_in, [proj], "gather_in_diagonal")
    wi = _weight_unview(_gather_in_diagonal_end(v_in, forwarded_in, [proj], "gather_in_diagonal_end"))
    proj = _in_proj_shard(h1, wi, proj, shard_of[3], "in_proj_diagonal")
    fs_o, fr_o, v_out = _gather_forward([v_out], [0], send_sems, recv_sems, proj, "gather_forward_out")
    mixed, attn_o, lse = _attn_fwd(proj, after=[v_out])
    fs_g, fr_g, v_gate = _gather_forward([v_gate], [1], send_sems, recv_sems, attn_o, "gather_forward_gate")
    mixed, ret_raw = _ret_fwd(proj, mixed, after=[v_gate])
    wo, = _gather_end([v_out], fs_o, fr_o, ret_raw, "gather_end_out")
    x1, = _matmul("out_proj", "nn", [mixed, mixed], [wo, wo], [0, 0], s, d, sec, s, 512, sec, [xs], [F32],
                  _epi_residual, b_koff=[0, 1], a_specs=[section(0, s), section(1, s)])
    h2 = _rms_fwd(x1, norm_ffn_w, "rms_ffn_fwd")
    fs_u, fr_u, v_up = _gather_forward([v_up], [2], send_sems, recv_sems, h2, "gather_forward_up")
    wg, = _gather_end([v_gate], fs_g, fr_g, v_up, "gather_end_gate")
    wu, = _gather_end([v_up], fs_u, fr_u, wg, "gather_end_up")
    gate, up, act = _matmul("gate_up", "nn", [h2, h2], [wg, wu], [0, 1], s, ffn, d, s, 256, d, [],
                            [BF16, BF16, BF16], _epi_swiglu)
    fs, fr, v_down = _gather_forward([v_down], [3], send_sems, recv_sems, act, "gather_forward_down")
    wd, = _gather_end([v_down], fs, fr, act, "gather_end_down")
    x2, = _matmul("down_proj", "nn", [act], [wd], [0], s, d, ffn, s // 2, 256, ffn, [x1], [F32],
                  _epi_residual)
    loss_row, dx2, dx2b, dwf = _final_norm_loss(x2, norm_final_w.reshape(1, d), target, "final_norm_loss")

    names = col_names + row_names
    grads, new = {}, {}

    def chip_sums(tag_names, views, sibs):
        return [(_chip_sum_col if k in col_names else _chip_sum_row)(v, sb, c_arr, "chip_sum_" + k)
                for k, v, sb in zip(tag_names, views, sibs)]

    def piece_sums(tag_names, pieces, received):
        return [_sum_pieces(p, r, place_arr, "sum_pieces_" + k) for k, p, r in zip(tag_names, pieces, received)]

    def update(k):
        new[k] = _adamw(big[k][0], grads[k], big[k][1], big[k][2], "adamw_" + k)

    dgate, dup = _matmul("d_act", "nt", [dx2b], [wd], [0], s, ffn, d, s, 256, d, [gate, up],
                         [BF16, BF16], _epi_swiglu_bwd)
    g_wd, = _matmul("g_w_down", "tn", [act], [dx2b], [0], ffn, d, s, 512, d, s, [], [BF16], _epi_plain)
    halves_d = _halves_start("down", [g_wd], [False])
    dh2, = _matmul("d_h2", "nt", [dgate, dup], [wg, wu], [0, 0], s, d, ffn, s // 2, 256, ffn, [], [F32],
                   _epi_plain, after=halves_d[2][-1:], a_single_buffer=True)
    pieces_d = _pieces_start("down", chip_sums(["w_down"], *_halves_wait("down", halves_d, dh2)))
    g_wg, g_wu = _matmul("g_w_gate_up", "tn", [h2, h2], [dgate, dup], [0, 1], d, ffn, s, 1024, 512, s, [],
                         [BF16, BF16], _epi_two, after=pieces_d[2][-1:])
    halves_gu = _halves_start("gate_up", [g_wg, g_wu], [True, True])
    dx1, dx1b, dw_ffn = _rms_bwd(x1, norm_ffn_w, dh2, dx2, "rms_ffn_bwd", after=halves_gu[2][-1:])

    dmixed, = _matmul("d_mixed", "nt", [dx1b], [wo], [0], s, mix, d, s, 512, d, [], [F32], _epi_plain)
    pieces_gu = _pieces_start("gate_up", chip_sums(["w_gate", "w_up"], *_halves_wait("gate_up", halves_gu, dmixed)))
    per = sec // 512
    g_wo, = _matmul("g_w_out", "tn", [mixed], [dx1b], [0], mix, d, s, 512, d, s, [], [BF16], _epi_plain,
                    after=pieces_gu[2][-1:],
                    a_specs=[pl.BlockSpec((None, s, 512), lambda i, j, kk: (i // per, 0, i % per))])
    halves_o = _halves_start("out", [g_wo], [False])
    dsec = _attn_bwd(proj, attn_o, lse, dmixed, after=halves_o[2][-1:])
    pieces_o = _pieces_start("out", chip_sums(["w_out"], *_halves_wait("out", halves_o, dsec)))
    dsec = _ret_bwd(proj, ret_raw, dmixed, dsec, after=pieces_o[2][-1:])
    where = [0, 1, 2, 4, 5, 6, 7]
    n_sec = len(where)
    g_wi, = _matmul("g_w_in", "tn", [h1], [dsec], [0], d, n_in, s, 1024, sec, s, [], [BF16], _epi_plain,
                    b_specs=[pl.BlockSpec((None, s, sec), lambda i, j, kk: (j + (j >= 3).astype(I32), 0, 0))])
    halves_i = _halves_start("in", [g_wi], [True])
    dh1, = _matmul("d_h1", "nt", [dsec] * n_sec, [wi] * n_sec, [0] * n_sec, s, d, sec, s // 2, 256, sec, [], [F32],
                   _epi_plain, b_koff=list(range(n_sec)), after=halves_i[2][-1:],
                   a_specs=[section(p, s // 2) for p in where])
    pieces_i = _pieces_start("in", chip_sums(["w_in"], *_halves_wait("in", halves_i, dh1)))
    grad_x, _, dw_mix = _rms_bwd(xs, norm_mix_w, dh1, dx1, "rms_mix_bwd", after=pieces_i[2][-1:])

    def rows8(*vs):
        return jnp.concatenate([v.reshape(1, d) for v in vs] + [jnp.zeros((8 - len(vs), d), F32)], axis=0)

    join_d = _join_start("down", piece_sums(["w_down"], *_pieces_wait("down", pieces_d, grad_x)))
    join_gu = _join_start("gate_up", piece_sums(["w_gate", "w_up"], *_pieces_wait("gate_up", pieces_gu, join_d[2][0])))
    join_o = _join_start("out", piece_sums(["w_out"], *_pieces_wait("out", pieces_o, join_gu[2][0])))
    grads["w_down"], = _join_wait("down", join_d, join_o[2][0])
    update("w_down")
    grads["w_gate"], grads["w_up"] = _join_wait("gate_up", join_gu, new["w_down"][0])
    update("w_gate")
    update("w_up")
    grads["w_out"], = _join_wait("out", join_o, new["w_up"][0])
    update("w_out")
    join_i = _join_start("in", piece_sums(["w_in"], *_pieces_wait("in", pieces_i, new["w_out"][0])))
    ng, nd, nm, nv = _norm_weights_step(
        rows8(dw_mix, dw_ffn, dwf), rows8(norm_mix_w, norm_ffn_w, norm_final_w),
        rows8(m_norm_mix_w, m_norm_ffn_w, m_norm_final_w), rows8(v_norm_mix_w, v_norm_ffn_w, v_norm_final_w),
        after=join_i[2][:1])
    grads["w_in"], = _join_wait("in", join_i, ng)
    update("w_in")

    loss = lax.psum(loss_row[0, 0], ("x", "y", "c"))

    def pack(small, per_weight):
        lead = lambda a: a.reshape((1,) + a.shape)
        return (small[0:1], lead(per_weight["w_in"]), lead(per_weight["w_out"]), small[1:2],
                lead(per_weight["w_gate"]), lead(per_weight["w_up"]), lead(per_weight["w_down"]), small[2])

    return (loss, grad_x.reshape(1, s, d),
            *pack(ng, {k: new[k][3] for k in names}),
            *pack(nd, {k: new[k][0] for k in names}),
            *pack(nm, {k: new[k][1] for k in names}),
            *pack(nv, {k: new[k][2] for k in names}))
```

```python
import functools
import math

import jax
import jax.numpy as jnp
from jax import lax
from jax.experimental import pallas as pl
from jax.experimental.pallas import tpu as pltpu

F32 = jnp.float32
BF16 = jnp.bfloat16
I32 = jnp.int32
MESH = pl.DeviceIdType.MESH
ANY = pl.BlockSpec(memory_space=pl.ANY)

ATTN_HEADS = 8
ATTN_HEAD_DIM = 128
RET_HEADS = 4
RET_HEAD_DIM = 256
ATTN_WIDTH = ATTN_HEADS * ATTN_HEAD_DIM
RET_WIDTH = RET_HEADS * RET_HEAD_DIM
DILATED_PATTERNS = ((128, 1), (512, 4), (2048, 16))
NORM_EPS = 1e-6
ADAM_LR = 0.001
ADAM_B1 = 0.9
ADAM_B2 = 0.999
ADAM_EPS = 1e-08
ADAM_WD = 0.01
ADAM_STEP = 10

N_CHIPS = 4
N_DEV = 8
NEG_BIG = -1e30
SEQ_TILE = 512
ATTN_FWD_HEADS_PER_STEP = 2
ATTN_HEADS_PER_STEP = 1
VMEM_LIMIT_BYTES = 56 * 1024 * 1024


def _params(semantics=None, vmem=VMEM_LIMIT_BYTES):
    return pltpu.CompilerParams(dimension_semantics=semantics, vmem_limit_bytes=vmem)


def _row_tile(rows, row_bytes, limit=2 * 1024 * 1024, mult=16):
    best = None
    for t in range(mult, rows + 1, mult):
        if rows % t == 0 and t * row_bytes <= limit:
            best = t
    assert best is not None, (rows, row_bytes)
    return best


def _sigmoid(x):
    return 1.0 / (1.0 + jnp.exp(-x))


def _select_by_index(idx, values):
    out = jnp.float32(values[-1])
    for i in range(len(values) - 2, -1, -1):
        out = jnp.where(idx == i, jnp.float32(values[i]), out)
    return out


def _place():
    x, y, c = lax.axis_index("x"), lax.axis_index("y"), lax.axis_index("c")
    return x, y, c


def _cast_into_full(w, shard_arr, column_sharded, name, after=()):
    after = tuple(after)
    rows, cols = w.shape
    tr = _row_tile(rows, cols * 4)
    steps = rows // tr
    if column_sharded:
        out_shape, out_map = (rows, N_CHIPS * cols), (lambda i, s_ref: (i, s_ref[0]))
    else:
        out_shape, out_map = (N_CHIPS * rows, cols), (lambda i, s_ref: (s_ref[0] * steps + i, 0))

    def body(s_ref, w_ref, *rest):
        del s_ref
        rest[-1][...] = w_ref[...].astype(BF16)

    grid_spec = pltpu.PrefetchScalarGridSpec(
        num_scalar_prefetch=1, grid=(steps,),
        in_specs=[pl.BlockSpec((tr, cols), lambda i, s_ref: (i, 0))] + [ANY] * len(after),
        out_specs=pl.BlockSpec((tr, cols), out_map))
    return pl.pallas_call(
        body, name=name, grid_spec=grid_spec,
        out_shape=jax.ShapeDtypeStruct(out_shape, BF16),
        compiler_params=_params(("parallel",)),
    )(shard_arr, w, *after)


def _rms_fwd(x, w, name):
    rows, d = x.shape
    tr = 256

    def body(x_ref, w_ref, h_ref):
        xv = x_ref[...]
        r = lax.rsqrt(jnp.mean(xv * xv, axis=-1, keepdims=True) + NORM_EPS)
        h_ref[...] = (xv * r * w_ref[...]).astype(BF16)

    return pl.pallas_call(
        body, name=name, grid=(rows // tr,),
        in_specs=[pl.BlockSpec((tr, d), lambda i: (i, 0)), pl.BlockSpec((1, d), lambda i: (0, 0))],
        out_specs=pl.BlockSpec((tr, d), lambda i: (i, 0)),
        out_shape=jax.ShapeDtypeStruct((rows, d), BF16),
        compiler_params=_params(("parallel",)),
    )(x, w)


def _rms_bwd(x, w, dh, dres, name, after=()):
    rows, d = x.shape
    tr = 256
    after = tuple(after)

    def body(x_ref, w_ref, dh_ref, dres_ref, *rest):
        dx_ref, dxb_ref, dw_ref = rest[len(after):]
        xv = x_ref[...]
        r = lax.rsqrt(jnp.mean(xv * xv, axis=-1, keepdims=True) + NORM_EPS)
        xhat = xv * r
        dy = dh_ref[...]
        dxhat = dy * w_ref[...]
        dx = dres_ref[...] + r * (dxhat - xhat * jnp.mean(dxhat * xhat, axis=-1, keepdims=True))
        dx_ref[...] = dx
        dxb_ref[...] = dx.astype(BF16)
        part = jnp.sum(dy * xhat, axis=0, keepdims=True)

        @pl.when(pl.program_id(0) == 0)
        def _():
            dw_ref[...] = part

        @pl.when(pl.program_id(0) != 0)
        def _():
            dw_ref[...] += part

    row = pl.BlockSpec((tr, d), lambda i: (i, 0))
    vec = pl.BlockSpec((1, d), lambda i: (0, 0))
    return pl.pallas_call(
        body, name=name, grid=(rows // tr,),
        in_specs=[row, vec, row, row] + [ANY] * len(after),
        out_specs=[row, row, vec],
        out_shape=[jax.ShapeDtypeStruct((rows, d), F32), jax.ShapeDtypeStruct((rows, d), BF16),
                   jax.ShapeDtypeStruct((1, d), F32)],
        compiler_params=_params(("arbitrary",)),
    )(x, w, dh, dres, *after)


def _final_norm_loss(x2, w, target, name):
    rows, d = x2.shape
    tr = 256

    def body(x_ref, w_ref, t_ref, loss_ref, dx_ref, dxb_ref, dw_ref):
        xv = x_ref[...]
        wv = w_ref[...]
        r = lax.rsqrt(jnp.mean(xv * xv, axis=-1, keepdims=True) + NORM_EPS)
        xhat = xv * r
        err = xhat * wv - t_ref[...]
        part_loss = 0.5 * jnp.sum(jnp.mean(err * err, axis=-1, keepdims=True), axis=0, keepdims=True)
        dy = err * (1.0 / d)
        dxhat = dy * wv
        dx = r * (dxhat - xhat * jnp.mean(dxhat * xhat, axis=-1, keepdims=True))
        dx_ref[...] = dx
        dxb_ref[...] = dx.astype(BF16)
        part_dw = jnp.sum(dy * xhat, axis=0, keepdims=True)
        part_loss = jnp.broadcast_to(part_loss, (1, 128))

        @pl.when(pl.program_id(0) == 0)
        def _():
            dw_ref[...] = part_dw
            loss_ref[...] = part_loss

        @pl.when(pl.program_id(0) != 0)
        def _():
            dw_ref[...] += part_dw
            loss_ref[...] += part_loss

    row = pl.BlockSpec((tr, d), lambda i: (i, 0))
    vec = pl.BlockSpec((1, d), lambda i: (0, 0))
    return pl.pallas_call(
        body, name=name, grid=(rows // tr,),
        in_specs=[row, vec, row],
        out_specs=[pl.BlockSpec((1, 128), lambda i: (0, 0)), row, row, vec],
        out_shape=[jax.ShapeDtypeStruct((1, 128), F32), jax.ShapeDtypeStruct((rows, d), F32),
                   jax.ShapeDtypeStruct((rows, d), BF16), jax.ShapeDtypeStruct((1, d), F32)],
        compiler_params=_params(("arbitrary",)),
    )(x2, w, target)


def _adamw_math(w, g, m, v):
    m = ADAM_B1 * m + (1.0 - ADAM_B1) * g
    v = ADAM_B2 * v + (1.0 - ADAM_B2) * (g * g)
    m_hat = m / (1.0 - ADAM_B1 ** ADAM_STEP)
    v_hat = v / (1.0 - ADAM_B2 ** ADAM_STEP)
    delta = -ADAM_LR * (m_hat / (jnp.sqrt(v_hat) + ADAM_EPS) + ADAM_WD * w)
    return delta, m, v


def _adamw(w, g, m, v, name):
    rows, cols = w.shape
    tr = _row_tile(rows, cols * 4)

    def body(w_ref, g_ref, m_ref, v_ref, d_ref, mo_ref, vo_ref, go_ref):
        g = g_ref[...]
        delta, m_new, v_new = _adamw_math(w_ref[...], g, m_ref[...], v_ref[...])
        d_ref[...] = delta
        mo_ref[...] = m_new
        vo_ref[...] = v_new
        go_ref[...] = g

    blk = pl.BlockSpec((tr, cols), lambda i: (i, 0))
    shp = jax.ShapeDtypeStruct((rows, cols), F32)
    return pl.pallas_call(
        body, name=name, grid=(rows // tr,),
        in_specs=[blk] * 4, out_specs=[blk] * 4, out_shape=[shp] * 4,
        compiler_params=_params(("parallel",)),
    )(w, g, m, v)


_DOT_DIMS = {"nn": ((1,), (0,)), "nt": ((1,), (1,)), "tn": ((0,), (0,))}


def _matmul(name, mode, a_list, b_list, acc_of, m, n, k, tm, tn, tk, extras, out_dtypes, epilogue,
            a_koff=None, b_koff=None, after=(), a_specs=None, b_specs=None, a_single_buffer=False):
    after = tuple(after)
    assert m % tm == 0 and n % tn == 0 and k % tk == 0, (name, m, n, k, tm, tn, tk)
    nk = k // tk
    n_acc = max(acc_of) + 1
    n_pairs = len(a_list)
    a_koff = a_koff or [0] * n_pairs
    b_koff = b_koff or [0] * n_pairs
    dims = (_DOT_DIMS[mode], ((), ()))
    n_ext, n_out = len(extras), len(out_dtypes)

    def body(*refs):
        a_refs = refs[:n_pairs]
        b_refs = refs[n_pairs:2 * n_pairs]
        e_refs = refs[2 * n_pairs:2 * n_pairs + n_ext]
        first_out = 2 * n_pairs + n_ext + len(after)
        o_refs = refs[first_out:first_out + n_out]
        acc_refs = refs[first_out + n_out:]

        parts = [None] * n_acc
        for p in range(n_pairs):
            d = lax.dot_general(a_refs[p][...], b_refs[p][...], dims, preferred_element_type=F32)
            parts[acc_of[p]] = d if parts[acc_of[p]] is None else parts[acc_of[p]] + d

        def finish(accs):
            outs = epilogue(accs, [e[...] for e in e_refs])
            for o_ref, o in zip(o_refs, outs):
                o_ref[...] = o.astype(o_ref.dtype)

        if nk == 1:
            finish(parts)
        else:
            kk = pl.program_id(2)

            @pl.when(kk == 0)
            def _():
                for acc_ref, part in zip(acc_refs, parts):
                    acc_ref[...] = part

            @pl.when(kk != 0)
            def _():
                for acc_ref, part in zip(acc_refs, parts):
                    acc_ref[...] += part

            @pl.when(kk == nk - 1)
            def _():
                finish([acc_ref[...] for acc_ref in acc_refs])

    def a_spec(off):
        mode_a = pl.Buffered(1) if a_single_buffer else None
        if mode == "tn":
            return pl.BlockSpec((tk, tm), lambda i, j, kk: (kk + off, i), pipeline_mode=mode_a)
        return pl.BlockSpec((tm, tk), lambda i, j, kk: (i, kk + off), pipeline_mode=mode_a)

    def b_spec(off):
        if mode == "nt":
            return pl.BlockSpec((tn, tk), lambda i, j, kk: (j, kk + off))
        return pl.BlockSpec((tk, tn), lambda i, j, kk: (kk + off, j))

    tile = pl.BlockSpec((tm, tn), lambda i, j, kk: (i, j))
    scratch = [pltpu.VMEM((tm, tn), F32) for _ in range(n_acc)] if nk > 1 else []
    return pl.pallas_call(
        body, name=name, grid=(m // tm, n // tn, nk),
        in_specs=(a_specs or [a_spec(o) for o in a_koff]) + (b_specs or [b_spec(o) for o in b_koff])
        + [tile] * n_ext + [ANY] * len(after),
        out_specs=[tile] * n_out,
        out_shape=[jax.ShapeDtypeStruct((m, n), dt) for dt in out_dtypes],
        scratch_shapes=scratch,
        compiler_params=_params(("parallel", "parallel", "arbitrary")),
    )(*a_list, *b_list, *extras, *after)


def _epi_plain(accs, extras):
    return (accs[0],)


def _epi_residual(accs, extras):
    return (accs[0] + extras[0],)


def _epi_two(accs, extras):
    return accs[0], accs[1]


def _epi_swiglu(accs, extras):
    g, u = accs
    return g, u, g * _sigmoid(g) * u


def _epi_swiglu_bwd(accs, extras):
    da = accs[0]
    g, u = (e.astype(F32) for e in extras)
    sg = _sigmoid(g)
    dg = da * u * sg * (1.0 + g * (1.0 - sg))
    du = da * g * sg
    return dg, du


_NT_DIMS = (((1,), (1,)), ((), ()))
_TN_DIMS = (((0,), (0,)), ((), ()))


def _tile_delta(tq, tk):
    return lax.broadcasted_iota(I32, (tq, tk), 0) - lax.broadcasted_iota(I32, (tq, tk), 1)


def _attn_log_count(delta):
    count = jnp.zeros(delta.shape, I32)
    for window, dilation in DILATED_PATTERNS:
        hit = ((delta & (dilation - 1)) == 0) & (delta <= window)
        count = count + jnp.where(hit, 1, 0)
    valid = (delta >= 0) & (count > 0)
    logm = jnp.where(count == 3, math.log(3.0), jnp.where(count == 2, math.log(2.0), 0.0))
    return jnp.where(valid, logm, NEG_BIG)


def _fill_attn_log_count(tab_ref):
    nb, t, _ = tab_ref.shape
    base = _tile_delta(t, t)
    for b in range(nb):
        tab_ref[b] = _attn_log_count(base + b * t)


def _fill_attn_bias(tab_ref, log_count_ref, slope):
    nb, t, _ = tab_ref.shape
    dist = _tile_delta(t, t).astype(F32)
    for b in range(nb):
        tab_ref[b] = log_count_ref[b] - slope * (dist + float(b * t))


def _fill_ret_decay(tab_ref, log_gamma):
    nb, t, _ = tab_ref.shape
    base = _tile_delta(t, t)
    for b in range(nb):
        tab_ref[b] = _ret_decay(base + b * t, log_gamma)


def _alibi_slopes():
    return [2.0 ** (-8.0 * (h + 1) / ATTN_HEADS) for h in range(ATTN_HEADS)]


def _attn_fwd(proj, after=()):
    s = proj.shape[0]
    t = SEQ_TILE
    hd = ATTN_HEAD_DIM
    hp = ATTN_FWD_HEADS_PER_STEP
    ng = ATTN_HEADS // hp
    w = hp * hd
    scale = 1.0 / math.sqrt(hd)
    slopes = _alibi_slopes()

    def body(q_ref, k_ref, v_ref, *rest):
        mix_ref, o_ref, lse_ref, kb, vb, bias_tab, log_count_tab = rest[len(after):]
        g = pl.program_id(0)
        i = pl.program_id(1)

        @pl.when((g == 0) & (i == 0))
        def _():
            _fill_attn_log_count(log_count_tab)

        @pl.when(i == 0)
        def _():
            kb[...] = k_ref[...].astype(BF16)
            vb[...] = v_ref[...].astype(BF16)
            for u in range(hp):
                _fill_attn_bias(bias_tab.at[u], log_count_tab, _select_by_index(g * hp + u, slopes))

        qs = [q_ref[:, u * hd:(u + 1) * hd].astype(BF16) for u in range(hp)]

        def step(j, carry):
            rows = pl.ds(pl.multiple_of(j * t, t), t)
            out = []
            for u in range(hp):
                m_i, l_i, acc = carry[u]
                lanes = slice(u * hd, (u + 1) * hd)
                sc = lax.dot_general(qs[u], kb[rows, lanes], _NT_DIMS, preferred_element_type=F32) * scale
                sc = sc + bias_tab[u, i - j]
                m_new = jnp.maximum(m_i, jnp.max(sc, axis=-1, keepdims=True))
                p = jnp.exp(sc - m_new)
                alpha = jnp.exp(m_i - m_new)
                l_new = alpha * l_i + jnp.sum(p, axis=-1, keepdims=True)
                acc = alpha * acc + jnp.dot(p.astype(BF16), vb[rows, lanes], preferred_element_type=F32)
                out.append((m_new, l_new, acc))
            return tuple(out)

        init = (jnp.full((t, 1), NEG_BIG, F32), jnp.zeros((t, 1), F32), jnp.zeros((t, hd), F32))
        final = lax.fori_loop(0, i + 1, step, (init,) * hp)
        for u in range(hp):
            m_i, l_i, acc = final[u]
            lanes = slice(u * hd, (u + 1) * hd)
            out = acc / l_i
            o_ref[:, lanes] = out
            mix_ref[:, lanes] = out.astype(BF16)
            lse_ref[:, lanes] = jnp.broadcast_to(m_i + jnp.log(l_i), (t, hd))

    return pl.pallas_call(
        body, name="attn_fwd", grid=(ng, s // t),
        in_specs=[pl.BlockSpec((t, w), lambda g, i: (i, g)),
                  pl.BlockSpec((s, w), lambda g, i: (0, ng + g)),
                  pl.BlockSpec((s, w), lambda g, i: (0, 2 * ng + g))] + [ANY] * len(after),
        out_specs=[pl.BlockSpec((None, t, w), lambda g, i: (0, i, g))] + [pl.BlockSpec((t, w), lambda g, i: (i, g))] * 2,
        out_shape=[jax.ShapeDtypeStruct((2, s, ATTN_WIDTH), BF16),
                   jax.ShapeDtypeStruct((s, ATTN_WIDTH), F32),
                   jax.ShapeDtypeStruct((s, ATTN_WIDTH), F32)],
        scratch_shapes=[pltpu.VMEM((s, w), BF16), pltpu.VMEM((s, w), BF16), pltpu.VMEM((hp, s // t, t, t), F32),
                        pltpu.VMEM((s // t, t, t), F32)],
        compiler_params=_params(("arbitrary", "arbitrary")),
    )(proj, proj, proj, *after)


def _attn_bwd(proj, attn_out, lse, dmixed, after=()):
    after = tuple(after)
    s = proj.shape[0]
    t = SEQ_TILE
    nt = s // t
    hd = ATTN_HEAD_DIM
    hp = ATTN_HEADS_PER_STEP
    ng = ATTN_HEADS // hp
    w = hp * hd
    scale = 1.0 / math.sqrt(hd)
    slopes = _alibi_slopes()

    def body(q_ref, k_ref, v_ref, o_ref, lse_ref, do_ref, *rest):
        dsec_ref, qb, kb, vb, dob, dsum, dq_acc, bias_tab, log_count_tab = rest[len(after):]
        g = pl.program_id(0)

        @pl.when(g == 0)
        def _():
            _fill_attn_log_count(log_count_tab)

        qb[...] = q_ref[...].astype(BF16)
        kb[...] = k_ref[...].astype(BF16)
        vb[...] = v_ref[...].astype(BF16)
        dob[...] = do_ref[...].astype(BF16)
        for u in range(hp):
            lanes = slice(u * hd, (u + 1) * hd)
            _fill_attn_bias(bias_tab.at[u], log_count_tab, _select_by_index(g * hp + u, slopes))
            rowsum = jnp.sum(do_ref[:, lanes] * o_ref[:, lanes], axis=-1, keepdims=True)
            dsum[:, lanes] = jnp.broadcast_to(rowsum, (s, hd))
        dq_acc[...] = jnp.zeros((s, w), F32)

        def over_keys(j, _):
            krows = pl.ds(pl.multiple_of(j * t, t), t)

            def over_queries(i, carry):
                qrows = pl.ds(pl.multiple_of(i * t, t), t)
                out = []
                for u in range(hp):
                    dk, dv = carry[u]
                    lanes = slice(u * hd, (u + 1) * hd)
                    qi, doi = qb[qrows, lanes], dob[qrows, lanes]
                    kj, vj = kb[krows, lanes], vb[krows, lanes]
                    lse_i = lse_ref[qrows, lanes][:, :1]
                    dsum_i = dsum[qrows, lanes][:, :1]
                    sc = lax.dot_general(qi, kj, _NT_DIMS, preferred_element_type=F32) * scale
                    p = jnp.exp(sc + bias_tab[u, i - j] - lse_i)
                    dp = lax.dot_general(doi, vj, _NT_DIMS, preferred_element_type=F32)
                    ds = (p * (dp - dsum_i)).astype(BF16)
                    dv = dv + lax.dot_general(p.astype(BF16), doi, _TN_DIMS, preferred_element_type=F32)
                    dk = dk + lax.dot_general(ds, qi, _TN_DIMS, preferred_element_type=F32)
                    dq_acc[qrows, lanes] += jnp.dot(ds, kj, preferred_element_type=F32)
                    out.append((dk, dv))
                return tuple(out)

            zero = jnp.zeros((t, hd), F32)
            final = lax.fori_loop(j, nt, over_queries, ((zero, zero),) * hp)
            for u in range(hp):
                lanes = slice(u * hd, (u + 1) * hd)
                dsec_ref[1, krows, lanes] = (final[u][0] * scale).astype(BF16)
                dsec_ref[2, krows, lanes] = final[u][1].astype(BF16)
            return 0

        lax.fori_loop(0, nt, over_keys, 0)
        dsec_ref[0] = (dq_acc[...] * scale).astype(BF16)

    def col(off):
        return pl.BlockSpec((s, w), lambda g: (0, off + g))

    return pl.pallas_call(
        body, name="attn_bwd", grid=(ng,),
        in_specs=[col(0), col(ng), col(2 * ng), col(0), col(0), col(0)] + [ANY] * len(after),
        out_specs=pl.BlockSpec((4, s, w), lambda g: (0, 0, g)),
        out_shape=jax.ShapeDtypeStruct((8, s, ATTN_WIDTH), BF16),
        scratch_shapes=[pltpu.VMEM((s, w), BF16)] * 4 + [pltpu.VMEM((s, w), F32)] * 2
        + [pltpu.VMEM((hp, nt, t, t), F32), pltpu.VMEM((nt, t, t), F32)],
        compiler_params=_params(("arbitrary",)),
    )(proj, proj, proj, attn_out, lse, dmixed, *after)


def _ret_log_gammas():
    return [math.log(1.0 - 2.0 ** (-5.0 - h)) for h in range(RET_HEADS)]


def _ret_decay(delta, log_gamma):
    dec = jnp.exp(delta.astype(F32) * log_gamma) * (1.0 / math.sqrt(RET_HEAD_DIM))
    return jnp.where(delta >= 0, dec, 0.0)


def _ret_fwd(proj, mixed, after=()):
    after = tuple(after)
    s = proj.shape[0]
    t = SEQ_TILE
    hd = RET_HEAD_DIM
    nh = RET_HEADS
    log_gammas = _ret_log_gammas()
    c0 = 3 * ATTN_WIDTH // hd

    def body(q_ref, k_ref, v_ref, g_ref, *rest):
        mix_ref, raw_ref, kb, vb, decay_tab = rest[1 + len(after):]
        h = pl.program_id(0)
        i = pl.program_id(1)

        @pl.when(i == 0)
        def _():
            kb[...] = k_ref[...].astype(BF16)
            vb[...] = v_ref[...].astype(BF16)
            _fill_ret_decay(decay_tab, _select_by_index(h, log_gammas))

        q = q_ref[...].astype(BF16)

        def step(j, acc):
            rows = pl.ds(pl.multiple_of(j * t, t), t)
            sc = lax.dot_general(q, kb[rows, :], _NT_DIMS, preferred_element_type=F32) * decay_tab[i - j]
            return acc + jnp.dot(sc.astype(BF16), vb[rows, :], preferred_element_type=F32)

        ret = lax.fori_loop(0, i + 1, step, jnp.zeros((t, hd), F32))
        raw_ref[...] = ret
        r = lax.rsqrt(jnp.mean(ret * ret, axis=-1, keepdims=True) + NORM_EPS)
        g = g_ref[...]
        mix_ref[...] = (g * _sigmoid(g) * (ret * r)).astype(BF16)

    return pl.pallas_call(
        body, name="ret_fwd", grid=(nh, s // t),
        in_specs=[pl.BlockSpec((t, hd), lambda h, i: (i, c0 + h)),
                  pl.BlockSpec((s, hd), lambda h, i: (0, c0 + nh + h)),
                  pl.BlockSpec((s, hd), lambda h, i: (0, c0 + 2 * nh + h)),
                  pl.BlockSpec((t, hd), lambda h, i: (i, c0 + 3 * nh + h))] + [ANY] * (1 + len(after)),
        out_specs=[pl.BlockSpec((None, t, hd), lambda h, i: (1, i, h)), pl.BlockSpec((t, hd), lambda h, i: (i, h))],
        out_shape=[jax.ShapeDtypeStruct(mixed.shape, BF16), jax.ShapeDtypeStruct((s, RET_WIDTH), F32)],
        input_output_aliases={4: 0},
        scratch_shapes=[pltpu.VMEM((s, hd), BF16), pltpu.VMEM((s, hd), BF16), pltpu.VMEM((s // t, t, t), F32)],
        compiler_params=_params(("arbitrary", "arbitrary")),
    )(proj, proj, proj, proj, mixed, *after)


def _ret_bwd(proj, ret_raw, dmixed, dsec, after=()):
    after = tuple(after)
    s = proj.shape[0]
    t = SEQ_TILE
    nt = s // t
    hd = RET_HEAD_DIM
    nh = RET_HEADS
    log_gammas = _ret_log_gammas()
    c0 = 3 * ATTN_WIDTH // hd
    mixed_blocks = ATTN_WIDTH // hd

    def body(q_ref, k_ref, v_ref, g_ref, raw_ref, dmix_ref, *rest):
        dsec_ref, qb, kb, vb, dretb, dq_acc, decay_tab = rest[1 + len(after):]
        h = pl.program_id(0)
        _fill_ret_decay(decay_tab, _select_by_index(h, log_gammas))
        qb[...] = q_ref[...].astype(BF16)
        kb[...] = k_ref[...].astype(BF16)
        vb[...] = v_ref[...].astype(BF16)
        ret = raw_ref[...]
        r = lax.rsqrt(jnp.mean(ret * ret, axis=-1, keepdims=True) + NORM_EPS)
        normed = ret * r
        g = g_ref[...]
        sg = _sigmoid(g)
        dout = dmix_ref[...]
        dsec_ref[3] = (dout * normed * sg * (1.0 + g * (1.0 - sg))).astype(BF16)
        dn = dout * g * sg
        dret = r * (dn - normed * jnp.mean(dn * normed, axis=-1, keepdims=True))
        dretb[...] = dret.astype(BF16)
        dq_acc[...] = jnp.zeros((s, hd), F32)

        def over_keys(j, _):
            krows = pl.ds(pl.multiple_of(j * t, t), t)
            kj = kb[krows, :]
            vj = vb[krows, :]

            def over_queries(i, carry):
                dk, dv = carry
                qrows = pl.ds(pl.multiple_of(i * t, t), t)
                qi = qb[qrows, :]
                doi = dretb[qrows, :]
                dec = decay_tab[i - j]
                a = (lax.dot_general(qi, kj, _NT_DIMS, preferred_element_type=F32) * dec).astype(BF16)
                da = (lax.dot_general(doi, vj, _NT_DIMS, preferred_element_type=F32) * dec).astype(BF16)
                dv = dv + lax.dot_general(a, doi, _TN_DIMS, preferred_element_type=F32)
                dk = dk + lax.dot_general(da, qi, _TN_DIMS, preferred_element_type=F32)
                dq_acc[qrows, :] += jnp.dot(da, kj, preferred_element_type=F32)
                return dk, dv

            zero = jnp.zeros((t, hd), F32)
            dk, dv = lax.fori_loop(j, nt, over_queries, (zero, zero))
            dsec_ref[1, krows, :] = dk.astype(BF16)
            dsec_ref[2, krows, :] = dv.astype(BF16)
            return 0

        lax.fori_loop(0, nt, over_keys, 0)
        dsec_ref[0] = dq_acc[...].astype(BF16)

    def col(off):
        return pl.BlockSpec((s, hd), lambda h: (0, off + h))

    return pl.pallas_call(
        body, name="ret_bwd", grid=(nh,),
        in_specs=[col(c0), col(c0 + nh), col(c0 + 2 * nh), col(c0 + 3 * nh), col(0), col(mixed_blocks)]
        + [ANY] * (1 + len(after)),
        out_specs=pl.BlockSpec((4, s, hd), lambda h: (1, 0, h)),
        out_shape=jax.ShapeDtypeStruct(dsec.shape, BF16),
        input_output_aliases={6: 0},
        scratch_shapes=[pltpu.VMEM((s, hd), BF16)] * 4 + [pltpu.VMEM((s, hd), F32)]
        + [pltpu.VMEM((nt, t, t), F32)],
        compiler_params=_params(("arbitrary",)),
    )(proj, proj, proj, proj, ret_raw, dmixed, dsec, *after)


_FLIPS = (2, 1, 3)


def _other_chips(x, y):
    return [(1 - x, y), (x, 1 - y), (1 - x, 1 - y)]


_HBM = pl.BlockSpec(memory_space=pltpu.HBM)
_SEM = pl.BlockSpec(memory_space=pltpu.SEMAPHORE)
_EFFECT = pltpu.SideEffectType.DATAFLOW_SIDE_EFFECTING


def _in_hbm(a):
    return pltpu.with_memory_space_constraint(a, pltpu.HBM)


def _weight_view(w, column_sharded):
    if column_sharded:
        return w.reshape(2, w.shape[0] // 2, w.shape[1])
    return w.reshape(N_CHIPS, 2, w.shape[0] // (2 * N_CHIPS), w.shape[1])


def _weight_unview(v):
    if v.ndim == 3:
        return v.reshape(2 * v.shape[1], v.shape[2])
    return v.reshape(N_CHIPS * 2 * v.shape[2], v.shape[3])


def _weight_region(buf, shard, half):
    if len(buf.shape) == 3:
        cols = buf.shape[2] // N_CHIPS
        return buf.at[half, :, pl.ds(shard * cols, cols)]
    return buf.at[shard, half]


def _remote(where, send_sem, recv_sem, to):
    return pltpu.make_async_remote_copy(src_ref=where, dst_ref=where, send_sem=send_sem, recv_sem=recv_sem,
                                        device_id=to, device_id_type=MESH)


def _for_my_shard(fn):
    x, y, _ = _place()
    for ss in range(N_CHIPS):
        pl.when(2 * x + y == ss)(functools.partial(fn, ss))


def _gather_start(views, name, after=()):
    n_w = len(views)
    after = tuple(after)

    def body(*refs):
        send_sems, recv_sems = refs[n_w + len(after):n_w + len(after) + 2]
        bufs = refs[n_w + len(after) + 2:]
        x, y, c = _place()

        def start(ss):
            for w in range(n_w):
                for j, chip in enumerate(_other_chips(x, y)):
                    _remote(_weight_region(bufs[w], ss, c), send_sems.at[3 * w + j], recv_sems.at[3 * w + j],
                            (*chip, c)).start()

        _for_my_shard(start)

    return pl.pallas_call(
        body, name=name,
        in_specs=[_HBM] * n_w + [ANY] * len(after), out_specs=[_SEM, _SEM] + [_HBM] * n_w,
        out_shape=[pltpu.SemaphoreType.DMA((3 * n_w,)), pltpu.SemaphoreType.DMA((3 * n_w,))]
        + [pltpu.HBM(v.shape, BF16) for v in views],
        input_output_aliases={w: 2 + w for w in range(n_w)},
        compiler_params=pltpu.CompilerParams(has_side_effects=_EFFECT),
    )(*[_in_hbm(v) for v in views], *after)


def _gather_forward(views, which, send_sems, recv_sems, after, name):
    n_w = len(views)

    def body(*refs):
        send_in, recv_in = refs[n_w:n_w + 2]
        fwd_send, fwd_recv = refs[n_w + 3:n_w + 5]
        bufs = refs[n_w + 5:]
        x, y, c = _place()
        sibling = (x, y, 1 - c)

        def forward(ss):
            for i, w in enumerate(which):
                for j in range(3):
                    landed = _weight_region(bufs[i], ss ^ _FLIPS[j], c)
                    _remote(landed, send_in.at[3 * w + j], recv_in.at[3 * w + j], sibling).wait_recv()
                    _remote(landed, fwd_send.at[3 * i + j], fwd_recv.at[3 * i + j], sibling).start()

        _for_my_shard(forward)
        for i, w in enumerate(which):
            for j in range(3):
                _remote(_weight_region(bufs[i], 0, 0), send_in.at[3 * w + j], recv_in.at[3 * w + j],
                        sibling).wait_send()

    return pl.pallas_call(
        body, name=name,
        in_specs=[_HBM] * n_w + [_SEM, _SEM, ANY], out_specs=[_SEM, _SEM] + [_HBM] * n_w,
        out_shape=[pltpu.SemaphoreType.DMA((3 * n_w,)), pltpu.SemaphoreType.DMA((3 * n_w,))]
        + [pltpu.HBM(v.shape, BF16) for v in views],
        input_output_aliases={w: 2 + w for w in range(n_w)},
        compiler_params=pltpu.CompilerParams(has_side_effects=_EFFECT),
    )(*views, send_sems, recv_sems, after)


def _gather_end(views, fwd_send, fwd_recv, after, name):
    n_w = len(views)

    def body(*refs):
        fwd_send_ref, fwd_recv_ref = refs[n_w:n_w + 2]
        bufs = refs[n_w + 3:]
        x, y, c = _place()
        for i in range(n_w):
            for j in range(3):
                cp = _remote(_weight_region(bufs[i], 0, 0), fwd_send_ref.at[3 * i + j], fwd_recv_ref.at[3 * i + j],
                             (x, y, 1 - c))
                cp.wait_recv()
                cp.wait_send()

    outs = pl.pallas_call(
        body, name=name,
        in_specs=[_HBM] * n_w + [_SEM, _SEM, ANY], out_specs=[_HBM] * n_w,
        out_shape=[pltpu.HBM(v.shape, BF16) for v in views],
        input_output_aliases={w: w for w in range(n_w)},
        compiler_params=pltpu.CompilerParams(has_side_effects=_EFFECT),
    )(*views, fwd_send, fwd_recv, after)
    return [_weight_unview(o) for o in outs]


def _comm_call(name, bufs, sem_pairs, after, n_new, fn):
    n, n_sem, after = len(bufs), 2 * len(sem_pairs), tuple(after)
    n_out_sem = 2 if n_new else 0

    def body(*refs):
        sems = refs[n:n + n_sem]
        outs = refs[n + n_sem + len(after):]
        new = outs[:n_out_sem] if n_new else (None, None)
        fn(outs[n_out_sem:], [(sems[2 * i], sems[2 * i + 1]) for i in range(len(sem_pairs))], *new)

    res = pl.pallas_call(
        body, name=name,
        in_specs=[_HBM] * n + [_SEM] * n_sem + [ANY] * len(after),
        out_specs=[_SEM] * n_out_sem + [_HBM] * n,
        out_shape=[pltpu.SemaphoreType.DMA((n_new,))] * n_out_sem + [pltpu.HBM(b.shape, b.dtype) for b in bufs],
        input_output_aliases={i: n_out_sem + i for i in range(n)},
        compiler_params=pltpu.CompilerParams(has_side_effects=_EFFECT),
    )(*bufs, *[s for pair in sem_pairs for s in pair], *after)
    return list(res[:n_out_sem]), list(res[n_out_sem:])


def _quarter(piece, q):
    rows = piece.shape[0] // 2
    return piece.at[pl.ds(q * rows, rows)]


def _gather_in_start(view, name):
    def fn(bufs, _, send, recv):
        x, y, c = _place()

        def go(ss):
            for j, chip in enumerate(_other_chips(x, y)[:2]):
                _remote(_weight_region(bufs[0], ss, c), send.at[j], recv.at[j], (*chip, c)).start()

        _for_my_shard(go)

    sems, (view,) = _comm_call(name, [_in_hbm(view)], [], (), 2, fn)
    return sems, view


def _gather_in_relay(view, started, after, name):
    def fn(bufs, pairs, send, recv):
        (send_in, recv_in), = pairs
        x, y, c = _place()
        chips = _other_chips(x, y)
        sibling = (x, y, 1 - c)

        def go(ss):
            landed = [_weight_region(bufs[0], ss ^ _FLIPS[j], c) for j in range(2)]
            for j in range(2):
                _remote(landed[j], send_in.at[j], recv_in.at[j], sibling).wait_recv()
            for j in range(2):
                _remote(_quarter(landed[j], j), send.at[j], recv.at[j], (*chips[1 - j], c)).start()
            for j in range(2):
                _remote(landed[j], send.at[2 + j], recv.at[2 + j], sibling).start()

        _for_my_shard(go)
        for j in range(2):
            _remote(_weight_region(bufs[0], 0, 0), send_in.at[j], recv_in.at[j], sibling).wait_send()

    sems, (view,) = _comm_call(name, [view], [started], after, 4, fn)
    return sems, view


def _gather_in_neighbours_end(view, relayed, after, name):
    def fn(bufs, pairs, *_):
        (send, recv), = pairs
        x, y, c = _place()
        for j in range(2):
            cp = _remote(_weight_region(bufs[0], 0, 0), send.at[2 + j], recv.at[2 + j], (x, y, 1 - c))
            cp.wait_recv()
            cp.wait_send()

    _, (view,) = _comm_call(name, [view], [relayed], after, 0, fn)
    return view


def _gather_in_diagonal(view, relayed, after, name):
    def fn(bufs, pairs, send, recv):
        (send_in, recv_in), = pairs
        x, y, c = _place()
        sibling = (x, y, 1 - c)
        any_quarter = _quarter(_weight_region(bufs[0], 0, 0), 0)
        for j in range(2):
            cp = _remote(any_quarter, send_in.at[j], recv_in.at[j], sibling)
            cp.wait_recv()
            cp.wait_send()

        def go(ss):
            _remote(_weight_region(bufs[0], ss ^ _FLIPS[2], c), send.at[0], recv.at[0], sibling).start()

        _for_my_shard(go)

    sems, (view,) = _comm_call(name, [view], [relayed], after, 1, fn)
    return sems, view


def _gather_in_diagonal_end(view, forwarded, after, name):
    def fn(bufs, pairs, *_):
        (send, recv), = pairs
        x, y, c = _place()
        cp = _remote(_weight_region(bufs[0], 0, 0), send.at[0], recv.at[0], (x, y, 1 - c))
        cp.wait_recv()
        cp.wait_send()

    _, (view,) = _comm_call(name, [view], [forwarded], after, 0, fn)
    return view


def _in_proj_shard(h1, wi, proj, shard_arr, name):
    s, d = h1.shape
    n = wi.shape[1]
    tn = 256
    blocks = n // (N_CHIPS * tn)
    given = [] if proj is None else [proj]

    def body(shard_ref, h_ref, w_ref, *rest):
        del shard_ref
        rest[-1][...] = jnp.dot(h_ref[...], w_ref[...], preferred_element_type=F32)

    grid_spec = pltpu.PrefetchScalarGridSpec(
        num_scalar_prefetch=1, grid=(blocks,),
        in_specs=[pl.BlockSpec((s, d), lambda j, shard_ref: (0, 0)),
                  pl.BlockSpec((d, tn), lambda j, shard_ref: (0, shard_ref[0] * blocks + j))] + [ANY] * len(given),
        out_specs=pl.BlockSpec((s, tn), lambda j, shard_ref: (0, shard_ref[0] * blocks + j)))
    return pl.pallas_call(
        body, name=name, grid_spec=grid_spec,
        out_shape=jax.ShapeDtypeStruct((s, n), F32),
        input_output_aliases={3: 0} if given else {},
        compiler_params=_params(("arbitrary",)),
    )(shard_arr, h1, wi, *given)


def _split_start(name, bufs, n_sems, copies):
    n = len(bufs)

    def body(*refs):
        send_sems, recv_sems = refs[n:n + 2]
        for cp in copies(refs[n + 2:], send_sems, recv_sems):
            cp.start()

    outs = pl.pallas_call(
        body, name=name,
        in_specs=[_HBM] * n, out_specs=[_SEM, _SEM] + [_HBM] * n,
        out_shape=[pltpu.SemaphoreType.DMA((n_sems,)), pltpu.SemaphoreType.DMA((n_sems,))]
        + [pltpu.HBM(b.shape, b.dtype) for b in bufs],
        input_output_aliases={i: 2 + i for i in range(n)},
        compiler_params=pltpu.CompilerParams(has_side_effects=_EFFECT),
    )(*[_in_hbm(b) for b in bufs])
    return outs[0], outs[1], list(outs[2:])


def _split_wait(name, bufs, send_sems, recv_sems, copies, after):
    n = len(bufs)

    def body(*refs):
        send_ref, recv_ref = refs[n:n + 2]
        for cp in copies(refs[n + 3:], send_ref, recv_ref):
            cp.wait()

    return list(pl.pallas_call(
        body, name=name,
        in_specs=[_HBM] * n + [_SEM, _SEM, ANY], out_specs=[_HBM] * n,
        out_shape=[pltpu.HBM(b.shape, b.dtype) for b in bufs],
        input_output_aliases={i: i for i in range(n)},
        compiler_params=pltpu.CompilerParams(has_side_effects=_EFFECT),
    )(*bufs, send_sems, recv_sems, after))


def _halves_copies(n_w):
    def copies(bufs, send_sems, recv_sems):
        x, y, c = _place()
        out = []
        for w in range(n_w):
            view, land = bufs[w], bufs[n_w + w]
            src = view.at[1 - c] if len(view.shape) == 3 else view.at[:, 1 - c]
            out.append(pltpu.make_async_remote_copy(
                src_ref=src, dst_ref=land, send_sem=send_sems.at[w], recv_sem=recv_sems.at[w],
                device_id=(x, y, 1 - c), device_id_type=MESH))
        return out
    return copies


def _pieces_copies(n_w):
    def copies(bufs, send_sems, recv_sems):
        x, y, c = _place()
        out = []
        for w in range(n_w):
            for j, (cx, cy) in enumerate(_other_chips(x, y)):
                out.append(pltpu.make_async_remote_copy(
                    src_ref=bufs[w].at[2 * cx + cy], dst_ref=bufs[n_w + w].at[j],
                    send_sem=send_sems.at[3 * w + j], recv_sem=recv_sems.at[3 * w + j],
                    device_id=(cx, cy, c), device_id_type=MESH))
        return out
    return copies


def _join_copies(n_w):
    def copies(bufs, send_sems, recv_sems):
        x, y, c = _place()
        return [pltpu.make_async_remote_copy(
            src_ref=bufs[w].at[c], dst_ref=bufs[w].at[c], send_sem=send_sems.at[w], recv_sem=recv_sems.at[w],
            device_id=(x, y, 1 - c), device_id_type=MESH) for w in range(n_w)]
    return copies


def _grad_view(g, column_sharded):
    return _weight_view(g, column_sharded)


def _halves_landing(view):
    shape = view.shape[1:] if view.ndim == 3 else (N_CHIPS,) + view.shape[2:]
    return lax.empty(shape, BF16)


def _halves_start(tag, grads, column_sharded):
    views = [_weight_view(g, cs) for g, cs in zip(grads, column_sharded)]
    n = len(views)
    return _split_start("halves_start_" + tag, views + [_halves_landing(v) for v in views], n, _halves_copies(n))


def _halves_wait(tag, state, after):
    send_sems, recv_sems, bufs = state
    n = len(bufs) // 2
    bufs = _split_wait("halves_wait_" + tag, bufs, send_sems, recv_sems, _halves_copies(n), after)
    return bufs[:n], bufs[n:]


def _pieces_start(tag, pieces):
    n = len(pieces)
    landing = [lax.empty((3,) + p.shape[1:], BF16) for p in pieces]
    return _split_start("pieces_start_" + tag, list(pieces) + landing, 3 * n, _pieces_copies(n))


def _pieces_wait(tag, state, after):
    send_sems, recv_sems, bufs = state
    n = len(bufs) // 2
    bufs = _split_wait("pieces_wait_" + tag, bufs, send_sems, recv_sems, _pieces_copies(n), after)
    return bufs[:n], bufs[n:]


def _join_start(tag, shards):
    n = len(shards)
    return _split_start("join_start_" + tag, list(shards), n, _join_copies(n))


def _join_wait(tag, state, after):
    send_sems, recv_sems, bufs = state
    bufs = _split_wait("join_wait_" + tag, bufs, send_sems, recv_sems, _join_copies(len(bufs)), after)
    return [b.reshape(2 * b.shape[1], b.shape[2]) for b in bufs]


def _chip_sum_col(g3, sib, c_arr, name):
    _, hk, n = g3.shape
    cols = n // N_CHIPS
    tr = _row_tile(hk, cols * 2, limit=4 * 1024 * 1024)

    def body(c_ref, g_ref, s_ref, o_ref):
        del c_ref
        o_ref[...] = (g_ref[...].astype(F32) + s_ref[...].astype(F32)).astype(BF16)

    grid_spec = pltpu.PrefetchScalarGridSpec(
        num_scalar_prefetch=1, grid=(N_CHIPS, hk // tr),
        in_specs=[pl.BlockSpec((None, tr, cols), lambda p, r, c_ref: (c_ref[0], r, p)),
                  pl.BlockSpec((tr, cols), lambda p, r, c_ref: (r, p))],
        out_specs=pl.BlockSpec((None, tr, cols), lambda p, r, c_ref: (p, r, 0)))
    return pl.pallas_call(
        body, name=name, grid_spec=grid_spec,
        out_shape=jax.ShapeDtypeStruct((N_CHIPS, hk, cols), BF16),
        compiler_params=_params(("parallel", "parallel")),
    )(c_arr, g3, sib)


def _chip_sum_row(g4, sib, c_arr, name):
    _, _, hr, n = g4.shape
    tr = _row_tile(hr, n * 2, limit=4 * 1024 * 1024)

    def body(c_ref, g_ref, s_ref, o_ref):
        del c_ref
        o_ref[...] = (g_ref[...].astype(F32) + s_ref[...].astype(F32)).astype(BF16)

    grid_spec = pltpu.PrefetchScalarGridSpec(
        num_scalar_prefetch=1, grid=(N_CHIPS, hr // tr),
        in_specs=[pl.BlockSpec((None, None, tr, n), lambda p, r, c_ref: (p, c_ref[0], r, 0)),
                  pl.BlockSpec((None, tr, n), lambda p, r, c_ref: (p, r, 0))],
        out_specs=pl.BlockSpec((None, tr, n), lambda p, r, c_ref: (p, r, 0)))
    return pl.pallas_call(
        body, name=name, grid_spec=grid_spec,
        out_shape=jax.ShapeDtypeStruct((N_CHIPS, hr, n), BF16),
        compiler_params=_params(("parallel", "parallel")),
    )(c_arr, g4, sib)


def _sum_pieces(pieces, received, place_arr, name):
    _, r, n = pieces.shape
    tr = _row_tile(r, n * 4, limit=4 * 1024 * 1024)

    def body(p_ref, own_ref, r0_ref, r1_ref, r2_ref, o_ref):
        del p_ref
        acc = own_ref[...].astype(F32) + r0_ref[...].astype(F32)
        acc = acc + r1_ref[...].astype(F32)
        o_ref[...] = acc + r2_ref[...].astype(F32)

    def recv_spec(j):
        return pl.BlockSpec((None, tr, n), lambda i, p_ref: (j, i, 0))

    grid_spec = pltpu.PrefetchScalarGridSpec(
        num_scalar_prefetch=1, grid=(r // tr,),
        in_specs=[pl.BlockSpec((None, tr, n), lambda i, p_ref: (p_ref[0], i, 0)),
                  recv_spec(0), recv_spec(1), recv_spec(2)],
        out_specs=pl.BlockSpec((None, tr, n), lambda i, p_ref: (p_ref[1], i, 0)))
    return pl.pallas_call(
        body, name=name, grid_spec=grid_spec,
        out_shape=jax.ShapeDtypeStruct((2, r, n), F32),
        compiler_params=_params(("parallel",)),
    )(place_arr, pieces, received, received, received)


def _norm_weights_step(parts, w, m, v, after=()):
    rows, d = parts.shape
    after = tuple(after)

    def body(p_ref, w_ref, m_ref, v_ref, *rest):
        g_ref, d_ref, mo_ref, vo_ref, gathered, send_sems, recv_sems = rest[len(after):]
        x, y, c = _place()
        me = 4 * x + 2 * y + c
        gathered[me] = p_ref[...]
        copies = []
        for k in range(1, N_DEV):
            peer = (x ^ ((k >> 2) & 1), y ^ ((k >> 1) & 1), c ^ (k & 1))
            copies.append(pltpu.make_async_remote_copy(
                src_ref=p_ref, dst_ref=gathered.at[me], send_sem=send_sems.at[k - 1],
                recv_sem=recv_sems.at[k - 1], device_id=peer, device_id_type=MESH))
        for cp in copies:
            cp.start()
        for cp in copies:
            cp.wait()
        g = gathered[0]
        for k in range(1, N_DEV):
            g = g + gathered[k]
        delta, m_new, v_new = _adamw_math(w_ref[...], g, m_ref[...], v_ref[...])
        g_ref[...] = g
        d_ref[...] = delta
        mo_ref[...] = m_new
        vo_ref[...] = v_new

    vmem = pl.BlockSpec(memory_space=pltpu.VMEM)
    shp = jax.ShapeDtypeStruct((rows, d), F32)
    return pl.pallas_call(
        body, name="norm_weights_step",
        in_specs=[vmem] * 4 + [ANY] * len(after), out_specs=[vmem] * 4, out_shape=[shp] * 4,
        scratch_shapes=[pltpu.VMEM((N_DEV, rows, d), F32), pltpu.SemaphoreType.DMA((N_DEV - 1,)),
                        pltpu.SemaphoreType.DMA((N_DEV - 1,))],
        compiler_params=pltpu.CompilerParams(has_side_effects=True),
    )(parts, w, m, v, *after)


def kernel(x, norm_mix_w, w_in, w_out, norm_ffn_w, w_gate, w_up, w_down, norm_final_w, loss_target, m_norm_mix_w, m_w_in, m_w_out, m_norm_ffn_w, m_w_gate, m_w_up, m_w_down, m_norm_final_w, v_norm_mix_w, v_w_in, v_w_out, v_norm_ffn_w, v_w_gate, v_w_up, v_w_down, v_norm_final_w):
    s, d = x.shape[1], x.shape[2]
    xs = x.reshape(s, d)
    target = loss_target.reshape(s, d)
    big = {"w_in": (w_in, m_w_in, v_w_in), "w_out": (w_out, m_w_out, v_w_out),
           "w_gate": (w_gate, m_w_gate, v_w_gate), "w_up": (w_up, m_w_up, v_w_up),
           "w_down": (w_down, m_w_down, v_w_down)}
    big = {k: tuple(a.reshape(a.shape[1:]) for a in t) for k, t in big.items()}
    col_names, row_names = ("w_in", "w_gate", "w_up"), ("w_out", "w_down")
    n_in = N_CHIPS * big["w_in"][0].shape[1]
    ffn = N_CHIPS * big["w_gate"][0].shape[1]
    mix = ATTN_WIDTH + RET_WIDTH
    c_arr = lax.axis_index("c").astype(I32).reshape(1)
    shard_arr = (2 * lax.axis_index("x") + lax.axis_index("y")).astype(I32).reshape(1)
    place_arr = jnp.concatenate([shard_arr, c_arr])

    def cast(k, after=()):
        return _weight_view(_cast_into_full(big[k][0], shard_arr, k in col_names, "cast_" + k, after), k in col_names)

    started_in, v_in = _gather_in_start(cast("w_in"), "gather_in_start")

    sec = ATTN_WIDTH

    def section(p, rows):
        return pl.BlockSpec((None, rows, sec), lambda i, j, kk: (p, i, 0))

    h1 = _rms_fwd(xs, norm_mix_w, "rms_mix_fwd")
    my_shard = shard_arr[0]
    shard_of = [jnp.bitwise_xor(my_shard, f).astype(I32).reshape(1) for f in (0,) + _FLIPS]
    proj = _in_proj_shard(h1, _weight_unview(v_in), None, shard_of[0], "in_proj_own")
    rest = ("w_out", "w_gate", "w_up", "w_down")
    rest_views = [cast(k, after=[proj]) for k in rest]
    relayed_in, v_in = _gather_in_relay(v_in, started_in, rest_views, "gather_in_relay")
    send_sems, recv_sems, v_out, v_gate, v_up, v_down = _gather_start(rest_views, "gather_start_rest", after=[v_in])
    v_in = _gather_in_neighbours_end(v_in, relayed_in, [v_out], "gather_in_neighbours_end")
    proj = _in_proj_shard(h1, _weight_unview(v_in), proj, shard_of[1], "in_proj_x")
    proj = _in_proj_shard(h1, _weight_unview(v_in), proj, shard_of[2], "in_proj_y")
    forwarded_in, v_in = _gather_in_diagonal(v_in, relayed_in, [proj], "gather_in_diagonal")
    wi = _weight_unview(_gather_in_diagonal_end(v_in, forwarded_in, [proj], "gather_in_diagonal_end"))
    proj = _in_proj_shard(h1, wi, proj, shard_of[3], "in_proj_diagonal")
    fs_o, fr_o, v_out = _gather_forward([v_out], [0], send_sems, recv_sems, proj, "gather_forward_out")
    mixed, attn_o, lse = _attn_fwd(proj, after=[v_out])
    fs_g, fr_g, v_gate = _gather_forward([v_gate], [1], send_sems, recv_sems, attn_o, "gather_forward_gate")
    mixed, ret_raw = _ret_fwd(proj, mixed, after=[v_gate])
    wo, = _gather_end([v_out], fs_o, fr_o, ret_raw, "gather_end_out")
    x1, = _matmul("out_proj", "nn", [mixed, mixed], [wo, wo], [0, 0], s, d, sec, s, 512, sec, [xs], [F32],
                  _epi_residual, b_koff=[0, 1], a_specs=[section(0, s), section(1, s)])
    h2 = _rms_fwd(x1, norm_ffn_w, "rms_ffn_fwd")
    fs_u, fr_u, v_up = _gather_forward([v_up], [2], send_sems, recv_sems, h2, "gather_forward_up")
    wg, = _gather_end([v_gate], fs_g, fr_g, v_up, "gather_end_gate")
    wu, = _gather_end([v_up], fs_u, fr_u, wg, "gather_end_up")
    gate, up, act = _matmul("gate_up", "nn", [h2, h2], [wg, wu], [0, 1], s, ffn, d, s, 256, d, [],
                            [BF16, BF16, BF16], _epi_swiglu)
    fs, fr, v_down = _gather_forward([v_down], [3], send_sems, recv_sems, act, "gather_forward_down")
    wd, = _gather_end([v_down], fs, fr, act, "gather_end_down")
    x2, = _matmul("down_proj", "nn", [act], [wd], [0], s, d, ffn, s // 2, 256, ffn, [x1], [F32],
                  _epi_residual)
    loss_row, dx2, dx2b, dwf = _final_norm_loss(x2, norm_final_w.reshape(1, d), target, "final_norm_loss")

    names = col_names + row_names
    grads, new = {}, {}

    def chip_sums(tag_names, views, sibs):
        return [(_chip_sum_col if k in col_names else _chip_sum_row)(v, sb, c_arr, "chip_sum_" + k)
                for k, v, sb in zip(tag_names, views, sibs)]

    def piece_sums(tag_names, pieces, received):
        return [_sum_pieces(p, r, place_arr, "sum_pieces_" + k) for k, p, r in zip(tag_names, pieces, received)]

    def update(k):
        new[k] = _adamw(big[k][0], grads[k], big[k][1], big[k][2], "adamw_" + k)

    dgate, dup = _matmul("d_act", "nt", [dx2b], [wd], [0], s, ffn, d, s, 256, d, [gate, up],
                         [BF16, BF16], _epi_swiglu_bwd)
    g_wd, = _matmul("g_w_down", "tn", [act], [dx2b], [0], ffn, d, s, 512, d, s, [], [BF16], _epi_plain)
    halves_d = _halves_start("down", [g_wd], [False])
    dh2, = _matmul("d_h2", "nt", [dgate, dup], [wg, wu], [0, 0], s, d, ffn, s // 2, 256, ffn, [], [F32],
                   _epi_plain, after=halves_d[2][-1:], a_single_buffer=True)
    pieces_d = _pieces_start("down", chip_sums(["w_down"], *_halves_wait("down", halves_d, dh2)))
    g_wg, g_wu = _matmul("g_w_gate_up", "tn", [h2, h2], [dgate, dup], [0, 1], d, ffn, s, 1024, 512, s, [],
                         [BF16, BF16], _epi_two, after=pieces_d[2][-1:])
    halves_gu = _halves_start("gate_up", [g_wg, g_wu], [True, True])
    dx1, dx1b, dw_ffn = _rms_bwd(x1, norm_ffn_w, dh2, dx2, "rms_ffn_bwd", after=halves_gu[2][-1:])

    dmixed, = _matmul("d_mixed", "nt", [dx1b], [wo], [0], s, mix, d, s, 512, d, [], [F32], _epi_plain)
    pieces_gu = _pieces_start("gate_up", chip_sums(["w_gate", "w_up"], *_halves_wait("gate_up", halves_gu, dmixed)))
    per = sec // 512
    g_wo, = _matmul("g_w_out", "tn", [mixed], [dx1b], [0], mix, d, s, 512, d, s, [], [BF16], _epi_plain,
                    after=pieces_gu[2][-1:],
                    a_specs=[pl.BlockSpec((None, s, 512), lambda i, j, kk: (i // per, 0, i % per))])
    halves_o = _halves_start("out", [g_wo], [False])
    dsec = _attn_bwd(proj, attn_o, lse, dmixed, after=halves_o[2][-1:])
    pieces_o = _pieces_start("out", chip_sums(["w_out"], *_halves_wait("out", halves_o, dsec)))
    dsec = _ret_bwd(proj, ret_raw, dmixed, dsec, after=pieces_o[2][-1:])
    where = [0, 1, 2, 4, 5, 6, 7]
    n_sec = len(where)
    g_wi, = _matmul("g_w_in", "tn", [h1], [dsec], [0], d, n_in, s, 1024, sec, s, [], [BF16], _epi_plain,
                    b_specs=[pl.BlockSpec((None, s, sec), lambda i, j, kk: (j + (j >= 3).astype(I32), 0, 0))])
    halves_i = _halves_start("in", [g_wi], [True])
    dh1, = _matmul("d_h1", "nt", [dsec] * n_sec, [wi] * n_sec, [0] * n_sec, s, d, sec, s // 2, 256, sec, [], [F32],
                   _epi_plain, b_koff=list(range(n_sec)), after=halves_i[2][-1:],
                   a_specs=[section(p, s // 2) for p in where])
    pieces_i = _pieces_start("in", chip_sums(["w_in"], *_halves_wait("in", halves_i, dh1)))
    grad_x, _, dw_mix = _rms_bwd(xs, norm_mix_w, dh1, dx1, "rms_mix_bwd", after=pieces_i[2][-1:])

    def rows8(*vs):
        return jnp.concatenate([v.reshape(1, d) for v in vs] + [jnp.zeros((8 - len(vs), d), F32)], axis=0)

    join_d = _join_start("down", piece_sums(["w_down"], *_pieces_wait("down", pieces_d, grad_x)))
    join_gu = _join_start("gate_up", piece_sums(["w_gate", "w_up"], *_pieces_wait("gate_up", pieces_gu, join_d[2][0])))
    join_o = _join_start("out", piece_sums(["w_out"], *_pieces_wait("out", pieces_o, join_gu[2][0])))
    grads["w_down"], = _join_wait("down", join_d, join_o[2][0])
    update("w_down")
    grads["w_gate"], grads["w_up"] = _join_wait("gate_up", join_gu, new["w_down"][0])
    update("w_gate")
    update("w_up")
    grads["w_out"], = _join_wait("out", join_o, new["w_up"][0])
    update("w_out")
    join_i = _join_start("in", piece_sums(["w_in"], *_pieces_wait("in", pieces_i, new["w_out"][0])))
    ng, nd, nm, nv = _norm_weights_step(
        rows8(dw_mix, dw_ffn, dwf), rows8(norm_mix_w, norm_ffn_w, norm_final_w),
        rows8(m_norm_mix_w, m_norm_ffn_w, m_norm_final_w), rows8(v_norm_mix_w, v_norm_ffn_w, v_norm_final_w),
        after=join_i[2][:1])
    grads["w_in"], = _join_wait("in", join_i, ng)
    update("w_in")

    loss = lax.psum(loss_row[0, 0], ("x", "y", "c"))

    def pack(small, per_weight):
        lead = lambda a: a.reshape((1,) + a.shape)
        return (small[0:1], lead(per_weight["w_in"]), lead(per_weight["w_out"]), small[1:2],
                lead(per_weight["w_gate"]), lead(per_weight["w_up"]), lead(per_weight["w_down"]), small[2])

    return (loss, grad_x.reshape(1, s, d),
            *pack(ng, {k: new[k][3] for k in names}),
            *pack(nd, {k: new[k][0] for k in names}),
            *pack(nm, {k: new[k][1] for k in names}),
            *pack(nv, {k: new[k][2] for k in names}))
```

```python
import functools
import math

import jax
import jax.numpy as jnp
from jax import lax
from jax.experimental import pallas as pl
from jax.experimental.pallas import tpu as pltpu

F32 = jnp.float32
BF16 = jnp.bfloat16
I32 = jnp.int32
MESH = pl.DeviceIdType.MESH
ANY = pl.BlockSpec(memory_space=pl.ANY)

ATTN_HEADS = 8
ATTN_HEAD_DIM = 128
RET_HEADS = 4
RET_HEAD_DIM = 256
ATTN_WIDTH = ATTN_HEADS * ATTN_HEAD_DIM
RET_WIDTH = RET_HEADS * RET_HEAD_DIM
DILATED_PATTERNS = ((128, 1), (512, 4), (2048, 16))
NORM_EPS = 1e-6
ADAM_LR = 0.001
ADAM_B1 = 0.9
ADAM_B2 = 0.999
ADAM_EPS = 1e-08
ADAM_WD = 0.01
ADAM_STEP = 10

N_CHIPS = 4
N_DEV = 8
NEG_BIG = -1e30
SEQ_TILE = 512
ATTN_FWD_HEADS_PER_STEP = 2
ATTN_HEADS_PER_STEP = 1
VMEM_LIMIT_BYTES = 56 * 1024 * 1024


def _params(semantics=None, vmem=VMEM_LIMIT_BYTES):
    return pltpu.CompilerParams(dimension_semantics=semantics, vmem_limit_bytes=vmem)


def _row_tile(rows, row_bytes, limit=2 * 1024 * 1024, mult=16):
    best = None
    for t in range(mult, rows + 1, mult):
        if rows % t == 0 and t * row_bytes <= limit:
            best = t
    assert best is not None, (rows, row_bytes)
    return best


def _sigmoid(x):
    return 1.0 / (1.0 + jnp.exp(-x))


def _select_by_index(idx, values):
    out = jnp.float32(values[-1])
    for i in range(len(values) - 2, -1, -1):
        out = jnp.where(idx == i, jnp.float32(values[i]), out)
    return out


def _place():
    x, y, c = lax.axis_index("x"), lax.axis_index("y"), lax.axis_index("c")
    return x, y, c


def _cast_into_full(w, shard_arr, column_sharded, name, after=()):
    after = tuple(after)
    rows, cols = w.shape
    tr = _row_tile(rows, cols * 4)
    steps = rows // tr
    if column_sharded:
        out_shape, out_map = (rows, N_CHIPS * cols), (lambda i, s_ref: (i, s_ref[0]))
    else:
        out_shape, out_map = (N_CHIPS * rows, cols), (lambda i, s_ref: (s_ref[0] * steps + i, 0))

    def body(s_ref, w_ref, *rest):
        del s_ref
        rest[-1][...] = w_ref[...].astype(BF16)

    grid_spec = pltpu.PrefetchScalarGridSpec(
        num_scalar_prefetch=1, grid=(steps,),
        in_specs=[pl.BlockSpec((tr, cols), lambda i, s_ref: (i, 0))] + [ANY] * len(after),
        out_specs=pl.BlockSpec((tr, cols), out_map))
    return pl.pallas_call(
        body, name=name, grid_spec=grid_spec,
        out_shape=jax.ShapeDtypeStruct(out_shape, BF16),
        compiler_params=_params(("parallel",)),
    )(shard_arr, w, *after)


def _rms_fwd(x, w, name):
    rows, d = x.shape
    tr = 256

    def body(x_ref, w_ref, h_ref):
        xv = x_ref[...]
        r = lax.rsqrt(jnp.mean(xv * xv, axis=-1, keepdims=True) + NORM_EPS)
        h_ref[...] = (xv * r * w_ref[...]).astype(BF16)

    return pl.pallas_call(
        body, name=name, grid=(rows // tr,),
        in_specs=[pl.BlockSpec((tr, d), lambda i: (i, 0)), pl.BlockSpec((1, d), lambda i: (0, 0))],
        out_specs=pl.BlockSpec((tr, d), lambda i: (i, 0)),
        out_shape=jax.ShapeDtypeStruct((rows, d), BF16),
        compiler_params=_params(("parallel",)),
    )(x, w)


def _rms_bwd(x, w, dh, dres, name, after=()):
    rows, d = x.shape
    tr = 256
    after = tuple(after)

    def body(x_ref, w_ref, dh_ref, dres_ref, *rest):
        dx_ref, dxb_ref, dw_ref = rest[len(after):]
        xv = x_ref[...]
        r = lax.rsqrt(jnp.mean(xv * xv, axis=-1, keepdims=True) + NORM_EPS)
        xhat = xv * r
        dy = dh_ref[...]
        dxhat = dy * w_ref[...]
        dx = dres_ref[...] + r * (dxhat - xhat * jnp.mean(dxhat * xhat, axis=-1, keepdims=True))
        dx_ref[...] = dx
        dxb_ref[...] = dx.astype(BF16)
        part = jnp.sum(dy * xhat, axis=0, keepdims=True)

        @pl.when(pl.program_id(0) == 0)
        def _():
            dw_ref[...] = part

        @pl.when(pl.program_id(0) != 0)
        def _():
            dw_ref[...] += part

    row = pl.BlockSpec((tr, d), lambda i: (i, 0))
    vec = pl.BlockSpec((1, d), lambda i: (0, 0))
    return pl.pallas_call(
        body, name=name, grid=(rows // tr,),
        in_specs=[row, vec, row, row] + [ANY] * len(after),
        out_specs=[row, row, vec],
        out_shape=[jax.ShapeDtypeStruct((rows, d), F32), jax.ShapeDtypeStruct((rows, d), BF16),
                   jax.ShapeDtypeStruct((1, d), F32)],
        compiler_params=_params(("arbitrary",)),
    )(x, w, dh, dres, *after)


def _final_norm_loss(x2, w, target, name):
    rows, d = x2.shape
    tr = 256

    def body(x_ref, w_ref, t_ref, loss_ref, dx_ref, dxb_ref, dw_ref):
        xv = x_ref[...]
        wv = w_ref[...]
        r = lax.rsqrt(jnp.mean(xv * xv, axis=-1, keepdims=True) + NORM_EPS)
        xhat = xv * r
        err = xhat * wv - t_ref[...]
        part_loss = 0.5 * jnp.sum(jnp.mean(err * err, axis=-1, keepdims=True), axis=0, keepdims=True)
        dy = err * (1.0 / d)
        dxhat = dy * wv
        dx = r * (dxhat - xhat * jnp.mean(dxhat * xhat, axis=-1, keepdims=True))
        dx_ref[...] = dx
        dxb_ref[...] = dx.astype(BF16)
        part_dw = jnp.sum(dy * xhat, axis=0, keepdims=True)
        part_loss = jnp.broadcast_to(part_loss, (1, 128))

        @pl.when(pl.program_id(0) == 0)
        def _():
            dw_ref[...] = part_dw
            loss_ref[...] = part_loss

        @pl.when(pl.program_id(0) != 0)
        def _():
            dw_ref[...] += part_dw
            loss_ref[...] += part_loss

    row = pl.BlockSpec((tr, d), lambda i: (i, 0))
    vec = pl.BlockSpec((1, d), lambda i: (0, 0))
    return pl.pallas_call(
        body, name=name, grid=(rows // tr,),
        in_specs=[row, vec, row],
        out_specs=[pl.BlockSpec((1, 128), lambda i: (0, 0)), row, row, vec],
        out_shape=[jax.ShapeDtypeStruct((1, 128), F32), jax.ShapeDtypeStruct((rows, d), F32),
                   jax.ShapeDtypeStruct((rows, d), BF16), jax.ShapeDtypeStruct((1, d), F32)],
        compiler_params=_params(("arbitrary",)),
    )(x2, w, target)


def _adamw_math(w, g, m, v):
    m = ADAM_B1 * m + (1.0 - ADAM_B1) * g
    v = ADAM_B2 * v + (1.0 - ADAM_B2) * (g * g)
    m_hat = m / (1.0 - ADAM_B1 ** ADAM_STEP)
    v_hat = v / (1.0 - ADAM_B2 ** ADAM_STEP)
    delta = -ADAM_LR * (m_hat / (jnp.sqrt(v_hat) + ADAM_EPS) + ADAM_WD * w)
    return delta, m, v


def _adamw(w, g, m, v, name):
    rows, cols = w.shape
    tr = _row_tile(rows, cols * 4)

    def body(w_ref, g_ref, m_ref, v_ref, d_ref, mo_ref, vo_ref, go_ref):
        g = g_ref[...]
        delta, m_new, v_new = _adamw_math(w_ref[...], g, m_ref[...], v_ref[...])
        d_ref[...] = delta
        mo_ref[...] = m_new
        vo_ref[...] = v_new
        go_ref[...] = g

    blk = pl.BlockSpec((tr, cols), lambda i: (i, 0))
    shp = jax.ShapeDtypeStruct((rows, cols), F32)
    return pl.pallas_call(
        body, name=name, grid=(rows // tr,),
        in_specs=[blk] * 4, out_specs=[blk] * 4, out_shape=[shp] * 4,
        compiler_params=_params(("parallel",)),
    )(w, g, m, v)


_DOT_DIMS = {"nn": ((1,), (0,)), "nt": ((1,), (1,)), "tn": ((0,), (0,))}


def _matmul(name, mode, a_list, b_list, acc_of, m, n, k, tm, tn, tk, extras, out_dtypes, epilogue,
            a_koff=None, b_koff=None, after=(), a_specs=None, b_specs=None, a_single_buffer=False):
    after = tuple(after)
    assert m % tm == 0 and n % tn == 0 and k % tk == 0, (name, m, n, k, tm, tn, tk)
    nk = k // tk
    n_acc = max(acc_of) + 1
    n_pairs = len(a_list)
    a_koff = a_koff or [0] * n_pairs
    b_koff = b_koff or [0] * n_pairs
    dims = (_DOT_DIMS[mode], ((), ()))
    n_ext, n_out = len(extras), len(out_dtypes)

    def body(*refs):
        a_refs = refs[:n_pairs]
        b_refs = refs[n_pairs:2 * n_pairs]
        e_refs = refs[2 * n_pairs:2 * n_pairs + n_ext]
        first_out = 2 * n_pairs + n_ext + len(after)
        o_refs = refs[first_out:first_out + n_out]
        acc_refs = refs[first_out + n_out:]

        parts = [None] * n_acc
        for p in range(n_pairs):
            d = lax.dot_general(a_refs[p][...], b_refs[p][...], dims, preferred_element_type=F32)
            parts[acc_of[p]] = d if parts[acc_of[p]] is None else parts[acc_of[p]] + d

        def finish(accs):
            outs = epilogue(accs, [e[...] for e in e_refs])
            for o_ref, o in zip(o_refs, outs):
                o_ref[...] = o.astype(o_ref.dtype)

        if nk == 1:
            finish(parts)
        else:
            kk = pl.program_id(2)

            @pl.when(kk == 0)
            def _():
                for acc_ref, part in zip(acc_refs, parts):
                    acc_ref[...] = part

            @pl.when(kk != 0)
            def _():
                for acc_ref, part in zip(acc_refs, parts):
                    acc_ref[...] += part

            @pl.when(kk == nk - 1)
            def _():
                finish([acc_ref[...] for acc_ref in acc_refs])

    def a_spec(off):
        mode_a = pl.Buffered(1) if a_single_buffer else None
        if mode == "tn":
            return pl.BlockSpec((tk, tm), lambda i, j, kk: (kk + off, i), pipeline_mode=mode_a)
        return pl.BlockSpec((tm, tk), lambda i, j, kk: (i, kk + off), pipeline_mode=mode_a)

    def b_spec(off):
        if mode == "nt":
            return pl.BlockSpec((tn, tk), lambda i, j, kk: (j, kk + off))
        return pl.BlockSpec((tk, tn), lambda i, j, kk: (kk + off, j))

    tile = pl.BlockSpec((tm, tn), lambda i, j, kk: (i, j))
    scratch = [pltpu.VMEM((tm, tn), F32) for _ in range(n_acc)] if nk > 1 else []
    return pl.pallas_call(
        body, name=name, grid=(m // tm, n // tn, nk),
        in_specs=(a_specs or [a_spec(o) for o in a_koff]) + (b_specs or [b_spec(o) for o in b_koff])
        + [tile] * n_ext + [ANY] * len(after),
        out_specs=[tile] * n_out,
        out_shape=[jax.ShapeDtypeStruct((m, n), dt) for dt in out_dtypes],
        scratch_shapes=scratch,
        compiler_params=_params(("parallel", "parallel", "arbitrary")),
    )(*a_list, *b_list, *extras, *after)


def _epi_plain(accs, extras):
    return (accs[0],)


def _epi_residual(accs, extras):
    return (accs[0] + extras[0],)


def _epi_two(accs, extras):
    return accs[0], accs[1]


def _epi_swiglu(accs, extras):
    g, u = accs
    return g, u, g * _sigmoid(g) * u


def _epi_swiglu_bwd(accs, extras):
    da = accs[0]
    g, u = (e.astype(F32) for e in extras)
    sg = _sigmoid(g)
    dg = da * u * sg * (1.0 + g * (1.0 - sg))
    du = da * g * sg
    return dg, du


_NT_DIMS = (((1,), (1,)), ((), ()))
_TN_DIMS = (((0,), (0,)), ((), ()))


def _tile_delta(tq, tk):
    return lax.broadcasted_iota(I32, (tq, tk), 0) - lax.broadcasted_iota(I32, (tq, tk), 1)


def _attn_log_count(delta):
    count = jnp.zeros(delta.shape, I32)
    for window, dilation in DILATED_PATTERNS:
        hit = ((delta & (dilation - 1)) == 0) & (delta <= window)
        count = count + jnp.where(hit, 1, 0)
    valid = (delta >= 0) & (count > 0)
    logm = jnp.where(count == 3, math.log(3.0), jnp.where(count == 2, math.log(2.0), 0.0))
    return jnp.where(valid, logm, NEG_BIG)


def _fill_attn_log_count(tab_ref):
    nb, t, _ = tab_ref.shape
    base = _tile_delta(t, t)
    for b in range(nb):
        tab_ref[b] = _attn_log_count(base + b * t)


def _fill_attn_bias(tab_ref, log_count_ref, slope):
    nb, t, _ = tab_ref.shape
    dist = _tile_delta(t, t).astype(F32)
    for b in range(nb):
        tab_ref[b] = log_count_ref[b] - slope * (dist + float(b * t))


def _fill_ret_decay(tab_ref, log_gamma):
    nb, t, _ = tab_ref.shape
    base = _tile_delta(t, t)
    for b in range(nb):
        tab_ref[b] = _ret_decay(base + b * t, log_gamma)


def _alibi_slopes():
    return [2.0 ** (-8.0 * (h + 1) / ATTN_HEADS) for h in range(ATTN_HEADS)]


def _attn_fwd(proj, after=()):
    s = proj.shape[0]
    t = SEQ_TILE
    hd = ATTN_HEAD_DIM
    hp = ATTN_FWD_HEADS_PER_STEP
    ng = ATTN_HEADS // hp
    w = hp * hd
    scale = 1.0 / math.sqrt(hd)
    slopes = _alibi_slopes()

    def body(q_ref, k_ref, v_ref, *rest):
        mix_ref, o_ref, lse_ref, kb, vb, bias_tab, log_count_tab = rest[len(after):]
        g = pl.program_id(0)
        i = pl.program_id(1)

        @pl.when((g == 0) & (i == 0))
        def _():
            _fill_attn_log_count(log_count_tab)

        @pl.when(i == 0)
        def _():
            kb[...] = k_ref[...].astype(BF16)
            vb[...] = v_ref[...].astype(BF16)
            for u in range(hp):
                _fill_attn_bias(bias_tab.at[u], log_count_tab, _select_by_index(g * hp + u, slopes))

        qs = [q_ref[:, u * hd:(u + 1) * hd].astype(BF16) for u in range(hp)]

        def step(j, carry):
            rows = pl.ds(pl.multiple_of(j * t, t), t)
            out = []
            for u in range(hp):
                m_i, l_i, acc = carry[u]
                lanes = slice(u * hd, (u + 1) * hd)
                sc = lax.dot_general(qs[u], kb[rows, lanes], _NT_DIMS, preferred_element_type=F32) * scale
                sc = sc + bias_tab[u, i - j]
                m_new = jnp.maximum(m_i, jnp.max(sc, axis=-1, keepdims=True))
                p = jnp.exp(sc - m_new)
                alpha = jnp.exp(m_i - m_new)
                l_new = alpha * l_i + jnp.sum(p, axis=-1, keepdims=True)
                acc = alpha * acc + jnp.dot(p.astype(BF16), vb[rows, lanes], preferred_element_type=F32)
                out.append((m_new, l_new, acc))
            return tuple(out)

        init = (jnp.full((t, 1), NEG_BIG, F32), jnp.zeros((t, 1), F32), jnp.zeros((t, hd), F32))
        final = lax.fori_loop(0, i + 1, step, (init,) * hp)
        for u in range(hp):
            m_i, l_i, acc = final[u]
            lanes = slice(u * hd, (u + 1) * hd)
            out = acc / l_i
            o_ref[:, lanes] = out
            mix_ref[:, lanes] = out.astype(BF16)
            lse_ref[:, lanes] = jnp.broadcast_to(m_i + jnp.log(l_i), (t, hd))

    return pl.pallas_call(
        body, name="attn_fwd", grid=(ng, s // t),
        in_specs=[pl.BlockSpec((t, w), lambda g, i: (i, g)),
                  pl.BlockSpec((s, w), lambda g, i: (0, ng + g)),
                  pl.BlockSpec((s, w), lambda g, i: (0, 2 * ng + g))] + [ANY] * len(after),
        out_specs=[pl.BlockSpec((None, t, w), lambda g, i: (0, i, g))] + [pl.BlockSpec((t, w), lambda g, i: (i, g))] * 2,
        out_shape=[jax.ShapeDtypeStruct((2, s, ATTN_WIDTH), BF16),
                   jax.ShapeDtypeStruct((s, ATTN_WIDTH), F32),
                   jax.ShapeDtypeStruct((s, ATTN_WIDTH), F32)],
        scratch_shapes=[pltpu.VMEM((s, w), BF16), pltpu.VMEM((s, w), BF16), pltpu.VMEM((hp, s // t, t, t), F32),
                        pltpu.VMEM((s // t, t, t), F32)],
        compiler_params=_params(("arbitrary", "arbitrary")),
    )(proj, proj, proj, *after)


def _attn_bwd(proj, attn_out, lse, dmixed, after=()):
    after = tuple(after)
    s = proj.shape[0]
    t = SEQ_TILE
    nt = s // t
    hd = ATTN_HEAD_DIM
    hp = ATTN_HEADS_PER_STEP
    ng = ATTN_HEADS // hp
    w = hp * hd
    scale = 1.0 / math.sqrt(hd)
    slopes = _alibi_slopes()

    def body(q_ref, k_ref, v_ref, o_ref, lse_ref, do_ref, *rest):
        dsec_ref, qb, kb, vb, dob, dsum, dq_acc, bias_tab, log_count_tab = rest[len(after):]
        g = pl.program_id(0)

        @pl.when(g == 0)
        def _():
            _fill_attn_log_count(log_count_tab)

        qb[...] = q_ref[...].astype(BF16)
        kb[...] = k_ref[...].astype(BF16)
        vb[...] = v_ref[...].astype(BF16)
        dob[...] = do_ref[...].astype(BF16)
        for u in range(hp):
            lanes = slice(u * hd, (u + 1) * hd)
            _fill_attn_bias(bias_tab.at[u], log_count_tab, _select_by_index(g * hp + u, slopes))
            rowsum = jnp.sum(do_ref[:, lanes] * o_ref[:, lanes], axis=-1, keepdims=True)
            dsum[:, lanes] = jnp.broadcast_to(rowsum, (s, hd))
        dq_acc[...] = jnp.zeros((s, w), F32)

        def over_keys(j, _):
            krows = pl.ds(pl.multiple_of(j * t, t), t)

            def over_queries(i, carry):
                qrows = pl.ds(pl.multiple_of(i * t, t), t)
                out = []
                for u in range(hp):
                    dk, dv = carry[u]
                    lanes = slice(u * hd, (u + 1) * hd)
                    qi, doi = qb[qrows, lanes], dob[qrows, lanes]
                    kj, vj = kb[krows, lanes], vb[krows, lanes]
                    lse_i = lse_ref[qrows, lanes][:, :1]
                    dsum_i = dsum[qrows, lanes][:, :1]
                    sc = lax.dot_general(qi, kj, _NT_DIMS, preferred_element_type=F32) * scale
                    p = jnp.exp(sc + bias_tab[u, i - j] - lse_i)
                    dp = lax.dot_general(doi, vj, _NT_DIMS, preferred_element_type=F32)
                    ds = (p * (dp - dsum_i)).astype(BF16)
                    dv = dv + lax.dot_general(p.astype(BF16), doi, _TN_DIMS, preferred_element_type=F32)
                    dk = dk + lax.dot_general(ds, qi, _TN_DIMS, preferred_element_type=F32)
                    dq_acc[qrows, lanes] += jnp.dot(ds, kj, preferred_element_type=F32)
                    out.append((dk, dv))
                return tuple(out)

            zero = jnp.zeros((t, hd), F32)
            final = lax.fori_loop(j, nt, over_queries, ((zero, zero),) * hp)
            for u in range(hp):
                lanes = slice(u * hd, (u + 1) * hd)
                dsec_ref[1, krows, lanes] = (final[u][0] * scale).astype(BF16)
                dsec_ref[2, krows, lanes] = final[u][1].astype(BF16)
            return 0

        lax.fori_loop(0, nt, over_keys, 0)
        dsec_ref[0] = (dq_acc[...] * scale).astype(BF16)

    def col(off):
        return pl.BlockSpec((s, w), lambda g: (0, off + g))

    return pl.pallas_call(
        body, name="attn_bwd", grid=(ng,),
        in_specs=[col(0), col(ng), col(2 * ng), col(0), col(0), col(0)] + [ANY] * len(after),
        out_specs=pl.BlockSpec((4, s, w), lambda g: (0, 0, g)),
        out_shape=jax.ShapeDtypeStruct((8, s, ATTN_WIDTH), BF16),
        scratch_shapes=[pltpu.VMEM((s, w), BF16)] * 4 + [pltpu.VMEM((s, w), F32)] * 2
        + [pltpu.VMEM((hp, nt, t, t), F32), pltpu.VMEM((nt, t, t), F32)],
        compiler_params=_params(("arbitrary",)),
    )(proj, proj, proj, attn_out, lse, dmixed, *after)


def _ret_log_gammas():
    return [math.log(1.0 - 2.0 ** (-5.0 - h)) for h in range(RET_HEADS)]


def _ret_decay(delta, log_gamma):
    dec = jnp.exp(delta.astype(F32) * log_gamma) * (1.0 / math.sqrt(RET_HEAD_DIM))
    return jnp.where(delta >= 0, dec, 0.0)


def _ret_fwd(proj, mixed, after=()):
    after = tuple(after)
    s = proj.shape[0]
    t = SEQ_TILE
    hd = RET_HEAD_DIM
    nh = RET_HEADS
    log_gammas = _ret_log_gammas()
    c0 = 3 * ATTN_WIDTH // hd

    def body(q_ref, k_ref, v_ref, g_ref, *rest):
        mix_ref, raw_ref, kb, vb, decay_tab = rest[1 + len(after):]
        h = pl.program_id(0)
        i = pl.program_id(1)

        @pl.when(i == 0)
        def _():
            kb[...] = k_ref[...].astype(BF16)
            vb[...] = v_ref[...].astype(BF16)
            _fill_ret_decay(decay_tab, _select_by_index(h, log_gammas))

        q = q_ref[...].astype(BF16)

        def step(j, acc):
            rows = pl.ds(pl.multiple_of(j * t, t), t)
            sc = lax.dot_general(q, kb[rows, :], _NT_DIMS, preferred_element_type=F32) * decay_tab[i - j]
            return acc + jnp.dot(sc.astype(BF16), vb[rows, :], preferred_element_type=F32)

        ret = lax.fori_loop(0, i + 1, step, jnp.zeros((t, hd), F32))
        raw_ref[...] = ret
        r = lax.rsqrt(jnp.mean(ret * ret, axis=-1, keepdims=True) + NORM_EPS)
        g = g_ref[...]
        mix_ref[...] = (g * _sigmoid(g) * (ret * r)).astype(BF16)

    return pl.pallas_call(
        body, name="ret_fwd", grid=(nh, s // t),
        in_specs=[pl.BlockSpec((t, hd), lambda h, i: (i, c0 + h)),
                  pl.BlockSpec((s, hd), lambda h, i: (0, c0 + nh + h)),
                  pl.BlockSpec((s, hd), lambda h, i: (0, c0 + 2 * nh + h)),
                  pl.BlockSpec((t, hd), lambda h, i: (i, c0 + 3 * nh + h))] + [ANY] * (1 + len(after)),
        out_specs=[pl.BlockSpec((None, t, hd), lambda h, i: (1, i, h)), pl.BlockSpec((t, hd), lambda h, i: (i, h))],
        out_shape=[jax.ShapeDtypeStruct(mixed.shape, BF16), jax.ShapeDtypeStruct((s, RET_WIDTH), F32)],
        input_output_aliases={4: 0},
        scratch_shapes=[pltpu.VMEM((s, hd), BF16), pltpu.VMEM((s, hd), BF16), pltpu.VMEM((s // t, t, t), F32)],
        compiler_params=_params(("arbitrary", "arbitrary")),
    )(proj, proj, proj, proj, mixed, *after)


def _ret_bwd(proj, ret_raw, dmixed, dsec, after=()):
    after = tuple(after)
    s = proj.shape[0]
    t = SEQ_TILE
    nt = s // t
    hd = RET_HEAD_DIM
    nh = RET_HEADS
    log_gammas = _ret_log_gammas()
    c0 = 3 * ATTN_WIDTH // hd
    mixed_blocks = ATTN_WIDTH // hd

    def body(q_ref, k_ref, v_ref, g_ref, raw_ref, dmix_ref, *rest):
        dsec_ref, qb, kb, vb, dretb, dq_acc, decay_tab = rest[1 + len(after):]
        h = pl.program_id(0)
        _fill_ret_decay(decay_tab, _select_by_index(h, log_gammas))
        qb[...] = q_ref[...].astype(BF16)
        kb[...] = k_ref[...].astype(BF16)
        vb[...] = v_ref[...].astype(BF16)
        ret = raw_ref[...]
        r = lax.rsqrt(jnp.mean(ret * ret, axis=-1, keepdims=True) + NORM_EPS)
        normed = ret * r
        g = g_ref[...]
        sg = _sigmoid(g)
        dout = dmix_ref[...]
        dsec_ref[3] = (dout * normed * sg * (1.0 + g * (1.0 - sg))).astype(BF16)
        dn = dout * g * sg
        dret = r * (dn - normed * jnp.mean(dn * normed, axis=-1, keepdims=True))
        dretb[...] = dret.astype(BF16)
        dq_acc[...] = jnp.zeros((s, hd), F32)

        def over_keys(j, _):
            krows = pl.ds(pl.multiple_of(j * t, t), t)
            kj = kb[krows, :]
            vj = vb[krows, :]

            def over_queries(i, carry):
                dk, dv = carry
                qrows = pl.ds(pl.multiple_of(i * t, t), t)
                qi = qb[qrows, :]
                doi = dretb[qrows, :]
                dec = decay_tab[i - j]
                a = (lax.dot_general(qi, kj, _NT_DIMS, preferred_element_type=F32) * dec).astype(BF16)
                da = (lax.dot_general(doi, vj, _NT_DIMS, preferred_element_type=F32) * dec).astype(BF16)
                dv = dv + lax.dot_general(a, doi, _TN_DIMS, preferred_element_type=F32)
                dk = dk + lax.dot_general(da, qi, _TN_DIMS, preferred_element_type=F32)
                dq_acc[qrows, :] += jnp.dot(da, kj, preferred_element_type=F32)
                return dk, dv

            zero = jnp.zeros((t, hd), F32)
            dk, dv = lax.fori_loop(j, nt, over_queries, (zero, zero))
            dsec_ref[1, krows, :] = dk.astype(BF16)
            dsec_ref[2, krows, :] = dv.astype(BF16)
            return 0

        lax.fori_loop(0, nt, over_keys, 0)
        dsec_ref[0] = dq_acc[...].astype(BF16)

    def col(off):
        return pl.BlockSpec((s, hd), lambda h: (0, off + h))

    return pl.pallas_call(
        body, name="ret_bwd", grid=(nh,),
        in_specs=[col(c0), col(c0 + nh), col(c0 + 2 * nh), col(c0 + 3 * nh), col(0), col(mixed_blocks)]
        + [ANY] * (1 + len(after)),
        out_specs=pl.BlockSpec((4, s, hd), lambda h: (1, 0, h)),
        out_shape=jax.ShapeDtypeStruct(dsec.shape, BF16),
        input_output_aliases={6: 0},
        scratch_shapes=[pltpu.VMEM((s, hd), BF16)] * 4 + [pltpu.VMEM((s, hd), F32)]
        + [pltpu.VMEM((nt, t, t), F32)],
        compiler_params=_params(("arbitrary",)),
    )(proj, proj, proj, proj, ret_raw, dmixed, dsec, *after)


_FLIPS = (2, 1, 3)


def _other_chips(x, y):
    return [(1 - x, y), (x, 1 - y), (1 - x, 1 - y)]


_HBM = pl.BlockSpec(memory_space=pltpu.HBM)
_SEM = pl.BlockSpec(memory_space=pltpu.SEMAPHORE)
_EFFECT = pltpu.SideEffectType.DATAFLOW_SIDE_EFFECTING


def _in_hbm(a):
    return pltpu.with_memory_space_constraint(a, pltpu.HBM)


def _weight_view(w, column_sharded):
    if column_sharded:
        return w.reshape(2, w.shape[0] // 2, w.shape[1])
    return w.reshape(N_CHIPS, 2, w.shape[0] // (2 * N_CHIPS), w.shape[1])


def _weight_unview(v):
    if v.ndim == 3:
        return v.reshape(2 * v.shape[1], v.shape[2])
    return v.reshape(N_CHIPS * 2 * v.shape[2], v.shape[3])


def _weight_region(buf, shard, half):
    if len(buf.shape) == 3:
        cols = buf.shape[2] // N_CHIPS
        return buf.at[half, :, pl.ds(shard * cols, cols)]
    return buf.at[shard, half]


def _remote(where, send_sem, recv_sem, to):
    return pltpu.make_async_remote_copy(src_ref=where, dst_ref=where, send_sem=send_sem, recv_sem=recv_sem,
                                        device_id=to, device_id_type=MESH)


def _for_my_shard(fn):
    x, y, _ = _place()
    for ss in range(N_CHIPS):
        pl.when(2 * x + y == ss)(functools.partial(fn, ss))


def _gather_start(views, name, after=()):
    n_w = len(views)
    after = tuple(after)

    def body(*refs):
        send_sems, recv_sems = refs[n_w + len(after):n_w + len(after) + 2]
        bufs = refs[n_w + len(after) + 2:]
        x, y, c = _place()

        def start(ss):
            for w in range(n_w):
                for j, chip in enumerate(_other_chips(x, y)):
                    _remote(_weight_region(bufs[w], ss, c), send_sems.at[3 * w + j], recv_sems.at[3 * w + j],
                            (*chip, c)).start()

        _for_my_shard(start)

    return pl.pallas_call(
        body, name=name,
        in_specs=[_HBM] * n_w + [ANY] * len(after), out_specs=[_SEM, _SEM] + [_HBM] * n_w,
        out_shape=[pltpu.SemaphoreType.DMA((3 * n_w,)), pltpu.SemaphoreType.DMA((3 * n_w,))]
        + [pltpu.HBM(v.shape, BF16) for v in views],
        input_output_aliases={w: 2 + w for w in range(n_w)},
        compiler_params=pltpu.CompilerParams(has_side_effects=_EFFECT),
    )(*[_in_hbm(v) for v in views], *after)


def _gather_forward(views, which, send_sems, recv_sems, after, name):
    n_w = len(views)

    def body(*refs):
        send_in, recv_in = refs[n_w:n_w + 2]
        fwd_send, fwd_recv = refs[n_w + 3:n_w + 5]
        bufs = refs[n_w + 5:]
        x, y, c = _place()
        sibling = (x, y, 1 - c)

        def forward(ss):
            for i, w in enumerate(which):
                for j in range(3):
                    landed = _weight_region(bufs[i], ss ^ _FLIPS[j], c)
                    _remote(landed, send_in.at[3 * w + j], recv_in.at[3 * w + j], sibling).wait_recv()
                    _remote(landed, fwd_send.at[3 * i + j], fwd_recv.at[3 * i + j], sibling).start()

        _for_my_shard(forward)
        for i, w in enumerate(which):
            for j in range(3):
                _remote(_weight_region(bufs[i], 0, 0), send_in.at[3 * w + j], recv_in.at[3 * w + j],
                        sibling).wait_send()

    return pl.pallas_call(
        body, name=name,
        in_specs=[_HBM] * n_w + [_SEM, _SEM, ANY], out_specs=[_SEM, _SEM] + [_HBM] * n_w,
        out_shape=[pltpu.SemaphoreType.DMA((3 * n_w,)), pltpu.SemaphoreType.DMA((3 * n_w,))]
        + [pltpu.HBM(v.shape, BF16) for v in views],
        input_output_aliases={w: 2 + w for w in range(n_w)},
        compiler_params=pltpu.CompilerParams(has_side_effects=_EFFECT),
    )(*views, send_sems, recv_sems, after)


def _gather_end(views, fwd_send, fwd_recv, after, name):
    n_w = len(views)

    def body(*refs):
        fwd_send_ref, fwd_recv_ref = refs[n_w:n_w + 2]
        bufs = refs[n_w + 3:]
        x, y, c = _place()
        for i in range(n_w):
            for j in range(3):
                cp = _remote(_weight_region(bufs[i], 0, 0), fwd_send_ref.at[3 * i + j], fwd_recv_ref.at[3 * i + j],
                             (x, y, 1 - c))
                cp.wait_recv()
                cp.wait_send()

    outs = pl.pallas_call(
        body, name=name,
        in_specs=[_HBM] * n_w + [_SEM, _SEM, ANY], out_specs=[_HBM] * n_w,
        out_shape=[pltpu.HBM(v.shape, BF16) for v in views],
        input_output_aliases={w: w for w in range(n_w)},
        compiler_params=pltpu.CompilerParams(has_side_effects=_EFFECT),
    )(*views, fwd_send, fwd_recv, after)
    return [_weight_unview(o) for o in outs]


def _comm_call(name, bufs, sem_pairs, after, n_new, fn):
    n, n_sem, after = len(bufs), 2 * len(sem_pairs), tuple(after)
    n_out_sem = 2 if n_new else 0

    def body(*refs):
        sems = refs[n:n + n_sem]
        outs = refs[n + n_sem + len(after):]
        new = outs[:n_out_sem] if n_new else (None, None)
        fn(outs[n_out_sem:], [(sems[2 * i], sems[2 * i + 1]) for i in range(len(sem_pairs))], *new)

    res = pl.pallas_call(
        body, name=name,
        in_specs=[_HBM] * n + [_SEM] * n_sem + [ANY] * len(after),
        out_specs=[_SEM] * n_out_sem + [_HBM] * n,
        out_shape=[pltpu.SemaphoreType.DMA((n_new,))] * n_out_sem + [pltpu.HBM(b.shape, b.dtype) for b in bufs],
        input_output_aliases={i: n_out_sem + i for i in range(n)},
        compiler_params=pltpu.CompilerParams(has_side_effects=_EFFECT),
    )(*bufs, *[s for pair in sem_pairs for s in pair], *after)
    return list(res[:n_out_sem]), list(res[n_out_sem:])


def _quarter(piece, q):
    rows = piece.shape[0] // 2
    return piece.at[pl.ds(q * rows, rows)]


def _gather_in_start(view, name):
    def fn(bufs, _, send, recv):
        x, y, c = _place()

        def go(ss):
            for j, chip in enumerate(_other_chips(x, y)[:2]):
                _remote(_weight_region(bufs[0], ss, c), send.at[j], recv.at[j], (*chip, c)).start()

        _for_my_shard(go)

    sems, (view,) = _comm_call(name, [_in_hbm(view)], [], (), 2, fn)
    return sems, view


def _gather_in_relay(view, started, after, name):
    def fn(bufs, pairs, send, recv):
        (send_in, recv_in), = pairs
        x, y, c = _place()
        chips = _other_chips(x, y)
        sibling = (x, y, 1 - c)

        def go(ss):
            landed = [_weight_region(bufs[0], ss ^ _FLIPS[j], c) for j in range(2)]
            for j in range(2):
                _remote(landed[j], send_in.at[j], recv_in.at[j], sibling).wait_recv()
            for j in range(2):
                _remote(_quarter(landed[j], j), send.at[j], recv.at[j], (*chips[1 - j], c)).start()
            for j in range(2):
                _remote(landed[j], send.at[2 + j], recv.at[2 + j], sibling).start()

        _for_my_shard(go)
        for j in range(2):
            _remote(_weight_region(bufs[0], 0, 0), send_in.at[j], recv_in.at[j], sibling).wait_send()

    sems, (view,) = _comm_call(name, [view], [started], after, 4, fn)
    return sems, view


def _gather_in_neighbours_end(view, relayed, after, name):
    def fn(bufs, pairs, *_):
        (send, recv), = pairs
        x, y, c = _place()
        for j in range(2):
            cp = _remote(_weight_region(bufs[0], 0, 0), send.at[2 + j], recv.at[2 + j], (x, y, 1 - c))
            cp.wait_recv()
            cp.wait_send()

    _, (view,) = _comm_call(name, [view], [relayed], after, 0, fn)
    return view


def _gather_in_diagonal(view, relayed, after, name):
    def fn(bufs, pairs, send, recv):
        (send_in, recv_in), = pairs
        x, y, c = _place()
        sibling = (x, y, 1 - c)
        any_quarter = _quarter(_weight_region(bufs[0], 0, 0), 0)
        for j in range(2):
            cp = _remote(any_quarter, send_in.at[j], recv_in.at[j], sibling)
            cp.wait_recv()
            cp.wait_send()

        def go(ss):
            _remote(_weight_region(bufs[0], ss ^ _FLIPS[2], c), send.at[0], recv.at[0], sibling).start()

        _for_my_shard(go)

    sems, (view,) = _comm_call(name, [view], [relayed], after, 1, fn)
    return sems, view


def _gather_in_diagonal_end(view, forwarded, after, name):
    def fn(bufs, pairs, *_):
        (send, recv), = pairs
        x, y, c = _place()
        cp = _remote(_weight_region(bufs[0], 0, 0), send.at[0], recv.at[0], (x, y, 1 - c))
        cp.wait_recv()
        cp.wait_send()

    _, (view,) = _comm_call(name, [view], [forwarded], after, 0, fn)
    return view


def _in_proj_shard(h1, wi, proj, shard_arr, name):
    s, d = h1.shape
    n = wi.shape[1]
    tn = 256
    blocks = n // (N_CHIPS * tn)
    given = [] if proj is None else [proj]

    def body(shard_ref, h_ref, w_ref, *rest):
        del shard_ref
        rest[-1][...] = jnp.dot(h_ref[...], w_ref[...], preferred_element_type=F32)

    grid_spec = pltpu.PrefetchScalarGridSpec(
        num_scalar_prefetch=1, grid=(blocks,),
        in_specs=[pl.BlockSpec((s, d), lambda j, shard_ref: (0, 0)),
                  pl.BlockSpec((d, tn), lambda j, shard_ref: (0, shard_ref[0] * blocks + j))] + [ANY] * len(given),
        out_specs=pl.BlockSpec((s, tn), lambda j, shard_ref: (0, shard_ref[0] * blocks + j)))
    return pl.pallas_call(
        body, name=name, grid_spec=grid_spec,
        out_shape=jax.ShapeDtypeStruct((s, n), F32),
        input_output_aliases={3: 0} if given else {},
        compiler_params=_params(("arbitrary",)),
    )(shard_arr, h1, wi, *given)


def _split_start(name, bufs, n_sems, copies):
    n = len(bufs)

    def body(*refs):
        send_sems, recv_sems = refs[n:n + 2]
        for cp in copies(refs[n + 2:], send_sems, recv_sems):
            cp.start()

    outs = pl.pallas_call(
        body, name=name,
        in_specs=[_HBM] * n, out_specs=[_SEM, _SEM] + [_HBM] * n,
        out_shape=[pltpu.SemaphoreType.DMA((n_sems,)), pltpu.SemaphoreType.DMA((n_sems,))]
        + [pltpu.HBM(b.shape, b.dtype) for b in bufs],
        input_output_aliases={i: 2 + i for i in range(n)},
        compiler_params=pltpu.CompilerParams(has_side_effects=_EFFECT),
    )(*[_in_hbm(b) for b in bufs])
    return outs[0], outs[1], list(outs[2:])


def _split_wait(name, bufs, send_sems, recv_sems, copies, after):
    n = len(bufs)

    def body(*refs):
        send_ref, recv_ref = refs[n:n + 2]
        for cp in copies(refs[n + 3:], send_ref, recv_ref):
            cp.wait()

    return list(pl.pallas_call(
        body, name=name,
        in_specs=[_HBM] * n + [_SEM, _SEM, ANY], out_specs=[_HBM] * n,
        out_shape=[pltpu.HBM(b.shape, b.dtype) for b in bufs],
        input_output_aliases={i: i for i in range(n)},
        compiler_params=pltpu.CompilerParams(has_side_effects=_EFFECT),
    )(*bufs, send_sems, recv_sems, after))


def _halves_copies(n_w):
    def copies(bufs, send_sems, recv_sems):
        x, y, c = _place()
        out = []
        for w in range(n_w):
            view, land = bufs[w], bufs[n_w + w]
            src = view.at[1 - c] if len(view.shape) == 3 else view.at[:, 1 - c]
            out.append(pltpu.make_async_remote_copy(
                src_ref=src, dst_ref=land, send_sem=send_sems.at[w], recv_sem=recv_sems.at[w],
                device_id=(x, y, 1 - c), device_id_type=MESH))
        return out
    return copies


def _pieces_copies(n_w):
    def copies(bufs, send_sems, recv_sems):
        x, y, c = _place()
        out = []
        for w in range(n_w):
            for j, (cx, cy) in enumerate(_other_chips(x, y)):
                out.append(pltpu.make_async_remote_copy(
                    src_ref=bufs[w].at[2 * cx + cy], dst_ref=bufs[n_w + w].at[j],
                    send_sem=send_sems.at[3 * w + j], recv_sem=recv_sems.at[3 * w + j],
                    device_id=(cx, cy, c), device_id_type=MESH))
        return out
    return copies


def _join_copies(n_w):
    def copies(bufs, send_sems, recv_sems):
        x, y, c = _place()
        return [pltpu.make_async_remote_copy(
            src_ref=bufs[w].at[c], dst_ref=bufs[w].at[c], send_sem=send_sems.at[w], recv_sem=recv_sems.at[w],
            device_id=(x, y, 1 - c), device_id_type=MESH) for w in range(n_w)]
    return copies


def _grad_view(g, column_sharded):
    return _weight_view(g, column_sharded)


def _halves_landing(view):
    shape = view.shape[1:] if view.ndim == 3 else (N_CHIPS,) + view.shape[2:]
    return lax.empty(shape, BF16)


def _halves_start(tag, grads, column_sharded):
    views = [_weight_view(g, cs) for g, cs in zip(grads, column_sharded)]
    n = len(views)
    return _split_start("halves_start_" + tag, views + [_halves_landing(v) for v in views], n, _halves_copies(n))


def _halves_wait(tag, state, after):
    send_sems, recv_sems, bufs = state
    n = len(bufs) // 2
    bufs = _split_wait("halves_wait_" + tag, bufs, send_sems, recv_sems, _halves_copies(n), after)
    return bufs[:n], bufs[n:]


def _pieces_start(tag, pieces):
    n = len(pieces)
    landing = [lax.empty((3,) + p.shape[1:], BF16) for p in pieces]
    return _split_start("pieces_start_" + tag, list(pieces) + landing, 3 * n, _pieces_copies(n))


def _pieces_wait(tag, state, after):
    send_sems, recv_sems, bufs = state
    n = len(bufs) // 2
    bufs = _split_wait("pieces_wait_" + tag, bufs, send_sems, recv_sems, _pieces_copies(n), after)
    return bufs[:n], bufs[n:]


def _join_start(tag, shards):
    n = len(shards)
    return _split_start("join_start_" + tag, list(shards), n, _join_copies(n))


def _join_wait(tag, state, after):
    send_sems, recv_sems, bufs = state
    bufs = _split_wait("join_wait_" + tag, bufs, send_sems, recv_sems, _join_copies(len(bufs)), after)
    return [b.reshape(2 * b.shape[1], b.shape[2]) for b in bufs]


def _chip_sum_col(g3, sib, c_arr, name):
    _, hk, n = g3.shape
    cols = n // N_CHIPS
    tr = _row_tile(hk, cols * 2, limit=4 * 1024 * 1024)

    def body(c_ref, g_ref, s_ref, o_ref):
        del c_ref
        o_ref[...] = (g_ref[...].astype(F32) + s_ref[...].astype(F32)).astype(BF16)

    grid_spec = pltpu.PrefetchScalarGridSpec(
        num_scalar_prefetch=1, grid=(N_CHIPS, hk // tr),
        in_specs=[pl.BlockSpec((None, tr, cols), lambda p, r, c_ref: (c_ref[0], r, p)),
                  pl.BlockSpec((tr, cols), lambda p, r, c_ref: (r, p))],
        out_specs=pl.BlockSpec((None, tr, cols), lambda p, r, c_ref: (p, r, 0)))
    return pl.pallas_call(
        body, name=name, grid_spec=grid_spec,
        out_shape=jax.ShapeDtypeStruct((N_CHIPS, hk, cols), BF16),
        compiler_params=_params(("parallel", "parallel")),
    )(c_arr, g3, sib)


def _chip_sum_row(g4, sib, c_arr, name):
    _, _, hr, n = g4.shape
    tr = _row_tile(hr, n * 2, limit=4 * 1024 * 1024)

    def body(c_ref, g_ref, s_ref, o_ref):
        del c_ref
        o_ref[...] = (g_ref[...].astype(F32) + s_ref[...].astype(F32)).astype(BF16)

    grid_spec = pltpu.PrefetchScalarGridSpec(
        num_scalar_prefetch=1, grid=(N_CHIPS, hr // tr),
        in_specs=[pl.BlockSpec((None, None, tr, n), lambda p, r, c_ref: (p, c_ref[0], r, 0)),
                  pl.BlockSpec((None, tr, n), lambda p, r, c_ref: (p, r, 0))],
        out_specs=pl.BlockSpec((None, tr, n), lambda p, r, c_ref: (p, r, 0)))
    return pl.pallas_call(
        body, name=name, grid_spec=grid_spec,
        out_shape=jax.ShapeDtypeStruct((N_CHIPS, hr, n), BF16),
        compiler_params=_params(("parallel", "parallel")),
    )(c_arr, g4, sib)


def _sum_pieces(pieces, received, place_arr, name):
    _, r, n = pieces.shape
    tr = _row_tile(r, n * 4, limit=4 * 1024 * 1024)

    def body(p_ref, own_ref, r0_ref, r1_ref, r2_ref, o_ref):
        del p_ref
        acc = own_ref[...].astype(F32) + r0_ref[...].astype(F32)
        acc = acc + r1_ref[...].astype(F32)
        o_ref[...] = acc + r2_ref[...].astype(F32)

    def recv_spec(j):
        return pl.BlockSpec((None, tr, n), lambda i, p_ref: (j, i, 0))

    grid_spec = pltpu.PrefetchScalarGridSpec(
        num_scalar_prefetch=1, grid=(r // tr,),
        in_specs=[pl.BlockSpec((None, tr, n), lambda i, p_ref: (p_ref[0], i, 0)),
                  recv_spec(0), recv_spec(1), recv_spec(2)],
        out_specs=pl.BlockSpec((None, tr, n), lambda i, p_ref: (p_ref[1], i, 0)))
    return pl.pallas_call(
        body, name=name, grid_spec=grid_spec,
        out_shape=jax.ShapeDtypeStruct((2, r, n), F32),
        compiler_params=_params(("parallel",)),
    )(place_arr, pieces, received, received, received)


def _norm_weights_step(parts, w, m, v, after=()):
    rows, d = parts.shape
    after = tuple(after)

    def body(p_ref, w_ref, m_ref, v_ref, *rest):
        g_ref, d_ref, mo_ref, vo_ref, gathered, send_sems, recv_sems = rest[len(after):]
        x, y, c = _place()
        me = 4 * x + 2 * y + c
        gathered[me] = p_ref[...]
        copies = []
        for k in range(1, N_DEV):
            peer = (x ^ ((k >> 2) & 1), y ^ ((k >> 1) & 1), c ^ (k & 1))
            copies.append(pltpu.make_async_remote_copy(
                src_ref=p_ref, dst_ref=gathered.at[me], send_sem=send_sems.at[k - 1],
                recv_sem=recv_sems.at[k - 1], device_id=peer, device_id_type=MESH))
        for cp in copies:
            cp.start()
        for cp in copies:
            cp.wait()
        g = gathered[0]
        for k in range(1, N_DEV):
            g = g + gathered[k]
        delta, m_new, v_new = _adamw_math(w_ref[...], g, m_ref[...], v_ref[...])
        g_ref[...] = g
        d_ref[...] = delta
        mo_ref[...] = m_new
        vo_ref[...] = v_new

    vmem = pl.BlockSpec(memory_space=pltpu.VMEM)
    shp = jax.ShapeDtypeStruct((rows, d), F32)
    return pl.pallas_call(
        body, name="norm_weights_step",
        in_specs=[vmem] * 4 + [ANY] * len(after), out_specs=[vmem] * 4, out_shape=[shp] * 4,
        scratch_shapes=[pltpu.VMEM((N_DEV, rows, d), F32), pltpu.SemaphoreType.DMA((N_DEV - 1,)),
                        pltpu.SemaphoreType.DMA((N_DEV - 1,))],
        compiler_params=pltpu.CompilerParams(has_side_effects=True),
    )(parts, w, m, v, *after)


def kernel(x, norm_mix_w, w_in, w_out, norm_ffn_w, w_gate, w_up, w_down, norm_final_w, loss_target, m_norm_mix_w, m_w_in, m_w_out, m_norm_ffn_w, m_w_gate, m_w_up, m_w_down, m_norm_final_w, v_norm_mix_w, v_w_in, v_w_out, v_norm_ffn_w, v_w_gate, v_w_up, v_w_down, v_norm_final_w):
    s, d = x.shape[1], x.shape[2]
    xs = x.reshape(s, d)
    target = loss_target.reshape(s, d)
    big = {"w_in": (w_in, m_w_in, v_w_in), "w_out": (w_out, m_w_out, v_w_out),
           "w_gate": (w_gate, m_w_gate, v_w_gate), "w_up": (w_up, m_w_up, v_w_up),
           "w_down": (w_down, m_w_down, v_w_down)}
    big = {k: tuple(a.reshape(a.shape[1:]) for a in t) for k, t in big.items()}
    col_names, row_names = ("w_in", "w_gate", "w_up"), ("w_out", "w_down")
    n_in = N_CHIPS * big["w_in"][0].shape[1]
    ffn = N_CHIPS * big["w_gate"][0].shape[1]
    mix = ATTN_WIDTH + RET_WIDTH
    c_arr = lax.axis_index("c").astype(I32).reshape(1)
    shard_arr = (2 * lax.axis_index("x") + lax.axis_index("y")).astype(I32).reshape(1)
    place_arr = jnp.concatenate([shard_arr, c_arr])

    def cast(k, after=()):
        return _weight_view(_cast_into_full(big[k][0], shard_arr, k in col_names, "cast_" + k, after), k in col_names)

    started_in, v_in = _gather_in_start(cast("w_in"), "gather_in_start")

    sec = ATTN_WIDTH

    def section(p, rows):
        return pl.BlockSpec((None, rows, sec), lambda i, j, kk: (p, i, 0))

    h1 = _rms_fwd(xs, norm_mix_w, "rms_mix_fwd")
    my_shard = shard_arr[0]
    shard_of = [jnp.bitwise_xor(my_shard, f).astype(I32).reshape(1) for f in (0,) + _FLIPS]
    proj = _in_proj_shard(h1, _weight_unview(v_in), None, shard_of[0], "in_proj_own")
    early_views = [cast(k, after=[proj]) for k in ("w_out", "w_gate")]
    relayed_in, v_in = _gather_in_relay(v_in, started_in, early_views, "gather_in_relay")
    late_views = [cast(k, after=[v_in]) for k in ("w_up", "w_down")]
    send_sems, recv_sems, v_out, v_gate, v_up, v_down = _gather_start(early_views + late_views, "gather_start_rest")
    v_in = _gather_in_neighbours_end(v_in, relayed_in, [v_out], "gather_in_neighbours_end")
    proj = _in_proj_shard(h1, _weight_unview(v_in), proj, shard_of[1], "in_proj_x")
    proj = _in_proj_shard(h1, _weight_unview(v_in), proj, shard_of[2], "in_proj_y")
    forwarded_in, v_in = _gather_in_diagonal(v_in, relayed_in, [proj], "gather_in_diagonal")
    wi = _weight_unview(_gather_in_diagonal_end(v_in, forwarded_in, [proj], "gather_in_diagonal_end"))
    proj = _in_proj_shard(h1, wi, proj, shard_of[3], "in_proj_diagonal")
    fs_o, fr_o, v_out = _gather_forward([v_out], [0], send_sems, recv_sems, proj, "gather_forward_out")
    mixed, attn_o, lse = _attn_fwd(proj, after=[v_out])
    fs_g, fr_g, v_gate = _gather_forward([v_gate], [1], send_sems, recv_sems, attn_o, "gather_forward_gate")
    mixed, ret_raw = _ret_fwd(proj, mixed, after=[v_gate])
    wo, = _gather_end([v_out], fs_o, fr_o, ret_raw, "gather_end_out")
    x1, = _matmul("out_proj", "nn", [mixed, mixed], [wo, wo], [0, 0], s, d, sec, s, 512, sec, [xs], [F32],
                  _epi_residual, b_koff=[0, 1], a_specs=[section(0, s), section(1, s)])
    h2 = _rms_fwd(x1, norm_ffn_w, "rms_ffn_fwd")
    fs_u, fr_u, v_up = _gather_forward([v_up], [2], send_sems, recv_sems, h2, "gather_forward_up")
    wg, = _gather_end([v_gate], fs_g, fr_g, v_up, "gather_end_gate")
    wu, = _gather_end([v_up], fs_u, fr_u, wg, "gather_end_up")
    gate, up, act = _matmul("gate_up", "nn", [h2, h2], [wg, wu], [0, 1], s, ffn, d, s, 256, d, [],
                            [BF16, BF16, BF16], _epi_swiglu)
    fs, fr, v_down = _gather_forward([v_down], [3], send_sems, recv_sems, act, "gather_forward_down")
    wd, = _gather_end([v_down], fs, fr, act, "gather_end_down")
    x2, = _matmul("down_proj", "nn", [act], [wd], [0], s, d, ffn, s // 2, 256, ffn, [x1], [F32],
                  _epi_residual)
    loss_row, dx2, dx2b, dwf = _final_norm_loss(x2, norm_final_w.reshape(1, d), target, "final_norm_loss")

    names = col_names + row_names
    grads, new = {}, {}

    def chip_sums(tag_names, views, sibs):
        return [(_chip_sum_col if k in col_names else _chip_sum_row)(v, sb, c_arr, "chip_sum_" + k)
                for k, v, sb in zip(tag_names, views, sibs)]

    def piece_sums(tag_names, pieces, received):
        return [_sum_pieces(p, r, place_arr, "sum_pieces_" + k) for k, p, r in zip(tag_names, pieces, received)]

    def update(k):
        new[k] = _adamw(big[k][0], grads[k], big[k][1], big[k][2], "adamw_" + k)

    dgate, dup = _matmul("d_act", "nt", [dx2b], [wd], [0], s, ffn, d, s, 256, d, [gate, up],
                         [BF16, BF16], _epi_swiglu_bwd)
    g_wd, = _matmul("g_w_down", "tn", [act], [dx2b], [0], ffn, d, s, 512, d, s, [], [BF16], _epi_plain)
    halves_d = _halves_start("down", [g_wd], [False])
    dh2, = _matmul("d_h2", "nt", [dgate, dup], [wg, wu], [0, 0], s, d, ffn, s // 2, 256, ffn, [], [F32],
                   _epi_plain, after=halves_d[2][-1:], a_single_buffer=True)
    pieces_d = _pieces_start("down", chip_sums(["w_down"], *_halves_wait("down", halves_d, dh2)))
    g_wg, g_wu = _matmul("g_w_gate_up", "tn", [h2, h2], [dgate, dup], [0, 1], d, ffn, s, 1024, 512, s, [],
                         [BF16, BF16], _epi_two, after=pieces_d[2][-1:])
    halves_gu = _halves_start("gate_up", [g_wg, g_wu], [True, True])
    dx1, dx1b, dw_ffn = _rms_bwd(x1, norm_ffn_w, dh2, dx2, "rms_ffn_bwd", after=halves_gu[2][-1:])

    dmixed, = _matmul("d_mixed", "nt", [dx1b], [wo], [0], s, mix, d, s, 512, d, [], [F32], _epi_plain)
    pieces_gu = _pieces_start("gate_up", chip_sums(["w_gate", "w_up"], *_halves_wait("gate_up", halves_gu, dmixed)))
    per = sec // 512
    g_wo, = _matmul("g_w_out", "tn", [mixed], [dx1b], [0], mix, d, s, 512, d, s, [], [BF16], _epi_plain,
                    after=pieces_gu[2][-1:],
                    a_specs=[pl.BlockSpec((None, s, 512), lambda i, j, kk: (i // per, 0, i % per))])
    halves_o = _halves_start("out", [g_wo], [False])
    dsec = _attn_bwd(proj, attn_o, lse, dmixed, after=halves_o[2][-1:])
    pieces_o = _pieces_start("out", chip_sums(["w_out"], *_halves_wait("out", halves_o, dsec)))
    dsec = _ret_bwd(proj, ret_raw, dmixed, dsec, after=pieces_o[2][-1:])
    where = [0, 1, 2, 4, 5, 6, 7]
    n_sec = len(where)
    g_wi, = _matmul("g_w_in", "tn", [h1], [dsec], [0], d, n_in, s, 1024, sec, s, [], [BF16], _epi_plain,
                    b_specs=[pl.BlockSpec((None, s, sec), lambda i, j, kk: (j + (j >= 3).astype(I32), 0, 0))])
    halves_i = _halves_start("in", [g_wi], [True])
    dh1, = _matmul("d_h1", "nt", [dsec] * n_sec, [wi] * n_sec, [0] * n_sec, s, d, sec, s // 2, 256, sec, [], [F32],
                   _epi_plain, b_koff=list(range(n_sec)), after=halves_i[2][-1:],
                   a_specs=[section(p, s // 2) for p in where])
    pieces_i = _pieces_start("in", chip_sums(["w_in"], *_halves_wait("in", halves_i, dh1)))
    grad_x, _, dw_mix = _rms_bwd(xs, norm_mix_w, dh1, dx1, "rms_mix_bwd", after=pieces_i[2][-1:])

    def rows8(*vs):
        return jnp.concatenate([v.reshape(1, d) for v in vs] + [jnp.zeros((8 - len(vs), d), F32)], axis=0)

    join_d = _join_start("down", piece_sums(["w_down"], *_pieces_wait("down", pieces_d, grad_x)))
    join_gu = _join_start("gate_up", piece_sums(["w_gate", "w_up"], *_pieces_wait("gate_up", pieces_gu, join_d[2][0])))
    join_o = _join_start("out", piece_sums(["w_out"], *_pieces_wait("out", pieces_o, join_gu[2][0])))
    grads["w_down"], = _join_wait("down", join_d, join_o[2][0])
    update("w_down")
    grads["w_gate"], grads["w_up"] = _join_wait("gate_up", join_gu, new["w_down"][0])
    update("w_gate")
    update("w_up")
    grads["w_out"], = _join_wait("out", join_o, new["w_up"][0])
    update("w_out")
    join_i = _join_start("in", piece_sums(["w_in"], *_pieces_wait("in", pieces_i, new["w_out"][0])))
    ng, nd, nm, nv = _norm_weights_step(
        rows8(dw_mix, dw_ffn, dwf, jnp.broadcast_to(loss_row[:, :1], (1, d))),
        rows8(norm_mix_w, norm_ffn_w, norm_final_w),
        rows8(m_norm_mix_w, m_norm_ffn_w, m_norm_final_w), rows8(v_norm_mix_w, v_norm_ffn_w, v_norm_final_w),
        after=join_i[2][:1])
    grads["w_in"], = _join_wait("in", join_i, ng)
    update("w_in")

    loss = ng[3, 0]

    def pack(small, per_weight):
        lead = lambda a: a.reshape((1,) + a.shape)
        return (small[0:1], lead(per_weight["w_in"]), lead(per_weight["w_out"]), small[1:2],
                lead(per_weight["w_gate"]), lead(per_weight["w_up"]), lead(per_weight["w_down"]), small[2])

    return (loss, grad_x.reshape(1, s, d),
            *pack(ng, {k: new[k][3] for k in names}),
            *pack(nd, {k: new[k][0] for k in names}),
            *pack(nm, {k: new[k][1] for k in names}),
            *pack(nv, {k: new[k][2] for k in names}))
```

```python
import functools
import math

import jax
import jax.numpy as jnp
from jax import lax
from jax.experimental import pallas as pl
from jax.experimental.pallas import tpu as pltpu

F32 = jnp.float32
BF16 = jnp.bfloat16
I32 = jnp.int32
MESH = pl.DeviceIdType.MESH
ANY = pl.BlockSpec(memory_space=pl.ANY)

ATTN_HEADS = 8
ATTN_HEAD_DIM = 128
RET_HEADS = 4
RET_HEAD_DIM = 256
ATTN_WIDTH = ATTN_HEADS * ATTN_HEAD_DIM
RET_WIDTH = RET_HEADS * RET_HEAD_DIM
DILATED_PATTERNS = ((128, 1), (512, 4), (2048, 16))
NORM_EPS = 1e-6
ADAM_LR = 0.001
ADAM_B1 = 0.9
ADAM_B2 = 0.999
ADAM_EPS = 1e-08
ADAM_WD = 0.01
ADAM_STEP = 10

N_CHIPS = 4
N_DEV = 8
NEG_BIG = -1e30
SEQ_TILE = 512
ATTN_FWD_HEADS_PER_STEP = 2
ATTN_HEADS_PER_STEP = 1
VMEM_LIMIT_BYTES = 56 * 1024 * 1024


def _params(semantics=None, vmem=VMEM_LIMIT_BYTES):
    return pltpu.CompilerParams(dimension_semantics=semantics, vmem_limit_bytes=vmem)


def _row_tile(rows, row_bytes, limit=2 * 1024 * 1024, mult=16):
    best = None
    for t in range(mult, rows + 1, mult):
        if rows % t == 0 and t * row_bytes <= limit:
            best = t
    assert best is not None, (rows, row_bytes)
    return best


def _sigmoid(x):
    return 1.0 / (1.0 + jnp.exp(-x))


def _select_by_index(idx, values):
    out = jnp.float32(values[-1])
    for i in range(len(values) - 2, -1, -1):
        out = jnp.where(idx == i, jnp.float32(values[i]), out)
    return out


def _place():
    x, y, c = lax.axis_index("x"), lax.axis_index("y"), lax.axis_index("c")
    return x, y, c


def _cast_into_full(w, shard_arr, column_sharded, name, after=()):
    after = tuple(after)
    rows, cols = w.shape
    tr = _row_tile(rows, cols * 4)
    steps = rows // tr
    if column_sharded:
        out_shape, out_map = (rows, N_CHIPS * cols), (lambda i, s_ref: (i, s_ref[0]))
    else:
        out_shape, out_map = (N_CHIPS * rows, cols), (lambda i, s_ref: (s_ref[0] * steps + i, 0))

    def body(s_ref, w_ref, *rest):
        del s_ref
        rest[-1][...] = w_ref[...].astype(BF16)

    grid_spec = pltpu.PrefetchScalarGridSpec(
        num_scalar_prefetch=1, grid=(steps,),
        in_specs=[pl.BlockSpec((tr, cols), lambda i, s_ref: (i, 0))] + [ANY] * len(after),
        out_specs=pl.BlockSpec((tr, cols), out_map))
    return pl.pallas_call(
        body, name=name, grid_spec=grid_spec,
        out_shape=jax.ShapeDtypeStruct(out_shape, BF16),
        compiler_params=_params(("parallel",)),
    )(shard_arr, w, *after)


def _rms_fwd(x, w, name):
    rows, d = x.shape
    tr = 256

    def body(x_ref, w_ref, h_ref):
        xv = x_ref[...]
        r = lax.rsqrt(jnp.mean(xv * xv, axis=-1, keepdims=True) + NORM_EPS)
        h_ref[...] = (xv * r * w_ref[...]).astype(BF16)

    return pl.pallas_call(
        body, name=name, grid=(rows // tr,),
        in_specs=[pl.BlockSpec((tr, d), lambda i: (i, 0)), pl.BlockSpec((1, d), lambda i: (0, 0))],
        out_specs=pl.BlockSpec((tr, d), lambda i: (i, 0)),
        out_shape=jax.ShapeDtypeStruct((rows, d), BF16),
        compiler_params=_params(("parallel",)),
    )(x, w)


def _rms_bwd(x, w, dh, dres, name, after=()):
    rows, d = x.shape
    tr = 256
    after = tuple(after)

    def body(x_ref, w_ref, dh_ref, dres_ref, *rest):
        dx_ref, dxb_ref, dw_ref = rest[len(after):]
        xv = x_ref[...]
        r = lax.rsqrt(jnp.mean(xv * xv, axis=-1, keepdims=True) + NORM_EPS)
        xhat = xv * r
        dy = dh_ref[...]
        dxhat = dy * w_ref[...]
        dx = dres_ref[...] + r * (dxhat - xhat * jnp.mean(dxhat * xhat, axis=-1, keepdims=True))
        dx_ref[...] = dx
        dxb_ref[...] = dx.astype(BF16)
        part = jnp.sum(dy * xhat, axis=0, keepdims=True)

        @pl.when(pl.program_id(0) == 0)
        def _():
            dw_ref[...] = part

        @pl.when(pl.program_id(0) != 0)
        def _():
            dw_ref[...] += part

    row = pl.BlockSpec((tr, d), lambda i: (i, 0))
    vec = pl.BlockSpec((1, d), lambda i: (0, 0))
    return pl.pallas_call(
        body, name=name, grid=(rows // tr,),
        in_specs=[row, vec, row, row] + [ANY] * len(after),
        out_specs=[row, row, vec],
        out_shape=[jax.ShapeDtypeStruct((rows, d), F32), jax.ShapeDtypeStruct((rows, d), BF16),
                   jax.ShapeDtypeStruct((1, d), F32)],
        compiler_params=_params(("arbitrary",)),
    )(x, w, dh, dres, *after)


def _final_norm_loss(x2, w, target, name):
    rows, d = x2.shape
    tr = 256

    def body(x_ref, w_ref, t_ref, loss_ref, dx_ref, dxb_ref, dw_ref):
        xv = x_ref[...]
        wv = w_ref[...]
        r = lax.rsqrt(jnp.mean(xv * xv, axis=-1, keepdims=True) + NORM_EPS)
        xhat = xv * r
        err = xhat * wv - t_ref[...]
        part_loss = 0.5 * jnp.sum(jnp.mean(err * err, axis=-1, keepdims=True), axis=0, keepdims=True)
        dy = err * (1.0 / d)
        dxhat = dy * wv
        dx = r * (dxhat - xhat * jnp.mean(dxhat * xhat, axis=-1, keepdims=True))
        dx_ref[...] = dx
        dxb_ref[...] = dx.astype(BF16)
        part_dw = jnp.sum(dy * xhat, axis=0, keepdims=True)
        part_loss = jnp.broadcast_to(part_loss, (1, 128))

        @pl.when(pl.program_id(0) == 0)
        def _():
            dw_ref[...] = part_dw
            loss_ref[...] = part_loss

        @pl.when(pl.program_id(0) != 0)
        def _():
            dw_ref[...] += part_dw
            loss_ref[...] += part_loss

    row = pl.BlockSpec((tr, d), lambda i: (i, 0))
    vec = pl.BlockSpec((1, d), lambda i: (0, 0))
    return pl.pallas_call(
        body, name=name, grid=(rows // tr,),
        in_specs=[row, vec, row],
        out_specs=[pl.BlockSpec((1, 128), lambda i: (0, 0)), row, row, vec],
        out_shape=[jax.ShapeDtypeStruct((1, 128), F32), jax.ShapeDtypeStruct((rows, d), F32),
                   jax.ShapeDtypeStruct((rows, d), BF16), jax.ShapeDtypeStruct((1, d), F32)],
        compiler_params=_params(("arbitrary",)),
    )(x2, w, target)


def _adamw_math(w, g, m, v):
    m = ADAM_B1 * m + (1.0 - ADAM_B1) * g
    v = ADAM_B2 * v + (1.0 - ADAM_B2) * (g * g)
    m_hat = m / (1.0 - ADAM_B1 ** ADAM_STEP)
    v_hat = v / (1.0 - ADAM_B2 ** ADAM_STEP)
    delta = -ADAM_LR * (m_hat / (jnp.sqrt(v_hat) + ADAM_EPS) + ADAM_WD * w)
    return delta, m, v


def _adamw(w, g, m, v, name):
    rows, cols = w.shape
    tr = _row_tile(rows, cols * 4)

    def body(w_ref, g_ref, m_ref, v_ref, d_ref, mo_ref, vo_ref, go_ref):
        g = g_ref[...]
        delta, m_new, v_new = _adamw_math(w_ref[...], g, m_ref[...], v_ref[...])
        d_ref[...] = delta
        mo_ref[...] = m_new
        vo_ref[...] = v_new
        go_ref[...] = g

    blk = pl.BlockSpec((tr, cols), lambda i: (i, 0))
    shp = jax.ShapeDtypeStruct((rows, cols), F32)
    return pl.pallas_call(
        body, name=name, grid=(rows // tr,),
        in_specs=[blk] * 4, out_specs=[blk] * 4, out_shape=[shp] * 4,
        compiler_params=_params(("parallel",)),
    )(w, g, m, v)


_DOT_DIMS = {"nn": ((1,), (0,)), "nt": ((1,), (1,)), "tn": ((0,), (0,))}


def _matmul(name, mode, a_list, b_list, acc_of, m, n, k, tm, tn, tk, extras, out_dtypes, epilogue,
            a_koff=None, b_koff=None, after=(), a_specs=None, b_specs=None, a_single_buffer=False):
    after = tuple(after)
    assert m % tm == 0 and n % tn == 0 and k % tk == 0, (name, m, n, k, tm, tn, tk)
    nk = k // tk
    n_acc = max(acc_of) + 1
    n_pairs = len(a_list)
    a_koff = a_koff or [0] * n_pairs
    b_koff = b_koff or [0] * n_pairs
    dims = (_DOT_DIMS[mode], ((), ()))
    n_ext, n_out = len(extras), len(out_dtypes)

    def body(*refs):
        a_refs = refs[:n_pairs]
        b_refs = refs[n_pairs:2 * n_pairs]
        e_refs = refs[2 * n_pairs:2 * n_pairs + n_ext]
        first_out = 2 * n_pairs + n_ext + len(after)
        o_refs = refs[first_out:first_out + n_out]
        acc_refs = refs[first_out + n_out:]

        parts = [None] * n_acc
        for p in range(n_pairs):
            d = lax.dot_general(a_refs[p][...], b_refs[p][...], dims, preferred_element_type=F32)
            parts[acc_of[p]] = d if parts[acc_of[p]] is None else parts[acc_of[p]] + d

        def finish(accs):
            outs = epilogue(accs, [e[...] for e in e_refs])
            for o_ref, o in zip(o_refs, outs):
                o_ref[...] = o.astype(o_ref.dtype)

        if nk == 1:
            finish(parts)
        else:
            kk = pl.program_id(2)

            @pl.when(kk == 0)
            def _():
                for acc_ref, part in zip(acc_refs, parts):
                    acc_ref[...] = part

            @pl.when(kk != 0)
            def _():
                for acc_ref, part in zip(acc_refs, parts):
                    acc_ref[...] += part

            @pl.when(kk == nk - 1)
            def _():
                finish([acc_ref[...] for acc_ref in acc_refs])

    def a_spec(off):
        mode_a = pl.Buffered(1) if a_single_buffer else None
        if mode == "tn":
            return pl.BlockSpec((tk, tm), lambda i, j, kk: (kk + off, i), pipeline_mode=mode_a)
        return pl.BlockSpec((tm, tk), lambda i, j, kk: (i, kk + off), pipeline_mode=mode_a)

    def b_spec(off):
        if mode == "nt":
            return pl.BlockSpec((tn, tk), lambda i, j, kk: (j, kk + off))
        return pl.BlockSpec((tk, tn), lambda i, j, kk: (kk + off, j))

    tile = pl.BlockSpec((tm, tn), lambda i, j, kk: (i, j))
    scratch = [pltpu.VMEM((tm, tn), F32) for _ in range(n_acc)] if nk > 1 else []
    return pl.pallas_call(
        body, name=name, grid=(m // tm, n // tn, nk),
        in_specs=(a_specs or [a_spec(o) for o in a_koff]) + (b_specs or [b_spec(o) for o in b_koff])
        + [tile] * n_ext + [ANY] * len(after),
        out_specs=[tile] * n_out,
        out_shape=[jax.ShapeDtypeStruct((m, n), dt) for dt in out_dtypes],
        scratch_shapes=scratch,
        compiler_params=_params(("parallel", "parallel", "arbitrary")),
    )(*a_list, *b_list, *extras, *after)


def _epi_plain(accs, extras):
    return (accs[0],)


def _epi_residual(accs, extras):
    return (accs[0] + extras[0],)


def _epi_two(accs, extras):
    return accs[0], accs[1]


def _epi_swiglu(accs, extras):
    g, u = accs
    return g, u, g * _sigmoid(g) * u


def _epi_swiglu_bwd(accs, extras):
    da = accs[0]
    g, u = (e.astype(F32) for e in extras)
    sg = _sigmoid(g)
    dg = da * u * sg * (1.0 + g * (1.0 - sg))
    du = da * g * sg
    return dg, du


_NT_DIMS = (((1,), (1,)), ((), ()))
_TN_DIMS = (((0,), (0,)), ((), ()))


def _tile_delta(tq, tk):
    return lax.broadcasted_iota(I32, (tq, tk), 0) - lax.broadcasted_iota(I32, (tq, tk), 1)


def _attn_log_count(delta):
    count = jnp.zeros(delta.shape, I32)
    for window, dilation in DILATED_PATTERNS:
        hit = ((delta & (dilation - 1)) == 0) & (delta <= window)
        count = count + jnp.where(hit, 1, 0)
    valid = (delta >= 0) & (count > 0)
    logm = jnp.where(count == 3, math.log(3.0), jnp.where(count == 2, math.log(2.0), 0.0))
    return jnp.where(valid, logm, NEG_BIG)


def _fill_attn_log_count(tab_ref):
    nb, t, _ = tab_ref.shape
    base = _tile_delta(t, t)
    for b in range(nb):
        tab_ref[b] = _attn_log_count(base + b * t)


def _fill_attn_bias(tab_ref, log_count_ref, slope):
    nb, t, _ = tab_ref.shape
    dist = _tile_delta(t, t).astype(F32)
    for b in range(nb):
        tab_ref[b] = log_count_ref[b] - slope * (dist + float(b * t))


def _fill_ret_decay(tab_ref, log_gamma):
    nb, t, _ = tab_ref.shape
    base = _tile_delta(t, t)
    for b in range(nb):
        tab_ref[b] = _ret_decay(base + b * t, log_gamma)


def _alibi_slopes():
    return [2.0 ** (-8.0 * (h + 1) / ATTN_HEADS) for h in range(ATTN_HEADS)]


def _attn_fwd(proj, after=()):
    s = proj.shape[0]
    t = SEQ_TILE
    hd = ATTN_HEAD_DIM
    hp = ATTN_FWD_HEADS_PER_STEP
    ng = ATTN_HEADS // hp
    w = hp * hd
    scale = 1.0 / math.sqrt(hd)
    slopes = _alibi_slopes()

    def body(q_ref, k_ref, v_ref, *rest):
        mix_ref, o_ref, lse_ref, kb, vb, bias_tab, log_count_tab = rest[len(after):]
        g = pl.program_id(0)
        i = pl.program_id(1)

        @pl.when((g == 0) & (i == 0))
        def _():
            _fill_attn_log_count(log_count_tab)

        @pl.when(i == 0)
        def _():
            kb[...] = k_ref[...].astype(BF16)
            vb[...] = v_ref[...].astype(BF16)
            for u in range(hp):
                _fill_attn_bias(bias_tab.at[u], log_count_tab, _select_by_index(g * hp + u, slopes))

        qs = [q_ref[:, u * hd:(u + 1) * hd].astype(BF16) for u in range(hp)]

        def step(j, carry):
            rows = pl.ds(pl.multiple_of(j * t, t), t)
            out = []
            for u in range(hp):
                m_i, l_i, acc = carry[u]
                lanes = slice(u * hd, (u + 1) * hd)
                sc = lax.dot_general(qs[u], kb[rows, lanes], _NT_DIMS, preferred_element_type=F32) * scale
                sc = sc + bias_tab[u, i - j]
                m_new = jnp.maximum(m_i, jnp.max(sc, axis=-1, keepdims=True))
                p = jnp.exp(sc - m_new)
                alpha = jnp.exp(m_i - m_new)
                l_new = alpha * l_i + jnp.sum(p, axis=-1, keepdims=True)
                acc = alpha * acc + jnp.dot(p.astype(BF16), vb[rows, lanes], preferred_element_type=F32)
                out.append((m_new, l_new, acc))
            return tuple(out)

        init = (jnp.full((t, 1), NEG_BIG, F32), jnp.zeros((t, 1), F32), jnp.zeros((t, hd), F32))
        final = lax.fori_loop(0, i + 1, step, (init,) * hp)
        for u in range(hp):
            m_i, l_i, acc = final[u]
            lanes = slice(u * hd, (u + 1) * hd)
            out = acc / l_i
            o_ref[:, lanes] = out
            mix_ref[:, lanes] = out.astype(BF16)
            lse_ref[:, lanes] = jnp.broadcast_to(m_i + jnp.log(l_i), (t, hd))

    return pl.pallas_call(
        body, name="attn_fwd", grid=(ng, s // t),
        in_specs=[pl.BlockSpec((t, w), lambda g, i: (i, g)),
                  pl.BlockSpec((s, w), lambda g, i: (0, ng + g)),
                  pl.BlockSpec((s, w), lambda g, i: (0, 2 * ng + g))] + [ANY] * len(after),
        out_specs=[pl.BlockSpec((None, t, w), lambda g, i: (0, i, g))] + [pl.BlockSpec((t, w), lambda g, i: (i, g))] * 2,
        out_shape=[jax.ShapeDtypeStruct((2, s, ATTN_WIDTH), BF16),
                   jax.ShapeDtypeStruct((s, ATTN_WIDTH), F32),
                   jax.ShapeDtypeStruct((s, ATTN_WIDTH), F32)],
        scratch_shapes=[pltpu.VMEM((s, w), BF16), pltpu.VMEM((s, w), BF16), pltpu.VMEM((hp, s // t, t, t), F32),
                        pltpu.VMEM((s // t, t, t), F32)],
        compiler_params=_params(("arbitrary", "arbitrary")),
    )(proj, proj, proj, *after)


def _attn_bwd(proj, attn_out, lse, dmixed, after=()):
    after = tuple(after)
    s = proj.shape[0]
    t = SEQ_TILE
    nt = s // t
    hd = ATTN_HEAD_DIM
    hp = ATTN_HEADS_PER_STEP
    ng = ATTN_HEADS // hp
    w = hp * hd
    scale = 1.0 / math.sqrt(hd)
    slopes = _alibi_slopes()

    def body(q_ref, k_ref, v_ref, o_ref, lse_ref, do_ref, *rest):
        dsec_ref, qb, kb, vb, dob, dsum, dq_acc, bias_tab, log_count_tab = rest[len(after):]
        g = pl.program_id(0)

        @pl.when(g == 0)
        def _():
            _fill_attn_log_count(log_count_tab)

        qb[...] = q_ref[...].astype(BF16)
        kb[...] = k_ref[...].astype(BF16)
        vb[...] = v_ref[...].astype(BF16)
        dob[...] = do_ref[...].astype(BF16)
        for u in range(hp):
            lanes = slice(u * hd, (u + 1) * hd)
            _fill_attn_bias(bias_tab.at[u], log_count_tab, _select_by_index(g * hp + u, slopes))
            rowsum = jnp.sum(do_ref[:, lanes] * o_ref[:, lanes], axis=-1, keepdims=True)
            dsum[:, lanes] = jnp.broadcast_to(rowsum, (s, hd))
        dq_acc[...] = jnp.zeros((s, w), F32)

        def over_keys(j, _):
            krows = pl.ds(pl.multiple_of(j * t, t), t)

            def over_queries(i, carry):
                qrows = pl.ds(pl.multiple_of(i * t, t), t)
                out = []
                for u in range(hp):
                    dk, dv = carry[u]
                    lanes = slice(u * hd, (u + 1) * hd)
                    qi, doi = qb[qrows, lanes], dob[qrows, lanes]
                    kj, vj = kb[krows, lanes], vb[krows, lanes]
                    lse_i = lse_ref[qrows, lanes][:, :1]
                    dsum_i = dsum[qrows, lanes][:, :1]
                    sc = lax.dot_general(qi, kj, _NT_DIMS, preferred_element_type=F32) * scale
                    p = jnp.exp(sc + bias_tab[u, i - j] - lse_i)
                    dp = lax.dot_general(doi, vj, _NT_DIMS, preferred_element_type=F32)
                    ds = (p * (dp - dsum_i)).astype(BF16)
                    dv = dv + lax.dot_general(p.astype(BF16), doi, _TN_DIMS, preferred_element_type=F32)
                    dk = dk + lax.dot_general(ds, qi, _TN_DIMS, preferred_element_type=F32)
                    dq_acc[qrows, lanes] += jnp.dot(ds, kj, preferred_element_type=F32)
                    out.append((dk, dv))
                return tuple(out)

            zero = jnp.zeros((t, hd), F32)
            final = lax.fori_loop(j, nt, over_queries, ((zero, zero),) * hp)
            for u in range(hp):
                lanes = slice(u * hd, (u + 1) * hd)
                dsec_ref[1, krows, lanes] = (final[u][0] * scale).astype(BF16)
                dsec_ref[2, krows, lanes] = final[u][1].astype(BF16)
            return 0

        lax.fori_loop(0, nt, over_keys, 0)
        dsec_ref[0] = (dq_acc[...] * scale).astype(BF16)

    def col(off):
        return pl.BlockSpec((s, w), lambda g: (0, off + g))

    return pl.pallas_call(
        body, name="attn_bwd", grid=(ng,),
        in_specs=[col(0), col(ng), col(2 * ng), col(0), col(0), col(0)] + [ANY] * len(after),
        out_specs=pl.BlockSpec((4, s, w), lambda g: (0, 0, g)),
        out_shape=jax.ShapeDtypeStruct((8, s, ATTN_WIDTH), BF16),
        scratch_shapes=[pltpu.VMEM((s, w), BF16)] * 4 + [pltpu.VMEM((s, w), F32)] * 2
        + [pltpu.VMEM((hp, nt, t, t), F32), pltpu.VMEM((nt, t, t), F32)],
        compiler_params=_params(("arbitrary",)),
    )(proj, proj, proj, attn_out, lse, dmixed, *after)


def _ret_log_gammas():
    return [math.log(1.0 - 2.0 ** (-5.0 - h)) for h in range(RET_HEADS)]


def _ret_decay(delta, log_gamma):
    dec = jnp.exp(delta.astype(F32) * log_gamma) * (1.0 / math.sqrt(RET_HEAD_DIM))
    return jnp.where(delta >= 0, dec, 0.0)


def _ret_fwd(proj, mixed, after=()):
    after = tuple(after)
    s = proj.shape[0]
    t = SEQ_TILE
    hd = RET_HEAD_DIM
    nh = RET_HEADS
    log_gammas = _ret_log_gammas()
    c0 = 3 * ATTN_WIDTH // hd

    def body(q_ref, k_ref, v_ref, g_ref, *rest):
        mix_ref, raw_ref, kb, vb, decay_tab = rest[1 + len(after):]
        h = pl.program_id(0)
        i = pl.program_id(1)

        @pl.when(i == 0)
        def _():
            kb[...] = k_ref[...].astype(BF16)
            vb[...] = v_ref[...].astype(BF16)
            _fill_ret_decay(decay_tab, _select_by_index(h, log_gammas))

        q = q_ref[...].astype(BF16)

        def step(j, acc):
            rows = pl.ds(pl.multiple_of(j * t, t), t)
            sc = lax.dot_general(q, kb[rows, :], _NT_DIMS, preferred_element_type=F32) * decay_tab[i - j]
            return acc + jnp.dot(sc.astype(BF16), vb[rows, :], preferred_element_type=F32)

        ret = lax.fori_loop(0, i + 1, step, jnp.zeros((t, hd), F32))
        raw_ref[...] = ret
        r = lax.rsqrt(jnp.mean(ret * ret, axis=-1, keepdims=True) + NORM_EPS)
        g = g_ref[...]
        mix_ref[...] = (g * _sigmoid(g) * (ret * r)).astype(BF16)

    return pl.pallas_call(
        body, name="ret_fwd", grid=(nh, s // t),
        in_specs=[pl.BlockSpec((t, hd), lambda h, i: (i, c0 + h)),
                  pl.BlockSpec((s, hd), lambda h, i: (0, c0 + nh + h)),
                  pl.BlockSpec((s, hd), lambda h, i: (0, c0 + 2 * nh + h)),
                  pl.BlockSpec((t, hd), lambda h, i: (i, c0 + 3 * nh + h))] + [ANY] * (1 + len(after)),
        out_specs=[pl.BlockSpec((None, t, hd), lambda h, i: (1, i, h)), pl.BlockSpec((t, hd), lambda h, i: (i, h))],
        out_shape=[jax.ShapeDtypeStruct(mixed.shape, BF16), jax.ShapeDtypeStruct((s, RET_WIDTH), F32)],
        input_output_aliases={4: 0},
        scratch_shapes=[pltpu.VMEM((s, hd), BF16), pltpu.VMEM((s, hd), BF16), pltpu.VMEM((s // t, t, t), F32)],
        compiler_params=_params(("arbitrary", "arbitrary")),
    )(proj, proj, proj, proj, mixed, *after)


def _ret_bwd(proj, ret_raw, dmixed, dsec, after=()):
    after = tuple(after)
    s = proj.shape[0]
    t = SEQ_TILE
    nt = s // t
    hd = RET_HEAD_DIM
    nh = RET_HEADS
    log_gammas = _ret_log_gammas()
    c0 = 3 * ATTN_WIDTH // hd
    mixed_blocks = ATTN_WIDTH // hd

    def body(q_ref, k_ref, v_ref, g_ref, raw_ref, dmix_ref, *rest):
        dsec_ref, qb, kb, vb, dretb, dq_acc, decay_tab = rest[1 + len(after):]
        h = pl.program_id(0)
        _fill_ret_decay(decay_tab, _select_by_index(h, log_gammas))
        qb[...] = q_ref[...].astype(BF16)
        kb[...] = k_ref[...].astype(BF16)
        vb[...] = v_ref[...].astype(BF16)
        ret = raw_ref[...]
        r = lax.rsqrt(jnp.mean(ret * ret, axis=-1, keepdims=True) + NORM_EPS)
        normed = ret * r
        g = g_ref[...]
        sg = _sigmoid(g)
        dout = dmix_ref[...]
        dsec_ref[3] = (dout * normed * sg * (1.0 + g * (1.0 - sg))).astype(BF16)
        dn = dout * g * sg
        dret = r * (dn - normed * jnp.mean(dn * normed, axis=-1, keepdims=True))
        dretb[...] = dret.astype(BF16)
        dq_acc[...] = jnp.zeros((s, hd), F32)

        def over_keys(j, _):
            krows = pl.ds(pl.multiple_of(j * t, t), t)
            kj = kb[krows, :]
            vj = vb[krows, :]

            def over_queries(i, carry):
                dk, dv = carry
                qrows = pl.ds(pl.multiple_of(i * t, t), t)
                qi = qb[qrows, :]
                doi = dretb[qrows, :]
                dec = decay_tab[i - j]
                a = (lax.dot_general(qi, kj, _NT_DIMS, preferred_element_type=F32) * dec).astype(BF16)
                da = (lax.dot_general(doi, vj, _NT_DIMS, preferred_element_type=F32) * dec).astype(BF16)
                dv = dv + lax.dot_general(a, doi, _TN_DIMS, preferred_element_type=F32)
                dk = dk + lax.dot_general(da, qi, _TN_DIMS, preferred_element_type=F32)
                dq_acc[qrows, :] += jnp.dot(da, kj, preferred_element_type=F32)
                return dk, dv

            zero = jnp.zeros((t, hd), F32)
            dk, dv = lax.fori_loop(j, nt, over_queries, (zero, zero))
            dsec_ref[1, krows, :] = dk.astype(BF16)
            dsec_ref[2, krows, :] = dv.astype(BF16)
            return 0

        lax.fori_loop(0, nt, over_keys, 0)
        dsec_ref[0] = dq_acc[...].astype(BF16)

    def col(off):
        return pl.BlockSpec((s, hd), lambda h: (0, off + h))

    return pl.pallas_call(
        body, name="ret_bwd", grid=(nh,),
        in_specs=[col(c0), col(c0 + nh), col(c0 + 2 * nh), col(c0 + 3 * nh), col(0), col(mixed_blocks)]
        + [ANY] * (1 + len(after)),
        out_specs=pl.BlockSpec((4, s, hd), lambda h: (1, 0, h)),
        out_shape=jax.ShapeDtypeStruct(dsec.shape, BF16),
        input_output_aliases={6: 0},
        scratch_shapes=[pltpu.VMEM((s, hd), BF16)] * 4 + [pltpu.VMEM((s, hd), F32)]
        + [pltpu.VMEM((nt, t, t), F32)],
        compiler_params=_params(("arbitrary",)),
    )(proj, proj, proj, proj, ret_raw, dmixed, dsec, *after)


_FLIPS = (2, 1, 3)


def _other_chips(x, y):
    return [(1 - x, y), (x, 1 - y), (1 - x, 1 - y)]


_HBM = pl.BlockSpec(memory_space=pltpu.HBM)
_SEM = pl.BlockSpec(memory_space=pltpu.SEMAPHORE)
_EFFECT = pltpu.SideEffectType.DATAFLOW_SIDE_EFFECTING


def _in_hbm(a):
    return pltpu.with_memory_space_constraint(a, pltpu.HBM)


def _weight_view(w, column_sharded):
    if column_sharded:
        return w.reshape(2, w.shape[0] // 2, w.shape[1])
    return w.reshape(N_CHIPS, 2, w.shape[0] // (2 * N_CHIPS), w.shape[1])


def _weight_unview(v):
    if v.ndim == 3:
        return v.reshape(2 * v.shape[1], v.shape[2])
    return v.reshape(N_CHIPS * 2 * v.shape[2], v.shape[3])


def _weight_region(buf, shard, half):
    if len(buf.shape) == 3:
        cols = buf.shape[2] // N_CHIPS
        return buf.at[half, :, pl.ds(shard * cols, cols)]
    return buf.at[shard, half]


def _remote(where, send_sem, recv_sem, to):
    return pltpu.make_async_remote_copy(src_ref=where, dst_ref=where, send_sem=send_sem, recv_sem=recv_sem,
                                        device_id=to, device_id_type=MESH)


def _for_my_shard(fn):
    x, y, _ = _place()
    for ss in range(N_CHIPS):
        pl.when(2 * x + y == ss)(functools.partial(fn, ss))


def _gather_start(views, name, after=()):
    n_w = len(views)
    after = tuple(after)

    def body(*refs):
        send_sems, recv_sems = refs[n_w + len(after):n_w + len(after) + 2]
        bufs = refs[n_w + len(after) + 2:]
        x, y, c = _place()

        def start(ss):
            for w in range(n_w):
                for j, chip in enumerate(_other_chips(x, y)):
                    _remote(_weight_region(bufs[w], ss, c), send_sems.at[3 * w + j], recv_sems.at[3 * w + j],
                            (*chip, c)).start()

        _for_my_shard(start)

    return pl.pallas_call(
        body, name=name,
        in_specs=[_HBM] * n_w + [ANY] * len(after), out_specs=[_SEM, _SEM] + [_HBM] * n_w,
        out_shape=[pltpu.SemaphoreType.DMA((3 * n_w,)), pltpu.SemaphoreType.DMA((3 * n_w,))]
        + [pltpu.HBM(v.shape, BF16) for v in views],
        input_output_aliases={w: 2 + w for w in range(n_w)},
        compiler_params=pltpu.CompilerParams(has_side_effects=_EFFECT),
    )(*[_in_hbm(v) for v in views], *after)


def _gather_forward(views, which, send_sems, recv_sems, after, name, base=0):
    n_w = len(views)
    which = [base // 3 + w for w in which] if base % 3 == 0 else None
    assert which is not None, "base must be a multiple of 3"

    def body(*refs):
        send_in, recv_in = refs[n_w:n_w + 2]
        fwd_send, fwd_recv = refs[n_w + 3:n_w + 5]
        bufs = refs[n_w + 5:]
        x, y, c = _place()
        sibling = (x, y, 1 - c)

        def forward(ss):
            for i, w in enumerate(which):
                for j in range(3):
                    landed = _weight_region(bufs[i], ss ^ _FLIPS[j], c)
                    _remote(landed, send_in.at[3 * w + j], recv_in.at[3 * w + j], sibling).wait_recv()
                    _remote(landed, fwd_send.at[3 * i + j], fwd_recv.at[3 * i + j], sibling).start()

        _for_my_shard(forward)
        for i, w in enumerate(which):
            for j in range(3):
                _remote(_weight_region(bufs[i], 0, 0), send_in.at[3 * w + j], recv_in.at[3 * w + j],
                        sibling).wait_send()

    return pl.pallas_call(
        body, name=name,
        in_specs=[_HBM] * n_w + [_SEM, _SEM, ANY], out_specs=[_SEM, _SEM] + [_HBM] * n_w,
        out_shape=[pltpu.SemaphoreType.DMA((3 * n_w,)), pltpu.SemaphoreType.DMA((3 * n_w,))]
        + [pltpu.HBM(v.shape, BF16) for v in views],
        input_output_aliases={w: 2 + w for w in range(n_w)},
        compiler_params=pltpu.CompilerParams(has_side_effects=_EFFECT),
    )(*views, send_sems, recv_sems, after)


def _gather_end(views, fwd_send, fwd_recv, after, name):
    n_w = len(views)

    def body(*refs):
        fwd_send_ref, fwd_recv_ref = refs[n_w:n_w + 2]
        bufs = refs[n_w + 3:]
        x, y, c = _place()
        for i in range(n_w):
            for j in range(3):
                cp = _remote(_weight_region(bufs[i], 0, 0), fwd_send_ref.at[3 * i + j], fwd_recv_ref.at[3 * i + j],
                             (x, y, 1 - c))
                cp.wait_recv()
                cp.wait_send()

    outs = pl.pallas_call(
        body, name=name,
        in_specs=[_HBM] * n_w + [_SEM, _SEM, ANY], out_specs=[_HBM] * n_w,
        out_shape=[pltpu.HBM(v.shape, BF16) for v in views],
        input_output_aliases={w: w for w in range(n_w)},
        compiler_params=pltpu.CompilerParams(has_side_effects=_EFFECT),
    )(*views, fwd_send, fwd_recv, after)
    return [_weight_unview(o) for o in outs]


def _comm_call(name, bufs, sem_pairs, after, n_new, fn):
    n, n_sem, after = len(bufs), 2 * len(sem_pairs), tuple(after)
    n_out_sem = 2 if n_new else 0

    def body(*refs):
        sems = refs[n:n + n_sem]
        outs = refs[n + n_sem + len(after):]
        new = outs[:n_out_sem] if n_new else (None, None)
        fn(outs[n_out_sem:], [(sems[2 * i], sems[2 * i + 1]) for i in range(len(sem_pairs))], *new)

    res = pl.pallas_call(
        body, name=name,
        in_specs=[_HBM] * n + [_SEM] * n_sem + [ANY] * len(after),
        out_specs=[_SEM] * n_out_sem + [_HBM] * n,
        out_shape=[pltpu.SemaphoreType.DMA((n_new,))] * n_out_sem + [pltpu.HBM(b.shape, b.dtype) for b in bufs],
        input_output_aliases={i: n_out_sem + i for i in range(n)},
        compiler_params=pltpu.CompilerParams(has_side_effects=_EFFECT),
    )(*bufs, *[s for pair in sem_pairs for s in pair], *after)
    return list(res[:n_out_sem]), list(res[n_out_sem:])


def _quarter(piece, q):
    rows = piece.shape[0] // 2
    return piece.at[pl.ds(q * rows, rows)]


def _gather_in_start(view, name):
    def fn(bufs, _, send, recv):
        x, y, c = _place()

        def go(ss):
            for j, chip in enumerate(_other_chips(x, y)[:2]):
                _remote(_weight_region(bufs[0], ss, c), send.at[j], recv.at[j], (*chip, c)).start()

        _for_my_shard(go)

    sems, (view,) = _comm_call(name, [_in_hbm(view)], [], (), 2, fn)
    return sems, view


def _gather_out_gate_start(v_out, v_gate, after, name):
    def fn(bufs, _, send, recv):
        x, y, c = _place()
        chips = _other_chips(x, y)

        def go(ss):
            for j in range(3):
                _remote(_weight_region(bufs[0], ss, c), send.at[j], recv.at[j], (*chips[j], c)).start()
            for j in range(2):
                _remote(_weight_region(bufs[1], ss, c), send.at[3 + j], recv.at[3 + j], (*chips[j], c)).start()

        _for_my_shard(go)

    sems, views = _comm_call(name, [_in_hbm(v_out), _in_hbm(v_gate)], [], after, 5, fn)
    return sems, views


def _gather_relay(view, started, base, after, name, then=None, then_peers=0):
    n_new = 6 + then_peers if then_peers else 4

    def fn(bufs, pairs, send, recv):
        (send_in, recv_in), = pairs
        x, y, c = _place()
        chips = _other_chips(x, y)
        sibling = (x, y, 1 - c)

        def go(ss):
            landed = [_weight_region(bufs[0], ss ^ _FLIPS[j], c) for j in range(2)]
            for j in range(2):
                _remote(landed[j], send_in.at[base + j], recv_in.at[base + j], sibling).wait_recv()
            for j in range(2):
                _remote(_quarter(landed[j], j), send.at[j], recv.at[j], (*chips[1 - j], c)).start()
            for j in range(2):
                _remote(landed[j], send.at[2 + j], recv.at[2 + j], sibling).start()
            for j in range(then_peers):
                _remote(_weight_region(bufs[1], ss, c), send.at[6 + j], recv.at[6 + j], (*chips[j], c)).start()

        _for_my_shard(go)
        for j in range(2):
            _remote(_weight_region(bufs[0], 0, 0), send_in.at[base + j], recv_in.at[base + j], sibling).wait_send()

    views = [view] if then is None else [view, _in_hbm(then)]
    sems, views = _comm_call(name, views, [started], after, n_new, fn)
    return sems, views


def _gather_in_neighbours_end(view, relayed, after, name):
    def fn(bufs, pairs, *_):
        (send, recv), = pairs
        x, y, c = _place()
        for j in range(2):
            cp = _remote(_weight_region(bufs[0], 0, 0), send.at[2 + j], recv.at[2 + j], (x, y, 1 - c))
            cp.wait_recv()
            cp.wait_send()

    _, (view,) = _comm_call(name, [view], [relayed], after, 0, fn)
    return view


def _gather_in_diagonal(view, relayed, after, name):
    def fn(bufs, pairs, send, recv):
        (send_in, recv_in), = pairs
        x, y, c = _place()
        sibling = (x, y, 1 - c)
        any_quarter = _quarter(_weight_region(bufs[0], 0, 0), 0)
        for j in range(2):
            cp = _remote(any_quarter, send_in.at[j], recv_in.at[j], sibling)
            cp.wait_recv()
            cp.wait_send()

        def go(ss):
            _remote(_weight_region(bufs[0], ss ^ _FLIPS[2], c), send.at[0], recv.at[0], sibling).start()

        _for_my_shard(go)

    sems, (view,) = _comm_call(name, [view], [relayed], after, 1, fn)
    return sems, view


def _gather_in_diagonal_end(view, forwarded, after, name):
    def fn(bufs, pairs, *_):
        (send, recv), = pairs
        x, y, c = _place()
        cp = _remote(_weight_region(bufs[0], 0, 0), send.at[0], recv.at[0], (x, y, 1 - c))
        cp.wait_recv()
        cp.wait_send()

    _, (view,) = _comm_call(name, [view], [forwarded], after, 0, fn)
    return view


def _in_proj_shard(h1, wi, proj, shard_arr, name):
    s, d = h1.shape
    n = wi.shape[1]
    tn = 256
    blocks = n // (N_CHIPS * tn)
    given = [] if proj is None else [proj]

    def body(shard_ref, h_ref, w_ref, *rest):
        del shard_ref
        rest[-1][...] = jnp.dot(h_ref[...], w_ref[...], preferred_element_type=F32)

    grid_spec = pltpu.PrefetchScalarGridSpec(
        num_scalar_prefetch=1, grid=(blocks,),
        in_specs=[pl.BlockSpec((s, d), lambda j, shard_ref: (0, 0)),
                  pl.BlockSpec((d, tn), lambda j, shard_ref: (0, shard_ref[0] * blocks + j))] + [ANY] * len(given),
        out_specs=pl.BlockSpec((s, tn), lambda j, shard_ref: (0, shard_ref[0] * blocks + j)))
    return pl.pallas_call(
        body, name=name, grid_spec=grid_spec,
        out_shape=jax.ShapeDtypeStruct((s, n), F32),
        input_output_aliases={3: 0} if given else {},
        compiler_params=_params(("arbitrary",)),
    )(shard_arr, h1, wi, *given)


def _split_start(name, bufs, n_sems, copies):
    n = len(bufs)

    def body(*refs):
        send_sems, recv_sems = refs[n:n + 2]
        for cp in copies(refs[n + 2:], send_sems, recv_sems):
            cp.start()

    outs = pl.pallas_call(
        body, name=name,
        in_specs=[_HBM] * n, out_specs=[_SEM, _SEM] + [_HBM] * n,
        out_shape=[pltpu.SemaphoreType.DMA((n_sems,)), pltpu.SemaphoreType.DMA((n_sems,))]
        + [pltpu.HBM(b.shape, b.dtype) for b in bufs],
        input_output_aliases={i: 2 + i for i in range(n)},
        compiler_params=pltpu.CompilerParams(has_side_effects=_EFFECT),
    )(*[_in_hbm(b) for b in bufs])
    return outs[0], outs[1], list(outs[2:])


def _split_wait(name, bufs, send_sems, recv_sems, copies, after):
    n = len(bufs)

    def body(*refs):
        send_ref, recv_ref = refs[n:n + 2]
        for cp in copies(refs[n + 3:], send_ref, recv_ref):
            cp.wait()

    return list(pl.pallas_call(
        body, name=name,
        in_specs=[_HBM] * n + [_SEM, _SEM, ANY], out_specs=[_HBM] * n,
        out_shape=[pltpu.HBM(b.shape, b.dtype) for b in bufs],
        input_output_aliases={i: i for i in range(n)},
        compiler_params=pltpu.CompilerParams(has_side_effects=_EFFECT),
    )(*bufs, send_sems, recv_sems, after))


def _halves_copies(n_w):
    def copies(bufs, send_sems, recv_sems):
        x, y, c = _place()
        out = []
        for w in range(n_w):
            view, land = bufs[w], bufs[n_w + w]
            src = view.at[1 - c] if len(view.shape) == 3 else view.at[:, 1 - c]
            out.append(pltpu.make_async_remote_copy(
                src_ref=src, dst_ref=land, send_sem=send_sems.at[w], recv_sem=recv_sems.at[w],
                device_id=(x, y, 1 - c), device_id_type=MESH))
        return out
    return copies


def _pieces_copies(n_w):
    def copies(bufs, send_sems, recv_sems):
        x, y, c = _place()
        out = []
        for w in range(n_w):
            for j, (cx, cy) in enumerate(_other_chips(x, y)):
                out.append(pltpu.make_async_remote_copy(
                    src_ref=bufs[w].at[2 * cx + cy], dst_ref=bufs[n_w + w].at[j],
                    send_sem=send_sems.at[3 * w + j], recv_sem=recv_sems.at[3 * w + j],
                    device_id=(cx, cy, c), device_id_type=MESH))
        return out
    return copies


def _join_copies(n_w):
    def copies(bufs, send_sems, recv_sems):
        x, y, c = _place()
        return [pltpu.make_async_remote_copy(
            src_ref=bufs[w].at[c], dst_ref=bufs[w].at[c], send_sem=send_sems.at[w], recv_sem=recv_sems.at[w],
            device_id=(x, y, 1 - c), device_id_type=MESH) for w in range(n_w)]
    return copies


def _grad_view(g, column_sharded):
    return _weight_view(g, column_sharded)


def _halves_landing(view):
    shape = view.shape[1:] if view.ndim == 3 else (N_CHIPS,) + view.shape[2:]
    return lax.empty(shape, BF16)


def _halves_start(tag, grads, column_sharded):
    views = [_weight_view(g, cs) for g, cs in zip(grads, column_sharded)]
    n = len(views)
    return _split_start("halves_start_" + tag, views + [_halves_landing(v) for v in views], n, _halves_copies(n))


def _halves_wait(tag, state, after):
    send_sems, recv_sems, bufs = state
    n = len(bufs) // 2
    bufs = _split_wait("halves_wait_" + tag, bufs, send_sems, recv_sems, _halves_copies(n), after)
    return bufs[:n], bufs[n:]


def _pieces_start(tag, pieces):
    n = len(pieces)
    landing = [lax.empty((3,) + p.shape[1:], BF16) for p in pieces]
    return _split_start("pieces_start_" + tag, list(pieces) + landing, 3 * n, _pieces_copies(n))


def _pieces_wait(tag, state, after):
    send_sems, recv_sems, bufs = state
    n = len(bufs) // 2
    bufs = _split_wait("pieces_wait_" + tag, bufs, send_sems, recv_sems, _pieces_copies(n), after)
    return bufs[:n], bufs[n:]


def _join_start(tag, shards):
    n = len(shards)
    return _split_start("join_start_" + tag, list(shards), n, _join_copies(n))


def _join_wait(tag, state, after):
    send_sems, recv_sems, bufs = state
    bufs = _split_wait("join_wait_" + tag, bufs, send_sems, recv_sems, _join_copies(len(bufs)), after)
    return [b.reshape(2 * b.shape[1], b.shape[2]) for b in bufs]


def _chip_sum_col(g3, sib, c_arr, name):
    _, hk, n = g3.shape
    cols = n // N_CHIPS
    tr = _row_tile(hk, cols * 2, limit=4 * 1024 * 1024)

    def body(c_ref, g_ref, s_ref, o_ref):
        del c_ref
        o_ref[...] = (g_ref[...].astype(F32) + s_ref[...].astype(F32)).astype(BF16)

    grid_spec = pltpu.PrefetchScalarGridSpec(
        num_scalar_prefetch=1, grid=(N_CHIPS, hk // tr),
        in_specs=[pl.BlockSpec((None, tr, cols), lambda p, r, c_ref: (c_ref[0], r, p)),
                  pl.BlockSpec((tr, cols), lambda p, r, c_ref: (r, p))],
        out_specs=pl.BlockSpec((None, tr, cols), lambda p, r, c_ref: (p, r, 0)))
    return pl.pallas_call(
        body, name=name, grid_spec=grid_spec,
        out_shape=jax.ShapeDtypeStruct((N_CHIPS, hk, cols), BF16),
        compiler_params=_params(("parallel", "parallel")),
    )(c_arr, g3, sib)


def _chip_sum_row(g4, sib, c_arr, name):
    _, _, hr, n = g4.shape
    tr = _row_tile(hr, n * 2, limit=4 * 1024 * 1024)

    def body(c_ref, g_ref, s_ref, o_ref):
        del c_ref
        o_ref[...] = (g_ref[...].astype(F32) + s_ref[...].astype(F32)).astype(BF16)

    grid_spec = pltpu.PrefetchScalarGridSpec(
        num_scalar_prefetch=1, grid=(N_CHIPS, hr // tr),
        in_specs=[pl.BlockSpec((None, None, tr, n), lambda p, r, c_ref: (p, c_ref[0], r, 0)),
                  pl.BlockSpec((None, tr, n), lambda p, r, c_ref: (p, r, 0))],
        out_specs=pl.BlockSpec((None, tr, n), lambda p, r, c_ref: (p, r, 0)))
    return pl.pallas_call(
        body, name=name, grid_spec=grid_spec,
        out_shape=jax.ShapeDtypeStruct((N_CHIPS, hr, n), BF16),
        compiler_params=_params(("parallel", "parallel")),
    )(c_arr, g4, sib)


def _sum_pieces(pieces, received, place_arr, name):
    _, r, n = pieces.shape
    tr = _row_tile(r, n * 4, limit=4 * 1024 * 1024)

    def body(p_ref, own_ref, r0_ref, r1_ref, r2_ref, o_ref):
        del p_ref
        acc = own_ref[...].astype(F32) + r0_ref[...].astype(F32)
        acc = acc + r1_ref[...].astype(F32)
        o_ref[...] = acc + r2_ref[...].astype(F32)

    def recv_spec(j):
        return pl.BlockSpec((None, tr, n), lambda i, p_ref: (j, i, 0))

    grid_spec = pltpu.PrefetchScalarGridSpec(
        num_scalar_prefetch=1, grid=(r // tr,),
        in_specs=[pl.BlockSpec((None, tr, n), lambda i, p_ref: (p_ref[0], i, 0)),
                  recv_spec(0), recv_spec(1), recv_spec(2)],
        out_specs=pl.BlockSpec((None, tr, n), lambda i, p_ref: (p_ref[1], i, 0)))
    return pl.pallas_call(
        body, name=name, grid_spec=grid_spec,
        out_shape=jax.ShapeDtypeStruct((2, r, n), F32),
        compiler_params=_params(("parallel",)),
    )(place_arr, pieces, received, received, received)


def _norm_weights_step(parts, w, m, v, after=()):
    rows, d = parts.shape
    after = tuple(after)

    def body(p_ref, w_ref, m_ref, v_ref, *rest):
        g_ref, d_ref, mo_ref, vo_ref, gathered, send_sems, recv_sems = rest[len(after):]
        x, y, c = _place()
        me = 4 * x + 2 * y + c
        gathered[me] = p_ref[...]
        copies = []
        for k in range(1, N_DEV):
            peer = (x ^ ((k >> 2) & 1), y ^ ((k >> 1) & 1), c ^ (k & 1))
            copies.append(pltpu.make_async_remote_copy(
                src_ref=p_ref, dst_ref=gathered.at[me], send_sem=send_sems.at[k - 1],
                recv_sem=recv_sems.at[k - 1], device_id=peer, device_id_type=MESH))
        for cp in copies:
            cp.start()
        for cp in copies:
            cp.wait()
        g = gathered[0]
        for k in range(1, N_DEV):
            g = g + gathered[k]
        delta, m_new, v_new = _adamw_math(w_ref[...], g, m_ref[...], v_ref[...])
        g_ref[...] = g
        d_ref[...] = delta
        mo_ref[...] = m_new
        vo_ref[...] = v_new

    vmem = pl.BlockSpec(memory_space=pltpu.VMEM)
    shp = jax.ShapeDtypeStruct((rows, d), F32)
    return pl.pallas_call(
        body, name="norm_weights_step",
        in_specs=[vmem] * 4 + [ANY] * len(after), out_specs=[vmem] * 4, out_shape=[shp] * 4,
        scratch_shapes=[pltpu.VMEM((N_DEV, rows, d), F32), pltpu.SemaphoreType.DMA((N_DEV - 1,)),
                        pltpu.SemaphoreType.DMA((N_DEV - 1,))],
        compiler_params=pltpu.CompilerParams(has_side_effects=True),
    )(parts, w, m, v, *after)


def kernel(x, norm_mix_w, w_in, w_out, norm_ffn_w, w_gate, w_up, w_down, norm_final_w, loss_target, m_norm_mix_w, m_w_in, m_w_out, m_norm_ffn_w, m_w_gate, m_w_up, m_w_down, m_norm_final_w, v_norm_mix_w, v_w_in, v_w_out, v_norm_ffn_w, v_w_gate, v_w_up, v_w_down, v_norm_final_w):
    s, d = x.shape[1], x.shape[2]
    xs = x.reshape(s, d)
    target = loss_target.reshape(s, d)
    big = {"w_in": (w_in, m_w_in, v_w_in), "w_out": (w_out, m_w_out, v_w_out),
           "w_gate": (w_gate, m_w_gate, v_w_gate), "w_up": (w_up, m_w_up, v_w_up),
           "w_down": (w_down, m_w_down, v_w_down)}
    big = {k: tuple(a.reshape(a.shape[1:]) for a in t) for k, t in big.items()}
    col_names, row_names = ("w_in", "w_gate", "w_up"), ("w_out", "w_down")
    n_in = N_CHIPS * big["w_in"][0].shape[1]
    ffn = N_CHIPS * big["w_gate"][0].shape[1]
    mix = ATTN_WIDTH + RET_WIDTH
    c_arr = lax.axis_index("c").astype(I32).reshape(1)
    shard_arr = (2 * lax.axis_index("x") + lax.axis_index("y")).astype(I32).reshape(1)
    place_arr = jnp.concatenate([shard_arr, c_arr])

    def cast(k, after=()):
        return _weight_view(_cast_into_full(big[k][0], shard_arr, k in col_names, "cast_" + k, after), k in col_names)

    started_in, v_in = _gather_in_start(cast("w_in"), "gather_in_start")

    sec = ATTN_WIDTH

    def section(p, rows):
        return pl.BlockSpec((None, rows, sec), lambda i, j, kk: (p, i, 0))

    h1 = _rms_fwd(xs, norm_mix_w, "rms_mix_fwd")
    my_shard = shard_arr[0]
    shard_of = [jnp.bitwise_xor(my_shard, f).astype(I32).reshape(1) for f in (0,) + _FLIPS]
    proj = _in_proj_shard(h1, _weight_unview(v_in), None, shard_of[0], "in_proj_own")
    early_views = [cast(k, after=[proj]) for k in ("w_out", "w_gate")]
    relayed_in, (v_in,) = _gather_relay(v_in, started_in, 0, early_views, "gather_in_relay")
    v_up, v_down = [cast(k, after=[v_in]) for k in ("w_up", "w_down")]
    started_og, (v_out, v_gate) = _gather_out_gate_start(*early_views, [v_down], "gather_out_gate_start")
    v_in = _gather_in_neighbours_end(v_in, relayed_in, [v_out], "gather_in_neighbours_end")
    proj = _in_proj_shard(h1, _weight_unview(v_in), proj, shard_of[1], "in_proj_x")
    proj = _in_proj_shard(h1, _weight_unview(v_in), proj, shard_of[2], "in_proj_y")
    forwarded_in, v_in = _gather_in_diagonal(v_in, relayed_in, [proj], "gather_in_diagonal")
    wi = _weight_unview(_gather_in_diagonal_end(v_in, forwarded_in, [proj], "gather_in_diagonal_end"))
    proj = _in_proj_shard(h1, wi, proj, shard_of[3], "in_proj_diagonal")
    fs_o, fr_o, v_out = _gather_forward([v_out], [0], *started_og, proj, "gather_forward_out")
    mixed, attn_o, lse = _attn_fwd(proj, after=[v_out])
    relayed_g, (v_gate, v_up) = _gather_relay(v_gate, started_og, 3, [attn_o], "gather_gate_relay",
                                              then=v_up, then_peers=2)
    mixed, ret_raw = _ret_fwd(proj, mixed, after=[v_gate])
    wo, = _gather_end([v_out], fs_o, fr_o, ret_raw, "gather_end_out")
    x1, = _matmul("out_proj", "nn", [mixed, mixed], [wo, wo], [0, 0], s, d, sec, s, 512, sec, [xs], [F32],
                  _epi_residual, b_koff=[0, 1], a_specs=[section(0, s), section(1, s)])
    h2 = _rms_fwd(x1, norm_ffn_w, "rms_ffn_fwd")
    relayed_u, (v_up, v_down) = _gather_relay(v_up, relayed_g, 6, [h2], "gather_up_relay",
                                              then=v_down, then_peers=3)
    v_gate = _gather_in_neighbours_end(v_gate, relayed_g, [v_up], "gather_gate_neighbours_end")
    forwarded_g, v_gate = _gather_in_diagonal(v_gate, relayed_g, [v_up], "gather_gate_diagonal")
    v_up = _gather_in_neighbours_end(v_up, relayed_u, [v_gate], "gather_up_neighbours_end")
    wg = _weight_unview(_gather_in_diagonal_end(v_gate, forwarded_g, [v_up], "gather_gate_diagonal_end"))
    forwarded_u, v_up = _gather_in_diagonal(v_up, relayed_u, [wg], "gather_up_diagonal")
    wu = _weight_unview(_gather_in_diagonal_end(v_up, forwarded_u, [wg], "gather_up_diagonal_end"))
    gate, up, act = _matmul("gate_up", "nn", [h2, h2], [wg, wu], [0, 1], s, ffn, d, s, 256, d, [],
                            [BF16, BF16, BF16], _epi_swiglu)
    fs, fr, v_down = _gather_forward([v_down], [0], *relayed_u, act, "gather_forward_down", base=6)
    wd, = _gather_end([v_down], fs, fr, act, "gather_end_down")
    x2, = _matmul("down_proj", "nn", [act], [wd], [0], s, d, ffn, s // 2, 256, ffn, [x1], [F32],
                  _epi_residual)
    loss_row, dx2, dx2b, dwf = _final_norm_loss(x2, norm_final_w.reshape(1, d), target, "final_norm_loss")

    names = col_names + row_names
    grads, new = {}, {}

    def chip_sums(tag_names, views, sibs):
        return [(_chip_sum_col if k in col_names else _chip_sum_row)(v, sb, c_arr, "chip_sum_" + k)
                for k, v, sb in zip(tag_names, views, sibs)]

    def piece_sums(tag_names, pieces, received):
        return [_sum_pieces(p, r, place_arr, "sum_pieces_" + k) for k, p, r in zip(tag_names, pieces, received)]

    def update(k):
        new[k] = _adamw(big[k][0], grads[k], big[k][1], big[k][2], "adamw_" + k)

    dgate, dup = _matmul("d_act", "nt", [dx2b], [wd], [0], s, ffn, d, s, 256, d, [gate, up],
                         [BF16, BF16], _epi_swiglu_bwd)
    g_wd, = _matmul("g_w_down", "tn", [act], [dx2b], [0], ffn, d, s, 512, d, s, [], [BF16], _epi_plain)
    halves_d = _halves_start("down", [g_wd], [False])
    dh2, = _matmul("d_h2", "nt", [dgate, dup], [wg, wu], [0, 0], s, d, ffn, s // 2, 256, ffn, [], [F32],
                   _epi_plain, after=halves_d[2][-1:], a_single_buffer=True)
    pieces_d = _pieces_start("down", chip_sums(["w_down"], *_halves_wait("down", halves_d, dh2)))
    g_wg, g_wu = _matmul("g_w_gate_up", "tn", [h2, h2], [dgate, dup], [0, 1], d, ffn, s, 1024, 512, s, [],
                         [BF16, BF16], _epi_two, after=pieces_d[2][-1:])
    halves_gu = _halves_start("gate_up", [g_wg, g_wu], [True, True])
    dx1, dx1b, dw_ffn = _rms_bwd(x1, norm_ffn_w, dh2, dx2, "rms_ffn_bwd", after=halves_gu[2][-1:])

    dmixed, = _matmul("d_mixed", "nt", [dx1b], [wo], [0], s, mix, d, s, 512, d, [], [F32], _epi_plain)
    pieces_gu = _pieces_start("gate_up", chip_sums(["w_gate", "w_up"], *_halves_wait("gate_up", halves_gu, dmixed)))
    per = sec // 512
    g_wo, = _matmul("g_w_out", "tn", [mixed], [dx1b], [0], mix, d, s, 512, d, s, [], [BF16], _epi_plain,
                    after=pieces_gu[2][-1:],
                    a_specs=[pl.BlockSpec((None, s, 512), lambda i, j, kk: (i // per, 0, i % per))])
    halves_o = _halves_start("out", [g_wo], [False])
    dsec = _attn_bwd(proj, attn_o, lse, dmixed, after=halves_o[2][-1:])
    pieces_o = _pieces_start("out", chip_sums(["w_out"], *_halves_wait("out", halves_o, dsec)))
    dsec = _ret_bwd(proj, ret_raw, dmixed, dsec, after=pieces_o[2][-1:])
    where = [0, 1, 2, 4, 5, 6, 7]
    n_sec = len(where)
    g_wi, = _matmul("g_w_in", "tn", [h1], [dsec], [0], d, n_in, s, 1024, sec, s, [], [BF16], _epi_plain,
                    b_specs=[pl.BlockSpec((None, s, sec), lambda i, j, kk: (j + (j >= 3).astype(I32), 0, 0))])
    halves_i = _halves_start("in", [g_wi], [True])
    dh1, = _matmul("d_h1", "nt", [dsec] * n_sec, [wi] * n_sec, [0] * n_sec, s, d, sec, s // 2, 256, sec, [], [F32],
                   _epi_plain, b_koff=list(range(n_sec)), after=halves_i[2][-1:],
                   a_specs=[section(p, s // 2) for p in where])
    pieces_i = _pieces_start("in", chip_sums(["w_in"], *_halves_wait("in", halves_i, dh1)))
    grad_x, _, dw_mix = _rms_bwd(xs, norm_mix_w, dh1, dx1, "rms_mix_bwd", after=pieces_i[2][-1:])

    def rows8(*vs):
        return jnp.concatenate([v.reshape(1, d) for v in vs] + [jnp.zeros((8 - len(vs), d), F32)], axis=0)

    join_d = _join_start("down", piece_sums(["w_down"], *_pieces_wait("down", pieces_d, grad_x)))
    join_gu = _join_start("gate_up", piece_sums(["w_gate", "w_up"], *_pieces_wait("gate_up", pieces_gu, join_d[2][0])))
    join_o = _join_start("out", piece_sums(["w_out"], *_pieces_wait("out", pieces_o, join_gu[2][0])))
    grads["w_down"], = _join_wait("down", join_d, join_o[2][0])
    update("w_down")
    grads["w_gate"], grads["w_up"] = _join_wait("gate_up", join_gu, new["w_down"][0])
    update("w_gate")
    update("w_up")
    grads["w_out"], = _join_wait("out", join_o, new["w_up"][0])
    update("w_out")
    join_i = _join_start("in", piece_sums(["w_in"], *_pieces_wait("in", pieces_i, new["w_out"][0])))
    ng, nd, nm, nv = _norm_weights_step(
        rows8(dw_mix, dw_ffn, dwf, jnp.broadcast_to(loss_row[:, :1], (1, d))),
        rows8(norm_mix_w, norm_ffn_w, norm_final_w),
        rows8(m_norm_mix_w, m_norm_ffn_w, m_norm_final_w), rows8(v_norm_mix_w, v_norm_ffn_w, v_norm_final_w),
        after=join_i[2][:1])
    grads["w_in"], = _join_wait("in", join_i, ng)
    update("w_in")

    loss = ng[3, 0]

    def pack(small, per_weight):
        lead = lambda a: a.reshape((1,) + a.shape)
        return (small[0:1], lead(per_weight["w_in"]), lead(per_weight["w_out"]), small[1:2],
                lead(per_weight["w_gate"]), lead(per_weight["w_up"]), lead(per_weight["w_down"]), small[2])

    return (loss, grad_x.reshape(1, s, d),
            *pack(ng, {k: new[k][3] for k in names}),
            *pack(nd, {k: new[k][0] for k in names}),
            *pack(nm, {k: new[k][1] for k in names}),
            *pack(nv, {k: new[k][2] for k in names}))
```

```python
import functools
import math

import jax
import jax.numpy as jnp
from jax import lax
from jax.experimental import pallas as pl
from jax.experimental.pallas import tpu as pltpu

F32 = jnp.float32
BF16 = jnp.bfloat16
I32 = jnp.int32
MESH = pl.DeviceIdType.MESH
ANY = pl.BlockSpec(memory_space=pl.ANY)

ATTN_HEADS = 8
ATTN_HEAD_DIM = 128
RET_HEADS = 4
RET_HEAD_DIM = 256
ATTN_WIDTH = ATTN_HEADS * ATTN_HEAD_DIM
RET_WIDTH = RET_HEADS * RET_HEAD_DIM
DILATED_PATTERNS = ((128, 1), (512, 4), (2048, 16))
NORM_EPS = 1e-6
ADAM_LR = 0.001
ADAM_B1 = 0.9
ADAM_B2 = 0.999
ADAM_EPS = 1e-08
ADAM_WD = 0.01
ADAM_STEP = 10

N_CHIPS = 4
N_DEV = 8
NEG_BIG = -1e30
SEQ_TILE = 512
ATTN_FWD_HEADS_PER_STEP = 2
ATTN_HEADS_PER_STEP = 1
VMEM_LIMIT_BYTES = 56 * 1024 * 1024


def _params(semantics=None, vmem=VMEM_LIMIT_BYTES):
    return pltpu.CompilerParams(dimension_semantics=semantics, vmem_limit_bytes=vmem)


def _row_tile(rows, row_bytes, limit=2 * 1024 * 1024, mult=16):
    best = None
    for t in range(mult, rows + 1, mult):
        if rows % t == 0 and t * row_bytes <= limit:
            best = t
    assert best is not None, (rows, row_bytes)
    return best


def _sigmoid(x):
    return 1.0 / (1.0 + jnp.exp(-x))


def _select_by_index(idx, values):
    out = jnp.float32(values[-1])
    for i in range(len(values) - 2, -1, -1):
        out = jnp.where(idx == i, jnp.float32(values[i]), out)
    return out


def _place():
    x, y, c = lax.axis_index("x"), lax.axis_index("y"), lax.axis_index("c")
    return x, y, c


def _cast_into_full(w, shard_arr, column_sharded, name, after=()):
    after = tuple(after)
    rows, cols = w.shape
    tr = _row_tile(rows, cols * 4)
    steps = rows // tr
    if column_sharded:
        out_shape, out_map = (rows, N_CHIPS * cols), (lambda i, s_ref: (i, s_ref[0]))
    else:
        out_shape, out_map = (N_CHIPS * rows, cols), (lambda i, s_ref: (s_ref[0] * steps + i, 0))

    def body(s_ref, w_ref, *rest):
        del s_ref
        rest[-1][...] = w_ref[...].astype(BF16)

    grid_spec = pltpu.PrefetchScalarGridSpec(
        num_scalar_prefetch=1, grid=(steps,),
        in_specs=[pl.BlockSpec((tr, cols), lambda i, s_ref: (i, 0))] + [ANY] * len(after),
        out_specs=pl.BlockSpec((tr, cols), out_map))
    return pl.pallas_call(
        body, name=name, grid_spec=grid_spec,
        out_shape=jax.ShapeDtypeStruct(out_shape, BF16),
        compiler_params=_params(("parallel",)),
    )(shard_arr, w, *after)


def _rms_fwd(x, w, name):
    rows, d = x.shape
    tr = 256

    def body(x_ref, w_ref, h_ref):
        xv = x_ref[...]
        r = lax.rsqrt(jnp.mean(xv * xv, axis=-1, keepdims=True) + NORM_EPS)
        h_ref[...] = (xv * r * w_ref[...]).astype(BF16)

    return pl.pallas_call(
        body, name=name, grid=(rows // tr,),
        in_specs=[pl.BlockSpec((tr, d), lambda i: (i, 0)), pl.BlockSpec((1, d), lambda i: (0, 0))],
        out_specs=pl.BlockSpec((tr, d), lambda i: (i, 0)),
        out_shape=jax.ShapeDtypeStruct((rows, d), BF16),
        compiler_params=_params(("parallel",)),
    )(x, w)


def _rms_bwd(x, w, dh, dres, name, after=()):
    rows, d = x.shape
    tr = 256
    after = tuple(after)

    def body(x_ref, w_ref, dh_ref, dres_ref, *rest):
        dx_ref, dxb_ref, dw_ref = rest[len(after):]
        xv = x_ref[...]
        r = lax.rsqrt(jnp.mean(xv * xv, axis=-1, keepdims=True) + NORM_EPS)
        xhat = xv * r
        dy = dh_ref[...]
        dxhat = dy * w_ref[...]
        dx = dres_ref[...] + r * (dxhat - xhat * jnp.mean(dxhat * xhat, axis=-1, keepdims=True))
        dx_ref[...] = dx
        dxb_ref[...] = dx.astype(BF16)
        part = jnp.sum(dy * xhat, axis=0, keepdims=True)

        @pl.when(pl.program_id(0) == 0)
        def _():
            dw_ref[...] = part

        @pl.when(pl.program_id(0) != 0)
        def _():
            dw_ref[...] += part

    row = pl.BlockSpec((tr, d), lambda i: (i, 0))
    vec = pl.BlockSpec((1, d), lambda i: (0, 0))
    return pl.pallas_call(
        body, name=name, grid=(rows // tr,),
        in_specs=[row, vec, row, row] + [ANY] * len(after),
        out_specs=[row, row, vec],
        out_shape=[jax.ShapeDtypeStruct((rows, d), F32), jax.ShapeDtypeStruct((rows, d), BF16),
                   jax.ShapeDtypeStruct((1, d), F32)],
        compiler_params=_params(("arbitrary",)),
    )(x, w, dh, dres, *after)


def _final_norm_loss(x2, w, target, name):
    rows, d = x2.shape
    tr = 256

    def body(x_ref, w_ref, t_ref, loss_ref, dx_ref, dxb_ref, dw_ref):
        xv = x_ref[...]
        wv = w_ref[...]
        r = lax.rsqrt(jnp.mean(xv * xv, axis=-1, keepdims=True) + NORM_EPS)
        xhat = xv * r
        err = xhat * wv - t_ref[...]
        part_loss = 0.5 * jnp.sum(jnp.mean(err * err, axis=-1, keepdims=True), axis=0, keepdims=True)
        dy = err * (1.0 / d)
        dxhat = dy * wv
        dx = r * (dxhat - xhat * jnp.mean(dxhat * xhat, axis=-1, keepdims=True))
        dx_ref[...] = dx
        dxb_ref[...] = dx.astype(BF16)
        part_dw = jnp.sum(dy * xhat, axis=0, keepdims=True)
        part_loss = jnp.broadcast_to(part_loss, (1, 128))

        @pl.when(pl.program_id(0) == 0)
        def _():
            dw_ref[...] = part_dw
            loss_ref[...] = part_loss

        @pl.when(pl.program_id(0) != 0)
        def _():
            dw_ref[...] += part_dw
            loss_ref[...] += part_loss

    row = pl.BlockSpec((tr, d), lambda i: (i, 0))
    vec = pl.BlockSpec((1, d), lambda i: (0, 0))
    return pl.pallas_call(
        body, name=name, grid=(rows // tr,),
        in_specs=[row, vec, row],
        out_specs=[pl.BlockSpec((1, 128), lambda i: (0, 0)), row, row, vec],
        out_shape=[jax.ShapeDtypeStruct((1, 128), F32), jax.ShapeDtypeStruct((rows, d), F32),
                   jax.ShapeDtypeStruct((rows, d), BF16), jax.ShapeDtypeStruct((1, d), F32)],
        compiler_params=_params(("arbitrary",)),
    )(x2, w, target)


def _adamw_math(w, g, m, v):
    m = ADAM_B1 * m + (1.0 - ADAM_B1) * g
    v = ADAM_B2 * v + (1.0 - ADAM_B2) * (g * g)
    m_hat = m / (1.0 - ADAM_B1 ** ADAM_STEP)
    v_hat = v / (1.0 - ADAM_B2 ** ADAM_STEP)
    delta = -ADAM_LR * (m_hat / (jnp.sqrt(v_hat) + ADAM_EPS) + ADAM_WD * w)
    return delta, m, v


def _adamw(w, g, m, v, name):
    rows, cols = w.shape
    tr = _row_tile(rows, cols * 4)

    def body(w_ref, g_ref, m_ref, v_ref, d_ref, mo_ref, vo_ref, go_ref):
        g = g_ref[...]
        delta, m_new, v_new = _adamw_math(w_ref[...], g, m_ref[...], v_ref[...])
        d_ref[...] = delta
        mo_ref[...] = m_new
        vo_ref[...] = v_new
        go_ref[...] = g

    blk = pl.BlockSpec((tr, cols), lambda i: (i, 0))
    shp = jax.ShapeDtypeStruct((rows, cols), F32)
    return pl.pallas_call(
        body, name=name, grid=(rows // tr,),
        in_specs=[blk] * 4, out_specs=[blk] * 4, out_shape=[shp] * 4,
        compiler_params=_params(("parallel",)),
    )(w, g, m, v)


_DOT_DIMS = {"nn": ((1,), (0,)), "nt": ((1,), (1,)), "tn": ((0,), (0,))}


def _matmul(name, mode, a_list, b_list, acc_of, m, n, k, tm, tn, tk, extras, out_dtypes, epilogue,
            a_koff=None, b_koff=None, after=(), a_specs=None, b_specs=None, a_single_buffer=False):
    after = tuple(after)
    assert m % tm == 0 and n % tn == 0 and k % tk == 0, (name, m, n, k, tm, tn, tk)
    nk = k // tk
    n_acc = max(acc_of) + 1
    n_pairs = len(a_list)
    a_koff = a_koff or [0] * n_pairs
    b_koff = b_koff or [0] * n_pairs
    dims = (_DOT_DIMS[mode], ((), ()))
    n_ext, n_out = len(extras), len(out_dtypes)

    def body(*refs):
        a_refs = refs[:n_pairs]
        b_refs = refs[n_pairs:2 * n_pairs]
        e_refs = refs[2 * n_pairs:2 * n_pairs + n_ext]
        first_out = 2 * n_pairs + n_ext + len(after)
        o_refs = refs[first_out:first_out + n_out]
        acc_refs = refs[first_out + n_out:]

        parts = [None] * n_acc
        for p in range(n_pairs):
            d = lax.dot_general(a_refs[p][...], b_refs[p][...], dims, preferred_element_type=F32)
            parts[acc_of[p]] = d if parts[acc_of[p]] is None else parts[acc_of[p]] + d

        def finish(accs):
            outs = epilogue(accs, [e[...] for e in e_refs])
            for o_ref, o in zip(o_refs, outs):
                o_ref[...] = o.astype(o_ref.dtype)

        if nk == 1:
            finish(parts)
        else:
            kk = pl.program_id(2)

            @pl.when(kk == 0)
            def _():
                for acc_ref, part in zip(acc_refs, parts):
                    acc_ref[...] = part

            @pl.when(kk != 0)
            def _():
                for acc_ref, part in zip(acc_refs, parts):
                    acc_ref[...] += part

            @pl.when(kk == nk - 1)
            def _():
                finish([acc_ref[...] for acc_ref in acc_refs])

    def a_spec(off):
        mode_a = pl.Buffered(1) if a_single_buffer else None
        if mode == "tn":
            return pl.BlockSpec((tk, tm), lambda i, j, kk: (kk + off, i), pipeline_mode=mode_a)
        return pl.BlockSpec((tm, tk), lambda i, j, kk: (i, kk + off), pipeline_mode=mode_a)

    def b_spec(off):
        if mode == "nt":
            return pl.BlockSpec((tn, tk), lambda i, j, kk: (j, kk + off))
        return pl.BlockSpec((tk, tn), lambda i, j, kk: (kk + off, j))

    tile = pl.BlockSpec((tm, tn), lambda i, j, kk: (i, j))
    scratch = [pltpu.VMEM((tm, tn), F32) for _ in range(n_acc)] if nk > 1 else []
    return pl.pallas_call(
        body, name=name, grid=(m // tm, n // tn, nk),
        in_specs=(a_specs or [a_spec(o) for o in a_koff]) + (b_specs or [b_spec(o) for o in b_koff])
        + [tile] * n_ext + [ANY] * len(after),
        out_specs=[tile] * n_out,
        out_shape=[jax.ShapeDtypeStruct((m, n), dt) for dt in out_dtypes],
        scratch_shapes=scratch,
        compiler_params=_params(("parallel", "parallel", "arbitrary")),
    )(*a_list, *b_list, *extras, *after)


def _epi_plain(accs, extras):
    return (accs[0],)


def _epi_residual(accs, extras):
    return (accs[0] + extras[0],)


def _epi_two(accs, extras):
    return accs[0], accs[1]


def _epi_swiglu(accs, extras):
    g, u = accs
    return g, u, g * _sigmoid(g) * u


def _epi_swiglu_bwd(accs, extras):
    da = accs[0]
    g, u = (e.astype(F32) for e in extras)
    sg = _sigmoid(g)
    dg = da * u * sg * (1.0 + g * (1.0 - sg))
    du = da * g * sg
    return dg, du


_NT_DIMS = (((1,), (1,)), ((), ()))
_TN_DIMS = (((0,), (0,)), ((), ()))


def _tile_delta(tq, tk):
    return lax.broadcasted_iota(I32, (tq, tk), 0) - lax.broadcasted_iota(I32, (tq, tk), 1)


def _attn_log_count(delta):
    count = jnp.zeros(delta.shape, I32)
    for window, dilation in DILATED_PATTERNS:
        hit = ((delta & (dilation - 1)) == 0) & (delta <= window)
        count = count + jnp.where(hit, 1, 0)
    valid = (delta >= 0) & (count > 0)
    logm = jnp.where(count == 3, math.log(3.0), jnp.where(count == 2, math.log(2.0), 0.0))
    return jnp.where(valid, logm, NEG_BIG)


def _fill_attn_log_count(tab_ref):
    nb, t, _ = tab_ref.shape
    base = _tile_delta(t, t)
    for b in range(nb):
        tab_ref[b] = _attn_log_count(base + b * t)


def _fill_attn_bias(tab_ref, log_count_ref, slope):
    nb, t, _ = tab_ref.shape
    dist = _tile_delta(t, t).astype(F32)
    for b in range(nb):
        tab_ref[b] = log_count_ref[b] - slope * (dist + float(b * t))


def _fill_ret_decay(tab_ref, log_gamma):
    nb, t, _ = tab_ref.shape
    base = _tile_delta(t, t)
    for b in range(nb):
        tab_ref[b] = _ret_decay(base + b * t, log_gamma)


def _alibi_slopes():
    return [2.0 ** (-8.0 * (h + 1) / ATTN_HEADS) for h in range(ATTN_HEADS)]


def _attn_fwd(proj, after=()):
    s = proj.shape[0]
    t = SEQ_TILE
    hd = ATTN_HEAD_DIM
    hp = ATTN_FWD_HEADS_PER_STEP
    ng = ATTN_HEADS // hp
    w = hp * hd
    scale = 1.0 / math.sqrt(hd)
    slopes = _alibi_slopes()

    def body(q_ref, k_ref, v_ref, *rest):
        mix_ref, o_ref, lse_ref, kb, vb, bias_tab, log_count_tab = rest[len(after):]
        g = pl.program_id(0)
        i = pl.program_id(1)

        @pl.when((g == 0) & (i == 0))
        def _():
            _fill_attn_log_count(log_count_tab)

        @pl.when(i == 0)
        def _():
            kb[...] = k_ref[...].astype(BF16)
            vb[...] = v_ref[...].astype(BF16)
            for u in range(hp):
                _fill_attn_bias(bias_tab.at[u], log_count_tab, _select_by_index(g * hp + u, slopes))

        qs = [q_ref[:, u * hd:(u + 1) * hd].astype(BF16) for u in range(hp)]

        def step(j, carry):
            rows = pl.ds(pl.multiple_of(j * t, t), t)
            out = []
            for u in range(hp):
                m_i, l_i, acc = carry[u]
                lanes = slice(u * hd, (u + 1) * hd)
                sc = lax.dot_general(qs[u], kb[rows, lanes], _NT_DIMS, preferred_element_type=F32) * scale
                sc = sc + bias_tab[u, i - j]
                m_new = jnp.maximum(m_i, jnp.max(sc, axis=-1, keepdims=True))
                p = jnp.exp(sc - m_new)
                alpha = jnp.exp(m_i - m_new)
                l_new = alpha * l_i + jnp.sum(p, axis=-1, keepdims=True)
                acc = alpha * acc + jnp.dot(p.astype(BF16), vb[rows, lanes], preferred_element_type=F32)
                out.append((m_new, l_new, acc))
            return tuple(out)

        init = (jnp.full((t, 1), NEG_BIG, F32), jnp.zeros((t, 1), F32), jnp.zeros((t, hd), F32))
        final = lax.fori_loop(0, i + 1, step, (init,) * hp)
        for u in range(hp):
            m_i, l_i, acc = final[u]
            lanes = slice(u * hd, (u + 1) * hd)
            out = acc / l_i
            o_ref[:, lanes] = out
            mix_ref[:, lanes] = out.astype(BF16)
            lse_ref[:, lanes] = jnp.broadcast_to(m_i + jnp.log(l_i), (t, hd))

    return pl.pallas_call(
        body, name="attn_fwd", grid=(ng, s // t),
        in_specs=[pl.BlockSpec((t, w), lambda g, i: (i, g)),
                  pl.BlockSpec((s, w), lambda g, i: (0, ng + g)),
                  pl.BlockSpec((s, w), lambda g, i: (0, 2 * ng + g))] + [ANY] * len(after),
        out_specs=[pl.BlockSpec((None, t, w), lambda g, i: (0, i, g))] + [pl.BlockSpec((t, w), lambda g, i: (i, g))] * 2,
        out_shape=[jax.ShapeDtypeStruct((2, s, ATTN_WIDTH), BF16),
                   jax.ShapeDtypeStruct((s, ATTN_WIDTH), F32),
                   jax.ShapeDtypeStruct((s, ATTN_WIDTH), F32)],
        scratch_shapes=[pltpu.VMEM((s, w), BF16), pltpu.VMEM((s, w), BF16), pltpu.VMEM((hp, s // t, t, t), F32),
                        pltpu.VMEM((s // t, t, t), F32)],
        compiler_params=_params(("arbitrary", "arbitrary")),
    )(proj, proj, proj, *after)


def _attn_bwd(proj, attn_out, lse, dmixed, after=()):
    after = tuple(after)
    s = proj.shape[0]
    t = SEQ_TILE
    nt = s // t
    hd = ATTN_HEAD_DIM
    hp = ATTN_HEADS_PER_STEP
    ng = ATTN_HEADS // hp
    w = hp * hd
    scale = 1.0 / math.sqrt(hd)
    slopes = _alibi_slopes()

    def body(q_ref, k_ref, v_ref, o_ref, lse_ref, do_ref, *rest):
        dsec_ref, qb, kb, vb, dob, dsum, dq_acc, bias_tab, log_count_tab = rest[len(after):]
        g = pl.program_id(0)

        @pl.when(g == 0)
        def _():
            _fill_attn_log_count(log_count_tab)

        qb[...] = q_ref[...].astype(BF16)
        kb[...] = k_ref[...].astype(BF16)
        vb[...] = v_ref[...].astype(BF16)
        dob[...] = do_ref[...].astype(BF16)
        for u in range(hp):
            lanes = slice(u * hd, (u + 1) * hd)
            _fill_attn_bias(bias_tab.at[u], log_count_tab, _select_by_index(g * hp + u, slopes))
            rowsum = jnp.sum(do_ref[:, lanes] * o_ref[:, lanes], axis=-1, keepdims=True)
            dsum[:, lanes] = jnp.broadcast_to(rowsum, (s, hd))
        dq_acc[...] = jnp.zeros((s, w), F32)

        def over_keys(j, _):
            krows = pl.ds(pl.multiple_of(j * t, t), t)

            def over_queries(i, carry):
                qrows = pl.ds(pl.multiple_of(i * t, t), t)
                out = []
                for u in range(hp):
                    dk, dv = carry[u]
                    lanes = slice(u * hd, (u + 1) * hd)
                    qi, doi = qb[qrows, lanes], dob[qrows, lanes]
                    kj, vj = kb[krows, lanes], vb[krows, lanes]
                    lse_i = lse_ref[qrows, lanes][:, :1]
                    dsum_i = dsum[qrows, lanes][:, :1]
                    sc = lax.dot_general(qi, kj, _NT_DIMS, preferred_element_type=F32) * scale
                    p = jnp.exp(sc + bias_tab[u, i - j] - lse_i)
                    dp = lax.dot_general(doi, vj, _NT_DIMS, preferred_element_type=F32)
                    ds = (p * (dp - dsum_i)).astype(BF16)
                    dv = dv + lax.dot_general(p.astype(BF16), doi, _TN_DIMS, preferred_element_type=F32)
                    dk = dk + lax.dot_general(ds, qi, _TN_DIMS, preferred_element_type=F32)
                    dq_acc[qrows, lanes] += jnp.dot(ds, kj, preferred_element_type=F32)
                    out.append((dk, dv))
                return tuple(out)

            zero = jnp.zeros((t, hd), F32)
            final = lax.fori_loop(j, nt, over_queries, ((zero, zero),) * hp)
            for u in range(hp):
                lanes = slice(u * hd, (u + 1) * hd)
                dsec_ref[1, krows, lanes] = (final[u][0] * scale).astype(BF16)
                dsec_ref[2, krows, lanes] = final[u][1].astype(BF16)
            return 0

        lax.fori_loop(0, nt, over_keys, 0)
        dsec_ref[0] = (dq_acc[...] * scale).astype(BF16)

    def col(off):
        return pl.BlockSpec((s, w), lambda g: (0, off + g))

    return pl.pallas_call(
        body, name="attn_bwd", grid=(ng,),
        in_specs=[col(0), col(ng), col(2 * ng), col(0), col(0), col(0)] + [ANY] * len(after),
        out_specs=pl.BlockSpec((4, s, w), lambda g: (0, 0, g)),
        out_shape=jax.ShapeDtypeStruct((8, s, ATTN_WIDTH), BF16),
        scratch_shapes=[pltpu.VMEM((s, w), BF16)] * 4 + [pltpu.VMEM((s, w), F32)] * 2
        + [pltpu.VMEM((hp, nt, t, t), F32), pltpu.VMEM((nt, t, t), F32)],
        compiler_params=_params(("arbitrary",)),
    )(proj, proj, proj, attn_out, lse, dmixed, *after)


def _ret_log_gammas():
    return [math.log(1.0 - 2.0 ** (-5.0 - h)) for h in range(RET_HEADS)]


def _ret_decay(delta, log_gamma):
    dec = jnp.exp(delta.astype(F32) * log_gamma) * (1.0 / math.sqrt(RET_HEAD_DIM))
    return jnp.where(delta >= 0, dec, 0.0)


def _ret_fwd(proj, mixed, after=()):
    after = tuple(after)
    s = proj.shape[0]
    t = SEQ_TILE
    hd = RET_HEAD_DIM
    nh = RET_HEADS
    log_gammas = _ret_log_gammas()
    c0 = 3 * ATTN_WIDTH // hd

    def body(q_ref, k_ref, v_ref, g_ref, *rest):
        mix_ref, raw_ref, kb, vb, decay_tab = rest[1 + len(after):]
        h = pl.program_id(0)
        i = pl.program_id(1)

        @pl.when(i == 0)
        def _():
            kb[...] = k_ref[...].astype(BF16)
            vb[...] = v_ref[...].astype(BF16)
            _fill_ret_decay(decay_tab, _select_by_index(h, log_gammas))

        q = q_ref[...].astype(BF16)

        def step(j, acc):
            rows = pl.ds(pl.multiple_of(j * t, t), t)
            sc = lax.dot_general(q, kb[rows, :], _NT_DIMS, preferred_element_type=F32) * decay_tab[i - j]
            return acc + jnp.dot(sc.astype(BF16), vb[rows, :], preferred_element_type=F32)

        ret = lax.fori_loop(0, i + 1, step, jnp.zeros((t, hd), F32))
        raw_ref[...] = ret
        r = lax.rsqrt(jnp.mean(ret * ret, axis=-1, keepdims=True) + NORM_EPS)
        g = g_ref[...]
        mix_ref[...] = (g * _sigmoid(g) * (ret * r)).astype(BF16)

    return pl.pallas_call(
        body, name="ret_fwd", grid=(nh, s // t),
        in_specs=[pl.BlockSpec((t, hd), lambda h, i: (i, c0 + h)),
                  pl.BlockSpec((s, hd), lambda h, i: (0, c0 + nh + h)),
                  pl.BlockSpec((s, hd), lambda h, i: (0, c0 + 2 * nh + h)),
                  pl.BlockSpec((t, hd), lambda h, i: (i, c0 + 3 * nh + h))] + [ANY] * (1 + len(after)),
        out_specs=[pl.BlockSpec((None, t, hd), lambda h, i: (1, i, h)), pl.BlockSpec((t, hd), lambda h, i: (i, h))],
        out_shape=[jax.ShapeDtypeStruct(mixed.shape, BF16), jax.ShapeDtypeStruct((s, RET_WIDTH), F32)],
        input_output_aliases={4: 0},
        scratch_shapes=[pltpu.VMEM((s, hd), BF16), pltpu.VMEM((s, hd), BF16), pltpu.VMEM((s // t, t, t), F32)],
        compiler_params=_params(("arbitrary", "arbitrary")),
    )(proj, proj, proj, proj, mixed, *after)


def _ret_bwd(proj, ret_raw, dmixed, dsec, after=()):
    after = tuple(after)
    s = proj.shape[0]
    t = SEQ_TILE
    nt = s // t
    hd = RET_HEAD_DIM
    nh = RET_HEADS
    log_gammas = _ret_log_gammas()
    c0 = 3 * ATTN_WIDTH // hd
    mixed_blocks = ATTN_WIDTH // hd

    def body(q_ref, k_ref, v_ref, g_ref, raw_ref, dmix_ref, *rest):
        dsec_ref, qb, kb, vb, dretb, dq_acc, decay_tab = rest[1 + len(after):]
        h = pl.program_id(0)
        _fill_ret_decay(decay_tab, _select_by_index(h, log_gammas))
        qb[...] = q_ref[...].astype(BF16)
        kb[...] = k_ref[...].astype(BF16)
        vb[...] = v_ref[...].astype(BF16)
        ret = raw_ref[...]
        r = lax.rsqrt(jnp.mean(ret * ret, axis=-1, keepdims=True) + NORM_EPS)
        normed = ret * r
        g = g_ref[...]
        sg = _sigmoid(g)
        dout = dmix_ref[...]
        dsec_ref[3] = (dout * normed * sg * (1.0 + g * (1.0 - sg))).astype(BF16)
        dn = dout * g * sg
        dret = r * (dn - normed * jnp.mean(dn * normed, axis=-1, keepdims=True))
        dretb[...] = dret.astype(BF16)
        dq_acc[...] = jnp.zeros((s, hd), F32)

        def over_keys(j, _):
            krows = pl.ds(pl.multiple_of(j * t, t), t)
            kj = kb[krows, :]
            vj = vb[krows, :]

            def over_queries(i, carry):
                dk, dv = carry
                qrows = pl.ds(pl.multiple_of(i * t, t), t)
                qi = qb[qrows, :]
                doi = dretb[qrows, :]
                dec = decay_tab[i - j]
                a = (lax.dot_general(qi, kj, _NT_DIMS, preferred_element_type=F32) * dec).astype(BF16)
                da = (lax.dot_general(doi, vj, _NT_DIMS, preferred_element_type=F32) * dec).astype(BF16)
                dv = dv + lax.dot_general(a, doi, _TN_DIMS, preferred_element_type=F32)
                dk = dk + lax.dot_general(da, qi, _TN_DIMS, preferred_element_type=F32)
                dq_acc[qrows, :] += jnp.dot(da, kj, preferred_element_type=F32)
                return dk, dv

            zero = jnp.zeros((t, hd), F32)
            dk, dv = lax.fori_loop(j, nt, over_queries, (zero, zero))
            dsec_ref[1, krows, :] = dk.astype(BF16)
            dsec_ref[2, krows, :] = dv.astype(BF16)
            return 0

        lax.fori_loop(0, nt, over_keys, 0)
        dsec_ref[0] = dq_acc[...].astype(BF16)

    def col(off):
        return pl.BlockSpec((s, hd), lambda h: (0, off + h))

    return pl.pallas_call(
        body, name="ret_bwd", grid=(nh,),
        in_specs=[col(c0), col(c0 + nh), col(c0 + 2 * nh), col(c0 + 3 * nh), col(0), col(mixed_blocks)]
        + [ANY] * (1 + len(after)),
        out_specs=pl.BlockSpec((4, s, hd), lambda h: (1, 0, h)),
        out_shape=jax.ShapeDtypeStruct(dsec.shape, BF16),
        input_output_aliases={6: 0},
        scratch_shapes=[pltpu.VMEM((s, hd), BF16)] * 4 + [pltpu.VMEM((s, hd), F32)]
        + [pltpu.VMEM((nt, t, t), F32)],
        compiler_params=_params(("arbitrary",)),
    )(proj, proj, proj, proj, ret_raw, dmixed, dsec, *after)


_FLIPS = (2, 1, 3)


def _other_chips(x, y):
    return [(1 - x, y), (x, 1 - y), (1 - x, 1 - y)]


_HBM = pl.BlockSpec(memory_space=pltpu.HBM)
_SEM = pl.BlockSpec(memory_space=pltpu.SEMAPHORE)
_EFFECT = pltpu.SideEffectType.DATAFLOW_SIDE_EFFECTING


def _in_hbm(a):
    return pltpu.with_memory_space_constraint(a, pltpu.HBM)


def _weight_view(w, column_sharded):
    if column_sharded:
        return w.reshape(2, w.shape[0] // 2, w.shape[1])
    return w.reshape(N_CHIPS, 2, w.shape[0] // (2 * N_CHIPS), w.shape[1])


def _weight_unview(v):
    if v.ndim == 3:
        return v.reshape(2 * v.shape[1], v.shape[2])
    return v.reshape(N_CHIPS * 2 * v.shape[2], v.shape[3])


def _weight_region(buf, shard, half):
    if len(buf.shape) == 3:
        cols = buf.shape[2] // N_CHIPS
        return buf.at[half, :, pl.ds(shard * cols, cols)]
    return buf.at[shard, half]


def _remote(where, send_sem, recv_sem, to):
    return pltpu.make_async_remote_copy(src_ref=where, dst_ref=where, send_sem=send_sem, recv_sem=recv_sem,
                                        device_id=to, device_id_type=MESH)


def _for_my_shard(fn):
    x, y, _ = _place()
    for ss in range(N_CHIPS):
        pl.when(2 * x + y == ss)(functools.partial(fn, ss))


def _gather_start(views, name, after=()):
    n_w = len(views)
    after = tuple(after)

    def body(*refs):
        send_sems, recv_sems = refs[n_w + len(after):n_w + len(after) + 2]
        bufs = refs[n_w + len(after) + 2:]
        x, y, c = _place()

        def start(ss):
            for w in range(n_w):
                for j, chip in enumerate(_other_chips(x, y)):
                    _remote(_weight_region(bufs[w], ss, c), send_sems.at[3 * w + j], recv_sems.at[3 * w + j],
                            (*chip, c)).start()

        _for_my_shard(start)

    return pl.pallas_call(
        body, name=name,
        in_specs=[_HBM] * n_w + [ANY] * len(after), out_specs=[_SEM, _SEM] + [_HBM] * n_w,
        out_shape=[pltpu.SemaphoreType.DMA((3 * n_w,)), pltpu.SemaphoreType.DMA((3 * n_w,))]
        + [pltpu.HBM(v.shape, BF16) for v in views],
        input_output_aliases={w: 2 + w for w in range(n_w)},
        compiler_params=pltpu.CompilerParams(has_side_effects=_EFFECT),
    )(*[_in_hbm(v) for v in views], *after)


def _gather_forward(views, which, send_sems, recv_sems, after, name, base=0):
    n_w = len(views)
    which = [base // 3 + w for w in which] if base % 3 == 0 else None
    assert which is not None, "base must be a multiple of 3"

    def body(*refs):
        send_in, recv_in = refs[n_w:n_w + 2]
        fwd_send, fwd_recv = refs[n_w + 3:n_w + 5]
        bufs = refs[n_w + 5:]
        x, y, c = _place()
        sibling = (x, y, 1 - c)

        def forward(ss):
            for i, w in enumerate(which):
                for j in range(3):
                    landed = _weight_region(bufs[i], ss ^ _FLIPS[j], c)
                    _remote(landed, send_in.at[3 * w + j], recv_in.at[3 * w + j], sibling).wait_recv()
                    _remote(landed, fwd_send.at[3 * i + j], fwd_recv.at[3 * i + j], sibling).start()

        _for_my_shard(forward)
        for i, w in enumerate(which):
            for j in range(3):
                _remote(_weight_region(bufs[i], 0, 0), send_in.at[3 * w + j], recv_in.at[3 * w + j],
                        sibling).wait_send()

    return pl.pallas_call(
        body, name=name,
        in_specs=[_HBM] * n_w + [_SEM, _SEM, ANY], out_specs=[_SEM, _SEM] + [_HBM] * n_w,
        out_shape=[pltpu.SemaphoreType.DMA((3 * n_w,)), pltpu.SemaphoreType.DMA((3 * n_w,))]
        + [pltpu.HBM(v.shape, BF16) for v in views],
        input_output_aliases={w: 2 + w for w in range(n_w)},
        compiler_params=pltpu.CompilerParams(has_side_effects=_EFFECT),
    )(*views, send_sems, recv_sems, after)


def _gather_end(views, fwd_send, fwd_recv, after, name):
    n_w = len(views)

    def body(*refs):
        fwd_send_ref, fwd_recv_ref = refs[n_w:n_w + 2]
        bufs = refs[n_w + 3:]
        x, y, c = _place()
        for i in range(n_w):
            for j in range(3):
                cp = _remote(_weight_region(bufs[i], 0, 0), fwd_send_ref.at[3 * i + j], fwd_recv_ref.at[3 * i + j],
                             (x, y, 1 - c))
                cp.wait_recv()
                cp.wait_send()

    outs = pl.pallas_call(
        body, name=name,
        in_specs=[_HBM] * n_w + [_SEM, _SEM, ANY], out_specs=[_HBM] * n_w,
        out_shape=[pltpu.HBM(v.shape, BF16) for v in views],
        input_output_aliases={w: w for w in range(n_w)},
        compiler_params=pltpu.CompilerParams(has_side_effects=_EFFECT),
    )(*views, fwd_send, fwd_recv, after)
    return [_weight_unview(o) for o in outs]


def _comm_call(name, bufs, sem_pairs, after, n_new, fn):
    n, n_sem, after = len(bufs), 2 * len(sem_pairs), tuple(after)
    n_out_sem = 2 if n_new else 0

    def body(*refs):
        sems = refs[n:n + n_sem]
        outs = refs[n + n_sem + len(after):]
        new = outs[:n_out_sem] if n_new else (None, None)
        fn(outs[n_out_sem:], [(sems[2 * i], sems[2 * i + 1]) for i in range(len(sem_pairs))], *new)

    res = pl.pallas_call(
        body, name=name,
        in_specs=[_HBM] * n + [_SEM] * n_sem + [ANY] * len(after),
        out_specs=[_SEM] * n_out_sem + [_HBM] * n,
        out_shape=[pltpu.SemaphoreType.DMA((n_new,))] * n_out_sem + [pltpu.HBM(b.shape, b.dtype) for b in bufs],
        input_output_aliases={i: n_out_sem + i for i in range(n)},
        compiler_params=pltpu.CompilerParams(has_side_effects=_EFFECT),
    )(*bufs, *[s for pair in sem_pairs for s in pair], *after)
    return list(res[:n_out_sem]), list(res[n_out_sem:])


def _quarter(piece, q):
    rows = piece.shape[0] // 2
    return piece.at[pl.ds(q * rows, rows)]


def _gather_in_start(view, name):
    def fn(bufs, _, send, recv):
        x, y, c = _place()

        def go(ss):
            for j, chip in enumerate(_other_chips(x, y)[:2]):
                _remote(_weight_region(bufs[0], ss, c), send.at[j], recv.at[j], (*chip, c)).start()

        _for_my_shard(go)

    sems, (view,) = _comm_call(name, [_in_hbm(view)], [], (), 2, fn)
    return sems, view


def _gather_out_gate_start(v_out, v_gate, after, name):
    def fn(bufs, _, send, recv):
        x, y, c = _place()
        chips = _other_chips(x, y)

        def go(ss):
            for j in range(3):
                _remote(_weight_region(bufs[0], ss, c), send.at[j], recv.at[j], (*chips[j], c)).start()
            for j in range(2):
                _remote(_weight_region(bufs[1], ss, c), send.at[3 + j], recv.at[3 + j], (*chips[j], c)).start()

        _for_my_shard(go)

    sems, views = _comm_call(name, [_in_hbm(v_out), _in_hbm(v_gate)], [], after, 5, fn)
    return sems, views


def _gather_relay(view, started, base, after, name, then=None, then_peers=0):
    n_new = 6 + then_peers if then_peers else 4

    def fn(bufs, pairs, send, recv):
        (send_in, recv_in), = pairs
        x, y, c = _place()
        chips = _other_chips(x, y)
        sibling = (x, y, 1 - c)

        def go(ss):
            landed = [_weight_region(bufs[0], ss ^ _FLIPS[j], c) for j in range(2)]
            for j in range(2):
                _remote(landed[j], send_in.at[base + j], recv_in.at[base + j], sibling).wait_recv()
            for j in range(2):
                _remote(_quarter(landed[j], j), send.at[j], recv.at[j], (*chips[1 - j], c)).start()
            for j in range(2):
                _remote(landed[j], send.at[2 + j], recv.at[2 + j], sibling).start()
            for j in range(then_peers):
                _remote(_weight_region(bufs[1], ss, c), send.at[6 + j], recv.at[6 + j], (*chips[j], c)).start()

        _for_my_shard(go)
        for j in range(2):
            _remote(_weight_region(bufs[0], 0, 0), send_in.at[base + j], recv_in.at[base + j], sibling).wait_send()

    views = [view] if then is None else [view, _in_hbm(then)]
    sems, views = _comm_call(name, views, [started], after, n_new, fn)
    return sems, views


def _gather_in_neighbours_end(view, relayed, after, name):
    def fn(bufs, pairs, *_):
        (send, recv), = pairs
        x, y, c = _place()
        for j in range(2):
            cp = _remote(_weight_region(bufs[0], 0, 0), send.at[2 + j], recv.at[2 + j], (x, y, 1 - c))
            cp.wait_recv()
            cp.wait_send()

    _, (view,) = _comm_call(name, [view], [relayed], after, 0, fn)
    return view


def _gather_in_diagonal(view, relayed, after, name):
    def fn(bufs, pairs, send, recv):
        (send_in, recv_in), = pairs
        x, y, c = _place()
        sibling = (x, y, 1 - c)
        any_quarter = _quarter(_weight_region(bufs[0], 0, 0), 0)
        for j in range(2):
            cp = _remote(any_quarter, send_in.at[j], recv_in.at[j], sibling)
            cp.wait_recv()
            cp.wait_send()

        def go(ss):
            _remote(_weight_region(bufs[0], ss ^ _FLIPS[2], c), send.at[0], recv.at[0], sibling).start()

        _for_my_shard(go)

    sems, (view,) = _comm_call(name, [view], [relayed], after, 1, fn)
    return sems, view


def _gather_in_diagonal_end(view, forwarded, after, name):
    def fn(bufs, pairs, *_):
        (send, recv), = pairs
        x, y, c = _place()
        cp = _remote(_weight_region(bufs[0], 0, 0), send.at[0], recv.at[0], (x, y, 1 - c))
        cp.wait_recv()
        cp.wait_send()

    _, (view,) = _comm_call(name, [view], [forwarded], after, 0, fn)
    return view


def _in_proj_shard(h1, wi, proj, shard_arr, name):
    s, d = h1.shape
    n = wi.shape[1]
    tn = 256
    blocks = n // (N_CHIPS * tn)
    given = [] if proj is None else [proj]

    def body(shard_ref, h_ref, w_ref, *rest):
        del shard_ref
        rest[-1][...] = jnp.dot(h_ref[...], w_ref[...], preferred_element_type=F32)

    grid_spec = pltpu.PrefetchScalarGridSpec(
        num_scalar_prefetch=1, grid=(blocks,),
        in_specs=[pl.BlockSpec((s, d), lambda j, shard_ref: (0, 0)),
                  pl.BlockSpec((d, tn), lambda j, shard_ref: (0, shard_ref[0] * blocks + j))] + [ANY] * len(given),
        out_specs=pl.BlockSpec((s, tn), lambda j, shard_ref: (0, shard_ref[0] * blocks + j)))
    return pl.pallas_call(
        body, name=name, grid_spec=grid_spec,
        out_shape=jax.ShapeDtypeStruct((s, n), F32),
        input_output_aliases={3: 0} if given else {},
        compiler_params=_params(("arbitrary",)),
    )(shard_arr, h1, wi, *given)


def _split_start(name, bufs, n_sems, copies):
    n = len(bufs)

    def body(*refs):
        send_sems, recv_sems = refs[n:n + 2]
        for cp in copies(refs[n + 2:], send_sems, recv_sems):
            cp.start()

    outs = pl.pallas_call(
        body, name=name,
        in_specs=[_HBM] * n, out_specs=[_SEM, _SEM] + [_HBM] * n,
        out_shape=[pltpu.SemaphoreType.DMA((n_sems,)), pltpu.SemaphoreType.DMA((n_sems,))]
        + [pltpu.HBM(b.shape, b.dtype) for b in bufs],
        input_output_aliases={i: 2 + i for i in range(n)},
        compiler_params=pltpu.CompilerParams(has_side_effects=_EFFECT),
    )(*[_in_hbm(b) for b in bufs])
    return outs[0], outs[1], list(outs[2:])


def _split_wait(name, bufs, send_sems, recv_sems, copies, after):
    n = len(bufs)

    def body(*refs):
        send_ref, recv_ref = refs[n:n + 2]
        for cp in copies(refs[n + 3:], send_ref, recv_ref):
            cp.wait()

    return list(pl.pallas_call(
        body, name=name,
        in_specs=[_HBM] * n + [_SEM, _SEM, ANY], out_specs=[_HBM] * n,
        out_shape=[pltpu.HBM(b.shape, b.dtype) for b in bufs],
        input_output_aliases={i: i for i in range(n)},
        compiler_params=pltpu.CompilerParams(has_side_effects=_EFFECT),
    )(*bufs, send_sems, recv_sems, after))


def _halves_copies(n_w):
    def copies(bufs, send_sems, recv_sems):
        x, y, c = _place()
        out = []
        for w in range(n_w):
            view, land = bufs[w], bufs[n_w + w]
            src = view.at[1 - c] if len(view.shape) == 3 else view.at[:, 1 - c]
            out.append(pltpu.make_async_remote_copy(
                src_ref=src, dst_ref=land, send_sem=send_sems.at[w], recv_sem=recv_sems.at[w],
                device_id=(x, y, 1 - c), device_id_type=MESH))
        return out
    return copies


def _pieces_copies(n_w):
    def copies(bufs, send_sems, recv_sems):
        x, y, c = _place()
        out = []
        for w in range(n_w):
            for j, (cx, cy) in enumerate(_other_chips(x, y)):
                out.append(pltpu.make_async_remote_copy(
                    src_ref=bufs[w].at[2 * cx + cy], dst_ref=bufs[n_w + w].at[j],
                    send_sem=send_sems.at[3 * w + j], recv_sem=recv_sems.at[3 * w + j],
                    device_id=(cx, cy, c), device_id_type=MESH))
        return out
    return copies


def _join_copies(n_w):
    def copies(bufs, send_sems, recv_sems):
        x, y, c = _place()
        return [pltpu.make_async_remote_copy(
            src_ref=bufs[w].at[c], dst_ref=bufs[w].at[c], send_sem=send_sems.at[w], recv_sem=recv_sems.at[w],
            device_id=(x, y, 1 - c), device_id_type=MESH) for w in range(n_w)]
    return copies


def _grad_view(g, column_sharded):
    return _weight_view(g, column_sharded)


def _halves_landing(view):
    shape = view.shape[1:] if view.ndim == 3 else (N_CHIPS,) + view.shape[2:]
    return lax.empty(shape, BF16)


def _halves_start(tag, grads, column_sharded):
    views = [_weight_view(g, cs) for g, cs in zip(grads, column_sharded)]
    n = len(views)
    return _split_start("halves_start_" + tag, views + [_halves_landing(v) for v in views], n, _halves_copies(n))


def _halves_wait(tag, state, after):
    send_sems, recv_sems, bufs = state
    n = len(bufs) // 2
    bufs = _split_wait("halves_wait_" + tag, bufs, send_sems, recv_sems, _halves_copies(n), after)
    return bufs[:n], bufs[n:]


def _pieces_start(tag, pieces):
    n = len(pieces)
    landing = [lax.empty((3,) + p.shape[1:], BF16) for p in pieces]
    return _split_start("pieces_start_" + tag, list(pieces) + landing, 3 * n, _pieces_copies(n))


def _pieces_wait(tag, state, after):
    send_sems, recv_sems, bufs = state
    n = len(bufs) // 2
    bufs = _split_wait("pieces_wait_" + tag, bufs, send_sems, recv_sems, _pieces_copies(n), after)
    return bufs[:n], bufs[n:]


def _join_start(tag, shards):
    n = len(shards)
    return _split_start("join_start_" + tag, list(shards), n, _join_copies(n))


def _join_wait(tag, state, after):
    send_sems, recv_sems, bufs = state
    bufs = _split_wait("join_wait_" + tag, bufs, send_sems, recv_sems, _join_copies(len(bufs)), after)
    return [b.reshape(2 * b.shape[1], b.shape[2]) for b in bufs]


def _chip_sum_col(g3, sib, c_arr, name):
    _, hk, n = g3.shape
    cols = n // N_CHIPS
    tr = _row_tile(hk, cols * 2, limit=4 * 1024 * 1024)

    def body(c_ref, g_ref, s_ref, o_ref):
        del c_ref
        o_ref[...] = (g_ref[...].astype(F32) + s_ref[...].astype(F32)).astype(BF16)

    grid_spec = pltpu.PrefetchScalarGridSpec(
        num_scalar_prefetch=1, grid=(N_CHIPS, hk // tr),
        in_specs=[pl.BlockSpec((None, tr, cols), lambda p, r, c_ref: (c_ref[0], r, p)),
                  pl.BlockSpec((tr, cols), lambda p, r, c_ref: (r, p))],
        out_specs=pl.BlockSpec((None, tr, cols), lambda p, r, c_ref: (p, r, 0)))
    return pl.pallas_call(
        body, name=name, grid_spec=grid_spec,
        out_shape=jax.ShapeDtypeStruct((N_CHIPS, hk, cols), BF16),
        compiler_params=_params(("parallel", "parallel")),
    )(c_arr, g3, sib)


def _chip_sum_row(g4, sib, c_arr, name):
    _, _, hr, n = g4.shape
    tr = _row_tile(hr, n * 2, limit=4 * 1024 * 1024)

    def body(c_ref, g_ref, s_ref, o_ref):
        del c_ref
        o_ref[...] = (g_ref[...].astype(F32) + s_ref[...].astype(F32)).astype(BF16)

    grid_spec = pltpu.PrefetchScalarGridSpec(
        num_scalar_prefetch=1, grid=(N_CHIPS, hr // tr),
        in_specs=[pl.BlockSpec((None, None, tr, n), lambda p, r, c_ref: (p, c_ref[0], r, 0)),
                  pl.BlockSpec((None, tr, n), lambda p, r, c_ref: (p, r, 0))],
        out_specs=pl.BlockSpec((None, tr, n), lambda p, r, c_ref: (p, r, 0)))
    return pl.pallas_call(
        body, name=name, grid_spec=grid_spec,
        out_shape=jax.ShapeDtypeStruct((N_CHIPS, hr, n), BF16),
        compiler_params=_params(("parallel", "parallel")),
    )(c_arr, g4, sib)


def _sum_pieces(pieces, received, place_arr, name):
    _, r, n = pieces.shape
    tr = _row_tile(r, n * 4, limit=4 * 1024 * 1024)

    def body(p_ref, own_ref, r0_ref, r1_ref, r2_ref, o_ref):
        del p_ref
        acc = own_ref[...].astype(F32) + r0_ref[...].astype(F32)
        acc = acc + r1_ref[...].astype(F32)
        o_ref[...] = acc + r2_ref[...].astype(F32)

    def recv_spec(j):
        return pl.BlockSpec((None, tr, n), lambda i, p_ref: (j, i, 0))

    grid_spec = pltpu.PrefetchScalarGridSpec(
        num_scalar_prefetch=1, grid=(r // tr,),
        in_specs=[pl.BlockSpec((None, tr, n), lambda i, p_ref: (p_ref[0], i, 0)),
                  recv_spec(0), recv_spec(1), recv_spec(2)],
        out_specs=pl.BlockSpec((None, tr, n), lambda i, p_ref: (p_ref[1], i, 0)))
    return pl.pallas_call(
        body, name=name, grid_spec=grid_spec,
        out_shape=jax.ShapeDtypeStruct((2, r, n), F32),
        compiler_params=_params(("parallel",)),
    )(place_arr, pieces, received, received, received)


def _norm_weights_step(parts, w, m, v, after=()):
    rows, d = parts.shape
    after = tuple(after)

    def body(p_ref, w_ref, m_ref, v_ref, *rest):
        g_ref, d_ref, mo_ref, vo_ref, gathered, send_sems, recv_sems = rest[len(after):]
        x, y, c = _place()
        me = 4 * x + 2 * y + c
        gathered[me] = p_ref[...]
        copies = []
        for k in range(1, N_DEV):
            peer = (x ^ ((k >> 2) & 1), y ^ ((k >> 1) & 1), c ^ (k & 1))
            copies.append(pltpu.make_async_remote_copy(
                src_ref=p_ref, dst_ref=gathered.at[me], send_sem=send_sems.at[k - 1],
                recv_sem=recv_sems.at[k - 1], device_id=peer, device_id_type=MESH))
        for cp in copies:
            cp.start()
        for cp in copies:
            cp.wait()
        g = gathered[0]
        for k in range(1, N_DEV):
            g = g + gathered[k]
        delta, m_new, v_new = _adamw_math(w_ref[...], g, m_ref[...], v_ref[...])
        g_ref[...] = g
        d_ref[...] = delta
        mo_ref[...] = m_new
        vo_ref[...] = v_new

    vmem = pl.BlockSpec(memory_space=pltpu.VMEM)
    shp = jax.ShapeDtypeStruct((rows, d), F32)
    return pl.pallas_call(
        body, name="norm_weights_step",
        in_specs=[vmem] * 4 + [ANY] * len(after), out_specs=[vmem] * 4, out_shape=[shp] * 4,
        scratch_shapes=[pltpu.VMEM((N_DEV, rows, d), F32), pltpu.SemaphoreType.DMA((N_DEV - 1,)),
                        pltpu.SemaphoreType.DMA((N_DEV - 1,))],
        compiler_params=pltpu.CompilerParams(has_side_effects=True),
    )(parts, w, m, v, *after)


def kernel(x, norm_mix_w, w_in, w_out, norm_ffn_w, w_gate, w_up, w_down, norm_final_w, loss_target, m_norm_mix_w, m_w_in, m_w_out, m_norm_ffn_w, m_w_gate, m_w_up, m_w_down, m_norm_final_w, v_norm_mix_w, v_w_in, v_w_out, v_norm_ffn_w, v_w_gate, v_w_up, v_w_down, v_norm_final_w):
    s, d = x.shape[1], x.shape[2]
    xs = x.reshape(s, d)
    target = loss_target.reshape(s, d)
    big = {"w_in": (w_in, m_w_in, v_w_in), "w_out": (w_out, m_w_out, v_w_out),
           "w_gate": (w_gate, m_w_gate, v_w_gate), "w_up": (w_up, m_w_up, v_w_up),
           "w_down": (w_down, m_w_down, v_w_down)}
    big = {k: tuple(a.reshape(a.shape[1:]) for a in t) for k, t in big.items()}
    col_names, row_names = ("w_in", "w_gate", "w_up"), ("w_out", "w_down")
    n_in = N_CHIPS * big["w_in"][0].shape[1]
    ffn = N_CHIPS * big["w_gate"][0].shape[1]
    mix = ATTN_WIDTH + RET_WIDTH
    c_arr = lax.axis_index("c").astype(I32).reshape(1)
    shard_arr = (2 * lax.axis_index("x") + lax.axis_index("y")).astype(I32).reshape(1)
    place_arr = jnp.concatenate([shard_arr, c_arr])

    def cast(k, after=()):
        return _weight_view(_cast_into_full(big[k][0], shard_arr, k in col_names, "cast_" + k, after), k in col_names)

    started_in, v_in = _gather_in_start(cast("w_in"), "gather_in_start")

    sec = ATTN_WIDTH

    def section(p, rows):
        return pl.BlockSpec((None, rows, sec), lambda i, j, kk: (p, i, 0))

    h1 = _rms_fwd(xs, norm_mix_w, "rms_mix_fwd")
    my_shard = shard_arr[0]
    shard_of = [jnp.bitwise_xor(my_shard, f).astype(I32).reshape(1) for f in (0,) + _FLIPS]
    proj = _in_proj_shard(h1, _weight_unview(v_in), None, shard_of[0], "in_proj_own")
    early_views = [cast(k, after=[proj]) for k in ("w_out", "w_gate")]
    v_up, v_down = [cast(k, after=[proj]) for k in ("w_up", "w_down")]
    relayed_in, (v_in,) = _gather_relay(v_in, started_in, 0, early_views + [v_up, v_down], "gather_in_relay")
    started_og, (v_out, v_gate) = _gather_out_gate_start(*early_views, [v_in], "gather_out_gate_start")
    v_in = _gather_in_neighbours_end(v_in, relayed_in, [v_out], "gather_in_neighbours_end")
    proj = _in_proj_shard(h1, _weight_unview(v_in), proj, shard_of[1], "in_proj_x")
    proj = _in_proj_shard(h1, _weight_unview(v_in), proj, shard_of[2], "in_proj_y")
    forwarded_in, v_in = _gather_in_diagonal(v_in, relayed_in, [proj], "gather_in_diagonal")
    wi = _weight_unview(_gather_in_diagonal_end(v_in, forwarded_in, [proj], "gather_in_diagonal_end"))
    proj = _in_proj_shard(h1, wi, proj, shard_of[3], "in_proj_diagonal")
    fs_o, fr_o, v_out = _gather_forward([v_out], [0], *started_og, proj, "gather_forward_out")
    mixed, attn_o, lse = _attn_fwd(proj, after=[v_out])
    relayed_g, (v_gate, v_up) = _gather_relay(v_gate, started_og, 3, [attn_o], "gather_gate_relay",
                                              then=v_up, then_peers=2)
    mixed, ret_raw = _ret_fwd(proj, mixed, after=[v_gate])
    wo, = _gather_end([v_out], fs_o, fr_o, ret_raw, "gather_end_out")
    x1, = _matmul("out_proj", "nn", [mixed, mixed], [wo, wo], [0, 0], s, d, sec, s, 512, sec, [xs], [F32],
                  _epi_residual, b_koff=[0, 1], a_specs=[section(0, s), section(1, s)])
    h2 = _rms_fwd(x1, norm_ffn_w, "rms_ffn_fwd")
    relayed_u, (v_up, v_down) = _gather_relay(v_up, relayed_g, 6, [h2], "gather_up_relay",
                                              then=v_down, then_peers=3)
    v_gate = _gather_in_neighbours_end(v_gate, relayed_g, [v_up], "gather_gate_neighbours_end")
    forwarded_g, v_gate = _gather_in_diagonal(v_gate, relayed_g, [v_up], "gather_gate_diagonal")
    v_up = _gather_in_neighbours_end(v_up, relayed_u, [v_gate], "gather_up_neighbours_end")
    wg = _weight_unview(_gather_in_diagonal_end(v_gate, forwarded_g, [v_up], "gather_gate_diagonal_end"))
    forwarded_u, v_up = _gather_in_diagonal(v_up, relayed_u, [wg], "gather_up_diagonal")
    wu = _weight_unview(_gather_in_diagonal_end(v_up, forwarded_u, [wg], "gather_up_diagonal_end"))
    gate, up, act = _matmul("gate_up", "nn", [h2, h2], [wg, wu], [0, 1], s, ffn, d, s, 256, d, [],
                            [BF16, BF16, BF16], _epi_swiglu)
    fs, fr, v_down = _gather_forward([v_down], [0], *relayed_u, act, "gather_forward_down", base=6)
    wd, = _gather_end([v_down], fs, fr, act, "gather_end_down")
    x2, = _matmul("down_proj", "nn", [act], [wd], [0], s, d, ffn, s // 2, 256, ffn, [x1], [F32],
                  _epi_residual)
    loss_row, dx2, dx2b, dwf = _final_norm_loss(x2, norm_final_w.reshape(1, d), target, "final_norm_loss")

    names = col_names + row_names
    grads, new = {}, {}

    def chip_sums(tag_names, views, sibs):
        return [(_chip_sum_col if k in col_names else _chip_sum_row)(v, sb, c_arr, "chip_sum_" + k)
                for k, v, sb in zip(tag_names, views, sibs)]

    def piece_sums(tag_names, pieces, received):
        return [_sum_pieces(p, r, place_arr, "sum_pieces_" + k) for k, p, r in zip(tag_names, pieces, received)]

    def update(k):
        new[k] = _adamw(big[k][0], grads[k], big[k][1], big[k][2], "adamw_" + k)

    dgate, dup = _matmul("d_act", "nt", [dx2b], [wd], [0], s, ffn, d, s, 256, d, [gate, up],
                         [BF16, BF16], _epi_swiglu_bwd)
    g_wd, = _matmul("g_w_down", "tn", [act], [dx2b], [0], ffn, d, s, 512, d, s, [], [BF16], _epi_plain)
    halves_d = _halves_start("down", [g_wd], [False])
    dh2, = _matmul("d_h2", "nt", [dgate, dup], [wg, wu], [0, 0], s, d, ffn, s // 2, 256, ffn, [], [F32],
                   _epi_plain, after=halves_d[2][-1:], a_single_buffer=True)
    pieces_d = _pieces_start("down", chip_sums(["w_down"], *_halves_wait("down", halves_d, dh2)))
    g_wg, g_wu = _matmul("g_w_gate_up", "tn", [h2, h2], [dgate, dup], [0, 1], d, ffn, s, 1024, 512, s, [],
                         [BF16, BF16], _epi_two, after=pieces_d[2][-1:])
    halves_gu = _halves_start("gate_up", [g_wg, g_wu], [True, True])
    dx1, dx1b, dw_ffn = _rms_bwd(x1, norm_ffn_w, dh2, dx2, "rms_ffn_bwd", after=halves_gu[2][-1:])

    dmixed, = _matmul("d_mixed", "nt", [dx1b], [wo], [0], s, mix, d, s, 512, d, [], [F32], _epi_plain)
    pieces_gu = _pieces_start("gate_up", chip_sums(["w_gate", "w_up"], *_halves_wait("gate_up", halves_gu, dmixed)))
    per = sec // 512
    g_wo, = _matmul("g_w_out", "tn", [mixed], [dx1b], [0], mix, d, s, 512, d, s, [], [BF16], _epi_plain,
                    after=pieces_gu[2][-1:],
                    a_specs=[pl.BlockSpec((None, s, 512), lambda i, j, kk: (i // per, 0, i % per))])
    halves_o = _halves_start("out", [g_wo], [False])
    dsec = _attn_bwd(proj, attn_o, lse, dmixed, after=halves_o[2][-1:])
    pieces_o = _pieces_start("out", chip_sums(["w_out"], *_halves_wait("out", halves_o, dsec)))
    dsec = _ret_bwd(proj, ret_raw, dmixed, dsec, after=pieces_o[2][-1:])
    where = [0, 1, 2, 4, 5, 6, 7]
    n_sec = len(where)
    g_wi, = _matmul("g_w_in", "tn", [h1], [dsec], [0], d, n_in, s, 1024, sec, s, [], [BF16], _epi_plain,
                    b_specs=[pl.BlockSpec((None, s, sec), lambda i, j, kk: (j + (j >= 3).astype(I32), 0, 0))])
    halves_i = _halves_start("in", [g_wi], [True])
    dh1, = _matmul("d_h1", "nt", [dsec] * n_sec, [wi] * n_sec, [0] * n_sec, s, d, sec, s // 2, 256, sec, [], [F32],
                   _epi_plain, b_koff=list(range(n_sec)), after=halves_i[2][-1:],
                   a_specs=[section(p, s // 2) for p in where])
    pieces_i = _pieces_start("in", chip_sums(["w_in"], *_halves_wait("in", halves_i, dh1)))
    grad_x, _, dw_mix = _rms_bwd(xs, norm_mix_w, dh1, dx1, "rms_mix_bwd", after=pieces_i[2][-1:])

    def rows8(*vs):
        return jnp.concatenate([v.reshape(1, d) for v in vs] + [jnp.zeros((8 - len(vs), d), F32)], axis=0)

    join_d = _join_start("down", piece_sums(["w_down"], *_pieces_wait("down", pieces_d, grad_x)))
    join_gu = _join_start("gate_up", piece_sums(["w_gate", "w_up"], *_pieces_wait("gate_up", pieces_gu, join_d[2][0])))
    join_o = _join_start("out", piece_sums(["w_out"], *_pieces_wait("out", pieces_o, join_gu[2][0])))
    grads["w_down"], = _join_wait("down", join_d, join_o[2][0])
    update("w_down")
    grads["w_gate"], grads["w_up"] = _join_wait("gate_up", join_gu, new["w_down"][0])
    update("w_gate")
    update("w_up")
    grads["w_out"], = _join_wait("out", join_o, new["w_up"][0])
    update("w_out")
    join_i = _join_start("in", piece_sums(["w_in"], *_pieces_wait("in", pieces_i, new["w_out"][0])))
    ng, nd, nm, nv = _norm_weights_step(
        rows8(dw_mix, dw_ffn, dwf, jnp.broadcast_to(loss_row[:, :1], (1, d))),
        rows8(norm_mix_w, norm_ffn_w, norm_final_w),
        rows8(m_norm_mix_w, m_norm_ffn_w, m_norm_final_w), rows8(v_norm_mix_w, v_norm_ffn_w, v_norm_final_w),
        after=join_i[2][:1])
    grads["w_in"], = _join_wait("in", join_i, ng)
    update("w_in")

    loss = ng[3, 0]

    def pack(small, per_weight):
        lead = lambda a: a.reshape((1,) + a.shape)
        return (small[0:1], lead(per_weight["w_in"]), lead(per_weight["w_out"]), small[1:2],
                lead(per_weight["w_gate"]), lead(per_weight["w_up"]), lead(per_weight["w_down"]), small[2])

    return (loss, grad_x.reshape(1, s, d),
            *pack(ng, {k: new[k][3] for k in names}),
            *pack(nd, {k: new[k][0] for k in names}),
            *pack(nm, {k: new[k][1] for k in names}),
            *pack(nv, {k: new[k][2] for k in names}))
```

```python
import functools
import math

import jax
import jax.numpy as jnp
from jax import lax
from jax.experimental import pallas as pl
from jax.experimental.pallas import tpu as pltpu

F32 = jnp.float32
BF16 = jnp.bfloat16
I32 = jnp.int32
MESH = pl.DeviceIdType.MESH
ANY = pl.BlockSpec(memory_space=pl.ANY)

ATTN_HEADS = 8
ATTN_HEAD_DIM = 128
RET_HEADS = 4
RET_HEAD_DIM = 256
ATTN_WIDTH = ATTN_HEADS * ATTN_HEAD_DIM
RET_WIDTH = RET_HEADS * RET_HEAD_DIM
DILATED_PATTERNS = ((128, 1), (512, 4), (2048, 16))
NORM_EPS = 1e-6
ADAM_LR = 0.001
ADAM_B1 = 0.9
ADAM_B2 = 0.999
ADAM_EPS = 1e-08
ADAM_WD = 0.01
ADAM_STEP = 10

N_CHIPS = 4
N_DEV = 8
NEG_BIG = -1e30
SEQ_TILE = 512
ATTN_FWD_HEADS_PER_STEP = 2
ATTN_HEADS_PER_STEP = 1
VMEM_LIMIT_BYTES = 56 * 1024 * 1024


def _params(semantics=None, vmem=VMEM_LIMIT_BYTES):
    return pltpu.CompilerParams(dimension_semantics=semantics, vmem_limit_bytes=vmem)


def _row_tile(rows, row_bytes, limit=2 * 1024 * 1024, mult=16):
    best = None
    for t in range(mult, rows + 1, mult):
        if rows % t == 0 and t * row_bytes <= limit:
            best = t
    assert best is not None, (rows, row_bytes)
    return best


def _sigmoid(x):
    return 1.0 / (1.0 + jnp.exp(-x))


def _select_by_index(idx, values):
    out = jnp.float32(values[-1])
    for i in range(len(values) - 2, -1, -1):
        out = jnp.where(idx == i, jnp.float32(values[i]), out)
    return out


def _place():
    x, y, c = lax.axis_index("x"), lax.axis_index("y"), lax.axis_index("c")
    return x, y, c


def _cast_into_full(w, shard_arr, column_sharded, name, after=()):
    after = tuple(after)
    rows, cols = w.shape
    tr = _row_tile(rows, cols * 4)
    steps = rows // tr
    if column_sharded:
        out_shape, out_map = (rows, N_CHIPS * cols), (lambda i, s_ref: (i, s_ref[0]))
    else:
        out_shape, out_map = (N_CHIPS * rows, cols), (lambda i, s_ref: (s_ref[0] * steps + i, 0))

    def body(s_ref, w_ref, *rest):
        del s_ref
        rest[-1][...] = w_ref[...].astype(BF16)

    grid_spec = pltpu.PrefetchScalarGridSpec(
        num_scalar_prefetch=1, grid=(steps,),
        in_specs=[pl.BlockSpec((tr, cols), lambda i, s_ref: (i, 0))] + [ANY] * len(after),
        out_specs=pl.BlockSpec((tr, cols), out_map))
    return pl.pallas_call(
        body, name=name, grid_spec=grid_spec,
        out_shape=jax.ShapeDtypeStruct(out_shape, BF16),
        compiler_params=_params(("parallel",)),
    )(shard_arr, w, *after)


def _rms_fwd(x, w, name):
    rows, d = x.shape
    tr = 256

    def body(x_ref, w_ref, h_ref):
        xv = x_ref[...]
        r = lax.rsqrt(jnp.mean(xv * xv, axis=-1, keepdims=True) + NORM_EPS)
        h_ref[...] = (xv * r * w_ref[...]).astype(BF16)

    return pl.pallas_call(
        body, name=name, grid=(rows // tr,),
        in_specs=[pl.BlockSpec((tr, d), lambda i: (i, 0)), pl.BlockSpec((1, d), lambda i: (0, 0))],
        out_specs=pl.BlockSpec((tr, d), lambda i: (i, 0)),
        out_shape=jax.ShapeDtypeStruct((rows, d), BF16),
        compiler_params=_params(("parallel",)),
    )(x, w)


def _rms_bwd(x, w, dh, dres, name, after=()):
    rows, d = x.shape
    tr = 256
    after = tuple(after)

    def body(x_ref, w_ref, dh_ref, dres_ref, *rest):
        dx_ref, dxb_ref, dw_ref = rest[len(after):]
        xv = x_ref[...]
        r = lax.rsqrt(jnp.mean(xv * xv, axis=-1, keepdims=True) + NORM_EPS)
        xhat = xv * r
        dy = dh_ref[...]
        dxhat = dy * w_ref[...]
        dx = dres_ref[...] + r * (dxhat - xhat * jnp.mean(dxhat * xhat, axis=-1, keepdims=True))
        dx_ref[...] = dx
        dxb_ref[...] = dx.astype(BF16)
        part = jnp.sum(dy * xhat, axis=0, keepdims=True)

        @pl.when(pl.program_id(0) == 0)
        def _():
            dw_ref[...] = part

        @pl.when(pl.program_id(0) != 0)
        def _():
            dw_ref[...] += part

    row = pl.BlockSpec((tr, d), lambda i: (i, 0))
    vec = pl.BlockSpec((1, d), lambda i: (0, 0))
    return pl.pallas_call(
        body, name=name, grid=(rows // tr,),
        in_specs=[row, vec, row, row] + [ANY] * len(after),
        out_specs=[row, row, vec],
        out_shape=[jax.ShapeDtypeStruct((rows, d), F32), jax.ShapeDtypeStruct((rows, d), BF16),
                   jax.ShapeDtypeStruct((1, d), F32)],
        compiler_params=_params(("arbitrary",)),
    )(x, w, dh, dres, *after)


def _final_norm_loss(x2, w, target, name):
    rows, d = x2.shape
    tr = 256

    def body(x_ref, w_ref, t_ref, loss_ref, dx_ref, dxb_ref, dw_ref):
        xv = x_ref[...]
        wv = w_ref[...]
        r = lax.rsqrt(jnp.mean(xv * xv, axis=-1, keepdims=True) + NORM_EPS)
        xhat = xv * r
        err = xhat * wv - t_ref[...]
        part_loss = 0.5 * jnp.sum(jnp.mean(err * err, axis=-1, keepdims=True), axis=0, keepdims=True)
        dy = err * (1.0 / d)
        dxhat = dy * wv
        dx = r * (dxhat - xhat * jnp.mean(dxhat * xhat, axis=-1, keepdims=True))
        dx_ref[...] = dx
        dxb_ref[...] = dx.astype(BF16)
        part_dw = jnp.sum(dy * xhat, axis=0, keepdims=True)
        part_loss = jnp.broadcast_to(part_loss, (1, 128))

        @pl.when(pl.program_id(0) == 0)
        def _():
            dw_ref[...] = part_dw
            loss_ref[...] = part_loss

        @pl.when(pl.program_id(0) != 0)
        def _():
            dw_ref[...] += part_dw
            loss_ref[...] += part_loss

    row = pl.BlockSpec((tr, d), lambda i: (i, 0))
    vec = pl.BlockSpec((1, d), lambda i: (0, 0))
    return pl.pallas_call(
        body, name=name, grid=(rows // tr,),
        in_specs=[row, vec, row],
        out_specs=[pl.BlockSpec((1, 128), lambda i: (0, 0)), row, row, vec],
        out_shape=[jax.ShapeDtypeStruct((1, 128), F32), jax.ShapeDtypeStruct((rows, d), F32),
                   jax.ShapeDtypeStruct((rows, d), BF16), jax.ShapeDtypeStruct((1, d), F32)],
        compiler_params=_params(("arbitrary",)),
    )(x2, w, target)


def _adamw_math(w, g, m, v):
    m = ADAM_B1 * m + (1.0 - ADAM_B1) * g
    v = ADAM_B2 * v + (1.0 - ADAM_B2) * (g * g)
    m_hat = m / (1.0 - ADAM_B1 ** ADAM_STEP)
    v_hat = v / (1.0 - ADAM_B2 ** ADAM_STEP)
    delta = -ADAM_LR * (m_hat / (jnp.sqrt(v_hat) + ADAM_EPS) + ADAM_WD * w)
    return delta, m, v


def _adamw(w, g, m, v, name):
    rows, cols = w.shape
    tr = _row_tile(rows, cols * 4)

    def body(w_ref, g_ref, m_ref, v_ref, d_ref, mo_ref, vo_ref, go_ref):
        g = g_ref[...]
        delta, m_new, v_new = _adamw_math(w_ref[...], g, m_ref[...], v_ref[...])
        d_ref[...] = delta
        mo_ref[...] = m_new
        vo_ref[...] = v_new
        go_ref[...] = g

    blk = pl.BlockSpec((tr, cols), lambda i: (i, 0))
    shp = jax.ShapeDtypeStruct((rows, cols), F32)
    return pl.pallas_call(
        body, name=name, grid=(rows // tr,),
        in_specs=[blk] * 4, out_specs=[blk] * 4, out_shape=[shp] * 4,
        compiler_params=_params(("parallel",)),
    )(w, g, m, v)


_DOT_DIMS = {"nn": ((1,), (0,)), "nt": ((1,), (1,)), "tn": ((0,), (0,))}


def _matmul(name, mode, a_list, b_list, acc_of, m, n, k, tm, tn, tk, extras, out_dtypes, epilogue,
            a_koff=None, b_koff=None, after=(), a_specs=None, b_specs=None, a_single_buffer=False):
    after = tuple(after)
    assert m % tm == 0 and n % tn == 0 and k % tk == 0, (name, m, n, k, tm, tn, tk)
    nk = k // tk
    n_acc = max(acc_of) + 1
    n_pairs = len(a_list)
    a_koff = a_koff or [0] * n_pairs
    b_koff = b_koff or [0] * n_pairs
    dims = (_DOT_DIMS[mode], ((), ()))
    n_ext, n_out = len(extras), len(out_dtypes)

    def body(*refs):
        a_refs = refs[:n_pairs]
        b_refs = refs[n_pairs:2 * n_pairs]
        e_refs = refs[2 * n_pairs:2 * n_pairs + n_ext]
        first_out = 2 * n_pairs + n_ext + len(after)
        o_refs = refs[first_out:first_out + n_out]
        acc_refs = refs[first_out + n_out:]

        parts = [None] * n_acc
        for p in range(n_pairs):
            d = lax.dot_general(a_refs[p][...], b_refs[p][...], dims, preferred_element_type=F32)
            parts[acc_of[p]] = d if parts[acc_of[p]] is None else parts[acc_of[p]] + d

        def finish(accs):
            outs = epilogue(accs, [e[...] for e in e_refs])
            for o_ref, o in zip(o_refs, outs):
                o_ref[...] = o.astype(o_ref.dtype)

        if nk == 1:
            finish(parts)
        else:
            kk = pl.program_id(2)

            @pl.when(kk == 0)
            def _():
                for acc_ref, part in zip(acc_refs, parts):
                    acc_ref[...] = part

            @pl.when(kk != 0)
            def _():
                for acc_ref, part in zip(acc_refs, parts):
                    acc_ref[...] += part

            @pl.when(kk == nk - 1)
            def _():
                finish([acc_ref[...] for acc_ref in acc_refs])

    def a_spec(off):
        mode_a = pl.Buffered(1) if a_single_buffer else None
        if mode == "tn":
            return pl.BlockSpec((tk, tm), lambda i, j, kk: (kk + off, i), pipeline_mode=mode_a)
        return pl.BlockSpec((tm, tk), lambda i, j, kk: (i, kk + off), pipeline_mode=mode_a)

    def b_spec(off):
        if mode == "nt":
            return pl.BlockSpec((tn, tk), lambda i, j, kk: (j, kk + off))
        return pl.BlockSpec((tk, tn), lambda i, j, kk: (kk + off, j))

    tile = pl.BlockSpec((tm, tn), lambda i, j, kk: (i, j))
    scratch = [pltpu.VMEM((tm, tn), F32) for _ in range(n_acc)] if nk > 1 else []
    return pl.pallas_call(
        body, name=name, grid=(m // tm, n // tn, nk),
        in_specs=(a_specs or [a_spec(o) for o in a_koff]) + (b_specs or [b_spec(o) for o in b_koff])
        + [tile] * n_ext + [ANY] * len(after),
        out_specs=[tile] * n_out,
        out_shape=[jax.ShapeDtypeStruct((m, n), dt) for dt in out_dtypes],
        scratch_shapes=scratch,
        compiler_params=_params(("parallel", "parallel", "arbitrary")),
    )(*a_list, *b_list, *extras, *after)


def _epi_plain(accs, extras):
    return (accs[0],)


def _epi_residual(accs, extras):
    return (accs[0] + extras[0],)


def _epi_two(accs, extras):
    return accs[0], accs[1]


def _epi_swiglu(accs, extras):
    g, u = accs
    return g, u, g * _sigmoid(g) * u


def _epi_swiglu_bwd(accs, extras):
    da = accs[0]
    g, u = (e.astype(F32) for e in extras)
    sg = _sigmoid(g)
    dg = da * u * sg * (1.0 + g * (1.0 - sg))
    du = da * g * sg
    return dg, du


_NT_DIMS = (((1,), (1,)), ((), ()))
_TN_DIMS = (((0,), (0,)), ((), ()))


def _tile_delta(tq, tk):
    return lax.broadcasted_iota(I32, (tq, tk), 0) - lax.broadcasted_iota(I32, (tq, tk), 1)


def _attn_log_count(delta):
    count = jnp.zeros(delta.shape, I32)
    for window, dilation in DILATED_PATTERNS:
        hit = ((delta & (dilation - 1)) == 0) & (delta <= window)
        count = count + jnp.where(hit, 1, 0)
    valid = (delta >= 0) & (count > 0)
    logm = jnp.where(count == 3, math.log(3.0), jnp.where(count == 2, math.log(2.0), 0.0))
    return jnp.where(valid, logm, NEG_BIG)


def _fill_attn_log_count(tab_ref):
    nb, t, _ = tab_ref.shape
    base = _tile_delta(t, t)
    for b in range(nb):
        tab_ref[b] = _attn_log_count(base + b * t)


def _fill_attn_bias(tab_ref, log_count_ref, slope):
    nb, t, _ = tab_ref.shape
    dist = _tile_delta(t, t).astype(F32)
    for b in range(nb):
        tab_ref[b] = log_count_ref[b] - slope * (dist + float(b * t))


def _fill_ret_decay(tab_ref, log_gamma):
    nb, t, _ = tab_ref.shape
    base = _tile_delta(t, t)
    for b in range(nb):
        tab_ref[b] = _ret_decay(base + b * t, log_gamma)


def _alibi_slopes():
    return [2.0 ** (-8.0 * (h + 1) / ATTN_HEADS) for h in range(ATTN_HEADS)]


def _attn_fwd(proj, after=()):
    s = proj.shape[0]
    t = SEQ_TILE
    hd = ATTN_HEAD_DIM
    hp = ATTN_FWD_HEADS_PER_STEP
    ng = ATTN_HEADS // hp
    w = hp * hd
    scale = 1.0 / math.sqrt(hd)
    slopes = _alibi_slopes()

    def body(q_ref, k_ref, v_ref, *rest):
        mix_ref, o_ref, lse_ref, kb, vb, bias_tab, log_count_tab = rest[len(after):]
        g = pl.program_id(0)
        i = pl.program_id(1)

        @pl.when((g == 0) & (i == 0))
        def _():
            _fill_attn_log_count(log_count_tab)

        @pl.when(i == 0)
        def _():
            kb[...] = k_ref[...].astype(BF16)
            vb[...] = v_ref[...].astype(BF16)
            for u in range(hp):
                _fill_attn_bias(bias_tab.at[u], log_count_tab, _select_by_index(g * hp + u, slopes))

        qs = [q_ref[:, u * hd:(u + 1) * hd].astype(BF16) for u in range(hp)]

        def step(j, carry):
            rows = pl.ds(pl.multiple_of(j * t, t), t)
            out = []
            for u in range(hp):
                m_i, l_i, acc = carry[u]
                lanes = slice(u * hd, (u + 1) * hd)
                sc = lax.dot_general(qs[u], kb[rows, lanes], _NT_DIMS, preferred_element_type=F32) * scale
                sc = sc + bias_tab[u, i - j]
                m_new = jnp.maximum(m_i, jnp.max(sc, axis=-1, keepdims=True))
                p = jnp.exp(sc - m_new)
                alpha = jnp.exp(m_i - m_new)
                l_new = alpha * l_i + jnp.sum(p, axis=-1, keepdims=True)
                acc = alpha * acc + jnp.dot(p.astype(BF16), vb[rows, lanes], preferred_element_type=F32)
                out.append((m_new, l_new, acc))
            return tuple(out)

        init = (jnp.full((t, 1), NEG_BIG, F32), jnp.zeros((t, 1), F32), jnp.zeros((t, hd), F32))
        final = lax.fori_loop(0, i + 1, step, (init,) * hp)
        for u in range(hp):
            m_i, l_i, acc = final[u]
            lanes = slice(u * hd, (u + 1) * hd)
            out = acc / l_i
            o_ref[:, lanes] = out
            mix_ref[:, lanes] = out.astype(BF16)
            lse_ref[:, lanes] = jnp.broadcast_to(m_i + jnp.log(l_i), (t, hd))

    return pl.pallas_call(
        body, name="attn_fwd", grid=(ng, s // t),
        in_specs=[pl.BlockSpec((t, w), lambda g, i: (i, g)),
                  pl.BlockSpec((s, w), lambda g, i: (0, ng + g)),
                  pl.BlockSpec((s, w), lambda g, i: (0, 2 * ng + g))] + [ANY] * len(after),
        out_specs=[pl.BlockSpec((None, t, w), lambda g, i: (0, i, g))] + [pl.BlockSpec((t, w), lambda g, i: (i, g))] * 2,
        out_shape=[jax.ShapeDtypeStruct((2, s, ATTN_WIDTH), BF16),
                   jax.ShapeDtypeStruct((s, ATTN_WIDTH), F32),
                   jax.ShapeDtypeStruct((s, ATTN_WIDTH), F32)],
        scratch_shapes=[pltpu.VMEM((s, w), BF16), pltpu.VMEM((s, w), BF16), pltpu.VMEM((hp, s // t, t, t), F32),
                        pltpu.VMEM((s // t, t, t), F32)],
        compiler_params=_params(("arbitrary", "arbitrary")),
    )(proj, proj, proj, *after)


def _attn_bwd(proj, attn_out, lse, dmixed, after=()):
    after = tuple(after)
    s = proj.shape[0]
    t = SEQ_TILE
    nt = s // t
    hd = ATTN_HEAD_DIM
    hp = ATTN_HEADS_PER_STEP
    ng = ATTN_HEADS // hp
    w = hp * hd
    scale = 1.0 / math.sqrt(hd)
    slopes = _alibi_slopes()

    def body(q_ref, k_ref, v_ref, o_ref, lse_ref, do_ref, *rest):
        dsec_ref, qb, kb, vb, dob, dsum, dq_acc, bias_tab, log_count_tab = rest[len(after):]
        g = pl.program_id(0)

        @pl.when(g == 0)
        def _():
            _fill_attn_log_count(log_count_tab)

        qb[...] = q_ref[...].astype(BF16)
        kb[...] = k_ref[...].astype(BF16)
        vb[...] = v_ref[...].astype(BF16)
        dob[...] = do_ref[...].astype(BF16)
        for u in range(hp):
            lanes = slice(u * hd, (u + 1) * hd)
            _fill_attn_bias(bias_tab.at[u], log_count_tab, _select_by_index(g * hp + u, slopes))
            rowsum = jnp.sum(do_ref[:, lanes] * o_ref[:, lanes], axis=-1, keepdims=True)
            dsum[:, lanes] = jnp.broadcast_to(rowsum, (s, hd))
        dq_acc[...] = jnp.zeros((s, w), F32)

        def over_keys(j, _):
            krows = pl.ds(pl.multiple_of(j * t, t), t)

            def over_queries(i, carry):
                qrows = pl.ds(pl.multiple_of(i * t, t), t)
                out = []
                for u in range(hp):
                    dk, dv = carry[u]
                    lanes = slice(u * hd, (u + 1) * hd)
                    qi, doi = qb[qrows, lanes], dob[qrows, lanes]
                    kj, vj = kb[krows, lanes], vb[krows, lanes]
                    lse_i = lse_ref[qrows, lanes][:, :1]
                    dsum_i = dsum[qrows, lanes][:, :1]
                    sc = lax.dot_general(qi, kj, _NT_DIMS, preferred_element_type=F32) * scale
                    p = jnp.exp(sc + bias_tab[u, i - j] - lse_i)
                    dp = lax.dot_general(doi, vj, _NT_DIMS, preferred_element_type=F32)
                    ds = (p * (dp - dsum_i)).astype(BF16)
                    dv = dv + lax.dot_general(p.astype(BF16), doi, _TN_DIMS, preferred_element_type=F32)
                    dk = dk + lax.dot_general(ds, qi, _TN_DIMS, preferred_element_type=F32)
                    dq_acc[qrows, lanes] += jnp.dot(ds, kj, preferred_element_type=F32)
                    out.append((dk, dv))
                return tuple(out)

            zero = jnp.zeros((t, hd), F32)
            final = lax.fori_loop(j, nt, over_queries, ((zero, zero),) * hp)
            for u in range(hp):
                lanes = slice(u * hd, (u + 1) * hd)
                dsec_ref[1, krows, lanes] = (final[u][0] * scale).astype(BF16)
                dsec_ref[2, krows, lanes] = final[u][1].astype(BF16)
            return 0

        lax.fori_loop(0, nt, over_keys, 0)
        dsec_ref[0] = (dq_acc[...] * scale).astype(BF16)

    def col(off):
        return pl.BlockSpec((s, w), lambda g: (0, off + g))

    return pl.pallas_call(
        body, name="attn_bwd", grid=(ng,),
        in_specs=[col(0), col(ng), col(2 * ng), col(0), col(0), col(0)] + [ANY] * len(after),
        out_specs=pl.BlockSpec((4, s, w), lambda g: (0, 0, g)),
        out_shape=jax.ShapeDtypeStruct((8, s, ATTN_WIDTH), BF16),
        scratch_shapes=[pltpu.VMEM((s, w), BF16)] * 4 + [pltpu.VMEM((s, w), F32)] * 2
        + [pltpu.VMEM((hp, nt, t, t), F32), pltpu.VMEM((nt, t, t), F32)],
        compiler_params=_params(("arbitrary",)),
    )(proj, proj, proj, attn_out, lse, dmixed, *after)


def _ret_log_gammas():
    return [math.log(1.0 - 2.0 ** (-5.0 - h)) for h in range(RET_HEADS)]


def _ret_decay(delta, log_gamma):
    dec = jnp.exp(delta.astype(F32) * log_gamma) * (1.0 / math.sqrt(RET_HEAD_DIM))
    return jnp.where(delta >= 0, dec, 0.0)


def _ret_fwd(proj, mixed, after=()):
    after = tuple(after)
    s = proj.shape[0]
    t = SEQ_TILE
    hd = RET_HEAD_DIM
    nh = RET_HEADS
    log_gammas = _ret_log_gammas()
    c0 = 3 * ATTN_WIDTH // hd

    def body(q_ref, k_ref, v_ref, g_ref, *rest):
        mix_ref, raw_ref, kb, vb, decay_tab = rest[1 + len(after):]
        h = pl.program_id(0)
        i = pl.program_id(1)

        @pl.when(i == 0)
        def _():
            kb[...] = k_ref[...].astype(BF16)
            vb[...] = v_ref[...].astype(BF16)
            _fill_ret_decay(decay_tab, _select_by_index(h, log_gammas))

        q = q_ref[...].astype(BF16)

        def step(j, acc):
            rows = pl.ds(pl.multiple_of(j * t, t), t)
            sc = lax.dot_general(q, kb[rows, :], _NT_DIMS, preferred_element_type=F32) * decay_tab[i - j]
            return acc + jnp.dot(sc.astype(BF16), vb[rows, :], preferred_element_type=F32)

        ret = lax.fori_loop(0, i + 1, step, jnp.zeros((t, hd), F32))
        raw_ref[...] = ret
        r = lax.rsqrt(jnp.mean(ret * ret, axis=-1, keepdims=True) + NORM_EPS)
        g = g_ref[...]
        mix_ref[...] = (g * _sigmoid(g) * (ret * r)).astype(BF16)

    return pl.pallas_call(
        body, name="ret_fwd", grid=(nh, s // t),
        in_specs=[pl.BlockSpec((t, hd), lambda h, i: (i, c0 + h)),
                  pl.BlockSpec((s, hd), lambda h, i: (0, c0 + nh + h)),
                  pl.BlockSpec((s, hd), lambda h, i: (0, c0 + 2 * nh + h)),
                  pl.BlockSpec((t, hd), lambda h, i: (i, c0 + 3 * nh + h))] + [ANY] * (1 + len(after)),
        out_specs=[pl.BlockSpec((None, t, hd), lambda h, i: (1, i, h)), pl.BlockSpec((t, hd), lambda h, i: (i, h))],
        out_shape=[jax.ShapeDtypeStruct(mixed.shape, BF16), jax.ShapeDtypeStruct((s, RET_WIDTH), F32)],
        input_output_aliases={4: 0},
        scratch_shapes=[pltpu.VMEM((s, hd), BF16), pltpu.VMEM((s, hd), BF16), pltpu.VMEM((s // t, t, t), F32)],
        compiler_params=_params(("arbitrary", "arbitrary")),
    )(proj, proj, proj, proj, mixed, *after)


def _ret_bwd(proj, ret_raw, dmixed, dsec, after=()):
    after = tuple(after)
    s = proj.shape[0]
    t = SEQ_TILE
    nt = s // t
    hd = RET_HEAD_DIM
    nh = RET_HEADS
    log_gammas = _ret_log_gammas()
    c0 = 3 * ATTN_WIDTH // hd
    mixed_blocks = ATTN_WIDTH // hd

    def body(q_ref, k_ref, v_ref, g_ref, raw_ref, dmix_ref, *rest):
        dsec_ref, qb, kb, vb, dretb, dq_acc, decay_tab = rest[1 + len(after):]
        h = pl.program_id(0)
        _fill_ret_decay(decay_tab, _select_by_index(h, log_gammas))
        qb[...] = q_ref[...].astype(BF16)
        kb[...] = k_ref[...].astype(BF16)
        vb[...] = v_ref[...].astype(BF16)
        ret = raw_ref[...]
        r = lax.rsqrt(jnp.mean(ret * ret, axis=-1, keepdims=True) + NORM_EPS)
        normed = ret * r
        g = g_ref[...]
        sg = _sigmoid(g)
        dout = dmix_ref[...]
        dsec_ref[3] = (dout * normed * sg * (1.0 + g * (1.0 - sg))).astype(BF16)
        dn = dout * g * sg
        dret = r * (dn - normed * jnp.mean(dn * normed, axis=-1, keepdims=True))
        dretb[...] = dret.astype(BF16)
        dq_acc[...] = jnp.zeros((s, hd), F32)

        def over_keys(j, _):
            krows = pl.ds(pl.multiple_of(j * t, t), t)
            kj = kb[krows, :]
            vj = vb[krows, :]

            def over_queries(i, carry):
                dk, dv = carry
                qrows = pl.ds(pl.multiple_of(i * t, t), t)
                qi = qb[qrows, :]
                doi = dretb[qrows, :]
                dec = decay_tab[i - j]
                a = (lax.dot_general(qi, kj, _NT_DIMS, preferred_element_type=F32) * dec).astype(BF16)
                da = (lax.dot_general(doi, vj, _NT_DIMS, preferred_element_type=F32) * dec).astype(BF16)
                dv = dv + lax.dot_general(a, doi, _TN_DIMS, preferred_element_type=F32)
                dk = dk + lax.dot_general(da, qi, _TN_DIMS, preferred_element_type=F32)
                dq_acc[qrows, :] += jnp.dot(da, kj, preferred_element_type=F32)
                return dk, dv

            zero = jnp.zeros((t, hd), F32)
            dk, dv = lax.fori_loop(j, nt, over_queries, (zero, zero))
            dsec_ref[1, krows, :] = dk.astype(BF16)
            dsec_ref[2, krows, :] = dv.astype(BF16)
            return 0

        lax.fori_loop(0, nt, over_keys, 0)
        dsec_ref[0] = dq_acc[...].astype(BF16)

    def col(off):
        return pl.BlockSpec((s, hd), lambda h: (0, off + h))

    return pl.pallas_call(
        body, name="ret_bwd", grid=(nh,),
        in_specs=[col(c0), col(c0 + nh), col(c0 + 2 * nh), col(c0 + 3 * nh), col(0), col(mixed_blocks)]
        + [ANY] * (1 + len(after)),
        out_specs=pl.BlockSpec((4, s, hd), lambda h: (1, 0, h)),
        out_shape=jax.ShapeDtypeStruct(dsec.shape, BF16),
        input_output_aliases={6: 0},
        scratch_shapes=[pltpu.VMEM((s, hd), BF16)] * 4 + [pltpu.VMEM((s, hd), F32)]
        + [pltpu.VMEM((nt, t, t), F32)],
        compiler_params=_params(("arbitrary",)),
    )(proj, proj, proj, proj, ret_raw, dmixed, dsec, *after)


_FLIPS = (2, 1, 3)


def _other_chips(x, y):
    return [(1 - x, y), (x, 1 - y), (1 - x, 1 - y)]


_HBM = pl.BlockSpec(memory_space=pltpu.HBM)
_SEM = pl.BlockSpec(memory_space=pltpu.SEMAPHORE)
_EFFECT = pltpu.SideEffectType.DATAFLOW_SIDE_EFFECTING


def _in_hbm(a):
    return pltpu.with_memory_space_constraint(a, pltpu.HBM)


def _weight_view(w, column_sharded):
    if column_sharded:
        return w.reshape(2, w.shape[0] // 2, w.shape[1])
    return w.reshape(N_CHIPS, 2, w.shape[0] // (2 * N_CHIPS), w.shape[1])


def _weight_unview(v):
    if v.ndim == 3:
        return v.reshape(2 * v.shape[1], v.shape[2])
    return v.reshape(N_CHIPS * 2 * v.shape[2], v.shape[3])


def _weight_region(buf, shard, half):
    if len(buf.shape) == 3:
        cols = buf.shape[2] // N_CHIPS
        return buf.at[half, :, pl.ds(shard * cols, cols)]
    return buf.at[shard, half]


def _remote(where, send_sem, recv_sem, to):
    return pltpu.make_async_remote_copy(src_ref=where, dst_ref=where, send_sem=send_sem, recv_sem=recv_sem,
                                        device_id=to, device_id_type=MESH)


def _for_my_shard(fn):
    x, y, _ = _place()
    for ss in range(N_CHIPS):
        pl.when(2 * x + y == ss)(functools.partial(fn, ss))


def _gather_start(views, name, after=()):
    n_w = len(views)
    after = tuple(after)

    def body(*refs):
        send_sems, recv_sems = refs[n_w + len(after):n_w + len(after) + 2]
        bufs = refs[n_w + len(after) + 2:]
        x, y, c = _place()

        def start(ss):
            for w in range(n_w):
                for j, chip in enumerate(_other_chips(x, y)):
                    _remote(_weight_region(bufs[w], ss, c), send_sems.at[3 * w + j], recv_sems.at[3 * w + j],
                            (*chip, c)).start()

        _for_my_shard(start)

    return pl.pallas_call(
        body, name=name,
        in_specs=[_HBM] * n_w + [ANY] * len(after), out_specs=[_SEM, _SEM] + [_HBM] * n_w,
        out_shape=[pltpu.SemaphoreType.DMA((3 * n_w,)), pltpu.SemaphoreType.DMA((3 * n_w,))]
        + [pltpu.HBM(v.shape, BF16) for v in views],
        input_output_aliases={w: 2 + w for w in range(n_w)},
        compiler_params=pltpu.CompilerParams(has_side_effects=_EFFECT),
    )(*[_in_hbm(v) for v in views], *after)


def _gather_forward(views, which, send_sems, recv_sems, after, name, base=0):
    n_w = len(views)
    which = [base // 3 + w for w in which] if base % 3 == 0 else None
    assert which is not None, "base must be a multiple of 3"

    def body(*refs):
        send_in, recv_in = refs[n_w:n_w + 2]
        fwd_send, fwd_recv = refs[n_w + 3:n_w + 5]
        bufs = refs[n_w + 5:]
        x, y, c = _place()
        sibling = (x, y, 1 - c)

        def forward(ss):
            for i, w in enumerate(which):
                for j in range(3):
                    landed = _weight_region(bufs[i], ss ^ _FLIPS[j], c)
                    _remote(landed, send_in.at[3 * w + j], recv_in.at[3 * w + j], sibling).wait_recv()
                    _remote(landed, fwd_send.at[3 * i + j], fwd_recv.at[3 * i + j], sibling).start()

        _for_my_shard(forward)
        for i, w in enumerate(which):
            for j in range(3):
                _remote(_weight_region(bufs[i], 0, 0), send_in.at[3 * w + j], recv_in.at[3 * w + j],
                        sibling).wait_send()

    return pl.pallas_call(
        body, name=name,
        in_specs=[_HBM] * n_w + [_SEM, _SEM, ANY], out_specs=[_SEM, _SEM] + [_HBM] * n_w,
        out_shape=[pltpu.SemaphoreType.DMA((3 * n_w,)), pltpu.SemaphoreType.DMA((3 * n_w,))]
        + [pltpu.HBM(v.shape, BF16) for v in views],
        input_output_aliases={w: 2 + w for w in range(n_w)},
        compiler_params=pltpu.CompilerParams(has_side_effects=_EFFECT),
    )(*views, send_sems, recv_sems, after)


def _gather_end(views, fwd_send, fwd_recv, after, name):
    n_w = len(views)

    def body(*refs):
        fwd_send_ref, fwd_recv_ref = refs[n_w:n_w + 2]
        bufs = refs[n_w + 3:]
        x, y, c = _place()
        for i in range(n_w):
            for j in range(3):
                cp = _remote(_weight_region(bufs[i], 0, 0), fwd_send_ref.at[3 * i + j], fwd_recv_ref.at[3 * i + j],
                             (x, y, 1 - c))
                cp.wait_recv()
                cp.wait_send()

    outs = pl.pallas_call(
        body, name=name,
        in_specs=[_HBM] * n_w + [_SEM, _SEM, ANY], out_specs=[_HBM] * n_w,
        out_shape=[pltpu.HBM(v.shape, BF16) for v in views],
        input_output_aliases={w: w for w in range(n_w)},
        compiler_params=pltpu.CompilerParams(has_side_effects=_EFFECT),
    )(*views, fwd_send, fwd_recv, after)
    return [_weight_unview(o) for o in outs]


def _comm_call(name, bufs, sem_pairs, after, n_new, fn):
    n, n_sem, after = len(bufs), 2 * len(sem_pairs), tuple(after)
    n_out_sem = 2 if n_new else 0

    def body(*refs):
        sems = refs[n:n + n_sem]
        outs = refs[n + n_sem + len(after):]
        new = outs[:n_out_sem] if n_new else (None, None)
        fn(outs[n_out_sem:], [(sems[2 * i], sems[2 * i + 1]) for i in range(len(sem_pairs))], *new)

    res = pl.pallas_call(
        body, name=name,
        in_specs=[_HBM] * n + [_SEM] * n_sem + [ANY] * len(after),
        out_specs=[_SEM] * n_out_sem + [_HBM] * n,
        out_shape=[pltpu.SemaphoreType.DMA((n_new,))] * n_out_sem + [pltpu.HBM(b.shape, b.dtype) for b in bufs],
        input_output_aliases={i: n_out_sem + i for i in range(n)},
        compiler_params=pltpu.CompilerParams(has_side_effects=_EFFECT),
    )(*bufs, *[s for pair in sem_pairs for s in pair], *after)
    return list(res[:n_out_sem]), list(res[n_out_sem:])


def _quarter(piece, q):
    rows = piece.shape[0] // 2
    return piece.at[pl.ds(q * rows, rows)]


def _gather_in_start(view, name):
    def fn(bufs, _, send, recv):
        x, y, c = _place()

        def go(ss):
            for j, chip in enumerate(_other_chips(x, y)[:2]):
                _remote(_weight_region(bufs[0], ss, c), send.at[j], recv.at[j], (*chip, c)).start()

        _for_my_shard(go)

    sems, (view,) = _comm_call(name, [_in_hbm(view)], [], (), 2, fn)
    return sems, view


def _gather_out_gate_start(v_out, v_gate, after, name):
    def fn(bufs, _, send, recv):
        x, y, c = _place()
        chips = _other_chips(x, y)

        def go(ss):
            for j in range(3):
                _remote(_weight_region(bufs[0], ss, c), send.at[j], recv.at[j], (*chips[j], c)).start()
            for j in range(2):
                _remote(_weight_region(bufs[1], ss, c), send.at[3 + j], recv.at[3 + j], (*chips[j], c)).start()

        _for_my_shard(go)

    sems, views = _comm_call(name, [_in_hbm(v_out), _in_hbm(v_gate)], [], after, 5, fn)
    return sems, views


def _gather_relay(view, started, base, after, name, then=None, then_peers=0):
    n_new = 6 + then_peers if then_peers else 4

    def fn(bufs, pairs, send, recv):
        (send_in, recv_in), = pairs
        x, y, c = _place()
        chips = _other_chips(x, y)
        sibling = (x, y, 1 - c)

        def go(ss):
            landed = [_weight_region(bufs[0], ss ^ _FLIPS[j], c) for j in range(2)]
            for j in range(2):
                _remote(landed[j], send_in.at[base + j], recv_in.at[base + j], sibling).wait_recv()
            for j in range(2):
                _remote(_quarter(landed[j], j), send.at[j], recv.at[j], (*chips[1 - j], c)).start()
            for j in range(2):
                _remote(landed[j], send.at[2 + j], recv.at[2 + j], sibling).start()
            for j in range(then_peers):
                _remote(_weight_region(bufs[1], ss, c), send.at[6 + j], recv.at[6 + j], (*chips[j], c)).start()

        _for_my_shard(go)
        for j in range(2):
            _remote(_weight_region(bufs[0], 0, 0), send_in.at[base + j], recv_in.at[base + j], sibling).wait_send()

    views = [view] if then is None else [view, _in_hbm(then)]
    sems, views = _comm_call(name, views, [started], after, n_new, fn)
    return sems, views


def _gather_in_neighbours_end(view, relayed, after, name):
    def fn(bufs, pairs, *_):
        (send, recv), = pairs
        x, y, c = _place()
        for j in range(2):
            cp = _remote(_weight_region(bufs[0], 0, 0), send.at[2 + j], recv.at[2 + j], (x, y, 1 - c))
            cp.wait_recv()
            cp.wait_send()

    _, (view,) = _comm_call(name, [view], [relayed], after, 0, fn)
    return view


def _gather_in_diagonal(view, relayed, after, name):
    def fn(bufs, pairs, send, recv):
        (send_in, recv_in), = pairs
        x, y, c = _place()
        sibling = (x, y, 1 - c)
        any_quarter = _quarter(_weight_region(bufs[0], 0, 0), 0)
        for j in range(2):
            cp = _remote(any_quarter, send_in.at[j], recv_in.at[j], sibling)
            cp.wait_recv()
            cp.wait_send()

        def go(ss):
            _remote(_weight_region(bufs[0], ss ^ _FLIPS[2], c), send.at[0], recv.at[0], sibling).start()

        _for_my_shard(go)

    sems, (view,) = _comm_call(name, [view], [relayed], after, 1, fn)
    return sems, view


def _gather_in_diagonal_end(view, forwarded, after, name):
    def fn(bufs, pairs, *_):
        (send, recv), = pairs
        x, y, c = _place()
        cp = _remote(_weight_region(bufs[0], 0, 0), send.at[0], recv.at[0], (x, y, 1 - c))
        cp.wait_recv()
        cp.wait_send()

    _, (view,) = _comm_call(name, [view], [forwarded], after, 0, fn)
    return view


def _in_proj_shard(h1, wi, proj, shard_arr, name):
    s, d = h1.shape
    n = wi.shape[1]
    tn = 256
    blocks = n // (N_CHIPS * tn)
    given = [] if proj is None else [proj]

    def body(shard_ref, h_ref, w_ref, *rest):
        del shard_ref
        rest[-1][...] = jnp.dot(h_ref[...], w_ref[...], preferred_element_type=F32)

    grid_spec = pltpu.PrefetchScalarGridSpec(
        num_scalar_prefetch=1, grid=(blocks,),
        in_specs=[pl.BlockSpec((s, d), lambda j, shard_ref: (0, 0)),
                  pl.BlockSpec((d, tn), lambda j, shard_ref: (0, shard_ref[0] * blocks + j))] + [ANY] * len(given),
        out_specs=pl.BlockSpec((s, tn), lambda j, shard_ref: (0, shard_ref[0] * blocks + j)))
    return pl.pallas_call(
        body, name=name, grid_spec=grid_spec,
        out_shape=jax.ShapeDtypeStruct((s, n), F32),
        input_output_aliases={3: 0} if given else {},
        compiler_params=_params(("arbitrary",)),
    )(shard_arr, h1, wi, *given)


def _split_start(name, bufs, n_sems, copies):
    n = len(bufs)

    def body(*refs):
        send_sems, recv_sems = refs[n:n + 2]
        for cp in copies(refs[n + 2:], send_sems, recv_sems):
            cp.start()

    outs = pl.pallas_call(
        body, name=name,
        in_specs=[_HBM] * n, out_specs=[_SEM, _SEM] + [_HBM] * n,
        out_shape=[pltpu.SemaphoreType.DMA((n_sems,)), pltpu.SemaphoreType.DMA((n_sems,))]
        + [pltpu.HBM(b.shape, b.dtype) for b in bufs],
        input_output_aliases={i: 2 + i for i in range(n)},
        compiler_params=pltpu.CompilerParams(has_side_effects=_EFFECT),
    )(*[_in_hbm(b) for b in bufs])
    return outs[0], outs[1], list(outs[2:])


def _split_wait(name, bufs, send_sems, recv_sems, copies, after):
    n = len(bufs)

    def body(*refs):
        send_ref, recv_ref = refs[n:n + 2]
        for cp in copies(refs[n + 3:], send_ref, recv_ref):
            cp.wait()

    return list(pl.pallas_call(
        body, name=name,
        in_specs=[_HBM] * n + [_SEM, _SEM, ANY], out_specs=[_HBM] * n,
        out_shape=[pltpu.HBM(b.shape, b.dtype) for b in bufs],
        input_output_aliases={i: i for i in range(n)},
        compiler_params=pltpu.CompilerParams(has_side_effects=_EFFECT),
    )(*bufs, send_sems, recv_sems, after))


def _halves_copies(n_w):
    def copies(bufs, send_sems, recv_sems):
        x, y, c = _place()
        out = []
        for w in range(n_w):
            view, land = bufs[w], bufs[n_w + w]
            src = view.at[1 - c] if len(view.shape) == 3 else view.at[:, 1 - c]
            out.append(pltpu.make_async_remote_copy(
                src_ref=src, dst_ref=land, send_sem=send_sems.at[w], recv_sem=recv_sems.at[w],
                device_id=(x, y, 1 - c), device_id_type=MESH))
        return out
    return copies


def _pieces_copies(n_w):
    def copies(bufs, send_sems, recv_sems):
        x, y, c = _place()
        out = []
        for w in range(n_w):
            for j, (cx, cy) in enumerate(_other_chips(x, y)):
                out.append(pltpu.make_async_remote_copy(
                    src_ref=bufs[w].at[2 * cx + cy], dst_ref=bufs[n_w + w].at[j],
                    send_sem=send_sems.at[3 * w + j], recv_sem=recv_sems.at[3 * w + j],
                    device_id=(cx, cy, c), device_id_type=MESH))
        return out
    return copies


def _join_copies(n_w):
    def copies(bufs, send_sems, recv_sems):
        x, y, c = _place()
        return [pltpu.make_async_remote_copy(
            src_ref=bufs[w].at[c], dst_ref=bufs[w].at[c], send_sem=send_sems.at[w], recv_sem=recv_sems.at[w],
            device_id=(x, y, 1 - c), device_id_type=MESH) for w in range(n_w)]
    return copies


def _grad_view(g, column_sharded):
    return _weight_view(g, column_sharded)


def _halves_landing(view):
    shape = view.shape[1:] if view.ndim == 3 else (N_CHIPS,) + view.shape[2:]
    return lax.empty(shape, BF16)


def _halves_start(tag, grads, column_sharded):
    views = [_weight_view(g, cs) for g, cs in zip(grads, column_sharded)]
    n = len(views)
    return _split_start("halves_start_" + tag, views + [_halves_landing(v) for v in views], n, _halves_copies(n))


def _halves_wait(tag, state, after):
    send_sems, recv_sems, bufs = state
    n = len(bufs) // 2
    bufs = _split_wait("halves_wait_" + tag, bufs, send_sems, recv_sems, _halves_copies(n), after)
    return bufs[:n], bufs[n:]


def _pieces_start(tag, pieces):
    n = len(pieces)
    landing = [lax.empty((3,) + p.shape[1:], BF16) for p in pieces]
    return _split_start("pieces_start_" + tag, list(pieces) + landing, 3 * n, _pieces_copies(n))


def _pieces_wait(tag, state, after):
    send_sems, recv_sems, bufs = state
    n = len(bufs) // 2
    bufs = _split_wait("pieces_wait_" + tag, bufs, send_sems, recv_sems, _pieces_copies(n), after)
    return bufs[:n], bufs[n:]


def _join_start(tag, shards):
    n = len(shards)
    return _split_start("join_start_" + tag, list(shards), n, _join_copies(n))


def _join_wait(tag, state, after):
    send_sems, recv_sems, bufs = state
    bufs = _split_wait("join_wait_" + tag, bufs, send_sems, recv_sems, _join_copies(len(bufs)), after)
    return [b.reshape(2 * b.shape[1], b.shape[2]) for b in bufs]


def _chip_sum_col(g3, sib, c_arr, name):
    _, hk, n = g3.shape
    cols = n // N_CHIPS
    tr = _row_tile(hk, cols * 2, limit=4 * 1024 * 1024)

    def body(c_ref, g_ref, s_ref, o_ref):
        del c_ref
        o_ref[...] = (g_ref[...].astype(F32) + s_ref[...].astype(F32)).astype(BF16)

    grid_spec = pltpu.PrefetchScalarGridSpec(
        num_scalar_prefetch=1, grid=(N_CHIPS, hk // tr),
        in_specs=[pl.BlockSpec((None, tr, cols), lambda p, r, c_ref: (c_ref[0], r, p)),
                  pl.BlockSpec((tr, cols), lambda p, r, c_ref: (r, p))],
        out_specs=pl.BlockSpec((None, tr, cols), lambda p, r, c_ref: (p, r, 0)))
    return pl.pallas_call(
        body, name=name, grid_spec=grid_spec,
        out_shape=jax.ShapeDtypeStruct((N_CHIPS, hk, cols), BF16),
        compiler_params=_params(("parallel", "parallel")),
    )(c_arr, g3, sib)


def _chip_sum_row(g4, sib, c_arr, name):
    _, _, hr, n = g4.shape
    tr = _row_tile(hr, n * 2, limit=4 * 1024 * 1024)

    def body(c_ref, g_ref, s_ref, o_ref):
        del c_ref
        o_ref[...] = (g_ref[...].astype(F32) + s_ref[...].astype(F32)).astype(BF16)

    grid_spec = pltpu.PrefetchScalarGridSpec(
        num_scalar_prefetch=1, grid=(N_CHIPS, hr // tr),
        in_specs=[pl.BlockSpec((None, None, tr, n), lambda p, r, c_ref: (p, c_ref[0], r, 0)),
                  pl.BlockSpec((None, tr, n), lambda p, r, c_ref: (p, r, 0))],
        out_specs=pl.BlockSpec((None, tr, n), lambda p, r, c_ref: (p, r, 0)))
    return pl.pallas_call(
        body, name=name, grid_spec=grid_spec,
        out_shape=jax.ShapeDtypeStruct((N_CHIPS, hr, n), BF16),
        compiler_params=_params(("parallel", "parallel")),
    )(c_arr, g4, sib)


def _sum_pieces(pieces, received, place_arr, name):
    _, r, n = pieces.shape
    tr = _row_tile(r, n * 4, limit=4 * 1024 * 1024)

    def body(p_ref, own_ref, r0_ref, r1_ref, r2_ref, o_ref):
        del p_ref
        acc = own_ref[...].astype(F32) + r0_ref[...].astype(F32)
        acc = acc + r1_ref[...].astype(F32)
        o_ref[...] = acc + r2_ref[...].astype(F32)

    def recv_spec(j):
        return pl.BlockSpec((None, tr, n), lambda i, p_ref: (j, i, 0))

    grid_spec = pltpu.PrefetchScalarGridSpec(
        num_scalar_prefetch=1, grid=(r // tr,),
        in_specs=[pl.BlockSpec((None, tr, n), lambda i, p_ref: (p_ref[0], i, 0)),
                  recv_spec(0), recv_spec(1), recv_spec(2)],
        out_specs=pl.BlockSpec((None, tr, n), lambda i, p_ref: (p_ref[1], i, 0)))
    return pl.pallas_call(
        body, name=name, grid_spec=grid_spec,
        out_shape=jax.ShapeDtypeStruct((2, r, n), F32),
        compiler_params=_params(("parallel",)),
    )(place_arr, pieces, received, received, received)


def _norm_weights_step(parts, w, m, v, after=()):
    rows, d = parts.shape
    after = tuple(after)

    def body(p_ref, w_ref, m_ref, v_ref, *rest):
        g_ref, d_ref, mo_ref, vo_ref, gathered, send_sems, recv_sems = rest[len(after):]
        x, y, c = _place()
        me = 4 * x + 2 * y + c
        gathered[me] = p_ref[...]
        copies = []
        for k in range(1, N_DEV):
            peer = (x ^ ((k >> 2) & 1), y ^ ((k >> 1) & 1), c ^ (k & 1))
            copies.append(pltpu.make_async_remote_copy(
                src_ref=p_ref, dst_ref=gathered.at[me], send_sem=send_sems.at[k - 1],
                recv_sem=recv_sems.at[k - 1], device_id=peer, device_id_type=MESH))
        for cp in copies:
            cp.start()
        for cp in copies:
            cp.wait()
        g = gathered[0]
        for k in range(1, N_DEV):
            g = g + gathered[k]
        delta, m_new, v_new = _adamw_math(w_ref[...], g, m_ref[...], v_ref[...])
        g_ref[...] = g
        d_ref[...] = delta
        mo_ref[...] = m_new
        vo_ref[...] = v_new

    vmem = pl.BlockSpec(memory_space=pltpu.VMEM)
    shp = jax.ShapeDtypeStruct((rows, d), F32)
    return pl.pallas_call(
        body, name="norm_weights_step",
        in_specs=[vmem] * 4 + [ANY] * len(after), out_specs=[vmem] * 4, out_shape=[shp] * 4,
        scratch_shapes=[pltpu.VMEM((N_DEV, rows, d), F32), pltpu.SemaphoreType.DMA((N_DEV - 1,)),
                        pltpu.SemaphoreType.DMA((N_DEV - 1,))],
        compiler_params=pltpu.CompilerParams(has_side_effects=True),
    )(parts, w, m, v, *after)


def kernel(x, norm_mix_w, w_in, w_out, norm_ffn_w, w_gate, w_up, w_down, norm_final_w, loss_target, m_norm_mix_w, m_w_in, m_w_out, m_norm_ffn_w, m_w_gate, m_w_up, m_w_down, m_norm_final_w, v_norm_mix_w, v_w_in, v_w_out, v_norm_ffn_w, v_w_gate, v_w_up, v_w_down, v_norm_final_w):
    s, d = x.shape[1], x.shape[2]
    xs = x.reshape(s, d)
    target = loss_target.reshape(s, d)
    big = {"w_in": (w_in, m_w_in, v_w_in), "w_out": (w_out, m_w_out, v_w_out),
           "w_gate": (w_gate, m_w_gate, v_w_gate), "w_up": (w_up, m_w_up, v_w_up),
           "w_down": (w_down, m_w_down, v_w_down)}
    big = {k: tuple(a.reshape(a.shape[1:]) for a in t) for k, t in big.items()}
    col_names, row_names = ("w_in", "w_gate", "w_up"), ("w_out", "w_down")
    n_in = N_CHIPS * big["w_in"][0].shape[1]
    ffn = N_CHIPS * big["w_gate"][0].shape[1]
    mix = ATTN_WIDTH + RET_WIDTH
    c_arr = lax.axis_index("c").astype(I32).reshape(1)
    shard_arr = (2 * lax.axis_index("x") + lax.axis_index("y")).astype(I32).reshape(1)
    place_arr = jnp.concatenate([shard_arr, c_arr])

    def cast(k, after=()):
        return _weight_view(_cast_into_full(big[k][0], shard_arr, k in col_names, "cast_" + k, after), k in col_names)

    started_in, v_in = _gather_in_start(cast("w_in"), "gather_in_start")

    sec = ATTN_WIDTH

    def section(p, rows):
        return pl.BlockSpec((None, rows, sec), lambda i, j, kk: (p, i, 0))

    h1 = _rms_fwd(xs, norm_mix_w, "rms_mix_fwd")
    my_shard = shard_arr[0]
    shard_of = [jnp.bitwise_xor(my_shard, f).astype(I32).reshape(1) for f in (0,) + _FLIPS]
    proj = _in_proj_shard(h1, _weight_unview(v_in), None, shard_of[0], "in_proj_own")
    early_views = [cast(k, after=[proj]) for k in ("w_out", "w_gate")]
    v_up, v_down = [cast(k, after=[proj]) for k in ("w_up", "w_down")]
    relayed_in, (v_in,) = _gather_relay(v_in, started_in, 0, early_views + [v_up, v_down], "gather_in_relay")
    started_og, (v_out, v_gate) = _gather_out_gate_start(*early_views, [v_in], "gather_out_gate_start")
    v_in = _gather_in_neighbours_end(v_in, relayed_in, [v_out], "gather_in_neighbours_end")
    proj = _in_proj_shard(h1, _weight_unview(v_in), proj, shard_of[1], "in_proj_x")
    proj = _in_proj_shard(h1, _weight_unview(v_in), proj, shard_of[2], "in_proj_y")
    forwarded_in, v_in = _gather_in_diagonal(v_in, relayed_in, [proj], "gather_in_diagonal")
    wi = _weight_unview(_gather_in_diagonal_end(v_in, forwarded_in, [proj], "gather_in_diagonal_end"))
    proj = _in_proj_shard(h1, wi, proj, shard_of[3], "in_proj_diagonal")
    fs_o, fr_o, v_out = _gather_forward([v_out], [0], *started_og, proj, "gather_forward_out")
    mixed, attn_o, lse = _attn_fwd(proj, after=[v_out])
    relayed_g, (v_gate, v_up) = _gather_relay(v_gate, started_og, 3, [attn_o], "gather_gate_relay",
                                              then=v_up, then_peers=2)
    mixed, ret_raw = _ret_fwd(proj, mixed, after=[v_gate])
    wo, = _gather_end([v_out], fs_o, fr_o, ret_raw, "gather_end_out")
    x1, = _matmul("out_proj", "nn", [mixed, mixed], [wo, wo], [0, 0], s, d, sec, s, 512, sec, [xs], [F32],
                  _epi_residual, b_koff=[0, 1], a_specs=[section(0, s), section(1, s)])
    h2 = _rms_fwd(x1, norm_ffn_w, "rms_ffn_fwd")
    relayed_u, (v_up, v_down) = _gather_relay(v_up, relayed_g, 6, [h2], "gather_up_relay",
                                              then=v_down, then_peers=3)
    v_gate = _gather_in_neighbours_end(v_gate, relayed_g, [v_up], "gather_gate_neighbours_end")
    forwarded_g, v_gate = _gather_in_diagonal(v_gate, relayed_g, [v_up], "gather_gate_diagonal")
    v_up = _gather_in_neighbours_end(v_up, relayed_u, [v_gate], "gather_up_neighbours_end")
    wg = _weight_unview(_gather_in_diagonal_end(v_gate, forwarded_g, [v_up], "gather_gate_diagonal_end"))
    forwarded_u, v_up = _gather_in_diagonal(v_up, relayed_u, [wg], "gather_up_diagonal")
    wu = _weight_unview(_gather_in_diagonal_end(v_up, forwarded_u, [wg], "gather_up_diagonal_end"))
    gate, up, act = _matmul("gate_up", "nn", [h2, h2], [wg, wu], [0, 1], s, ffn, d, s, 512, d, [],
                            [BF16, BF16, BF16], _epi_swiglu, a_single_buffer=True)
    fs, fr, v_down = _gather_forward([v_down], [0], *relayed_u, act, "gather_forward_down", base=6)
    wd, = _gather_end([v_down], fs, fr, act, "gather_end_down")
    x2, = _matmul("down_proj", "nn", [act], [wd], [0], s, d, ffn, s // 2, 512, ffn, [x1], [F32],
                  _epi_residual)
    loss_row, dx2, dx2b, dwf = _final_norm_loss(x2, norm_final_w.reshape(1, d), target, "final_norm_loss")

    names = col_names + row_names
    grads, new = {}, {}

    def chip_sums(tag_names, views, sibs):
        return [(_chip_sum_col if k in col_names else _chip_sum_row)(v, sb, c_arr, "chip_sum_" + k)
                for k, v, sb in zip(tag_names, views, sibs)]

    def piece_sums(tag_names, pieces, received):
        return [_sum_pieces(p, r, place_arr, "sum_pieces_" + k) for k, p, r in zip(tag_names, pieces, received)]

    def update(k):
        new[k] = _adamw(big[k][0], grads[k], big[k][1], big[k][2], "adamw_" + k)

    dgate, dup = _matmul("d_act", "nt", [dx2b], [wd], [0], s, ffn, d, s, 512, d, [gate, up],
                         [BF16, BF16], _epi_swiglu_bwd, a_single_buffer=True)
    g_wd, = _matmul("g_w_down", "tn", [act], [dx2b], [0], ffn, d, s, 512, d, s, [], [BF16], _epi_plain)
    halves_d = _halves_start("down", [g_wd], [False])
    dh2, = _matmul("d_h2", "nt", [dgate, dup], [wg, wu], [0, 0], s, d, ffn, s // 2, 256, ffn, [], [F32],
                   _epi_plain, after=halves_d[2][-1:], a_single_buffer=True)
    pieces_d = _pieces_start("down", chip_sums(["w_down"], *_halves_wait("down", halves_d, dh2)))
    g_wg, g_wu = _matmul("g_w_gate_up", "tn", [h2, h2], [dgate, dup], [0, 1], d, ffn, s, 1024, 512, s, [],
                         [BF16, BF16], _epi_two, after=pieces_d[2][-1:])
    halves_gu = _halves_start("gate_up", [g_wg, g_wu], [True, True])
    dx1, dx1b, dw_ffn = _rms_bwd(x1, norm_ffn_w, dh2, dx2, "rms_ffn_bwd", after=halves_gu[2][-1:])

    dmixed, = _matmul("d_mixed", "nt", [dx1b], [wo], [0], s, mix, d, s, 512, d, [], [F32], _epi_plain)
    pieces_gu = _pieces_start("gate_up", chip_sums(["w_gate", "w_up"], *_halves_wait("gate_up", halves_gu, dmixed)))
    per = sec // 512
    g_wo, = _matmul("g_w_out", "tn", [mixed], [dx1b], [0], mix, d, s, 512, d, s, [], [BF16], _epi_plain,
                    after=pieces_gu[2][-1:],
                    a_specs=[pl.BlockSpec((None, s, 512), lambda i, j, kk: (i // per, 0, i % per))])
    halves_o = _halves_start("out", [g_wo], [False])
    dsec = _attn_bwd(proj, attn_o, lse, dmixed, after=halves_o[2][-1:])
    pieces_o = _pieces_start("out", chip_sums(["w_out"], *_halves_wait("out", halves_o, dsec)))
    dsec = _ret_bwd(proj, ret_raw, dmixed, dsec, after=pieces_o[2][-1:])
    where = [0, 1, 2, 4, 5, 6, 7]
    n_sec = len(where)
    g_wi, = _matmul("g_w_in", "tn", [h1], [dsec], [0], d, n_in, s, 1024, sec, s, [], [BF16], _epi_plain,
                    b_specs=[pl.BlockSpec((None, s, sec), lambda i, j, kk: (j + (j >= 3).astype(I32), 0, 0))])
    halves_i = _halves_start("in", [g_wi], [True])
    dh1, = _matmul("d_h1", "nt", [dsec] * n_sec, [wi] * n_sec, [0] * n_sec, s, d, sec, s // 2, 256, sec, [], [F32],
                   _epi_plain, b_koff=list(range(n_sec)), after=halves_i[2][-1:],
                   a_specs=[section(p, s // 2) for p in where])
    pieces_i = _pieces_start("in", chip_sums(["w_in"], *_halves_wait("in", halves_i, dh1)))
    grad_x, _, dw_mix = _rms_bwd(xs, norm_mix_w, dh1, dx1, "rms_mix_bwd", after=pieces_i[2][-1:])

    def rows8(*vs):
        return jnp.concatenate([v.reshape(1, d) for v in vs] + [jnp.zeros((8 - len(vs), d), F32)], axis=0)

    join_d = _join_start("down", piece_sums(["w_down"], *_pieces_wait("down", pieces_d, grad_x)))
    join_gu = _join_start("gate_up", piece_sums(["w_gate", "w_up"], *_pieces_wait("gate_up", pieces_gu, join_d[2][0])))
    join_o = _join_start("out", piece_sums(["w_out"], *_pieces_wait("out", pieces_o, join_gu[2][0])))
    grads["w_down"], = _join_wait("down", join_d, join_o[2][0])
    update("w_down")
    grads["w_gate"], grads["w_up"] = _join_wait("gate_up", join_gu, new["w_down"][0])
    update("w_gate")
    update("w_up")
    grads["w_out"], = _join_wait("out", join_o, new["w_up"][0])
    update("w_out")
    join_i = _join_start("in", piece_sums(["w_in"], *_pieces_wait("in", pieces_i, new["w_out"][0])))
    ng, nd, nm, nv = _norm_weights_step(
        rows8(dw_mix, dw_ffn, dwf, jnp.broadcast_to(loss_row[:, :1], (1, d))),
        rows8(norm_mix_w, norm_ffn_w, norm_final_w),
        rows8(m_norm_mix_w, m_norm_ffn_w, m_norm_final_w), rows8(v_norm_mix_w, v_norm_ffn_w, v_norm_final_w),
        after=join_i[2][:1])
    grads["w_in"], = _join_wait("in", join_i, ng)
    update("w_in")

    loss = ng[3, 0]

    def pack(small, per_weight):
        lead = lambda a: a.reshape((1,) + a.shape)
        return (small[0:1], lead(per_weight["w_in"]), lead(per_weight["w_out"]), small[1:2],
                lead(per_weight["w_gate"]), lead(per_weight["w_up"]), lead(per_weight["w_down"]), small[2])

    return (loss, grad_x.reshape(1, s, d),
            *pack(ng, {k: new[k][3] for k in names}),
            *pack(nd, {k: new[k][0] for k in names}),
            *pack(nm, {k: new[k][1] for k in names}),
            *pack(nv, {k: new[k][2] for k in names}))
```

```python
import functools
import math

import jax
import jax.numpy as jnp
from jax import lax
from jax.experimental import pallas as pl
from jax.experimental.pallas import tpu as pltpu

F32 = jnp.float32
BF16 = jnp.bfloat16
I32 = jnp.int32
MESH = pl.DeviceIdType.MESH
ANY = pl.BlockSpec(memory_space=pl.ANY)

ATTN_HEADS = 8
ATTN_HEAD_DIM = 128
RET_HEADS = 4
RET_HEAD_DIM = 256
ATTN_WIDTH = ATTN_HEADS * ATTN_HEAD_DIM
RET_WIDTH = RET_HEADS * RET_HEAD_DIM
DILATED_PATTERNS = ((128, 1), (512, 4), (2048, 16))
NORM_EPS = 1e-6
ADAM_LR = 0.001
ADAM_B1 = 0.9
ADAM_B2 = 0.999
ADAM_EPS = 1e-08
ADAM_WD = 0.01
ADAM_STEP = 10

N_CHIPS = 4
N_DEV = 8
NEG_BIG = -1e30
SEQ_TILE = 512
ATTN_FWD_HEADS_PER_STEP = 2
ATTN_HEADS_PER_STEP = 1
VMEM_LIMIT_BYTES = 56 * 1024 * 1024


def _params(semantics=None, vmem=VMEM_LIMIT_BYTES):
    return pltpu.CompilerParams(dimension_semantics=semantics, vmem_limit_bytes=vmem)


def _row_tile(rows, row_bytes, limit=2 * 1024 * 1024, mult=16):
    best = None
    for t in range(mult, rows + 1, mult):
        if rows % t == 0 and t * row_bytes <= limit:
            best = t
    assert best is not None, (rows, row_bytes)
    return best


def _sigmoid(x):
    return 1.0 / (1.0 + jnp.exp(-x))


def _select_by_index(idx, values):
    out = jnp.float32(values[-1])
    for i in range(len(values) - 2, -1, -1):
        out = jnp.where(idx == i, jnp.float32(values[i]), out)
    return out


def _place():
    x, y, c = lax.axis_index("x"), lax.axis_index("y"), lax.axis_index("c")
    return x, y, c


def _cast_into_full(w, shard_arr, column_sharded, name, after=()):
    after = tuple(after)
    rows, cols = w.shape
    tr = _row_tile(rows, cols * 4)
    steps = rows // tr
    if column_sharded:
        out_shape, out_map = (rows, N_CHIPS * cols), (lambda i, s_ref: (i, s_ref[0]))
    else:
        out_shape, out_map = (N_CHIPS * rows, cols), (lambda i, s_ref: (s_ref[0] * steps + i, 0))

    def body(s_ref, w_ref, *rest):
        del s_ref
        rest[-1][...] = w_ref[...].astype(BF16)

    grid_spec = pltpu.PrefetchScalarGridSpec(
        num_scalar_prefetch=1, grid=(steps,),
        in_specs=[pl.BlockSpec((tr, cols), lambda i, s_ref: (i, 0))] + [ANY] * len(after),
        out_specs=pl.BlockSpec((tr, cols), out_map))
    return pl.pallas_call(
        body, name=name, grid_spec=grid_spec,
        out_shape=jax.ShapeDtypeStruct(out_shape, BF16),
        compiler_params=_params(("parallel",)),
    )(shard_arr, w, *after)


def _rms_fwd(x, w, name, after=()):
    rows, d = x.shape
    tr = 256
    after = tuple(after)

    def body(x_ref, w_ref, *rest):
        xv = x_ref[...]
        r = lax.rsqrt(jnp.mean(xv * xv, axis=-1, keepdims=True) + NORM_EPS)
        rest[-1][...] = (xv * r * w_ref[...]).astype(BF16)

    return pl.pallas_call(
        body, name=name, grid=(rows // tr,),
        in_specs=[pl.BlockSpec((tr, d), lambda i: (i, 0)), pl.BlockSpec((1, d), lambda i: (0, 0))]
        + [ANY] * len(after),
        out_specs=pl.BlockSpec((tr, d), lambda i: (i, 0)),
        out_shape=jax.ShapeDtypeStruct((rows, d), BF16),
        compiler_params=_params(("parallel",)),
    )(x, w, *after)


def _rms_bwd(x, w, dh, dres, name, after=()):
    rows, d = x.shape
    tr = 256
    after = tuple(after)

    def body(x_ref, w_ref, dh_ref, dres_ref, *rest):
        dx_ref, dxb_ref, dw_ref = rest[len(after):]
        xv = x_ref[...]
        r = lax.rsqrt(jnp.mean(xv * xv, axis=-1, keepdims=True) + NORM_EPS)
        xhat = xv * r
        dy = dh_ref[...]
        dxhat = dy * w_ref[...]
        dx = dres_ref[...] + r * (dxhat - xhat * jnp.mean(dxhat * xhat, axis=-1, keepdims=True))
        dx_ref[...] = dx
        dxb_ref[...] = dx.astype(BF16)
        part = jnp.sum(dy * xhat, axis=0, keepdims=True)

        @pl.when(pl.program_id(0) == 0)
        def _():
            dw_ref[...] = part

        @pl.when(pl.program_id(0) != 0)
        def _():
            dw_ref[...] += part

    row = pl.BlockSpec((tr, d), lambda i: (i, 0))
    vec = pl.BlockSpec((1, d), lambda i: (0, 0))
    return pl.pallas_call(
        body, name=name, grid=(rows // tr,),
        in_specs=[row, vec, row, row] + [ANY] * len(after),
        out_specs=[row, row, vec],
        out_shape=[jax.ShapeDtypeStruct((rows, d), F32), jax.ShapeDtypeStruct((rows, d), BF16),
                   jax.ShapeDtypeStruct((1, d), F32)],
        compiler_params=_params(("arbitrary",)),
    )(x, w, dh, dres, *after)


def _final_norm_loss(x2, w, target, name):
    rows, d = x2.shape
    tr = 256

    def body(x_ref, w_ref, t_ref, loss_ref, dx_ref, dxb_ref, dw_ref):
        xv = x_ref[...]
        wv = w_ref[...]
        r = lax.rsqrt(jnp.mean(xv * xv, axis=-1, keepdims=True) + NORM_EPS)
        xhat = xv * r
        err = xhat * wv - t_ref[...]
        part_loss = 0.5 * jnp.sum(jnp.mean(err * err, axis=-1, keepdims=True), axis=0, keepdims=True)
        dy = err * (1.0 / d)
        dxhat = dy * wv
        dx = r * (dxhat - xhat * jnp.mean(dxhat * xhat, axis=-1, keepdims=True))
        dx_ref[...] = dx
        dxb_ref[...] = dx.astype(BF16)
        part_dw = jnp.sum(dy * xhat, axis=0, keepdims=True)
        part_loss = jnp.broadcast_to(part_loss, (1, 128))

        @pl.when(pl.program_id(0) == 0)
        def _():
            dw_ref[...] = part_dw
            loss_ref[...] = part_loss

        @pl.when(pl.program_id(0) != 0)
        def _():
            dw_ref[...] += part_dw
            loss_ref[...] += part_loss

    row = pl.BlockSpec((tr, d), lambda i: (i, 0))
    vec = pl.BlockSpec((1, d), lambda i: (0, 0))
    return pl.pallas_call(
        body, name=name, grid=(rows // tr,),
        in_specs=[row, vec, row],
        out_specs=[pl.BlockSpec((1, 128), lambda i: (0, 0)), row, row, vec],
        out_shape=[jax.ShapeDtypeStruct((1, 128), F32), jax.ShapeDtypeStruct((rows, d), F32),
                   jax.ShapeDtypeStruct((rows, d), BF16), jax.ShapeDtypeStruct((1, d), F32)],
        compiler_params=_params(("arbitrary",)),
    )(x2, w, target)


def _adamw_math(w, g, m, v):
    m = ADAM_B1 * m + (1.0 - ADAM_B1) * g
    v = ADAM_B2 * v + (1.0 - ADAM_B2) * (g * g)
    m_hat = m / (1.0 - ADAM_B1 ** ADAM_STEP)
    v_hat = v / (1.0 - ADAM_B2 ** ADAM_STEP)
    delta = -ADAM_LR * (m_hat / (jnp.sqrt(v_hat) + ADAM_EPS) + ADAM_WD * w)
    return delta, m, v


def _adamw(w, g, m, v, name):
    rows, cols = w.shape
    tr = _row_tile(rows, cols * 4)

    def body(w_ref, g_ref, m_ref, v_ref, d_ref, mo_ref, vo_ref, go_ref):
        g = g_ref[...]
        delta, m_new, v_new = _adamw_math(w_ref[...], g, m_ref[...], v_ref[...])
        d_ref[...] = delta
        mo_ref[...] = m_new
        vo_ref[...] = v_new
        go_ref[...] = g

    blk = pl.BlockSpec((tr, cols), lambda i: (i, 0))
    shp = jax.ShapeDtypeStruct((rows, cols), F32)
    return pl.pallas_call(
        body, name=name, grid=(rows // tr,),
        in_specs=[blk] * 4, out_specs=[blk] * 4, out_shape=[shp] * 4,
        compiler_params=_params(("parallel",)),
    )(w, g, m, v)


_DOT_DIMS = {"nn": ((1,), (0,)), "nt": ((1,), (1,)), "tn": ((0,), (0,))}


def _matmul(name, mode, a_list, b_list, acc_of, m, n, k, tm, tn, tk, extras, out_dtypes, epilogue,
            a_koff=None, b_koff=None, after=(), a_specs=None, b_specs=None, a_single_buffer=False):
    after = tuple(after)
    assert m % tm == 0 and n % tn == 0 and k % tk == 0, (name, m, n, k, tm, tn, tk)
    nk = k // tk
    n_acc = max(acc_of) + 1
    n_pairs = len(a_list)
    a_koff = a_koff or [0] * n_pairs
    b_koff = b_koff or [0] * n_pairs
    dims = (_DOT_DIMS[mode], ((), ()))
    n_ext, n_out = len(extras), len(out_dtypes)

    def body(*refs):
        a_refs = refs[:n_pairs]
        b_refs = refs[n_pairs:2 * n_pairs]
        e_refs = refs[2 * n_pairs:2 * n_pairs + n_ext]
        first_out = 2 * n_pairs + n_ext + len(after)
        o_refs = refs[first_out:first_out + n_out]
        acc_refs = refs[first_out + n_out:]

        parts = [None] * n_acc
        for p in range(n_pairs):
            d = lax.dot_general(a_refs[p][...], b_refs[p][...], dims, preferred_element_type=F32)
            parts[acc_of[p]] = d if parts[acc_of[p]] is None else parts[acc_of[p]] + d

        def finish(accs):
            outs = epilogue(accs, [e[...] for e in e_refs])
            for o_ref, o in zip(o_refs, outs):
                o_ref[...] = o.astype(o_ref.dtype)

        if nk == 1:
            finish(parts)
        else:
            kk = pl.program_id(2)

            @pl.when(kk == 0)
            def _():
                for acc_ref, part in zip(acc_refs, parts):
                    acc_ref[...] = part

            @pl.when(kk != 0)
            def _():
                for acc_ref, part in zip(acc_refs, parts):
                    acc_ref[...] += part

            @pl.when(kk == nk - 1)
            def _():
                finish([acc_ref[...] for acc_ref in acc_refs])

    def a_spec(off):
        mode_a = pl.Buffered(1) if a_single_buffer else None
        if mode == "tn":
            return pl.BlockSpec((tk, tm), lambda i, j, kk: (kk + off, i), pipeline_mode=mode_a)
        return pl.BlockSpec((tm, tk), lambda i, j, kk: (i, kk + off), pipeline_mode=mode_a)

    def b_spec(off):
        if mode == "nt":
            return pl.BlockSpec((tn, tk), lambda i, j, kk: (j, kk + off))
        return pl.BlockSpec((tk, tn), lambda i, j, kk: (kk + off, j))

    tile = pl.BlockSpec((tm, tn), lambda i, j, kk: (i, j))
    scratch = [pltpu.VMEM((tm, tn), F32) for _ in range(n_acc)] if nk > 1 else []
    return pl.pallas_call(
        body, name=name, grid=(m // tm, n // tn, nk),
        in_specs=(a_specs or [a_spec(o) for o in a_koff]) + (b_specs or [b_spec(o) for o in b_koff])
        + [tile] * n_ext + [ANY] * len(after),
        out_specs=[tile] * n_out,
        out_shape=[jax.ShapeDtypeStruct((m, n), dt) for dt in out_dtypes],
        scratch_shapes=scratch,
        compiler_params=_params(("parallel", "parallel", "arbitrary")),
    )(*a_list, *b_list, *extras, *after)


def _epi_plain(accs, extras):
    return (accs[0],)


def _epi_residual(accs, extras):
    return (accs[0] + extras[0],)


def _epi_two(accs, extras):
    return accs[0], accs[1]


def _epi_swiglu(accs, extras):
    g, u = accs
    return g, u, g * _sigmoid(g) * u


def _epi_swiglu_bwd(accs, extras):
    da = accs[0]
    g, u = (e.astype(F32) for e in extras)
    sg = _sigmoid(g)
    dg = da * u * sg * (1.0 + g * (1.0 - sg))
    du = da * g * sg
    return dg, du


_NT_DIMS = (((1,), (1,)), ((), ()))
_TN_DIMS = (((0,), (0,)), ((), ()))


def _tile_delta(tq, tk):
    return lax.broadcasted_iota(I32, (tq, tk), 0) - lax.broadcasted_iota(I32, (tq, tk), 1)


def _attn_log_count(delta):
    count = jnp.zeros(delta.shape, I32)
    for window, dilation in DILATED_PATTERNS:
        hit = ((delta & (dilation - 1)) == 0) & (delta <= window)
        count = count + jnp.where(hit, 1, 0)
    valid = (delta >= 0) & (count > 0)
    logm = jnp.where(count == 3, math.log(3.0), jnp.where(count == 2, math.log(2.0), 0.0))
    return jnp.where(valid, logm, NEG_BIG)


def _fill_attn_log_count(tab_ref):
    nb, t, _ = tab_ref.shape
    base = _tile_delta(t, t)
    for b in range(nb):
        tab_ref[b] = _attn_log_count(base + b * t)


def _fill_attn_bias(tab_ref, log_count_ref, slope):
    nb, t, _ = tab_ref.shape
    dist = _tile_delta(t, t).astype(F32)
    for b in range(nb):
        tab_ref[b] = log_count_ref[b] - slope * (dist + float(b * t))


def _fill_ret_decay(tab_ref, log_gamma):
    nb, t, _ = tab_ref.shape
    base = _tile_delta(t, t)
    for b in range(nb):
        tab_ref[b] = _ret_decay(base + b * t, log_gamma)


def _alibi_slopes():
    return [2.0 ** (-8.0 * (h + 1) / ATTN_HEADS) for h in range(ATTN_HEADS)]


def _attn_fwd(proj, after=()):
    s = proj.shape[0]
    t = SEQ_TILE
    hd = ATTN_HEAD_DIM
    hp = ATTN_FWD_HEADS_PER_STEP
    ng = ATTN_HEADS // hp
    w = hp * hd
    scale = 1.0 / math.sqrt(hd)
    slopes = _alibi_slopes()

    def body(q_ref, k_ref, v_ref, *rest):
        mix_ref, o_ref, lse_ref, kb, vb, bias_tab, log_count_tab = rest[len(after):]
        g = pl.program_id(0)
        i = pl.program_id(1)

        @pl.when((g == 0) & (i == 0))
        def _():
            _fill_attn_log_count(log_count_tab)

        @pl.when(i == 0)
        def _():
            kb[...] = k_ref[...].astype(BF16)
            vb[...] = v_ref[...].astype(BF16)
            for u in range(hp):
                _fill_attn_bias(bias_tab.at[u], log_count_tab, _select_by_index(g * hp + u, slopes))

        qs = [q_ref[:, u * hd:(u + 1) * hd].astype(BF16) for u in range(hp)]

        def step(j, carry):
            rows = pl.ds(pl.multiple_of(j * t, t), t)
            out = []
            for u in range(hp):
                m_i, l_i, acc = carry[u]
                lanes = slice(u * hd, (u + 1) * hd)
                sc = lax.dot_general(qs[u], kb[rows, lanes], _NT_DIMS, preferred_element_type=F32) * scale
                sc = sc + bias_tab[u, i - j]
                m_new = jnp.maximum(m_i, jnp.max(sc, axis=-1, keepdims=True))
                p = jnp.exp(sc - m_new)
                alpha = jnp.exp(m_i - m_new)
                l_new = alpha * l_i + jnp.sum(p, axis=-1, keepdims=True)
                acc = alpha * acc + jnp.dot(p.astype(BF16), vb[rows, lanes], preferred_element_type=F32)
                out.append((m_new, l_new, acc))
            return tuple(out)

        init = (jnp.full((t, 1), NEG_BIG, F32), jnp.zeros((t, 1), F32), jnp.zeros((t, hd), F32))
        final = lax.fori_loop(0, i + 1, step, (init,) * hp)
        for u in range(hp):
            m_i, l_i, acc = final[u]
            lanes = slice(u * hd, (u + 1) * hd)
            out = acc / l_i
            o_ref[:, lanes] = out
            mix_ref[:, lanes] = out.astype(BF16)
            lse_ref[:, lanes] = jnp.broadcast_to(m_i + jnp.log(l_i), (t, hd))

    return pl.pallas_call(
        body, name="attn_fwd", grid=(ng, s // t),
        in_specs=[pl.BlockSpec((t, w), lambda g, i: (i, g)),
                  pl.BlockSpec((s, w), lambda g, i: (0, ng + g)),
                  pl.BlockSpec((s, w), lambda g, i: (0, 2 * ng + g))] + [ANY] * len(after),
        out_specs=[pl.BlockSpec((None, t, w), lambda g, i: (0, i, g))] + [pl.BlockSpec((t, w), lambda g, i: (i, g))] * 2,
        out_shape=[jax.ShapeDtypeStruct((2, s, ATTN_WIDTH), BF16),
                   jax.ShapeDtypeStruct((s, ATTN_WIDTH), F32),
                   jax.ShapeDtypeStruct((s, ATTN_WIDTH), F32)],
        scratch_shapes=[pltpu.VMEM((s, w), BF16), pltpu.VMEM((s, w), BF16), pltpu.VMEM((hp, s // t, t, t), F32),
                        pltpu.VMEM((s // t, t, t), F32)],
        compiler_params=_params(("arbitrary", "arbitrary")),
    )(proj, proj, proj, *after)


def _attn_bwd(proj, attn_out, lse, dmixed, after=()):
    after = tuple(after)
    s = proj.shape[0]
    t = SEQ_TILE
    nt = s // t
    hd = ATTN_HEAD_DIM
    hp = ATTN_HEADS_PER_STEP
    ng = ATTN_HEADS // hp
    w = hp * hd
    scale = 1.0 / math.sqrt(hd)
    slopes = _alibi_slopes()

    def body(q_ref, k_ref, v_ref, o_ref, lse_ref, do_ref, *rest):
        dsec_ref, qb, kb, vb, dob, dsum, dq_acc, bias_tab, log_count_tab = rest[len(after):]
        g = pl.program_id(0)

        @pl.when(g == 0)
        def _():
            _fill_attn_log_count(log_count_tab)

        qb[...] = q_ref[...].astype(BF16)
        kb[...] = k_ref[...].astype(BF16)
        vb[...] = v_ref[...].astype(BF16)
        dob[...] = do_ref[...].astype(BF16)
        for u in range(hp):
            lanes = slice(u * hd, (u + 1) * hd)
            _fill_attn_bias(bias_tab.at[u], log_count_tab, _select_by_index(g * hp + u, slopes))
            rowsum = jnp.sum(do_ref[:, lanes] * o_ref[:, lanes], axis=-1, keepdims=True)
            dsum[:, lanes] = jnp.broadcast_to(rowsum, (s, hd))
        dq_acc[...] = jnp.zeros((s, w), F32)

        def over_keys(j, _):
            krows = pl.ds(pl.multiple_of(j * t, t), t)

            def over_queries(i, carry):
                qrows = pl.ds(pl.multiple_of(i * t, t), t)
                out = []
                for u in range(hp):
                    dk, dv = carry[u]
                    lanes = slice(u * hd, (u + 1) * hd)
                    qi, doi = qb[qrows, lanes], dob[qrows, lanes]
                    kj, vj = kb[krows, lanes], vb[krows, lanes]
                    lse_i = lse_ref[qrows, lanes][:, :1]
                    dsum_i = dsum[qrows, lanes][:, :1]
                    sc = lax.dot_general(qi, kj, _NT_DIMS, preferred_element_type=F32) * scale
                    p = jnp.exp(sc + bias_tab[u, i - j] - lse_i)
                    dp = lax.dot_general(doi, vj, _NT_DIMS, preferred_element_type=F32)
                    ds = (p * (dp - dsum_i)).astype(BF16)
                    dv = dv + lax.dot_general(p.astype(BF16), doi, _TN_DIMS, preferred_element_type=F32)
                    dk = dk + lax.dot_general(ds, qi, _TN_DIMS, preferred_element_type=F32)
                    dq_acc[qrows, lanes] += jnp.dot(ds, kj, preferred_element_type=F32)
                    out.append((dk, dv))
                return tuple(out)

            zero = jnp.zeros((t, hd), F32)
            final = lax.fori_loop(j, nt, over_queries, ((zero, zero),) * hp)
            for u in range(hp):
                lanes = slice(u * hd, (u + 1) * hd)
                dsec_ref[1, krows, lanes] = (final[u][0] * scale).astype(BF16)
                dsec_ref[2, krows, lanes] = final[u][1].astype(BF16)
            return 0

        lax.fori_loop(0, nt, over_keys, 0)
        dsec_ref[0] = (dq_acc[...] * scale).astype(BF16)

    def col(off):
        return pl.BlockSpec((s, w), lambda g: (0, off + g))

    return pl.pallas_call(
        body, name="attn_bwd", grid=(ng,),
        in_specs=[col(0), col(ng), col(2 * ng), col(0), col(0), col(0)] + [ANY] * len(after),
        out_specs=pl.BlockSpec((4, s, w), lambda g: (0, 0, g)),
        out_shape=jax.ShapeDtypeStruct((8, s, ATTN_WIDTH), BF16),
        scratch_shapes=[pltpu.VMEM((s, w), BF16)] * 4 + [pltpu.VMEM((s, w), F32)] * 2
        + [pltpu.VMEM((hp, nt, t, t), F32), pltpu.VMEM((nt, t, t), F32)],
        compiler_params=_params(("arbitrary",)),
    )(proj, proj, proj, attn_out, lse, dmixed, *after)


def _ret_log_gammas():
    return [math.log(1.0 - 2.0 ** (-5.0 - h)) for h in range(RET_HEADS)]


def _ret_decay(delta, log_gamma):
    dec = jnp.exp(delta.astype(F32) * log_gamma) * (1.0 / math.sqrt(RET_HEAD_DIM))
    return jnp.where(delta >= 0, dec, 0.0)


def _ret_fwd(proj, mixed, after=()):
    after = tuple(after)
    s = proj.shape[0]
    t = SEQ_TILE
    hd = RET_HEAD_DIM
    nh = RET_HEADS
    log_gammas = _ret_log_gammas()
    c0 = 3 * ATTN_WIDTH // hd

    def body(q_ref, k_ref, v_ref, g_ref, *rest):
        mix_ref, raw_ref, kb, vb, decay_tab = rest[1 + len(after):]
        h = pl.program_id(0)
        i = pl.program_id(1)

        @pl.when(i == 0)
        def _():
            kb[...] = k_ref[...].astype(BF16)
            vb[...] = v_ref[...].astype(BF16)
            _fill_ret_decay(decay_tab, _select_by_index(h, log_gammas))

        q = q_ref[...].astype(BF16)

        def step(j, acc):
            rows = pl.ds(pl.multiple_of(j * t, t), t)
            sc = lax.dot_general(q, kb[rows, :], _NT_DIMS, preferred_element_type=F32) * decay_tab[i - j]
            return acc + jnp.dot(sc.astype(BF16), vb[rows, :], preferred_element_type=F32)

        ret = lax.fori_loop(0, i + 1, step, jnp.zeros((t, hd), F32))
        raw_ref[...] = ret
        r = lax.rsqrt(jnp.mean(ret * ret, axis=-1, keepdims=True) + NORM_EPS)
        g = g_ref[...]
        mix_ref[...] = (g * _sigmoid(g) * (ret * r)).astype(BF16)

    return pl.pallas_call(
        body, name="ret_fwd", grid=(nh, s // t),
        in_specs=[pl.BlockSpec((t, hd), lambda h, i: (i, c0 + h)),
                  pl.BlockSpec((s, hd), lambda h, i: (0, c0 + nh + h)),
                  pl.BlockSpec((s, hd), lambda h, i: (0, c0 + 2 * nh + h)),
                  pl.BlockSpec((t, hd), lambda h, i: (i, c0 + 3 * nh + h))] + [ANY] * (1 + len(after)),
        out_specs=[pl.BlockSpec((None, t, hd), lambda h, i: (1, i, h)), pl.BlockSpec((t, hd), lambda h, i: (i, h))],
        out_shape=[jax.ShapeDtypeStruct(mixed.shape, BF16), jax.ShapeDtypeStruct((s, RET_WIDTH), F32)],
        input_output_aliases={4: 0},
        scratch_shapes=[pltpu.VMEM((s, hd), BF16), pltpu.VMEM((s, hd), BF16), pltpu.VMEM((s // t, t, t), F32)],
        compiler_params=_params(("arbitrary", "arbitrary")),
    )(proj, proj, proj, proj, mixed, *after)


def _ret_bwd(proj, ret_raw, dmixed, dsec, after=()):
    after = tuple(after)
    s = proj.shape[0]
    t = SEQ_TILE
    nt = s // t
    hd = RET_HEAD_DIM
    nh = RET_HEADS
    log_gammas = _ret_log_gammas()
    c0 = 3 * ATTN_WIDTH // hd
    mixed_blocks = ATTN_WIDTH // hd

    def body(q_ref, k_ref, v_ref, g_ref, raw_ref, dmix_ref, *rest):
        dsec_ref, qb, kb, vb, dretb, dq_acc, decay_tab = rest[1 + len(after):]
        h = pl.program_id(0)
        _fill_ret_decay(decay_tab, _select_by_index(h, log_gammas))
        qb[...] = q_ref[...].astype(BF16)
        kb[...] = k_ref[...].astype(BF16)
        vb[...] = v_ref[...].astype(BF16)
        ret = raw_ref[...]
        r = lax.rsqrt(jnp.mean(ret * ret, axis=-1, keepdims=True) + NORM_EPS)
        normed = ret * r
        g = g_ref[...]
        sg = _sigmoid(g)
        dout = dmix_ref[...]
        dsec_ref[3] = (dout * normed * sg * (1.0 + g * (1.0 - sg))).astype(BF16)
        dn = dout * g * sg
        dret = r * (dn - normed * jnp.mean(dn * normed, axis=-1, keepdims=True))
        dretb[...] = dret.astype(BF16)
        dq_acc[...] = jnp.zeros((s, hd), F32)

        def over_keys(j, _):
            krows = pl.ds(pl.multiple_of(j * t, t), t)
            kj = kb[krows, :]
            vj = vb[krows, :]

            def over_queries(i, carry):
                dk, dv = carry
                qrows = pl.ds(pl.multiple_of(i * t, t), t)
                qi = qb[qrows, :]
                doi = dretb[qrows, :]
                dec = decay_tab[i - j]
                a = (lax.dot_general(qi, kj, _NT_DIMS, preferred_element_type=F32) * dec).astype(BF16)
                da = (lax.dot_general(doi, vj, _NT_DIMS, preferred_element_type=F32) * dec).astype(BF16)
                dv = dv + lax.dot_general(a, doi, _TN_DIMS, preferred_element_type=F32)
                dk = dk + lax.dot_general(da, qi, _TN_DIMS, preferred_element_type=F32)
                dq_acc[qrows, :] += jnp.dot(da, kj, preferred_element_type=F32)
                return dk, dv

            zero = jnp.zeros((t, hd), F32)
            dk, dv = lax.fori_loop(j, nt, over_queries, (zero, zero))
            dsec_ref[1, krows, :] = dk.astype(BF16)
            dsec_ref[2, krows, :] = dv.astype(BF16)
            return 0

        lax.fori_loop(0, nt, over_keys, 0)
        dsec_ref[0] = dq_acc[...].astype(BF16)

    def col(off):
        return pl.BlockSpec((s, hd), lambda h: (0, off + h))

    return pl.pallas_call(
        body, name="ret_bwd", grid=(nh,),
        in_specs=[col(c0), col(c0 + nh), col(c0 + 2 * nh), col(c0 + 3 * nh), col(0), col(mixed_blocks)]
        + [ANY] * (1 + len(after)),
        out_specs=pl.BlockSpec((4, s, hd), lambda h: (1, 0, h)),
        out_shape=jax.ShapeDtypeStruct(dsec.shape, BF16),
        input_output_aliases={6: 0},
        scratch_shapes=[pltpu.VMEM((s, hd), BF16)] * 4 + [pltpu.VMEM((s, hd), F32)]
        + [pltpu.VMEM((nt, t, t), F32)],
        compiler_params=_params(("arbitrary",)),
    )(proj, proj, proj, proj, ret_raw, dmixed, dsec, *after)


_FLIPS = (2, 1, 3)


def _other_chips(x, y):
    return [(1 - x, y), (x, 1 - y), (1 - x, 1 - y)]


_HBM = pl.BlockSpec(memory_space=pltpu.HBM)
_SEM = pl.BlockSpec(memory_space=pltpu.SEMAPHORE)
_EFFECT = pltpu.SideEffectType.DATAFLOW_SIDE_EFFECTING


def _in_hbm(a):
    return pltpu.with_memory_space_constraint(a, pltpu.HBM)


def _weight_view(w, column_sharded):
    if column_sharded:
        return w.reshape(2, w.shape[0] // 2, w.shape[1])
    return w.reshape(N_CHIPS, 2, w.shape[0] // (2 * N_CHIPS), w.shape[1])


def _weight_unview(v):
    if v.ndim == 3:
        return v.reshape(2 * v.shape[1], v.shape[2])
    return v.reshape(N_CHIPS * 2 * v.shape[2], v.shape[3])


def _weight_region(buf, shard, half):
    if len(buf.shape) == 3:
        cols = buf.shape[2] // N_CHIPS
        return buf.at[half, :, pl.ds(shard * cols, cols)]
    return buf.at[shard, half]


def _remote(where, send_sem, recv_sem, to):
    return pltpu.make_async_remote_copy(src_ref=where, dst_ref=where, send_sem=send_sem, recv_sem=recv_sem,
                                        device_id=to, device_id_type=MESH)


def _for_my_shard(fn):
    x, y, _ = _place()
    for ss in range(N_CHIPS):
        pl.when(2 * x + y == ss)(functools.partial(fn, ss))


def _gather_forward(views, which, send_sems, recv_sems, after, name, base=0):
    n_w = len(views)
    which = [base // 3 + w for w in which] if base % 3 == 0 else None
    assert which is not None, "base must be a multiple of 3"

    def body(*refs):
        send_in, recv_in = refs[n_w:n_w + 2]
        fwd_send, fwd_recv = refs[n_w + 3:n_w + 5]
        bufs = refs[n_w + 5:]
        x, y, c = _place()
        sibling = (x, y, 1 - c)

        def forward(ss):
            for i, w in enumerate(which):
                for j in range(3):
                    landed = _weight_region(bufs[i], ss ^ _FLIPS[j], c)
                    _remote(landed, send_in.at[3 * w + j], recv_in.at[3 * w + j], sibling).wait_recv()
                    _remote(landed, fwd_send.at[3 * i + j], fwd_recv.at[3 * i + j], sibling).start()

        _for_my_shard(forward)
        for i, w in enumerate(which):
            for j in range(3):
                _remote(_weight_region(bufs[i], 0, 0), send_in.at[3 * w + j], recv_in.at[3 * w + j],
                        sibling).wait_send()

    return pl.pallas_call(
        body, name=name,
        in_specs=[_HBM] * n_w + [_SEM, _SEM, ANY], out_specs=[_SEM, _SEM] + [_HBM] * n_w,
        out_shape=[pltpu.SemaphoreType.DMA((3 * n_w,)), pltpu.SemaphoreType.DMA((3 * n_w,))]
        + [pltpu.HBM(v.shape, BF16) for v in views],
        input_output_aliases={w: 2 + w for w in range(n_w)},
        compiler_params=pltpu.CompilerParams(has_side_effects=_EFFECT),
    )(*views, send_sems, recv_sems, after)


def _gather_end(views, fwd_send, fwd_recv, after, name):
    n_w = len(views)

    def body(*refs):
        fwd_send_ref, fwd_recv_ref = refs[n_w:n_w + 2]
        bufs = refs[n_w + 3:]
        x, y, c = _place()
        for i in range(n_w):
            for j in range(3):
                cp = _remote(_weight_region(bufs[i], 0, 0), fwd_send_ref.at[3 * i + j], fwd_recv_ref.at[3 * i + j],
                             (x, y, 1 - c))
                cp.wait_recv()
                cp.wait_send()

    outs = pl.pallas_call(
        body, name=name,
        in_specs=[_HBM] * n_w + [_SEM, _SEM, ANY], out_specs=[_HBM] * n_w,
        out_shape=[pltpu.HBM(v.shape, BF16) for v in views],
        input_output_aliases={w: w for w in range(n_w)},
        compiler_params=pltpu.CompilerParams(has_side_effects=_EFFECT),
    )(*views, fwd_send, fwd_recv, after)
    return [_weight_unview(o) for o in outs]


def _comm_call(name, bufs, sem_pairs, after, n_new, fn):
    n, n_sem, after = len(bufs), 2 * len(sem_pairs), tuple(after)
    n_out_sem = 2 if n_new else 0

    def body(*refs):
        sems = refs[n:n + n_sem]
        outs = refs[n + n_sem + len(after):]
        new = outs[:n_out_sem] if n_new else (None, None)
        fn(outs[n_out_sem:], [(sems[2 * i], sems[2 * i + 1]) for i in range(len(sem_pairs))], *new)

    res = pl.pallas_call(
        body, name=name,
        in_specs=[_HBM] * n + [_SEM] * n_sem + [ANY] * len(after),
        out_specs=[_SEM] * n_out_sem + [_HBM] * n,
        out_shape=[pltpu.SemaphoreType.DMA((n_new,))] * n_out_sem + [pltpu.HBM(b.shape, b.dtype) for b in bufs],
        input_output_aliases={i: n_out_sem + i for i in range(n)},
        compiler_params=pltpu.CompilerParams(has_side_effects=_EFFECT),
    )(*bufs, *[s for pair in sem_pairs for s in pair], *after)
    return list(res[:n_out_sem]), list(res[n_out_sem:])


def _quarter(piece, q):
    rows = piece.shape[0] // 2
    return piece.at[pl.ds(q * rows, rows)]


def _gather_in_start(view, name):
    def fn(bufs, _, send, recv):
        x, y, c = _place()

        def go(ss):
            for j, chip in enumerate(_other_chips(x, y)[:2]):
                _remote(_weight_region(bufs[0], ss, c), send.at[j], recv.at[j], (*chip, c)).start()

        _for_my_shard(go)

    sems, (view,) = _comm_call(name, [_in_hbm(view)], [], (), 2, fn)
    return sems, view


def _gather_out_gate_start(v_out, v_gate, after, name):
    def fn(bufs, _, send, recv):
        x, y, c = _place()
        chips = _other_chips(x, y)

        def go(ss):
            for j in range(3):
                _remote(_weight_region(bufs[0], ss, c), send.at[j], recv.at[j], (*chips[j], c)).start()
            for j in range(2):
                _remote(_weight_region(bufs[1], ss, c), send.at[3 + j], recv.at[3 + j], (*chips[j], c)).start()

        _for_my_shard(go)

    sems, views = _comm_call(name, [_in_hbm(v_out), _in_hbm(v_gate)], [], after, 5, fn)
    return sems, views


def _gather_relay(view, started, base, after, name, then=None, then_peers=0):
    n_new = 6 + then_peers if then_peers else 4

    def fn(bufs, pairs, send, recv):
        (send_in, recv_in), = pairs
        x, y, c = _place()
        chips = _other_chips(x, y)
        sibling = (x, y, 1 - c)

        def go(ss):
            landed = [_weight_region(bufs[0], ss ^ _FLIPS[j], c) for j in range(2)]
            for j in range(2):
                _remote(landed[j], send_in.at[base + j], recv_in.at[base + j], sibling).wait_recv()
            for j in range(2):
                _remote(_quarter(landed[j], j), send.at[j], recv.at[j], (*chips[1 - j], c)).start()
            for j in range(2):
                _remote(landed[j], send.at[2 + j], recv.at[2 + j], sibling).start()
            for j in range(then_peers):
                _remote(_weight_region(bufs[1], ss, c), send.at[6 + j], recv.at[6 + j], (*chips[j], c)).start()

        _for_my_shard(go)
        for j in range(2):
            _remote(_weight_region(bufs[0], 0, 0), send_in.at[base + j], recv_in.at[base + j], sibling).wait_send()

    views = [view] if then is None else [view, _in_hbm(then)]
    sems, views = _comm_call(name, views, [started], after, n_new, fn)
    return sems, views


def _gather_in_neighbours_end(view, relayed, after, name):
    def fn(bufs, pairs, *_):
        (send, recv), = pairs
        x, y, c = _place()
        for j in range(2):
            cp = _remote(_weight_region(bufs[0], 0, 0), send.at[2 + j], recv.at[2 + j], (x, y, 1 - c))
            cp.wait_recv()
            cp.wait_send()

    _, (view,) = _comm_call(name, [view], [relayed], after, 0, fn)
    return view


def _gather_in_diagonal(view, relayed, after, name):
    def fn(bufs, pairs, send, recv):
        (send_in, recv_in), = pairs
        x, y, c = _place()
        sibling = (x, y, 1 - c)
        any_quarter = _quarter(_weight_region(bufs[0], 0, 0), 0)
        for j in range(2):
            cp = _remote(any_quarter, send_in.at[j], recv_in.at[j], sibling)
            cp.wait_recv()
            cp.wait_send()

        def go(ss):
            _remote(_weight_region(bufs[0], ss ^ _FLIPS[2], c), send.at[0], recv.at[0], sibling).start()

        _for_my_shard(go)

    sems, (view,) = _comm_call(name, [view], [relayed], after, 1, fn)
    return sems, view


def _gather_in_diagonal_end(view, forwarded, after, name):
    def fn(bufs, pairs, *_):
        (send, recv), = pairs
        x, y, c = _place()
        cp = _remote(_weight_region(bufs[0], 0, 0), send.at[0], recv.at[0], (x, y, 1 - c))
        cp.wait_recv()
        cp.wait_send()

    _, (view,) = _comm_call(name, [view], [forwarded], after, 0, fn)
    return view


def _in_proj_shard(h1, wi, proj, shard_arr, name):
    s, d = h1.shape
    n = wi.shape[1]
    tn = 256
    blocks = n // (N_CHIPS * tn)
    given = [] if proj is None else [proj]

    def body(shard_ref, h_ref, w_ref, *rest):
        del shard_ref
        rest[-1][...] = jnp.dot(h_ref[...], w_ref[...], preferred_element_type=F32)

    grid_spec = pltpu.PrefetchScalarGridSpec(
        num_scalar_prefetch=1, grid=(blocks,),
        in_specs=[pl.BlockSpec((s, d), lambda j, shard_ref: (0, 0)),
                  pl.BlockSpec((d, tn), lambda j, shard_ref: (0, shard_ref[0] * blocks + j))] + [ANY] * len(given),
        out_specs=pl.BlockSpec((s, tn), lambda j, shard_ref: (0, shard_ref[0] * blocks + j)))
    return pl.pallas_call(
        body, name=name, grid_spec=grid_spec,
        out_shape=jax.ShapeDtypeStruct((s, n), F32),
        input_output_aliases={3: 0} if given else {},
        compiler_params=_params(("arbitrary",)),
    )(shard_arr, h1, wi, *given)


def _split_start(name, bufs, n_sems, copies):
    n = len(bufs)

    def body(*refs):
        send_sems, recv_sems = refs[n:n + 2]
        for cp in copies(refs[n + 2:], send_sems, recv_sems):
            cp.start()

    outs = pl.pallas_call(
        body, name=name,
        in_specs=[_HBM] * n, out_specs=[_SEM, _SEM] + [_HBM] * n,
        out_shape=[pltpu.SemaphoreType.DMA((n_sems,)), pltpu.SemaphoreType.DMA((n_sems,))]
        + [pltpu.HBM(b.shape, b.dtype) for b in bufs],
        input_output_aliases={i: 2 + i for i in range(n)},
        compiler_params=pltpu.CompilerParams(has_side_effects=_EFFECT),
    )(*[_in_hbm(b) for b in bufs])
    return outs[0], outs[1], list(outs[2:])


def _split_wait(name, bufs, send_sems, recv_sems, copies, after):
    n = len(bufs)

    def body(*refs):
        send_ref, recv_ref = refs[n:n + 2]
        for cp in copies(refs[n + 3:], send_ref, recv_ref):
            cp.wait()

    return list(pl.pallas_call(
        body, name=name,
        in_specs=[_HBM] * n + [_SEM, _SEM, ANY], out_specs=[_HBM] * n,
        out_shape=[pltpu.HBM(b.shape, b.dtype) for b in bufs],
        input_output_aliases={i: i for i in range(n)},
        compiler_params=pltpu.CompilerParams(has_side_effects=_EFFECT),
    )(*bufs, send_sems, recv_sems, after))


def _halves_copies(n_w):
    def copies(bufs, send_sems, recv_sems):
        x, y, c = _place()
        out = []
        for w in range(n_w):
            view, land = bufs[w], bufs[n_w + w]
            src = view.at[1 - c] if len(view.shape) == 3 else view.at[:, 1 - c]
            out.append(pltpu.make_async_remote_copy(
                src_ref=src, dst_ref=land, send_sem=send_sems.at[w], recv_sem=recv_sems.at[w],
                device_id=(x, y, 1 - c), device_id_type=MESH))
        return out
    return copies


def _pieces_copies(n_w):
    def copies(bufs, send_sems, recv_sems):
        x, y, c = _place()
        out = []
        for w in range(n_w):
            for j, (cx, cy) in enumerate(_other_chips(x, y)):
                out.append(pltpu.make_async_remote_copy(
                    src_ref=bufs[w].at[2 * cx + cy], dst_ref=bufs[n_w + w].at[j],
                    send_sem=send_sems.at[3 * w + j], recv_sem=recv_sems.at[3 * w + j],
                    device_id=(cx, cy, c), device_id_type=MESH))
        return out
    return copies


def _join_copies(n_w):
    def copies(bufs, send_sems, recv_sems):
        x, y, c = _place()
        return [pltpu.make_async_remote_copy(
            src_ref=bufs[w].at[c], dst_ref=bufs[w].at[c], send_sem=send_sems.at[w], recv_sem=recv_sems.at[w],
            device_id=(x, y, 1 - c), device_id_type=MESH) for w in range(n_w)]
    return copies


def _halves_landing(view):
    shape = view.shape[1:] if view.ndim == 3 else (N_CHIPS,) + view.shape[2:]
    return lax.empty(shape, BF16)


def _halves_start(tag, grads, column_sharded):
    views = [_weight_view(g, cs) for g, cs in zip(grads, column_sharded)]
    n = len(views)
    return _split_start("halves_start_" + tag, views + [_halves_landing(v) for v in views], n, _halves_copies(n))


def _halves_wait(tag, state, after):
    send_sems, recv_sems, bufs = state
    n = len(bufs) // 2
    bufs = _split_wait("halves_wait_" + tag, bufs, send_sems, recv_sems, _halves_copies(n), after)
    return bufs[:n], bufs[n:]


def _pieces_start(tag, pieces):
    n = len(pieces)
    landing = [lax.empty((3,) + p.shape[1:], BF16) for p in pieces]
    return _split_start("pieces_start_" + tag, list(pieces) + landing, 3 * n, _pieces_copies(n))


def _pieces_wait(tag, state, after):
    send_sems, recv_sems, bufs = state
    n = len(bufs) // 2
    bufs = _split_wait("pieces_wait_" + tag, bufs, send_sems, recv_sems, _pieces_copies(n), after)
    return bufs[:n], bufs[n:]


def _join_start(tag, shards):
    n = len(shards)
    return _split_start("join_start_" + tag, list(shards), n, _join_copies(n))


def _join_wait(tag, state, after):
    send_sems, recv_sems, bufs = state
    bufs = _split_wait("join_wait_" + tag, bufs, send_sems, recv_sems, _join_copies(len(bufs)), after)
    return [b.reshape(2 * b.shape[1], b.shape[2]) for b in bufs]


def _chip_sum_col(g3, sib, c_arr, name):
    _, hk, n = g3.shape
    cols = n // N_CHIPS
    tr = _row_tile(hk, cols * 2, limit=4 * 1024 * 1024)

    def body(c_ref, g_ref, s_ref, o_ref):
        del c_ref
        o_ref[...] = (g_ref[...].astype(F32) + s_ref[...].astype(F32)).astype(BF16)

    grid_spec = pltpu.PrefetchScalarGridSpec(
        num_scalar_prefetch=1, grid=(N_CHIPS, hk // tr),
        in_specs=[pl.BlockSpec((None, tr, cols), lambda p, r, c_ref: (c_ref[0], r, p)),
                  pl.BlockSpec((tr, cols), lambda p, r, c_ref: (r, p))],
        out_specs=pl.BlockSpec((None, tr, cols), lambda p, r, c_ref: (p, r, 0)))
    return pl.pallas_call(
        body, name=name, grid_spec=grid_spec,
        out_shape=jax.ShapeDtypeStruct((N_CHIPS, hk, cols), BF16),
        compiler_params=_params(("parallel", "parallel")),
    )(c_arr, g3, sib)


def _chip_sum_row(g4, sib, c_arr, name):
    _, _, hr, n = g4.shape
    tr = _row_tile(hr, n * 2, limit=4 * 1024 * 1024)

    def body(c_ref, g_ref, s_ref, o_ref):
        del c_ref
        o_ref[...] = (g_ref[...].astype(F32) + s_ref[...].astype(F32)).astype(BF16)

    grid_spec = pltpu.PrefetchScalarGridSpec(
        num_scalar_prefetch=1, grid=(N_CHIPS, hr // tr),
        in_specs=[pl.BlockSpec((None, None, tr, n), lambda p, r, c_ref: (p, c_ref[0], r, 0)),
                  pl.BlockSpec((None, tr, n), lambda p, r, c_ref: (p, r, 0))],
        out_specs=pl.BlockSpec((None, tr, n), lambda p, r, c_ref: (p, r, 0)))
    return pl.pallas_call(
        body, name=name, grid_spec=grid_spec,
        out_shape=jax.ShapeDtypeStruct((N_CHIPS, hr, n), BF16),
        compiler_params=_params(("parallel", "parallel")),
    )(c_arr, g4, sib)


def _sum_pieces(pieces, received, place_arr, name):
    _, r, n = pieces.shape
    tr = _row_tile(r, n * 4, limit=4 * 1024 * 1024)

    def body(p_ref, own_ref, r0_ref, r1_ref, r2_ref, o_ref):
        del p_ref
        acc = own_ref[...].astype(F32) + r0_ref[...].astype(F32)
        acc = acc + r1_ref[...].astype(F32)
        o_ref[...] = acc + r2_ref[...].astype(F32)

    def recv_spec(j):
        return pl.BlockSpec((None, tr, n), lambda i, p_ref: (j, i, 0))

    grid_spec = pltpu.PrefetchScalarGridSpec(
        num_scalar_prefetch=1, grid=(r // tr,),
        in_specs=[pl.BlockSpec((None, tr, n), lambda i, p_ref: (p_ref[0], i, 0)),
                  recv_spec(0), recv_spec(1), recv_spec(2)],
        out_specs=pl.BlockSpec((None, tr, n), lambda i, p_ref: (p_ref[1], i, 0)))
    return pl.pallas_call(
        body, name=name, grid_spec=grid_spec,
        out_shape=jax.ShapeDtypeStruct((2, r, n), F32),
        compiler_params=_params(("parallel",)),
    )(place_arr, pieces, received, received, received)


def _norm_weights_step(parts, w, m, v, after=()):
    rows, d = parts.shape
    after = tuple(after)

    def body(p_ref, w_ref, m_ref, v_ref, *rest):
        g_ref, d_ref, mo_ref, vo_ref, gathered, send_sems, recv_sems = rest[len(after):]
        x, y, c = _place()
        me = 4 * x + 2 * y + c
        gathered[me] = p_ref[...]
        copies = []
        for k in range(1, N_DEV):
            peer = (x ^ ((k >> 2) & 1), y ^ ((k >> 1) & 1), c ^ (k & 1))
            copies.append(pltpu.make_async_remote_copy(
                src_ref=p_ref, dst_ref=gathered.at[me], send_sem=send_sems.at[k - 1],
                recv_sem=recv_sems.at[k - 1], device_id=peer, device_id_type=MESH))
        for cp in copies:
            cp.start()
        for cp in copies:
            cp.wait()
        g = gathered[0]
        for k in range(1, N_DEV):
            g = g + gathered[k]
        delta, m_new, v_new = _adamw_math(w_ref[...], g, m_ref[...], v_ref[...])
        g_ref[...] = g
        d_ref[...] = delta
        mo_ref[...] = m_new
        vo_ref[...] = v_new

    vmem = pl.BlockSpec(memory_space=pltpu.VMEM)
    shp = jax.ShapeDtypeStruct((rows, d), F32)
    return pl.pallas_call(
        body, name="norm_weights_step",
        in_specs=[vmem] * 4 + [ANY] * len(after), out_specs=[vmem] * 4, out_shape=[shp] * 4,
        scratch_shapes=[pltpu.VMEM((N_DEV, rows, d), F32), pltpu.SemaphoreType.DMA((N_DEV - 1,)),
                        pltpu.SemaphoreType.DMA((N_DEV - 1,))],
        compiler_params=pltpu.CompilerParams(has_side_effects=True),
    )(parts, w, m, v, *after)


def kernel(x, norm_mix_w, w_in, w_out, norm_ffn_w, w_gate, w_up, w_down, norm_final_w, loss_target, m_norm_mix_w, m_w_in, m_w_out, m_norm_ffn_w, m_w_gate, m_w_up, m_w_down, m_norm_final_w, v_norm_mix_w, v_w_in, v_w_out, v_norm_ffn_w, v_w_gate, v_w_up, v_w_down, v_norm_final_w):
    s, d = x.shape[1], x.shape[2]
    xs = x.reshape(s, d)
    target = loss_target.reshape(s, d)
    big = {"w_in": (w_in, m_w_in, v_w_in), "w_out": (w_out, m_w_out, v_w_out),
           "w_gate": (w_gate, m_w_gate, v_w_gate), "w_up": (w_up, m_w_up, v_w_up),
           "w_down": (w_down, m_w_down, v_w_down)}
    big = {k: tuple(a.reshape(a.shape[1:]) for a in t) for k, t in big.items()}
    col_names, row_names = ("w_in", "w_gate", "w_up"), ("w_out", "w_down")
    n_in = N_CHIPS * big["w_in"][0].shape[1]
    ffn = N_CHIPS * big["w_gate"][0].shape[1]
    mix = ATTN_WIDTH + RET_WIDTH
    c_arr = lax.axis_index("c").astype(I32).reshape(1)
    shard_arr = (2 * lax.axis_index("x") + lax.axis_index("y")).astype(I32).reshape(1)
    place_arr = jnp.concatenate([shard_arr, c_arr])

    def cast(k, after=()):
        return _weight_view(_cast_into_full(big[k][0], shard_arr, k in col_names, "cast_" + k, after), k in col_names)

    started_in, v_in = _gather_in_start(cast("w_in"), "gather_in_start")

    sec = ATTN_WIDTH

    def section(p, rows):
        return pl.BlockSpec((None, rows, sec), lambda i, j, kk: (p, i, 0))

    h1 = _rms_fwd(xs, norm_mix_w, "rms_mix_fwd", after=[v_in])
    my_shard = shard_arr[0]
    shard_of = [jnp.bitwise_xor(my_shard, f).astype(I32).reshape(1) for f in (0,) + _FLIPS]
    proj = _in_proj_shard(h1, _weight_unview(v_in), None, shard_of[0], "in_proj_own")
    early_views = [cast(k, after=[proj]) for k in ("w_out", "w_gate")]
    v_up, v_down = [cast(k, after=[proj]) for k in ("w_up", "w_down")]
    relayed_in, (v_in,) = _gather_relay(v_in, started_in, 0, early_views + [v_up, v_down], "gather_in_relay")
    started_og, (v_out, v_gate) = _gather_out_gate_start(*early_views, [v_in], "gather_out_gate_start")
    v_in = _gather_in_neighbours_end(v_in, relayed_in, [v_out], "gather_in_neighbours_end")
    proj = _in_proj_shard(h1, _weight_unview(v_in), proj, shard_of[1], "in_proj_x")
    proj = _in_proj_shard(h1, _weight_unview(v_in), proj, shard_of[2], "in_proj_y")
    forwarded_in, v_in = _gather_in_diagonal(v_in, relayed_in, [proj], "gather_in_diagonal")
    wi = _weight_unview(_gather_in_diagonal_end(v_in, forwarded_in, [proj], "gather_in_diagonal_end"))
    proj = _in_proj_shard(h1, wi, proj, shard_of[3], "in_proj_diagonal")
    fs_o, fr_o, v_out = _gather_forward([v_out], [0], *started_og, proj, "gather_forward_out")
    mixed, attn_o, lse = _attn_fwd(proj, after=[v_out])
    relayed_g, (v_gate, v_up) = _gather_relay(v_gate, started_og, 3, [attn_o], "gather_gate_relay",
                                              then=v_up, then_peers=2)
    mixed, ret_raw = _ret_fwd(proj, mixed, after=[v_gate])
    wo, = _gather_end([v_out], fs_o, fr_o, ret_raw, "gather_end_out")
    x1, = _matmul("out_proj", "nn", [mixed, mixed], [wo, wo], [0, 0], s, d, sec, s, 512, sec, [xs], [F32],
                  _epi_residual, b_koff=[0, 1], a_specs=[section(0, s), section(1, s)])
    h2 = _rms_fwd(x1, norm_ffn_w, "rms_ffn_fwd")
    relayed_u, (v_up, v_down) = _gather_relay(v_up, relayed_g, 6, [h2], "gather_up_relay",
                                              then=v_down, then_peers=3)
    v_gate = _gather_in_neighbours_end(v_gate, relayed_g, [v_up], "gather_gate_neighbours_end")
    forwarded_g, v_gate = _gather_in_diagonal(v_gate, relayed_g, [v_up], "gather_gate_diagonal")
    v_up = _gather_in_neighbours_end(v_up, relayed_u, [v_gate], "gather_up_neighbours_end")
    wg = _weight_unview(_gather_in_diagonal_end(v_gate, forwarded_g, [v_up], "gather_gate_diagonal_end"))
    forwarded_u, v_up = _gather_in_diagonal(v_up, relayed_u, [wg], "gather_up_diagonal")
    wu = _weight_unview(_gather_in_diagonal_end(v_up, forwarded_u, [wg], "gather_up_diagonal_end"))
    gate, up, act = _matmul("gate_up", "nn", [h2, h2], [wg, wu], [0, 1], s, ffn, d, s, 512, d, [],
                            [BF16, BF16, BF16], _epi_swiglu, a_single_buffer=True)
    fs, fr, v_down = _gather_forward([v_down], [0], *relayed_u, act, "gather_forward_down", base=6)
    wd, = _gather_end([v_down], fs, fr, act, "gather_end_down")
    x2, = _matmul("down_proj", "nn", [act], [wd], [0], s, d, ffn, s // 2, 512, ffn, [x1], [F32],
                  _epi_residual)
    loss_row, dx2, dx2b, dwf = _final_norm_loss(x2, norm_final_w.reshape(1, d), target, "final_norm_loss")

    names = col_names + row_names
    grads, new = {}, {}

    def chip_sums(tag_names, views, sibs):
        return [(_chip_sum_col if k in col_names else _chip_sum_row)(v, sb, c_arr, "chip_sum_" + k)
                for k, v, sb in zip(tag_names, views, sibs)]

    def piece_sums(tag_names, pieces, received):
        return [_sum_pieces(p, r, place_arr, "sum_pieces_" + k) for k, p, r in zip(tag_names, pieces, received)]

    def update(k):
        new[k] = _adamw(big[k][0], grads[k], big[k][1], big[k][2], "adamw_" + k)

    dgate, dup = _matmul("d_act", "nt", [dx2b], [wd], [0], s, ffn, d, s, 512, d, [gate, up],
                         [BF16, BF16], _epi_swiglu_bwd, a_single_buffer=True)
    g_wd, = _matmul("g_w_down", "tn", [act], [dx2b], [0], ffn, d, s, 512, d, s, [], [BF16], _epi_plain)
    halves_d = _halves_start("down", [g_wd], [False])
    dh2, = _matmul("d_h2", "nt", [dgate, dup], [wg, wu], [0, 0], s, d, ffn, s // 2, 256, ffn, [], [F32],
                   _epi_plain, after=halves_d[2][-1:], a_single_buffer=True)
    pieces_d = _pieces_start("down", chip_sums(["w_down"], *_halves_wait("down", halves_d, dh2)))
    g_wg, g_wu = _matmul("g_w_gate_up", "tn", [h2, h2], [dgate, dup], [0, 1], d, ffn, s, 1024, 512, s, [],
                         [BF16, BF16], _epi_two, after=pieces_d[2][-1:])
    halves_gu = _halves_start("gate_up", [g_wg, g_wu], [True, True])
    dx1, dx1b, dw_ffn = _rms_bwd(x1, norm_ffn_w, dh2, dx2, "rms_ffn_bwd", after=halves_gu[2][-1:])

    dmixed, = _matmul("d_mixed", "nt", [dx1b], [wo], [0], s, mix, d, s, 512, d, [], [F32], _epi_plain)
    pieces_gu = _pieces_start("gate_up", chip_sums(["w_gate", "w_up"], *_halves_wait("gate_up", halves_gu, dmixed)))
    per = sec // 512
    g_wo, = _matmul("g_w_out", "tn", [mixed], [dx1b], [0], mix, d, s, 512, d, s, [], [BF16], _epi_plain,
                    after=pieces_gu[2][-1:],
                    a_specs=[pl.BlockSpec((None, s, 512), lambda i, j, kk: (i // per, 0, i % per))])
    halves_o = _halves_start("out", [g_wo], [False])
    dsec = _attn_bwd(proj, attn_o, lse, dmixed, after=halves_o[2][-1:])
    pieces_o = _pieces_start("out", chip_sums(["w_out"], *_halves_wait("out", halves_o, dsec)))
    dsec = _ret_bwd(proj, ret_raw, dmixed, dsec, after=pieces_o[2][-1:])
    where = [0, 1, 2, 4, 5, 6, 7]
    n_sec = len(where)
    g_wi, = _matmul("g_w_in", "tn", [h1], [dsec], [0], d, n_in, s, 1024, sec, s, [], [BF16], _epi_plain,
                    b_specs=[pl.BlockSpec((None, s, sec), lambda i, j, kk: (j + (j >= 3).astype(I32), 0, 0))])
    halves_i = _halves_start("in", [g_wi], [True])
    dh1, = _matmul("d_h1", "nt", [dsec] * n_sec, [wi] * n_sec, [0] * n_sec, s, d, sec, s // 2, 256, sec, [], [F32],
                   _epi_plain, b_koff=list(range(n_sec)), after=halves_i[2][-1:],
                   a_specs=[section(p, s // 2) for p in where])
    pieces_i = _pieces_start("in", chip_sums(["w_in"], *_halves_wait("in", halves_i, dh1)))
    grad_x, _, dw_mix = _rms_bwd(xs, norm_mix_w, dh1, dx1, "rms_mix_bwd", after=pieces_i[2][-1:])

    def rows8(*vs):
        return jnp.concatenate([v.reshape(1, d) for v in vs] + [jnp.zeros((8 - len(vs), d), F32)], axis=0)

    join_d = _join_start("down", piece_sums(["w_down"], *_pieces_wait("down", pieces_d, grad_x)))
    join_gu = _join_start("gate_up", piece_sums(["w_gate", "w_up"], *_pieces_wait("gate_up", pieces_gu, join_d[2][0])))
    join_o = _join_start("out", piece_sums(["w_out"], *_pieces_wait("out", pieces_o, join_gu[2][0])))
    grads["w_down"], = _join_wait("down", join_d, join_o[2][0])
    update("w_down")
    grads["w_gate"], grads["w_up"] = _join_wait("gate_up", join_gu, new["w_down"][0])
    update("w_gate")
    update("w_up")
    grads["w_out"], = _join_wait("out", join_o, new["w_up"][0])
    update("w_out")
    join_i = _join_start("in", piece_sums(["w_in"], *_pieces_wait("in", pieces_i, new["w_out"][0])))
    ng, nd, nm, nv = _norm_weights_step(
        rows8(dw_mix, dw_ffn, dwf, jnp.broadcast_to(loss_row[:, :1], (1, d))),
        rows8(norm_mix_w, norm_ffn_w, norm_final_w),
        rows8(m_norm_mix_w, m_norm_ffn_w, m_norm_final_w), rows8(v_norm_mix_w, v_norm_ffn_w, v_norm_final_w),
        after=join_i[2][:1])
    grads["w_in"], = _join_wait("in", join_i, ng)
    update("w_in")

    loss = ng[3, 0]

    def pack(small, per_weight):
        lead = lambda a: a.reshape((1,) + a.shape)
        return (small[0:1], lead(per_weight["w_in"]), lead(per_weight["w_out"]), small[1:2],
                lead(per_weight["w_gate"]), lead(per_weight["w_up"]), lead(per_weight["w_down"]), small[2])

    return (loss, grad_x.reshape(1, s, d),
            *pack(ng, {k: new[k][3] for k in names}),
            *pack(nd, {k: new[k][0] for k in names}),
            *pack(nm, {k: new[k][1] for k in names}),
            *pack(nv, {k: new[k][2] for k in names}))
```

```python
import functools
import math

import jax
import jax.numpy as jnp
from jax import lax
from jax.experimental import pallas as pl
from jax.experimental.pallas import tpu as pltpu

F32 = jnp.float32
BF16 = jnp.bfloat16
I32 = jnp.int32
MESH = pl.DeviceIdType.MESH
ANY = pl.BlockSpec(memory_space=pl.ANY)

ATTN_HEADS = 8
ATTN_HEAD_DIM = 128
RET_HEADS = 4
RET_HEAD_DIM = 256
ATTN_WIDTH = ATTN_HEADS * ATTN_HEAD_DIM
RET_WIDTH = RET_HEADS * RET_HEAD_DIM
DILATED_PATTERNS = ((128, 1), (512, 4), (2048, 16))
NORM_EPS = 1e-6
ADAM_LR = 0.001
ADAM_B1 = 0.9
ADAM_B2 = 0.999
ADAM_EPS = 1e-08
ADAM_WD = 0.01
ADAM_STEP = 10

N_CHIPS = 4
N_DEV = 8
NEG_BIG = -1e30
SEQ_TILE = 512
ATTN_FWD_HEADS_PER_STEP = 2
ATTN_HEADS_PER_STEP = 1
VMEM_LIMIT_BYTES = 56 * 1024 * 1024


def _params(semantics=None, vmem=VMEM_LIMIT_BYTES):
    return pltpu.CompilerParams(dimension_semantics=semantics, vmem_limit_bytes=vmem)


def _row_tile(rows, row_bytes, limit=2 * 1024 * 1024, mult=16):
    best = None
    for t in range(mult, rows + 1, mult):
        if rows % t == 0 and t * row_bytes <= limit:
            best = t
    assert best is not None, (rows, row_bytes)
    return best


def _sigmoid(x):
    return 1.0 / (1.0 + jnp.exp(-x))


def _select_by_index(idx, values):
    out = jnp.float32(values[-1])
    for i in range(len(values) - 2, -1, -1):
        out = jnp.where(idx == i, jnp.float32(values[i]), out)
    return out


def _place():
    x, y, c = lax.axis_index("x"), lax.axis_index("y"), lax.axis_index("c")
    return x, y, c


def _cast_into_full(w, shard_arr, column_sharded, name, after=()):
    after = tuple(after)
    rows, cols = w.shape
    tr = _row_tile(rows, cols * 4)
    steps = rows // tr
    if column_sharded:
        out_shape, out_map = (rows, N_CHIPS * cols), (lambda i, s_ref: (i, s_ref[0]))
    else:
        out_shape, out_map = (N_CHIPS * rows, cols), (lambda i, s_ref: (s_ref[0] * steps + i, 0))

    def body(s_ref, w_ref, *rest):
        del s_ref
        rest[-1][...] = w_ref[...].astype(BF16)

    grid_spec = pltpu.PrefetchScalarGridSpec(
        num_scalar_prefetch=1, grid=(steps,),
        in_specs=[pl.BlockSpec((tr, cols), lambda i, s_ref: (i, 0))] + [ANY] * len(after),
        out_specs=pl.BlockSpec((tr, cols), out_map))
    return pl.pallas_call(
        body, name=name, grid_spec=grid_spec,
        out_shape=jax.ShapeDtypeStruct(out_shape, BF16),
        compiler_params=_params(("parallel",)),
    )(shard_arr, w, *after)


def _rms_fwd(x, w, name, after=()):
    rows, d = x.shape
    tr = 256
    after = tuple(after)

    def body(x_ref, w_ref, *rest):
        xv = x_ref[...]
        r = lax.rsqrt(jnp.mean(xv * xv, axis=-1, keepdims=True) + NORM_EPS)
        rest[-1][...] = (xv * r * w_ref[...]).astype(BF16)

    return pl.pallas_call(
        body, name=name, grid=(rows // tr,),
        in_specs=[pl.BlockSpec((tr, d), lambda i: (i, 0)), pl.BlockSpec((1, d), lambda i: (0, 0))]
        + [ANY] * len(after),
        out_specs=pl.BlockSpec((tr, d), lambda i: (i, 0)),
        out_shape=jax.ShapeDtypeStruct((rows, d), BF16),
        compiler_params=_params(("parallel",)),
    )(x, w, *after)


def _rms_bwd(x, w, dh, dres, name, after=()):
    rows, d = x.shape
    tr = 256
    after = tuple(after)

    def body(x_ref, w_ref, dh_ref, dres_ref, *rest):
        dx_ref, dxb_ref, dw_ref = rest[len(after):]
        xv = x_ref[...]
        r = lax.rsqrt(jnp.mean(xv * xv, axis=-1, keepdims=True) + NORM_EPS)
        xhat = xv * r
        dy = dh_ref[...]
        dxhat = dy * w_ref[...]
        dx = dres_ref[...] + r * (dxhat - xhat * jnp.mean(dxhat * xhat, axis=-1, keepdims=True))
        dx_ref[...] = dx
        dxb_ref[...] = dx.astype(BF16)
        part = jnp.sum(dy * xhat, axis=0, keepdims=True)

        @pl.when(pl.program_id(0) == 0)
        def _():
            dw_ref[...] = part

        @pl.when(pl.program_id(0) != 0)
        def _():
            dw_ref[...] += part

    row = pl.BlockSpec((tr, d), lambda i: (i, 0))
    vec = pl.BlockSpec((1, d), lambda i: (0, 0))
    return pl.pallas_call(
        body, name=name, grid=(rows // tr,),
        in_specs=[row, vec, row, row] + [ANY] * len(after),
        out_specs=[row, row, vec],
        out_shape=[jax.ShapeDtypeStruct((rows, d), F32), jax.ShapeDtypeStruct((rows, d), BF16),
                   jax.ShapeDtypeStruct((1, d), F32)],
        compiler_params=_params(("arbitrary",)),
    )(x, w, dh, dres, *after)


def _final_norm_loss(x2, w, target, name):
    rows, d = x2.shape
    tr = 256

    def body(x_ref, w_ref, t_ref, loss_ref, dx_ref, dxb_ref, dw_ref):
        xv = x_ref[...]
        wv = w_ref[...]
        r = lax.rsqrt(jnp.mean(xv * xv, axis=-1, keepdims=True) + NORM_EPS)
        xhat = xv * r
        err = xhat * wv - t_ref[...]
        part_loss = 0.5 * jnp.sum(jnp.mean(err * err, axis=-1, keepdims=True), axis=0, keepdims=True)
        dy = err * (1.0 / d)
        dxhat = dy * wv
        dx = r * (dxhat - xhat * jnp.mean(dxhat * xhat, axis=-1, keepdims=True))
        dx_ref[...] = dx
        dxb_ref[...] = dx.astype(BF16)
        part_dw = jnp.sum(dy * xhat, axis=0, keepdims=True)
        part_loss = jnp.broadcast_to(part_loss, (1, 128))

        @pl.when(pl.program_id(0) == 0)
        def _():
            dw_ref[...] = part_dw
            loss_ref[...] = part_loss

        @pl.when(pl.program_id(0) != 0)
        def _():
            dw_ref[...] += part_dw
            loss_ref[...] += part_loss

    row = pl.BlockSpec((tr, d), lambda i: (i, 0))
    vec = pl.BlockSpec((1, d), lambda i: (0, 0))
    return pl.pallas_call(
        body, name=name, grid=(rows // tr,),
        in_specs=[row, vec, row],
        out_specs=[pl.BlockSpec((1, 128), lambda i: (0, 0)), row, row, vec],
        out_shape=[jax.ShapeDtypeStruct((1, 128), F32), jax.ShapeDtypeStruct((rows, d), F32),
                   jax.ShapeDtypeStruct((rows, d), BF16), jax.ShapeDtypeStruct((1, d), F32)],
        compiler_params=_params(("arbitrary",)),
    )(x2, w, target)


def _adamw_math(w, g, m, v):
    m = ADAM_B1 * m + (1.0 - ADAM_B1) * g
    v = ADAM_B2 * v + (1.0 - ADAM_B2) * (g * g)
    m_hat = m / (1.0 - ADAM_B1 ** ADAM_STEP)
    v_hat = v / (1.0 - ADAM_B2 ** ADAM_STEP)
    delta = -ADAM_LR * (m_hat / (jnp.sqrt(v_hat) + ADAM_EPS) + ADAM_WD * w)
    return delta, m, v


def _adamw(w, g, m, v, name):
    rows, cols = w.shape
    tr = _row_tile(rows, cols * 4)

    def body(w_ref, g_ref, m_ref, v_ref, d_ref, mo_ref, vo_ref, go_ref):
        g = g_ref[...]
        delta, m_new, v_new = _adamw_math(w_ref[...], g, m_ref[...], v_ref[...])
        d_ref[...] = delta
        mo_ref[...] = m_new
        vo_ref[...] = v_new
        go_ref[...] = g

    blk = pl.BlockSpec((tr, cols), lambda i: (i, 0))
    shp = jax.ShapeDtypeStruct((rows, cols), F32)
    return pl.pallas_call(
        body, name=name, grid=(rows // tr,),
        in_specs=[blk] * 4, out_specs=[blk] * 4, out_shape=[shp] * 4,
        compiler_params=_params(("parallel",)),
    )(w, g, m, v)


_DOT_DIMS = {"nn": ((1,), (0,)), "nt": ((1,), (1,)), "tn": ((0,), (0,))}


def _matmul(name, mode, a_list, b_list, acc_of, m, n, k, tm, tn, tk, extras, out_dtypes, epilogue,
            a_koff=None, b_koff=None, after=(), a_specs=None, b_specs=None, a_single_buffer=False):
    after = tuple(after)
    assert m % tm == 0 and n % tn == 0 and k % tk == 0, (name, m, n, k, tm, tn, tk)
    nk = k // tk
    n_acc = max(acc_of) + 1
    n_pairs = len(a_list)
    a_koff = a_koff or [0] * n_pairs
    b_koff = b_koff or [0] * n_pairs
    dims = (_DOT_DIMS[mode], ((), ()))
    n_ext, n_out = len(extras), len(out_dtypes)

    def body(*refs):
        a_refs = refs[:n_pairs]
        b_refs = refs[n_pairs:2 * n_pairs]
        e_refs = refs[2 * n_pairs:2 * n_pairs + n_ext]
        first_out = 2 * n_pairs + n_ext + len(after)
        o_refs = refs[first_out:first_out + n_out]
        acc_refs = refs[first_out + n_out:]

        parts = [None] * n_acc
        for p in range(n_pairs):
            d = lax.dot_general(a_refs[p][...], b_refs[p][...], dims, preferred_element_type=F32)
            parts[acc_of[p]] = d if parts[acc_of[p]] is None else parts[acc_of[p]] + d

        def finish(accs):
            outs = epilogue(accs, [e[...] for e in e_refs])
            for o_ref, o in zip(o_refs, outs):
                o_ref[...] = o.astype(o_ref.dtype)

        if nk == 1:
            finish(parts)
        else:
            kk = pl.program_id(2)

            @pl.when(kk == 0)
            def _():
                for acc_ref, part in zip(acc_refs, parts):
                    acc_ref[...] = part

            @pl.when(kk != 0)
            def _():
                for acc_ref, part in zip(acc_refs, parts):
                    acc_ref[...] += part

            @pl.when(kk == nk - 1)
            def _():
                finish([acc_ref[...] for acc_ref in acc_refs])

    def a_spec(off):
        mode_a = pl.Buffered(1) if a_single_buffer else None
        if mode == "tn":
            return pl.BlockSpec((tk, tm), lambda i, j, kk: (kk + off, i), pipeline_mode=mode_a)
        return pl.BlockSpec((tm, tk), lambda i, j, kk: (i, kk + off), pipeline_mode=mode_a)

    def b_spec(off):
        if mode == "nt":
            return pl.BlockSpec((tn, tk), lambda i, j, kk: (j, kk + off))
        return pl.BlockSpec((tk, tn), lambda i, j, kk: (kk + off, j))

    tile = pl.BlockSpec((tm, tn), lambda i, j, kk: (i, j))
    scratch = [pltpu.VMEM((tm, tn), F32) for _ in range(n_acc)] if nk > 1 else []
    return pl.pallas_call(
        body, name=name, grid=(m // tm, n // tn, nk),
        in_specs=(a_specs or [a_spec(o) for o in a_koff]) + (b_specs or [b_spec(o) for o in b_koff])
        + [tile] * n_ext + [ANY] * len(after),
        out_specs=[tile] * n_out,
        out_shape=[jax.ShapeDtypeStruct((m, n), dt) for dt in out_dtypes],
        scratch_shapes=scratch,
        compiler_params=_params(("parallel", "parallel", "arbitrary")),
    )(*a_list, *b_list, *extras, *after)


def _epi_plain(accs, extras):
    return (accs[0],)


def _epi_residual(accs, extras):
    return (accs[0] + extras[0],)


def _epi_two(accs, extras):
    return accs[0], accs[1]


def _epi_swiglu(accs, extras):
    g, u = accs
    return g, u, g * _sigmoid(g) * u


def _epi_swiglu_bwd(accs, extras):
    da = accs[0]
    g, u = (e.astype(F32) for e in extras)
    sg = _sigmoid(g)
    dg = da * u * sg * (1.0 + g * (1.0 - sg))
    du = da * g * sg
    return dg, du


_NT_DIMS = (((1,), (1,)), ((), ()))
_TN_DIMS = (((0,), (0,)), ((), ()))


def _tile_delta(tq, tk):
    return lax.broadcasted_iota(I32, (tq, tk), 0) - lax.broadcasted_iota(I32, (tq, tk), 1)


def _attn_log_count(delta):
    count = jnp.zeros(delta.shape, I32)
    for window, dilation in DILATED_PATTERNS:
        hit = ((delta & (dilation - 1)) == 0) & (delta <= window)
        count = count + jnp.where(hit, 1, 0)
    valid = (delta >= 0) & (count > 0)
    logm = jnp.where(count == 3, math.log(3.0), jnp.where(count == 2, math.log(2.0), 0.0))
    return jnp.where(valid, logm, NEG_BIG)


def _fill_attn_log_count(tab_ref):
    nb, t, _ = tab_ref.shape
    base = _tile_delta(t, t)
    for b in range(nb):
        tab_ref[b] = _attn_log_count(base + b * t)


def _fill_attn_bias(tab_ref, log_count_ref, slope):
    nb, t, _ = tab_ref.shape
    dist = _tile_delta(t, t).astype(F32)
    for b in range(nb):
        tab_ref[b] = log_count_ref[b] - slope * (dist + float(b * t))


def _fill_ret_decay(tab_ref, log_gamma):
    nb, t, _ = tab_ref.shape
    base = _tile_delta(t, t)
    for b in range(nb):
        tab_ref[b] = _ret_decay(base + b * t, log_gamma)


def _alibi_slopes():
    return [2.0 ** (-8.0 * (h + 1) / ATTN_HEADS) for h in range(ATTN_HEADS)]


def _attn_fwd(proj, after=()):
    s = proj.shape[0]
    t = SEQ_TILE
    hd = ATTN_HEAD_DIM
    hp = ATTN_FWD_HEADS_PER_STEP
    ng = ATTN_HEADS // hp
    w = hp * hd
    scale = 1.0 / math.sqrt(hd)
    slopes = _alibi_slopes()

    def body(q_ref, k_ref, v_ref, *rest):
        mix_ref, o_ref, lse_ref, kb, vb, bias_tab, log_count_tab = rest[len(after):]
        g = pl.program_id(0)
        i = pl.program_id(1)

        @pl.when((g == 0) & (i == 0))
        def _():
            _fill_attn_log_count(log_count_tab)

        @pl.when(i == 0)
        def _():
            kb[...] = k_ref[...].astype(BF16)
            vb[...] = v_ref[...].astype(BF16)
            for u in range(hp):
                _fill_attn_bias(bias_tab.at[u], log_count_tab, _select_by_index(g * hp + u, slopes))

        qs = [q_ref[:, u * hd:(u + 1) * hd].astype(BF16) for u in range(hp)]

        def step(j, carry):
            rows = pl.ds(pl.multiple_of(j * t, t), t)
            out = []
            for u in range(hp):
                m_i, l_i, acc = carry[u]
                lanes = slice(u * hd, (u + 1) * hd)
                sc = lax.dot_general(qs[u], kb[rows, lanes], _NT_DIMS, preferred_element_type=F32) * scale
                sc = sc + bias_tab[u, i - j]
                m_new = jnp.maximum(m_i, jnp.max(sc, axis=-1, keepdims=True))
                p = jnp.exp(sc - m_new)
                alpha = jnp.exp(m_i - m_new)
                l_new = alpha * l_i + jnp.sum(p, axis=-1, keepdims=True)
                acc = alpha * acc + jnp.dot(p.astype(BF16), vb[rows, lanes], preferred_element_type=F32)
                out.append((m_new, l_new, acc))
            return tuple(out)

        init = (jnp.full((t, 1), NEG_BIG, F32), jnp.zeros((t, 1), F32), jnp.zeros((t, hd), F32))
        final = lax.fori_loop(0, i + 1, step, (init,) * hp)
        for u in range(hp):
            m_i, l_i, acc = final[u]
            lanes = slice(u * hd, (u + 1) * hd)
            out = acc / l_i
            o_ref[:, lanes] = out
            mix_ref[:, lanes] = out.astype(BF16)
            lse_ref[:, lanes] = jnp.broadcast_to(m_i + jnp.log(l_i), (t, hd))

    return pl.pallas_call(
        body, name="attn_fwd", grid=(ng, s // t),
        in_specs=[pl.BlockSpec((t, w), lambda g, i: (i, g)),
                  pl.BlockSpec((s, w), lambda g, i: (0, ng + g)),
                  pl.BlockSpec((s, w), lambda g, i: (0, 2 * ng + g))] + [ANY] * len(after),
        out_specs=[pl.BlockSpec((None, t, w), lambda g, i: (0, i, g))] + [pl.BlockSpec((t, w), lambda g, i: (i, g))] * 2,
        out_shape=[jax.ShapeDtypeStruct((2, s, ATTN_WIDTH), BF16),
                   jax.ShapeDtypeStruct((s, ATTN_WIDTH), F32),
                   jax.ShapeDtypeStruct((s, ATTN_WIDTH), F32)],
        scratch_shapes=[pltpu.VMEM((s, w), BF16), pltpu.VMEM((s, w), BF16), pltpu.VMEM((hp, s // t, t, t), F32),
                        pltpu.VMEM((s // t, t, t), F32)],
        compiler_params=_params(("arbitrary", "arbitrary")),
    )(proj, proj, proj, *after)


def _attn_bwd(proj, attn_out, lse, dmixed, after=()):
    after = tuple(after)
    s = proj.shape[0]
    t = SEQ_TILE
    nt = s // t
    hd = ATTN_HEAD_DIM
    hp = ATTN_HEADS_PER_STEP
    ng = ATTN_HEADS // hp
    w = hp * hd
    scale = 1.0 / math.sqrt(hd)
    slopes = _alibi_slopes()

    def body(q_ref, k_ref, v_ref, o_ref, lse_ref, do_ref, *rest):
        dsec_ref, qb, kb, vb, dob, dsum, dq_acc, bias_tab, log_count_tab = rest[len(after):]
        g = pl.program_id(0)

        @pl.when(g == 0)
        def _():
            _fill_attn_log_count(log_count_tab)

        qb[...] = q_ref[...].astype(BF16)
        kb[...] = k_ref[...].astype(BF16)
        vb[...] = v_ref[...].astype(BF16)
        dob[...] = do_ref[...].astype(BF16)
        for u in range(hp):
            lanes = slice(u * hd, (u + 1) * hd)
            _fill_attn_bias(bias_tab.at[u], log_count_tab, _select_by_index(g * hp + u, slopes))
            rowsum = jnp.sum(do_ref[:, lanes] * o_ref[:, lanes], axis=-1, keepdims=True)
            dsum[:, lanes] = jnp.broadcast_to(rowsum, (s, hd))
        dq_acc[...] = jnp.zeros((s, w), F32)

        def over_keys(j, _):
            krows = pl.ds(pl.multiple_of(j * t, t), t)

            def over_queries(i, carry):
                qrows = pl.ds(pl.multiple_of(i * t, t), t)
                out = []
                for u in range(hp):
                    dk, dv = carry[u]
                    lanes = slice(u * hd, (u + 1) * hd)
                    qi, doi = qb[qrows, lanes], dob[qrows, lanes]
                    kj, vj = kb[krows, lanes], vb[krows, lanes]
                    lse_i = lse_ref[qrows, lanes][:, :1]
                    dsum_i = dsum[qrows, lanes][:, :1]
                    sc = lax.dot_general(qi, kj, _NT_DIMS, preferred_element_type=F32) * scale
                    p = jnp.exp(sc + bias_tab[u, i - j] - lse_i)
                    dp = lax.dot_general(doi, vj, _NT_DIMS, preferred_element_type=F32)
                    ds = (p * (dp - dsum_i)).astype(BF16)
                    dv = dv + lax.dot_general(p.astype(BF16), doi, _TN_DIMS, preferred_element_type=F32)
                    dk = dk + lax.dot_general(ds, qi, _TN_DIMS, preferred_element_type=F32)
                    dq_acc[qrows, lanes] += jnp.dot(ds, kj, preferred_element_type=F32)
                    out.append((dk, dv))
                return tuple(out)

            zero = jnp.zeros((t, hd), F32)
            final = lax.fori_loop(j, nt, over_queries, ((zero, zero),) * hp)
            for u in range(hp):
                lanes = slice(u * hd, (u + 1) * hd)
                dsec_ref[1, krows, lanes] = (final[u][0] * scale).astype(BF16)
                dsec_ref[2, krows, lanes] = final[u][1].astype(BF16)
            return 0

        lax.fori_loop(0, nt, over_keys, 0)
        dsec_ref[0] = (dq_acc[...] * scale).astype(BF16)

    def col(off):
        return pl.BlockSpec((s, w), lambda g: (0, off + g))

    return pl.pallas_call(
        body, name="attn_bwd", grid=(ng,),
        in_specs=[col(0), col(ng), col(2 * ng), col(0), col(0), col(0)] + [ANY] * len(after),
        out_specs=pl.BlockSpec((4, s, w), lambda g: (0, 0, g)),
        out_shape=jax.ShapeDtypeStruct((8, s, ATTN_WIDTH), BF16),
        scratch_shapes=[pltpu.VMEM((s, w), BF16)] * 4 + [pltpu.VMEM((s, w), F32)] * 2
        + [pltpu.VMEM((hp, nt, t, t), F32), pltpu.VMEM((nt, t, t), F32)],
        compiler_params=_params(("arbitrary",)),
    )(proj, proj, proj, attn_out, lse, dmixed, *after)


def _ret_log_gammas():
    return [math.log(1.0 - 2.0 ** (-5.0 - h)) for h in range(RET_HEADS)]


def _ret_decay(delta, log_gamma):
    dec = jnp.exp(delta.astype(F32) * log_gamma) * (1.0 / math.sqrt(RET_HEAD_DIM))
    return jnp.where(delta >= 0, dec, 0.0)


def _ret_fwd(proj, mixed, after=()):
    after = tuple(after)
    s = proj.shape[0]
    t = SEQ_TILE
    hd = RET_HEAD_DIM
    nh = RET_HEADS
    log_gammas = _ret_log_gammas()
    c0 = 3 * ATTN_WIDTH // hd

    def body(q_ref, k_ref, v_ref, g_ref, *rest):
        mix_ref, raw_ref, kb, vb, decay_tab = rest[1 + len(after):]
        h = pl.program_id(0)
        i = pl.program_id(1)

        @pl.when(i == 0)
        def _():
            kb[...] = k_ref[...].astype(BF16)
            vb[...] = v_ref[...].astype(BF16)
            _fill_ret_decay(decay_tab, _select_by_index(h, log_gammas))

        q = q_ref[...].astype(BF16)

        def step(j, acc):
            rows = pl.ds(pl.multiple_of(j * t, t), t)
            sc = lax.dot_general(q, kb[rows, :], _NT_DIMS, preferred_element_type=F32) * decay_tab[i - j]
            return acc + jnp.dot(sc.astype(BF16), vb[rows, :], preferred_element_type=F32)

        ret = lax.fori_loop(0, i + 1, step, jnp.zeros((t, hd), F32))
        raw_ref[...] = ret
        r = lax.rsqrt(jnp.mean(ret * ret, axis=-1, keepdims=True) + NORM_EPS)
        g = g_ref[...]
        mix_ref[...] = (g * _sigmoid(g) * (ret * r)).astype(BF16)

    return pl.pallas_call(
        body, name="ret_fwd", grid=(nh, s // t),
        in_specs=[pl.BlockSpec((t, hd), lambda h, i: (i, c0 + h)),
                  pl.BlockSpec((s, hd), lambda h, i: (0, c0 + nh + h)),
                  pl.BlockSpec((s, hd), lambda h, i: (0, c0 + 2 * nh + h)),
                  pl.BlockSpec((t, hd), lambda h, i: (i, c0 + 3 * nh + h))] + [ANY] * (1 + len(after)),
        out_specs=[pl.BlockSpec((None, t, hd), lambda h, i: (1, i, h)), pl.BlockSpec((t, hd), lambda h, i: (i, h))],
        out_shape=[jax.ShapeDtypeStruct(mixed.shape, BF16), jax.ShapeDtypeStruct((s, RET_WIDTH), F32)],
        input_output_aliases={4: 0},
        scratch_shapes=[pltpu.VMEM((s, hd), BF16), pltpu.VMEM((s, hd), BF16), pltpu.VMEM((s // t, t, t), F32)],
        compiler_params=_params(("arbitrary", "arbitrary")),
    )(proj, proj, proj, proj, mixed, *after)


def _ret_bwd(proj, ret_raw, dmixed, dsec, after=()):
    after = tuple(after)
    s = proj.shape[0]
    t = SEQ_TILE
    nt = s // t
    hd = RET_HEAD_DIM
    nh = RET_HEADS
    log_gammas = _ret_log_gammas()
    c0 = 3 * ATTN_WIDTH // hd
    mixed_blocks = ATTN_WIDTH // hd

    def body(q_ref, k_ref, v_ref, g_ref, raw_ref, dmix_ref, *rest):
        dsec_ref, qb, kb, vb, dretb, dq_acc, decay_tab = rest[1 + len(after):]
        h = pl.program_id(0)
        _fill_ret_decay(decay_tab, _select_by_index(h, log_gammas))
        qb[...] = q_ref[...].astype(BF16)
        kb[...] = k_ref[...].astype(BF16)
        vb[...] = v_ref[...].astype(BF16)
        ret = raw_ref[...]
        r = lax.rsqrt(jnp.mean(ret * ret, axis=-1, keepdims=True) + NORM_EPS)
        normed = ret * r
        g = g_ref[...]
        sg = _sigmoid(g)
        dout = dmix_ref[...]
        dsec_ref[3] = (dout * normed * sg * (1.0 + g * (1.0 - sg))).astype(BF16)
        dn = dout * g * sg
        dret = r * (dn - normed * jnp.mean(dn * normed, axis=-1, keepdims=True))
        dretb[...] = dret.astype(BF16)
        dq_acc[...] = jnp.zeros((s, hd), F32)

        def over_keys(j, _):
            krows = pl.ds(pl.multiple_of(j * t, t), t)
            kj = kb[krows, :]
            vj = vb[krows, :]

            def over_queries(i, carry):
                dk, dv = carry
                qrows = pl.ds(pl.multiple_of(i * t, t), t)
                qi = qb[qrows, :]
                doi = dretb[qrows, :]
                dec = decay_tab[i - j]
                a = (lax.dot_general(qi, kj, _NT_DIMS, preferred_element_type=F32) * dec).astype(BF16)
                da = (lax.dot_general(doi, vj, _NT_DIMS, preferred_element_type=F32) * dec).astype(BF16)
                dv = dv + lax.dot_general(a, doi, _TN_DIMS, preferred_element_type=F32)
                dk = dk + lax.dot_general(da, qi, _TN_DIMS, preferred_element_type=F32)
                dq_acc[qrows, :] += jnp.dot(da, kj, preferred_element_type=F32)
                return dk, dv

            zero = jnp.zeros((t, hd), F32)
            dk, dv = lax.fori_loop(j, nt, over_queries, (zero, zero))
            dsec_ref[1, krows, :] = dk.astype(BF16)
            dsec_ref[2, krows, :] = dv.astype(BF16)
            return 0

        lax.fori_loop(0, nt, over_keys, 0)
        dsec_ref[0] = dq_acc[...].astype(BF16)

    def col(off):
        return pl.BlockSpec((s, hd), lambda h: (0, off + h))

    return pl.pallas_call(
        body, name="ret_bwd", grid=(nh,),
        in_specs=[col(c0), col(c0 + nh), col(c0 + 2 * nh), col(c0 + 3 * nh), col(0), col(mixed_blocks)]
        + [ANY] * (1 + len(after)),
        out_specs=pl.BlockSpec((4, s, hd), lambda h: (1, 0, h)),
        out_shape=jax.ShapeDtypeStruct(dsec.shape, BF16),
        input_output_aliases={6: 0},
        scratch_shapes=[pltpu.VMEM((s, hd), BF16)] * 4 + [pltpu.VMEM((s, hd), F32)]
        + [pltpu.VMEM((nt, t, t), F32)],
        compiler_params=_params(("arbitrary",)),
    )(proj, proj, proj, proj, ret_raw, dmixed, dsec, *after)


_FLIPS = (2, 1, 3)


def _other_chips(x, y):
    return [(1 - x, y), (x, 1 - y), (1 - x, 1 - y)]


_HBM = pl.BlockSpec(memory_space=pltpu.HBM)
_SEM = pl.BlockSpec(memory_space=pltpu.SEMAPHORE)
_EFFECT = pltpu.SideEffectType.DATAFLOW_SIDE_EFFECTING


def _in_hbm(a):
    return pltpu.with_memory_space_constraint(a, pltpu.HBM)


def _weight_view(w, column_sharded):
    if column_sharded:
        return w.reshape(2, w.shape[0] // 2, w.shape[1])
    return w.reshape(N_CHIPS, 2, w.shape[0] // (2 * N_CHIPS), w.shape[1])


def _weight_unview(v):
    if v.ndim == 3:
        return v.reshape(2 * v.shape[1], v.shape[2])
    return v.reshape(N_CHIPS * 2 * v.shape[2], v.shape[3])


def _weight_region(buf, shard, half):
    if len(buf.shape) == 3:
        cols = buf.shape[2] // N_CHIPS
        return buf.at[half, :, pl.ds(shard * cols, cols)]
    return buf.at[shard, half]


def _remote(where, send_sem, recv_sem, to):
    return pltpu.make_async_remote_copy(src_ref=where, dst_ref=where, send_sem=send_sem, recv_sem=recv_sem,
                                        device_id=to, device_id_type=MESH)


def _for_my_shard(fn):
    x, y, _ = _place()
    for ss in range(N_CHIPS):
        pl.when(2 * x + y == ss)(functools.partial(fn, ss))


def _gather_forward(views, which, send_sems, recv_sems, after, name, base=0):
    n_w = len(views)
    which = [base // 3 + w for w in which] if base % 3 == 0 else None
    assert which is not None, "base must be a multiple of 3"

    def body(*refs):
        send_in, recv_in = refs[n_w:n_w + 2]
        fwd_send, fwd_recv = refs[n_w + 3:n_w + 5]
        bufs = refs[n_w + 5:]
        x, y, c = _place()
        sibling = (x, y, 1 - c)

        def forward(ss):
            for i, w in enumerate(which):
                for j in range(3):
                    landed = _weight_region(bufs[i], ss ^ _FLIPS[j], c)
                    _remote(landed, send_in.at[3 * w + j], recv_in.at[3 * w + j], sibling).wait_recv()
                    _remote(landed, fwd_send.at[3 * i + j], fwd_recv.at[3 * i + j], sibling).start()

        _for_my_shard(forward)
        for i, w in enumerate(which):
            for j in range(3):
                _remote(_weight_region(bufs[i], 0, 0), send_in.at[3 * w + j], recv_in.at[3 * w + j],
                        sibling).wait_send()

    return pl.pallas_call(
        body, name=name,
        in_specs=[_HBM] * n_w + [_SEM, _SEM, ANY], out_specs=[_SEM, _SEM] + [_HBM] * n_w,
        out_shape=[pltpu.SemaphoreType.DMA((3 * n_w,)), pltpu.SemaphoreType.DMA((3 * n_w,))]
        + [pltpu.HBM(v.shape, BF16) for v in views],
        input_output_aliases={w: 2 + w for w in range(n_w)},
        compiler_params=pltpu.CompilerParams(has_side_effects=_EFFECT),
    )(*views, send_sems, recv_sems, after)


def _gather_end(views, fwd_send, fwd_recv, after, name):
    n_w = len(views)

    def body(*refs):
        fwd_send_ref, fwd_recv_ref = refs[n_w:n_w + 2]
        bufs = refs[n_w + 3:]
        x, y, c = _place()
        for i in range(n_w):
            for j in range(3):
                cp = _remote(_weight_region(bufs[i], 0, 0), fwd_send_ref.at[3 * i + j], fwd_recv_ref.at[3 * i + j],
                             (x, y, 1 - c))
                cp.wait_recv()
                cp.wait_send()

    outs = pl.pallas_call(
        body, name=name,
        in_specs=[_HBM] * n_w + [_SEM, _SEM, ANY], out_specs=[_HBM] * n_w,
        out_shape=[pltpu.HBM(v.shape, BF16) for v in views],
        input_output_aliases={w: w for w in range(n_w)},
        compiler_params=pltpu.CompilerParams(has_side_effects=_EFFECT),
    )(*views, fwd_send, fwd_recv, after)
    return [_weight_unview(o) for o in outs]


def _comm_call(name, bufs, sem_pairs, after, n_new, fn):
    n, n_sem, after = len(bufs), 2 * len(sem_pairs), tuple(after)
    n_out_sem = 2 if n_new else 0

    def body(*refs):
        sems = refs[n:n + n_sem]
        outs = refs[n + n_sem + len(after):]
        new = outs[:n_out_sem] if n_new else (None, None)
        fn(outs[n_out_sem:], [(sems[2 * i], sems[2 * i + 1]) for i in range(len(sem_pairs))], *new)

    res = pl.pallas_call(
        body, name=name,
        in_specs=[_HBM] * n + [_SEM] * n_sem + [ANY] * len(after),
        out_specs=[_SEM] * n_out_sem + [_HBM] * n,
        out_shape=[pltpu.SemaphoreType.DMA((n_new,))] * n_out_sem + [pltpu.HBM(b.shape, b.dtype) for b in bufs],
        input_output_aliases={i: n_out_sem + i for i in range(n)},
        compiler_params=pltpu.CompilerParams(has_side_effects=_EFFECT),
    )(*bufs, *[s for pair in sem_pairs for s in pair], *after)
    return list(res[:n_out_sem]), list(res[n_out_sem:])


def _quarter(piece, q):
    rows = piece.shape[0] // 2
    return piece.at[pl.ds(q * rows, rows)]


def _gather_in_start(view, name):
    def fn(bufs, _, send, recv):
        x, y, c = _place()

        def go(ss):
            for j, chip in enumerate(_other_chips(x, y)[:2]):
                _remote(_weight_region(bufs[0], ss, c), send.at[j], recv.at[j], (*chip, c)).start()

        _for_my_shard(go)

    sems, (view,) = _comm_call(name, [_in_hbm(view)], [], (), 2, fn)
    return sems, view


def _gather_out_gate_start(v_out, v_gate, after, name):
    def fn(bufs, _, send, recv):
        x, y, c = _place()
        chips = _other_chips(x, y)

        def go(ss):
            for j in range(3):
                _remote(_weight_region(bufs[0], ss, c), send.at[j], recv.at[j], (*chips[j], c)).start()
            for j in range(2):
                _remote(_weight_region(bufs[1], ss, c), send.at[3 + j], recv.at[3 + j], (*chips[j], c)).start()

        _for_my_shard(go)

    sems, views = _comm_call(name, [_in_hbm(v_out), _in_hbm(v_gate)], [], after, 5, fn)
    return sems, views


def _gather_relay(view, started, base, after, name, then=None, then_peers=0):
    n_new = 6 + then_peers if then_peers else 4

    def fn(bufs, pairs, send, recv):
        (send_in, recv_in), = pairs
        x, y, c = _place()
        chips = _other_chips(x, y)
        sibling = (x, y, 1 - c)

        def go(ss):
            landed = [_weight_region(bufs[0], ss ^ _FLIPS[j], c) for j in range(2)]
            for j in range(2):
                _remote(landed[j], send_in.at[base + j], recv_in.at[base + j], sibling).wait_recv()
            for j in range(2):
                _remote(_quarter(landed[j], j), send.at[j], recv.at[j], (*chips[1 - j], c)).start()
            for j in range(2):
                _remote(landed[j], send.at[2 + j], recv.at[2 + j], sibling).start()
            for j in range(then_peers):
                _remote(_weight_region(bufs[1], ss, c), send.at[6 + j], recv.at[6 + j], (*chips[j], c)).start()

        _for_my_shard(go)
        for j in range(2):
            _remote(_weight_region(bufs[0], 0, 0), send_in.at[base + j], recv_in.at[base + j], sibling).wait_send()

    views = [view] if then is None else [view, _in_hbm(then)]
    sems, views = _comm_call(name, views, [started], after, n_new, fn)
    return sems, views


def _gather_in_neighbours_end(view, relayed, after, name):
    def fn(bufs, pairs, *_):
        (send, recv), = pairs
        x, y, c = _place()
        for j in range(2):
            cp = _remote(_weight_region(bufs[0], 0, 0), send.at[2 + j], recv.at[2 + j], (x, y, 1 - c))
            cp.wait_recv()
            cp.wait_send()

    _, (view,) = _comm_call(name, [view], [relayed], after, 0, fn)
    return view


def _gather_in_diagonal(view, relayed, after, name):
    def fn(bufs, pairs, send, recv):
        (send_in, recv_in), = pairs
        x, y, c = _place()
        sibling = (x, y, 1 - c)
        any_quarter = _quarter(_weight_region(bufs[0], 0, 0), 0)
        for j in range(2):
            cp = _remote(any_quarter, send_in.at[j], recv_in.at[j], sibling)
            cp.wait_recv()
            cp.wait_send()

        def go(ss):
            _remote(_weight_region(bufs[0], ss ^ _FLIPS[2], c), send.at[0], recv.at[0], sibling).start()

        _for_my_shard(go)

    sems, (view,) = _comm_call(name, [view], [relayed], after, 1, fn)
    return sems, view


def _gather_in_diagonal_end(view, forwarded, after, name):
    def fn(bufs, pairs, *_):
        (send, recv), = pairs
        x, y, c = _place()
        cp = _remote(_weight_region(bufs[0], 0, 0), send.at[0], recv.at[0], (x, y, 1 - c))
        cp.wait_recv()
        cp.wait_send()

    _, (view,) = _comm_call(name, [view], [forwarded], after, 0, fn)
    return view


def _in_proj_shard(h1, wi, proj, shard_arr, name):
    s, d = h1.shape
    n = wi.shape[1]
    tn = 256
    blocks = n // (N_CHIPS * tn)
    given = [] if proj is None else [proj]

    def body(shard_ref, h_ref, w_ref, *rest):
        del shard_ref
        rest[-1][...] = jnp.dot(h_ref[...], w_ref[...], preferred_element_type=F32)

    grid_spec = pltpu.PrefetchScalarGridSpec(
        num_scalar_prefetch=1, grid=(blocks,),
        in_specs=[pl.BlockSpec((s, d), lambda j, shard_ref: (0, 0)),
                  pl.BlockSpec((d, tn), lambda j, shard_ref: (0, shard_ref[0] * blocks + j))] + [ANY] * len(given),
        out_specs=pl.BlockSpec((s, tn), lambda j, shard_ref: (0, shard_ref[0] * blocks + j)))
    return pl.pallas_call(
        body, name=name, grid_spec=grid_spec,
        out_shape=jax.ShapeDtypeStruct((s, n), F32),
        input_output_aliases={3: 0} if given else {},
        compiler_params=_params(("arbitrary",)),
    )(shard_arr, h1, wi, *given)


def _split_start(name, bufs, n_sems, copies):
    n = len(bufs)

    def body(*refs):
        send_sems, recv_sems = refs[n:n + 2]
        for cp in copies(refs[n + 2:], send_sems, recv_sems):
            cp.start()

    outs = pl.pallas_call(
        body, name=name,
        in_specs=[_HBM] * n, out_specs=[_SEM, _SEM] + [_HBM] * n,
        out_shape=[pltpu.SemaphoreType.DMA((n_sems,)), pltpu.SemaphoreType.DMA((n_sems,))]
        + [pltpu.HBM(b.shape, b.dtype) for b in bufs],
        input_output_aliases={i: 2 + i for i in range(n)},
        compiler_params=pltpu.CompilerParams(has_side_effects=_EFFECT),
    )(*[_in_hbm(b) for b in bufs])
    return outs[0], outs[1], list(outs[2:])


def _split_wait(name, bufs, send_sems, recv_sems, copies, after):
    n = len(bufs)

    def body(*refs):
        send_ref, recv_ref = refs[n:n + 2]
        for cp in copies(refs[n + 3:], send_ref, recv_ref):
            cp.wait()

    return list(pl.pallas_call(
        body, name=name,
        in_specs=[_HBM] * n + [_SEM, _SEM, ANY], out_specs=[_HBM] * n,
        out_shape=[pltpu.HBM(b.shape, b.dtype) for b in bufs],
        input_output_aliases={i: i for i in range(n)},
        compiler_params=pltpu.CompilerParams(has_side_effects=_EFFECT),
    )(*bufs, send_sems, recv_sems, after))


def _halves_copies(n_w):
    def copies(bufs, send_sems, recv_sems):
        x, y, c = _place()
        out = []
        for w in range(n_w):
            view, land = bufs[w], bufs[n_w + w]
            src = view.at[1 - c] if len(view.shape) == 3 else view.at[:, 1 - c]
            out.append(pltpu.make_async_remote_copy(
                src_ref=src, dst_ref=land, send_sem=send_sems.at[w], recv_sem=recv_sems.at[w],
                device_id=(x, y, 1 - c), device_id_type=MESH))
        return out
    return copies


def _pieces_copies(n_w):
    def copies(bufs, send_sems, recv_sems):
        x, y, c = _place()
        out = []
        for w in range(n_w):
            for j, (cx, cy) in enumerate(_other_chips(x, y)):
                out.append(pltpu.make_async_remote_copy(
                    src_ref=bufs[w].at[2 * cx + cy], dst_ref=bufs[n_w + w].at[j],
                    send_sem=send_sems.at[3 * w + j], recv_sem=recv_sems.at[3 * w + j],
                    device_id=(cx, cy, c), device_id_type=MESH))
        return out
    return copies


def _join_copies(n_w):
    def copies(bufs, send_sems, recv_sems):
        x, y, c = _place()
        return [pltpu.make_async_remote_copy(
            src_ref=bufs[w].at[c], dst_ref=bufs[w].at[c], send_sem=send_sems.at[w], recv_sem=recv_sems.at[w],
            device_id=(x, y, 1 - c), device_id_type=MESH) for w in range(n_w)]
    return copies


def _halves_landing(view):
    shape = view.shape[1:] if view.ndim == 3 else (N_CHIPS,) + view.shape[2:]
    return lax.empty(shape, BF16)


def _halves_start(tag, grads, column_sharded):
    views = [_weight_view(g, cs) for g, cs in zip(grads, column_sharded)]
    n = len(views)
    return _split_start("halves_start_" + tag, views + [_halves_landing(v) for v in views], n, _halves_copies(n))


def _halves_wait(tag, state, after):
    send_sems, recv_sems, bufs = state
    n = len(bufs) // 2
    bufs = _split_wait("halves_wait_" + tag, bufs, send_sems, recv_sems, _halves_copies(n), after)
    return bufs[:n], bufs[n:]


def _pieces_start(tag, pieces):
    n = len(pieces)
    landing = [lax.empty((3,) + p.shape[1:], BF16) for p in pieces]
    return _split_start("pieces_start_" + tag, list(pieces) + landing, 3 * n, _pieces_copies(n))


def _pieces_wait(tag, state, after):
    send_sems, recv_sems, bufs = state
    n = len(bufs) // 2
    bufs = _split_wait("pieces_wait_" + tag, bufs, send_sems, recv_sems, _pieces_copies(n), after)
    return bufs[:n], bufs[n:]


def _join_start(tag, shards):
    n = len(shards)
    return _split_start("join_start_" + tag, list(shards), n, _join_copies(n))


def _join_wait(tag, state, after):
    send_sems, recv_sems, bufs = state
    bufs = _split_wait("join_wait_" + tag, bufs, send_sems, recv_sems, _join_copies(len(bufs)), after)
    return [b.reshape(2 * b.shape[1], b.shape[2]) for b in bufs]


def _chip_sum_col(g3, sib, c_arr, name):
    _, hk, n = g3.shape
    cols = n // N_CHIPS
    tr = _row_tile(hk, cols * 2, limit=4 * 1024 * 1024)

    def body(c_ref, g_ref, s_ref, o_ref):
        del c_ref
        o_ref[...] = (g_ref[...].astype(F32) + s_ref[...].astype(F32)).astype(BF16)

    grid_spec = pltpu.PrefetchScalarGridSpec(
        num_scalar_prefetch=1, grid=(N_CHIPS, hk // tr),
        in_specs=[pl.BlockSpec((None, tr, cols), lambda p, r, c_ref: (c_ref[0], r, p)),
                  pl.BlockSpec((tr, cols), lambda p, r, c_ref: (r, p))],
        out_specs=pl.BlockSpec((None, tr, cols), lambda p, r, c_ref: (p, r, 0)))
    return pl.pallas_call(
        body, name=name, grid_spec=grid_spec,
        out_shape=jax.ShapeDtypeStruct((N_CHIPS, hk, cols), BF16),
        compiler_params=_params(("parallel", "parallel")),
    )(c_arr, g3, sib)


def _chip_sum_row(g4, sib, c_arr, name):
    _, _, hr, n = g4.shape
    tr = _row_tile(hr, n * 2, limit=4 * 1024 * 1024)

    def body(c_ref, g_ref, s_ref, o_ref):
        del c_ref
        o_ref[...] = (g_ref[...].astype(F32) + s_ref[...].astype(F32)).astype(BF16)

    grid_spec = pltpu.PrefetchScalarGridSpec(
        num_scalar_prefetch=1, grid=(N_CHIPS, hr // tr),
        in_specs=[pl.BlockSpec((None, None, tr, n), lambda p, r, c_ref: (p, c_ref[0], r, 0)),
                  pl.BlockSpec((None, tr, n), lambda p, r, c_ref: (p, r, 0))],
        out_specs=pl.BlockSpec((None, tr, n), lambda p, r, c_ref: (p, r, 0)))
    return pl.pallas_call(
        body, name=name, grid_spec=grid_spec,
        out_shape=jax.ShapeDtypeStruct((N_CHIPS, hr, n), BF16),
        compiler_params=_params(("parallel", "parallel")),
    )(c_arr, g4, sib)


def _sum_pieces(pieces, received, place_arr, name):
    _, r, n = pieces.shape
    tr = _row_tile(r, n * 4, limit=4 * 1024 * 1024)

    def body(p_ref, own_ref, r0_ref, r1_ref, r2_ref, o_ref):
        del p_ref
        acc = own_ref[...].astype(F32) + r0_ref[...].astype(F32)
        acc = acc + r1_ref[...].astype(F32)
        o_ref[...] = acc + r2_ref[...].astype(F32)

    def recv_spec(j):
        return pl.BlockSpec((None, tr, n), lambda i, p_ref: (j, i, 0))

    grid_spec = pltpu.PrefetchScalarGridSpec(
        num_scalar_prefetch=1, grid=(r // tr,),
        in_specs=[pl.BlockSpec((None, tr, n), lambda i, p_ref: (p_ref[0], i, 0)),
                  recv_spec(0), recv_spec(1), recv_spec(2)],
        out_specs=pl.BlockSpec((None, tr, n), lambda i, p_ref: (p_ref[1], i, 0)))
    return pl.pallas_call(
        body, name=name, grid_spec=grid_spec,
        out_shape=jax.ShapeDtypeStruct((2, r, n), F32),
        compiler_params=_params(("parallel",)),
    )(place_arr, pieces, received, received, received)


def _norm_weights_step(parts, w, m, v, after=()):
    rows, d = parts.shape
    after = tuple(after)

    def body(p_ref, w_ref, m_ref, v_ref, *rest):
        g_ref, d_ref, mo_ref, vo_ref, gathered, send_sems, recv_sems = rest[len(after):]
        x, y, c = _place()
        me = 4 * x + 2 * y + c
        gathered[me] = p_ref[...]
        copies = []
        for k in range(1, N_DEV):
            peer = (x ^ ((k >> 2) & 1), y ^ ((k >> 1) & 1), c ^ (k & 1))
            copies.append(pltpu.make_async_remote_copy(
                src_ref=p_ref, dst_ref=gathered.at[me], send_sem=send_sems.at[k - 1],
                recv_sem=recv_sems.at[k - 1], device_id=peer, device_id_type=MESH))
        for cp in copies:
            cp.start()
        for cp in copies:
            cp.wait()
        g = gathered[0]
        for k in range(1, N_DEV):
            g = g + gathered[k]
        delta, m_new, v_new = _adamw_math(w_ref[...], g, m_ref[...], v_ref[...])
        g_ref[...] = g
        d_ref[...] = delta
        mo_ref[...] = m_new
        vo_ref[...] = v_new

    vmem = pl.BlockSpec(memory_space=pltpu.VMEM)
    shp = jax.ShapeDtypeStruct((rows, d), F32)
    return pl.pallas_call(
        body, name="norm_weights_step",
        in_specs=[vmem] * 4 + [ANY] * len(after), out_specs=[vmem] * 4, out_shape=[shp] * 4,
        scratch_shapes=[pltpu.VMEM((N_DEV, rows, d), F32), pltpu.SemaphoreType.DMA((N_DEV - 1,)),
                        pltpu.SemaphoreType.DMA((N_DEV - 1,))],
        compiler_params=pltpu.CompilerParams(has_side_effects=True),
    )(parts, w, m, v, *after)


def kernel(x, norm_mix_w, w_in, w_out, norm_ffn_w, w_gate, w_up, w_down, norm_final_w, loss_target, m_norm_mix_w, m_w_in, m_w_out, m_norm_ffn_w, m_w_gate, m_w_up, m_w_down, m_norm_final_w, v_norm_mix_w, v_w_in, v_w_out, v_norm_ffn_w, v_w_gate, v_w_up, v_w_down, v_norm_final_w):
    s, d = x.shape[1], x.shape[2]
    xs = x.reshape(s, d)
    target = loss_target.reshape(s, d)
    big = {"w_in": (w_in, m_w_in, v_w_in), "w_out": (w_out, m_w_out, v_w_out),
           "w_gate": (w_gate, m_w_gate, v_w_gate), "w_up": (w_up, m_w_up, v_w_up),
           "w_down": (w_down, m_w_down, v_w_down)}
    big = {k: tuple(a.reshape(a.shape[1:]) for a in t) for k, t in big.items()}
    col_names, row_names = ("w_in", "w_gate", "w_up"), ("w_out", "w_down")
    n_in = N_CHIPS * big["w_in"][0].shape[1]
    ffn = N_CHIPS * big["w_gate"][0].shape[1]
    mix = ATTN_WIDTH + RET_WIDTH
    c_arr = lax.axis_index("c").astype(I32).reshape(1)
    shard_arr = (2 * lax.axis_index("x") + lax.axis_index("y")).astype(I32).reshape(1)
    place_arr = jnp.concatenate([shard_arr, c_arr])

    def cast(k, after=()):
        return _weight_view(_cast_into_full(big[k][0], shard_arr, k in col_names, "cast_" + k, after), k in col_names)

    started_in, v_in = _gather_in_start(cast("w_in"), "gather_in_start")

    sec = ATTN_WIDTH

    def section(p, rows):
        return pl.BlockSpec((None, rows, sec), lambda i, j, kk: (p, i, 0))

    h1 = _rms_fwd(xs, norm_mix_w, "rms_mix_fwd", after=[v_in])
    my_shard = shard_arr[0]
    shard_of = [jnp.bitwise_xor(my_shard, f).astype(I32).reshape(1) for f in (0,) + _FLIPS]
    proj = _in_proj_shard(h1, _weight_unview(v_in), None, shard_of[0], "in_proj_own")
    early_views = [cast(k, after=[proj]) for k in ("w_out", "w_gate")]
    v_up, v_down = [cast(k, after=[proj]) for k in ("w_up", "w_down")]
    relayed_in, (v_in,) = _gather_relay(v_in, started_in, 0, early_views + [v_up, v_down], "gather_in_relay")
    started_og, (v_out, v_gate) = _gather_out_gate_start(*early_views, [v_in], "gather_out_gate_start")
    v_in = _gather_in_neighbours_end(v_in, relayed_in, [v_out], "gather_in_neighbours_end")
    proj = _in_proj_shard(h1, _weight_unview(v_in), proj, shard_of[1], "in_proj_x")
    proj = _in_proj_shard(h1, _weight_unview(v_in), proj, shard_of[2], "in_proj_y")
    forwarded_in, v_in = _gather_in_diagonal(v_in, relayed_in, [proj], "gather_in_diagonal")
    wi = _weight_unview(_gather_in_diagonal_end(v_in, forwarded_in, [proj], "gather_in_diagonal_end"))
    proj = _in_proj_shard(h1, wi, proj, shard_of[3], "in_proj_diagonal")
    fs_o, fr_o, v_out = _gather_forward([v_out], [0], *started_og, proj, "gather_forward_out")
    mixed, attn_o, lse = _attn_fwd(proj, after=[v_out])
    relayed_g, (v_gate, v_up) = _gather_relay(v_gate, started_og, 3, [attn_o], "gather_gate_relay",
                                              then=v_up, then_peers=2)
    mixed, ret_raw = _ret_fwd(proj, mixed, after=[v_gate])
    wo, = _gather_end([v_out], fs_o, fr_o, ret_raw, "gather_end_out")
    x1, = _matmul("out_proj", "nn", [mixed, mixed], [wo, wo], [0, 0], s, d, sec, s // 2, 512, sec, [xs], [F32],
                  _epi_residual, b_koff=[0, 1], a_specs=[section(0, s // 2), section(1, s // 2)])
    h2 = _rms_fwd(x1, norm_ffn_w, "rms_ffn_fwd")
    relayed_u, (v_up, v_down) = _gather_relay(v_up, relayed_g, 6, [h2], "gather_up_relay",
                                              then=v_down, then_peers=3)
    v_gate = _gather_in_neighbours_end(v_gate, relayed_g, [v_up], "gather_gate_neighbours_end")
    forwarded_g, v_gate = _gather_in_diagonal(v_gate, relayed_g, [v_up], "gather_gate_diagonal")
    v_up = _gather_in_neighbours_end(v_up, relayed_u, [v_gate], "gather_up_neighbours_end")
    wg = _weight_unview(_gather_in_diagonal_end(v_gate, forwarded_g, [v_up], "gather_gate_diagonal_end"))
    forwarded_u, v_up = _gather_in_diagonal(v_up, relayed_u, [wg], "gather_up_diagonal")
    wu = _weight_unview(_gather_in_diagonal_end(v_up, forwarded_u, [wg], "gather_up_diagonal_end"))
    gate, up, act = _matmul("gate_up", "nn", [h2, h2], [wg, wu], [0, 1], s, ffn, d, s, 512, d, [],
                            [BF16, BF16, BF16], _epi_swiglu, a_single_buffer=True)
    fs, fr, v_down = _gather_forward([v_down], [0], *relayed_u, act, "gather_forward_down", base=6)
    wd, = _gather_end([v_down], fs, fr, act, "gather_end_down")
    x2, = _matmul("down_proj", "nn", [act], [wd], [0], s, d, ffn, s // 2, 512, ffn, [x1], [F32],
                  _epi_residual)
    loss_row, dx2, dx2b, dwf = _final_norm_loss(x2, norm_final_w.reshape(1, d), target, "final_norm_loss")

    names = col_names + row_names
    grads, new = {}, {}

    def chip_sums(tag_names, views, sibs):
        return [(_chip_sum_col if k in col_names else _chip_sum_row)(v, sb, c_arr, "chip_sum_" + k)
                for k, v, sb in zip(tag_names, views, sibs)]

    def piece_sums(tag_names, pieces, received):
        return [_sum_pieces(p, r, place_arr, "sum_pieces_" + k) for k, p, r in zip(tag_names, pieces, received)]

    def update(k):
        new[k] = _adamw(big[k][0], grads[k], big[k][1], big[k][2], "adamw_" + k)

    dgate, dup = _matmul("d_act", "nt", [dx2b], [wd], [0], s, ffn, d, s, 512, d, [gate, up],
                         [BF16, BF16], _epi_swiglu_bwd, a_single_buffer=True)
    g_wd, = _matmul("g_w_down", "tn", [act], [dx2b], [0], ffn, d, s, 512, d, s, [], [BF16], _epi_plain)
    halves_d = _halves_start("down", [g_wd], [False])
    dh2, = _matmul("d_h2", "nt", [dgate, dup], [wg, wu], [0, 0], s, d, ffn, s // 2, 256, ffn, [], [F32],
                   _epi_plain, after=halves_d[2][-1:], a_single_buffer=True)
    pieces_d = _pieces_start("down", chip_sums(["w_down"], *_halves_wait("down", halves_d, dh2)))
    g_wg, g_wu = _matmul("g_w_gate_up", "tn", [h2, h2], [dgate, dup], [0, 1], d, ffn, s, 1024, 512, s, [],
                         [BF16, BF16], _epi_two, after=pieces_d[2][-1:])
    halves_gu = _halves_start("gate_up", [g_wg, g_wu], [True, True])
    dx1, dx1b, dw_ffn = _rms_bwd(x1, norm_ffn_w, dh2, dx2, "rms_ffn_bwd", after=halves_gu[2][-1:])

    dmixed, = _matmul("d_mixed", "nt", [dx1b], [wo], [0], s, mix, d, s // 2, 512, d, [], [F32], _epi_plain)
    pieces_gu = _pieces_start("gate_up", chip_sums(["w_gate", "w_up"], *_halves_wait("gate_up", halves_gu, dmixed)))
    per = sec // 512
    g_wo, = _matmul("g_w_out", "tn", [mixed], [dx1b], [0], mix, d, s, 512, d, s, [], [BF16], _epi_plain,
                    after=pieces_gu[2][-1:],
                    a_specs=[pl.BlockSpec((None, s, 512), lambda i, j, kk: (i // per, 0, i % per))])
    halves_o = _halves_start("out", [g_wo], [False])
    dsec = _attn_bwd(proj, attn_o, lse, dmixed, after=halves_o[2][-1:])
    pieces_o = _pieces_start("out", chip_sums(["w_out"], *_halves_wait("out", halves_o, dsec)))
    dsec = _ret_bwd(proj, ret_raw, dmixed, dsec, after=pieces_o[2][-1:])
    where = [0, 1, 2, 4, 5, 6, 7]
    n_sec = len(where)
    g_wi, = _matmul("g_w_in", "tn", [h1], [dsec], [0], d, n_in, s, 1024, sec, s, [], [BF16], _epi_plain,
                    b_specs=[pl.BlockSpec((None, s, sec), lambda i, j, kk: (j + (j >= 3).astype(I32), 0, 0))])
    halves_i = _halves_start("in", [g_wi], [True])
    dh1, = _matmul("d_h1", "nt", [dsec] * n_sec, [wi] * n_sec, [0] * n_sec, s, d, sec, s // 2, 256, sec, [], [F32],
                   _epi_plain, b_koff=list(range(n_sec)), after=halves_i[2][-1:],
                   a_specs=[section(p, s // 2) for p in where])
    pieces_i = _pieces_start("in", chip_sums(["w_in"], *_halves_wait("in", halves_i, dh1)))
    grad_x, _, dw_mix = _rms_bwd(xs, norm_mix_w, dh1, dx1, "rms_mix_bwd", after=pieces_i[2][-1:])

    def rows8(*vs):
        return jnp.concatenate([v.reshape(1, d) for v in vs] + [jnp.zeros((8 - len(vs), d), F32)], axis=0)

    join_d = _join_start("down", piece_sums(["w_down"], *_pieces_wait("down", pieces_d, grad_x)))
    join_gu = _join_start("gate_up", piece_sums(["w_gate", "w_up"], *_pieces_wait("gate_up", pieces_gu, join_d[2][0])))
    join_o = _join_start("out", piece_sums(["w_out"], *_pieces_wait("out", pieces_o, join_gu[2][0])))
    grads["w_down"], = _join_wait("down", join_d, join_o[2][0])
    update("w_down")
    grads["w_gate"], grads["w_up"] = _join_wait("gate_up", join_gu, new["w_down"][0])
    update("w_gate")
    update("w_up")
    grads["w_out"], = _join_wait("out", join_o, new["w_up"][0])
    update("w_out")
    join_i = _join_start("in", piece_sums(["w_in"], *_pieces_wait("in", pieces_i, new["w_out"][0])))
    ng, nd, nm, nv = _norm_weights_step(
        rows8(dw_mix, dw_ffn, dwf, jnp.broadcast_to(loss_row[:, :1], (1, d))),
        rows8(norm_mix_w, norm_ffn_w, norm_final_w),
        rows8(m_norm_mix_w, m_norm_ffn_w, m_norm_final_w), rows8(v_norm_mix_w, v_norm_ffn_w, v_norm_final_w),
        after=join_i[2][:1])
    grads["w_in"], = _join_wait("in", join_i, ng)
    update("w_in")

    loss = ng[3, 0]

    def pack(small, per_weight):
        lead = lambda a: a.reshape((1,) + a.shape)
        return (small[0:1], lead(per_weight["w_in"]), lead(per_weight["w_out"]), small[1:2],
                lead(per_weight["w_gate"]), lead(per_weight["w_up"]), lead(per_weight["w_down"]), small[2])

    return (loss, grad_x.reshape(1, s, d),
            *pack(ng, {k: new[k][3] for k in names}),
            *pack(nd, {k: new[k][0] for k in names}),
            *pack(nm, {k: new[k][1] for k in names}),
            *pack(nv, {k: new[k][2] for k in names}))
```

```python
import functools
import math

import jax
import jax.numpy as jnp
from jax import lax
from jax.experimental import pallas as pl
from jax.experimental.pallas import tpu as pltpu

F32 = jnp.float32
BF16 = jnp.bfloat16
I32 = jnp.int32
MESH = pl.DeviceIdType.MESH
ANY = pl.BlockSpec(memory_space=pl.ANY)

ATTN_HEADS = 8
ATTN_HEAD_DIM = 128
RET_HEADS = 4
RET_HEAD_DIM = 256
ATTN_WIDTH = ATTN_HEADS * ATTN_HEAD_DIM
RET_WIDTH = RET_HEADS * RET_HEAD_DIM
DILATED_PATTERNS = ((128, 1), (512, 4), (2048, 16))
NORM_EPS = 1e-6
ADAM_LR = 0.001
ADAM_B1 = 0.9
ADAM_B2 = 0.999
ADAM_EPS = 1e-08
ADAM_WD = 0.01
ADAM_STEP = 10

N_CHIPS = 4
N_DEV = 8
NEG_BIG = -1e30
SEQ_TILE = 512
ATTN_FWD_HEADS_PER_STEP = 2
ATTN_HEADS_PER_STEP = 1
VMEM_LIMIT_BYTES = 56 * 1024 * 1024


def _params(semantics=None, vmem=VMEM_LIMIT_BYTES):
    return pltpu.CompilerParams(dimension_semantics=semantics, vmem_limit_bytes=vmem)


def _row_tile(rows, row_bytes, limit=2 * 1024 * 1024, mult=16):
    best = None
    for t in range(mult, rows + 1, mult):
        if rows % t == 0 and t * row_bytes <= limit:
            best = t
    assert best is not None, (rows, row_bytes)
    return best


def _sigmoid(x):
    return 1.0 / (1.0 + jnp.exp(-x))


def _select_by_index(idx, values):
    out = jnp.float32(values[-1])
    for i in range(len(values) - 2, -1, -1):
        out = jnp.where(idx == i, jnp.float32(values[i]), out)
    return out


def _place():
    x, y, c = lax.axis_index("x"), lax.axis_index("y"), lax.axis_index("c")
    return x, y, c


def _cast_into_full(w, shard_arr, column_sharded, name, after=()):
    after = tuple(after)
    rows, cols = w.shape
    tr = _row_tile(rows, cols * 4)
    steps = rows // tr
    if column_sharded:
        out_shape, out_map = (rows, N_CHIPS * cols), (lambda i, s_ref: (i, s_ref[0]))
    else:
        out_shape, out_map = (N_CHIPS * rows, cols), (lambda i, s_ref: (s_ref[0] * steps + i, 0))

    def body(s_ref, w_ref, *rest):
        del s_ref
        rest[-1][...] = w_ref[...].astype(BF16)

    grid_spec = pltpu.PrefetchScalarGridSpec(
        num_scalar_prefetch=1, grid=(steps,),
        in_specs=[pl.BlockSpec((tr, cols), lambda i, s_ref: (i, 0))] + [ANY] * len(after),
        out_specs=pl.BlockSpec((tr, cols), out_map))
    return pl.pallas_call(
        body, name=name, grid_spec=grid_spec,
        out_shape=jax.ShapeDtypeStruct(out_shape, BF16),
        compiler_params=_params(("parallel",)),
    )(shard_arr, w, *after)


def _rms_fwd(x, w, name, after=()):
    rows, d = x.shape
    tr = 256
    after = tuple(after)

    def body(x_ref, w_ref, *rest):
        xv = x_ref[...]
        r = lax.rsqrt(jnp.mean(xv * xv, axis=-1, keepdims=True) + NORM_EPS)
        rest[-1][...] = (xv * r * w_ref[...]).astype(BF16)

    return pl.pallas_call(
        body, name=name, grid=(rows // tr,),
        in_specs=[pl.BlockSpec((tr, d), lambda i: (i, 0)), pl.BlockSpec((1, d), lambda i: (0, 0))]
        + [ANY] * len(after),
        out_specs=pl.BlockSpec((tr, d), lambda i: (i, 0)),
        out_shape=jax.ShapeDtypeStruct((rows, d), BF16),
        compiler_params=_params(("parallel",)),
    )(x, w, *after)


def _rms_bwd(x, w, dh, dres, name, after=()):
    rows, d = x.shape
    tr = 256
    after = tuple(after)

    def body(x_ref, w_ref, dh_ref, dres_ref, *rest):
        dx_ref, dxb_ref, dw_ref = rest[len(after):]
        xv = x_ref[...]
        r = lax.rsqrt(jnp.mean(xv * xv, axis=-1, keepdims=True) + NORM_EPS)
        xhat = xv * r
        dy = dh_ref[...]
        dxhat = dy * w_ref[...]
        dx = dres_ref[...] + r * (dxhat - xhat * jnp.mean(dxhat * xhat, axis=-1, keepdims=True))
        dx_ref[...] = dx
        dxb_ref[...] = dx.astype(BF16)
        part = jnp.sum(dy * xhat, axis=0, keepdims=True)

        @pl.when(pl.program_id(0) == 0)
        def _():
            dw_ref[...] = part

        @pl.when(pl.program_id(0) != 0)
        def _():
            dw_ref[...] += part

    row = pl.BlockSpec((tr, d), lambda i: (i, 0))
    vec = pl.BlockSpec((1, d), lambda i: (0, 0))
    return pl.pallas_call(
        body, name=name, grid=(rows // tr,),
        in_specs=[row, vec, row, row] + [ANY] * len(after),
        out_specs=[row, row, vec],
        out_shape=[jax.ShapeDtypeStruct((rows, d), F32), jax.ShapeDtypeStruct((rows, d), BF16),
                   jax.ShapeDtypeStruct((1, d), F32)],
        compiler_params=_params(("arbitrary",)),
    )(x, w, dh, dres, *after)


def _final_norm_loss(x2, w, target, name):
    rows, d = x2.shape
    tr = 256

    def body(x_ref, w_ref, t_ref, loss_ref, dx_ref, dxb_ref, dw_ref):
        xv = x_ref[...]
        wv = w_ref[...]
        r = lax.rsqrt(jnp.mean(xv * xv, axis=-1, keepdims=True) + NORM_EPS)
        xhat = xv * r
        err = xhat * wv - t_ref[...]
        part_loss = 0.5 * jnp.sum(jnp.mean(err * err, axis=-1, keepdims=True), axis=0, keepdims=True)
        dy = err * (1.0 / d)
        dxhat = dy * wv
        dx = r * (dxhat - xhat * jnp.mean(dxhat * xhat, axis=-1, keepdims=True))
        dx_ref[...] = dx
        dxb_ref[...] = dx.astype(BF16)
        part_dw = jnp.sum(dy * xhat, axis=0, keepdims=True)
        part_loss = jnp.broadcast_to(part_loss, (1, 128))

        @pl.when(pl.program_id(0) == 0)
        def _():
            dw_ref[...] = part_dw
            loss_ref[...] = part_loss

        @pl.when(pl.program_id(0) != 0)
        def _():
            dw_ref[...] += part_dw
            loss_ref[...] += part_loss

    row = pl.BlockSpec((tr, d), lambda i: (i, 0))
    vec = pl.BlockSpec((1, d), lambda i: (0, 0))
    return pl.pallas_call(
        body, name=name, grid=(rows // tr,),
        in_specs=[row, vec, row],
        out_specs=[pl.BlockSpec((1, 128), lambda i: (0, 0)), row, row, vec],
        out_shape=[jax.ShapeDtypeStruct((1, 128), F32), jax.ShapeDtypeStruct((rows, d), F32),
                   jax.ShapeDtypeStruct((rows, d), BF16), jax.ShapeDtypeStruct((1, d), F32)],
        compiler_params=_params(("arbitrary",)),
    )(x2, w, target)


def _adamw_math(w, g, m, v):
    m = ADAM_B1 * m + (1.0 - ADAM_B1) * g
    v = ADAM_B2 * v + (1.0 - ADAM_B2) * (g * g)
    m_hat = m / (1.0 - ADAM_B1 ** ADAM_STEP)
    v_hat = v / (1.0 - ADAM_B2 ** ADAM_STEP)
    delta = -ADAM_LR * (m_hat / (jnp.sqrt(v_hat) + ADAM_EPS) + ADAM_WD * w)
    return delta, m, v


def _adamw(w, g, m, v, name):
    rows, cols = w.shape
    tr = _row_tile(rows, cols * 4)

    def body(w_ref, g_ref, m_ref, v_ref, d_ref, mo_ref, vo_ref, go_ref):
        g = g_ref[...]
        delta, m_new, v_new = _adamw_math(w_ref[...], g, m_ref[...], v_ref[...])
        d_ref[...] = delta
        mo_ref[...] = m_new
        vo_ref[...] = v_new
        go_ref[...] = g

    blk = pl.BlockSpec((tr, cols), lambda i: (i, 0))
    shp = jax.ShapeDtypeStruct((rows, cols), F32)
    return pl.pallas_call(
        body, name=name, grid=(rows // tr,),
        in_specs=[blk] * 4, out_specs=[blk] * 4, out_shape=[shp] * 4,
        compiler_params=_params(("parallel",)),
    )(w, g, m, v)


_DOT_DIMS = {"nn": ((1,), (0,)), "nt": ((1,), (1,)), "tn": ((0,), (0,))}


def _matmul(name, mode, a_list, b_list, acc_of, m, n, k, tm, tn, tk, extras, out_dtypes, epilogue,
            a_koff=None, b_koff=None, after=(), a_specs=None, b_specs=None, a_single_buffer=False):
    after = tuple(after)
    assert m % tm == 0 and n % tn == 0 and k % tk == 0, (name, m, n, k, tm, tn, tk)
    nk = k // tk
    n_acc = max(acc_of) + 1
    n_pairs = len(a_list)
    a_koff = a_koff or [0] * n_pairs
    b_koff = b_koff or [0] * n_pairs
    dims = (_DOT_DIMS[mode], ((), ()))
    n_ext, n_out = len(extras), len(out_dtypes)

    def body(*refs):
        a_refs = refs[:n_pairs]
        b_refs = refs[n_pairs:2 * n_pairs]
        e_refs = refs[2 * n_pairs:2 * n_pairs + n_ext]
        first_out = 2 * n_pairs + n_ext + len(after)
        o_refs = refs[first_out:first_out + n_out]
        acc_refs = refs[first_out + n_out:]

        parts = [None] * n_acc
        for p in range(n_pairs):
            d = lax.dot_general(a_refs[p][...], b_refs[p][...], dims, preferred_element_type=F32)
            parts[acc_of[p]] = d if parts[acc_of[p]] is None else parts[acc_of[p]] + d

        def finish(accs):
            outs = epilogue(accs, [e[...] for e in e_refs])
            for o_ref, o in zip(o_refs, outs):
                o_ref[...] = o.astype(o_ref.dtype)

        if nk == 1:
            finish(parts)
        else:
            kk = pl.program_id(2)

            @pl.when(kk == 0)
            def _():
                for acc_ref, part in zip(acc_refs, parts):
                    acc_ref[...] = part

            @pl.when(kk != 0)
            def _():
                for acc_ref, part in zip(acc_refs, parts):
                    acc_ref[...] += part

            @pl.when(kk == nk - 1)
            def _():
                finish([acc_ref[...] for acc_ref in acc_refs])

    def a_spec(off):
        mode_a = pl.Buffered(1) if a_single_buffer else None
        if mode == "tn":
            return pl.BlockSpec((tk, tm), lambda i, j, kk: (kk + off, i), pipeline_mode=mode_a)
        return pl.BlockSpec((tm, tk), lambda i, j, kk: (i, kk + off), pipeline_mode=mode_a)

    def b_spec(off):
        if mode == "nt":
            return pl.BlockSpec((tn, tk), lambda i, j, kk: (j, kk + off))
        return pl.BlockSpec((tk, tn), lambda i, j, kk: (kk + off, j))

    tile = pl.BlockSpec((tm, tn), lambda i, j, kk: (i, j))
    scratch = [pltpu.VMEM((tm, tn), F32) for _ in range(n_acc)] if nk > 1 else []
    return pl.pallas_call(
        body, name=name, grid=(m // tm, n // tn, nk),
        in_specs=(a_specs or [a_spec(o) for o in a_koff]) + (b_specs or [b_spec(o) for o in b_koff])
        + [tile] * n_ext + [ANY] * len(after),
        out_specs=[tile] * n_out,
        out_shape=[jax.ShapeDtypeStruct((m, n), dt) for dt in out_dtypes],
        scratch_shapes=scratch,
        compiler_params=_params(("parallel", "parallel", "arbitrary")),
    )(*a_list, *b_list, *extras, *after)


def _epi_plain(accs, extras):
    return (accs[0],)


def _epi_residual(accs, extras):
    return (accs[0] + extras[0],)


def _epi_two(accs, extras):
    return accs[0], accs[1]


def _epi_swiglu(accs, extras):
    g, u = accs
    return g, u, g * _sigmoid(g) * u


def _epi_swiglu_bwd(accs, extras):
    da = accs[0]
    g, u = (e.astype(F32) for e in extras)
    sg = _sigmoid(g)
    dg = da * u * sg * (1.0 + g * (1.0 - sg))
    du = da * g * sg
    return dg, du


_NT_DIMS = (((1,), (1,)), ((), ()))
_TN_DIMS = (((0,), (0,)), ((), ()))


def _tile_delta(tq, tk):
    return lax.broadcasted_iota(I32, (tq, tk), 0) - lax.broadcasted_iota(I32, (tq, tk), 1)


def _attn_log_count(delta):
    count = jnp.zeros(delta.shape, I32)
    for window, dilation in DILATED_PATTERNS:
        hit = ((delta & (dilation - 1)) == 0) & (delta <= window)
        count = count + jnp.where(hit, 1, 0)
    valid = (delta >= 0) & (count > 0)
    logm = jnp.where(count == 3, math.log(3.0), jnp.where(count == 2, math.log(2.0), 0.0))
    return jnp.where(valid, logm, NEG_BIG)


def _fill_attn_log_count(tab_ref):
    nb, t, _ = tab_ref.shape
    base = _tile_delta(t, t)
    for b in range(nb):
        tab_ref[b] = _attn_log_count(base + b * t)


def _fill_attn_bias(tab_ref, log_count_ref, slope):
    nb, t, _ = tab_ref.shape
    dist = _tile_delta(t, t).astype(F32)
    for b in range(nb):
        tab_ref[b] = log_count_ref[b] - slope * (dist + float(b * t))


def _fill_ret_decay(tab_ref, log_gamma):
    nb, t, _ = tab_ref.shape
    base = _tile_delta(t, t)
    for b in range(nb):
        tab_ref[b] = _ret_decay(base + b * t, log_gamma)


def _alibi_slopes():
    return [2.0 ** (-8.0 * (h + 1) / ATTN_HEADS) for h in range(ATTN_HEADS)]


def _attn_fwd(proj, after=()):
    s = proj.shape[0]
    t = SEQ_TILE
    hd = ATTN_HEAD_DIM
    hp = ATTN_FWD_HEADS_PER_STEP
    ng = ATTN_HEADS // hp
    w = hp * hd
    scale = 1.0 / math.sqrt(hd)
    slopes = _alibi_slopes()

    def body(q_ref, k_ref, v_ref, *rest):
        mix_ref, o_ref, lse_ref, kb, vb, bias_tab, log_count_tab = rest[len(after):]
        g = pl.program_id(0)
        i = pl.program_id(1)

        @pl.when((g == 0) & (i == 0))
        def _():
            _fill_attn_log_count(log_count_tab)

        @pl.when(i == 0)
        def _():
            kb[...] = k_ref[...].astype(BF16)
            vb[...] = v_ref[...].astype(BF16)
            for u in range(hp):
                _fill_attn_bias(bias_tab.at[u], log_count_tab, _select_by_index(g * hp + u, slopes))

        qs = [q_ref[:, u * hd:(u + 1) * hd].astype(BF16) for u in range(hp)]

        def step(j, carry):
            rows = pl.ds(pl.multiple_of(j * t, t), t)
            out = []
            for u in range(hp):
                m_i, l_i, acc = carry[u]
                lanes = slice(u * hd, (u + 1) * hd)
                sc = lax.dot_general(qs[u], kb[rows, lanes], _NT_DIMS, preferred_element_type=F32) * scale
                sc = sc + bias_tab[u, i - j]
                m_new = jnp.maximum(m_i, jnp.max(sc, axis=-1, keepdims=True))
                p = jnp.exp(sc - m_new)
                alpha = jnp.exp(m_i - m_new)
                l_new = alpha * l_i + jnp.sum(p, axis=-1, keepdims=True)
                acc = alpha * acc + jnp.dot(p.astype(BF16), vb[rows, lanes], preferred_element_type=F32)
                out.append((m_new, l_new, acc))
            return tuple(out)

        init = (jnp.full((t, 1), NEG_BIG, F32), jnp.zeros((t, 1), F32), jnp.zeros((t, hd), F32))
        final = lax.fori_loop(0, i + 1, step, (init,) * hp)
        for u in range(hp):
            m_i, l_i, acc = final[u]
            lanes = slice(u * hd, (u + 1) * hd)
            out = acc / l_i
            o_ref[:, lanes] = out
            mix_ref[:, lanes] = out.astype(BF16)
            lse_ref[:, lanes] = jnp.broadcast_to(m_i + jnp.log(l_i), (t, hd))

    return pl.pallas_call(
        body, name="attn_fwd", grid=(ng, s // t),
        in_specs=[pl.BlockSpec((t, w), lambda g, i: (i, g)),
                  pl.BlockSpec((s, w), lambda g, i: (0, ng + g)),
                  pl.BlockSpec((s, w), lambda g, i: (0, 2 * ng + g))] + [ANY] * len(after),
        out_specs=[pl.BlockSpec((None, t, w), lambda g, i: (0, i, g))] + [pl.BlockSpec((t, w), lambda g, i: (i, g))] * 2,
        out_shape=[jax.ShapeDtypeStruct((2, s, ATTN_WIDTH), BF16),
                   jax.ShapeDtypeStruct((s, ATTN_WIDTH), F32),
                   jax.ShapeDtypeStruct((s, ATTN_WIDTH), F32)],
        scratch_shapes=[pltpu.VMEM((s, w), BF16), pltpu.VMEM((s, w), BF16), pltpu.VMEM((hp, s // t, t, t), F32),
                        pltpu.VMEM((s // t, t, t), F32)],
        compiler_params=_params(("arbitrary", "arbitrary")),
    )(proj, proj, proj, *after)


def _attn_bwd(proj, attn_out, lse, dmixed, after=()):
    after = tuple(after)
    s = proj.shape[0]
    t = SEQ_TILE
    nt = s // t
    hd = ATTN_HEAD_DIM
    hp = ATTN_HEADS_PER_STEP
    ng = ATTN_HEADS // hp
    w = hp * hd
    scale = 1.0 / math.sqrt(hd)
    slopes = _alibi_slopes()

    def body(q_ref, k_ref, v_ref, o_ref, lse_ref, do_ref, *rest):
        dsec_ref, qb, kb, vb, dob, dsum, dq_acc, bias_tab, log_count_tab = rest[len(after):]
        g = pl.program_id(0)

        @pl.when(g == 0)
        def _():
            _fill_attn_log_count(log_count_tab)

        qb[...] = q_ref[...].astype(BF16)
        kb[...] = k_ref[...].astype(BF16)
        vb[...] = v_ref[...].astype(BF16)
        dob[...] = do_ref[...].astype(BF16)
        for u in range(hp):
            lanes = slice(u * hd, (u + 1) * hd)
            _fill_attn_bias(bias_tab.at[u], log_count_tab, _select_by_index(g * hp + u, slopes))
            rowsum = jnp.sum(do_ref[:, lanes] * o_ref[:, lanes], axis=-1, keepdims=True)
            dsum[:, lanes] = jnp.broadcast_to(rowsum, (s, hd))
        dq_acc[...] = jnp.zeros((s, w), F32)

        def over_keys(j, _):
            krows = pl.ds(pl.multiple_of(j * t, t), t)

            def over_queries(i, carry):
                qrows = pl.ds(pl.multiple_of(i * t, t), t)
                out = []
                for u in range(hp):
                    dk, dv = carry[u]
                    lanes = slice(u * hd, (u + 1) * hd)
                    qi, doi = qb[qrows, lanes], dob[qrows, lanes]
                    kj, vj = kb[krows, lanes], vb[krows, lanes]
                    lse_i = lse_ref[qrows, lanes][:, :1]
                    dsum_i = dsum[qrows, lanes][:, :1]
                    sc = lax.dot_general(qi, kj, _NT_DIMS, preferred_element_type=F32) * scale
                    p = jnp.exp(sc + bias_tab[u, i - j] - lse_i)
                    dp = lax.dot_general(doi, vj, _NT_DIMS, preferred_element_type=F32)
                    ds = (p * (dp - dsum_i)).astype(BF16)
                    dv = dv + lax.dot_general(p.astype(BF16), doi, _TN_DIMS, preferred_element_type=F32)
                    dk = dk + lax.dot_general(ds, qi, _TN_DIMS, preferred_element_type=F32)
                    dq_acc[qrows, lanes] += jnp.dot(ds, kj, preferred_element_type=F32)
                    out.append((dk, dv))
                return tuple(out)

            zero = jnp.zeros((t, hd), F32)
            final = lax.fori_loop(j, nt, over_queries, ((zero, zero),) * hp)
            for u in range(hp):
                lanes = slice(u * hd, (u + 1) * hd)
                dsec_ref[1, krows, lanes] = (final[u][0] * scale).astype(BF16)
                dsec_ref[2, krows, lanes] = final[u][1].astype(BF16)
            return 0

        lax.fori_loop(0, nt, over_keys, 0)
        dsec_ref[0] = (dq_acc[...] * scale).astype(BF16)

    def col(off):
        return pl.BlockSpec((s, w), lambda g: (0, off + g))

    return pl.pallas_call(
        body, name="attn_bwd", grid=(ng,),
        in_specs=[col(0), col(ng), col(2 * ng), col(0), col(0), col(0)] + [ANY] * len(after),
        out_specs=pl.BlockSpec((4, s, w), lambda g: (0, 0, g)),
        out_shape=jax.ShapeDtypeStruct((8, s, ATTN_WIDTH), BF16),
        scratch_shapes=[pltpu.VMEM((s, w), BF16)] * 4 + [pltpu.VMEM((s, w), F32)] * 2
        + [pltpu.VMEM((hp, nt, t, t), F32), pltpu.VMEM((nt, t, t), F32)],
        compiler_params=_params(("arbitrary",)),
    )(proj, proj, proj, attn_out, lse, dmixed, *after)


def _ret_log_gammas():
    return [math.log(1.0 - 2.0 ** (-5.0 - h)) for h in range(RET_HEADS)]


def _ret_decay(delta, log_gamma):
    dec = jnp.exp(delta.astype(F32) * log_gamma) * (1.0 / math.sqrt(RET_HEAD_DIM))
    return jnp.where(delta >= 0, dec, 0.0)


def _ret_fwd(proj, mixed, after=()):
    after = tuple(after)
    s = proj.shape[0]
    t = SEQ_TILE
    hd = RET_HEAD_DIM
    nh = RET_HEADS
    log_gammas = _ret_log_gammas()
    c0 = 3 * ATTN_WIDTH // hd

    def body(q_ref, k_ref, v_ref, g_ref, *rest):
        mix_ref, raw_ref, kb, vb, decay_tab = rest[1 + len(after):]
        h = pl.program_id(0)
        i = pl.program_id(1)

        @pl.when(i == 0)
        def _():
            kb[...] = k_ref[...].astype(BF16)
            vb[...] = v_ref[...].astype(BF16)
            _fill_ret_decay(decay_tab, _select_by_index(h, log_gammas))

        q = q_ref[...].astype(BF16)

        def step(j, acc):
            rows = pl.ds(pl.multiple_of(j * t, t), t)
            sc = lax.dot_general(q, kb[rows, :], _NT_DIMS, preferred_element_type=F32) * decay_tab[i - j]
            return acc + jnp.dot(sc.astype(BF16), vb[rows, :], preferred_element_type=F32)

        ret = lax.fori_loop(0, i + 1, step, jnp.zeros((t, hd), F32))
        raw_ref[...] = ret
        r = lax.rsqrt(jnp.mean(ret * ret, axis=-1, keepdims=True) + NORM_EPS)
        g = g_ref[...]
        mix_ref[...] = (g * _sigmoid(g) * (ret * r)).astype(BF16)

    return pl.pallas_call(
        body, name="ret_fwd", grid=(nh, s // t),
        in_specs=[pl.BlockSpec((t, hd), lambda h, i: (i, c0 + h)),
                  pl.BlockSpec((s, hd), lambda h, i: (0, c0 + nh + h)),
                  pl.BlockSpec((s, hd), lambda h, i: (0, c0 + 2 * nh + h)),
                  pl.BlockSpec((t, hd), lambda h, i: (i, c0 + 3 * nh + h))] + [ANY] * (1 + len(after)),
        out_specs=[pl.BlockSpec((None, t, hd), lambda h, i: (1, i, h)), pl.BlockSpec((t, hd), lambda h, i: (i, h))],
        out_shape=[jax.ShapeDtypeStruct(mixed.shape, BF16), jax.ShapeDtypeStruct((s, RET_WIDTH), F32)],
        input_output_aliases={4: 0},
        scratch_shapes=[pltpu.VMEM((s, hd), BF16), pltpu.VMEM((s, hd), BF16), pltpu.VMEM((s // t, t, t), F32)],
        compiler_params=_params(("arbitrary", "arbitrary")),
    )(proj, proj, proj, proj, mixed, *after)


def _ret_bwd(proj, ret_raw, dmixed, dsec, after=()):
    after = tuple(after)
    s = proj.shape[0]
    t = SEQ_TILE
    nt = s // t
    hd = RET_HEAD_DIM
    nh = RET_HEADS
    log_gammas = _ret_log_gammas()
    c0 = 3 * ATTN_WIDTH // hd
    mixed_blocks = ATTN_WIDTH // hd

    def body(q_ref, k_ref, v_ref, g_ref, raw_ref, dmix_ref, *rest):
        dsec_ref, qb, kb, vb, dretb, dq_acc, decay_tab = rest[1 + len(after):]
        h = pl.program_id(0)
        _fill_ret_decay(decay_tab, _select_by_index(h, log_gammas))
        qb[...] = q_ref[...].astype(BF16)
        kb[...] = k_ref[...].astype(BF16)
        vb[...] = v_ref[...].astype(BF16)
        ret = raw_ref[...]
        r = lax.rsqrt(jnp.mean(ret * ret, axis=-1, keepdims=True) + NORM_EPS)
        normed = ret * r
        g = g_ref[...]
        sg = _sigmoid(g)
        dout = dmix_ref[...]
        dsec_ref[3] = (dout * normed * sg * (1.0 + g * (1.0 - sg))).astype(BF16)
        dn = dout * g * sg
        dret = r * (dn - normed * jnp.mean(dn * normed, axis=-1, keepdims=True))
        dretb[...] = dret.astype(BF16)
        dq_acc[...] = jnp.zeros((s, hd), F32)

        def over_keys(j, _):
            krows = pl.ds(pl.multiple_of(j * t, t), t)
            kj = kb[krows, :]
            vj = vb[krows, :]

            def over_queries(i, carry):
                dk, dv = carry
                qrows = pl.ds(pl.multiple_of(i * t, t), t)
                qi = qb[qrows, :]
                doi = dretb[qrows, :]
                dec = decay_tab[i - j]
                a = (lax.dot_general(qi, kj, _NT_DIMS, preferred_element_type=F32) * dec).astype(BF16)
                da = (lax.dot_general(doi, vj, _NT_DIMS, preferred_element_type=F32) * dec).astype(BF16)
                dv = dv + lax.dot_general(a, doi, _TN_DIMS, preferred_element_type=F32)
                dk = dk + lax.dot_general(da, qi, _TN_DIMS, preferred_element_type=F32)
                dq_acc[qrows, :] += jnp.dot(da, kj, preferred_element_type=F32)
                return dk, dv

            zero = jnp.zeros((t, hd), F32)
            dk, dv = lax.fori_loop(j, nt, over_queries, (zero, zero))
            dsec_ref[1, krows, :] = dk.astype(BF16)
            dsec_ref[2, krows, :] = dv.astype(BF16)
            return 0

        lax.fori_loop(0, nt, over_keys, 0)
        dsec_ref[0] = dq_acc[...].astype(BF16)

    def col(off):
        return pl.BlockSpec((s, hd), lambda h: (0, off + h))

    return pl.pallas_call(
        body, name="ret_bwd", grid=(nh,),
        in_specs=[col(c0), col(c0 + nh), col(c0 + 2 * nh), col(c0 + 3 * nh), col(0), col(mixed_blocks)]
        + [ANY] * (1 + len(after)),
        out_specs=pl.BlockSpec((4, s, hd), lambda h: (1, 0, h)),
        out_shape=jax.ShapeDtypeStruct(dsec.shape, BF16),
        input_output_aliases={6: 0},
        scratch_shapes=[pltpu.VMEM((s, hd), BF16)] * 4 + [pltpu.VMEM((s, hd), F32)]
        + [pltpu.VMEM((nt, t, t), F32)],
        compiler_params=_params(("arbitrary",)),
    )(proj, proj, proj, proj, ret_raw, dmixed, dsec, *after)


_FLIPS = (2, 1, 3)


def _other_chips(x, y):
    return [(1 - x, y), (x, 1 - y), (1 - x, 1 - y)]


_HBM = pl.BlockSpec(memory_space=pltpu.HBM)
_SEM = pl.BlockSpec(memory_space=pltpu.SEMAPHORE)
_EFFECT = pltpu.SideEffectType.DATAFLOW_SIDE_EFFECTING


def _in_hbm(a):
    return pltpu.with_memory_space_constraint(a, pltpu.HBM)


def _weight_view(w, column_sharded):
    if column_sharded:
        return w.reshape(2, w.shape[0] // 2, w.shape[1])
    return w.reshape(N_CHIPS, 2, w.shape[0] // (2 * N_CHIPS), w.shape[1])


def _weight_unview(v):
    if v.ndim == 3:
        return v.reshape(2 * v.shape[1], v.shape[2])
    return v.reshape(N_CHIPS * 2 * v.shape[2], v.shape[3])


def _weight_region(buf, shard, half):
    if len(buf.shape) == 3:
        cols = buf.shape[2] // N_CHIPS
        return buf.at[half, :, pl.ds(shard * cols, cols)]
    return buf.at[shard, half]


def _remote(where, send_sem, recv_sem, to):
    return pltpu.make_async_remote_copy(src_ref=where, dst_ref=where, send_sem=send_sem, recv_sem=recv_sem,
                                        device_id=to, device_id_type=MESH)


def _for_my_shard(fn):
    x, y, _ = _place()
    for ss in range(N_CHIPS):
        pl.when(2 * x + y == ss)(functools.partial(fn, ss))


def _gather_forward(views, which, send_sems, recv_sems, after, name, base=0):
    n_w = len(views)
    which = [base // 3 + w for w in which] if base % 3 == 0 else None
    assert which is not None, "base must be a multiple of 3"

    def body(*refs):
        send_in, recv_in = refs[n_w:n_w + 2]
        fwd_send, fwd_recv = refs[n_w + 3:n_w + 5]
        bufs = refs[n_w + 5:]
        x, y, c = _place()
        sibling = (x, y, 1 - c)

        def forward(ss):
            for i, w in enumerate(which):
                for j in range(3):
                    landed = _weight_region(bufs[i], ss ^ _FLIPS[j], c)
                    _remote(landed, send_in.at[3 * w + j], recv_in.at[3 * w + j], sibling).wait_recv()
                    _remote(landed, fwd_send.at[3 * i + j], fwd_recv.at[3 * i + j], sibling).start()

        _for_my_shard(forward)
        for i, w in enumerate(which):
            for j in range(3):
                _remote(_weight_region(bufs[i], 0, 0), send_in.at[3 * w + j], recv_in.at[3 * w + j],
                        sibling).wait_send()

    return pl.pallas_call(
        body, name=name,
        in_specs=[_HBM] * n_w + [_SEM, _SEM, ANY], out_specs=[_SEM, _SEM] + [_HBM] * n_w,
        out_shape=[pltpu.SemaphoreType.DMA((3 * n_w,)), pltpu.SemaphoreType.DMA((3 * n_w,))]
        + [pltpu.HBM(v.shape, BF16) for v in views],
        input_output_aliases={w: 2 + w for w in range(n_w)},
        compiler_params=pltpu.CompilerParams(has_side_effects=_EFFECT),
    )(*views, send_sems, recv_sems, after)


def _gather_end(views, fwd_send, fwd_recv, after, name):
    n_w = len(views)

    def body(*refs):
        fwd_send_ref, fwd_recv_ref = refs[n_w:n_w + 2]
        bufs = refs[n_w + 3:]
        x, y, c = _place()
        for i in range(n_w):
            for j in range(3):
                cp = _remote(_weight_region(bufs[i], 0, 0), fwd_send_ref.at[3 * i + j], fwd_recv_ref.at[3 * i + j],
                             (x, y, 1 - c))
                cp.wait_recv()
                cp.wait_send()

    outs = pl.pallas_call(
        body, name=name,
        in_specs=[_HBM] * n_w + [_SEM, _SEM, ANY], out_specs=[_HBM] * n_w,
        out_shape=[pltpu.HBM(v.shape, BF16) for v in views],
        input_output_aliases={w: w for w in range(n_w)},
        compiler_params=pltpu.CompilerParams(has_side_effects=_EFFECT),
    )(*views, fwd_send, fwd_recv, after)
    return [_weight_unview(o) for o in outs]


def _comm_call(name, bufs, sem_pairs, after, n_new, fn):
    n, n_sem, after = len(bufs), 2 * len(sem_pairs), tuple(after)
    n_out_sem = 2 if n_new else 0

    def body(*refs):
        sems = refs[n:n + n_sem]
        outs = refs[n + n_sem + len(after):]
        new = outs[:n_out_sem] if n_new else (None, None)
        fn(outs[n_out_sem:], [(sems[2 * i], sems[2 * i + 1]) for i in range(len(sem_pairs))], *new)

    res = pl.pallas_call(
        body, name=name,
        in_specs=[_HBM] * n + [_SEM] * n_sem + [ANY] * len(after),
        out_specs=[_SEM] * n_out_sem + [_HBM] * n,
        out_shape=[pltpu.SemaphoreType.DMA((n_new,))] * n_out_sem + [pltpu.HBM(b.shape, b.dtype) for b in bufs],
        input_output_aliases={i: n_out_sem + i for i in range(n)},
        compiler_params=pltpu.CompilerParams(has_side_effects=_EFFECT),
    )(*bufs, *[s for pair in sem_pairs for s in pair], *after)
    return list(res[:n_out_sem]), list(res[n_out_sem:])


def _quarter(piece, q):
    rows = piece.shape[0] // 2
    return piece.at[pl.ds(q * rows, rows)]


def _gather_in_start(view, name):
    def fn(bufs, _, send, recv):
        x, y, c = _place()

        def go(ss):
            for j, chip in enumerate(_other_chips(x, y)[:2]):
                _remote(_weight_region(bufs[0], ss, c), send.at[j], recv.at[j], (*chip, c)).start()

        _for_my_shard(go)

    sems, (view,) = _comm_call(name, [_in_hbm(view)], [], (), 2, fn)
    return sems, view


def _gather_out_gate_start(v_out, v_gate, after, name):
    def fn(bufs, _, send, recv):
        x, y, c = _place()
        chips = _other_chips(x, y)

        def go(ss):
            for j in range(3):
                _remote(_weight_region(bufs[0], ss, c), send.at[j], recv.at[j], (*chips[j], c)).start()
            for j in range(2):
                _remote(_weight_region(bufs[1], ss, c), send.at[3 + j], recv.at[3 + j], (*chips[j], c)).start()

        _for_my_shard(go)

    sems, views = _comm_call(name, [_in_hbm(v_out), _in_hbm(v_gate)], [], after, 5, fn)
    return sems, views


def _gather_relay(view, started, base, after, name, then=None, then_peers=0):
    n_new = 6 + then_peers if then_peers else 4

    def fn(bufs, pairs, send, recv):
        (send_in, recv_in), = pairs
        x, y, c = _place()
        chips = _other_chips(x, y)
        sibling = (x, y, 1 - c)

        def go(ss):
            landed = [_weight_region(bufs[0], ss ^ _FLIPS[j], c) for j in range(2)]
            for j in range(2):
                _remote(landed[j], send_in.at[base + j], recv_in.at[base + j], sibling).wait_recv()
            for j in range(2):
                _remote(_quarter(landed[j], j), send.at[j], recv.at[j], (*chips[1 - j], c)).start()
            for j in range(2):
                _remote(landed[j], send.at[2 + j], recv.at[2 + j], sibling).start()
            for j in range(then_peers):
                _remote(_weight_region(bufs[1], ss, c), send.at[6 + j], recv.at[6 + j], (*chips[j], c)).start()

        _for_my_shard(go)
        for j in range(2):
            _remote(_weight_region(bufs[0], 0, 0), send_in.at[base + j], recv_in.at[base + j], sibling).wait_send()

    views = [view] if then is None else [view, _in_hbm(then)]
    sems, views = _comm_call(name, views, [started], after, n_new, fn)
    return sems, views


def _gather_in_neighbours_end(view, relayed, after, name):
    def fn(bufs, pairs, *_):
        (send, recv), = pairs
        x, y, c = _place()
        for j in range(2):
            cp = _remote(_weight_region(bufs[0], 0, 0), send.at[2 + j], recv.at[2 + j], (x, y, 1 - c))
            cp.wait_recv()
            cp.wait_send()

    _, (view,) = _comm_call(name, [view], [relayed], after, 0, fn)
    return view


def _gather_in_diagonal(view, relayed, after, name):
    def fn(bufs, pairs, send, recv):
        (send_in, recv_in), = pairs
        x, y, c = _place()
        sibling = (x, y, 1 - c)
        any_quarter = _quarter(_weight_region(bufs[0], 0, 0), 0)
        for j in range(2):
            cp = _remote(any_quarter, send_in.at[j], recv_in.at[j], sibling)
            cp.wait_recv()
            cp.wait_send()

        def go(ss):
            _remote(_weight_region(bufs[0], ss ^ _FLIPS[2], c), send.at[0], recv.at[0], sibling).start()

        _for_my_shard(go)

    sems, (view,) = _comm_call(name, [view], [relayed], after, 1, fn)
    return sems, view


def _gather_in_diagonal_end(view, forwarded, after, name):
    def fn(bufs, pairs, *_):
        (send, recv), = pairs
        x, y, c = _place()
        cp = _remote(_weight_region(bufs[0], 0, 0), send.at[0], recv.at[0], (x, y, 1 - c))
        cp.wait_recv()
        cp.wait_send()

    _, (view,) = _comm_call(name, [view], [forwarded], after, 0, fn)
    return view


def _in_proj_shard(h1, wi, proj, shard_arr, name):
    s, d = h1.shape
    n = wi.shape[1]
    blocks = 2
    tn = n // (N_CHIPS * blocks)
    given = [] if proj is None else [proj]

    def body(shard_ref, h_ref, w_ref, *rest):
        del shard_ref
        rest[-1][...] = jnp.dot(h_ref[...], w_ref[...], preferred_element_type=F32)

    grid_spec = pltpu.PrefetchScalarGridSpec(
        num_scalar_prefetch=1, grid=(shard_arr.shape[0], blocks),
        in_specs=[pl.BlockSpec((s, d), lambda p, j, shard_ref: (0, 0)),
                  pl.BlockSpec((d, tn), lambda p, j, shard_ref: (0, shard_ref[p] * blocks + j))]
        + [ANY] * len(given),
        out_specs=pl.BlockSpec((s, tn), lambda p, j, shard_ref: (0, shard_ref[p] * blocks + j)))
    return pl.pallas_call(
        body, name=name, grid_spec=grid_spec,
        out_shape=jax.ShapeDtypeStruct((s, n), F32),
        input_output_aliases={3: 0} if given else {},
        compiler_params=_params(("arbitrary", "arbitrary")),
    )(shard_arr, h1, wi, *given)


def _split_start(name, bufs, n_sems, copies):
    n = len(bufs)

    def body(*refs):
        send_sems, recv_sems = refs[n:n + 2]
        for cp in copies(refs[n + 2:], send_sems, recv_sems):
            cp.start()

    outs = pl.pallas_call(
        body, name=name,
        in_specs=[_HBM] * n, out_specs=[_SEM, _SEM] + [_HBM] * n,
        out_shape=[pltpu.SemaphoreType.DMA((n_sems,)), pltpu.SemaphoreType.DMA((n_sems,))]
        + [pltpu.HBM(b.shape, b.dtype) for b in bufs],
        input_output_aliases={i: 2 + i for i in range(n)},
        compiler_params=pltpu.CompilerParams(has_side_effects=_EFFECT),
    )(*[_in_hbm(b) for b in bufs])
    return outs[0], outs[1], list(outs[2:])


def _split_wait(name, bufs, send_sems, recv_sems, copies, after):
    n = len(bufs)

    def body(*refs):
        send_ref, recv_ref = refs[n:n + 2]
        for cp in copies(refs[n + 3:], send_ref, recv_ref):
            cp.wait()

    return list(pl.pallas_call(
        body, name=name,
        in_specs=[_HBM] * n + [_SEM, _SEM, ANY], out_specs=[_HBM] * n,
        out_shape=[pltpu.HBM(b.shape, b.dtype) for b in bufs],
        input_output_aliases={i: i for i in range(n)},
        compiler_params=pltpu.CompilerParams(has_side_effects=_EFFECT),
    )(*bufs, send_sems, recv_sems, after))


def _halves_copies(n_w):
    def copies(bufs, send_sems, recv_sems):
        x, y, c = _place()
        out = []
        for w in range(n_w):
            view, land = bufs[w], bufs[n_w + w]
            src = view.at[1 - c] if len(view.shape) == 3 else view.at[:, 1 - c]
            out.append(pltpu.make_async_remote_copy(
                src_ref=src, dst_ref=land, send_sem=send_sems.at[w], recv_sem=recv_sems.at[w],
                device_id=(x, y, 1 - c), device_id_type=MESH))
        return out
    return copies


def _pieces_copies(n_w):
    def copies(bufs, send_sems, recv_sems):
        x, y, c = _place()
        out = []
        for w in range(n_w):
            for j, (cx, cy) in enumerate(_other_chips(x, y)):
                out.append(pltpu.make_async_remote_copy(
                    src_ref=bufs[w].at[2 * cx + cy], dst_ref=bufs[n_w + w].at[j],
                    send_sem=send_sems.at[3 * w + j], recv_sem=recv_sems.at[3 * w + j],
                    device_id=(cx, cy, c), device_id_type=MESH))
        return out
    return copies


def _join_copies(n_w):
    def copies(bufs, send_sems, recv_sems):
        x, y, c = _place()
        return [pltpu.make_async_remote_copy(
            src_ref=bufs[w].at[c], dst_ref=bufs[w].at[c], send_sem=send_sems.at[w], recv_sem=recv_sems.at[w],
            device_id=(x, y, 1 - c), device_id_type=MESH) for w in range(n_w)]
    return copies


def _halves_landing(view):
    shape = view.shape[1:] if view.ndim == 3 else (N_CHIPS,) + view.shape[2:]
    return lax.empty(shape, BF16)


def _halves_start(tag, grads, column_sharded):
    views = [_weight_view(g, cs) for g, cs in zip(grads, column_sharded)]
    n = len(views)
    return _split_start("halves_start_" + tag, views + [_halves_landing(v) for v in views], n, _halves_copies(n))


def _halves_wait(tag, state, after):
    send_sems, recv_sems, bufs = state
    n = len(bufs) // 2
    bufs = _split_wait("halves_wait_" + tag, bufs, send_sems, recv_sems, _halves_copies(n), after)
    return bufs[:n], bufs[n:]


def _pieces_start(tag, pieces):
    n = len(pieces)
    landing = [lax.empty((3,) + p.shape[1:], BF16) for p in pieces]
    return _split_start("pieces_start_" + tag, list(pieces) + landing, 3 * n, _pieces_copies(n))


def _pieces_wait(tag, state, after):
    send_sems, recv_sems, bufs = state
    n = len(bufs) // 2
    bufs = _split_wait("pieces_wait_" + tag, bufs, send_sems, recv_sems, _pieces_copies(n), after)
    return bufs[:n], bufs[n:]


def _join_start(tag, shards):
    n = len(shards)
    return _split_start("join_start_" + tag, list(shards), n, _join_copies(n))


def _join_wait(tag, state, after):
    send_sems, recv_sems, bufs = state
    bufs = _split_wait("join_wait_" + tag, bufs, send_sems, recv_sems, _join_copies(len(bufs)), after)
    return [b.reshape(2 * b.shape[1], b.shape[2]) for b in bufs]


def _chip_sum_col(g3, sib, c_arr, name):
    _, hk, n = g3.shape
    cols = n // N_CHIPS
    tr = _row_tile(hk, cols * 2, limit=4 * 1024 * 1024)

    def body(c_ref, g_ref, s_ref, o_ref):
        del c_ref
        o_ref[...] = (g_ref[...].astype(F32) + s_ref[...].astype(F32)).astype(BF16)

    grid_spec = pltpu.PrefetchScalarGridSpec(
        num_scalar_prefetch=1, grid=(N_CHIPS, hk // tr),
        in_specs=[pl.BlockSpec((None, tr, cols), lambda p, r, c_ref: (c_ref[0], r, p)),
                  pl.BlockSpec((tr, cols), lambda p, r, c_ref: (r, p))],
        out_specs=pl.BlockSpec((None, tr, cols), lambda p, r, c_ref: (p, r, 0)))
    return pl.pallas_call(
        body, name=name, grid_spec=grid_spec,
        out_shape=jax.ShapeDtypeStruct((N_CHIPS, hk, cols), BF16),
        compiler_params=_params(("parallel", "parallel")),
    )(c_arr, g3, sib)


def _chip_sum_row(g4, sib, c_arr, name):
    _, _, hr, n = g4.shape
    tr = _row_tile(hr, n * 2, limit=4 * 1024 * 1024)

    def body(c_ref, g_ref, s_ref, o_ref):
        del c_ref
        o_ref[...] = (g_ref[...].astype(F32) + s_ref[...].astype(F32)).astype(BF16)

    grid_spec = pltpu.PrefetchScalarGridSpec(
        num_scalar_prefetch=1, grid=(N_CHIPS, hr // tr),
        in_specs=[pl.BlockSpec((None, None, tr, n), lambda p, r, c_ref: (p, c_ref[0], r, 0)),
                  pl.BlockSpec((None, tr, n), lambda p, r, c_ref: (p, r, 0))],
        out_specs=pl.BlockSpec((None, tr, n), lambda p, r, c_ref: (p, r, 0)))
    return pl.pallas_call(
        body, name=name, grid_spec=grid_spec,
        out_shape=jax.ShapeDtypeStruct((N_CHIPS, hr, n), BF16),
        compiler_params=_params(("parallel", "parallel")),
    )(c_arr, g4, sib)


def _sum_pieces(pieces, received, place_arr, name):
    _, r, n = pieces.shape
    tr = _row_tile(r, n * 4, limit=4 * 1024 * 1024)

    def body(p_ref, own_ref, r0_ref, r1_ref, r2_ref, o_ref):
        del p_ref
        acc = own_ref[...].astype(F32) + r0_ref[...].astype(F32)
        acc = acc + r1_ref[...].astype(F32)
        o_ref[...] = acc + r2_ref[...].astype(F32)

    def recv_spec(j):
        return pl.BlockSpec((None, tr, n), lambda i, p_ref: (j, i, 0))

    grid_spec = pltpu.PrefetchScalarGridSpec(
        num_scalar_prefetch=1, grid=(r // tr,),
        in_specs=[pl.BlockSpec((None, tr, n), lambda i, p_ref: (p_ref[0], i, 0)),
                  recv_spec(0), recv_spec(1), recv_spec(2)],
        out_specs=pl.BlockSpec((None, tr, n), lambda i, p_ref: (p_ref[1], i, 0)))
    return pl.pallas_call(
        body, name=name, grid_spec=grid_spec,
        out_shape=jax.ShapeDtypeStruct((2, r, n), F32),
        compiler_params=_params(("parallel",)),
    )(place_arr, pieces, received, received, received)


def _norm_weights_step(parts, w, m, v, after=()):
    rows, d = parts.shape
    after = tuple(after)

    def body(p_ref, w_ref, m_ref, v_ref, *rest):
        g_ref, d_ref, mo_ref, vo_ref, gathered, send_sems, recv_sems = rest[len(after):]
        x, y, c = _place()
        me = 4 * x + 2 * y + c
        gathered[me] = p_ref[...]
        copies = []
        for k in range(1, N_DEV):
            peer = (x ^ ((k >> 2) & 1), y ^ ((k >> 1) & 1), c ^ (k & 1))
            copies.append(pltpu.make_async_remote_copy(
                src_ref=p_ref, dst_ref=gathered.at[me], send_sem=send_sems.at[k - 1],
                recv_sem=recv_sems.at[k - 1], device_id=peer, device_id_type=MESH))
        for cp in copies:
            cp.start()
        for cp in copies:
            cp.wait()
        g = gathered[0]
        for k in range(1, N_DEV):
            g = g + gathered[k]
        delta, m_new, v_new = _adamw_math(w_ref[...], g, m_ref[...], v_ref[...])
        g_ref[...] = g
        d_ref[...] = delta
        mo_ref[...] = m_new
        vo_ref[...] = v_new

    vmem = pl.BlockSpec(memory_space=pltpu.VMEM)
    shp = jax.ShapeDtypeStruct((rows, d), F32)
    return pl.pallas_call(
        body, name="norm_weights_step",
        in_specs=[vmem] * 4 + [ANY] * len(after), out_specs=[vmem] * 4, out_shape=[shp] * 4,
        scratch_shapes=[pltpu.VMEM((N_DEV, rows, d), F32), pltpu.SemaphoreType.DMA((N_DEV - 1,)),
                        pltpu.SemaphoreType.DMA((N_DEV - 1,))],
        compiler_params=pltpu.CompilerParams(has_side_effects=True),
    )(parts, w, m, v, *after)


def kernel(x, norm_mix_w, w_in, w_out, norm_ffn_w, w_gate, w_up, w_down, norm_final_w, loss_target, m_norm_mix_w, m_w_in, m_w_out, m_norm_ffn_w, m_w_gate, m_w_up, m_w_down, m_norm_final_w, v_norm_mix_w, v_w_in, v_w_out, v_norm_ffn_w, v_w_gate, v_w_up, v_w_down, v_norm_final_w):
    s, d = x.shape[1], x.shape[2]
    xs = x.reshape(s, d)
    target = loss_target.reshape(s, d)
    big = {"w_in": (w_in, m_w_in, v_w_in), "w_out": (w_out, m_w_out, v_w_out),
           "w_gate": (w_gate, m_w_gate, v_w_gate), "w_up": (w_up, m_w_up, v_w_up),
           "w_down": (w_down, m_w_down, v_w_down)}
    big = {k: tuple(a.reshape(a.shape[1:]) for a in t) for k, t in big.items()}
    col_names, row_names = ("w_in", "w_gate", "w_up"), ("w_out", "w_down")
    n_in = N_CHIPS * big["w_in"][0].shape[1]
    ffn = N_CHIPS * big["w_gate"][0].shape[1]
    mix = ATTN_WIDTH + RET_WIDTH
    c_arr = lax.axis_index("c").astype(I32).reshape(1)
    shard_arr = (2 * lax.axis_index("x") + lax.axis_index("y")).astype(I32).reshape(1)
    place_arr = jnp.concatenate([shard_arr, c_arr])

    def cast(k, after=()):
        return _weight_view(_cast_into_full(big[k][0], shard_arr, k in col_names, "cast_" + k, after), k in col_names)

    started_in, v_in = _gather_in_start(cast("w_in"), "gather_in_start")

    sec = ATTN_WIDTH

    def section(p, rows):
        return pl.BlockSpec((None, rows, sec), lambda i, j, kk: (p, i, 0))

    h1 = _rms_fwd(xs, norm_mix_w, "rms_mix_fwd", after=[v_in])
    my_shard = shard_arr[0]
    shard_of = [jnp.bitwise_xor(my_shard, f).astype(I32).reshape(1) for f in (0,) + _FLIPS]
    proj = _in_proj_shard(h1, _weight_unview(v_in), None, shard_of[0], "in_proj_own")
    early_views = [cast(k, after=[proj]) for k in ("w_out", "w_gate")]
    v_up, v_down = [cast(k, after=[proj]) for k in ("w_up", "w_down")]
    relayed_in, (v_in,) = _gather_relay(v_in, started_in, 0, early_views + [v_up, v_down], "gather_in_relay")
    started_og, (v_out, v_gate) = _gather_out_gate_start(*early_views, [v_in], "gather_out_gate_start")
    v_in = _gather_in_neighbours_end(v_in, relayed_in, [v_out], "gather_in_neighbours_end")
    proj = _in_proj_shard(h1, _weight_unview(v_in), proj, jnp.concatenate(shard_of[1:3]), "in_proj_neighbours")
    forwarded_in, v_in = _gather_in_diagonal(v_in, relayed_in, [proj], "gather_in_diagonal")
    wi = _weight_unview(_gather_in_diagonal_end(v_in, forwarded_in, [proj], "gather_in_diagonal_end"))
    proj = _in_proj_shard(h1, wi, proj, shard_of[3], "in_proj_diagonal")
    fs_o, fr_o, v_out = _gather_forward([v_out], [0], *started_og, proj, "gather_forward_out")
    mixed, attn_o, lse = _attn_fwd(proj, after=[v_out])
    relayed_g, (v_gate, v_up) = _gather_relay(v_gate, started_og, 3, [attn_o], "gather_gate_relay",
                                              then=v_up, then_peers=2)
    mixed, ret_raw = _ret_fwd(proj, mixed, after=[v_gate])
    wo, = _gather_end([v_out], fs_o, fr_o, ret_raw, "gather_end_out")
    x1, = _matmul("out_proj", "nn", [mixed, mixed], [wo, wo], [0, 0], s, d, sec, s // 2, 512, sec, [xs], [F32],
                  _epi_residual, b_koff=[0, 1], a_specs=[section(0, s // 2), section(1, s // 2)])
    h2 = _rms_fwd(x1, norm_ffn_w, "rms_ffn_fwd")
    relayed_u, (v_up, v_down) = _gather_relay(v_up, relayed_g, 6, [h2], "gather_up_relay",
                                              then=v_down, then_peers=3)
    v_gate = _gather_in_neighbours_end(v_gate, relayed_g, [v_up], "gather_gate_neighbours_end")
    forwarded_g, v_gate = _gather_in_diagonal(v_gate, relayed_g, [v_up], "gather_gate_diagonal")
    v_up = _gather_in_neighbours_end(v_up, relayed_u, [v_gate], "gather_up_neighbours_end")
    wg = _weight_unview(_gather_in_diagonal_end(v_gate, forwarded_g, [v_up], "gather_gate_diagonal_end"))
    forwarded_u, v_up = _gather_in_diagonal(v_up, relayed_u, [wg], "gather_up_diagonal")
    wu = _weight_unview(_gather_in_diagonal_end(v_up, forwarded_u, [wg], "gather_up_diagonal_end"))
    gate, up, act = _matmul("gate_up", "nn", [h2, h2], [wg, wu], [0, 1], s, ffn, d, s, 512, d, [],
                            [BF16, BF16, BF16], _epi_swiglu, a_single_buffer=True)
    fs, fr, v_down = _gather_forward([v_down], [0], *relayed_u, act, "gather_forward_down", base=6)
    wd, = _gather_end([v_down], fs, fr, act, "gather_end_down")
    x2, = _matmul("down_proj", "nn", [act], [wd], [0], s, d, ffn, s // 2, 512, ffn, [x1], [F32],
                  _epi_residual)
    loss_row, dx2, dx2b, dwf = _final_norm_loss(x2, norm_final_w.reshape(1, d), target, "final_norm_loss")

    names = col_names + row_names
    grads, new = {}, {}

    def chip_sums(tag_names, views, sibs):
        return [(_chip_sum_col if k in col_names else _chip_sum_row)(v, sb, c_arr, "chip_sum_" + k)
                for k, v, sb in zip(tag_names, views, sibs)]

    def piece_sums(tag_names, pieces, received):
        return [_sum_pieces(p, r, place_arr, "sum_pieces_" + k) for k, p, r in zip(tag_names, pieces, received)]

    def update(k):
        new[k] = _adamw(big[k][0], grads[k], big[k][1], big[k][2], "adamw_" + k)

    dgate, dup = _matmul("d_act", "nt", [dx2b], [wd], [0], s, ffn, d, s, 512, d, [gate, up],
                         [BF16, BF16], _epi_swiglu_bwd, a_single_buffer=True)
    g_wd, = _matmul("g_w_down", "tn", [act], [dx2b], [0], ffn, d, s, 512, d, s, [], [BF16], _epi_plain)
    halves_d = _halves_start("down", [g_wd], [False])
    dh2, = _matmul("d_h2", "nt", [dgate, dup], [wg, wu], [0, 0], s, d, ffn, s // 2, 256, ffn, [], [F32],
                   _epi_plain, after=halves_d[2][-1:], a_single_buffer=True)
    pieces_d = _pieces_start("down", chip_sums(["w_down"], *_halves_wait("down", halves_d, dh2)))
    g_wg, g_wu = _matmul("g_w_gate_up", "tn", [h2, h2], [dgate, dup], [0, 1], d, ffn, s, 1024, 512, s, [],
                         [BF16, BF16], _epi_two, after=pieces_d[2][-1:])
    halves_gu = _halves_start("gate_up", [g_wg, g_wu], [True, True])
    dx1, dx1b, dw_ffn = _rms_bwd(x1, norm_ffn_w, dh2, dx2, "rms_ffn_bwd", after=halves_gu[2][-1:])

    dmixed, = _matmul("d_mixed", "nt", [dx1b], [wo], [0], s, mix, d, s // 2, 512, d, [], [F32], _epi_plain)
    pieces_gu = _pieces_start("gate_up", chip_sums(["w_gate", "w_up"], *_halves_wait("gate_up", halves_gu, dmixed)))
    per = sec // 512
    g_wo, = _matmul("g_w_out", "tn", [mixed], [dx1b], [0], mix, d, s, 512, d, s, [], [BF16], _epi_plain,
                    after=pieces_gu[2][-1:],
                    a_specs=[pl.BlockSpec((None, s, 512), lambda i, j, kk: (i // per, 0, i % per))])
    halves_o = _halves_start("out", [g_wo], [False])
    dsec = _attn_bwd(proj, attn_o, lse, dmixed, after=halves_o[2][-1:])
    pieces_o = _pieces_start("out", chip_sums(["w_out"], *_halves_wait("out", halves_o, dsec)))
    dsec = _ret_bwd(proj, ret_raw, dmixed, dsec, after=pieces_o[2][-1:])
    where = [0, 1, 2, 4, 5, 6, 7]
    n_sec = len(where)
    g_wi, = _matmul("g_w_in", "tn", [h1], [dsec], [0], d, n_in, s, 1024, sec, s, [], [BF16], _epi_plain,
                    b_specs=[pl.BlockSpec((None, s, sec), lambda i, j, kk: (j + (j >= 3).astype(I32), 0, 0))])
    halves_i = _halves_start("in", [g_wi], [True])
    dh1, = _matmul("d_h1", "nt", [dsec] * n_sec, [wi] * n_sec, [0] * n_sec, s, d, sec, s // 2, 256, sec, [], [F32],
                   _epi_plain, b_koff=list(range(n_sec)), after=halves_i[2][-1:],
                   a_specs=[section(p, s // 2) for p in where])
    pieces_i = _pieces_start("in", chip_sums(["w_in"], *_halves_wait("in", halves_i, dh1)))
    grad_x, _, dw_mix = _rms_bwd(xs, norm_mix_w, dh1, dx1, "rms_mix_bwd", after=pieces_i[2][-1:])

    def rows8(*vs):
        return jnp.concatenate([v.reshape(1, d) for v in vs] + [jnp.zeros((8 - len(vs), d), F32)], axis=0)

    join_d = _join_start("down", piece_sums(["w_down"], *_pieces_wait("down", pieces_d, grad_x)))
    join_gu = _join_start("gate_up", piece_sums(["w_gate", "w_up"], *_pieces_wait("gate_up", pieces_gu, join_d[2][0])))
    join_o = _join_start("out", piece_sums(["w_out"], *_pieces_wait("out", pieces_o, join_gu[2][0])))
    grads["w_down"], = _join_wait("down", join_d, join_o[2][0])
    update("w_down")
    grads["w_gate"], grads["w_up"] = _join_wait("gate_up", join_gu, new["w_down"][0])
    update("w_gate")
    update("w_up")
    grads["w_out"], = _join_wait("out", join_o, new["w_up"][0])
    update("w_out")
    join_i = _join_start("in", piece_sums(["w_in"], *_pieces_wait("in", pieces_i, new["w_out"][0])))
    ng, nd, nm, nv = _norm_weights_step(
        rows8(dw_mix, dw_ffn, dwf, jnp.broadcast_to(loss_row[:, :1], (1, d))),
        rows8(norm_mix_w, norm_ffn_w, norm_final_w),
        rows8(m_norm_mix_w, m_norm_ffn_w, m_norm_final_w), rows8(v_norm_mix_w, v_norm_ffn_w, v_norm_final_w),
        after=join_i[2][:1])
    grads["w_in"], = _join_wait("in", join_i, ng)
    update("w_in")

    loss = ng[3, 0]

    def pack(small, per_weight):
        lead = lambda a: a.reshape((1,) + a.shape)
        return (small[0:1], lead(per_weight["w_in"]), lead(per_weight["w_out"]), small[1:2],
                lead(per_weight["w_gate"]), lead(per_weight["w_up"]), lead(per_weight["w_down"]), small[2])

    return (loss, grad_x.reshape(1, s, d),
            *pack(ng, {k: new[k][3] for k in names}),
            *pack(nd, {k: new[k][0] for k in names}),
            *pack(nm, {k: new[k][1] for k in names}),
            *pack(nv, {k: new[k][2] for k in names}))
```

```python
import functools
import math

import jax
import jax.numpy as jnp
from jax import lax
from jax.experimental import pallas as pl
from jax.experimental.pallas import tpu as pltpu

F32 = jnp.float32
BF16 = jnp.bfloat16
I32 = jnp.int32
MESH = pl.DeviceIdType.MESH
ANY = pl.BlockSpec(memory_space=pl.ANY)

ATTN_HEADS = 8
ATTN_HEAD_DIM = 128
RET_HEADS = 4
RET_HEAD_DIM = 256
ATTN_WIDTH = ATTN_HEADS * ATTN_HEAD_DIM
RET_WIDTH = RET_HEADS * RET_HEAD_DIM
DILATED_PATTERNS = ((128, 1), (512, 4), (2048, 16))
NORM_EPS = 1e-6
ADAM_LR = 0.001
ADAM_B1 = 0.9
ADAM_B2 = 0.999
ADAM_EPS = 1e-08
ADAM_WD = 0.01
ADAM_STEP = 10

N_CHIPS = 4
N_DEV = 8
NEG_BIG = -1e30
SEQ_TILE = 512
ATTN_FWD_HEADS_PER_STEP = 2
ATTN_HEADS_PER_STEP = 1
VMEM_LIMIT_BYTES = 56 * 1024 * 1024


def _params(semantics=None, vmem=VMEM_LIMIT_BYTES):
    return pltpu.CompilerParams(dimension_semantics=semantics, vmem_limit_bytes=vmem)


def _row_tile(rows, row_bytes, limit=2 * 1024 * 1024, mult=16):
    best = None
    for t in range(mult, rows + 1, mult):
        if rows % t == 0 and t * row_bytes <= limit:
            best = t
    assert best is not None, (rows, row_bytes)
    return best


def _sigmoid(x):
    return 1.0 / (1.0 + jnp.exp(-x))


def _select_by_index(idx, values):
    out = jnp.float32(values[-1])
    for i in range(len(values) - 2, -1, -1):
        out = jnp.where(idx == i, jnp.float32(values[i]), out)
    return out


def _place():
    x, y, c = lax.axis_index("x"), lax.axis_index("y"), lax.axis_index("c")
    return x, y, c


def _cast_into_full(w, shard_arr, column_sharded, name, after=()):
    after = tuple(after)
    rows, cols = w.shape
    tr = _row_tile(rows, cols * 4)
    steps = rows // tr
    if column_sharded:
        out_shape, out_map = (rows, N_CHIPS * cols), (lambda i, s_ref: (i, s_ref[0]))
    else:
        out_shape, out_map = (N_CHIPS * rows, cols), (lambda i, s_ref: (s_ref[0] * steps + i, 0))

    def body(s_ref, w_ref, *rest):
        del s_ref
        rest[-1][...] = w_ref[...].astype(BF16)

    grid_spec = pltpu.PrefetchScalarGridSpec(
        num_scalar_prefetch=1, grid=(steps,),
        in_specs=[pl.BlockSpec((tr, cols), lambda i, s_ref: (i, 0))] + [ANY] * len(after),
        out_specs=pl.BlockSpec((tr, cols), out_map))
    return pl.pallas_call(
        body, name=name, grid_spec=grid_spec,
        out_shape=jax.ShapeDtypeStruct(out_shape, BF16),
        compiler_params=_params(("parallel",)),
    )(shard_arr, w, *after)


def _rms_fwd(x, w, name, after=()):
    rows, d = x.shape
    tr = 256
    after = tuple(after)

    def body(x_ref, w_ref, *rest):
        xv = x_ref[...]
        r = lax.rsqrt(jnp.mean(xv * xv, axis=-1, keepdims=True) + NORM_EPS)
        rest[-1][...] = (xv * r * w_ref[...]).astype(BF16)

    return pl.pallas_call(
        body, name=name, grid=(rows // tr,),
        in_specs=[pl.BlockSpec((tr, d), lambda i: (i, 0)), pl.BlockSpec((1, d), lambda i: (0, 0))]
        + [ANY] * len(after),
        out_specs=pl.BlockSpec((tr, d), lambda i: (i, 0)),
        out_shape=jax.ShapeDtypeStruct((rows, d), BF16),
        compiler_params=_params(("parallel",)),
    )(x, w, *after)


def _rms_bwd(x, w, dh, dres, name, after=()):
    rows, d = x.shape
    tr = 256
    after = tuple(after)

    def body(x_ref, w_ref, dh_ref, dres_ref, *rest):
        dx_ref, dxb_ref, dw_ref = rest[len(after):]
        xv = x_ref[...]
        r = lax.rsqrt(jnp.mean(xv * xv, axis=-1, keepdims=True) + NORM_EPS)
        xhat = xv * r
        dy = dh_ref[...]
        dxhat = dy * w_ref[...]
        dx = dres_ref[...] + r * (dxhat - xhat * jnp.mean(dxhat * xhat, axis=-1, keepdims=True))
        dx_ref[...] = dx
        dxb_ref[...] = dx.astype(BF16)
        part = jnp.sum(dy * xhat, axis=0, keepdims=True)

        @pl.when(pl.program_id(0) == 0)
        def _():
            dw_ref[...] = part

        @pl.when(pl.program_id(0) != 0)
        def _():
            dw_ref[...] += part

    row = pl.BlockSpec((tr, d), lambda i: (i, 0))
    vec = pl.BlockSpec((1, d), lambda i: (0, 0))
    return pl.pallas_call(
        body, name=name, grid=(rows // tr,),
        in_specs=[row, vec, row, row] + [ANY] * len(after),
        out_specs=[row, row, vec],
        out_shape=[jax.ShapeDtypeStruct((rows, d), F32), jax.ShapeDtypeStruct((rows, d), BF16),
                   jax.ShapeDtypeStruct((1, d), F32)],
        compiler_params=_params(("arbitrary",)),
    )(x, w, dh, dres, *after)


def _final_norm_loss(x2, w, target, name):
    rows, d = x2.shape
    tr = 256

    def body(x_ref, w_ref, t_ref, loss_ref, dx_ref, dxb_ref, dw_ref):
        xv = x_ref[...]
        wv = w_ref[...]
        r = lax.rsqrt(jnp.mean(xv * xv, axis=-1, keepdims=True) + NORM_EPS)
        xhat = xv * r
        err = xhat * wv - t_ref[...]
        part_loss = 0.5 * jnp.sum(jnp.mean(err * err, axis=-1, keepdims=True), axis=0, keepdims=True)
        dy = err * (1.0 / d)
        dxhat = dy * wv
        dx = r * (dxhat - xhat * jnp.mean(dxhat * xhat, axis=-1, keepdims=True))
        dx_ref[...] = dx
        dxb_ref[...] = dx.astype(BF16)
        part_dw = jnp.sum(dy * xhat, axis=0, keepdims=True)
        part_loss = jnp.broadcast_to(part_loss, (1, 128))

        @pl.when(pl.program_id(0) == 0)
        def _():
            dw_ref[...] = part_dw
            loss_ref[...] = part_loss

        @pl.when(pl.program_id(0) != 0)
        def _():
            dw_ref[...] += part_dw
            loss_ref[...] += part_loss

    row = pl.BlockSpec((tr, d), lambda i: (i, 0))
    vec = pl.BlockSpec((1, d), lambda i: (0, 0))
    return pl.pallas_call(
        body, name=name, grid=(rows // tr,),
        in_specs=[row, vec, row],
        out_specs=[pl.BlockSpec((1, 128), lambda i: (0, 0)), row, row, vec],
        out_shape=[jax.ShapeDtypeStruct((1, 128), F32), jax.ShapeDtypeStruct((rows, d), F32),
                   jax.ShapeDtypeStruct((rows, d), BF16), jax.ShapeDtypeStruct((1, d), F32)],
        compiler_params=_params(("arbitrary",)),
    )(x2, w, target)


def _adamw_math(w, g, m, v):
    m = ADAM_B1 * m + (1.0 - ADAM_B1) * g
    v = ADAM_B2 * v + (1.0 - ADAM_B2) * (g * g)
    m_hat = m / (1.0 - ADAM_B1 ** ADAM_STEP)
    v_hat = v / (1.0 - ADAM_B2 ** ADAM_STEP)
    delta = -ADAM_LR * (m_hat / (jnp.sqrt(v_hat) + ADAM_EPS) + ADAM_WD * w)
    return delta, m, v


def _adamw(w, g, m, v, name):
    rows, cols = w.shape
    tr = _row_tile(rows, cols * 4)

    def body(w_ref, g_ref, m_ref, v_ref, d_ref, mo_ref, vo_ref, go_ref):
        g = g_ref[...]
        delta, m_new, v_new = _adamw_math(w_ref[...], g, m_ref[...], v_ref[...])
        d_ref[...] = delta
        mo_ref[...] = m_new
        vo_ref[...] = v_new
        go_ref[...] = g

    blk = pl.BlockSpec((tr, cols), lambda i: (i, 0))
    shp = jax.ShapeDtypeStruct((rows, cols), F32)
    return pl.pallas_call(
        body, name=name, grid=(rows // tr,),
        in_specs=[blk] * 4, out_specs=[blk] * 4, out_shape=[shp] * 4,
        compiler_params=_params(("parallel",)),
    )(w, g, m, v)


_DOT_DIMS = {"nn": ((1,), (0,)), "nt": ((1,), (1,)), "tn": ((0,), (0,))}


def _matmul(name, mode, a_list, b_list, acc_of, m, n, k, tm, tn, tk, extras, out_dtypes, epilogue,
            a_koff=None, b_koff=None, after=(), a_specs=None, b_specs=None, a_single_buffer=False,
            side_ins=(), side_fn=None, side_n_out=0):
    after = tuple(after)
    side_ins = tuple(side_ins)
    assert m % tm == 0 and n % tn == 0 and k % tk == 0, (name, m, n, k, tm, tn, tk)
    nk = k // tk
    n_acc = max(acc_of) + 1
    n_pairs = len(a_list)
    a_koff = a_koff or [0] * n_pairs
    b_koff = b_koff or [0] * n_pairs
    dims = (_DOT_DIMS[mode], ((), ()))
    n_ext, n_out = len(extras), len(out_dtypes)
    n_side = len(side_ins)

    def body(*refs):
        a_refs = refs[:n_pairs]
        b_refs = refs[n_pairs:2 * n_pairs]
        e_refs = refs[2 * n_pairs:2 * n_pairs + n_ext]
        first_side = 2 * n_pairs + n_ext + len(after)
        first_out = first_side + n_side
        o_refs = refs[first_out:first_out + n_out]
        acc_refs = refs[first_out + n_out + side_n_out:]
        if n_side:
            side_out = side_fn(*[r[...] for r in refs[first_side:first_out]])
            for o_ref, o in zip(refs[first_out + n_out:first_out + n_out + side_n_out], side_out):
                o_ref[...] = o

        parts = [None] * n_acc
        for p in range(n_pairs):
            d = lax.dot_general(a_refs[p][...], b_refs[p][...], dims, preferred_element_type=F32)
            parts[acc_of[p]] = d if parts[acc_of[p]] is None else parts[acc_of[p]] + d

        def finish(accs):
            outs = epilogue(accs, [e[...] for e in e_refs])
            for o_ref, o in zip(o_refs, outs):
                o_ref[...] = o.astype(o_ref.dtype)

        if nk == 1:
            finish(parts)
        else:
            kk = pl.program_id(2)

            @pl.when(kk == 0)
            def _():
                for acc_ref, part in zip(acc_refs, parts):
                    acc_ref[...] = part

            @pl.when(kk != 0)
            def _():
                for acc_ref, part in zip(acc_refs, parts):
                    acc_ref[...] += part

            @pl.when(kk == nk - 1)
            def _():
                finish([acc_ref[...] for acc_ref in acc_refs])

    def a_spec(off):
        mode_a = pl.Buffered(1) if a_single_buffer else None
        if mode == "tn":
            return pl.BlockSpec((tk, tm), lambda i, j, kk: (kk + off, i), pipeline_mode=mode_a)
        return pl.BlockSpec((tm, tk), lambda i, j, kk: (i, kk + off), pipeline_mode=mode_a)

    def b_spec(off):
        if mode == "nt":
            return pl.BlockSpec((tn, tk), lambda i, j, kk: (j, kk + off))
        return pl.BlockSpec((tk, tn), lambda i, j, kk: (kk + off, j))

    tile = pl.BlockSpec((tm, tn), lambda i, j, kk: (i, j))
    scratch = [pltpu.VMEM((tm, tn), F32) for _ in range(n_acc)] if nk > 1 else []
    side_specs, side_shapes = [], []
    if n_side:
        steps_j = n // tn
        rows, cols = side_ins[0].shape
        side_rows = rows // ((m // tm) * steps_j)
        assert nk == 1 and side_rows * (m // tm) * steps_j == rows and side_rows % 8 == 0, (name, rows)
        side_block = pl.BlockSpec((side_rows, cols), lambda i, j, kk: (i * steps_j + j, 0))
        side_specs = [side_block] * n_side
        side_shapes = [jax.ShapeDtypeStruct((rows, cols), F32)] * side_n_out
    return pl.pallas_call(
        body, name=name, grid=(m // tm, n // tn, nk),
        in_specs=(a_specs or [a_spec(o) for o in a_koff]) + (b_specs or [b_spec(o) for o in b_koff])
        + [tile] * n_ext + [ANY] * len(after) + side_specs,
        out_specs=[tile] * n_out + side_specs[:1] * side_n_out,
        out_shape=[jax.ShapeDtypeStruct((m, n), dt) for dt in out_dtypes] + side_shapes,
        scratch_shapes=scratch,
        compiler_params=_params(("parallel", "parallel", "arbitrary")),
    )(*a_list, *b_list, *extras, *after, *side_ins)


def _epi_plain(accs, extras):
    return (accs[0],)


def _epi_residual(accs, extras):
    return (accs[0] + extras[0],)


def _epi_two(accs, extras):
    return accs[0], accs[1]


def _epi_swiglu(accs, extras):
    g, u = accs
    return g, u, g * _sigmoid(g) * u


def _epi_swiglu_bwd(accs, extras):
    da = accs[0]
    g, u = (e.astype(F32) for e in extras)
    sg = _sigmoid(g)
    dg = da * u * sg * (1.0 + g * (1.0 - sg))
    du = da * g * sg
    return dg, du


_NT_DIMS = (((1,), (1,)), ((), ()))
_TN_DIMS = (((0,), (0,)), ((), ()))


def _tile_delta(tq, tk):
    return lax.broadcasted_iota(I32, (tq, tk), 0) - lax.broadcasted_iota(I32, (tq, tk), 1)


def _attn_log_count(delta):
    count = jnp.zeros(delta.shape, I32)
    for window, dilation in DILATED_PATTERNS:
        hit = ((delta & (dilation - 1)) == 0) & (delta <= window)
        count = count + jnp.where(hit, 1, 0)
    valid = (delta >= 0) & (count > 0)
    logm = jnp.where(count == 3, math.log(3.0), jnp.where(count == 2, math.log(2.0), 0.0))
    return jnp.where(valid, logm, NEG_BIG)


def _fill_attn_log_count(tab_ref):
    nb, t, _ = tab_ref.shape
    base = _tile_delta(t, t)
    for b in range(nb):
        tab_ref[b] = _attn_log_count(base + b * t)


def _fill_attn_bias(tab_ref, log_count_ref, slope):
    nb, t, _ = tab_ref.shape
    dist = _tile_delta(t, t).astype(F32)
    for b in range(nb):
        tab_ref[b] = log_count_ref[b] - slope * (dist + float(b * t))


def _fill_ret_decay(tab_ref, log_gamma):
    nb, t, _ = tab_ref.shape
    base = _tile_delta(t, t)
    for b in range(nb):
        tab_ref[b] = _ret_decay(base + b * t, log_gamma)


def _alibi_slopes():
    return [2.0 ** (-8.0 * (h + 1) / ATTN_HEADS) for h in range(ATTN_HEADS)]


def _attn_fwd(proj, after=()):
    s = proj.shape[0]
    t = SEQ_TILE
    hd = ATTN_HEAD_DIM
    hp = ATTN_FWD_HEADS_PER_STEP
    ng = ATTN_HEADS // hp
    w = hp * hd
    scale = 1.0 / math.sqrt(hd)
    slopes = _alibi_slopes()

    def body(q_ref, k_ref, v_ref, *rest):
        mix_ref, o_ref, lse_ref, kb, vb, bias_tab, log_count_tab = rest[len(after):]
        g = pl.program_id(0)
        i = pl.program_id(1)

        @pl.when((g == 0) & (i == 0))
        def _():
            _fill_attn_log_count(log_count_tab)

        @pl.when(i == 0)
        def _():
            kb[...] = k_ref[...].astype(BF16)
            vb[...] = v_ref[...].astype(BF16)
            for u in range(hp):
                _fill_attn_bias(bias_tab.at[u], log_count_tab, _select_by_index(g * hp + u, slopes))

        qs = [q_ref[:, u * hd:(u + 1) * hd].astype(BF16) for u in range(hp)]

        def step(j, carry):
            rows = pl.ds(pl.multiple_of(j * t, t), t)
            out = []
            for u in range(hp):
                m_i, l_i, acc = carry[u]
                lanes = slice(u * hd, (u + 1) * hd)
                sc = lax.dot_general(qs[u], kb[rows, lanes], _NT_DIMS, preferred_element_type=F32) * scale
                sc = sc + bias_tab[u, i - j]
                m_new = jnp.maximum(m_i, jnp.max(sc, axis=-1, keepdims=True))
                p = jnp.exp(sc - m_new)
                alpha = jnp.exp(m_i - m_new)
                l_new = alpha * l_i + jnp.sum(p, axis=-1, keepdims=True)
                acc = alpha * acc + jnp.dot(p.astype(BF16), vb[rows, lanes], preferred_element_type=F32)
                out.append((m_new, l_new, acc))
            return tuple(out)

        init = (jnp.full((t, 1), NEG_BIG, F32), jnp.zeros((t, 1), F32), jnp.zeros((t, hd), F32))
        final = lax.fori_loop(0, i + 1, step, (init,) * hp)
        for u in range(hp):
            m_i, l_i, acc = final[u]
            lanes = slice(u * hd, (u + 1) * hd)
            out = acc / l_i
            o_ref[:, lanes] = out
            mix_ref[:, lanes] = out.astype(BF16)
            lse_ref[:, lanes] = jnp.broadcast_to(m_i + jnp.log(l_i), (t, hd))

    return pl.pallas_call(
        body, name="attn_fwd", grid=(ng, s // t),
        in_specs=[pl.BlockSpec((t, w), lambda g, i: (i, g)),
                  pl.BlockSpec((s, w), lambda g, i: (0, ng + g)),
                  pl.BlockSpec((s, w), lambda g, i: (0, 2 * ng + g))] + [ANY] * len(after),
        out_specs=[pl.BlockSpec((None, t, w), lambda g, i: (0, i, g))] + [pl.BlockSpec((t, w), lambda g, i: (i, g))] * 2,
        out_shape=[jax.ShapeDtypeStruct((2, s, ATTN_WIDTH), BF16),
                   jax.ShapeDtypeStruct((s, ATTN_WIDTH), F32),
                   jax.ShapeDtypeStruct((s, ATTN_WIDTH), F32)],
        scratch_shapes=[pltpu.VMEM((s, w), BF16), pltpu.VMEM((s, w), BF16), pltpu.VMEM((hp, s // t, t, t), F32),
                        pltpu.VMEM((s // t, t, t), F32)],
        compiler_params=_params(("arbitrary", "arbitrary")),
    )(proj, proj, proj, *after)


def _attn_bwd(proj, attn_out, lse, dmixed, after=()):
    after = tuple(after)
    s = proj.shape[0]
    t = SEQ_TILE
    nt = s // t
    hd = ATTN_HEAD_DIM
    hp = ATTN_HEADS_PER_STEP
    ng = ATTN_HEADS // hp
    w = hp * hd
    scale = 1.0 / math.sqrt(hd)
    slopes = _alibi_slopes()

    def body(q_ref, k_ref, v_ref, o_ref, lse_ref, do_ref, *rest):
        dsec_ref, qb, kb, vb, dob, dsum, dq_acc, bias_tab, log_count_tab = rest[len(after):]
        g = pl.program_id(0)

        @pl.when(g == 0)
        def _():
            _fill_attn_log_count(log_count_tab)

        qb[...] = q_ref[...].astype(BF16)
        kb[...] = k_ref[...].astype(BF16)
        vb[...] = v_ref[...].astype(BF16)
        dob[...] = do_ref[...].astype(BF16)
        for u in range(hp):
            lanes = slice(u * hd, (u + 1) * hd)
            _fill_attn_bias(bias_tab.at[u], log_count_tab, _select_by_index(g * hp + u, slopes))
            rowsum = jnp.sum(do_ref[:, lanes] * o_ref[:, lanes], axis=-1, keepdims=True)
            dsum[:, lanes] = jnp.broadcast_to(rowsum, (s, hd))
        dq_acc[...] = jnp.zeros((s, w), F32)

        def over_keys(j, _):
            krows = pl.ds(pl.multiple_of(j * t, t), t)

            def over_queries(i, carry):
                qrows = pl.ds(pl.multiple_of(i * t, t), t)
                out = []
                for u in range(hp):
                    dk, dv = carry[u]
                    lanes = slice(u * hd, (u + 1) * hd)
                    qi, doi = qb[qrows, lanes], dob[qrows, lanes]
                    kj, vj = kb[krows, lanes], vb[krows, lanes]
                    lse_i = lse_ref[qrows, lanes][:, :1]
                    dsum_i = dsum[qrows, lanes][:, :1]
                    sc = lax.dot_general(qi, kj, _NT_DIMS, preferred_element_type=F32) * scale
                    p = jnp.exp(sc + bias_tab[u, i - j] - lse_i)
                    dp = lax.dot_general(doi, vj, _NT_DIMS, preferred_element_type=F32)
                    ds = (p * (dp - dsum_i)).astype(BF16)
                    dv = dv + lax.dot_general(p.astype(BF16), doi, _TN_DIMS, preferred_element_type=F32)
                    dk = dk + lax.dot_general(ds, qi, _TN_DIMS, preferred_element_type=F32)
                    dq_acc[qrows, lanes] += jnp.dot(ds, kj, preferred_element_type=F32)
                    out.append((dk, dv))
                return tuple(out)

            zero = jnp.zeros((t, hd), F32)
            final = lax.fori_loop(j, nt, over_queries, ((zero, zero),) * hp)
            for u in range(hp):
                lanes = slice(u * hd, (u + 1) * hd)
                dsec_ref[1, krows, lanes] = (final[u][0] * scale).astype(BF16)
                dsec_ref[2, krows, lanes] = final[u][1].astype(BF16)
            return 0

        lax.fori_loop(0, nt, over_keys, 0)
        dsec_ref[0] = (dq_acc[...] * scale).astype(BF16)

    def col(off):
        return pl.BlockSpec((s, w), lambda g: (0, off + g))

    return pl.pallas_call(
        body, name="attn_bwd", grid=(ng,),
        in_specs=[col(0), col(ng), col(2 * ng), col(0), col(0), col(0)] + [ANY] * len(after),
        out_specs=pl.BlockSpec((4, s, w), lambda g: (0, 0, g)),
        out_shape=jax.ShapeDtypeStruct((8, s, ATTN_WIDTH), BF16),
        scratch_shapes=[pltpu.VMEM((s, w), BF16)] * 4 + [pltpu.VMEM((s, w), F32)] * 2
        + [pltpu.VMEM((hp, nt, t, t), F32), pltpu.VMEM((nt, t, t), F32)],
        compiler_params=_params(("arbitrary",)),
    )(proj, proj, proj, attn_out, lse, dmixed, *after)


def _ret_log_gammas():
    return [math.log(1.0 - 2.0 ** (-5.0 - h)) for h in range(RET_HEADS)]


def _ret_decay(delta, log_gamma):
    dec = jnp.exp(delta.astype(F32) * log_gamma) * (1.0 / math.sqrt(RET_HEAD_DIM))
    return jnp.where(delta >= 0, dec, 0.0)


def _ret_fwd(proj, mixed, after=()):
    after = tuple(after)
    s = proj.shape[0]
    t = SEQ_TILE
    hd = RET_HEAD_DIM
    nh = RET_HEADS
    log_gammas = _ret_log_gammas()
    c0 = 3 * ATTN_WIDTH // hd

    def body(q_ref, k_ref, v_ref, g_ref, *rest):
        mix_ref, raw_ref, kb, vb, decay_tab = rest[1 + len(after):]
        h = pl.program_id(0)
        i = pl.program_id(1)

        @pl.when(i == 0)
        def _():
            kb[...] = k_ref[...].astype(BF16)
            vb[...] = v_ref[...].astype(BF16)
            _fill_ret_decay(decay_tab, _select_by_index(h, log_gammas))

        q = q_ref[...].astype(BF16)

        def step(j, acc):
            rows = pl.ds(pl.multiple_of(j * t, t), t)
            sc = lax.dot_general(q, kb[rows, :], _NT_DIMS, preferred_element_type=F32) * decay_tab[i - j]
            return acc + jnp.dot(sc.astype(BF16), vb[rows, :], preferred_element_type=F32)

        ret = lax.fori_loop(0, i + 1, step, jnp.zeros((t, hd), F32))
        raw_ref[...] = ret
        r = lax.rsqrt(jnp.mean(ret * ret, axis=-1, keepdims=True) + NORM_EPS)
        g = g_ref[...]
        mix_ref[...] = (g * _sigmoid(g) * (ret * r)).astype(BF16)

    return pl.pallas_call(
        body, name="ret_fwd", grid=(nh, s // t),
        in_specs=[pl.BlockSpec((t, hd), lambda h, i: (i, c0 + h)),
                  pl.BlockSpec((s, hd), lambda h, i: (0, c0 + nh + h)),
                  pl.BlockSpec((s, hd), lambda h, i: (0, c0 + 2 * nh + h)),
                  pl.BlockSpec((t, hd), lambda h, i: (i, c0 + 3 * nh + h))] + [ANY] * (1 + len(after)),
        out_specs=[pl.BlockSpec((None, t, hd), lambda h, i: (1, i, h)), pl.BlockSpec((t, hd), lambda h, i: (i, h))],
        out_shape=[jax.ShapeDtypeStruct(mixed.shape, BF16), jax.ShapeDtypeStruct((s, RET_WIDTH), F32)],
        input_output_aliases={4: 0},
        scratch_shapes=[pltpu.VMEM((s, hd), BF16), pltpu.VMEM((s, hd), BF16), pltpu.VMEM((s // t, t, t), F32)],
        compiler_params=_params(("arbitrary", "arbitrary")),
    )(proj, proj, proj, proj, mixed, *after)


def _ret_bwd(proj, ret_raw, dmixed, dsec, after=()):
    after = tuple(after)
    s = proj.shape[0]
    t = SEQ_TILE
    nt = s // t
    hd = RET_HEAD_DIM
    nh = RET_HEADS
    log_gammas = _ret_log_gammas()
    c0 = 3 * ATTN_WIDTH // hd
    mixed_blocks = ATTN_WIDTH // hd

    def body(q_ref, k_ref, v_ref, g_ref, raw_ref, dmix_ref, *rest):
        dsec_ref, qb, kb, vb, dretb, dq_acc, decay_tab = rest[1 + len(after):]
        h = pl.program_id(0)
        _fill_ret_decay(decay_tab, _select_by_index(h, log_gammas))
        qb[...] = q_ref[...].astype(BF16)
        kb[...] = k_ref[...].astype(BF16)
        vb[...] = v_ref[...].astype(BF16)
        ret = raw_ref[...]
        r = lax.rsqrt(jnp.mean(ret * ret, axis=-1, keepdims=True) + NORM_EPS)
        normed = ret * r
        g = g_ref[...]
        sg = _sigmoid(g)
        dout = dmix_ref[...]
        dsec_ref[3] = (dout * normed * sg * (1.0 + g * (1.0 - sg))).astype(BF16)
        dn = dout * g * sg
        dret = r * (dn - normed * jnp.mean(dn * normed, axis=-1, keepdims=True))
        dretb[...] = dret.astype(BF16)
        dq_acc[...] = jnp.zeros((s, hd), F32)

        def over_keys(j, _):
            krows = pl.ds(pl.multiple_of(j * t, t), t)
            kj = kb[krows, :]
            vj = vb[krows, :]

            def over_queries(i, carry):
                dk, dv = carry
                qrows = pl.ds(pl.multiple_of(i * t, t), t)
                qi = qb[qrows, :]
                doi = dretb[qrows, :]
                dec = decay_tab[i - j]
                a = (lax.dot_general(qi, kj, _NT_DIMS, preferred_element_type=F32) * dec).astype(BF16)
                da = (lax.dot_general(doi, vj, _NT_DIMS, preferred_element_type=F32) * dec).astype(BF16)
                dv = dv + lax.dot_general(a, doi, _TN_DIMS, preferred_element_type=F32)
                dk = dk + lax.dot_general(da, qi, _TN_DIMS, preferred_element_type=F32)
                dq_acc[qrows, :] += jnp.dot(da, kj, preferred_element_type=F32)
                return dk, dv

            zero = jnp.zeros((t, hd), F32)
            dk, dv = lax.fori_loop(j, nt, over_queries, (zero, zero))
            dsec_ref[1, krows, :] = dk.astype(BF16)
            dsec_ref[2, krows, :] = dv.astype(BF16)
            return 0

        lax.fori_loop(0, nt, over_keys, 0)
        dsec_ref[0] = dq_acc[...].astype(BF16)

    def col(off):
        return pl.BlockSpec((s, hd), lambda h: (0, off + h))

    return pl.pallas_call(
        body, name="ret_bwd", grid=(nh,),
        in_specs=[col(c0), col(c0 + nh), col(c0 + 2 * nh), col(c0 + 3 * nh), col(0), col(mixed_blocks)]
        + [ANY] * (1 + len(after)),
        out_specs=pl.BlockSpec((4, s, hd), lambda h: (1, 0, h)),
        out_shape=jax.ShapeDtypeStruct(dsec.shape, BF16),
        input_output_aliases={6: 0},
        scratch_shapes=[pltpu.VMEM((s, hd), BF16)] * 4 + [pltpu.VMEM((s, hd), F32)]
        + [pltpu.VMEM((nt, t, t), F32)],
        compiler_params=_params(("arbitrary",)),
    )(proj, proj, proj, proj, ret_raw, dmixed, dsec, *after)


_FLIPS = (2, 1, 3)


def _other_chips(x, y):
    return [(1 - x, y), (x, 1 - y), (1 - x, 1 - y)]


_HBM = pl.BlockSpec(memory_space=pltpu.HBM)
_SEM = pl.BlockSpec(memory_space=pltpu.SEMAPHORE)
_EFFECT = pltpu.SideEffectType.DATAFLOW_SIDE_EFFECTING


def _in_hbm(a):
    return pltpu.with_memory_space_constraint(a, pltpu.HBM)


def _weight_view(w, column_sharded):
    if column_sharded:
        return w.reshape(2, w.shape[0] // 2, w.shape[1])
    return w.reshape(N_CHIPS, 2, w.shape[0] // (2 * N_CHIPS), w.shape[1])


def _weight_unview(v):
    if v.ndim == 3:
        return v.reshape(2 * v.shape[1], v.shape[2])
    return v.reshape(N_CHIPS * 2 * v.shape[2], v.shape[3])


def _weight_region(buf, shard, half):
    if len(buf.shape) == 3:
        cols = buf.shape[2] // N_CHIPS
        return buf.at[half, :, pl.ds(shard * cols, cols)]
    return buf.at[shard, half]


def _remote(where, send_sem, recv_sem, to):
    return pltpu.make_async_remote_copy(src_ref=where, dst_ref=where, send_sem=send_sem, recv_sem=recv_sem,
                                        device_id=to, device_id_type=MESH)


def _for_my_shard(fn):
    x, y, _ = _place()
    for ss in range(N_CHIPS):
        pl.when(2 * x + y == ss)(functools.partial(fn, ss))


def _gather_forward(views, which, send_sems, recv_sems, after, name, base=0):
    n_w = len(views)
    which = [base // 3 + w for w in which] if base % 3 == 0 else None
    assert which is not None, "base must be a multiple of 3"

    def body(*refs):
        send_in, recv_in = refs[n_w:n_w + 2]
        fwd_send, fwd_recv = refs[n_w + 3:n_w + 5]
        bufs = refs[n_w + 5:]
        x, y, c = _place()
        sibling = (x, y, 1 - c)

        def forward(ss):
            for i, w in enumerate(which):
                for j in range(3):
                    landed = _weight_region(bufs[i], ss ^ _FLIPS[j], c)
                    _remote(landed, send_in.at[3 * w + j], recv_in.at[3 * w + j], sibling).wait_recv()
                    _remote(landed, fwd_send.at[3 * i + j], fwd_recv.at[3 * i + j], sibling).start()

        _for_my_shard(forward)
        for i, w in enumerate(which):
            for j in range(3):
                _remote(_weight_region(bufs[i], 0, 0), send_in.at[3 * w + j], recv_in.at[3 * w + j],
                        sibling).wait_send()

    return pl.pallas_call(
        body, name=name,
        in_specs=[_HBM] * n_w + [_SEM, _SEM, ANY], out_specs=[_SEM, _SEM] + [_HBM] * n_w,
        out_shape=[pltpu.SemaphoreType.DMA((3 * n_w,)), pltpu.SemaphoreType.DMA((3 * n_w,))]
        + [pltpu.HBM(v.shape, BF16) for v in views],
        input_output_aliases={w: 2 + w for w in range(n_w)},
        compiler_params=pltpu.CompilerParams(has_side_effects=_EFFECT),
    )(*views, send_sems, recv_sems, after)


def _gather_end(views, fwd_send, fwd_recv, after, name):
    n_w = len(views)

    def body(*refs):
        fwd_send_ref, fwd_recv_ref = refs[n_w:n_w + 2]
        bufs = refs[n_w + 3:]
        x, y, c = _place()
        for i in range(n_w):
            for j in range(3):
                cp = _remote(_weight_region(bufs[i], 0, 0), fwd_send_ref.at[3 * i + j], fwd_recv_ref.at[3 * i + j],
                             (x, y, 1 - c))
                cp.wait_recv()
                cp.wait_send()

    outs = pl.pallas_call(
        body, name=name,
        in_specs=[_HBM] * n_w + [_SEM, _SEM, ANY], out_specs=[_HBM] * n_w,
        out_shape=[pltpu.HBM(v.shape, BF16) for v in views],
        input_output_aliases={w: w for w in range(n_w)},
        compiler_params=pltpu.CompilerParams(has_side_effects=_EFFECT),
    )(*views, fwd_send, fwd_recv, after)
    return [_weight_unview(o) for o in outs]


def _comm_call(name, bufs, sem_pairs, after, n_new, fn):
    n, n_sem, after = len(bufs), 2 * len(sem_pairs), tuple(after)
    n_out_sem = 2 if n_new else 0

    def body(*refs):
        sems = refs[n:n + n_sem]
        outs = refs[n + n_sem + len(after):]
        new = outs[:n_out_sem] if n_new else (None, None)
        fn(outs[n_out_sem:], [(sems[2 * i], sems[2 * i + 1]) for i in range(len(sem_pairs))], *new)

    res = pl.pallas_call(
        body, name=name,
        in_specs=[_HBM] * n + [_SEM] * n_sem + [ANY] * len(after),
        out_specs=[_SEM] * n_out_sem + [_HBM] * n,
        out_shape=[pltpu.SemaphoreType.DMA((n_new,))] * n_out_sem + [pltpu.HBM(b.shape, b.dtype) for b in bufs],
        input_output_aliases={i: n_out_sem + i for i in range(n)},
        compiler_params=pltpu.CompilerParams(has_side_effects=_EFFECT),
    )(*bufs, *[s for pair in sem_pairs for s in pair], *after)
    return list(res[:n_out_sem]), list(res[n_out_sem:])


def _quarter(piece, q):
    rows = piece.shape[0] // 2
    return piece.at[pl.ds(q * rows, rows)]


def _gather_in_start(view, name):
    def fn(bufs, _, send, recv):
        x, y, c = _place()

        def go(ss):
            for j, chip in enumerate(_other_chips(x, y)[:2]):
                _remote(_weight_region(bufs[0], ss, c), send.at[j], recv.at[j], (*chip, c)).start()

        _for_my_shard(go)

    sems, (view,) = _comm_call(name, [_in_hbm(view)], [], (), 2, fn)
    return sems, view


def _gather_out_gate_start(v_out, v_gate, after, name):
    def fn(bufs, _, send, recv):
        x, y, c = _place()
        chips = _other_chips(x, y)

        def go(ss):
            for j in range(3):
                _remote(_weight_region(bufs[0], ss, c), send.at[j], recv.at[j], (*chips[j], c)).start()
            for j in range(2):
                _remote(_weight_region(bufs[1], ss, c), send.at[3 + j], recv.at[3 + j], (*chips[j], c)).start()

        _for_my_shard(go)

    sems, views = _comm_call(name, [_in_hbm(v_out), _in_hbm(v_gate)], [], after, 5, fn)
    return sems, views


def _gather_relay(view, started, base, after, name, then=None, then_peers=0):
    n_new = 6 + then_peers if then_peers else 4

    def fn(bufs, pairs, send, recv):
        (send_in, recv_in), = pairs
        x, y, c = _place()
        chips = _other_chips(x, y)
        sibling = (x, y, 1 - c)

        def go(ss):
            landed = [_weight_region(bufs[0], ss ^ _FLIPS[j], c) for j in range(2)]
            for j in range(2):
                _remote(landed[j], send_in.at[base + j], recv_in.at[base + j], sibling).wait_recv()
            for j in range(2):
                _remote(_quarter(landed[j], j), send.at[j], recv.at[j], (*chips[1 - j], c)).start()
            for j in range(2):
                _remote(landed[j], send.at[2 + j], recv.at[2 + j], sibling).start()
            for j in range(then_peers):
                _remote(_weight_region(bufs[1], ss, c), send.at[6 + j], recv.at[6 + j], (*chips[j], c)).start()

        _for_my_shard(go)
        for j in range(2):
            _remote(_weight_region(bufs[0], 0, 0), send_in.at[base + j], recv_in.at[base + j], sibling).wait_send()

    views = [view] if then is None else [view, _in_hbm(then)]
    sems, views = _comm_call(name, views, [started], after, n_new, fn)
    return sems, views


def _gather_in_neighbours_end(view, relayed, after, name):
    def fn(bufs, pairs, *_):
        (send, recv), = pairs
        x, y, c = _place()
        for j in range(2):
            cp = _remote(_weight_region(bufs[0], 0, 0), send.at[2 + j], recv.at[2 + j], (x, y, 1 - c))
            cp.wait_recv()
            cp.wait_send()

    _, (view,) = _comm_call(name, [view], [relayed], after, 0, fn)
    return view


def _gather_in_diagonal(view, relayed, after, name):
    def fn(bufs, pairs, send, recv):
        (send_in, recv_in), = pairs
        x, y, c = _place()
        sibling = (x, y, 1 - c)
        any_quarter = _quarter(_weight_region(bufs[0], 0, 0), 0)
        for j in range(2):
            cp = _remote(any_quarter, send_in.at[j], recv_in.at[j], sibling)
            cp.wait_recv()
            cp.wait_send()

        def go(ss):
            _remote(_weight_region(bufs[0], ss ^ _FLIPS[2], c), send.at[0], recv.at[0], sibling).start()

        _for_my_shard(go)

    sems, (view,) = _comm_call(name, [view], [relayed], after, 1, fn)
    return sems, view


def _gather_in_diagonal_end(view, forwarded, after, name):
    def fn(bufs, pairs, *_):
        (send, recv), = pairs
        x, y, c = _place()
        cp = _remote(_weight_region(bufs[0], 0, 0), send.at[0], recv.at[0], (x, y, 1 - c))
        cp.wait_recv()
        cp.wait_send()

    _, (view,) = _comm_call(name, [view], [forwarded], after, 0, fn)
    return view


def _in_proj_shard(h1, wi, proj, shard_arr, name):
    s, d = h1.shape
    n = wi.shape[1]
    tn = 256
    blocks = n // (N_CHIPS * tn)
    given = [] if proj is None else [proj]

    def body(shard_ref, h_ref, w_ref, *rest):
        del shard_ref
        rest[-1][...] = jnp.dot(h_ref[...], w_ref[...], preferred_element_type=F32)

    grid_spec = pltpu.PrefetchScalarGridSpec(
        num_scalar_prefetch=1, grid=(blocks,),
        in_specs=[pl.BlockSpec((s, d), lambda j, shard_ref: (0, 0)),
                  pl.BlockSpec((d, tn), lambda j, shard_ref: (0, shard_ref[0] * blocks + j))] + [ANY] * len(given),
        out_specs=pl.BlockSpec((s, tn), lambda j, shard_ref: (0, shard_ref[0] * blocks + j)))
    return pl.pallas_call(
        body, name=name, grid_spec=grid_spec,
        out_shape=jax.ShapeDtypeStruct((s, n), F32),
        input_output_aliases={3: 0} if given else {},
        compiler_params=_params(("arbitrary",)),
    )(shard_arr, h1, wi, *given)


def _split_start(name, bufs, n_sems, copies):
    n = len(bufs)

    def body(*refs):
        send_sems, recv_sems = refs[n:n + 2]
        for cp in copies(refs[n + 2:], send_sems, recv_sems):
            cp.start()

    outs = pl.pallas_call(
        body, name=name,
        in_specs=[_HBM] * n, out_specs=[_SEM, _SEM] + [_HBM] * n,
        out_shape=[pltpu.SemaphoreType.DMA((n_sems,)), pltpu.SemaphoreType.DMA((n_sems,))]
        + [pltpu.HBM(b.shape, b.dtype) for b in bufs],
        input_output_aliases={i: 2 + i for i in range(n)},
        compiler_params=pltpu.CompilerParams(has_side_effects=_EFFECT),
    )(*[_in_hbm(b) for b in bufs])
    return outs[0], outs[1], list(outs[2:])


def _split_wait(name, bufs, send_sems, recv_sems, copies, after):
    n = len(bufs)

    def body(*refs):
        send_ref, recv_ref = refs[n:n + 2]
        for cp in copies(refs[n + 3:], send_ref, recv_ref):
            cp.wait()

    return list(pl.pallas_call(
        body, name=name,
        in_specs=[_HBM] * n + [_SEM, _SEM, ANY], out_specs=[_HBM] * n,
        out_shape=[pltpu.HBM(b.shape, b.dtype) for b in bufs],
        input_output_aliases={i: i for i in range(n)},
        compiler_params=pltpu.CompilerParams(has_side_effects=_EFFECT),
    )(*bufs, send_sems, recv_sems, after))


def _halves_copies(n_w):
    def copies(bufs, send_sems, recv_sems):
        x, y, c = _place()
        out = []
        for w in range(n_w):
            view, land = bufs[w], bufs[n_w + w]
            src = view.at[1 - c] if len(view.shape) == 3 else view.at[:, 1 - c]
            out.append(pltpu.make_async_remote_copy(
                src_ref=src, dst_ref=land, send_sem=send_sems.at[w], recv_sem=recv_sems.at[w],
                device_id=(x, y, 1 - c), device_id_type=MESH))
        return out
    return copies


def _pieces_copies(n_w):
    def copies(bufs, send_sems, recv_sems):
        x, y, c = _place()
        out = []
        for w in range(n_w):
            for j, (cx, cy) in enumerate(_other_chips(x, y)):
                out.append(pltpu.make_async_remote_copy(
                    src_ref=bufs[w].at[2 * cx + cy], dst_ref=bufs[n_w + w].at[j],
                    send_sem=send_sems.at[3 * w + j], recv_sem=recv_sems.at[3 * w + j],
                    device_id=(cx, cy, c), device_id_type=MESH))
        return out
    return copies


def _join_copies(n_w):
    def copies(bufs, send_sems, recv_sems):
        x, y, c = _place()
        return [pltpu.make_async_remote_copy(
            src_ref=bufs[w].at[c], dst_ref=bufs[w].at[c], send_sem=send_sems.at[w], recv_sem=recv_sems.at[w],
            device_id=(x, y, 1 - c), device_id_type=MESH) for w in range(n_w)]
    return copies


def _halves_landing(view):
    shape = view.shape[1:] if view.ndim == 3 else (N_CHIPS,) + view.shape[2:]
    return lax.empty(shape, BF16)


def _halves_start(tag, grads, column_sharded):
    views = [_weight_view(g, cs) for g, cs in zip(grads, column_sharded)]
    n = len(views)
    return _split_start("halves_start_" + tag, views + [_halves_landing(v) for v in views], n, _halves_copies(n))


def _halves_wait(tag, state, after):
    send_sems, recv_sems, bufs = state
    n = len(bufs) // 2
    bufs = _split_wait("halves_wait_" + tag, bufs, send_sems, recv_sems, _halves_copies(n), after)
    return bufs[:n], bufs[n:]


def _pieces_start(tag, pieces):
    n = len(pieces)
    landing = [lax.empty((3,) + p.shape[1:], BF16) for p in pieces]
    return _split_start("pieces_start_" + tag, list(pieces) + landing, 3 * n, _pieces_copies(n))


def _pieces_wait(tag, state, after):
    send_sems, recv_sems, bufs = state
    n = len(bufs) // 2
    bufs = _split_wait("pieces_wait_" + tag, bufs, send_sems, recv_sems, _pieces_copies(n), after)
    return bufs[:n], bufs[n:]


def _join_start(tag, shards):
    n = len(shards)
    return _split_start("join_start_" + tag, list(shards), n, _join_copies(n))


def _join_wait(tag, state, after):
    send_sems, recv_sems, bufs = state
    bufs = _split_wait("join_wait_" + tag, bufs, send_sems, recv_sems, _join_copies(len(bufs)), after)
    return [b.reshape(2 * b.shape[1], b.shape[2]) for b in bufs]


def _chip_sum_col(g3, sib, c_arr, name):
    _, hk, n = g3.shape
    cols = n // N_CHIPS
    tr = _row_tile(hk, cols * 2, limit=4 * 1024 * 1024)

    def body(c_ref, g_ref, s_ref, o_ref):
        del c_ref
        o_ref[...] = (g_ref[...].astype(F32) + s_ref[...].astype(F32)).astype(BF16)

    grid_spec = pltpu.PrefetchScalarGridSpec(
        num_scalar_prefetch=1, grid=(N_CHIPS, hk // tr),
        in_specs=[pl.BlockSpec((None, tr, cols), lambda p, r, c_ref: (c_ref[0], r, p)),
                  pl.BlockSpec((tr, cols), lambda p, r, c_ref: (r, p))],
        out_specs=pl.BlockSpec((None, tr, cols), lambda p, r, c_ref: (p, r, 0)))
    return pl.pallas_call(
        body, name=name, grid_spec=grid_spec,
        out_shape=jax.ShapeDtypeStruct((N_CHIPS, hk, cols), BF16),
        compiler_params=_params(("parallel", "parallel")),
    )(c_arr, g3, sib)


def _chip_sum_row(g4, sib, c_arr, name):
    _, _, hr, n = g4.shape
    tr = _row_tile(hr, n * 2, limit=4 * 1024 * 1024)

    def body(c_ref, g_ref, s_ref, o_ref):
        del c_ref
        o_ref[...] = (g_ref[...].astype(F32) + s_ref[...].astype(F32)).astype(BF16)

    grid_spec = pltpu.PrefetchScalarGridSpec(
        num_scalar_prefetch=1, grid=(N_CHIPS, hr // tr),
        in_specs=[pl.BlockSpec((None, None, tr, n), lambda p, r, c_ref: (p, c_ref[0], r, 0)),
                  pl.BlockSpec((None, tr, n), lambda p, r, c_ref: (p, r, 0))],
        out_specs=pl.BlockSpec((None, tr, n), lambda p, r, c_ref: (p, r, 0)))
    return pl.pallas_call(
        body, name=name, grid_spec=grid_spec,
        out_shape=jax.ShapeDtypeStruct((N_CHIPS, hr, n), BF16),
        compiler_params=_params(("parallel", "parallel")),
    )(c_arr, g4, sib)


def _sum_pieces(pieces, received, place_arr, name):
    _, r, n = pieces.shape
    tr = _row_tile(r, n * 4, limit=4 * 1024 * 1024)

    def body(p_ref, own_ref, r0_ref, r1_ref, r2_ref, o_ref):
        del p_ref
        acc = own_ref[...].astype(F32) + r0_ref[...].astype(F32)
        acc = acc + r1_ref[...].astype(F32)
        o_ref[...] = acc + r2_ref[...].astype(F32)

    def recv_spec(j):
        return pl.BlockSpec((None, tr, n), lambda i, p_ref: (j, i, 0))

    grid_spec = pltpu.PrefetchScalarGridSpec(
        num_scalar_prefetch=1, grid=(r // tr,),
        in_specs=[pl.BlockSpec((None, tr, n), lambda i, p_ref: (p_ref[0], i, 0)),
                  recv_spec(0), recv_spec(1), recv_spec(2)],
        out_specs=pl.BlockSpec((None, tr, n), lambda i, p_ref: (p_ref[1], i, 0)))
    return pl.pallas_call(
        body, name=name, grid_spec=grid_spec,
        out_shape=jax.ShapeDtypeStruct((2, r, n), F32),
        compiler_params=_params(("parallel",)),
    )(place_arr, pieces, received, received, received)


def _norm_weights_step(parts, w, m, v, after=()):
    rows, d = parts.shape
    after = tuple(after)

    def body(p_ref, w_ref, m_ref, v_ref, *rest):
        g_ref, d_ref, mo_ref, vo_ref, gathered, send_sems, recv_sems = rest[len(after):]
        x, y, c = _place()
        me = 4 * x + 2 * y + c
        gathered[me] = p_ref[...]
        copies = []
        for k in range(1, N_DEV):
            peer = (x ^ ((k >> 2) & 1), y ^ ((k >> 1) & 1), c ^ (k & 1))
            copies.append(pltpu.make_async_remote_copy(
                src_ref=p_ref, dst_ref=gathered.at[me], send_sem=send_sems.at[k - 1],
                recv_sem=recv_sems.at[k - 1], device_id=peer, device_id_type=MESH))
        for cp in copies:
            cp.start()
        for cp in copies:
            cp.wait()
        g = gathered[0]
        for k in range(1, N_DEV):
            g = g + gathered[k]
        delta, m_new, v_new = _adamw_math(w_ref[...], g, m_ref[...], v_ref[...])
        g_ref[...] = g
        d_ref[...] = delta
        mo_ref[...] = m_new
        vo_ref[...] = v_new

    vmem = pl.BlockSpec(memory_space=pltpu.VMEM)
    shp = jax.ShapeDtypeStruct((rows, d), F32)
    return pl.pallas_call(
        body, name="norm_weights_step",
        in_specs=[vmem] * 4 + [ANY] * len(after), out_specs=[vmem] * 4, out_shape=[shp] * 4,
        scratch_shapes=[pltpu.VMEM((N_DEV, rows, d), F32), pltpu.SemaphoreType.DMA((N_DEV - 1,)),
                        pltpu.SemaphoreType.DMA((N_DEV - 1,))],
        compiler_params=pltpu.CompilerParams(has_side_effects=True),
    )(parts, w, m, v, *after)


def kernel(x, norm_mix_w, w_in, w_out, norm_ffn_w, w_gate, w_up, w_down, norm_final_w, loss_target, m_norm_mix_w, m_w_in, m_w_out, m_norm_ffn_w, m_w_gate, m_w_up, m_w_down, m_norm_final_w, v_norm_mix_w, v_w_in, v_w_out, v_norm_ffn_w, v_w_gate, v_w_up, v_w_down, v_norm_final_w):
    s, d = x.shape[1], x.shape[2]
    xs = x.reshape(s, d)
    target = loss_target.reshape(s, d)
    big = {"w_in": (w_in, m_w_in, v_w_in), "w_out": (w_out, m_w_out, v_w_out),
           "w_gate": (w_gate, m_w_gate, v_w_gate), "w_up": (w_up, m_w_up, v_w_up),
           "w_down": (w_down, m_w_down, v_w_down)}
    big = {k: tuple(a.reshape(a.shape[1:]) for a in t) for k, t in big.items()}
    col_names, row_names = ("w_in", "w_gate", "w_up"), ("w_out", "w_down")
    n_in = N_CHIPS * big["w_in"][0].shape[1]
    ffn = N_CHIPS * big["w_gate"][0].shape[1]
    mix = ATTN_WIDTH + RET_WIDTH
    c_arr = lax.axis_index("c").astype(I32).reshape(1)
    shard_arr = (2 * lax.axis_index("x") + lax.axis_index("y")).astype(I32).reshape(1)
    place_arr = jnp.concatenate([shard_arr, c_arr])

    def cast(k, after=()):
        return _weight_view(_cast_into_full(big[k][0], shard_arr, k in col_names, "cast_" + k, after), k in col_names)

    started_in, v_in = _gather_in_start(cast("w_in"), "gather_in_start")

    sec = ATTN_WIDTH

    def section(p, rows):
        return pl.BlockSpec((None, rows, sec), lambda i, j, kk: (p, i, 0))

    h1 = _rms_fwd(xs, norm_mix_w, "rms_mix_fwd", after=[v_in])
    my_shard = shard_arr[0]
    shard_of = [jnp.bitwise_xor(my_shard, f).astype(I32).reshape(1) for f in (0,) + _FLIPS]
    proj = _in_proj_shard(h1, _weight_unview(v_in), None, shard_of[0], "in_proj_own")
    early_views = [cast(k, after=[proj]) for k in ("w_out", "w_gate")]
    v_up, v_down = [cast(k, after=[proj]) for k in ("w_up", "w_down")]
    relayed_in, (v_in,) = _gather_relay(v_in, started_in, 0, early_views + [v_up, v_down], "gather_in_relay")
    started_og, (v_out, v_gate) = _gather_out_gate_start(*early_views, [v_in], "gather_out_gate_start")
    v_in = _gather_in_neighbours_end(v_in, relayed_in, [v_out], "gather_in_neighbours_end")
    proj = _in_proj_shard(h1, _weight_unview(v_in), proj, shard_of[1], "in_proj_x")
    proj = _in_proj_shard(h1, _weight_unview(v_in), proj, shard_of[2], "in_proj_y")
    forwarded_in, v_in = _gather_in_diagonal(v_in, relayed_in, [proj], "gather_in_diagonal")
    wi = _weight_unview(_gather_in_diagonal_end(v_in, forwarded_in, [proj], "gather_in_diagonal_end"))
    proj = _in_proj_shard(h1, wi, proj, shard_of[3], "in_proj_diagonal")
    fs_o, fr_o, v_out = _gather_forward([v_out], [0], *started_og, proj, "gather_forward_out")
    mixed, attn_o, lse = _attn_fwd(proj, after=[v_out])
    relayed_g, (v_gate, v_up) = _gather_relay(v_gate, started_og, 3, [attn_o], "gather_gate_relay",
                                              then=v_up, then_peers=2)
    mixed, ret_raw = _ret_fwd(proj, mixed, after=[v_gate])
    wo, = _gather_end([v_out], fs_o, fr_o, ret_raw, "gather_end_out")
    x1, = _matmul("out_proj", "nn", [mixed, mixed], [wo, wo], [0, 0], s, d, sec, s // 2, 512, sec, [xs], [F32],
                  _epi_residual, b_koff=[0, 1], a_specs=[section(0, s // 2), section(1, s // 2)])
    h2 = _rms_fwd(x1, norm_ffn_w, "rms_ffn_fwd")
    relayed_u, (v_up, v_down) = _gather_relay(v_up, relayed_g, 6, [h2], "gather_up_relay",
                                              then=v_down, then_peers=3)
    v_gate = _gather_in_neighbours_end(v_gate, relayed_g, [v_up], "gather_gate_neighbours_end")
    forwarded_g, v_gate = _gather_in_diagonal(v_gate, relayed_g, [v_up], "gather_gate_diagonal")
    v_up = _gather_in_neighbours_end(v_up, relayed_u, [v_gate], "gather_up_neighbours_end")
    wg = _weight_unview(_gather_in_diagonal_end(v_gate, forwarded_g, [v_up], "gather_gate_diagonal_end"))
    forwarded_u, v_up = _gather_in_diagonal(v_up, relayed_u, [wg], "gather_up_diagonal")
    wu = _weight_unview(_gather_in_diagonal_end(v_up, forwarded_u, [wg], "gather_up_diagonal_end"))
    gate, up, act = _matmul("gate_up", "nn", [h2, h2], [wg, wu], [0, 1], s, ffn, d, s, 512, d, [],
                            [BF16, BF16, BF16], _epi_swiglu, a_single_buffer=True)
    fs, fr, v_down = _gather_forward([v_down], [0], *relayed_u, act, "gather_forward_down", base=6)
    wd, = _gather_end([v_down], fs, fr, act, "gather_end_down")
    x2, = _matmul("down_proj", "nn", [act], [wd], [0], s, d, ffn, s // 2, 512, ffn, [x1], [F32],
                  _epi_residual)
    loss_row, dx2, dx2b, dwf = _final_norm_loss(x2, norm_final_w.reshape(1, d), target, "final_norm_loss")

    names = col_names + row_names
    grads, new = {}, {}

    def chip_sums(tag_names, views, sibs):
        return [(_chip_sum_col if k in col_names else _chip_sum_row)(v, sb, c_arr, "chip_sum_" + k)
                for k, v, sb in zip(tag_names, views, sibs)]

    def piece_sums(tag_names, pieces, received):
        return [_sum_pieces(p, r, place_arr, "sum_pieces_" + k) for k, p, r in zip(tag_names, pieces, received)]

    def update(k):
        new[k] = _adamw(big[k][0], grads[k], big[k][1], big[k][2], "adamw_" + k)

    dgate, dup = _matmul("d_act", "nt", [dx2b], [wd], [0], s, ffn, d, s, 512, d, [gate, up],
                         [BF16, BF16], _epi_swiglu_bwd, a_single_buffer=True)
    g_wd, = _matmul("g_w_down", "tn", [act], [dx2b], [0], ffn, d, s, 512, d, s, [], [BF16], _epi_plain)
    halves_d = _halves_start("down", [g_wd], [False])
    dh2, = _matmul("d_h2", "nt", [dgate, dup], [wg, wu], [0, 0], s, d, ffn, s // 2, 256, ffn, [], [F32],
                   _epi_plain, after=halves_d[2][-1:], a_single_buffer=True)
    pieces_d = _pieces_start("down", chip_sums(["w_down"], *_halves_wait("down", halves_d, dh2)))
    g_wg, g_wu = _matmul("g_w_gate_up", "tn", [h2, h2], [dgate, dup], [0, 1], d, ffn, s, 1024, 512, s, [],
                         [BF16, BF16], _epi_two, after=pieces_d[2][-1:])
    halves_gu = _halves_start("gate_up", [g_wg, g_wu], [True, True])
    dx1, dx1b, dw_ffn = _rms_bwd(x1, norm_ffn_w, dh2, dx2, "rms_ffn_bwd", after=halves_gu[2][-1:])

    dmixed, = _matmul("d_mixed", "nt", [dx1b], [wo], [0], s, mix, d, s // 2, 512, d, [], [F32], _epi_plain)
    pieces_gu = _pieces_start("gate_up", chip_sums(["w_gate", "w_up"], *_halves_wait("gate_up", halves_gu, dmixed)))
    per = sec // 512
    g_wo, = _matmul("g_w_out", "tn", [mixed], [dx1b], [0], mix, d, s, 512, d, s, [], [BF16], _epi_plain,
                    after=pieces_gu[2][-1:],
                    a_specs=[pl.BlockSpec((None, s, 512), lambda i, j, kk: (i // per, 0, i % per))])
    halves_o = _halves_start("out", [g_wo], [False])
    join_d = _join_start("down", piece_sums(["w_down"], *_pieces_wait("down", pieces_d, halves_o[2][-1])))
    dsec = _attn_bwd(proj, attn_o, lse, dmixed, after=[halves_o[2][-1], join_d[2][0]])
    pieces_o = _pieces_start("out", chip_sums(["w_out"], *_halves_wait("out", halves_o, dsec)))
    dsec = _ret_bwd(proj, ret_raw, dmixed, dsec, after=pieces_o[2][-1:])
    where = [0, 1, 2, 4, 5, 6, 7]
    n_sec = len(where)
    g_wi, = _matmul("g_w_in", "tn", [h1], [dsec], [0], d, n_in, s, 1024, sec, s, [], [BF16], _epi_plain,
                    b_specs=[pl.BlockSpec((None, s, sec), lambda i, j, kk: (j + (j >= 3).astype(I32), 0, 0))])
    halves_i = _halves_start("in", [g_wi], [True])
    grads["w_down"], = _join_wait("down", join_d, g_wi)
    dh1, *new["w_down"] = _matmul(
        "d_h1", "nt", [dsec] * n_sec, [wi] * n_sec, [0] * n_sec, s, d, sec, s // 2, 256, sec, [], [F32],
        _epi_plain, b_koff=list(range(n_sec)), after=halves_i[2][-1:], a_specs=[section(p, s // 2) for p in where],
        side_ins=[big["w_down"][0], grads["w_down"], big["w_down"][1], big["w_down"][2]],
        side_fn=lambda w, g, m, v: (*_adamw_math(w, g, m, v), g), side_n_out=4)
    pieces_i = _pieces_start("in", chip_sums(["w_in"], *_halves_wait("in", halves_i, dh1)))
    grad_x, _, dw_mix = _rms_bwd(xs, norm_mix_w, dh1, dx1, "rms_mix_bwd", after=pieces_i[2][-1:])

    def rows8(*vs):
        return jnp.concatenate([v.reshape(1, d) for v in vs] + [jnp.zeros((8 - len(vs), d), F32)], axis=0)

    join_gu = _join_start("gate_up", piece_sums(["w_gate", "w_up"], *_pieces_wait("gate_up", pieces_gu, grad_x)))
    join_o = _join_start("out", piece_sums(["w_out"], *_pieces_wait("out", pieces_o, join_gu[2][0])))
    grads["w_gate"], grads["w_up"] = _join_wait("gate_up", join_gu, join_o[2][0])
    update("w_gate")
    update("w_up")
    grads["w_out"], = _join_wait("out", join_o, new["w_up"][0])
    update("w_out")
    join_i = _join_start("in", piece_sums(["w_in"], *_pieces_wait("in", pieces_i, new["w_out"][0])))
    ng, nd, nm, nv = _norm_weights_step(
        rows8(dw_mix, dw_ffn, dwf, jnp.broadcast_to(loss_row[:, :1], (1, d))),
        rows8(norm_mix_w, norm_ffn_w, norm_final_w),
        rows8(m_norm_mix_w, m_norm_ffn_w, m_norm_final_w), rows8(v_norm_mix_w, v_norm_ffn_w, v_norm_final_w),
        after=join_i[2][:1])
    grads["w_in"], = _join_wait("in", join_i, ng)
    update("w_in")

    loss = ng[3, 0]

    def pack(small, per_weight):
        lead = lambda a: a.reshape((1,) + a.shape)
        return (small[0:1], lead(per_weight["w_in"]), lead(per_weight["w_out"]), small[1:2],
                lead(per_weight["w_gate"]), lead(per_weight["w_up"]), lead(per_weight["w_down"]), small[2])

    return (loss, grad_x.reshape(1, s, d),
            *pack(ng, {k: new[k][3] for k in names}),
            *pack(nd, {k: new[k][0] for k in names}),
            *pack(nm, {k: new[k][1] for k in names}),
            *pack(nv, {k: new[k][2] for k in names}))
```

```python
import functools
import math

import jax
import jax.numpy as jnp
from jax import lax
from jax.experimental import pallas as pl
from jax.experimental.pallas import tpu as pltpu

F32 = jnp.float32
BF16 = jnp.bfloat16
I32 = jnp.int32
MESH = pl.DeviceIdType.MESH
ANY = pl.BlockSpec(memory_space=pl.ANY)

ATTN_HEADS = 8
ATTN_HEAD_DIM = 128
RET_HEADS = 4
RET_HEAD_DIM = 256
ATTN_WIDTH = ATTN_HEADS * ATTN_HEAD_DIM
RET_WIDTH = RET_HEADS * RET_HEAD_DIM
DILATED_PATTERNS = ((128, 1), (512, 4), (2048, 16))
NORM_EPS = 1e-6
ADAM_LR = 0.001
ADAM_B1 = 0.9
ADAM_B2 = 0.999
ADAM_EPS = 1e-08
ADAM_WD = 0.01
ADAM_STEP = 10

N_CHIPS = 4
N_DEV = 8
NEG_BIG = -1e30
SEQ_TILE = 512
ATTN_FWD_HEADS_PER_STEP = 2
ATTN_HEADS_PER_STEP = 1
VMEM_LIMIT_BYTES = 56 * 1024 * 1024


def _params(semantics=None, vmem=VMEM_LIMIT_BYTES):
    return pltpu.CompilerParams(dimension_semantics=semantics, vmem_limit_bytes=vmem)


def _row_tile(rows, row_bytes, limit=2 * 1024 * 1024, mult=16):
    best = None
    for t in range(mult, rows + 1, mult):
        if rows % t == 0 and t * row_bytes <= limit:
            best = t
    assert best is not None, (rows, row_bytes)
    return best


def _sigmoid(x):
    return 1.0 / (1.0 + jnp.exp(-x))


def _select_by_index(idx, values):
    out = jnp.float32(values[-1])
    for i in range(len(values) - 2, -1, -1):
        out = jnp.where(idx == i, jnp.float32(values[i]), out)
    return out


def _place():
    x, y, c = lax.axis_index("x"), lax.axis_index("y"), lax.axis_index("c")
    return x, y, c


def _cast_into_full(w, shard_arr, column_sharded, name, after=()):
    after = tuple(after)
    rows, cols = w.shape
    tr = _row_tile(rows, cols * 4)
    steps = rows // tr
    if column_sharded:
        out_shape, out_map = (rows, N_CHIPS * cols), (lambda i, s_ref: (i, s_ref[0]))
    else:
        out_shape, out_map = (N_CHIPS * rows, cols), (lambda i, s_ref: (s_ref[0] * steps + i, 0))

    def body(s_ref, w_ref, *rest):
        del s_ref
        rest[-1][...] = w_ref[...].astype(BF16)

    grid_spec = pltpu.PrefetchScalarGridSpec(
        num_scalar_prefetch=1, grid=(steps,),
        in_specs=[pl.BlockSpec((tr, cols), lambda i, s_ref: (i, 0))] + [ANY] * len(after),
        out_specs=pl.BlockSpec((tr, cols), out_map))
    return pl.pallas_call(
        body, name=name, grid_spec=grid_spec,
        out_shape=jax.ShapeDtypeStruct(out_shape, BF16),
        compiler_params=_params(("parallel",)),
    )(shard_arr, w, *after)


def _rms_fwd(x, w, name, after=()):
    rows, d = x.shape
    tr = 256
    after = tuple(after)

    def body(x_ref, w_ref, *rest):
        xv = x_ref[...]
        r = lax.rsqrt(jnp.mean(xv * xv, axis=-1, keepdims=True) + NORM_EPS)
        rest[-1][...] = (xv * r * w_ref[...]).astype(BF16)

    return pl.pallas_call(
        body, name=name, grid=(rows // tr,),
        in_specs=[pl.BlockSpec((tr, d), lambda i: (i, 0)), pl.BlockSpec((1, d), lambda i: (0, 0))]
        + [ANY] * len(after),
        out_specs=pl.BlockSpec((tr, d), lambda i: (i, 0)),
        out_shape=jax.ShapeDtypeStruct((rows, d), BF16),
        compiler_params=_params(("parallel",)),
    )(x, w, *after)


def _rms_bwd(x, w, dh, dres, name, after=()):
    rows, d = x.shape
    tr = 256
    after = tuple(after)

    def body(x_ref, w_ref, dh_ref, dres_ref, *rest):
        dx_ref, dxb_ref, dw_ref = rest[len(after):]
        xv = x_ref[...]
        r = lax.rsqrt(jnp.mean(xv * xv, axis=-1, keepdims=True) + NORM_EPS)
        xhat = xv * r
        dy = dh_ref[...]
        dxhat = dy * w_ref[...]
        dx = dres_ref[...] + r * (dxhat - xhat * jnp.mean(dxhat * xhat, axis=-1, keepdims=True))
        dx_ref[...] = dx
        dxb_ref[...] = dx.astype(BF16)
        part = jnp.sum(dy * xhat, axis=0, keepdims=True)

        @pl.when(pl.program_id(0) == 0)
        def _():
            dw_ref[...] = part

        @pl.when(pl.program_id(0) != 0)
        def _():
            dw_ref[...] += part

    row = pl.BlockSpec((tr, d), lambda i: (i, 0))
    vec = pl.BlockSpec((1, d), lambda i: (0, 0))
    return pl.pallas_call(
        body, name=name, grid=(rows // tr,),
        in_specs=[row, vec, row, row] + [ANY] * len(after),
        out_specs=[row, row, vec],
        out_shape=[jax.ShapeDtypeStruct((rows, d), F32), jax.ShapeDtypeStruct((rows, d), BF16),
                   jax.ShapeDtypeStruct((1, d), F32)],
        compiler_params=_params(("arbitrary",)),
    )(x, w, dh, dres, *after)


def _final_norm_loss(x2, w, target, name):
    rows, d = x2.shape
    tr = 256

    def body(x_ref, w_ref, t_ref, loss_ref, dx_ref, dxb_ref, dw_ref):
        xv = x_ref[...]
        wv = w_ref[...]
        r = lax.rsqrt(jnp.mean(xv * xv, axis=-1, keepdims=True) + NORM_EPS)
        xhat = xv * r
        err = xhat * wv - t_ref[...]
        part_loss = 0.5 * jnp.sum(jnp.mean(err * err, axis=-1, keepdims=True), axis=0, keepdims=True)
        dy = err * (1.0 / d)
        dxhat = dy * wv
        dx = r * (dxhat - xhat * jnp.mean(dxhat * xhat, axis=-1, keepdims=True))
        dx_ref[...] = dx
        dxb_ref[...] = dx.astype(BF16)
        part_dw = jnp.sum(dy * xhat, axis=0, keepdims=True)
        part_loss = jnp.broadcast_to(part_loss, (1, 128))

        @pl.when(pl.program_id(0) == 0)
        def _():
            dw_ref[...] = part_dw
            loss_ref[...] = part_loss

        @pl.when(pl.program_id(0) != 0)
        def _():
            dw_ref[...] += part_dw
            loss_ref[...] += part_loss

    row = pl.BlockSpec((tr, d), lambda i: (i, 0))
    vec = pl.BlockSpec((1, d), lambda i: (0, 0))
    return pl.pallas_call(
        body, name=name, grid=(rows // tr,),
        in_specs=[row, vec, row],
        out_specs=[pl.BlockSpec((1, 128), lambda i: (0, 0)), row, row, vec],
        out_shape=[jax.ShapeDtypeStruct((1, 128), F32), jax.ShapeDtypeStruct((rows, d), F32),
                   jax.ShapeDtypeStruct((rows, d), BF16), jax.ShapeDtypeStruct((1, d), F32)],
        compiler_params=_params(("arbitrary",)),
    )(x2, w, target)


def _adamw_math(w, g, m, v):
    m = ADAM_B1 * m + (1.0 - ADAM_B1) * g
    v = ADAM_B2 * v + (1.0 - ADAM_B2) * (g * g)
    m_hat = m / (1.0 - ADAM_B1 ** ADAM_STEP)
    v_hat = v / (1.0 - ADAM_B2 ** ADAM_STEP)
    delta = -ADAM_LR * (m_hat / (jnp.sqrt(v_hat) + ADAM_EPS) + ADAM_WD * w)
    return delta, m, v


def _adamw(w, g, m, v, name):
    rows, cols = w.shape
    tr = _row_tile(rows, cols * 4)

    def body(w_ref, g_ref, m_ref, v_ref, d_ref, mo_ref, vo_ref, go_ref):
        g = g_ref[...]
        delta, m_new, v_new = _adamw_math(w_ref[...], g, m_ref[...], v_ref[...])
        d_ref[...] = delta
        mo_ref[...] = m_new
        vo_ref[...] = v_new
        go_ref[...] = g

    blk = pl.BlockSpec((tr, cols), lambda i: (i, 0))
    shp = jax.ShapeDtypeStruct((rows, cols), F32)
    return pl.pallas_call(
        body, name=name, grid=(rows // tr,),
        in_specs=[blk] * 4, out_specs=[blk] * 4, out_shape=[shp] * 4,
        compiler_params=_params(("parallel",)),
    )(w, g, m, v)


_DOT_DIMS = {"nn": ((1,), (0,)), "nt": ((1,), (1,)), "tn": ((0,), (0,))}


def _matmul(name, mode, a_list, b_list, acc_of, m, n, k, tm, tn, tk, extras, out_dtypes, epilogue,
            a_koff=None, b_koff=None, after=(), a_specs=None, b_specs=None, a_single_buffer=False):
    after = tuple(after)
    assert m % tm == 0 and n % tn == 0 and k % tk == 0, (name, m, n, k, tm, tn, tk)
    nk = k // tk
    n_acc = max(acc_of) + 1
    n_pairs = len(a_list)
    a_koff = a_koff or [0] * n_pairs
    b_koff = b_koff or [0] * n_pairs
    dims = (_DOT_DIMS[mode], ((), ()))
    n_ext, n_out = len(extras), len(out_dtypes)

    def body(*refs):
        a_refs = refs[:n_pairs]
        b_refs = refs[n_pairs:2 * n_pairs]
        e_refs = refs[2 * n_pairs:2 * n_pairs + n_ext]
        first_out = 2 * n_pairs + n_ext + len(after)
        o_refs = refs[first_out:first_out + n_out]
        acc_refs = refs[first_out + n_out:]

        parts = [None] * n_acc
        for p in range(n_pairs):
            d = lax.dot_general(a_refs[p][...], b_refs[p][...], dims, preferred_element_type=F32)
            parts[acc_of[p]] = d if parts[acc_of[p]] is None else parts[acc_of[p]] + d

        def finish(accs):
            outs = epilogue(accs, [e[...] for e in e_refs])
            for o_ref, o in zip(o_refs, outs):
                o_ref[...] = o.astype(o_ref.dtype)

        if nk == 1:
            finish(parts)
        else:
            kk = pl.program_id(2)

            @pl.when(kk == 0)
            def _():
                for acc_ref, part in zip(acc_refs, parts):
                    acc_ref[...] = part

            @pl.when(kk != 0)
            def _():
                for acc_ref, part in zip(acc_refs, parts):
                    acc_ref[...] += part

            @pl.when(kk == nk - 1)
            def _():
                finish([acc_ref[...] for acc_ref in acc_refs])

    def a_spec(off):
        mode_a = pl.Buffered(1) if a_single_buffer else None
        if mode == "tn":
            return pl.BlockSpec((tk, tm), lambda i, j, kk: (kk + off, i), pipeline_mode=mode_a)
        return pl.BlockSpec((tm, tk), lambda i, j, kk: (i, kk + off), pipeline_mode=mode_a)

    def b_spec(off):
        if mode == "nt":
            return pl.BlockSpec((tn, tk), lambda i, j, kk: (j, kk + off))
        return pl.BlockSpec((tk, tn), lambda i, j, kk: (kk + off, j))

    tile = pl.BlockSpec((tm, tn), lambda i, j, kk: (i, j))
    scratch = [pltpu.VMEM((tm, tn), F32) for _ in range(n_acc)] if nk > 1 else []
    return pl.pallas_call(
        body, name=name, grid=(m // tm, n // tn, nk),
        in_specs=(a_specs or [a_spec(o) for o in a_koff]) + (b_specs or [b_spec(o) for o in b_koff])
        + [tile] * n_ext + [ANY] * len(after),
        out_specs=[tile] * n_out,
        out_shape=[jax.ShapeDtypeStruct((m, n), dt) for dt in out_dtypes],
        scratch_shapes=scratch,
        compiler_params=_params(("parallel", "parallel", "arbitrary")),
    )(*a_list, *b_list, *extras, *after)


def _epi_plain(accs, extras):
    return (accs[0],)


def _epi_residual(accs, extras):
    return (accs[0] + extras[0],)


def _epi_two(accs, extras):
    return accs[0], accs[1]


def _epi_swiglu(accs, extras):
    g, u = accs
    return g, u, g * _sigmoid(g) * u


def _epi_swiglu_bwd(accs, extras):
    da = accs[0]
    g, u = (e.astype(F32) for e in extras)
    sg = _sigmoid(g)
    dg = da * u * sg * (1.0 + g * (1.0 - sg))
    du = da * g * sg
    return dg, du


_NT_DIMS = (((1,), (1,)), ((), ()))
_TN_DIMS = (((0,), (0,)), ((), ()))


def _tile_delta(tq, tk):
    return lax.broadcasted_iota(I32, (tq, tk), 0) - lax.broadcasted_iota(I32, (tq, tk), 1)


def _attn_log_count(delta):
    count = jnp.zeros(delta.shape, I32)
    for window, dilation in DILATED_PATTERNS:
        hit = ((delta & (dilation - 1)) == 0) & (delta <= window)
        count = count + jnp.where(hit, 1, 0)
    valid = (delta >= 0) & (count > 0)
    logm = jnp.where(count == 3, math.log(3.0), jnp.where(count == 2, math.log(2.0), 0.0))
    return jnp.where(valid, logm, NEG_BIG)


def _fill_attn_log_count(tab_ref):
    nb, t, _ = tab_ref.shape
    base = _tile_delta(t, t)
    for b in range(nb):
        tab_ref[b] = _attn_log_count(base + b * t)


def _fill_attn_bias(tab_ref, log_count_ref, slope):
    nb, t, _ = tab_ref.shape
    dist = _tile_delta(t, t).astype(F32)
    for b in range(nb):
        tab_ref[b] = log_count_ref[b] - slope * (dist + float(b * t))


def _fill_ret_decay(tab_ref, log_gamma):
    nb, t, _ = tab_ref.shape
    base = _tile_delta(t, t)
    for b in range(nb):
        tab_ref[b] = _ret_decay(base + b * t, log_gamma)


def _alibi_slopes():
    return [2.0 ** (-8.0 * (h + 1) / ATTN_HEADS) for h in range(ATTN_HEADS)]


def _attn_fwd(proj, after=()):
    s = proj.shape[0]
    t = SEQ_TILE
    hd = ATTN_HEAD_DIM
    hp = ATTN_FWD_HEADS_PER_STEP
    ng = ATTN_HEADS // hp
    w = hp * hd
    scale = 1.0 / math.sqrt(hd)
    slopes = _alibi_slopes()

    def body(q_ref, k_ref, v_ref, *rest):
        mix_ref, o_ref, lse_ref, kb, vb, bias_tab, log_count_tab = rest[len(after):]
        g = pl.program_id(0)
        i = pl.program_id(1)

        @pl.when((g == 0) & (i == 0))
        def _():
            _fill_attn_log_count(log_count_tab)

        @pl.when(i == 0)
        def _():
            kb[...] = k_ref[...].astype(BF16)
            vb[...] = v_ref[...].astype(BF16)
            for u in range(hp):
                _fill_attn_bias(bias_tab.at[u], log_count_tab, _select_by_index(g * hp + u, slopes))

        qs = [q_ref[:, u * hd:(u + 1) * hd].astype(BF16) for u in range(hp)]

        def step(j, carry):
            rows = pl.ds(pl.multiple_of(j * t, t), t)
            out = []
            for u in range(hp):
                m_i, l_i, acc = carry[u]
                lanes = slice(u * hd, (u + 1) * hd)
                sc = lax.dot_general(qs[u], kb[rows, lanes], _NT_DIMS, preferred_element_type=F32) * scale
                sc = sc + bias_tab[u, i - j]
                m_new = jnp.maximum(m_i, jnp.max(sc, axis=-1, keepdims=True))
                p = jnp.exp(sc - m_new)
                alpha = jnp.exp(m_i - m_new)
                l_new = alpha * l_i + jnp.sum(p, axis=-1, keepdims=True)
                acc = alpha * acc + jnp.dot(p.astype(BF16), vb[rows, lanes], preferred_element_type=F32)
                out.append((m_new, l_new, acc))
            return tuple(out)

        init = (jnp.full((t, 1), NEG_BIG, F32), jnp.zeros((t, 1), F32), jnp.zeros((t, hd), F32))
        final = lax.fori_loop(0, i + 1, step, (init,) * hp)
        for u in range(hp):
            m_i, l_i, acc = final[u]
            lanes = slice(u * hd, (u + 1) * hd)
            out = acc / l_i
            o_ref[:, lanes] = out
            mix_ref[:, lanes] = out.astype(BF16)
            lse_ref[:, lanes] = jnp.broadcast_to(m_i + jnp.log(l_i), (t, hd))

    return pl.pallas_call(
        body, name="attn_fwd", grid=(ng, s // t),
        in_specs=[pl.BlockSpec((t, w), lambda g, i: (i, g)),
                  pl.BlockSpec((s, w), lambda g, i: (0, ng + g)),
                  pl.BlockSpec((s, w), lambda g, i: (0, 2 * ng + g))] + [ANY] * len(after),
        out_specs=[pl.BlockSpec((None, t, w), lambda g, i: (0, i, g))] + [pl.BlockSpec((t, w), lambda g, i: (i, g))] * 2,
        out_shape=[jax.ShapeDtypeStruct((2, s, ATTN_WIDTH), BF16),
                   jax.ShapeDtypeStruct((s, ATTN_WIDTH), F32),
                   jax.ShapeDtypeStruct((s, ATTN_WIDTH), F32)],
        scratch_shapes=[pltpu.VMEM((s, w), BF16), pltpu.VMEM((s, w), BF16), pltpu.VMEM((hp, s // t, t, t), F32),
                        pltpu.VMEM((s // t, t, t), F32)],
        compiler_params=_params(("arbitrary", "arbitrary")),
    )(proj, proj, proj, *after)


def _attn_bwd(proj, attn_out, lse, dmixed, after=()):
    after = tuple(after)
    s = proj.shape[0]
    t = SEQ_TILE
    nt = s // t
    hd = ATTN_HEAD_DIM
    hp = ATTN_HEADS_PER_STEP
    ng = ATTN_HEADS // hp
    w = hp * hd
    scale = 1.0 / math.sqrt(hd)
    slopes = _alibi_slopes()

    def body(q_ref, k_ref, v_ref, o_ref, lse_ref, do_ref, *rest):
        dsec_ref, qb, kb, vb, dob, dsum, dq_acc, bias_tab, log_count_tab = rest[len(after):]
        g = pl.program_id(0)

        @pl.when(g == 0)
        def _():
            _fill_attn_log_count(log_count_tab)

        qb[...] = q_ref[...].astype(BF16)
        kb[...] = k_ref[...].astype(BF16)
        vb[...] = v_ref[...].astype(BF16)
        dob[...] = do_ref[...].astype(BF16)
        for u in range(hp):
            lanes = slice(u * hd, (u + 1) * hd)
            _fill_attn_bias(bias_tab.at[u], log_count_tab, _select_by_index(g * hp + u, slopes))
            rowsum = jnp.sum(do_ref[:, lanes] * o_ref[:, lanes], axis=-1, keepdims=True)
            dsum[:, lanes] = jnp.broadcast_to(rowsum, (s, hd))
        dq_acc[...] = jnp.zeros((s, w), F32)

        def over_keys(j, _):
            krows = pl.ds(pl.multiple_of(j * t, t), t)

            def over_queries(i, carry):
                qrows = pl.ds(pl.multiple_of(i * t, t), t)
                out = []
                for u in range(hp):
                    dk, dv = carry[u]
                    lanes = slice(u * hd, (u + 1) * hd)
                    qi, doi = qb[qrows, lanes], dob[qrows, lanes]
                    kj, vj = kb[krows, lanes], vb[krows, lanes]
                    lse_i = lse_ref[qrows, lanes][:, :1]
                    dsum_i = dsum[qrows, lanes][:, :1]
                    sc = lax.dot_general(qi, kj, _NT_DIMS, preferred_element_type=F32) * scale
                    p = jnp.exp(sc + bias_tab[u, i - j] - lse_i)
                    dp = lax.dot_general(doi, vj, _NT_DIMS, preferred_element_type=F32)
                    ds = (p * (dp - dsum_i)).astype(BF16)
                    dv = dv + lax.dot_general(p.astype(BF16), doi, _TN_DIMS, preferred_element_type=F32)
                    dk = dk + lax.dot_general(ds, qi, _TN_DIMS, preferred_element_type=F32)
                    dq_acc[qrows, lanes] += jnp.dot(ds, kj, preferred_element_type=F32)
                    out.append((dk, dv))
                return tuple(out)

            zero = jnp.zeros((t, hd), F32)
            final = lax.fori_loop(j, nt, over_queries, ((zero, zero),) * hp)
            for u in range(hp):
                lanes = slice(u * hd, (u + 1) * hd)
                dsec_ref[1, krows, lanes] = (final[u][0] * scale).astype(BF16)
                dsec_ref[2, krows, lanes] = final[u][1].astype(BF16)
            return 0

        lax.fori_loop(0, nt, over_keys, 0)
        dsec_ref[0] = (dq_acc[...] * scale).astype(BF16)

    def col(off):
        return pl.BlockSpec((s, w), lambda g: (0, off + g))

    return pl.pallas_call(
        body, name="attn_bwd", grid=(ng,),
        in_specs=[col(0), col(ng), col(2 * ng), col(0), col(0), col(0)] + [ANY] * len(after),
        out_specs=pl.BlockSpec((4, s, w), lambda g: (0, 0, g)),
        out_shape=jax.ShapeDtypeStruct((8, s, ATTN_WIDTH), BF16),
        scratch_shapes=[pltpu.VMEM((s, w), BF16)] * 4 + [pltpu.VMEM((s, w), F32)] * 2
        + [pltpu.VMEM((hp, nt, t, t), F32), pltpu.VMEM((nt, t, t), F32)],
        compiler_params=_params(("arbitrary",)),
    )(proj, proj, proj, attn_out, lse, dmixed, *after)


def _ret_log_gammas():
    return [math.log(1.0 - 2.0 ** (-5.0 - h)) for h in range(RET_HEADS)]


def _ret_decay(delta, log_gamma):
    dec = jnp.exp(delta.astype(F32) * log_gamma) * (1.0 / math.sqrt(RET_HEAD_DIM))
    return jnp.where(delta >= 0, dec, 0.0)


def _ret_fwd(proj, mixed, after=()):
    after = tuple(after)
    s = proj.shape[0]
    t = SEQ_TILE
    hd = RET_HEAD_DIM
    nh = RET_HEADS
    log_gammas = _ret_log_gammas()
    c0 = 3 * ATTN_WIDTH // hd

    def body(q_ref, k_ref, v_ref, g_ref, *rest):
        mix_ref, raw_ref, kb, vb, decay_tab = rest[1 + len(after):]
        h = pl.program_id(0)
        i = pl.program_id(1)

        @pl.when(i == 0)
        def _():
            kb[...] = k_ref[...].astype(BF16)
            vb[...] = v_ref[...].astype(BF16)
            _fill_ret_decay(decay_tab, _select_by_index(h, log_gammas))

        q = q_ref[...].astype(BF16)

        def step(j, acc):
            rows = pl.ds(pl.multiple_of(j * t, t), t)
            sc = lax.dot_general(q, kb[rows, :], _NT_DIMS, preferred_element_type=F32) * decay_tab[i - j]
            return acc + jnp.dot(sc.astype(BF16), vb[rows, :], preferred_element_type=F32)

        ret = lax.fori_loop(0, i + 1, step, jnp.zeros((t, hd), F32))
        raw_ref[...] = ret
        r = lax.rsqrt(jnp.mean(ret * ret, axis=-1, keepdims=True) + NORM_EPS)
        g = g_ref[...]
        mix_ref[...] = (g * _sigmoid(g) * (ret * r)).astype(BF16)

    return pl.pallas_call(
        body, name="ret_fwd", grid=(nh, s // t),
        in_specs=[pl.BlockSpec((t, hd), lambda h, i: (i, c0 + h)),
                  pl.BlockSpec((s, hd), lambda h, i: (0, c0 + nh + h)),
                  pl.BlockSpec((s, hd), lambda h, i: (0, c0 + 2 * nh + h)),
                  pl.BlockSpec((t, hd), lambda h, i: (i, c0 + 3 * nh + h))] + [ANY] * (1 + len(after)),
        out_specs=[pl.BlockSpec((None, t, hd), lambda h, i: (1, i, h)), pl.BlockSpec((t, hd), lambda h, i: (i, h))],
        out_shape=[jax.ShapeDtypeStruct(mixed.shape, BF16), jax.ShapeDtypeStruct((s, RET_WIDTH), F32)],
        input_output_aliases={4: 0},
        scratch_shapes=[pltpu.VMEM((s, hd), BF16), pltpu.VMEM((s, hd), BF16), pltpu.VMEM((s // t, t, t), F32)],
        compiler_params=_params(("arbitrary", "arbitrary")),
    )(proj, proj, proj, proj, mixed, *after)


def _ret_bwd(proj, ret_raw, dmixed, dsec, after=()):
    after = tuple(after)
    s = proj.shape[0]
    t = SEQ_TILE
    nt = s // t
    hd = RET_HEAD_DIM
    nh = RET_HEADS
    log_gammas = _ret_log_gammas()
    c0 = 3 * ATTN_WIDTH // hd
    mixed_blocks = ATTN_WIDTH // hd

    def body(q_ref, k_ref, v_ref, g_ref, raw_ref, dmix_ref, *rest):
        dsec_ref, qb, kb, vb, dretb, dq_acc, decay_tab = rest[1 + len(after):]
        h = pl.program_id(0)
        _fill_ret_decay(decay_tab, _select_by_index(h, log_gammas))
        qb[...] = q_ref[...].astype(BF16)
        kb[...] = k_ref[...].astype(BF16)
        vb[...] = v_ref[...].astype(BF16)
        ret = raw_ref[...]
        r = lax.rsqrt(jnp.mean(ret * ret, axis=-1, keepdims=True) + NORM_EPS)
        normed = ret * r
        g = g_ref[...]
        sg = _sigmoid(g)
        dout = dmix_ref[...]
        dsec_ref[3] = (dout * normed * sg * (1.0 + g * (1.0 - sg))).astype(BF16)
        dn = dout * g * sg
        dret = r * (dn - normed * jnp.mean(dn * normed, axis=-1, keepdims=True))
        dretb[...] = dret.astype(BF16)
        dq_acc[...] = jnp.zeros((s, hd), F32)

        def over_keys(j, _):
            krows = pl.ds(pl.multiple_of(j * t, t), t)
            kj = kb[krows, :]
            vj = vb[krows, :]

            def over_queries(i, carry):
                dk, dv = carry
                qrows = pl.ds(pl.multiple_of(i * t, t), t)
                qi = qb[qrows, :]
                doi = dretb[qrows, :]
                dec = decay_tab[i - j]
                a = (lax.dot_general(qi, kj, _NT_DIMS, preferred_element_type=F32) * dec).astype(BF16)
                da = (lax.dot_general(doi, vj, _NT_DIMS, preferred_element_type=F32) * dec).astype(BF16)
                dv = dv + lax.dot_general(a, doi, _TN_DIMS, preferred_element_type=F32)
                dk = dk + lax.dot_general(da, qi, _TN_DIMS, preferred_element_type=F32)
                dq_acc[qrows, :] += jnp.dot(da, kj, preferred_element_type=F32)
                return dk, dv

            zero = jnp.zeros((t, hd), F32)
            dk, dv = lax.fori_loop(j, nt, over_queries, (zero, zero))
            dsec_ref[1, krows, :] = dk.astype(BF16)
            dsec_ref[2, krows, :] = dv.astype(BF16)
            return 0

        lax.fori_loop(0, nt, over_keys, 0)
        dsec_ref[0] = dq_acc[...].astype(BF16)

    def col(off):
        return pl.BlockSpec((s, hd), lambda h: (0, off + h))

    return pl.pallas_call(
        body, name="ret_bwd", grid=(nh,),
        in_specs=[col(c0), col(c0 + nh), col(c0 + 2 * nh), col(c0 + 3 * nh), col(0), col(mixed_blocks)]
        + [ANY] * (1 + len(after)),
        out_specs=pl.BlockSpec((4, s, hd), lambda h: (1, 0, h)),
        out_shape=jax.ShapeDtypeStruct(dsec.shape, BF16),
        input_output_aliases={6: 0},
        scratch_shapes=[pltpu.VMEM((s, hd), BF16)] * 4 + [pltpu.VMEM((s, hd), F32)]
        + [pltpu.VMEM((nt, t, t), F32)],
        compiler_params=_params(("arbitrary",)),
    )(proj, proj, proj, proj, ret_raw, dmixed, dsec, *after)


_FLIPS = (2, 1, 3)


def _other_chips(x, y):
    return [(1 - x, y), (x, 1 - y), (1 - x, 1 - y)]


_HBM = pl.BlockSpec(memory_space=pltpu.HBM)
_SEM = pl.BlockSpec(memory_space=pltpu.SEMAPHORE)
_EFFECT = pltpu.SideEffectType.DATAFLOW_SIDE_EFFECTING


def _in_hbm(a):
    return pltpu.with_memory_space_constraint(a, pltpu.HBM)


def _weight_view(w, column_sharded):
    if column_sharded:
        return w.reshape(2, w.shape[0] // 2, w.shape[1])
    return w.reshape(N_CHIPS, 2, w.shape[0] // (2 * N_CHIPS), w.shape[1])


def _weight_unview(v):
    if v.ndim == 3:
        return v.reshape(2 * v.shape[1], v.shape[2])
    return v.reshape(N_CHIPS * 2 * v.shape[2], v.shape[3])


def _weight_region(buf, shard, half):
    if len(buf.shape) == 3:
        cols = buf.shape[2] // N_CHIPS
        return buf.at[half, :, pl.ds(shard * cols, cols)]
    return buf.at[shard, half]


def _remote(where, send_sem, recv_sem, to):
    return pltpu.make_async_remote_copy(src_ref=where, dst_ref=where, send_sem=send_sem, recv_sem=recv_sem,
                                        device_id=to, device_id_type=MESH)


def _for_my_shard(fn):
    x, y, _ = _place()
    for ss in range(N_CHIPS):
        pl.when(2 * x + y == ss)(functools.partial(fn, ss))


def _gather_forward(views, which, send_sems, recv_sems, after, name, base=0):
    n_w = len(views)
    which = [base // 3 + w for w in which] if base % 3 == 0 else None
    assert which is not None, "base must be a multiple of 3"

    def body(*refs):
        send_in, recv_in = refs[n_w:n_w + 2]
        fwd_send, fwd_recv = refs[n_w + 3:n_w + 5]
        bufs = refs[n_w + 5:]
        x, y, c = _place()
        sibling = (x, y, 1 - c)

        def forward(ss):
            for i, w in enumerate(which):
                for j in range(3):
                    landed = _weight_region(bufs[i], ss ^ _FLIPS[j], c)
                    _remote(landed, send_in.at[3 * w + j], recv_in.at[3 * w + j], sibling).wait_recv()
                    _remote(landed, fwd_send.at[3 * i + j], fwd_recv.at[3 * i + j], sibling).start()

        _for_my_shard(forward)
        for i, w in enumerate(which):
            for j in range(3):
                _remote(_weight_region(bufs[i], 0, 0), send_in.at[3 * w + j], recv_in.at[3 * w + j],
                        sibling).wait_send()

    return pl.pallas_call(
        body, name=name,
        in_specs=[_HBM] * n_w + [_SEM, _SEM, ANY], out_specs=[_SEM, _SEM] + [_HBM] * n_w,
        out_shape=[pltpu.SemaphoreType.DMA((3 * n_w,)), pltpu.SemaphoreType.DMA((3 * n_w,))]
        + [pltpu.HBM(v.shape, BF16) for v in views],
        input_output_aliases={w: 2 + w for w in range(n_w)},
        compiler_params=pltpu.CompilerParams(has_side_effects=_EFFECT),
    )(*views, send_sems, recv_sems, after)


def _gather_end(views, fwd_send, fwd_recv, after, name):
    n_w = len(views)

    def body(*refs):
        fwd_send_ref, fwd_recv_ref = refs[n_w:n_w + 2]
        bufs = refs[n_w + 3:]
        x, y, c = _place()
        for i in range(n_w):
            for j in range(3):
                cp = _remote(_weight_region(bufs[i], 0, 0), fwd_send_ref.at[3 * i + j], fwd_recv_ref.at[3 * i + j],
                             (x, y, 1 - c))
                cp.wait_recv()
                cp.wait_send()

    outs = pl.pallas_call(
        body, name=name,
        in_specs=[_HBM] * n_w + [_SEM, _SEM, ANY], out_specs=[_HBM] * n_w,
        out_shape=[pltpu.HBM(v.shape, BF16) for v in views],
        input_output_aliases={w: w for w in range(n_w)},
        compiler_params=pltpu.CompilerParams(has_side_effects=_EFFECT),
    )(*views, fwd_send, fwd_recv, after)
    return [_weight_unview(o) for o in outs]


def _comm_call(name, bufs, sem_pairs, after, n_new, fn):
    n, n_sem, after = len(bufs), 2 * len(sem_pairs), tuple(after)
    n_out_sem = 2 if n_new else 0

    def body(*refs):
        sems = refs[n:n + n_sem]
        outs = refs[n + n_sem + len(after):]
        new = outs[:n_out_sem] if n_new else (None, None)
        fn(outs[n_out_sem:], [(sems[2 * i], sems[2 * i + 1]) for i in range(len(sem_pairs))], *new)

    res = pl.pallas_call(
        body, name=name,
        in_specs=[_HBM] * n + [_SEM] * n_sem + [ANY] * len(after),
        out_specs=[_SEM] * n_out_sem + [_HBM] * n,
        out_shape=[pltpu.SemaphoreType.DMA((n_new,))] * n_out_sem + [pltpu.HBM(b.shape, b.dtype) for b in bufs],
        input_output_aliases={i: n_out_sem + i for i in range(n)},
        compiler_params=pltpu.CompilerParams(has_side_effects=_EFFECT),
    )(*bufs, *[s for pair in sem_pairs for s in pair], *after)
    return list(res[:n_out_sem]), list(res[n_out_sem:])


def _quarter(piece, q):
    rows = piece.shape[0] // 2
    return piece.at[pl.ds(q * rows, rows)]


def _gather_in_start(view, name):
    def fn(bufs, _, send, recv):
        x, y, c = _place()

        def go(ss):
            for j, chip in enumerate(_other_chips(x, y)[:2]):
                _remote(_weight_region(bufs[0], ss, c), send.at[j], recv.at[j], (*chip, c)).start()

        _for_my_shard(go)

    sems, (view,) = _comm_call(name, [_in_hbm(view)], [], (), 2, fn)
    return sems, view


def _gather_out_gate_start(v_out, v_gate, after, name):
    def fn(bufs, _, send, recv):
        x, y, c = _place()
        chips = _other_chips(x, y)

        def go(ss):
            for j in range(3):
                _remote(_weight_region(bufs[0], ss, c), send.at[j], recv.at[j], (*chips[j], c)).start()
            for j in range(2):
                _remote(_weight_region(bufs[1], ss, c), send.at[3 + j], recv.at[3 + j], (*chips[j], c)).start()

        _for_my_shard(go)

    sems, views = _comm_call(name, [_in_hbm(v_out), _in_hbm(v_gate)], [], after, 5, fn)
    return sems, views


def _gather_relay(view, started, base, after, name, then=None, then_peers=0):
    n_new = 6 + then_peers if then_peers else 4

    def fn(bufs, pairs, send, recv):
        (send_in, recv_in), = pairs
        x, y, c = _place()
        chips = _other_chips(x, y)
        sibling = (x, y, 1 - c)

        def go(ss):
            landed = [_weight_region(bufs[0], ss ^ _FLIPS[j], c) for j in range(2)]
            for j in range(2):
                _remote(landed[j], send_in.at[base + j], recv_in.at[base + j], sibling).wait_recv()
            for j in range(2):
                _remote(_quarter(landed[j], j), send.at[j], recv.at[j], (*chips[1 - j], c)).start()
            for j in range(2):
                _remote(landed[j], send.at[2 + j], recv.at[2 + j], sibling).start()
            for j in range(then_peers):
                _remote(_weight_region(bufs[1], ss, c), send.at[6 + j], recv.at[6 + j], (*chips[j], c)).start()

        _for_my_shard(go)
        for j in range(2):
            _remote(_weight_region(bufs[0], 0, 0), send_in.at[base + j], recv_in.at[base + j], sibling).wait_send()

    views = [view] if then is None else [view, _in_hbm(then)]
    sems, views = _comm_call(name, views, [started], after, n_new, fn)
    return sems, views


def _gather_in_neighbours_end(view, relayed, after, name):
    def fn(bufs, pairs, *_):
        (send, recv), = pairs
        x, y, c = _place()
        for j in range(2):
            cp = _remote(_weight_region(bufs[0], 0, 0), send.at[2 + j], recv.at[2 + j], (x, y, 1 - c))
            cp.wait_recv()
            cp.wait_send()

    _, (view,) = _comm_call(name, [view], [relayed], after, 0, fn)
    return view


def _gather_in_diagonal(view, relayed, after, name):
    def fn(bufs, pairs, send, recv):
        (send_in, recv_in), = pairs
        x, y, c = _place()
        sibling = (x, y, 1 - c)
        any_quarter = _quarter(_weight_region(bufs[0], 0, 0), 0)
        for j in range(2):
            cp = _remote(any_quarter, send_in.at[j], recv_in.at[j], sibling)
            cp.wait_recv()
            cp.wait_send()

        def go(ss):
            _remote(_weight_region(bufs[0], ss ^ _FLIPS[2], c), send.at[0], recv.at[0], sibling).start()

        _for_my_shard(go)

    sems, (view,) = _comm_call(name, [view], [relayed], after, 1, fn)
    return sems, view


def _gather_in_diagonal_end(view, forwarded, after, name):
    def fn(bufs, pairs, *_):
        (send, recv), = pairs
        x, y, c = _place()
        cp = _remote(_weight_region(bufs[0], 0, 0), send.at[0], recv.at[0], (x, y, 1 - c))
        cp.wait_recv()
        cp.wait_send()

    _, (view,) = _comm_call(name, [view], [forwarded], after, 0, fn)
    return view


def _in_proj_shard(h1, wi, proj, shard_arr, name):
    s, d = h1.shape
    n = wi.shape[1]
    tn = 256
    blocks = n // (N_CHIPS * tn)
    given = [] if proj is None else [proj]

    def body(shard_ref, h_ref, w_ref, *rest):
        del shard_ref
        rest[-1][...] = jnp.dot(h_ref[...], w_ref[...], preferred_element_type=F32)

    grid_spec = pltpu.PrefetchScalarGridSpec(
        num_scalar_prefetch=1, grid=(blocks,),
        in_specs=[pl.BlockSpec((s, d), lambda j, shard_ref: (0, 0)),
                  pl.BlockSpec((d, tn), lambda j, shard_ref: (0, shard_ref[0] * blocks + j))] + [ANY] * len(given),
        out_specs=pl.BlockSpec((s, tn), lambda j, shard_ref: (0, shard_ref[0] * blocks + j)))
    return pl.pallas_call(
        body, name=name, grid_spec=grid_spec,
        out_shape=jax.ShapeDtypeStruct((s, n), F32),
        input_output_aliases={3: 0} if given else {},
        compiler_params=_params(("arbitrary",)),
    )(shard_arr, h1, wi, *given)


def _sibling_handshake():
    x, y, c = _place()
    barrier = pltpu.get_barrier_semaphore()
    pl.semaphore_signal(barrier, inc=1, device_id=(x, y, 1 - c), device_id_type=MESH)
    pl.semaphore_wait(barrier, 1)


def _split_start(name, bufs, n_sems, copies, sibling_id=None):
    n = len(bufs)

    def body(*refs):
        if sibling_id is not None:
            _sibling_handshake()
        send_sems, recv_sems = refs[n:n + 2]
        for cp in copies(refs[n + 2:], send_sems, recv_sems):
            cp.start()

    outs = pl.pallas_call(
        body, name=name,
        in_specs=[_HBM] * n, out_specs=[_SEM, _SEM] + [_HBM] * n,
        out_shape=[pltpu.SemaphoreType.DMA((n_sems,)), pltpu.SemaphoreType.DMA((n_sems,))]
        + [pltpu.HBM(b.shape, b.dtype) for b in bufs],
        input_output_aliases={i: 2 + i for i in range(n)},
        compiler_params=pltpu.CompilerParams(has_side_effects=_EFFECT, collective_id=sibling_id),
    )(*[_in_hbm(b) for b in bufs])
    return outs[0], outs[1], list(outs[2:])


def _split_wait(name, bufs, send_sems, recv_sems, copies, after):
    n = len(bufs)

    def body(*refs):
        send_ref, recv_ref = refs[n:n + 2]
        for cp in copies(refs[n + 3:], send_ref, recv_ref):
            cp.wait()

    return list(pl.pallas_call(
        body, name=name,
        in_specs=[_HBM] * n + [_SEM, _SEM, ANY], out_specs=[_HBM] * n,
        out_shape=[pltpu.HBM(b.shape, b.dtype) for b in bufs],
        input_output_aliases={i: i for i in range(n)},
        compiler_params=pltpu.CompilerParams(has_side_effects=_EFFECT),
    )(*bufs, send_sems, recv_sems, after))


def _halves_copies(n_w):
    def copies(bufs, send_sems, recv_sems):
        x, y, c = _place()
        out = []
        for w in range(n_w):
            view, land = bufs[w], bufs[n_w + w]
            src = view.at[1 - c] if len(view.shape) == 3 else view.at[:, 1 - c]
            out.append(pltpu.make_async_remote_copy(
                src_ref=src, dst_ref=land, send_sem=send_sems.at[w], recv_sem=recv_sems.at[w],
                device_id=(x, y, 1 - c), device_id_type=MESH))
        return out
    return copies


def _pieces_copies(n_w):
    def copies(bufs, send_sems, recv_sems):
        x, y, c = _place()
        out = []
        for w in range(n_w):
            for j, (cx, cy) in enumerate(_other_chips(x, y)):
                out.append(pltpu.make_async_remote_copy(
                    src_ref=bufs[w].at[2 * cx + cy], dst_ref=bufs[n_w + w].at[j],
                    send_sem=send_sems.at[3 * w + j], recv_sem=recv_sems.at[3 * w + j],
                    device_id=(cx, cy, c), device_id_type=MESH))
        return out
    return copies


def _join_copies(n_w):
    def copies(bufs, send_sems, recv_sems):
        x, y, c = _place()
        return [pltpu.make_async_remote_copy(
            src_ref=bufs[w].at[c], dst_ref=bufs[w].at[c], send_sem=send_sems.at[w], recv_sem=recv_sems.at[w],
            device_id=(x, y, 1 - c), device_id_type=MESH) for w in range(n_w)]
    return copies


def _halves_landing(view):
    shape = view.shape[1:] if view.ndim == 3 else (N_CHIPS,) + view.shape[2:]
    return lax.empty(shape, BF16)


_SIBLING_IDS = {"halves_down": 1, "halves_gate_up": 2, "halves_out": 3, "halves_in": 4,
                "join_down": 5, "join_gate_up": 6, "join_out": 7, "join_in": 8}


def _halves_start(tag, grads, column_sharded):
    views = [_weight_view(g, cs) for g, cs in zip(grads, column_sharded)]
    n = len(views)
    return _split_start("halves_start_" + tag, views + [_halves_landing(v) for v in views], n, _halves_copies(n),
                        sibling_id=_SIBLING_IDS["halves_" + tag])


def _halves_wait(tag, state, after):
    send_sems, recv_sems, bufs = state
    n = len(bufs) // 2
    bufs = _split_wait("halves_wait_" + tag, bufs, send_sems, recv_sems, _halves_copies(n), after)
    return bufs[:n], bufs[n:]


def _pieces_start(tag, pieces):
    n = len(pieces)
    landing = [lax.empty((3,) + p.shape[1:], BF16) for p in pieces]
    return _split_start("pieces_start_" + tag, list(pieces) + landing, 3 * n, _pieces_copies(n))


def _pieces_wait(tag, state, after):
    send_sems, recv_sems, bufs = state
    n = len(bufs) // 2
    bufs = _split_wait("pieces_wait_" + tag, bufs, send_sems, recv_sems, _pieces_copies(n), after)
    return bufs[:n], bufs[n:]


def _join_start(tag, shards):
    n = len(shards)
    return _split_start("join_start_" + tag, list(shards), n, _join_copies(n), sibling_id=_SIBLING_IDS["join_" + tag])


def _join_wait(tag, state, after):
    send_sems, recv_sems, bufs = state
    bufs = _split_wait("join_wait_" + tag, bufs, send_sems, recv_sems, _join_copies(len(bufs)), after)
    return [b.reshape(2 * b.shape[1], b.shape[2]) for b in bufs]


def _chip_sum_col(g3, sib, c_arr, name):
    _, hk, n = g3.shape
    cols = n // N_CHIPS
    tr = _row_tile(hk, cols * 2, limit=4 * 1024 * 1024)

    def body(c_ref, g_ref, s_ref, o_ref):
        del c_ref
        o_ref[...] = (g_ref[...].astype(F32) + s_ref[...].astype(F32)).astype(BF16)

    grid_spec = pltpu.PrefetchScalarGridSpec(
        num_scalar_prefetch=1, grid=(N_CHIPS, hk // tr),
        in_specs=[pl.BlockSpec((None, tr, cols), lambda p, r, c_ref: (c_ref[0], r, p)),
                  pl.BlockSpec((tr, cols), lambda p, r, c_ref: (r, p))],
        out_specs=pl.BlockSpec((None, tr, cols), lambda p, r, c_ref: (p, r, 0)))
    return pl.pallas_call(
        body, name=name, grid_spec=grid_spec,
        out_shape=jax.ShapeDtypeStruct((N_CHIPS, hk, cols), BF16),
        compiler_params=_params(("parallel", "parallel")),
    )(c_arr, g3, sib)


def _chip_sum_row(g4, sib, c_arr, name):
    _, _, hr, n = g4.shape
    tr = _row_tile(hr, n * 2, limit=4 * 1024 * 1024)

    def body(c_ref, g_ref, s_ref, o_ref):
        del c_ref
        o_ref[...] = (g_ref[...].astype(F32) + s_ref[...].astype(F32)).astype(BF16)

    grid_spec = pltpu.PrefetchScalarGridSpec(
        num_scalar_prefetch=1, grid=(N_CHIPS, hr // tr),
        in_specs=[pl.BlockSpec((None, None, tr, n), lambda p, r, c_ref: (p, c_ref[0], r, 0)),
                  pl.BlockSpec((None, tr, n), lambda p, r, c_ref: (p, r, 0))],
        out_specs=pl.BlockSpec((None, tr, n), lambda p, r, c_ref: (p, r, 0)))
    return pl.pallas_call(
        body, name=name, grid_spec=grid_spec,
        out_shape=jax.ShapeDtypeStruct((N_CHIPS, hr, n), BF16),
        compiler_params=_params(("parallel", "parallel")),
    )(c_arr, g4, sib)


def _sum_pieces(pieces, received, place_arr, name):
    _, r, n = pieces.shape
    tr = _row_tile(r, n * 4, limit=4 * 1024 * 1024)

    def body(p_ref, own_ref, r0_ref, r1_ref, r2_ref, o_ref):
        del p_ref
        acc = own_ref[...].astype(F32) + r0_ref[...].astype(F32)
        acc = acc + r1_ref[...].astype(F32)
        o_ref[...] = acc + r2_ref[...].astype(F32)

    def recv_spec(j):
        return pl.BlockSpec((None, tr, n), lambda i, p_ref: (j, i, 0))

    grid_spec = pltpu.PrefetchScalarGridSpec(
        num_scalar_prefetch=1, grid=(r // tr,),
        in_specs=[pl.BlockSpec((None, tr, n), lambda i, p_ref: (p_ref[0], i, 0)),
                  recv_spec(0), recv_spec(1), recv_spec(2)],
        out_specs=pl.BlockSpec((None, tr, n), lambda i, p_ref: (p_ref[1], i, 0)))
    return pl.pallas_call(
        body, name=name, grid_spec=grid_spec,
        out_shape=jax.ShapeDtypeStruct((2, r, n), F32),
        compiler_params=_params(("parallel",)),
    )(place_arr, pieces, received, received, received)


def _norm_weights_step(parts, w, m, v, after=()):
    rows, d = parts.shape
    after = tuple(after)

    def body(p_ref, w_ref, m_ref, v_ref, *rest):
        g_ref, d_ref, mo_ref, vo_ref, gathered, send_sems, recv_sems = rest[len(after):]
        x, y, c = _place()
        me = 4 * x + 2 * y + c
        gathered[me] = p_ref[...]
        copies = []
        for k in range(1, N_DEV):
            peer = (x ^ ((k >> 2) & 1), y ^ ((k >> 1) & 1), c ^ (k & 1))
            copies.append(pltpu.make_async_remote_copy(
                src_ref=p_ref, dst_ref=gathered.at[me], send_sem=send_sems.at[k - 1],
                recv_sem=recv_sems.at[k - 1], device_id=peer, device_id_type=MESH))
        for cp in copies:
            cp.start()
        for cp in copies:
            cp.wait()
        g = gathered[0]
        for k in range(1, N_DEV):
            g = g + gathered[k]
        delta, m_new, v_new = _adamw_math(w_ref[...], g, m_ref[...], v_ref[...])
        g_ref[...] = g
        d_ref[...] = delta
        mo_ref[...] = m_new
        vo_ref[...] = v_new

    vmem = pl.BlockSpec(memory_space=pltpu.VMEM)
    shp = jax.ShapeDtypeStruct((rows, d), F32)
    return pl.pallas_call(
        body, name="norm_weights_step",
        in_specs=[vmem] * 4 + [ANY] * len(after), out_specs=[vmem] * 4, out_shape=[shp] * 4,
        scratch_shapes=[pltpu.VMEM((N_DEV, rows, d), F32), pltpu.SemaphoreType.DMA((N_DEV - 1,)),
                        pltpu.SemaphoreType.DMA((N_DEV - 1,))],
        compiler_params=pltpu.CompilerParams(has_side_effects=True),
    )(parts, w, m, v, *after)


def kernel(x, norm_mix_w, w_in, w_out, norm_ffn_w, w_gate, w_up, w_down, norm_final_w, loss_target, m_norm_mix_w, m_w_in, m_w_out, m_norm_ffn_w, m_w_gate, m_w_up, m_w_down, m_norm_final_w, v_norm_mix_w, v_w_in, v_w_out, v_norm_ffn_w, v_w_gate, v_w_up, v_w_down, v_norm_final_w):
    s, d = x.shape[1], x.shape[2]
    xs = x.reshape(s, d)
    target = loss_target.reshape(s, d)
    big = {"w_in": (w_in, m_w_in, v_w_in), "w_out": (w_out, m_w_out, v_w_out),
           "w_gate": (w_gate, m_w_gate, v_w_gate), "w_up": (w_up, m_w_up, v_w_up),
           "w_down": (w_down, m_w_down, v_w_down)}
    big = {k: tuple(a.reshape(a.shape[1:]) for a in t) for k, t in big.items()}
    col_names, row_names = ("w_in", "w_gate", "w_up"), ("w_out", "w_down")
    n_in = N_CHIPS * big["w_in"][0].shape[1]
    ffn = N_CHIPS * big["w_gate"][0].shape[1]
    mix = ATTN_WIDTH + RET_WIDTH
    c_arr = lax.axis_index("c").astype(I32).reshape(1)
    shard_arr = (2 * lax.axis_index("x") + lax.axis_index("y")).astype(I32).reshape(1)
    place_arr = jnp.concatenate([shard_arr, c_arr])

    def cast(k, after=()):
        return _weight_view(_cast_into_full(big[k][0], shard_arr, k in col_names, "cast_" + k, after), k in col_names)

    started_in, v_in = _gather_in_start(cast("w_in"), "gather_in_start")

    sec = ATTN_WIDTH

    def section(p, rows):
        return pl.BlockSpec((None, rows, sec), lambda i, j, kk: (p, i, 0))

    h1 = _rms_fwd(xs, norm_mix_w, "rms_mix_fwd", after=[v_in])
    my_shard = shard_arr[0]
    shard_of = [jnp.bitwise_xor(my_shard, f).astype(I32).reshape(1) for f in (0,) + _FLIPS]
    proj = _in_proj_shard(h1, _weight_unview(v_in), None, shard_of[0], "in_proj_own")
    early_views = [cast(k, after=[proj]) for k in ("w_out", "w_gate")]
    v_up, v_down = [cast(k, after=[proj]) for k in ("w_up", "w_down")]
    relayed_in, (v_in,) = _gather_relay(v_in, started_in, 0, early_views + [v_up, v_down], "gather_in_relay")
    started_og, (v_out, v_gate) = _gather_out_gate_start(*early_views, [v_in], "gather_out_gate_start")
    v_in = _gather_in_neighbours_end(v_in, relayed_in, [v_out], "gather_in_neighbours_end")
    proj = _in_proj_shard(h1, _weight_unview(v_in), proj, shard_of[1], "in_proj_x")
    proj = _in_proj_shard(h1, _weight_unview(v_in), proj, shard_of[2], "in_proj_y")
    forwarded_in, v_in = _gather_in_diagonal(v_in, relayed_in, [proj], "gather_in_diagonal")
    wi = _weight_unview(_gather_in_diagonal_end(v_in, forwarded_in, [proj], "gather_in_diagonal_end"))
    proj = _in_proj_shard(h1, wi, proj, shard_of[3], "in_proj_diagonal")
    fs_o, fr_o, v_out = _gather_forward([v_out], [0], *started_og, proj, "gather_forward_out")
    mixed, attn_o, lse = _attn_fwd(proj, after=[v_out])
    relayed_g, (v_gate, v_up) = _gather_relay(v_gate, started_og, 3, [attn_o], "gather_gate_relay",
                                              then=v_up, then_peers=2)
    mixed, ret_raw = _ret_fwd(proj, mixed, after=[v_gate])
    wo, = _gather_end([v_out], fs_o, fr_o, ret_raw, "gather_end_out")
    x1, = _matmul("out_proj", "nn", [mixed, mixed], [wo, wo], [0, 0], s, d, sec, s // 2, 512, sec, [xs], [F32],
                  _epi_residual, b_koff=[0, 1], a_specs=[section(0, s // 2), section(1, s // 2)])
    h2 = _rms_fwd(x1, norm_ffn_w, "rms_ffn_fwd")
    relayed_u, (v_up, v_down) = _gather_relay(v_up, relayed_g, 6, [h2], "gather_up_relay",
                                              then=v_down, then_peers=3)
    v_gate = _gather_in_neighbours_end(v_gate, relayed_g, [v_up], "gather_gate_neighbours_end")
    forwarded_g, v_gate = _gather_in_diagonal(v_gate, relayed_g, [v_up], "gather_gate_diagonal")
    v_up = _gather_in_neighbours_end(v_up, relayed_u, [v_gate], "gather_up_neighbours_end")
    wg = _weight_unview(_gather_in_diagonal_end(v_gate, forwarded_g, [v_up], "gather_gate_diagonal_end"))
    forwarded_u, v_up = _gather_in_diagonal(v_up, relayed_u, [wg], "gather_up_diagonal")
    wu = _weight_unview(_gather_in_diagonal_end(v_up, forwarded_u, [wg], "gather_up_diagonal_end"))
    gate, up, act = _matmul("gate_up", "nn", [h2, h2], [wg, wu], [0, 1], s, ffn, d, s, 512, d, [],
                            [BF16, BF16, BF16], _epi_swiglu, a_single_buffer=True)
    fs, fr, v_down = _gather_forward([v_down], [0], *relayed_u, act, "gather_forward_down", base=6)
    wd, = _gather_end([v_down], fs, fr, act, "gather_end_down")
    x2, = _matmul("down_proj", "nn", [act], [wd], [0], s, d, ffn, s // 2, 512, ffn, [x1], [F32],
                  _epi_residual)
    loss_row, dx2, dx2b, dwf = _final_norm_loss(x2, norm_final_w.reshape(1, d), target, "final_norm_loss")

    names = col_names + row_names
    grads, new = {}, {}

    def chip_sums(tag_names, views, sibs):
        return [(_chip_sum_col if k in col_names else _chip_sum_row)(v, sb, c_arr, "chip_sum_" + k)
                for k, v, sb in zip(tag_names, views, sibs)]

    def piece_sums(tag_names, pieces, received):
        return [_sum_pieces(p, r, place_arr, "sum_pieces_" + k) for k, p, r in zip(tag_names, pieces, received)]

    def update(k):
        new[k] = _adamw(big[k][0], grads[k], big[k][1], big[k][2], "adamw_" + k)

    dgate, dup = _matmul("d_act", "nt", [dx2b], [wd], [0], s, ffn, d, s, 512, d, [gate, up],
                         [BF16, BF16], _epi_swiglu_bwd, a_single_buffer=True)
    g_wd, = _matmul("g_w_down", "tn", [act], [dx2b], [0], ffn, d, s, 512, d, s, [], [BF16], _epi_plain)
    halves_d = _halves_start("down", [g_wd], [False])
    dh2, = _matmul("d_h2", "nt", [dgate, dup], [wg, wu], [0, 0], s, d, ffn, s // 2, 256, ffn, [], [F32],
                   _epi_plain, after=halves_d[2][-1:], a_single_buffer=True)
    pieces_d = _pieces_start("down", chip_sums(["w_down"], *_halves_wait("down", halves_d, dh2)))
    g_wg, g_wu = _matmul("g_w_gate_up", "tn", [h2, h2], [dgate, dup], [0, 1], d, ffn, s, 1024, 512, s, [],
                         [BF16, BF16], _epi_two, after=pieces_d[2][-1:])
    halves_gu = _halves_start("gate_up", [g_wg, g_wu], [True, True])
    dx1, dx1b, dw_ffn = _rms_bwd(x1, norm_ffn_w, dh2, dx2, "rms_ffn_bwd", after=halves_gu[2][-1:])

    dmixed, = _matmul("d_mixed", "nt", [dx1b], [wo], [0], s, mix, d, s // 2, 512, d, [], [F32], _epi_plain)
    pieces_gu = _pieces_start("gate_up", chip_sums(["w_gate", "w_up"], *_halves_wait("gate_up", halves_gu, dmixed)))
    per = sec // 512
    g_wo, = _matmul("g_w_out", "tn", [mixed], [dx1b], [0], mix, d, s, 512, d, s, [], [BF16], _epi_plain,
                    after=pieces_gu[2][-1:],
                    a_specs=[pl.BlockSpec((None, s, 512), lambda i, j, kk: (i // per, 0, i % per))])
    halves_o = _halves_start("out", [g_wo], [False])
    dsec = _attn_bwd(proj, attn_o, lse, dmixed, after=halves_o[2][-1:])
    pieces_o = _pieces_start("out", chip_sums(["w_out"], *_halves_wait("out", halves_o, dsec)))
    dsec = _ret_bwd(proj, ret_raw, dmixed, dsec, after=pieces_o[2][-1:])
    where = [0, 1, 2, 4, 5, 6, 7]
    n_sec = len(where)
    g_wi, = _matmul("g_w_in", "tn", [h1], [dsec], [0], d, n_in, s, 1024, sec, s, [], [BF16], _epi_plain,
                    b_specs=[pl.BlockSpec((None, s, sec), lambda i, j, kk: (j + (j >= 3).astype(I32), 0, 0))])
    halves_i = _halves_start("in", [g_wi], [True])
    dh1, = _matmul("d_h1", "nt", [dsec] * n_sec, [wi] * n_sec, [0] * n_sec, s, d, sec, s // 2, 256, sec, [], [F32],
                   _epi_plain, b_koff=list(range(n_sec)), after=halves_i[2][-1:],
                   a_specs=[section(p, s // 2) for p in where])
    pieces_i = _pieces_start("in", chip_sums(["w_in"], *_halves_wait("in", halves_i, dh1)))
    grad_x, _, dw_mix = _rms_bwd(xs, norm_mix_w, dh1, dx1, "rms_mix_bwd", after=pieces_i[2][-1:])

    def rows8(*vs):
        return jnp.concatenate([v.reshape(1, d) for v in vs] + [jnp.zeros((8 - len(vs), d), F32)], axis=0)

    join_d = _join_start("down", piece_sums(["w_down"], *_pieces_wait("down", pieces_d, grad_x)))
    join_gu = _join_start("gate_up", piece_sums(["w_gate", "w_up"], *_pieces_wait("gate_up", pieces_gu, join_d[2][0])))
    join_o = _join_start("out", piece_sums(["w_out"], *_pieces_wait("out", pieces_o, join_gu[2][0])))
    grads["w_down"], = _join_wait("down", join_d, join_o[2][0])
    update("w_down")
    grads["w_gate"], grads["w_up"] = _join_wait("gate_up", join_gu, new["w_down"][0])
    update("w_gate")
    update("w_up")
    grads["w_out"], = _join_wait("out", join_o, new["w_up"][0])
    update("w_out")
    join_i = _join_start("in", piece_sums(["w_in"], *_pieces_wait("in", pieces_i, new["w_out"][0])))
    ng, nd, nm, nv = _norm_weights_step(
        rows8(dw_mix, dw_ffn, dwf, jnp.broadcast_to(loss_row[:, :1], (1, d))),
        rows8(norm_mix_w, norm_ffn_w, norm_final_w),
        rows8(m_norm_mix_w, m_norm_ffn_w, m_norm_final_w), rows8(v_norm_mix_w, v_norm_ffn_w, v_norm_final_w),
        after=join_i[2][:1])
    grads["w_in"], = _join_wait("in", join_i, ng)
    update("w_in")

    loss = ng[3, 0]

    def pack(small, per_weight):
        lead = lambda a: a.reshape((1,) + a.shape)
        return (small[0:1], lead(per_weight["w_in"]), lead(per_weight["w_out"]), small[1:2],
                lead(per_weight["w_gate"]), lead(per_weight["w_up"]), lead(per_weight["w_down"]), small[2])

    return (loss, grad_x.reshape(1, s, d),
            *pack(ng, {k: new[k][3] for k in names}),
            *pack(nd, {k: new[k][0] for k in names}),
            *pack(nm, {k: new[k][1] for k in names}),
            *pack(nv, {k: new[k][2] for k in names}))
```

```python
import functools
import math

import jax
import jax.numpy as jnp
from jax import lax
from jax.experimental import pallas as pl
from jax.experimental.pallas import tpu as pltpu

F32 = jnp.float32
BF16 = jnp.bfloat16
I32 = jnp.int32
MESH = pl.DeviceIdType.MESH
ANY = pl.BlockSpec(memory_space=pl.ANY)

ATTN_HEADS = 8
ATTN_HEAD_DIM = 128
RET_HEADS = 4
RET_HEAD_DIM = 256
ATTN_WIDTH = ATTN_HEADS * ATTN_HEAD_DIM
RET_WIDTH = RET_HEADS * RET_HEAD_DIM
DILATED_PATTERNS = ((128, 1), (512, 4), (2048, 16))
NORM_EPS = 1e-6
ADAM_LR = 0.001
ADAM_B1 = 0.9
ADAM_B2 = 0.999
ADAM_EPS = 1e-08
ADAM_WD = 0.01
ADAM_STEP = 10

N_CHIPS = 4
N_DEV = 8
NEG_BIG = -1e30
SEQ_TILE = 512
ATTN_FWD_HEADS_PER_STEP = 2
ATTN_HEADS_PER_STEP = 1
VMEM_LIMIT_BYTES = 56 * 1024 * 1024


def _params(semantics=None, vmem=VMEM_LIMIT_BYTES):
    return pltpu.CompilerParams(dimension_semantics=semantics, vmem_limit_bytes=vmem)


def _row_tile(rows, row_bytes, limit=2 * 1024 * 1024, mult=16):
    best = None
    for t in range(mult, rows + 1, mult):
        if rows % t == 0 and t * row_bytes <= limit:
            best = t
    assert best is not None, (rows, row_bytes)
    return best


def _sigmoid(x):
    return 1.0 / (1.0 + jnp.exp(-x))


def _select_by_index(idx, values):
    out = jnp.float32(values[-1])
    for i in range(len(values) - 2, -1, -1):
        out = jnp.where(idx == i, jnp.float32(values[i]), out)
    return out


def _place():
    x, y, c = lax.axis_index("x"), lax.axis_index("y"), lax.axis_index("c")
    return x, y, c


def _cast_into_full(w, shard_arr, column_sharded, name, after=()):
    after = tuple(after)
    rows, cols = w.shape
    tr = _row_tile(rows, cols * 4)
    steps = rows // tr
    if column_sharded:
        out_shape, out_map = (rows, N_CHIPS * cols), (lambda i, s_ref: (i, s_ref[0]))
    else:
        out_shape, out_map = (N_CHIPS * rows, cols), (lambda i, s_ref: (s_ref[0] * steps + i, 0))

    def body(s_ref, w_ref, *rest):
        del s_ref
        rest[-1][...] = w_ref[...].astype(BF16)

    grid_spec = pltpu.PrefetchScalarGridSpec(
        num_scalar_prefetch=1, grid=(steps,),
        in_specs=[pl.BlockSpec((tr, cols), lambda i, s_ref: (i, 0))] + [ANY] * len(after),
        out_specs=pl.BlockSpec((tr, cols), out_map))
    return pl.pallas_call(
        body, name=name, grid_spec=grid_spec,
        out_shape=jax.ShapeDtypeStruct(out_shape, BF16),
        compiler_params=_params(("parallel",)),
    )(shard_arr, w, *after)


def _rms_fwd(x, w, name, after=()):
    rows, d = x.shape
    tr = 256
    after = tuple(after)

    def body(x_ref, w_ref, *rest):
        xv = x_ref[...]
        r = lax.rsqrt(jnp.mean(xv * xv, axis=-1, keepdims=True) + NORM_EPS)
        rest[-1][...] = (xv * r * w_ref[...]).astype(BF16)

    return pl.pallas_call(
        body, name=name, grid=(rows // tr,),
        in_specs=[pl.BlockSpec((tr, d), lambda i: (i, 0)), pl.BlockSpec((1, d), lambda i: (0, 0))]
        + [ANY] * len(after),
        out_specs=pl.BlockSpec((tr, d), lambda i: (i, 0)),
        out_shape=jax.ShapeDtypeStruct((rows, d), BF16),
        compiler_params=_params(("parallel",)),
    )(x, w, *after)


def _rms_bwd(x, w, dh, dres, name, after=()):
    rows, d = x.shape
    tr = 256
    after = tuple(after)

    def body(x_ref, w_ref, dh_ref, dres_ref, *rest):
        dx_ref, dxb_ref, dw_ref = rest[len(after):]
        xv = x_ref[...]
        r = lax.rsqrt(jnp.mean(xv * xv, axis=-1, keepdims=True) + NORM_EPS)
        xhat = xv * r
        dy = dh_ref[...]
        dxhat = dy * w_ref[...]
        dx = dres_ref[...] + r * (dxhat - xhat * jnp.mean(dxhat * xhat, axis=-1, keepdims=True))
        dx_ref[...] = dx
        dxb_ref[...] = dx.astype(BF16)
        part = jnp.sum(dy * xhat, axis=0, keepdims=True)

        @pl.when(pl.program_id(0) == 0)
        def _():
            dw_ref[...] = part

        @pl.when(pl.program_id(0) != 0)
        def _():
            dw_ref[...] += part

    row = pl.BlockSpec((tr, d), lambda i: (i, 0))
    vec = pl.BlockSpec((1, d), lambda i: (0, 0))
    return pl.pallas_call(
        body, name=name, grid=(rows // tr,),
        in_specs=[row, vec, row, row] + [ANY] * len(after),
        out_specs=[row, row, vec],
        out_shape=[jax.ShapeDtypeStruct((rows, d), F32), jax.ShapeDtypeStruct((rows, d), BF16),
                   jax.ShapeDtypeStruct((1, d), F32)],
        compiler_params=_params(("arbitrary",)),
    )(x, w, dh, dres, *after)


def _final_norm_loss(x2, w, target, name):
    rows, d = x2.shape
    tr = 256

    def body(x_ref, w_ref, t_ref, loss_ref, dx_ref, dxb_ref, dw_ref):
        xv = x_ref[...]
        wv = w_ref[...]
        r = lax.rsqrt(jnp.mean(xv * xv, axis=-1, keepdims=True) + NORM_EPS)
        xhat = xv * r
        err = xhat * wv - t_ref[...]
        part_loss = 0.5 * jnp.sum(jnp.mean(err * err, axis=-1, keepdims=True), axis=0, keepdims=True)
        dy = err * (1.0 / d)
        dxhat = dy * wv
        dx = r * (dxhat - xhat * jnp.mean(dxhat * xhat, axis=-1, keepdims=True))
        dx_ref[...] = dx
        dxb_ref[...] = dx.astype(BF16)
        part_dw = jnp.sum(dy * xhat, axis=0, keepdims=True)
        part_loss = jnp.broadcast_to(part_loss, (1, 128))

        @pl.when(pl.program_id(0) == 0)
        def _():
            dw_ref[...] = part_dw
            loss_ref[...] = part_loss

        @pl.when(pl.program_id(0) != 0)
        def _():
            dw_ref[...] += part_dw
            loss_ref[...] += part_loss

    row = pl.BlockSpec((tr, d), lambda i: (i, 0))
    vec = pl.BlockSpec((1, d), lambda i: (0, 0))
    return pl.pallas_call(
        body, name=name, grid=(rows // tr,),
        in_specs=[row, vec, row],
        out_specs=[pl.BlockSpec((1, 128), lambda i: (0, 0)), row, row, vec],
        out_shape=[jax.ShapeDtypeStruct((1, 128), F32), jax.ShapeDtypeStruct((rows, d), F32),
                   jax.ShapeDtypeStruct((rows, d), BF16), jax.ShapeDtypeStruct((1, d), F32)],
        compiler_params=_params(("arbitrary",)),
    )(x2, w, target)


def _adamw_math(w, g, m, v):
    m = ADAM_B1 * m + (1.0 - ADAM_B1) * g
    v = ADAM_B2 * v + (1.0 - ADAM_B2) * (g * g)
    m_hat = m / (1.0 - ADAM_B1 ** ADAM_STEP)
    v_hat = v / (1.0 - ADAM_B2 ** ADAM_STEP)
    delta = -ADAM_LR * (m_hat / (jnp.sqrt(v_hat) + ADAM_EPS) + ADAM_WD * w)
    return delta, m, v


def _adamw(w, g, m, v, name):
    rows, cols = w.shape
    tr = _row_tile(rows, cols * 4)

    def body(w_ref, g_ref, m_ref, v_ref, d_ref, mo_ref, vo_ref, go_ref):
        g = g_ref[...]
        delta, m_new, v_new = _adamw_math(w_ref[...], g, m_ref[...], v_ref[...])
        d_ref[...] = delta
        mo_ref[...] = m_new
        vo_ref[...] = v_new
        go_ref[...] = g

    blk = pl.BlockSpec((tr, cols), lambda i: (i, 0))
    shp = jax.ShapeDtypeStruct((rows, cols), F32)
    return pl.pallas_call(
        body, name=name, grid=(rows // tr,),
        in_specs=[blk] * 4, out_specs=[blk] * 4, out_shape=[shp] * 4,
        compiler_params=_params(("parallel",)),
    )(w, g, m, v)


_DOT_DIMS = {"nn": ((1,), (0,)), "nt": ((1,), (1,)), "tn": ((0,), (0,))}


def _matmul(name, mode, a_list, b_list, acc_of, m, n, k, tm, tn, tk, extras, out_dtypes, epilogue,
            a_koff=None, b_koff=None, after=(), a_specs=None, b_specs=None, a_single_buffer=False):
    after = tuple(after)
    assert m % tm == 0 and n % tn == 0 and k % tk == 0, (name, m, n, k, tm, tn, tk)
    nk = k // tk
    n_acc = max(acc_of) + 1
    n_pairs = len(a_list)
    a_koff = a_koff or [0] * n_pairs
    b_koff = b_koff or [0] * n_pairs
    dims = (_DOT_DIMS[mode], ((), ()))
    n_ext, n_out = len(extras), len(out_dtypes)

    def body(*refs):
        a_refs = refs[:n_pairs]
        b_refs = refs[n_pairs:2 * n_pairs]
        e_refs = refs[2 * n_pairs:2 * n_pairs + n_ext]
        first_out = 2 * n_pairs + n_ext + len(after)
        o_refs = refs[first_out:first_out + n_out]
        acc_refs = refs[first_out + n_out:]

        parts = [None] * n_acc
        for p in range(n_pairs):
            d = lax.dot_general(a_refs[p][...], b_refs[p][...], dims, preferred_element_type=F32)
            parts[acc_of[p]] = d if parts[acc_of[p]] is None else parts[acc_of[p]] + d

        def finish(accs):
            outs = epilogue(accs, [e[...] for e in e_refs])
            for o_ref, o in zip(o_refs, outs):
                o_ref[...] = o.astype(o_ref.dtype)

        if nk == 1:
            finish(parts)
        else:
            kk = pl.program_id(2)

            @pl.when(kk == 0)
            def _():
                for acc_ref, part in zip(acc_refs, parts):
                    acc_ref[...] = part

            @pl.when(kk != 0)
            def _():
                for acc_ref, part in zip(acc_refs, parts):
                    acc_ref[...] += part

            @pl.when(kk == nk - 1)
            def _():
                finish([acc_ref[...] for acc_ref in acc_refs])

    def a_spec(off):
        mode_a = pl.Buffered(1) if a_single_buffer else None
        if mode == "tn":
            return pl.BlockSpec((tk, tm), lambda i, j, kk: (kk + off, i), pipeline_mode=mode_a)
        return pl.BlockSpec((tm, tk), lambda i, j, kk: (i, kk + off), pipeline_mode=mode_a)

    def b_spec(off):
        if mode == "nt":
            return pl.BlockSpec((tn, tk), lambda i, j, kk: (j, kk + off))
        return pl.BlockSpec((tk, tn), lambda i, j, kk: (kk + off, j))

    tile = pl.BlockSpec((tm, tn), lambda i, j, kk: (i, j))
    scratch = [pltpu.VMEM((tm, tn), F32) for _ in range(n_acc)] if nk > 1 else []
    return pl.pallas_call(
        body, name=name, grid=(m // tm, n // tn, nk),
        in_specs=(a_specs or [a_spec(o) for o in a_koff]) + (b_specs or [b_spec(o) for o in b_koff])
        + [tile] * n_ext + [ANY] * len(after),
        out_specs=[tile] * n_out,
        out_shape=[jax.ShapeDtypeStruct((m, n), dt) for dt in out_dtypes],
        scratch_shapes=scratch,
        compiler_params=_params(("parallel", "parallel", "arbitrary")),
    )(*a_list, *b_list, *extras, *after)


def _epi_plain(accs, extras):
    return (accs[0],)


def _epi_residual(accs, extras):
    return (accs[0] + extras[0],)


def _epi_two(accs, extras):
    return accs[0], accs[1]


def _epi_swiglu(accs, extras):
    g, u = accs
    return g, u, g * _sigmoid(g) * u


def _epi_swiglu_bwd(accs, extras):
    da = accs[0]
    g, u = (e.astype(F32) for e in extras)
    sg = _sigmoid(g)
    dg = da * u * sg * (1.0 + g * (1.0 - sg))
    du = da * g * sg
    return dg, du


_NT_DIMS = (((1,), (1,)), ((), ()))
_TN_DIMS = (((0,), (0,)), ((), ()))


def _tile_delta(tq, tk):
    return lax.broadcasted_iota(I32, (tq, tk), 0) - lax.broadcasted_iota(I32, (tq, tk), 1)


def _attn_log_count(delta):
    count = jnp.zeros(delta.shape, I32)
    for window, dilation in DILATED_PATTERNS:
        hit = ((delta & (dilation - 1)) == 0) & (delta <= window)
        count = count + jnp.where(hit, 1, 0)
    valid = (delta >= 0) & (count > 0)
    logm = jnp.where(count == 3, math.log(3.0), jnp.where(count == 2, math.log(2.0), 0.0))
    return jnp.where(valid, logm, NEG_BIG)


def _fill_attn_log_count(tab_ref):
    nb, t, _ = tab_ref.shape
    base = _tile_delta(t, t)
    for b in range(nb):
        tab_ref[b] = _attn_log_count(base + b * t)


def _fill_attn_bias(tab_ref, log_count_ref, slope):
    nb, t, _ = tab_ref.shape
    dist = _tile_delta(t, t).astype(F32)
    for b in range(nb):
        tab_ref[b] = log_count_ref[b] - slope * (dist + float(b * t))


def _fill_ret_decay(tab_ref, log_gamma):
    nb, t, _ = tab_ref.shape
    base = _tile_delta(t, t)
    for b in range(nb):
        tab_ref[b] = _ret_decay(base + b * t, log_gamma)


def _alibi_slopes():
    return [2.0 ** (-8.0 * (h + 1) / ATTN_HEADS) for h in range(ATTN_HEADS)]


def _attn_fwd(proj, after=()):
    s = proj.shape[0]
    t = SEQ_TILE
    hd = ATTN_HEAD_DIM
    hp = ATTN_FWD_HEADS_PER_STEP
    ng = ATTN_HEADS // hp
    w = hp * hd
    scale = 1.0 / math.sqrt(hd)
    slopes = _alibi_slopes()

    def body(q_ref, k_ref, v_ref, *rest):
        mix_ref, o_ref, lse_ref, kb, vb, bias_tab, log_count_tab = rest[len(after):]
        g = pl.program_id(0)
        i = pl.program_id(1)

        @pl.when((g == 0) & (i == 0))
        def _():
            _fill_attn_log_count(log_count_tab)

        @pl.when(i == 0)
        def _():
            kb[...] = k_ref[...].astype(BF16)
            vb[...] = v_ref[...].astype(BF16)
            for u in range(hp):
                _fill_attn_bias(bias_tab.at[u], log_count_tab, _select_by_index(g * hp + u, slopes))

        qs = [q_ref[:, u * hd:(u + 1) * hd].astype(BF16) for u in range(hp)]

        def step(j, carry):
            rows = pl.ds(pl.multiple_of(j * t, t), t)
            out = []
            for u in range(hp):
                m_i, l_i, acc = carry[u]
                lanes = slice(u * hd, (u + 1) * hd)
                sc = lax.dot_general(qs[u], kb[rows, lanes], _NT_DIMS, preferred_element_type=F32) * scale
                sc = sc + bias_tab[u, i - j]
                m_new = jnp.maximum(m_i, jnp.max(sc, axis=-1, keepdims=True))
                p = jnp.exp(sc - m_new)
                alpha = jnp.exp(m_i - m_new)
                l_new = alpha * l_i + jnp.sum(p, axis=-1, keepdims=True)
                acc = alpha * acc + jnp.dot(p.astype(BF16), vb[rows, lanes], preferred_element_type=F32)
                out.append((m_new, l_new, acc))
            return tuple(out)

        init = (jnp.full((t, 1), NEG_BIG, F32), jnp.zeros((t, 1), F32), jnp.zeros((t, hd), F32))
        final = lax.fori_loop(0, i + 1, step, (init,) * hp)
        for u in range(hp):
            m_i, l_i, acc = final[u]
            lanes = slice(u * hd, (u + 1) * hd)
            out = acc / l_i
            o_ref[:, lanes] = out
            mix_ref[:, lanes] = out.astype(BF16)
            lse_ref[:, lanes] = jnp.broadcast_to(m_i + jnp.log(l_i), (t, hd))

    return pl.pallas_call(
        body, name="attn_fwd", grid=(ng, s // t),
        in_specs=[pl.BlockSpec((t, w), lambda g, i: (i, g)),
                  pl.BlockSpec((s, w), lambda g, i: (0, ng + g)),
                  pl.BlockSpec((s, w), lambda g, i: (0, 2 * ng + g))] + [ANY] * len(after),
        out_specs=[pl.BlockSpec((None, t, w), lambda g, i: (0, i, g))] + [pl.BlockSpec((t, w), lambda g, i: (i, g))] * 2,
        out_shape=[jax.ShapeDtypeStruct((2, s, ATTN_WIDTH), BF16),
                   jax.ShapeDtypeStruct((s, ATTN_WIDTH), F32),
                   jax.ShapeDtypeStruct((s, ATTN_WIDTH), F32)],
        scratch_shapes=[pltpu.VMEM((s, w), BF16), pltpu.VMEM((s, w), BF16), pltpu.VMEM((hp, s // t, t, t), F32),
                        pltpu.VMEM((s // t, t, t), F32)],
        compiler_params=_params(("arbitrary", "arbitrary")),
    )(proj, proj, proj, *after)


def _attn_bwd(proj, attn_out, lse, dmixed, after=()):
    after = tuple(after)
    s = proj.shape[0]
    t = SEQ_TILE
    nt = s // t
    hd = ATTN_HEAD_DIM
    hp = ATTN_HEADS_PER_STEP
    ng = ATTN_HEADS // hp
    w = hp * hd
    scale = 1.0 / math.sqrt(hd)
    slopes = _alibi_slopes()

    def body(q_ref, k_ref, v_ref, o_ref, lse_ref, do_ref, *rest):
        dsec_ref, qb, kb, vb, dob, dsum, dq_acc, bias_tab, log_count_tab = rest[len(after):]
        g = pl.program_id(0)

        @pl.when(g == 0)
        def _():
            _fill_attn_log_count(log_count_tab)

        qb[...] = q_ref[...].astype(BF16)
        kb[...] = k_ref[...].astype(BF16)
        vb[...] = v_ref[...].astype(BF16)
        dob[...] = do_ref[...].astype(BF16)
        for u in range(hp):
            lanes = slice(u * hd, (u + 1) * hd)
            _fill_attn_bias(bias_tab.at[u], log_count_tab, _select_by_index(g * hp + u, slopes))
            rowsum = jnp.sum(do_ref[:, lanes] * o_ref[:, lanes], axis=-1, keepdims=True)
            dsum[:, lanes] = jnp.broadcast_to(rowsum, (s, hd))
        dq_acc[...] = jnp.zeros((s, w), F32)

        def over_keys(j, _):
            krows = pl.ds(pl.multiple_of(j * t, t), t)

            def over_queries(i, carry):
                qrows = pl.ds(pl.multiple_of(i * t, t), t)
                out = []
                for u in range(hp):
                    dk, dv = carry[u]
                    lanes = slice(u * hd, (u + 1) * hd)
                    qi, doi = qb[qrows, lanes], dob[qrows, lanes]
                    kj, vj = kb[krows, lanes], vb[krows, lanes]
                    lse_i = lse_ref[qrows, lanes][:, :1]
                    dsum_i = dsum[qrows, lanes][:, :1]
                    sc = lax.dot_general(qi, kj, _NT_DIMS, preferred_element_type=F32) * scale
                    p = jnp.exp(sc + bias_tab[u, i - j] - lse_i)
                    dp = lax.dot_general(doi, vj, _NT_DIMS, preferred_element_type=F32)
                    ds = (p * (dp - dsum_i)).astype(BF16)
                    dv = dv + lax.dot_general(p.astype(BF16), doi, _TN_DIMS, preferred_element_type=F32)
                    dk = dk + lax.dot_general(ds, qi, _TN_DIMS, preferred_element_type=F32)
                    dq_acc[qrows, lanes] += jnp.dot(ds, kj, preferred_element_type=F32)
                    out.append((dk, dv))
                return tuple(out)

            zero = jnp.zeros((t, hd), F32)
            final = lax.fori_loop(j, nt, over_queries, ((zero, zero),) * hp)
            for u in range(hp):
                lanes = slice(u * hd, (u + 1) * hd)
                dsec_ref[1, krows, lanes] = (final[u][0] * scale).astype(BF16)
                dsec_ref[2, krows, lanes] = final[u][1].astype(BF16)
            return 0

        lax.fori_loop(0, nt, over_keys, 0)
        dsec_ref[0] = (dq_acc[...] * scale).astype(BF16)

    def col(off):
        return pl.BlockSpec((s, w), lambda g: (0, off + g))

    return pl.pallas_call(
        body, name="attn_bwd", grid=(ng,),
        in_specs=[col(0), col(ng), col(2 * ng), col(0), col(0), col(0)] + [ANY] * len(after),
        out_specs=pl.BlockSpec((4, s, w), lambda g: (0, 0, g)),
        out_shape=jax.ShapeDtypeStruct((8, s, ATTN_WIDTH), BF16),
        scratch_shapes=[pltpu.VMEM((s, w), BF16)] * 4 + [pltpu.VMEM((s, w), F32)] * 2
        + [pltpu.VMEM((hp, nt, t, t), F32), pltpu.VMEM((nt, t, t), F32)],
        compiler_params=_params(("arbitrary",)),
    )(proj, proj, proj, attn_out, lse, dmixed, *after)


def _ret_log_gammas():
    return [math.log(1.0 - 2.0 ** (-5.0 - h)) for h in range(RET_HEADS)]


def _ret_decay(delta, log_gamma):
    dec = jnp.exp(delta.astype(F32) * log_gamma) * (1.0 / math.sqrt(RET_HEAD_DIM))
    return jnp.where(delta >= 0, dec, 0.0)


def _ret_fwd(proj, mixed, after=()):
    after = tuple(after)
    s = proj.shape[0]
    t = SEQ_TILE
    hd = RET_HEAD_DIM
    nh = RET_HEADS
    log_gammas = _ret_log_gammas()
    c0 = 3 * ATTN_WIDTH // hd

    def body(q_ref, k_ref, v_ref, g_ref, *rest):
        mix_ref, raw_ref, kb, vb, decay_tab = rest[1 + len(after):]
        h = pl.program_id(0)
        i = pl.program_id(1)

        @pl.when(i == 0)
        def _():
            kb[...] = k_ref[...].astype(BF16)
            vb[...] = v_ref[...].astype(BF16)
            _fill_ret_decay(decay_tab, _select_by_index(h, log_gammas))

        q = q_ref[...].astype(BF16)

        def step(j, acc):
            rows = pl.ds(pl.multiple_of(j * t, t), t)
            sc = lax.dot_general(q, kb[rows, :], _NT_DIMS, preferred_element_type=F32) * decay_tab[i - j]
            return acc + jnp.dot(sc.astype(BF16), vb[rows, :], preferred_element_type=F32)

        ret = lax.fori_loop(0, i + 1, step, jnp.zeros((t, hd), F32))
        raw_ref[...] = ret
        r = lax.rsqrt(jnp.mean(ret * ret, axis=-1, keepdims=True) + NORM_EPS)
        g = g_ref[...]
        mix_ref[...] = (g * _sigmoid(g) * (ret * r)).astype(BF16)

    return pl.pallas_call(
        body, name="ret_fwd", grid=(nh, s // t),
        in_specs=[pl.BlockSpec((t, hd), lambda h, i: (i, c0 + h)),
                  pl.BlockSpec((s, hd), lambda h, i: (0, c0 + nh + h)),
                  pl.BlockSpec((s, hd), lambda h, i: (0, c0 + 2 * nh + h)),
                  pl.BlockSpec((t, hd), lambda h, i: (i, c0 + 3 * nh + h))] + [ANY] * (1 + len(after)),
        out_specs=[pl.BlockSpec((None, t, hd), lambda h, i: (1, i, h)), pl.BlockSpec((t, hd), lambda h, i: (i, h))],
        out_shape=[jax.ShapeDtypeStruct(mixed.shape, BF16), jax.ShapeDtypeStruct((s, RET_WIDTH), F32)],
        input_output_aliases={4: 0},
        scratch_shapes=[pltpu.VMEM((s, hd), BF16), pltpu.VMEM((s, hd), BF16), pltpu.VMEM((s // t, t, t), F32)],
        compiler_params=_params(("arbitrary", "arbitrary")),
    )(proj, proj, proj, proj, mixed, *after)


def _ret_bwd(proj, ret_raw, dmixed, dsec, after=()):
    after = tuple(after)
    s = proj.shape[0]
    t = SEQ_TILE
    nt = s // t
    hd = RET_HEAD_DIM
    nh = RET_HEADS
    log_gammas = _ret_log_gammas()
    c0 = 3 * ATTN_WIDTH // hd
    mixed_blocks = ATTN_WIDTH // hd

    def body(q_ref, k_ref, v_ref, g_ref, raw_ref, dmix_ref, *rest):
        dsec_ref, qb, kb, vb, dretb, dq_acc, decay_tab = rest[1 + len(after):]
        h = pl.program_id(0)
        _fill_ret_decay(decay_tab, _select_by_index(h, log_gammas))
        qb[...] = q_ref[...].astype(BF16)
        kb[...] = k_ref[...].astype(BF16)
        vb[...] = v_ref[...].astype(BF16)
        ret = raw_ref[...]
        r = lax.rsqrt(jnp.mean(ret * ret, axis=-1, keepdims=True) + NORM_EPS)
        normed = ret * r
        g = g_ref[...]
        sg = _sigmoid(g)
        dout = dmix_ref[...]
        dsec_ref[3] = (dout * normed * sg * (1.0 + g * (1.0 - sg))).astype(BF16)
        dn = dout * g * sg
        dret = r * (dn - normed * jnp.mean(dn * normed, axis=-1, keepdims=True))
        dretb[...] = dret.astype(BF16)
        dq_acc[...] = jnp.zeros((s, hd), F32)

        def over_keys(j, _):
            krows = pl.ds(pl.multiple_of(j * t, t), t)
            kj = kb[krows, :]
            vj = vb[krows, :]

            def over_queries(i, carry):
                dk, dv = carry
                qrows = pl.ds(pl.multiple_of(i * t, t), t)
                qi = qb[qrows, :]
                doi = dretb[qrows, :]
                dec = decay_tab[i - j]
                a = (lax.dot_general(qi, kj, _NT_DIMS, preferred_element_type=F32) * dec).astype(BF16)
                da = (lax.dot_general(doi, vj, _NT_DIMS, preferred_element_type=F32) * dec).astype(BF16)
                dv = dv + lax.dot_general(a, doi, _TN_DIMS, preferred_element_type=F32)
                dk = dk + lax.dot_general(da, qi, _TN_DIMS, preferred_element_type=F32)
                dq_acc[qrows, :] += jnp.dot(da, kj, preferred_element_type=F32)
                return dk, dv

            zero = jnp.zeros((t, hd), F32)
            dk, dv = lax.fori_loop(j, nt, over_queries, (zero, zero))
            dsec_ref[1, krows, :] = dk.astype(BF16)
            dsec_ref[2, krows, :] = dv.astype(BF16)
            return 0

        lax.fori_loop(0, nt, over_keys, 0)
        dsec_ref[0] = dq_acc[...].astype(BF16)

    def col(off):
        return pl.BlockSpec((s, hd), lambda h: (0, off + h))

    return pl.pallas_call(
        body, name="ret_bwd", grid=(nh,),
        in_specs=[col(c0), col(c0 + nh), col(c0 + 2 * nh), col(c0 + 3 * nh), col(0), col(mixed_blocks)]
        + [ANY] * (1 + len(after)),
        out_specs=pl.BlockSpec((4, s, hd), lambda h: (1, 0, h)),
        out_shape=jax.ShapeDtypeStruct(dsec.shape, BF16),
        input_output_aliases={6: 0},
        scratch_shapes=[pltpu.VMEM((s, hd), BF16)] * 4 + [pltpu.VMEM((s, hd), F32)]
        + [pltpu.VMEM((nt, t, t), F32)],
        compiler_params=_params(("arbitrary",)),
    )(proj, proj, proj, proj, ret_raw, dmixed, dsec, *after)


_FLIPS = (2, 1, 3)


def _other_chips(x, y):
    return [(1 - x, y), (x, 1 - y), (1 - x, 1 - y)]


_HBM = pl.BlockSpec(memory_space=pltpu.HBM)
_SEM = pl.BlockSpec(memory_space=pltpu.SEMAPHORE)
_EFFECT = pltpu.SideEffectType.DATAFLOW_SIDE_EFFECTING


def _in_hbm(a):
    return pltpu.with_memory_space_constraint(a, pltpu.HBM)


def _weight_view(w, column_sharded):
    if column_sharded:
        return w.reshape(2, w.shape[0] // 2, w.shape[1])
    return w.reshape(N_CHIPS, 2, w.shape[0] // (2 * N_CHIPS), w.shape[1])


def _weight_unview(v):
    if v.ndim == 3:
        return v.reshape(2 * v.shape[1], v.shape[2])
    return v.reshape(N_CHIPS * 2 * v.shape[2], v.shape[3])


def _weight_region(buf, shard, half):
    if len(buf.shape) == 3:
        cols = buf.shape[2] // N_CHIPS
        return buf.at[half, :, pl.ds(shard * cols, cols)]
    return buf.at[shard, half]


def _remote(where, send_sem, recv_sem, to):
    return pltpu.make_async_remote_copy(src_ref=where, dst_ref=where, send_sem=send_sem, recv_sem=recv_sem,
                                        device_id=to, device_id_type=MESH)


def _for_my_shard(fn):
    x, y, _ = _place()
    for ss in range(N_CHIPS):
        pl.when(2 * x + y == ss)(functools.partial(fn, ss))


def _gather_forward(views, which, send_sems, recv_sems, after, name, base=0):
    n_w = len(views)
    which = [base // 3 + w for w in which] if base % 3 == 0 else None
    assert which is not None, "base must be a multiple of 3"

    def body(*refs):
        send_in, recv_in = refs[n_w:n_w + 2]
        fwd_send, fwd_recv = refs[n_w + 3:n_w + 5]
        bufs = refs[n_w + 5:]
        x, y, c = _place()
        sibling = (x, y, 1 - c)

        def forward(ss):
            for i, w in enumerate(which):
                for j in range(3):
                    landed = _weight_region(bufs[i], ss ^ _FLIPS[j], c)
                    _remote(landed, send_in.at[3 * w + j], recv_in.at[3 * w + j], sibling).wait_recv()
                    _remote(landed, fwd_send.at[3 * i + j], fwd_recv.at[3 * i + j], sibling).start()

        _for_my_shard(forward)
        for i, w in enumerate(which):
            for j in range(3):
                _remote(_weight_region(bufs[i], 0, 0), send_in.at[3 * w + j], recv_in.at[3 * w + j],
                        sibling).wait_send()

    return pl.pallas_call(
        body, name=name,
        in_specs=[_HBM] * n_w + [_SEM, _SEM, ANY], out_specs=[_SEM, _SEM] + [_HBM] * n_w,
        out_shape=[pltpu.SemaphoreType.DMA((3 * n_w,)), pltpu.SemaphoreType.DMA((3 * n_w,))]
        + [pltpu.HBM(v.shape, BF16) for v in views],
        input_output_aliases={w: 2 + w for w in range(n_w)},
        compiler_params=pltpu.CompilerParams(has_side_effects=_EFFECT),
    )(*views, send_sems, recv_sems, after)


def _gather_end(views, fwd_send, fwd_recv, after, name):
    n_w = len(views)

    def body(*refs):
        fwd_send_ref, fwd_recv_ref = refs[n_w:n_w + 2]
        bufs = refs[n_w + 3:]
        x, y, c = _place()
        for i in range(n_w):
            for j in range(3):
                cp = _remote(_weight_region(bufs[i], 0, 0), fwd_send_ref.at[3 * i + j], fwd_recv_ref.at[3 * i + j],
                             (x, y, 1 - c))
                cp.wait_recv()
                cp.wait_send()

    outs = pl.pallas_call(
        body, name=name,
        in_specs=[_HBM] * n_w + [_SEM, _SEM, ANY], out_specs=[_HBM] * n_w,
        out_shape=[pltpu.HBM(v.shape, BF16) for v in views],
        input_output_aliases={w: w for w in range(n_w)},
        compiler_params=pltpu.CompilerParams(has_side_effects=_EFFECT),
    )(*views, fwd_send, fwd_recv, after)
    return [_weight_unview(o) for o in outs]


def _comm_call(name, bufs, sem_pairs, after, n_new, fn, sibling_id=None):
    n, n_sem, after = len(bufs), 2 * len(sem_pairs), tuple(after)
    n_out_sem = 2 if n_new else 0

    def body(*refs):
        if sibling_id is not None:
            _sibling_handshake()
        sems = refs[n:n + n_sem]
        outs = refs[n + n_sem + len(after):]
        new = outs[:n_out_sem] if n_new else (None, None)
        fn(outs[n_out_sem:], [(sems[2 * i], sems[2 * i + 1]) for i in range(len(sem_pairs))], *new)

    res = pl.pallas_call(
        body, name=name,
        in_specs=[_HBM] * n + [_SEM] * n_sem + [ANY] * len(after),
        out_specs=[_SEM] * n_out_sem + [_HBM] * n,
        out_shape=[pltpu.SemaphoreType.DMA((n_new,))] * n_out_sem + [pltpu.HBM(b.shape, b.dtype) for b in bufs],
        input_output_aliases={i: n_out_sem + i for i in range(n)},
        compiler_params=pltpu.CompilerParams(has_side_effects=_EFFECT, collective_id=sibling_id),
    )(*bufs, *[s for pair in sem_pairs for s in pair], *after)
    return list(res[:n_out_sem]), list(res[n_out_sem:])


def _quarter(piece, q):
    rows = piece.shape[0] // 2
    return piece.at[pl.ds(q * rows, rows)]


def _gather_in_start(view, name):
    def fn(bufs, _, send, recv):
        x, y, c = _place()

        def go(ss):
            for j, chip in enumerate(_other_chips(x, y)[:2]):
                _remote(_weight_region(bufs[0], ss, c), send.at[j], recv.at[j], (*chip, c)).start()

        _for_my_shard(go)

    sems, (view,) = _comm_call(name, [_in_hbm(view)], [], (), 2, fn)
    return sems, view


def _gather_out_gate_start(v_out, v_gate, after, name):
    def fn(bufs, _, send, recv):
        x, y, c = _place()
        chips = _other_chips(x, y)

        def go(ss):
            for j in range(3):
                _remote(_weight_region(bufs[0], ss, c), send.at[j], recv.at[j], (*chips[j], c)).start()
            for j in range(2):
                _remote(_weight_region(bufs[1], ss, c), send.at[3 + j], recv.at[3 + j], (*chips[j], c)).start()

        _for_my_shard(go)

    sems, views = _comm_call(name, [_in_hbm(v_out), _in_hbm(v_gate)], [], after, 5, fn)
    return sems, views


def _gather_relay(view, started, base, after, name, then=None, then_peers=0):
    n_new = 6 + then_peers if then_peers else 4

    def fn(bufs, pairs, send, recv):
        (send_in, recv_in), = pairs
        x, y, c = _place()
        chips = _other_chips(x, y)
        sibling = (x, y, 1 - c)

        def go(ss):
            landed = [_weight_region(bufs[0], ss ^ _FLIPS[j], c) for j in range(2)]
            for j in range(2):
                _remote(landed[j], send_in.at[base + j], recv_in.at[base + j], sibling).wait_recv()
            for j in range(2):
                _remote(_quarter(landed[j], j), send.at[j], recv.at[j], (*chips[1 - j], c)).start()
            for j in range(2):
                _remote(landed[j], send.at[2 + j], recv.at[2 + j], sibling).start()
            for j in range(then_peers):
                _remote(_weight_region(bufs[1], ss, c), send.at[6 + j], recv.at[6 + j], (*chips[j], c)).start()

        _for_my_shard(go)
        for j in range(2):
            _remote(_weight_region(bufs[0], 0, 0), send_in.at[base + j], recv_in.at[base + j], sibling).wait_send()

    views = [view] if then is None else [view, _in_hbm(then)]
    sems, views = _comm_call(name, views, [started], after, n_new, fn)
    return sems, views


def _gather_in_neighbours_end(view, relayed, after, name):
    def fn(bufs, pairs, *_):
        (send, recv), = pairs
        x, y, c = _place()
        for j in range(2):
            cp = _remote(_weight_region(bufs[0], 0, 0), send.at[2 + j], recv.at[2 + j], (x, y, 1 - c))
            cp.wait_recv()
            cp.wait_send()

    _, (view,) = _comm_call(name, [view], [relayed], after, 0, fn)
    return view


def _gather_in_diagonal(view, relayed, after, name, sibling_id):
    def fn(bufs, pairs, send, recv):
        (send_in, recv_in), = pairs
        x, y, c = _place()
        sibling = (x, y, 1 - c)
        any_quarter = _quarter(_weight_region(bufs[0], 0, 0), 0)
        for j in range(2):
            cp = _remote(any_quarter, send_in.at[j], recv_in.at[j], sibling)
            cp.wait_recv()
            cp.wait_send()

        def go(ss):
            _remote(_weight_region(bufs[0], ss ^ _FLIPS[2], c), send.at[0], recv.at[0], sibling).start()

        _for_my_shard(go)

    sems, (view,) = _comm_call(name, [view], [relayed], after, 1, fn, sibling_id=sibling_id)
    return sems, view


def _gather_in_diagonal_end(view, forwarded, after, name):
    def fn(bufs, pairs, *_):
        (send, recv), = pairs
        x, y, c = _place()
        cp = _remote(_weight_region(bufs[0], 0, 0), send.at[0], recv.at[0], (x, y, 1 - c))
        cp.wait_recv()
        cp.wait_send()

    _, (view,) = _comm_call(name, [view], [forwarded], after, 0, fn)
    return view


def _in_proj_shard(h1, wi, proj, shard_arr, name):
    s, d = h1.shape
    n = wi.shape[1]
    tn = 256
    blocks = n // (N_CHIPS * tn)
    given = [] if proj is None else [proj]

    def body(shard_ref, h_ref, w_ref, *rest):
        del shard_ref
        rest[-1][...] = jnp.dot(h_ref[...], w_ref[...], preferred_element_type=F32)

    grid_spec = pltpu.PrefetchScalarGridSpec(
        num_scalar_prefetch=1, grid=(blocks,),
        in_specs=[pl.BlockSpec((s, d), lambda j, shard_ref: (0, 0)),
                  pl.BlockSpec((d, tn), lambda j, shard_ref: (0, shard_ref[0] * blocks + j))] + [ANY] * len(given),
        out_specs=pl.BlockSpec((s, tn), lambda j, shard_ref: (0, shard_ref[0] * blocks + j)))
    return pl.pallas_call(
        body, name=name, grid_spec=grid_spec,
        out_shape=jax.ShapeDtypeStruct((s, n), F32),
        input_output_aliases={3: 0} if given else {},
        compiler_params=_params(("arbitrary",)),
    )(shard_arr, h1, wi, *given)


def _sibling_handshake():
    x, y, c = _place()
    barrier = pltpu.get_barrier_semaphore()
    pl.semaphore_signal(barrier, inc=1, device_id=(x, y, 1 - c), device_id_type=MESH)
    pl.semaphore_wait(barrier, 1)


def _split_start(name, bufs, n_sems, copies, sibling_id=None):
    n = len(bufs)

    def body(*refs):
        if sibling_id is not None:
            _sibling_handshake()
        send_sems, recv_sems = refs[n:n + 2]
        for cp in copies(refs[n + 2:], send_sems, recv_sems):
            cp.start()

    outs = pl.pallas_call(
        body, name=name,
        in_specs=[_HBM] * n, out_specs=[_SEM, _SEM] + [_HBM] * n,
        out_shape=[pltpu.SemaphoreType.DMA((n_sems,)), pltpu.SemaphoreType.DMA((n_sems,))]
        + [pltpu.HBM(b.shape, b.dtype) for b in bufs],
        input_output_aliases={i: 2 + i for i in range(n)},
        compiler_params=pltpu.CompilerParams(has_side_effects=_EFFECT, collective_id=sibling_id),
    )(*[_in_hbm(b) for b in bufs])
    return outs[0], outs[1], list(outs[2:])


def _split_wait(name, bufs, send_sems, recv_sems, copies, after):
    n = len(bufs)
    after = tuple(after) if isinstance(after, (list, tuple)) else (after,)

    def body(*refs):
        send_ref, recv_ref = refs[n:n + 2]
        for cp in copies(refs[n + 2 + len(after):], send_ref, recv_ref):
            cp.wait()

    return list(pl.pallas_call(
        body, name=name,
        in_specs=[_HBM] * n + [_SEM, _SEM] + [ANY] * len(after), out_specs=[_HBM] * n,
        out_shape=[pltpu.HBM(b.shape, b.dtype) for b in bufs],
        input_output_aliases={i: i for i in range(n)},
        compiler_params=pltpu.CompilerParams(has_side_effects=_EFFECT),
    )(*bufs, send_sems, recv_sems, *after))


def _halves_copies(n_w):
    def copies(bufs, send_sems, recv_sems):
        x, y, c = _place()
        out = []
        for w in range(n_w):
            view, land = bufs[w], bufs[n_w + w]
            src = view.at[1 - c] if len(view.shape) == 3 else view.at[:, 1 - c]
            out.append(pltpu.make_async_remote_copy(
                src_ref=src, dst_ref=land, send_sem=send_sems.at[w], recv_sem=recv_sems.at[w],
                device_id=(x, y, 1 - c), device_id_type=MESH))
        return out
    return copies


def _pieces_copies(n_w):
    def copies(bufs, send_sems, recv_sems):
        x, y, c = _place()
        out = []
        for w in range(n_w):
            for j, (cx, cy) in enumerate(_other_chips(x, y)):
                out.append(pltpu.make_async_remote_copy(
                    src_ref=bufs[w].at[2 * cx + cy], dst_ref=bufs[n_w + w].at[j],
                    send_sem=send_sems.at[3 * w + j], recv_sem=recv_sems.at[3 * w + j],
                    device_id=(cx, cy, c), device_id_type=MESH))
        return out
    return copies


def _join_copies(n_w):
    def copies(bufs, send_sems, recv_sems):
        x, y, c = _place()
        return [pltpu.make_async_remote_copy(
            src_ref=bufs[w].at[c], dst_ref=bufs[w].at[c], send_sem=send_sems.at[w], recv_sem=recv_sems.at[w],
            device_id=(x, y, 1 - c), device_id_type=MESH) for w in range(n_w)]
    return copies


def _halves_landing(view):
    shape = view.shape[1:] if view.ndim == 3 else (N_CHIPS,) + view.shape[2:]
    return lax.empty(shape, BF16)


_SIBLING_IDS = {"halves_down": 1, "halves_gate_up": 2, "halves_out": 3, "halves_in": 4,
                "join_down": 5, "join_gate_up": 6, "join_out": 7, "join_in": 8,
                "diagonal_in": 9, "diagonal_gate": 10, "diagonal_up": 11}


def _halves_start(tag, grads, column_sharded):
    views = [_weight_view(g, cs) for g, cs in zip(grads, column_sharded)]
    n = len(views)
    return _split_start("halves_start_" + tag, views + [_halves_landing(v) for v in views], n, _halves_copies(n),
                        sibling_id=_SIBLING_IDS["halves_" + tag])


def _halves_wait(tag, state, after):
    send_sems, recv_sems, bufs = state
    n = len(bufs) // 2
    bufs = _split_wait("halves_wait_" + tag, bufs, send_sems, recv_sems, _halves_copies(n), after)
    return bufs[:n], bufs[n:]


def _pieces_start(tag, pieces):
    n = len(pieces)
    landing = [lax.empty((3,) + p.shape[1:], BF16) for p in pieces]
    return _split_start("pieces_start_" + tag, list(pieces) + landing, 3 * n, _pieces_copies(n))


def _pieces_wait(tag, state, after):
    send_sems, recv_sems, bufs = state
    n = len(bufs) // 2
    bufs = _split_wait("pieces_wait_" + tag, bufs, send_sems, recv_sems, _pieces_copies(n), after)
    return bufs[:n], bufs[n:]


def _join_start(tag, shards):
    n = len(shards)
    return _split_start("join_start_" + tag, list(shards), n, _join_copies(n), sibling_id=_SIBLING_IDS["join_" + tag])


def _join_wait(tag, state, after):
    send_sems, recv_sems, bufs = state
    bufs = _split_wait("join_wait_" + tag, bufs, send_sems, recv_sems, _join_copies(len(bufs)), after)
    return [b.reshape(2 * b.shape[1], b.shape[2]) for b in bufs]


def _chip_sum_col(g3, sib, c_arr, name):
    _, hk, n = g3.shape
    cols = n // N_CHIPS
    tr = _row_tile(hk, cols * 2, limit=4 * 1024 * 1024)

    def body(c_ref, g_ref, s_ref, o_ref):
        del c_ref
        o_ref[...] = (g_ref[...].astype(F32) + s_ref[...].astype(F32)).astype(BF16)

    grid_spec = pltpu.PrefetchScalarGridSpec(
        num_scalar_prefetch=1, grid=(N_CHIPS, hk // tr),
        in_specs=[pl.BlockSpec((None, tr, cols), lambda p, r, c_ref: (c_ref[0], r, p)),
                  pl.BlockSpec((tr, cols), lambda p, r, c_ref: (r, p))],
        out_specs=pl.BlockSpec((None, tr, cols), lambda p, r, c_ref: (p, r, 0)))
    return pl.pallas_call(
        body, name=name, grid_spec=grid_spec,
        out_shape=jax.ShapeDtypeStruct((N_CHIPS, hk, cols), BF16),
        compiler_params=_params(("parallel", "parallel")),
    )(c_arr, g3, sib)


def _chip_sum_row(g4, sib, c_arr, name):
    _, _, hr, n = g4.shape
    tr = _row_tile(hr, n * 2, limit=4 * 1024 * 1024)

    def body(c_ref, g_ref, s_ref, o_ref):
        del c_ref
        o_ref[...] = (g_ref[...].astype(F32) + s_ref[...].astype(F32)).astype(BF16)

    grid_spec = pltpu.PrefetchScalarGridSpec(
        num_scalar_prefetch=1, grid=(N_CHIPS, hr // tr),
        in_specs=[pl.BlockSpec((None, None, tr, n), lambda p, r, c_ref: (p, c_ref[0], r, 0)),
                  pl.BlockSpec((None, tr, n), lambda p, r, c_ref: (p, r, 0))],
        out_specs=pl.BlockSpec((None, tr, n), lambda p, r, c_ref: (p, r, 0)))
    return pl.pallas_call(
        body, name=name, grid_spec=grid_spec,
        out_shape=jax.ShapeDtypeStruct((N_CHIPS, hr, n), BF16),
        compiler_params=_params(("parallel", "parallel")),
    )(c_arr, g4, sib)


def _sum_pieces(pieces, received, place_arr, name):
    _, r, n = pieces.shape
    tr = _row_tile(r, n * 4, limit=4 * 1024 * 1024)

    def body(p_ref, own_ref, r0_ref, r1_ref, r2_ref, o_ref):
        del p_ref
        acc = own_ref[...].astype(F32) + r0_ref[...].astype(F32)
        acc = acc + r1_ref[...].astype(F32)
        o_ref[...] = acc + r2_ref[...].astype(F32)

    def recv_spec(j):
        return pl.BlockSpec((None, tr, n), lambda i, p_ref: (j, i, 0))

    grid_spec = pltpu.PrefetchScalarGridSpec(
        num_scalar_prefetch=1, grid=(r // tr,),
        in_specs=[pl.BlockSpec((None, tr, n), lambda i, p_ref: (p_ref[0], i, 0)),
                  recv_spec(0), recv_spec(1), recv_spec(2)],
        out_specs=pl.BlockSpec((None, tr, n), lambda i, p_ref: (p_ref[1], i, 0)))
    return pl.pallas_call(
        body, name=name, grid_spec=grid_spec,
        out_shape=jax.ShapeDtypeStruct((2, r, n), F32),
        compiler_params=_params(("parallel",)),
    )(place_arr, pieces, received, received, received)


def _norm_weights_step(parts, w, m, v, after=()):
    rows, d = parts.shape
    after = tuple(after)

    def body(p_ref, w_ref, m_ref, v_ref, *rest):
        g_ref, d_ref, mo_ref, vo_ref, gathered, send_sems, recv_sems = rest[len(after):]
        x, y, c = _place()
        me = 4 * x + 2 * y + c
        gathered[me] = p_ref[...]
        copies = []
        for k in range(1, N_DEV):
            peer = (x ^ ((k >> 2) & 1), y ^ ((k >> 1) & 1), c ^ (k & 1))
            copies.append(pltpu.make_async_remote_copy(
                src_ref=p_ref, dst_ref=gathered.at[me], send_sem=send_sems.at[k - 1],
                recv_sem=recv_sems.at[k - 1], device_id=peer, device_id_type=MESH))
        for cp in copies:
            cp.start()
        for cp in copies:
            cp.wait()
        g = gathered[0]
        for k in range(1, N_DEV):
            g = g + gathered[k]
        delta, m_new, v_new = _adamw_math(w_ref[...], g, m_ref[...], v_ref[...])
        g_ref[...] = g
        d_ref[...] = delta
        mo_ref[...] = m_new
        vo_ref[...] = v_new

    vmem = pl.BlockSpec(memory_space=pltpu.VMEM)
    shp = jax.ShapeDtypeStruct((rows, d), F32)
    return pl.pallas_call(
        body, name="norm_weights_step",
        in_specs=[vmem] * 4 + [ANY] * len(after), out_specs=[vmem] * 4, out_shape=[shp] * 4,
        scratch_shapes=[pltpu.VMEM((N_DEV, rows, d), F32), pltpu.SemaphoreType.DMA((N_DEV - 1,)),
                        pltpu.SemaphoreType.DMA((N_DEV - 1,))],
        compiler_params=pltpu.CompilerParams(has_side_effects=True),
    )(parts, w, m, v, *after)


def kernel(x, norm_mix_w, w_in, w_out, norm_ffn_w, w_gate, w_up, w_down, norm_final_w, loss_target, m_norm_mix_w, m_w_in, m_w_out, m_norm_ffn_w, m_w_gate, m_w_up, m_w_down, m_norm_final_w, v_norm_mix_w, v_w_in, v_w_out, v_norm_ffn_w, v_w_gate, v_w_up, v_w_down, v_norm_final_w):
    s, d = x.shape[1], x.shape[2]
    xs = x.reshape(s, d)
    target = loss_target.reshape(s, d)
    big = {"w_in": (w_in, m_w_in, v_w_in), "w_out": (w_out, m_w_out, v_w_out),
           "w_gate": (w_gate, m_w_gate, v_w_gate), "w_up": (w_up, m_w_up, v_w_up),
           "w_down": (w_down, m_w_down, v_w_down)}
    big = {k: tuple(a.reshape(a.shape[1:]) for a in t) for k, t in big.items()}
    col_names, row_names = ("w_in", "w_gate", "w_up"), ("w_out", "w_down")
    n_in = N_CHIPS * big["w_in"][0].shape[1]
    ffn = N_CHIPS * big["w_gate"][0].shape[1]
    mix = ATTN_WIDTH + RET_WIDTH
    c_arr = lax.axis_index("c").astype(I32).reshape(1)
    shard_arr = (2 * lax.axis_index("x") + lax.axis_index("y")).astype(I32).reshape(1)
    place_arr = jnp.concatenate([shard_arr, c_arr])

    def cast(k, after=()):
        return _weight_view(_cast_into_full(big[k][0], shard_arr, k in col_names, "cast_" + k, after), k in col_names)

    started_in, v_in = _gather_in_start(cast("w_in"), "gather_in_start")

    sec = ATTN_WIDTH

    def section(p, rows):
        return pl.BlockSpec((None, rows, sec), lambda i, j, kk: (p, i, 0))

    h1 = _rms_fwd(xs, norm_mix_w, "rms_mix_fwd", after=[v_in])
    my_shard = shard_arr[0]
    shard_of = [jnp.bitwise_xor(my_shard, f).astype(I32).reshape(1) for f in (0,) + _FLIPS]
    proj = _in_proj_shard(h1, _weight_unview(v_in), None, shard_of[0], "in_proj_own")
    early_views = [cast(k, after=[proj]) for k in ("w_out", "w_gate")]
    v_up, v_down = [cast(k, after=[proj]) for k in ("w_up", "w_down")]
    relayed_in, (v_in,) = _gather_relay(v_in, started_in, 0, early_views + [v_up, v_down], "gather_in_relay")
    started_og, (v_out, v_gate) = _gather_out_gate_start(*early_views, [v_in], "gather_out_gate_start")
    v_in = _gather_in_neighbours_end(v_in, relayed_in, [v_out], "gather_in_neighbours_end")
    proj = _in_proj_shard(h1, _weight_unview(v_in), proj, shard_of[1], "in_proj_x")
    proj = _in_proj_shard(h1, _weight_unview(v_in), proj, shard_of[2], "in_proj_y")
    forwarded_in, v_in = _gather_in_diagonal(v_in, relayed_in, [proj], "gather_in_diagonal",
                                             _SIBLING_IDS["diagonal_in"])
    wi = _weight_unview(_gather_in_diagonal_end(v_in, forwarded_in, [proj], "gather_in_diagonal_end"))
    proj = _in_proj_shard(h1, wi, proj, shard_of[3], "in_proj_diagonal")
    fs_o, fr_o, v_out = _gather_forward([v_out], [0], *started_og, proj, "gather_forward_out")
    mixed, attn_o, lse = _attn_fwd(proj, after=[v_out])
    relayed_g, (v_gate, v_up) = _gather_relay(v_gate, started_og, 3, [attn_o], "gather_gate_relay",
                                              then=v_up, then_peers=2)
    mixed, ret_raw = _ret_fwd(proj, mixed, after=[v_gate])
    wo, = _gather_end([v_out], fs_o, fr_o, ret_raw, "gather_end_out")
    x1, = _matmul("out_proj", "nn", [mixed, mixed], [wo, wo], [0, 0], s, d, sec, s // 2, 512, sec, [xs], [F32],
                  _epi_residual, b_koff=[0, 1], a_specs=[section(0, s // 2), section(1, s // 2)])
    h2 = _rms_fwd(x1, norm_ffn_w, "rms_ffn_fwd")
    relayed_u, (v_up, v_down) = _gather_relay(v_up, relayed_g, 6, [h2], "gather_up_relay",
                                              then=v_down, then_peers=3)
    v_gate = _gather_in_neighbours_end(v_gate, relayed_g, [v_up], "gather_gate_neighbours_end")
    forwarded_g, v_gate = _gather_in_diagonal(v_gate, relayed_g, [v_up], "gather_gate_diagonal",
                                              _SIBLING_IDS["diagonal_gate"])
    v_up = _gather_in_neighbours_end(v_up, relayed_u, [v_gate], "gather_up_neighbours_end")
    wg = _weight_unview(_gather_in_diagonal_end(v_gate, forwarded_g, [v_up], "gather_gate_diagonal_end"))
    forwarded_u, v_up = _gather_in_diagonal(v_up, relayed_u, [wg], "gather_up_diagonal",
                                            _SIBLING_IDS["diagonal_up"])
    wu = _weight_unview(_gather_in_diagonal_end(v_up, forwarded_u, [wg], "gather_up_diagonal_end"))
    gate, up, act = _matmul("gate_up", "nn", [h2, h2], [wg, wu], [0, 1], s, ffn, d, s, 512, d, [],
                            [BF16, BF16, BF16], _epi_swiglu, a_single_buffer=True)
    fs, fr, v_down = _gather_forward([v_down], [0], *relayed_u, act, "gather_forward_down", base=6)
    wd, = _gather_end([v_down], fs, fr, act, "gather_end_down")
    x2, = _matmul("down_proj", "nn", [act], [wd], [0], s, d, ffn, s // 2, 512, ffn, [x1], [F32],
                  _epi_residual)
    loss_row, dx2, dx2b, dwf = _final_norm_loss(x2, norm_final_w.reshape(1, d), target, "final_norm_loss")

    names = col_names + row_names
    grads, new = {}, {}

    def chip_sums(tag_names, views, sibs):
        return [(_chip_sum_col if k in col_names else _chip_sum_row)(v, sb, c_arr, "chip_sum_" + k)
                for k, v, sb in zip(tag_names, views, sibs)]

    def piece_sums(tag_names, pieces, received):
        return [_sum_pieces(p, r, place_arr, "sum_pieces_" + k) for k, p, r in zip(tag_names, pieces, received)]

    def update(k):
        new[k] = _adamw(big[k][0], grads[k], big[k][1], big[k][2], "adamw_" + k)

    dgate, dup = _matmul("d_act", "nt", [dx2b], [wd], [0], s, ffn, d, s, 512, d, [gate, up],
                         [BF16, BF16], _epi_swiglu_bwd, a_single_buffer=True)
    g_wd, = _matmul("g_w_down", "tn", [act], [dx2b], [0], ffn, d, s, 512, d, s, [], [BF16], _epi_plain)
    halves_d = _halves_start("down", [g_wd], [False])
    dh2, = _matmul("d_h2", "nt", [dgate, dup], [wg, wu], [0, 0], s, d, ffn, s // 2, 256, ffn, [], [F32],
                   _epi_plain, after=halves_d[2][-1:], a_single_buffer=True)
    pieces_d = _pieces_start("down", chip_sums(["w_down"], *_halves_wait("down", halves_d, dh2)))
    g_wg, g_wu = _matmul("g_w_gate_up", "tn", [h2, h2], [dgate, dup], [0, 1], d, ffn, s, 1024, 512, s, [],
                         [BF16, BF16], _epi_two, after=pieces_d[2][-1:])
    halves_gu = _halves_start("gate_up", [g_wg, g_wu], [True, True])
    dx1, dx1b, dw_ffn = _rms_bwd(x1, norm_ffn_w, dh2, dx2, "rms_ffn_bwd", after=halves_gu[2][-1:])

    dmixed, = _matmul("d_mixed", "nt", [dx1b], [wo], [0], s, mix, d, s // 2, 512, d, [], [F32], _epi_plain)
    pieces_gu = _pieces_start("gate_up", chip_sums(["w_gate", "w_up"], *_halves_wait("gate_up", halves_gu, dmixed)))
    per = sec // 512
    g_wo, = _matmul("g_w_out", "tn", [mixed], [dx1b], [0], mix, d, s, 512, d, s, [], [BF16], _epi_plain,
                    after=pieces_gu[2][-1:],
                    a_specs=[pl.BlockSpec((None, s, 512), lambda i, j, kk: (i // per, 0, i % per))])
    halves_o = _halves_start("out", [g_wo], [False])
    dsec = _attn_bwd(proj, attn_o, lse, dmixed, after=halves_o[2][-1:])
    pieces_o = _pieces_start("out", chip_sums(["w_out"], *_halves_wait("out", halves_o, dsec)))
    dsec = _ret_bwd(proj, ret_raw, dmixed, dsec, after=pieces_o[2][-1:])
    where = [0, 1, 2, 4, 5, 6, 7]
    n_sec = len(where)
    g_wi, = _matmul("g_w_in", "tn", [h1], [dsec], [0], d, n_in, s, 1024, sec, s, [], [BF16], _epi_plain,
                    b_specs=[pl.BlockSpec((None, s, sec), lambda i, j, kk: (j + (j >= 3).astype(I32), 0, 0))])
    halves_i = _halves_start("in", [g_wi], [True])
    dh1, = _matmul("d_h1", "nt", [dsec] * n_sec, [wi] * n_sec, [0] * n_sec, s, d, sec, s // 2, 256, sec, [], [F32],
                   _epi_plain, b_koff=list(range(n_sec)), after=halves_i[2][-1:],
                   a_specs=[section(p, s // 2) for p in where])
    pieces_i = _pieces_start("in", chip_sums(["w_in"], *_halves_wait("in", halves_i, dh1)))
    grad_x, _, dw_mix = _rms_bwd(xs, norm_mix_w, dh1, dx1, "rms_mix_bwd", after=pieces_i[2][-1:])

    def rows8(*vs):
        return jnp.concatenate([v.reshape(1, d) for v in vs] + [jnp.zeros((8 - len(vs), d), F32)], axis=0)

    join_d = _join_start("down", piece_sums(["w_down"], *_pieces_wait("down", pieces_d, grad_x)))
    join_gu = _join_start("gate_up", piece_sums(["w_gate", "w_up"], *_pieces_wait("gate_up", pieces_gu, join_d[2][0])))
    join_o = _join_start("out", piece_sums(["w_out"], *_pieces_wait("out", pieces_o, join_gu[2][0])))
    grads["w_down"], = _join_wait("down", join_d, join_o[2][0])
    update("w_down")
    grads["w_gate"], grads["w_up"] = _join_wait("gate_up", join_gu, new["w_down"][0])
    update("w_gate")
    update("w_up")
    grads["w_out"], = _join_wait("out", join_o, new["w_up"][0])
    update("w_out")
    others_done = [new[k][0] for k in ("w_down", "w_gate", "w_up", "w_out")]
    join_i = _join_start("in", piece_sums(["w_in"], *_pieces_wait("in", pieces_i, others_done)))
    ng, nd, nm, nv = _norm_weights_step(
        rows8(dw_mix, dw_ffn, dwf, jnp.broadcast_to(loss_row[:, :1], (1, d))),
        rows8(norm_mix_w, norm_ffn_w, norm_final_w),
        rows8(m_norm_mix_w, m_norm_ffn_w, m_norm_final_w), rows8(v_norm_mix_w, v_norm_ffn_w, v_norm_final_w),
        after=join_i[2][:1])
    grads["w_in"], = _join_wait("in", join_i, ng)
    update("w_in")

    loss = ng[3, 0]

    def pack(small, per_weight):
        lead = lambda a: a.reshape((1,) + a.shape)
        return (small[0:1], lead(per_weight["w_in"]), lead(per_weight["w_out"]), small[1:2],
                lead(per_weight["w_gate"]), lead(per_weight["w_up"]), lead(per_weight["w_down"]), small[2])

    return (loss, grad_x.reshape(1, s, d),
            *pack(ng, {k: new[k][3] for k in names}),
            *pack(nd, {k: new[k][0] for k in names}),
            *pack(nm, {k: new[k][1] for k in names}),
            *pack(nv, {k: new[k][2] for k in names}))
```

```python
import functools
import math

import jax
import jax.numpy as jnp
from jax import lax
from jax.experimental import pallas as pl
from jax.experimental.pallas import tpu as pltpu

F32 = jnp.float32
BF16 = jnp.bfloat16
I32 = jnp.int32
MESH = pl.DeviceIdType.MESH
ANY = pl.BlockSpec(memory_space=pl.ANY)

ATTN_HEADS = 8
ATTN_HEAD_DIM = 128
RET_HEADS = 4
RET_HEAD_DIM = 256
ATTN_WIDTH = ATTN_HEADS * ATTN_HEAD_DIM
RET_WIDTH = RET_HEADS * RET_HEAD_DIM
DILATED_PATTERNS = ((128, 1), (512, 4), (2048, 16))
NORM_EPS = 1e-6
ADAM_LR = 0.001
ADAM_B1 = 0.9
ADAM_B2 = 0.999
ADAM_EPS = 1e-08
ADAM_WD = 0.01
ADAM_STEP = 10

N_CHIPS = 4
N_DEV = 8
NEG_BIG = -1e30
SEQ_TILE = 512
ATTN_FWD_HEADS_PER_STEP = 2
ATTN_HEADS_PER_STEP = 1
VMEM_LIMIT_BYTES = 56 * 1024 * 1024


def _params(semantics=None, vmem=VMEM_LIMIT_BYTES):
    return pltpu.CompilerParams(dimension_semantics=semantics, vmem_limit_bytes=vmem)


def _row_tile(rows, row_bytes, limit=2 * 1024 * 1024, mult=16):
    best = None
    for t in range(mult, rows + 1, mult):
        if rows % t == 0 and t * row_bytes <= limit:
            best = t
    assert best is not None, (rows, row_bytes)
    return best


def _sigmoid(x):
    return 1.0 / (1.0 + jnp.exp(-x))


def _select_by_index(idx, values):
    out = jnp.float32(values[-1])
    for i in range(len(values) - 2, -1, -1):
        out = jnp.where(idx == i, jnp.float32(values[i]), out)
    return out


def _place():
    x, y, c = lax.axis_index("x"), lax.axis_index("y"), lax.axis_index("c")
    return x, y, c


def _cast_into_full(w, shard_arr, column_sharded, name, after=()):
    after = tuple(after)
    rows, cols = w.shape
    tr = _row_tile(rows, cols * 4)
    steps = rows // tr
    if column_sharded:
        out_shape, out_map = (rows, N_CHIPS * cols), (lambda i, s_ref: (i, s_ref[0]))
    else:
        out_shape, out_map = (N_CHIPS * rows, cols), (lambda i, s_ref: (s_ref[0] * steps + i, 0))

    def body(s_ref, w_ref, *rest):
        del s_ref
        rest[-1][...] = w_ref[...].astype(BF16)

    grid_spec = pltpu.PrefetchScalarGridSpec(
        num_scalar_prefetch=1, grid=(steps,),
        in_specs=[pl.BlockSpec((tr, cols), lambda i, s_ref: (i, 0))] + [ANY] * len(after),
        out_specs=pl.BlockSpec((tr, cols), out_map))
    return pl.pallas_call(
        body, name=name, grid_spec=grid_spec,
        out_shape=jax.ShapeDtypeStruct(out_shape, BF16),
        compiler_params=_params(("parallel",)),
    )(shard_arr, w, *after)


def _rms_fwd(x, w, name, after=()):
    rows, d = x.shape
    tr = 256
    after = tuple(after)

    def body(x_ref, w_ref, *rest):
        xv = x_ref[...]
        r = lax.rsqrt(jnp.mean(xv * xv, axis=-1, keepdims=True) + NORM_EPS)
        rest[-1][...] = (xv * r * w_ref[...]).astype(BF16)

    return pl.pallas_call(
        body, name=name, grid=(rows // tr,),
        in_specs=[pl.BlockSpec((tr, d), lambda i: (i, 0)), pl.BlockSpec((1, d), lambda i: (0, 0))]
        + [ANY] * len(after),
        out_specs=pl.BlockSpec((tr, d), lambda i: (i, 0)),
        out_shape=jax.ShapeDtypeStruct((rows, d), BF16),
        compiler_params=_params(("parallel",)),
    )(x, w, *after)


def _rms_bwd(x, w, dh, dres, name, after=()):
    rows, d = x.shape
    tr = 256
    after = tuple(after)

    def body(x_ref, w_ref, dh_ref, dres_ref, *rest):
        dx_ref, dxb_ref, dw_ref = rest[len(after):]
        xv = x_ref[...]
        r = lax.rsqrt(jnp.mean(xv * xv, axis=-1, keepdims=True) + NORM_EPS)
        xhat = xv * r
        dy = dh_ref[...]
        dxhat = dy * w_ref[...]
        dx = dres_ref[...] + r * (dxhat - xhat * jnp.mean(dxhat * xhat, axis=-1, keepdims=True))
        dx_ref[...] = dx
        dxb_ref[...] = dx.astype(BF16)
        part = jnp.sum(dy * xhat, axis=0, keepdims=True)

        @pl.when(pl.program_id(0) == 0)
        def _():
            dw_ref[...] = part

        @pl.when(pl.program_id(0) != 0)
        def _():
            dw_ref[...] += part

    row = pl.BlockSpec((tr, d), lambda i: (i, 0))
    vec = pl.BlockSpec((1, d), lambda i: (0, 0))
    return pl.pallas_call(
        body, name=name, grid=(rows // tr,),
        in_specs=[row, vec, row, row] + [ANY] * len(after),
        out_specs=[row, row, vec],
        out_shape=[jax.ShapeDtypeStruct((rows, d), F32), jax.ShapeDtypeStruct((rows, d), BF16),
                   jax.ShapeDtypeStruct((1, d), F32)],
        compiler_params=_params(("arbitrary",)),
    )(x, w, dh, dres, *after)


def _final_norm_loss(x2, w, target, name):
    rows, d = x2.shape
    tr = 256

    def body(x_ref, w_ref, t_ref, loss_ref, dx_ref, dxb_ref, dw_ref):
        xv = x_ref[...]
        wv = w_ref[...]
        r = lax.rsqrt(jnp.mean(xv * xv, axis=-1, keepdims=True) + NORM_EPS)
        xhat = xv * r
        err = xhat * wv - t_ref[...]
        part_loss = 0.5 * jnp.sum(jnp.mean(err * err, axis=-1, keepdims=True), axis=0, keepdims=True)
        dy = err * (1.0 / d)
        dxhat = dy * wv
        dx = r * (dxhat - xhat * jnp.mean(dxhat * xhat, axis=-1, keepdims=True))
        dx_ref[...] = dx
        dxb_ref[...] = dx.astype(BF16)
        part_dw = jnp.sum(dy * xhat, axis=0, keepdims=True)
        part_loss = jnp.broadcast_to(part_loss, (1, 128))

        @pl.when(pl.program_id(0) == 0)
        def _():
            dw_ref[...] = part_dw
            loss_ref[...] = part_loss

        @pl.when(pl.program_id(0) != 0)
        def _():
            dw_ref[...] += part_dw
            loss_ref[...] += part_loss

    row = pl.BlockSpec((tr, d), lambda i: (i, 0))
    vec = pl.BlockSpec((1, d), lambda i: (0, 0))
    return pl.pallas_call(
        body, name=name, grid=(rows // tr,),
        in_specs=[row, vec, row],
        out_specs=[pl.BlockSpec((1, 128), lambda i: (0, 0)), row, row, vec],
        out_shape=[jax.ShapeDtypeStruct((1, 128), F32), jax.ShapeDtypeStruct((rows, d), F32),
                   jax.ShapeDtypeStruct((rows, d), BF16), jax.ShapeDtypeStruct((1, d), F32)],
        compiler_params=_params(("arbitrary",)),
    )(x2, w, target)


def _adamw_math(w, g, m, v):
    m = ADAM_B1 * m + (1.0 - ADAM_B1) * g
    v = ADAM_B2 * v + (1.0 - ADAM_B2) * (g * g)
    m_hat = m / (1.0 - ADAM_B1 ** ADAM_STEP)
    v_hat = v / (1.0 - ADAM_B2 ** ADAM_STEP)
    delta = -ADAM_LR * (m_hat / (jnp.sqrt(v_hat) + ADAM_EPS) + ADAM_WD * w)
    return delta, m, v


def _adamw(w, g, m, v, name):
    rows, cols = w.shape
    tr = _row_tile(rows, cols * 4)

    def body(w_ref, g_ref, m_ref, v_ref, d_ref, mo_ref, vo_ref, go_ref):
        g = g_ref[...]
        delta, m_new, v_new = _adamw_math(w_ref[...], g, m_ref[...], v_ref[...])
        d_ref[...] = delta
        mo_ref[...] = m_new
        vo_ref[...] = v_new
        go_ref[...] = g

    blk = pl.BlockSpec((tr, cols), lambda i: (i, 0))
    shp = jax.ShapeDtypeStruct((rows, cols), F32)
    return pl.pallas_call(
        body, name=name, grid=(rows // tr,),
        in_specs=[blk] * 4, out_specs=[blk] * 4, out_shape=[shp] * 4,
        compiler_params=_params(("parallel",)),
    )(w, g, m, v)


_DOT_DIMS = {"nn": ((1,), (0,)), "nt": ((1,), (1,)), "tn": ((0,), (0,))}


def _matmul(name, mode, a_list, b_list, acc_of, m, n, k, tm, tn, tk, extras, out_dtypes, epilogue,
            a_koff=None, b_koff=None, after=(), a_specs=None, b_specs=None, a_single_buffer=False):
    after = tuple(after)
    assert m % tm == 0 and n % tn == 0 and k % tk == 0, (name, m, n, k, tm, tn, tk)
    nk = k // tk
    n_acc = max(acc_of) + 1
    n_pairs = len(a_list)
    a_koff = a_koff or [0] * n_pairs
    b_koff = b_koff or [0] * n_pairs
    dims = (_DOT_DIMS[mode], ((), ()))
    n_ext, n_out = len(extras), len(out_dtypes)

    def body(*refs):
        a_refs = refs[:n_pairs]
        b_refs = refs[n_pairs:2 * n_pairs]
        e_refs = refs[2 * n_pairs:2 * n_pairs + n_ext]
        first_out = 2 * n_pairs + n_ext + len(after)
        o_refs = refs[first_out:first_out + n_out]
        acc_refs = refs[first_out + n_out:]

        parts = [None] * n_acc
        for p in range(n_pairs):
            d = lax.dot_general(a_refs[p][...], b_refs[p][...], dims, preferred_element_type=F32)
            parts[acc_of[p]] = d if parts[acc_of[p]] is None else parts[acc_of[p]] + d

        def finish(accs):
            outs = epilogue(accs, [e[...] for e in e_refs])
            for o_ref, o in zip(o_refs, outs):
                o_ref[...] = o.astype(o_ref.dtype)

        if nk == 1:
            finish(parts)
        else:
            kk = pl.program_id(2)

            @pl.when(kk == 0)
            def _():
                for acc_ref, part in zip(acc_refs, parts):
                    acc_ref[...] = part

            @pl.when(kk != 0)
            def _():
                for acc_ref, part in zip(acc_refs, parts):
                    acc_ref[...] += part

            @pl.when(kk == nk - 1)
            def _():
                finish([acc_ref[...] for acc_ref in acc_refs])

    def a_spec(off):
        mode_a = pl.Buffered(1) if a_single_buffer else None
        if mode == "tn":
            return pl.BlockSpec((tk, tm), lambda i, j, kk: (kk + off, i), pipeline_mode=mode_a)
        return pl.BlockSpec((tm, tk), lambda i, j, kk: (i, kk + off), pipeline_mode=mode_a)

    def b_spec(off):
        if mode == "nt":
            return pl.BlockSpec((tn, tk), lambda i, j, kk: (j, kk + off))
        return pl.BlockSpec((tk, tn), lambda i, j, kk: (kk + off, j))

    tile = pl.BlockSpec((tm, tn), lambda i, j, kk: (i, j))
    scratch = [pltpu.VMEM((tm, tn), F32) for _ in range(n_acc)] if nk > 1 else []
    return pl.pallas_call(
        body, name=name, grid=(m // tm, n // tn, nk),
        in_specs=(a_specs or [a_spec(o) for o in a_koff]) + (b_specs or [b_spec(o) for o in b_koff])
        + [tile] * n_ext + [ANY] * len(after),
        out_specs=[tile] * n_out,
        out_shape=[jax.ShapeDtypeStruct((m, n), dt) for dt in out_dtypes],
        scratch_shapes=scratch,
        compiler_params=_params(("parallel", "parallel", "arbitrary")),
    )(*a_list, *b_list, *extras, *after)


def _epi_plain(accs, extras):
    return (accs[0],)


def _epi_residual(accs, extras):
    return (accs[0] + extras[0],)


def _epi_two(accs, extras):
    return accs[0], accs[1]


def _epi_swiglu(accs, extras):
    g, u = accs
    return g, u, g * _sigmoid(g) * u


def _epi_swiglu_bwd(accs, extras):
    da = accs[0]
    g, u = (e.astype(F32) for e in extras)
    sg = _sigmoid(g)
    dg = da * u * sg * (1.0 + g * (1.0 - sg))
    du = da * g * sg
    return dg, du


_NT_DIMS = (((1,), (1,)), ((), ()))
_TN_DIMS = (((0,), (0,)), ((), ()))


def _tile_delta(tq, tk):
    return lax.broadcasted_iota(I32, (tq, tk), 0) - lax.broadcasted_iota(I32, (tq, tk), 1)


def _attn_log_count(delta):
    count = jnp.zeros(delta.shape, I32)
    for window, dilation in DILATED_PATTERNS:
        hit = ((delta & (dilation - 1)) == 0) & (delta <= window)
        count = count + jnp.where(hit, 1, 0)
    valid = (delta >= 0) & (count > 0)
    logm = jnp.where(count == 3, math.log(3.0), jnp.where(count == 2, math.log(2.0), 0.0))
    return jnp.where(valid, logm, NEG_BIG)


def _fill_attn_log_count(tab_ref):
    nb, t, _ = tab_ref.shape
    base = _tile_delta(t, t)
    for b in range(nb):
        tab_ref[b] = _attn_log_count(base + b * t)


def _fill_attn_bias(tab_ref, log_count_ref, slope):
    nb, t, _ = tab_ref.shape
    dist = _tile_delta(t, t).astype(F32)
    for b in range(nb):
        tab_ref[b] = log_count_ref[b] - slope * (dist + float(b * t))


def _fill_ret_decay(tab_ref, log_gamma):
    nb, t, _ = tab_ref.shape
    base = _tile_delta(t, t)
    for b in range(nb):
        tab_ref[b] = _ret_decay(base + b * t, log_gamma)


def _alibi_slopes():
    return [2.0 ** (-8.0 * (h + 1) / ATTN_HEADS) for h in range(ATTN_HEADS)]


def _attn_fwd(proj, after=()):
    s = proj.shape[0]
    t = SEQ_TILE
    hd = ATTN_HEAD_DIM
    hp = ATTN_FWD_HEADS_PER_STEP
    ng = ATTN_HEADS // hp
    w = hp * hd
    scale = 1.0 / math.sqrt(hd)
    slopes = _alibi_slopes()

    def body(q_ref, k_ref, v_ref, *rest):
        mix_ref, o_ref, lse_ref, kb, vb, bias_tab, log_count_tab = rest[len(after):]
        g = pl.program_id(0)
        i = pl.program_id(1)

        @pl.when((g == 0) & (i == 0))
        def _():
            _fill_attn_log_count(log_count_tab)

        @pl.when(i == 0)
        def _():
            kb[...] = k_ref[...].astype(BF16)
            vb[...] = v_ref[...].astype(BF16)
            for u in range(hp):
                _fill_attn_bias(bias_tab.at[u], log_count_tab, _select_by_index(g * hp + u, slopes))

        qs = [q_ref[:, u * hd:(u + 1) * hd].astype(BF16) for u in range(hp)]

        def step(j, carry):
            rows = pl.ds(pl.multiple_of(j * t, t), t)
            out = []
            for u in range(hp):
                m_i, l_i, acc = carry[u]
                lanes = slice(u * hd, (u + 1) * hd)
                sc = lax.dot_general(qs[u], kb[rows, lanes], _NT_DIMS, preferred_element_type=F32) * scale
                sc = sc + bias_tab[u, i - j]
                m_new = jnp.maximum(m_i, jnp.max(sc, axis=-1, keepdims=True))
                p = jnp.exp(sc - m_new)
                alpha = jnp.exp(m_i - m_new)
                l_new = alpha * l_i + jnp.sum(p, axis=-1, keepdims=True)
                acc = alpha * acc + jnp.dot(p.astype(BF16), vb[rows, lanes], preferred_element_type=F32)
                out.append((m_new, l_new, acc))
            return tuple(out)

        init = (jnp.full((t, 1), NEG_BIG, F32), jnp.zeros((t, 1), F32), jnp.zeros((t, hd), F32))
        final = lax.fori_loop(0, i + 1, step, (init,) * hp)
        for u in range(hp):
            m_i, l_i, acc = final[u]
            lanes = slice(u * hd, (u + 1) * hd)
            out = acc / l_i
            o_ref[:, lanes] = out
            mix_ref[:, lanes] = out.astype(BF16)
            lse_ref[:, lanes] = jnp.broadcast_to(m_i + jnp.log(l_i), (t, hd))

    return pl.pallas_call(
        body, name="attn_fwd", grid=(ng, s // t),
        in_specs=[pl.BlockSpec((t, w), lambda g, i: (i, g)),
                  pl.BlockSpec((s, w), lambda g, i: (0, ng + g)),
                  pl.BlockSpec((s, w), lambda g, i: (0, 2 * ng + g))] + [ANY] * len(after),
        out_specs=[pl.BlockSpec((None, t, w), lambda g, i: (0, i, g))] + [pl.BlockSpec((t, w), lambda g, i: (i, g))] * 2,
        out_shape=[jax.ShapeDtypeStruct((2, s, ATTN_WIDTH), BF16),
                   jax.ShapeDtypeStruct((s, ATTN_WIDTH), F32),
                   jax.ShapeDtypeStruct((s, ATTN_WIDTH), F32)],
        scratch_shapes=[pltpu.VMEM((s, w), BF16), pltpu.VMEM((s, w), BF16), pltpu.VMEM((hp, s // t, t, t), F32),
                        pltpu.VMEM((s // t, t, t), F32)],
        compiler_params=_params(("arbitrary", "arbitrary")),
    )(proj, proj, proj, *after)


def _attn_bwd(proj, attn_out, lse, dmixed, after=()):
    after = tuple(after)
    s = proj.shape[0]
    t = SEQ_TILE
    nt = s // t
    hd = ATTN_HEAD_DIM
    hp = ATTN_HEADS_PER_STEP
    ng = ATTN_HEADS // hp
    w = hp * hd
    scale = 1.0 / math.sqrt(hd)
    slopes = _alibi_slopes()

    def body(q_ref, k_ref, v_ref, o_ref, lse_ref, do_ref, *rest):
        dsec_ref, qb, kb, vb, dob, dsum, dq_acc, bias_tab, log_count_tab = rest[len(after):]
        g = pl.program_id(0)

        @pl.when(g == 0)
        def _():
            _fill_attn_log_count(log_count_tab)

        qb[...] = q_ref[...].astype(BF16)
        kb[...] = k_ref[...].astype(BF16)
        vb[...] = v_ref[...].astype(BF16)
        dob[...] = do_ref[...].astype(BF16)
        for u in range(hp):
            lanes = slice(u * hd, (u + 1) * hd)
            _fill_attn_bias(bias_tab.at[u], log_count_tab, _select_by_index(g * hp + u, slopes))
            rowsum = jnp.sum(do_ref[:, lanes] * o_ref[:, lanes], axis=-1, keepdims=True)
            dsum[:, lanes] = jnp.broadcast_to(rowsum, (s, hd))
        dq_acc[...] = jnp.zeros((s, w), F32)

        def over_keys(j, _):
            krows = pl.ds(pl.multiple_of(j * t, t), t)

            def over_queries(i, carry):
                qrows = pl.ds(pl.multiple_of(i * t, t), t)
                out = []
                for u in range(hp):
                    dk, dv = carry[u]
                    lanes = slice(u * hd, (u + 1) * hd)
                    qi, doi = qb[qrows, lanes], dob[qrows, lanes]
                    kj, vj = kb[krows, lanes], vb[krows, lanes]
                    lse_i = lse_ref[qrows, lanes][:, :1]
                    dsum_i = dsum[qrows, lanes][:, :1]
                    sc = lax.dot_general(qi, kj, _NT_DIMS, preferred_element_type=F32) * scale
                    p = jnp.exp(sc + bias_tab[u, i - j] - lse_i)
                    dp = lax.dot_general(doi, vj, _NT_DIMS, preferred_element_type=F32)
                    ds = (p * (dp - dsum_i)).astype(BF16)
                    dv = dv + lax.dot_general(p.astype(BF16), doi, _TN_DIMS, preferred_element_type=F32)
                    dk = dk + lax.dot_general(ds, qi, _TN_DIMS, preferred_element_type=F32)
                    dq_acc[qrows, lanes] += jnp.dot(ds, kj, preferred_element_type=F32)
                    out.append((dk, dv))
                return tuple(out)

            zero = jnp.zeros((t, hd), F32)
            final = lax.fori_loop(j, nt, over_queries, ((zero, zero),) * hp)
            for u in range(hp):
                lanes = slice(u * hd, (u + 1) * hd)
                dsec_ref[1, krows, lanes] = (final[u][0] * scale).astype(BF16)
                dsec_ref[2, krows, lanes] = final[u][1].astype(BF16)
            return 0

        lax.fori_loop(0, nt, over_keys, 0)
        dsec_ref[0] = (dq_acc[...] * scale).astype(BF16)

    def col(off):
        return pl.BlockSpec((s, w), lambda g: (0, off + g))

    return pl.pallas_call(
        body, name="attn_bwd", grid=(ng,),
        in_specs=[col(0), col(ng), col(2 * ng), col(0), col(0), col(0)] + [ANY] * len(after),
        out_specs=pl.BlockSpec((4, s, w), lambda g: (0, 0, g)),
        out_shape=jax.ShapeDtypeStruct((8, s, ATTN_WIDTH), BF16),
        scratch_shapes=[pltpu.VMEM((s, w), BF16)] * 4 + [pltpu.VMEM((s, w), F32)] * 2
        + [pltpu.VMEM((hp, nt, t, t), F32), pltpu.VMEM((nt, t, t), F32)],
        compiler_params=_params(("arbitrary",)),
    )(proj, proj, proj, attn_out, lse, dmixed, *after)


def _ret_log_gammas():
    return [math.log(1.0 - 2.0 ** (-5.0 - h)) for h in range(RET_HEADS)]


def _ret_decay(delta, log_gamma):
    dec = jnp.exp(delta.astype(F32) * log_gamma) * (1.0 / math.sqrt(RET_HEAD_DIM))
    return jnp.where(delta >= 0, dec, 0.0)


def _ret_fwd(proj, mixed, after=()):
    after = tuple(after)
    s = proj.shape[0]
    t = SEQ_TILE
    hd = RET_HEAD_DIM
    nh = RET_HEADS
    log_gammas = _ret_log_gammas()
    c0 = 3 * ATTN_WIDTH // hd

    def body(q_ref, k_ref, v_ref, g_ref, *rest):
        mix_ref, raw_ref, kb, vb, decay_tab = rest[1 + len(after):]
        h = pl.program_id(0)
        i = pl.program_id(1)

        @pl.when(i == 0)
        def _():
            kb[...] = k_ref[...].astype(BF16)
            vb[...] = v_ref[...].astype(BF16)
            _fill_ret_decay(decay_tab, _select_by_index(h, log_gammas))

        q = q_ref[...].astype(BF16)

        def step(j, acc):
            rows = pl.ds(pl.multiple_of(j * t, t), t)
            sc = lax.dot_general(q, kb[rows, :], _NT_DIMS, preferred_element_type=F32) * decay_tab[i - j]
            return acc + jnp.dot(sc.astype(BF16), vb[rows, :], preferred_element_type=F32)

        ret = lax.fori_loop(0, i + 1, step, jnp.zeros((t, hd), F32))
        raw_ref[...] = ret
        r = lax.rsqrt(jnp.mean(ret * ret, axis=-1, keepdims=True) + NORM_EPS)
        g = g_ref[...]
        mix_ref[...] = (g * _sigmoid(g) * (ret * r)).astype(BF16)

    return pl.pallas_call(
        body, name="ret_fwd", grid=(nh, s // t),
        in_specs=[pl.BlockSpec((t, hd), lambda h, i: (i, c0 + h)),
                  pl.BlockSpec((s, hd), lambda h, i: (0, c0 + nh + h)),
                  pl.BlockSpec((s, hd), lambda h, i: (0, c0 + 2 * nh + h)),
                  pl.BlockSpec((t, hd), lambda h, i: (i, c0 + 3 * nh + h))] + [ANY] * (1 + len(after)),
        out_specs=[pl.BlockSpec((None, t, hd), lambda h, i: (1, i, h)), pl.BlockSpec((t, hd), lambda h, i: (i, h))],
        out_shape=[jax.ShapeDtypeStruct(mixed.shape, BF16), jax.ShapeDtypeStruct((s, RET_WIDTH), F32)],
        input_output_aliases={4: 0},
        scratch_shapes=[pltpu.VMEM((s, hd), BF16), pltpu.VMEM((s, hd), BF16), pltpu.VMEM((s // t, t, t), F32)],
        compiler_params=_params(("arbitrary", "arbitrary")),
    )(proj, proj, proj, proj, mixed, *after)


def _ret_bwd(proj, ret_raw, dmixed, dsec, after=()):
    after = tuple(after)
    s = proj.shape[0]
    t = SEQ_TILE
    nt = s // t
    hd = RET_HEAD_DIM
    nh = RET_HEADS
    log_gammas = _ret_log_gammas()
    c0 = 3 * ATTN_WIDTH // hd
    mixed_blocks = ATTN_WIDTH // hd

    def body(q_ref, k_ref, v_ref, g_ref, raw_ref, dmix_ref, *rest):
        dsec_ref, qb, kb, vb, dretb, dq_acc, decay_tab = rest[1 + len(after):]
        h = pl.program_id(0)
        _fill_ret_decay(decay_tab, _select_by_index(h, log_gammas))
        qb[...] = q_ref[...].astype(BF16)
        kb[...] = k_ref[...].astype(BF16)
        vb[...] = v_ref[...].astype(BF16)
        ret = raw_ref[...]
        r = lax.rsqrt(jnp.mean(ret * ret, axis=-1, keepdims=True) + NORM_EPS)
        normed = ret * r
        g = g_ref[...]
        sg = _sigmoid(g)
        dout = dmix_ref[...]
        dsec_ref[3] = (dout * normed * sg * (1.0 + g * (1.0 - sg))).astype(BF16)
        dn = dout * g * sg
        dret = r * (dn - normed * jnp.mean(dn * normed, axis=-1, keepdims=True))
        dretb[...] = dret.astype(BF16)
        dq_acc[...] = jnp.zeros((s, hd), F32)

        def over_keys(j, _):
            krows = pl.ds(pl.multiple_of(j * t, t), t)
            kj = kb[krows, :]
            vj = vb[krows, :]

            def over_queries(i, carry):
                dk, dv = carry
                qrows = pl.ds(pl.multiple_of(i * t, t), t)
                qi = qb[qrows, :]
                doi = dretb[qrows, :]
                dec = decay_tab[i - j]
                a = (lax.dot_general(qi, kj, _NT_DIMS, preferred_element_type=F32) * dec).astype(BF16)
                da = (lax.dot_general(doi, vj, _NT_DIMS, preferred_element_type=F32) * dec).astype(BF16)
                dv = dv + lax.dot_general(a, doi, _TN_DIMS, preferred_element_type=F32)
                dk = dk + lax.dot_general(da, qi, _TN_DIMS, preferred_element_type=F32)
                dq_acc[qrows, :] += jnp.dot(da, kj, preferred_element_type=F32)
                return dk, dv

            zero = jnp.zeros((t, hd), F32)
            dk, dv = lax.fori_loop(j, nt, over_queries, (zero, zero))
            dsec_ref[1, krows, :] = dk.astype(BF16)
            dsec_ref[2, krows, :] = dv.astype(BF16)
            return 0

        lax.fori_loop(0, nt, over_keys, 0)
        dsec_ref[0] = dq_acc[...].astype(BF16)

    def col(off):
        return pl.BlockSpec((s, hd), lambda h: (0, off + h))

    return pl.pallas_call(
        body, name="ret_bwd", grid=(nh,),
        in_specs=[col(c0), col(c0 + nh), col(c0 + 2 * nh), col(c0 + 3 * nh), col(0), col(mixed_blocks)]
        + [ANY] * (1 + len(after)),
        out_specs=pl.BlockSpec((4, s, hd), lambda h: (1, 0, h)),
        out_shape=jax.ShapeDtypeStruct(dsec.shape, BF16),
        input_output_aliases={6: 0},
        scratch_shapes=[pltpu.VMEM((s, hd), BF16)] * 4 + [pltpu.VMEM((s, hd), F32)]
        + [pltpu.VMEM((nt, t, t), F32)],
        compiler_params=_params(("arbitrary",)),
    )(proj, proj, proj, proj, ret_raw, dmixed, dsec, *after)


_FLIPS = (2, 1, 3)


def _other_chips(x, y):
    return [(1 - x, y), (x, 1 - y), (1 - x, 1 - y)]


_HBM = pl.BlockSpec(memory_space=pltpu.HBM)
_SEM = pl.BlockSpec(memory_space=pltpu.SEMAPHORE)
_EFFECT = pltpu.SideEffectType.DATAFLOW_SIDE_EFFECTING


def _in_hbm(a):
    return pltpu.with_memory_space_constraint(a, pltpu.HBM)


def _weight_view(w, column_sharded):
    if column_sharded:
        return w.reshape(2, w.shape[0] // 2, w.shape[1])
    return w.reshape(N_CHIPS, 2, w.shape[0] // (2 * N_CHIPS), w.shape[1])


def _weight_unview(v):
    if v.ndim == 3:
        return v.reshape(2 * v.shape[1], v.shape[2])
    return v.reshape(N_CHIPS * 2 * v.shape[2], v.shape[3])


def _weight_region(buf, shard, half):
    if len(buf.shape) == 3:
        cols = buf.shape[2] // N_CHIPS
        return buf.at[half, :, pl.ds(shard * cols, cols)]
    return buf.at[shard, half]


def _remote(where, send_sem, recv_sem, to):
    return pltpu.make_async_remote_copy(src_ref=where, dst_ref=where, send_sem=send_sem, recv_sem=recv_sem,
                                        device_id=to, device_id_type=MESH)


def _for_my_shard(fn):
    x, y, _ = _place()
    for ss in range(N_CHIPS):
        pl.when(2 * x + y == ss)(functools.partial(fn, ss))


def _gather_forward(views, which, send_sems, recv_sems, after, name, base=0, sibling_id=None):
    n_w = len(views)
    which = [base // 3 + w for w in which] if base % 3 == 0 else None
    assert which is not None, "base must be a multiple of 3"

    def body(*refs):
        if sibling_id is not None:
            _sibling_handshake()
        send_in, recv_in = refs[n_w:n_w + 2]
        fwd_send, fwd_recv = refs[n_w + 3:n_w + 5]
        bufs = refs[n_w + 5:]
        x, y, c = _place()
        sibling = (x, y, 1 - c)

        def forward(ss):
            for i, w in enumerate(which):
                for j in range(3):
                    landed = _weight_region(bufs[i], ss ^ _FLIPS[j], c)
                    _remote(landed, send_in.at[3 * w + j], recv_in.at[3 * w + j], sibling).wait_recv()
                    _remote(landed, fwd_send.at[3 * i + j], fwd_recv.at[3 * i + j], sibling).start()

        _for_my_shard(forward)
        for i, w in enumerate(which):
            for j in range(3):
                _remote(_weight_region(bufs[i], 0, 0), send_in.at[3 * w + j], recv_in.at[3 * w + j],
                        sibling).wait_send()

    return pl.pallas_call(
        body, name=name,
        in_specs=[_HBM] * n_w + [_SEM, _SEM, ANY], out_specs=[_SEM, _SEM] + [_HBM] * n_w,
        out_shape=[pltpu.SemaphoreType.DMA((3 * n_w,)), pltpu.SemaphoreType.DMA((3 * n_w,))]
        + [pltpu.HBM(v.shape, BF16) for v in views],
        input_output_aliases={w: 2 + w for w in range(n_w)},
        compiler_params=pltpu.CompilerParams(has_side_effects=_EFFECT, collective_id=sibling_id),
    )(*views, send_sems, recv_sems, after)


def _gather_end(views, fwd_send, fwd_recv, after, name):
    n_w = len(views)

    def body(*refs):
        fwd_send_ref, fwd_recv_ref = refs[n_w:n_w + 2]
        bufs = refs[n_w + 3:]
        x, y, c = _place()
        for i in range(n_w):
            for j in range(3):
                cp = _remote(_weight_region(bufs[i], 0, 0), fwd_send_ref.at[3 * i + j], fwd_recv_ref.at[3 * i + j],
                             (x, y, 1 - c))
                cp.wait_recv()
                cp.wait_send()

    outs = pl.pallas_call(
        body, name=name,
        in_specs=[_HBM] * n_w + [_SEM, _SEM, ANY], out_specs=[_HBM] * n_w,
        out_shape=[pltpu.HBM(v.shape, BF16) for v in views],
        input_output_aliases={w: w for w in range(n_w)},
        compiler_params=pltpu.CompilerParams(has_side_effects=_EFFECT),
    )(*views, fwd_send, fwd_recv, after)
    return [_weight_unview(o) for o in outs]


def _comm_call(name, bufs, sem_pairs, after, n_new, fn, sibling_id=None):
    n, n_sem, after = len(bufs), 2 * len(sem_pairs), tuple(after)
    n_out_sem = 2 if n_new else 0

    def body(*refs):
        if sibling_id is not None:
            _sibling_handshake()
        sems = refs[n:n + n_sem]
        outs = refs[n + n_sem + len(after):]
        new = outs[:n_out_sem] if n_new else (None, None)
        fn(outs[n_out_sem:], [(sems[2 * i], sems[2 * i + 1]) for i in range(len(sem_pairs))], *new)

    res = pl.pallas_call(
        body, name=name,
        in_specs=[_HBM] * n + [_SEM] * n_sem + [ANY] * len(after),
        out_specs=[_SEM] * n_out_sem + [_HBM] * n,
        out_shape=[pltpu.SemaphoreType.DMA((n_new,))] * n_out_sem + [pltpu.HBM(b.shape, b.dtype) for b in bufs],
        input_output_aliases={i: n_out_sem + i for i in range(n)},
        compiler_params=pltpu.CompilerParams(has_side_effects=_EFFECT, collective_id=sibling_id),
    )(*bufs, *[s for pair in sem_pairs for s in pair], *after)
    return list(res[:n_out_sem]), list(res[n_out_sem:])


def _quarter(piece, q):
    rows = piece.shape[0] // 2
    return piece.at[pl.ds(q * rows, rows)]


def _gather_in_start(view, name):
    def fn(bufs, _, send, recv):
        x, y, c = _place()

        def go(ss):
            for j, chip in enumerate(_other_chips(x, y)[:2]):
                _remote(_weight_region(bufs[0], ss, c), send.at[j], recv.at[j], (*chip, c)).start()

        _for_my_shard(go)

    sems, (view,) = _comm_call(name, [_in_hbm(view)], [], (), 2, fn)
    return sems, view


def _gather_out_gate_start(v_out, v_gate, after, name):
    def fn(bufs, _, send, recv):
        x, y, c = _place()
        chips = _other_chips(x, y)

        def go(ss):
            for j in range(3):
                _remote(_weight_region(bufs[0], ss, c), send.at[j], recv.at[j], (*chips[j], c)).start()
            for j in range(2):
                _remote(_weight_region(bufs[1], ss, c), send.at[3 + j], recv.at[3 + j], (*chips[j], c)).start()

        _for_my_shard(go)

    sems, views = _comm_call(name, [_in_hbm(v_out), _in_hbm(v_gate)], [], after, 5, fn)
    return sems, views


def _gather_relay(view, started, base, after, name, then=None, then_peers=0):
    n_new = 6 + then_peers if then_peers else 4

    def fn(bufs, pairs, send, recv):
        (send_in, recv_in), = pairs
        x, y, c = _place()
        chips = _other_chips(x, y)
        sibling = (x, y, 1 - c)

        def go(ss):
            landed = [_weight_region(bufs[0], ss ^ _FLIPS[j], c) for j in range(2)]
            for j in range(2):
                _remote(landed[j], send_in.at[base + j], recv_in.at[base + j], sibling).wait_recv()
            for j in range(2):
                _remote(_quarter(landed[j], j), send.at[j], recv.at[j], (*chips[1 - j], c)).start()
            for j in range(2):
                _remote(landed[j], send.at[2 + j], recv.at[2 + j], sibling).start()
            for j in range(then_peers):
                _remote(_weight_region(bufs[1], ss, c), send.at[6 + j], recv.at[6 + j], (*chips[j], c)).start()

        _for_my_shard(go)
        for j in range(2):
            _remote(_weight_region(bufs[0], 0, 0), send_in.at[base + j], recv_in.at[base + j], sibling).wait_send()

    views = [view] if then is None else [view, _in_hbm(then)]
    sems, views = _comm_call(name, views, [started], after, n_new, fn)
    return sems, views


def _gather_in_neighbours_end(view, relayed, after, name):
    def fn(bufs, pairs, *_):
        (send, recv), = pairs
        x, y, c = _place()
        for j in range(2):
            cp = _remote(_weight_region(bufs[0], 0, 0), send.at[2 + j], recv.at[2 + j], (x, y, 1 - c))
            cp.wait_recv()
            cp.wait_send()

    _, (view,) = _comm_call(name, [view], [relayed], after, 0, fn)
    return view


def _gather_in_diagonal(view, relayed, after, name, sibling_id):
    def fn(bufs, pairs, send, recv):
        (send_in, recv_in), = pairs
        x, y, c = _place()
        sibling = (x, y, 1 - c)
        any_quarter = _quarter(_weight_region(bufs[0], 0, 0), 0)
        for j in range(2):
            cp = _remote(any_quarter, send_in.at[j], recv_in.at[j], sibling)
            cp.wait_recv()
            cp.wait_send()

        def go(ss):
            _remote(_weight_region(bufs[0], ss ^ _FLIPS[2], c), send.at[0], recv.at[0], sibling).start()

        _for_my_shard(go)

    sems, (view,) = _comm_call(name, [view], [relayed], after, 1, fn, sibling_id=sibling_id)
    return sems, view


def _gather_in_diagonal_end(view, forwarded, after, name):
    def fn(bufs, pairs, *_):
        (send, recv), = pairs
        x, y, c = _place()
        cp = _remote(_weight_region(bufs[0], 0, 0), send.at[0], recv.at[0], (x, y, 1 - c))
        cp.wait_recv()
        cp.wait_send()

    _, (view,) = _comm_call(name, [view], [forwarded], after, 0, fn)
    return view


def _in_proj_shard(h1, wi, proj, shard_arr, name):
    s, d = h1.shape
    n = wi.shape[1]
    tn = 256
    blocks = n // (N_CHIPS * tn)
    given = [] if proj is None else [proj]

    def body(shard_ref, h_ref, w_ref, *rest):
        del shard_ref
        rest[-1][...] = jnp.dot(h_ref[...], w_ref[...], preferred_element_type=F32)

    grid_spec = pltpu.PrefetchScalarGridSpec(
        num_scalar_prefetch=1, grid=(blocks,),
        in_specs=[pl.BlockSpec((s, d), lambda j, shard_ref: (0, 0)),
                  pl.BlockSpec((d, tn), lambda j, shard_ref: (0, shard_ref[0] * blocks + j))] + [ANY] * len(given),
        out_specs=pl.BlockSpec((s, tn), lambda j, shard_ref: (0, shard_ref[0] * blocks + j)))
    return pl.pallas_call(
        body, name=name, grid_spec=grid_spec,
        out_shape=jax.ShapeDtypeStruct((s, n), F32),
        input_output_aliases={3: 0} if given else {},
        compiler_params=_params(("arbitrary",)),
    )(shard_arr, h1, wi, *given)


def _sibling_handshake():
    x, y, c = _place()
    barrier = pltpu.get_barrier_semaphore()
    pl.semaphore_signal(barrier, inc=1, device_id=(x, y, 1 - c), device_id_type=MESH)
    pl.semaphore_wait(barrier, 1)


def _chips_handshake():
    x, y, c = _place()
    barrier = pltpu.get_barrier_semaphore()
    for cx, cy in _other_chips(x, y):
        pl.semaphore_signal(barrier, inc=1, device_id=(cx, cy, c), device_id_type=MESH)
    pl.semaphore_wait(barrier, N_CHIPS - 1)


def _split_start(name, bufs, n_sems, copies, sibling_id=None, chips_id=None):
    n = len(bufs)
    assert sibling_id is None or chips_id is None

    def body(*refs):
        if sibling_id is not None:
            _sibling_handshake()
        if chips_id is not None:
            _chips_handshake()
        send_sems, recv_sems = refs[n:n + 2]
        for cp in copies(refs[n + 2:], send_sems, recv_sems):
            cp.start()

    outs = pl.pallas_call(
        body, name=name,
        in_specs=[_HBM] * n, out_specs=[_SEM, _SEM] + [_HBM] * n,
        out_shape=[pltpu.SemaphoreType.DMA((n_sems,)), pltpu.SemaphoreType.DMA((n_sems,))]
        + [pltpu.HBM(b.shape, b.dtype) for b in bufs],
        input_output_aliases={i: 2 + i for i in range(n)},
        compiler_params=pltpu.CompilerParams(has_side_effects=_EFFECT,
                                             collective_id=sibling_id if chips_id is None else chips_id),
    )(*[_in_hbm(b) for b in bufs])
    return outs[0], outs[1], list(outs[2:])


def _split_wait(name, bufs, send_sems, recv_sems, copies, after):
    n = len(bufs)
    after = tuple(after) if isinstance(after, (list, tuple)) else (after,)

    def body(*refs):
        send_ref, recv_ref = refs[n:n + 2]
        for cp in copies(refs[n + 2 + len(after):], send_ref, recv_ref):
            cp.wait()

    return list(pl.pallas_call(
        body, name=name,
        in_specs=[_HBM] * n + [_SEM, _SEM] + [ANY] * len(after), out_specs=[_HBM] * n,
        out_shape=[pltpu.HBM(b.shape, b.dtype) for b in bufs],
        input_output_aliases={i: i for i in range(n)},
        compiler_params=pltpu.CompilerParams(has_side_effects=_EFFECT),
    )(*bufs, send_sems, recv_sems, *after))


def _halves_copies(n_w):
    def copies(bufs, send_sems, recv_sems):
        x, y, c = _place()
        out = []
        for w in range(n_w):
            view, land = bufs[w], bufs[n_w + w]
            src = view.at[1 - c] if len(view.shape) == 3 else view.at[:, 1 - c]
            out.append(pltpu.make_async_remote_copy(
                src_ref=src, dst_ref=land, send_sem=send_sems.at[w], recv_sem=recv_sems.at[w],
                device_id=(x, y, 1 - c), device_id_type=MESH))
        return out
    return copies


def _pieces_copies(n_w):
    def copies(bufs, send_sems, recv_sems):
        x, y, c = _place()
        out = []
        for w in range(n_w):
            for j, (cx, cy) in enumerate(_other_chips(x, y)):
                out.append(pltpu.make_async_remote_copy(
                    src_ref=bufs[w].at[2 * cx + cy], dst_ref=bufs[n_w + w].at[j],
                    send_sem=send_sems.at[3 * w + j], recv_sem=recv_sems.at[3 * w + j],
                    device_id=(cx, cy, c), device_id_type=MESH))
        return out
    return copies


def _join_copies(n_w):
    def copies(bufs, send_sems, recv_sems):
        x, y, c = _place()
        return [pltpu.make_async_remote_copy(
            src_ref=bufs[w].at[c], dst_ref=bufs[w].at[c], send_sem=send_sems.at[w], recv_sem=recv_sems.at[w],
            device_id=(x, y, 1 - c), device_id_type=MESH) for w in range(n_w)]
    return copies


def _halves_landing(view):
    shape = view.shape[1:] if view.ndim == 3 else (N_CHIPS,) + view.shape[2:]
    return lax.empty(shape, BF16)


_SIBLING_IDS = {"halves_down": 1, "halves_gate_up": 2, "halves_out": 3, "halves_in": 4,
                "join_down": 5, "join_gate_up": 6, "join_out": 7, "join_in": 8,
                "diagonal_in": 9, "diagonal_gate": 10, "diagonal_up": 11, "forward_out": 12, "forward_down": 13,
                "pieces_down": 14, "pieces_gate_up": 15, "pieces_out": 16, "pieces_in": 17}


def _halves_start(tag, grads, column_sharded):
    views = [_weight_view(g, cs) for g, cs in zip(grads, column_sharded)]
    n = len(views)
    return _split_start("halves_start_" + tag, views + [_halves_landing(v) for v in views], n, _halves_copies(n),
                        sibling_id=_SIBLING_IDS["halves_" + tag])


def _halves_wait(tag, state, after):
    send_sems, recv_sems, bufs = state
    n = len(bufs) // 2
    bufs = _split_wait("halves_wait_" + tag, bufs, send_sems, recv_sems, _halves_copies(n), after)
    return bufs[:n], bufs[n:]


def _pieces_start(tag, pieces):
    n = len(pieces)
    landing = [lax.empty((3,) + p.shape[1:], BF16) for p in pieces]
    return _split_start("pieces_start_" + tag, list(pieces) + landing, 3 * n, _pieces_copies(n),
                        chips_id=_SIBLING_IDS["pieces_" + tag])


def _pieces_wait(tag, state, after):
    send_sems, recv_sems, bufs = state
    n = len(bufs) // 2
    bufs = _split_wait("pieces_wait_" + tag, bufs, send_sems, recv_sems, _pieces_copies(n), after)
    return bufs[:n], bufs[n:]


def _join_start(tag, shards):
    n = len(shards)
    return _split_start("join_start_" + tag, list(shards), n, _join_copies(n), sibling_id=_SIBLING_IDS["join_" + tag])


def _join_wait(tag, state, after):
    send_sems, recv_sems, bufs = state
    bufs = _split_wait("join_wait_" + tag, bufs, send_sems, recv_sems, _join_copies(len(bufs)), after)
    return [b.reshape(2 * b.shape[1], b.shape[2]) for b in bufs]


def _chip_sum_col(g3, sib, c_arr, name):
    _, hk, n = g3.shape
    cols = n // N_CHIPS
    tr = _row_tile(hk, cols * 2, limit=4 * 1024 * 1024)

    def body(c_ref, g_ref, s_ref, o_ref):
        del c_ref
        o_ref[...] = (g_ref[...].astype(F32) + s_ref[...].astype(F32)).astype(BF16)

    grid_spec = pltpu.PrefetchScalarGridSpec(
        num_scalar_prefetch=1, grid=(N_CHIPS, hk // tr),
        in_specs=[pl.BlockSpec((None, tr, cols), lambda p, r, c_ref: (c_ref[0], r, p)),
                  pl.BlockSpec((tr, cols), lambda p, r, c_ref: (r, p))],
        out_specs=pl.BlockSpec((None, tr, cols), lambda p, r, c_ref: (p, r, 0)))
    return pl.pallas_call(
        body, name=name, grid_spec=grid_spec,
        out_shape=jax.ShapeDtypeStruct((N_CHIPS, hk, cols), BF16),
        compiler_params=_params(("parallel", "parallel")),
    )(c_arr, g3, sib)


def _chip_sum_row(g4, sib, c_arr, name):
    _, _, hr, n = g4.shape
    tr = _row_tile(hr, n * 2, limit=4 * 1024 * 1024)

    def body(c_ref, g_ref, s_ref, o_ref):
        del c_ref
        o_ref[...] = (g_ref[...].astype(F32) + s_ref[...].astype(F32)).astype(BF16)

    grid_spec = pltpu.PrefetchScalarGridSpec(
        num_scalar_prefetch=1, grid=(N_CHIPS, hr // tr),
        in_specs=[pl.BlockSpec((None, None, tr, n), lambda p, r, c_ref: (p, c_ref[0], r, 0)),
                  pl.BlockSpec((None, tr, n), lambda p, r, c_ref: (p, r, 0))],
        out_specs=pl.BlockSpec((None, tr, n), lambda p, r, c_ref: (p, r, 0)))
    return pl.pallas_call(
        body, name=name, grid_spec=grid_spec,
        out_shape=jax.ShapeDtypeStruct((N_CHIPS, hr, n), BF16),
        compiler_params=_params(("parallel", "parallel")),
    )(c_arr, g4, sib)


def _sum_pieces(pieces, received, place_arr, name):
    _, r, n = pieces.shape
    tr = _row_tile(r, n * 4, limit=4 * 1024 * 1024)

    def body(p_ref, own_ref, r0_ref, r1_ref, r2_ref, o_ref):
        del p_ref
        acc = own_ref[...].astype(F32) + r0_ref[...].astype(F32)
        acc = acc + r1_ref[...].astype(F32)
        o_ref[...] = acc + r2_ref[...].astype(F32)

    def recv_spec(j):
        return pl.BlockSpec((None, tr, n), lambda i, p_ref: (j, i, 0))

    grid_spec = pltpu.PrefetchScalarGridSpec(
        num_scalar_prefetch=1, grid=(r // tr,),
        in_specs=[pl.BlockSpec((None, tr, n), lambda i, p_ref: (p_ref[0], i, 0)),
                  recv_spec(0), recv_spec(1), recv_spec(2)],
        out_specs=pl.BlockSpec((None, tr, n), lambda i, p_ref: (p_ref[1], i, 0)))
    return pl.pallas_call(
        body, name=name, grid_spec=grid_spec,
        out_shape=jax.ShapeDtypeStruct((2, r, n), F32),
        compiler_params=_params(("parallel",)),
    )(place_arr, pieces, received, received, received)


def _norm_weights_step(parts, w, m, v, after=()):
    rows, d = parts.shape
    after = tuple(after)

    def body(p_ref, w_ref, m_ref, v_ref, *rest):
        g_ref, d_ref, mo_ref, vo_ref, gathered, send_sems, recv_sems = rest[len(after):]
        x, y, c = _place()
        me = 4 * x + 2 * y + c
        gathered[me] = p_ref[...]
        copies = []
        for k in range(1, N_DEV):
            peer = (x ^ ((k >> 2) & 1), y ^ ((k >> 1) & 1), c ^ (k & 1))
            copies.append(pltpu.make_async_remote_copy(
                src_ref=p_ref, dst_ref=gathered.at[me], send_sem=send_sems.at[k - 1],
                recv_sem=recv_sems.at[k - 1], device_id=peer, device_id_type=MESH))
        for cp in copies:
            cp.start()
        for cp in copies:
            cp.wait()
        g = gathered[0]
        for k in range(1, N_DEV):
            g = g + gathered[k]
        delta, m_new, v_new = _adamw_math(w_ref[...], g, m_ref[...], v_ref[...])
        g_ref[...] = g
        d_ref[...] = delta
        mo_ref[...] = m_new
        vo_ref[...] = v_new

    vmem = pl.BlockSpec(memory_space=pltpu.VMEM)
    shp = jax.ShapeDtypeStruct((rows, d), F32)
    return pl.pallas_call(
        body, name="norm_weights_step",
        in_specs=[vmem] * 4 + [ANY] * len(after), out_specs=[vmem] * 4, out_shape=[shp] * 4,
        scratch_shapes=[pltpu.VMEM((N_DEV, rows, d), F32), pltpu.SemaphoreType.DMA((N_DEV - 1,)),
                        pltpu.SemaphoreType.DMA((N_DEV - 1,))],
        compiler_params=pltpu.CompilerParams(has_side_effects=True),
    )(parts, w, m, v, *after)


def kernel(x, norm_mix_w, w_in, w_out, norm_ffn_w, w_gate, w_up, w_down, norm_final_w, loss_target, m_norm_mix_w, m_w_in, m_w_out, m_norm_ffn_w, m_w_gate, m_w_up, m_w_down, m_norm_final_w, v_norm_mix_w, v_w_in, v_w_out, v_norm_ffn_w, v_w_gate, v_w_up, v_w_down, v_norm_final_w):
    s, d = x.shape[1], x.shape[2]
    xs = x.reshape(s, d)
    target = loss_target.reshape(s, d)
    big = {"w_in": (w_in, m_w_in, v_w_in), "w_out": (w_out, m_w_out, v_w_out),
           "w_gate": (w_gate, m_w_gate, v_w_gate), "w_up": (w_up, m_w_up, v_w_up),
           "w_down": (w_down, m_w_down, v_w_down)}
    big = {k: tuple(a.reshape(a.shape[1:]) for a in t) for k, t in big.items()}
    col_names, row_names = ("w_in", "w_gate", "w_up"), ("w_out", "w_down")
    n_in = N_CHIPS * big["w_in"][0].shape[1]
    ffn = N_CHIPS * big["w_gate"][0].shape[1]
    mix = ATTN_WIDTH + RET_WIDTH
    c_arr = lax.axis_index("c").astype(I32).reshape(1)
    shard_arr = (2 * lax.axis_index("x") + lax.axis_index("y")).astype(I32).reshape(1)
    place_arr = jnp.concatenate([shard_arr, c_arr])

    def cast(k, after=()):
        return _weight_view(_cast_into_full(big[k][0], shard_arr, k in col_names, "cast_" + k, after), k in col_names)

    started_in, v_in = _gather_in_start(cast("w_in"), "gather_in_start")

    sec = ATTN_WIDTH

    def section(p, rows):
        return pl.BlockSpec((None, rows, sec), lambda i, j, kk: (p, i, 0))

    h1 = _rms_fwd(xs, norm_mix_w, "rms_mix_fwd", after=[v_in])
    my_shard = shard_arr[0]
    shard_of = [jnp.bitwise_xor(my_shard, f).astype(I32).reshape(1) for f in (0,) + _FLIPS]
    proj = _in_proj_shard(h1, _weight_unview(v_in), None, shard_of[0], "in_proj_own")
    early_views = [cast(k, after=[proj]) for k in ("w_out", "w_gate")]
    v_up, v_down = [cast(k, after=[proj]) for k in ("w_up", "w_down")]
    relayed_in, (v_in,) = _gather_relay(v_in, started_in, 0, early_views + [v_up, v_down], "gather_in_relay")
    started_og, (v_out, v_gate) = _gather_out_gate_start(*early_views, [v_in], "gather_out_gate_start")
    v_in = _gather_in_neighbours_end(v_in, relayed_in, [v_out], "gather_in_neighbours_end")
    proj = _in_proj_shard(h1, _weight_unview(v_in), proj, shard_of[1], "in_proj_x")
    proj = _in_proj_shard(h1, _weight_unview(v_in), proj, shard_of[2], "in_proj_y")
    forwarded_in, v_in = _gather_in_diagonal(v_in, relayed_in, [proj], "gather_in_diagonal",
                                             _SIBLING_IDS["diagonal_in"])
    wi = _weight_unview(_gather_in_diagonal_end(v_in, forwarded_in, [proj], "gather_in_diagonal_end"))
    proj = _in_proj_shard(h1, wi, proj, shard_of[3], "in_proj_diagonal")
    fs_o, fr_o, v_out = _gather_forward([v_out], [0], *started_og, proj, "gather_forward_out",
                                        sibling_id=_SIBLING_IDS["forward_out"])
    mixed, attn_o, lse = _attn_fwd(proj, after=[v_out])
    relayed_g, (v_gate, v_up) = _gather_relay(v_gate, started_og, 3, [attn_o], "gather_gate_relay",
                                              then=v_up, then_peers=2)
    mixed, ret_raw = _ret_fwd(proj, mixed, after=[v_gate])
    wo, = _gather_end([v_out], fs_o, fr_o, ret_raw, "gather_end_out")
    x1, = _matmul("out_proj", "nn", [mixed, mixed], [wo, wo], [0, 0], s, d, sec, s // 2, 512, sec, [xs], [F32],
                  _epi_residual, b_koff=[0, 1], a_specs=[section(0, s // 2), section(1, s // 2)])
    h2 = _rms_fwd(x1, norm_ffn_w, "rms_ffn_fwd")
    relayed_u, (v_up, v_down) = _gather_relay(v_up, relayed_g, 6, [h2], "gather_up_relay",
                                              then=v_down, then_peers=3)
    v_gate = _gather_in_neighbours_end(v_gate, relayed_g, [v_up], "gather_gate_neighbours_end")
    forwarded_g, v_gate = _gather_in_diagonal(v_gate, relayed_g, [v_up], "gather_gate_diagonal",
                                              _SIBLING_IDS["diagonal_gate"])
    v_up = _gather_in_neighbours_end(v_up, relayed_u, [v_gate], "gather_up_neighbours_end")
    wg = _weight_unview(_gather_in_diagonal_end(v_gate, forwarded_g, [v_up], "gather_gate_diagonal_end"))
    forwarded_u, v_up = _gather_in_diagonal(v_up, relayed_u, [wg], "gather_up_diagonal",
                                            _SIBLING_IDS["diagonal_up"])
    wu = _weight_unview(_gather_in_diagonal_end(v_up, forwarded_u, [wg], "gather_up_diagonal_end"))
    gate, up, act = _matmul("gate_up", "nn", [h2, h2], [wg, wu], [0, 1], s, ffn, d, s, 512, d, [],
                            [BF16, BF16, BF16], _epi_swiglu, a_single_buffer=True)
    fs, fr, v_down = _gather_forward([v_down], [0], *relayed_u, act, "gather_forward_down", base=6,
                                     sibling_id=_SIBLING_IDS["forward_down"])
    wd, = _gather_end([v_down], fs, fr, act, "gather_end_down")
    x2, = _matmul("down_proj", "nn", [act], [wd], [0], s, d, ffn, s // 2, 512, ffn, [x1], [F32],
                  _epi_residual)
    loss_row, dx2, dx2b, dwf = _final_norm_loss(x2, norm_final_w.reshape(1, d), target, "final_norm_loss")

    names = col_names + row_names
    grads, new = {}, {}

    def chip_sums(tag_names, views, sibs):
        return [(_chip_sum_col if k in col_names else _chip_sum_row)(v, sb, c_arr, "chip_sum_" + k)
                for k, v, sb in zip(tag_names, views, sibs)]

    def piece_sums(tag_names, pieces, received):
        return [_sum_pieces(p, r, place_arr, "sum_pieces_" + k) for k, p, r in zip(tag_names, pieces, received)]

    def update(k):
        new[k] = _adamw(big[k][0], grads[k], big[k][1], big[k][2], "adamw_" + k)

    dgate, dup = _matmul("d_act", "nt", [dx2b], [wd], [0], s, ffn, d, s, 512, d, [gate, up],
                         [BF16, BF16], _epi_swiglu_bwd, a_single_buffer=True)
    g_wd, = _matmul("g_w_down", "tn", [act], [dx2b], [0], ffn, d, s, 512, d, s, [], [BF16], _epi_plain)
    halves_d = _halves_start("down", [g_wd], [False])
    dh2, = _matmul("d_h2", "nt", [dgate, dup], [wg, wu], [0, 0], s, d, ffn, s // 2, 256, ffn, [], [F32],
                   _epi_plain, after=halves_d[2][-1:], a_single_buffer=True)
    pieces_d = _pieces_start("down", chip_sums(["w_down"], *_halves_wait("down", halves_d, dh2)))
    g_wg, g_wu = _matmul("g_w_gate_up", "tn", [h2, h2], [dgate, dup], [0, 1], d, ffn, s, 1024, 512, s, [],
                         [BF16, BF16], _epi_two, after=pieces_d[2][-1:])
    halves_gu = _halves_start("gate_up", [g_wg, g_wu], [True, True])
    dx1, dx1b, dw_ffn = _rms_bwd(x1, norm_ffn_w, dh2, dx2, "rms_ffn_bwd", after=halves_gu[2][-1:])

    dmixed, = _matmul("d_mixed", "nt", [dx1b], [wo], [0], s, mix, d, s // 2, 512, d, [], [F32], _epi_plain)
    pieces_gu = _pieces_start("gate_up", chip_sums(["w_gate", "w_up"], *_halves_wait("gate_up", halves_gu, dmixed)))
    per = sec // 512
    g_wo, = _matmul("g_w_out", "tn", [mixed], [dx1b], [0], mix, d, s, 512, d, s, [], [BF16], _epi_plain,
                    after=pieces_gu[2][-1:],
                    a_specs=[pl.BlockSpec((None, s, 512), lambda i, j, kk: (i // per, 0, i % per))])
    halves_o = _halves_start("out", [g_wo], [False])
    dsec = _attn_bwd(proj, attn_o, lse, dmixed, after=halves_o[2][-1:])
    pieces_o = _pieces_start("out", chip_sums(["w_out"], *_halves_wait("out", halves_o, dsec)))
    dsec = _ret_bwd(proj, ret_raw, dmixed, dsec, after=pieces_o[2][-1:])
    where = [0, 1, 2, 4, 5, 6, 7]
    n_sec = len(where)
    g_wi, = _matmul("g_w_in", "tn", [h1], [dsec], [0], d, n_in, s, 1024, sec, s, [], [BF16], _epi_plain,
                    b_specs=[pl.BlockSpec((None, s, sec), lambda i, j, kk: (j + (j >= 3).astype(I32), 0, 0))])
    halves_i = _halves_start("in", [g_wi], [True])
    dh1, = _matmul("d_h1", "nt", [dsec] * n_sec, [wi] * n_sec, [0] * n_sec, s, d, sec, s // 2, 256, sec, [], [F32],
                   _epi_plain, b_koff=list(range(n_sec)), after=halves_i[2][-1:],
                   a_specs=[section(p, s // 2) for p in where])
    pieces_i = _pieces_start("in", chip_sums(["w_in"], *_halves_wait("in", halves_i, dh1)))
    grad_x, _, dw_mix = _rms_bwd(xs, norm_mix_w, dh1, dx1, "rms_mix_bwd", after=pieces_i[2][-1:])

    def rows8(*vs):
        return jnp.concatenate([v.reshape(1, d) for v in vs] + [jnp.zeros((8 - len(vs), d), F32)], axis=0)

    join_d = _join_start("down", piece_sums(["w_down"], *_pieces_wait("down", pieces_d, grad_x)))
    join_gu = _join_start("gate_up", piece_sums(["w_gate", "w_up"], *_pieces_wait("gate_up", pieces_gu, join_d[2][0])))
    join_o = _join_start("out", piece_sums(["w_out"], *_pieces_wait("out", pieces_o, join_gu[2][0])))
    grads["w_down"], = _join_wait("down", join_d, join_o[2][0])
    update("w_down")
    grads["w_gate"], grads["w_up"] = _join_wait("gate_up", join_gu, new["w_down"][0])
    update("w_gate")
    update("w_up")
    grads["w_out"], = _join_wait("out", join_o, new["w_up"][0])
    update("w_out")
    others_done = [new[k][0] for k in ("w_down", "w_gate", "w_up", "w_out")]
    join_i = _join_start("in", piece_sums(["w_in"], *_pieces_wait("in", pieces_i, others_done)))
    ng, nd, nm, nv = _norm_weights_step(
        rows8(dw_mix, dw_ffn, dwf, jnp.broadcast_to(loss_row[:, :1], (1, d))),
        rows8(norm_mix_w, norm_ffn_w, norm_final_w),
        rows8(m_norm_mix_w, m_norm_ffn_w, m_norm_final_w), rows8(v_norm_mix_w, v_norm_ffn_w, v_norm_final_w),
        after=join_i[2][:1])
    grads["w_in"], = _join_wait("in", join_i, ng)
    update("w_in")

    loss = ng[3, 0]

    def pack(small, per_weight):
        lead = lambda a: a.reshape((1,) + a.shape)
        return (small[0:1], lead(per_weight["w_in"]), lead(per_weight["w_out"]), small[1:2],
                lead(per_weight["w_gate"]), lead(per_weight["w_up"]), lead(per_weight["w_down"]), small[2])

    return (loss, grad_x.reshape(1, s, d),
            *pack(ng, {k: new[k][3] for k in names}),
            *pack(nd, {k: new[k][0] for k in names}),
            *pack(nm, {k: new[k][1] for k in names}),
            *pack(nv, {k: new[k][2] for k in names}))
```

```python
import functools
import math

import jax
import jax.numpy as jnp
from jax import lax
from jax.experimental import pallas as pl
from jax.experimental.pallas import tpu as pltpu

F32 = jnp.float32
BF16 = jnp.bfloat16
I32 = jnp.int32
MESH = pl.DeviceIdType.MESH
ANY = pl.BlockSpec(memory_space=pl.ANY)

ATTN_HEADS = 8
ATTN_HEAD_DIM = 128
RET_HEADS = 4
RET_HEAD_DIM = 256
ATTN_WIDTH = ATTN_HEADS * ATTN_HEAD_DIM
RET_WIDTH = RET_HEADS * RET_HEAD_DIM
DILATED_PATTERNS = ((128, 1), (512, 4), (2048, 16))
NORM_EPS = 1e-6
ADAM_LR = 0.001
ADAM_B1 = 0.9
ADAM_B2 = 0.999
ADAM_EPS = 1e-08
ADAM_WD = 0.01
ADAM_STEP = 10

N_CHIPS = 4
N_DEV = 8
NEG_BIG = -1e30
SEQ_TILE = 512
ATTN_FWD_HEADS_PER_STEP = 2
ATTN_HEADS_PER_STEP = 1
VMEM_LIMIT_BYTES = 56 * 1024 * 1024


def _params(semantics=None, vmem=VMEM_LIMIT_BYTES):
    return pltpu.CompilerParams(dimension_semantics=semantics, vmem_limit_bytes=vmem)


def _row_tile(rows, row_bytes, limit=2 * 1024 * 1024, mult=16):
    best = None
    for t in range(mult, rows + 1, mult):
        if rows % t == 0 and t * row_bytes <= limit:
            best = t
    assert best is not None, (rows, row_bytes)
    return best


def _sigmoid(x):
    return 1.0 / (1.0 + jnp.exp(-x))


def _select_by_index(idx, values):
    out = jnp.float32(values[-1])
    for i in range(len(values) - 2, -1, -1):
        out = jnp.where(idx == i, jnp.float32(values[i]), out)
    return out


def _place():
    x, y, c = lax.axis_index("x"), lax.axis_index("y"), lax.axis_index("c")
    return x, y, c


def _cast_into_full(w, shard_arr, column_sharded, name, after=()):
    after = tuple(after)
    rows, cols = w.shape
    tr = _row_tile(rows, cols * 4)
    steps = rows // tr
    if column_sharded:
        out_shape, out_map = (rows, N_CHIPS * cols), (lambda i, s_ref: (i, s_ref[0]))
    else:
        out_shape, out_map = (N_CHIPS * rows, cols), (lambda i, s_ref: (s_ref[0] * steps + i, 0))

    def body(s_ref, w_ref, *rest):
        del s_ref
        rest[-1][...] = w_ref[...].astype(BF16)

    grid_spec = pltpu.PrefetchScalarGridSpec(
        num_scalar_prefetch=1, grid=(steps,),
        in_specs=[pl.BlockSpec((tr, cols), lambda i, s_ref: (i, 0))] + [ANY] * len(after),
        out_specs=pl.BlockSpec((tr, cols), out_map))
    return pl.pallas_call(
        body, name=name, grid_spec=grid_spec,
        out_shape=jax.ShapeDtypeStruct(out_shape, BF16),
        compiler_params=_params(("parallel",)),
    )(shard_arr, w, *after)


def _rms_fwd(x, w, name, after=()):
    rows, d = x.shape
    tr = 256
    after = tuple(after)

    def body(x_ref, w_ref, *rest):
        xv = x_ref[...]
        r = lax.rsqrt(jnp.mean(xv * xv, axis=-1, keepdims=True) + NORM_EPS)
        rest[-1][...] = (xv * r * w_ref[...]).astype(BF16)

    return pl.pallas_call(
        body, name=name, grid=(rows // tr,),
        in_specs=[pl.BlockSpec((tr, d), lambda i: (i, 0)), pl.BlockSpec((1, d), lambda i: (0, 0))]
        + [ANY] * len(after),
        out_specs=pl.BlockSpec((tr, d), lambda i: (i, 0)),
        out_shape=jax.ShapeDtypeStruct((rows, d), BF16),
        compiler_params=_params(("parallel",)),
    )(x, w, *after)


def _rms_bwd(x, w, dh, dres, name, after=()):
    rows, d = x.shape
    tr = 256
    after = tuple(after)

    def body(x_ref, w_ref, dh_ref, dres_ref, *rest):
        dx_ref, dxb_ref, dw_ref = rest[len(after):]
        xv = x_ref[...]
        r = lax.rsqrt(jnp.mean(xv * xv, axis=-1, keepdims=True) + NORM_EPS)
        xhat = xv * r
        dy = dh_ref[...]
        dxhat = dy * w_ref[...]
        dx = dres_ref[...] + r * (dxhat - xhat * jnp.mean(dxhat * xhat, axis=-1, keepdims=True))
        dx_ref[...] = dx
        dxb_ref[...] = dx.astype(BF16)
        part = jnp.sum(dy * xhat, axis=0, keepdims=True)

        @pl.when(pl.program_id(0) == 0)
        def _():
            dw_ref[...] = part

        @pl.when(pl.program_id(0) != 0)
        def _():
            dw_ref[...] += part

    row = pl.BlockSpec((tr, d), lambda i: (i, 0))
    vec = pl.BlockSpec((1, d), lambda i: (0, 0))
    return pl.pallas_call(
        body, name=name, grid=(rows // tr,),
        in_specs=[row, vec, row, row] + [ANY] * len(after),
        out_specs=[row, row, vec],
        out_shape=[jax.ShapeDtypeStruct((rows, d), F32), jax.ShapeDtypeStruct((rows, d), BF16),
                   jax.ShapeDtypeStruct((1, d), F32)],
        compiler_params=_params(("arbitrary",)),
    )(x, w, dh, dres, *after)


def _final_norm_loss(x2, w, target, name):
    rows, d = x2.shape
    tr = 256

    def body(x_ref, w_ref, t_ref, loss_ref, dx_ref, dxb_ref, dw_ref):
        xv = x_ref[...]
        wv = w_ref[...]
        r = lax.rsqrt(jnp.mean(xv * xv, axis=-1, keepdims=True) + NORM_EPS)
        xhat = xv * r
        err = xhat * wv - t_ref[...]
        part_loss = 0.5 * jnp.sum(jnp.mean(err * err, axis=-1, keepdims=True), axis=0, keepdims=True)
        dy = err * (1.0 / d)
        dxhat = dy * wv
        dx = r * (dxhat - xhat * jnp.mean(dxhat * xhat, axis=-1, keepdims=True))
        dx_ref[...] = dx
        dxb_ref[...] = dx.astype(BF16)
        part_dw = jnp.sum(dy * xhat, axis=0, keepdims=True)
        part_loss = jnp.broadcast_to(part_loss, (1, 128))

        @pl.when(pl.program_id(0) == 0)
        def _():
            dw_ref[...] = part_dw
            loss_ref[...] = part_loss

        @pl.when(pl.program_id(0) != 0)
        def _():
            dw_ref[...] += part_dw
            loss_ref[...] += part_loss

    row = pl.BlockSpec((tr, d), lambda i: (i, 0))
    vec = pl.BlockSpec((1, d), lambda i: (0, 0))
    return pl.pallas_call(
        body, name=name, grid=(rows // tr,),
        in_specs=[row, vec, row],
        out_specs=[pl.BlockSpec((1, 128), lambda i: (0, 0)), row, row, vec],
        out_shape=[jax.ShapeDtypeStruct((1, 128), F32), jax.ShapeDtypeStruct((rows, d), F32),
                   jax.ShapeDtypeStruct((rows, d), BF16), jax.ShapeDtypeStruct((1, d), F32)],
        compiler_params=_params(("arbitrary",)),
    )(x2, w, target)


def _adamw_math(w, g, m, v):
    m = ADAM_B1 * m + (1.0 - ADAM_B1) * g
    v = ADAM_B2 * v + (1.0 - ADAM_B2) * (g * g)
    m_hat = m / (1.0 - ADAM_B1 ** ADAM_STEP)
    v_hat = v / (1.0 - ADAM_B2 ** ADAM_STEP)
    delta = -ADAM_LR * (m_hat / (jnp.sqrt(v_hat) + ADAM_EPS) + ADAM_WD * w)
    return delta, m, v


def _adamw(w, g, m, v, name):
    rows, cols = w.shape
    tr = _row_tile(rows, cols * 4)

    def body(w_ref, g_ref, m_ref, v_ref, d_ref, mo_ref, vo_ref, go_ref):
        g = g_ref[...]
        delta, m_new, v_new = _adamw_math(w_ref[...], g, m_ref[...], v_ref[...])
        d_ref[...] = delta
        mo_ref[...] = m_new
        vo_ref[...] = v_new
        go_ref[...] = g

    blk = pl.BlockSpec((tr, cols), lambda i: (i, 0))
    shp = jax.ShapeDtypeStruct((rows, cols), F32)
    return pl.pallas_call(
        body, name=name, grid=(rows // tr,),
        in_specs=[blk] * 4, out_specs=[blk] * 4, out_shape=[shp] * 4,
        compiler_params=_params(("parallel",)),
    )(w, g, m, v)


_DOT_DIMS = {"nn": ((1,), (0,)), "nt": ((1,), (1,)), "tn": ((0,), (0,))}


def _matmul(name, mode, a_list, b_list, acc_of, m, n, k, tm, tn, tk, extras, out_dtypes, epilogue,
            a_koff=None, b_koff=None, after=(), a_specs=None, b_specs=None, a_single_buffer=False):
    after = tuple(after)
    assert m % tm == 0 and n % tn == 0 and k % tk == 0, (name, m, n, k, tm, tn, tk)
    nk = k // tk
    n_acc = max(acc_of) + 1
    n_pairs = len(a_list)
    a_koff = a_koff or [0] * n_pairs
    b_koff = b_koff or [0] * n_pairs
    dims = (_DOT_DIMS[mode], ((), ()))
    n_ext, n_out = len(extras), len(out_dtypes)

    def body(*refs):
        a_refs = refs[:n_pairs]
        b_refs = refs[n_pairs:2 * n_pairs]
        e_refs = refs[2 * n_pairs:2 * n_pairs + n_ext]
        first_out = 2 * n_pairs + n_ext + len(after)
        o_refs = refs[first_out:first_out + n_out]
        acc_refs = refs[first_out + n_out:]

        parts = [None] * n_acc
        for p in range(n_pairs):
            d = lax.dot_general(a_refs[p][...], b_refs[p][...], dims, preferred_element_type=F32)
            parts[acc_of[p]] = d if parts[acc_of[p]] is None else parts[acc_of[p]] + d

        def finish(accs):
            outs = epilogue(accs, [e[...] for e in e_refs])
            for o_ref, o in zip(o_refs, outs):
                o_ref[...] = o.astype(o_ref.dtype)

        if nk == 1:
            finish(parts)
        else:
            kk = pl.program_id(2)

            @pl.when(kk == 0)
            def _():
                for acc_ref, part in zip(acc_refs, parts):
                    acc_ref[...] = part

            @pl.when(kk != 0)
            def _():
                for acc_ref, part in zip(acc_refs, parts):
                    acc_ref[...] += part

            @pl.when(kk == nk - 1)
            def _():
                finish([acc_ref[...] for acc_ref in acc_refs])

    def a_spec(off):
        mode_a = pl.Buffered(1) if a_single_buffer else None
        if mode == "tn":
            return pl.BlockSpec((tk, tm), lambda i, j, kk: (kk + off, i), pipeline_mode=mode_a)
        return pl.BlockSpec((tm, tk), lambda i, j, kk: (i, kk + off), pipeline_mode=mode_a)

    def b_spec(off):
        if mode == "nt":
            return pl.BlockSpec((tn, tk), lambda i, j, kk: (j, kk + off))
        return pl.BlockSpec((tk, tn), lambda i, j, kk: (kk + off, j))

    tile = pl.BlockSpec((tm, tn), lambda i, j, kk: (i, j))
    scratch = [pltpu.VMEM((tm, tn), F32) for _ in range(n_acc)] if nk > 1 else []
    return pl.pallas_call(
        body, name=name, grid=(m // tm, n // tn, nk),
        in_specs=(a_specs or [a_spec(o) for o in a_koff]) + (b_specs or [b_spec(o) for o in b_koff])
        + [tile] * n_ext + [ANY] * len(after),
        out_specs=[tile] * n_out,
        out_shape=[jax.ShapeDtypeStruct((m, n), dt) for dt in out_dtypes],
        scratch_shapes=scratch,
        compiler_params=_params(("parallel", "parallel", "arbitrary")),
    )(*a_list, *b_list, *extras, *after)


def _epi_plain(accs, extras):
    return (accs[0],)


def _epi_residual(accs, extras):
    return (accs[0] + extras[0],)


def _epi_two(accs, extras):
    return accs[0], accs[1]


def _epi_swiglu(accs, extras):
    g, u = accs
    return g, u, g * _sigmoid(g) * u


def _epi_swiglu_bwd(accs, extras):
    da = accs[0]
    g, u = (e.astype(F32) for e in extras)
    sg = _sigmoid(g)
    dg = da * u * sg * (1.0 + g * (1.0 - sg))
    du = da * g * sg
    return dg, du


_NT_DIMS = (((1,), (1,)), ((), ()))
_TN_DIMS = (((0,), (0,)), ((), ()))


def _tile_delta(tq, tk):
    return lax.broadcasted_iota(I32, (tq, tk), 0) - lax.broadcasted_iota(I32, (tq, tk), 1)


def _attn_log_count(delta):
    count = jnp.zeros(delta.shape, I32)
    for window, dilation in DILATED_PATTERNS:
        hit = ((delta & (dilation - 1)) == 0) & (delta <= window)
        count = count + jnp.where(hit, 1, 0)
    valid = (delta >= 0) & (count > 0)
    logm = jnp.where(count == 3, math.log(3.0), jnp.where(count == 2, math.log(2.0), 0.0))
    return jnp.where(valid, logm, NEG_BIG)


def _fill_attn_log_count(tab_ref):
    nb, t, _ = tab_ref.shape
    base = _tile_delta(t, t)
    for b in range(nb):
        tab_ref[b] = _attn_log_count(base + b * t)


def _fill_attn_bias(tab_ref, log_count_ref, slope):
    nb, t, _ = tab_ref.shape
    dist = _tile_delta(t, t).astype(F32)
    for b in range(nb):
        tab_ref[b] = log_count_ref[b] - slope * (dist + float(b * t))


def _fill_ret_decay(tab_ref, log_gamma):
    nb, t, _ = tab_ref.shape
    base = _tile_delta(t, t)
    for b in range(nb):
        tab_ref[b] = _ret_decay(base + b * t, log_gamma)


def _alibi_slopes():
    return [2.0 ** (-8.0 * (h + 1) / ATTN_HEADS) for h in range(ATTN_HEADS)]


def _attn_fwd(proj, after=()):
    s = proj.shape[0]
    t = SEQ_TILE
    hd = ATTN_HEAD_DIM
    hp = ATTN_FWD_HEADS_PER_STEP
    ng = ATTN_HEADS // hp
    w = hp * hd
    scale = 1.0 / math.sqrt(hd)
    slopes = _alibi_slopes()

    def body(q_ref, k_ref, v_ref, *rest):
        mix_ref, o_ref, lse_ref, kb, vb, bias_tab, log_count_tab = rest[len(after):]
        g = pl.program_id(0)
        i = pl.program_id(1)

        @pl.when((g == 0) & (i == 0))
        def _():
            _fill_attn_log_count(log_count_tab)

        @pl.when(i == 0)
        def _():
            kb[...] = k_ref[...].astype(BF16)
            vb[...] = v_ref[...].astype(BF16)
            for u in range(hp):
                _fill_attn_bias(bias_tab.at[u], log_count_tab, _select_by_index(g * hp + u, slopes))

        qs = [q_ref[:, u * hd:(u + 1) * hd].astype(BF16) for u in range(hp)]

        def step(j, carry):
            rows = pl.ds(pl.multiple_of(j * t, t), t)
            out = []
            for u in range(hp):
                m_i, l_i, acc = carry[u]
                lanes = slice(u * hd, (u + 1) * hd)
                sc = lax.dot_general(qs[u], kb[rows, lanes], _NT_DIMS, preferred_element_type=F32) * scale
                sc = sc + bias_tab[u, i - j]
                m_new = jnp.maximum(m_i, jnp.max(sc, axis=-1, keepdims=True))
                p = jnp.exp(sc - m_new)
                alpha = jnp.exp(m_i - m_new)
                l_new = alpha * l_i + jnp.sum(p, axis=-1, keepdims=True)
                acc = alpha * acc + jnp.dot(p.astype(BF16), vb[rows, lanes], preferred_element_type=F32)
                out.append((m_new, l_new, acc))
            return tuple(out)

        init = (jnp.full((t, 1), NEG_BIG, F32), jnp.zeros((t, 1), F32), jnp.zeros((t, hd), F32))
        final = lax.fori_loop(0, i + 1, step, (init,) * hp)
        for u in range(hp):
            m_i, l_i, acc = final[u]
            lanes = slice(u * hd, (u + 1) * hd)
            out = acc / l_i
            o_ref[:, lanes] = out
            mix_ref[:, lanes] = out.astype(BF16)
            lse_ref[:, lanes] = jnp.broadcast_to(m_i + jnp.log(l_i), (t, hd))

    return pl.pallas_call(
        body, name="attn_fwd", grid=(ng, s // t),
        in_specs=[pl.BlockSpec((t, w), lambda g, i: (i, g)),
                  pl.BlockSpec((s, w), lambda g, i: (0, ng + g)),
                  pl.BlockSpec((s, w), lambda g, i: (0, 2 * ng + g))] + [ANY] * len(after),
        out_specs=[pl.BlockSpec((None, t, w), lambda g, i: (0, i, g))] + [pl.BlockSpec((t, w), lambda g, i: (i, g))] * 2,
        out_shape=[jax.ShapeDtypeStruct((2, s, ATTN_WIDTH), BF16),
                   jax.ShapeDtypeStruct((s, ATTN_WIDTH), F32),
                   jax.ShapeDtypeStruct((s, ATTN_WIDTH), F32)],
        scratch_shapes=[pltpu.VMEM((s, w), BF16), pltpu.VMEM((s, w), BF16), pltpu.VMEM((hp, s // t, t, t), F32),
                        pltpu.VMEM((s // t, t, t), F32)],
        compiler_params=_params(("arbitrary", "arbitrary")),
    )(proj, proj, proj, *after)


def _attn_bwd(proj, attn_out, lse, dmixed, after=()):
    after = tuple(after)
    s = proj.shape[0]
    t = SEQ_TILE
    nt = s // t
    hd = ATTN_HEAD_DIM
    hp = ATTN_HEADS_PER_STEP
    ng = ATTN_HEADS // hp
    w = hp * hd
    scale = 1.0 / math.sqrt(hd)
    slopes = _alibi_slopes()

    def body(q_ref, k_ref, v_ref, o_ref, lse_ref, do_ref, *rest):
        dsec_ref, qb, kb, vb, dob, dsum, dq_acc, bias_tab, log_count_tab = rest[len(after):]
        g = pl.program_id(0)

        @pl.when(g == 0)
        def _():
            _fill_attn_log_count(log_count_tab)

        qb[...] = q_ref[...].astype(BF16)
        kb[...] = k_ref[...].astype(BF16)
        vb[...] = v_ref[...].astype(BF16)
        dob[...] = do_ref[...].astype(BF16)
        for u in range(hp):
            lanes = slice(u * hd, (u + 1) * hd)
            _fill_attn_bias(bias_tab.at[u], log_count_tab, _select_by_index(g * hp + u, slopes))
            rowsum = jnp.sum(do_ref[:, lanes] * o_ref[:, lanes], axis=-1, keepdims=True)
            dsum[:, lanes] = jnp.broadcast_to(rowsum, (s, hd))
        dq_acc[...] = jnp.zeros((s, w), F32)

        def over_keys(j, _):
            krows = pl.ds(pl.multiple_of(j * t, t), t)

            def over_queries(i, carry):
                qrows = pl.ds(pl.multiple_of(i * t, t), t)
                out = []
                for u in range(hp):
                    dk, dv = carry[u]
                    lanes = slice(u * hd, (u + 1) * hd)
                    qi, doi = qb[qrows, lanes], dob[qrows, lanes]
                    kj, vj = kb[krows, lanes], vb[krows, lanes]
                    lse_i = lse_ref[qrows, lanes][:, :1]
                    dsum_i = dsum[qrows, lanes][:, :1]
                    sc = lax.dot_general(qi, kj, _NT_DIMS, preferred_element_type=F32) * scale
                    p = jnp.exp(sc + bias_tab[u, i - j] - lse_i)
                    dp = lax.dot_general(doi, vj, _NT_DIMS, preferred_element_type=F32)
                    ds = (p * (dp - dsum_i)).astype(BF16)
                    dv = dv + lax.dot_general(p.astype(BF16), doi, _TN_DIMS, preferred_element_type=F32)
                    dk = dk + lax.dot_general(ds, qi, _TN_DIMS, preferred_element_type=F32)
                    dq_acc[qrows, lanes] += jnp.dot(ds, kj, preferred_element_type=F32)
                    out.append((dk, dv))
                return tuple(out)

            zero = jnp.zeros((t, hd), F32)
            final = lax.fori_loop(j, nt, over_queries, ((zero, zero),) * hp)
            for u in range(hp):
                lanes = slice(u * hd, (u + 1) * hd)
                dsec_ref[1, krows, lanes] = (final[u][0] * scale).astype(BF16)
                dsec_ref[2, krows, lanes] = final[u][1].astype(BF16)
            return 0

        lax.fori_loop(0, nt, over_keys, 0)
        dsec_ref[0] = (dq_acc[...] * scale).astype(BF16)

    def col(off):
        return pl.BlockSpec((s, w), lambda g: (0, off + g))

    return pl.pallas_call(
        body, name="attn_bwd", grid=(ng,),
        in_specs=[col(0), col(ng), col(2 * ng), col(0), col(0), col(0)] + [ANY] * len(after),
        out_specs=pl.BlockSpec((4, s, w), lambda g: (0, 0, g)),
        out_shape=jax.ShapeDtypeStruct((8, s, ATTN_WIDTH), BF16),
        scratch_shapes=[pltpu.VMEM((s, w), BF16)] * 4 + [pltpu.VMEM((s, w), F32)] * 2
        + [pltpu.VMEM((hp, nt, t, t), F32), pltpu.VMEM((nt, t, t), F32)],
        compiler_params=_params(("arbitrary",)),
    )(proj, proj, proj, attn_out, lse, dmixed, *after)


def _ret_log_gammas():
    return [math.log(1.0 - 2.0 ** (-5.0 - h)) for h in range(RET_HEADS)]


def _ret_decay(delta, log_gamma):
    dec = jnp.exp(delta.astype(F32) * log_gamma) * (1.0 / math.sqrt(RET_HEAD_DIM))
    return jnp.where(delta >= 0, dec, 0.0)


def _ret_fwd(proj, mixed, after=()):
    after = tuple(after)
    s = proj.shape[0]
    t = SEQ_TILE
    hd = RET_HEAD_DIM
    nh = RET_HEADS
    log_gammas = _ret_log_gammas()
    c0 = 3 * ATTN_WIDTH // hd

    def body(q_ref, k_ref, v_ref, g_ref, *rest):
        mix_ref, raw_ref, kb, vb, decay_tab = rest[1 + len(after):]
        h = pl.program_id(0)
        i = pl.program_id(1)

        @pl.when(i == 0)
        def _():
            kb[...] = k_ref[...].astype(BF16)
            vb[...] = v_ref[...].astype(BF16)
            _fill_ret_decay(decay_tab, _select_by_index(h, log_gammas))

        q = q_ref[...].astype(BF16)

        def step(j, acc):
            rows = pl.ds(pl.multiple_of(j * t, t), t)
            sc = lax.dot_general(q, kb[rows, :], _NT_DIMS, preferred_element_type=F32) * decay_tab[i - j]
            return acc + jnp.dot(sc.astype(BF16), vb[rows, :], preferred_element_type=F32)

        ret = lax.fori_loop(0, i + 1, step, jnp.zeros((t, hd), F32))
        raw_ref[...] = ret
        r = lax.rsqrt(jnp.mean(ret * ret, axis=-1, keepdims=True) + NORM_EPS)
        g = g_ref[...]
        mix_ref[...] = (g * _sigmoid(g) * (ret * r)).astype(BF16)

    return pl.pallas_call(
        body, name="ret_fwd", grid=(nh, s // t),
        in_specs=[pl.BlockSpec((t, hd), lambda h, i: (i, c0 + h)),
                  pl.BlockSpec((s, hd), lambda h, i: (0, c0 + nh + h)),
                  pl.BlockSpec((s, hd), lambda h, i: (0, c0 + 2 * nh + h)),
                  pl.BlockSpec((t, hd), lambda h, i: (i, c0 + 3 * nh + h))] + [ANY] * (1 + len(after)),
        out_specs=[pl.BlockSpec((None, t, hd), lambda h, i: (1, i, h)), pl.BlockSpec((t, hd), lambda h, i: (i, h))],
        out_shape=[jax.ShapeDtypeStruct(mixed.shape, BF16), jax.ShapeDtypeStruct((s, RET_WIDTH), F32)],
        input_output_aliases={4: 0},
        scratch_shapes=[pltpu.VMEM((s, hd), BF16), pltpu.VMEM((s, hd), BF16), pltpu.VMEM((s // t, t, t), F32)],
        compiler_params=_params(("arbitrary", "arbitrary")),
    )(proj, proj, proj, proj, mixed, *after)


def _ret_bwd(proj, ret_raw, dmixed, dsec, after=()):
    after = tuple(after)
    s = proj.shape[0]
    t = SEQ_TILE
    nt = s // t
    hd = RET_HEAD_DIM
    nh = RET_HEADS
    log_gammas = _ret_log_gammas()
    c0 = 3 * ATTN_WIDTH // hd
    mixed_blocks = ATTN_WIDTH // hd

    def body(q_ref, k_ref, v_ref, g_ref, raw_ref, dmix_ref, *rest):
        dsec_ref, qb, kb, vb, dretb, dq_acc, decay_tab = rest[1 + len(after):]
        h = pl.program_id(0)
        _fill_ret_decay(decay_tab, _select_by_index(h, log_gammas))
        qb[...] = q_ref[...].astype(BF16)
        kb[...] = k_ref[...].astype(BF16)
        vb[...] = v_ref[...].astype(BF16)
        ret = raw_ref[...]
        r = lax.rsqrt(jnp.mean(ret * ret, axis=-1, keepdims=True) + NORM_EPS)
        normed = ret * r
        g = g_ref[...]
        sg = _sigmoid(g)
        dout = dmix_ref[...]
        dsec_ref[3] = (dout * normed * sg * (1.0 + g * (1.0 - sg))).astype(BF16)
        dn = dout * g * sg
        dret = r * (dn - normed * jnp.mean(dn * normed, axis=-1, keepdims=True))
        dretb[...] = dret.astype(BF16)
        dq_acc[...] = jnp.zeros((s, hd), F32)

        def over_keys(j, _):
            krows = pl.ds(pl.multiple_of(j * t, t), t)
            kj = kb[krows, :]
            vj = vb[krows, :]

            def over_queries(i, carry):
                dk, dv = carry
                qrows = pl.ds(pl.multiple_of(i * t, t), t)
                qi = qb[qrows, :]
                doi = dretb[qrows, :]
                dec = decay_tab[i - j]
                a = (lax.dot_general(qi, kj, _NT_DIMS, preferred_element_type=F32) * dec).astype(BF16)
                da = (lax.dot_general(doi, vj, _NT_DIMS, preferred_element_type=F32) * dec).astype(BF16)
                dv = dv + lax.dot_general(a, doi, _TN_DIMS, preferred_element_type=F32)
                dk = dk + lax.dot_general(da, qi, _TN_DIMS, preferred_element_type=F32)
                dq_acc[qrows, :] += jnp.dot(da, kj, preferred_element_type=F32)
                return dk, dv

            zero = jnp.zeros((t, hd), F32)
            dk, dv = lax.fori_loop(j, nt, over_queries, (zero, zero))
            dsec_ref[1, krows, :] = dk.astype(BF16)
            dsec_ref[2, krows, :] = dv.astype(BF16)
            return 0

        lax.fori_loop(0, nt, over_keys, 0)
        dsec_ref[0] = dq_acc[...].astype(BF16)

    def col(off):
        return pl.BlockSpec((s, hd), lambda h: (0, off + h))

    return pl.pallas_call(
        body, name="ret_bwd", grid=(nh,),
        in_specs=[col(c0), col(c0 + nh), col(c0 + 2 * nh), col(c0 + 3 * nh), col(0), col(mixed_blocks)]
        + [ANY] * (1 + len(after)),
        out_specs=pl.BlockSpec((4, s, hd), lambda h: (1, 0, h)),
        out_shape=jax.ShapeDtypeStruct(dsec.shape, BF16),
        input_output_aliases={6: 0},
        scratch_shapes=[pltpu.VMEM((s, hd), BF16)] * 4 + [pltpu.VMEM((s, hd), F32)]
        + [pltpu.VMEM((nt, t, t), F32)],
        compiler_params=_params(("arbitrary",)),
    )(proj, proj, proj, proj, ret_raw, dmixed, dsec, *after)


_FLIPS = (2, 1, 3)


def _other_chips(x, y):
    return [(1 - x, y), (x, 1 - y), (1 - x, 1 - y)]


_HBM = pl.BlockSpec(memory_space=pltpu.HBM)
_SEM = pl.BlockSpec(memory_space=pltpu.SEMAPHORE)
_EFFECT = pltpu.SideEffectType.DATAFLOW_SIDE_EFFECTING


def _in_hbm(a):
    return pltpu.with_memory_space_constraint(a, pltpu.HBM)


def _weight_view(w, column_sharded):
    if column_sharded:
        return w.reshape(2, w.shape[0] // 2, w.shape[1])
    return w.reshape(N_CHIPS, 2, w.shape[0] // (2 * N_CHIPS), w.shape[1])


def _weight_unview(v):
    if v.ndim == 3:
        return v.reshape(2 * v.shape[1], v.shape[2])
    return v.reshape(N_CHIPS * 2 * v.shape[2], v.shape[3])


def _weight_region(buf, shard, half):
    if len(buf.shape) == 3:
        cols = buf.shape[2] // N_CHIPS
        return buf.at[half, :, pl.ds(shard * cols, cols)]
    return buf.at[shard, half]


def _remote(where, send_sem, recv_sem, to):
    return pltpu.make_async_remote_copy(src_ref=where, dst_ref=where, send_sem=send_sem, recv_sem=recv_sem,
                                        device_id=to, device_id_type=MESH)


def _for_my_shard(fn):
    x, y, _ = _place()
    for ss in range(N_CHIPS):
        pl.when(2 * x + y == ss)(functools.partial(fn, ss))


def _gather_forward(views, which, send_sems, recv_sems, after, name, base=0, sibling_id=None):
    n_w = len(views)
    which = [base // 3 + w for w in which] if base % 3 == 0 else None
    assert which is not None, "base must be a multiple of 3"

    def body(*refs):
        if sibling_id is not None:
            _sibling_handshake()
        send_in, recv_in = refs[n_w:n_w + 2]
        fwd_send, fwd_recv = refs[n_w + 3:n_w + 5]
        bufs = refs[n_w + 5:]
        x, y, c = _place()
        sibling = (x, y, 1 - c)

        def forward(ss):
            for i, w in enumerate(which):
                for j in range(3):
                    landed = _weight_region(bufs[i], ss ^ _FLIPS[j], c)
                    _remote(landed, send_in.at[3 * w + j], recv_in.at[3 * w + j], sibling).wait_recv()
                    _remote(landed, fwd_send.at[3 * i + j], fwd_recv.at[3 * i + j], sibling).start()

        _for_my_shard(forward)
        for i, w in enumerate(which):
            for j in range(3):
                _remote(_weight_region(bufs[i], 0, 0), send_in.at[3 * w + j], recv_in.at[3 * w + j],
                        sibling).wait_send()

    return pl.pallas_call(
        body, name=name,
        in_specs=[_HBM] * n_w + [_SEM, _SEM, ANY], out_specs=[_SEM, _SEM] + [_HBM] * n_w,
        out_shape=[pltpu.SemaphoreType.DMA((3 * n_w,)), pltpu.SemaphoreType.DMA((3 * n_w,))]
        + [pltpu.HBM(v.shape, BF16) for v in views],
        input_output_aliases={w: 2 + w for w in range(n_w)},
        compiler_params=pltpu.CompilerParams(has_side_effects=_EFFECT, collective_id=sibling_id),
    )(*views, send_sems, recv_sems, after)


def _gather_end(views, fwd_send, fwd_recv, after, name):
    n_w = len(views)

    def body(*refs):
        fwd_send_ref, fwd_recv_ref = refs[n_w:n_w + 2]
        bufs = refs[n_w + 3:]
        x, y, c = _place()
        for i in range(n_w):
            for j in range(3):
                cp = _remote(_weight_region(bufs[i], 0, 0), fwd_send_ref.at[3 * i + j], fwd_recv_ref.at[3 * i + j],
                             (x, y, 1 - c))
                cp.wait_recv()
                cp.wait_send()

    outs = pl.pallas_call(
        body, name=name,
        in_specs=[_HBM] * n_w + [_SEM, _SEM, ANY], out_specs=[_HBM] * n_w,
        out_shape=[pltpu.HBM(v.shape, BF16) for v in views],
        input_output_aliases={w: w for w in range(n_w)},
        compiler_params=pltpu.CompilerParams(has_side_effects=_EFFECT),
    )(*views, fwd_send, fwd_recv, after)
    return [_weight_unview(o) for o in outs]


def _handshake(peers):
    x, y, c = _place()
    chips = _other_chips(x, y)
    devices = [(x, y, 1 - c)] if "sibling" in peers else []
    if "chips" in peers:
        devices += [(*chip, c) for chip in chips]
    elif "neighbours" in peers:
        devices += [(*chip, c) for chip in chips[:2]]
    barrier = pltpu.get_barrier_semaphore()
    for device in devices:
        pl.semaphore_signal(barrier, inc=1, device_id=device, device_id_type=MESH)
    pl.semaphore_wait(barrier, len(devices))


def _comm_call(name, bufs, sem_pairs, after, n_new, fn, sibling_id=None, barrier=None):
    n, n_sem, after = len(bufs), 2 * len(sem_pairs), tuple(after)
    n_out_sem = 2 if n_new else 0
    assert sibling_id is None or barrier is None
    collective_id = sibling_id if barrier is None else barrier[0]

    def body(*refs):
        if sibling_id is not None:
            _sibling_handshake()
        if barrier is not None:
            _handshake(barrier[1])
        sems = refs[n:n + n_sem]
        outs = refs[n + n_sem + len(after):]
        new = outs[:n_out_sem] if n_new else (None, None)
        fn(outs[n_out_sem:], [(sems[2 * i], sems[2 * i + 1]) for i in range(len(sem_pairs))], *new)

    res = pl.pallas_call(
        body, name=name,
        in_specs=[_HBM] * n + [_SEM] * n_sem + [ANY] * len(after),
        out_specs=[_SEM] * n_out_sem + [_HBM] * n,
        out_shape=[pltpu.SemaphoreType.DMA((n_new,))] * n_out_sem + [pltpu.HBM(b.shape, b.dtype) for b in bufs],
        input_output_aliases={i: n_out_sem + i for i in range(n)},
        compiler_params=pltpu.CompilerParams(has_side_effects=_EFFECT, collective_id=collective_id),
    )(*bufs, *[s for pair in sem_pairs for s in pair], *after)
    return list(res[:n_out_sem]), list(res[n_out_sem:])


def _quarter(piece, q):
    rows = piece.shape[0] // 2
    return piece.at[pl.ds(q * rows, rows)]


def _gather_in_start(view, name):
    def fn(bufs, _, send, recv):
        x, y, c = _place()

        def go(ss):
            for j, chip in enumerate(_other_chips(x, y)[:2]):
                _remote(_weight_region(bufs[0], ss, c), send.at[j], recv.at[j], (*chip, c)).start()

        _for_my_shard(go)

    sems, (view,) = _comm_call(name, [_in_hbm(view)], [], (), 2, fn,
                               barrier=(_SIBLING_IDS["in_start"], ("neighbours",)))
    return sems, view


def _gather_out_gate_start(v_out, v_gate, after, name):
    def fn(bufs, _, send, recv):
        x, y, c = _place()
        chips = _other_chips(x, y)

        def go(ss):
            for j in range(3):
                _remote(_weight_region(bufs[0], ss, c), send.at[j], recv.at[j], (*chips[j], c)).start()
            for j in range(2):
                _remote(_weight_region(bufs[1], ss, c), send.at[3 + j], recv.at[3 + j], (*chips[j], c)).start()

        _for_my_shard(go)

    sems, views = _comm_call(name, [_in_hbm(v_out), _in_hbm(v_gate)], [], after, 5, fn,
                             barrier=(_SIBLING_IDS["out_gate_start"], ("chips",)))
    return sems, views


def _gather_relay(view, started, base, after, name, barrier_id, then=None, then_peers=0):
    n_new = 6 + then_peers if then_peers else 4

    def fn(bufs, pairs, send, recv):
        (send_in, recv_in), = pairs
        x, y, c = _place()
        chips = _other_chips(x, y)
        sibling = (x, y, 1 - c)

        def go(ss):
            landed = [_weight_region(bufs[0], ss ^ _FLIPS[j], c) for j in range(2)]
            for j in range(2):
                _remote(landed[j], send_in.at[base + j], recv_in.at[base + j], sibling).wait_recv()
            for j in range(2):
                _remote(_quarter(landed[j], j), send.at[j], recv.at[j], (*chips[1 - j], c)).start()
            for j in range(2):
                _remote(landed[j], send.at[2 + j], recv.at[2 + j], sibling).start()
            for j in range(then_peers):
                _remote(_weight_region(bufs[1], ss, c), send.at[6 + j], recv.at[6 + j], (*chips[j], c)).start()

        _for_my_shard(go)
        for j in range(2):
            _remote(_weight_region(bufs[0], 0, 0), send_in.at[base + j], recv_in.at[base + j], sibling).wait_send()

    views = [view] if then is None else [view, _in_hbm(then)]
    peers = ("sibling", "chips" if then_peers == 3 else "neighbours")
    sems, views = _comm_call(name, views, [started], after, n_new, fn, barrier=(barrier_id, peers))
    return sems, views


def _gather_in_neighbours_end(view, relayed, after, name):
    def fn(bufs, pairs, *_):
        (send, recv), = pairs
        x, y, c = _place()
        for j in range(2):
            cp = _remote(_weight_region(bufs[0], 0, 0), send.at[2 + j], recv.at[2 + j], (x, y, 1 - c))
            cp.wait_recv()
            cp.wait_send()

    _, (view,) = _comm_call(name, [view], [relayed], after, 0, fn)
    return view


def _gather_in_diagonal(view, relayed, after, name, sibling_id):
    def fn(bufs, pairs, send, recv):
        (send_in, recv_in), = pairs
        x, y, c = _place()
        sibling = (x, y, 1 - c)
        any_quarter = _quarter(_weight_region(bufs[0], 0, 0), 0)
        for j in range(2):
            cp = _remote(any_quarter, send_in.at[j], recv_in.at[j], sibling)
            cp.wait_recv()
            cp.wait_send()

        def go(ss):
            _remote(_weight_region(bufs[0], ss ^ _FLIPS[2], c), send.at[0], recv.at[0], sibling).start()

        _for_my_shard(go)

    sems, (view,) = _comm_call(name, [view], [relayed], after, 1, fn, sibling_id=sibling_id)
    return sems, view


def _gather_in_diagonal_end(view, forwarded, after, name):
    def fn(bufs, pairs, *_):
        (send, recv), = pairs
        x, y, c = _place()
        cp = _remote(_weight_region(bufs[0], 0, 0), send.at[0], recv.at[0], (x, y, 1 - c))
        cp.wait_recv()
        cp.wait_send()

    _, (view,) = _comm_call(name, [view], [forwarded], after, 0, fn)
    return view


def _in_proj_shard(h1, wi, proj, shard_arr, name):
    s, d = h1.shape
    n = wi.shape[1]
    tn = 256
    blocks = n // (N_CHIPS * tn)
    given = [] if proj is None else [proj]

    def body(shard_ref, h_ref, w_ref, *rest):
        del shard_ref
        rest[-1][...] = jnp.dot(h_ref[...], w_ref[...], preferred_element_type=F32)

    grid_spec = pltpu.PrefetchScalarGridSpec(
        num_scalar_prefetch=1, grid=(blocks,),
        in_specs=[pl.BlockSpec((s, d), lambda j, shard_ref: (0, 0)),
                  pl.BlockSpec((d, tn), lambda j, shard_ref: (0, shard_ref[0] * blocks + j))] + [ANY] * len(given),
        out_specs=pl.BlockSpec((s, tn), lambda j, shard_ref: (0, shard_ref[0] * blocks + j)))
    return pl.pallas_call(
        body, name=name, grid_spec=grid_spec,
        out_shape=jax.ShapeDtypeStruct((s, n), F32),
        input_output_aliases={3: 0} if given else {},
        compiler_params=_params(("arbitrary",)),
    )(shard_arr, h1, wi, *given)


def _sibling_handshake():
    x, y, c = _place()
    barrier = pltpu.get_barrier_semaphore()
    pl.semaphore_signal(barrier, inc=1, device_id=(x, y, 1 - c), device_id_type=MESH)
    pl.semaphore_wait(barrier, 1)


def _chips_handshake():
    x, y, c = _place()
    barrier = pltpu.get_barrier_semaphore()
    for cx, cy in _other_chips(x, y):
        pl.semaphore_signal(barrier, inc=1, device_id=(cx, cy, c), device_id_type=MESH)
    pl.semaphore_wait(barrier, N_CHIPS - 1)


def _split_start(name, bufs, n_sems, copies, sibling_id=None, chips_id=None):
    n = len(bufs)
    assert sibling_id is None or chips_id is None

    def body(*refs):
        if sibling_id is not None:
            _sibling_handshake()
        if chips_id is not None:
            _chips_handshake()
        send_sems, recv_sems = refs[n:n + 2]
        for cp in copies(refs[n + 2:], send_sems, recv_sems):
            cp.start()

    outs = pl.pallas_call(
        body, name=name,
        in_specs=[_HBM] * n, out_specs=[_SEM, _SEM] + [_HBM] * n,
        out_shape=[pltpu.SemaphoreType.DMA((n_sems,)), pltpu.SemaphoreType.DMA((n_sems,))]
        + [pltpu.HBM(b.shape, b.dtype) for b in bufs],
        input_output_aliases={i: 2 + i for i in range(n)},
        compiler_params=pltpu.CompilerParams(has_side_effects=_EFFECT,
                                             collective_id=sibling_id if chips_id is None else chips_id),
    )(*[_in_hbm(b) for b in bufs])
    return outs[0], outs[1], list(outs[2:])


def _split_wait(name, bufs, send_sems, recv_sems, copies, after):
    n = len(bufs)
    after = tuple(after) if isinstance(after, (list, tuple)) else (after,)

    def body(*refs):
        send_ref, recv_ref = refs[n:n + 2]
        for cp in copies(refs[n + 2 + len(after):], send_ref, recv_ref):
            cp.wait()

    return list(pl.pallas_call(
        body, name=name,
        in_specs=[_HBM] * n + [_SEM, _SEM] + [ANY] * len(after), out_specs=[_HBM] * n,
        out_shape=[pltpu.HBM(b.shape, b.dtype) for b in bufs],
        input_output_aliases={i: i for i in range(n)},
        compiler_params=pltpu.CompilerParams(has_side_effects=_EFFECT),
    )(*bufs, send_sems, recv_sems, *after))


def _halves_copies(n_w):
    def copies(bufs, send_sems, recv_sems):
        x, y, c = _place()
        out = []
        for w in range(n_w):
            view, land = bufs[w], bufs[n_w + w]
            src = view.at[1 - c] if len(view.shape) == 3 else view.at[:, 1 - c]
            out.append(pltpu.make_async_remote_copy(
                src_ref=src, dst_ref=land, send_sem=send_sems.at[w], recv_sem=recv_sems.at[w],
                device_id=(x, y, 1 - c), device_id_type=MESH))
        return out
    return copies


def _pieces_copies(n_w):
    def copies(bufs, send_sems, recv_sems):
        x, y, c = _place()
        out = []
        for w in range(n_w):
            for j, (cx, cy) in enumerate(_other_chips(x, y)):
                out.append(pltpu.make_async_remote_copy(
                    src_ref=bufs[w].at[2 * cx + cy], dst_ref=bufs[n_w + w].at[j],
                    send_sem=send_sems.at[3 * w + j], recv_sem=recv_sems.at[3 * w + j],
                    device_id=(cx, cy, c), device_id_type=MESH))
        return out
    return copies


def _join_copies(n_w):
    def copies(bufs, send_sems, recv_sems):
        x, y, c = _place()
        return [pltpu.make_async_remote_copy(
            src_ref=bufs[w].at[c], dst_ref=bufs[w].at[c], send_sem=send_sems.at[w], recv_sem=recv_sems.at[w],
            device_id=(x, y, 1 - c), device_id_type=MESH) for w in range(n_w)]
    return copies


def _halves_landing(view):
    shape = view.shape[1:] if view.ndim == 3 else (N_CHIPS,) + view.shape[2:]
    return lax.empty(shape, BF16)


_SIBLING_IDS = {"halves_down": 1, "halves_gate_up": 2, "halves_out": 3, "halves_in": 4,
                "join_down": 5, "join_gate_up": 6, "join_out": 7, "join_in": 8,
                "diagonal_in": 9, "diagonal_gate": 10, "diagonal_up": 11, "forward_out": 12, "forward_down": 13,
                "pieces_down": 14, "pieces_gate_up": 15, "pieces_out": 16, "pieces_in": 17,
                "in_start": 18, "out_gate_start": 19, "relay_in": 20, "relay_gate": 21, "relay_up": 22}


def _halves_start(tag, grads, column_sharded):
    views = [_weight_view(g, cs) for g, cs in zip(grads, column_sharded)]
    n = len(views)
    return _split_start("halves_start_" + tag, views + [_halves_landing(v) for v in views], n, _halves_copies(n),
                        sibling_id=_SIBLING_IDS["halves_" + tag])


def _halves_wait(tag, state, after):
    send_sems, recv_sems, bufs = state
    n = len(bufs) // 2
    bufs = _split_wait("halves_wait_" + tag, bufs, send_sems, recv_sems, _halves_copies(n), after)
    return bufs[:n], bufs[n:]


def _pieces_start(tag, pieces):
    n = len(pieces)
    landing = [lax.empty((3,) + p.shape[1:], BF16) for p in pieces]
    return _split_start("pieces_start_" + tag, list(pieces) + landing, 3 * n, _pieces_copies(n),
                        chips_id=_SIBLING_IDS["pieces_" + tag])


def _pieces_wait(tag, state, after):
    send_sems, recv_sems, bufs = state
    n = len(bufs) // 2
    bufs = _split_wait("pieces_wait_" + tag, bufs, send_sems, recv_sems, _pieces_copies(n), after)
    return bufs[:n], bufs[n:]


def _join_start(tag, shards):
    n = len(shards)
    return _split_start("join_start_" + tag, list(shards), n, _join_copies(n), sibling_id=_SIBLING_IDS["join_" + tag])


def _join_wait(tag, state, after):
    send_sems, recv_sems, bufs = state
    bufs = _split_wait("join_wait_" + tag, bufs, send_sems, recv_sems, _join_copies(len(bufs)), after)
    return [b.reshape(2 * b.shape[1], b.shape[2]) for b in bufs]


def _chip_sum_col(g3, sib, c_arr, name):
    _, hk, n = g3.shape
    cols = n // N_CHIPS
    tr = _row_tile(hk, cols * 2, limit=4 * 1024 * 1024)

    def body(c_ref, g_ref, s_ref, o_ref):
        del c_ref
        o_ref[...] = (g_ref[...].astype(F32) + s_ref[...].astype(F32)).astype(BF16)

    grid_spec = pltpu.PrefetchScalarGridSpec(
        num_scalar_prefetch=1, grid=(N_CHIPS, hk // tr),
        in_specs=[pl.BlockSpec((None, tr, cols), lambda p, r, c_ref: (c_ref[0], r, p)),
                  pl.BlockSpec((tr, cols), lambda p, r, c_ref: (r, p))],
        out_specs=pl.BlockSpec((None, tr, cols), lambda p, r, c_ref: (p, r, 0)))
    return pl.pallas_call(
        body, name=name, grid_spec=grid_spec,
        out_shape=jax.ShapeDtypeStruct((N_CHIPS, hk, cols), BF16),
        compiler_params=_params(("parallel", "parallel")),
    )(c_arr, g3, sib)


def _chip_sum_row(g4, sib, c_arr, name):
    _, _, hr, n = g4.shape
    tr = _row_tile(hr, n * 2, limit=4 * 1024 * 1024)

    def body(c_ref, g_ref, s_ref, o_ref):
        del c_ref
        o_ref[...] = (g_ref[...].astype(F32) + s_ref[...].astype(F32)).astype(BF16)

    grid_spec = pltpu.PrefetchScalarGridSpec(
        num_scalar_prefetch=1, grid=(N_CHIPS, hr // tr),
        in_specs=[pl.BlockSpec((None, None, tr, n), lambda p, r, c_ref: (p, c_ref[0], r, 0)),
                  pl.BlockSpec((None, tr, n), lambda p, r, c_ref: (p, r, 0))],
        out_specs=pl.BlockSpec((None, tr, n), lambda p, r, c_ref: (p, r, 0)))
    return pl.pallas_call(
        body, name=name, grid_spec=grid_spec,
        out_shape=jax.ShapeDtypeStruct((N_CHIPS, hr, n), BF16),
        compiler_params=_params(("parallel", "parallel")),
    )(c_arr, g4, sib)


def _sum_pieces(pieces, received, place_arr, name):
    _, r, n = pieces.shape
    tr = _row_tile(r, n * 4, limit=4 * 1024 * 1024)

    def body(p_ref, own_ref, r0_ref, r1_ref, r2_ref, o_ref):
        del p_ref
        acc = own_ref[...].astype(F32) + r0_ref[...].astype(F32)
        acc = acc + r1_ref[...].astype(F32)
        o_ref[...] = acc + r2_ref[...].astype(F32)

    def recv_spec(j):
        return pl.BlockSpec((None, tr, n), lambda i, p_ref: (j, i, 0))

    grid_spec = pltpu.PrefetchScalarGridSpec(
        num_scalar_prefetch=1, grid=(r // tr,),
        in_specs=[pl.BlockSpec((None, tr, n), lambda i, p_ref: (p_ref[0], i, 0)),
                  recv_spec(0), recv_spec(1), recv_spec(2)],
        out_specs=pl.BlockSpec((None, tr, n), lambda i, p_ref: (p_ref[1], i, 0)))
    return pl.pallas_call(
        body, name=name, grid_spec=grid_spec,
        out_shape=jax.ShapeDtypeStruct((2, r, n), F32),
        compiler_params=_params(("parallel",)),
    )(place_arr, pieces, received, received, received)


def _norm_weights_step(parts, w, m, v, after=()):
    rows, d = parts.shape
    after = tuple(after)

    def body(p_ref, w_ref, m_ref, v_ref, *rest):
        g_ref, d_ref, mo_ref, vo_ref, gathered, send_sems, recv_sems = rest[len(after):]
        x, y, c = _place()
        me = 4 * x + 2 * y + c
        gathered[me] = p_ref[...]
        copies = []
        for k in range(1, N_DEV):
            peer = (x ^ ((k >> 2) & 1), y ^ ((k >> 1) & 1), c ^ (k & 1))
            copies.append(pltpu.make_async_remote_copy(
                src_ref=p_ref, dst_ref=gathered.at[me], send_sem=send_sems.at[k - 1],
                recv_sem=recv_sems.at[k - 1], device_id=peer, device_id_type=MESH))
        for cp in copies:
            cp.start()
        for cp in copies:
            cp.wait()
        g = gathered[0]
        for k in range(1, N_DEV):
            g = g + gathered[k]
        delta, m_new, v_new = _adamw_math(w_ref[...], g, m_ref[...], v_ref[...])
        g_ref[...] = g
        d_ref[...] = delta
        mo_ref[...] = m_new
        vo_ref[...] = v_new

    vmem = pl.BlockSpec(memory_space=pltpu.VMEM)
    shp = jax.ShapeDtypeStruct((rows, d), F32)
    return pl.pallas_call(
        body, name="norm_weights_step",
        in_specs=[vmem] * 4 + [ANY] * len(after), out_specs=[vmem] * 4, out_shape=[shp] * 4,
        scratch_shapes=[pltpu.VMEM((N_DEV, rows, d), F32), pltpu.SemaphoreType.DMA((N_DEV - 1,)),
                        pltpu.SemaphoreType.DMA((N_DEV - 1,))],
        compiler_params=pltpu.CompilerParams(has_side_effects=True),
    )(parts, w, m, v, *after)


def kernel(x, norm_mix_w, w_in, w_out, norm_ffn_w, w_gate, w_up, w_down, norm_final_w, loss_target, m_norm_mix_w, m_w_in, m_w_out, m_norm_ffn_w, m_w_gate, m_w_up, m_w_down, m_norm_final_w, v_norm_mix_w, v_w_in, v_w_out, v_norm_ffn_w, v_w_gate, v_w_up, v_w_down, v_norm_final_w):
    s, d = x.shape[1], x.shape[2]
    xs = x.reshape(s, d)
    target = loss_target.reshape(s, d)
    big = {"w_in": (w_in, m_w_in, v_w_in), "w_out": (w_out, m_w_out, v_w_out),
           "w_gate": (w_gate, m_w_gate, v_w_gate), "w_up": (w_up, m_w_up, v_w_up),
           "w_down": (w_down, m_w_down, v_w_down)}
    big = {k: tuple(a.reshape(a.shape[1:]) for a in t) for k, t in big.items()}
    col_names, row_names = ("w_in", "w_gate", "w_up"), ("w_out", "w_down")
    n_in = N_CHIPS * big["w_in"][0].shape[1]
    ffn = N_CHIPS * big["w_gate"][0].shape[1]
    mix = ATTN_WIDTH + RET_WIDTH
    c_arr = lax.axis_index("c").astype(I32).reshape(1)
    shard_arr = (2 * lax.axis_index("x") + lax.axis_index("y")).astype(I32).reshape(1)
    place_arr = jnp.concatenate([shard_arr, c_arr])

    def cast(k, after=()):
        return _weight_view(_cast_into_full(big[k][0], shard_arr, k in col_names, "cast_" + k, after), k in col_names)

    started_in, v_in = _gather_in_start(cast("w_in"), "gather_in_start")

    sec = ATTN_WIDTH

    def section(p, rows):
        return pl.BlockSpec((None, rows, sec), lambda i, j, kk: (p, i, 0))

    h1 = _rms_fwd(xs, norm_mix_w, "rms_mix_fwd", after=[v_in])
    my_shard = shard_arr[0]
    shard_of = [jnp.bitwise_xor(my_shard, f).astype(I32).reshape(1) for f in (0,) + _FLIPS]
    proj = _in_proj_shard(h1, _weight_unview(v_in), None, shard_of[0], "in_proj_own")
    early_views = [cast(k, after=[proj]) for k in ("w_out", "w_gate")]
    v_up, v_down = [cast(k, after=[proj]) for k in ("w_up", "w_down")]
    relayed_in, (v_in,) = _gather_relay(v_in, started_in, 0, early_views + [v_up, v_down], "gather_in_relay",
                                        _SIBLING_IDS["relay_in"])
    started_og, (v_out, v_gate) = _gather_out_gate_start(*early_views, [v_in], "gather_out_gate_start")
    v_in = _gather_in_neighbours_end(v_in, relayed_in, [v_out], "gather_in_neighbours_end")
    proj = _in_proj_shard(h1, _weight_unview(v_in), proj, shard_of[1], "in_proj_x")
    proj = _in_proj_shard(h1, _weight_unview(v_in), proj, shard_of[2], "in_proj_y")
    forwarded_in, v_in = _gather_in_diagonal(v_in, relayed_in, [proj], "gather_in_diagonal",
                                             _SIBLING_IDS["diagonal_in"])
    wi = _weight_unview(_gather_in_diagonal_end(v_in, forwarded_in, [proj], "gather_in_diagonal_end"))
    proj = _in_proj_shard(h1, wi, proj, shard_of[3], "in_proj_diagonal")
    fs_o, fr_o, v_out = _gather_forward([v_out], [0], *started_og, proj, "gather_forward_out",
                                        sibling_id=_SIBLING_IDS["forward_out"])
    mixed, attn_o, lse = _attn_fwd(proj, after=[v_out])
    relayed_g, (v_gate, v_up) = _gather_relay(v_gate, started_og, 3, [attn_o], "gather_gate_relay",
                                              _SIBLING_IDS["relay_gate"], then=v_up, then_peers=2)
    mixed, ret_raw = _ret_fwd(proj, mixed, after=[v_gate])
    wo, = _gather_end([v_out], fs_o, fr_o, ret_raw, "gather_end_out")
    x1, = _matmul("out_proj", "nn", [mixed, mixed], [wo, wo], [0, 0], s, d, sec, s // 2, 512, sec, [xs], [F32],
                  _epi_residual, b_koff=[0, 1], a_specs=[section(0, s // 2), section(1, s // 2)])
    h2 = _rms_fwd(x1, norm_ffn_w, "rms_ffn_fwd")
    relayed_u, (v_up, v_down) = _gather_relay(v_up, relayed_g, 6, [h2], "gather_up_relay",
                                              _SIBLING_IDS["relay_up"], then=v_down, then_peers=3)
    v_gate = _gather_in_neighbours_end(v_gate, relayed_g, [v_up], "gather_gate_neighbours_end")
    forwarded_g, v_gate = _gather_in_diagonal(v_gate, relayed_g, [v_up], "gather_gate_diagonal",
                                              _SIBLING_IDS["diagonal_gate"])
    v_up = _gather_in_neighbours_end(v_up, relayed_u, [v_gate], "gather_up_neighbours_end")
    wg = _weight_unview(_gather_in_diagonal_end(v_gate, forwarded_g, [v_up], "gather_gate_diagonal_end"))
    forwarded_u, v_up = _gather_in_diagonal(v_up, relayed_u, [wg], "gather_up_diagonal",
                                            _SIBLING_IDS["diagonal_up"])
    wu = _weight_unview(_gather_in_diagonal_end(v_up, forwarded_u, [wg], "gather_up_diagonal_end"))
    gate, up, act = _matmul("gate_up", "nn", [h2, h2], [wg, wu], [0, 1], s, ffn, d, s, 512, d, [],
                            [BF16, BF16, BF16], _epi_swiglu, a_single_buffer=True)
    fs, fr, v_down = _gather_forward([v_down], [0], *relayed_u, act, "gather_forward_down", base=6,
                                     sibling_id=_SIBLING_IDS["forward_down"])
    wd, = _gather_end([v_down], fs, fr, act, "gather_end_down")
    x2, = _matmul("down_proj", "nn", [act], [wd], [0], s, d, ffn, s // 2, 512, ffn, [x1], [F32],
                  _epi_residual)
    loss_row, dx2, dx2b, dwf = _final_norm_loss(x2, norm_final_w.reshape(1, d), target, "final_norm_loss")

    names = col_names + row_names
    grads, new = {}, {}

    def chip_sums(tag_names, views, sibs):
        return [(_chip_sum_col if k in col_names else _chip_sum_row)(v, sb, c_arr, "chip_sum_" + k)
                for k, v, sb in zip(tag_names, views, sibs)]

    def piece_sums(tag_names, pieces, received):
        return [_sum_pieces(p, r, place_arr, "sum_pieces_" + k) for k, p, r in zip(tag_names, pieces, received)]

    def update(k):
        new[k] = _adamw(big[k][0], grads[k], big[k][1], big[k][2], "adamw_" + k)

    dgate, dup = _matmul("d_act", "nt", [dx2b], [wd], [0], s, ffn, d, s, 512, d, [gate, up],
                         [BF16, BF16], _epi_swiglu_bwd, a_single_buffer=True)
    g_wd, = _matmul("g_w_down", "tn", [act], [dx2b], [0], ffn, d, s, 512, d, s, [], [BF16], _epi_plain)
    halves_d = _halves_start("down", [g_wd], [False])
    dh2, = _matmul("d_h2", "nt", [dgate, dup], [wg, wu], [0, 0], s, d, ffn, s // 2, 256, ffn, [], [F32],
                   _epi_plain, after=halves_d[2][-1:], a_single_buffer=True)
    pieces_d = _pieces_start("down", chip_sums(["w_down"], *_halves_wait("down", halves_d, dh2)))
    g_wg, g_wu = _matmul("g_w_gate_up", "tn", [h2, h2], [dgate, dup], [0, 1], d, ffn, s, 1024, 512, s, [],
                         [BF16, BF16], _epi_two, after=pieces_d[2][-1:])
    halves_gu = _halves_start("gate_up", [g_wg, g_wu], [True, True])
    dx1, dx1b, dw_ffn = _rms_bwd(x1, norm_ffn_w, dh2, dx2, "rms_ffn_bwd", after=halves_gu[2][-1:])

    dmixed, = _matmul("d_mixed", "nt", [dx1b], [wo], [0], s, mix, d, s // 2, 512, d, [], [F32], _epi_plain)
    pieces_gu = _pieces_start("gate_up", chip_sums(["w_gate", "w_up"], *_halves_wait("gate_up", halves_gu, dmixed)))
    per = sec // 512
    g_wo, = _matmul("g_w_out", "tn", [mixed], [dx1b], [0], mix, d, s, 512, d, s, [], [BF16], _epi_plain,
                    after=pieces_gu[2][-1:],
                    a_specs=[pl.BlockSpec((None, s, 512), lambda i, j, kk: (i // per, 0, i % per))])
    halves_o = _halves_start("out", [g_wo], [False])
    dsec = _attn_bwd(proj, attn_o, lse, dmixed, after=halves_o[2][-1:])
    pieces_o = _pieces_start("out", chip_sums(["w_out"], *_halves_wait("out", halves_o, dsec)))
    dsec = _ret_bwd(proj, ret_raw, dmixed, dsec, after=pieces_o[2][-1:])
    where = [0, 1, 2, 4, 5, 6, 7]
    n_sec = len(where)
    g_wi, = _matmul("g_w_in", "tn", [h1], [dsec], [0], d, n_in, s, 1024, sec, s, [], [BF16], _epi_plain,
                    b_specs=[pl.BlockSpec((None, s, sec), lambda i, j, kk: (j + (j >= 3).astype(I32), 0, 0))])
    halves_i = _halves_start("in", [g_wi], [True])
    dh1, = _matmul("d_h1", "nt", [dsec] * n_sec, [wi] * n_sec, [0] * n_sec, s, d, sec, s // 2, 256, sec, [], [F32],
                   _epi_plain, b_koff=list(range(n_sec)), after=halves_i[2][-1:],
                   a_specs=[section(p, s // 2) for p in where])
    pieces_i = _pieces_start("in", chip_sums(["w_in"], *_halves_wait("in", halves_i, dh1)))
    grad_x, _, dw_mix = _rms_bwd(xs, norm_mix_w, dh1, dx1, "rms_mix_bwd", after=pieces_i[2][-1:])

    def rows8(*vs):
        return jnp.concatenate([v.reshape(1, d) for v in vs] + [jnp.zeros((8 - len(vs), d), F32)], axis=0)

    join_d = _join_start("down", piece_sums(["w_down"], *_pieces_wait("down", pieces_d, grad_x)))
    join_gu = _join_start("gate_up", piece_sums(["w_gate", "w_up"], *_pieces_wait("gate_up", pieces_gu, join_d[2][0])))
    join_o = _join_start("out", piece_sums(["w_out"], *_pieces_wait("out", pieces_o, join_gu[2][0])))
    grads["w_down"], = _join_wait("down", join_d, join_o[2][0])
    update("w_down")
    grads["w_gate"], grads["w_up"] = _join_wait("gate_up", join_gu, new["w_down"][0])
    update("w_gate")
    update("w_up")
    grads["w_out"], = _join_wait("out", join_o, new["w_up"][0])
    update("w_out")
    others_done = [new[k][0] for k in ("w_down", "w_gate", "w_up", "w_out")]
    join_i = _join_start("in", piece_sums(["w_in"], *_pieces_wait("in", pieces_i, others_done)))
    ng, nd, nm, nv = _norm_weights_step(
        rows8(dw_mix, dw_ffn, dwf, jnp.broadcast_to(loss_row[:, :1], (1, d))),
        rows8(norm_mix_w, norm_ffn_w, norm_final_w),
        rows8(m_norm_mix_w, m_norm_ffn_w, m_norm_final_w), rows8(v_norm_mix_w, v_norm_ffn_w, v_norm_final_w),
        after=join_i[2][:1])
    grads["w_in"], = _join_wait("in", join_i, ng)
    update("w_in")

    loss = ng[3, 0]

    def pack(small, per_weight):
        lead = lambda a: a.reshape((1,) + a.shape)
        return (small[0:1], lead(per_weight["w_in"]), lead(per_weight["w_out"]), small[1:2],
                lead(per_weight["w_gate"]), lead(per_weight["w_up"]), lead(per_weight["w_down"]), small[2])

    return (loss, grad_x.reshape(1, s, d),
            *pack(ng, {k: new[k][3] for k in names}),
            *pack(nd, {k: new[k][0] for k in names}),
            *pack(nm, {k: new[k][1] for k in names}),
            *pack(nv, {k: new[k][2] for k in names}))
```

```python
import functools
import math

import jax
import jax.numpy as jnp
from jax import lax
from jax.experimental import pallas as pl
from jax.experimental.pallas import tpu as pltpu

F32 = jnp.float32
BF16 = jnp.bfloat16
I32 = jnp.int32
MESH = pl.DeviceIdType.MESH
ANY = pl.BlockSpec(memory_space=pl.ANY)

ATTN_HEADS = 8
ATTN_HEAD_DIM = 128
RET_HEADS = 4
RET_HEAD_DIM = 256
ATTN_WIDTH = ATTN_HEADS * ATTN_HEAD_DIM
RET_WIDTH = RET_HEADS * RET_HEAD_DIM
DILATED_PATTERNS = ((128, 1), (512, 4), (2048, 16))
NORM_EPS = 1e-6
ADAM_LR = 0.001
ADAM_B1 = 0.9
ADAM_B2 = 0.999
ADAM_EPS = 1e-08
ADAM_WD = 0.01
ADAM_STEP = 10

N_CHIPS = 4
N_DEV = 8
NEG_BIG = -1e30
SEQ_TILE = 512
ATTN_FWD_HEADS_PER_STEP = 2
ATTN_HEADS_PER_STEP = 1
VMEM_LIMIT_BYTES = 56 * 1024 * 1024


def _params(semantics=None, vmem=VMEM_LIMIT_BYTES):
    return pltpu.CompilerParams(dimension_semantics=semantics, vmem_limit_bytes=vmem)


def _row_tile(rows, row_bytes, limit=2 * 1024 * 1024, mult=16):
    best = None
    for t in range(mult, rows + 1, mult):
        if rows % t == 0 and t * row_bytes <= limit:
            best = t
    assert best is not None, (rows, row_bytes)
    return best


def _sigmoid(x):
    return 1.0 / (1.0 + jnp.exp(-x))


def _select_by_index(idx, values):
    out = jnp.float32(values[-1])
    for i in range(len(values) - 2, -1, -1):
        out = jnp.where(idx == i, jnp.float32(values[i]), out)
    return out


def _place():
    x, y, c = lax.axis_index("x"), lax.axis_index("y"), lax.axis_index("c")
    return x, y, c


def _cast_into_full(w, shard_arr, column_sharded, name, after=()):
    after = tuple(after)
    rows, cols = w.shape
    tr = _row_tile(rows, cols * 4)
    steps = rows // tr
    if column_sharded:
        out_shape, out_map = (rows, N_CHIPS * cols), (lambda i, s_ref: (i, s_ref[0]))
    else:
        out_shape, out_map = (N_CHIPS * rows, cols), (lambda i, s_ref: (s_ref[0] * steps + i, 0))

    def body(s_ref, w_ref, *rest):
        del s_ref
        rest[-1][...] = w_ref[...].astype(BF16)

    grid_spec = pltpu.PrefetchScalarGridSpec(
        num_scalar_prefetch=1, grid=(steps,),
        in_specs=[pl.BlockSpec((tr, cols), lambda i, s_ref: (i, 0))] + [ANY] * len(after),
        out_specs=pl.BlockSpec((tr, cols), out_map))
    return pl.pallas_call(
        body, name=name, grid_spec=grid_spec,
        out_shape=jax.ShapeDtypeStruct(out_shape, BF16),
        compiler_params=_params(("parallel",)),
    )(shard_arr, w, *after)


def _rms_fwd(x, w, name, after=()):
    rows, d = x.shape
    tr = 256
    after = tuple(after)

    def body(x_ref, w_ref, *rest):
        xv = x_ref[...]
        r = lax.rsqrt(jnp.mean(xv * xv, axis=-1, keepdims=True) + NORM_EPS)
        rest[-1][...] = (xv * r * w_ref[...]).astype(BF16)

    return pl.pallas_call(
        body, name=name, grid=(rows // tr,),
        in_specs=[pl.BlockSpec((tr, d), lambda i: (i, 0)), pl.BlockSpec((1, d), lambda i: (0, 0))]
        + [ANY] * len(after),
        out_specs=pl.BlockSpec((tr, d), lambda i: (i, 0)),
        out_shape=jax.ShapeDtypeStruct((rows, d), BF16),
        compiler_params=_params(("parallel",)),
    )(x, w, *after)


def _rms_bwd(x, w, dh, dres, name, after=()):
    rows, d = x.shape
    tr = 256
    after = tuple(after)

    def body(x_ref, w_ref, dh_ref, dres_ref, *rest):
        dx_ref, dxb_ref, dw_ref = rest[len(after):]
        xv = x_ref[...]
        r = lax.rsqrt(jnp.mean(xv * xv, axis=-1, keepdims=True) + NORM_EPS)
        xhat = xv * r
        dy = dh_ref[...]
        dxhat = dy * w_ref[...]
        dx = dres_ref[...] + r * (dxhat - xhat * jnp.mean(dxhat * xhat, axis=-1, keepdims=True))
        dx_ref[...] = dx
        dxb_ref[...] = dx.astype(BF16)
        part = jnp.sum(dy * xhat, axis=0, keepdims=True)

        @pl.when(pl.program_id(0) == 0)
        def _():
            dw_ref[...] = part

        @pl.when(pl.program_id(0) != 0)
        def _():
            dw_ref[...] += part

    row = pl.BlockSpec((tr, d), lambda i: (i, 0))
    vec = pl.BlockSpec((1, d), lambda i: (0, 0))
    return pl.pallas_call(
        body, name=name, grid=(rows // tr,),
        in_specs=[row, vec, row, row] + [ANY] * len(after),
        out_specs=[row, row, vec],
        out_shape=[jax.ShapeDtypeStruct((rows, d), F32), jax.ShapeDtypeStruct((rows, d), BF16),
                   jax.ShapeDtypeStruct((1, d), F32)],
        compiler_params=_params(("arbitrary",)),
    )(x, w, dh, dres, *after)


def _final_norm_loss(x2, w, target, name):
    rows, d = x2.shape
    tr = 256

    def body(x_ref, w_ref, t_ref, loss_ref, dx_ref, dxb_ref, dw_ref):
        xv = x_ref[...]
        wv = w_ref[...]
        r = lax.rsqrt(jnp.mean(xv * xv, axis=-1, keepdims=True) + NORM_EPS)
        xhat = xv * r
        err = xhat * wv - t_ref[...]
        part_loss = 0.5 * jnp.sum(jnp.mean(err * err, axis=-1, keepdims=True), axis=0, keepdims=True)
        dy = err * (1.0 / d)
        dxhat = dy * wv
        dx = r * (dxhat - xhat * jnp.mean(dxhat * xhat, axis=-1, keepdims=True))
        dx_ref[...] = dx
        dxb_ref[...] = dx.astype(BF16)
        part_dw = jnp.sum(dy * xhat, axis=0, keepdims=True)
        part_loss = jnp.broadcast_to(part_loss, (1, 128))

        @pl.when(pl.program_id(0) == 0)
        def _():
            dw_ref[...] = part_dw
            loss_ref[...] = part_loss

        @pl.when(pl.program_id(0) != 0)
        def _():
            dw_ref[...] += part_dw
            loss_ref[...] += part_loss

    row = pl.BlockSpec((tr, d), lambda i: (i, 0))
    vec = pl.BlockSpec((1, d), lambda i: (0, 0))
    return pl.pallas_call(
        body, name=name, grid=(rows // tr,),
        in_specs=[row, vec, row],
        out_specs=[pl.BlockSpec((1, 128), lambda i: (0, 0)), row, row, vec],
        out_shape=[jax.ShapeDtypeStruct((1, 128), F32), jax.ShapeDtypeStruct((rows, d), F32),
                   jax.ShapeDtypeStruct((rows, d), BF16), jax.ShapeDtypeStruct((1, d), F32)],
        compiler_params=_params(("arbitrary",)),
    )(x2, w, target)


def _adamw_math(w, g, m, v):
    m = ADAM_B1 * m + (1.0 - ADAM_B1) * g
    v = ADAM_B2 * v + (1.0 - ADAM_B2) * (g * g)
    m_hat = m / (1.0 - ADAM_B1 ** ADAM_STEP)
    v_hat = v / (1.0 - ADAM_B2 ** ADAM_STEP)
    delta = -ADAM_LR * (m_hat / (jnp.sqrt(v_hat) + ADAM_EPS) + ADAM_WD * w)
    return delta, m, v


def _adamw(w, g, m, v, name):
    rows, cols = w.shape
    tr = _row_tile(rows, cols * 4)

    def body(w_ref, g_ref, m_ref, v_ref, d_ref, mo_ref, vo_ref, go_ref):
        g = g_ref[...]
        delta, m_new, v_new = _adamw_math(w_ref[...], g, m_ref[...], v_ref[...])
        d_ref[...] = delta
        mo_ref[...] = m_new
        vo_ref[...] = v_new
        go_ref[...] = g

    blk = pl.BlockSpec((tr, cols), lambda i: (i, 0))
    shp = jax.ShapeDtypeStruct((rows, cols), F32)
    return pl.pallas_call(
        body, name=name, grid=(rows // tr,),
        in_specs=[blk] * 4, out_specs=[blk] * 4, out_shape=[shp] * 4,
        compiler_params=_params(("parallel",)),
    )(w, g, m, v)


_DOT_DIMS = {"nn": ((1,), (0,)), "nt": ((1,), (1,)), "tn": ((0,), (0,))}


def _matmul(name, mode, a_list, b_list, acc_of, m, n, k, tm, tn, tk, extras, out_dtypes, epilogue,
            a_koff=None, b_koff=None, after=(), a_specs=None, b_specs=None, a_single_buffer=False):
    after = tuple(after)
    assert m % tm == 0 and n % tn == 0 and k % tk == 0, (name, m, n, k, tm, tn, tk)
    nk = k // tk
    n_acc = max(acc_of) + 1
    n_pairs = len(a_list)
    a_koff = a_koff or [0] * n_pairs
    b_koff = b_koff or [0] * n_pairs
    dims = (_DOT_DIMS[mode], ((), ()))
    n_ext, n_out = len(extras), len(out_dtypes)

    def body(*refs):
        a_refs = refs[:n_pairs]
        b_refs = refs[n_pairs:2 * n_pairs]
        e_refs = refs[2 * n_pairs:2 * n_pairs + n_ext]
        first_out = 2 * n_pairs + n_ext + len(after)
        o_refs = refs[first_out:first_out + n_out]
        acc_refs = refs[first_out + n_out:]

        parts = [None] * n_acc
        for p in range(n_pairs):
            d = lax.dot_general(a_refs[p][...], b_refs[p][...], dims, preferred_element_type=F32)
            parts[acc_of[p]] = d if parts[acc_of[p]] is None else parts[acc_of[p]] + d

        def finish(accs):
            outs = epilogue(accs, [e[...] for e in e_refs])
            for o_ref, o in zip(o_refs, outs):
                o_ref[...] = o.astype(o_ref.dtype)

        if nk == 1:
            finish(parts)
        else:
            kk = pl.program_id(2)

            @pl.when(kk == 0)
            def _():
                for acc_ref, part in zip(acc_refs, parts):
                    acc_ref[...] = part

            @pl.when(kk != 0)
            def _():
                for acc_ref, part in zip(acc_refs, parts):
                    acc_ref[...] += part

            @pl.when(kk == nk - 1)
            def _():
                finish([acc_ref[...] for acc_ref in acc_refs])

    def a_spec(off):
        mode_a = pl.Buffered(1) if a_single_buffer else None
        if mode == "tn":
            return pl.BlockSpec((tk, tm), lambda i, j, kk: (kk + off, i), pipeline_mode=mode_a)
        return pl.BlockSpec((tm, tk), lambda i, j, kk: (i, kk + off), pipeline_mode=mode_a)

    def b_spec(off):
        if mode == "nt":
            return pl.BlockSpec((tn, tk), lambda i, j, kk: (j, kk + off))
        return pl.BlockSpec((tk, tn), lambda i, j, kk: (kk + off, j))

    tile = pl.BlockSpec((tm, tn), lambda i, j, kk: (i, j))
    scratch = [pltpu.VMEM((tm, tn), F32) for _ in range(n_acc)] if nk > 1 else []
    return pl.pallas_call(
        body, name=name, grid=(m // tm, n // tn, nk),
        in_specs=(a_specs or [a_spec(o) for o in a_koff]) + (b_specs or [b_spec(o) for o in b_koff])
        + [tile] * n_ext + [ANY] * len(after),
        out_specs=[tile] * n_out,
        out_shape=[jax.ShapeDtypeStruct((m, n), dt) for dt in out_dtypes],
        scratch_shapes=scratch,
        compiler_params=_params(("parallel", "parallel", "arbitrary")),
    )(*a_list, *b_list, *extras, *after)


def _epi_plain(accs, extras):
    return (accs[0],)


def _epi_residual(accs, extras):
    return (accs[0] + extras[0],)


def _epi_two(accs, extras):
    return accs[0], accs[1]


def _epi_swiglu(accs, extras):
    g, u = accs
    return g, u, g * _sigmoid(g) * u


def _epi_swiglu_bwd(accs, extras):
    da = accs[0]
    g, u = (e.astype(F32) for e in extras)
    sg = _sigmoid(g)
    dg = da * u * sg * (1.0 + g * (1.0 - sg))
    du = da * g * sg
    return dg, du


_NT_DIMS = (((1,), (1,)), ((), ()))
_TN_DIMS = (((0,), (0,)), ((), ()))


def _tile_delta(tq, tk):
    return lax.broadcasted_iota(I32, (tq, tk), 0) - lax.broadcasted_iota(I32, (tq, tk), 1)


def _attn_log_count(delta):
    count = jnp.zeros(delta.shape, I32)
    for window, dilation in DILATED_PATTERNS:
        hit = ((delta & (dilation - 1)) == 0) & (delta <= window)
        count = count + jnp.where(hit, 1, 0)
    valid = (delta >= 0) & (count > 0)
    logm = jnp.where(count == 3, math.log(3.0), jnp.where(count == 2, math.log(2.0), 0.0))
    return jnp.where(valid, logm, NEG_BIG)


def _fill_attn_log_count(tab_ref):
    nb, t, _ = tab_ref.shape
    base = _tile_delta(t, t)
    for b in range(nb):
        tab_ref[b] = _attn_log_count(base + b * t)


def _fill_attn_bias(tab_ref, log_count_ref, slope):
    nb, t, _ = tab_ref.shape
    dist = _tile_delta(t, t).astype(F32)
    for b in range(nb):
        tab_ref[b] = log_count_ref[b] - slope * (dist + float(b * t))


def _fill_ret_decay(tab_ref, log_gamma):
    nb, t, _ = tab_ref.shape
    base = _tile_delta(t, t)
    for b in range(nb):
        tab_ref[b] = _ret_decay(base + b * t, log_gamma)


def _alibi_slopes():
    return [2.0 ** (-8.0 * (h + 1) / ATTN_HEADS) for h in range(ATTN_HEADS)]


def _attn_fwd(proj, after=()):
    s = proj.shape[0]
    t = SEQ_TILE
    hd = ATTN_HEAD_DIM
    hp = ATTN_FWD_HEADS_PER_STEP
    ng = ATTN_HEADS // hp
    w = hp * hd
    scale = 1.0 / math.sqrt(hd)
    slopes = _alibi_slopes()

    def body(q_ref, k_ref, v_ref, *rest):
        mix_ref, o_ref, lse_ref, kb, vb, bias_tab, log_count_tab = rest[len(after):]
        g = pl.program_id(0)
        i = pl.program_id(1)

        @pl.when((g == 0) & (i == 0))
        def _():
            _fill_attn_log_count(log_count_tab)

        @pl.when(i == 0)
        def _():
            kb[...] = k_ref[...].astype(BF16)
            vb[...] = v_ref[...].astype(BF16)
            for u in range(hp):
                _fill_attn_bias(bias_tab.at[u], log_count_tab, _select_by_index(g * hp + u, slopes))

        qs = [q_ref[:, u * hd:(u + 1) * hd].astype(BF16) for u in range(hp)]

        def step(j, carry):
            rows = pl.ds(pl.multiple_of(j * t, t), t)
            out = []
            for u in range(hp):
                m_i, l_i, acc = carry[u]
                lanes = slice(u * hd, (u + 1) * hd)
                sc = lax.dot_general(qs[u], kb[rows, lanes], _NT_DIMS, preferred_element_type=F32) * scale
                sc = sc + bias_tab[u, i - j]
                m_new = jnp.maximum(m_i, jnp.max(sc, axis=-1, keepdims=True))
                p = jnp.exp(sc - m_new)
                alpha = jnp.exp(m_i - m_new)
                l_new = alpha * l_i + jnp.sum(p, axis=-1, keepdims=True)
                acc = alpha * acc + jnp.dot(p.astype(BF16), vb[rows, lanes], preferred_element_type=F32)
                out.append((m_new, l_new, acc))
            return tuple(out)

        init = (jnp.full((t, 1), NEG_BIG, F32), jnp.zeros((t, 1), F32), jnp.zeros((t, hd), F32))
        final = lax.fori_loop(0, i + 1, step, (init,) * hp)
        for u in range(hp):
            m_i, l_i, acc = final[u]
            lanes = slice(u * hd, (u + 1) * hd)
            out = acc / l_i
            o_ref[:, lanes] = out
            mix_ref[:, lanes] = out.astype(BF16)
            lse_ref[:, lanes] = jnp.broadcast_to(m_i + jnp.log(l_i), (t, hd))

    return pl.pallas_call(
        body, name="attn_fwd", grid=(ng, s // t),
        in_specs=[pl.BlockSpec((t, w), lambda g, i: (i, g)),
                  pl.BlockSpec((s, w), lambda g, i: (0, ng + g)),
                  pl.BlockSpec((s, w), lambda g, i: (0, 2 * ng + g))] + [ANY] * len(after),
        out_specs=[pl.BlockSpec((None, t, w), lambda g, i: (0, i, g))] + [pl.BlockSpec((t, w), lambda g, i: (i, g))] * 2,
        out_shape=[jax.ShapeDtypeStruct((2, s, ATTN_WIDTH), BF16),
                   jax.ShapeDtypeStruct((s, ATTN_WIDTH), F32),
                   jax.ShapeDtypeStruct((s, ATTN_WIDTH), F32)],
        scratch_shapes=[pltpu.VMEM((s, w), BF16), pltpu.VMEM((s, w), BF16), pltpu.VMEM((hp, s // t, t, t), F32),
                        pltpu.VMEM((s // t, t, t), F32)],
        compiler_params=_params(("arbitrary", "arbitrary")),
    )(proj, proj, proj, *after)


def _attn_bwd(proj, attn_out, lse, dmixed, after=()):
    after = tuple(after)
    s = proj.shape[0]
    t = SEQ_TILE
    nt = s // t
    hd = ATTN_HEAD_DIM
    hp = ATTN_HEADS_PER_STEP
    ng = ATTN_HEADS // hp
    w = hp * hd
    scale = 1.0 / math.sqrt(hd)
    slopes = _alibi_slopes()

    def body(q_ref, k_ref, v_ref, o_ref, lse_ref, do_ref, *rest):
        dsec_ref, qb, kb, vb, dob, dsum, dq_acc, bias_tab, log_count_tab = rest[len(after):]
        g = pl.program_id(0)

        @pl.when(g == 0)
        def _():
            _fill_attn_log_count(log_count_tab)

        qb[...] = q_ref[...].astype(BF16)
        kb[...] = k_ref[...].astype(BF16)
        vb[...] = v_ref[...].astype(BF16)
        dob[...] = do_ref[...].astype(BF16)
        for u in range(hp):
            lanes = slice(u * hd, (u + 1) * hd)
            _fill_attn_bias(bias_tab.at[u], log_count_tab, _select_by_index(g * hp + u, slopes))
            rowsum = jnp.sum(do_ref[:, lanes] * o_ref[:, lanes], axis=-1, keepdims=True)
            dsum[:, lanes] = jnp.broadcast_to(rowsum, (s, hd))
        dq_acc[...] = jnp.zeros((s, w), F32)

        def over_keys(j, _):
            krows = pl.ds(pl.multiple_of(j * t, t), t)

            def over_queries(i, carry):
                qrows = pl.ds(pl.multiple_of(i * t, t), t)
                out = []
                for u in range(hp):
                    dk, dv = carry[u]
                    lanes = slice(u * hd, (u + 1) * hd)
                    qi, doi = qb[qrows, lanes], dob[qrows, lanes]
                    kj, vj = kb[krows, lanes], vb[krows, lanes]
                    lse_i = lse_ref[qrows, lanes][:, :1]
                    dsum_i = dsum[qrows, lanes][:, :1]
                    sc = lax.dot_general(qi, kj, _NT_DIMS, preferred_element_type=F32) * scale
                    p = jnp.exp(sc + bias_tab[u, i - j] - lse_i)
                    dp = lax.dot_general(doi, vj, _NT_DIMS, preferred_element_type=F32)
                    ds = (p * (dp - dsum_i)).astype(BF16)
                    dv = dv + lax.dot_general(p.astype(BF16), doi, _TN_DIMS, preferred_element_type=F32)
                    dk = dk + lax.dot_general(ds, qi, _TN_DIMS, preferred_element_type=F32)
                    dq_acc[qrows, lanes] += jnp.dot(ds, kj, preferred_element_type=F32)
                    out.append((dk, dv))
                return tuple(out)

            zero = jnp.zeros((t, hd), F32)
            final = lax.fori_loop(j, nt, over_queries, ((zero, zero),) * hp)
            for u in range(hp):
                lanes = slice(u * hd, (u + 1) * hd)
                dsec_ref[1, krows, lanes] = (final[u][0] * scale).astype(BF16)
                dsec_ref[2, krows, lanes] = final[u][1].astype(BF16)
            return 0

        lax.fori_loop(0, nt, over_keys, 0)
        dsec_ref[0] = (dq_acc[...] * scale).astype(BF16)

    def col(off):
        return pl.BlockSpec((s, w), lambda g: (0, off + g))

    return pl.pallas_call(
        body, name="attn_bwd", grid=(ng,),
        in_specs=[col(0), col(ng), col(2 * ng), col(0), col(0), col(0)] + [ANY] * len(after),
        out_specs=pl.BlockSpec((4, s, w), lambda g: (0, 0, g)),
        out_shape=jax.ShapeDtypeStruct((8, s, ATTN_WIDTH), BF16),
        scratch_shapes=[pltpu.VMEM((s, w), BF16)] * 4 + [pltpu.VMEM((s, w), F32)] * 2
        + [pltpu.VMEM((hp, nt, t, t), F32), pltpu.VMEM((nt, t, t), F32)],
        compiler_params=_params(("arbitrary",)),
    )(proj, proj, proj, attn_out, lse, dmixed, *after)


def _ret_log_gammas():
    return [math.log(1.0 - 2.0 ** (-5.0 - h)) for h in range(RET_HEADS)]


def _ret_decay(delta, log_gamma):
    dec = jnp.exp(delta.astype(F32) * log_gamma) * (1.0 / math.sqrt(RET_HEAD_DIM))
    return jnp.where(delta >= 0, dec, 0.0)


def _ret_fwd(proj, mixed, after=()):
    after = tuple(after)
    s = proj.shape[0]
    t = SEQ_TILE
    hd = RET_HEAD_DIM
    nh = RET_HEADS
    log_gammas = _ret_log_gammas()
    c0 = 3 * ATTN_WIDTH // hd

    def body(q_ref, k_ref, v_ref, g_ref, *rest):
        mix_ref, raw_ref, kb, vb, decay_tab = rest[1 + len(after):]
        h = pl.program_id(0)
        i = pl.program_id(1)

        @pl.when(i == 0)
        def _():
            kb[...] = k_ref[...].astype(BF16)
            vb[...] = v_ref[...].astype(BF16)
            _fill_ret_decay(decay_tab, _select_by_index(h, log_gammas))

        q = q_ref[...].astype(BF16)

        def step(j, acc):
            rows = pl.ds(pl.multiple_of(j * t, t), t)
            sc = lax.dot_general(q, kb[rows, :], _NT_DIMS, preferred_element_type=F32) * decay_tab[i - j]
            return acc + jnp.dot(sc.astype(BF16), vb[rows, :], preferred_element_type=F32)

        ret = lax.fori_loop(0, i + 1, step, jnp.zeros((t, hd), F32))
        raw_ref[...] = ret
        r = lax.rsqrt(jnp.mean(ret * ret, axis=-1, keepdims=True) + NORM_EPS)
        g = g_ref[...]
        mix_ref[...] = (g * _sigmoid(g) * (ret * r)).astype(BF16)

    return pl.pallas_call(
        body, name="ret_fwd", grid=(nh, s // t),
        in_specs=[pl.BlockSpec((t, hd), lambda h, i: (i, c0 + h)),
                  pl.BlockSpec((s, hd), lambda h, i: (0, c0 + nh + h)),
                  pl.BlockSpec((s, hd), lambda h, i: (0, c0 + 2 * nh + h)),
                  pl.BlockSpec((t, hd), lambda h, i: (i, c0 + 3 * nh + h))] + [ANY] * (1 + len(after)),
        out_specs=[pl.BlockSpec((None, t, hd), lambda h, i: (1, i, h)), pl.BlockSpec((t, hd), lambda h, i: (i, h))],
        out_shape=[jax.ShapeDtypeStruct(mixed.shape, BF16), jax.ShapeDtypeStruct((s, RET_WIDTH), F32)],
        input_output_aliases={4: 0},
        scratch_shapes=[pltpu.VMEM((s, hd), BF16), pltpu.VMEM((s, hd), BF16), pltpu.VMEM((s // t, t, t), F32)],
        compiler_params=_params(("arbitrary", "arbitrary")),
    )(proj, proj, proj, proj, mixed, *after)


def _ret_bwd(proj, ret_raw, dmixed, dsec, after=()):
    after = tuple(after)
    s = proj.shape[0]
    t = SEQ_TILE
    nt = s // t
    hd = RET_HEAD_DIM
    nh = RET_HEADS
    log_gammas = _ret_log_gammas()
    c0 = 3 * ATTN_WIDTH // hd
    mixed_blocks = ATTN_WIDTH // hd

    def body(q_ref, k_ref, v_ref, g_ref, raw_ref, dmix_ref, *rest):
        dsec_ref, qb, kb, vb, dretb, dq_acc, decay_tab = rest[1 + len(after):]
        h = pl.program_id(0)
        _fill_ret_decay(decay_tab, _select_by_index(h, log_gammas))
        qb[...] = q_ref[...].astype(BF16)
        kb[...] = k_ref[...].astype(BF16)
        vb[...] = v_ref[...].astype(BF16)
        ret = raw_ref[...]
        r = lax.rsqrt(jnp.mean(ret * ret, axis=-1, keepdims=True) + NORM_EPS)
        normed = ret * r
        g = g_ref[...]
        sg = _sigmoid(g)
        dout = dmix_ref[...]
        dsec_ref[3] = (dout * normed * sg * (1.0 + g * (1.0 - sg))).astype(BF16)
        dn = dout * g * sg
        dret = r * (dn - normed * jnp.mean(dn * normed, axis=-1, keepdims=True))
        dretb[...] = dret.astype(BF16)
        dq_acc[...] = jnp.zeros((s, hd), F32)

        def over_keys(j, _):
            krows = pl.ds(pl.multiple_of(j * t, t), t)
            kj = kb[krows, :]
            vj = vb[krows, :]

            def over_queries(i, carry):
                dk, dv = carry
                qrows = pl.ds(pl.multiple_of(i * t, t), t)
                qi = qb[qrows, :]
                doi = dretb[qrows, :]
                dec = decay_tab[i - j]
                a = (lax.dot_general(qi, kj, _NT_DIMS, preferred_element_type=F32) * dec).astype(BF16)
                da = (lax.dot_general(doi, vj, _NT_DIMS, preferred_element_type=F32) * dec).astype(BF16)
                dv = dv + lax.dot_general(a, doi, _TN_DIMS, preferred_element_type=F32)
                dk = dk + lax.dot_general(da, qi, _TN_DIMS, preferred_element_type=F32)
                dq_acc[qrows, :] += jnp.dot(da, kj, preferred_element_type=F32)
                return dk, dv

            zero = jnp.zeros((t, hd), F32)
            dk, dv = lax.fori_loop(j, nt, over_queries, (zero, zero))
            dsec_ref[1, krows, :] = dk.astype(BF16)
            dsec_ref[2, krows, :] = dv.astype(BF16)
            return 0

        lax.fori_loop(0, nt, over_keys, 0)
        dsec_ref[0] = dq_acc[...].astype(BF16)

    def col(off):
        return pl.BlockSpec((s, hd), lambda h: (0, off + h))

    return pl.pallas_call(
        body, name="ret_bwd", grid=(nh,),
        in_specs=[col(c0), col(c0 + nh), col(c0 + 2 * nh), col(c0 + 3 * nh), col(0), col(mixed_blocks)]
        + [ANY] * (1 + len(after)),
        out_specs=pl.BlockSpec((4, s, hd), lambda h: (1, 0, h)),
        out_shape=jax.ShapeDtypeStruct(dsec.shape, BF16),
        input_output_aliases={6: 0},
        scratch_shapes=[pltpu.VMEM((s, hd), BF16)] * 4 + [pltpu.VMEM((s, hd), F32)]
        + [pltpu.VMEM((nt, t, t), F32)],
        compiler_params=_params(("arbitrary",)),
    )(proj, proj, proj, proj, ret_raw, dmixed, dsec, *after)


_FLIPS = (2, 1, 3)


def _other_chips(x, y):
    return [(1 - x, y), (x, 1 - y), (1 - x, 1 - y)]


_HBM = pl.BlockSpec(memory_space=pltpu.HBM)
_SEM = pl.BlockSpec(memory_space=pltpu.SEMAPHORE)
_EFFECT = pltpu.SideEffectType.DATAFLOW_SIDE_EFFECTING


def _in_hbm(a):
    return pltpu.with_memory_space_constraint(a, pltpu.HBM)


def _weight_view(w, column_sharded):
    if column_sharded:
        return w.reshape(2, w.shape[0] // 2, w.shape[1])
    return w.reshape(N_CHIPS, 2, w.shape[0] // (2 * N_CHIPS), w.shape[1])


def _weight_unview(v):
    if v.ndim == 3:
        return v.reshape(2 * v.shape[1], v.shape[2])
    return v.reshape(N_CHIPS * 2 * v.shape[2], v.shape[3])


def _weight_region(buf, shard, half):
    if len(buf.shape) == 3:
        cols = buf.shape[2] // N_CHIPS
        return buf.at[half, :, pl.ds(shard * cols, cols)]
    return buf.at[shard, half]


def _remote(where, send_sem, recv_sem, to):
    return pltpu.make_async_remote_copy(src_ref=where, dst_ref=where, send_sem=send_sem, recv_sem=recv_sem,
                                        device_id=to, device_id_type=MESH)


def _for_my_shard(fn):
    x, y, _ = _place()
    for ss in range(N_CHIPS):
        pl.when(2 * x + y == ss)(functools.partial(fn, ss))


def _gather_forward(views, which, send_sems, recv_sems, after, name, base=0, sibling_id=None):
    n_w = len(views)
    which = [base // 3 + w for w in which] if base % 3 == 0 else None
    assert which is not None, "base must be a multiple of 3"

    def body(*refs):
        if sibling_id is not None:
            _sibling_handshake()
        send_in, recv_in = refs[n_w:n_w + 2]
        fwd_send, fwd_recv = refs[n_w + 3:n_w + 5]
        bufs = refs[n_w + 5:]
        x, y, c = _place()
        sibling = (x, y, 1 - c)

        def forward(ss):
            for i, w in enumerate(which):
                for j in range(3):
                    landed = _weight_region(bufs[i], ss ^ _FLIPS[j], c)
                    _remote(landed, send_in.at[3 * w + j], recv_in.at[3 * w + j], sibling).wait_recv()
                    _remote(landed, fwd_send.at[3 * i + j], fwd_recv.at[3 * i + j], sibling).start()

        _for_my_shard(forward)
        for i, w in enumerate(which):
            for j in range(3):
                _remote(_weight_region(bufs[i], 0, 0), send_in.at[3 * w + j], recv_in.at[3 * w + j],
                        sibling).wait_send()

    return pl.pallas_call(
        body, name=name,
        in_specs=[_HBM] * n_w + [_SEM, _SEM, ANY], out_specs=[_SEM, _SEM] + [_HBM] * n_w,
        out_shape=[pltpu.SemaphoreType.DMA((3 * n_w,)), pltpu.SemaphoreType.DMA((3 * n_w,))]
        + [pltpu.HBM(v.shape, BF16) for v in views],
        input_output_aliases={w: 2 + w for w in range(n_w)},
        compiler_params=pltpu.CompilerParams(has_side_effects=_EFFECT, collective_id=sibling_id),
    )(*views, send_sems, recv_sems, after)


def _gather_end(views, fwd_send, fwd_recv, after, name):
    n_w = len(views)

    def body(*refs):
        fwd_send_ref, fwd_recv_ref = refs[n_w:n_w + 2]
        bufs = refs[n_w + 3:]
        x, y, c = _place()
        for i in range(n_w):
            for j in range(3):
                cp = _remote(_weight_region(bufs[i], 0, 0), fwd_send_ref.at[3 * i + j], fwd_recv_ref.at[3 * i + j],
                             (x, y, 1 - c))
                cp.wait_recv()
                cp.wait_send()

    outs = pl.pallas_call(
        body, name=name,
        in_specs=[_HBM] * n_w + [_SEM, _SEM, ANY], out_specs=[_HBM] * n_w,
        out_shape=[pltpu.HBM(v.shape, BF16) for v in views],
        input_output_aliases={w: w for w in range(n_w)},
        compiler_params=pltpu.CompilerParams(has_side_effects=_EFFECT),
    )(*views, fwd_send, fwd_recv, after)
    return [_weight_unview(o) for o in outs]


def _comm_call(name, bufs, sem_pairs, after, n_new, fn, sibling_id=None):
    n, n_sem, after = len(bufs), 2 * len(sem_pairs), tuple(after)
    n_out_sem = 2 if n_new else 0

    def body(*refs):
        if sibling_id is not None:
            _sibling_handshake()
        sems = refs[n:n + n_sem]
        outs = refs[n + n_sem + len(after):]
        new = outs[:n_out_sem] if n_new else (None, None)
        fn(outs[n_out_sem:], [(sems[2 * i], sems[2 * i + 1]) for i in range(len(sem_pairs))], *new)

    res = pl.pallas_call(
        body, name=name,
        in_specs=[_HBM] * n + [_SEM] * n_sem + [ANY] * len(after),
        out_specs=[_SEM] * n_out_sem + [_HBM] * n,
        out_shape=[pltpu.SemaphoreType.DMA((n_new,))] * n_out_sem + [pltpu.HBM(b.shape, b.dtype) for b in bufs],
        input_output_aliases={i: n_out_sem + i for i in range(n)},
        compiler_params=pltpu.CompilerParams(has_side_effects=_EFFECT, collective_id=sibling_id),
    )(*bufs, *[s for pair in sem_pairs for s in pair], *after)
    return list(res[:n_out_sem]), list(res[n_out_sem:])


def _quarter(piece, q):
    rows = piece.shape[0] // 2
    return piece.at[pl.ds(q * rows, rows)]


def _gather_in_start(view, name):
    def fn(bufs, _, send, recv):
        x, y, c = _place()

        def go(ss):
            for j, chip in enumerate(_other_chips(x, y)[:2]):
                _remote(_weight_region(bufs[0], ss, c), send.at[j], recv.at[j], (*chip, c)).start()

        _for_my_shard(go)

    sems, (view,) = _comm_call(name, [_in_hbm(view)], [], (), 2, fn)
    return sems, view


def _gather_out_gate_start(v_out, v_gate, after, name):
    def fn(bufs, _, send, recv):
        x, y, c = _place()
        chips = _other_chips(x, y)

        def go(ss):
            for j in range(3):
                _remote(_weight_region(bufs[0], ss, c), send.at[j], recv.at[j], (*chips[j], c)).start()
            for j in range(2):
                _remote(_weight_region(bufs[1], ss, c), send.at[3 + j], recv.at[3 + j], (*chips[j], c)).start()

        _for_my_shard(go)

    sems, views = _comm_call(name, [_in_hbm(v_out), _in_hbm(v_gate)], [], after, 5, fn)
    return sems, views


def _gather_relay(view, started, base, after, name, then=None, then_peers=0, then_first=False):
    n_new = 6 + then_peers if then_peers else 4

    def fn(bufs, pairs, send, recv):
        (send_in, recv_in), = pairs
        x, y, c = _place()
        chips = _other_chips(x, y)
        sibling = (x, y, 1 - c)

        def go(ss):
            def start_then():
                for j in range(then_peers):
                    _remote(_weight_region(bufs[1], ss, c), send.at[6 + j], recv.at[6 + j], (*chips[j], c)).start()

            if then_first:
                start_then()
            landed = [_weight_region(bufs[0], ss ^ _FLIPS[j], c) for j in range(2)]
            for j in range(2):
                _remote(landed[j], send_in.at[base + j], recv_in.at[base + j], sibling).wait_recv()
            for j in range(2):
                _remote(_quarter(landed[j], j), send.at[j], recv.at[j], (*chips[1 - j], c)).start()
            for j in range(2):
                _remote(landed[j], send.at[2 + j], recv.at[2 + j], sibling).start()
            if not then_first:
                start_then()

        _for_my_shard(go)
        for j in range(2):
            _remote(_weight_region(bufs[0], 0, 0), send_in.at[base + j], recv_in.at[base + j], sibling).wait_send()

    views = [view] if then is None else [view, _in_hbm(then)]
    sems, views = _comm_call(name, views, [started], after, n_new, fn)
    return sems, views


def _gather_in_neighbours_end(view, relayed, after, name):
    def fn(bufs, pairs, *_):
        (send, recv), = pairs
        x, y, c = _place()
        for j in range(2):
            cp = _remote(_weight_region(bufs[0], 0, 0), send.at[2 + j], recv.at[2 + j], (x, y, 1 - c))
            cp.wait_recv()
            cp.wait_send()

    _, (view,) = _comm_call(name, [view], [relayed], after, 0, fn)
    return view


def _gather_in_diagonal(view, relayed, after, name, sibling_id):
    def fn(bufs, pairs, send, recv):
        (send_in, recv_in), = pairs
        x, y, c = _place()
        sibling = (x, y, 1 - c)
        any_quarter = _quarter(_weight_region(bufs[0], 0, 0), 0)
        for j in range(2):
            cp = _remote(any_quarter, send_in.at[j], recv_in.at[j], sibling)
            cp.wait_recv()
            cp.wait_send()

        def go(ss):
            _remote(_weight_region(bufs[0], ss ^ _FLIPS[2], c), send.at[0], recv.at[0], sibling).start()

        _for_my_shard(go)

    sems, (view,) = _comm_call(name, [view], [relayed], after, 1, fn, sibling_id=sibling_id)
    return sems, view


def _gather_in_diagonal_end(view, forwarded, after, name):
    def fn(bufs, pairs, *_):
        (send, recv), = pairs
        x, y, c = _place()
        cp = _remote(_weight_region(bufs[0], 0, 0), send.at[0], recv.at[0], (x, y, 1 - c))
        cp.wait_recv()
        cp.wait_send()

    _, (view,) = _comm_call(name, [view], [forwarded], after, 0, fn)
    return view


def _in_proj_shard(h1, wi, proj, shard_arr, name):
    s, d = h1.shape
    n = wi.shape[1]
    tn = 256
    blocks = n // (N_CHIPS * tn)
    given = [] if proj is None else [proj]

    def body(shard_ref, h_ref, w_ref, *rest):
        del shard_ref
        rest[-1][...] = jnp.dot(h_ref[...], w_ref[...], preferred_element_type=F32)

    grid_spec = pltpu.PrefetchScalarGridSpec(
        num_scalar_prefetch=1, grid=(blocks,),
        in_specs=[pl.BlockSpec((s, d), lambda j, shard_ref: (0, 0)),
                  pl.BlockSpec((d, tn), lambda j, shard_ref: (0, shard_ref[0] * blocks + j))] + [ANY] * len(given),
        out_specs=pl.BlockSpec((s, tn), lambda j, shard_ref: (0, shard_ref[0] * blocks + j)))
    return pl.pallas_call(
        body, name=name, grid_spec=grid_spec,
        out_shape=jax.ShapeDtypeStruct((s, n), F32),
        input_output_aliases={3: 0} if given else {},
        compiler_params=_params(("arbitrary",)),
    )(shard_arr, h1, wi, *given)


def _sibling_handshake():
    x, y, c = _place()
    barrier = pltpu.get_barrier_semaphore()
    pl.semaphore_signal(barrier, inc=1, device_id=(x, y, 1 - c), device_id_type=MESH)
    pl.semaphore_wait(barrier, 1)


def _chips_handshake():
    x, y, c = _place()
    barrier = pltpu.get_barrier_semaphore()
    for cx, cy in _other_chips(x, y):
        pl.semaphore_signal(barrier, inc=1, device_id=(cx, cy, c), device_id_type=MESH)
    pl.semaphore_wait(barrier, N_CHIPS - 1)


def _split_start(name, bufs, n_sems, copies, sibling_id=None, chips_id=None):
    n = len(bufs)
    assert sibling_id is None or chips_id is None

    def body(*refs):
        if sibling_id is not None:
            _sibling_handshake()
        if chips_id is not None:
            _chips_handshake()
        send_sems, recv_sems = refs[n:n + 2]
        for cp in copies(refs[n + 2:], send_sems, recv_sems):
            cp.start()

    outs = pl.pallas_call(
        body, name=name,
        in_specs=[_HBM] * n, out_specs=[_SEM, _SEM] + [_HBM] * n,
        out_shape=[pltpu.SemaphoreType.DMA((n_sems,)), pltpu.SemaphoreType.DMA((n_sems,))]
        + [pltpu.HBM(b.shape, b.dtype) for b in bufs],
        input_output_aliases={i: 2 + i for i in range(n)},
        compiler_params=pltpu.CompilerParams(has_side_effects=_EFFECT,
                                             collective_id=sibling_id if chips_id is None else chips_id),
    )(*[_in_hbm(b) for b in bufs])
    return outs[0], outs[1], list(outs[2:])


def _split_wait(name, bufs, send_sems, recv_sems, copies, after):
    n = len(bufs)
    after = tuple(after) if isinstance(after, (list, tuple)) else (after,)

    def body(*refs):
        send_ref, recv_ref = refs[n:n + 2]
        for cp in copies(refs[n + 2 + len(after):], send_ref, recv_ref):
            cp.wait()

    return list(pl.pallas_call(
        body, name=name,
        in_specs=[_HBM] * n + [_SEM, _SEM] + [ANY] * len(after), out_specs=[_HBM] * n,
        out_shape=[pltpu.HBM(b.shape, b.dtype) for b in bufs],
        input_output_aliases={i: i for i in range(n)},
        compiler_params=pltpu.CompilerParams(has_side_effects=_EFFECT),
    )(*bufs, send_sems, recv_sems, *after))


def _halves_copies(n_w):
    def copies(bufs, send_sems, recv_sems):
        x, y, c = _place()
        out = []
        for w in range(n_w):
            view, land = bufs[w], bufs[n_w + w]
            src = view.at[1 - c] if len(view.shape) == 3 else view.at[:, 1 - c]
            out.append(pltpu.make_async_remote_copy(
                src_ref=src, dst_ref=land, send_sem=send_sems.at[w], recv_sem=recv_sems.at[w],
                device_id=(x, y, 1 - c), device_id_type=MESH))
        return out
    return copies


def _pieces_copies(n_w):
    def copies(bufs, send_sems, recv_sems):
        x, y, c = _place()
        out = []
        for w in range(n_w):
            for j, (cx, cy) in enumerate(_other_chips(x, y)):
                out.append(pltpu.make_async_remote_copy(
                    src_ref=bufs[w].at[2 * cx + cy], dst_ref=bufs[n_w + w].at[j],
                    send_sem=send_sems.at[3 * w + j], recv_sem=recv_sems.at[3 * w + j],
                    device_id=(cx, cy, c), device_id_type=MESH))
        return out
    return copies


def _join_copies(n_w):
    def copies(bufs, send_sems, recv_sems):
        x, y, c = _place()
        return [pltpu.make_async_remote_copy(
            src_ref=bufs[w].at[c], dst_ref=bufs[w].at[c], send_sem=send_sems.at[w], recv_sem=recv_sems.at[w],
            device_id=(x, y, 1 - c), device_id_type=MESH) for w in range(n_w)]
    return copies


def _halves_landing(view):
    shape = view.shape[1:] if view.ndim == 3 else (N_CHIPS,) + view.shape[2:]
    return lax.empty(shape, BF16)


_SIBLING_IDS = {"halves_down": 1, "halves_gate_up": 2, "halves_out": 3, "halves_in": 4,
                "join_down": 5, "join_gate_up": 6, "join_out": 7, "join_in": 8,
                "diagonal_in": 9, "diagonal_gate": 10, "diagonal_up": 11, "forward_out": 12, "forward_down": 13,
                "pieces_down": 14, "pieces_gate_up": 15, "pieces_out": 16, "pieces_in": 17}


def _halves_start(tag, grads, column_sharded):
    views = [_weight_view(g, cs) for g, cs in zip(grads, column_sharded)]
    n = len(views)
    return _split_start("halves_start_" + tag, views + [_halves_landing(v) for v in views], n, _halves_copies(n),
                        sibling_id=_SIBLING_IDS["halves_" + tag])


def _halves_wait(tag, state, after):
    send_sems, recv_sems, bufs = state
    n = len(bufs) // 2
    bufs = _split_wait("halves_wait_" + tag, bufs, send_sems, recv_sems, _halves_copies(n), after)
    return bufs[:n], bufs[n:]


def _pieces_start(tag, pieces):
    n = len(pieces)
    landing = [lax.empty((3,) + p.shape[1:], BF16) for p in pieces]
    return _split_start("pieces_start_" + tag, list(pieces) + landing, 3 * n, _pieces_copies(n),
                        chips_id=_SIBLING_IDS["pieces_" + tag])


def _pieces_wait(tag, state, after):
    send_sems, recv_sems, bufs = state
    n = len(bufs) // 2
    bufs = _split_wait("pieces_wait_" + tag, bufs, send_sems, recv_sems, _pieces_copies(n), after)
    return bufs[:n], bufs[n:]


def _join_start(tag, shards):
    n = len(shards)
    return _split_start("join_start_" + tag, list(shards), n, _join_copies(n), sibling_id=_SIBLING_IDS["join_" + tag])


def _join_wait(tag, state, after):
    send_sems, recv_sems, bufs = state
    bufs = _split_wait("join_wait_" + tag, bufs, send_sems, recv_sems, _join_copies(len(bufs)), after)
    return [b.reshape(2 * b.shape[1], b.shape[2]) for b in bufs]


def _chip_sum_col(g3, sib, c_arr, name):
    _, hk, n = g3.shape
    cols = n // N_CHIPS
    tr = _row_tile(hk, cols * 2, limit=4 * 1024 * 1024)

    def body(c_ref, g_ref, s_ref, o_ref):
        del c_ref
        o_ref[...] = (g_ref[...].astype(F32) + s_ref[...].astype(F32)).astype(BF16)

    grid_spec = pltpu.PrefetchScalarGridSpec(
        num_scalar_prefetch=1, grid=(N_CHIPS, hk // tr),
        in_specs=[pl.BlockSpec((None, tr, cols), lambda p, r, c_ref: (c_ref[0], r, p)),
                  pl.BlockSpec((tr, cols), lambda p, r, c_ref: (r, p))],
        out_specs=pl.BlockSpec((None, tr, cols), lambda p, r, c_ref: (p, r, 0)))
    return pl.pallas_call(
        body, name=name, grid_spec=grid_spec,
        out_shape=jax.ShapeDtypeStruct((N_CHIPS, hk, cols), BF16),
        compiler_params=_params(("parallel", "parallel")),
    )(c_arr, g3, sib)


def _chip_sum_row(g4, sib, c_arr, name):
    _, _, hr, n = g4.shape
    tr = _row_tile(hr, n * 2, limit=4 * 1024 * 1024)

    def body(c_ref, g_ref, s_ref, o_ref):
        del c_ref
        o_ref[...] = (g_ref[...].astype(F32) + s_ref[...].astype(F32)).astype(BF16)

    grid_spec = pltpu.PrefetchScalarGridSpec(
        num_scalar_prefetch=1, grid=(N_CHIPS, hr // tr),
        in_specs=[pl.BlockSpec((None, None, tr, n), lambda p, r, c_ref: (p, c_ref[0], r, 0)),
                  pl.BlockSpec((None, tr, n), lambda p, r, c_ref: (p, r, 0))],
        out_specs=pl.BlockSpec((None, tr, n), lambda p, r, c_ref: (p, r, 0)))
    return pl.pallas_call(
        body, name=name, grid_spec=grid_spec,
        out_shape=jax.ShapeDtypeStruct((N_CHIPS, hr, n), BF16),
        compiler_params=_params(("parallel", "parallel")),
    )(c_arr, g4, sib)


def _sum_pieces(pieces, received, place_arr, name):
    _, r, n = pieces.shape
    tr = _row_tile(r, n * 4, limit=4 * 1024 * 1024)

    def body(p_ref, own_ref, r0_ref, r1_ref, r2_ref, o_ref):
        del p_ref
        acc = own_ref[...].astype(F32) + r0_ref[...].astype(F32)
        acc = acc + r1_ref[...].astype(F32)
        o_ref[...] = acc + r2_ref[...].astype(F32)

    def recv_spec(j):
        return pl.BlockSpec((None, tr, n), lambda i, p_ref: (j, i, 0))

    grid_spec = pltpu.PrefetchScalarGridSpec(
        num_scalar_prefetch=1, grid=(r // tr,),
        in_specs=[pl.BlockSpec((None, tr, n), lambda i, p_ref: (p_ref[0], i, 0)),
                  recv_spec(0), recv_spec(1), recv_spec(2)],
        out_specs=pl.BlockSpec((None, tr, n), lambda i, p_ref: (p_ref[1], i, 0)))
    return pl.pallas_call(
        body, name=name, grid_spec=grid_spec,
        out_shape=jax.ShapeDtypeStruct((2, r, n), F32),
        compiler_params=_params(("parallel",)),
    )(place_arr, pieces, received, received, received)


def _norm_weights_step(parts, w, m, v, after=()):
    rows, d = parts.shape
    after = tuple(after)

    def body(p_ref, w_ref, m_ref, v_ref, *rest):
        g_ref, d_ref, mo_ref, vo_ref, gathered, send_sems, recv_sems = rest[len(after):]
        x, y, c = _place()
        me = 4 * x + 2 * y + c
        gathered[me] = p_ref[...]
        copies = []
        for k in range(1, N_DEV):
            peer = (x ^ ((k >> 2) & 1), y ^ ((k >> 1) & 1), c ^ (k & 1))
            copies.append(pltpu.make_async_remote_copy(
                src_ref=p_ref, dst_ref=gathered.at[me], send_sem=send_sems.at[k - 1],
                recv_sem=recv_sems.at[k - 1], device_id=peer, device_id_type=MESH))
        for cp in copies:
            cp.start()
        for cp in copies:
            cp.wait()
        g = gathered[0]
        for k in range(1, N_DEV):
            g = g + gathered[k]
        delta, m_new, v_new = _adamw_math(w_ref[...], g, m_ref[...], v_ref[...])
        g_ref[...] = g
        d_ref[...] = delta
        mo_ref[...] = m_new
        vo_ref[...] = v_new

    vmem = pl.BlockSpec(memory_space=pltpu.VMEM)
    shp = jax.ShapeDtypeStruct((rows, d), F32)
    return pl.pallas_call(
        body, name="norm_weights_step",
        in_specs=[vmem] * 4 + [ANY] * len(after), out_specs=[vmem] * 4, out_shape=[shp] * 4,
        scratch_shapes=[pltpu.VMEM((N_DEV, rows, d), F32), pltpu.SemaphoreType.DMA((N_DEV - 1,)),
                        pltpu.SemaphoreType.DMA((N_DEV - 1,))],
        compiler_params=pltpu.CompilerParams(has_side_effects=True),
    )(parts, w, m, v, *after)


def kernel(x, norm_mix_w, w_in, w_out, norm_ffn_w, w_gate, w_up, w_down, norm_final_w, loss_target, m_norm_mix_w, m_w_in, m_w_out, m_norm_ffn_w, m_w_gate, m_w_up, m_w_down, m_norm_final_w, v_norm_mix_w, v_w_in, v_w_out, v_norm_ffn_w, v_w_gate, v_w_up, v_w_down, v_norm_final_w):
    s, d = x.shape[1], x.shape[2]
    xs = x.reshape(s, d)
    target = loss_target.reshape(s, d)
    big = {"w_in": (w_in, m_w_in, v_w_in), "w_out": (w_out, m_w_out, v_w_out),
           "w_gate": (w_gate, m_w_gate, v_w_gate), "w_up": (w_up, m_w_up, v_w_up),
           "w_down": (w_down, m_w_down, v_w_down)}
    big = {k: tuple(a.reshape(a.shape[1:]) for a in t) for k, t in big.items()}
    col_names, row_names = ("w_in", "w_gate", "w_up"), ("w_out", "w_down")
    n_in = N_CHIPS * big["w_in"][0].shape[1]
    ffn = N_CHIPS * big["w_gate"][0].shape[1]
    mix = ATTN_WIDTH + RET_WIDTH
    c_arr = lax.axis_index("c").astype(I32).reshape(1)
    shard_arr = (2 * lax.axis_index("x") + lax.axis_index("y")).astype(I32).reshape(1)
    place_arr = jnp.concatenate([shard_arr, c_arr])

    def cast(k, after=()):
        return _weight_view(_cast_into_full(big[k][0], shard_arr, k in col_names, "cast_" + k, after), k in col_names)

    started_in, v_in = _gather_in_start(cast("w_in"), "gather_in_start")

    sec = ATTN_WIDTH

    def section(p, rows):
        return pl.BlockSpec((None, rows, sec), lambda i, j, kk: (p, i, 0))

    h1 = _rms_fwd(xs, norm_mix_w, "rms_mix_fwd", after=[v_in])
    my_shard = shard_arr[0]
    shard_of = [jnp.bitwise_xor(my_shard, f).astype(I32).reshape(1) for f in (0,) + _FLIPS]
    proj = _in_proj_shard(h1, _weight_unview(v_in), None, shard_of[0], "in_proj_own")
    early_views = [cast(k, after=[proj]) for k in ("w_out", "w_gate")]
    v_up, v_down = [cast(k, after=[proj]) for k in ("w_up", "w_down")]
    relayed_in, (v_in,) = _gather_relay(v_in, started_in, 0, early_views + [v_up, v_down], "gather_in_relay")
    started_og, (v_out, v_gate) = _gather_out_gate_start(*early_views, [v_in], "gather_out_gate_start")
    v_in = _gather_in_neighbours_end(v_in, relayed_in, [v_out], "gather_in_neighbours_end")
    proj = _in_proj_shard(h1, _weight_unview(v_in), proj, shard_of[1], "in_proj_x")
    proj = _in_proj_shard(h1, _weight_unview(v_in), proj, shard_of[2], "in_proj_y")
    forwarded_in, v_in = _gather_in_diagonal(v_in, relayed_in, [proj], "gather_in_diagonal",
                                             _SIBLING_IDS["diagonal_in"])
    wi = _weight_unview(_gather_in_diagonal_end(v_in, forwarded_in, [proj], "gather_in_diagonal_end"))
    proj = _in_proj_shard(h1, wi, proj, shard_of[3], "in_proj_diagonal")
    fs_o, fr_o, v_out = _gather_forward([v_out], [0], *started_og, proj, "gather_forward_out",
                                        sibling_id=_SIBLING_IDS["forward_out"])
    mixed, attn_o, lse = _attn_fwd(proj, after=[v_out])
    relayed_g, (v_gate, v_up) = _gather_relay(v_gate, started_og, 3, [attn_o], "gather_gate_relay",
                                              then=v_up, then_peers=2, then_first=True)
    mixed, ret_raw = _ret_fwd(proj, mixed, after=[v_gate])
    wo, = _gather_end([v_out], fs_o, fr_o, ret_raw, "gather_end_out")
    x1, = _matmul("out_proj", "nn", [mixed, mixed], [wo, wo], [0, 0], s, d, sec, s // 2, 512, sec, [xs], [F32],
                  _epi_residual, b_koff=[0, 1], a_specs=[section(0, s // 2), section(1, s // 2)])
    h2 = _rms_fwd(x1, norm_ffn_w, "rms_ffn_fwd")
    relayed_u, (v_up, v_down) = _gather_relay(v_up, relayed_g, 6, [h2], "gather_up_relay",
                                              then=v_down, then_peers=3)
    v_gate = _gather_in_neighbours_end(v_gate, relayed_g, [v_up], "gather_gate_neighbours_end")
    forwarded_g, v_gate = _gather_in_diagonal(v_gate, relayed_g, [v_up], "gather_gate_diagonal",
                                              _SIBLING_IDS["diagonal_gate"])
    v_up = _gather_in_neighbours_end(v_up, relayed_u, [v_gate], "gather_up_neighbours_end")
    wg = _weight_unview(_gather_in_diagonal_end(v_gate, forwarded_g, [v_up], "gather_gate_diagonal_end"))
    forwarded_u, v_up = _gather_in_diagonal(v_up, relayed_u, [wg], "gather_up_diagonal",
                                            _SIBLING_IDS["diagonal_up"])
    wu = _weight_unview(_gather_in_diagonal_end(v_up, forwarded_u, [wg], "gather_up_diagonal_end"))
    gate, up, act = _matmul("gate_up", "nn", [h2, h2], [wg, wu], [0, 1], s, ffn, d, s, 512, d, [],
                            [BF16, BF16, BF16], _epi_swiglu, a_single_buffer=True)
    fs, fr, v_down = _gather_forward([v_down], [0], *relayed_u, act, "gather_forward_down", base=6,
                                     sibling_id=_SIBLING_IDS["forward_down"])
    wd, = _gather_end([v_down], fs, fr, act, "gather_end_down")
    x2, = _matmul("down_proj", "nn", [act], [wd], [0], s, d, ffn, s // 2, 512, ffn, [x1], [F32],
                  _epi_residual)
    loss_row, dx2, dx2b, dwf = _final_norm_loss(x2, norm_final_w.reshape(1, d), target, "final_norm_loss")

    names = col_names + row_names
    grads, new = {}, {}

    def chip_sums(tag_names, views, sibs):
        return [(_chip_sum_col if k in col_names else _chip_sum_row)(v, sb, c_arr, "chip_sum_" + k)
                for k, v, sb in zip(tag_names, views, sibs)]

    def piece_sums(tag_names, pieces, received):
        return [_sum_pieces(p, r, place_arr, "sum_pieces_" + k) for k, p, r in zip(tag_names, pieces, received)]

    def update(k):
        new[k] = _adamw(big[k][0], grads[k], big[k][1], big[k][2], "adamw_" + k)

    dgate, dup = _matmul("d_act", "nt", [dx2b], [wd], [0], s, ffn, d, s, 512, d, [gate, up],
                         [BF16, BF16], _epi_swiglu_bwd, a_single_buffer=True)
    g_wd, = _matmul("g_w_down", "tn", [act], [dx2b], [0], ffn, d, s, 512, d, s, [], [BF16], _epi_plain)
    halves_d = _halves_start("down", [g_wd], [False])
    dh2, = _matmul("d_h2", "nt", [dgate, dup], [wg, wu], [0, 0], s, d, ffn, s // 2, 256, ffn, [], [F32],
                   _epi_plain, after=halves_d[2][-1:], a_single_buffer=True)
    pieces_d = _pieces_start("down", chip_sums(["w_down"], *_halves_wait("down", halves_d, dh2)))
    g_wg, g_wu = _matmul("g_w_gate_up", "tn", [h2, h2], [dgate, dup], [0, 1], d, ffn, s, 1024, 512, s, [],
                         [BF16, BF16], _epi_two, after=pieces_d[2][-1:])
    halves_gu = _halves_start("gate_up", [g_wg, g_wu], [True, True])
    dx1, dx1b, dw_ffn = _rms_bwd(x1, norm_ffn_w, dh2, dx2, "rms_ffn_bwd", after=halves_gu[2][-1:])

    dmixed, = _matmul("d_mixed", "nt", [dx1b], [wo], [0], s, mix, d, s // 2, 512, d, [], [F32], _epi_plain)
    pieces_gu = _pieces_start("gate_up", chip_sums(["w_gate", "w_up"], *_halves_wait("gate_up", halves_gu, dmixed)))
    per = sec // 512
    g_wo, = _matmul("g_w_out", "tn", [mixed], [dx1b], [0], mix, d, s, 512, d, s, [], [BF16], _epi_plain,
                    after=pieces_gu[2][-1:],
                    a_specs=[pl.BlockSpec((None, s, 512), lambda i, j, kk: (i // per, 0, i % per))])
    halves_o = _halves_start("out", [g_wo], [False])
    dsec = _attn_bwd(proj, attn_o, lse, dmixed, after=halves_o[2][-1:])
    pieces_o = _pieces_start("out", chip_sums(["w_out"], *_halves_wait("out", halves_o, dsec)))
    dsec = _ret_bwd(proj, ret_raw, dmixed, dsec, after=pieces_o[2][-1:])
    where = [0, 1, 2, 4, 5, 6, 7]
    n_sec = len(where)
    g_wi, = _matmul("g_w_in", "tn", [h1], [dsec], [0], d, n_in, s, 1024, sec, s, [], [BF16], _epi_plain,
                    b_specs=[pl.BlockSpec((None, s, sec), lambda i, j, kk: (j + (j >= 3).astype(I32), 0, 0))])
    halves_i = _halves_start("in", [g_wi], [True])
    dh1, = _matmul("d_h1", "nt", [dsec] * n_sec, [wi] * n_sec, [0] * n_sec, s, d, sec, s // 2, 256, sec, [], [F32],
                   _epi_plain, b_koff=list(range(n_sec)), after=halves_i[2][-1:],
                   a_specs=[section(p, s // 2) for p in where])
    pieces_i = _pieces_start("in", chip_sums(["w_in"], *_halves_wait("in", halves_i, dh1)))
    grad_x, _, dw_mix = _rms_bwd(xs, norm_mix_w, dh1, dx1, "rms_mix_bwd", after=pieces_i[2][-1:])

    def rows8(*vs):
        return jnp.concatenate([v.reshape(1, d) for v in vs] + [jnp.zeros((8 - len(vs), d), F32)], axis=0)

    join_d = _join_start("down", piece_sums(["w_down"], *_pieces_wait("down", pieces_d, grad_x)))
    join_gu = _join_start("gate_up", piece_sums(["w_gate", "w_up"], *_pieces_wait("gate_up", pieces_gu, join_d[2][0])))
    join_o = _join_start("out", piece_sums(["w_out"], *_pieces_wait("out", pieces_o, join_gu[2][0])))
    grads["w_down"], = _join_wait("down", join_d, join_o[2][0])
    update("w_down")
    grads["w_gate"], grads["w_up"] = _join_wait("gate_up", join_gu, new["w_down"][0])
    update("w_gate")
    update("w_up")
    grads["w_out"], = _join_wait("out", join_o, new["w_up"][0])
    update("w_out")
    others_done = [new[k][0] for k in ("w_down", "w_gate", "w_up", "w_out")]
    join_i = _join_start("in", piece_sums(["w_in"], *_pieces_wait("in", pieces_i, others_done)))
    ng, nd, nm, nv = _norm_weights_step(
        rows8(dw_mix, dw_ffn, dwf, jnp.broadcast_to(loss_row[:, :1], (1, d))),
        rows8(norm_mix_w, norm_ffn_w, norm_final_w),
        rows8(m_norm_mix_w, m_norm_ffn_w, m_norm_final_w), rows8(v_norm_mix_w, v_norm_ffn_w, v_norm_final_w),
        after=join_i[2][:1])
    grads["w_in"], = _join_wait("in", join_i, ng)
    update("w_in")

    loss = ng[3, 0]

    def pack(small, per_weight):
        lead = lambda a: a.reshape((1,) + a.shape)
        return (small[0:1], lead(per_weight["w_in"]), lead(per_weight["w_out"]), small[1:2],
                lead(per_weight["w_gate"]), lead(per_weight["w_up"]), lead(per_weight["w_down"]), small[2])

    return (loss, grad_x.reshape(1, s, d),
            *pack(ng, {k: new[k][3] for k in names}),
            *pack(nd, {k: new[k][0] for k in names}),
            *pack(nm, {k: new[k][1] for k in names}),
            *pack(nv, {k: new[k][2] for k in names}))
```

```python
import functools
import math

import jax
import jax.numpy as jnp
from jax import lax
from jax.experimental import pallas as pl
from jax.experimental.pallas import tpu as pltpu

F32 = jnp.float32
BF16 = jnp.bfloat16
I32 = jnp.int32
MESH = pl.DeviceIdType.MESH
ANY = pl.BlockSpec(memory_space=pl.ANY)

ATTN_HEADS = 8
ATTN_HEAD_DIM = 128
RET_HEADS = 4
RET_HEAD_DIM = 256
ATTN_WIDTH = ATTN_HEADS * ATTN_HEAD_DIM
RET_WIDTH = RET_HEADS * RET_HEAD_DIM
DILATED_PATTERNS = ((128, 1), (512, 4), (2048, 16))
NORM_EPS = 1e-6
ADAM_LR = 0.001
ADAM_B1 = 0.9
ADAM_B2 = 0.999
ADAM_EPS = 1e-08
ADAM_WD = 0.01
ADAM_STEP = 10

N_CHIPS = 4
N_DEV = 8
NEG_BIG = -1e30
SEQ_TILE = 512
ATTN_FWD_HEADS_PER_STEP = 2
ATTN_HEADS_PER_STEP = 1
VMEM_LIMIT_BYTES = 56 * 1024 * 1024


def _params(semantics=None, vmem=VMEM_LIMIT_BYTES):
    return pltpu.CompilerParams(dimension_semantics=semantics, vmem_limit_bytes=vmem)


def _row_tile(rows, row_bytes, limit=2 * 1024 * 1024, mult=16):
    best = None
    for t in range(mult, rows + 1, mult):
        if rows % t == 0 and t * row_bytes <= limit:
            best = t
    assert best is not None, (rows, row_bytes)
    return best


def _sigmoid(x):
    return 1.0 / (1.0 + jnp.exp(-x))


def _select_by_index(idx, values):
    out = jnp.float32(values[-1])
    for i in range(len(values) - 2, -1, -1):
        out = jnp.where(idx == i, jnp.float32(values[i]), out)
    return out


def _place():
    x, y, c = lax.axis_index("x"), lax.axis_index("y"), lax.axis_index("c")
    return x, y, c


def _cast_into_full(w, shard_arr, column_sharded, name, after=()):
    after = tuple(after)
    rows, cols = w.shape
    tr = _row_tile(rows, cols * 4)
    steps = rows // tr
    if column_sharded:
        out_shape, out_map = (rows, N_CHIPS * cols), (lambda i, s_ref: (i, s_ref[0]))
    else:
        out_shape, out_map = (N_CHIPS * rows, cols), (lambda i, s_ref: (s_ref[0] * steps + i, 0))

    def body(s_ref, w_ref, *rest):
        del s_ref
        rest[-1][...] = w_ref[...].astype(BF16)

    grid_spec = pltpu.PrefetchScalarGridSpec(
        num_scalar_prefetch=1, grid=(steps,),
        in_specs=[pl.BlockSpec((tr, cols), lambda i, s_ref: (i, 0))] + [ANY] * len(after),
        out_specs=pl.BlockSpec((tr, cols), out_map))
    return pl.pallas_call(
        body, name=name, grid_spec=grid_spec,
        out_shape=jax.ShapeDtypeStruct(out_shape, BF16),
        compiler_params=_params(("parallel",)),
    )(shard_arr, w, *after)


def _rms_fwd(x, w, name, after=()):
    rows, d = x.shape
    tr = 256
    after = tuple(after)

    def body(x_ref, w_ref, *rest):
        xv = x_ref[...]
        r = lax.rsqrt(jnp.mean(xv * xv, axis=-1, keepdims=True) + NORM_EPS)
        rest[-1][...] = (xv * r * w_ref[...]).astype(BF16)

    return pl.pallas_call(
        body, name=name, grid=(rows // tr,),
        in_specs=[pl.BlockSpec((tr, d), lambda i: (i, 0)), pl.BlockSpec((1, d), lambda i: (0, 0))]
        + [ANY] * len(after),
        out_specs=pl.BlockSpec((tr, d), lambda i: (i, 0)),
        out_shape=jax.ShapeDtypeStruct((rows, d), BF16),
        compiler_params=_params(("parallel",)),
    )(x, w, *after)


def _rms_bwd(x, w, dh, dres, name, after=()):
    rows, d = x.shape
    tr = 256
    after = tuple(after)

    def body(x_ref, w_ref, dh_ref, dres_ref, *rest):
        dx_ref, dxb_ref, dw_ref = rest[len(after):]
        xv = x_ref[...]
        r = lax.rsqrt(jnp.mean(xv * xv, axis=-1, keepdims=True) + NORM_EPS)
        xhat = xv * r
        dy = dh_ref[...]
        dxhat = dy * w_ref[...]
        dx = dres_ref[...] + r * (dxhat - xhat * jnp.mean(dxhat * xhat, axis=-1, keepdims=True))
        dx_ref[...] = dx
        dxb_ref[...] = dx.astype(BF16)
        part = jnp.sum(dy * xhat, axis=0, keepdims=True)

        @pl.when(pl.program_id(0) == 0)
        def _():
            dw_ref[...] = part

        @pl.when(pl.program_id(0) != 0)
        def _():
            dw_ref[...] += part

    row = pl.BlockSpec((tr, d), lambda i: (i, 0))
    vec = pl.BlockSpec((1, d), lambda i: (0, 0))
    return pl.pallas_call(
        body, name=name, grid=(rows // tr,),
        in_specs=[row, vec, row, row] + [ANY] * len(after),
        out_specs=[row, row, vec],
        out_shape=[jax.ShapeDtypeStruct((rows, d), F32), jax.ShapeDtypeStruct((rows, d), BF16),
                   jax.ShapeDtypeStruct((1, d), F32)],
        compiler_params=_params(("arbitrary",)),
    )(x, w, dh, dres, *after)


def _final_norm_loss(x2, w, target, name):
    rows, d = x2.shape
    tr = 256

    def body(x_ref, w_ref, t_ref, loss_ref, dx_ref, dxb_ref, dw_ref):
        xv = x_ref[...]
        wv = w_ref[...]
        r = lax.rsqrt(jnp.mean(xv * xv, axis=-1, keepdims=True) + NORM_EPS)
        xhat = xv * r
        err = xhat * wv - t_ref[...]
        part_loss = 0.5 * jnp.sum(jnp.mean(err * err, axis=-1, keepdims=True), axis=0, keepdims=True)
        dy = err * (1.0 / d)
        dxhat = dy * wv
        dx = r * (dxhat - xhat * jnp.mean(dxhat * xhat, axis=-1, keepdims=True))
        dx_ref[...] = dx
        dxb_ref[...] = dx.astype(BF16)
        part_dw = jnp.sum(dy * xhat, axis=0, keepdims=True)
        part_loss = jnp.broadcast_to(part_loss, (1, 128))

        @pl.when(pl.program_id(0) == 0)
        def _():
            dw_ref[...] = part_dw
            loss_ref[...] = part_loss

        @pl.when(pl.program_id(0) != 0)
        def _():
            dw_ref[...] += part_dw
            loss_ref[...] += part_loss

    row = pl.BlockSpec((tr, d), lambda i: (i, 0))
    vec = pl.BlockSpec((1, d), lambda i: (0, 0))
    return pl.pallas_call(
        body, name=name, grid=(rows // tr,),
        in_specs=[row, vec, row],
        out_specs=[pl.BlockSpec((1, 128), lambda i: (0, 0)), row, row, vec],
        out_shape=[jax.ShapeDtypeStruct((1, 128), F32), jax.ShapeDtypeStruct((rows, d), F32),
                   jax.ShapeDtypeStruct((rows, d), BF16), jax.ShapeDtypeStruct((1, d), F32)],
        compiler_params=_params(("arbitrary",)),
    )(x2, w, target)


def _adamw_math(w, g, m, v):
    m = ADAM_B1 * m + (1.0 - ADAM_B1) * g
    v = ADAM_B2 * v + (1.0 - ADAM_B2) * (g * g)
    m_hat = m / (1.0 - ADAM_B1 ** ADAM_STEP)
    v_hat = v / (1.0 - ADAM_B2 ** ADAM_STEP)
    delta = -ADAM_LR * (m_hat / (jnp.sqrt(v_hat) + ADAM_EPS) + ADAM_WD * w)
    return delta, m, v


def _adamw(w, g, m, v, name):
    rows, cols = w.shape
    tr = _row_tile(rows, cols * 4)

    def body(w_ref, g_ref, m_ref, v_ref, d_ref, mo_ref, vo_ref, go_ref):
        g = g_ref[...]
        delta, m_new, v_new = _adamw_math(w_ref[...], g, m_ref[...], v_ref[...])
        d_ref[...] = delta
        mo_ref[...] = m_new
        vo_ref[...] = v_new
        go_ref[...] = g

    blk = pl.BlockSpec((tr, cols), lambda i: (i, 0))
    shp = jax.ShapeDtypeStruct((rows, cols), F32)
    return pl.pallas_call(
        body, name=name, grid=(rows // tr,),
        in_specs=[blk] * 4, out_specs=[blk] * 4, out_shape=[shp] * 4,
        compiler_params=_params(("parallel",)),
    )(w, g, m, v)


_DOT_DIMS = {"nn": ((1,), (0,)), "nt": ((1,), (1,)), "tn": ((0,), (0,))}


def _matmul(name, mode, a_list, b_list, acc_of, m, n, k, tm, tn, tk, extras, out_dtypes, epilogue,
            a_koff=None, b_koff=None, after=(), a_specs=None, b_specs=None, a_single_buffer=False):
    after = tuple(after)
    assert m % tm == 0 and n % tn == 0 and k % tk == 0, (name, m, n, k, tm, tn, tk)
    nk = k // tk
    n_acc = max(acc_of) + 1
    n_pairs = len(a_list)
    a_koff = a_koff or [0] * n_pairs
    b_koff = b_koff or [0] * n_pairs
    dims = (_DOT_DIMS[mode], ((), ()))
    n_ext, n_out = len(extras), len(out_dtypes)

    def body(*refs):
        a_refs = refs[:n_pairs]
        b_refs = refs[n_pairs:2 * n_pairs]
        e_refs = refs[2 * n_pairs:2 * n_pairs + n_ext]
        first_out = 2 * n_pairs + n_ext + len(after)
        o_refs = refs[first_out:first_out + n_out]
        acc_refs = refs[first_out + n_out:]

        parts = [None] * n_acc
        for p in range(n_pairs):
            d = lax.dot_general(a_refs[p][...], b_refs[p][...], dims, preferred_element_type=F32)
            parts[acc_of[p]] = d if parts[acc_of[p]] is None else parts[acc_of[p]] + d

        def finish(accs):
            outs = epilogue(accs, [e[...] for e in e_refs])
            for o_ref, o in zip(o_refs, outs):
                o_ref[...] = o.astype(o_ref.dtype)

        if nk == 1:
            finish(parts)
        else:
            kk = pl.program_id(2)

            @pl.when(kk == 0)
            def _():
                for acc_ref, part in zip(acc_refs, parts):
                    acc_ref[...] = part

            @pl.when(kk != 0)
            def _():
                for acc_ref, part in zip(acc_refs, parts):
                    acc_ref[...] += part

            @pl.when(kk == nk - 1)
            def _():
                finish([acc_ref[...] for acc_ref in acc_refs])

    def a_spec(off):
        mode_a = pl.Buffered(1) if a_single_buffer else None
        if mode == "tn":
            return pl.BlockSpec((tk, tm), lambda i, j, kk: (kk + off, i), pipeline_mode=mode_a)
        return pl.BlockSpec((tm, tk), lambda i, j, kk: (i, kk + off), pipeline_mode=mode_a)

    def b_spec(off):
        if mode == "nt":
            return pl.BlockSpec((tn, tk), lambda i, j, kk: (j, kk + off))
        return pl.BlockSpec((tk, tn), lambda i, j, kk: (kk + off, j))

    tile = pl.BlockSpec((tm, tn), lambda i, j, kk: (i, j))
    scratch = [pltpu.VMEM((tm, tn), F32) for _ in range(n_acc)] if nk > 1 else []
    return pl.pallas_call(
        body, name=name, grid=(m // tm, n // tn, nk),
        in_specs=(a_specs or [a_spec(o) for o in a_koff]) + (b_specs or [b_spec(o) for o in b_koff])
        + [tile] * n_ext + [ANY] * len(after),
        out_specs=[tile] * n_out,
        out_shape=[jax.ShapeDtypeStruct((m, n), dt) for dt in out_dtypes],
        scratch_shapes=scratch,
        compiler_params=_params(("parallel", "parallel", "arbitrary")),
    )(*a_list, *b_list, *extras, *after)


def _epi_plain(accs, extras):
    return (accs[0],)


def _epi_residual(accs, extras):
    return (accs[0] + extras[0],)


def _epi_two(accs, extras):
    return accs[0], accs[1]


def _epi_swiglu(accs, extras):
    g, u = accs
    return g, u, g * _sigmoid(g) * u


def _epi_swiglu_bwd(accs, extras):
    da = accs[0]
    g, u = (e.astype(F32) for e in extras)
    sg = _sigmoid(g)
    dg = da * u * sg * (1.0 + g * (1.0 - sg))
    du = da * g * sg
    return dg, du


_NT_DIMS = (((1,), (1,)), ((), ()))
_TN_DIMS = (((0,), (0,)), ((), ()))


def _tile_delta(tq, tk):
    return lax.broadcasted_iota(I32, (tq, tk), 0) - lax.broadcasted_iota(I32, (tq, tk), 1)


def _attn_log_count(delta):
    count = jnp.zeros(delta.shape, I32)
    for window, dilation in DILATED_PATTERNS:
        hit = ((delta & (dilation - 1)) == 0) & (delta <= window)
        count = count + jnp.where(hit, 1, 0)
    valid = (delta >= 0) & (count > 0)
    logm = jnp.where(count == 3, math.log(3.0), jnp.where(count == 2, math.log(2.0), 0.0))
    return jnp.where(valid, logm, NEG_BIG)


def _fill_attn_log_count(tab_ref):
    nb, t, _ = tab_ref.shape
    base = _tile_delta(t, t)
    for b in range(nb):
        tab_ref[b] = _attn_log_count(base + b * t)


def _fill_attn_bias(tab_ref, log_count_ref, slope):
    nb, t, _ = tab_ref.shape
    dist = _tile_delta(t, t).astype(F32)
    for b in range(nb):
        tab_ref[b] = log_count_ref[b] - slope * (dist + float(b * t))


def _fill_ret_decay(tab_ref, log_gamma):
    nb, t, _ = tab_ref.shape
    base = _tile_delta(t, t)
    for b in range(nb):
        tab_ref[b] = _ret_decay(base + b * t, log_gamma)


def _alibi_slopes():
    return [2.0 ** (-8.0 * (h + 1) / ATTN_HEADS) for h in range(ATTN_HEADS)]


def _attn_fwd(proj, after=()):
    s = proj.shape[0]
    t = SEQ_TILE
    hd = ATTN_HEAD_DIM
    hp = ATTN_FWD_HEADS_PER_STEP
    ng = ATTN_HEADS // hp
    w = hp * hd
    scale = 1.0 / math.sqrt(hd)
    slopes = _alibi_slopes()

    def body(q_ref, k_ref, v_ref, *rest):
        mix_ref, o_ref, lse_ref, kb, vb, bias_tab, log_count_tab = rest[len(after):]
        g = pl.program_id(0)
        i = pl.program_id(1)

        @pl.when((g == 0) & (i == 0))
        def _():
            _fill_attn_log_count(log_count_tab)

        @pl.when(i == 0)
        def _():
            kb[...] = k_ref[...].astype(BF16)
            vb[...] = v_ref[...].astype(BF16)
            for u in range(hp):
                _fill_attn_bias(bias_tab.at[u], log_count_tab, _select_by_index(g * hp + u, slopes))

        qs = [q_ref[:, u * hd:(u + 1) * hd].astype(BF16) for u in range(hp)]

        def step(j, carry):
            rows = pl.ds(pl.multiple_of(j * t, t), t)
            out = []
            for u in range(hp):
                m_i, l_i, acc = carry[u]
                lanes = slice(u * hd, (u + 1) * hd)
                sc = lax.dot_general(qs[u], kb[rows, lanes], _NT_DIMS, preferred_element_type=F32) * scale
                sc = sc + bias_tab[u, i - j]
                m_new = jnp.maximum(m_i, jnp.max(sc, axis=-1, keepdims=True))
                p = jnp.exp(sc - m_new)
                alpha = jnp.exp(m_i - m_new)
                l_new = alpha * l_i + jnp.sum(p, axis=-1, keepdims=True)
                acc = alpha * acc + jnp.dot(p.astype(BF16), vb[rows, lanes], preferred_element_type=F32)
                out.append((m_new, l_new, acc))
            return tuple(out)

        init = (jnp.full((t, 1), NEG_BIG, F32), jnp.zeros((t, 1), F32), jnp.zeros((t, hd), F32))
        final = lax.fori_loop(0, i + 1, step, (init,) * hp)
        for u in range(hp):
            m_i, l_i, acc = final[u]
            lanes = slice(u * hd, (u + 1) * hd)
            out = acc / l_i
            o_ref[:, lanes] = out
            mix_ref[:, lanes] = out.astype(BF16)
            lse_ref[:, lanes] = jnp.broadcast_to(m_i + jnp.log(l_i), (t, hd))

    return pl.pallas_call(
        body, name="attn_fwd", grid=(ng, s // t),
        in_specs=[pl.BlockSpec((t, w), lambda g, i: (i, g)),
                  pl.BlockSpec((s, w), lambda g, i: (0, ng + g)),
                  pl.BlockSpec((s, w), lambda g, i: (0, 2 * ng + g))] + [ANY] * len(after),
        out_specs=[pl.BlockSpec((None, t, w), lambda g, i: (0, i, g))] + [pl.BlockSpec((t, w), lambda g, i: (i, g))] * 2,
        out_shape=[jax.ShapeDtypeStruct((2, s, ATTN_WIDTH), BF16),
                   jax.ShapeDtypeStruct((s, ATTN_WIDTH), F32),
                   jax.ShapeDtypeStruct((s, ATTN_WIDTH), F32)],
        scratch_shapes=[pltpu.VMEM((s, w), BF16), pltpu.VMEM((s, w), BF16), pltpu.VMEM((hp, s // t, t, t), F32),
                        pltpu.VMEM((s // t, t, t), F32)],
        compiler_params=_params(("arbitrary", "arbitrary")),
    )(proj, proj, proj, *after)


def _attn_bwd(proj, attn_out, lse, dmixed, after=()):
    after = tuple(after)
    s = proj.shape[0]
    t = SEQ_TILE
    nt = s // t
    hd = ATTN_HEAD_DIM
    hp = ATTN_HEADS_PER_STEP
    ng = ATTN_HEADS // hp
    w = hp * hd
    scale = 1.0 / math.sqrt(hd)
    slopes = _alibi_slopes()

    def body(q_ref, k_ref, v_ref, o_ref, lse_ref, do_ref, *rest):
        dsec_ref, qb, kb, vb, dob, dsum, dq_acc, bias_tab, log_count_tab = rest[len(after):]
        g = pl.program_id(0)

        @pl.when(g == 0)
        def _():
            _fill_attn_log_count(log_count_tab)

        qb[...] = q_ref[...].astype(BF16)
        kb[...] = k_ref[...].astype(BF16)
        vb[...] = v_ref[...].astype(BF16)
        dob[...] = do_ref[...].astype(BF16)
        for u in range(hp):
            lanes = slice(u * hd, (u + 1) * hd)
            _fill_attn_bias(bias_tab.at[u], log_count_tab, _select_by_index(g * hp + u, slopes))
            rowsum = jnp.sum(do_ref[:, lanes] * o_ref[:, lanes], axis=-1, keepdims=True)
            dsum[:, lanes] = jnp.broadcast_to(rowsum, (s, hd))
        dq_acc[...] = jnp.zeros((s, w), F32)

        def over_keys(j, _):
            krows = pl.ds(pl.multiple_of(j * t, t), t)

            def over_queries(i, carry):
                qrows = pl.ds(pl.multiple_of(i * t, t), t)
                out = []
                for u in range(hp):
                    dk, dv = carry[u]
                    lanes = slice(u * hd, (u + 1) * hd)
                    qi, doi = qb[qrows, lanes], dob[qrows, lanes]
                    kj, vj = kb[krows, lanes], vb[krows, lanes]
                    lse_i = lse_ref[qrows, lanes][:, :1]
                    dsum_i = dsum[qrows, lanes][:, :1]
                    sc = lax.dot_general(qi, kj, _NT_DIMS, preferred_element_type=F32) * scale
                    p = jnp.exp(sc + bias_tab[u, i - j] - lse_i)
                    dp = lax.dot_general(doi, vj, _NT_DIMS, preferred_element_type=F32)
                    ds = (p * (dp - dsum_i)).astype(BF16)
                    dv = dv + lax.dot_general(p.astype(BF16), doi, _TN_DIMS, preferred_element_type=F32)
                    dk = dk + lax.dot_general(ds, qi, _TN_DIMS, preferred_element_type=F32)
                    dq_acc[qrows, lanes] += jnp.dot(ds, kj, preferred_element_type=F32)
                    out.append((dk, dv))
                return tuple(out)

            zero = jnp.zeros((t, hd), F32)
            final = lax.fori_loop(j, nt, over_queries, ((zero, zero),) * hp)
            for u in range(hp):
                lanes = slice(u * hd, (u + 1) * hd)
                dsec_ref[1, krows, lanes] = (final[u][0] * scale).astype(BF16)
                dsec_ref[2, krows, lanes] = final[u][1].astype(BF16)
            return 0

        lax.fori_loop(0, nt, over_keys, 0)
        dsec_ref[0] = (dq_acc[...] * scale).astype(BF16)

    def col(off):
        return pl.BlockSpec((s, w), lambda g: (0, off + g))

    return pl.pallas_call(
        body, name="attn_bwd", grid=(ng,),
        in_specs=[col(0), col(ng), col(2 * ng), col(0), col(0), col(0)] + [ANY] * len(after),
        out_specs=pl.BlockSpec((4, s, w), lambda g: (0, 0, g)),
        out_shape=jax.ShapeDtypeStruct((8, s, ATTN_WIDTH), BF16),
        scratch_shapes=[pltpu.VMEM((s, w), BF16)] * 4 + [pltpu.VMEM((s, w), F32)] * 2
        + [pltpu.VMEM((hp, nt, t, t), F32), pltpu.VMEM((nt, t, t), F32)],
        compiler_params=_params(("arbitrary",)),
    )(proj, proj, proj, attn_out, lse, dmixed, *after)


def _ret_log_gammas():
    return [math.log(1.0 - 2.0 ** (-5.0 - h)) for h in range(RET_HEADS)]


def _ret_decay(delta, log_gamma):
    dec = jnp.exp(delta.astype(F32) * log_gamma) * (1.0 / math.sqrt(RET_HEAD_DIM))
    return jnp.where(delta >= 0, dec, 0.0)


def _ret_fwd(proj, mixed, after=()):
    after = tuple(after)
    s = proj.shape[0]
    t = SEQ_TILE
    hd = RET_HEAD_DIM
    nh = RET_HEADS
    log_gammas = _ret_log_gammas()
    c0 = 3 * ATTN_WIDTH // hd

    def body(q_ref, k_ref, v_ref, g_ref, *rest):
        mix_ref, raw_ref, kb, vb, decay_tab = rest[1 + len(after):]
        h = pl.program_id(0)
        i = pl.program_id(1)

        @pl.when(i == 0)
        def _():
            kb[...] = k_ref[...].astype(BF16)
            vb[...] = v_ref[...].astype(BF16)
            _fill_ret_decay(decay_tab, _select_by_index(h, log_gammas))

        q = q_ref[...].astype(BF16)

        def step(j, acc):
            rows = pl.ds(pl.multiple_of(j * t, t), t)
            sc = lax.dot_general(q, kb[rows, :], _NT_DIMS, preferred_element_type=F32) * decay_tab[i - j]
            return acc + jnp.dot(sc.astype(BF16), vb[rows, :], preferred_element_type=F32)

        ret = lax.fori_loop(0, i + 1, step, jnp.zeros((t, hd), F32))
        raw_ref[...] = ret
        r = lax.rsqrt(jnp.mean(ret * ret, axis=-1, keepdims=True) + NORM_EPS)
        g = g_ref[...]
        mix_ref[...] = (g * _sigmoid(g) * (ret * r)).astype(BF16)

    return pl.pallas_call(
        body, name="ret_fwd", grid=(nh, s // t),
        in_specs=[pl.BlockSpec((t, hd), lambda h, i: (i, c0 + h)),
                  pl.BlockSpec((s, hd), lambda h, i: (0, c0 + nh + h)),
                  pl.BlockSpec((s, hd), lambda h, i: (0, c0 + 2 * nh + h)),
                  pl.BlockSpec((t, hd), lambda h, i: (i, c0 + 3 * nh + h))] + [ANY] * (1 + len(after)),
        out_specs=[pl.BlockSpec((None, t, hd), lambda h, i: (1, i, h)), pl.BlockSpec((t, hd), lambda h, i: (i, h))],
        out_shape=[jax.ShapeDtypeStruct(mixed.shape, BF16), jax.ShapeDtypeStruct((s, RET_WIDTH), F32)],
        input_output_aliases={4: 0},
        scratch_shapes=[pltpu.VMEM((s, hd), BF16), pltpu.VMEM((s, hd), BF16), pltpu.VMEM((s // t, t, t), F32)],
        compiler_params=_params(("arbitrary", "arbitrary")),
    )(proj, proj, proj, proj, mixed, *after)


def _ret_bwd(proj, ret_raw, dmixed, dsec, after=()):
    after = tuple(after)
    s = proj.shape[0]
    t = SEQ_TILE
    nt = s // t
    hd = RET_HEAD_DIM
    nh = RET_HEADS
    log_gammas = _ret_log_gammas()
    c0 = 3 * ATTN_WIDTH // hd
    mixed_blocks = ATTN_WIDTH // hd

    def body(q_ref, k_ref, v_ref, g_ref, raw_ref, dmix_ref, *rest):
        dsec_ref, qb, kb, vb, dretb, dq_acc, decay_tab = rest[1 + len(after):]
        h = pl.program_id(0)
        _fill_ret_decay(decay_tab, _select_by_index(h, log_gammas))
        qb[...] = q_ref[...].astype(BF16)
        kb[...] = k_ref[...].astype(BF16)
        vb[...] = v_ref[...].astype(BF16)
        ret = raw_ref[...]
        r = lax.rsqrt(jnp.mean(ret * ret, axis=-1, keepdims=True) + NORM_EPS)
        normed = ret * r
        g = g_ref[...]
        sg = _sigmoid(g)
        dout = dmix_ref[...]
        dsec_ref[3] = (dout * normed * sg * (1.0 + g * (1.0 - sg))).astype(BF16)
        dn = dout * g * sg
        dret = r * (dn - normed * jnp.mean(dn * normed, axis=-1, keepdims=True))
        dretb[...] = dret.astype(BF16)
        dq_acc[...] = jnp.zeros((s, hd), F32)

        def over_keys(j, _):
            krows = pl.ds(pl.multiple_of(j * t, t), t)
            kj = kb[krows, :]
            vj = vb[krows, :]

            def over_queries(i, carry):
                dk, dv = carry
                qrows = pl.ds(pl.multiple_of(i * t, t), t)
                qi = qb[qrows, :]
                doi = dretb[qrows, :]
                dec = decay_tab[i - j]
                a = (lax.dot_general(qi, kj, _NT_DIMS, preferred_element_type=F32) * dec).astype(BF16)
                da = (lax.dot_general(doi, vj, _NT_DIMS, preferred_element_type=F32) * dec).astype(BF16)
                dv = dv + lax.dot_general(a, doi, _TN_DIMS, preferred_element_type=F32)
                dk = dk + lax.dot_general(da, qi, _TN_DIMS, preferred_element_type=F32)
                dq_acc[qrows, :] += jnp.dot(da, kj, preferred_element_type=F32)
                return dk, dv

            zero = jnp.zeros((t, hd), F32)
            dk, dv = lax.fori_loop(j, nt, over_queries, (zero, zero))
            dsec_ref[1, krows, :] = dk.astype(BF16)
            dsec_ref[2, krows, :] = dv.astype(BF16)
            return 0

        lax.fori_loop(0, nt, over_keys, 0)
        dsec_ref[0] = dq_acc[...].astype(BF16)

    def col(off):
        return pl.BlockSpec((s, hd), lambda h: (0, off + h))

    return pl.pallas_call(
        body, name="ret_bwd", grid=(nh,),
        in_specs=[col(c0), col(c0 + nh), col(c0 + 2 * nh), col(c0 + 3 * nh), col(0), col(mixed_blocks)]
        + [ANY] * (1 + len(after)),
        out_specs=pl.BlockSpec((4, s, hd), lambda h: (1, 0, h)),
        out_shape=jax.ShapeDtypeStruct(dsec.shape, BF16),
        input_output_aliases={6: 0},
        scratch_shapes=[pltpu.VMEM((s, hd), BF16)] * 4 + [pltpu.VMEM((s, hd), F32)]
        + [pltpu.VMEM((nt, t, t), F32)],
        compiler_params=_params(("arbitrary",)),
    )(proj, proj, proj, proj, ret_raw, dmixed, dsec, *after)


_FLIPS = (2, 1, 3)


def _other_chips(x, y):
    return [(1 - x, y), (x, 1 - y), (1 - x, 1 - y)]


_HBM = pl.BlockSpec(memory_space=pltpu.HBM)
_SEM = pl.BlockSpec(memory_space=pltpu.SEMAPHORE)
_EFFECT = pltpu.SideEffectType.DATAFLOW_SIDE_EFFECTING


def _in_hbm(a):
    return pltpu.with_memory_space_constraint(a, pltpu.HBM)


def _weight_view(w, column_sharded):
    if column_sharded:
        return w.reshape(2, w.shape[0] // 2, w.shape[1])
    return w.reshape(N_CHIPS, 2, w.shape[0] // (2 * N_CHIPS), w.shape[1])


def _weight_unview(v):
    if v.ndim == 3:
        return v.reshape(2 * v.shape[1], v.shape[2])
    return v.reshape(N_CHIPS * 2 * v.shape[2], v.shape[3])


def _weight_region(buf, shard, half):
    if len(buf.shape) == 3:
        cols = buf.shape[2] // N_CHIPS
        return buf.at[half, :, pl.ds(shard * cols, cols)]
    return buf.at[shard, half]


def _remote(where, send_sem, recv_sem, to):
    return pltpu.make_async_remote_copy(src_ref=where, dst_ref=where, send_sem=send_sem, recv_sem=recv_sem,
                                        device_id=to, device_id_type=MESH)


def _for_my_shard(fn):
    x, y, _ = _place()
    for ss in range(N_CHIPS):
        pl.when(2 * x + y == ss)(functools.partial(fn, ss))


def _gather_forward(views, which, send_sems, recv_sems, after, name, base=0, sibling_id=None):
    n_w = len(views)
    which = [base // 3 + w for w in which] if base % 3 == 0 else None
    assert which is not None, "base must be a multiple of 3"

    def body(*refs):
        if sibling_id is not None:
            _sibling_handshake()
        send_in, recv_in = refs[n_w:n_w + 2]
        fwd_send, fwd_recv = refs[n_w + 3:n_w + 5]
        bufs = refs[n_w + 5:]
        x, y, c = _place()
        sibling = (x, y, 1 - c)

        def forward(ss):
            for i, w in enumerate(which):
                for j in range(3):
                    landed = _weight_region(bufs[i], ss ^ _FLIPS[j], c)
                    _remote(landed, send_in.at[3 * w + j], recv_in.at[3 * w + j], sibling).wait_recv()
                    _remote(landed, fwd_send.at[3 * i + j], fwd_recv.at[3 * i + j], sibling).start()

        _for_my_shard(forward)
        for i, w in enumerate(which):
            for j in range(3):
                _remote(_weight_region(bufs[i], 0, 0), send_in.at[3 * w + j], recv_in.at[3 * w + j],
                        sibling).wait_send()

    return pl.pallas_call(
        body, name=name,
        in_specs=[_HBM] * n_w + [_SEM, _SEM, ANY], out_specs=[_SEM, _SEM] + [_HBM] * n_w,
        out_shape=[pltpu.SemaphoreType.DMA((3 * n_w,)), pltpu.SemaphoreType.DMA((3 * n_w,))]
        + [pltpu.HBM(v.shape, BF16) for v in views],
        input_output_aliases={w: 2 + w for w in range(n_w)},
        compiler_params=pltpu.CompilerParams(has_side_effects=_EFFECT, collective_id=sibling_id),
    )(*views, send_sems, recv_sems, after)


def _gather_end(views, fwd_send, fwd_recv, after, name):
    n_w = len(views)

    def body(*refs):
        fwd_send_ref, fwd_recv_ref = refs[n_w:n_w + 2]
        bufs = refs[n_w + 3:]
        x, y, c = _place()
        for i in range(n_w):
            for j in range(3):
                cp = _remote(_weight_region(bufs[i], 0, 0), fwd_send_ref.at[3 * i + j], fwd_recv_ref.at[3 * i + j],
                             (x, y, 1 - c))
                cp.wait_recv()
                cp.wait_send()

    outs = pl.pallas_call(
        body, name=name,
        in_specs=[_HBM] * n_w + [_SEM, _SEM, ANY], out_specs=[_HBM] * n_w,
        out_shape=[pltpu.HBM(v.shape, BF16) for v in views],
        input_output_aliases={w: w for w in range(n_w)},
        compiler_params=pltpu.CompilerParams(has_side_effects=_EFFECT),
    )(*views, fwd_send, fwd_recv, after)
    return [_weight_unview(o) for o in outs]


def _comm_call(name, bufs, sem_pairs, after, n_new, fn, sibling_id=None):
    n, n_sem, after = len(bufs), 2 * len(sem_pairs), tuple(after)
    n_out_sem = 2 if n_new else 0

    def body(*refs):
        if sibling_id is not None:
            _sibling_handshake()
        sems = refs[n:n + n_sem]
        outs = refs[n + n_sem + len(after):]
        new = outs[:n_out_sem] if n_new else (None, None)
        fn(outs[n_out_sem:], [(sems[2 * i], sems[2 * i + 1]) for i in range(len(sem_pairs))], *new)

    res = pl.pallas_call(
        body, name=name,
        in_specs=[_HBM] * n + [_SEM] * n_sem + [ANY] * len(after),
        out_specs=[_SEM] * n_out_sem + [_HBM] * n,
        out_shape=[pltpu.SemaphoreType.DMA((n_new,))] * n_out_sem + [pltpu.HBM(b.shape, b.dtype) for b in bufs],
        input_output_aliases={i: n_out_sem + i for i in range(n)},
        compiler_params=pltpu.CompilerParams(has_side_effects=_EFFECT, collective_id=sibling_id),
    )(*bufs, *[s for pair in sem_pairs for s in pair], *after)
    return list(res[:n_out_sem]), list(res[n_out_sem:])


def _quarter(piece, q):
    rows = piece.shape[0] // 2
    return piece.at[pl.ds(q * rows, rows)]


def _gather_in_start(view, name):
    def fn(bufs, _, send, recv):
        x, y, c = _place()

        def go(ss):
            for j, chip in enumerate(_other_chips(x, y)[:2]):
                _remote(_weight_region(bufs[0], ss, c), send.at[j], recv.at[j], (*chip, c)).start()

        _for_my_shard(go)

    sems, (view,) = _comm_call(name, [_in_hbm(view)], [], (), 2, fn)
    return sems, view


def _gather_out_gate_start(v_out, v_gate, after, name):
    def fn(bufs, _, send, recv):
        x, y, c = _place()
        chips = _other_chips(x, y)

        def go(ss):
            for j in range(3):
                _remote(_weight_region(bufs[0], ss, c), send.at[j], recv.at[j], (*chips[j], c)).start()
            for j in range(2):
                _remote(_weight_region(bufs[1], ss, c), send.at[3 + j], recv.at[3 + j], (*chips[j], c)).start()

        _for_my_shard(go)

    sems, views = _comm_call(name, [_in_hbm(v_out), _in_hbm(v_gate)], [], after, 5, fn)
    return sems, views


def _gather_relay(view, started, base, after, name, then=None, then_peers=0, then_first=False):
    n_new = 6 + then_peers if then_peers else 4

    def fn(bufs, pairs, send, recv):
        (send_in, recv_in), = pairs
        x, y, c = _place()
        chips = _other_chips(x, y)
        sibling = (x, y, 1 - c)

        def go(ss):
            def start_then():
                for j in range(then_peers):
                    _remote(_weight_region(bufs[1], ss, c), send.at[6 + j], recv.at[6 + j], (*chips[j], c)).start()

            if then_first:
                start_then()
            landed = [_weight_region(bufs[0], ss ^ _FLIPS[j], c) for j in range(2)]
            for j in range(2):
                _remote(landed[j], send_in.at[base + j], recv_in.at[base + j], sibling).wait_recv()
            for j in range(2):
                _remote(_quarter(landed[j], j), send.at[j], recv.at[j], (*chips[1 - j], c)).start()
            for j in range(2):
                _remote(landed[j], send.at[2 + j], recv.at[2 + j], sibling).start()
            if not then_first:
                start_then()

        _for_my_shard(go)
        for j in range(2):
            _remote(_weight_region(bufs[0], 0, 0), send_in.at[base + j], recv_in.at[base + j], sibling).wait_send()

    views = [view] if then is None else [view, _in_hbm(then)]
    sems, views = _comm_call(name, views, [started], after, n_new, fn)
    return sems, views


def _gather_in_neighbours_end(view, relayed, after, name):
    def fn(bufs, pairs, *_):
        (send, recv), = pairs
        x, y, c = _place()
        for j in range(2):
            cp = _remote(_weight_region(bufs[0], 0, 0), send.at[2 + j], recv.at[2 + j], (x, y, 1 - c))
            cp.wait_recv()
            cp.wait_send()

    _, (view,) = _comm_call(name, [view], [relayed], after, 0, fn)
    return view


def _gather_in_diagonal(view, relayed, after, name, sibling_id):
    def fn(bufs, pairs, send, recv):
        (send_in, recv_in), = pairs
        x, y, c = _place()
        sibling = (x, y, 1 - c)
        any_quarter = _quarter(_weight_region(bufs[0], 0, 0), 0)
        for j in range(2):
            cp = _remote(any_quarter, send_in.at[j], recv_in.at[j], sibling)
            cp.wait_recv()
            cp.wait_send()

        def go(ss):
            _remote(_weight_region(bufs[0], ss ^ _FLIPS[2], c), send.at[0], recv.at[0], sibling).start()

        _for_my_shard(go)

    sems, (view,) = _comm_call(name, [view], [relayed], after, 1, fn, sibling_id=sibling_id)
    return sems, view


def _gather_in_diagonal_end(view, forwarded, after, name):
    def fn(bufs, pairs, *_):
        (send, recv), = pairs
        x, y, c = _place()
        cp = _remote(_weight_region(bufs[0], 0, 0), send.at[0], recv.at[0], (x, y, 1 - c))
        cp.wait_recv()
        cp.wait_send()

    _, (view,) = _comm_call(name, [view], [forwarded], after, 0, fn)
    return view


def _in_proj_shard(h1, wi, proj, shard_arr, name):
    s, d = h1.shape
    n = wi.shape[1]
    tn = 256
    blocks = n // (N_CHIPS * tn)
    given = [] if proj is None else [proj]

    def body(shard_ref, h_ref, w_ref, *rest):
        del shard_ref
        rest[-1][...] = jnp.dot(h_ref[...], w_ref[...], preferred_element_type=F32)

    grid_spec = pltpu.PrefetchScalarGridSpec(
        num_scalar_prefetch=1, grid=(blocks,),
        in_specs=[pl.BlockSpec((s, d), lambda j, shard_ref: (0, 0)),
                  pl.BlockSpec((d, tn), lambda j, shard_ref: (0, shard_ref[0] * blocks + j))] + [ANY] * len(given),
        out_specs=pl.BlockSpec((s, tn), lambda j, shard_ref: (0, shard_ref[0] * blocks + j)))
    return pl.pallas_call(
        body, name=name, grid_spec=grid_spec,
        out_shape=jax.ShapeDtypeStruct((s, n), F32),
        input_output_aliases={3: 0} if given else {},
        compiler_params=_params(("arbitrary",)),
    )(shard_arr, h1, wi, *given)


def _sibling_handshake():
    x, y, c = _place()
    barrier = pltpu.get_barrier_semaphore()
    pl.semaphore_signal(barrier, inc=1, device_id=(x, y, 1 - c), device_id_type=MESH)
    pl.semaphore_wait(barrier, 1)


def _chips_handshake():
    x, y, c = _place()
    barrier = pltpu.get_barrier_semaphore()
    for cx, cy in _other_chips(x, y):
        pl.semaphore_signal(barrier, inc=1, device_id=(cx, cy, c), device_id_type=MESH)
    pl.semaphore_wait(barrier, N_CHIPS - 1)


def _split_start(name, bufs, n_sems, copies, sibling_id=None, chips_id=None):
    n = len(bufs)
    assert sibling_id is None or chips_id is None

    def body(*refs):
        if sibling_id is not None:
            _sibling_handshake()
        if chips_id is not None:
            _chips_handshake()
        send_sems, recv_sems = refs[n:n + 2]
        for cp in copies(refs[n + 2:], send_sems, recv_sems):
            cp.start()

    outs = pl.pallas_call(
        body, name=name,
        in_specs=[_HBM] * n, out_specs=[_SEM, _SEM] + [_HBM] * n,
        out_shape=[pltpu.SemaphoreType.DMA((n_sems,)), pltpu.SemaphoreType.DMA((n_sems,))]
        + [pltpu.HBM(b.shape, b.dtype) for b in bufs],
        input_output_aliases={i: 2 + i for i in range(n)},
        compiler_params=pltpu.CompilerParams(has_side_effects=_EFFECT,
                                             collective_id=sibling_id if chips_id is None else chips_id),
    )(*[_in_hbm(b) for b in bufs])
    return outs[0], outs[1], list(outs[2:])


def _split_wait(name, bufs, send_sems, recv_sems, copies, after):
    n = len(bufs)
    after = tuple(after) if isinstance(after, (list, tuple)) else (after,)

    def body(*refs):
        send_ref, recv_ref = refs[n:n + 2]
        for cp in copies(refs[n + 2 + len(after):], send_ref, recv_ref):
            cp.wait()

    return list(pl.pallas_call(
        body, name=name,
        in_specs=[_HBM] * n + [_SEM, _SEM] + [ANY] * len(after), out_specs=[_HBM] * n,
        out_shape=[pltpu.HBM(b.shape, b.dtype) for b in bufs],
        input_output_aliases={i: i for i in range(n)},
        compiler_params=pltpu.CompilerParams(has_side_effects=_EFFECT),
    )(*bufs, send_sems, recv_sems, *after))


def _halves_copies(n_w):
    def copies(bufs, send_sems, recv_sems):
        x, y, c = _place()
        out = []
        for w in range(n_w):
            view, land = bufs[w], bufs[n_w + w]
            src = view.at[1 - c] if len(view.shape) == 3 else view.at[:, 1 - c]
            out.append(pltpu.make_async_remote_copy(
                src_ref=src, dst_ref=land, send_sem=send_sems.at[w], recv_sem=recv_sems.at[w],
                device_id=(x, y, 1 - c), device_id_type=MESH))
        return out
    return copies


def _pieces_copies(n_w):
    def copies(bufs, send_sems, recv_sems):
        x, y, c = _place()
        out = []
        for w in range(n_w):
            for j, (cx, cy) in enumerate(_other_chips(x, y)):
                out.append(pltpu.make_async_remote_copy(
                    src_ref=bufs[w].at[2 * cx + cy], dst_ref=bufs[n_w + w].at[j],
                    send_sem=send_sems.at[3 * w + j], recv_sem=recv_sems.at[3 * w + j],
                    device_id=(cx, cy, c), device_id_type=MESH))
        return out
    return copies


def _join_copies(n_w):
    def copies(bufs, send_sems, recv_sems):
        x, y, c = _place()
        return [pltpu.make_async_remote_copy(
            src_ref=bufs[w].at[c], dst_ref=bufs[w].at[c], send_sem=send_sems.at[w], recv_sem=recv_sems.at[w],
            device_id=(x, y, 1 - c), device_id_type=MESH) for w in range(n_w)]
    return copies


def _halves_landing(view):
    shape = view.shape[1:] if view.ndim == 3 else (N_CHIPS,) + view.shape[2:]
    return lax.empty(shape, BF16)


_SIBLING_IDS = {"halves_down": 1, "halves_gate_up": 2, "halves_out": 3, "halves_in": 4,
                "join_down": 5, "join_gate_up": 6, "join_out": 7, "join_in": 8,
                "diagonal_in": 9, "diagonal_gate": 10, "diagonal_up": 11, "forward_out": 12, "forward_down": 13,
                "pieces_down": 14, "pieces_gate_up": 15, "pieces_out": 16, "pieces_in": 17}


def _halves_start(tag, grads, column_sharded):
    views = [_weight_view(g, cs) for g, cs in zip(grads, column_sharded)]
    n = len(views)
    return _split_start("halves_start_" + tag, views + [_halves_landing(v) for v in views], n, _halves_copies(n),
                        sibling_id=_SIBLING_IDS["halves_" + tag])


def _halves_wait(tag, state, after):
    send_sems, recv_sems, bufs = state
    n = len(bufs) // 2
    bufs = _split_wait("halves_wait_" + tag, bufs, send_sems, recv_sems, _halves_copies(n), after)
    return bufs[:n], bufs[n:]


def _pieces_start(tag, pieces):
    n = len(pieces)
    landing = [lax.empty((3,) + p.shape[1:], BF16) for p in pieces]
    return _split_start("pieces_start_" + tag, list(pieces) + landing, 3 * n, _pieces_copies(n),
                        chips_id=_SIBLING_IDS["pieces_" + tag])


def _pieces_wait(tag, state, after):
    send_sems, recv_sems, bufs = state
    n = len(bufs) // 2
    bufs = _split_wait("pieces_wait_" + tag, bufs, send_sems, recv_sems, _pieces_copies(n), after)
    return bufs[:n], bufs[n:]


def _join_start(tag, shards):
    n = len(shards)
    return _split_start("join_start_" + tag, list(shards), n, _join_copies(n), sibling_id=_SIBLING_IDS["join_" + tag])


def _join_wait(tag, state, after):
    send_sems, recv_sems, bufs = state
    bufs = _split_wait("join_wait_" + tag, bufs, send_sems, recv_sems, _join_copies(len(bufs)), after)
    return [b.reshape(2 * b.shape[1], b.shape[2]) for b in bufs]


def _chip_sum_col(g3, sib, c_arr, name):
    _, hk, n = g3.shape
    cols = n // N_CHIPS
    tr = _row_tile(hk, cols * 2, limit=4 * 1024 * 1024)

    def body(c_ref, g_ref, s_ref, o_ref):
        del c_ref
        o_ref[...] = (g_ref[...].astype(F32) + s_ref[...].astype(F32)).astype(BF16)

    grid_spec = pltpu.PrefetchScalarGridSpec(
        num_scalar_prefetch=1, grid=(N_CHIPS, hk // tr),
        in_specs=[pl.BlockSpec((None, tr, cols), lambda p, r, c_ref: (c_ref[0], r, p)),
                  pl.BlockSpec((tr, cols), lambda p, r, c_ref: (r, p))],
        out_specs=pl.BlockSpec((None, tr, cols), lambda p, r, c_ref: (p, r, 0)))
    return pl.pallas_call(
        body, name=name, grid_spec=grid_spec,
        out_shape=jax.ShapeDtypeStruct((N_CHIPS, hk, cols), BF16),
        compiler_params=_params(("parallel", "parallel")),
    )(c_arr, g3, sib)


def _chip_sum_row(g4, sib, c_arr, name):
    _, _, hr, n = g4.shape
    tr = _row_tile(hr, n * 2, limit=4 * 1024 * 1024)

    def body(c_ref, g_ref, s_ref, o_ref):
        del c_ref
        o_ref[...] = (g_ref[...].astype(F32) + s_ref[...].astype(F32)).astype(BF16)

    grid_spec = pltpu.PrefetchScalarGridSpec(
        num_scalar_prefetch=1, grid=(N_CHIPS, hr // tr),
        in_specs=[pl.BlockSpec((None, None, tr, n), lambda p, r, c_ref: (p, c_ref[0], r, 0)),
                  pl.BlockSpec((None, tr, n), lambda p, r, c_ref: (p, r, 0))],
        out_specs=pl.BlockSpec((None, tr, n), lambda p, r, c_ref: (p, r, 0)))
    return pl.pallas_call(
        body, name=name, grid_spec=grid_spec,
        out_shape=jax.ShapeDtypeStruct((N_CHIPS, hr, n), BF16),
        compiler_params=_params(("parallel", "parallel")),
    )(c_arr, g4, sib)


def _sum_pieces(pieces, received, place_arr, name):
    _, r, n = pieces.shape
    tr = _row_tile(r, n * 4, limit=4 * 1024 * 1024)

    def body(p_ref, own_ref, r0_ref, r1_ref, r2_ref, o_ref):
        del p_ref
        acc = own_ref[...].astype(F32) + r0_ref[...].astype(F32)
        acc = acc + r1_ref[...].astype(F32)
        o_ref[...] = acc + r2_ref[...].astype(F32)

    def recv_spec(j):
        return pl.BlockSpec((None, tr, n), lambda i, p_ref: (j, i, 0))

    grid_spec = pltpu.PrefetchScalarGridSpec(
        num_scalar_prefetch=1, grid=(r // tr,),
        in_specs=[pl.BlockSpec((None, tr, n), lambda i, p_ref: (p_ref[0], i, 0)),
                  recv_spec(0), recv_spec(1), recv_spec(2)],
        out_specs=pl.BlockSpec((None, tr, n), lambda i, p_ref: (p_ref[1], i, 0)))
    return pl.pallas_call(
        body, name=name, grid_spec=grid_spec,
        out_shape=jax.ShapeDtypeStruct((2, r, n), F32),
        compiler_params=_params(("parallel",)),
    )(place_arr, pieces, received, received, received)


def _norm_weights_step(parts, w, m, v, after=()):
    rows, d = parts.shape
    after = tuple(after)

    def body(p_ref, w_ref, m_ref, v_ref, *rest):
        g_ref, d_ref, mo_ref, vo_ref, gathered, send_sems, recv_sems = rest[len(after):]
        x, y, c = _place()
        me = 4 * x + 2 * y + c
        gathered[me] = p_ref[...]
        copies = []
        for k in range(1, N_DEV):
            peer = (x ^ ((k >> 2) & 1), y ^ ((k >> 1) & 1), c ^ (k & 1))
            copies.append(pltpu.make_async_remote_copy(
                src_ref=p_ref, dst_ref=gathered.at[me], send_sem=send_sems.at[k - 1],
                recv_sem=recv_sems.at[k - 1], device_id=peer, device_id_type=MESH))
        for cp in copies:
            cp.start()
        for cp in copies:
            cp.wait()
        g = gathered[0]
        for k in range(1, N_DEV):
            g = g + gathered[k]
        delta, m_new, v_new = _adamw_math(w_ref[...], g, m_ref[...], v_ref[...])
        g_ref[...] = g
        d_ref[...] = delta
        mo_ref[...] = m_new
        vo_ref[...] = v_new

    vmem = pl.BlockSpec(memory_space=pltpu.VMEM)
    shp = jax.ShapeDtypeStruct((rows, d), F32)
    return pl.pallas_call(
        body, name="norm_weights_step",
        in_specs=[vmem] * 4 + [ANY] * len(after), out_specs=[vmem] * 4, out_shape=[shp] * 4,
        scratch_shapes=[pltpu.VMEM((N_DEV, rows, d), F32), pltpu.SemaphoreType.DMA((N_DEV - 1,)),
                        pltpu.SemaphoreType.DMA((N_DEV - 1,))],
        compiler_params=pltpu.CompilerParams(has_side_effects=True),
    )(parts, w, m, v, *after)


def kernel(x, norm_mix_w, w_in, w_out, norm_ffn_w, w_gate, w_up, w_down, norm_final_w, loss_target, m_norm_mix_w, m_w_in, m_w_out, m_norm_ffn_w, m_w_gate, m_w_up, m_w_down, m_norm_final_w, v_norm_mix_w, v_w_in, v_w_out, v_norm_ffn_w, v_w_gate, v_w_up, v_w_down, v_norm_final_w):
    s, d = x.shape[1], x.shape[2]
    xs = x.reshape(s, d)
    target = loss_target.reshape(s, d)
    big = {"w_in": (w_in, m_w_in, v_w_in), "w_out": (w_out, m_w_out, v_w_out),
           "w_gate": (w_gate, m_w_gate, v_w_gate), "w_up": (w_up, m_w_up, v_w_up),
           "w_down": (w_down, m_w_down, v_w_down)}
    big = {k: tuple(a.reshape(a.shape[1:]) for a in t) for k, t in big.items()}
    col_names, row_names = ("w_in", "w_gate", "w_up"), ("w_out", "w_down")
    n_in = N_CHIPS * big["w_in"][0].shape[1]
    ffn = N_CHIPS * big["w_gate"][0].shape[1]
    mix = ATTN_WIDTH + RET_WIDTH
    c_arr = lax.axis_index("c").astype(I32).reshape(1)
    shard_arr = (2 * lax.axis_index("x") + lax.axis_index("y")).astype(I32).reshape(1)
    place_arr = jnp.concatenate([shard_arr, c_arr])

    def cast(k, after=()):
        return _weight_view(_cast_into_full(big[k][0], shard_arr, k in col_names, "cast_" + k, after), k in col_names)

    started_in, v_in = _gather_in_start(cast("w_in"), "gather_in_start")

    sec = ATTN_WIDTH

    def section(p, rows):
        return pl.BlockSpec((None, rows, sec), lambda i, j, kk: (p, i, 0))

    h1 = _rms_fwd(xs, norm_mix_w, "rms_mix_fwd", after=[v_in])
    my_shard = shard_arr[0]
    shard_of = [jnp.bitwise_xor(my_shard, f).astype(I32).reshape(1) for f in (0,) + _FLIPS]
    proj = _in_proj_shard(h1, _weight_unview(v_in), None, shard_of[0], "in_proj_own")
    early_views = [cast(k, after=[proj]) for k in ("w_out", "w_gate")]
    v_up, v_down = [cast(k, after=[proj]) for k in ("w_up", "w_down")]
    relayed_in, (v_in,) = _gather_relay(v_in, started_in, 0, early_views + [v_up, v_down], "gather_in_relay")
    started_og, (v_out, v_gate) = _gather_out_gate_start(*early_views, [v_in], "gather_out_gate_start")
    v_in = _gather_in_neighbours_end(v_in, relayed_in, [v_out], "gather_in_neighbours_end")
    proj = _in_proj_shard(h1, _weight_unview(v_in), proj, shard_of[1], "in_proj_x")
    proj = _in_proj_shard(h1, _weight_unview(v_in), proj, shard_of[2], "in_proj_y")
    forwarded_in, v_in = _gather_in_diagonal(v_in, relayed_in, [proj], "gather_in_diagonal",
                                             _SIBLING_IDS["diagonal_in"])
    wi = _weight_unview(_gather_in_diagonal_end(v_in, forwarded_in, [proj], "gather_in_diagonal_end"))
    proj = _in_proj_shard(h1, wi, proj, shard_of[3], "in_proj_diagonal")
    fs_o, fr_o, v_out = _gather_forward([v_out], [0], *started_og, proj, "gather_forward_out",
                                        sibling_id=_SIBLING_IDS["forward_out"])
    mixed, attn_o, lse = _attn_fwd(proj, after=[v_out])
    relayed_g, (v_gate, v_up) = _gather_relay(v_gate, started_og, 3, [attn_o], "gather_gate_relay",
                                              then=v_up, then_peers=2, then_first=True)
    mixed, ret_raw = _ret_fwd(proj, mixed, after=[v_gate])
    wo, = _gather_end([v_out], fs_o, fr_o, ret_raw, "gather_end_out")
    x1, = _matmul("out_proj", "nn", [mixed, mixed], [wo, wo], [0, 0], s, d, sec, s // 2, 512, sec, [xs], [F32],
                  _epi_residual, b_koff=[0, 1], a_specs=[section(0, s // 2), section(1, s // 2)])
    relayed_u, (v_up, v_down) = _gather_relay(v_up, relayed_g, 6, [x1], "gather_up_relay",
                                              then=v_down, then_peers=3)
    h2 = _rms_fwd(x1, norm_ffn_w, "rms_ffn_fwd", after=[v_up])
    v_gate = _gather_in_neighbours_end(v_gate, relayed_g, [v_up], "gather_gate_neighbours_end")
    forwarded_g, v_gate = _gather_in_diagonal(v_gate, relayed_g, [v_up], "gather_gate_diagonal",
                                              _SIBLING_IDS["diagonal_gate"])
    v_up = _gather_in_neighbours_end(v_up, relayed_u, [v_gate], "gather_up_neighbours_end")
    wg = _weight_unview(_gather_in_diagonal_end(v_gate, forwarded_g, [v_up], "gather_gate_diagonal_end"))
    forwarded_u, v_up = _gather_in_diagonal(v_up, relayed_u, [wg], "gather_up_diagonal",
                                            _SIBLING_IDS["diagonal_up"])
    wu = _weight_unview(_gather_in_diagonal_end(v_up, forwarded_u, [wg], "gather_up_diagonal_end"))
    gate, up, act = _matmul("gate_up", "nn", [h2, h2], [wg, wu], [0, 1], s, ffn, d, s, 512, d, [],
                            [BF16, BF16, BF16], _epi_swiglu, a_single_buffer=True)
    fs, fr, v_down = _gather_forward([v_down], [0], *relayed_u, act, "gather_forward_down", base=6,
                                     sibling_id=_SIBLING_IDS["forward_down"])
    wd, = _gather_end([v_down], fs, fr, act, "gather_end_down")
    x2, = _matmul("down_proj", "nn", [act], [wd], [0], s, d, ffn, s // 2, 512, ffn, [x1], [F32],
                  _epi_residual)
    loss_row, dx2, dx2b, dwf = _final_norm_loss(x2, norm_final_w.reshape(1, d), target, "final_norm_loss")

    names = col_names + row_names
    grads, new = {}, {}

    def chip_sums(tag_names, views, sibs):
        return [(_chip_sum_col if k in col_names else _chip_sum_row)(v, sb, c_arr, "chip_sum_" + k)
                for k, v, sb in zip(tag_names, views, sibs)]

    def piece_sums(tag_names, pieces, received):
        return [_sum_pieces(p, r, place_arr, "sum_pieces_" + k) for k, p, r in zip(tag_names, pieces, received)]

    def update(k):
        new[k] = _adamw(big[k][0], grads[k], big[k][1], big[k][2], "adamw_" + k)

    dgate, dup = _matmul("d_act", "nt", [dx2b], [wd], [0], s, ffn, d, s, 512, d, [gate, up],
                         [BF16, BF16], _epi_swiglu_bwd, a_single_buffer=True)
    g_wd, = _matmul("g_w_down", "tn", [act], [dx2b], [0], ffn, d, s, 512, d, s, [], [BF16], _epi_plain)
    halves_d = _halves_start("down", [g_wd], [False])
    dh2, = _matmul("d_h2", "nt", [dgate, dup], [wg, wu], [0, 0], s, d, ffn, s // 2, 256, ffn, [], [F32],
                   _epi_plain, after=halves_d[2][-1:], a_single_buffer=True)
    pieces_d = _pieces_start("down", chip_sums(["w_down"], *_halves_wait("down", halves_d, dh2)))
    g_wg, g_wu = _matmul("g_w_gate_up", "tn", [h2, h2], [dgate, dup], [0, 1], d, ffn, s, 1024, 512, s, [],
                         [BF16, BF16], _epi_two, after=pieces_d[2][-1:])
    halves_gu = _halves_start("gate_up", [g_wg, g_wu], [True, True])
    dx1, dx1b, dw_ffn = _rms_bwd(x1, norm_ffn_w, dh2, dx2, "rms_ffn_bwd", after=halves_gu[2][-1:])

    dmixed, = _matmul("d_mixed", "nt", [dx1b], [wo], [0], s, mix, d, s // 2, 512, d, [], [F32], _epi_plain)
    pieces_gu = _pieces_start("gate_up", chip_sums(["w_gate", "w_up"], *_halves_wait("gate_up", halves_gu, dmixed)))
    per = sec // 512
    g_wo, = _matmul("g_w_out", "tn", [mixed], [dx1b], [0], mix, d, s, 512, d, s, [], [BF16], _epi_plain,
                    after=pieces_gu[2][-1:],
                    a_specs=[pl.BlockSpec((None, s, 512), lambda i, j, kk: (i // per, 0, i % per))])
    halves_o = _halves_start("out", [g_wo], [False])
    dsec = _attn_bwd(proj, attn_o, lse, dmixed, after=halves_o[2][-1:])
    pieces_o = _pieces_start("out", chip_sums(["w_out"], *_halves_wait("out", halves_o, dsec)))
    dsec = _ret_bwd(proj, ret_raw, dmixed, dsec, after=pieces_o[2][-1:])
    where = [0, 1, 2, 4, 5, 6, 7]
    n_sec = len(where)
    g_wi, = _matmul("g_w_in", "tn", [h1], [dsec], [0], d, n_in, s, 1024, sec, s, [], [BF16], _epi_plain,
                    b_specs=[pl.BlockSpec((None, s, sec), lambda i, j, kk: (j + (j >= 3).astype(I32), 0, 0))])
    halves_i = _halves_start("in", [g_wi], [True])
    dh1, = _matmul("d_h1", "nt", [dsec] * n_sec, [wi] * n_sec, [0] * n_sec, s, d, sec, s // 2, 256, sec, [], [F32],
                   _epi_plain, b_koff=list(range(n_sec)), after=halves_i[2][-1:],
                   a_specs=[section(p, s // 2) for p in where])
    pieces_i = _pieces_start("in", chip_sums(["w_in"], *_halves_wait("in", halves_i, dh1)))
    grad_x, _, dw_mix = _rms_bwd(xs, norm_mix_w, dh1, dx1, "rms_mix_bwd", after=pieces_i[2][-1:])

    def rows8(*vs):
        return jnp.concatenate([v.reshape(1, d) for v in vs] + [jnp.zeros((8 - len(vs), d), F32)], axis=0)

    join_d = _join_start("down", piece_sums(["w_down"], *_pieces_wait("down", pieces_d, grad_x)))
    join_gu = _join_start("gate_up", piece_sums(["w_gate", "w_up"], *_pieces_wait("gate_up", pieces_gu, join_d[2][0])))
    join_o = _join_start("out", piece_sums(["w_out"], *_pieces_wait("out", pieces_o, join_gu[2][0])))
    grads["w_down"], = _join_wait("down", join_d, join_o[2][0])
    update("w_down")
    grads["w_gate"], grads["w_up"] = _join_wait("gate_up", join_gu, new["w_down"][0])
    update("w_gate")
    update("w_up")
    grads["w_out"], = _join_wait("out", join_o, new["w_up"][0])
    update("w_out")
    others_done = [new[k][0] for k in ("w_down", "w_gate", "w_up", "w_out")]
    join_i = _join_start("in", piece_sums(["w_in"], *_pieces_wait("in", pieces_i, others_done)))
    ng, nd, nm, nv = _norm_weights_step(
        rows8(dw_mix, dw_ffn, dwf, jnp.broadcast_to(loss_row[:, :1], (1, d))),
        rows8(norm_mix_w, norm_ffn_w, norm_final_w),
        rows8(m_norm_mix_w, m_norm_ffn_w, m_norm_final_w), rows8(v_norm_mix_w, v_norm_ffn_w, v_norm_final_w),
        after=join_i[2][:1])
    grads["w_in"], = _join_wait("in", join_i, ng)
    update("w_in")

    loss = ng[3, 0]

    def pack(small, per_weight):
        lead = lambda a: a.reshape((1,) + a.shape)
        return (small[0:1], lead(per_weight["w_in"]), lead(per_weight["w_out"]), small[1:2],
                lead(per_weight["w_gate"]), lead(per_weight["w_up"]), lead(per_weight["w_down"]), small[2])

    return (loss, grad_x.reshape(1, s, d),
            *pack(ng, {k: new[k][3] for k in names}),
            *pack(nd, {k: new[k][0] for k in names}),
            *pack(nm, {k: new[k][1] for k in names}),
            *pack(nv, {k: new[k][2] for k in names}))
```

```python
import functools
import math

import jax
import jax.numpy as jnp
from jax import lax
from jax.experimental import pallas as pl
from jax.experimental.pallas import tpu as pltpu

F32 = jnp.float32
BF16 = jnp.bfloat16
I32 = jnp.int32
MESH = pl.DeviceIdType.MESH
ANY = pl.BlockSpec(memory_space=pl.ANY)

ATTN_HEADS = 8
ATTN_HEAD_DIM = 128
RET_HEADS = 4
RET_HEAD_DIM = 256
ATTN_WIDTH = ATTN_HEADS * ATTN_HEAD_DIM
RET_WIDTH = RET_HEADS * RET_HEAD_DIM
DILATED_PATTERNS = ((128, 1), (512, 4), (2048, 16))
NORM_EPS = 1e-6
ADAM_LR = 0.001
ADAM_B1 = 0.9
ADAM_B2 = 0.999
ADAM_EPS = 1e-08
ADAM_WD = 0.01
ADAM_STEP = 10

N_CHIPS = 4
N_DEV = 8
NEG_BIG = -1e30
SEQ_TILE = 512
ATTN_FWD_HEADS_PER_STEP = 2
ATTN_HEADS_PER_STEP = 1
VMEM_LIMIT_BYTES = 56 * 1024 * 1024


def _params(semantics=None, vmem=VMEM_LIMIT_BYTES):
    return pltpu.CompilerParams(dimension_semantics=semantics, vmem_limit_bytes=vmem)


def _row_tile(rows, row_bytes, limit=2 * 1024 * 1024, mult=16):
    best = None
    for t in range(mult, rows + 1, mult):
        if rows % t == 0 and t * row_bytes <= limit:
            best = t
    assert best is not None, (rows, row_bytes)
    return best


def _sigmoid(x):
    return 1.0 / (1.0 + jnp.exp(-x))


def _select_by_index(idx, values):
    out = jnp.float32(values[-1])
    for i in range(len(values) - 2, -1, -1):
        out = jnp.where(idx == i, jnp.float32(values[i]), out)
    return out


def _place():
    x, y, c = lax.axis_index("x"), lax.axis_index("y"), lax.axis_index("c")
    return x, y, c


def _cast_into_full(w, shard_arr, column_sharded, name, after=()):
    after = tuple(after)
    rows, cols = w.shape
    tr = _row_tile(rows, cols * 4)
    steps = rows // tr
    if column_sharded:
        out_shape, out_map = (rows, N_CHIPS * cols), (lambda i, s_ref: (i, s_ref[0]))
    else:
        out_shape, out_map = (N_CHIPS * rows, cols), (lambda i, s_ref: (s_ref[0] * steps + i, 0))

    def body(s_ref, w_ref, *rest):
        del s_ref
        rest[-1][...] = w_ref[...].astype(BF16)

    grid_spec = pltpu.PrefetchScalarGridSpec(
        num_scalar_prefetch=1, grid=(steps,),
        in_specs=[pl.BlockSpec((tr, cols), lambda i, s_ref: (i, 0))] + [ANY] * len(after),
        out_specs=pl.BlockSpec((tr, cols), out_map))
    return pl.pallas_call(
        body, name=name, grid_spec=grid_spec,
        out_shape=jax.ShapeDtypeStruct(out_shape, BF16),
        compiler_params=_params(("parallel",)),
    )(shard_arr, w, *after)


def _rms_fwd(x, w, name, after=()):
    rows, d = x.shape
    tr = 256
    after = tuple(after)

    def body(x_ref, w_ref, *rest):
        xv = x_ref[...]
        r = lax.rsqrt(jnp.mean(xv * xv, axis=-1, keepdims=True) + NORM_EPS)
        rest[-1][...] = (xv * r * w_ref[...]).astype(BF16)

    return pl.pallas_call(
        body, name=name, grid=(rows // tr,),
        in_specs=[pl.BlockSpec((tr, d), lambda i: (i, 0)), pl.BlockSpec((1, d), lambda i: (0, 0))]
        + [ANY] * len(after),
        out_specs=pl.BlockSpec((tr, d), lambda i: (i, 0)),
        out_shape=jax.ShapeDtypeStruct((rows, d), BF16),
        compiler_params=_params(("parallel",)),
    )(x, w, *after)


def _rms_bwd(x, w, dh, dres, name, after=()):
    rows, d = x.shape
    tr = 256
    after = tuple(after)

    def body(x_ref, w_ref, dh_ref, dres_ref, *rest):
        dx_ref, dxb_ref, dw_ref = rest[len(after):]
        xv = x_ref[...]
        r = lax.rsqrt(jnp.mean(xv * xv, axis=-1, keepdims=True) + NORM_EPS)
        xhat = xv * r
        dy = dh_ref[...]
        dxhat = dy * w_ref[...]
        dx = dres_ref[...] + r * (dxhat - xhat * jnp.mean(dxhat * xhat, axis=-1, keepdims=True))
        dx_ref[...] = dx
        dxb_ref[...] = dx.astype(BF16)
        part = jnp.sum(dy * xhat, axis=0, keepdims=True)

        @pl.when(pl.program_id(0) == 0)
        def _():
            dw_ref[...] = part

        @pl.when(pl.program_id(0) != 0)
        def _():
            dw_ref[...] += part

    row = pl.BlockSpec((tr, d), lambda i: (i, 0))
    vec = pl.BlockSpec((1, d), lambda i: (0, 0))
    return pl.pallas_call(
        body, name=name, grid=(rows // tr,),
        in_specs=[row, vec, row, row] + [ANY] * len(after),
        out_specs=[row, row, vec],
        out_shape=[jax.ShapeDtypeStruct((rows, d), F32), jax.ShapeDtypeStruct((rows, d), BF16),
                   jax.ShapeDtypeStruct((1, d), F32)],
        compiler_params=_params(("arbitrary",)),
    )(x, w, dh, dres, *after)


def _final_norm_loss(x2, w, target, name):
    rows, d = x2.shape
    tr = 256

    def body(x_ref, w_ref, t_ref, loss_ref, dx_ref, dxb_ref, dw_ref):
        xv = x_ref[...]
        wv = w_ref[...]
        r = lax.rsqrt(jnp.mean(xv * xv, axis=-1, keepdims=True) + NORM_EPS)
        xhat = xv * r
        err = xhat * wv - t_ref[...]
        part_loss = 0.5 * jnp.sum(jnp.mean(err * err, axis=-1, keepdims=True), axis=0, keepdims=True)
        dy = err * (1.0 / d)
        dxhat = dy * wv
        dx = r * (dxhat - xhat * jnp.mean(dxhat * xhat, axis=-1, keepdims=True))
        dx_ref[...] = dx
        dxb_ref[...] = dx.astype(BF16)
        part_dw = jnp.sum(dy * xhat, axis=0, keepdims=True)
        part_loss = jnp.broadcast_to(part_loss, (1, 128))

        @pl.when(pl.program_id(0) == 0)
        def _():
            dw_ref[...] = part_dw
            loss_ref[...] = part_loss

        @pl.when(pl.program_id(0) != 0)
        def _():
            dw_ref[...] += part_dw
            loss_ref[...] += part_loss

    row = pl.BlockSpec((tr, d), lambda i: (i, 0))
    vec = pl.BlockSpec((1, d), lambda i: (0, 0))
    return pl.pallas_call(
        body, name=name, grid=(rows // tr,),
        in_specs=[row, vec, row],
        out_specs=[pl.BlockSpec((1, 128), lambda i: (0, 0)), row, row, vec],
        out_shape=[jax.ShapeDtypeStruct((1, 128), F32), jax.ShapeDtypeStruct((rows, d), F32),
                   jax.ShapeDtypeStruct((rows, d), BF16), jax.ShapeDtypeStruct((1, d), F32)],
        compiler_params=_params(("arbitrary",)),
    )(x2, w, target)


def _adamw_math(w, g, m, v):
    m = ADAM_B1 * m + (1.0 - ADAM_B1) * g
    v = ADAM_B2 * v + (1.0 - ADAM_B2) * (g * g)
    m_hat = m / (1.0 - ADAM_B1 ** ADAM_STEP)
    v_hat = v / (1.0 - ADAM_B2 ** ADAM_STEP)
    delta = -ADAM_LR * (m_hat / (jnp.sqrt(v_hat) + ADAM_EPS) + ADAM_WD * w)
    return delta, m, v


def _adamw(w, g, m, v, name):
    rows, cols = w.shape
    tr = _row_tile(rows, cols * 4)

    def body(w_ref, g_ref, m_ref, v_ref, d_ref, mo_ref, vo_ref, go_ref):
        g = g_ref[...]
        delta, m_new, v_new = _adamw_math(w_ref[...], g, m_ref[...], v_ref[...])
        d_ref[...] = delta
        mo_ref[...] = m_new
        vo_ref[...] = v_new
        go_ref[...] = g

    blk = pl.BlockSpec((tr, cols), lambda i: (i, 0))
    shp = jax.ShapeDtypeStruct((rows, cols), F32)
    return pl.pallas_call(
        body, name=name, grid=(rows // tr,),
        in_specs=[blk] * 4, out_specs=[blk] * 4, out_shape=[shp] * 4,
        compiler_params=_params(("parallel",)),
    )(w, g, m, v)


_DOT_DIMS = {"nn": ((1,), (0,)), "nt": ((1,), (1,)), "tn": ((0,), (0,))}


def _matmul(name, mode, a_list, b_list, acc_of, m, n, k, tm, tn, tk, extras, out_dtypes, epilogue,
            a_koff=None, b_koff=None, after=(), a_specs=None, b_specs=None, a_single_buffer=False):
    after = tuple(after)
    assert m % tm == 0 and n % tn == 0 and k % tk == 0, (name, m, n, k, tm, tn, tk)
    nk = k // tk
    n_acc = max(acc_of) + 1
    n_pairs = len(a_list)
    a_koff = a_koff or [0] * n_pairs
    b_koff = b_koff or [0] * n_pairs
    dims = (_DOT_DIMS[mode], ((), ()))
    n_ext, n_out = len(extras), len(out_dtypes)

    def body(*refs):
        a_refs = refs[:n_pairs]
        b_refs = refs[n_pairs:2 * n_pairs]
        e_refs = refs[2 * n_pairs:2 * n_pairs + n_ext]
        first_out = 2 * n_pairs + n_ext + len(after)
        o_refs = refs[first_out:first_out + n_out]
        acc_refs = refs[first_out + n_out:]

        parts = [None] * n_acc
        for p in range(n_pairs):
            d = lax.dot_general(a_refs[p][...], b_refs[p][...], dims, preferred_element_type=F32)
            parts[acc_of[p]] = d if parts[acc_of[p]] is None else parts[acc_of[p]] + d

        def finish(accs):
            outs = epilogue(accs, [e[...] for e in e_refs])
            for o_ref, o in zip(o_refs, outs):
                o_ref[...] = o.astype(o_ref.dtype)

        if nk == 1:
            finish(parts)
        else:
            kk = pl.program_id(2)

            @pl.when(kk == 0)
            def _():
                for acc_ref, part in zip(acc_refs, parts):
                    acc_ref[...] = part

            @pl.when(kk != 0)
            def _():
                for acc_ref, part in zip(acc_refs, parts):
                    acc_ref[...] += part

            @pl.when(kk == nk - 1)
            def _():
                finish([acc_ref[...] for acc_ref in acc_refs])

    def a_spec(off):
        mode_a = pl.Buffered(1) if a_single_buffer else None
        if mode == "tn":
            return pl.BlockSpec((tk, tm), lambda i, j, kk: (kk + off, i), pipeline_mode=mode_a)
        return pl.BlockSpec((tm, tk), lambda i, j, kk: (i, kk + off), pipeline_mode=mode_a)

    def b_spec(off):
        if mode == "nt":
            return pl.BlockSpec((tn, tk), lambda i, j, kk: (j, kk + off))
        return pl.BlockSpec((tk, tn), lambda i, j, kk: (kk + off, j))

    tile = pl.BlockSpec((tm, tn), lambda i, j, kk: (i, j))
    scratch = [pltpu.VMEM((tm, tn), F32) for _ in range(n_acc)] if nk > 1 else []
    return pl.pallas_call(
        body, name=name, grid=(m // tm, n // tn, nk),
        in_specs=(a_specs or [a_spec(o) for o in a_koff]) + (b_specs or [b_spec(o) for o in b_koff])
        + [tile] * n_ext + [ANY] * len(after),
        out_specs=[tile] * n_out,
        out_shape=[jax.ShapeDtypeStruct((m, n), dt) for dt in out_dtypes],
        scratch_shapes=scratch,
        compiler_params=_params(("parallel", "parallel", "arbitrary")),
    )(*a_list, *b_list, *extras, *after)


def _epi_plain(accs, extras):
    return (accs[0],)


def _epi_residual(accs, extras):
    return (accs[0] + extras[0],)


def _epi_two(accs, extras):
    return accs[0], accs[1]


def _epi_swiglu(accs, extras):
    g, u = accs
    return g, u, g * _sigmoid(g) * u


def _epi_swiglu_bwd(accs, extras):
    da = accs[0]
    g, u = (e.astype(F32) for e in extras)
    sg = _sigmoid(g)
    dg = da * u * sg * (1.0 + g * (1.0 - sg))
    du = da * g * sg
    return dg, du


_NT_DIMS = (((1,), (1,)), ((), ()))
_TN_DIMS = (((0,), (0,)), ((), ()))


def _tile_delta(tq, tk):
    return lax.broadcasted_iota(I32, (tq, tk), 0) - lax.broadcasted_iota(I32, (tq, tk), 1)


def _attn_log_count(delta):
    count = jnp.zeros(delta.shape, I32)
    for window, dilation in DILATED_PATTERNS:
        hit = ((delta & (dilation - 1)) == 0) & (delta <= window)
        count = count + jnp.where(hit, 1, 0)
    valid = (delta >= 0) & (count > 0)
    logm = jnp.where(count == 3, math.log(3.0), jnp.where(count == 2, math.log(2.0), 0.0))
    return jnp.where(valid, logm, NEG_BIG)


def _fill_attn_log_count(tab_ref):
    nb, t, _ = tab_ref.shape
    base = _tile_delta(t, t)
    for b in range(nb):
        tab_ref[b] = _attn_log_count(base + b * t)


def _fill_attn_bias(tab_ref, log_count_ref, slope):
    nb, t, _ = tab_ref.shape
    dist = _tile_delta(t, t).astype(F32)
    for b in range(nb):
        tab_ref[b] = log_count_ref[b] - slope * (dist + float(b * t))


def _fill_ret_decay(tab_ref, log_gamma):
    nb, t, _ = tab_ref.shape
    base = _tile_delta(t, t)
    for b in range(nb):
        tab_ref[b] = _ret_decay(base + b * t, log_gamma)


def _alibi_slopes():
    return [2.0 ** (-8.0 * (h + 1) / ATTN_HEADS) for h in range(ATTN_HEADS)]


def _attn_fwd(proj, after=()):
    s = proj.shape[0]
    t = SEQ_TILE
    hd = ATTN_HEAD_DIM
    hp = ATTN_FWD_HEADS_PER_STEP
    ng = ATTN_HEADS // hp
    w = hp * hd
    scale = 1.0 / math.sqrt(hd)
    slopes = _alibi_slopes()

    def body(q_ref, k_ref, v_ref, *rest):
        mix_ref, o_ref, lse_ref, kb, vb, bias_tab, log_count_tab = rest[len(after):]
        g = pl.program_id(0)
        i = pl.program_id(1)

        @pl.when((g == 0) & (i == 0))
        def _():
            _fill_attn_log_count(log_count_tab)

        @pl.when(i == 0)
        def _():
            kb[...] = k_ref[...].astype(BF16)
            vb[...] = v_ref[...].astype(BF16)
            for u in range(hp):
                _fill_attn_bias(bias_tab.at[u], log_count_tab, _select_by_index(g * hp + u, slopes))

        qs = [q_ref[:, u * hd:(u + 1) * hd].astype(BF16) for u in range(hp)]

        def step(j, carry):
            rows = pl.ds(pl.multiple_of(j * t, t), t)
            out = []
            for u in range(hp):
                m_i, l_i, acc = carry[u]
                lanes = slice(u * hd, (u + 1) * hd)
                sc = lax.dot_general(qs[u], kb[rows, lanes], _NT_DIMS, preferred_element_type=F32) * scale
                sc = sc + bias_tab[u, i - j]
                m_new = jnp.maximum(m_i, jnp.max(sc, axis=-1, keepdims=True))
                p = jnp.exp(sc - m_new)
                alpha = jnp.exp(m_i - m_new)
                l_new = alpha * l_i + jnp.sum(p, axis=-1, keepdims=True)
                acc = alpha * acc + jnp.dot(p.astype(BF16), vb[rows, lanes], preferred_element_type=F32)
                out.append((m_new, l_new, acc))
            return tuple(out)

        init = (jnp.full((t, 1), NEG_BIG, F32), jnp.zeros((t, 1), F32), jnp.zeros((t, hd), F32))
        final = lax.fori_loop(0, i + 1, step, (init,) * hp)
        for u in range(hp):
            m_i, l_i, acc = final[u]
            lanes = slice(u * hd, (u + 1) * hd)
            out = acc / l_i
            o_ref[:, lanes] = out
            mix_ref[:, lanes] = out.astype(BF16)
            lse_ref[:, lanes] = jnp.broadcast_to(m_i + jnp.log(l_i), (t, hd))

    return pl.pallas_call(
        body, name="attn_fwd", grid=(ng, s // t),
        in_specs=[pl.BlockSpec((t, w), lambda g, i: (i, g)),
                  pl.BlockSpec((s, w), lambda g, i: (0, ng + g)),
                  pl.BlockSpec((s, w), lambda g, i: (0, 2 * ng + g))] + [ANY] * len(after),
        out_specs=[pl.BlockSpec((None, t, w), lambda g, i: (0, i, g))] + [pl.BlockSpec((t, w), lambda g, i: (i, g))] * 2,
        out_shape=[jax.ShapeDtypeStruct((2, s, ATTN_WIDTH), BF16),
                   jax.ShapeDtypeStruct((s, ATTN_WIDTH), F32),
                   jax.ShapeDtypeStruct((s, ATTN_WIDTH), F32)],
        scratch_shapes=[pltpu.VMEM((s, w), BF16), pltpu.VMEM((s, w), BF16), pltpu.VMEM((hp, s // t, t, t), F32),
                        pltpu.VMEM((s // t, t, t), F32)],
        compiler_params=_params(("arbitrary", "arbitrary")),
    )(proj, proj, proj, *after)


def _attn_bwd(proj, attn_out, lse, dmixed, after=()):
    after = tuple(after)
    s = proj.shape[0]
    t = SEQ_TILE
    nt = s // t
    hd = ATTN_HEAD_DIM
    hp = ATTN_HEADS_PER_STEP
    ng = ATTN_HEADS // hp
    w = hp * hd
    scale = 1.0 / math.sqrt(hd)
    slopes = _alibi_slopes()

    def body(q_ref, k_ref, v_ref, o_ref, lse_ref, do_ref, *rest):
        dsec_ref, qb, kb, vb, dob, dsum, dq_acc, bias_tab, log_count_tab = rest[len(after):]
        g = pl.program_id(0)

        @pl.when(g == 0)
        def _():
            _fill_attn_log_count(log_count_tab)

        qb[...] = q_ref[...].astype(BF16)
        kb[...] = k_ref[...].astype(BF16)
        vb[...] = v_ref[...].astype(BF16)
        dob[...] = do_ref[...].astype(BF16)
        for u in range(hp):
            lanes = slice(u * hd, (u + 1) * hd)
            _fill_attn_bias(bias_tab.at[u], log_count_tab, _select_by_index(g * hp + u, slopes))
            rowsum = jnp.sum(do_ref[:, lanes] * o_ref[:, lanes], axis=-1, keepdims=True)
            dsum[:, lanes] = jnp.broadcast_to(rowsum, (s, hd))
        dq_acc[...] = jnp.zeros((s, w), F32)

        def over_keys(j, _):
            krows = pl.ds(pl.multiple_of(j * t, t), t)

            def over_queries(i, carry):
                qrows = pl.ds(pl.multiple_of(i * t, t), t)
                out = []
                for u in range(hp):
                    dk, dv = carry[u]
                    lanes = slice(u * hd, (u + 1) * hd)
                    qi, doi = qb[qrows, lanes], dob[qrows, lanes]
                    kj, vj = kb[krows, lanes], vb[krows, lanes]
                    lse_i = lse_ref[qrows, lanes][:, :1]
                    dsum_i = dsum[qrows, lanes][:, :1]
                    sc = lax.dot_general(qi, kj, _NT_DIMS, preferred_element_type=F32) * scale
                    p = jnp.exp(sc + bias_tab[u, i - j] - lse_i)
                    dp = lax.dot_general(doi, vj, _NT_DIMS, preferred_element_type=F32)
                    ds = (p * (dp - dsum_i)).astype(BF16)
                    dv = dv + lax.dot_general(p.astype(BF16), doi, _TN_DIMS, preferred_element_type=F32)
                    dk = dk + lax.dot_general(ds, qi, _TN_DIMS, preferred_element_type=F32)
                    dq_acc[qrows, lanes] += jnp.dot(ds, kj, preferred_element_type=F32)
                    out.append((dk, dv))
                return tuple(out)

            zero = jnp.zeros((t, hd), F32)
            final = lax.fori_loop(j, nt, over_queries, ((zero, zero),) * hp)
            for u in range(hp):
                lanes = slice(u * hd, (u + 1) * hd)
                dsec_ref[1, krows, lanes] = (final[u][0] * scale).astype(BF16)
                dsec_ref[2, krows, lanes] = final[u][1].astype(BF16)
            return 0

        lax.fori_loop(0, nt, over_keys, 0)
        dsec_ref[0] = (dq_acc[...] * scale).astype(BF16)

    def col(off):
        return pl.BlockSpec((s, w), lambda g: (0, off + g))

    return pl.pallas_call(
        body, name="attn_bwd", grid=(ng,),
        in_specs=[col(0), col(ng), col(2 * ng), col(0), col(0), col(0)] + [ANY] * len(after),
        out_specs=pl.BlockSpec((4, s, w), lambda g: (0, 0, g)),
        out_shape=jax.ShapeDtypeStruct((8, s, ATTN_WIDTH), BF16),
        scratch_shapes=[pltpu.VMEM((s, w), BF16)] * 4 + [pltpu.VMEM((s, w), F32)] * 2
        + [pltpu.VMEM((hp, nt, t, t), F32), pltpu.VMEM((nt, t, t), F32)],
        compiler_params=_params(("arbitrary",)),
    )(proj, proj, proj, attn_out, lse, dmixed, *after)


def _ret_log_gammas():
    return [math.log(1.0 - 2.0 ** (-5.0 - h)) for h in range(RET_HEADS)]


def _ret_decay(delta, log_gamma):
    dec = jnp.exp(delta.astype(F32) * log_gamma) * (1.0 / math.sqrt(RET_HEAD_DIM))
    return jnp.where(delta >= 0, dec, 0.0)


def _ret_fwd(proj, mixed, after=()):
    after = tuple(after)
    s = proj.shape[0]
    t = SEQ_TILE
    hd = RET_HEAD_DIM
    nh = RET_HEADS
    log_gammas = _ret_log_gammas()
    c0 = 3 * ATTN_WIDTH // hd

    def body(q_ref, k_ref, v_ref, g_ref, *rest):
        mix_ref, raw_ref, kb, vb, decay_tab = rest[1 + len(after):]
        h = pl.program_id(0)
        i = pl.program_id(1)

        @pl.when(i == 0)
        def _():
            kb[...] = k_ref[...].astype(BF16)
            vb[...] = v_ref[...].astype(BF16)
            _fill_ret_decay(decay_tab, _select_by_index(h, log_gammas))

        q = q_ref[...].astype(BF16)

        def step(j, acc):
            rows = pl.ds(pl.multiple_of(j * t, t), t)
            sc = lax.dot_general(q, kb[rows, :], _NT_DIMS, preferred_element_type=F32) * decay_tab[i - j]
            return acc + jnp.dot(sc.astype(BF16), vb[rows, :], preferred_element_type=F32)

        ret = lax.fori_loop(0, i + 1, step, jnp.zeros((t, hd), F32))
        raw_ref[...] = ret
        r = lax.rsqrt(jnp.mean(ret * ret, axis=-1, keepdims=True) + NORM_EPS)
        g = g_ref[...].astype(F32)
        mix_ref[...] = (g * _sigmoid(g) * (ret * r)).astype(BF16)

    return pl.pallas_call(
        body, name="ret_fwd", grid=(nh, s // t),
        in_specs=[pl.BlockSpec((t, hd), lambda h, i: (i, c0 + h)),
                  pl.BlockSpec((s, hd), lambda h, i: (0, c0 + nh + h)),
                  pl.BlockSpec((s, hd), lambda h, i: (0, c0 + 2 * nh + h)),
                  pl.BlockSpec((t, hd), lambda h, i: (i, c0 + 3 * nh + h))] + [ANY] * (1 + len(after)),
        out_specs=[pl.BlockSpec((None, t, hd), lambda h, i: (1, i, h)), pl.BlockSpec((t, hd), lambda h, i: (i, h))],
        out_shape=[jax.ShapeDtypeStruct(mixed.shape, BF16), jax.ShapeDtypeStruct((s, RET_WIDTH), F32)],
        input_output_aliases={4: 0},
        scratch_shapes=[pltpu.VMEM((s, hd), BF16), pltpu.VMEM((s, hd), BF16), pltpu.VMEM((s // t, t, t), F32)],
        compiler_params=_params(("arbitrary", "arbitrary")),
    )(proj, proj, proj, proj, mixed, *after)


def _ret_bwd(proj, ret_raw, dmixed, dsec, after=()):
    after = tuple(after)
    s = proj.shape[0]
    t = SEQ_TILE
    nt = s // t
    hd = RET_HEAD_DIM
    nh = RET_HEADS
    log_gammas = _ret_log_gammas()
    c0 = 3 * ATTN_WIDTH // hd
    mixed_blocks = ATTN_WIDTH // hd

    def body(q_ref, k_ref, v_ref, g_ref, raw_ref, dmix_ref, *rest):
        dsec_ref, qb, kb, vb, dretb, dq_acc, decay_tab = rest[1 + len(after):]
        h = pl.program_id(0)
        _fill_ret_decay(decay_tab, _select_by_index(h, log_gammas))
        qb[...] = q_ref[...].astype(BF16)
        kb[...] = k_ref[...].astype(BF16)
        vb[...] = v_ref[...].astype(BF16)
        ret = raw_ref[...]
        r = lax.rsqrt(jnp.mean(ret * ret, axis=-1, keepdims=True) + NORM_EPS)
        normed = ret * r
        g = g_ref[...].astype(F32)
        sg = _sigmoid(g)
        dout = dmix_ref[...]
        dsec_ref[3] = (dout * normed * sg * (1.0 + g * (1.0 - sg))).astype(BF16)
        dn = dout * g * sg
        dret = r * (dn - normed * jnp.mean(dn * normed, axis=-1, keepdims=True))
        dretb[...] = dret.astype(BF16)
        dq_acc[...] = jnp.zeros((s, hd), F32)

        def over_keys(j, _):
            krows = pl.ds(pl.multiple_of(j * t, t), t)
            kj = kb[krows, :]
            vj = vb[krows, :]

            def over_queries(i, carry):
                dk, dv = carry
                qrows = pl.ds(pl.multiple_of(i * t, t), t)
                qi = qb[qrows, :]
                doi = dretb[qrows, :]
                dec = decay_tab[i - j]
                a = (lax.dot_general(qi, kj, _NT_DIMS, preferred_element_type=F32) * dec).astype(BF16)
                da = (lax.dot_general(doi, vj, _NT_DIMS, preferred_element_type=F32) * dec).astype(BF16)
                dv = dv + lax.dot_general(a, doi, _TN_DIMS, preferred_element_type=F32)
                dk = dk + lax.dot_general(da, qi, _TN_DIMS, preferred_element_type=F32)
                dq_acc[qrows, :] += jnp.dot(da, kj, preferred_element_type=F32)
                return dk, dv

            zero = jnp.zeros((t, hd), F32)
            dk, dv = lax.fori_loop(j, nt, over_queries, (zero, zero))
            dsec_ref[1, krows, :] = dk.astype(BF16)
            dsec_ref[2, krows, :] = dv.astype(BF16)
            return 0

        lax.fori_loop(0, nt, over_keys, 0)
        dsec_ref[0] = dq_acc[...].astype(BF16)

    def col(off):
        return pl.BlockSpec((s, hd), lambda h: (0, off + h))

    return pl.pallas_call(
        body, name="ret_bwd", grid=(nh,),
        in_specs=[col(c0), col(c0 + nh), col(c0 + 2 * nh), col(c0 + 3 * nh), col(0), col(mixed_blocks)]
        + [ANY] * (1 + len(after)),
        out_specs=pl.BlockSpec((4, s, hd), lambda h: (1, 0, h)),
        out_shape=jax.ShapeDtypeStruct(dsec.shape, BF16),
        input_output_aliases={6: 0},
        scratch_shapes=[pltpu.VMEM((s, hd), BF16)] * 4 + [pltpu.VMEM((s, hd), F32)]
        + [pltpu.VMEM((nt, t, t), F32)],
        compiler_params=_params(("arbitrary",)),
    )(proj, proj, proj, proj, ret_raw, dmixed, dsec, *after)


_FLIPS = (2, 1, 3)


def _other_chips(x, y):
    return [(1 - x, y), (x, 1 - y), (1 - x, 1 - y)]


_HBM = pl.BlockSpec(memory_space=pltpu.HBM)
_SEM = pl.BlockSpec(memory_space=pltpu.SEMAPHORE)
_EFFECT = pltpu.SideEffectType.DATAFLOW_SIDE_EFFECTING


def _in_hbm(a):
    return pltpu.with_memory_space_constraint(a, pltpu.HBM)


def _weight_view(w, column_sharded):
    if column_sharded:
        return w.reshape(2, w.shape[0] // 2, w.shape[1])
    return w.reshape(N_CHIPS, 2, w.shape[0] // (2 * N_CHIPS), w.shape[1])


def _weight_unview(v):
    if v.ndim == 3:
        return v.reshape(2 * v.shape[1], v.shape[2])
    return v.reshape(N_CHIPS * 2 * v.shape[2], v.shape[3])


def _weight_region(buf, shard, half):
    if len(buf.shape) == 3:
        cols = buf.shape[2] // N_CHIPS
        return buf.at[half, :, pl.ds(shard * cols, cols)]
    return buf.at[shard, half]


def _remote(where, send_sem, recv_sem, to):
    return pltpu.make_async_remote_copy(src_ref=where, dst_ref=where, send_sem=send_sem, recv_sem=recv_sem,
                                        device_id=to, device_id_type=MESH)


def _for_my_shard(fn):
    x, y, _ = _place()
    for ss in range(N_CHIPS):
        pl.when(2 * x + y == ss)(functools.partial(fn, ss))


def _gather_forward(views, which, send_sems, recv_sems, after, name, base=0, sibling_id=None):
    n_w = len(views)
    which = [base // 3 + w for w in which] if base % 3 == 0 else None
    assert which is not None, "base must be a multiple of 3"

    def body(*refs):
        if sibling_id is not None:
            _sibling_handshake()
        send_in, recv_in = refs[n_w:n_w + 2]
        fwd_send, fwd_recv = refs[n_w + 3:n_w + 5]
        bufs = refs[n_w + 5:]
        x, y, c = _place()
        sibling = (x, y, 1 - c)

        def forward(ss):
            for i, w in enumerate(which):
                for j in range(3):
                    landed = _weight_region(bufs[i], ss ^ _FLIPS[j], c)
                    _remote(landed, send_in.at[3 * w + j], recv_in.at[3 * w + j], sibling).wait_recv()
                    _remote(landed, fwd_send.at[3 * i + j], fwd_recv.at[3 * i + j], sibling).start()

        _for_my_shard(forward)
        for i, w in enumerate(which):
            for j in range(3):
                _remote(_weight_region(bufs[i], 0, 0), send_in.at[3 * w + j], recv_in.at[3 * w + j],
                        sibling).wait_send()

    return pl.pallas_call(
        body, name=name,
        in_specs=[_HBM] * n_w + [_SEM, _SEM, ANY], out_specs=[_SEM, _SEM] + [_HBM] * n_w,
        out_shape=[pltpu.SemaphoreType.DMA((3 * n_w,)), pltpu.SemaphoreType.DMA((3 * n_w,))]
        + [pltpu.HBM(v.shape, BF16) for v in views],
        input_output_aliases={w: 2 + w for w in range(n_w)},
        compiler_params=pltpu.CompilerParams(has_side_effects=_EFFECT, collective_id=sibling_id),
    )(*views, send_sems, recv_sems, after)


def _gather_end(views, fwd_send, fwd_recv, after, name):
    n_w = len(views)

    def body(*refs):
        fwd_send_ref, fwd_recv_ref = refs[n_w:n_w + 2]
        bufs = refs[n_w + 3:]
        x, y, c = _place()
        for i in range(n_w):
            for j in range(3):
                cp = _remote(_weight_region(bufs[i], 0, 0), fwd_send_ref.at[3 * i + j], fwd_recv_ref.at[3 * i + j],
                             (x, y, 1 - c))
                cp.wait_recv()
                cp.wait_send()

    outs = pl.pallas_call(
        body, name=name,
        in_specs=[_HBM] * n_w + [_SEM, _SEM, ANY], out_specs=[_HBM] * n_w,
        out_shape=[pltpu.HBM(v.shape, BF16) for v in views],
        input_output_aliases={w: w for w in range(n_w)},
        compiler_params=pltpu.CompilerParams(has_side_effects=_EFFECT),
    )(*views, fwd_send, fwd_recv, after)
    return [_weight_unview(o) for o in outs]


def _comm_call(name, bufs, sem_pairs, after, n_new, fn, sibling_id=None):
    n, n_sem, after = len(bufs), 2 * len(sem_pairs), tuple(after)
    n_out_sem = 2 if n_new else 0

    def body(*refs):
        if sibling_id is not None:
            _sibling_handshake()
        sems = refs[n:n + n_sem]
        outs = refs[n + n_sem + len(after):]
        new = outs[:n_out_sem] if n_new else (None, None)
        fn(outs[n_out_sem:], [(sems[2 * i], sems[2 * i + 1]) for i in range(len(sem_pairs))], *new)

    res = pl.pallas_call(
        body, name=name,
        in_specs=[_HBM] * n + [_SEM] * n_sem + [ANY] * len(after),
        out_specs=[_SEM] * n_out_sem + [_HBM] * n,
        out_shape=[pltpu.SemaphoreType.DMA((n_new,))] * n_out_sem + [pltpu.HBM(b.shape, b.dtype) for b in bufs],
        input_output_aliases={i: n_out_sem + i for i in range(n)},
        compiler_params=pltpu.CompilerParams(has_side_effects=_EFFECT, collective_id=sibling_id),
    )(*bufs, *[s for pair in sem_pairs for s in pair], *after)
    return list(res[:n_out_sem]), list(res[n_out_sem:])


def _quarter(piece, q):
    rows = piece.shape[0] // 2
    return piece.at[pl.ds(q * rows, rows)]


def _gather_in_start(view, name):
    def fn(bufs, _, send, recv):
        x, y, c = _place()

        def go(ss):
            for j, chip in enumerate(_other_chips(x, y)[:2]):
                _remote(_weight_region(bufs[0], ss, c), send.at[j], recv.at[j], (*chip, c)).start()

        _for_my_shard(go)

    sems, (view,) = _comm_call(name, [_in_hbm(view)], [], (), 2, fn)
    return sems, view


def _gather_out_gate_start(v_out, v_gate, after, name):
    def fn(bufs, _, send, recv):
        x, y, c = _place()
        chips = _other_chips(x, y)

        def go(ss):
            for j in range(3):
                _remote(_weight_region(bufs[0], ss, c), send.at[j], recv.at[j], (*chips[j], c)).start()
            for j in range(2):
                _remote(_weight_region(bufs[1], ss, c), send.at[3 + j], recv.at[3 + j], (*chips[j], c)).start()

        _for_my_shard(go)

    sems, views = _comm_call(name, [_in_hbm(v_out), _in_hbm(v_gate)], [], after, 5, fn)
    return sems, views


def _gather_relay(view, started, base, after, name, then=None, then_peers=0, then_first=False):
    n_new = 6 + then_peers if then_peers else 4

    def fn(bufs, pairs, send, recv):
        (send_in, recv_in), = pairs
        x, y, c = _place()
        chips = _other_chips(x, y)
        sibling = (x, y, 1 - c)

        def go(ss):
            def start_then():
                for j in range(then_peers):
                    _remote(_weight_region(bufs[1], ss, c), send.at[6 + j], recv.at[6 + j], (*chips[j], c)).start()

            if then_first:
                start_then()
            landed = [_weight_region(bufs[0], ss ^ _FLIPS[j], c) for j in range(2)]
            for j in range(2):
                _remote(landed[j], send_in.at[base + j], recv_in.at[base + j], sibling).wait_recv()
            for j in range(2):
                _remote(_quarter(landed[j], j), send.at[j], recv.at[j], (*chips[1 - j], c)).start()
            for j in range(2):
                _remote(landed[j], send.at[2 + j], recv.at[2 + j], sibling).start()
            if not then_first:
                start_then()

        _for_my_shard(go)
        for j in range(2):
            _remote(_weight_region(bufs[0], 0, 0), send_in.at[base + j], recv_in.at[base + j], sibling).wait_send()

    views = [view] if then is None else [view, _in_hbm(then)]
    sems, views = _comm_call(name, views, [started], after, n_new, fn)
    return sems, views


def _gather_in_neighbours_end(view, relayed, after, name):
    def fn(bufs, pairs, *_):
        (send, recv), = pairs
        x, y, c = _place()
        for j in range(2):
            cp = _remote(_weight_region(bufs[0], 0, 0), send.at[2 + j], recv.at[2 + j], (x, y, 1 - c))
            cp.wait_recv()
            cp.wait_send()

    _, (view,) = _comm_call(name, [view], [relayed], after, 0, fn)
    return view


def _gather_in_diagonal(view, relayed, after, name, sibling_id):
    def fn(bufs, pairs, send, recv):
        (send_in, recv_in), = pairs
        x, y, c = _place()
        sibling = (x, y, 1 - c)
        any_quarter = _quarter(_weight_region(bufs[0], 0, 0), 0)
        for j in range(2):
            cp = _remote(any_quarter, send_in.at[j], recv_in.at[j], sibling)
            cp.wait_recv()
            cp.wait_send()

        def go(ss):
            _remote(_weight_region(bufs[0], ss ^ _FLIPS[2], c), send.at[0], recv.at[0], sibling).start()

        _for_my_shard(go)

    sems, (view,) = _comm_call(name, [view], [relayed], after, 1, fn, sibling_id=sibling_id)
    return sems, view


def _gather_in_diagonal_end(view, forwarded, after, name):
    def fn(bufs, pairs, *_):
        (send, recv), = pairs
        x, y, c = _place()
        cp = _remote(_weight_region(bufs[0], 0, 0), send.at[0], recv.at[0], (x, y, 1 - c))
        cp.wait_recv()
        cp.wait_send()

    _, (view,) = _comm_call(name, [view], [forwarded], after, 0, fn)
    return view


def _in_proj_shard(h1, wi, proj, shard_arr, name):
    s, d = h1.shape
    n = wi.shape[1]
    tn = 256
    blocks = n // (N_CHIPS * tn)
    given = [] if proj is None else [proj]

    def body(shard_ref, h_ref, w_ref, *rest):
        del shard_ref
        rest[-1][...] = jnp.dot(h_ref[...], w_ref[...], preferred_element_type=F32).astype(BF16)

    grid_spec = pltpu.PrefetchScalarGridSpec(
        num_scalar_prefetch=1, grid=(blocks,),
        in_specs=[pl.BlockSpec((s, d), lambda j, shard_ref: (0, 0)),
                  pl.BlockSpec((d, tn), lambda j, shard_ref: (0, shard_ref[0] * blocks + j))] + [ANY] * len(given),
        out_specs=pl.BlockSpec((s, tn), lambda j, shard_ref: (0, shard_ref[0] * blocks + j)))
    return pl.pallas_call(
        body, name=name, grid_spec=grid_spec,
        out_shape=jax.ShapeDtypeStruct((s, n), BF16),
        input_output_aliases={3: 0} if given else {},
        compiler_params=_params(("arbitrary",)),
    )(shard_arr, h1, wi, *given)


def _sibling_handshake():
    x, y, c = _place()
    barrier = pltpu.get_barrier_semaphore()
    pl.semaphore_signal(barrier, inc=1, device_id=(x, y, 1 - c), device_id_type=MESH)
    pl.semaphore_wait(barrier, 1)


def _chips_handshake():
    x, y, c = _place()
    barrier = pltpu.get_barrier_semaphore()
    for cx, cy in _other_chips(x, y):
        pl.semaphore_signal(barrier, inc=1, device_id=(cx, cy, c), device_id_type=MESH)
    pl.semaphore_wait(barrier, N_CHIPS - 1)


def _split_start(name, bufs, n_sems, copies, sibling_id=None, chips_id=None):
    n = len(bufs)
    assert sibling_id is None or chips_id is None

    def body(*refs):
        if sibling_id is not None:
            _sibling_handshake()
        if chips_id is not None:
            _chips_handshake()
        send_sems, recv_sems = refs[n:n + 2]
        for cp in copies(refs[n + 2:], send_sems, recv_sems):
            cp.start()

    outs = pl.pallas_call(
        body, name=name,
        in_specs=[_HBM] * n, out_specs=[_SEM, _SEM] + [_HBM] * n,
        out_shape=[pltpu.SemaphoreType.DMA((n_sems,)), pltpu.SemaphoreType.DMA((n_sems,))]
        + [pltpu.HBM(b.shape, b.dtype) for b in bufs],
        input_output_aliases={i: 2 + i for i in range(n)},
        compiler_params=pltpu.CompilerParams(has_side_effects=_EFFECT,
                                             collective_id=sibling_id if chips_id is None else chips_id),
    )(*[_in_hbm(b) for b in bufs])
    return outs[0], outs[1], list(outs[2:])


def _split_wait(name, bufs, send_sems, recv_sems, copies, after):
    n = len(bufs)
    after = tuple(after) if isinstance(after, (list, tuple)) else (after,)

    def body(*refs):
        send_ref, recv_ref = refs[n:n + 2]
        for cp in copies(refs[n + 2 + len(after):], send_ref, recv_ref):
            cp.wait()

    return list(pl.pallas_call(
        body, name=name,
        in_specs=[_HBM] * n + [_SEM, _SEM] + [ANY] * len(after), out_specs=[_HBM] * n,
        out_shape=[pltpu.HBM(b.shape, b.dtype) for b in bufs],
        input_output_aliases={i: i for i in range(n)},
        compiler_params=pltpu.CompilerParams(has_side_effects=_EFFECT),
    )(*bufs, send_sems, recv_sems, *after))


def _halves_copies(n_w):
    def copies(bufs, send_sems, recv_sems):
        x, y, c = _place()
        out = []
        for w in range(n_w):
            view, land = bufs[w], bufs[n_w + w]
            src = view.at[1 - c] if len(view.shape) == 3 else view.at[:, 1 - c]
            out.append(pltpu.make_async_remote_copy(
                src_ref=src, dst_ref=land, send_sem=send_sems.at[w], recv_sem=recv_sems.at[w],
                device_id=(x, y, 1 - c), device_id_type=MESH))
        return out
    return copies


def _pieces_copies(n_w):
    def copies(bufs, send_sems, recv_sems):
        x, y, c = _place()
        out = []
        for w in range(n_w):
            for j, (cx, cy) in enumerate(_other_chips(x, y)):
                out.append(pltpu.make_async_remote_copy(
                    src_ref=bufs[w].at[2 * cx + cy], dst_ref=bufs[n_w + w].at[j],
                    send_sem=send_sems.at[3 * w + j], recv_sem=recv_sems.at[3 * w + j],
                    device_id=(cx, cy, c), device_id_type=MESH))
        return out
    return copies


def _join_copies(n_w):
    def copies(bufs, send_sems, recv_sems):
        x, y, c = _place()
        return [pltpu.make_async_remote_copy(
            src_ref=bufs[w].at[c], dst_ref=bufs[w].at[c], send_sem=send_sems.at[w], recv_sem=recv_sems.at[w],
            device_id=(x, y, 1 - c), device_id_type=MESH) for w in range(n_w)]
    return copies


def _halves_landing(view):
    shape = view.shape[1:] if view.ndim == 3 else (N_CHIPS,) + view.shape[2:]
    return lax.empty(shape, BF16)


_SIBLING_IDS = {"halves_down": 1, "halves_gate_up": 2, "halves_out": 3, "halves_in": 4,
                "join_down": 5, "join_gate_up": 6, "join_out": 7, "join_in": 8,
                "diagonal_in": 9, "diagonal_gate": 10, "diagonal_up": 11, "forward_out": 12, "forward_down": 13,
                "pieces_down": 14, "pieces_gate_up": 15, "pieces_out": 16, "pieces_in": 17}


def _halves_start(tag, grads, column_sharded):
    views = [_weight_view(g, cs) for g, cs in zip(grads, column_sharded)]
    n = len(views)
    return _split_start("halves_start_" + tag, views + [_halves_landing(v) for v in views], n, _halves_copies(n),
                        sibling_id=_SIBLING_IDS["halves_" + tag])


def _halves_wait(tag, state, after):
    send_sems, recv_sems, bufs = state
    n = len(bufs) // 2
    bufs = _split_wait("halves_wait_" + tag, bufs, send_sems, recv_sems, _halves_copies(n), after)
    return bufs[:n], bufs[n:]


def _pieces_start(tag, pieces):
    n = len(pieces)
    landing = [lax.empty((3,) + p.shape[1:], BF16) for p in pieces]
    return _split_start("pieces_start_" + tag, list(pieces) + landing, 3 * n, _pieces_copies(n),
                        chips_id=_SIBLING_IDS["pieces_" + tag])


def _pieces_wait(tag, state, after):
    send_sems, recv_sems, bufs = state
    n = len(bufs) // 2
    bufs = _split_wait("pieces_wait_" + tag, bufs, send_sems, recv_sems, _pieces_copies(n), after)
    return bufs[:n], bufs[n:]


def _join_start(tag, shards):
    n = len(shards)
    return _split_start("join_start_" + tag, list(shards), n, _join_copies(n), sibling_id=_SIBLING_IDS["join_" + tag])


def _join_wait(tag, state, after):
    send_sems, recv_sems, bufs = state
    bufs = _split_wait("join_wait_" + tag, bufs, send_sems, recv_sems, _join_copies(len(bufs)), after)
    return [b.reshape(2 * b.shape[1], b.shape[2]) for b in bufs]


def _chip_sum_col(g3, sib, c_arr, name):
    _, hk, n = g3.shape
    cols = n // N_CHIPS
    tr = _row_tile(hk, cols * 2, limit=4 * 1024 * 1024)

    def body(c_ref, g_ref, s_ref, o_ref):
        del c_ref
        o_ref[...] = (g_ref[...].astype(F32) + s_ref[...].astype(F32)).astype(BF16)

    grid_spec = pltpu.PrefetchScalarGridSpec(
        num_scalar_prefetch=1, grid=(N_CHIPS, hk // tr),
        in_specs=[pl.BlockSpec((None, tr, cols), lambda p, r, c_ref: (c_ref[0], r, p)),
                  pl.BlockSpec((tr, cols), lambda p, r, c_ref: (r, p))],
        out_specs=pl.BlockSpec((None, tr, cols), lambda p, r, c_ref: (p, r, 0)))
    return pl.pallas_call(
        body, name=name, grid_spec=grid_spec,
        out_shape=jax.ShapeDtypeStruct((N_CHIPS, hk, cols), BF16),
        compiler_params=_params(("parallel", "parallel")),
    )(c_arr, g3, sib)


def _chip_sum_row(g4, sib, c_arr, name):
    _, _, hr, n = g4.shape
    tr = _row_tile(hr, n * 2, limit=4 * 1024 * 1024)

    def body(c_ref, g_ref, s_ref, o_ref):
        del c_ref
        o_ref[...] = (g_ref[...].astype(F32) + s_ref[...].astype(F32)).astype(BF16)

    grid_spec = pltpu.PrefetchScalarGridSpec(
        num_scalar_prefetch=1, grid=(N_CHIPS, hr // tr),
        in_specs=[pl.BlockSpec((None, None, tr, n), lambda p, r, c_ref: (p, c_ref[0], r, 0)),
                  pl.BlockSpec((None, tr, n), lambda p, r, c_ref: (p, r, 0))],
        out_specs=pl.BlockSpec((None, tr, n), lambda p, r, c_ref: (p, r, 0)))
    return pl.pallas_call(
        body, name=name, grid_spec=grid_spec,
        out_shape=jax.ShapeDtypeStruct((N_CHIPS, hr, n), BF16),
        compiler_params=_params(("parallel", "parallel")),
    )(c_arr, g4, sib)


def _sum_pieces(pieces, received, place_arr, name):
    _, r, n = pieces.shape
    tr = _row_tile(r, n * 4, limit=4 * 1024 * 1024)

    def body(p_ref, own_ref, r0_ref, r1_ref, r2_ref, o_ref):
        del p_ref
        acc = own_ref[...].astype(F32) + r0_ref[...].astype(F32)
        acc = acc + r1_ref[...].astype(F32)
        o_ref[...] = acc + r2_ref[...].astype(F32)

    def recv_spec(j):
        return pl.BlockSpec((None, tr, n), lambda i, p_ref: (j, i, 0))

    grid_spec = pltpu.PrefetchScalarGridSpec(
        num_scalar_prefetch=1, grid=(r // tr,),
        in_specs=[pl.BlockSpec((None, tr, n), lambda i, p_ref: (p_ref[0], i, 0)),
                  recv_spec(0), recv_spec(1), recv_spec(2)],
        out_specs=pl.BlockSpec((None, tr, n), lambda i, p_ref: (p_ref[1], i, 0)))
    return pl.pallas_call(
        body, name=name, grid_spec=grid_spec,
        out_shape=jax.ShapeDtypeStruct((2, r, n), F32),
        compiler_params=_params(("parallel",)),
    )(place_arr, pieces, received, received, received)


def _norm_weights_step(parts, w, m, v, after=()):
    rows, d = parts.shape
    after = tuple(after)

    def body(p_ref, w_ref, m_ref, v_ref, *rest):
        g_ref, d_ref, mo_ref, vo_ref, gathered, send_sems, recv_sems = rest[len(after):]
        x, y, c = _place()
        me = 4 * x + 2 * y + c
        gathered[me] = p_ref[...]
        copies = []
        for k in range(1, N_DEV):
            peer = (x ^ ((k >> 2) & 1), y ^ ((k >> 1) & 1), c ^ (k & 1))
            copies.append(pltpu.make_async_remote_copy(
                src_ref=p_ref, dst_ref=gathered.at[me], send_sem=send_sems.at[k - 1],
                recv_sem=recv_sems.at[k - 1], device_id=peer, device_id_type=MESH))
        for cp in copies:
            cp.start()
        for cp in copies:
            cp.wait()
        g = gathered[0]
        for k in range(1, N_DEV):
            g = g + gathered[k]
        delta, m_new, v_new = _adamw_math(w_ref[...], g, m_ref[...], v_ref[...])
        g_ref[...] = g
        d_ref[...] = delta
        mo_ref[...] = m_new
        vo_ref[...] = v_new

    vmem = pl.BlockSpec(memory_space=pltpu.VMEM)
    shp = jax.ShapeDtypeStruct((rows, d), F32)
    return pl.pallas_call(
        body, name="norm_weights_step",
        in_specs=[vmem] * 4 + [ANY] * len(after), out_specs=[vmem] * 4, out_shape=[shp] * 4,
        scratch_shapes=[pltpu.VMEM((N_DEV, rows, d), F32), pltpu.SemaphoreType.DMA((N_DEV - 1,)),
                        pltpu.SemaphoreType.DMA((N_DEV - 1,))],
        compiler_params=pltpu.CompilerParams(has_side_effects=True),
    )(parts, w, m, v, *after)


def kernel(x, norm_mix_w, w_in, w_out, norm_ffn_w, w_gate, w_up, w_down, norm_final_w, loss_target, m_norm_mix_w, m_w_in, m_w_out, m_norm_ffn_w, m_w_gate, m_w_up, m_w_down, m_norm_final_w, v_norm_mix_w, v_w_in, v_w_out, v_norm_ffn_w, v_w_gate, v_w_up, v_w_down, v_norm_final_w):
    s, d = x.shape[1], x.shape[2]
    xs = x.reshape(s, d)
    target = loss_target.reshape(s, d)
    big = {"w_in": (w_in, m_w_in, v_w_in), "w_out": (w_out, m_w_out, v_w_out),
           "w_gate": (w_gate, m_w_gate, v_w_gate), "w_up": (w_up, m_w_up, v_w_up),
           "w_down": (w_down, m_w_down, v_w_down)}
    big = {k: tuple(a.reshape(a.shape[1:]) for a in t) for k, t in big.items()}
    col_names, row_names = ("w_in", "w_gate", "w_up"), ("w_out", "w_down")
    n_in = N_CHIPS * big["w_in"][0].shape[1]
    ffn = N_CHIPS * big["w_gate"][0].shape[1]
    mix = ATTN_WIDTH + RET_WIDTH
    c_arr = lax.axis_index("c").astype(I32).reshape(1)
    shard_arr = (2 * lax.axis_index("x") + lax.axis_index("y")).astype(I32).reshape(1)
    place_arr = jnp.concatenate([shard_arr, c_arr])

    def cast(k, after=()):
        return _weight_view(_cast_into_full(big[k][0], shard_arr, k in col_names, "cast_" + k, after), k in col_names)

    started_in, v_in = _gather_in_start(cast("w_in"), "gather_in_start")

    sec = ATTN_WIDTH

    def section(p, rows):
        return pl.BlockSpec((None, rows, sec), lambda i, j, kk: (p, i, 0))

    h1 = _rms_fwd(xs, norm_mix_w, "rms_mix_fwd", after=[v_in])
    my_shard = shard_arr[0]
    shard_of = [jnp.bitwise_xor(my_shard, f).astype(I32).reshape(1) for f in (0,) + _FLIPS]
    proj = _in_proj_shard(h1, _weight_unview(v_in), None, shard_of[0], "in_proj_own")
    early_views = [cast(k, after=[proj]) for k in ("w_out", "w_gate")]
    v_up, v_down = [cast(k, after=[proj]) for k in ("w_up", "w_down")]
    relayed_in, (v_in,) = _gather_relay(v_in, started_in, 0, early_views + [v_up, v_down], "gather_in_relay")
    started_og, (v_out, v_gate) = _gather_out_gate_start(*early_views, [v_in], "gather_out_gate_start")
    v_in = _gather_in_neighbours_end(v_in, relayed_in, [v_out], "gather_in_neighbours_end")
    proj = _in_proj_shard(h1, _weight_unview(v_in), proj, shard_of[1], "in_proj_x")
    proj = _in_proj_shard(h1, _weight_unview(v_in), proj, shard_of[2], "in_proj_y")
    forwarded_in, v_in = _gather_in_diagonal(v_in, relayed_in, [proj], "gather_in_diagonal",
                                             _SIBLING_IDS["diagonal_in"])
    wi = _weight_unview(_gather_in_diagonal_end(v_in, forwarded_in, [proj], "gather_in_diagonal_end"))
    proj = _in_proj_shard(h1, wi, proj, shard_of[3], "in_proj_diagonal")
    fs_o, fr_o, v_out = _gather_forward([v_out], [0], *started_og, proj, "gather_forward_out",
                                        sibling_id=_SIBLING_IDS["forward_out"])
    mixed, attn_o, lse = _attn_fwd(proj, after=[v_out])
    relayed_g, (v_gate, v_up) = _gather_relay(v_gate, started_og, 3, [attn_o], "gather_gate_relay",
                                              then=v_up, then_peers=2, then_first=True)
    mixed, ret_raw = _ret_fwd(proj, mixed, after=[v_gate])
    wo, = _gather_end([v_out], fs_o, fr_o, ret_raw, "gather_end_out")
    x1, = _matmul("out_proj", "nn", [mixed, mixed], [wo, wo], [0, 0], s, d, sec, s // 2, 512, sec, [xs], [F32],
                  _epi_residual, b_koff=[0, 1], a_specs=[section(0, s // 2), section(1, s // 2)])
    h2 = _rms_fwd(x1, norm_ffn_w, "rms_ffn_fwd")
    relayed_u, (v_up, v_down) = _gather_relay(v_up, relayed_g, 6, [h2], "gather_up_relay",
                                              then=v_down, then_peers=3)
    v_gate = _gather_in_neighbours_end(v_gate, relayed_g, [v_up], "gather_gate_neighbours_end")
    forwarded_g, v_gate = _gather_in_diagonal(v_gate, relayed_g, [v_up], "gather_gate_diagonal",
                                              _SIBLING_IDS["diagonal_gate"])
    v_up = _gather_in_neighbours_end(v_up, relayed_u, [v_gate], "gather_up_neighbours_end")
    wg = _weight_unview(_gather_in_diagonal_end(v_gate, forwarded_g, [v_up], "gather_gate_diagonal_end"))
    forwarded_u, v_up = _gather_in_diagonal(v_up, relayed_u, [wg], "gather_up_diagonal",
                                            _SIBLING_IDS["diagonal_up"])
    wu = _weight_unview(_gather_in_diagonal_end(v_up, forwarded_u, [wg], "gather_up_diagonal_end"))
    gate, up, act = _matmul("gate_up", "nn", [h2, h2], [wg, wu], [0, 1], s, ffn, d, s, 512, d, [],
                            [BF16, BF16, BF16], _epi_swiglu, a_single_buffer=True)
    fs, fr, v_down = _gather_forward([v_down], [0], *relayed_u, act, "gather_forward_down", base=6,
                                     sibling_id=_SIBLING_IDS["forward_down"])
    wd, = _gather_end([v_down], fs, fr, act, "gather_end_down")
    x2, = _matmul("down_proj", "nn", [act], [wd], [0], s, d, ffn, s // 2, 512, ffn, [x1], [F32],
                  _epi_residual)
    loss_row, dx2, dx2b, dwf = _final_norm_loss(x2, norm_final_w.reshape(1, d), target, "final_norm_loss")

    names = col_names + row_names
    grads, new = {}, {}

    def chip_sums(tag_names, views, sibs):
        return [(_chip_sum_col if k in col_names else _chip_sum_row)(v, sb, c_arr, "chip_sum_" + k)
                for k, v, sb in zip(tag_names, views, sibs)]

    def piece_sums(tag_names, pieces, received):
        return [_sum_pieces(p, r, place_arr, "sum_pieces_" + k) for k, p, r in zip(tag_names, pieces, received)]

    def update(k):
        new[k] = _adamw(big[k][0], grads[k], big[k][1], big[k][2], "adamw_" + k)

    dgate, dup = _matmul("d_act", "nt", [dx2b], [wd], [0], s, ffn, d, s, 512, d, [gate, up],
                         [BF16, BF16], _epi_swiglu_bwd, a_single_buffer=True)
    g_wd, = _matmul("g_w_down", "tn", [act], [dx2b], [0], ffn, d, s, 512, d, s, [], [BF16], _epi_plain)
    halves_d = _halves_start("down", [g_wd], [False])
    dh2, = _matmul("d_h2", "nt", [dgate, dup], [wg, wu], [0, 0], s, d, ffn, s // 2, 256, ffn, [], [F32],
                   _epi_plain, after=halves_d[2][-1:], a_single_buffer=True)
    pieces_d = _pieces_start("down", chip_sums(["w_down"], *_halves_wait("down", halves_d, dh2)))
    g_wg, g_wu = _matmul("g_w_gate_up", "tn", [h2, h2], [dgate, dup], [0, 1], d, ffn, s, 1024, 512, s, [],
                         [BF16, BF16], _epi_two, after=pieces_d[2][-1:])
    halves_gu = _halves_start("gate_up", [g_wg, g_wu], [True, True])
    dx1, dx1b, dw_ffn = _rms_bwd(x1, norm_ffn_w, dh2, dx2, "rms_ffn_bwd", after=halves_gu[2][-1:])

    dmixed, = _matmul("d_mixed", "nt", [dx1b], [wo], [0], s, mix, d, s // 2, 512, d, [], [F32], _epi_plain)
    pieces_gu = _pieces_start("gate_up", chip_sums(["w_gate", "w_up"], *_halves_wait("gate_up", halves_gu, dmixed)))
    per = sec // 512
    g_wo, = _matmul("g_w_out", "tn", [mixed], [dx1b], [0], mix, d, s, 512, d, s, [], [BF16], _epi_plain,
                    after=pieces_gu[2][-1:],
                    a_specs=[pl.BlockSpec((None, s, 512), lambda i, j, kk: (i // per, 0, i % per))])
    halves_o = _halves_start("out", [g_wo], [False])
    dsec = _attn_bwd(proj, attn_o, lse, dmixed, after=halves_o[2][-1:])
    pieces_o = _pieces_start("out", chip_sums(["w_out"], *_halves_wait("out", halves_o, dsec)))
    dsec = _ret_bwd(proj, ret_raw, dmixed, dsec, after=pieces_o[2][-1:])
    where = [0, 1, 2, 4, 5, 6, 7]
    n_sec = len(where)
    g_wi, = _matmul("g_w_in", "tn", [h1], [dsec], [0], d, n_in, s, 1024, sec, s, [], [BF16], _epi_plain,
                    b_specs=[pl.BlockSpec((None, s, sec), lambda i, j, kk: (j + (j >= 3).astype(I32), 0, 0))])
    halves_i = _halves_start("in", [g_wi], [True])
    dh1, = _matmul("d_h1", "nt", [dsec] * n_sec, [wi] * n_sec, [0] * n_sec, s, d, sec, s // 2, 256, sec, [], [F32],
                   _epi_plain, b_koff=list(range(n_sec)), after=halves_i[2][-1:],
                   a_specs=[section(p, s // 2) for p in where])
    pieces_i = _pieces_start("in", chip_sums(["w_in"], *_halves_wait("in", halves_i, dh1)))
    grad_x, _, dw_mix = _rms_bwd(xs, norm_mix_w, dh1, dx1, "rms_mix_bwd", after=pieces_i[2][-1:])

    def rows8(*vs):
        return jnp.concatenate([v.reshape(1, d) for v in vs] + [jnp.zeros((8 - len(vs), d), F32)], axis=0)

    join_d = _join_start("down", piece_sums(["w_down"], *_pieces_wait("down", pieces_d, grad_x)))
    join_gu = _join_start("gate_up", piece_sums(["w_gate", "w_up"], *_pieces_wait("gate_up", pieces_gu, join_d[2][0])))
    join_o = _join_start("out", piece_sums(["w_out"], *_pieces_wait("out", pieces_o, join_gu[2][0])))
    grads["w_down"], = _join_wait("down", join_d, join_o[2][0])
    update("w_down")
    grads["w_gate"], grads["w_up"] = _join_wait("gate_up", join_gu, new["w_down"][0])
    update("w_gate")
    update("w_up")
    grads["w_out"], = _join_wait("out", join_o, new["w_up"][0])
    update("w_out")
    others_done = [new[k][0] for k in ("w_down", "w_gate", "w_up", "w_out")]
    join_i = _join_start("in", piece_sums(["w_in"], *_pieces_wait("in", pieces_i, others_done)))
    ng, nd, nm, nv = _norm_weights_step(
        rows8(dw_mix, dw_ffn, dwf, jnp.broadcast_to(loss_row[:, :1], (1, d))),
        rows8(norm_mix_w, norm_ffn_w, norm_final_w),
        rows8(m_norm_mix_w, m_norm_ffn_w, m_norm_final_w), rows8(v_norm_mix_w, v_norm_ffn_w, v_norm_final_w),
        after=join_i[2][:1])
    grads["w_in"], = _join_wait("in", join_i, ng)
    update("w_in")

    loss = ng[3, 0]

    def pack(small, per_weight):
        lead = lambda a: a.reshape((1,) + a.shape)
        return (small[0:1], lead(per_weight["w_in"]), lead(per_weight["w_out"]), small[1:2],
                lead(per_weight["w_gate"]), lead(per_weight["w_up"]), lead(per_weight["w_down"]), small[2])

    return (loss, grad_x.reshape(1, s, d),
            *pack(ng, {k: new[k][3] for k in names}),
            *pack(nd, {k: new[k][0] for k in names}),
            *pack(nm, {k: new[k][1] for k in names}),
            *pack(nv, {k: new[k][2] for k in names}))
```

```python
import functools
import math

import jax
import jax.numpy as jnp
from jax import lax
from jax.experimental import pallas as pl
from jax.experimental.pallas import tpu as pltpu

F32 = jnp.float32
BF16 = jnp.bfloat16
I32 = jnp.int32
MESH = pl.DeviceIdType.MESH
ANY = pl.BlockSpec(memory_space=pl.ANY)

ATTN_HEADS = 8
ATTN_HEAD_DIM = 128
RET_HEADS = 4
RET_HEAD_DIM = 256
ATTN_WIDTH = ATTN_HEADS * ATTN_HEAD_DIM
RET_WIDTH = RET_HEADS * RET_HEAD_DIM
DILATED_PATTERNS = ((128, 1), (512, 4), (2048, 16))
NORM_EPS = 1e-6
ADAM_LR = 0.001
ADAM_B1 = 0.9
ADAM_B2 = 0.999
ADAM_EPS = 1e-08
ADAM_WD = 0.01
ADAM_STEP = 10

N_CHIPS = 4
N_DEV = 8
NEG_BIG = -1e30
SEQ_TILE = 512
ATTN_FWD_HEADS_PER_STEP = 2
ATTN_HEADS_PER_STEP = 1
VMEM_LIMIT_BYTES = 56 * 1024 * 1024


def _params(semantics=None, vmem=VMEM_LIMIT_BYTES):
    return pltpu.CompilerParams(dimension_semantics=semantics, vmem_limit_bytes=vmem)


def _row_tile(rows, row_bytes, limit=2 * 1024 * 1024, mult=16):
    best = None
    for t in range(mult, rows + 1, mult):
        if rows % t == 0 and t * row_bytes <= limit:
            best = t
    assert best is not None, (rows, row_bytes)
    return best


def _sigmoid(x):
    return 1.0 / (1.0 + jnp.exp(-x))


def _select_by_index(idx, values):
    out = jnp.float32(values[-1])
    for i in range(len(values) - 2, -1, -1):
        out = jnp.where(idx == i, jnp.float32(values[i]), out)
    return out


def _place():
    x, y, c = lax.axis_index("x"), lax.axis_index("y"), lax.axis_index("c")
    return x, y, c


def _cast_into_full(w, shard_arr, column_sharded, name, after=()):
    after = tuple(after)
    rows, cols = w.shape
    tr = _row_tile(rows, cols * 4)
    steps = rows // tr
    if column_sharded:
        out_shape, out_map = (rows, N_CHIPS * cols), (lambda i, s_ref: (i, s_ref[0]))
    else:
        out_shape, out_map = (N_CHIPS * rows, cols), (lambda i, s_ref: (s_ref[0] * steps + i, 0))

    def body(s_ref, w_ref, *rest):
        del s_ref
        rest[-1][...] = w_ref[...].astype(BF16)

    grid_spec = pltpu.PrefetchScalarGridSpec(
        num_scalar_prefetch=1, grid=(steps,),
        in_specs=[pl.BlockSpec((tr, cols), lambda i, s_ref: (i, 0))] + [ANY] * len(after),
        out_specs=pl.BlockSpec((tr, cols), out_map))
    return pl.pallas_call(
        body, name=name, grid_spec=grid_spec,
        out_shape=jax.ShapeDtypeStruct(out_shape, BF16),
        compiler_params=_params(("parallel",)),
    )(shard_arr, w, *after)


def _rms_fwd(x, w, name, after=()):
    rows, d = x.shape
    tr = 256
    after = tuple(after)

    def body(x_ref, w_ref, *rest):
        xv = x_ref[...]
        r = lax.rsqrt(jnp.mean(xv * xv, axis=-1, keepdims=True) + NORM_EPS)
        rest[-1][...] = (xv * r * w_ref[...]).astype(BF16)

    return pl.pallas_call(
        body, name=name, grid=(rows // tr,),
        in_specs=[pl.BlockSpec((tr, d), lambda i: (i, 0)), pl.BlockSpec((1, d), lambda i: (0, 0))]
        + [ANY] * len(after),
        out_specs=pl.BlockSpec((tr, d), lambda i: (i, 0)),
        out_shape=jax.ShapeDtypeStruct((rows, d), BF16),
        compiler_params=_params(("parallel",)),
    )(x, w, *after)


def _rms_bwd(x, w, dh, dres, name, after=()):
    rows, d = x.shape
    tr = 256
    after = tuple(after)

    def body(x_ref, w_ref, dh_ref, dres_ref, *rest):
        dx_ref, dxb_ref, dw_ref = rest[len(after):]
        xv = x_ref[...]
        r = lax.rsqrt(jnp.mean(xv * xv, axis=-1, keepdims=True) + NORM_EPS)
        xhat = xv * r
        dy = dh_ref[...]
        dxhat = dy * w_ref[...]
        dx = dres_ref[...] + r * (dxhat - xhat * jnp.mean(dxhat * xhat, axis=-1, keepdims=True))
        dx_ref[...] = dx
        dxb_ref[...] = dx.astype(BF16)
        part = jnp.sum(dy * xhat, axis=0, keepdims=True)

        @pl.when(pl.program_id(0) == 0)
        def _():
            dw_ref[...] = part

        @pl.when(pl.program_id(0) != 0)
        def _():
            dw_ref[...] += part

    row = pl.BlockSpec((tr, d), lambda i: (i, 0))
    vec = pl.BlockSpec((1, d), lambda i: (0, 0))
    return pl.pallas_call(
        body, name=name, grid=(rows // tr,),
        in_specs=[row, vec, row, row] + [ANY] * len(after),
        out_specs=[row, row, vec],
        out_shape=[jax.ShapeDtypeStruct((rows, d), F32), jax.ShapeDtypeStruct((rows, d), BF16),
                   jax.ShapeDtypeStruct((1, d), F32)],
        compiler_params=_params(("arbitrary",)),
    )(x, w, dh, dres, *after)


def _final_norm_loss(x2, w, target, name):
    rows, d = x2.shape
    tr = 256

    def body(x_ref, w_ref, t_ref, loss_ref, dx_ref, dxb_ref, dw_ref):
        xv = x_ref[...]
        wv = w_ref[...]
        r = lax.rsqrt(jnp.mean(xv * xv, axis=-1, keepdims=True) + NORM_EPS)
        xhat = xv * r
        err = xhat * wv - t_ref[...]
        part_loss = 0.5 * jnp.sum(jnp.mean(err * err, axis=-1, keepdims=True), axis=0, keepdims=True)
        dy = err * (1.0 / d)
        dxhat = dy * wv
        dx = r * (dxhat - xhat * jnp.mean(dxhat * xhat, axis=-1, keepdims=True))
        dx_ref[...] = dx
        dxb_ref[...] = dx.astype(BF16)
        part_dw = jnp.sum(dy * xhat, axis=0, keepdims=True)
        part_loss = jnp.broadcast_to(part_loss, (1, 128))

        @pl.when(pl.program_id(0) == 0)
        def _():
            dw_ref[...] = part_dw
            loss_ref[...] = part_loss

        @pl.when(pl.program_id(0) != 0)
        def _():
            dw_ref[...] += part_dw
            loss_ref[...] += part_loss

    row = pl.BlockSpec((tr, d), lambda i: (i, 0))
    vec = pl.BlockSpec((1, d), lambda i: (0, 0))
    return pl.pallas_call(
        body, name=name, grid=(rows // tr,),
        in_specs=[row, vec, row],
        out_specs=[pl.BlockSpec((1, 128), lambda i: (0, 0)), row, row, vec],
        out_shape=[jax.ShapeDtypeStruct((1, 128), F32), jax.ShapeDtypeStruct((rows, d), F32),
                   jax.ShapeDtypeStruct((rows, d), BF16), jax.ShapeDtypeStruct((1, d), F32)],
        compiler_params=_params(("arbitrary",)),
    )(x2, w, target)


def _adamw_math(w, g, m, v):
    m = ADAM_B1 * m + (1.0 - ADAM_B1) * g
    v = ADAM_B2 * v + (1.0 - ADAM_B2) * (g * g)
    m_hat = m / (1.0 - ADAM_B1 ** ADAM_STEP)
    v_hat = v / (1.0 - ADAM_B2 ** ADAM_STEP)
    delta = -ADAM_LR * (m_hat / (jnp.sqrt(v_hat) + ADAM_EPS) + ADAM_WD * w)
    return delta, m, v


def _adamw(w, g, m, v, name):
    rows, cols = w.shape
    tr = _row_tile(rows, cols * 4)

    def body(w_ref, g_ref, m_ref, v_ref, d_ref, mo_ref, vo_ref, go_ref):
        g = g_ref[...]
        delta, m_new, v_new = _adamw_math(w_ref[...], g, m_ref[...], v_ref[...])
        d_ref[...] = delta
        mo_ref[...] = m_new
        vo_ref[...] = v_new
        go_ref[...] = g

    blk = pl.BlockSpec((tr, cols), lambda i: (i, 0))
    shp = jax.ShapeDtypeStruct((rows, cols), F32)
    return pl.pallas_call(
        body, name=name, grid=(rows // tr,),
        in_specs=[blk] * 4, out_specs=[blk] * 4, out_shape=[shp] * 4,
        compiler_params=_params(("parallel",)),
    )(w, g, m, v)


_DOT_DIMS = {"nn": ((1,), (0,)), "nt": ((1,), (1,)), "tn": ((0,), (0,))}


def _matmul(name, mode, a_list, b_list, acc_of, m, n, k, tm, tn, tk, extras, out_dtypes, epilogue,
            a_koff=None, b_koff=None, after=(), a_specs=None, b_specs=None, a_single_buffer=False):
    after = tuple(after)
    assert m % tm == 0 and n % tn == 0 and k % tk == 0, (name, m, n, k, tm, tn, tk)
    nk = k // tk
    n_acc = max(acc_of) + 1
    n_pairs = len(a_list)
    a_koff = a_koff or [0] * n_pairs
    b_koff = b_koff or [0] * n_pairs
    dims = (_DOT_DIMS[mode], ((), ()))
    n_ext, n_out = len(extras), len(out_dtypes)

    def body(*refs):
        a_refs = refs[:n_pairs]
        b_refs = refs[n_pairs:2 * n_pairs]
        e_refs = refs[2 * n_pairs:2 * n_pairs + n_ext]
        first_out = 2 * n_pairs + n_ext + len(after)
        o_refs = refs[first_out:first_out + n_out]
        acc_refs = refs[first_out + n_out:]

        parts = [None] * n_acc
        for p in range(n_pairs):
            d = lax.dot_general(a_refs[p][...], b_refs[p][...], dims, preferred_element_type=F32)
            parts[acc_of[p]] = d if parts[acc_of[p]] is None else parts[acc_of[p]] + d

        def finish(accs):
            outs = epilogue(accs, [e[...] for e in e_refs])
            for o_ref, o in zip(o_refs, outs):
                o_ref[...] = o.astype(o_ref.dtype)

        if nk == 1:
            finish(parts)
        else:
            kk = pl.program_id(2)

            @pl.when(kk == 0)
            def _():
                for acc_ref, part in zip(acc_refs, parts):
                    acc_ref[...] = part

            @pl.when(kk != 0)
            def _():
                for acc_ref, part in zip(acc_refs, parts):
                    acc_ref[...] += part

            @pl.when(kk == nk - 1)
            def _():
                finish([acc_ref[...] for acc_ref in acc_refs])

    def a_spec(off):
        mode_a = pl.Buffered(1) if a_single_buffer else None
        if mode == "tn":
            return pl.BlockSpec((tk, tm), lambda i, j, kk: (kk + off, i), pipeline_mode=mode_a)
        return pl.BlockSpec((tm, tk), lambda i, j, kk: (i, kk + off), pipeline_mode=mode_a)

    def b_spec(off):
        if mode == "nt":
            return pl.BlockSpec((tn, tk), lambda i, j, kk: (j, kk + off))
        return pl.BlockSpec((tk, tn), lambda i, j, kk: (kk + off, j))

    tile = pl.BlockSpec((tm, tn), lambda i, j, kk: (i, j))
    scratch = [pltpu.VMEM((tm, tn), F32) for _ in range(n_acc)] if nk > 1 else []
    return pl.pallas_call(
        body, name=name, grid=(m // tm, n // tn, nk),
        in_specs=(a_specs or [a_spec(o) for o in a_koff]) + (b_specs or [b_spec(o) for o in b_koff])
        + [tile] * n_ext + [ANY] * len(after),
        out_specs=[tile] * n_out,
        out_shape=[jax.ShapeDtypeStruct((m, n), dt) for dt in out_dtypes],
        scratch_shapes=scratch,
        compiler_params=_params(("parallel", "parallel", "arbitrary")),
    )(*a_list, *b_list, *extras, *after)


def _d_h2_streamed(dgate, dup, wg, wu, after=()):
    after = tuple(after)
    s, ffn = dgate.shape
    d = wg.shape[0]
    tm, tn, n_buf = s // 2, 256, 3
    ni, nj = s // tm, d // tn
    total = ni * nj

    def body(a0_ref, a1_ref, w0_ref, w1_ref, *rest):
        o_ref, buf, sems = rest[len(after):]
        step = pl.program_id(0) * nj + pl.program_id(1)
        weights = (w0_ref, w1_ref)

        def fetch(of_step, slot):
            rows = pl.ds(pl.multiple_of(lax.rem(of_step, nj) * tn, tn), tn)
            return [pltpu.make_async_copy(weights[p].at[rows, :], buf.at[p, slot], sems.at[p, slot]) for p in range(2)]

        @pl.when(step == 0)
        def _():
            for k in range(min(n_buf, total)):
                for cp in fetch(k, k):
                    cp.start()

        slot = lax.rem(step, n_buf)
        for cp in fetch(step, slot):
            cp.wait()
        acc = lax.dot_general(a0_ref[...], buf[0, slot], _NT_DIMS, preferred_element_type=F32)
        o_ref[...] = acc + lax.dot_general(a1_ref[...], buf[1, slot], _NT_DIMS, preferred_element_type=F32)

        @pl.when(step + n_buf < total)
        def _():
            for cp in fetch(step + n_buf, slot):
                cp.start()

    a_spec = pl.BlockSpec((tm, ffn), lambda i, j: (i, 0), pipeline_mode=pl.Buffered(1))
    return pl.pallas_call(
        body, name="d_h2", grid=(ni, nj),
        in_specs=[a_spec, a_spec, ANY, ANY] + [ANY] * len(after),
        out_specs=pl.BlockSpec((tm, tn), lambda i, j: (i, j)),
        out_shape=jax.ShapeDtypeStruct((s, d), F32),
        scratch_shapes=[pltpu.VMEM((2, n_buf, tn, ffn), BF16), pltpu.SemaphoreType.DMA((2, n_buf))],
        compiler_params=_params(("arbitrary", "arbitrary")),
    )(dgate, dup, wg, wu, *after)


def _epi_plain(accs, extras):
    return (accs[0],)


def _epi_residual(accs, extras):
    return (accs[0] + extras[0],)


def _epi_two(accs, extras):
    return accs[0], accs[1]


def _epi_swiglu(accs, extras):
    g, u = accs
    return g, u, g * _sigmoid(g) * u


def _epi_swiglu_bwd(accs, extras):
    da = accs[0]
    g, u = (e.astype(F32) for e in extras)
    sg = _sigmoid(g)
    dg = da * u * sg * (1.0 + g * (1.0 - sg))
    du = da * g * sg
    return dg, du


_NT_DIMS = (((1,), (1,)), ((), ()))
_TN_DIMS = (((0,), (0,)), ((), ()))


def _tile_delta(tq, tk):
    return lax.broadcasted_iota(I32, (tq, tk), 0) - lax.broadcasted_iota(I32, (tq, tk), 1)


def _attn_log_count(delta):
    count = jnp.zeros(delta.shape, I32)
    for window, dilation in DILATED_PATTERNS:
        hit = ((delta & (dilation - 1)) == 0) & (delta <= window)
        count = count + jnp.where(hit, 1, 0)
    valid = (delta >= 0) & (count > 0)
    logm = jnp.where(count == 3, math.log(3.0), jnp.where(count == 2, math.log(2.0), 0.0))
    return jnp.where(valid, logm, NEG_BIG)


def _fill_attn_log_count(tab_ref):
    nb, t, _ = tab_ref.shape
    base = _tile_delta(t, t)
    for b in range(nb):
        tab_ref[b] = _attn_log_count(base + b * t)


def _fill_attn_bias(tab_ref, log_count_ref, slope):
    nb, t, _ = tab_ref.shape
    dist = _tile_delta(t, t).astype(F32)
    for b in range(nb):
        tab_ref[b] = log_count_ref[b] - slope * (dist + float(b * t))


def _fill_ret_decay(tab_ref, log_gamma):
    nb, t, _ = tab_ref.shape
    base = _tile_delta(t, t)
    for b in range(nb):
        tab_ref[b] = _ret_decay(base + b * t, log_gamma)


def _alibi_slopes():
    return [2.0 ** (-8.0 * (h + 1) / ATTN_HEADS) for h in range(ATTN_HEADS)]


def _attn_fwd(proj, after=()):
    s = proj.shape[0]
    t = SEQ_TILE
    hd = ATTN_HEAD_DIM
    hp = ATTN_FWD_HEADS_PER_STEP
    ng = ATTN_HEADS // hp
    w = hp * hd
    scale = 1.0 / math.sqrt(hd)
    slopes = _alibi_slopes()

    def body(q_ref, k_ref, v_ref, *rest):
        mix_ref, o_ref, lse_ref, kb, vb, bias_tab, log_count_tab = rest[len(after):]
        g = pl.program_id(0)
        i = pl.program_id(1)

        @pl.when((g == 0) & (i == 0))
        def _():
            _fill_attn_log_count(log_count_tab)

        @pl.when(i == 0)
        def _():
            kb[...] = k_ref[...].astype(BF16)
            vb[...] = v_ref[...].astype(BF16)
            for u in range(hp):
                _fill_attn_bias(bias_tab.at[u], log_count_tab, _select_by_index(g * hp + u, slopes))

        qs = [q_ref[:, u * hd:(u + 1) * hd].astype(BF16) for u in range(hp)]

        def step(j, carry):
            rows = pl.ds(pl.multiple_of(j * t, t), t)
            out = []
            for u in range(hp):
                m_i, l_i, acc = carry[u]
                lanes = slice(u * hd, (u + 1) * hd)
                sc = lax.dot_general(qs[u], kb[rows, lanes], _NT_DIMS, preferred_element_type=F32) * scale
                sc = sc + bias_tab[u, i - j]
                m_new = jnp.maximum(m_i, jnp.max(sc, axis=-1, keepdims=True))
                p = jnp.exp(sc - m_new)
                alpha = jnp.exp(m_i - m_new)
                l_new = alpha * l_i + jnp.sum(p, axis=-1, keepdims=True)
                acc = alpha * acc + jnp.dot(p.astype(BF16), vb[rows, lanes], preferred_element_type=F32)
                out.append((m_new, l_new, acc))
            return tuple(out)

        init = (jnp.full((t, 1), NEG_BIG, F32), jnp.zeros((t, 1), F32), jnp.zeros((t, hd), F32))
        final = lax.fori_loop(0, i + 1, step, (init,) * hp)
        for u in range(hp):
            m_i, l_i, acc = final[u]
            lanes = slice(u * hd, (u + 1) * hd)
            out = acc / l_i
            o_ref[:, lanes] = out
            mix_ref[:, lanes] = out.astype(BF16)
            lse_ref[:, lanes] = jnp.broadcast_to(m_i + jnp.log(l_i), (t, hd))

    return pl.pallas_call(
        body, name="attn_fwd", grid=(ng, s // t),
        in_specs=[pl.BlockSpec((t, w), lambda g, i: (i, g)),
                  pl.BlockSpec((s, w), lambda g, i: (0, ng + g)),
                  pl.BlockSpec((s, w), lambda g, i: (0, 2 * ng + g))] + [ANY] * len(after),
        out_specs=[pl.BlockSpec((None, t, w), lambda g, i: (0, i, g))] + [pl.BlockSpec((t, w), lambda g, i: (i, g))] * 2,
        out_shape=[jax.ShapeDtypeStruct((2, s, ATTN_WIDTH), BF16),
                   jax.ShapeDtypeStruct((s, ATTN_WIDTH), F32),
                   jax.ShapeDtypeStruct((s, ATTN_WIDTH), F32)],
        scratch_shapes=[pltpu.VMEM((s, w), BF16), pltpu.VMEM((s, w), BF16), pltpu.VMEM((hp, s // t, t, t), F32),
                        pltpu.VMEM((s // t, t, t), F32)],
        compiler_params=_params(("arbitrary", "arbitrary")),
    )(proj, proj, proj, *after)


def _attn_bwd(proj, attn_out, lse, dmixed, after=()):
    after = tuple(after)
    s = proj.shape[0]
    t = SEQ_TILE
    nt = s // t
    hd = ATTN_HEAD_DIM
    hp = ATTN_HEADS_PER_STEP
    ng = ATTN_HEADS // hp
    w = hp * hd
    scale = 1.0 / math.sqrt(hd)
    slopes = _alibi_slopes()

    def body(q_ref, k_ref, v_ref, o_ref, lse_ref, do_ref, *rest):
        dsec_ref, qb, kb, vb, dob, dsum, dq_acc, bias_tab, log_count_tab = rest[len(after):]
        g = pl.program_id(0)

        @pl.when(g == 0)
        def _():
            _fill_attn_log_count(log_count_tab)

        qb[...] = q_ref[...].astype(BF16)
        kb[...] = k_ref[...].astype(BF16)
        vb[...] = v_ref[...].astype(BF16)
        dob[...] = do_ref[...].astype(BF16)
        for u in range(hp):
            lanes = slice(u * hd, (u + 1) * hd)
            _fill_attn_bias(bias_tab.at[u], log_count_tab, _select_by_index(g * hp + u, slopes))
            rowsum = jnp.sum(do_ref[:, lanes] * o_ref[:, lanes], axis=-1, keepdims=True)
            dsum[:, lanes] = jnp.broadcast_to(rowsum, (s, hd))
        dq_acc[...] = jnp.zeros((s, w), F32)

        def over_keys(j, _):
            krows = pl.ds(pl.multiple_of(j * t, t), t)

            def over_queries(i, carry):
                qrows = pl.ds(pl.multiple_of(i * t, t), t)
                out = []
                for u in range(hp):
                    dk, dv = carry[u]
                    lanes = slice(u * hd, (u + 1) * hd)
                    qi, doi = qb[qrows, lanes], dob[qrows, lanes]
                    kj, vj = kb[krows, lanes], vb[krows, lanes]
                    lse_i = lse_ref[qrows, lanes][:, :1]
                    dsum_i = dsum[qrows, lanes][:, :1]
                    sc = lax.dot_general(qi, kj, _NT_DIMS, preferred_element_type=F32) * scale
                    p = jnp.exp(sc + bias_tab[u, i - j] - lse_i)
                    dp = lax.dot_general(doi, vj, _NT_DIMS, preferred_element_type=F32)
                    ds = (p * (dp - dsum_i)).astype(BF16)
                    dv = dv + lax.dot_general(p.astype(BF16), doi, _TN_DIMS, preferred_element_type=F32)
                    dk = dk + lax.dot_general(ds, qi, _TN_DIMS, preferred_element_type=F32)
                    dq_acc[qrows, lanes] += jnp.dot(ds, kj, preferred_element_type=F32)
                    out.append((dk, dv))
                return tuple(out)

            zero = jnp.zeros((t, hd), F32)
            final = lax.fori_loop(j, nt, over_queries, ((zero, zero),) * hp)
            for u in range(hp):
                lanes = slice(u * hd, (u + 1) * hd)
                dsec_ref[1, krows, lanes] = (final[u][0] * scale).astype(BF16)
                dsec_ref[2, krows, lanes] = final[u][1].astype(BF16)
            return 0

        lax.fori_loop(0, nt, over_keys, 0)
        dsec_ref[0] = (dq_acc[...] * scale).astype(BF16)

    def col(off):
        return pl.BlockSpec((s, w), lambda g: (0, off + g))

    return pl.pallas_call(
        body, name="attn_bwd", grid=(ng,),
        in_specs=[col(0), col(ng), col(2 * ng), col(0), col(0), col(0)] + [ANY] * len(after),
        out_specs=pl.BlockSpec((4, s, w), lambda g: (0, 0, g)),
        out_shape=jax.ShapeDtypeStruct((8, s, ATTN_WIDTH), BF16),
        scratch_shapes=[pltpu.VMEM((s, w), BF16)] * 4 + [pltpu.VMEM((s, w), F32)] * 2
        + [pltpu.VMEM((hp, nt, t, t), F32), pltpu.VMEM((nt, t, t), F32)],
        compiler_params=_params(("arbitrary",)),
    )(proj, proj, proj, attn_out, lse, dmixed, *after)


def _ret_log_gammas():
    return [math.log(1.0 - 2.0 ** (-5.0 - h)) for h in range(RET_HEADS)]


def _ret_decay(delta, log_gamma):
    dec = jnp.exp(delta.astype(F32) * log_gamma) * (1.0 / math.sqrt(RET_HEAD_DIM))
    return jnp.where(delta >= 0, dec, 0.0)


def _ret_fwd(proj, mixed, after=()):
    after = tuple(after)
    s = proj.shape[0]
    t = SEQ_TILE
    hd = RET_HEAD_DIM
    nh = RET_HEADS
    log_gammas = _ret_log_gammas()
    c0 = 3 * ATTN_WIDTH // hd

    def body(q_ref, k_ref, v_ref, g_ref, *rest):
        mix_ref, raw_ref, kb, vb, decay_tab = rest[1 + len(after):]
        h = pl.program_id(0)
        i = pl.program_id(1)

        @pl.when(i == 0)
        def _():
            kb[...] = k_ref[...].astype(BF16)
            vb[...] = v_ref[...].astype(BF16)
            _fill_ret_decay(decay_tab, _select_by_index(h, log_gammas))

        q = q_ref[...].astype(BF16)

        def step(j, acc):
            rows = pl.ds(pl.multiple_of(j * t, t), t)
            sc = lax.dot_general(q, kb[rows, :], _NT_DIMS, preferred_element_type=F32) * decay_tab[i - j]
            return acc + jnp.dot(sc.astype(BF16), vb[rows, :], preferred_element_type=F32)

        ret = lax.fori_loop(0, i + 1, step, jnp.zeros((t, hd), F32))
        raw_ref[...] = ret
        r = lax.rsqrt(jnp.mean(ret * ret, axis=-1, keepdims=True) + NORM_EPS)
        g = g_ref[...]
        mix_ref[...] = (g * _sigmoid(g) * (ret * r)).astype(BF16)

    return pl.pallas_call(
        body, name="ret_fwd", grid=(nh, s // t),
        in_specs=[pl.BlockSpec((t, hd), lambda h, i: (i, c0 + h)),
                  pl.BlockSpec((s, hd), lambda h, i: (0, c0 + nh + h)),
                  pl.BlockSpec((s, hd), lambda h, i: (0, c0 + 2 * nh + h)),
                  pl.BlockSpec((t, hd), lambda h, i: (i, c0 + 3 * nh + h))] + [ANY] * (1 + len(after)),
        out_specs=[pl.BlockSpec((None, t, hd), lambda h, i: (1, i, h)), pl.BlockSpec((t, hd), lambda h, i: (i, h))],
        out_shape=[jax.ShapeDtypeStruct(mixed.shape, BF16), jax.ShapeDtypeStruct((s, RET_WIDTH), F32)],
        input_output_aliases={4: 0},
        scratch_shapes=[pltpu.VMEM((s, hd), BF16), pltpu.VMEM((s, hd), BF16), pltpu.VMEM((s // t, t, t), F32)],
        compiler_params=_params(("arbitrary", "arbitrary")),
    )(proj, proj, proj, proj, mixed, *after)


def _ret_bwd(proj, ret_raw, dmixed, dsec, after=()):
    after = tuple(after)
    s = proj.shape[0]
    t = SEQ_TILE
    nt = s // t
    hd = RET_HEAD_DIM
    nh = RET_HEADS
    log_gammas = _ret_log_gammas()
    c0 = 3 * ATTN_WIDTH // hd
    mixed_blocks = ATTN_WIDTH // hd

    def body(q_ref, k_ref, v_ref, g_ref, raw_ref, dmix_ref, *rest):
        dsec_ref, qb, kb, vb, dretb, dq_acc, decay_tab = rest[1 + len(after):]
        h = pl.program_id(0)
        _fill_ret_decay(decay_tab, _select_by_index(h, log_gammas))
        qb[...] = q_ref[...].astype(BF16)
        kb[...] = k_ref[...].astype(BF16)
        vb[...] = v_ref[...].astype(BF16)
        ret = raw_ref[...]
        r = lax.rsqrt(jnp.mean(ret * ret, axis=-1, keepdims=True) + NORM_EPS)
        normed = ret * r
        g = g_ref[...]
        sg = _sigmoid(g)
        dout = dmix_ref[...]
        dsec_ref[3] = (dout * normed * sg * (1.0 + g * (1.0 - sg))).astype(BF16)
        dn = dout * g * sg
        dret = r * (dn - normed * jnp.mean(dn * normed, axis=-1, keepdims=True))
        dretb[...] = dret.astype(BF16)
        dq_acc[...] = jnp.zeros((s, hd), F32)

        def over_keys(j, _):
            krows = pl.ds(pl.multiple_of(j * t, t), t)
            kj = kb[krows, :]
            vj = vb[krows, :]

            def over_queries(i, carry):
                dk, dv = carry
                qrows = pl.ds(pl.multiple_of(i * t, t), t)
                qi = qb[qrows, :]
                doi = dretb[qrows, :]
                dec = decay_tab[i - j]
                a = (lax.dot_general(qi, kj, _NT_DIMS, preferred_element_type=F32) * dec).astype(BF16)
                da = (lax.dot_general(doi, vj, _NT_DIMS, preferred_element_type=F32) * dec).astype(BF16)
                dv = dv + lax.dot_general(a, doi, _TN_DIMS, preferred_element_type=F32)
                dk = dk + lax.dot_general(da, qi, _TN_DIMS, preferred_element_type=F32)
                dq_acc[qrows, :] += jnp.dot(da, kj, preferred_element_type=F32)
                return dk, dv

            zero = jnp.zeros((t, hd), F32)
            dk, dv = lax.fori_loop(j, nt, over_queries, (zero, zero))
            dsec_ref[1, krows, :] = dk.astype(BF16)
            dsec_ref[2, krows, :] = dv.astype(BF16)
            return 0

        lax.fori_loop(0, nt, over_keys, 0)
        dsec_ref[0] = dq_acc[...].astype(BF16)

    def col(off):
        return pl.BlockSpec((s, hd), lambda h: (0, off + h))

    return pl.pallas_call(
        body, name="ret_bwd", grid=(nh,),
        in_specs=[col(c0), col(c0 + nh), col(c0 + 2 * nh), col(c0 + 3 * nh), col(0), col(mixed_blocks)]
        + [ANY] * (1 + len(after)),
        out_specs=pl.BlockSpec((4, s, hd), lambda h: (1, 0, h)),
        out_shape=jax.ShapeDtypeStruct(dsec.shape, BF16),
        input_output_aliases={6: 0},
        scratch_shapes=[pltpu.VMEM((s, hd), BF16)] * 4 + [pltpu.VMEM((s, hd), F32)]
        + [pltpu.VMEM((nt, t, t), F32)],
        compiler_params=_params(("arbitrary",)),
    )(proj, proj, proj, proj, ret_raw, dmixed, dsec, *after)


_FLIPS = (2, 1, 3)


def _other_chips(x, y):
    return [(1 - x, y), (x, 1 - y), (1 - x, 1 - y)]


_HBM = pl.BlockSpec(memory_space=pltpu.HBM)
_SEM = pl.BlockSpec(memory_space=pltpu.SEMAPHORE)
_EFFECT = pltpu.SideEffectType.DATAFLOW_SIDE_EFFECTING


def _in_hbm(a):
    return pltpu.with_memory_space_constraint(a, pltpu.HBM)


def _weight_view(w, column_sharded):
    if column_sharded:
        return w.reshape(2, w.shape[0] // 2, w.shape[1])
    return w.reshape(N_CHIPS, 2, w.shape[0] // (2 * N_CHIPS), w.shape[1])


def _weight_unview(v):
    if v.ndim == 3:
        return v.reshape(2 * v.shape[1], v.shape[2])
    return v.reshape(N_CHIPS * 2 * v.shape[2], v.shape[3])


def _weight_region(buf, shard, half):
    if len(buf.shape) == 3:
        cols = buf.shape[2] // N_CHIPS
        return buf.at[half, :, pl.ds(shard * cols, cols)]
    return buf.at[shard, half]


def _remote(where, send_sem, recv_sem, to):
    return pltpu.make_async_remote_copy(src_ref=where, dst_ref=where, send_sem=send_sem, recv_sem=recv_sem,
                                        device_id=to, device_id_type=MESH)


def _for_my_shard(fn):
    x, y, _ = _place()
    for ss in range(N_CHIPS):
        pl.when(2 * x + y == ss)(functools.partial(fn, ss))


def _gather_forward(views, which, send_sems, recv_sems, after, name, base=0, sibling_id=None):
    n_w = len(views)
    which = [base // 3 + w for w in which] if base % 3 == 0 else None
    assert which is not None, "base must be a multiple of 3"

    def body(*refs):
        if sibling_id is not None:
            _sibling_handshake()
        send_in, recv_in = refs[n_w:n_w + 2]
        fwd_send, fwd_recv = refs[n_w + 3:n_w + 5]
        bufs = refs[n_w + 5:]
        x, y, c = _place()
        sibling = (x, y, 1 - c)

        def forward(ss):
            for i, w in enumerate(which):
                for j in range(3):
                    landed = _weight_region(bufs[i], ss ^ _FLIPS[j], c)
                    _remote(landed, send_in.at[3 * w + j], recv_in.at[3 * w + j], sibling).wait_recv()
                    _remote(landed, fwd_send.at[3 * i + j], fwd_recv.at[3 * i + j], sibling).start()

        _for_my_shard(forward)
        for i, w in enumerate(which):
            for j in range(3):
                _remote(_weight_region(bufs[i], 0, 0), send_in.at[3 * w + j], recv_in.at[3 * w + j],
                        sibling).wait_send()

    return pl.pallas_call(
        body, name=name,
        in_specs=[_HBM] * n_w + [_SEM, _SEM, ANY], out_specs=[_SEM, _SEM] + [_HBM] * n_w,
        out_shape=[pltpu.SemaphoreType.DMA((3 * n_w,)), pltpu.SemaphoreType.DMA((3 * n_w,))]
        + [pltpu.HBM(v.shape, BF16) for v in views],
        input_output_aliases={w: 2 + w for w in range(n_w)},
        compiler_params=pltpu.CompilerParams(has_side_effects=_EFFECT, collective_id=sibling_id),
    )(*views, send_sems, recv_sems, after)


def _gather_end(views, fwd_send, fwd_recv, after, name):
    n_w = len(views)

    def body(*refs):
        fwd_send_ref, fwd_recv_ref = refs[n_w:n_w + 2]
        bufs = refs[n_w + 3:]
        x, y, c = _place()
        for i in range(n_w):
            for j in range(3):
                cp = _remote(_weight_region(bufs[i], 0, 0), fwd_send_ref.at[3 * i + j], fwd_recv_ref.at[3 * i + j],
                             (x, y, 1 - c))
                cp.wait_recv()
                cp.wait_send()

    outs = pl.pallas_call(
        body, name=name,
        in_specs=[_HBM] * n_w + [_SEM, _SEM, ANY], out_specs=[_HBM] * n_w,
        out_shape=[pltpu.HBM(v.shape, BF16) for v in views],
        input_output_aliases={w: w for w in range(n_w)},
        compiler_params=pltpu.CompilerParams(has_side_effects=_EFFECT),
    )(*views, fwd_send, fwd_recv, after)
    return [_weight_unview(o) for o in outs]


def _comm_call(name, bufs, sem_pairs, after, n_new, fn, sibling_id=None):
    n, n_sem, after = len(bufs), 2 * len(sem_pairs), tuple(after)
    n_out_sem = 2 if n_new else 0

    def body(*refs):
        if sibling_id is not None:
            _sibling_handshake()
        sems = refs[n:n + n_sem]
        outs = refs[n + n_sem + len(after):]
        new = outs[:n_out_sem] if n_new else (None, None)
        fn(outs[n_out_sem:], [(sems[2 * i], sems[2 * i + 1]) for i in range(len(sem_pairs))], *new)

    res = pl.pallas_call(
        body, name=name,
        in_specs=[_HBM] * n + [_SEM] * n_sem + [ANY] * len(after),
        out_specs=[_SEM] * n_out_sem + [_HBM] * n,
        out_shape=[pltpu.SemaphoreType.DMA((n_new,))] * n_out_sem + [pltpu.HBM(b.shape, b.dtype) for b in bufs],
        input_output_aliases={i: n_out_sem + i for i in range(n)},
        compiler_params=pltpu.CompilerParams(has_side_effects=_EFFECT, collective_id=sibling_id),
    )(*bufs, *[s for pair in sem_pairs for s in pair], *after)
    return list(res[:n_out_sem]), list(res[n_out_sem:])


def _quarter(piece, q):
    rows = piece.shape[0] // 2
    return piece.at[pl.ds(q * rows, rows)]


def _gather_in_start(view, name):
    def fn(bufs, _, send, recv):
        x, y, c = _place()

        def go(ss):
            for j, chip in enumerate(_other_chips(x, y)[:2]):
                _remote(_weight_region(bufs[0], ss, c), send.at[j], recv.at[j], (*chip, c)).start()

        _for_my_shard(go)

    sems, (view,) = _comm_call(name, [_in_hbm(view)], [], (), 2, fn)
    return sems, view


def _gather_out_gate_start(v_out, v_gate, after, name):
    def fn(bufs, _, send, recv):
        x, y, c = _place()
        chips = _other_chips(x, y)

        def go(ss):
            for j in range(3):
                _remote(_weight_region(bufs[0], ss, c), send.at[j], recv.at[j], (*chips[j], c)).start()
            for j in range(2):
                _remote(_weight_region(bufs[1], ss, c), send.at[3 + j], recv.at[3 + j], (*chips[j], c)).start()

        _for_my_shard(go)

    sems, views = _comm_call(name, [_in_hbm(v_out), _in_hbm(v_gate)], [], after, 5, fn)
    return sems, views


def _gather_relay(view, started, base, after, name, then=None, then_peers=0, then_first=False):
    n_new = 6 + then_peers if then_peers else 4

    def fn(bufs, pairs, send, recv):
        (send_in, recv_in), = pairs
        x, y, c = _place()
        chips = _other_chips(x, y)
        sibling = (x, y, 1 - c)

        def go(ss):
            def start_then():
                for j in range(then_peers):
                    _remote(_weight_region(bufs[1], ss, c), send.at[6 + j], recv.at[6 + j], (*chips[j], c)).start()

            if then_first:
                start_then()
            landed = [_weight_region(bufs[0], ss ^ _FLIPS[j], c) for j in range(2)]
            for j in range(2):
                _remote(landed[j], send_in.at[base + j], recv_in.at[base + j], sibling).wait_recv()
            for j in range(2):
                _remote(_quarter(landed[j], j), send.at[j], recv.at[j], (*chips[1 - j], c)).start()
            for j in range(2):
                _remote(landed[j], send.at[2 + j], recv.at[2 + j], sibling).start()
            if not then_first:
                start_then()

        _for_my_shard(go)
        for j in range(2):
            _remote(_weight_region(bufs[0], 0, 0), send_in.at[base + j], recv_in.at[base + j], sibling).wait_send()

    views = [view] if then is None else [view, _in_hbm(then)]
    sems, views = _comm_call(name, views, [started], after, n_new, fn)
    return sems, views


def _gather_in_neighbours_end(view, relayed, after, name):
    def fn(bufs, pairs, *_):
        (send, recv), = pairs
        x, y, c = _place()
        for j in range(2):
            cp = _remote(_weight_region(bufs[0], 0, 0), send.at[2 + j], recv.at[2 + j], (x, y, 1 - c))
            cp.wait_recv()
            cp.wait_send()

    _, (view,) = _comm_call(name, [view], [relayed], after, 0, fn)
    return view


def _gather_in_diagonal(view, relayed, after, name, sibling_id):
    def fn(bufs, pairs, send, recv):
        (send_in, recv_in), = pairs
        x, y, c = _place()
        sibling = (x, y, 1 - c)
        any_quarter = _quarter(_weight_region(bufs[0], 0, 0), 0)
        for j in range(2):
            cp = _remote(any_quarter, send_in.at[j], recv_in.at[j], sibling)
            cp.wait_recv()
            cp.wait_send()

        def go(ss):
            _remote(_weight_region(bufs[0], ss ^ _FLIPS[2], c), send.at[0], recv.at[0], sibling).start()

        _for_my_shard(go)

    sems, (view,) = _comm_call(name, [view], [relayed], after, 1, fn, sibling_id=sibling_id)
    return sems, view


def _gather_in_diagonal_end(view, forwarded, after, name):
    def fn(bufs, pairs, *_):
        (send, recv), = pairs
        x, y, c = _place()
        cp = _remote(_weight_region(bufs[0], 0, 0), send.at[0], recv.at[0], (x, y, 1 - c))
        cp.wait_recv()
        cp.wait_send()

    _, (view,) = _comm_call(name, [view], [forwarded], after, 0, fn)
    return view


def _in_proj_shard(h1, wi, proj, shard_arr, name):
    s, d = h1.shape
    n = wi.shape[1]
    tn = 256
    blocks = n // (N_CHIPS * tn)
    given = [] if proj is None else [proj]

    def body(shard_ref, h_ref, w_ref, *rest):
        del shard_ref
        rest[-1][...] = jnp.dot(h_ref[...], w_ref[...], preferred_element_type=F32)

    grid_spec = pltpu.PrefetchScalarGridSpec(
        num_scalar_prefetch=1, grid=(blocks,),
        in_specs=[pl.BlockSpec((s, d), lambda j, shard_ref: (0, 0)),
                  pl.BlockSpec((d, tn), lambda j, shard_ref: (0, shard_ref[0] * blocks + j))] + [ANY] * len(given),
        out_specs=pl.BlockSpec((s, tn), lambda j, shard_ref: (0, shard_ref[0] * blocks + j)))
    return pl.pallas_call(
        body, name=name, grid_spec=grid_spec,
        out_shape=jax.ShapeDtypeStruct((s, n), F32),
        input_output_aliases={3: 0} if given else {},
        compiler_params=_params(("arbitrary",)),
    )(shard_arr, h1, wi, *given)


def _sibling_handshake():
    x, y, c = _place()
    barrier = pltpu.get_barrier_semaphore()
    pl.semaphore_signal(barrier, inc=1, device_id=(x, y, 1 - c), device_id_type=MESH)
    pl.semaphore_wait(barrier, 1)


def _chips_handshake():
    x, y, c = _place()
    barrier = pltpu.get_barrier_semaphore()
    for cx, cy in _other_chips(x, y):
        pl.semaphore_signal(barrier, inc=1, device_id=(cx, cy, c), device_id_type=MESH)
    pl.semaphore_wait(barrier, N_CHIPS - 1)


def _split_start(name, bufs, n_sems, copies, sibling_id=None, chips_id=None):
    n = len(bufs)
    assert sibling_id is None or chips_id is None

    def body(*refs):
        if sibling_id is not None:
            _sibling_handshake()
        if chips_id is not None:
            _chips_handshake()
        send_sems, recv_sems = refs[n:n + 2]
        for cp in copies(refs[n + 2:], send_sems, recv_sems):
            cp.start()

    outs = pl.pallas_call(
        body, name=name,
        in_specs=[_HBM] * n, out_specs=[_SEM, _SEM] + [_HBM] * n,
        out_shape=[pltpu.SemaphoreType.DMA((n_sems,)), pltpu.SemaphoreType.DMA((n_sems,))]
        + [pltpu.HBM(b.shape, b.dtype) for b in bufs],
        input_output_aliases={i: 2 + i for i in range(n)},
        compiler_params=pltpu.CompilerParams(has_side_effects=_EFFECT,
                                             collective_id=sibling_id if chips_id is None else chips_id),
    )(*[_in_hbm(b) for b in bufs])
    return outs[0], outs[1], list(outs[2:])


def _split_wait(name, bufs, send_sems, recv_sems, copies, after):
    n = len(bufs)
    after = tuple(after) if isinstance(after, (list, tuple)) else (after,)

    def body(*refs):
        send_ref, recv_ref = refs[n:n + 2]
        for cp in copies(refs[n + 2 + len(after):], send_ref, recv_ref):
            cp.wait()

    return list(pl.pallas_call(
        body, name=name,
        in_specs=[_HBM] * n + [_SEM, _SEM] + [ANY] * len(after), out_specs=[_HBM] * n,
        out_shape=[pltpu.HBM(b.shape, b.dtype) for b in bufs],
        input_output_aliases={i: i for i in range(n)},
        compiler_params=pltpu.CompilerParams(has_side_effects=_EFFECT),
    )(*bufs, send_sems, recv_sems, *after))


def _halves_copies(n_w):
    def copies(bufs, send_sems, recv_sems):
        x, y, c = _place()
        out = []
        for w in range(n_w):
            view, land = bufs[w], bufs[n_w + w]
            src = view.at[1 - c] if len(view.shape) == 3 else view.at[:, 1 - c]
            out.append(pltpu.make_async_remote_copy(
                src_ref=src, dst_ref=land, send_sem=send_sems.at[w], recv_sem=recv_sems.at[w],
                device_id=(x, y, 1 - c), device_id_type=MESH))
        return out
    return copies


def _pieces_copies(n_w):
    def copies(bufs, send_sems, recv_sems):
        x, y, c = _place()
        out = []
        for w in range(n_w):
            for j, (cx, cy) in enumerate(_other_chips(x, y)):
                out.append(pltpu.make_async_remote_copy(
                    src_ref=bufs[w].at[2 * cx + cy], dst_ref=bufs[n_w + w].at[j],
                    send_sem=send_sems.at[3 * w + j], recv_sem=recv_sems.at[3 * w + j],
                    device_id=(cx, cy, c), device_id_type=MESH))
        return out
    return copies


def _join_copies(n_w):
    def copies(bufs, send_sems, recv_sems):
        x, y, c = _place()
        return [pltpu.make_async_remote_copy(
            src_ref=bufs[w].at[c], dst_ref=bufs[w].at[c], send_sem=send_sems.at[w], recv_sem=recv_sems.at[w],
            device_id=(x, y, 1 - c), device_id_type=MESH) for w in range(n_w)]
    return copies


def _halves_landing(view):
    shape = view.shape[1:] if view.ndim == 3 else (N_CHIPS,) + view.shape[2:]
    return lax.empty(shape, BF16)


_SIBLING_IDS = {"halves_down": 1, "halves_gate_up": 2, "halves_out": 3, "halves_in": 4,
                "join_down": 5, "join_gate_up": 6, "join_out": 7, "join_in": 8,
                "diagonal_in": 9, "diagonal_gate": 10, "diagonal_up": 11, "forward_out": 12, "forward_down": 13,
                "pieces_down": 14, "pieces_gate_up": 15, "pieces_out": 16, "pieces_in": 17}


def _halves_start(tag, grads, column_sharded):
    views = [_weight_view(g, cs) for g, cs in zip(grads, column_sharded)]
    n = len(views)
    return _split_start("halves_start_" + tag, views + [_halves_landing(v) for v in views], n, _halves_copies(n),
                        sibling_id=_SIBLING_IDS["halves_" + tag])


def _halves_wait(tag, state, after):
    send_sems, recv_sems, bufs = state
    n = len(bufs) // 2
    bufs = _split_wait("halves_wait_" + tag, bufs, send_sems, recv_sems, _halves_copies(n), after)
    return bufs[:n], bufs[n:]


def _pieces_start(tag, pieces):
    n = len(pieces)
    landing = [lax.empty((3,) + p.shape[1:], BF16) for p in pieces]
    return _split_start("pieces_start_" + tag, list(pieces) + landing, 3 * n, _pieces_copies(n),
                        chips_id=_SIBLING_IDS["pieces_" + tag])


def _pieces_wait(tag, state, after):
    send_sems, recv_sems, bufs = state
    n = len(bufs) // 2
    bufs = _split_wait("pieces_wait_" + tag, bufs, send_sems, recv_sems, _pieces_copies(n), after)
    return bufs[:n], bufs[n:]


def _join_start(tag, shards):
    n = len(shards)
    return _split_start("join_start_" + tag, list(shards), n, _join_copies(n), sibling_id=_SIBLING_IDS["join_" + tag])


def _join_wait(tag, state, after):
    send_sems, recv_sems, bufs = state
    bufs = _split_wait("join_wait_" + tag, bufs, send_sems, recv_sems, _join_copies(len(bufs)), after)
    return [b.reshape(2 * b.shape[1], b.shape[2]) for b in bufs]


def _chip_sum_col(g3, sib, c_arr, name):
    _, hk, n = g3.shape
    cols = n // N_CHIPS
    tr = _row_tile(hk, cols * 2, limit=4 * 1024 * 1024)

    def body(c_ref, g_ref, s_ref, o_ref):
        del c_ref
        o_ref[...] = (g_ref[...].astype(F32) + s_ref[...].astype(F32)).astype(BF16)

    grid_spec = pltpu.PrefetchScalarGridSpec(
        num_scalar_prefetch=1, grid=(N_CHIPS, hk // tr),
        in_specs=[pl.BlockSpec((None, tr, cols), lambda p, r, c_ref: (c_ref[0], r, p)),
                  pl.BlockSpec((tr, cols), lambda p, r, c_ref: (r, p))],
        out_specs=pl.BlockSpec((None, tr, cols), lambda p, r, c_ref: (p, r, 0)))
    return pl.pallas_call(
        body, name=name, grid_spec=grid_spec,
        out_shape=jax.ShapeDtypeStruct((N_CHIPS, hk, cols), BF16),
        compiler_params=_params(("parallel", "parallel")),
    )(c_arr, g3, sib)


def _chip_sum_row(g4, sib, c_arr, name):
    _, _, hr, n = g4.shape
    tr = _row_tile(hr, n * 2, limit=4 * 1024 * 1024)

    def body(c_ref, g_ref, s_ref, o_ref):
        del c_ref
        o_ref[...] = (g_ref[...].astype(F32) + s_ref[...].astype(F32)).astype(BF16)

    grid_spec = pltpu.PrefetchScalarGridSpec(
        num_scalar_prefetch=1, grid=(N_CHIPS, hr // tr),
        in_specs=[pl.BlockSpec((None, None, tr, n), lambda p, r, c_ref: (p, c_ref[0], r, 0)),
                  pl.BlockSpec((None, tr, n), lambda p, r, c_ref: (p, r, 0))],
        out_specs=pl.BlockSpec((None, tr, n), lambda p, r, c_ref: (p, r, 0)))
    return pl.pallas_call(
        body, name=name, grid_spec=grid_spec,
        out_shape=jax.ShapeDtypeStruct((N_CHIPS, hr, n), BF16),
        compiler_params=_params(("parallel", "parallel")),
    )(c_arr, g4, sib)


def _sum_pieces(pieces, received, place_arr, name):
    _, r, n = pieces.shape
    tr = _row_tile(r, n * 4, limit=4 * 1024 * 1024)

    def body(p_ref, own_ref, r0_ref, r1_ref, r2_ref, o_ref):
        del p_ref
        acc = own_ref[...].astype(F32) + r0_ref[...].astype(F32)
        acc = acc + r1_ref[...].astype(F32)
        o_ref[...] = acc + r2_ref[...].astype(F32)

    def recv_spec(j):
        return pl.BlockSpec((None, tr, n), lambda i, p_ref: (j, i, 0))

    grid_spec = pltpu.PrefetchScalarGridSpec(
        num_scalar_prefetch=1, grid=(r // tr,),
        in_specs=[pl.BlockSpec((None, tr, n), lambda i, p_ref: (p_ref[0], i, 0)),
                  recv_spec(0), recv_spec(1), recv_spec(2)],
        out_specs=pl.BlockSpec((None, tr, n), lambda i, p_ref: (p_ref[1], i, 0)))
    return pl.pallas_call(
        body, name=name, grid_spec=grid_spec,
        out_shape=jax.ShapeDtypeStruct((2, r, n), F32),
        compiler_params=_params(("parallel",)),
    )(place_arr, pieces, received, received, received)


def _norm_weights_step(parts, w, m, v, after=()):
    rows, d = parts.shape
    after = tuple(after)

    def body(p_ref, w_ref, m_ref, v_ref, *rest):
        g_ref, d_ref, mo_ref, vo_ref, gathered, send_sems, recv_sems = rest[len(after):]
        x, y, c = _place()
        me = 4 * x + 2 * y + c
        gathered[me] = p_ref[...]
        copies = []
        for k in range(1, N_DEV):
            peer = (x ^ ((k >> 2) & 1), y ^ ((k >> 1) & 1), c ^ (k & 1))
            copies.append(pltpu.make_async_remote_copy(
                src_ref=p_ref, dst_ref=gathered.at[me], send_sem=send_sems.at[k - 1],
                recv_sem=recv_sems.at[k - 1], device_id=peer, device_id_type=MESH))
        for cp in copies:
            cp.start()
        for cp in copies:
            cp.wait()
        g = gathered[0]
        for k in range(1, N_DEV):
            g = g + gathered[k]
        delta, m_new, v_new = _adamw_math(w_ref[...], g, m_ref[...], v_ref[...])
        g_ref[...] = g
        d_ref[...] = delta
        mo_ref[...] = m_new
        vo_ref[...] = v_new

    vmem = pl.BlockSpec(memory_space=pltpu.VMEM)
    shp = jax.ShapeDtypeStruct((rows, d), F32)
    return pl.pallas_call(
        body, name="norm_weights_step",
        in_specs=[vmem] * 4 + [ANY] * len(after), out_specs=[vmem] * 4, out_shape=[shp] * 4,
        scratch_shapes=[pltpu.VMEM((N_DEV, rows, d), F32), pltpu.SemaphoreType.DMA((N_DEV - 1,)),
                        pltpu.SemaphoreType.DMA((N_DEV - 1,))],
        compiler_params=pltpu.CompilerParams(has_side_effects=True),
    )(parts, w, m, v, *after)


def kernel(x, norm_mix_w, w_in, w_out, norm_ffn_w, w_gate, w_up, w_down, norm_final_w, loss_target, m_norm_mix_w, m_w_in, m_w_out, m_norm_ffn_w, m_w_gate, m_w_up, m_w_down, m_norm_final_w, v_norm_mix_w, v_w_in, v_w_out, v_norm_ffn_w, v_w_gate, v_w_up, v_w_down, v_norm_final_w):
    s, d = x.shape[1], x.shape[2]
    xs = x.reshape(s, d)
    target = loss_target.reshape(s, d)
    big = {"w_in": (w_in, m_w_in, v_w_in), "w_out": (w_out, m_w_out, v_w_out),
           "w_gate": (w_gate, m_w_gate, v_w_gate), "w_up": (w_up, m_w_up, v_w_up),
           "w_down": (w_down, m_w_down, v_w_down)}
    big = {k: tuple(a.reshape(a.shape[1:]) for a in t) for k, t in big.items()}
    col_names, row_names = ("w_in", "w_gate", "w_up"), ("w_out", "w_down")
    n_in = N_CHIPS * big["w_in"][0].shape[1]
    ffn = N_CHIPS * big["w_gate"][0].shape[1]
    mix = ATTN_WIDTH + RET_WIDTH
    c_arr = lax.axis_index("c").astype(I32).reshape(1)
    shard_arr = (2 * lax.axis_index("x") + lax.axis_index("y")).astype(I32).reshape(1)
    place_arr = jnp.concatenate([shard_arr, c_arr])

    def cast(k, after=()):
        return _weight_view(_cast_into_full(big[k][0], shard_arr, k in col_names, "cast_" + k, after), k in col_names)

    started_in, v_in = _gather_in_start(cast("w_in"), "gather_in_start")

    sec = ATTN_WIDTH

    def section(p, rows):
        return pl.BlockSpec((None, rows, sec), lambda i, j, kk: (p, i, 0))

    h1 = _rms_fwd(xs, norm_mix_w, "rms_mix_fwd", after=[v_in])
    my_shard = shard_arr[0]
    shard_of = [jnp.bitwise_xor(my_shard, f).astype(I32).reshape(1) for f in (0,) + _FLIPS]
    proj = _in_proj_shard(h1, _weight_unview(v_in), None, shard_of[0], "in_proj_own")
    early_views = [cast(k, after=[proj]) for k in ("w_out", "w_gate")]
    v_up, v_down = [cast(k, after=[proj]) for k in ("w_up", "w_down")]
    relayed_in, (v_in,) = _gather_relay(v_in, started_in, 0, early_views + [v_up, v_down], "gather_in_relay")
    started_og, (v_out, v_gate) = _gather_out_gate_start(*early_views, [v_in], "gather_out_gate_start")
    v_in = _gather_in_neighbours_end(v_in, relayed_in, [v_out], "gather_in_neighbours_end")
    proj = _in_proj_shard(h1, _weight_unview(v_in), proj, shard_of[1], "in_proj_x")
    proj = _in_proj_shard(h1, _weight_unview(v_in), proj, shard_of[2], "in_proj_y")
    forwarded_in, v_in = _gather_in_diagonal(v_in, relayed_in, [proj], "gather_in_diagonal",
                                             _SIBLING_IDS["diagonal_in"])
    wi = _weight_unview(_gather_in_diagonal_end(v_in, forwarded_in, [proj], "gather_in_diagonal_end"))
    proj = _in_proj_shard(h1, wi, proj, shard_of[3], "in_proj_diagonal")
    fs_o, fr_o, v_out = _gather_forward([v_out], [0], *started_og, proj, "gather_forward_out",
                                        sibling_id=_SIBLING_IDS["forward_out"])
    mixed, attn_o, lse = _attn_fwd(proj, after=[v_out])
    relayed_g, (v_gate, v_up) = _gather_relay(v_gate, started_og, 3, [attn_o], "gather_gate_relay",
                                              then=v_up, then_peers=2, then_first=True)
    mixed, ret_raw = _ret_fwd(proj, mixed, after=[v_gate])
    wo, = _gather_end([v_out], fs_o, fr_o, ret_raw, "gather_end_out")
    x1, = _matmul("out_proj", "nn", [mixed, mixed], [wo, wo], [0, 0], s, d, sec, s // 2, 512, sec, [xs], [F32],
                  _epi_residual, b_koff=[0, 1], a_specs=[section(0, s // 2), section(1, s // 2)])
    h2 = _rms_fwd(x1, norm_ffn_w, "rms_ffn_fwd")
    relayed_u, (v_up, v_down) = _gather_relay(v_up, relayed_g, 6, [h2], "gather_up_relay",
                                              then=v_down, then_peers=3)
    v_gate = _gather_in_neighbours_end(v_gate, relayed_g, [v_up], "gather_gate_neighbours_end")
    forwarded_g, v_gate = _gather_in_diagonal(v_gate, relayed_g, [v_up], "gather_gate_diagonal",
                                              _SIBLING_IDS["diagonal_gate"])
    v_up = _gather_in_neighbours_end(v_up, relayed_u, [v_gate], "gather_up_neighbours_end")
    wg = _weight_unview(_gather_in_diagonal_end(v_gate, forwarded_g, [v_up], "gather_gate_diagonal_end"))
    forwarded_u, v_up = _gather_in_diagonal(v_up, relayed_u, [wg], "gather_up_diagonal",
                                            _SIBLING_IDS["diagonal_up"])
    wu = _weight_unview(_gather_in_diagonal_end(v_up, forwarded_u, [wg], "gather_up_diagonal_end"))
    gate, up, act = _matmul("gate_up", "nn", [h2, h2], [wg, wu], [0, 1], s, ffn, d, s, 512, d, [],
                            [BF16, BF16, BF16], _epi_swiglu, a_single_buffer=True)
    fs, fr, v_down = _gather_forward([v_down], [0], *relayed_u, act, "gather_forward_down", base=6,
                                     sibling_id=_SIBLING_IDS["forward_down"])
    wd, = _gather_end([v_down], fs, fr, act, "gather_end_down")
    x2, = _matmul("down_proj", "nn", [act], [wd], [0], s, d, ffn, s // 2, 512, ffn, [x1], [F32],
                  _epi_residual)
    loss_row, dx2, dx2b, dwf = _final_norm_loss(x2, norm_final_w.reshape(1, d), target, "final_norm_loss")

    names = col_names + row_names
    grads, new = {}, {}

    def chip_sums(tag_names, views, sibs):
        return [(_chip_sum_col if k in col_names else _chip_sum_row)(v, sb, c_arr, "chip_sum_" + k)
                for k, v, sb in zip(tag_names, views, sibs)]

    def piece_sums(tag_names, pieces, received):
        return [_sum_pieces(p, r, place_arr, "sum_pieces_" + k) for k, p, r in zip(tag_names, pieces, received)]

    def update(k):
        new[k] = _adamw(big[k][0], grads[k], big[k][1], big[k][2], "adamw_" + k)

    dgate, dup = _matmul("d_act", "nt", [dx2b], [wd], [0], s, ffn, d, s, 512, d, [gate, up],
                         [BF16, BF16], _epi_swiglu_bwd, a_single_buffer=True)
    g_wd, = _matmul("g_w_down", "tn", [act], [dx2b], [0], ffn, d, s, 512, d, s, [], [BF16], _epi_plain)
    halves_d = _halves_start("down", [g_wd], [False])
    dh2 = _d_h2_streamed(dgate, dup, wg, wu, after=halves_d[2][-1:])
    pieces_d = _pieces_start("down", chip_sums(["w_down"], *_halves_wait("down", halves_d, dh2)))
    g_wg, g_wu = _matmul("g_w_gate_up", "tn", [h2, h2], [dgate, dup], [0, 1], d, ffn, s, 1024, 512, s, [],
                         [BF16, BF16], _epi_two, after=pieces_d[2][-1:])
    halves_gu = _halves_start("gate_up", [g_wg, g_wu], [True, True])
    dx1, dx1b, dw_ffn = _rms_bwd(x1, norm_ffn_w, dh2, dx2, "rms_ffn_bwd", after=halves_gu[2][-1:])

    dmixed, = _matmul("d_mixed", "nt", [dx1b], [wo], [0], s, mix, d, s // 2, 512, d, [], [F32], _epi_plain)
    pieces_gu = _pieces_start("gate_up", chip_sums(["w_gate", "w_up"], *_halves_wait("gate_up", halves_gu, dmixed)))
    per = sec // 512
    g_wo, = _matmul("g_w_out", "tn", [mixed], [dx1b], [0], mix, d, s, 512, d, s, [], [BF16], _epi_plain,
                    after=pieces_gu[2][-1:],
                    a_specs=[pl.BlockSpec((None, s, 512), lambda i, j, kk: (i // per, 0, i % per))])
    halves_o = _halves_start("out", [g_wo], [False])
    dsec = _attn_bwd(proj, attn_o, lse, dmixed, after=halves_o[2][-1:])
    pieces_o = _pieces_start("out", chip_sums(["w_out"], *_halves_wait("out", halves_o, dsec)))
    dsec = _ret_bwd(proj, ret_raw, dmixed, dsec, after=pieces_o[2][-1:])
    where = [0, 1, 2, 4, 5, 6, 7]
    n_sec = len(where)
    g_wi, = _matmul("g_w_in", "tn", [h1], [dsec], [0], d, n_in, s, 1024, sec, s, [], [BF16], _epi_plain,
                    b_specs=[pl.BlockSpec((None, s, sec), lambda i, j, kk: (j + (j >= 3).astype(I32), 0, 0))])
    halves_i = _halves_start("in", [g_wi], [True])
    dh1, = _matmul("d_h1", "nt", [dsec] * n_sec, [wi] * n_sec, [0] * n_sec, s, d, sec, s // 2, 256, sec, [], [F32],
                   _epi_plain, b_koff=list(range(n_sec)), after=halves_i[2][-1:],
                   a_specs=[section(p, s // 2) for p in where])
    pieces_i = _pieces_start("in", chip_sums(["w_in"], *_halves_wait("in", halves_i, dh1)))
    grad_x, _, dw_mix = _rms_bwd(xs, norm_mix_w, dh1, dx1, "rms_mix_bwd", after=pieces_i[2][-1:])

    def rows8(*vs):
        return jnp.concatenate([v.reshape(1, d) for v in vs] + [jnp.zeros((8 - len(vs), d), F32)], axis=0)

    join_d = _join_start("down", piece_sums(["w_down"], *_pieces_wait("down", pieces_d, grad_x)))
    join_gu = _join_start("gate_up", piece_sums(["w_gate", "w_up"], *_pieces_wait("gate_up", pieces_gu, join_d[2][0])))
    join_o = _join_start("out", piece_sums(["w_out"], *_pieces_wait("out", pieces_o, join_gu[2][0])))
    grads["w_down"], = _join_wait("down", join_d, join_o[2][0])
    update("w_down")
    grads["w_gate"], grads["w_up"] = _join_wait("gate_up", join_gu, new["w_down"][0])
    update("w_gate")
    update("w_up")
    grads["w_out"], = _join_wait("out", join_o, new["w_up"][0])
    update("w_out")
    others_done = [new[k][0] for k in ("w_down", "w_gate", "w_up", "w_out")]
    join_i = _join_start("in", piece_sums(["w_in"], *_pieces_wait("in", pieces_i, others_done)))
    ng, nd, nm, nv = _norm_weights_step(
        rows8(dw_mix, dw_ffn, dwf, jnp.broadcast_to(loss_row[:, :1], (1, d))),
        rows8(norm_mix_w, norm_ffn_w, norm_final_w),
        rows8(m_norm_mix_w, m_norm_ffn_w, m_norm_final_w), rows8(v_norm_mix_w, v_norm_ffn_w, v_norm_final_w),
        after=join_i[2][:1])
    grads["w_in"], = _join_wait("in", join_i, ng)
    update("w_in")

    loss = ng[3, 0]

    def pack(small, per_weight):
        lead = lambda a: a.reshape((1,) + a.shape)
        return (small[0:1], lead(per_weight["w_in"]), lead(per_weight["w_out"]), small[1:2],
                lead(per_weight["w_gate"]), lead(per_weight["w_up"]), lead(per_weight["w_down"]), small[2])

    return (loss, grad_x.reshape(1, s, d),
            *pack(ng, {k: new[k][3] for k in names}),
            *pack(nd, {k: new[k][0] for k in names}),
            *pack(nm, {k: new[k][1] for k in names}),
            *pack(nv, {k: new[k][2] for k in names}))
```

```python
import functools
import math

import jax
import jax.numpy as jnp
from jax import lax
from jax.experimental import pallas as pl
from jax.experimental.pallas import tpu as pltpu

F32 = jnp.float32
BF16 = jnp.bfloat16
I32 = jnp.int32
MESH = pl.DeviceIdType.MESH
ANY = pl.BlockSpec(memory_space=pl.ANY)

ATTN_HEADS = 8
ATTN_HEAD_DIM = 128
RET_HEADS = 4
RET_HEAD_DIM = 256
ATTN_WIDTH = ATTN_HEADS * ATTN_HEAD_DIM
RET_WIDTH = RET_HEADS * RET_HEAD_DIM
DILATED_PATTERNS = ((128, 1), (512, 4), (2048, 16))
NORM_EPS = 1e-6
ADAM_LR = 0.001
ADAM_B1 = 0.9
ADAM_B2 = 0.999
ADAM_EPS = 1e-08
ADAM_WD = 0.01
ADAM_STEP = 10

N_CHIPS = 4
N_DEV = 8
NEG_BIG = -1e30
SEQ_TILE = 512
ATTN_FWD_HEADS_PER_STEP = 2
ATTN_HEADS_PER_STEP = 1
VMEM_LIMIT_BYTES = 56 * 1024 * 1024


def _params(semantics=None, vmem=VMEM_LIMIT_BYTES):
    return pltpu.CompilerParams(dimension_semantics=semantics, vmem_limit_bytes=vmem)


def _row_tile(rows, row_bytes, limit=2 * 1024 * 1024, mult=16):
    best = None
    for t in range(mult, rows + 1, mult):
        if rows % t == 0 and t * row_bytes <= limit:
            best = t
    assert best is not None, (rows, row_bytes)
    return best


def _sigmoid(x):
    return 1.0 / (1.0 + jnp.exp(-x))


def _select_by_index(idx, values):
    out = jnp.float32(values[-1])
    for i in range(len(values) - 2, -1, -1):
        out = jnp.where(idx == i, jnp.float32(values[i]), out)
    return out


def _place():
    x, y, c = lax.axis_index("x"), lax.axis_index("y"), lax.axis_index("c")
    return x, y, c


def _cast_into_full(w, shard_arr, column_sharded, name, after=()):
    after = tuple(after)
    rows, cols = w.shape
    tr = _row_tile(rows, cols * 4)
    steps = rows // tr
    if column_sharded:
        out_shape, out_map = (rows, N_CHIPS * cols), (lambda i, s_ref: (i, s_ref[0]))
    else:
        out_shape, out_map = (N_CHIPS * rows, cols), (lambda i, s_ref: (s_ref[0] * steps + i, 0))

    def body(s_ref, w_ref, *rest):
        del s_ref
        rest[-1][...] = w_ref[...].astype(BF16)

    grid_spec = pltpu.PrefetchScalarGridSpec(
        num_scalar_prefetch=1, grid=(steps,),
        in_specs=[pl.BlockSpec((tr, cols), lambda i, s_ref: (i, 0))] + [ANY] * len(after),
        out_specs=pl.BlockSpec((tr, cols), out_map))
    return pl.pallas_call(
        body, name=name, grid_spec=grid_spec,
        out_shape=jax.ShapeDtypeStruct(out_shape, BF16),
        compiler_params=_params(("parallel",)),
    )(shard_arr, w, *after)


def _rms_fwd(x, w, name, after=()):
    rows, d = x.shape
    tr = 256
    after = tuple(after)

    def body(x_ref, w_ref, *rest):
        xv = x_ref[...]
        r = lax.rsqrt(jnp.mean(xv * xv, axis=-1, keepdims=True) + NORM_EPS)
        rest[-1][...] = (xv * r * w_ref[...]).astype(BF16)

    return pl.pallas_call(
        body, name=name, grid=(rows // tr,),
        in_specs=[pl.BlockSpec((tr, d), lambda i: (i, 0)), pl.BlockSpec((1, d), lambda i: (0, 0))]
        + [ANY] * len(after),
        out_specs=pl.BlockSpec((tr, d), lambda i: (i, 0)),
        out_shape=jax.ShapeDtypeStruct((rows, d), BF16),
        compiler_params=_params(("parallel",)),
    )(x, w, *after)


def _rms_bwd(x, w, dh, dres, name, after=()):
    rows, d = x.shape
    tr = 256
    after = tuple(after)

    def body(x_ref, w_ref, dh_ref, dres_ref, *rest):
        dx_ref, dxb_ref, dw_ref = rest[len(after):]
        xv = x_ref[...]
        r = lax.rsqrt(jnp.mean(xv * xv, axis=-1, keepdims=True) + NORM_EPS)
        xhat = xv * r
        dy = dh_ref[...]
        dxhat = dy * w_ref[...]
        dx = dres_ref[...] + r * (dxhat - xhat * jnp.mean(dxhat * xhat, axis=-1, keepdims=True))
        dx_ref[...] = dx
        dxb_ref[...] = dx.astype(BF16)
        part = jnp.sum(dy * xhat, axis=0, keepdims=True)

        @pl.when(pl.program_id(0) == 0)
        def _():
            dw_ref[...] = part

        @pl.when(pl.program_id(0) != 0)
        def _():
            dw_ref[...] += part

    row = pl.BlockSpec((tr, d), lambda i: (i, 0))
    vec = pl.BlockSpec((1, d), lambda i: (0, 0))
    return pl.pallas_call(
        body, name=name, grid=(rows // tr,),
        in_specs=[row, vec, row, row] + [ANY] * len(after),
        out_specs=[row, row, vec],
        out_shape=[jax.ShapeDtypeStruct((rows, d), F32), jax.ShapeDtypeStruct((rows, d), BF16),
                   jax.ShapeDtypeStruct((1, d), F32)],
        compiler_params=_params(("arbitrary",)),
    )(x, w, dh, dres, *after)


def _final_norm_loss(x2, w, target, name):
    rows, d = x2.shape
    tr = 256

    def body(x_ref, w_ref, t_ref, loss_ref, dx_ref, dxb_ref, dw_ref):
        xv = x_ref[...]
        wv = w_ref[...]
        r = lax.rsqrt(jnp.mean(xv * xv, axis=-1, keepdims=True) + NORM_EPS)
        xhat = xv * r
        err = xhat * wv - t_ref[...]
        part_loss = 0.5 * jnp.sum(jnp.mean(err * err, axis=-1, keepdims=True), axis=0, keepdims=True)
        dy = err * (1.0 / d)
        dxhat = dy * wv
        dx = r * (dxhat - xhat * jnp.mean(dxhat * xhat, axis=-1, keepdims=True))
        dx_ref[...] = dx
        dxb_ref[...] = dx.astype(BF16)
        part_dw = jnp.sum(dy * xhat, axis=0, keepdims=True)
        part_loss = jnp.broadcast_to(part_loss, (1, 128))

        @pl.when(pl.program_id(0) == 0)
        def _():
            dw_ref[...] = part_dw
            loss_ref[...] = part_loss

        @pl.when(pl.program_id(0) != 0)
        def _():
            dw_ref[...] += part_dw
            loss_ref[...] += part_loss

    row = pl.BlockSpec((tr, d), lambda i: (i, 0))
    vec = pl.BlockSpec((1, d), lambda i: (0, 0))
    return pl.pallas_call(
        body, name=name, grid=(rows // tr,),
        in_specs=[row, vec, row],
        out_specs=[pl.BlockSpec((1, 128), lambda i: (0, 0)), row, row, vec],
        out_shape=[jax.ShapeDtypeStruct((1, 128), F32), jax.ShapeDtypeStruct((rows, d), F32),
                   jax.ShapeDtypeStruct((rows, d), BF16), jax.ShapeDtypeStruct((1, d), F32)],
        compiler_params=_params(("arbitrary",)),
    )(x2, w, target)


def _adamw_math(w, g, m, v):
    m = ADAM_B1 * m + (1.0 - ADAM_B1) * g
    v = ADAM_B2 * v + (1.0 - ADAM_B2) * (g * g)
    m_hat = m / (1.0 - ADAM_B1 ** ADAM_STEP)
    v_hat = v / (1.0 - ADAM_B2 ** ADAM_STEP)
    delta = -ADAM_LR * (m_hat / (jnp.sqrt(v_hat) + ADAM_EPS) + ADAM_WD * w)
    return delta, m, v


def _adamw(w, g, m, v, name):
    rows, cols = w.shape
    tr = _row_tile(rows, cols * 4)

    def body(w_ref, g_ref, m_ref, v_ref, d_ref, mo_ref, vo_ref, go_ref):
        g = g_ref[...]
        delta, m_new, v_new = _adamw_math(w_ref[...], g, m_ref[...], v_ref[...])
        d_ref[...] = delta
        mo_ref[...] = m_new
        vo_ref[...] = v_new
        go_ref[...] = g

    blk = pl.BlockSpec((tr, cols), lambda i: (i, 0))
    shp = jax.ShapeDtypeStruct((rows, cols), F32)
    return pl.pallas_call(
        body, name=name, grid=(rows // tr,),
        in_specs=[blk] * 4, out_specs=[blk] * 4, out_shape=[shp] * 4,
        compiler_params=_params(("parallel",)),
    )(w, g, m, v)


_DOT_DIMS = {"nn": ((1,), (0,)), "nt": ((1,), (1,)), "tn": ((0,), (0,))}


def _matmul(name, mode, a_list, b_list, acc_of, m, n, k, tm, tn, tk, extras, out_dtypes, epilogue,
            a_koff=None, b_koff=None, after=(), a_specs=None, b_specs=None, a_single_buffer=False):
    after = tuple(after)
    assert m % tm == 0 and n % tn == 0 and k % tk == 0, (name, m, n, k, tm, tn, tk)
    nk = k // tk
    n_acc = max(acc_of) + 1
    n_pairs = len(a_list)
    a_koff = a_koff or [0] * n_pairs
    b_koff = b_koff or [0] * n_pairs
    dims = (_DOT_DIMS[mode], ((), ()))
    n_ext, n_out = len(extras), len(out_dtypes)

    def body(*refs):
        a_refs = refs[:n_pairs]
        b_refs = refs[n_pairs:2 * n_pairs]
        e_refs = refs[2 * n_pairs:2 * n_pairs + n_ext]
        first_out = 2 * n_pairs + n_ext + len(after)
        o_refs = refs[first_out:first_out + n_out]
        acc_refs = refs[first_out + n_out:]

        parts = [None] * n_acc
        for p in range(n_pairs):
            d = lax.dot_general(a_refs[p][...], b_refs[p][...], dims, preferred_element_type=F32)
            parts[acc_of[p]] = d if parts[acc_of[p]] is None else parts[acc_of[p]] + d

        def finish(accs):
            outs = epilogue(accs, [e[...] for e in e_refs])
            for o_ref, o in zip(o_refs, outs):
                o_ref[...] = o.astype(o_ref.dtype)

        if nk == 1:
            finish(parts)
        else:
            kk = pl.program_id(2)

            @pl.when(kk == 0)
            def _():
                for acc_ref, part in zip(acc_refs, parts):
                    acc_ref[...] = part

            @pl.when(kk != 0)
            def _():
                for acc_ref, part in zip(acc_refs, parts):
                    acc_ref[...] += part

            @pl.when(kk == nk - 1)
            def _():
                finish([acc_ref[...] for acc_ref in acc_refs])

    def a_spec(off):
        mode_a = pl.Buffered(1) if a_single_buffer else None
        if mode == "tn":
            return pl.BlockSpec((tk, tm), lambda i, j, kk: (kk + off, i), pipeline_mode=mode_a)
        return pl.BlockSpec((tm, tk), lambda i, j, kk: (i, kk + off), pipeline_mode=mode_a)

    def b_spec(off):
        if mode == "nt":
            return pl.BlockSpec((tn, tk), lambda i, j, kk: (j, kk + off))
        return pl.BlockSpec((tk, tn), lambda i, j, kk: (kk + off, j))

    tile = pl.BlockSpec((tm, tn), lambda i, j, kk: (i, j))
    scratch = [pltpu.VMEM((tm, tn), F32) for _ in range(n_acc)] if nk > 1 else []
    return pl.pallas_call(
        body, name=name, grid=(m // tm, n // tn, nk),
        in_specs=(a_specs or [a_spec(o) for o in a_koff]) + (b_specs or [b_spec(o) for o in b_koff])
        + [tile] * n_ext + [ANY] * len(after),
        out_specs=[tile] * n_out,
        out_shape=[jax.ShapeDtypeStruct((m, n), dt) for dt in out_dtypes],
        scratch_shapes=scratch,
        compiler_params=_params(("parallel", "parallel", "arbitrary")),
    )(*a_list, *b_list, *extras, *after)


def _d_h2_streamed(dgate, dup, wg, wu, after=()):
    after = tuple(after)
    s, ffn = dgate.shape
    d = wg.shape[0]
    tm, tn, n_buf, nc = s // 2, 256, 3, 4
    ni, nj = s // tm, d // tn
    total = ni * nj
    kc = ffn // nc
    assert nj > 1 and ffn % nc == 0 and kc % 128 == 0

    def body(a0_ref, a1_ref, w0_ref, w1_ref, *rest):
        o_ref, abuf, buf, sems, asems = rest[len(after):]
        i, j = pl.program_id(0), pl.program_id(1)
        step = i * nj + j
        acts = (a0_ref, a1_ref)
        weights = (w0_ref, w1_ref)

        def fetch(of_step, slot):
            rows = pl.ds(pl.multiple_of(lax.rem(of_step, nj) * tn, tn), tn)
            return [pltpu.make_async_copy(weights[p].at[rows, :], buf.at[p, slot], sems.at[p, slot]) for p in range(2)]

        def a_fetch(of_i, c):
            rows = pl.ds(pl.multiple_of(of_i * tm, tm), tm)
            return [pltpu.make_async_copy(acts[p].at[rows, pl.ds(c * kc, kc)],
                                          abuf.at[p, pl.ds(0, tm), pl.ds(c * kc, kc)], asems.at[p, c])
                    for p in range(2)]

        @pl.when(step == 0)
        def _():
            for cp in a_fetch(0, 0) + fetch(0, 0):
                cp.start()
            for c in range(1, nc):
                for cp in a_fetch(0, c):
                    cp.start()
            for k in range(1, min(n_buf, total)):
                for cp in fetch(k, k):
                    cp.start()

        slot = lax.rem(step, n_buf)
        for cp in fetch(step, slot):
            cp.wait()

        def by_chunks(landing, refill):
            acc = None
            for c in range(nc):
                if landing:
                    for cp in a_fetch(i, c):
                        cp.wait()
                cols = slice(c * kc, (c + 1) * kc)
                part = lax.dot_general(abuf[0, :, cols], buf[0, slot, :, cols], _NT_DIMS, preferred_element_type=F32)
                part += lax.dot_general(abuf[1, :, cols], buf[1, slot, :, cols], _NT_DIMS, preferred_element_type=F32)
                acc = part if acc is None else acc + part
                if refill:
                    for cp in a_fetch(i + 1, c):
                        cp.start()
            o_ref[...] = acc

        first = j == 0
        last = jnp.logical_and(j == nj - 1, i < ni - 1)

        @pl.when(first)
        def _():
            by_chunks(True, False)

        @pl.when(last)
        def _():
            by_chunks(False, True)

        @pl.when(jnp.logical_not(jnp.logical_or(first, last)))
        def _():
            acc = lax.dot_general(abuf[0], buf[0, slot], _NT_DIMS, preferred_element_type=F32)
            o_ref[...] = acc + lax.dot_general(abuf[1], buf[1, slot], _NT_DIMS, preferred_element_type=F32)

        @pl.when(step + n_buf < total)
        def _():
            for cp in fetch(step + n_buf, slot):
                cp.start()

    return pl.pallas_call(
        body, name="d_h2", grid=(ni, nj),
        in_specs=[ANY, ANY, ANY, ANY] + [ANY] * len(after),
        out_specs=pl.BlockSpec((tm, tn), lambda i, j: (i, j)),
        out_shape=jax.ShapeDtypeStruct((s, d), F32),
        scratch_shapes=[pltpu.VMEM((2, tm, ffn), BF16), pltpu.VMEM((2, n_buf, tn, ffn), BF16),
                        pltpu.SemaphoreType.DMA((2, n_buf)), pltpu.SemaphoreType.DMA((2, nc))],
        compiler_params=_params(("arbitrary", "arbitrary")),
    )(dgate, dup, wg, wu, *after)


def _epi_plain(accs, extras):
    return (accs[0],)


def _epi_residual(accs, extras):
    return (accs[0] + extras[0],)


def _epi_two(accs, extras):
    return accs[0], accs[1]


def _epi_swiglu(accs, extras):
    g, u = accs
    return g, u, g * _sigmoid(g) * u


def _epi_swiglu_bwd(accs, extras):
    da = accs[0]
    g, u = (e.astype(F32) for e in extras)
    sg = _sigmoid(g)
    dg = da * u * sg * (1.0 + g * (1.0 - sg))
    du = da * g * sg
    return dg, du


_NT_DIMS = (((1,), (1,)), ((), ()))
_TN_DIMS = (((0,), (0,)), ((), ()))


def _tile_delta(tq, tk):
    return lax.broadcasted_iota(I32, (tq, tk), 0) - lax.broadcasted_iota(I32, (tq, tk), 1)


def _attn_log_count(delta):
    count = jnp.zeros(delta.shape, I32)
    for window, dilation in DILATED_PATTERNS:
        hit = ((delta & (dilation - 1)) == 0) & (delta <= window)
        count = count + jnp.where(hit, 1, 0)
    valid = (delta >= 0) & (count > 0)
    logm = jnp.where(count == 3, math.log(3.0), jnp.where(count == 2, math.log(2.0), 0.0))
    return jnp.where(valid, logm, NEG_BIG)


def _fill_attn_log_count(tab_ref):
    nb, t, _ = tab_ref.shape
    base = _tile_delta(t, t)
    for b in range(nb):
        tab_ref[b] = _attn_log_count(base + b * t)


def _fill_attn_bias(tab_ref, log_count_ref, slope):
    nb, t, _ = tab_ref.shape
    dist = _tile_delta(t, t).astype(F32)
    for b in range(nb):
        tab_ref[b] = log_count_ref[b] - slope * (dist + float(b * t))


def _fill_ret_decay(tab_ref, log_gamma):
    nb, t, _ = tab_ref.shape
    base = _tile_delta(t, t)
    for b in range(nb):
        tab_ref[b] = _ret_decay(base + b * t, log_gamma)


def _alibi_slopes():
    return [2.0 ** (-8.0 * (h + 1) / ATTN_HEADS) for h in range(ATTN_HEADS)]


def _attn_fwd(proj, after=()):
    s = proj.shape[0]
    t = SEQ_TILE
    hd = ATTN_HEAD_DIM
    hp = ATTN_FWD_HEADS_PER_STEP
    ng = ATTN_HEADS // hp
    w = hp * hd
    scale = 1.0 / math.sqrt(hd)
    slopes = _alibi_slopes()

    def body(q_ref, k_ref, v_ref, *rest):
        mix_ref, o_ref, lse_ref, kb, vb, bias_tab, log_count_tab = rest[len(after):]
        g = pl.program_id(0)
        i = pl.program_id(1)

        @pl.when((g == 0) & (i == 0))
        def _():
            _fill_attn_log_count(log_count_tab)

        @pl.when(i == 0)
        def _():
            kb[...] = k_ref[...].astype(BF16)
            vb[...] = v_ref[...].astype(BF16)
            for u in range(hp):
                _fill_attn_bias(bias_tab.at[u], log_count_tab, _select_by_index(g * hp + u, slopes))

        qs = [q_ref[:, u * hd:(u + 1) * hd].astype(BF16) for u in range(hp)]

        def step(j, carry):
            rows = pl.ds(pl.multiple_of(j * t, t), t)
            out = []
            for u in range(hp):
                m_i, l_i, acc = carry[u]
                lanes = slice(u * hd, (u + 1) * hd)
                sc = lax.dot_general(qs[u], kb[rows, lanes], _NT_DIMS, preferred_element_type=F32) * scale
                sc = sc + bias_tab[u, i - j]
                m_new = jnp.maximum(m_i, jnp.max(sc, axis=-1, keepdims=True))
                p = jnp.exp(sc - m_new)
                alpha = jnp.exp(m_i - m_new)
                l_new = alpha * l_i + jnp.sum(p, axis=-1, keepdims=True)
                acc = alpha * acc + jnp.dot(p.astype(BF16), vb[rows, lanes], preferred_element_type=F32)
                out.append((m_new, l_new, acc))
            return tuple(out)

        init = (jnp.full((t, 1), NEG_BIG, F32), jnp.zeros((t, 1), F32), jnp.zeros((t, hd), F32))
        final = lax.fori_loop(0, i + 1, step, (init,) * hp)
        for u in range(hp):
            m_i, l_i, acc = final[u]
            lanes = slice(u * hd, (u + 1) * hd)
            out = acc / l_i
            o_ref[:, lanes] = out
            mix_ref[:, lanes] = out.astype(BF16)
            lse_ref[:, lanes] = jnp.broadcast_to(m_i + jnp.log(l_i), (t, hd))

    return pl.pallas_call(
        body, name="attn_fwd", grid=(ng, s // t),
        in_specs=[pl.BlockSpec((t, w), lambda g, i: (i, g)),
                  pl.BlockSpec((s, w), lambda g, i: (0, ng + g)),
                  pl.BlockSpec((s, w), lambda g, i: (0, 2 * ng + g))] + [ANY] * len(after),
        out_specs=[pl.BlockSpec((None, t, w), lambda g, i: (0, i, g))] + [pl.BlockSpec((t, w), lambda g, i: (i, g))] * 2,
        out_shape=[jax.ShapeDtypeStruct((2, s, ATTN_WIDTH), BF16),
                   jax.ShapeDtypeStruct((s, ATTN_WIDTH), F32),
                   jax.ShapeDtypeStruct((s, ATTN_WIDTH), F32)],
        scratch_shapes=[pltpu.VMEM((s, w), BF16), pltpu.VMEM((s, w), BF16), pltpu.VMEM((hp, s // t, t, t), F32),
                        pltpu.VMEM((s // t, t, t), F32)],
        compiler_params=_params(("arbitrary", "arbitrary")),
    )(proj, proj, proj, *after)


def _attn_bwd(proj, attn_out, lse, dmixed, after=()):
    after = tuple(after)
    s = proj.shape[0]
    t = SEQ_TILE
    nt = s // t
    hd = ATTN_HEAD_DIM
    hp = ATTN_HEADS_PER_STEP
    ng = ATTN_HEADS // hp
    w = hp * hd
    scale = 1.0 / math.sqrt(hd)
    slopes = _alibi_slopes()

    def body(q_ref, k_ref, v_ref, o_ref, lse_ref, do_ref, *rest):
        dsec_ref, qb, kb, vb, dob, dsum, dq_acc, bias_tab, log_count_tab = rest[len(after):]
        g = pl.program_id(0)

        @pl.when(g == 0)
        def _():
            _fill_attn_log_count(log_count_tab)

        qb[...] = q_ref[...].astype(BF16)
        kb[...] = k_ref[...].astype(BF16)
        vb[...] = v_ref[...].astype(BF16)
        dob[...] = do_ref[...].astype(BF16)
        for u in range(hp):
            lanes = slice(u * hd, (u + 1) * hd)
            _fill_attn_bias(bias_tab.at[u], log_count_tab, _select_by_index(g * hp + u, slopes))
            rowsum = jnp.sum(do_ref[:, lanes] * o_ref[:, lanes], axis=-1, keepdims=True)
            dsum[:, lanes] = jnp.broadcast_to(rowsum, (s, hd))
        dq_acc[...] = jnp.zeros((s, w), F32)

        def over_keys(j, _):
            krows = pl.ds(pl.multiple_of(j * t, t), t)

            def over_queries(i, carry):
                qrows = pl.ds(pl.multiple_of(i * t, t), t)
                out = []
                for u in range(hp):
                    dk, dv = carry[u]
                    lanes = slice(u * hd, (u + 1) * hd)
                    qi, doi = qb[qrows, lanes], dob[qrows, lanes]
                    kj, vj = kb[krows, lanes], vb[krows, lanes]
                    lse_i = lse_ref[qrows, lanes][:, :1]
                    dsum_i = dsum[qrows, lanes][:, :1]
                    sc = lax.dot_general(qi, kj, _NT_DIMS, preferred_element_type=F32) * scale
                    p = jnp.exp(sc + bias_tab[u, i - j] - lse_i)
                    dp = lax.dot_general(doi, vj, _NT_DIMS, preferred_element_type=F32)
                    ds = (p * (dp - dsum_i)).astype(BF16)
                    dv = dv + lax.dot_general(p.astype(BF16), doi, _TN_DIMS, preferred_element_type=F32)
                    dk = dk + lax.dot_general(ds, qi, _TN_DIMS, preferred_element_type=F32)
                    dq_acc[qrows, lanes] += jnp.dot(ds, kj, preferred_element_type=F32)
                    out.append((dk, dv))
                return tuple(out)

            zero = jnp.zeros((t, hd), F32)
            final = lax.fori_loop(j, nt, over_queries, ((zero, zero),) * hp)
            for u in range(hp):
                lanes = slice(u * hd, (u + 1) * hd)
                dsec_ref[1, krows, lanes] = (final[u][0] * scale).astype(BF16)
                dsec_ref[2, krows, lanes] = final[u][1].astype(BF16)
            return 0

        lax.fori_loop(0, nt, over_keys, 0)
        dsec_ref[0] = (dq_acc[...] * scale).astype(BF16)

    def col(off):
        return pl.BlockSpec((s, w), lambda g: (0, off + g))

    return pl.pallas_call(
        body, name="attn_bwd", grid=(ng,),
        in_specs=[col(0), col(ng), col(2 * ng), col(0), col(0), col(0)] + [ANY] * len(after),
        out_specs=pl.BlockSpec((4, s, w), lambda g: (0, 0, g)),
        out_shape=jax.ShapeDtypeStruct((8, s, ATTN_WIDTH), BF16),
        scratch_shapes=[pltpu.VMEM((s, w), BF16)] * 4 + [pltpu.VMEM((s, w), F32)] * 2
        + [pltpu.VMEM((hp, nt, t, t), F32), pltpu.VMEM((nt, t, t), F32)],
        compiler_params=_params(("arbitrary",)),
    )(proj, proj, proj, attn_out, lse, dmixed, *after)


def _ret_log_gammas():
    return [math.log(1.0 - 2.0 ** (-5.0 - h)) for h in range(RET_HEADS)]


def _ret_decay(delta, log_gamma):
    dec = jnp.exp(delta.astype(F32) * log_gamma) * (1.0 / math.sqrt(RET_HEAD_DIM))
    return jnp.where(delta >= 0, dec, 0.0)


def _ret_fwd(proj, mixed, after=()):
    after = tuple(after)
    s = proj.shape[0]
    t = SEQ_TILE
    hd = RET_HEAD_DIM
    nh = RET_HEADS
    log_gammas = _ret_log_gammas()
    c0 = 3 * ATTN_WIDTH // hd

    def body(q_ref, k_ref, v_ref, g_ref, *rest):
        mix_ref, raw_ref, kb, vb, decay_tab = rest[1 + len(after):]
        h = pl.program_id(0)
        i = pl.program_id(1)

        @pl.when(i == 0)
        def _():
            kb[...] = k_ref[...].astype(BF16)
            vb[...] = v_ref[...].astype(BF16)
            _fill_ret_decay(decay_tab, _select_by_index(h, log_gammas))

        q = q_ref[...].astype(BF16)

        def step(j, acc):
            rows = pl.ds(pl.multiple_of(j * t, t), t)
            sc = lax.dot_general(q, kb[rows, :], _NT_DIMS, preferred_element_type=F32) * decay_tab[i - j]
            return acc + jnp.dot(sc.astype(BF16), vb[rows, :], preferred_element_type=F32)

        ret = lax.fori_loop(0, i + 1, step, jnp.zeros((t, hd), F32))
        raw_ref[...] = ret
        r = lax.rsqrt(jnp.mean(ret * ret, axis=-1, keepdims=True) + NORM_EPS)
        g = g_ref[...]
        mix_ref[...] = (g * _sigmoid(g) * (ret * r)).astype(BF16)

    return pl.pallas_call(
        body, name="ret_fwd", grid=(nh, s // t),
        in_specs=[pl.BlockSpec((t, hd), lambda h, i: (i, c0 + h)),
                  pl.BlockSpec((s, hd), lambda h, i: (0, c0 + nh + h)),
                  pl.BlockSpec((s, hd), lambda h, i: (0, c0 + 2 * nh + h)),
                  pl.BlockSpec((t, hd), lambda h, i: (i, c0 + 3 * nh + h))] + [ANY] * (1 + len(after)),
        out_specs=[pl.BlockSpec((None, t, hd), lambda h, i: (1, i, h)), pl.BlockSpec((t, hd), lambda h, i: (i, h))],
        out_shape=[jax.ShapeDtypeStruct(mixed.shape, BF16), jax.ShapeDtypeStruct((s, RET_WIDTH), F32)],
        input_output_aliases={4: 0},
        scratch_shapes=[pltpu.VMEM((s, hd), BF16), pltpu.VMEM((s, hd), BF16), pltpu.VMEM((s // t, t, t), F32)],
        compiler_params=_params(("arbitrary", "arbitrary")),
    )(proj, proj, proj, proj, mixed, *after)


def _ret_bwd(proj, ret_raw, dmixed, dsec, after=()):
    after = tuple(after)
    s = proj.shape[0]
    t = SEQ_TILE
    nt = s // t
    hd = RET_HEAD_DIM
    nh = RET_HEADS
    log_gammas = _ret_log_gammas()
    c0 = 3 * ATTN_WIDTH // hd
    mixed_blocks = ATTN_WIDTH // hd

    def body(q_ref, k_ref, v_ref, g_ref, raw_ref, dmix_ref, *rest):
        dsec_ref, qb, kb, vb, dretb, dq_acc, decay_tab = rest[1 + len(after):]
        h = pl.program_id(0)
        _fill_ret_decay(decay_tab, _select_by_index(h, log_gammas))
        qb[...] = q_ref[...].astype(BF16)
        kb[...] = k_ref[...].astype(BF16)
        vb[...] = v_ref[...].astype(BF16)
        ret = raw_ref[...]
        r = lax.rsqrt(jnp.mean(ret * ret, axis=-1, keepdims=True) + NORM_EPS)
        normed = ret * r
        g = g_ref[...]
        sg = _sigmoid(g)
        dout = dmix_ref[...]
        dsec_ref[3] = (dout * normed * sg * (1.0 + g * (1.0 - sg))).astype(BF16)
        dn = dout * g * sg
        dret = r * (dn - normed * jnp.mean(dn * normed, axis=-1, keepdims=True))
        dretb[...] = dret.astype(BF16)
        dq_acc[...] = jnp.zeros((s, hd), F32)

        def over_keys(j, _):
            krows = pl.ds(pl.multiple_of(j * t, t), t)
            kj = kb[krows, :]
            vj = vb[krows, :]

            def over_queries(i, carry):
                dk, dv = carry
                qrows = pl.ds(pl.multiple_of(i * t, t), t)
                qi = qb[qrows, :]
                doi = dretb[qrows, :]
                dec = decay_tab[i - j]
                a = (lax.dot_general(qi, kj, _NT_DIMS, preferred_element_type=F32) * dec).astype(BF16)
                da = (lax.dot_general(doi, vj, _NT_DIMS, preferred_element_type=F32) * dec).astype(BF16)
                dv = dv + lax.dot_general(a, doi, _TN_DIMS, preferred_element_type=F32)
                dk = dk + lax.dot_general(da, qi, _TN_DIMS, preferred_element_type=F32)
                dq_acc[qrows, :] += jnp.dot(da, kj, preferred_element_type=F32)
                return dk, dv

            zero = jnp.zeros((t, hd), F32)
            dk, dv = lax.fori_loop(j, nt, over_queries, (zero, zero))
            dsec_ref[1, krows, :] = dk.astype(BF16)
            dsec_ref[2, krows, :] = dv.astype(BF16)
            return 0

        lax.fori_loop(0, nt, over_keys, 0)
        dsec_ref[0] = dq_acc[...].astype(BF16)

    def col(off):
        return pl.BlockSpec((s, hd), lambda h: (0, off + h))

    return pl.pallas_call(
        body, name="ret_bwd", grid=(nh,),
        in_specs=[col(c0), col(c0 + nh), col(c0 + 2 * nh), col(c0 + 3 * nh), col(0), col(mixed_blocks)]
        + [ANY] * (1 + len(after)),
        out_specs=pl.BlockSpec((4, s, hd), lambda h: (1, 0, h)),
        out_shape=jax.ShapeDtypeStruct(dsec.shape, BF16),
        input_output_aliases={6: 0},
        scratch_shapes=[pltpu.VMEM((s, hd), BF16)] * 4 + [pltpu.VMEM((s, hd), F32)]
        + [pltpu.VMEM((nt, t, t), F32)],
        compiler_params=_params(("arbitrary",)),
    )(proj, proj, proj, proj, ret_raw, dmixed, dsec, *after)


_FLIPS = (2, 1, 3)


def _other_chips(x, y):
    return [(1 - x, y), (x, 1 - y), (1 - x, 1 - y)]


_HBM = pl.BlockSpec(memory_space=pltpu.HBM)
_SEM = pl.BlockSpec(memory_space=pltpu.SEMAPHORE)
_EFFECT = pltpu.SideEffectType.DATAFLOW_SIDE_EFFECTING


def _in_hbm(a):
    return pltpu.with_memory_space_constraint(a, pltpu.HBM)


def _weight_view(w, column_sharded):
    if column_sharded:
        return w.reshape(2, w.shape[0] // 2, w.shape[1])
    return w.reshape(N_CHIPS, 2, w.shape[0] // (2 * N_CHIPS), w.shape[1])


def _weight_unview(v):
    if v.ndim == 3:
        return v.reshape(2 * v.shape[1], v.shape[2])
    return v.reshape(N_CHIPS * 2 * v.shape[2], v.shape[3])


def _weight_region(buf, shard, half):
    if len(buf.shape) == 3:
        cols = buf.shape[2] // N_CHIPS
        return buf.at[half, :, pl.ds(shard * cols, cols)]
    return buf.at[shard, half]


def _remote(where, send_sem, recv_sem, to):
    return pltpu.make_async_remote_copy(src_ref=where, dst_ref=where, send_sem=send_sem, recv_sem=recv_sem,
                                        device_id=to, device_id_type=MESH)


def _for_my_shard(fn):
    x, y, _ = _place()
    for ss in range(N_CHIPS):
        pl.when(2 * x + y == ss)(functools.partial(fn, ss))


def _gather_forward(views, which, send_sems, recv_sems, after, name, base=0, sibling_id=None):
    n_w = len(views)
    which = [base // 3 + w for w in which] if base % 3 == 0 else None
    assert which is not None, "base must be a multiple of 3"

    def body(*refs):
        if sibling_id is not None:
            _sibling_handshake()
        send_in, recv_in = refs[n_w:n_w + 2]
        fwd_send, fwd_recv = refs[n_w + 3:n_w + 5]
        bufs = refs[n_w + 5:]
        x, y, c = _place()
        sibling = (x, y, 1 - c)

        def forward(ss):
            for i, w in enumerate(which):
                for j in range(3):
                    landed = _weight_region(bufs[i], ss ^ _FLIPS[j], c)
                    _remote(landed, send_in.at[3 * w + j], recv_in.at[3 * w + j], sibling).wait_recv()
                    _remote(landed, fwd_send.at[3 * i + j], fwd_recv.at[3 * i + j], sibling).start()

        _for_my_shard(forward)
        for i, w in enumerate(which):
            for j in range(3):
                _remote(_weight_region(bufs[i], 0, 0), send_in.at[3 * w + j], recv_in.at[3 * w + j],
                        sibling).wait_send()

    return pl.pallas_call(
        body, name=name,
        in_specs=[_HBM] * n_w + [_SEM, _SEM, ANY], out_specs=[_SEM, _SEM] + [_HBM] * n_w,
        out_shape=[pltpu.SemaphoreType.DMA((3 * n_w,)), pltpu.SemaphoreType.DMA((3 * n_w,))]
        + [pltpu.HBM(v.shape, BF16) for v in views],
        input_output_aliases={w: 2 + w for w in range(n_w)},
        compiler_params=pltpu.CompilerParams(has_side_effects=_EFFECT, collective_id=sibling_id),
    )(*views, send_sems, recv_sems, after)


def _gather_end(views, fwd_send, fwd_recv, after, name):
    n_w = len(views)

    def body(*refs):
        fwd_send_ref, fwd_recv_ref = refs[n_w:n_w + 2]
        bufs = refs[n_w + 3:]
        x, y, c = _place()
        for i in range(n_w):
            for j in range(3):
                cp = _remote(_weight_region(bufs[i], 0, 0), fwd_send_ref.at[3 * i + j], fwd_recv_ref.at[3 * i + j],
                             (x, y, 1 - c))
                cp.wait_recv()
                cp.wait_send()

    outs = pl.pallas_call(
        body, name=name,
        in_specs=[_HBM] * n_w + [_SEM, _SEM, ANY], out_specs=[_HBM] * n_w,
        out_shape=[pltpu.HBM(v.shape, BF16) for v in views],
        input_output_aliases={w: w for w in range(n_w)},
        compiler_params=pltpu.CompilerParams(has_side_effects=_EFFECT),
    )(*views, fwd_send, fwd_recv, after)
    return [_weight_unview(o) for o in outs]


def _comm_call(name, bufs, sem_pairs, after, n_new, fn, sibling_id=None):
    n, n_sem, after = len(bufs), 2 * len(sem_pairs), tuple(after)
    n_out_sem = 2 if n_new else 0

    def body(*refs):
        if sibling_id is not None:
            _sibling_handshake()
        sems = refs[n:n + n_sem]
        outs = refs[n + n_sem + len(after):]
        new = outs[:n_out_sem] if n_new else (None, None)
        fn(outs[n_out_sem:], [(sems[2 * i], sems[2 * i + 1]) for i in range(len(sem_pairs))], *new)

    res = pl.pallas_call(
        body, name=name,
        in_specs=[_HBM] * n + [_SEM] * n_sem + [ANY] * len(after),
        out_specs=[_SEM] * n_out_sem + [_HBM] * n,
        out_shape=[pltpu.SemaphoreType.DMA((n_new,))] * n_out_sem + [pltpu.HBM(b.shape, b.dtype) for b in bufs],
        input_output_aliases={i: n_out_sem + i for i in range(n)},
        compiler_params=pltpu.CompilerParams(has_side_effects=_EFFECT, collective_id=sibling_id),
    )(*bufs, *[s for pair in sem_pairs for s in pair], *after)
    return list(res[:n_out_sem]), list(res[n_out_sem:])


def _quarter(piece, q):
    rows = piece.shape[0] // 2
    return piece.at[pl.ds(q * rows, rows)]


def _gather_in_start(view, name):
    def fn(bufs, _, send, recv):
        x, y, c = _place()

        def go(ss):
            for j, chip in enumerate(_other_chips(x, y)[:2]):
                _remote(_weight_region(bufs[0], ss, c), send.at[j], recv.at[j], (*chip, c)).start()

        _for_my_shard(go)

    sems, (view,) = _comm_call(name, [_in_hbm(view)], [], (), 2, fn)
    return sems, view


def _gather_out_gate_start(v_out, v_gate, after, name):
    def fn(bufs, _, send, recv):
        x, y, c = _place()
        chips = _other_chips(x, y)

        def go(ss):
            for j in range(3):
                _remote(_weight_region(bufs[0], ss, c), send.at[j], recv.at[j], (*chips[j], c)).start()
            for j in range(2):
                _remote(_weight_region(bufs[1], ss, c), send.at[3 + j], recv.at[3 + j], (*chips[j], c)).start()

        _for_my_shard(go)

    sems, views = _comm_call(name, [_in_hbm(v_out), _in_hbm(v_gate)], [], after, 5, fn)
    return sems, views


def _gather_relay(view, started, base, after, name, then=None, then_peers=0, then_first=False):
    n_new = 6 + then_peers if then_peers else 4

    def fn(bufs, pairs, send, recv):
        (send_in, recv_in), = pairs
        x, y, c = _place()
        chips = _other_chips(x, y)
        sibling = (x, y, 1 - c)

        def go(ss):
            def start_then():
                for j in range(then_peers):
                    _remote(_weight_region(bufs[1], ss, c), send.at[6 + j], recv.at[6 + j], (*chips[j], c)).start()

            if then_first:
                start_then()
            landed = [_weight_region(bufs[0], ss ^ _FLIPS[j], c) for j in range(2)]
            for j in range(2):
                _remote(landed[j], send_in.at[base + j], recv_in.at[base + j], sibling).wait_recv()
            for j in range(2):
                _remote(_quarter(landed[j], j), send.at[j], recv.at[j], (*chips[1 - j], c)).start()
            for j in range(2):
                _remote(landed[j], send.at[2 + j], recv.at[2 + j], sibling).start()
            if not then_first:
                start_then()

        _for_my_shard(go)
        for j in range(2):
            _remote(_weight_region(bufs[0], 0, 0), send_in.at[base + j], recv_in.at[base + j], sibling).wait_send()

    views = [view] if then is None else [view, _in_hbm(then)]
    sems, views = _comm_call(name, views, [started], after, n_new, fn)
    return sems, views


def _gather_in_neighbours_end(view, relayed, after, name):
    def fn(bufs, pairs, *_):
        (send, recv), = pairs
        x, y, c = _place()
        for j in range(2):
            cp = _remote(_weight_region(bufs[0], 0, 0), send.at[2 + j], recv.at[2 + j], (x, y, 1 - c))
            cp.wait_recv()
            cp.wait_send()

    _, (view,) = _comm_call(name, [view], [relayed], after, 0, fn)
    return view


def _gather_in_diagonal(view, relayed, after, name, sibling_id):
    def fn(bufs, pairs, send, recv):
        (send_in, recv_in), = pairs
        x, y, c = _place()
        sibling = (x, y, 1 - c)
        any_quarter = _quarter(_weight_region(bufs[0], 0, 0), 0)
        for j in range(2):
            cp = _remote(any_quarter, send_in.at[j], recv_in.at[j], sibling)
            cp.wait_recv()
            cp.wait_send()

        def go(ss):
            _remote(_weight_region(bufs[0], ss ^ _FLIPS[2], c), send.at[0], recv.at[0], sibling).start()

        _for_my_shard(go)

    sems, (view,) = _comm_call(name, [view], [relayed], after, 1, fn, sibling_id=sibling_id)
    return sems, view


def _gather_in_diagonal_end(view, forwarded, after, name):
    def fn(bufs, pairs, *_):
        (send, recv), = pairs
        x, y, c = _place()
        cp = _remote(_weight_region(bufs[0], 0, 0), send.at[0], recv.at[0], (x, y, 1 - c))
        cp.wait_recv()
        cp.wait_send()

    _, (view,) = _comm_call(name, [view], [forwarded], after, 0, fn)
    return view


def _in_proj_shard(h1, wi, proj, shard_arr, name):
    s, d = h1.shape
    n = wi.shape[1]
    tn = 256
    blocks = n // (N_CHIPS * tn)
    given = [] if proj is None else [proj]

    def body(shard_ref, h_ref, w_ref, *rest):
        del shard_ref
        rest[-1][...] = jnp.dot(h_ref[...], w_ref[...], preferred_element_type=F32)

    grid_spec = pltpu.PrefetchScalarGridSpec(
        num_scalar_prefetch=1, grid=(blocks,),
        in_specs=[pl.BlockSpec((s, d), lambda j, shard_ref: (0, 0)),
                  pl.BlockSpec((d, tn), lambda j, shard_ref: (0, shard_ref[0] * blocks + j))] + [ANY] * len(given),
        out_specs=pl.BlockSpec((s, tn), lambda j, shard_ref: (0, shard_ref[0] * blocks + j)))
    return pl.pallas_call(
        body, name=name, grid_spec=grid_spec,
        out_shape=jax.ShapeDtypeStruct((s, n), F32),
        input_output_aliases={3: 0} if given else {},
        compiler_params=_params(("arbitrary",)),
    )(shard_arr, h1, wi, *given)


def _sibling_handshake():
    x, y, c = _place()
    barrier = pltpu.get_barrier_semaphore()
    pl.semaphore_signal(barrier, inc=1, device_id=(x, y, 1 - c), device_id_type=MESH)
    pl.semaphore_wait(barrier, 1)


def _chips_handshake():
    x, y, c = _place()
    barrier = pltpu.get_barrier_semaphore()
    for cx, cy in _other_chips(x, y):
        pl.semaphore_signal(barrier, inc=1, device_id=(cx, cy, c), device_id_type=MESH)
    pl.semaphore_wait(barrier, N_CHIPS - 1)


def _split_start(name, bufs, n_sems, copies, sibling_id=None, chips_id=None):
    n = len(bufs)
    assert sibling_id is None or chips_id is None

    def body(*refs):
        if sibling_id is not None:
            _sibling_handshake()
        if chips_id is not None:
            _chips_handshake()
        send_sems, recv_sems = refs[n:n + 2]
        for cp in copies(refs[n + 2:], send_sems, recv_sems):
            cp.start()

    outs = pl.pallas_call(
        body, name=name,
        in_specs=[_HBM] * n, out_specs=[_SEM, _SEM] + [_HBM] * n,
        out_shape=[pltpu.SemaphoreType.DMA((n_sems,)), pltpu.SemaphoreType.DMA((n_sems,))]
        + [pltpu.HBM(b.shape, b.dtype) for b in bufs],
        input_output_aliases={i: 2 + i for i in range(n)},
        compiler_params=pltpu.CompilerParams(has_side_effects=_EFFECT,
                                             collective_id=sibling_id if chips_id is None else chips_id),
    )(*[_in_hbm(b) for b in bufs])
    return outs[0], outs[1], list(outs[2:])


def _split_wait(name, bufs, send_sems, recv_sems, copies, after):
    n = len(bufs)
    after = tuple(after) if isinstance(after, (list, tuple)) else (after,)

    def body(*refs):
        send_ref, recv_ref = refs[n:n + 2]
        for cp in copies(refs[n + 2 + len(after):], send_ref, recv_ref):
            cp.wait()

    return list(pl.pallas_call(
        body, name=name,
        in_specs=[_HBM] * n + [_SEM, _SEM] + [ANY] * len(after), out_specs=[_HBM] * n,
        out_shape=[pltpu.HBM(b.shape, b.dtype) for b in bufs],
        input_output_aliases={i: i for i in range(n)},
        compiler_params=pltpu.CompilerParams(has_side_effects=_EFFECT),
    )(*bufs, send_sems, recv_sems, *after))


def _halves_copies(n_w):
    def copies(bufs, send_sems, recv_sems):
        x, y, c = _place()
        out = []
        for w in range(n_w):
            view, land = bufs[w], bufs[n_w + w]
            src = view.at[1 - c] if len(view.shape) == 3 else view.at[:, 1 - c]
            out.append(pltpu.make_async_remote_copy(
                src_ref=src, dst_ref=land, send_sem=send_sems.at[w], recv_sem=recv_sems.at[w],
                device_id=(x, y, 1 - c), device_id_type=MESH))
        return out
    return copies


def _pieces_copies(n_w):
    def copies(bufs, send_sems, recv_sems):
        x, y, c = _place()
        out = []
        for w in range(n_w):
            for j, (cx, cy) in enumerate(_other_chips(x, y)):
                out.append(pltpu.make_async_remote_copy(
                    src_ref=bufs[w].at[2 * cx + cy], dst_ref=bufs[n_w + w].at[j],
                    send_sem=send_sems.at[3 * w + j], recv_sem=recv_sems.at[3 * w + j],
                    device_id=(cx, cy, c), device_id_type=MESH))
        return out
    return copies


def _join_copies(n_w):
    def copies(bufs, send_sems, recv_sems):
        x, y, c = _place()
        return [pltpu.make_async_remote_copy(
            src_ref=bufs[w].at[c], dst_ref=bufs[w].at[c], send_sem=send_sems.at[w], recv_sem=recv_sems.at[w],
            device_id=(x, y, 1 - c), device_id_type=MESH) for w in range(n_w)]
    return copies


def _halves_landing(view):
    shape = view.shape[1:] if view.ndim == 3 else (N_CHIPS,) + view.shape[2:]
    return lax.empty(shape, BF16)


_SIBLING_IDS = {"halves_down": 1, "halves_gate_up": 2, "halves_out": 3, "halves_in": 4,
                "join_down": 5, "join_gate_up": 6, "join_out": 7, "join_in": 8,
                "diagonal_in": 9, "diagonal_gate": 10, "diagonal_up": 11, "forward_out": 12, "forward_down": 13,
                "pieces_down": 14, "pieces_gate_up": 15, "pieces_out": 16, "pieces_in": 17}


def _halves_start(tag, grads, column_sharded):
    views = [_weight_view(g, cs) for g, cs in zip(grads, column_sharded)]
    n = len(views)
    return _split_start("halves_start_" + tag, views + [_halves_landing(v) for v in views], n, _halves_copies(n),
                        sibling_id=_SIBLING_IDS["halves_" + tag])


def _halves_wait(tag, state, after):
    send_sems, recv_sems, bufs = state
    n = len(bufs) // 2
    bufs = _split_wait("halves_wait_" + tag, bufs, send_sems, recv_sems, _halves_copies(n), after)
    return bufs[:n], bufs[n:]


def _pieces_start(tag, pieces):
    n = len(pieces)
    landing = [lax.empty((3,) + p.shape[1:], BF16) for p in pieces]
    return _split_start("pieces_start_" + tag, list(pieces) + landing, 3 * n, _pieces_copies(n),
                        chips_id=_SIBLING_IDS["pieces_" + tag])


def _pieces_wait(tag, state, after):
    send_sems, recv_sems, bufs = state
    n = len(bufs) // 2
    bufs = _split_wait("pieces_wait_" + tag, bufs, send_sems, recv_sems, _pieces_copies(n), after)
    return bufs[:n], bufs[n:]


def _join_start(tag, shards):
    n = len(shards)
    return _split_start("join_start_" + tag, list(shards), n, _join_copies(n), sibling_id=_SIBLING_IDS["join_" + tag])


def _join_wait(tag, state, after):
    send_sems, recv_sems, bufs = state
    bufs = _split_wait("join_wait_" + tag, bufs, send_sems, recv_sems, _join_copies(len(bufs)), after)
    return [b.reshape(2 * b.shape[1], b.shape[2]) for b in bufs]


def _chip_sum_col(g3, sib, c_arr, name):
    _, hk, n = g3.shape
    cols = n // N_CHIPS
    tr = _row_tile(hk, cols * 2, limit=4 * 1024 * 1024)

    def body(c_ref, g_ref, s_ref, o_ref):
        del c_ref
        o_ref[...] = (g_ref[...].astype(F32) + s_ref[...].astype(F32)).astype(BF16)

    grid_spec = pltpu.PrefetchScalarGridSpec(
        num_scalar_prefetch=1, grid=(N_CHIPS, hk // tr),
        in_specs=[pl.BlockSpec((None, tr, cols), lambda p, r, c_ref: (c_ref[0], r, p)),
                  pl.BlockSpec((tr, cols), lambda p, r, c_ref: (r, p))],
        out_specs=pl.BlockSpec((None, tr, cols), lambda p, r, c_ref: (p, r, 0)))
    return pl.pallas_call(
        body, name=name, grid_spec=grid_spec,
        out_shape=jax.ShapeDtypeStruct((N_CHIPS, hk, cols), BF16),
        compiler_params=_params(("parallel", "parallel")),
    )(c_arr, g3, sib)


def _chip_sum_row(g4, sib, c_arr, name):
    _, _, hr, n = g4.shape
    tr = _row_tile(hr, n * 2, limit=4 * 1024 * 1024)

    def body(c_ref, g_ref, s_ref, o_ref):
        del c_ref
        o_ref[...] = (g_ref[...].astype(F32) + s_ref[...].astype(F32)).astype(BF16)

    grid_spec = pltpu.PrefetchScalarGridSpec(
        num_scalar_prefetch=1, grid=(N_CHIPS, hr // tr),
        in_specs=[pl.BlockSpec((None, None, tr, n), lambda p, r, c_ref: (p, c_ref[0], r, 0)),
                  pl.BlockSpec((None, tr, n), lambda p, r, c_ref: (p, r, 0))],
        out_specs=pl.BlockSpec((None, tr, n), lambda p, r, c_ref: (p, r, 0)))
    return pl.pallas_call(
        body, name=name, grid_spec=grid_spec,
        out_shape=jax.ShapeDtypeStruct((N_CHIPS, hr, n), BF16),
        compiler_params=_params(("parallel", "parallel")),
    )(c_arr, g4, sib)


def _sum_pieces(pieces, received, place_arr, name):
    _, r, n = pieces.shape
    tr = _row_tile(r, n * 4, limit=4 * 1024 * 1024)

    def body(p_ref, own_ref, r0_ref, r1_ref, r2_ref, o_ref):
        del p_ref
        acc = own_ref[...].astype(F32) + r0_ref[...].astype(F32)
        acc = acc + r1_ref[...].astype(F32)
        o_ref[...] = acc + r2_ref[...].astype(F32)

    def recv_spec(j):
        return pl.BlockSpec((None, tr, n), lambda i, p_ref: (j, i, 0))

    grid_spec = pltpu.PrefetchScalarGridSpec(
        num_scalar_prefetch=1, grid=(r // tr,),
        in_specs=[pl.BlockSpec((None, tr, n), lambda i, p_ref: (p_ref[0], i, 0)),
                  recv_spec(0), recv_spec(1), recv_spec(2)],
        out_specs=pl.BlockSpec((None, tr, n), lambda i, p_ref: (p_ref[1], i, 0)))
    return pl.pallas_call(
        body, name=name, grid_spec=grid_spec,
        out_shape=jax.ShapeDtypeStruct((2, r, n), F32),
        compiler_params=_params(("parallel",)),
    )(place_arr, pieces, received, received, received)


def _norm_weights_step(parts, w, m, v, after=()):
    rows, d = parts.shape
    after = tuple(after)

    def body(p_ref, w_ref, m_ref, v_ref, *rest):
        g_ref, d_ref, mo_ref, vo_ref, gathered, send_sems, recv_sems = rest[len(after):]
        x, y, c = _place()
        me = 4 * x + 2 * y + c
        gathered[me] = p_ref[...]
        copies = []
        for k in range(1, N_DEV):
            peer = (x ^ ((k >> 2) & 1), y ^ ((k >> 1) & 1), c ^ (k & 1))
            copies.append(pltpu.make_async_remote_copy(
                src_ref=p_ref, dst_ref=gathered.at[me], send_sem=send_sems.at[k - 1],
                recv_sem=recv_sems.at[k - 1], device_id=peer, device_id_type=MESH))
        for cp in copies:
            cp.start()
        for cp in copies:
            cp.wait()
        g = gathered[0]
        for k in range(1, N_DEV):
            g = g + gathered[k]
        delta, m_new, v_new = _adamw_math(w_ref[...], g, m_ref[...], v_ref[...])
        g_ref[...] = g
        d_ref[...] = delta
        mo_ref[...] = m_new
        vo_ref[...] = v_new

    vmem = pl.BlockSpec(memory_space=pltpu.VMEM)
    shp = jax.ShapeDtypeStruct((rows, d), F32)
    return pl.pallas_call(
        body, name="norm_weights_step",
        in_specs=[vmem] * 4 + [ANY] * len(after), out_specs=[vmem] * 4, out_shape=[shp] * 4,
        scratch_shapes=[pltpu.VMEM((N_DEV, rows, d), F32), pltpu.SemaphoreType.DMA((N_DEV - 1,)),
                        pltpu.SemaphoreType.DMA((N_DEV - 1,))],
        compiler_params=pltpu.CompilerParams(has_side_effects=True),
    )(parts, w, m, v, *after)


def kernel(x, norm_mix_w, w_in, w_out, norm_ffn_w, w_gate, w_up, w_down, norm_final_w, loss_target, m_norm_mix_w, m_w_in, m_w_out, m_norm_ffn_w, m_w_gate, m_w_up, m_w_down, m_norm_final_w, v_norm_mix_w, v_w_in, v_w_out, v_norm_ffn_w, v_w_gate, v_w_up, v_w_down, v_norm_final_w):
    s, d = x.shape[1], x.shape[2]
    xs = x.reshape(s, d)
    target = loss_target.reshape(s, d)
    big = {"w_in": (w_in, m_w_in, v_w_in), "w_out": (w_out, m_w_out, v_w_out),
           "w_gate": (w_gate, m_w_gate, v_w_gate), "w_up": (w_up, m_w_up, v_w_up),
           "w_down": (w_down, m_w_down, v_w_down)}
    big = {k: tuple(a.reshape(a.shape[1:]) for a in t) for k, t in big.items()}
    col_names, row_names = ("w_in", "w_gate", "w_up"), ("w_out", "w_down")
    n_in = N_CHIPS * big["w_in"][0].shape[1]
    ffn = N_CHIPS * big["w_gate"][0].shape[1]
    mix = ATTN_WIDTH + RET_WIDTH
    c_arr = lax.axis_index("c").astype(I32).reshape(1)
    shard_arr = (2 * lax.axis_index("x") + lax.axis_index("y")).astype(I32).reshape(1)
    place_arr = jnp.concatenate([shard_arr, c_arr])

    def cast(k, after=()):
        return _weight_view(_cast_into_full(big[k][0], shard_arr, k in col_names, "cast_" + k, after), k in col_names)

    started_in, v_in = _gather_in_start(cast("w_in"), "gather_in_start")

    sec = ATTN_WIDTH

    def section(p, rows):
        return pl.BlockSpec((None, rows, sec), lambda i, j, kk: (p, i, 0))

    h1 = _rms_fwd(xs, norm_mix_w, "rms_mix_fwd", after=[v_in])
    my_shard = shard_arr[0]
    shard_of = [jnp.bitwise_xor(my_shard, f).astype(I32).reshape(1) for f in (0,) + _FLIPS]
    proj = _in_proj_shard(h1, _weight_unview(v_in), None, shard_of[0], "in_proj_own")
    early_views = [cast(k, after=[proj]) for k in ("w_out", "w_gate")]
    v_up, v_down = [cast(k, after=[proj]) for k in ("w_up", "w_down")]
    relayed_in, (v_in,) = _gather_relay(v_in, started_in, 0, early_views + [v_up, v_down], "gather_in_relay")
    started_og, (v_out, v_gate) = _gather_out_gate_start(*early_views, [v_in], "gather_out_gate_start")
    v_in = _gather_in_neighbours_end(v_in, relayed_in, [v_out], "gather_in_neighbours_end")
    proj = _in_proj_shard(h1, _weight_unview(v_in), proj, shard_of[1], "in_proj_x")
    proj = _in_proj_shard(h1, _weight_unview(v_in), proj, shard_of[2], "in_proj_y")
    forwarded_in, v_in = _gather_in_diagonal(v_in, relayed_in, [proj], "gather_in_diagonal",
                                             _SIBLING_IDS["diagonal_in"])
    wi = _weight_unview(_gather_in_diagonal_end(v_in, forwarded_in, [proj], "gather_in_diagonal_end"))
    proj = _in_proj_shard(h1, wi, proj, shard_of[3], "in_proj_diagonal")
    fs_o, fr_o, v_out = _gather_forward([v_out], [0], *started_og, proj, "gather_forward_out",
                                        sibling_id=_SIBLING_IDS["forward_out"])
    mixed, attn_o, lse = _attn_fwd(proj, after=[v_out])
    relayed_g, (v_gate, v_up) = _gather_relay(v_gate, started_og, 3, [attn_o], "gather_gate_relay",
                                              then=v_up, then_peers=2, then_first=True)
    mixed, ret_raw = _ret_fwd(proj, mixed, after=[v_gate])
    wo, = _gather_end([v_out], fs_o, fr_o, ret_raw, "gather_end_out")
    x1, = _matmul("out_proj", "nn", [mixed, mixed], [wo, wo], [0, 0], s, d, sec, s // 2, 512, sec, [xs], [F32],
                  _epi_residual, b_koff=[0, 1], a_specs=[section(0, s // 2), section(1, s // 2)])
    h2 = _rms_fwd(x1, norm_ffn_w, "rms_ffn_fwd")
    relayed_u, (v_up, v_down) = _gather_relay(v_up, relayed_g, 6, [h2], "gather_up_relay",
                                              then=v_down, then_peers=3)
    v_gate = _gather_in_neighbours_end(v_gate, relayed_g, [v_up], "gather_gate_neighbours_end")
    forwarded_g, v_gate = _gather_in_diagonal(v_gate, relayed_g, [v_up], "gather_gate_diagonal",
                                              _SIBLING_IDS["diagonal_gate"])
    v_up = _gather_in_neighbours_end(v_up, relayed_u, [v_gate], "gather_up_neighbours_end")
    wg = _weight_unview(_gather_in_diagonal_end(v_gate, forwarded_g, [v_up], "gather_gate_diagonal_end"))
    forwarded_u, v_up = _gather_in_diagonal(v_up, relayed_u, [wg], "gather_up_diagonal",
                                            _SIBLING_IDS["diagonal_up"])
    wu = _weight_unview(_gather_in_diagonal_end(v_up, forwarded_u, [wg], "gather_up_diagonal_end"))
    gate, up, act = _matmul("gate_up", "nn", [h2, h2], [wg, wu], [0, 1], s, ffn, d, s, 512, d, [],
                            [BF16, BF16, BF16], _epi_swiglu, a_single_buffer=True)
    fs, fr, v_down = _gather_forward([v_down], [0], *relayed_u, act, "gather_forward_down", base=6,
                                     sibling_id=_SIBLING_IDS["forward_down"])
    wd, = _gather_end([v_down], fs, fr, act, "gather_end_down")
    x2, = _matmul("down_proj", "nn", [act], [wd], [0], s, d, ffn, s // 2, 512, ffn, [x1], [F32],
                  _epi_residual)
    loss_row, dx2, dx2b, dwf = _final_norm_loss(x2, norm_final_w.reshape(1, d), target, "final_norm_loss")

    names = col_names + row_names
    grads, new = {}, {}

    def chip_sums(tag_names, views, sibs):
        return [(_chip_sum_col if k in col_names else _chip_sum_row)(v, sb, c_arr, "chip_sum_" + k)
                for k, v, sb in zip(tag_names, views, sibs)]

    def piece_sums(tag_names, pieces, received):
        return [_sum_pieces(p, r, place_arr, "sum_pieces_" + k) for k, p, r in zip(tag_names, pieces, received)]

    def update(k):
        new[k] = _adamw(big[k][0], grads[k], big[k][1], big[k][2], "adamw_" + k)

    dgate, dup = _matmul("d_act", "nt", [dx2b], [wd], [0], s, ffn, d, s, 512, d, [gate, up],
                         [BF16, BF16], _epi_swiglu_bwd, a_single_buffer=True)
    g_wd, = _matmul("g_w_down", "tn", [act], [dx2b], [0], ffn, d, s, 512, d, s, [], [BF16], _epi_plain)
    halves_d = _halves_start("down", [g_wd], [False])
    dh2 = _d_h2_streamed(dgate, dup, wg, wu, after=halves_d[2][-1:])
    pieces_d = _pieces_start("down", chip_sums(["w_down"], *_halves_wait("down", halves_d, dh2)))
    g_wg, g_wu = _matmul("g_w_gate_up", "tn", [h2, h2], [dgate, dup], [0, 1], d, ffn, s, 1024, 512, s, [],
                         [BF16, BF16], _epi_two, after=pieces_d[2][-1:])
    halves_gu = _halves_start("gate_up", [g_wg, g_wu], [True, True])
    dx1, dx1b, dw_ffn = _rms_bwd(x1, norm_ffn_w, dh2, dx2, "rms_ffn_bwd", after=halves_gu[2][-1:])

    dmixed, = _matmul("d_mixed", "nt", [dx1b], [wo], [0], s, mix, d, s // 2, 512, d, [], [F32], _epi_plain)
    pieces_gu = _pieces_start("gate_up", chip_sums(["w_gate", "w_up"], *_halves_wait("gate_up", halves_gu, dmixed)))
    per = sec // 512
    g_wo, = _matmul("g_w_out", "tn", [mixed], [dx1b], [0], mix, d, s, 512, d, s, [], [BF16], _epi_plain,
                    after=pieces_gu[2][-1:],
                    a_specs=[pl.BlockSpec((None, s, 512), lambda i, j, kk: (i // per, 0, i % per))])
    halves_o = _halves_start("out", [g_wo], [False])
    dsec = _attn_bwd(proj, attn_o, lse, dmixed, after=halves_o[2][-1:])
    pieces_o = _pieces_start("out", chip_sums(["w_out"], *_halves_wait("out", halves_o, dsec)))
    dsec = _ret_bwd(proj, ret_raw, dmixed, dsec, after=pieces_o[2][-1:])
    where = [0, 1, 2, 4, 5, 6, 7]
    n_sec = len(where)
    g_wi, = _matmul("g_w_in", "tn", [h1], [dsec], [0], d, n_in, s, 1024, sec, s, [], [BF16], _epi_plain,
                    b_specs=[pl.BlockSpec((None, s, sec), lambda i, j, kk: (j + (j >= 3).astype(I32), 0, 0))])
    halves_i = _halves_start("in", [g_wi], [True])
    dh1, = _matmul("d_h1", "nt", [dsec] * n_sec, [wi] * n_sec, [0] * n_sec, s, d, sec, s // 2, 256, sec, [], [F32],
                   _epi_plain, b_koff=list(range(n_sec)), after=halves_i[2][-1:],
                   a_specs=[section(p, s // 2) for p in where])
    pieces_i = _pieces_start("in", chip_sums(["w_in"], *_halves_wait("in", halves_i, dh1)))
    grad_x, _, dw_mix = _rms_bwd(xs, norm_mix_w, dh1, dx1, "rms_mix_bwd", after=pieces_i[2][-1:])

    def rows8(*vs):
        return jnp.concatenate([v.reshape(1, d) for v in vs] + [jnp.zeros((8 - len(vs), d), F32)], axis=0)

    join_d = _join_start("down", piece_sums(["w_down"], *_pieces_wait("down", pieces_d, grad_x)))
    join_gu = _join_start("gate_up", piece_sums(["w_gate", "w_up"], *_pieces_wait("gate_up", pieces_gu, join_d[2][0])))
    join_o = _join_start("out", piece_sums(["w_out"], *_pieces_wait("out", pieces_o, join_gu[2][0])))
    grads["w_down"], = _join_wait("down", join_d, join_o[2][0])
    update("w_down")
    grads["w_gate"], grads["w_up"] = _join_wait("gate_up", join_gu, new["w_down"][0])
    update("w_gate")
    update("w_up")
    grads["w_out"], = _join_wait("out", join_o, new["w_up"][0])
    update("w_out")
    others_done = [new[k][0] for k in ("w_down", "w_gate", "w_up", "w_out")]
    join_i = _join_start("in", piece_sums(["w_in"], *_pieces_wait("in", pieces_i, others_done)))
    ng, nd, nm, nv = _norm_weights_step(
        rows8(dw_mix, dw_ffn, dwf, jnp.broadcast_to(loss_row[:, :1], (1, d))),
        rows8(norm_mix_w, norm_ffn_w, norm_final_w),
        rows8(m_norm_mix_w, m_norm_ffn_w, m_norm_final_w), rows8(v_norm_mix_w, v_norm_ffn_w, v_norm_final_w),
        after=join_i[2][:1])
    grads["w_in"], = _join_wait("in", join_i, ng)
    update("w_in")

    loss = ng[3, 0]

    def pack(small, per_weight):
        lead = lambda a: a.reshape((1,) + a.shape)
        return (small[0:1], lead(per_weight["w_in"]), lead(per_weight["w_out"]), small[1:2],
                lead(per_weight["w_gate"]), lead(per_weight["w_up"]), lead(per_weight["w_down"]), small[2])

    return (loss, grad_x.reshape(1, s, d),
            *pack(ng, {k: new[k][3] for k in names}),
            *pack(nd, {k: new[k][0] for k in names}),
            *pack(nm, {k: new[k][1] for k in names}),
            *pack(nv, {k: new[k][2] for k in names}))
```

```python
import functools
import math

import jax
import jax.numpy as jnp
from jax import lax
from jax.experimental import pallas as pl
from jax.experimental.pallas import tpu as pltpu

F32 = jnp.float32
BF16 = jnp.bfloat16
I32 = jnp.int32
MESH = pl.DeviceIdType.MESH
ANY = pl.BlockSpec(memory_space=pl.ANY)

ATTN_HEADS = 8
ATTN_HEAD_DIM = 128
RET_HEADS = 4
RET_HEAD_DIM = 256
ATTN_WIDTH = ATTN_HEADS * ATTN_HEAD_DIM
RET_WIDTH = RET_HEADS * RET_HEAD_DIM
DILATED_PATTERNS = ((128, 1), (512, 4), (2048, 16))
NORM_EPS = 1e-6
ADAM_LR = 0.001
ADAM_B1 = 0.9
ADAM_B2 = 0.999
ADAM_EPS = 1e-08
ADAM_WD = 0.01
ADAM_STEP = 10

N_CHIPS = 4
N_DEV = 8
NEG_BIG = -1e30
SEQ_TILE = 512
ATTN_FWD_HEADS_PER_STEP = 2
ATTN_HEADS_PER_STEP = 1
VMEM_LIMIT_BYTES = 56 * 1024 * 1024


def _params(semantics=None, vmem=VMEM_LIMIT_BYTES):
    return pltpu.CompilerParams(dimension_semantics=semantics, vmem_limit_bytes=vmem)


def _row_tile(rows, row_bytes, limit=2 * 1024 * 1024, mult=16):
    best = None
    for t in range(mult, rows + 1, mult):
        if rows % t == 0 and t * row_bytes <= limit:
            best = t
    assert best is not None, (rows, row_bytes)
    return best


def _sigmoid(x):
    return 1.0 / (1.0 + jnp.exp(-x))


def _select_by_index(idx, values):
    out = jnp.float32(values[-1])
    for i in range(len(values) - 2, -1, -1):
        out = jnp.where(idx == i, jnp.float32(values[i]), out)
    return out


def _place():
    x, y, c = lax.axis_index("x"), lax.axis_index("y"), lax.axis_index("c")
    return x, y, c


def _cast_into_full(w, shard_arr, column_sharded, name, after=()):
    after = tuple(after)
    rows, cols = w.shape
    tr = _row_tile(rows, cols * 4)
    steps = rows // tr
    if column_sharded:
        out_shape, out_map = (rows, N_CHIPS * cols), (lambda i, s_ref: (i, s_ref[0]))
    else:
        out_shape, out_map = (N_CHIPS * rows, cols), (lambda i, s_ref: (s_ref[0] * steps + i, 0))

    def body(s_ref, w_ref, *rest):
        del s_ref
        rest[-1][...] = w_ref[...].astype(BF16)

    grid_spec = pltpu.PrefetchScalarGridSpec(
        num_scalar_prefetch=1, grid=(steps,),
        in_specs=[pl.BlockSpec((tr, cols), lambda i, s_ref: (i, 0))] + [ANY] * len(after),
        out_specs=pl.BlockSpec((tr, cols), out_map))
    return pl.pallas_call(
        body, name=name, grid_spec=grid_spec,
        out_shape=jax.ShapeDtypeStruct(out_shape, BF16),
        compiler_params=_params(("parallel",)),
    )(shard_arr, w, *after)


def _rms_fwd(x, w, name, after=()):
    rows, d = x.shape
    tr = 256
    after = tuple(after)

    def body(x_ref, w_ref, *rest):
        xv = x_ref[...]
        r = lax.rsqrt(jnp.mean(xv * xv, axis=-1, keepdims=True) + NORM_EPS)
        rest[-1][...] = (xv * r * w_ref[...]).astype(BF16)

    return pl.pallas_call(
        body, name=name, grid=(rows // tr,),
        in_specs=[pl.BlockSpec((tr, d), lambda i: (i, 0)), pl.BlockSpec((1, d), lambda i: (0, 0))]
        + [ANY] * len(after),
        out_specs=pl.BlockSpec((tr, d), lambda i: (i, 0)),
        out_shape=jax.ShapeDtypeStruct((rows, d), BF16),
        compiler_params=_params(("parallel",)),
    )(x, w, *after)


def _rms_bwd(x, w, dh, dres, name, after=()):
    rows, d = x.shape
    tr = 256
    after = tuple(after)

    def body(x_ref, w_ref, dh_ref, dres_ref, *rest):
        dx_ref, dxb_ref, dw_ref = rest[len(after):]
        xv = x_ref[...]
        r = lax.rsqrt(jnp.mean(xv * xv, axis=-1, keepdims=True) + NORM_EPS)
        xhat = xv * r
        dy = dh_ref[...]
        dxhat = dy * w_ref[...]
        dx = dres_ref[...] + r * (dxhat - xhat * jnp.mean(dxhat * xhat, axis=-1, keepdims=True))
        dx_ref[...] = dx
        dxb_ref[...] = dx.astype(BF16)
        part = jnp.sum(dy * xhat, axis=0, keepdims=True)

        @pl.when(pl.program_id(0) == 0)
        def _():
            dw_ref[...] = part

        @pl.when(pl.program_id(0) != 0)
        def _():
            dw_ref[...] += part

    row = pl.BlockSpec((tr, d), lambda i: (i, 0))
    vec = pl.BlockSpec((1, d), lambda i: (0, 0))
    return pl.pallas_call(
        body, name=name, grid=(rows // tr,),
        in_specs=[row, vec, row, row] + [ANY] * len(after),
        out_specs=[row, row, vec],
        out_shape=[jax.ShapeDtypeStruct((rows, d), F32), jax.ShapeDtypeStruct((rows, d), BF16),
                   jax.ShapeDtypeStruct((1, d), F32)],
        compiler_params=_params(("arbitrary",)),
    )(x, w, dh, dres, *after)


def _final_norm_loss(x2, w, target, name):
    rows, d = x2.shape
    tr = 256

    def body(x_ref, w_ref, t_ref, loss_ref, dx_ref, dxb_ref, dw_ref):
        xv = x_ref[...]
        wv = w_ref[...]
        r = lax.rsqrt(jnp.mean(xv * xv, axis=-1, keepdims=True) + NORM_EPS)
        xhat = xv * r
        err = xhat * wv - t_ref[...]
        part_loss = 0.5 * jnp.sum(jnp.mean(err * err, axis=-1, keepdims=True), axis=0, keepdims=True)
        dy = err * (1.0 / d)
        dxhat = dy * wv
        dx = r * (dxhat - xhat * jnp.mean(dxhat * xhat, axis=-1, keepdims=True))
        dx_ref[...] = dx
        dxb_ref[...] = dx.astype(BF16)
        part_dw = jnp.sum(dy * xhat, axis=0, keepdims=True)
        part_loss = jnp.broadcast_to(part_loss, (1, 128))

        @pl.when(pl.program_id(0) == 0)
        def _():
            dw_ref[...] = part_dw
            loss_ref[...] = part_loss

        @pl.when(pl.program_id(0) != 0)
        def _():
            dw_ref[...] += part_dw
            loss_ref[...] += part_loss

    row = pl.BlockSpec((tr, d), lambda i: (i, 0))
    vec = pl.BlockSpec((1, d), lambda i: (0, 0))
    return pl.pallas_call(
        body, name=name, grid=(rows // tr,),
        in_specs=[row, vec, row],
        out_specs=[pl.BlockSpec((1, 128), lambda i: (0, 0)), row, row, vec],
        out_shape=[jax.ShapeDtypeStruct((1, 128), F32), jax.ShapeDtypeStruct((rows, d), F32),
                   jax.ShapeDtypeStruct((rows, d), BF16), jax.ShapeDtypeStruct((1, d), F32)],
        compiler_params=_params(("arbitrary",)),
    )(x2, w, target)


def _adamw_math(w, g, m, v):
    m = ADAM_B1 * m + (1.0 - ADAM_B1) * g
    v = ADAM_B2 * v + (1.0 - ADAM_B2) * (g * g)
    m_hat = m / (1.0 - ADAM_B1 ** ADAM_STEP)
    v_hat = v / (1.0 - ADAM_B2 ** ADAM_STEP)
    delta = -ADAM_LR * (m_hat / (jnp.sqrt(v_hat) + ADAM_EPS) + ADAM_WD * w)
    return delta, m, v


def _adamw(w, g, m, v, name):
    rows, cols = w.shape
    tr = _row_tile(rows, cols * 4)

    def body(w_ref, g_ref, m_ref, v_ref, d_ref, mo_ref, vo_ref, go_ref):
        g = g_ref[...]
        delta, m_new, v_new = _adamw_math(w_ref[...], g, m_ref[...], v_ref[...])
        d_ref[...] = delta
        mo_ref[...] = m_new
        vo_ref[...] = v_new
        go_ref[...] = g

    blk = pl.BlockSpec((tr, cols), lambda i: (i, 0))
    shp = jax.ShapeDtypeStruct((rows, cols), F32)
    return pl.pallas_call(
        body, name=name, grid=(rows // tr,),
        in_specs=[blk] * 4, out_specs=[blk] * 4, out_shape=[shp] * 4,
        compiler_params=_params(("parallel",)),
    )(w, g, m, v)


_DOT_DIMS = {"nn": ((1,), (0,)), "nt": ((1,), (1,)), "tn": ((0,), (0,))}


def _matmul(name, mode, a_list, b_list, acc_of, m, n, k, tm, tn, tk, extras, out_dtypes, epilogue,
            a_koff=None, b_koff=None, after=(), a_specs=None, b_specs=None, a_single_buffer=False):
    after = tuple(after)
    assert m % tm == 0 and n % tn == 0 and k % tk == 0, (name, m, n, k, tm, tn, tk)
    nk = k // tk
    n_acc = max(acc_of) + 1
    n_pairs = len(a_list)
    a_koff = a_koff or [0] * n_pairs
    b_koff = b_koff or [0] * n_pairs
    dims = (_DOT_DIMS[mode], ((), ()))
    n_ext, n_out = len(extras), len(out_dtypes)

    def body(*refs):
        a_refs = refs[:n_pairs]
        b_refs = refs[n_pairs:2 * n_pairs]
        e_refs = refs[2 * n_pairs:2 * n_pairs + n_ext]
        first_out = 2 * n_pairs + n_ext + len(after)
        o_refs = refs[first_out:first_out + n_out]
        acc_refs = refs[first_out + n_out:]

        parts = [None] * n_acc
        for p in range(n_pairs):
            d = lax.dot_general(a_refs[p][...], b_refs[p][...], dims, preferred_element_type=F32)
            parts[acc_of[p]] = d if parts[acc_of[p]] is None else parts[acc_of[p]] + d

        def finish(accs):
            outs = epilogue(accs, [e[...] for e in e_refs])
            for o_ref, o in zip(o_refs, outs):
                o_ref[...] = o.astype(o_ref.dtype)

        if nk == 1:
            finish(parts)
        else:
            kk = pl.program_id(2)

            @pl.when(kk == 0)
            def _():
                for acc_ref, part in zip(acc_refs, parts):
                    acc_ref[...] = part

            @pl.when(kk != 0)
            def _():
                for acc_ref, part in zip(acc_refs, parts):
                    acc_ref[...] += part

            @pl.when(kk == nk - 1)
            def _():
                finish([acc_ref[...] for acc_ref in acc_refs])

    def a_spec(off):
        mode_a = pl.Buffered(1) if a_single_buffer else None
        if mode == "tn":
            return pl.BlockSpec((tk, tm), lambda i, j, kk: (kk + off, i), pipeline_mode=mode_a)
        return pl.BlockSpec((tm, tk), lambda i, j, kk: (i, kk + off), pipeline_mode=mode_a)

    def b_spec(off):
        if mode == "nt":
            return pl.BlockSpec((tn, tk), lambda i, j, kk: (j, kk + off))
        return pl.BlockSpec((tk, tn), lambda i, j, kk: (kk + off, j))

    tile = pl.BlockSpec((tm, tn), lambda i, j, kk: (i, j))
    scratch = [pltpu.VMEM((tm, tn), F32) for _ in range(n_acc)] if nk > 1 else []
    return pl.pallas_call(
        body, name=name, grid=(m // tm, n // tn, nk),
        in_specs=(a_specs or [a_spec(o) for o in a_koff]) + (b_specs or [b_spec(o) for o in b_koff])
        + [tile] * n_ext + [ANY] * len(after),
        out_specs=[tile] * n_out,
        out_shape=[jax.ShapeDtypeStruct((m, n), dt) for dt in out_dtypes],
        scratch_shapes=scratch,
        compiler_params=_params(("parallel", "parallel", "arbitrary")),
    )(*a_list, *b_list, *extras, *after)


def _epi_plain(accs, extras):
    return (accs[0],)


def _epi_residual(accs, extras):
    return (accs[0] + extras[0],)


def _epi_two(accs, extras):
    return accs[0], accs[1]


def _epi_swiglu(accs, extras):
    g, u = accs
    return g, u, g * _sigmoid(g) * u


def _epi_swiglu_bwd(accs, extras):
    da = accs[0]
    g, u = (e.astype(F32) for e in extras)
    sg = _sigmoid(g)
    dg = da * u * sg * (1.0 + g * (1.0 - sg))
    du = da * g * sg
    return dg, du


_NT_DIMS = (((1,), (1,)), ((), ()))
_TN_DIMS = (((0,), (0,)), ((), ()))


def _tile_delta(tq, tk):
    return lax.broadcasted_iota(I32, (tq, tk), 0) - lax.broadcasted_iota(I32, (tq, tk), 1)


def _attn_log_count(delta):
    count = jnp.zeros(delta.shape, I32)
    for window, dilation in DILATED_PATTERNS:
        hit = ((delta & (dilation - 1)) == 0) & (delta <= window)
        count = count + jnp.where(hit, 1, 0)
    valid = (delta >= 0) & (count > 0)
    logm = jnp.where(count == 3, math.log(3.0), jnp.where(count == 2, math.log(2.0), 0.0))
    return jnp.where(valid, logm, NEG_BIG)


def _fill_attn_log_count(tab_ref):
    nb, t, _ = tab_ref.shape
    base = _tile_delta(t, t)
    for b in range(nb):
        tab_ref[b] = _attn_log_count(base + b * t)


def _fill_attn_bias(tab_ref, log_count_ref, slope):
    nb, t, _ = tab_ref.shape
    dist = _tile_delta(t, t).astype(F32)
    for b in range(nb):
        tab_ref[b] = log_count_ref[b] - slope * (dist + float(b * t))


def _fill_ret_decay(tab_ref, log_gamma):
    nb, t, _ = tab_ref.shape
    base = _tile_delta(t, t)
    for b in range(nb):
        tab_ref[b] = _ret_decay(base + b * t, log_gamma)


def _alibi_slopes():
    return [2.0 ** (-8.0 * (h + 1) / ATTN_HEADS) for h in range(ATTN_HEADS)]


def _attn_fwd(proj, after=()):
    s = proj.shape[0]
    t = SEQ_TILE
    hd = ATTN_HEAD_DIM
    hp = ATTN_FWD_HEADS_PER_STEP
    ng = ATTN_HEADS // hp
    w = hp * hd
    scale = 1.0 / math.sqrt(hd)
    slopes = _alibi_slopes()

    def body(q_ref, k_ref, v_ref, *rest):
        mix_ref, o_ref, lse_ref, kb, vb, bias_tab, log_count_tab = rest[len(after):]
        g = pl.program_id(0)
        i = pl.program_id(1)

        @pl.when((g == 0) & (i == 0))
        def _():
            _fill_attn_log_count(log_count_tab)

        @pl.when(i == 0)
        def _():
            kb[...] = k_ref[...].astype(BF16)
            vb[...] = v_ref[...].astype(BF16)
            for u in range(hp):
                _fill_attn_bias(bias_tab.at[u], log_count_tab, _select_by_index(g * hp + u, slopes))

        qs = [q_ref[:, u * hd:(u + 1) * hd].astype(BF16) for u in range(hp)]

        def step(j, carry):
            rows = pl.ds(pl.multiple_of(j * t, t), t)
            out = []
            for u in range(hp):
                m_i, l_i, acc = carry[u]
                lanes = slice(u * hd, (u + 1) * hd)
                sc = lax.dot_general(qs[u], kb[rows, lanes], _NT_DIMS, preferred_element_type=F32) * scale
                sc = sc + bias_tab[u, i - j]
                m_new = jnp.maximum(m_i, jnp.max(sc, axis=-1, keepdims=True))
                p = jnp.exp(sc - m_new)
                alpha = jnp.exp(m_i - m_new)
                l_new = alpha * l_i + jnp.sum(p, axis=-1, keepdims=True)
                acc = alpha * acc + jnp.dot(p.astype(BF16), vb[rows, lanes], preferred_element_type=F32)
                out.append((m_new, l_new, acc))
            return tuple(out)

        init = (jnp.full((t, 1), NEG_BIG, F32), jnp.zeros((t, 1), F32), jnp.zeros((t, hd), F32))
        final = lax.fori_loop(0, i + 1, step, (init,) * hp)
        for u in range(hp):
            m_i, l_i, acc = final[u]
            lanes = slice(u * hd, (u + 1) * hd)
            out = acc / l_i
            o_ref[:, lanes] = out
            mix_ref[:, lanes] = out.astype(BF16)
            lse_ref[:, lanes] = jnp.broadcast_to(m_i + jnp.log(l_i), (t, hd))

    return pl.pallas_call(
        body, name="attn_fwd", grid=(ng, s // t),
        in_specs=[pl.BlockSpec((t, w), lambda g, i: (i, g)),
                  pl.BlockSpec((s, w), lambda g, i: (0, ng + g)),
                  pl.BlockSpec((s, w), lambda g, i: (0, 2 * ng + g))] + [ANY] * len(after),
        out_specs=[pl.BlockSpec((None, t, w), lambda g, i: (0, i, g))] + [pl.BlockSpec((t, w), lambda g, i: (i, g))] * 2,
        out_shape=[jax.ShapeDtypeStruct((2, s, ATTN_WIDTH), BF16),
                   jax.ShapeDtypeStruct((s, ATTN_WIDTH), F32),
                   jax.ShapeDtypeStruct((s, ATTN_WIDTH), F32)],
        scratch_shapes=[pltpu.VMEM((s, w), BF16), pltpu.VMEM((s, w), BF16), pltpu.VMEM((hp, s // t, t, t), F32),
                        pltpu.VMEM((s // t, t, t), F32)],
        compiler_params=_params(("arbitrary", "arbitrary")),
    )(proj, proj, proj, *after)


def _attn_bwd(proj, attn_out, lse, dmixed, after=()):
    after = tuple(after)
    s = proj.shape[0]
    t = SEQ_TILE
    nt = s // t
    hd = ATTN_HEAD_DIM
    hp = ATTN_HEADS_PER_STEP
    ng = ATTN_HEADS // hp
    w = hp * hd
    scale = 1.0 / math.sqrt(hd)
    slopes = _alibi_slopes()

    def body(q_ref, k_ref, v_ref, o_ref, lse_ref, do_ref, *rest):
        dsec_ref, qb, kb, vb, dob, dsum, dq_acc, bias_tab, log_count_tab = rest[len(after):]
        g = pl.program_id(0)

        @pl.when(g == 0)
        def _():
            _fill_attn_log_count(log_count_tab)

        qb[...] = q_ref[...].astype(BF16)
        kb[...] = k_ref[...].astype(BF16)
        vb[...] = v_ref[...].astype(BF16)
        dob[...] = do_ref[...].astype(BF16)
        for u in range(hp):
            lanes = slice(u * hd, (u + 1) * hd)
            _fill_attn_bias(bias_tab.at[u], log_count_tab, _select_by_index(g * hp + u, slopes))
            rowsum = jnp.sum(do_ref[:, lanes] * o_ref[:, lanes], axis=-1, keepdims=True)
            dsum[:, lanes] = jnp.broadcast_to(rowsum, (s, hd))
        dq_acc[...] = jnp.zeros((s, w), F32)

        def over_keys(j, _):
            krows = pl.ds(pl.multiple_of(j * t, t), t)

            def over_queries(i, carry):
                qrows = pl.ds(pl.multiple_of(i * t, t), t)
                out = []
                for u in range(hp):
                    dk, dv = carry[u]
                    lanes = slice(u * hd, (u + 1) * hd)
                    qi, doi = qb[qrows, lanes], dob[qrows, lanes]
                    kj, vj = kb[krows, lanes], vb[krows, lanes]
                    lse_i = lse_ref[qrows, lanes][:, :1]
                    dsum_i = dsum[qrows, lanes][:, :1]
                    sc = lax.dot_general(qi, kj, _NT_DIMS, preferred_element_type=F32) * scale
                    p = jnp.exp(sc + bias_tab[u, i - j] - lse_i)
                    dp = lax.dot_general(doi, vj, _NT_DIMS, preferred_element_type=F32)
                    ds = (p * (dp - dsum_i)).astype(BF16)
                    dv = dv + lax.dot_general(p.astype(BF16), doi, _TN_DIMS, preferred_element_type=F32)
                    dk = dk + lax.dot_general(ds, qi, _TN_DIMS, preferred_element_type=F32)
                    dq_acc[qrows, lanes] += jnp.dot(ds, kj, preferred_element_type=F32)
                    out.append((dk, dv))
                return tuple(out)

            zero = jnp.zeros((t, hd), F32)
            final = lax.fori_loop(j, nt, over_queries, ((zero, zero),) * hp)
            for u in range(hp):
                lanes = slice(u * hd, (u + 1) * hd)
                dsec_ref[1, krows, lanes] = (final[u][0] * scale).astype(BF16)
                dsec_ref[2, krows, lanes] = final[u][1].astype(BF16)
            return 0

        lax.fori_loop(0, nt, over_keys, 0)
        dsec_ref[0] = (dq_acc[...] * scale).astype(BF16)

    def col(off):
        return pl.BlockSpec((s, w), lambda g: (0, off + g))

    return pl.pallas_call(
        body, name="attn_bwd", grid=(ng,),
        in_specs=[col(0), col(ng), col(2 * ng), col(0), col(0), col(0)] + [ANY] * len(after),
        out_specs=pl.BlockSpec((4, s, w), lambda g: (0, 0, g)),
        out_shape=jax.ShapeDtypeStruct((8, s, ATTN_WIDTH), BF16),
        scratch_shapes=[pltpu.VMEM((s, w), BF16)] * 4 + [pltpu.VMEM((s, w), F32)] * 2
        + [pltpu.VMEM((hp, nt, t, t), F32), pltpu.VMEM((nt, t, t), F32)],
        compiler_params=_params(("arbitrary",)),
    )(proj, proj, proj, attn_out, lse, dmixed, *after)


def _ret_log_gammas():
    return [math.log(1.0 - 2.0 ** (-5.0 - h)) for h in range(RET_HEADS)]


def _ret_decay(delta, log_gamma):
    dec = jnp.exp(delta.astype(F32) * log_gamma) * (1.0 / math.sqrt(RET_HEAD_DIM))
    return jnp.where(delta >= 0, dec, 0.0)


def _ret_fwd(proj, mixed, after=()):
    after = tuple(after)
    s = proj.shape[0]
    t = SEQ_TILE
    hd = RET_HEAD_DIM
    nh = RET_HEADS
    log_gammas = _ret_log_gammas()
    c0 = 3 * ATTN_WIDTH // hd

    def body(q_ref, k_ref, v_ref, g_ref, *rest):
        mix_ref, raw_ref, kb, vb, decay_tab = rest[1 + len(after):]
        h = pl.program_id(0)
        i = pl.program_id(1)

        @pl.when(i == 0)
        def _():
            kb[...] = k_ref[...].astype(BF16)
            vb[...] = v_ref[...].astype(BF16)
            _fill_ret_decay(decay_tab, _select_by_index(h, log_gammas))

        q = q_ref[...].astype(BF16)

        def step(j, acc):
            rows = pl.ds(pl.multiple_of(j * t, t), t)
            sc = lax.dot_general(q, kb[rows, :], _NT_DIMS, preferred_element_type=F32) * decay_tab[i - j]
            return acc + jnp.dot(sc.astype(BF16), vb[rows, :], preferred_element_type=F32)

        ret = lax.fori_loop(0, i + 1, step, jnp.zeros((t, hd), F32))
        raw_ref[...] = ret
        r = lax.rsqrt(jnp.mean(ret * ret, axis=-1, keepdims=True) + NORM_EPS)
        g = g_ref[...]
        mix_ref[...] = (g * _sigmoid(g) * (ret * r)).astype(BF16)

    return pl.pallas_call(
        body, name="ret_fwd", grid=(nh, s // t),
        in_specs=[pl.BlockSpec((t, hd), lambda h, i: (i, c0 + h)),
                  pl.BlockSpec((s, hd), lambda h, i: (0, c0 + nh + h)),
                  pl.BlockSpec((s, hd), lambda h, i: (0, c0 + 2 * nh + h)),
                  pl.BlockSpec((t, hd), lambda h, i: (i, c0 + 3 * nh + h))] + [ANY] * (1 + len(after)),
        out_specs=[pl.BlockSpec((None, t, hd), lambda h, i: (1, i, h)), pl.BlockSpec((t, hd), lambda h, i: (i, h))],
        out_shape=[jax.ShapeDtypeStruct(mixed.shape, BF16), jax.ShapeDtypeStruct((s, RET_WIDTH), F32)],
        input_output_aliases={4: 0},
        scratch_shapes=[pltpu.VMEM((s, hd), BF16), pltpu.VMEM((s, hd), BF16), pltpu.VMEM((s // t, t, t), F32)],
        compiler_params=_params(("arbitrary", "arbitrary")),
    )(proj, proj, proj, proj, mixed, *after)


def _ret_bwd(proj, ret_raw, dmixed, dsec, after=()):
    after = tuple(after)
    s = proj.shape[0]
    t = SEQ_TILE
    nt = s // t
    hd = RET_HEAD_DIM
    nh = RET_HEADS
    log_gammas = _ret_log_gammas()
    c0 = 3 * ATTN_WIDTH // hd
    mixed_blocks = ATTN_WIDTH // hd

    def body(q_ref, k_ref, v_ref, g_ref, raw_ref, dmix_ref, *rest):
        dsec_ref, qb, kb, vb, dretb, dq_acc, decay_tab = rest[1 + len(after):]
        h = pl.program_id(0)
        _fill_ret_decay(decay_tab, _select_by_index(h, log_gammas))
        qb[...] = q_ref[...].astype(BF16)
        kb[...] = k_ref[...].astype(BF16)
        vb[...] = v_ref[...].astype(BF16)
        ret = raw_ref[...]
        r = lax.rsqrt(jnp.mean(ret * ret, axis=-1, keepdims=True) + NORM_EPS)
        normed = ret * r
        g = g_ref[...]
        sg = _sigmoid(g)
        dout = dmix_ref[...]
        dsec_ref[3] = (dout * normed * sg * (1.0 + g * (1.0 - sg))).astype(BF16)
        dn = dout * g * sg
        dret = r * (dn - normed * jnp.mean(dn * normed, axis=-1, keepdims=True))
        dretb[...] = dret.astype(BF16)
        dq_acc[...] = jnp.zeros((s, hd), F32)

        def over_keys(j, _):
            krows = pl.ds(pl.multiple_of(j * t, t), t)
            kj = kb[krows, :]
            vj = vb[krows, :]

            def over_queries(i, carry):
                dk, dv = carry
                qrows = pl.ds(pl.multiple_of(i * t, t), t)
                qi = qb[qrows, :]
                doi = dretb[qrows, :]
                dec = decay_tab[i - j]
                a = (lax.dot_general(qi, kj, _NT_DIMS, preferred_element_type=F32) * dec).astype(BF16)
                da = (lax.dot_general(doi, vj, _NT_DIMS, preferred_element_type=F32) * dec).astype(BF16)
                dv = dv + lax.dot_general(a, doi, _TN_DIMS, preferred_element_type=F32)
                dk = dk + lax.dot_general(da, qi, _TN_DIMS, preferred_element_type=F32)
                dq_acc[qrows, :] += jnp.dot(da, kj, preferred_element_type=F32)
                return dk, dv

            zero = jnp.zeros((t, hd), F32)
            dk, dv = lax.fori_loop(j, nt, over_queries, (zero, zero))
            dsec_ref[1, krows, :] = dk.astype(BF16)
            dsec_ref[2, krows, :] = dv.astype(BF16)
            return 0

        lax.fori_loop(0, nt, over_keys, 0)
        dsec_ref[0] = dq_acc[...].astype(BF16)

    def col(off):
        return pl.BlockSpec((s, hd), lambda h: (0, off + h))

    return pl.pallas_call(
        body, name="ret_bwd", grid=(nh,),
        in_specs=[col(c0), col(c0 + nh), col(c0 + 2 * nh), col(c0 + 3 * nh), col(0), col(mixed_blocks)]
        + [ANY] * (1 + len(after)),
        out_specs=pl.BlockSpec((4, s, hd), lambda h: (1, 0, h)),
        out_shape=jax.ShapeDtypeStruct(dsec.shape, BF16),
        input_output_aliases={6: 0},
        scratch_shapes=[pltpu.VMEM((s, hd), BF16)] * 4 + [pltpu.VMEM((s, hd), F32)]
        + [pltpu.VMEM((nt, t, t), F32)],
        compiler_params=_params(("arbitrary",)),
    )(proj, proj, proj, proj, ret_raw, dmixed, dsec, *after)


_FLIPS = (2, 1, 3)


def _other_chips(x, y):
    return [(1 - x, y), (x, 1 - y), (1 - x, 1 - y)]


_HBM = pl.BlockSpec(memory_space=pltpu.HBM)
_SEM = pl.BlockSpec(memory_space=pltpu.SEMAPHORE)
_EFFECT = pltpu.SideEffectType.DATAFLOW_SIDE_EFFECTING


def _in_hbm(a):
    return pltpu.with_memory_space_constraint(a, pltpu.HBM)


def _weight_view(w, column_sharded):
    if column_sharded:
        return w.reshape(2, w.shape[0] // 2, w.shape[1])
    return w.reshape(N_CHIPS, 2, w.shape[0] // (2 * N_CHIPS), w.shape[1])


def _weight_unview(v):
    if v.ndim == 3:
        return v.reshape(2 * v.shape[1], v.shape[2])
    return v.reshape(N_CHIPS * 2 * v.shape[2], v.shape[3])


def _weight_region(buf, shard, half):
    if len(buf.shape) == 3:
        cols = buf.shape[2] // N_CHIPS
        return buf.at[half, :, pl.ds(shard * cols, cols)]
    return buf.at[shard, half]


def _remote(where, send_sem, recv_sem, to):
    return pltpu.make_async_remote_copy(src_ref=where, dst_ref=where, send_sem=send_sem, recv_sem=recv_sem,
                                        device_id=to, device_id_type=MESH)


def _for_my_shard(fn):
    x, y, _ = _place()
    for ss in range(N_CHIPS):
        pl.when(2 * x + y == ss)(functools.partial(fn, ss))


def _gather_forward(views, which, send_sems, recv_sems, after, name, base=0, sibling_id=None):
    n_w = len(views)
    which = [base // 3 + w for w in which] if base % 3 == 0 else None
    assert which is not None, "base must be a multiple of 3"

    def body(*refs):
        if sibling_id is not None:
            _sibling_handshake()
        send_in, recv_in = refs[n_w:n_w + 2]
        fwd_send, fwd_recv = refs[n_w + 3:n_w + 5]
        bufs = refs[n_w + 5:]
        x, y, c = _place()
        sibling = (x, y, 1 - c)

        def forward(ss):
            for i, w in enumerate(which):
                for j in range(3):
                    landed = _weight_region(bufs[i], ss ^ _FLIPS[j], c)
                    _remote(landed, send_in.at[3 * w + j], recv_in.at[3 * w + j], sibling).wait_recv()
                    _remote(landed, fwd_send.at[3 * i + j], fwd_recv.at[3 * i + j], sibling).start()

        _for_my_shard(forward)
        for i, w in enumerate(which):
            for j in range(3):
                _remote(_weight_region(bufs[i], 0, 0), send_in.at[3 * w + j], recv_in.at[3 * w + j],
                        sibling).wait_send()

    return pl.pallas_call(
        body, name=name,
        in_specs=[_HBM] * n_w + [_SEM, _SEM, ANY], out_specs=[_SEM, _SEM] + [_HBM] * n_w,
        out_shape=[pltpu.SemaphoreType.DMA((3 * n_w,)), pltpu.SemaphoreType.DMA((3 * n_w,))]
        + [pltpu.HBM(v.shape, BF16) for v in views],
        input_output_aliases={w: 2 + w for w in range(n_w)},
        compiler_params=pltpu.CompilerParams(has_side_effects=_EFFECT, collective_id=sibling_id),
    )(*views, send_sems, recv_sems, after)


def _gather_end(views, fwd_send, fwd_recv, after, name):
    n_w = len(views)

    def body(*refs):
        fwd_send_ref, fwd_recv_ref = refs[n_w:n_w + 2]
        bufs = refs[n_w + 3:]
        x, y, c = _place()
        for i in range(n_w):
            for j in range(3):
                cp = _remote(_weight_region(bufs[i], 0, 0), fwd_send_ref.at[3 * i + j], fwd_recv_ref.at[3 * i + j],
                             (x, y, 1 - c))
                cp.wait_recv()
                cp.wait_send()

    outs = pl.pallas_call(
        body, name=name,
        in_specs=[_HBM] * n_w + [_SEM, _SEM, ANY], out_specs=[_HBM] * n_w,
        out_shape=[pltpu.HBM(v.shape, BF16) for v in views],
        input_output_aliases={w: w for w in range(n_w)},
        compiler_params=pltpu.CompilerParams(has_side_effects=_EFFECT),
    )(*views, fwd_send, fwd_recv, after)
    return [_weight_unview(o) for o in outs]


def _comm_call(name, bufs, sem_pairs, after, n_new, fn, sibling_id=None):
    n, n_sem, after = len(bufs), 2 * len(sem_pairs), tuple(after)
    n_out_sem = 2 if n_new else 0

    def body(*refs):
        if sibling_id is not None:
            _sibling_handshake()
        sems = refs[n:n + n_sem]
        outs = refs[n + n_sem + len(after):]
        new = outs[:n_out_sem] if n_new else (None, None)
        fn(outs[n_out_sem:], [(sems[2 * i], sems[2 * i + 1]) for i in range(len(sem_pairs))], *new)

    res = pl.pallas_call(
        body, name=name,
        in_specs=[_HBM] * n + [_SEM] * n_sem + [ANY] * len(after),
        out_specs=[_SEM] * n_out_sem + [_HBM] * n,
        out_shape=[pltpu.SemaphoreType.DMA((n_new,))] * n_out_sem + [pltpu.HBM(b.shape, b.dtype) for b in bufs],
        input_output_aliases={i: n_out_sem + i for i in range(n)},
        compiler_params=pltpu.CompilerParams(has_side_effects=_EFFECT, collective_id=sibling_id),
    )(*bufs, *[s for pair in sem_pairs for s in pair], *after)
    return list(res[:n_out_sem]), list(res[n_out_sem:])


def _quarter(piece, q):
    rows = piece.shape[0] // 2
    return piece.at[pl.ds(q * rows, rows)]


def _gather_in_start(view, name):
    def fn(bufs, _, send, recv):
        x, y, c = _place()

        def go(ss):
            for j, chip in enumerate(_other_chips(x, y)[:2]):
                _remote(_weight_region(bufs[0], ss, c), send.at[j], recv.at[j], (*chip, c)).start()

        _for_my_shard(go)

    sems, (view,) = _comm_call(name, [_in_hbm(view)], [], (), 2, fn)
    return sems, view


def _gather_out_gate_start(v_out, v_gate, after, name):
    def fn(bufs, _, send, recv):
        x, y, c = _place()
        chips = _other_chips(x, y)

        def go(ss):
            for j in range(3):
                _remote(_weight_region(bufs[0], ss, c), send.at[j], recv.at[j], (*chips[j], c)).start()
            for j in range(2):
                _remote(_weight_region(bufs[1], ss, c), send.at[3 + j], recv.at[3 + j], (*chips[j], c)).start()

        _for_my_shard(go)

    sems, views = _comm_call(name, [_in_hbm(v_out), _in_hbm(v_gate)], [], after, 5, fn)
    return sems, views


def _gather_relay(view, started, base, after, name, then=None, then_peers=0, then_first=False):
    n_new = 6 + then_peers if then_peers else 4

    def fn(bufs, pairs, send, recv):
        (send_in, recv_in), = pairs
        x, y, c = _place()
        chips = _other_chips(x, y)
        sibling = (x, y, 1 - c)

        def go(ss):
            def start_then():
                for j in range(then_peers):
                    _remote(_weight_region(bufs[1], ss, c), send.at[6 + j], recv.at[6 + j], (*chips[j], c)).start()

            if then_first:
                start_then()
            landed = [_weight_region(bufs[0], ss ^ _FLIPS[j], c) for j in range(2)]
            for j in range(2):
                _remote(landed[j], send_in.at[base + j], recv_in.at[base + j], sibling).wait_recv()
            for j in range(2):
                _remote(_quarter(landed[j], j), send.at[j], recv.at[j], (*chips[1 - j], c)).start()
            for j in range(2):
                _remote(landed[j], send.at[2 + j], recv.at[2 + j], sibling).start()
            if not then_first:
                start_then()

        _for_my_shard(go)
        for j in range(2):
            _remote(_weight_region(bufs[0], 0, 0), send_in.at[base + j], recv_in.at[base + j], sibling).wait_send()

    views = [view] if then is None else [view, _in_hbm(then)]
    sems, views = _comm_call(name, views, [started], after, n_new, fn)
    return sems, views


def _gather_in_neighbours_end(view, relayed, after, name):
    def fn(bufs, pairs, *_):
        (send, recv), = pairs
        x, y, c = _place()
        for j in range(2):
            cp = _remote(_weight_region(bufs[0], 0, 0), send.at[2 + j], recv.at[2 + j], (x, y, 1 - c))
            cp.wait_recv()
            cp.wait_send()

    _, (view,) = _comm_call(name, [view], [relayed], after, 0, fn)
    return view


def _gather_in_diagonal(view, relayed, after, name, sibling_id):
    def fn(bufs, pairs, send, recv):
        (send_in, recv_in), = pairs
        x, y, c = _place()
        sibling = (x, y, 1 - c)
        any_quarter = _quarter(_weight_region(bufs[0], 0, 0), 0)
        for j in range(2):
            cp = _remote(any_quarter, send_in.at[j], recv_in.at[j], sibling)
            cp.wait_recv()
            cp.wait_send()

        def go(ss):
            _remote(_weight_region(bufs[0], ss ^ _FLIPS[2], c), send.at[0], recv.at[0], sibling).start()

        _for_my_shard(go)

    sems, (view,) = _comm_call(name, [view], [relayed], after, 1, fn, sibling_id=sibling_id)
    return sems, view


def _gather_in_diagonal_end(view, forwarded, after, name):
    def fn(bufs, pairs, *_):
        (send, recv), = pairs
        x, y, c = _place()
        cp = _remote(_weight_region(bufs[0], 0, 0), send.at[0], recv.at[0], (x, y, 1 - c))
        cp.wait_recv()
        cp.wait_send()

    _, (view,) = _comm_call(name, [view], [forwarded], after, 0, fn)
    return view


def _in_proj_shard(h1, wi, proj, shard_arr, name):
    s, d = h1.shape
    n = wi.shape[1]
    tn = 256
    blocks = n // (N_CHIPS * tn)
    given = [] if proj is None else [proj]

    def body(shard_ref, h_ref, w_ref, *rest):
        del shard_ref
        rest[-1][...] = jnp.dot(h_ref[...], w_ref[...], preferred_element_type=F32)

    grid_spec = pltpu.PrefetchScalarGridSpec(
        num_scalar_prefetch=1, grid=(blocks,),
        in_specs=[pl.BlockSpec((s, d), lambda j, shard_ref: (0, 0)),
                  pl.BlockSpec((d, tn), lambda j, shard_ref: (0, shard_ref[0] * blocks + j))] + [ANY] * len(given),
        out_specs=pl.BlockSpec((s, tn), lambda j, shard_ref: (0, shard_ref[0] * blocks + j)))
    return pl.pallas_call(
        body, name=name, grid_spec=grid_spec,
        out_shape=jax.ShapeDtypeStruct((s, n), F32),
        input_output_aliases={3: 0} if given else {},
        compiler_params=_params(("arbitrary",)),
    )(shard_arr, h1, wi, *given)


def _sibling_handshake():
    x, y, c = _place()
    barrier = pltpu.get_barrier_semaphore()
    pl.semaphore_signal(barrier, inc=1, device_id=(x, y, 1 - c), device_id_type=MESH)
    pl.semaphore_wait(barrier, 1)


def _chips_handshake():
    x, y, c = _place()
    barrier = pltpu.get_barrier_semaphore()
    for cx, cy in _other_chips(x, y):
        pl.semaphore_signal(barrier, inc=1, device_id=(cx, cy, c), device_id_type=MESH)
    pl.semaphore_wait(barrier, N_CHIPS - 1)


def _split_start(name, bufs, n_sems, copies, sibling_id=None, chips_id=None):
    n = len(bufs)
    assert sibling_id is None or chips_id is None

    def body(*refs):
        if sibling_id is not None:
            _sibling_handshake()
        if chips_id is not None:
            _chips_handshake()
        send_sems, recv_sems = refs[n:n + 2]
        for cp in copies(refs[n + 2:], send_sems, recv_sems):
            cp.start()

    outs = pl.pallas_call(
        body, name=name,
        in_specs=[_HBM] * n, out_specs=[_SEM, _SEM] + [_HBM] * n,
        out_shape=[pltpu.SemaphoreType.DMA((n_sems,)), pltpu.SemaphoreType.DMA((n_sems,))]
        + [pltpu.HBM(b.shape, b.dtype) for b in bufs],
        input_output_aliases={i: 2 + i for i in range(n)},
        compiler_params=pltpu.CompilerParams(has_side_effects=_EFFECT,
                                             collective_id=sibling_id if chips_id is None else chips_id),
    )(*[_in_hbm(b) for b in bufs])
    return outs[0], outs[1], list(outs[2:])


def _split_wait(name, bufs, send_sems, recv_sems, copies, after):
    n = len(bufs)
    after = tuple(after) if isinstance(after, (list, tuple)) else (after,)

    def body(*refs):
        send_ref, recv_ref = refs[n:n + 2]
        for cp in copies(refs[n + 2 + len(after):], send_ref, recv_ref):
            cp.wait()

    return list(pl.pallas_call(
        body, name=name,
        in_specs=[_HBM] * n + [_SEM, _SEM] + [ANY] * len(after), out_specs=[_HBM] * n,
        out_shape=[pltpu.HBM(b.shape, b.dtype) for b in bufs],
        input_output_aliases={i: i for i in range(n)},
        compiler_params=pltpu.CompilerParams(has_side_effects=_EFFECT),
    )(*bufs, send_sems, recv_sems, *after))


def _halves_copies(n_w):
    def copies(bufs, send_sems, recv_sems):
        x, y, c = _place()
        out = []
        for w in range(n_w):
            view, land = bufs[w], bufs[n_w + w]
            src = view.at[1 - c] if len(view.shape) == 3 else view.at[:, 1 - c]
            out.append(pltpu.make_async_remote_copy(
                src_ref=src, dst_ref=land, send_sem=send_sems.at[w], recv_sem=recv_sems.at[w],
                device_id=(x, y, 1 - c), device_id_type=MESH))
        return out
    return copies


def _pieces_copies(n_w):
    def copies(bufs, send_sems, recv_sems):
        x, y, c = _place()
        out = []
        for w in range(n_w):
            for j, (cx, cy) in enumerate(_other_chips(x, y)):
                out.append(pltpu.make_async_remote_copy(
                    src_ref=bufs[w].at[2 * cx + cy], dst_ref=bufs[n_w + w].at[j],
                    send_sem=send_sems.at[3 * w + j], recv_sem=recv_sems.at[3 * w + j],
                    device_id=(cx, cy, c), device_id_type=MESH))
        return out
    return copies


def _join_copies(n_w):
    def copies(bufs, send_sems, recv_sems):
        x, y, c = _place()
        return [pltpu.make_async_remote_copy(
            src_ref=bufs[w].at[c], dst_ref=bufs[w].at[c], send_sem=send_sems.at[w], recv_sem=recv_sems.at[w],
            device_id=(x, y, 1 - c), device_id_type=MESH) for w in range(n_w)]
    return copies


def _halves_landing(view):
    shape = view.shape[1:] if view.ndim == 3 else (N_CHIPS,) + view.shape[2:]
    return lax.empty(shape, BF16)


_SIBLING_IDS = {"halves_down": 1, "halves_gate_up": 2, "halves_out": 3, "halves_in": 4,
                "join_down": 5, "join_gate_up": 6, "join_out": 7, "join_in": 8,
                "diagonal_in": 9, "diagonal_gate": 10, "diagonal_up": 11, "forward_out": 12, "forward_down": 13,
                "pieces_down": 14, "pieces_gate_up": 15, "pieces_out": 16, "pieces_in": 17}


def _halves_start(tag, grads, column_sharded):
    views = [_weight_view(g, cs) for g, cs in zip(grads, column_sharded)]
    n = len(views)
    return _split_start("halves_start_" + tag, views + [_halves_landing(v) for v in views], n, _halves_copies(n),
                        sibling_id=_SIBLING_IDS["halves_" + tag])


def _halves_wait(tag, state, after):
    send_sems, recv_sems, bufs = state
    n = len(bufs) // 2
    bufs = _split_wait("halves_wait_" + tag, bufs, send_sems, recv_sems, _halves_copies(n), after)
    return bufs[:n], bufs[n:]


def _pieces_start(tag, pieces):
    n = len(pieces)
    landing = [lax.empty((3,) + p.shape[1:], BF16) for p in pieces]
    return _split_start("pieces_start_" + tag, list(pieces) + landing, 3 * n, _pieces_copies(n),
                        chips_id=_SIBLING_IDS["pieces_" + tag])


def _pieces_wait(tag, state, after):
    send_sems, recv_sems, bufs = state
    n = len(bufs) // 2
    bufs = _split_wait("pieces_wait_" + tag, bufs, send_sems, recv_sems, _pieces_copies(n), after)
    return bufs[:n], bufs[n:]


def _join_start(tag, shards):
    n = len(shards)
    return _split_start("join_start_" + tag, list(shards), n, _join_copies(n), sibling_id=_SIBLING_IDS["join_" + tag])


def _join_wait(tag, state, after):
    send_sems, recv_sems, bufs = state
    bufs = _split_wait("join_wait_" + tag, bufs, send_sems, recv_sems, _join_copies(len(bufs)), after)
    return [b.reshape(2 * b.shape[1], b.shape[2]) for b in bufs]


def _chip_sum_col(g3, sib, c_arr, name):
    _, hk, n = g3.shape
    cols = n // N_CHIPS
    tr = _row_tile(hk, cols * 2, limit=4 * 1024 * 1024)

    def body(c_ref, g_ref, s_ref, o_ref):
        del c_ref
        o_ref[...] = (g_ref[...].astype(F32) + s_ref[...].astype(F32)).astype(BF16)

    grid_spec = pltpu.PrefetchScalarGridSpec(
        num_scalar_prefetch=1, grid=(N_CHIPS, hk // tr),
        in_specs=[pl.BlockSpec((None, tr, cols), lambda p, r, c_ref: (c_ref[0], r, p)),
                  pl.BlockSpec((tr, cols), lambda p, r, c_ref: (r, p))],
        out_specs=pl.BlockSpec((None, tr, cols), lambda p, r, c_ref: (p, r, 0)))
    return pl.pallas_call(
        body, name=name, grid_spec=grid_spec,
        out_shape=jax.ShapeDtypeStruct((N_CHIPS, hk, cols), BF16),
        compiler_params=_params(("parallel", "parallel")),
    )(c_arr, g3, sib)


def _chip_sum_row(g4, sib, c_arr, name):
    _, _, hr, n = g4.shape
    tr = _row_tile(hr, n * 2, limit=4 * 1024 * 1024)

    def body(c_ref, g_ref, s_ref, o_ref):
        del c_ref
        o_ref[...] = (g_ref[...].astype(F32) + s_ref[...].astype(F32)).astype(BF16)

    grid_spec = pltpu.PrefetchScalarGridSpec(
        num_scalar_prefetch=1, grid=(N_CHIPS, hr // tr),
        in_specs=[pl.BlockSpec((None, None, tr, n), lambda p, r, c_ref: (p, c_ref[0], r, 0)),
                  pl.BlockSpec((None, tr, n), lambda p, r, c_ref: (p, r, 0))],
        out_specs=pl.BlockSpec((None, tr, n), lambda p, r, c_ref: (p, r, 0)))
    return pl.pallas_call(
        body, name=name, grid_spec=grid_spec,
        out_shape=jax.ShapeDtypeStruct((N_CHIPS, hr, n), BF16),
        compiler_params=_params(("parallel", "parallel")),
    )(c_arr, g4, sib)


def _sum_pieces(pieces, received, place_arr, name):
    _, r, n = pieces.shape
    tr = _row_tile(r, n * 4, limit=4 * 1024 * 1024)

    def body(p_ref, own_ref, r0_ref, r1_ref, r2_ref, o_ref):
        del p_ref
        acc = own_ref[...].astype(F32) + r0_ref[...].astype(F32)
        acc = acc + r1_ref[...].astype(F32)
        o_ref[...] = acc + r2_ref[...].astype(F32)

    def recv_spec(j):
        return pl.BlockSpec((None, tr, n), lambda i, p_ref: (j, i, 0))

    grid_spec = pltpu.PrefetchScalarGridSpec(
        num_scalar_prefetch=1, grid=(r // tr,),
        in_specs=[pl.BlockSpec((None, tr, n), lambda i, p_ref: (p_ref[0], i, 0)),
                  recv_spec(0), recv_spec(1), recv_spec(2)],
        out_specs=pl.BlockSpec((None, tr, n), lambda i, p_ref: (p_ref[1], i, 0)))
    return pl.pallas_call(
        body, name=name, grid_spec=grid_spec,
        out_shape=jax.ShapeDtypeStruct((2, r, n), F32),
        compiler_params=_params(("parallel",)),
    )(place_arr, pieces, received, received, received)


def _norm_weights_step(parts, w, m, v, after=()):
    rows, d = parts.shape
    after = tuple(after)

    def body(p_ref, w_ref, m_ref, v_ref, *rest):
        g_ref, d_ref, mo_ref, vo_ref, gathered, send_sems, recv_sems = rest[len(after):]
        x, y, c = _place()
        me = 4 * x + 2 * y + c
        gathered[me] = p_ref[...]
        copies = []
        for k in range(1, N_DEV):
            peer = (x ^ ((k >> 2) & 1), y ^ ((k >> 1) & 1), c ^ (k & 1))
            copies.append(pltpu.make_async_remote_copy(
                src_ref=p_ref, dst_ref=gathered.at[me], send_sem=send_sems.at[k - 1],
                recv_sem=recv_sems.at[k - 1], device_id=peer, device_id_type=MESH))
        for cp in copies:
            cp.start()
        for cp in copies:
            cp.wait()
        g = gathered[0]
        for k in range(1, N_DEV):
            g = g + gathered[k]
        delta, m_new, v_new = _adamw_math(w_ref[...], g, m_ref[...], v_ref[...])
        g_ref[...] = g
        d_ref[...] = delta
        mo_ref[...] = m_new
        vo_ref[...] = v_new

    vmem = pl.BlockSpec(memory_space=pltpu.VMEM)
    shp = jax.ShapeDtypeStruct((rows, d), F32)
    return pl.pallas_call(
        body, name="norm_weights_step",
        in_specs=[vmem] * 4 + [ANY] * len(after), out_specs=[vmem] * 4, out_shape=[shp] * 4,
        scratch_shapes=[pltpu.VMEM((N_DEV, rows, d), F32), pltpu.SemaphoreType.DMA((N_DEV - 1,)),
                        pltpu.SemaphoreType.DMA((N_DEV - 1,))],
        compiler_params=pltpu.CompilerParams(has_side_effects=True),
    )(parts, w, m, v, *after)


def kernel(x, norm_mix_w, w_in, w_out, norm_ffn_w, w_gate, w_up, w_down, norm_final_w, loss_target, m_norm_mix_w, m_w_in, m_w_out, m_norm_ffn_w, m_w_gate, m_w_up, m_w_down, m_norm_final_w, v_norm_mix_w, v_w_in, v_w_out, v_norm_ffn_w, v_w_gate, v_w_up, v_w_down, v_norm_final_w):
    s, d = x.shape[1], x.shape[2]
    xs = x.reshape(s, d)
    target = loss_target.reshape(s, d)
    big = {"w_in": (w_in, m_w_in, v_w_in), "w_out": (w_out, m_w_out, v_w_out),
           "w_gate": (w_gate, m_w_gate, v_w_gate), "w_up": (w_up, m_w_up, v_w_up),
           "w_down": (w_down, m_w_down, v_w_down)}
    big = {k: tuple(a.reshape(a.shape[1:]) for a in t) for k, t in big.items()}
    col_names, row_names = ("w_in", "w_gate", "w_up"), ("w_out", "w_down")
    n_in = N_CHIPS * big["w_in"][0].shape[1]
    ffn = N_CHIPS * big["w_gate"][0].shape[1]
    mix = ATTN_WIDTH + RET_WIDTH
    c_arr = lax.axis_index("c").astype(I32).reshape(1)
    shard_arr = (2 * lax.axis_index("x") + lax.axis_index("y")).astype(I32).reshape(1)
    place_arr = jnp.concatenate([shard_arr, c_arr])

    def cast(k, after=()):
        return _weight_view(_cast_into_full(big[k][0], shard_arr, k in col_names, "cast_" + k, after), k in col_names)

    started_in, v_in = _gather_in_start(cast("w_in"), "gather_in_start")

    sec = ATTN_WIDTH

    def section(p, rows):
        return pl.BlockSpec((None, rows, sec), lambda i, j, kk: (p, i, 0))

    h1 = _rms_fwd(xs, norm_mix_w, "rms_mix_fwd", after=[v_in])
    my_shard = shard_arr[0]
    shard_of = [jnp.bitwise_xor(my_shard, f).astype(I32).reshape(1) for f in (0,) + _FLIPS]
    proj = _in_proj_shard(h1, _weight_unview(v_in), None, shard_of[0], "in_proj_own")
    early_views = [cast(k, after=[proj]) for k in ("w_out", "w_gate")]
    v_up, v_down = [cast(k, after=[proj]) for k in ("w_up", "w_down")]
    relayed_in, (v_in,) = _gather_relay(v_in, started_in, 0, early_views + [v_up, v_down], "gather_in_relay")
    started_og, (v_out, v_gate) = _gather_out_gate_start(*early_views, [v_in], "gather_out_gate_start")
    v_in = _gather_in_neighbours_end(v_in, relayed_in, [v_out], "gather_in_neighbours_end")
    proj = _in_proj_shard(h1, _weight_unview(v_in), proj, shard_of[1], "in_proj_x")
    proj = _in_proj_shard(h1, _weight_unview(v_in), proj, shard_of[2], "in_proj_y")
    forwarded_in, v_in = _gather_in_diagonal(v_in, relayed_in, [proj], "gather_in_diagonal",
                                             _SIBLING_IDS["diagonal_in"])
    wi = _weight_unview(_gather_in_diagonal_end(v_in, forwarded_in, [proj], "gather_in_diagonal_end"))
    proj = _in_proj_shard(h1, wi, proj, shard_of[3], "in_proj_diagonal")
    fs_o, fr_o, v_out = _gather_forward([v_out], [0], *started_og, proj, "gather_forward_out",
                                        sibling_id=_SIBLING_IDS["forward_out"])
    mixed, attn_o, lse = _attn_fwd(proj, after=[v_out])
    relayed_g, (v_gate, v_up) = _gather_relay(v_gate, started_og, 3, [attn_o], "gather_gate_relay",
                                              then=v_up, then_peers=2, then_first=True)
    mixed, ret_raw = _ret_fwd(proj, mixed, after=[v_gate])
    wo, = _gather_end([v_out], fs_o, fr_o, ret_raw, "gather_end_out")
    x1, = _matmul("out_proj", "nn", [mixed, mixed], [wo, wo], [0, 0], s, d, sec, s // 2, 512, sec, [xs], [F32],
                  _epi_residual, b_koff=[0, 1], a_specs=[section(0, s // 2), section(1, s // 2)])
    h2 = _rms_fwd(x1, norm_ffn_w, "rms_ffn_fwd")
    relayed_u, (v_up, v_down) = _gather_relay(v_up, relayed_g, 6, [h2], "gather_up_relay",
                                              then=v_down, then_peers=3)
    v_gate = _gather_in_neighbours_end(v_gate, relayed_g, [v_up], "gather_gate_neighbours_end")
    forwarded_g, v_gate = _gather_in_diagonal(v_gate, relayed_g, [v_up], "gather_gate_diagonal",
                                              _SIBLING_IDS["diagonal_gate"])
    v_up = _gather_in_neighbours_end(v_up, relayed_u, [v_gate], "gather_up_neighbours_end")
    wg = _weight_unview(_gather_in_diagonal_end(v_gate, forwarded_g, [v_up], "gather_gate_diagonal_end"))
    forwarded_u, v_up = _gather_in_diagonal(v_up, relayed_u, [wg], "gather_up_diagonal",
                                            _SIBLING_IDS["diagonal_up"])
    wu = _weight_unview(_gather_in_diagonal_end(v_up, forwarded_u, [wg], "gather_up_diagonal_end"))
    gate, up, act = _matmul("gate_up", "nn", [h2, h2], [wg, wu], [0, 1], s, ffn, d, s, 512, d, [],
                            [BF16, BF16, BF16], _epi_swiglu, a_single_buffer=True)
    fs, fr, v_down = _gather_forward([v_down], [0], *relayed_u, act, "gather_forward_down", base=6,
                                     sibling_id=_SIBLING_IDS["forward_down"])
    wd, = _gather_end([v_down], fs, fr, act, "gather_end_down")
    x2, = _matmul("down_proj", "nn", [act], [wd], [0], s, d, ffn, s // 2, 512, ffn, [x1], [F32],
                  _epi_residual)
    loss_row, dx2, dx2b, dwf = _final_norm_loss(x2, norm_final_w.reshape(1, d), target, "final_norm_loss")

    names = col_names + row_names
    grads, new = {}, {}

    def chip_sums(tag_names, views, sibs):
        return [(_chip_sum_col if k in col_names else _chip_sum_row)(v, sb, c_arr, "chip_sum_" + k)
                for k, v, sb in zip(tag_names, views, sibs)]

    def piece_sums(tag_names, pieces, received):
        return [_sum_pieces(p, r, place_arr, "sum_pieces_" + k) for k, p, r in zip(tag_names, pieces, received)]

    def update(k):
        new[k] = _adamw(big[k][0], grads[k], big[k][1], big[k][2], "adamw_" + k)

    dgate, dup = _matmul("d_act", "nt", [dx2b], [wd], [0], s, ffn, d, s, 512, d, [gate, up],
                         [BF16, BF16], _epi_swiglu_bwd, a_single_buffer=True)
    g_wd, = _matmul("g_w_down", "tn", [act], [dx2b], [0], ffn, d, s, 512, d, s, [], [BF16], _epi_plain)
    halves_d = _halves_start("down", [g_wd], [False])
    dh2, = _matmul("d_h2", "nt", [dgate, dup], [wg, wu], [0, 0], s, d, ffn, s // 2, 512, ffn, [], [F32],
                   _epi_plain, after=halves_d[2][-1:], a_single_buffer=True)
    pieces_d = _pieces_start("down", chip_sums(["w_down"], *_halves_wait("down", halves_d, dh2)))
    g_wg, g_wu = _matmul("g_w_gate_up", "tn", [h2, h2], [dgate, dup], [0, 1], d, ffn, s, 1024, 512, s, [],
                         [BF16, BF16], _epi_two, after=pieces_d[2][-1:])
    halves_gu = _halves_start("gate_up", [g_wg, g_wu], [True, True])
    dx1, dx1b, dw_ffn = _rms_bwd(x1, norm_ffn_w, dh2, dx2, "rms_ffn_bwd", after=halves_gu[2][-1:])

    dmixed, = _matmul("d_mixed", "nt", [dx1b], [wo], [0], s, mix, d, s // 2, 512, d, [], [F32], _epi_plain)
    pieces_gu = _pieces_start("gate_up", chip_sums(["w_gate", "w_up"], *_halves_wait("gate_up", halves_gu, dmixed)))
    per = sec // 512
    g_wo, = _matmul("g_w_out", "tn", [mixed], [dx1b], [0], mix, d, s, 512, d, s, [], [BF16], _epi_plain,
                    after=pieces_gu[2][-1:],
                    a_specs=[pl.BlockSpec((None, s, 512), lambda i, j, kk: (i // per, 0, i % per))])
    halves_o = _halves_start("out", [g_wo], [False])
    dsec = _attn_bwd(proj, attn_o, lse, dmixed, after=halves_o[2][-1:])
    pieces_o = _pieces_start("out", chip_sums(["w_out"], *_halves_wait("out", halves_o, dsec)))
    dsec = _ret_bwd(proj, ret_raw, dmixed, dsec, after=pieces_o[2][-1:])
    where = [0, 1, 2, 4, 5, 6, 7]
    n_sec = len(where)
    g_wi, = _matmul("g_w_in", "tn", [h1], [dsec], [0], d, n_in, s, 1024, sec, s, [], [BF16], _epi_plain,
                    b_specs=[pl.BlockSpec((None, s, sec), lambda i, j, kk: (j + (j >= 3).astype(I32), 0, 0))])
    halves_i = _halves_start("in", [g_wi], [True])
    dh1, = _matmul("d_h1", "nt", [dsec] * n_sec, [wi] * n_sec, [0] * n_sec, s, d, sec, s // 2, 256, sec, [], [F32],
                   _epi_plain, b_koff=list(range(n_sec)), after=halves_i[2][-1:],
                   a_specs=[section(p, s // 2) for p in where])
    pieces_i = _pieces_start("in", chip_sums(["w_in"], *_halves_wait("in", halves_i, dh1)))
    grad_x, _, dw_mix = _rms_bwd(xs, norm_mix_w, dh1, dx1, "rms_mix_bwd", after=pieces_i[2][-1:])

    def rows8(*vs):
        return jnp.concatenate([v.reshape(1, d) for v in vs] + [jnp.zeros((8 - len(vs), d), F32)], axis=0)

    join_d = _join_start("down", piece_sums(["w_down"], *_pieces_wait("down", pieces_d, grad_x)))
    join_gu = _join_start("gate_up", piece_sums(["w_gate", "w_up"], *_pieces_wait("gate_up", pieces_gu, join_d[2][0])))
    join_o = _join_start("out", piece_sums(["w_out"], *_pieces_wait("out", pieces_o, join_gu[2][0])))
    grads["w_down"], = _join_wait("down", join_d, join_o[2][0])
    update("w_down")
    grads["w_gate"], grads["w_up"] = _join_wait("gate_up", join_gu, new["w_down"][0])
    update("w_gate")
    update("w_up")
    grads["w_out"], = _join_wait("out", join_o, new["w_up"][0])
    update("w_out")
    others_done = [new[k][0] for k in ("w_down", "w_gate", "w_up", "w_out")]
    join_i = _join_start("in", piece_sums(["w_in"], *_pieces_wait("in", pieces_i, others_done)))
    ng, nd, nm, nv = _norm_weights_step(
        rows8(dw_mix, dw_ffn, dwf, jnp.broadcast_to(loss_row[:, :1], (1, d))),
        rows8(norm_mix_w, norm_ffn_w, norm_final_w),
        rows8(m_norm_mix_w, m_norm_ffn_w, m_norm_final_w), rows8(v_norm_mix_w, v_norm_ffn_w, v_norm_final_w),
        after=join_i[2][:1])
    grads["w_in"], = _join_wait("in", join_i, ng)
    update("w_in")

    loss = ng[3, 0]

    def pack(small, per_weight):
        lead = lambda a: a.reshape((1,) + a.shape)
        return (small[0:1], lead(per_weight["w_in"]), lead(per_weight["w_out"]), small[1:2],
                lead(per_weight["w_gate"]), lead(per_weight["w_up"]), lead(per_weight["w_down"]), small[2])

    return (loss, grad_x.reshape(1, s, d),
            *pack(ng, {k: new[k][3] for k in names}),
            *pack(nd, {k: new[k][0] for k in names}),
            *pack(nm, {k: new[k][1] for k in names}),
            *pack(nv, {k: new[k][2] for k in names}))
```
